```python
import jax, jax.numpy as jnp
from jax import lax
import numpy as np

D_MODEL = 1024
BATCH = 8
SEQ = 8192
DEPTH = 2

HEAD_DIM = 64
ATTN_WIDTH = D_MODEL // 2
N_Q_HEADS = ATTN_WIDTH // HEAD_DIM
N_KV_HEADS = 2
Q_PER_KV = N_Q_HEADS // N_KV_HEADS
KV_WIDTH = N_KV_HEADS * HEAD_DIM
WINDOW = 128
BLOCK = 128
CONV_WIDTH = D_MODEL // 4
CONV_KERNEL = 31
LRU_WIDTH = D_MODEL // 4
LRU_HEADS = 4
LRU_HEAD_DIM = LRU_WIDTH // LRU_HEADS
LRU_CONV_KERNEL = 4
LRU_C = 8.0
MIX_WIDTH = ATTN_WIDTH + CONV_WIDTH + LRU_WIDTH
IN_SPLIT_SIZES = (ATTN_WIDTH, KV_WIDTH, KV_WIDTH, CONV_WIDTH, CONV_WIDTH, LRU_WIDTH, LRU_WIDTH)
IN_WIDTH = sum(IN_SPLIT_SIZES)
IN_SPLIT_IDX = [int(v) for v in np.cumsum(IN_SPLIT_SIZES)[:-1]]
D_FF = 4 * D_MODEL
RMS_EPS = 1e-6
LN_EPS = 1e-5
MASK_VALUE = -1e30

kernel_name = "hymba_style_swa_conformer_rglru_hybrid"


def rms_norm(x, g):
    xf = x.astype(jnp.float32)
    y = xf * lax.rsqrt(jnp.mean(xf * xf, axis=-1, keepdims=True) + RMS_EPS)
    return (y * g.astype(jnp.float32)).astype(x.dtype)


def layer_norm(x, g, b):
    xf = x.astype(jnp.float32)
    mu = jnp.mean(xf, axis=-1, keepdims=True)
    xc = xf - mu
    y = xc * lax.rsqrt(jnp.mean(xc * xc, axis=-1, keepdims=True) + LN_EPS)
    return (y * g.astype(jnp.float32) + b.astype(jnp.float32)).astype(x.dtype)


def causal_depthwise_conv(x, w, b):
    k_width, c = w.shape
    out = lax.conv_general_dilated(
        x, w[:, None, :].astype(x.dtype), window_strides=(1,), padding=[(k_width - 1, 0)],
        dimension_numbers=("NWC", "WIO", "NWC"), feature_group_count=c)
    return out + b.astype(x.dtype)


def sliding_window_attention(q, k, v, sinks):
    b, s, _ = q.shape
    nb = s // BLOCK
    q = q.reshape(b, nb, BLOCK, N_KV_HEADS, Q_PER_KV, HEAD_DIM)
    k = k.reshape(b, nb, BLOCK, N_KV_HEADS, HEAD_DIM)
    v = v.reshape(b, nb, BLOCK, N_KV_HEADS, HEAD_DIM)
    k_band = jnp.concatenate([jnp.concatenate([jnp.zeros_like(k[:, :1]), k[:, :-1]], axis=1), k], axis=2)
    v_band = jnp.concatenate([jnp.concatenate([jnp.zeros_like(v[:, :1]), v[:, :-1]], axis=1), v], axis=2)
    scores = jnp.einsum("bnqhgd,bnkhd->bnhgqk", q, k_band).astype(jnp.float32) * (HEAD_DIM ** -0.5)
    blk = jnp.arange(nb)[:, None]
    q_pos = blk * BLOCK + jnp.arange(BLOCK)[None, :]
    k_pos = (blk - 1) * BLOCK + jnp.arange(2 * BLOCK)[None, :]
    diff = q_pos[:, :, None] - k_pos[:, None, :]
    mask = (diff >= 0) & (diff < WINDOW) & (k_pos[:, None, :] >= 0)
    scores = jnp.where(mask[None, :, None, None], scores, MASK_VALUE)
    sink = sinks.astype(jnp.float32).reshape(N_KV_HEADS, Q_PER_KV)[None, None, :, :, None, None]
    m = jnp.maximum(jnp.max(scores, axis=-1, keepdims=True), sink)
    p = jnp.exp(scores - m)
    probs = p / (jnp.sum(p, axis=-1, keepdims=True) + jnp.exp(sink - m))
    out = jnp.einsum("bnhgqk,bnkhd->bnqhgd", probs.astype(v.dtype), v_band)
    return out.reshape(b, s, ATTN_WIDTH)


def conformer_conv(u_val, u_gate, dw_w, dw_b, ln_g, ln_b):
    u = u_val * jax.nn.sigmoid(u_gate)
    u = causal_depthwise_conv(u, dw_w, dw_b)
    u = layer_norm(u, ln_g, ln_b)
    return jax.nn.silu(u)


def _linear_recurrence_combine(c1, c2):
    a1, b1 = c1
    a2, b2 = c2
    return a1 * a2, a2 * b1 + b2


def rglru_branch(u_x, u_gate, conv_w, conv_b, wa, ba, wx, bx, lam):
    xc = causal_depthwise_conv(u_x, conv_w, conv_b)
    b, s, _ = xc.shape
    xh = xc.reshape(b, s, LRU_HEADS, LRU_HEAD_DIM)
    r = jax.nn.sigmoid(jnp.einsum("bshi,hij->bshj", xh, wa) + ba).reshape(b, s, LRU_WIDTH)
    i = jax.nn.sigmoid(jnp.einsum("bshi,hij->bshj", xh, wx) + bx).reshape(b, s, LRU_WIDTH)
    log_a = (-LRU_C * r.astype(jnp.float32)) * jax.nn.softplus(-lam.astype(jnp.float32))
    a = jnp.exp(log_a)
    gated_x = jnp.sqrt(-jnp.expm1(2.0 * log_a)) * (i * xc).astype(jnp.float32)
    _, h = lax.associative_scan(_linear_recurrence_combine, (a, gated_x), axis=1)
    return h.astype(u_x.dtype) * jax.nn.gelu(u_gate)


def _fwd_setup_inputs(seed: int = 0) -> dict:
    key = jax.random.key(seed)
    ks = jax.random.split(key, 24)
    f32 = jnp.float32

    def nrm(k, shape, scale):
        return jax.random.normal(k, shape, f32) * scale

    def gain(k, shape):
        return 1.0 + 0.02 * jax.random.normal(k, shape, f32)

    a0 = jax.random.uniform(ks[14], (DEPTH, LRU_WIDTH), f32, 0.9, 0.999)
    s0 = a0 ** (1.0 / LRU_C)
    lru_lambda = jnp.log(s0) - jnp.log1p(-s0)
    return {
        "x": jax.random.normal(ks[0], (BATCH, SEQ, D_MODEL), f32),
        "norm1": gain(ks[1], (DEPTH, D_MODEL)),
        "w_in": nrm(ks[2], (DEPTH, D_MODEL, IN_WIDTH), D_MODEL ** -0.5),
        "attn_sinks": nrm(ks[3], (DEPTH, N_Q_HEADS), 0.5),
        "conv_dw_w": nrm(ks[4], (DEPTH, CONV_KERNEL, CONV_WIDTH), CONV_KERNEL ** -0.5),
        "conv_dw_b": nrm(ks[5], (DEPTH, CONV_WIDTH), 0.01),
        "conv_ln_g": gain(ks[6], (DEPTH, CONV_WIDTH)),
        "conv_ln_b": nrm(ks[7], (DEPTH, CONV_WIDTH), 0.01),
        "lru_conv_w": nrm(ks[8], (DEPTH, LRU_CONV_KERNEL, LRU_WIDTH), LRU_CONV_KERNEL ** -0.5),
        "lru_conv_b": nrm(ks[9], (DEPTH, LRU_WIDTH), 0.01),
        "lru_wa": nrm(ks[10], (DEPTH, LRU_HEADS, LRU_HEAD_DIM, LRU_HEAD_DIM), LRU_HEAD_DIM ** -0.5),
        "lru_ba": nrm(ks[11], (DEPTH, LRU_HEADS, LRU_HEAD_DIM), 0.01),
        "lru_wx": nrm(ks[12], (DEPTH, LRU_HEADS, LRU_HEAD_DIM, LRU_HEAD_DIM), LRU_HEAD_DIM ** -0.5),
        "lru_bx": nrm(ks[13], (DEPTH, LRU_HEADS, LRU_HEAD_DIM), 0.01),
        "lru_lambda": lru_lambda,
        "mix_norm": gain(ks[15], (DEPTH, MIX_WIDTH)),
        "w_out": nrm(ks[16], (DEPTH, MIX_WIDTH, D_MODEL), MIX_WIDTH ** -0.5),
        "norm2": gain(ks[17], (DEPTH, D_MODEL)),
        "w_up": nrm(ks[18], (DEPTH, D_MODEL, D_FF), D_MODEL ** -0.5),
        "w_down": nrm(ks[19], (DEPTH, D_FF, D_MODEL), D_FF ** -0.5),
        "final_norm": gain(ks[20], (D_MODEL,)),
    }


def _fwd_reference(x, norm1, w_in, attn_sinks, conv_dw_w, conv_dw_b, conv_ln_g, conv_ln_b,
              lru_conv_w, lru_conv_b, lru_wa, lru_ba, lru_wx, lru_bx, lru_lambda,
              mix_norm, w_out, norm2, w_up, w_down, final_norm):
    h = x
    a_end = ATTN_WIDTH
    c_end = ATTN_WIDTH + CONV_WIDTH
    for l in range(DEPTH):
        hn = rms_norm(h, norm1[l])
        z = hn @ w_in[l]
        q, k, v, c_val, c_gate, r_x, r_gate = jnp.split(z, IN_SPLIT_IDX, axis=-1)
        y_attn = sliding_window_attention(q, k, v, attn_sinks[l])
        y_conv = conformer_conv(c_val, c_gate, conv_dw_w[l], conv_dw_b[l], conv_ln_g[l], conv_ln_b[l])
        y_lru = rglru_branch(r_x, r_gate, lru_conv_w[l], lru_conv_b[l], lru_wa[l], lru_ba[l],
                             lru_wx[l], lru_bx[l], lru_lambda[l])
        g = mix_norm[l]
        y = jnp.concatenate([rms_norm(y_attn, g[:a_end]),
                             rms_norm(y_conv, g[a_end:c_end]),
                             rms_norm(y_lru, g[c_end:])], axis=-1)
        h = h + y @ w_out[l]
        hn = rms_norm(h, norm2[l])
        h = h + jnp.square(jax.nn.relu(hn @ w_up[l])) @ w_down[l]
    return rms_norm(h, final_norm)


import jax as _jax
import jax.numpy as _jnp

TWIN_FORMAT = 'train_step'
FWD_PARAMS = ['x', 'norm1', 'w_in', 'attn_sinks', 'conv_dw_w', 'conv_dw_b', 'conv_ln_g', 'conv_ln_b', 'lru_conv_w', 'lru_conv_b', 'lru_wa', 'lru_ba', 'lru_wx', 'lru_bx', 'lru_lambda', 'mix_norm', 'w_out', 'norm2', 'w_up', 'w_down', 'final_norm']
TWIN_WEIGHTS = ['norm1', 'w_in', 'attn_sinks', 'conv_dw_w', 'conv_dw_b', 'conv_ln_g', 'conv_ln_b', 'lru_conv_w', 'lru_conv_b', 'lru_wa', 'lru_ba', 'lru_wx', 'lru_bx', 'lru_lambda', 'mix_norm', 'w_out', 'norm2', 'w_up', 'w_down', 'final_norm']
TWIN_DIFF_INPUT = 'x'
TWIN_INPUTS = ['x', 'norm1', 'w_in', 'attn_sinks', 'conv_dw_w', 'conv_dw_b', 'conv_ln_g', 'conv_ln_b', 'lru_conv_w', 'lru_conv_b', 'lru_wa', 'lru_ba', 'lru_wx', 'lru_bx', 'lru_lambda', 'mix_norm', 'w_out', 'norm2', 'w_up', 'w_down', 'final_norm', 'loss_target', 'm_norm1', 'm_w_in', 'm_attn_sinks', 'm_conv_dw_w', 'm_conv_dw_b', 'm_conv_ln_g', 'm_conv_ln_b', 'm_lru_conv_w', 'm_lru_conv_b', 'm_lru_wa', 'm_lru_ba', 'm_lru_wx', 'm_lru_bx', 'm_lru_lambda', 'm_mix_norm', 'm_w_out', 'm_norm2', 'm_w_up', 'm_w_down', 'm_final_norm', 'v_norm1', 'v_w_in', 'v_attn_sinks', 'v_conv_dw_w', 'v_conv_dw_b', 'v_conv_ln_g', 'v_conv_ln_b', 'v_lru_conv_w', 'v_lru_conv_b', 'v_lru_wa', 'v_lru_ba', 'v_lru_wx', 'v_lru_bx', 'v_lru_lambda', 'v_mix_norm', 'v_w_out', 'v_norm2', 'v_w_up', 'v_w_down', 'v_final_norm']
TWIN_OUTPUTS = ['loss', 'grad_x', 'grad_norm1', 'grad_w_in', 'grad_attn_sinks', 'grad_conv_dw_w', 'grad_conv_dw_b', 'grad_conv_ln_g', 'grad_conv_ln_b', 'grad_lru_conv_w', 'grad_lru_conv_b', 'grad_lru_wa', 'grad_lru_ba', 'grad_lru_wx', 'grad_lru_bx', 'grad_lru_lambda', 'grad_mix_norm', 'grad_w_out', 'grad_norm2', 'grad_w_up', 'grad_w_down', 'grad_final_norm', 'delta_norm1', 'delta_w_in', 'delta_attn_sinks', 'delta_conv_dw_w', 'delta_conv_dw_b', 'delta_conv_ln_g', 'delta_conv_ln_b', 'delta_lru_conv_w', 'delta_lru_conv_b', 'delta_lru_wa', 'delta_lru_ba', 'delta_lru_wx', 'delta_lru_bx', 'delta_lru_lambda', 'delta_mix_norm', 'delta_w_out', 'delta_norm2', 'delta_w_up', 'delta_w_down', 'delta_final_norm', 'new_m_norm1', 'new_m_w_in', 'new_m_attn_sinks', 'new_m_conv_dw_w', 'new_m_conv_dw_b', 'new_m_conv_ln_g', 'new_m_conv_ln_b', 'new_m_lru_conv_w', 'new_m_lru_conv_b', 'new_m_lru_wa', 'new_m_lru_ba', 'new_m_lru_wx', 'new_m_lru_bx', 'new_m_lru_lambda', 'new_m_mix_norm', 'new_m_w_out', 'new_m_norm2', 'new_m_w_up', 'new_m_w_down', 'new_m_final_norm', 'new_v_norm1', 'new_v_w_in', 'new_v_attn_sinks', 'new_v_conv_dw_w', 'new_v_conv_dw_b', 'new_v_conv_ln_g', 'new_v_conv_ln_b', 'new_v_lru_conv_w', 'new_v_lru_conv_b', 'new_v_lru_wa', 'new_v_lru_ba', 'new_v_lru_wx', 'new_v_lru_bx', 'new_v_lru_lambda', 'new_v_mix_norm', 'new_v_w_out', 'new_v_norm2', 'new_v_w_up', 'new_v_w_down', 'new_v_final_norm']
TWIN_LEAF_KINDS = {'loss': 'loss', 'grad_x': 'grad_x', 'grad_norm1': 'grad_w', 'grad_w_in': 'grad_w', 'grad_attn_sinks': 'grad_w', 'grad_conv_dw_w': 'grad_w', 'grad_conv_dw_b': 'grad_w', 'grad_conv_ln_g': 'grad_w', 'grad_conv_ln_b': 'grad_w', 'grad_lru_conv_w': 'grad_w', 'grad_lru_conv_b': 'grad_w', 'grad_lru_wa': 'grad_w', 'grad_lru_ba': 'grad_w', 'grad_lru_wx': 'grad_w', 'grad_lru_bx': 'grad_w', 'grad_lru_lambda': 'grad_w', 'grad_mix_norm': 'grad_w', 'grad_w_out': 'grad_w', 'grad_norm2': 'grad_w', 'grad_w_up': 'grad_w', 'grad_w_down': 'grad_w', 'grad_final_norm': 'grad_w', 'delta_norm1': 'delta_w', 'delta_w_in': 'delta_w', 'delta_attn_sinks': 'delta_w', 'delta_conv_dw_w': 'delta_w', 'delta_conv_dw_b': 'delta_w', 'delta_conv_ln_g': 'delta_w', 'delta_conv_ln_b': 'delta_w', 'delta_lru_conv_w': 'delta_w', 'delta_lru_conv_b': 'delta_w', 'delta_lru_wa': 'delta_w', 'delta_lru_ba': 'delta_w', 'delta_lru_wx': 'delta_w', 'delta_lru_bx': 'delta_w', 'delta_lru_lambda': 'delta_w', 'delta_mix_norm': 'delta_w', 'delta_w_out': 'delta_w', 'delta_norm2': 'delta_w', 'delta_w_up': 'delta_w', 'delta_w_down': 'delta_w', 'delta_final_norm': 'delta_w', 'new_m_norm1': 'new_m', 'new_m_w_in': 'new_m', 'new_m_attn_sinks': 'new_m', 'new_m_conv_dw_w': 'new_m', 'new_m_conv_dw_b': 'new_m', 'new_m_conv_ln_g': 'new_m', 'new_m_conv_ln_b': 'new_m', 'new_m_lru_conv_w': 'new_m', 'new_m_lru_conv_b': 'new_m', 'new_m_lru_wa': 'new_m', 'new_m_lru_ba': 'new_m', 'new_m_lru_wx': 'new_m', 'new_m_lru_bx': 'new_m', 'new_m_lru_lambda': 'new_m', 'new_m_mix_norm': 'new_m', 'new_m_w_out': 'new_m', 'new_m_norm2': 'new_m', 'new_m_w_up': 'new_m', 'new_m_w_down': 'new_m', 'new_m_final_norm': 'new_m', 'new_v_norm1': 'new_v', 'new_v_w_in': 'new_v', 'new_v_attn_sinks': 'new_v', 'new_v_conv_dw_w': 'new_v', 'new_v_conv_dw_b': 'new_v', 'new_v_conv_ln_g': 'new_v', 'new_v_conv_ln_b': 'new_v', 'new_v_lru_conv_w': 'new_v', 'new_v_lru_conv_b': 'new_v', 'new_v_lru_wa': 'new_v', 'new_v_lru_ba': 'new_v', 'new_v_lru_wx': 'new_v', 'new_v_lru_bx': 'new_v', 'new_v_lru_lambda': 'new_v', 'new_v_mix_norm': 'new_v', 'new_v_w_out': 'new_v', 'new_v_norm2': 'new_v', 'new_v_w_up': 'new_v', 'new_v_w_down': 'new_v', 'new_v_final_norm': 'new_v'}


def _forward(args):
    return _fwd_reference(*[args[k] for k in FWD_PARAMS])


def _output_shape():
    def fwd():
        inp = _fwd_setup_inputs(0)
        return _fwd_reference(*[inp[k] for k in FWD_PARAMS])
    out = _jax.eval_shape(fwd)
    return out.shape, out.dtype

N_MICROBATCH = 1
ADAM_LR = 0.001
ADAM_B1 = 0.9
ADAM_B2 = 0.999
ADAM_EPS = 1e-08
ADAM_WD = 0.01
ADAM_STEP = 10
PER_EXAMPLE_BATCH_AXIS = {'x': 0, 'loss_target': 0}
SHARED_INPUTS = []
_WEIGHT_DTYPES = {'norm1': _jnp.float32, 'w_in': _jnp.float32, 'attn_sinks': _jnp.float32, 'conv_dw_w': _jnp.float32, 'conv_dw_b': _jnp.float32, 'conv_ln_g': _jnp.float32, 'conv_ln_b': _jnp.float32, 'lru_conv_w': _jnp.float32, 'lru_conv_b': _jnp.float32, 'lru_wa': _jnp.float32, 'lru_ba': _jnp.float32, 'lru_wx': _jnp.float32, 'lru_bx': _jnp.float32, 'lru_lambda': _jnp.float32, 'mix_norm': _jnp.float32, 'w_out': _jnp.float32, 'norm2': _jnp.float32, 'w_up': _jnp.float32, 'w_down': _jnp.float32, 'final_norm': _jnp.float32}
MOMENT_SCALE = {'norm1': 2.399985e-01, 'w_in': 1.930717e-01, 'attn_sinks': 7.598335e-02, 'conv_dw_w': 1.746015e-01, 'conv_dw_b': 6.267649e-01, 'conv_ln_g': 3.026262e-01, 'conv_ln_b': 3.599895e-01, 'lru_conv_w': 1.890323e-01, 'lru_conv_b': 1.738187e+00, 'lru_wa': 5.825352e-02, 'lru_ba': 5.340382e-02, 'lru_wx': 1.055949e-01, 'lru_bx': 6.644241e-02, 'lru_lambda': 1.028703e-01, 'mix_norm': 1.884407e-01, 'w_out': 2.037857e-01, 'norm2': 1.903821e-01, 'w_up': 9.144566e-02, 'w_down': 2.296904e-01, 'final_norm': 6.575712e+01}


def _to_microbatches(a, axis):
    t = _jnp.moveaxis(a, axis, 0)
    t = t.reshape((N_MICROBATCH, t.shape[0] // N_MICROBATCH) + t.shape[1:])
    return _jnp.moveaxis(t, 1, axis + 1)


def setup_inputs(seed: int = 0) -> dict:
    inp = _fwd_setup_inputs(seed)
    key = _jax.random.fold_in(_jax.random.key(seed), 7919)
    shape, _ = _output_shape()
    out = dict(inp)
    out["loss_target"] = _jax.random.normal(_jax.random.fold_in(key, 0), shape, _jnp.float32)
    for i, name in enumerate(TWIN_WEIGHTS):
        w = inp[name].astype(_jnp.float32)
        if MOMENT_SCALE is None:
            s = _jnp.sqrt(_jnp.mean(_jnp.square(w)) + 1e-30)
        else:
            s = MOMENT_SCALE[name]
        km, kv = _jax.random.split(_jax.random.fold_in(key, i + 1))
        out[name] = w
        out["m_" + name] = s * _jax.random.normal(km, w.shape, _jnp.float32)
        out["v_" + name] = (s * s) * _jax.random.uniform(kv, w.shape, _jnp.float32, 0.5, 1.5)
    if N_MICROBATCH > 1:
        for name, axis in PER_EXAMPLE_BATCH_AXIS.items():
            out[name] = _to_microbatches(out[name], axis)
    return {'x': out['x'], 'norm1': out['norm1'], 'w_in': out['w_in'], 'attn_sinks': out['attn_sinks'], 'conv_dw_w': out['conv_dw_w'], 'conv_dw_b': out['conv_dw_b'], 'conv_ln_g': out['conv_ln_g'], 'conv_ln_b': out['conv_ln_b'], 'lru_conv_w': out['lru_conv_w'], 'lru_conv_b': out['lru_conv_b'], 'lru_wa': out['lru_wa'], 'lru_ba': out['lru_ba'], 'lru_wx': out['lru_wx'], 'lru_bx': out['lru_bx'], 'lru_lambda': out['lru_lambda'], 'mix_norm': out['mix_norm'], 'w_out': out['w_out'], 'norm2': out['norm2'], 'w_up': out['w_up'], 'w_down': out['w_down'], 'final_norm': out['final_norm'], 'loss_target': out['loss_target'], 'm_norm1': out['m_norm1'], 'm_w_in': out['m_w_in'], 'm_attn_sinks': out['m_attn_sinks'], 'm_conv_dw_w': out['m_conv_dw_w'], 'm_conv_dw_b': out['m_conv_dw_b'], 'm_conv_ln_g': out['m_conv_ln_g'], 'm_conv_ln_b': out['m_conv_ln_b'], 'm_lru_conv_w': out['m_lru_conv_w'], 'm_lru_conv_b': out['m_lru_conv_b'], 'm_lru_wa': out['m_lru_wa'], 'm_lru_ba': out['m_lru_ba'], 'm_lru_wx': out['m_lru_wx'], 'm_lru_bx': out['m_lru_bx'], 'm_lru_lambda': out['m_lru_lambda'], 'm_mix_norm': out['m_mix_norm'], 'm_w_out': out['m_w_out'], 'm_norm2': out['m_norm2'], 'm_w_up': out['m_w_up'], 'm_w_down': out['m_w_down'], 'm_final_norm': out['m_final_norm'], 'v_norm1': out['v_norm1'], 'v_w_in': out['v_w_in'], 'v_attn_sinks': out['v_attn_sinks'], 'v_conv_dw_w': out['v_conv_dw_w'], 'v_conv_dw_b': out['v_conv_dw_b'], 'v_conv_ln_g': out['v_conv_ln_g'], 'v_conv_ln_b': out['v_conv_ln_b'], 'v_lru_conv_w': out['v_lru_conv_w'], 'v_lru_conv_b': out['v_lru_conv_b'], 'v_lru_wa': out['v_lru_wa'], 'v_lru_ba': out['v_lru_ba'], 'v_lru_wx': out['v_lru_wx'], 'v_lru_bx': out['v_lru_bx'], 'v_lru_lambda': out['v_lru_lambda'], 'v_mix_norm': out['v_mix_norm'], 'v_w_out': out['v_w_out'], 'v_norm2': out['v_norm2'], 'v_w_up': out['v_w_up'], 'v_w_down': out['v_w_down'], 'v_final_norm': out['v_final_norm']}


def _loss(weights, diff, rest, loss_target):
    with _jax.named_scope("forward"):
        args = {**rest, TWIN_DIFF_INPUT: diff, **{k: w.astype(_WEIGHT_DTYPES[k]) for k, w in weights.items()}}
        y = _forward(args)
    with _jax.named_scope("loss_head"):
        err = _jnp.square(y.astype(_jnp.float32) - loss_target)
        return 0.5 * _jnp.sum(_jnp.mean(err, axis=-1)) if err.ndim else 0.5 * err


def _adamw(w, g, m, v):
    m = ADAM_B1 * m + (1.0 - ADAM_B1) * g
    v = ADAM_B2 * v + (1.0 - ADAM_B2) * _jnp.square(g)
    m_hat = m / (1.0 - ADAM_B1 ** ADAM_STEP)
    v_hat = v / (1.0 - ADAM_B2 ** ADAM_STEP)
    delta = -ADAM_LR * (m_hat / (_jnp.sqrt(v_hat) + ADAM_EPS) + ADAM_WD * w)
    return delta, m, v


def reference(x, norm1, w_in, attn_sinks, conv_dw_w, conv_dw_b, conv_ln_g, conv_ln_b, lru_conv_w, lru_conv_b, lru_wa, lru_ba, lru_wx, lru_bx, lru_lambda, mix_norm, w_out, norm2, w_up, w_down, final_norm, loss_target, m_norm1, m_w_in, m_attn_sinks, m_conv_dw_w, m_conv_dw_b, m_conv_ln_g, m_conv_ln_b, m_lru_conv_w, m_lru_conv_b, m_lru_wa, m_lru_ba, m_lru_wx, m_lru_bx, m_lru_lambda, m_mix_norm, m_w_out, m_norm2, m_w_up, m_w_down, m_final_norm, v_norm1, v_w_in, v_attn_sinks, v_conv_dw_w, v_conv_dw_b, v_conv_ln_g, v_conv_ln_b, v_lru_conv_w, v_lru_conv_b, v_lru_wa, v_lru_ba, v_lru_wx, v_lru_bx, v_lru_lambda, v_mix_norm, v_w_out, v_norm2, v_w_up, v_w_down, v_final_norm):
    given = dict(x=x, norm1=norm1, w_in=w_in, attn_sinks=attn_sinks, conv_dw_w=conv_dw_w, conv_dw_b=conv_dw_b, conv_ln_g=conv_ln_g, conv_ln_b=conv_ln_b, lru_conv_w=lru_conv_w, lru_conv_b=lru_conv_b, lru_wa=lru_wa, lru_ba=lru_ba, lru_wx=lru_wx, lru_bx=lru_bx, lru_lambda=lru_lambda, mix_norm=mix_norm, w_out=w_out, norm2=norm2, w_up=w_up, w_down=w_down, final_norm=final_norm, loss_target=loss_target, m_norm1=m_norm1, m_w_in=m_w_in, m_attn_sinks=m_attn_sinks, m_conv_dw_w=m_conv_dw_w, m_conv_dw_b=m_conv_dw_b, m_conv_ln_g=m_conv_ln_g, m_conv_ln_b=m_conv_ln_b, m_lru_conv_w=m_lru_conv_w, m_lru_conv_b=m_lru_conv_b, m_lru_wa=m_lru_wa, m_lru_ba=m_lru_ba, m_lru_wx=m_lru_wx, m_lru_bx=m_lru_bx, m_lru_lambda=m_lru_lambda, m_mix_norm=m_mix_norm, m_w_out=m_w_out, m_norm2=m_norm2, m_w_up=m_w_up, m_w_down=m_w_down, m_final_norm=m_final_norm, v_norm1=v_norm1, v_w_in=v_w_in, v_attn_sinks=v_attn_sinks, v_conv_dw_w=v_conv_dw_w, v_conv_dw_b=v_conv_dw_b, v_conv_ln_g=v_conv_ln_g, v_conv_ln_b=v_conv_ln_b, v_lru_conv_w=v_lru_conv_w, v_lru_conv_b=v_lru_conv_b, v_lru_wa=v_lru_wa, v_lru_ba=v_lru_ba, v_lru_wx=v_lru_wx, v_lru_bx=v_lru_bx, v_lru_lambda=v_lru_lambda, v_mix_norm=v_mix_norm, v_w_out=v_w_out, v_norm2=v_norm2, v_w_up=v_w_up, v_w_down=v_w_down, v_final_norm=v_final_norm)
    weights = {n: given[n] for n in TWIN_WEIGHTS}
    shared = {n: given[n] for n in SHARED_INPUTS}
    per_example = {n: given[n] for n in ['x']}
    grad_fn = _jax.value_and_grad(_loss, argnums=(0, 1))

    def one_microbatch(ex, loss_target):
        ex = dict(ex)
        diff = ex.pop(TWIN_DIFF_INPUT)
        return grad_fn(weights, diff, {**shared, **ex}, loss_target)

    if N_MICROBATCH == 1:
        loss, (grad_w, grad_x) = one_microbatch(per_example, given["loss_target"])
    else:
        def body(carry, xs):
            loss_sum, grad_sum = carry
            l_k, (gw_k, gx_k) = one_microbatch(xs[0], xs[1])
            with _jax.named_scope("update"):
                return (loss_sum + l_k, _jax.tree.map(_jnp.add, grad_sum, gw_k)), gx_k

        init = (_jnp.zeros((), _jnp.float32), _jax.tree.map(_jnp.zeros_like, weights))
        (loss, grad_w), grad_x = _jax.lax.scan(body, init, (per_example, given["loss_target"]))
    with _jax.named_scope("update"):
        delta_w, new_m, new_v = {}, {}, {}
        for n in TWIN_WEIGHTS:
            delta_w[n], new_m[n], new_v[n] = _adamw(weights[n], grad_w[n], given["m_" + n], given["v_" + n])
    return (loss, grad_x, *[grad_w[n] for n in TWIN_WEIGHTS], *[delta_w[n] for n in TWIN_WEIGHTS],
            *[new_m[n] for n in TWIN_WEIGHTS], *[new_v[n] for n in TWIN_WEIGHTS])
```

```python
import functools

import jax
import jax.numpy as jnp
from jax import lax
from jax.experimental import pallas as pl
from jax.experimental.pallas import tpu as pltpu

F32 = jnp.float32
MX = jnp.bfloat16
WIRE = jnp.bfloat16

D_MODEL = 1024
HEAD_DIM = 64
ATTN_W = 512
KV_W = 128
BLK = 128
CONV_W = 256
CONV_K = 31
LRU_W = 256
LRU_K = 4
LRU_C = 8.0
IN_W = 1792
D_FF = 4096
FF_BLK = 512
N_DEV = 8
RMS_EPS = 1e-6
LN_EPS = 1e-5
MASK_VALUE = -1e30
SCALE = HEAD_DIM ** -0.5
CONV_HALO = 32
LRU_HALO = 8
POST_TILE = 256
Q0, K0, V0, CV0, CG0, RX0, RG0 = 0, 512, 640, 768, 1024, 1280, 1536
R_CONV_B, R_LN_G, R_LN_B, R_LCONV_B, R_BA, R_BX, R_LAM, R_LCW = 0, 1, 2, 3, 4, 5, 6, 8

ADAM_LR, ADAM_B1, ADAM_B2, ADAM_EPS, ADAM_WD, ADAM_STEP = 0.001, 0.9, 0.999, 1e-08, 0.01, 10

VMEM_LIMIT = 56 * 1024 * 1024
MESH = pl.DeviceIdType.MESH


def _tile(t, cap=512):
    return min(cap, t)


def _dot(a, b):
    return jnp.dot(a.astype(MX), b.astype(MX), preferred_element_type=F32)


def _dot_nt(a, b):
    return lax.dot_general(a.astype(MX), b.astype(MX), (((1,), (1,)), ((), ())), preferred_element_type=F32)


def _dot_tn(a, b):
    return lax.dot_general(a.astype(MX), b.astype(MX), (((0,), (0,)), ((), ())), preferred_element_type=F32)


def _const_spec(shape):
    nd = len(shape)
    return pl.BlockSpec(shape, lambda *_: (0,) * nd, pipeline_mode=pl.Buffered(1))


def _acc_spec(shape):
    nd = len(shape)
    return pl.BlockSpec(shape, lambda *_: (0,) * nd)


def _params(sem):
    return pltpu.CompilerParams(dimension_semantics=sem, vmem_limit_bytes=VMEM_LIMIT)


def _sigmoid(x):
    return jax.nn.sigmoid(x)


def _rms_fwd(x, g):
    r = lax.rsqrt(jnp.mean(x * x, axis=-1, keepdims=True) + RMS_EPS)
    xh = x * r
    return xh * g, xh, r


def _rms_bwd(dy, xh, r, g):
    t = dy * g
    dx = r * (t - xh * jnp.mean(t * xh, axis=-1, keepdims=True))
    return dx, jnp.sum(dy * xh, axis=0, keepdims=True)


_GROUPS = ((0, 512), (512, 768), (768, 1024))


def _group_rms_fwd(y, g):
    parts = [_rms_fwd(y[:, a:b], g[:, a:b]) for a, b in _GROUPS]
    return (jnp.concatenate([p[0] for p in parts], axis=1),
            jnp.concatenate([p[1] for p in parts], axis=1),
            [p[2] for p in parts])


def _gelu(x):
    c = 0.7978845608028654
    u = c * (x + 0.044715 * x * x * x)
    th = jnp.tanh(u)
    val = 0.5 * x * (1.0 + th)
    grad = 0.5 * (1.0 + th) + 0.5 * x * (1.0 - th * th) * c * (1.0 + 3.0 * 0.044715 * x * x)
    return val, grad


def _neg_expm1(x):
    series = -x * (1.0 + x * (0.5 + x * (1.0 / 6.0 + x * (1.0 / 24.0))))
    return jnp.where(x > -0.02, series, 1.0 - jnp.exp(x))


def _ln_in(h, g1, w_in, name):
    t = h.shape[0]
    tm = _tile(t)

    def body(h_ref, g_ref, w_ref, z_ref, hn_ref):
        y, _, _ = _rms_fwd(h_ref[...], g_ref[...])
        hn = y.astype(MX)
        hn_ref[...] = hn
        z_ref[...] = jnp.dot(hn, w_ref[...], preferred_element_type=F32)

    return pl.pallas_call(
        body, name=name, grid=(t // tm,),
        in_specs=[pl.BlockSpec((tm, D_MODEL), lambda i: (i, 0)), _const_spec((1, D_MODEL)),
                  _const_spec((D_MODEL, IN_W))],
        out_specs=[pl.BlockSpec((tm, IN_W), lambda i: (i, 0)), pl.BlockSpec((tm, D_MODEL), lambda i: (i, 0))],
        out_shape=[jax.ShapeDtypeStruct((t, IN_W), F32), jax.ShapeDtypeStruct((t, D_MODEL), MX)],
        compiler_params=_params(("arbitrary",)),
    )(h, g1, w_in)


def _band2(kb, g):
    lo = lax.broadcasted_iota(jnp.int32, kb.shape, 1) < HEAD_DIM
    kr = pltpu.roll(kb, HEAD_DIM, 1)
    if g == 0:
        top, bot = jnp.where(lo, kb, 0.0), jnp.where(lo, 0.0, kr)
    else:
        top, bot = jnp.where(lo, kr, 0.0), jnp.where(lo, 0.0, kb)
    return jnp.concatenate([top, bot], axis=0)


def _unband2(d2, g):
    top, bot = d2[0:2 * BLK], d2[2 * BLK:4 * BLK]
    lo = lax.broadcasted_iota(jnp.int32, top.shape, 1) < HEAD_DIM
    if g == 0:
        return jnp.where(lo, top + pltpu.roll(bot, HEAD_DIM, 1), 0.0)
    return jnp.where(lo, 0.0, bot + pltpu.roll(top, HEAD_DIM, 1))


def _band_mask(first_block):
    rr = lax.broadcasted_iota(jnp.int32, (BLK, 2 * BLK), 0)
    cc = lax.broadcasted_iota(jnp.int32, (BLK, 2 * BLK), 1)
    m = jnp.logical_and(cc > rr, cc <= rr + BLK)
    m = jnp.logical_and(m, cc >= BLK * first_block.astype(jnp.int32))
    return jnp.concatenate([m, m], axis=1)


def _attn_probs(qp, k2, mask2, sink_row):
    s = _dot_nt(qp, k2) * SCALE
    s = jnp.where(mask2, s, MASK_VALUE)
    w = 2 * BLK
    out, psink = [], []
    for hh in range(2):
        sh = s[:, hh * w:(hh + 1) * w]
        sk = sink_row[:, hh * w:hh * w + 1]
        m = jnp.maximum(jnp.max(sh, axis=1, keepdims=True), sk)
        p = jnp.exp(sh - m)
        es = jnp.exp(sk - m)
        inv = 1.0 / (jnp.sum(p, axis=1, keepdims=True) + es)
        out.append(p * inv)
        psink.append(es * inv)
    return jnp.concatenate(out, axis=1), psink


def _scan_fwd(a, b, tm):
    rows = lax.broadcasted_iota(jnp.int32, a.shape, 0)
    d = 1
    while d < tm:
        keep = rows >= d
        a_sh = jnp.where(keep, pltpu.roll(a, d, 0), 1.0)
        b_sh = jnp.where(keep, pltpu.roll(b, d, 0), 0.0)
        b = a * b_sh + b
        a = a * a_sh
        d *= 2
    return a, b


def _scan_bwd(c, b, tm):
    rows = lax.broadcasted_iota(jnp.int32, c.shape, 0)
    d = 1
    while d < tm:
        keep = rows < tm - d
        c_sh = jnp.where(keep, pltpu.roll(c, tm - d, 0), 1.0)
        b_sh = jnp.where(keep, pltpu.roll(b, tm - d, 0), 0.0)
        b = c * b_sh + b
        c = c * c_sh
        d *= 2
    return b


def _conv_branch(z_ref, zh_ref, cw_ref, pv_ref, uext, first, tm):
    cv = z_ref[:, CV0:CV0 + CONV_W]
    sg = _sigmoid(z_ref[:, CG0:CG0 + CONV_W])
    hrow = BLK - CONV_HALO
    uh = zh_ref[hrow:BLK, CV0:CV0 + CONV_W] * _sigmoid(zh_ref[hrow:BLK, CG0:CG0 + CONV_W])
    uext[0:CONV_HALO, :] = jnp.where(first, 0.0, uh)
    uext[CONV_HALO:CONV_HALO + tm, :] = cv * sg
    uc = jnp.broadcast_to(pv_ref[R_CONV_B:R_CONV_B + 1, :], (tm, CONV_W))
    for k in range(CONV_K):
        uc = uc + cw_ref[k:k + 1, :] * uext[pl.ds(CONV_HALO - (CONV_K - 1) + k, tm), :]
    return cv, sg, uc


def _ln_silu(uc, pv_ref):
    mu = jnp.mean(uc, axis=-1, keepdims=True)
    xc = uc - mu
    rs = lax.rsqrt(jnp.mean(xc * xc, axis=-1, keepdims=True) + LN_EPS)
    xh = xc * rs
    ln = xh * pv_ref[R_LN_G:R_LN_G + 1, :] + pv_ref[R_LN_B:R_LN_B + 1, :]
    sg = _sigmoid(ln)
    return xh, rs, ln, sg


def _lru_gates(z_ref, zh_ref, pv_ref, wa_ref, wx_ref, rxext, first, tm):
    rxext[0:LRU_HALO, :] = jnp.where(first, 0.0, zh_ref[BLK - LRU_HALO:BLK, RX0:RX0 + LRU_W])
    rxext[LRU_HALO:LRU_HALO + tm, :] = z_ref[:, RX0:RX0 + LRU_W]
    xc = jnp.broadcast_to(pv_ref[R_LCONV_B:R_LCONV_B + 1, :], (tm, LRU_W))
    for k in range(LRU_K):
        xc = xc + pv_ref[R_LCW + k:R_LCW + k + 1, :] * rxext[pl.ds(LRU_HALO - (LRU_K - 1) + k, tm), :]
    r = _sigmoid(_dot(xc, wa_ref[...]) + pv_ref[R_BA:R_BA + 1, :])
    ig = _sigmoid(_dot(xc, wx_ref[...]) + pv_ref[R_BX:R_BX + 1, :])
    lam = pv_ref[R_LAM:R_LAM + 1, :]
    sp = jnp.log1p(jnp.exp(-lam))
    la = (-LRU_C * r) * sp
    a = jnp.exp(la)
    mult = jnp.sqrt(_neg_expm1(2.0 * la))
    return xc, r, ig, sp, la, a, mult


def _mixer_in_specs(tm, tile_of):
    hb = tm // BLK
    return [
        pl.BlockSpec((tm, IN_W), lambda i: (tile_of(i), 0)),
        pl.BlockSpec((BLK, IN_W), lambda i: (jnp.maximum(tile_of(i) * hb - 1, 0), 0)),
        _const_spec((8, 4 * BLK)),
        _const_spec((32, CONV_W)),
        _const_spec((16, CONV_W)),
        _const_spec((LRU_W, LRU_W)),
        _const_spec((LRU_W, LRU_W)),
    ]


def _mixer_fwd(z, sink, cw, pv, wa, wx, name):
    t = z.shape[0]
    tm = _tile(t)
    nb = tm // BLK

    def body(z_ref, zh_ref, sink_ref, cw_ref, pv_ref, wa_ref, wx_ref, y_ref, hl_ref, uc_ref, uext, rxext, hcar):
        i = pl.program_id(0)
        first = i == 0

        @pl.when(first)
        def _():
            hcar[...] = jnp.zeros_like(hcar)

        for b in range(nb):
            rows = slice(b * BLK, (b + 1) * BLK)
            prev = zh_ref if b == 0 else z_ref
            prow = slice(0, BLK) if b == 0 else slice((b - 1) * BLK, b * BLK)
            kb = jnp.concatenate([prev[prow, K0:K0 + KV_W], z_ref[rows, K0:K0 + KV_W]], axis=0)
            vb = jnp.concatenate([prev[prow, V0:V0 + KV_W], z_ref[rows, V0:V0 + KV_W]], axis=0)
            mask2 = _band_mask(jnp.logical_and(first, b == 0))
            for g in range(2):
                k2, v2 = _band2(kb, g), _band2(vb, g)
                for p in (2 * g, 2 * g + 1):
                    cols = slice(p * BLK, (p + 1) * BLK)
                    prob, _ = _attn_probs(z_ref[rows, cols], k2, mask2, sink_ref[p:p + 1, :])
                    y_ref[rows, cols] = _dot(prob, v2)
        _, _, uc = _conv_branch(z_ref, zh_ref, cw_ref, pv_ref, uext, first, tm)
        uc_ref[...] = uc
        _, _, ln, sg = _ln_silu(uc, pv_ref)
        y_ref[:, ATTN_W:ATTN_W + CONV_W] = ln * sg
        xc, _, ig, _, _, a, mult = _lru_gates(z_ref, zh_ref, pv_ref, wa_ref, wx_ref, rxext, first, tm)
        acum, h = _scan_fwd(a, mult * (ig * xc), tm)
        h = h + acum * hcar[0:1, :]
        hl_ref[...] = h
        hcar[0:1, :] = h[tm - 1:tm, :]
        gl, _ = _gelu(z_ref[:, RG0:RG0 + LRU_W])
        y_ref[:, ATTN_W + CONV_W:ATTN_W + CONV_W + LRU_W] = h * gl

    return pl.pallas_call(
        body, name=name, grid=(t // tm,),
        in_specs=_mixer_in_specs(tm, lambda i: i),
        out_specs=[pl.BlockSpec((tm, D_MODEL), lambda i: (i, 0)), pl.BlockSpec((tm, LRU_W), lambda i: (i, 0)),
                   pl.BlockSpec((tm, CONV_W), lambda i: (i, 0))],
        out_shape=[jax.ShapeDtypeStruct((t, D_MODEL), F32), jax.ShapeDtypeStruct((t, LRU_W), F32),
                   jax.ShapeDtypeStruct((t, CONV_W), F32)],
        scratch_shapes=[pltpu.VMEM((tm + CONV_HALO, CONV_W), F32), pltpu.VMEM((tm + LRU_HALO, LRU_W), F32),
                        pltpu.VMEM((8, LRU_W), F32)],
        compiler_params=_params(("arbitrary",)),
    )(z, z, sink, cw, pv, wa, wx)


def _mixer_bwd(dy, z, ycat, hl, uc, sink, cw, pv, wa, wx, name):
    t = z.shape[0]
    tm = _tile(t)
    nt = t // tm
    nb = tm // BLK
    rev = lambda i: nt - 1 - i

    def body(dy_ref, z_ref, zh_ref, sink_ref, cw_ref, pv_ref, wa_ref, wx_ref, y_ref, hl_ref, hlh_ref, uc_ref,
             dz_ref, dsink_ref, dcw_ref, dpv_ref, dwa_ref, dwx_ref,
             uext, rxext, dkext, dvext, ducext, dxcext, kcar, vcar, uccar, xccar, gcar):
        i = pl.program_id(0)
        first = i == nt - 1

        @pl.when(i == 0)
        def _():
            for car in (kcar, vcar, uccar, xccar, gcar):
                car[...] = jnp.zeros_like(car)
            dsink_ref[...] = jnp.zeros_like(dsink_ref)
            dcw_ref[...] = jnp.zeros_like(dcw_ref)
            dpv_ref[...] = jnp.zeros_like(dpv_ref)
            dwa_ref[...] = jnp.zeros_like(dwa_ref)
            dwx_ref[...] = jnp.zeros_like(dwx_ref)

        def addrow(r, val):
            dpv_ref[r:r + 1, :] += jnp.sum(val, axis=0, keepdims=True)

        dkext[0:tm, :] = jnp.zeros((tm, KV_W), F32)
        dvext[0:tm, :] = jnp.zeros((tm, KV_W), F32)
        dkext[tm:tm + BLK, :] = kcar[...]
        dvext[tm:tm + BLK, :] = vcar[...]
        lane512 = lax.broadcasted_iota(jnp.int32, (1, 4 * BLK), 1) < 2 * BLK
        lo = lax.broadcasted_iota(jnp.int32, (BLK, BLK), 1) < HEAD_DIM
        for b in range(nb):
            rows = slice(b * BLK, (b + 1) * BLK)
            prev = zh_ref if b == 0 else z_ref
            prow = slice(0, BLK) if b == 0 else slice((b - 1) * BLK, b * BLK)
            kb = jnp.concatenate([prev[prow, K0:K0 + KV_W], z_ref[rows, K0:K0 + KV_W]], axis=0)
            vb = jnp.concatenate([prev[prow, V0:V0 + KV_W], z_ref[rows, V0:V0 + KV_W]], axis=0)
            mask2 = _band_mask(jnp.logical_and(first, b == 0))
            dkb = jnp.zeros((2 * BLK, KV_W), F32)
            dvb = jnp.zeros((2 * BLK, KV_W), F32)
            for g in range(2):
                k2, v2 = _band2(kb, g), _band2(vb, g)
                for p in (2 * g, 2 * g + 1):
                    cols = slice(p * BLK, (p + 1) * BLK)
                    qp = z_ref[rows, cols]
                    do = dy_ref[rows, cols]
                    prob, psink = _attn_probs(qp, k2, mask2, sink_ref[p:p + 1, :])
                    dlt = do * y_ref[rows, cols]
                    d0 = jnp.sum(jnp.where(lo, dlt, 0.0), axis=1, keepdims=True)
                    d1 = jnp.sum(jnp.where(lo, 0.0, dlt), axis=1, keepdims=True)
                    dp = _dot_nt(do, v2)
                    dl = jnp.concatenate([jnp.broadcast_to(d0, (BLK, 2 * BLK)), jnp.broadcast_to(d1, (BLK, 2 * BLK))], axis=1)
                    ds = prob * (dp - dl)
                    s0 = jnp.sum(psink[0] * d0, axis=0, keepdims=True)
                    s1 = jnp.sum(psink[1] * d1, axis=0, keepdims=True)
                    dsink_ref[p:p + 1, :] += -jnp.where(lane512, s0, s1)
                    draw = ds * SCALE
                    dz_ref[rows, cols] = _dot(draw, k2).astype(dz_ref.dtype)
                    dkb = dkb + _unband2(_dot_tn(draw, qp), g)
                    dvb = dvb + _unband2(_dot_tn(prob, do), g)
            band = slice(b * BLK, (b + 2) * BLK)
            dkext[band, :] += dkb
            dvext[band, :] += dvb
        dz_ref[:, K0:K0 + KV_W] = dkext[BLK:BLK + tm, :].astype(dz_ref.dtype)
        dz_ref[:, V0:V0 + KV_W] = dvext[BLK:BLK + tm, :].astype(dz_ref.dtype)
        kcar[...] = dkext[0:BLK, :]
        vcar[...] = dvext[0:BLK, :]

        cv, sgc, _ = _conv_branch(z_ref, zh_ref, cw_ref, pv_ref, uext, first, tm)
        xh, rs, ln, sg = _ln_silu(uc_ref[...], pv_ref)
        dln = dy_ref[:, ATTN_W:ATTN_W + CONV_W] * (sg * (1.0 + ln * (1.0 - sg)))
        addrow(R_LN_G, dln * xh)
        addrow(R_LN_B, dln)
        dxh = dln * pv_ref[R_LN_G:R_LN_G + 1, :]
        duc = rs * (dxh - jnp.mean(dxh, axis=-1, keepdims=True) - xh * jnp.mean(dxh * xh, axis=-1, keepdims=True))
        addrow(R_CONV_B, duc)
        ducext[0:tm, :] = duc
        ducext[tm:tm + CONV_HALO, :] = uccar[...]
        uccar[...] = duc[0:CONV_HALO, :]
        du = jnp.zeros((tm, CONV_W), F32)
        for k in range(CONV_K):
            dcw_ref[k:k + 1, :] += jnp.sum(duc * uext[pl.ds(CONV_HALO - (CONV_K - 1) + k, tm), :], axis=0, keepdims=True)
            du = du + cw_ref[k:k + 1, :] * ducext[pl.ds(CONV_K - 1 - k, tm), :]
        dz_ref[:, CV0:CV0 + CONV_W] = (du * sgc).astype(dz_ref.dtype)
        dz_ref[:, CG0:CG0 + CONV_W] = (du * cv * sgc * (1.0 - sgc)).astype(dz_ref.dtype)

        xc, r, ig, sp, la, a, mult = _lru_gates(z_ref, zh_ref, pv_ref, wa_ref, wx_ref, rxext, first, tm)
        h = hl_ref[...]
        rowi = lax.broadcasted_iota(jnp.int32, (tm, LRU_W), 0)
        hlast = jnp.where(first, 0.0, hlh_ref[7:8, :])
        hprev = jnp.where(rowi == 0, hlast, pltpu.roll(h, 1, 0))
        dyl = dy_ref[:, ATTN_W + CONV_W:ATTN_W + CONV_W + LRU_W]
        gl, dgl = _gelu(z_ref[:, RG0:RG0 + LRU_W])
        dz_ref[:, RG0:RG0 + LRU_W] = (dyl * h * dgl).astype(dz_ref.dtype)
        dh = dyl * gl + jnp.where(rowi == tm - 1, gcar[0:1, :], 0.0)
        c = jnp.where(rowi == tm - 1, 0.0, pltpu.roll(a, tm - 1, 0))
        gg = _scan_bwd(c, dh, tm)
        gcar[0:1, :] = a[0:1, :] * gg[0:1, :]
        dmult = gg * (ig * xc)
        dig = gg * mult * xc
        dxc = gg * mult * ig
        dla = gg * hprev * a - dmult * a * a / mult
        dr = dla * (-LRU_C * sp)
        lam = pv_ref[R_LAM:R_LAM + 1, :]
        dpv_ref[R_LAM:R_LAM + 1, :] += jnp.sum(dla * (-LRU_C * r), axis=0, keepdims=True) * (-_sigmoid(-lam))
        dpa = dr * r * (1.0 - r)
        dpx = dig * ig * (1.0 - ig)
        addrow(R_BA, dpa)
        addrow(R_BX, dpx)
        dxc = dxc + _dot_nt(dpa, wa_ref[...]) + _dot_nt(dpx, wx_ref[...])
        dwa_ref[...] += _dot_tn(xc, dpa)
        dwx_ref[...] += _dot_tn(xc, dpx)
        addrow(R_LCONV_B, dxc)
        dxcext[0:tm, :] = dxc
        dxcext[tm:tm + LRU_HALO, :] = xccar[...]
        xccar[...] = dxc[0:LRU_HALO, :]
        drx = jnp.zeros((tm, LRU_W), F32)
        for k in range(LRU_K):
            addrow(R_LCW + k, dxc * rxext[pl.ds(LRU_HALO - (LRU_K - 1) + k, tm), :])
            drx = drx + pv_ref[R_LCW + k:R_LCW + k + 1, :] * dxcext[pl.ds(LRU_K - 1 - k, tm), :]
        dz_ref[:, RX0:RX0 + LRU_W] = drx.astype(dz_ref.dtype)

    tile = lambda w: pl.BlockSpec((tm, w), lambda i: (rev(i), 0))
    in_specs = [tile(D_MODEL)] + _mixer_in_specs(tm, rev) + [
        tile(D_MODEL), tile(LRU_W),
        pl.BlockSpec((8, LRU_W), lambda i: (jnp.maximum(rev(i) * (tm // 8) - 1, 0), 0)),
        tile(CONV_W)]
    return pl.pallas_call(
        body, name=name, grid=(nt,),
        in_specs=in_specs,
        out_specs=[tile(IN_W), _acc_spec((8, 4 * BLK)), _acc_spec((32, CONV_W)), _acc_spec((16, CONV_W)),
                   _acc_spec((LRU_W, LRU_W)), _acc_spec((LRU_W, LRU_W))],
        out_shape=[jax.ShapeDtypeStruct((t, IN_W), MX), jax.ShapeDtypeStruct((8, 4 * BLK), F32),
                   jax.ShapeDtypeStruct((32, CONV_W), F32), jax.ShapeDtypeStruct((16, CONV_W), F32),
                   jax.ShapeDtypeStruct((LRU_W, LRU_W), F32), jax.ShapeDtypeStruct((LRU_W, LRU_W), F32)],
        scratch_shapes=[pltpu.VMEM((tm + CONV_HALO, CONV_W), F32), pltpu.VMEM((tm + LRU_HALO, LRU_W), F32),
                        pltpu.VMEM((tm + BLK, KV_W), F32), pltpu.VMEM((tm + BLK, KV_W), F32),
                        pltpu.VMEM((tm + CONV_HALO, CONV_W), F32), pltpu.VMEM((tm + LRU_HALO, LRU_W), F32),
                        pltpu.VMEM((BLK, KV_W), F32), pltpu.VMEM((BLK, KV_W), F32),
                        pltpu.VMEM((CONV_HALO, CONV_W), F32), pltpu.VMEM((LRU_HALO, LRU_W), F32),
                        pltpu.VMEM((8, LRU_W), F32)],
        compiler_params=_params(("arbitrary",)),
    )(dy, z, z, sink, cw, pv, wa, wx, ycat, hl, hl, uc)


def _post_fwd(ycat, h0, gmix, w_out, g2, w_up, w_down, name):
    t = h0.shape[0]
    tm = _tile(t, POST_TILE)
    nj = D_FF // FF_BLK

    def body(y_ref, h_ref, gm_ref, wo_ref, g2_ref, wu_ref, wd_ref, h1_ref, u_ref, h2_ref, ym_ref, hn_ref):
        ym, _, _ = _group_rms_fwd(y_ref[...], gm_ref[...])
        ym = ym.astype(MX)
        ym_ref[...] = ym
        h1 = h_ref[...] + jnp.dot(ym, wo_ref[...], preferred_element_type=F32)
        h1_ref[...] = h1
        hn, _, _ = _rms_fwd(h1, g2_ref[...])
        hn = hn.astype(MX)
        hn_ref[...] = hn
        acc = h1
        for j in range(nj):
            u = jnp.dot(hn, wu_ref[j], preferred_element_type=F32)
            u_ref[:, j * FF_BLK:(j + 1) * FF_BLK] = u.astype(u_ref.dtype)
            act = jnp.square(jnp.maximum(u, 0.0)).astype(MX)
            acc = acc + jnp.dot(act, wd_ref[j], preferred_element_type=F32)
        h2_ref[...] = acc

    tile = lambda w: pl.BlockSpec((tm, w), lambda i: (i, 0))
    return pl.pallas_call(
        body, name=name, grid=(t // tm,),
        in_specs=[tile(D_MODEL), tile(D_MODEL), _const_spec((1, D_MODEL)), _const_spec((D_MODEL, D_MODEL)),
                  _const_spec((1, D_MODEL)), _const_spec((nj, D_MODEL, FF_BLK)), _const_spec((nj, FF_BLK, D_MODEL))],
        out_specs=[tile(D_MODEL), tile(D_FF), tile(D_MODEL), tile(D_MODEL), tile(D_MODEL)],
        out_shape=[jax.ShapeDtypeStruct((t, D_MODEL), F32), jax.ShapeDtypeStruct((t, D_FF), MX),
                   jax.ShapeDtypeStruct((t, D_MODEL), F32), jax.ShapeDtypeStruct((t, D_MODEL), MX),
                   jax.ShapeDtypeStruct((t, D_MODEL), MX)],
        compiler_params=_params(("arbitrary",)),
    )(ycat, h0, gmix, w_out, g2, w_up, w_down)


def _post_bwd(dh2, u, h1, ycat, gmix, w_out, g2, w_up, w_down, name):
    t = h1.shape[0]
    tm = _tile(t, POST_TILE)
    nj = D_FF // FF_BLK

    def body(dh2_ref, u_ref, h1_ref, y_ref, gm_ref, wo_ref, g2_ref, wu_ref, wd_ref,
             dh1_ref, du_ref, dy_ref, dg2_ref, dgm_ref):
        i = pl.program_id(0)

        @pl.when(i == 0)
        def _():
            dg2_ref[...] = jnp.zeros_like(dg2_ref)
            dgm_ref[...] = jnp.zeros_like(dgm_ref)

        dh2 = dh2_ref[...]
        dh2b = dh2.astype(MX)
        dhn = jnp.zeros((tm, D_MODEL), F32)
        for j in range(nj):
            cols = slice(j * FF_BLK, (j + 1) * FF_BLK)
            da = _dot_nt(dh2b, wd_ref[j])
            du = (da * (2.0 * jnp.maximum(u_ref[:, cols].astype(F32), 0.0))).astype(MX)
            du_ref[:, cols] = du
            dhn = dhn + _dot_nt(du, wu_ref[j])
        _, xh, r = _rms_fwd(h1_ref[...], g2_ref[...])
        dx, dg = _rms_bwd(dhn, xh, r, g2_ref[...])
        dg2_ref[...] += dg
        dh1 = dh2 + dx
        dh1_ref[...] = dh1
        dym = _dot_nt(dh1, wo_ref[...])
        gm = gm_ref[...]
        _, yh, rr = _group_rms_fwd(y_ref[...], gm)
        outs, dgs = [], []
        for (a, b), rg in zip(_GROUPS, rr):
            dxg, dgg = _rms_bwd(dym[:, a:b], yh[:, a:b], rg, gm[:, a:b])
            outs.append(dxg)
            dgs.append(dgg)
        dy_ref[...] = jnp.concatenate(outs, axis=1)
        dgm_ref[...] += jnp.concatenate(dgs, axis=1)

    tile = lambda w: pl.BlockSpec((tm, w), lambda i: (i, 0))
    return pl.pallas_call(
        body, name=name, grid=(t // tm,),
        in_specs=[tile(D_MODEL), tile(D_FF), tile(D_MODEL), tile(D_MODEL), _const_spec((1, D_MODEL)),
                  _const_spec((D_MODEL, D_MODEL)), _const_spec((1, D_MODEL)),
                  _const_spec((nj, D_MODEL, FF_BLK)), _const_spec((nj, FF_BLK, D_MODEL))],
        out_specs=[tile(D_MODEL), tile(D_FF), tile(D_MODEL), _acc_spec((1, D_MODEL)), _acc_spec((1, D_MODEL))],
        out_shape=[jax.ShapeDtypeStruct((t, D_MODEL), F32), jax.ShapeDtypeStruct((t, D_FF), MX),
                   jax.ShapeDtypeStruct((t, D_MODEL), F32), jax.ShapeDtypeStruct((1, D_MODEL), F32),
                   jax.ShapeDtypeStruct((1, D_MODEL), F32)],
        compiler_params=_params(("arbitrary",)),
    )(dh2, u, h1, ycat, gmix, w_out, g2, w_up, w_down)


def _in_bwd(dz, h0, dh1, g1, w_in, name):
    t = h0.shape[0]
    tm = _tile(t)

    def body(dz_ref, h_ref, dh1_ref, g_ref, w_ref, dh0_ref, dg_ref):
        @pl.when(pl.program_id(0) == 0)
        def _():
            dg_ref[...] = jnp.zeros_like(dg_ref)

        dhn = _dot_nt(dz_ref[...], w_ref[...])
        _, xh, r = _rms_fwd(h_ref[...], g_ref[...])
        dx, dg = _rms_bwd(dhn, xh, r, g_ref[...])
        dg_ref[...] += dg
        dh0_ref[...] = dh1_ref[...] + dx

    tile = lambda w: pl.BlockSpec((tm, w), lambda i: (i, 0))
    return pl.pallas_call(
        body, name=name, grid=(t // tm,),
        in_specs=[tile(IN_W), tile(D_MODEL), tile(D_MODEL), _const_spec((1, D_MODEL)), _const_spec((D_MODEL, IN_W))],
        out_specs=[tile(D_MODEL), _acc_spec((1, D_MODEL))],
        out_shape=[jax.ShapeDtypeStruct((t, D_MODEL), F32), jax.ShapeDtypeStruct((1, D_MODEL), F32)],
        compiler_params=_params(("arbitrary",)),
    )(dz, h0, dh1, g1, w_in)


def _loss_head(h, gf, target, name):
    t = h.shape[0]
    tm = _tile(t)

    def body(h_ref, g_ref, t_ref, dh_ref, loss_ref, dg_ref):
        @pl.when(pl.program_id(0) == 0)
        def _():
            loss_ref[...] = jnp.zeros_like(loss_ref)
            dg_ref[...] = jnp.zeros_like(dg_ref)

        g = g_ref[...]
        y, xh, r = _rms_fwd(h_ref[...], g)
        err = y - t_ref[...]
        part = 0.5 * jnp.sum(jnp.mean(err * err, axis=-1, keepdims=True), axis=0, keepdims=True)
        loss_ref[...] += jnp.broadcast_to(part, loss_ref.shape)
        dx, dg = _rms_bwd(err * (1.0 / D_MODEL), xh, r, g)
        dg_ref[...] += dg
        dh_ref[...] = dx

    tile = pl.BlockSpec((tm, D_MODEL), lambda i: (i, 0))
    return pl.pallas_call(
        body, name=name, grid=(t // tm,),
        in_specs=[tile, _const_spec((1, D_MODEL)), tile],
        out_specs=[tile, _acc_spec((1, 128)), _acc_spec((1, D_MODEL))],
        out_shape=[jax.ShapeDtypeStruct((t, D_MODEL), F32), jax.ShapeDtypeStruct((1, 128), F32),
                   jax.ShapeDtypeStruct((1, D_MODEL), F32)],
        compiler_params=_params(("arbitrary",)),
    )(h, gf, target)


def _tn_matmul(x, y, bm, bn, name, relu2=False):
    t, m = x.shape
    n = y.shape[1]
    tk = _tile(t)
    nk = t // tk

    def body(x_ref, y_ref, o_ref):
        @pl.when(pl.program_id(2) == 0)
        def _():
            o_ref[...] = jnp.zeros_like(o_ref)

        xv = x_ref[...]
        if relu2:
            xv = jnp.square(jnp.maximum(xv.astype(F32), 0.0))
        o_ref[0] += _dot_tn(xv, y_ref[...])

    return pl.pallas_call(
        body, name=name, grid=(m // bm, n // bn, nk),
        in_specs=[pl.BlockSpec((tk, bm), lambda a, b, k: (k, a)), pl.BlockSpec((tk, bn), lambda a, b, k: (k, b))],
        out_specs=pl.BlockSpec((1, bm, bn), lambda a, b, k: (b, a, 0)),
        out_shape=jax.ShapeDtypeStruct((n // bn, m, bn), F32),
        compiler_params=_params(("arbitrary", "arbitrary", "arbitrary")),
    )(x, y)


def _me():
    return lax.axis_index("x"), lax.axis_index("y"), lax.axis_index("c")


def _all_gather(arrays, name):
    n = len(arrays)
    any_spec = pl.BlockSpec(memory_space=pl.ANY)

    def body(*refs):
        ins, outs = refs[:n], refs[n:2 * n]
        ssem, rsem, lsem = refs[2 * n:]
        x, y, c = _me()
        sib = (x, y, 1 - c)
        chips = [(1 - x, y), (x, 1 - y), (1 - x, 1 - y)]

        def copy(a, k, block, to, src=None):
            dst = outs[a].at[4 * block[0] + 2 * block[1] + block[2]]
            return pltpu.make_async_remote_copy(
                src_ref=dst if src is None else src, dst_ref=dst, send_sem=ssem.at[7 * a + k],
                recv_sem=rsem.at[7 * a + k], device_id=to, device_id_type=MESH)

        mine, first, passed = [], [], []
        for a in range(n):
            m = pltpu.make_async_copy(ins[a], outs[a].at[4 * x + 2 * y + c], lsem.at[a])
            m.start()
            mine.append(m)
            first.append(copy(a, 0, (x, y, c), sib, src=ins[a]))
            for j, chip in enumerate(chips):
                first.append(copy(a, 1 + j, (x, y, c), (*chip, c), src=ins[a]))
        for cp in first:
            cp.start()
        for a in range(n):
            for j, chip in enumerate(chips):
                copy(a, 1 + j, (*chip, c), (x, y, c)).wait_recv()
                cp = copy(a, 4 + j, (*chip, c), sib)
                cp.start()
                passed.append(cp)
        for a in range(n):
            copy(a, 0, (x, y, 1 - c), (x, y, c)).wait_recv()
            for j, chip in enumerate(chips):
                copy(a, 4 + j, (*chip, 1 - c), (x, y, c)).wait_recv()
        for cp in first + passed:
            cp.wait_send()
        for m in mine:
            m.wait()

    return pl.pallas_call(
        body, name=name,
        in_specs=[any_spec] * n, out_specs=[any_spec] * n,
        out_shape=[jax.ShapeDtypeStruct((N_DEV,) + a.shape, a.dtype) for a in arrays],
        scratch_shapes=[pltpu.SemaphoreType.DMA((7 * n,)), pltpu.SemaphoreType.DMA((7 * n,)),
                        pltpu.SemaphoreType.DMA((n,))],
    )(*arrays)


def _swap_sibling(arrays, name):
    n = len(arrays)
    any_spec = pl.BlockSpec(memory_space=pl.ANY)

    def body(*refs):
        ins, outs = refs[:n], refs[n:2 * n]
        ssem, rsem = refs[2 * n:]
        x, y, c = _me()
        copies = []
        for a in range(n):
            for q in range(4):
                cp = pltpu.make_async_remote_copy(
                    src_ref=ins[a].at[q, 1 - c], dst_ref=outs[a].at[q], send_sem=ssem.at[4 * a + q],
                    recv_sem=rsem.at[4 * a + q], device_id=(x, y, 1 - c), device_id_type=MESH)
                cp.start()
                copies.append(cp)
        for cp in copies:
            cp.wait()

    return pl.pallas_call(
        body, name=name,
        in_specs=[any_spec] * n, out_specs=[any_spec] * n,
        out_shape=[jax.ShapeDtypeStruct((4,) + a.shape[2:], a.dtype) for a in arrays],
        scratch_shapes=[pltpu.SemaphoreType.DMA((4 * n,)), pltpu.SemaphoreType.DMA((4 * n,))],
    )(*arrays)


def _swap_chips(arrays, name):
    n = len(arrays)
    any_spec = pl.BlockSpec(memory_space=pl.ANY)

    def body(*refs):
        ins, outs = refs[:n], refs[n:2 * n]
        ssem, rsem = refs[2 * n:]
        x, y, c = _me()
        chips = [(1 - x, y), (x, 1 - y), (1 - x, 1 - y)]
        copies = []
        for a in range(n):
            for j, (px, py) in enumerate(chips):
                cp = pltpu.make_async_remote_copy(
                    src_ref=ins[a].at[2 * px + py], dst_ref=outs[a].at[j], send_sem=ssem.at[3 * a + j],
                    recv_sem=rsem.at[3 * a + j], device_id=(px, py, c), device_id_type=MESH)
                cp.start()
                copies.append(cp)
        for cp in copies:
            cp.wait()

    return pl.pallas_call(
        body, name=name,
        in_specs=[any_spec] * n, out_specs=[any_spec] * n,
        out_shape=[jax.ShapeDtypeStruct((3,) + a.shape[1:], a.dtype) for a in arrays],
        scratch_shapes=[pltpu.SemaphoreType.DMA((3 * n,)), pltpu.SemaphoreType.DMA((3 * n,))],
    )(*arrays)


def _chip_sum(g, recv, core, name):
    _, _, r, c = g.shape
    br = r
    for cand in (512, 256, 128, 112, 64, 56, 32, 16, 8):
        if r % cand == 0 and cand * c * 4 <= 2 * 1024 * 1024:
            br = cand
            break

    def body(core_ref, g_ref, r_ref, o_ref):
        o_ref[...] = (g_ref[0, 0] + r_ref[0]).astype(o_ref.dtype)[None]

    return pl.pallas_call(
        body, name=name,
        grid_spec=pltpu.PrefetchScalarGridSpec(
            num_scalar_prefetch=1, grid=(4, r // br),
            in_specs=[pl.BlockSpec((1, 1, br, c), lambda q, i, core_ref: (q, core_ref[0], i, 0)),
                      pl.BlockSpec((1, br, c), lambda q, i, core_ref: (q, i, 0))],
            out_specs=pl.BlockSpec((1, br, c), lambda q, i, core_ref: (q, i, 0))),
        out_shape=jax.ShapeDtypeStruct((4, r, c), WIRE),
        compiler_params=_params(("arbitrary", "arbitrary")),
    )(core, g, recv)


def _adamw_math(w, g, m, v):
    m = ADAM_B1 * m + (1.0 - ADAM_B1) * g
    v = ADAM_B2 * v + (1.0 - ADAM_B2) * jnp.square(g)
    m_hat = m / (1.0 - ADAM_B1 ** ADAM_STEP)
    v_hat = v / (1.0 - ADAM_B2 ** ADAM_STEP)
    delta = -ADAM_LR * (m_hat / (jnp.sqrt(v_hat) + ADAM_EPS) + ADAM_WD * w)
    return delta, m, v


def _adamw_shard(g_own, g_sib, g_chips, chip, w, m, v, name):
    r, c = w.shape
    br = r
    for cand in (256, 128, 112, 64, 56, 32, 16, 8):
        if r % cand == 0:
            br = cand
            break

    def body(idx_ref, go_ref, gs_ref, gc_ref, w_ref, m_ref, v_ref, g_out, d_out, m_out, v_out):
        g = go_ref[0, 0] + gs_ref[0]
        for j in range(3):
            g = g + gc_ref[j].astype(F32)
        delta, mn, vn = _adamw_math(w_ref[...], g, m_ref[...], v_ref[...])
        g_out[...] = g
        d_out[...] = delta
        m_out[...] = mn
        v_out[...] = vn

    tile = pl.BlockSpec((br, c), lambda i, idx: (i, 0))
    return pl.pallas_call(
        body, name=name,
        grid_spec=pltpu.PrefetchScalarGridSpec(
            num_scalar_prefetch=1, grid=(r // br,),
            in_specs=[pl.BlockSpec((1, 1, br, c), lambda i, idx: (idx[0], idx[1], i, 0)),
                      pl.BlockSpec((1, br, c), lambda i, idx: (idx[0], i, 0)),
                      pl.BlockSpec((3, br, c), lambda i, idx: (0, i, 0)),
                      tile, tile, tile],
            out_specs=[tile, tile, tile, tile]),
        out_shape=[jax.ShapeDtypeStruct((r, c), F32)] * 4,
        compiler_params=_params(("arbitrary",)),
    )(chip, g_own, g_sib, g_chips, w, m, v)


def _adamw_small(parts, ws, ms, vs, name):
    n = len(parts)

    def body(*refs):
        p_refs, w_refs, m_refs, v_refs = (refs[k * n:(k + 1) * n] for k in range(4))
        outs = refs[4 * n:]
        for k in range(n):
            g = p_refs[k][0]
            for d in range(1, N_DEV):
                g = g + p_refs[k][d]
            delta, mn, vn = _adamw_math(w_refs[k][...], g, m_refs[k][...], v_refs[k][...])
            outs[k][...] = g
            outs[n + k][...] = delta
            outs[2 * n + k][...] = mn
            outs[3 * n + k][...] = vn

    shapes = [jax.ShapeDtypeStruct(w.shape, F32) for w in ws]
    res = pl.pallas_call(body, name=name, out_shape=shapes * 4,
                         compiler_params=pltpu.CompilerParams(vmem_limit_bytes=VMEM_LIMIT))(*parts, *ws, *ms, *vs)
    return res[:n], res[n:2 * n], res[2 * n:3 * n], res[3 * n:]


def _sum_parts(part, name):
    def body(p_ref, o_ref):
        g = p_ref[0]
        for d in range(1, N_DEV):
            g = g + p_ref[d]
        o_ref[...] = g

    return pl.pallas_call(body, name=name, out_shape=jax.ShapeDtypeStruct(part.shape[1:], F32))(part)


def _block_diag(w):
    out = jnp.zeros((LRU_W, LRU_W), w.dtype)
    for h in range(4):
        out = lax.dynamic_update_slice(out, w[h], (h * 64, h * 64))
    return out


def _unblock_diag(w):
    return jnp.concatenate([w[h * 64:(h + 1) * 64, h * 64:(h + 1) * 64] for h in range(4)], axis=0)


def _layer_params(p, l):
    row = lambda a: a[l].reshape(1, -1)
    sink = p["attn_sinks"][l]
    sink_rows = jnp.repeat(sink.reshape(4, 2), 2 * BLK, axis=1)
    sink_rows = jnp.concatenate([sink_rows, jnp.zeros((4, 4 * BLK), F32)], axis=0)
    cw = jnp.concatenate([p["conv_dw_w"][l], jnp.zeros((1, CONV_W), F32)], axis=0)
    pv = jnp.concatenate([
        row(p["conv_dw_b"]), row(p["conv_ln_g"]), row(p["conv_ln_b"]), row(p["lru_conv_b"]), row(p["lru_ba"]),
        row(p["lru_bx"]), row(p["lru_lambda"]), jnp.zeros((1, LRU_W), F32), p["lru_conv_w"][l],
        jnp.zeros((4, LRU_W), F32)], axis=0)
    return dict(
        g1=row(p["norm1"]), w_in=p["w_in"][l], sink=sink_rows, cw=cw, pv=pv,
        wa=_block_diag(p["lru_wa"][l]).astype(MX), wx=_block_diag(p["lru_wx"][l]).astype(MX),
        gmix=row(p["mix_norm"]), w_out=p["w_out"][l], g2=row(p["norm2"]), w_up=p["w_up"][l], w_down=p["w_down"][l])


def _local_step(x, target, p):
    depth = p["w_in"].shape[0]
    lp = [_layer_params(p, l) for l in range(depth)]
    saved = []
    h = x
    for l in range(depth):
        q = lp[l]
        z, hn1 = _ln_in(h, q["g1"], q["w_in"], f"ln_in{l}")
        ycat, hl, uc = _mixer_fwd(z, q["sink"], q["cw"], q["pv"], q["wa"], q["wx"], f"mixer_fwd{l}")
        h1, u, h2, ym, hn2 = _post_fwd(ycat, h, q["gmix"], q["w_out"], q["g2"], q["w_up"], q["w_down"], f"post_fwd{l}")
        saved.append(dict(h0=h, z=z, hn1=hn1, ycat=ycat, hl=hl, uc=uc, h1=h1, u=u, ym=ym, hn2=hn2))
        h = h2
    dh, loss, dgf = _loss_head(h, p["final_norm"].reshape(1, -1), target, "loss_head")
    grads = [None] * depth
    for l in reversed(range(depth)):
        q, s = lp[l], saved[l]
        dh1, du, dycat, dg2, dgm = _post_bwd(dh, s["u"], s["h1"], s["ycat"], q["gmix"], q["w_out"], q["g2"],
                                             q["w_up"], q["w_down"], f"post_bwd{l}")
        d_wdown = _tn_matmul(s["u"], dh, 1024, D_MODEL, f"dw_down{l}", relu2=True)
        d_wup = _tn_matmul(s["hn2"], du, D_MODEL, FF_BLK, f"dw_up{l}")
        d_wout = _tn_matmul(s["ym"], dh1, D_MODEL, D_MODEL, f"dw_out{l}")
        dz, dsink, dcw, dpv, dwa, dwx = _mixer_bwd(dycat, s["z"], s["ycat"], s["hl"], s["uc"], q["sink"], q["cw"],
                                                   q["pv"], q["wa"], q["wx"], f"mixer_bwd{l}")
        d_win = _tn_matmul(s["hn1"], dz, D_MODEL, 896, f"dw_in{l}")
        dh, dg1 = _in_bwd(dz, s["h0"], dh1, q["g1"], q["w_in"], f"in_bwd{l}")
        grads[l] = dict(
            norm1=dg1[0], w_in=d_win, attn_sinks=jnp.stack([dsink[0:4, 0], dsink[0:4, 2 * BLK]], axis=1).reshape(8),
            conv_dw_w=dcw[0:CONV_K], conv_dw_b=dpv[R_CONV_B], conv_ln_g=dpv[R_LN_G], conv_ln_b=dpv[R_LN_B],
            lru_conv_w=dpv[R_LCW:R_LCW + LRU_K], lru_conv_b=dpv[R_LCONV_B], lru_wa=_unblock_diag(dwa),
            lru_ba=dpv[R_BA].reshape(4, 64), lru_wx=_unblock_diag(dwx), lru_bx=dpv[R_BX].reshape(4, 64),
            lru_lambda=dpv[R_LAM], mix_norm=dgm[0], w_out=d_wout, norm2=dg2[0], w_up=d_wup, w_down=d_wdown)
    return loss, dh, grads, dgf[0]


_SMALL = ["norm1", "attn_sinks", "conv_dw_w", "conv_dw_b", "conv_ln_g", "conv_ln_b", "lru_conv_w", "lru_conv_b",
          "lru_wa", "lru_ba", "lru_wx", "lru_bx", "lru_lambda", "mix_norm", "norm2"]
_BIG = ["w_in", "w_out", "w_up", "w_down"]
_WEIGHTS = ["norm1", "w_in", "attn_sinks", "conv_dw_w", "conv_dw_b", "conv_ln_g", "conv_ln_b", "lru_conv_w",
            "lru_conv_b", "lru_wa", "lru_ba", "lru_wx", "lru_bx", "lru_lambda", "mix_norm", "w_out", "norm2", "w_up",
            "w_down", "final_norm"]


def kernel(x, norm1, w_in, attn_sinks, conv_dw_w, conv_dw_b, conv_ln_g, conv_ln_b, lru_conv_w, lru_conv_b, lru_wa, lru_ba, lru_wx, lru_bx, lru_lambda, mix_norm, w_out, norm2, w_up, w_down, final_norm, loss_target, m_norm1, m_w_in, m_attn_sinks, m_conv_dw_w, m_conv_dw_b, m_conv_ln_g, m_conv_ln_b, m_lru_conv_w, m_lru_conv_b, m_lru_wa, m_lru_ba, m_lru_wx, m_lru_bx, m_lru_lambda, m_mix_norm, m_w_out, m_norm2, m_w_up, m_w_down, m_final_norm, v_norm1, v_w_in, v_attn_sinks, v_conv_dw_w, v_conv_dw_b, v_conv_ln_g, v_conv_ln_b, v_lru_conv_w, v_lru_conv_b, v_lru_wa, v_lru_ba, v_lru_wx, v_lru_bx, v_lru_lambda, v_mix_norm, v_w_out, v_norm2, v_w_up, v_w_down, v_final_norm):
    w = dict(norm1=norm1, w_in=w_in, attn_sinks=attn_sinks, conv_dw_w=conv_dw_w, conv_dw_b=conv_dw_b,
             conv_ln_g=conv_ln_g, conv_ln_b=conv_ln_b, lru_conv_w=lru_conv_w, lru_conv_b=lru_conv_b, lru_wa=lru_wa,
             lru_ba=lru_ba, lru_wx=lru_wx, lru_bx=lru_bx, lru_lambda=lru_lambda, mix_norm=mix_norm, w_out=w_out,
             norm2=norm2, w_up=w_up, w_down=w_down, final_norm=final_norm)
    m = dict(norm1=m_norm1, w_in=m_w_in, attn_sinks=m_attn_sinks, conv_dw_w=m_conv_dw_w, conv_dw_b=m_conv_dw_b,
             conv_ln_g=m_conv_ln_g, conv_ln_b=m_conv_ln_b, lru_conv_w=m_lru_conv_w, lru_conv_b=m_lru_conv_b,
             lru_wa=m_lru_wa, lru_ba=m_lru_ba, lru_wx=m_lru_wx, lru_bx=m_lru_bx, lru_lambda=m_lru_lambda,
             mix_norm=m_mix_norm, w_out=m_w_out, norm2=m_norm2, w_up=m_w_up, w_down=m_w_down, final_norm=m_final_norm)
    v = dict(norm1=v_norm1, w_in=v_w_in, attn_sinks=v_attn_sinks, conv_dw_w=v_conv_dw_w, conv_dw_b=v_conv_dw_b,
             conv_ln_g=v_conv_ln_g, conv_ln_b=v_conv_ln_b, lru_conv_w=v_lru_conv_w, lru_conv_b=v_lru_conv_b,
             lru_wa=v_lru_wa, lru_ba=v_lru_ba, lru_wx=v_lru_wx, lru_bx=v_lru_bx, lru_lambda=v_lru_lambda,
             mix_norm=v_mix_norm, w_out=v_w_out, norm2=v_norm2, w_up=v_w_up, w_down=v_w_down, final_norm=v_final_norm)
    depth = w_in.shape[0]
    xi, yi, ci = _me()
    chip = 2 * xi + yi
    dev = 2 * chip + ci

    g_in, g_out, g_up, g_down = _all_gather([w_in.astype(MX), w_out.astype(MX), w_up.astype(MX), w_down.astype(MX)],
                                            "gather_weights")
    g_cw, g_lcw = _all_gather([conv_dw_w, lru_conv_w], "gather_conv_weights")
    cols = lambda g: jnp.moveaxis(g, 0, -2).reshape(g.shape[1:-1] + (N_DEV * g.shape[-1],))
    full = dict(w)
    full["w_in"] = cols(g_in)
    full["w_out"] = jnp.moveaxis(g_out, 0, 1).reshape(depth, D_MODEL, D_MODEL)
    full["w_up"] = jnp.moveaxis(g_up, 0, 1)
    full["w_down"] = jnp.moveaxis(g_down, 0, 1)
    full["conv_dw_w"] = cols(g_cw)
    full["lru_conv_w"] = cols(g_lcw)

    loss, grad_x, grads, d_final = _local_step(x[0], loss_target[0], full)

    idx = jnp.stack([chip, ci]).astype(jnp.int32)
    core = ci.reshape(1).astype(jnp.int32)
    chunked = {"w_in": [], "w_out": [], "w_up": [], "w_down": []}
    for l in range(depth):
        gl = grads[l]
        d_win = gl["w_in"]
        d_win = jnp.moveaxis(d_win.reshape(2, D_MODEL, 4, IN_W // N_DEV), 2, 1)
        chunked["w_in"].append(d_win.reshape(4, 2, D_MODEL, IN_W // N_DEV))
        chunked["w_out"].append(gl["w_out"].reshape(4, 2, D_MODEL // N_DEV, D_MODEL))
        chunked["w_up"].append(gl["w_up"].reshape(4, 2, D_MODEL, FF_BLK))
        chunked["w_down"].append(gl["w_down"].reshape(4, 2, FF_BLK, D_MODEL))
    own = [chunked[n][l] for n in _BIG for l in range(depth)]
    sib = _swap_sibling(own, "grad_swap_sibling")
    sums = [_chip_sum(o, s, core, f"grad_chip_sum{k}") for k, (o, s) in enumerate(zip(own, sib))]
    far = _swap_chips(sums, "grad_swap_chips")
    out = {}
    k = 0
    for n in _BIG:
        res = []
        for l in range(depth):
            res.append(_adamw_shard(own[k], sib[k], far[k], idx, w[n][l], m[n][l], v[n][l], f"adamw_{n}{l}"))
            k += 1
        out[n] = [jnp.stack([res[l][j] for l in range(depth)]) for j in range(4)]

    small = [jnp.stack([grads[l][n] for l in range(depth)]) for n in _SMALL] + [d_final.reshape(1, -1), loss]
    gathered = _all_gather(small, "gather_small_grads")
    shard = lambda a: lax.dynamic_slice_in_dim(a, dev * (a.shape[-1] // N_DEV), a.shape[-1] // N_DEV, axis=a.ndim - 1)
    flat = {"lru_wa": (depth, LRU_W, 64), "lru_wx": (depth, LRU_W, 64), "final_norm": (1, D_MODEL)}
    parts, ws, ms, vs = [], [], [], []
    for n, g in zip(_SMALL + ["final_norm"], gathered[:-1]):
        parts.append(shard(g) if n in ("conv_dw_w", "lru_conv_w") else g)
        shp = flat.get(n, w[n].shape)
        ws.append(w[n].reshape(shp))
        ms.append(m[n].reshape(shp))
        vs.append(v[n].reshape(shp))
    sg, sd, sm, sv = _adamw_small(parts, ws, ms, vs, "adamw_small")
    for j, n in enumerate(_SMALL + ["final_norm"]):
        out[n] = [a.reshape(w[n].shape) for a in (sg[j], sd[j], sm[j], sv[j])]
    loss_total = _sum_parts(gathered[-1], "loss_sum")[0, 0]

    result = [loss_total, grad_x[None]]
    for j in range(4):
        result += [out[n][j] for n in _WEIGHTS]
    return tuple(result)
```

```python
import types

import jax
import jax.numpy as jnp
from jax import lax
from jax.experimental import pallas as pl
from jax.experimental.pallas import tpu as pltpu

F32 = jnp.float32
MX = jnp.bfloat16
WIRE = jnp.bfloat16

D_MODEL = 1024
HEAD_DIM = 64
ATTN_W = 512
KV_W = 128
BLK = 128
CONV_W = 256
CONV_K = 31
LRU_W = 256
LRU_K = 4
LRU_C = 8.0
IN_W = 1792
D_FF = 4096
FF_BLK = 512
N_DEV = 8
IN_SHARD = IN_W // N_DEV
RMS_EPS = 1e-6
LN_EPS = 1e-5
MASK_VALUE = -1e30
SCALE = HEAD_DIM ** -0.5
CONV_HALO = 32
LRU_HALO = 8
POST_TILE = 256
Q0, K0, V0, CV0, CG0, RX0, RG0 = 0, 512, 640, 768, 1024, 1280, 1536
R_CONV_B, R_LN_G, R_LN_B, R_LCONV_B, R_BA, R_BX, R_LAM, R_LCW = 0, 1, 2, 3, 4, 5, 6, 8

ADAM_LR, ADAM_B1, ADAM_B2, ADAM_EPS, ADAM_WD, ADAM_STEP = 0.001, 0.9, 0.999, 1e-08, 0.01, 10

VMEM_LIMIT = 56 * 1024 * 1024
MESH = pl.DeviceIdType.MESH
ANY = pl.BlockSpec(memory_space=pl.ANY)


def _tile(t, cap=512):
    return min(cap, t)


def _dot(a, b):
    return jnp.dot(a.astype(MX), b.astype(MX), preferred_element_type=F32)


def _dot_nt(a, b):
    return lax.dot_general(a.astype(MX), b.astype(MX), (((1,), (1,)), ((), ())), preferred_element_type=F32)


def _dot_tn(a, b):
    return lax.dot_general(a.astype(MX), b.astype(MX), (((0,), (0,)), ((), ())), preferred_element_type=F32)


def _const_spec(shape):
    nd = len(shape)
    return pl.BlockSpec(shape, lambda *_: (0,) * nd, pipeline_mode=pl.Buffered(1))


def _acc_spec(shape):
    nd = len(shape)
    return pl.BlockSpec(shape, lambda *_: (0,) * nd)


def _sds(shape, dtype):
    return jax.ShapeDtypeStruct(shape, dtype)


def _sigmoid(x):
    return jax.nn.sigmoid(x)


def _rms_fwd(x, g):
    r = lax.rsqrt(jnp.mean(x * x, axis=-1, keepdims=True) + RMS_EPS)
    xh = x * r
    return xh * g, xh, r


def _rms_bwd(dy, xh, r, g):
    t = dy * g
    dx = r * (t - xh * jnp.mean(t * xh, axis=-1, keepdims=True))
    return dx, jnp.sum(dy * xh, axis=0, keepdims=True)


_GROUPS = ((0, 512), (512, 768), (768, 1024))


def _group_rms_fwd(y, g):
    parts = [_rms_fwd(y[:, a:b], g[:, a:b]) for a, b in _GROUPS]
    return (jnp.concatenate([p[0] for p in parts], axis=1),
            jnp.concatenate([p[1] for p in parts], axis=1),
            [p[2] for p in parts])


def _gelu(x):
    c = 0.7978845608028654
    u = c * (x + 0.044715 * x * x * x)
    th = jnp.tanh(u)
    val = 0.5 * x * (1.0 + th)
    grad = 0.5 * (1.0 + th) + 0.5 * x * (1.0 - th * th) * c * (1.0 + 3.0 * 0.044715 * x * x)
    return val, grad


def _neg_expm1(x):
    series = -x * (1.0 + x * (0.5 + x * (1.0 / 6.0 + x * (1.0 / 24.0))))
    return jnp.where(x > -0.02, series, 1.0 - jnp.exp(x))


def _me():
    return lax.axis_index("x"), lax.axis_index("y"), lax.axis_index("c")


def _gather_rider(arrays):
    arrays = list(arrays)
    n = len(arrays)

    def plan(ins, outs, sems):
        ssem, rsem, lsem = sems
        x, y, c = _me()
        chips = [(1 - x, y), (x, 1 - y), (1 - x, 1 - y)]

        def copy(a, k, block, to, own=False):
            dst = outs[a].at[4 * block[0] + 2 * block[1] + block[2]]
            return pltpu.make_async_remote_copy(
                src_ref=ins[a] if own else dst, dst_ref=dst, send_sem=ssem.at[7 * a + k],
                recv_sem=rsem.at[7 * a + k], device_id=to, device_id_type=MESH)

        return x, y, c, chips, copy, lsem

    def start(ins, outs, sems):
        x, y, c, chips, copy, lsem = plan(ins, outs, sems)
        for a in range(n):
            pltpu.make_async_copy(ins[a], outs[a].at[4 * x + 2 * y + c], lsem.at[a]).start()
            copy(a, 0, (x, y, c), (x, y, 1 - c), own=True).start()
            for j, chip in enumerate(chips):
                copy(a, 1 + j, (x, y, c), (*chip, c), own=True).start()

    def mid(ins, outs, sems):
        x, y, c, chips, copy, _ = plan(ins, outs, sems)
        for a in range(n):
            for j, chip in enumerate(chips):
                copy(a, 1 + j, (*chip, c), (x, y, c)).wait_recv()
                copy(a, 4 + j, (*chip, c), (x, y, 1 - c)).start()

    def finish(ins, outs, sems):
        x, y, c, chips, copy, lsem = plan(ins, outs, sems)
        for a in range(n):
            copy(a, 0, (x, y, 1 - c), (x, y, c)).wait_recv()
            for j, chip in enumerate(chips):
                copy(a, 4 + j, (*chip, 1 - c), (x, y, c)).wait_recv()
        for a in range(n):
            copy(a, 0, (x, y, c), (x, y, 1 - c), own=True).wait_send()
            for j, chip in enumerate(chips):
                copy(a, 1 + j, (x, y, c), (*chip, c), own=True).wait_send()
                copy(a, 4 + j, (*chip, c), (x, y, 1 - c)).wait_send()
            pltpu.make_async_copy(ins[a], outs[a].at[4 * x + 2 * y + c], lsem.at[a]).wait()

    return types.SimpleNamespace(
        arrays=arrays, out_shape=[_sds((N_DEV,) + a.shape, a.dtype) for a in arrays],
        scratch=[pltpu.SemaphoreType.DMA((7 * n,)), pltpu.SemaphoreType.DMA((7 * n,)), pltpu.SemaphoreType.DMA((n,))],
        start=start, mid=mid, finish=finish)


def _scatter_rider(arrays):
    arrays = list(arrays)
    n = len(arrays)

    def copies(ins, outs, sems):
        ssem, rsem = sems
        x, y, c = _me()
        out = []
        for a in range(n):
            for f in range(1, N_DEV):
                px = 1 - x if f & 4 else x
                py = 1 - y if f & 2 else y
                pc = 1 - c if f & 1 else c
                out.append(pltpu.make_async_remote_copy(
                    src_ref=ins[a].at[4 * px + 2 * py + pc], dst_ref=outs[a].at[f - 1], send_sem=ssem.at[7 * a + f - 1],
                    recv_sem=rsem.at[7 * a + f - 1], device_id=(px, py, pc), device_id_type=MESH))
        return out

    def start(ins, outs, sems):
        for cp in copies(ins, outs, sems):
            cp.start()

    def finish(ins, outs, sems):
        for cp in copies(ins, outs, sems):
            cp.wait()

    return types.SimpleNamespace(
        arrays=arrays, out_shape=[_sds((N_DEV - 1,) + a.shape[1:], a.dtype) for a in arrays],
        scratch=[pltpu.SemaphoreType.DMA((7 * n,)), pltpu.SemaphoreType.DMA((7 * n,))],
        start=start, mid=None, finish=finish)


def _call(body, name, grid, in_specs, out_specs, out_shape, scratch, operands, riders=()):
    n_in, n_out, n_scr = len(operands), len(out_shape), len(scratch)
    nsteps = grid[0] if grid else 1
    sizes = [(len(r.arrays), len(r.out_shape), len(r.scratch)) for r in riders]

    def wrapped(*refs):
        pos = n_in
        r_ins = []
        for ri, _, _ in sizes:
            r_ins.append(refs[pos:pos + ri])
            pos += ri
        outs = refs[pos:pos + n_out]
        pos += n_out
        r_outs = []
        for _, ro, _ in sizes:
            r_outs.append(refs[pos:pos + ro])
            pos += ro
        scr = refs[pos:pos + n_scr]
        pos += n_scr
        r_sems = []
        for _, _, rs in sizes:
            r_sems.append(refs[pos:pos + rs])
            pos += rs
        step = pl.program_id(0) if grid else 0

        def at(s, fn):
            if grid:
                pl.when(step == s)(fn)
            else:
                fn()

        for r, a, b, c in zip(riders, r_ins, r_outs, r_sems):
            at(0, lambda r=r, a=a, b=b, c=c: r.start(a, b, c))
        for r, a, b, c in zip(riders, r_ins, r_outs, r_sems):
            if r.mid is not None:
                at((3 * nsteps) // 4, lambda r=r, a=a, b=b, c=c: r.mid(a, b, c))
        if body is not None:
            body(*refs[:n_in], *outs, *scr)
        for r, a, b, c in zip(riders, r_ins, r_outs, r_sems):
            at(nsteps - 1, lambda r=r, a=a, b=b, c=c: r.finish(a, b, c))

    r_arrays = [a for r in riders for a in r.arrays]
    r_shapes = [s for r in riders for s in r.out_shape]
    kwargs = {}
    if grid:
        kwargs = dict(grid=grid, compiler_params=pltpu.CompilerParams(
            dimension_semantics=("arbitrary",) * len(grid), vmem_limit_bytes=VMEM_LIMIT))
    res = pl.pallas_call(
        wrapped, name=name,
        in_specs=list(in_specs) + [ANY] * len(r_arrays),
        out_specs=list(out_specs) + [ANY] * len(r_shapes),
        out_shape=list(out_shape) + r_shapes,
        scratch_shapes=list(scratch) + [s for r in riders for s in r.scratch],
        **kwargs,
    )(*operands, *r_arrays)
    host, rest = res[:n_out], res[n_out:]
    r_res = []
    for _, ro, _ in sizes:
        r_res.append(rest[:ro])
        rest = rest[ro:]
    return host, r_res


def _assemble_w_in(wg_ref, w_scr):
    for j in range(N_DEV):
        w_scr[:, j * IN_SHARD:(j + 1) * IN_SHARD] = wg_ref[j]


def _ln_in(h, g1, w_in, name):
    t = h.shape[0]
    tm = _tile(t)

    def body(h_ref, g_ref, wg_ref, z_ref, hn_ref, w_scr):
        @pl.when(pl.program_id(0) == 0)
        def _():
            _assemble_w_in(wg_ref, w_scr)

        y, _, _ = _rms_fwd(h_ref[...], g_ref[...])
        hn = y.astype(MX)
        hn_ref[...] = hn
        z_ref[...] = jnp.dot(hn, w_scr[...], preferred_element_type=F32)

    tile = lambda w: pl.BlockSpec((tm, w), lambda i: (i, 0))
    (z, hn), _ = _call(
        body, name, (t // tm,),
        [tile(D_MODEL), _const_spec((1, D_MODEL)), _const_spec((N_DEV, D_MODEL, IN_SHARD))],
        [tile(IN_W), tile(D_MODEL)], [_sds((t, IN_W), F32), _sds((t, D_MODEL), MX)],
        [pltpu.VMEM((D_MODEL, IN_W), MX)], [h, g1, w_in])
    return z, hn


def _band2(kb, g):
    lo = lax.broadcasted_iota(jnp.int32, kb.shape, 1) < HEAD_DIM
    kr = pltpu.roll(kb, HEAD_DIM, 1)
    if g == 0:
        top, bot = jnp.where(lo, kb, 0.0), jnp.where(lo, 0.0, kr)
    else:
        top, bot = jnp.where(lo, kr, 0.0), jnp.where(lo, 0.0, kb)
    return jnp.concatenate([top, bot], axis=0)


def _unband2(d2, g):
    top, bot = d2[0:2 * BLK], d2[2 * BLK:4 * BLK]
    lo = lax.broadcasted_iota(jnp.int32, top.shape, 1) < HEAD_DIM
    if g == 0:
        return jnp.where(lo, top + pltpu.roll(bot, HEAD_DIM, 1), 0.0)
    return jnp.where(lo, 0.0, bot + pltpu.roll(top, HEAD_DIM, 1))


def _band_mask(first_block):
    rr = lax.broadcasted_iota(jnp.int32, (BLK, 2 * BLK), 0)
    cc = lax.broadcasted_iota(jnp.int32, (BLK, 2 * BLK), 1)
    m = jnp.logical_and(cc > rr, cc <= rr + BLK)
    m = jnp.logical_and(m, cc >= BLK * first_block.astype(jnp.int32))
    return jnp.concatenate([m, m], axis=1)


def _attn_probs(qp, k2, mask2, sink_row):
    s = _dot_nt(qp, k2) * SCALE
    s = jnp.where(mask2, s, MASK_VALUE)
    w = 2 * BLK
    out, psink = [], []
    for hh in range(2):
        sh = s[:, hh * w:(hh + 1) * w]
        sk = sink_row[:, hh * w:hh * w + 1]
        m = jnp.maximum(jnp.max(sh, axis=1, keepdims=True), sk)
        p = jnp.exp(sh - m)
        es = jnp.exp(sk - m)
        inv = 1.0 / (jnp.sum(p, axis=1, keepdims=True) + es)
        out.append(p * inv)
        psink.append(es * inv)
    return jnp.concatenate(out, axis=1), psink


def _scan_fwd(a, b, tm):
    rows = lax.broadcasted_iota(jnp.int32, a.shape, 0)
    d = 1
    while d < tm:
        keep = rows >= d
        a_sh = jnp.where(keep, pltpu.roll(a, d, 0), 1.0)
        b_sh = jnp.where(keep, pltpu.roll(b, d, 0), 0.0)
        b = a * b_sh + b
        a = a * a_sh
        d *= 2
    return a, b


def _scan_bwd(c, b, tm):
    rows = lax.broadcasted_iota(jnp.int32, c.shape, 0)
    d = 1
    while d < tm:
        keep = rows < tm - d
        c_sh = jnp.where(keep, pltpu.roll(c, tm - d, 0), 1.0)
        b_sh = jnp.where(keep, pltpu.roll(b, tm - d, 0), 0.0)
        b = c * b_sh + b
        c = c * c_sh
        d *= 2
    return b


def _conv_branch(z_ref, zh_ref, cw_ref, pv_ref, uext, first, tm):
    cv = z_ref[:, CV0:CV0 + CONV_W]
    sg = _sigmoid(z_ref[:, CG0:CG0 + CONV_W])
    hrow = BLK - CONV_HALO
    uh = zh_ref[hrow:BLK, CV0:CV0 + CONV_W] * _sigmoid(zh_ref[hrow:BLK, CG0:CG0 + CONV_W])
    uext[0:CONV_HALO, :] = jnp.where(first, 0.0, uh)
    uext[CONV_HALO:CONV_HALO + tm, :] = cv * sg
    uc = jnp.broadcast_to(pv_ref[R_CONV_B:R_CONV_B + 1, :], (tm, CONV_W))
    for k in range(CONV_K):
        uc = uc + cw_ref[k:k + 1, :] * uext[pl.ds(CONV_HALO - (CONV_K - 1) + k, tm), :]
    return cv, sg, uc


def _ln_silu(uc, pv_ref):
    mu = jnp.mean(uc, axis=-1, keepdims=True)
    xc = uc - mu
    rs = lax.rsqrt(jnp.mean(xc * xc, axis=-1, keepdims=True) + LN_EPS)
    xh = xc * rs
    ln = xh * pv_ref[R_LN_G:R_LN_G + 1, :] + pv_ref[R_LN_B:R_LN_B + 1, :]
    sg = _sigmoid(ln)
    return xh, rs, ln, sg


def _lru_gates(z_ref, zh_ref, pv_ref, wa_ref, wx_ref, rxext, first, tm):
    rxext[0:LRU_HALO, :] = jnp.where(first, 0.0, zh_ref[BLK - LRU_HALO:BLK, RX0:RX0 + LRU_W])
    rxext[LRU_HALO:LRU_HALO + tm, :] = z_ref[:, RX0:RX0 + LRU_W]
    xc = jnp.broadcast_to(pv_ref[R_LCONV_B:R_LCONV_B + 1, :], (tm, LRU_W))
    for k in range(LRU_K):
        xc = xc + pv_ref[R_LCW + k:R_LCW + k + 1, :] * rxext[pl.ds(LRU_HALO - (LRU_K - 1) + k, tm), :]
    r = _sigmoid(_dot(xc, wa_ref[...]) + pv_ref[R_BA:R_BA + 1, :])
    ig = _sigmoid(_dot(xc, wx_ref[...]) + pv_ref[R_BX:R_BX + 1, :])
    lam = pv_ref[R_LAM:R_LAM + 1, :]
    sp = jnp.log1p(jnp.exp(-lam))
    la = (-LRU_C * r) * sp
    a = jnp.exp(la)
    mult = jnp.sqrt(_neg_expm1(2.0 * la))
    return xc, r, ig, sp, la, a, mult


def _mixer_in_specs(tm, tile_of):
    hb = tm // BLK
    return [
        pl.BlockSpec((tm, IN_W), lambda i: (tile_of(i), 0)),
        pl.BlockSpec((BLK, IN_W), lambda i: (jnp.maximum(tile_of(i) * hb - 1, 0), 0)),
        _const_spec((8, 4 * BLK)),
        _const_spec((32, CONV_W)),
        _const_spec((16, CONV_W)),
        _const_spec((LRU_W, LRU_W)),
        _const_spec((LRU_W, LRU_W)),
    ]


def _mixer_fwd(z, sink, cw, pv, wa, wx, name, riders=()):
    t = z.shape[0]
    tm = _tile(t)
    nb = tm // BLK

    def body(z_ref, zh_ref, sink_ref, cw_ref, pv_ref, wa_ref, wx_ref, y_ref, hl_ref, uc_ref, uext, rxext, hcar):
        i = pl.program_id(0)
        first = i == 0

        @pl.when(first)
        def _():
            hcar[...] = jnp.zeros_like(hcar)

        for b in range(nb):
            rows = slice(b * BLK, (b + 1) * BLK)
            prev = zh_ref if b == 0 else z_ref
            prow = slice(0, BLK) if b == 0 else slice((b - 1) * BLK, b * BLK)
            kb = jnp.concatenate([prev[prow, K0:K0 + KV_W], z_ref[rows, K0:K0 + KV_W]], axis=0)
            vb = jnp.concatenate([prev[prow, V0:V0 + KV_W], z_ref[rows, V0:V0 + KV_W]], axis=0)
            mask2 = _band_mask(jnp.logical_and(first, b == 0))
            for g in range(2):
                k2, v2 = _band2(kb, g), _band2(vb, g)
                for p in (2 * g, 2 * g + 1):
                    cols = slice(p * BLK, (p + 1) * BLK)
                    prob, _ = _attn_probs(z_ref[rows, cols], k2, mask2, sink_ref[p:p + 1, :])
                    y_ref[rows, cols] = _dot(prob, v2)
        _, _, uc = _conv_branch(z_ref, zh_ref, cw_ref, pv_ref, uext, first, tm)
        uc_ref[...] = uc
        _, _, ln, sg = _ln_silu(uc, pv_ref)
        y_ref[:, ATTN_W:ATTN_W + CONV_W] = ln * sg
        xc, _, ig, _, _, a, mult = _lru_gates(z_ref, zh_ref, pv_ref, wa_ref, wx_ref, rxext, first, tm)
        acum, h = _scan_fwd(a, mult * (ig * xc), tm)
        h = h + acum * hcar[0:1, :]
        hl_ref[...] = h
        hcar[0:1, :] = h[tm - 1:tm, :]
        gl, _ = _gelu(z_ref[:, RG0:RG0 + LRU_W])
        y_ref[:, ATTN_W + CONV_W:ATTN_W + CONV_W + LRU_W] = h * gl

    tile = lambda w: pl.BlockSpec((tm, w), lambda i: (i, 0))
    return _call(
        body, name, (t // tm,), _mixer_in_specs(tm, lambda i: i),
        [tile(D_MODEL), tile(LRU_W), tile(CONV_W)],
        [_sds((t, D_MODEL), F32), _sds((t, LRU_W), F32), _sds((t, CONV_W), F32)],
        [pltpu.VMEM((tm + CONV_HALO, CONV_W), F32), pltpu.VMEM((tm + LRU_HALO, LRU_W), F32), pltpu.VMEM((8, LRU_W), F32)],
        [z, z, sink, cw, pv, wa, wx], riders)


def _mixer_bwd(dy, z, ycat, hl, uc, sink, cw, pv, wa, wx, name, riders=()):
    t = z.shape[0]
    tm = _tile(t)
    nt = t // tm
    nb = tm // BLK
    rev = lambda i: nt - 1 - i

    def body(dy_ref, z_ref, zh_ref, sink_ref, cw_ref, pv_ref, wa_ref, wx_ref, y_ref, hl_ref, hlh_ref, uc_ref,
             dz_ref, dsink_ref, dcw_ref, dpv_ref, dwa_ref, dwx_ref,
             uext, rxext, dkext, dvext, ducext, dxcext, kcar, vcar, uccar, xccar, gcar):
        i = pl.program_id(0)
        first = i == nt - 1

        @pl.when(i == 0)
        def _():
            for car in (kcar, vcar, uccar, xccar, gcar):
                car[...] = jnp.zeros_like(car)
            dsink_ref[...] = jnp.zeros_like(dsink_ref)
            dcw_ref[...] = jnp.zeros_like(dcw_ref)
            dpv_ref[...] = jnp.zeros_like(dpv_ref)
            dwa_ref[...] = jnp.zeros_like(dwa_ref)
            dwx_ref[...] = jnp.zeros_like(dwx_ref)

        def addrow(r, val):
            dpv_ref[r:r + 1, :] += jnp.sum(val, axis=0, keepdims=True)

        dkext[0:tm, :] = jnp.zeros((tm, KV_W), F32)
        dvext[0:tm, :] = jnp.zeros((tm, KV_W), F32)
        dkext[tm:tm + BLK, :] = kcar[...]
        dvext[tm:tm + BLK, :] = vcar[...]
        lane512 = lax.broadcasted_iota(jnp.int32, (1, 4 * BLK), 1) < 2 * BLK
        lo = lax.broadcasted_iota(jnp.int32, (BLK, BLK), 1) < HEAD_DIM
        for b in range(nb):
            rows = slice(b * BLK, (b + 1) * BLK)
            prev = zh_ref if b == 0 else z_ref
            prow = slice(0, BLK) if b == 0 else slice((b - 1) * BLK, b * BLK)
            kb = jnp.concatenate([prev[prow, K0:K0 + KV_W], z_ref[rows, K0:K0 + KV_W]], axis=0)
            vb = jnp.concatenate([prev[prow, V0:V0 + KV_W], z_ref[rows, V0:V0 + KV_W]], axis=0)
            mask2 = _band_mask(jnp.logical_and(first, b == 0))
            dkb = jnp.zeros((2 * BLK, KV_W), F32)
            dvb = jnp.zeros((2 * BLK, KV_W), F32)
            for g in range(2):
                k2, v2 = _band2(kb, g), _band2(vb, g)
                for p in (2 * g, 2 * g + 1):
                    cols = slice(p * BLK, (p + 1) * BLK)
                    qp = z_ref[rows, cols]
                    do = dy_ref[rows, cols]
                    prob, psink = _attn_probs(qp, k2, mask2, sink_ref[p:p + 1, :])
                    dlt = do * y_ref[rows, cols]
                    d0 = jnp.sum(jnp.where(lo, dlt, 0.0), axis=1, keepdims=True)
                    d1 = jnp.sum(jnp.where(lo, 0.0, dlt), axis=1, keepdims=True)
                    dp = _dot_nt(do, v2)
                    dl = jnp.concatenate([jnp.broadcast_to(d0, (BLK, 2 * BLK)), jnp.broadcast_to(d1, (BLK, 2 * BLK))], axis=1)
                    ds = prob * (dp - dl)
                    s0 = jnp.sum(psink[0] * d0, axis=0, keepdims=True)
                    s1 = jnp.sum(psink[1] * d1, axis=0, keepdims=True)
                    dsink_ref[p:p + 1, :] += -jnp.where(lane512, s0, s1)
                    draw = ds * SCALE
                    dz_ref[rows, cols] = _dot(draw, k2).astype(dz_ref.dtype)
                    dkb = dkb + _unband2(_dot_tn(draw, qp), g)
                    dvb = dvb + _unband2(_dot_tn(prob, do), g)
            band = slice(b * BLK, (b + 2) * BLK)
            dkext[band, :] += dkb
            dvext[band, :] += dvb
        dz_ref[:, K0:K0 + KV_W] = dkext[BLK:BLK + tm, :].astype(dz_ref.dtype)
        dz_ref[:, V0:V0 + KV_W] = dvext[BLK:BLK + tm, :].astype(dz_ref.dtype)
        kcar[...] = dkext[0:BLK, :]
        vcar[...] = dvext[0:BLK, :]

        cv, sgc, _ = _conv_branch(z_ref, zh_ref, cw_ref, pv_ref, uext, first, tm)
        xh, rs, ln, sg = _ln_silu(uc_ref[...], pv_ref)
        dln = dy_ref[:, ATTN_W:ATTN_W + CONV_W] * (sg * (1.0 + ln * (1.0 - sg)))
        addrow(R_LN_G, dln * xh)
        addrow(R_LN_B, dln)
        dxh = dln * pv_ref[R_LN_G:R_LN_G + 1, :]
        duc = rs * (dxh - jnp.mean(dxh, axis=-1, keepdims=True) - xh * jnp.mean(dxh * xh, axis=-1, keepdims=True))
        addrow(R_CONV_B, duc)
        ducext[0:tm, :] = duc
        ducext[tm:tm + CONV_HALO, :] = uccar[...]
        uccar[...] = duc[0:CONV_HALO, :]
        du = jnp.zeros((tm, CONV_W), F32)
        for k in range(CONV_K):
            dcw_ref[k:k + 1, :] += jnp.sum(duc * uext[pl.ds(CONV_HALO - (CONV_K - 1) + k, tm), :], axis=0, keepdims=True)
            du = du + cw_ref[k:k + 1, :] * ducext[pl.ds(CONV_K - 1 - k, tm), :]
        dz_ref[:, CV0:CV0 + CONV_W] = (du * sgc).astype(dz_ref.dtype)
        dz_ref[:, CG0:CG0 + CONV_W] = (du * cv * sgc * (1.0 - sgc)).astype(dz_ref.dtype)

        xc, r, ig, sp, la, a, mult = _lru_gates(z_ref, zh_ref, pv_ref, wa_ref, wx_ref, rxext, first, tm)
        h = hl_ref[...]
        rowi = lax.broadcasted_iota(jnp.int32, (tm, LRU_W), 0)
        hlast = jnp.where(first, 0.0, hlh_ref[7:8, :])
        hprev = jnp.where(rowi == 0, hlast, pltpu.roll(h, 1, 0))
        dyl = dy_ref[:, ATTN_W + CONV_W:ATTN_W + CONV_W + LRU_W]
        gl, dgl = _gelu(z_ref[:, RG0:RG0 + LRU_W])
        dz_ref[:, RG0:RG0 + LRU_W] = (dyl * h * dgl).astype(dz_ref.dtype)
        dh = dyl * gl + jnp.where(rowi == tm - 1, gcar[0:1, :], 0.0)
        c = jnp.where(rowi == tm - 1, 0.0, pltpu.roll(a, tm - 1, 0))
        gg = _scan_bwd(c, dh, tm)
        gcar[0:1, :] = a[0:1, :] * gg[0:1, :]
        dmult = gg * (ig * xc)
        dig = gg * mult * xc
        dxc = gg * mult * ig
        dla = gg * hprev * a - dmult * a * a / mult
        dr = dla * (-LRU_C * sp)
        lam = pv_ref[R_LAM:R_LAM + 1, :]
        dpv_ref[R_LAM:R_LAM + 1, :] += jnp.sum(dla * (-LRU_C * r), axis=0, keepdims=True) * (-_sigmoid(-lam))
        dpa = dr * r * (1.0 - r)
        dpx = dig * ig * (1.0 - ig)
        addrow(R_BA, dpa)
        addrow(R_BX, dpx)
        dxc = dxc + _dot_nt(dpa, wa_ref[...]) + _dot_nt(dpx, wx_ref[...])
        dwa_ref[...] += _dot_tn(xc, dpa)
        dwx_ref[...] += _dot_tn(xc, dpx)
        addrow(R_LCONV_B, dxc)
        dxcext[0:tm, :] = dxc
        dxcext[tm:tm + LRU_HALO, :] = xccar[...]
        xccar[...] = dxc[0:LRU_HALO, :]
        drx = jnp.zeros((tm, LRU_W), F32)
        for k in range(LRU_K):
            addrow(R_LCW + k, dxc * rxext[pl.ds(LRU_HALO - (LRU_K - 1) + k, tm), :])
            drx = drx + pv_ref[R_LCW + k:R_LCW + k + 1, :] * dxcext[pl.ds(LRU_K - 1 - k, tm), :]
        dz_ref[:, RX0:RX0 + LRU_W] = drx.astype(dz_ref.dtype)

    tile = lambda w: pl.BlockSpec((tm, w), lambda i: (rev(i), 0))
    in_specs = [tile(D_MODEL)] + _mixer_in_specs(tm, rev) + [
        tile(D_MODEL), tile(LRU_W),
        pl.BlockSpec((8, LRU_W), lambda i: (jnp.maximum(rev(i) * (tm // 8) - 1, 0), 0)),
        tile(CONV_W)]
    return _call(
        body, name, (nt,), in_specs,
        [tile(IN_W), _acc_spec((8, 4 * BLK)), _acc_spec((32, CONV_W)), _acc_spec((16, CONV_W)),
         _acc_spec((LRU_W, LRU_W)), _acc_spec((LRU_W, LRU_W))],
        [_sds((t, IN_W), MX), _sds((8, 4 * BLK), F32), _sds((32, CONV_W), F32), _sds((16, CONV_W), F32),
         _sds((LRU_W, LRU_W), F32), _sds((LRU_W, LRU_W), F32)],
        [pltpu.VMEM((tm + CONV_HALO, CONV_W), F32), pltpu.VMEM((tm + LRU_HALO, LRU_W), F32),
         pltpu.VMEM((tm + BLK, KV_W), F32), pltpu.VMEM((tm + BLK, KV_W), F32),
         pltpu.VMEM((tm + CONV_HALO, CONV_W), F32), pltpu.VMEM((tm + LRU_HALO, LRU_W), F32),
         pltpu.VMEM((BLK, KV_W), F32), pltpu.VMEM((BLK, KV_W), F32),
         pltpu.VMEM((CONV_HALO, CONV_W), F32), pltpu.VMEM((LRU_HALO, LRU_W), F32), pltpu.VMEM((8, LRU_W), F32)],
        [dy, z, z, sink, cw, pv, wa, wx, ycat, hl, hl, uc], riders)


def _post_fwd(ycat, h0, gmix, w_out, g2, w_up, w_down, name, riders=()):
    t = h0.shape[0]
    tm = _tile(t, POST_TILE)
    nj = D_FF // FF_BLK

    def body(y_ref, h_ref, gm_ref, wo_ref, g2_ref, wu_ref, wd_ref, h1_ref, a_ref, h2_ref, ym_ref, hn_ref):
        ym, _, _ = _group_rms_fwd(y_ref[...], gm_ref[...])
        ym = ym.astype(MX)
        ym_ref[...] = ym
        h1 = h_ref[...] + jnp.dot(ym, wo_ref[...], preferred_element_type=F32)
        h1_ref[...] = h1
        hn, _, _ = _rms_fwd(h1, g2_ref[...])
        hn = hn.astype(MX)
        hn_ref[...] = hn
        acc = h1
        for j in range(nj):
            u = jnp.dot(hn, wu_ref[j], preferred_element_type=F32)
            act = jnp.square(jnp.maximum(u, 0.0)).astype(MX)
            a_ref[:, j * FF_BLK:(j + 1) * FF_BLK] = act
            acc = acc + jnp.dot(act, wd_ref[j], preferred_element_type=F32)
        h2_ref[...] = acc

    tile = lambda w: pl.BlockSpec((tm, w), lambda i: (i, 0))
    return _call(
        body, name, (t // tm,),
        [tile(D_MODEL), tile(D_MODEL), _const_spec((1, D_MODEL)), _const_spec((D_MODEL, D_MODEL)),
         _const_spec((1, D_MODEL)), _const_spec((nj, D_MODEL, FF_BLK)), _const_spec((nj, FF_BLK, D_MODEL))],
        [tile(D_MODEL), tile(D_FF), tile(D_MODEL), tile(D_MODEL), tile(D_MODEL)],
        [_sds((t, D_MODEL), F32), _sds((t, D_FF), MX), _sds((t, D_MODEL), F32), _sds((t, D_MODEL), MX),
         _sds((t, D_MODEL), MX)],
        [], [ycat, h0, gmix, w_out, g2, w_up, w_down], riders)


def _post_bwd(dh2, act, h1, ycat, gmix, w_out, g2, w_up, w_down, name, riders=()):
    t = h1.shape[0]
    tm = _tile(t, POST_TILE)
    nj = D_FF // FF_BLK

    def body(dh2_ref, a_ref, h1_ref, y_ref, gm_ref, wo_ref, g2_ref, wu_ref, wd_ref,
             dh1_ref, du_ref, dy_ref, dg2_ref, dgm_ref):
        i = pl.program_id(0)

        @pl.when(i == 0)
        def _():
            dg2_ref[...] = jnp.zeros_like(dg2_ref)
            dgm_ref[...] = jnp.zeros_like(dgm_ref)

        dh2 = dh2_ref[...]
        dh2b = dh2.astype(MX)
        dhn = jnp.zeros((tm, D_MODEL), F32)
        for j in range(nj):
            cols = slice(j * FF_BLK, (j + 1) * FF_BLK)
            da = _dot_nt(dh2b, wd_ref[j])
            du = (da * (2.0 * jnp.sqrt(a_ref[:, cols].astype(F32)))).astype(MX)
            du_ref[:, cols] = du
            dhn = dhn + _dot_nt(du, wu_ref[j])
        _, xh, r = _rms_fwd(h1_ref[...], g2_ref[...])
        dx, dg = _rms_bwd(dhn, xh, r, g2_ref[...])
        dg2_ref[...] += dg
        dh1 = dh2 + dx
        dh1_ref[...] = dh1
        dym = _dot_nt(dh1, wo_ref[...])
        gm = gm_ref[...]
        _, yh, rr = _group_rms_fwd(y_ref[...], gm)
        outs, dgs = [], []
        for (a, b), rg in zip(_GROUPS, rr):
            dxg, dgg = _rms_bwd(dym[:, a:b], yh[:, a:b], rg, gm[:, a:b])
            outs.append(dxg)
            dgs.append(dgg)
        dy_ref[...] = jnp.concatenate(outs, axis=1)
        dgm_ref[...] += jnp.concatenate(dgs, axis=1)

    tile = lambda w: pl.BlockSpec((tm, w), lambda i: (i, 0))
    return _call(
        body, name, (t // tm,),
        [tile(D_MODEL), tile(D_FF), tile(D_MODEL), tile(D_MODEL), _const_spec((1, D_MODEL)),
         _const_spec((D_MODEL, D_MODEL)), _const_spec((1, D_MODEL)),
         _const_spec((nj, D_MODEL, FF_BLK)), _const_spec((nj, FF_BLK, D_MODEL))],
        [tile(D_MODEL), tile(D_FF), tile(D_MODEL), _acc_spec((1, D_MODEL)), _acc_spec((1, D_MODEL))],
        [_sds((t, D_MODEL), F32), _sds((t, D_FF), MX), _sds((t, D_MODEL), F32), _sds((1, D_MODEL), F32),
         _sds((1, D_MODEL), F32)],
        [], [dh2, act, h1, ycat, gmix, w_out, g2, w_up, w_down], riders)


def _in_bwd(dz, h0, dh1, g1, w_in, name):
    t = h0.shape[0]
    tm = _tile(t)

    def body(dz_ref, h_ref, dh1_ref, g_ref, wg_ref, dh0_ref, dg_ref, w_scr):
        @pl.when(pl.program_id(0) == 0)
        def _():
            dg_ref[...] = jnp.zeros_like(dg_ref)
            _assemble_w_in(wg_ref, w_scr)

        dhn = _dot_nt(dz_ref[...], w_scr[...])
        _, xh, r = _rms_fwd(h_ref[...], g_ref[...])
        dx, dg = _rms_bwd(dhn, xh, r, g_ref[...])
        dg_ref[...] += dg
        dh0_ref[...] = dh1_ref[...] + dx

    tile = lambda w: pl.BlockSpec((tm, w), lambda i: (i, 0))
    (dh0, dg), _ = _call(
        body, name, (t // tm,),
        [tile(IN_W), tile(D_MODEL), tile(D_MODEL), _const_spec((1, D_MODEL)), _const_spec((N_DEV, D_MODEL, IN_SHARD))],
        [tile(D_MODEL), _acc_spec((1, D_MODEL))], [_sds((t, D_MODEL), F32), _sds((1, D_MODEL), F32)],
        [pltpu.VMEM((D_MODEL, IN_W), MX)], [dz, h0, dh1, g1, w_in])
    return dh0, dg


def _loss_head(h, gf, target, name):
    t = h.shape[0]
    tm = _tile(t)

    def body(h_ref, g_ref, t_ref, dh_ref, loss_ref, dg_ref):
        @pl.when(pl.program_id(0) == 0)
        def _():
            loss_ref[...] = jnp.zeros_like(loss_ref)
            dg_ref[...] = jnp.zeros_like(dg_ref)

        g = g_ref[...]
        y, xh, r = _rms_fwd(h_ref[...], g)
        err = y - t_ref[...]
        part = 0.5 * jnp.sum(jnp.mean(err * err, axis=-1, keepdims=True), axis=0, keepdims=True)
        loss_ref[...] += jnp.broadcast_to(part, loss_ref.shape)
        dx, dg = _rms_bwd(err * (1.0 / D_MODEL), xh, r, g)
        dg_ref[...] += dg
        dh_ref[...] = dx

    tile = pl.BlockSpec((tm, D_MODEL), lambda i: (i, 0))
    (dh, loss, dg), _ = _call(
        body, name, (t // tm,), [tile, _const_spec((1, D_MODEL)), tile],
        [tile, _acc_spec((1, 128)), _acc_spec((1, D_MODEL))],
        [_sds((t, D_MODEL), F32), _sds((1, 128), F32), _sds((1, D_MODEL), F32)], [], [h, gf, target])
    return dh, loss, dg


def _dw(x, y, name, split, bm, bn):
    t, m = x.shape
    n = y.shape[1]
    tk = _tile(t)
    nk = t // tk
    if split == "rows":
        assert bn == n
        r, c = m // N_DEV, n
        per = bm // r
        out_block = pl.BlockSpec((per, r, c), lambda a, b, k: (a, 0, 0))
    else:
        assert bm == m
        r, c = m, n // N_DEV
        per = bn // c
        out_block = pl.BlockSpec((per, r, c), lambda a, b, k: (b, 0, 0))

    def body(x_ref, y_ref, o_ref, o16_ref, acc):
        k = pl.program_id(2)

        @pl.when(k == 0)
        def _():
            acc[...] = jnp.zeros_like(acc)

        acc[...] += _dot_tn(x_ref[...], y_ref[...])

        @pl.when(k == nk - 1)
        def _():
            for d in range(per):
                v = acc[d * r:(d + 1) * r, :] if split == "rows" else acc[:, d * c:(d + 1) * c]
                o_ref[d] = v
                o16_ref[d] = v.astype(o16_ref.dtype)

    return pl.pallas_call(
        body, name=name, grid=(m // bm, n // bn, nk),
        in_specs=[pl.BlockSpec((tk, bm), lambda a, b, k: (k, a)), pl.BlockSpec((tk, bn), lambda a, b, k: (k, b))],
        out_specs=[out_block, out_block],
        out_shape=[_sds((N_DEV, r, c), F32), _sds((N_DEV, r, c), WIRE)],
        scratch_shapes=[pltpu.VMEM((bm, bn), F32)],
        compiler_params=pltpu.CompilerParams(dimension_semantics=("arbitrary",) * 3, vmem_limit_bytes=VMEM_LIMIT),
    )(x, y)


def _adamw_math(w, g, m, v):
    m = ADAM_B1 * m + (1.0 - ADAM_B1) * g
    v = ADAM_B2 * v + (1.0 - ADAM_B2) * jnp.square(g)
    m_hat = m / (1.0 - ADAM_B1 ** ADAM_STEP)
    v_hat = v / (1.0 - ADAM_B2 ** ADAM_STEP)
    delta = -ADAM_LR * (m_hat / (jnp.sqrt(v_hat) + ADAM_EPS) + ADAM_WD * w)
    return delta, m, v


def _adamw_shard(g_own, g_recv, dev, w, m, v, name):
    r, c = w.shape
    br = r
    for cand in (256, 128, 112, 64, 56, 32, 16, 8):
        if r % cand == 0:
            br = cand
            break

    def body(dev_ref, go_ref, gr_ref, w_ref, m_ref, v_ref, g_out, d_out, m_out, v_out):
        g = go_ref[0]
        for j in range(N_DEV - 1):
            g = g + gr_ref[j].astype(F32)
        delta, mn, vn = _adamw_math(w_ref[...], g, m_ref[...], v_ref[...])
        g_out[...] = g
        d_out[...] = delta
        m_out[...] = mn
        v_out[...] = vn

    tile = pl.BlockSpec((br, c), lambda i, dev_ref: (i, 0))
    return pl.pallas_call(
        body, name=name,
        grid_spec=pltpu.PrefetchScalarGridSpec(
            num_scalar_prefetch=1, grid=(r // br,),
            in_specs=[pl.BlockSpec((1, br, c), lambda i, dev_ref: (dev_ref[0], i, 0)),
                      pl.BlockSpec((N_DEV - 1, br, c), lambda i, dev_ref: (0, i, 0)),
                      tile, tile, tile],
            out_specs=[tile, tile, tile, tile]),
        out_shape=[_sds((r, c), F32)] * 4,
        compiler_params=pltpu.CompilerParams(dimension_semantics=("arbitrary",), vmem_limit_bytes=VMEM_LIMIT),
    )(dev, g_own, g_recv, w, m, v)


def _adamw_small(parts, ws, ms, vs, name):
    n = len(parts)

    def body(*refs):
        p_refs, w_refs, m_refs, v_refs = (refs[k * n:(k + 1) * n] for k in range(4))
        outs = refs[4 * n:]
        for k in range(n):
            g = p_refs[k][0]
            for d in range(1, N_DEV):
                g = g + p_refs[k][d]
            delta, mn, vn = _adamw_math(w_refs[k][...], g, m_refs[k][...], v_refs[k][...])
            outs[k][...] = g
            outs[n + k][...] = delta
            outs[2 * n + k][...] = mn
            outs[3 * n + k][...] = vn

    shapes = [_sds(w.shape, F32) for w in ws]
    res = pl.pallas_call(body, name=name, out_shape=shapes * 4,
                         compiler_params=pltpu.CompilerParams(vmem_limit_bytes=VMEM_LIMIT))(*parts, *ws, *ms, *vs)
    return res[:n], res[n:2 * n], res[2 * n:3 * n], res[3 * n:]


def _sum_parts(part, name):
    def body(p_ref, o_ref):
        g = p_ref[0]
        for d in range(1, N_DEV):
            g = g + p_ref[d]
        o_ref[...] = g

    return pl.pallas_call(body, name=name, out_shape=_sds(part.shape[1:], F32))(part)


def _block_diag(w):
    out = jnp.zeros((LRU_W, LRU_W), w.dtype)
    for h in range(4):
        out = lax.dynamic_update_slice(out, w[h], (h * 64, h * 64))
    return out


def _unblock_diag(w):
    return jnp.concatenate([w[h * 64:(h + 1) * 64, h * 64:(h + 1) * 64] for h in range(4)], axis=0)


def _layer_params(p, l):
    row = lambda a: a[l].reshape(1, -1)
    sink_rows = jnp.repeat(p["attn_sinks"][l].reshape(4, 2), 2 * BLK, axis=1)
    sink_rows = jnp.concatenate([sink_rows, jnp.zeros((4, 4 * BLK), F32)], axis=0)
    cw = jnp.concatenate([p["conv_dw_w"][l], jnp.zeros((1, CONV_W), F32)], axis=0)
    pv = jnp.concatenate([
        row(p["conv_dw_b"]), row(p["conv_ln_g"]), row(p["conv_ln_b"]), row(p["lru_conv_b"]), row(p["lru_ba"]),
        row(p["lru_bx"]), row(p["lru_lambda"]), jnp.zeros((1, LRU_W), F32), p["lru_conv_w"][l],
        jnp.zeros((4, LRU_W), F32)], axis=0)
    return dict(
        g1=row(p["norm1"]), sink=sink_rows, cw=cw, pv=pv,
        wa=_block_diag(p["lru_wa"][l]).astype(MX), wx=_block_diag(p["lru_wx"][l]).astype(MX),
        gmix=row(p["mix_norm"]), g2=row(p["norm2"]))


_SMALL = ["norm1", "attn_sinks", "conv_dw_w", "conv_dw_b", "conv_ln_g", "conv_ln_b", "lru_conv_w", "lru_conv_b",
          "lru_wa", "lru_ba", "lru_wx", "lru_bx", "lru_lambda", "mix_norm", "norm2"]
_BIG = ["w_in", "w_out", "w_up", "w_down"]
_WEIGHTS = ["norm1", "w_in", "attn_sinks", "conv_dw_w", "conv_dw_b", "conv_ln_g", "conv_ln_b", "lru_conv_w",
            "lru_conv_b", "lru_wa", "lru_ba", "lru_wx", "lru_bx", "lru_lambda", "mix_norm", "w_out", "norm2", "w_up",
            "w_down", "final_norm"]


def kernel(x, norm1, w_in, attn_sinks, conv_dw_w, conv_dw_b, conv_ln_g, conv_ln_b, lru_conv_w, lru_conv_b, lru_wa, lru_ba, lru_wx, lru_bx, lru_lambda, mix_norm, w_out, norm2, w_up, w_down, final_norm, loss_target, m_norm1, m_w_in, m_attn_sinks, m_conv_dw_w, m_conv_dw_b, m_conv_ln_g, m_conv_ln_b, m_lru_conv_w, m_lru_conv_b, m_lru_wa, m_lru_ba, m_lru_wx, m_lru_bx, m_lru_lambda, m_mix_norm, m_w_out, m_norm2, m_w_up, m_w_down, m_final_norm, v_norm1, v_w_in, v_attn_sinks, v_conv_dw_w, v_conv_dw_b, v_conv_ln_g, v_conv_ln_b, v_lru_conv_w, v_lru_conv_b, v_lru_wa, v_lru_ba, v_lru_wx, v_lru_bx, v_lru_lambda, v_mix_norm, v_w_out, v_norm2, v_w_up, v_w_down, v_final_norm):
    w = dict(norm1=norm1, w_in=w_in, attn_sinks=attn_sinks, conv_dw_w=conv_dw_w, conv_dw_b=conv_dw_b,
             conv_ln_g=conv_ln_g, conv_ln_b=conv_ln_b, lru_conv_w=lru_conv_w, lru_conv_b=lru_conv_b, lru_wa=lru_wa,
             lru_ba=lru_ba, lru_wx=lru_wx, lru_bx=lru_bx, lru_lambda=lru_lambda, mix_norm=mix_norm, w_out=w_out,
             norm2=norm2, w_up=w_up, w_down=w_down, final_norm=final_norm)
    m = dict(norm1=m_norm1, w_in=m_w_in, attn_sinks=m_attn_sinks, conv_dw_w=m_conv_dw_w, conv_dw_b=m_conv_dw_b,
             conv_ln_g=m_conv_ln_g, conv_ln_b=m_conv_ln_b, lru_conv_w=m_lru_conv_w, lru_conv_b=m_lru_conv_b,
             lru_wa=m_lru_wa, lru_ba=m_lru_ba, lru_wx=m_lru_wx, lru_bx=m_lru_bx, lru_lambda=m_lru_lambda,
             mix_norm=m_mix_norm, w_out=m_w_out, norm2=m_norm2, w_up=m_w_up, w_down=m_w_down, final_norm=m_final_norm)
    v = dict(norm1=v_norm1, w_in=v_w_in, attn_sinks=v_attn_sinks, conv_dw_w=v_conv_dw_w, conv_dw_b=v_conv_dw_b,
             conv_ln_g=v_conv_ln_g, conv_ln_b=v_conv_ln_b, lru_conv_w=v_lru_conv_w, lru_conv_b=v_lru_conv_b,
             lru_wa=v_lru_wa, lru_ba=v_lru_ba, lru_wx=v_lru_wx, lru_bx=v_lru_bx, lru_lambda=v_lru_lambda,
             mix_norm=v_mix_norm, w_out=v_w_out, norm2=v_norm2, w_up=v_w_up, w_down=v_w_down, final_norm=v_final_norm)
    depth = w_in.shape[0]
    xi, yi, ci = _me()
    dev = (4 * xi + 2 * yi + ci).astype(jnp.int32)
    dev1 = dev.reshape(1)
    wb = {n: w[n].astype(MX) for n in _BIG}
    layer_shards = lambda l: [wb["w_out"][l], wb["w_up"][l], wb["w_down"][l]]

    _, ((g_in0, g_cw, g_lcw),) = _call(None, "gather_first", None, [], [], [], [], [],
                                        [_gather_rider([wb["w_in"][0], conv_dw_w, lru_conv_w])])
    cols = lambda g: jnp.moveaxis(g, 0, -2).reshape(g.shape[1:-1] + (N_DEV * g.shape[-1],))
    p = dict(w)
    p["conv_dw_w"] = cols(g_cw)
    p["lru_conv_w"] = cols(g_lcw)
    lp = [_layer_params(p, l) for l in range(depth)]

    gathered = [dict(w_in=g_in0), dict()]
    saved = []
    h = x[0]
    for l in range(depth):
        q, gw = lp[l], gathered[l]
        z, hn1 = _ln_in(h, q["g1"], gw["w_in"], f"ln_in{l}")
        riders = [_gather_rider(layer_shards(0))] if l == 0 else []
        (ycat, hl, uc), got = _mixer_fwd(z, q["sink"], q["cw"], q["pv"], q["wa"], q["wx"], f"mixer_fwd{l}", riders)
        if l == 0:
            gw["w_out"], gw["w_up"], gw["w_down"] = got[0]
            gw["w_out"] = gw["w_out"].reshape(D_MODEL, D_MODEL)
        riders = [_gather_rider([wb["w_in"][1]] + layer_shards(1))] if l == 0 else []
        (h1, act, h2, ym, hn2), got = _post_fwd(ycat, h, q["gmix"], gw["w_out"], q["g2"], gw["w_up"], gw["w_down"],
                                                f"post_fwd{l}", riders)
        if l == 0:
            nxt = gathered[1]
            nxt["w_in"], nxt["w_out"], nxt["w_up"], nxt["w_down"] = got[0]
            nxt["w_out"] = nxt["w_out"].reshape(D_MODEL, D_MODEL)
        saved.append(dict(h0=h, z=z, hn1=hn1, ycat=ycat, hl=hl, uc=uc, h1=h1, act=act, ym=ym, hn2=hn2))
        h = h2
    dh, loss, dgf = _loss_head(h, final_norm.reshape(1, -1), loss_target[0], "loss_head")

    grads = [None] * depth
    big = {n: [None] * depth for n in _BIG}
    pending = []

    def send_pending():
        riders = [_scatter_rider([item[3] for item in pending])] if pending else []
        return riders, list(pending)

    def record(sent, got):
        for item, recv in zip(sent, got[0] if sent else []):
            big[item[0]][item[1]] = (item[2], recv)
        del pending[:len(sent)]

    for l in reversed(range(depth)):
        q, s, gw = lp[l], saved[l], gathered[l]
        riders, sent = send_pending()
        (dh1, du, dycat, dg2, dgm), got = _post_bwd(dh, s["act"], s["h1"], s["ycat"], q["gmix"], gw["w_out"], q["g2"],
                                                    gw["w_up"], gw["w_down"], f"post_bwd{l}", riders)
        record(sent, got)
        pending.append(("w_down", l) + tuple(_dw(s["act"], dh, f"dw_down{l}", "rows", 2048, D_MODEL)))
        pending.append(("w_up", l) + tuple(_dw(s["hn2"], du, f"dw_up{l}", "cols", D_MODEL, 2048)))
        pending.append(("w_out", l) + tuple(_dw(s["ym"], dh1, f"dw_out{l}", "rows", D_MODEL, D_MODEL)))
        riders, sent = send_pending()
        (dz, dsink, dcw, dpv, dwa, dwx), got = _mixer_bwd(dycat, s["z"], s["ycat"], s["hl"], s["uc"], q["sink"],
                                                          q["cw"], q["pv"], q["wa"], q["wx"], f"mixer_bwd{l}", riders)
        record(sent, got)
        pending.append(("w_in", l) + tuple(_dw(s["hn1"], dz, f"dw_in{l}", "cols", D_MODEL, IN_W)))
        dh, dg1 = _in_bwd(dz, s["h0"], dh1, q["g1"], gw["w_in"], f"in_bwd{l}")
        grads[l] = dict(
            norm1=dg1[0], attn_sinks=jnp.stack([dsink[0:4, 0], dsink[0:4, 2 * BLK]], axis=1).reshape(8),
            conv_dw_w=dcw[0:CONV_K], conv_dw_b=dpv[R_CONV_B], conv_ln_g=dpv[R_LN_G], conv_ln_b=dpv[R_LN_B],
            lru_conv_w=dpv[R_LCW:R_LCW + LRU_K], lru_conv_b=dpv[R_LCONV_B], lru_wa=_unblock_diag(dwa),
            lru_ba=dpv[R_BA].reshape(4, 64), lru_wx=_unblock_diag(dwx), lru_bx=dpv[R_BX].reshape(4, 64),
            lru_lambda=dpv[R_LAM], mix_norm=dgm[0], norm2=dg2[0])

    small = [jnp.stack([grads[l][n] for l in range(depth)]) for n in _SMALL] + [dgf, loss]
    riders, sent = send_pending()
    _, got = _call(None, "tail_exchange", None, [], [], [], [], [], riders + [_gather_rider(small)])
    record(sent, got)
    parts_all = got[1]

    out = {}
    for n in _BIG:
        res = [_adamw_shard(big[n][l][0], big[n][l][1], dev1, w[n][l], m[n][l], v[n][l], f"adamw_{n}{l}")
               for l in range(depth)]
        out[n] = [jnp.stack([res[l][j] for l in range(depth)]) for j in range(4)]
    shard = lambda a: lax.dynamic_slice_in_dim(a, dev * (a.shape[-1] // N_DEV), a.shape[-1] // N_DEV, axis=a.ndim - 1)
    flat = {"lru_wa": (depth, LRU_W, 64), "lru_wx": (depth, LRU_W, 64), "final_norm": (1, D_MODEL)}
    parts, ws, ms, vs = [], [], [], []
    for n, g in zip(_SMALL + ["final_norm"], parts_all[:-1]):
        parts.append(shard(g) if n in ("conv_dw_w", "lru_conv_w") else g)
        shp = flat.get(n, w[n].shape)
        ws.append(w[n].reshape(shp))
        ms.append(m[n].reshape(shp))
        vs.append(v[n].reshape(shp))
    sg, sd, sm, sv = _adamw_small(parts, ws, ms, vs, "adamw_small")
    for j, n in enumerate(_SMALL + ["final_norm"]):
        out[n] = [a.reshape(w[n].shape) for a in (sg[j], sd[j], sm[j], sv[j])]
    loss_total = _sum_parts(parts_all[-1], "loss_sum")[0, 0]

    result = [loss_total, dh[None]]
    for j in range(4):
        result += [out[n][j] for n in _WEIGHTS]
    return tuple(result)
```

```python
import types

import jax
import jax.numpy as jnp
from jax import lax
from jax.experimental import pallas as pl
from jax.experimental.pallas import tpu as pltpu

F32 = jnp.float32
MX = jnp.bfloat16
WIRE = jnp.bfloat16

D_MODEL = 1024
HEAD_DIM = 64
ATTN_W = 512
KV_W = 128
BLK = 128
CONV_W = 256
CONV_K = 31
LRU_W = 256
LRU_K = 4
LRU_C = 8.0
IN_W = 1792
D_FF = 4096
FF_BLK = 512
N_DEV = 8
IN_SHARD = IN_W // N_DEV
RMS_EPS = 1e-6
LN_EPS = 1e-5
MASK_VALUE = -1e30
SCALE = HEAD_DIM ** -0.5
CONV_HALO = 32
LRU_HALO = 8
CONV_CHUNK = 64
POST_TILE = 256
Q0, K0, V0, CV0, CG0, RX0, RG0 = 0, 512, 640, 768, 1024, 1280, 1536
R_CONV_B, R_LN_G, R_LN_B, R_LCONV_B, R_BA, R_BX, R_LAM, R_LCW = 0, 1, 2, 3, 4, 5, 6, 8

ADAM_LR, ADAM_B1, ADAM_B2, ADAM_EPS, ADAM_WD, ADAM_STEP = 0.001, 0.9, 0.999, 1e-08, 0.01, 10

VMEM_LIMIT = 56 * 1024 * 1024
MESH = pl.DeviceIdType.MESH
ANY = pl.BlockSpec(memory_space=pl.ANY)


def _tile(t, cap=512):
    return min(cap, t)


def _dot(a, b):
    return jnp.dot(a.astype(MX), b.astype(MX), preferred_element_type=F32)


def _dot_nt(a, b):
    return lax.dot_general(a.astype(MX), b.astype(MX), (((1,), (1,)), ((), ())), preferred_element_type=F32)


def _dot_tn(a, b):
    return lax.dot_general(a.astype(MX), b.astype(MX), (((0,), (0,)), ((), ())), preferred_element_type=F32)


def _const_spec(shape):
    nd = len(shape)
    return pl.BlockSpec(shape, lambda *_: (0,) * nd, pipeline_mode=pl.Buffered(1))


def _acc_spec(shape):
    nd = len(shape)
    return pl.BlockSpec(shape, lambda *_: (0,) * nd)


def _sds(shape, dtype):
    return jax.ShapeDtypeStruct(shape, dtype)


def _sigmoid(x):
    return jax.nn.sigmoid(x)


def _rms_fwd(x, g):
    r = lax.rsqrt(jnp.mean(x * x, axis=-1, keepdims=True) + RMS_EPS)
    xh = x * r
    return xh * g, xh, r


def _rms_bwd(dy, xh, r, g):
    t = dy * g
    dx = r * (t - xh * jnp.mean(t * xh, axis=-1, keepdims=True))
    return dx, jnp.sum(dy * xh, axis=0, keepdims=True)


_GROUPS = ((0, 512), (512, 768), (768, 1024))


def _group_rms_fwd(y, g):
    parts = [_rms_fwd(y[:, a:b], g[:, a:b]) for a, b in _GROUPS]
    return (jnp.concatenate([p[0] for p in parts], axis=1),
            jnp.concatenate([p[1] for p in parts], axis=1),
            [p[2] for p in parts])


def _gelu(x):
    c = 0.7978845608028654
    u = c * (x + 0.044715 * x * x * x)
    th = jnp.tanh(u)
    val = 0.5 * x * (1.0 + th)
    grad = 0.5 * (1.0 + th) + 0.5 * x * (1.0 - th * th) * c * (1.0 + 3.0 * 0.044715 * x * x)
    return val, grad


def _neg_expm1(x):
    series = -x * (1.0 + x * (0.5 + x * (1.0 / 6.0 + x * (1.0 / 24.0))))
    return jnp.where(x > -0.02, series, 1.0 - jnp.exp(x))


def _me():
    return lax.axis_index("x"), lax.axis_index("y"), lax.axis_index("c")


def _gather_rider(arrays):
    arrays = list(arrays)
    n = len(arrays)

    def plan(ins, outs, sems):
        ssem, rsem, lsem = sems
        x, y, c = _me()
        chips = [(1 - x, y), (x, 1 - y), (1 - x, 1 - y)]

        def copy(a, k, block, to, own=False):
            dst = outs[a].at[4 * block[0] + 2 * block[1] + block[2]]
            return pltpu.make_async_remote_copy(
                src_ref=ins[a] if own else dst, dst_ref=dst, send_sem=ssem.at[7 * a + k],
                recv_sem=rsem.at[7 * a + k], device_id=to, device_id_type=MESH)

        return x, y, c, chips, copy, lsem

    def start(ins, outs, sems):
        x, y, c, chips, copy, lsem = plan(ins, outs, sems)
        for a in range(n):
            pltpu.make_async_copy(ins[a], outs[a].at[4 * x + 2 * y + c], lsem.at[a]).start()
            copy(a, 0, (x, y, c), (x, y, 1 - c), own=True).start()
            for j, chip in enumerate(chips):
                copy(a, 1 + j, (x, y, c), (*chip, c), own=True).start()

    def mid(ins, outs, sems):
        x, y, c, chips, copy, _ = plan(ins, outs, sems)
        for a in range(n):
            for j, chip in enumerate(chips):
                copy(a, 1 + j, (*chip, c), (x, y, c)).wait_recv()
                copy(a, 4 + j, (*chip, c), (x, y, 1 - c)).start()

    def finish(ins, outs, sems):
        x, y, c, chips, copy, lsem = plan(ins, outs, sems)
        for a in range(n):
            copy(a, 0, (x, y, 1 - c), (x, y, c)).wait_recv()
            for j, chip in enumerate(chips):
                copy(a, 4 + j, (*chip, 1 - c), (x, y, c)).wait_recv()
        for a in range(n):
            copy(a, 0, (x, y, c), (x, y, 1 - c), own=True).wait_send()
            for j, chip in enumerate(chips):
                copy(a, 1 + j, (x, y, c), (*chip, c), own=True).wait_send()
                copy(a, 4 + j, (*chip, c), (x, y, 1 - c)).wait_send()
            pltpu.make_async_copy(ins[a], outs[a].at[4 * x + 2 * y + c], lsem.at[a]).wait()

    return types.SimpleNamespace(
        arrays=arrays, out_shape=[_sds((N_DEV,) + a.shape, a.dtype) for a in arrays],
        scratch=[pltpu.SemaphoreType.DMA((7 * n,)), pltpu.SemaphoreType.DMA((7 * n,)), pltpu.SemaphoreType.DMA((n,))],
        start=start, mid=mid, finish=finish)


def _scatter_rider(arrays):
    arrays = list(arrays)
    n = len(arrays)

    def copies(ins, outs, sems):
        ssem, rsem = sems
        x, y, c = _me()
        out = []
        for a in range(n):
            for f in range(1, N_DEV):
                px = 1 - x if f & 4 else x
                py = 1 - y if f & 2 else y
                pc = 1 - c if f & 1 else c
                out.append(pltpu.make_async_remote_copy(
                    src_ref=ins[a].at[4 * px + 2 * py + pc], dst_ref=outs[a].at[f - 1], send_sem=ssem.at[7 * a + f - 1],
                    recv_sem=rsem.at[7 * a + f - 1], device_id=(px, py, pc), device_id_type=MESH))
        return out

    def start(ins, outs, sems):
        for cp in copies(ins, outs, sems):
            cp.start()

    def finish(ins, outs, sems):
        for cp in copies(ins, outs, sems):
            cp.wait()

    return types.SimpleNamespace(
        arrays=arrays, out_shape=[_sds((N_DEV - 1,) + a.shape[1:], a.dtype) for a in arrays],
        scratch=[pltpu.SemaphoreType.DMA((7 * n,)), pltpu.SemaphoreType.DMA((7 * n,))],
        start=start, mid=None, finish=finish)


def _call(body, name, grid, in_specs, out_specs, out_shape, scratch, operands, riders=()):
    n_in, n_out, n_scr = len(operands), len(out_shape), len(scratch)
    nsteps = grid[0] if grid else 1
    sizes = [(len(r.arrays), len(r.out_shape), len(r.scratch)) for r in riders]

    def wrapped(*refs):
        pos = n_in
        r_ins = []
        for ri, _, _ in sizes:
            r_ins.append(refs[pos:pos + ri])
            pos += ri
        outs = refs[pos:pos + n_out]
        pos += n_out
        r_outs = []
        for _, ro, _ in sizes:
            r_outs.append(refs[pos:pos + ro])
            pos += ro
        scr = refs[pos:pos + n_scr]
        pos += n_scr
        r_sems = []
        for _, _, rs in sizes:
            r_sems.append(refs[pos:pos + rs])
            pos += rs
        step = pl.program_id(0) if grid else 0

        def at(s, fn):
            if grid:
                pl.when(step == s)(fn)
            else:
                fn()

        for r, a, b, c in zip(riders, r_ins, r_outs, r_sems):
            at(0, lambda r=r, a=a, b=b, c=c: r.start(a, b, c))
        for r, a, b, c in zip(riders, r_ins, r_outs, r_sems):
            if r.mid is not None:
                at((3 * nsteps) // 4, lambda r=r, a=a, b=b, c=c: r.mid(a, b, c))
        if body is not None:
            body(*refs[:n_in], *outs, *scr)
        for r, a, b, c in zip(riders, r_ins, r_outs, r_sems):
            at(nsteps - 1, lambda r=r, a=a, b=b, c=c: r.finish(a, b, c))

    r_arrays = [a for r in riders for a in r.arrays]
    r_shapes = [s for r in riders for s in r.out_shape]
    kwargs = {}
    if grid:
        kwargs = dict(grid=grid, compiler_params=pltpu.CompilerParams(
            dimension_semantics=("arbitrary",) * len(grid), vmem_limit_bytes=VMEM_LIMIT))
    res = pl.pallas_call(
        wrapped, name=name,
        in_specs=list(in_specs) + [ANY] * len(r_arrays),
        out_specs=list(out_specs) + [ANY] * len(r_shapes),
        out_shape=list(out_shape) + r_shapes,
        scratch_shapes=list(scratch) + [s for r in riders for s in r.scratch],
        **kwargs,
    )(*operands, *r_arrays)
    host, rest = res[:n_out], res[n_out:]
    r_res = []
    for _, ro, _ in sizes:
        r_res.append(rest[:ro])
        rest = rest[ro:]
    return host, r_res


def _assemble_w_in(wg_ref, w_scr):
    for j in range(N_DEV):
        w_scr[:, j * IN_SHARD:(j + 1) * IN_SHARD] = wg_ref[j]


def _ln_in(h, g1, w_in, name):
    t = h.shape[0]
    tm = _tile(t)

    def body(h_ref, g_ref, wg_ref, z_ref, hn_ref, w_scr):
        @pl.when(pl.program_id(0) == 0)
        def _():
            _assemble_w_in(wg_ref, w_scr)

        y, _, _ = _rms_fwd(h_ref[...], g_ref[...])
        hn = y.astype(MX)
        hn_ref[...] = hn
        z_ref[...] = jnp.dot(hn, w_scr[...], preferred_element_type=F32)

    tile = lambda w: pl.BlockSpec((tm, w), lambda i: (i, 0))
    (z, hn), _ = _call(
        body, name, (t // tm,),
        [tile(D_MODEL), _const_spec((1, D_MODEL)), _const_spec((N_DEV, D_MODEL, IN_SHARD))],
        [tile(IN_W), tile(D_MODEL)], [_sds((t, IN_W), F32), _sds((t, D_MODEL), MX)],
        [pltpu.VMEM((D_MODEL, IN_W), MX)], [h, g1, w_in])
    return z, hn


def _band2(kb, g):
    lo = lax.broadcasted_iota(jnp.int32, kb.shape, 1) < HEAD_DIM
    kr = pltpu.roll(kb, HEAD_DIM, 1)
    if g == 0:
        top, bot = jnp.where(lo, kb, 0.0), jnp.where(lo, 0.0, kr)
    else:
        top, bot = jnp.where(lo, kr, 0.0), jnp.where(lo, 0.0, kb)
    return jnp.concatenate([top, bot], axis=0)


def _attn_block(z_ref, zh_ref, sink_ref, b, first):
    rows = slice(b * BLK, (b + 1) * BLK)
    prev = zh_ref if b == 0 else z_ref
    prow = slice(0, BLK) if b == 0 else slice((b - 1) * BLK, b * BLK)
    kb = jnp.concatenate([prev[prow, K0:K0 + KV_W], z_ref[rows, K0:K0 + KV_W]], axis=0)
    vb = jnp.concatenate([prev[prow, V0:V0 + KV_W], z_ref[rows, V0:V0 + KV_W]], axis=0)
    k2 = [_band2(kb, g) for g in range(2)]
    v2 = [_band2(vb, g) for g in range(2)]
    q2 = [jnp.concatenate([z_ref[rows, (2 * g) * BLK:(2 * g + 1) * BLK], z_ref[rows, (2 * g + 1) * BLK:(2 * g + 2) * BLK]],
                          axis=0) for g in range(2)]
    rr = lax.broadcasted_iota(jnp.int32, (4 * BLK, 2 * BLK), 0) & (BLK - 1)
    cc = lax.broadcasted_iota(jnp.int32, (4 * BLK, 2 * BLK), 1)
    first_block = jnp.logical_and(first, b == 0).astype(jnp.int32)
    mask = jnp.logical_and(jnp.logical_and(cc > rr, cc <= rr + BLK), cc >= BLK * first_block)
    s = jnp.concatenate([_dot_nt(q2[g], k2[g]) for g in range(2)], axis=0) * SCALE
    w = 2 * BLK
    out, psink = [], []
    for hh in range(2):
        sh = jnp.where(mask, s[:, hh * w:(hh + 1) * w], MASK_VALUE)
        sk = jnp.concatenate([jnp.broadcast_to(sink_ref[p:p + 1, hh * w:hh * w + 1], (BLK, 1)) for p in range(4)], axis=0)
        m = jnp.maximum(jnp.max(sh, axis=1, keepdims=True), sk)
        p = jnp.exp(sh - m)
        es = jnp.exp(sk - m)
        inv = 1.0 / (jnp.sum(p, axis=1, keepdims=True) + es)
        out.append(p * inv)
        psink.append(es * inv)
    return q2, k2, v2, jnp.concatenate(out, axis=1), psink


def _scan_fwd(a, b, tm):
    rows = lax.broadcasted_iota(jnp.int32, a.shape, 0)
    d = 1
    while d < tm:
        keep = rows >= d
        a_sh = jnp.where(keep, pltpu.roll(a, d, 0), 1.0)
        b_sh = jnp.where(keep, pltpu.roll(b, d, 0), 0.0)
        b = a * b_sh + b
        a = a * a_sh
        d *= 2
    return a, b


def _scan_bwd(c, b, tm):
    rows = lax.broadcasted_iota(jnp.int32, c.shape, 0)
    d = 1
    while d < tm:
        keep = rows < tm - d
        c_sh = jnp.where(keep, pltpu.roll(c, tm - d, 0), 1.0)
        b_sh = jnp.where(keep, pltpu.roll(b, tm - d, 0), 0.0)
        b = c * b_sh + b
        c = c * c_sh
        d *= 2
    return b


def _shifted_copies(ext, shifts, tm):
    rows = tm + CONV_HALO - 8
    for r in range(1, 8):
        shifts[r - 1, 0:rows, :] = ext[pl.ds(r, rows), :]


def _tap(ext, shifts, off, r0, n):
    a, r = divmod(off, 8)
    lo = 8 * a + r0
    if r == 0:
        return ext[lo:lo + n, :]
    return shifts[r - 1, lo:lo + n, :]


def _glu_fill(z_ref, zh_ref, uext, ush, first, tm, sg_out=None):
    cv = z_ref[:, CV0:CV0 + CONV_W]
    sg = _sigmoid(z_ref[:, CG0:CG0 + CONV_W])
    if sg_out is not None:
        sg_out[...] = sg
    hrow = BLK - CONV_HALO
    uh = zh_ref[hrow:BLK, CV0:CV0 + CONV_W] * _sigmoid(zh_ref[hrow:BLK, CG0:CG0 + CONV_W])
    uext[0:CONV_HALO, :] = jnp.where(first, 0.0, uh)
    uext[CONV_HALO:CONV_HALO + tm, :] = cv * sg
    _shifted_copies(uext, ush, tm)


def _conv_taps(cw_ref, pv_ref, uext, ush, out_ref, tm):
    for r0 in range(0, tm, CONV_CHUNK):
        acc = jnp.broadcast_to(pv_ref[R_CONV_B:R_CONV_B + 1, :], (CONV_CHUNK, CONV_W))
        for k in range(CONV_K):
            acc = acc + cw_ref[k:k + 1, :] * _tap(uext, ush, CONV_HALO - (CONV_K - 1) + k, r0, CONV_CHUNK)
        out_ref[r0:r0 + CONV_CHUNK, :] = acc


def _ln_silu(uc, pv_ref):
    mu = jnp.mean(uc, axis=-1, keepdims=True)
    xc = uc - mu
    rs = lax.rsqrt(jnp.mean(xc * xc, axis=-1, keepdims=True) + LN_EPS)
    xh = xc * rs
    ln = xh * pv_ref[R_LN_G:R_LN_G + 1, :] + pv_ref[R_LN_B:R_LN_B + 1, :]
    sg = _sigmoid(ln)
    return xh, rs, ln, sg


def _lru_gates(z_ref, zh_ref, pv_ref, wa_ref, wx_ref, rxext, first, tm):
    rxext[0:LRU_HALO, :] = jnp.where(first, 0.0, zh_ref[BLK - LRU_HALO:BLK, RX0:RX0 + LRU_W])
    rxext[LRU_HALO:LRU_HALO + tm, :] = z_ref[:, RX0:RX0 + LRU_W]
    xc = jnp.broadcast_to(pv_ref[R_LCONV_B:R_LCONV_B + 1, :], (tm, LRU_W))
    for k in range(LRU_K):
        xc = xc + pv_ref[R_LCW + k:R_LCW + k + 1, :] * rxext[pl.ds(LRU_HALO - (LRU_K - 1) + k, tm), :]
    r = _sigmoid(_dot(xc, wa_ref[...]) + pv_ref[R_BA:R_BA + 1, :])
    ig = _sigmoid(_dot(xc, wx_ref[...]) + pv_ref[R_BX:R_BX + 1, :])
    lam = pv_ref[R_LAM:R_LAM + 1, :]
    sp = jnp.log1p(jnp.exp(-lam))
    la = (-LRU_C * r) * sp
    a = jnp.exp(la)
    mult = jnp.sqrt(_neg_expm1(2.0 * la))
    return xc, r, ig, sp, la, a, mult


def _mixer_in_specs(tm, tile_of):
    hb = tm // BLK
    return [
        pl.BlockSpec((tm, IN_W), lambda i: (tile_of(i), 0)),
        pl.BlockSpec((BLK, IN_W), lambda i: (jnp.maximum(tile_of(i) * hb - 1, 0), 0)),
        _const_spec((8, 4 * BLK)),
        _const_spec((32, CONV_W)),
        _const_spec((16, CONV_W)),
        _const_spec((LRU_W, LRU_W)),
        _const_spec((LRU_W, LRU_W)),
    ]


def _mixer_fwd(z, sink, cw, pv, wa, wx, name, riders=()):
    t = z.shape[0]
    tm = _tile(t)
    nb = tm // BLK

    def body(z_ref, zh_ref, sink_ref, cw_ref, pv_ref, wa_ref, wx_ref, y_ref, hl_ref, uc_ref, uext, ush, rxext, hcar):
        i = pl.program_id(0)
        first = i == 0

        @pl.when(first)
        def _():
            hcar[...] = jnp.zeros_like(hcar)

        for b in range(nb):
            rows = slice(b * BLK, (b + 1) * BLK)
            _, _, v2, prob, _ = _attn_block(z_ref, zh_ref, sink_ref, b, first)
            for g in range(2):
                o = _dot(prob[2 * g * BLK:(2 * g + 2) * BLK], v2[g])
                y_ref[rows, (2 * g) * BLK:(2 * g + 1) * BLK] = o[0:BLK]
                y_ref[rows, (2 * g + 1) * BLK:(2 * g + 2) * BLK] = o[BLK:2 * BLK]
        _glu_fill(z_ref, zh_ref, uext, ush, first, tm)
        _conv_taps(cw_ref, pv_ref, uext, ush, uc_ref, tm)
        _, _, ln, sg = _ln_silu(uc_ref[...], pv_ref)
        y_ref[:, ATTN_W:ATTN_W + CONV_W] = ln * sg
        xc, _, ig, _, _, a, mult = _lru_gates(z_ref, zh_ref, pv_ref, wa_ref, wx_ref, rxext, first, tm)
        acum, h = _scan_fwd(a, mult * (ig * xc), tm)
        h = h + acum * hcar[0:1, :]
        hl_ref[...] = h
        hcar[0:1, :] = h[tm - 1:tm, :]
        gl, _ = _gelu(z_ref[:, RG0:RG0 + LRU_W])
        y_ref[:, ATTN_W + CONV_W:ATTN_W + CONV_W + LRU_W] = h * gl

    tile = lambda w: pl.BlockSpec((tm, w), lambda i: (i, 0))
    return _call(
        body, name, (t // tm,), _mixer_in_specs(tm, lambda i: i),
        [tile(D_MODEL), tile(LRU_W), tile(CONV_W)],
        [_sds((t, D_MODEL), F32), _sds((t, LRU_W), F32), _sds((t, CONV_W), F32)],
        [pltpu.VMEM((tm + CONV_HALO, CONV_W), F32), pltpu.VMEM((7, tm + CONV_HALO - 8, CONV_W), F32),
         pltpu.VMEM((tm + LRU_HALO, LRU_W), F32), pltpu.VMEM((8, LRU_W), F32)],
        [z, z, sink, cw, pv, wa, wx], riders)


def _mixer_bwd(dy, z, ycat, hl, uc, sink, cw, pv, wa, wx, name, riders=()):
    t = z.shape[0]
    tm = _tile(t)
    nt = t // tm
    nb = tm // BLK
    rev = lambda i: nt - 1 - i

    def body(dy_ref, z_ref, zh_ref, sink_ref, cw_ref, pv_ref, wa_ref, wx_ref, y_ref, hl_ref, hlh_ref, uc_ref,
             dz_ref, dsink_ref, dcw_ref, dpv_ref, dwa_ref, dwx_ref,
             uext, ush, sgs, rxext, dkext, dvext, ducext, dsh, dcw8, dxcext, kcar, vcar, uccar, xccar, gcar):
        i = pl.program_id(0)
        first = i == nt - 1

        @pl.when(i == 0)
        def _():
            for car in (kcar, vcar, uccar, xccar, gcar, dcw8):
                car[...] = jnp.zeros_like(car)
            dsink_ref[...] = jnp.zeros_like(dsink_ref)
            dpv_ref[...] = jnp.zeros_like(dpv_ref)
            dwa_ref[...] = jnp.zeros_like(dwa_ref)
            dwx_ref[...] = jnp.zeros_like(dwx_ref)

        def addrow(r, val):
            dpv_ref[r:r + 1, :] += jnp.sum(val, axis=0, keepdims=True)

        dkext[:, 0:tm] = jnp.zeros((KV_W, tm), F32)
        dvext[:, 0:tm] = jnp.zeros((KV_W, tm), F32)
        dkext[:, tm:tm + BLK] = kcar[...]
        dvext[:, tm:tm + BLK] = vcar[...]
        lane512 = lax.broadcasted_iota(jnp.int32, (1, 4 * BLK), 1) < 2 * BLK
        lo = lax.broadcasted_iota(jnp.int32, (4 * BLK, BLK), 1) < HEAD_DIM
        hd, w2 = HEAD_DIM, 2 * BLK
        for b in range(nb):
            rows = slice(b * BLK, (b + 1) * BLK)
            band = slice(b * BLK, (b + 2) * BLK)
            q2, k2, v2, prob, psink = _attn_block(z_ref, zh_ref, sink_ref, b, first)
            stack = lambda ref: jnp.concatenate([ref[rows, p * BLK:(p + 1) * BLK] for p in range(4)], axis=0)
            do4 = stack(dy_ref)
            dlt = do4 * stack(y_ref)
            d0 = jnp.sum(jnp.where(lo, dlt, 0.0), axis=1, keepdims=True)
            d1 = jnp.sum(jnp.where(lo, 0.0, dlt), axis=1, keepdims=True)
            dp = jnp.concatenate([_dot_nt(do4[g * w2:(g + 1) * w2], v2[g]) for g in range(2)], axis=0)
            dl = jnp.concatenate([jnp.broadcast_to(d0, (4 * BLK, w2)), jnp.broadcast_to(d1, (4 * BLK, w2))], axis=1)
            draw = (prob * (dp - dl)) * SCALE
            e0, e1 = psink[0] * d0, psink[1] * d1
            for p in range(4):
                prs = slice(p * BLK, (p + 1) * BLK)
                s0 = jnp.sum(e0[prs], axis=0, keepdims=True)
                s1 = jnp.sum(e1[prs], axis=0, keepdims=True)
                dsink_ref[p:p + 1, :] += -jnp.where(lane512, s0, s1)
            for g in range(2):
                grs = slice(g * w2, (g + 1) * w2)
                dq = _dot(draw[grs], k2[g])
                dz_ref[rows, (2 * g) * BLK:(2 * g + 1) * BLK] = dq[0:BLK].astype(dz_ref.dtype)
                dz_ref[rows, (2 * g + 1) * BLK:(2 * g + 2) * BLK] = dq[BLK:2 * BLK].astype(dz_ref.dtype)
                tk = _dot_tn(q2[g], draw[grs])
                tv = _dot_tn(do4[grs], prob[grs])
                dkext[g * hd:(g + 1) * hd, band] += tk[0:hd, 0:w2] + tk[hd:2 * hd, w2:2 * w2]
                dvext[g * hd:(g + 1) * hd, band] += tv[0:hd, 0:w2] + tv[hd:2 * hd, w2:2 * w2]
        dz_ref[:, K0:K0 + KV_W] = jnp.transpose(dkext[:, BLK:BLK + tm]).astype(dz_ref.dtype)
        dz_ref[:, V0:V0 + KV_W] = jnp.transpose(dvext[:, BLK:BLK + tm]).astype(dz_ref.dtype)
        kcar[...] = dkext[:, 0:BLK]
        vcar[...] = dvext[:, 0:BLK]

        _glu_fill(z_ref, zh_ref, uext, ush, first, tm, sg_out=sgs)
        xh, rs, ln, sg = _ln_silu(uc_ref[...], pv_ref)
        dln = dy_ref[:, ATTN_W:ATTN_W + CONV_W] * (sg * (1.0 + ln * (1.0 - sg)))
        addrow(R_LN_G, dln * xh)
        addrow(R_LN_B, dln)
        dxh = dln * pv_ref[R_LN_G:R_LN_G + 1, :]
        duc = rs * (dxh - jnp.mean(dxh, axis=-1, keepdims=True) - xh * jnp.mean(dxh * xh, axis=-1, keepdims=True))
        addrow(R_CONV_B, duc)
        ducext[0:tm, :] = duc
        ducext[tm:tm + CONV_HALO, :] = uccar[...]
        uccar[...] = duc[0:CONV_HALO, :]
        _shifted_copies(ducext, dsh, tm)
        for r0 in range(0, tm, CONV_CHUNK):
            crow = slice(r0, r0 + CONV_CHUNK)
            duc_c = ducext[crow, :]
            du = jnp.zeros((CONV_CHUNK, CONV_W), F32)
            for k in range(CONV_K):
                prod = duc_c * _tap(uext, ush, CONV_HALO - (CONV_K - 1) + k, r0, CONV_CHUNK)
                part = prod[0:8]
                for s in range(8, CONV_CHUNK, 8):
                    part = part + prod[s:s + 8]
                dcw8[k] += part
                du = du + cw_ref[k:k + 1, :] * _tap(ducext, dsh, CONV_K - 1 - k, r0, CONV_CHUNK)
            sgc = sgs[crow, :]
            dz_ref[crow, CV0:CV0 + CONV_W] = (du * sgc).astype(dz_ref.dtype)
            u_c = uext[CONV_HALO + r0:CONV_HALO + r0 + CONV_CHUNK, :]
            dz_ref[crow, CG0:CG0 + CONV_W] = (du * u_c * (1.0 - sgc)).astype(dz_ref.dtype)

        @pl.when(i == nt - 1)
        def _():
            dcw_ref[...] = jnp.sum(dcw8[...], axis=1)

        xc, r, ig, sp, la, a, mult = _lru_gates(z_ref, zh_ref, pv_ref, wa_ref, wx_ref, rxext, first, tm)
        h = hl_ref[...]
        rowi = lax.broadcasted_iota(jnp.int32, (tm, LRU_W), 0)
        hlast = jnp.where(first, 0.0, hlh_ref[7:8, :])
        hprev = jnp.where(rowi == 0, hlast, pltpu.roll(h, 1, 0))
        dyl = dy_ref[:, ATTN_W + CONV_W:ATTN_W + CONV_W + LRU_W]
        gl, dgl = _gelu(z_ref[:, RG0:RG0 + LRU_W])
        dz_ref[:, RG0:RG0 + LRU_W] = (dyl * h * dgl).astype(dz_ref.dtype)
        dh = dyl * gl + jnp.where(rowi == tm - 1, gcar[0:1, :], 0.0)
        c = jnp.where(rowi == tm - 1, 0.0, pltpu.roll(a, tm - 1, 0))
        gg = _scan_bwd(c, dh, tm)
        gcar[0:1, :] = a[0:1, :] * gg[0:1, :]
        dmult = gg * (ig * xc)
        dig = gg * mult * xc
        dxc = gg * mult * ig
        dla = gg * hprev * a - dmult * a * a / mult
        dr = dla * (-LRU_C * sp)
        lam = pv_ref[R_LAM:R_LAM + 1, :]
        dpv_ref[R_LAM:R_LAM + 1, :] += jnp.sum(dla * (-LRU_C * r), axis=0, keepdims=True) * (-_sigmoid(-lam))
        dpa = dr * r * (1.0 - r)
        dpx = dig * ig * (1.0 - ig)
        addrow(R_BA, dpa)
        addrow(R_BX, dpx)
        dxc = dxc + _dot_nt(dpa, wa_ref[...]) + _dot_nt(dpx, wx_ref[...])
        dwa_ref[...] += _dot_tn(xc, dpa)
        dwx_ref[...] += _dot_tn(xc, dpx)
        addrow(R_LCONV_B, dxc)
        dxcext[0:tm, :] = dxc
        dxcext[tm:tm + LRU_HALO, :] = xccar[...]
        xccar[...] = dxc[0:LRU_HALO, :]
        drx = jnp.zeros((tm, LRU_W), F32)
        for k in range(LRU_K):
            addrow(R_LCW + k, dxc * rxext[pl.ds(LRU_HALO - (LRU_K - 1) + k, tm), :])
            drx = drx + pv_ref[R_LCW + k:R_LCW + k + 1, :] * dxcext[pl.ds(LRU_K - 1 - k, tm), :]
        dz_ref[:, RX0:RX0 + LRU_W] = drx.astype(dz_ref.dtype)

    tile = lambda w: pl.BlockSpec((tm, w), lambda i: (rev(i), 0))
    in_specs = [tile(D_MODEL)] + _mixer_in_specs(tm, rev) + [
        tile(D_MODEL), tile(LRU_W),
        pl.BlockSpec((8, LRU_W), lambda i: (jnp.maximum(rev(i) * (tm // 8) - 1, 0), 0)),
        tile(CONV_W)]
    return _call(
        body, name, (nt,), in_specs,
        [tile(IN_W), _acc_spec((8, 4 * BLK)), _acc_spec((32, CONV_W)), _acc_spec((16, CONV_W)),
         _acc_spec((LRU_W, LRU_W)), _acc_spec((LRU_W, LRU_W))],
        [_sds((t, IN_W), MX), _sds((8, 4 * BLK), F32), _sds((32, CONV_W), F32), _sds((16, CONV_W), F32),
         _sds((LRU_W, LRU_W), F32), _sds((LRU_W, LRU_W), F32)],
        [pltpu.VMEM((tm + CONV_HALO, CONV_W), F32), pltpu.VMEM((7, tm + CONV_HALO - 8, CONV_W), F32),
         pltpu.VMEM((tm, CONV_W), F32), pltpu.VMEM((tm + LRU_HALO, LRU_W), F32),
         pltpu.VMEM((KV_W, tm + BLK), F32), pltpu.VMEM((KV_W, tm + BLK), F32),
         pltpu.VMEM((tm + CONV_HALO, CONV_W), F32), pltpu.VMEM((7, tm + CONV_HALO - 8, CONV_W), F32),
         pltpu.VMEM((32, 8, CONV_W), F32), pltpu.VMEM((tm + LRU_HALO, LRU_W), F32),
         pltpu.VMEM((KV_W, BLK), F32), pltpu.VMEM((KV_W, BLK), F32),
         pltpu.VMEM((CONV_HALO, CONV_W), F32), pltpu.VMEM((LRU_HALO, LRU_W), F32), pltpu.VMEM((8, LRU_W), F32)],
        [dy, z, z, sink, cw, pv, wa, wx, ycat, hl, hl, uc], riders)


def _post_fwd(ycat, h0, gmix, w_out, g2, w_up, w_down, name, riders=()):
    t = h0.shape[0]
    tm = _tile(t, POST_TILE)
    nj = D_FF // FF_BLK

    def body(y_ref, h_ref, gm_ref, wo_ref, g2_ref, wu_ref, wd_ref, h1_ref, a_ref, h2_ref, ym_ref, hn_ref):
        ym, _, _ = _group_rms_fwd(y_ref[...], gm_ref[...])
        ym = ym.astype(MX)
        ym_ref[...] = ym
        h1 = h_ref[...] + jnp.dot(ym, wo_ref[...], preferred_element_type=F32)
        h1_ref[...] = h1
        hn, _, _ = _rms_fwd(h1, g2_ref[...])
        hn = hn.astype(MX)
        hn_ref[...] = hn
        acc = h1
        for j in range(nj):
            u = jnp.dot(hn, wu_ref[j], preferred_element_type=F32)
            act = jnp.square(jnp.maximum(u, 0.0)).astype(MX)
            a_ref[:, j * FF_BLK:(j + 1) * FF_BLK] = act
            acc = acc + jnp.dot(act, wd_ref[j], preferred_element_type=F32)
        h2_ref[...] = acc

    tile = lambda w: pl.BlockSpec((tm, w), lambda i: (i, 0))
    return _call(
        body, name, (t // tm,),
        [tile(D_MODEL), tile(D_MODEL), _const_spec((1, D_MODEL)), _const_spec((D_MODEL, D_MODEL)),
         _const_spec((1, D_MODEL)), _const_spec((nj, D_MODEL, FF_BLK)), _const_spec((nj, FF_BLK, D_MODEL))],
        [tile(D_MODEL), tile(D_FF), tile(D_MODEL), tile(D_MODEL), tile(D_MODEL)],
        [_sds((t, D_MODEL), F32), _sds((t, D_FF), MX), _sds((t, D_MODEL), F32), _sds((t, D_MODEL), MX),
         _sds((t, D_MODEL), MX)],
        [], [ycat, h0, gmix, w_out, g2, w_up, w_down], riders)


def _post_bwd(dh2, act, h1, ycat, gmix, w_out, g2, w_up, w_down, name, riders=()):
    t = h1.shape[0]
    tm = _tile(t, POST_TILE)
    nj = D_FF // FF_BLK

    def body(dh2_ref, a_ref, h1_ref, y_ref, gm_ref, wo_ref, g2_ref, wu_ref, wd_ref,
             dh1_ref, du_ref, dy_ref, dg2_ref, dgm_ref):
        i = pl.program_id(0)

        @pl.when(i == 0)
        def _():
            dg2_ref[...] = jnp.zeros_like(dg2_ref)
            dgm_ref[...] = jnp.zeros_like(dgm_ref)

        dh2 = dh2_ref[...]
        dh2b = dh2.astype(MX)
        dhn = jnp.zeros((tm, D_MODEL), F32)
        for j in range(nj):
            cols = slice(j * FF_BLK, (j + 1) * FF_BLK)
            da = _dot_nt(dh2b, wd_ref[j])
            du = (da * (2.0 * jnp.sqrt(a_ref[:, cols].astype(F32)))).astype(MX)
            du_ref[:, cols] = du
            dhn = dhn + _dot_nt(du, wu_ref[j])
        _, xh, r = _rms_fwd(h1_ref[...], g2_ref[...])
        dx, dg = _rms_bwd(dhn, xh, r, g2_ref[...])
        dg2_ref[...] += dg
        dh1 = dh2 + dx
        dh1_ref[...] = dh1
        dym = _dot_nt(dh1, wo_ref[...])
        gm = gm_ref[...]
        _, yh, rr = _group_rms_fwd(y_ref[...], gm)
        outs, dgs = [], []
        for (a, b), rg in zip(_GROUPS, rr):
            dxg, dgg = _rms_bwd(dym[:, a:b], yh[:, a:b], rg, gm[:, a:b])
            outs.append(dxg)
            dgs.append(dgg)
        dy_ref[...] = jnp.concatenate(outs, axis=1)
        dgm_ref[...] += jnp.concatenate(dgs, axis=1)

    tile = lambda w: pl.BlockSpec((tm, w), lambda i: (i, 0))
    return _call(
        body, name, (t // tm,),
        [tile(D_MODEL), tile(D_FF), tile(D_MODEL), tile(D_MODEL), _const_spec((1, D_MODEL)),
         _const_spec((D_MODEL, D_MODEL)), _const_spec((1, D_MODEL)),
         _const_spec((nj, D_MODEL, FF_BLK)), _const_spec((nj, FF_BLK, D_MODEL))],
        [tile(D_MODEL), tile(D_FF), tile(D_MODEL), _acc_spec((1, D_MODEL)), _acc_spec((1, D_MODEL))],
        [_sds((t, D_MODEL), F32), _sds((t, D_FF), MX), _sds((t, D_MODEL), F32), _sds((1, D_MODEL), F32),
         _sds((1, D_MODEL), F32)],
        [], [dh2, act, h1, ycat, gmix, w_out, g2, w_up, w_down], riders)


def _in_bwd(dz, h0, dh1, g1, w_in, name):
    t = h0.shape[0]
    tm = _tile(t)

    def body(dz_ref, h_ref, dh1_ref, g_ref, wg_ref, dh0_ref, dg_ref, w_scr):
        @pl.when(pl.program_id(0) == 0)
        def _():
            dg_ref[...] = jnp.zeros_like(dg_ref)
            _assemble_w_in(wg_ref, w_scr)

        dhn = _dot_nt(dz_ref[...], w_scr[...])
        _, xh, r = _rms_fwd(h_ref[...], g_ref[...])
        dx, dg = _rms_bwd(dhn, xh, r, g_ref[...])
        dg_ref[...] += dg
        dh0_ref[...] = dh1_ref[...] + dx

    tile = lambda w: pl.BlockSpec((tm, w), lambda i: (i, 0))
    (dh0, dg), _ = _call(
        body, name, (t // tm,),
        [tile(IN_W), tile(D_MODEL), tile(D_MODEL), _const_spec((1, D_MODEL)), _const_spec((N_DEV, D_MODEL, IN_SHARD))],
        [tile(D_MODEL), _acc_spec((1, D_MODEL))], [_sds((t, D_MODEL), F32), _sds((1, D_MODEL), F32)],
        [pltpu.VMEM((D_MODEL, IN_W), MX)], [dz, h0, dh1, g1, w_in])
    return dh0, dg


def _loss_head(h, gf, target, name):
    t = h.shape[0]
    tm = _tile(t)

    def body(h_ref, g_ref, t_ref, dh_ref, loss_ref, dg_ref):
        @pl.when(pl.program_id(0) == 0)
        def _():
            loss_ref[...] = jnp.zeros_like(loss_ref)
            dg_ref[...] = jnp.zeros_like(dg_ref)

        g = g_ref[...]
        y, xh, r = _rms_fwd(h_ref[...], g)
        err = y - t_ref[...]
        part = 0.5 * jnp.sum(jnp.mean(err * err, axis=-1, keepdims=True), axis=0, keepdims=True)
        loss_ref[...] += jnp.broadcast_to(part, loss_ref.shape)
        dx, dg = _rms_bwd(err * (1.0 / D_MODEL), xh, r, g)
        dg_ref[...] += dg
        dh_ref[...] = dx

    tile = pl.BlockSpec((tm, D_MODEL), lambda i: (i, 0))
    (dh, loss, dg), _ = _call(
        body, name, (t // tm,), [tile, _const_spec((1, D_MODEL)), tile],
        [tile, _acc_spec((1, 128)), _acc_spec((1, D_MODEL))],
        [_sds((t, D_MODEL), F32), _sds((1, 128), F32), _sds((1, D_MODEL), F32)], [], [h, gf, target])
    return dh, loss, dg


def _dw(x, y, name, split, bm, bn):
    t, m = x.shape
    n = y.shape[1]
    tk = _tile(t)
    nk = t // tk
    if split == "rows":
        assert bn == n
        r, c = m // N_DEV, n
        per = bm // r
        out_block = pl.BlockSpec((per, r, c), lambda a, b, k: (a, 0, 0))
    else:
        assert bm == m
        r, c = m, n // N_DEV
        per = bn // c
        out_block = pl.BlockSpec((per, r, c), lambda a, b, k: (b, 0, 0))

    def body(x_ref, y_ref, o_ref, o16_ref, acc):
        k = pl.program_id(2)

        @pl.when(k == 0)
        def _():
            acc[...] = jnp.zeros_like(acc)

        acc[...] += _dot_tn(x_ref[...], y_ref[...])

        @pl.when(k == nk - 1)
        def _():
            for d in range(per):
                v = acc[d * r:(d + 1) * r, :] if split == "rows" else acc[:, d * c:(d + 1) * c]
                o_ref[d] = v
                o16_ref[d] = v.astype(o16_ref.dtype)

    return pl.pallas_call(
        body, name=name, grid=(m // bm, n // bn, nk),
        in_specs=[pl.BlockSpec((tk, bm), lambda a, b, k: (k, a)), pl.BlockSpec((tk, bn), lambda a, b, k: (k, b))],
        out_specs=[out_block, out_block],
        out_shape=[_sds((N_DEV, r, c), F32), _sds((N_DEV, r, c), WIRE)],
        scratch_shapes=[pltpu.VMEM((bm, bn), F32)],
        compiler_params=pltpu.CompilerParams(dimension_semantics=("arbitrary",) * 3, vmem_limit_bytes=VMEM_LIMIT),
    )(x, y)


def _adamw_math(w, g, m, v):
    m = ADAM_B1 * m + (1.0 - ADAM_B1) * g
    v = ADAM_B2 * v + (1.0 - ADAM_B2) * jnp.square(g)
    m_hat = m / (1.0 - ADAM_B1 ** ADAM_STEP)
    v_hat = v / (1.0 - ADAM_B2 ** ADAM_STEP)
    delta = -ADAM_LR * (m_hat / (jnp.sqrt(v_hat) + ADAM_EPS) + ADAM_WD * w)
    return delta, m, v


def _adamw_shard(g_own, g_recv, dev, w, m, v, name):
    r, c = w.shape
    br = r
    for cand in (256, 128, 112, 64, 56, 32, 16, 8):
        if r % cand == 0:
            br = cand
            break

    def body(dev_ref, go_ref, gr_ref, w_ref, m_ref, v_ref, g_out, d_out, m_out, v_out):
        g = go_ref[0]
        for j in range(N_DEV - 1):
            g = g + gr_ref[j].astype(F32)
        delta, mn, vn = _adamw_math(w_ref[...], g, m_ref[...], v_ref[...])
        g_out[...] = g
        d_out[...] = delta
        m_out[...] = mn
        v_out[...] = vn

    tile = pl.BlockSpec((br, c), lambda i, dev_ref: (i, 0))
    return pl.pallas_call(
        body, name=name,
        grid_spec=pltpu.PrefetchScalarGridSpec(
            num_scalar_prefetch=1, grid=(r // br,),
            in_specs=[pl.BlockSpec((1, br, c), lambda i, dev_ref: (dev_ref[0], i, 0)),
                      pl.BlockSpec((N_DEV - 1, br, c), lambda i, dev_ref: (0, i, 0)),
                      tile, tile, tile],
            out_specs=[tile, tile, tile, tile]),
        out_shape=[_sds((r, c), F32)] * 4,
        compiler_params=pltpu.CompilerParams(dimension_semantics=("arbitrary",), vmem_limit_bytes=VMEM_LIMIT),
    )(dev, g_own, g_recv, w, m, v)


def _adamw_small(parts, ws, ms, vs, name):
    n = len(parts)

    def body(*refs):
        p_refs, w_refs, m_refs, v_refs = (refs[k * n:(k + 1) * n] for k in range(4))
        outs = refs[4 * n:]
        for k in range(n):
            g = p_refs[k][0]
            for d in range(1, N_DEV):
                g = g + p_refs[k][d]
            delta, mn, vn = _adamw_math(w_refs[k][...], g, m_refs[k][...], v_refs[k][...])
            outs[k][...] = g
            outs[n + k][...] = delta
            outs[2 * n + k][...] = mn
            outs[3 * n + k][...] = vn

    shapes = [_sds(w.shape, F32) for w in ws]
    res = pl.pallas_call(body, name=name, out_shape=shapes * 4,
                         compiler_params=pltpu.CompilerParams(vmem_limit_bytes=VMEM_LIMIT))(*parts, *ws, *ms, *vs)
    return res[:n], res[n:2 * n], res[2 * n:3 * n], res[3 * n:]


def _sum_parts(part, name):
    def body(p_ref, o_ref):
        g = p_ref[0]
        for d in range(1, N_DEV):
            g = g + p_ref[d]
        o_ref[...] = g

    return pl.pallas_call(body, name=name, out_shape=_sds(part.shape[1:], F32))(part)


def _block_diag(w):
    out = jnp.zeros((LRU_W, LRU_W), w.dtype)
    for h in range(4):
        out = lax.dynamic_update_slice(out, w[h], (h * 64, h * 64))
    return out


def _unblock_diag(w):
    return jnp.concatenate([w[h * 64:(h + 1) * 64, h * 64:(h + 1) * 64] for h in range(4)], axis=0)


def _layer_params(p, l):
    row = lambda a: a[l].reshape(1, -1)
    sink_rows = jnp.repeat(p["attn_sinks"][l].reshape(4, 2), 2 * BLK, axis=1)
    sink_rows = jnp.concatenate([sink_rows, jnp.zeros((4, 4 * BLK), F32)], axis=0)
    cw = jnp.concatenate([p["conv_dw_w"][l], jnp.zeros((1, CONV_W), F32)], axis=0)
    pv = jnp.concatenate([
        row(p["conv_dw_b"]), row(p["conv_ln_g"]), row(p["conv_ln_b"]), row(p["lru_conv_b"]), row(p["lru_ba"]),
        row(p["lru_bx"]), row(p["lru_lambda"]), jnp.zeros((1, LRU_W), F32), p["lru_conv_w"][l],
        jnp.zeros((4, LRU_W), F32)], axis=0)
    return dict(
        g1=row(p["norm1"]), sink=sink_rows, cw=cw, pv=pv,
        wa=_block_diag(p["lru_wa"][l]).astype(MX), wx=_block_diag(p["lru_wx"][l]).astype(MX),
        gmix=row(p["mix_norm"]), g2=row(p["norm2"]))


_SMALL = ["norm1", "attn_sinks", "conv_dw_w", "conv_dw_b", "conv_ln_g", "conv_ln_b", "lru_conv_w", "lru_conv_b",
          "lru_wa", "lru_ba", "lru_wx", "lru_bx", "lru_lambda", "mix_norm", "norm2"]
_BIG = ["w_in", "w_out", "w_up", "w_down"]
_WEIGHTS = ["norm1", "w_in", "attn_sinks", "conv_dw_w", "conv_dw_b", "conv_ln_g", "conv_ln_b", "lru_conv_w",
            "lru_conv_b", "lru_wa", "lru_ba", "lru_wx", "lru_bx", "lru_lambda", "mix_norm", "w_out", "norm2", "w_up",
            "w_down", "final_norm"]


def kernel(x, norm1, w_in, attn_sinks, conv_dw_w, conv_dw_b, conv_ln_g, conv_ln_b, lru_conv_w, lru_conv_b, lru_wa, lru_ba, lru_wx, lru_bx, lru_lambda, mix_norm, w_out, norm2, w_up, w_down, final_norm, loss_target, m_norm1, m_w_in, m_attn_sinks, m_conv_dw_w, m_conv_dw_b, m_conv_ln_g, m_conv_ln_b, m_lru_conv_w, m_lru_conv_b, m_lru_wa, m_lru_ba, m_lru_wx, m_lru_bx, m_lru_lambda, m_mix_norm, m_w_out, m_norm2, m_w_up, m_w_down, m_final_norm, v_norm1, v_w_in, v_attn_sinks, v_conv_dw_w, v_conv_dw_b, v_conv_ln_g, v_conv_ln_b, v_lru_conv_w, v_lru_conv_b, v_lru_wa, v_lru_ba, v_lru_wx, v_lru_bx, v_lru_lambda, v_mix_norm, v_w_out, v_norm2, v_w_up, v_w_down, v_final_norm):
    w = dict(norm1=norm1, w_in=w_in, attn_sinks=attn_sinks, conv_dw_w=conv_dw_w, conv_dw_b=conv_dw_b,
             conv_ln_g=conv_ln_g, conv_ln_b=conv_ln_b, lru_conv_w=lru_conv_w, lru_conv_b=lru_conv_b, lru_wa=lru_wa,
             lru_ba=lru_ba, lru_wx=lru_wx, lru_bx=lru_bx, lru_lambda=lru_lambda, mix_norm=mix_norm, w_out=w_out,
             norm2=norm2, w_up=w_up, w_down=w_down, final_norm=final_norm)
    m = dict(norm1=m_norm1, w_in=m_w_in, attn_sinks=m_attn_sinks, conv_dw_w=m_conv_dw_w, conv_dw_b=m_conv_dw_b,
             conv_ln_g=m_conv_ln_g, conv_ln_b=m_conv_ln_b, lru_conv_w=m_lru_conv_w, lru_conv_b=m_lru_conv_b,
             lru_wa=m_lru_wa, lru_ba=m_lru_ba, lru_wx=m_lru_wx, lru_bx=m_lru_bx, lru_lambda=m_lru_lambda,
             mix_norm=m_mix_norm, w_out=m_w_out, norm2=m_norm2, w_up=m_w_up, w_down=m_w_down, final_norm=m_final_norm)
    v = dict(norm1=v_norm1, w_in=v_w_in, attn_sinks=v_attn_sinks, conv_dw_w=v_conv_dw_w, conv_dw_b=v_conv_dw_b,
             conv_ln_g=v_conv_ln_g, conv_ln_b=v_conv_ln_b, lru_conv_w=v_lru_conv_w, lru_conv_b=v_lru_conv_b,
             lru_wa=v_lru_wa, lru_ba=v_lru_ba, lru_wx=v_lru_wx, lru_bx=v_lru_bx, lru_lambda=v_lru_lambda,
             mix_norm=v_mix_norm, w_out=v_w_out, norm2=v_norm2, w_up=v_w_up, w_down=v_w_down, final_norm=v_final_norm)
    depth = w_in.shape[0]
    xi, yi, ci = _me()
    dev = (4 * xi + 2 * yi + ci).astype(jnp.int32)
    dev1 = dev.reshape(1)
    wb = {n: w[n].astype(MX) for n in _BIG}
    layer_shards = lambda l: [wb["w_out"][l], wb["w_up"][l], wb["w_down"][l]]

    _, ((g_in0, g_cw, g_lcw),) = _call(None, "gather_first", None, [], [], [], [], [],
                                        [_gather_rider([wb["w_in"][0], conv_dw_w, lru_conv_w])])
    cols = lambda g: jnp.moveaxis(g, 0, -2).reshape(g.shape[1:-1] + (N_DEV * g.shape[-1],))
    p = dict(w)
    p["conv_dw_w"] = cols(g_cw)
    p["lru_conv_w"] = cols(g_lcw)
    lp = [_layer_params(p, l) for l in range(depth)]

    gathered = [dict(w_in=g_in0), dict()]
    saved = []
    h = x[0]
    for l in range(depth):
        q, gw = lp[l], gathered[l]
        z, hn1 = _ln_in(h, q["g1"], gw["w_in"], f"ln_in{l}")
        riders = [_gather_rider(layer_shards(0))] if l == 0 else []
        (ycat, hl, uc), got = _mixer_fwd(z, q["sink"], q["cw"], q["pv"], q["wa"], q["wx"], f"mixer_fwd{l}", riders)
        if l == 0:
            gw["w_out"], gw["w_up"], gw["w_down"] = got[0]
            gw["w_out"] = gw["w_out"].reshape(D_MODEL, D_MODEL)
        riders = [_gather_rider([wb["w_in"][1]] + layer_shards(1))] if l == 0 else []
        (h1, act, h2, ym, hn2), got = _post_fwd(ycat, h, q["gmix"], gw["w_out"], q["g2"], gw["w_up"], gw["w_down"],
                                                f"post_fwd{l}", riders)
        if l == 0:
            nxt = gathered[1]
            nxt["w_in"], nxt["w_out"], nxt["w_up"], nxt["w_down"] = got[0]
            nxt["w_out"] = nxt["w_out"].reshape(D_MODEL, D_MODEL)
        saved.append(dict(h0=h, z=z, hn1=hn1, ycat=ycat, hl=hl, uc=uc, h1=h1, act=act, ym=ym, hn2=hn2))
        h = h2
    dh, loss, dgf = _loss_head(h, final_norm.reshape(1, -1), loss_target[0], "loss_head")

    grads = [None] * depth
    big = {n: [None] * depth for n in _BIG}
    pending = []

    def send_pending():
        riders = [_scatter_rider([item[3] for item in pending])] if pending else []
        return riders, list(pending)

    def record(sent, got):
        for item, recv in zip(sent, got[0] if sent else []):
            big[item[0]][item[1]] = (item[2], recv)
        del pending[:len(sent)]

    for l in reversed(range(depth)):
        q, s, gw = lp[l], saved[l], gathered[l]
        riders, sent = send_pending()
        (dh1, du, dycat, dg2, dgm), got = _post_bwd(dh, s["act"], s["h1"], s["ycat"], q["gmix"], gw["w_out"], q["g2"],
                                                    gw["w_up"], gw["w_down"], f"post_bwd{l}", riders)
        record(sent, got)
        pending.append(("w_down", l) + tuple(_dw(s["act"], dh, f"dw_down{l}", "rows", 2048, D_MODEL)))
        pending.append(("w_up", l) + tuple(_dw(s["hn2"], du, f"dw_up{l}", "cols", D_MODEL, 2048)))
        pending.append(("w_out", l) + tuple(_dw(s["ym"], dh1, f"dw_out{l}", "rows", D_MODEL, D_MODEL)))
        riders, sent = send_pending()
        (dz, dsink, dcw, dpv, dwa, dwx), got = _mixer_bwd(dycat, s["z"], s["ycat"], s["hl"], s["uc"], q["sink"],
                                                          q["cw"], q["pv"], q["wa"], q["wx"], f"mixer_bwd{l}", riders)
        record(sent, got)
        pending.append(("w_in", l) + tuple(_dw(s["hn1"], dz, f"dw_in{l}", "cols", D_MODEL, IN_W)))
        dh, dg1 = _in_bwd(dz, s["h0"], dh1, q["g1"], gw["w_in"], f"in_bwd{l}")
        grads[l] = dict(
            norm1=dg1[0], attn_sinks=jnp.stack([dsink[0:4, 0], dsink[0:4, 2 * BLK]], axis=1).reshape(8),
            conv_dw_w=dcw[0:CONV_K], conv_dw_b=dpv[R_CONV_B], conv_ln_g=dpv[R_LN_G], conv_ln_b=dpv[R_LN_B],
            lru_conv_w=dpv[R_LCW:R_LCW + LRU_K], lru_conv_b=dpv[R_LCONV_B], lru_wa=_unblock_diag(dwa),
            lru_ba=dpv[R_BA].reshape(4, 64), lru_wx=_unblock_diag(dwx), lru_bx=dpv[R_BX].reshape(4, 64),
            lru_lambda=dpv[R_LAM], mix_norm=dgm[0], norm2=dg2[0])

    small = [jnp.stack([grads[l][n] for l in range(depth)]) for n in _SMALL] + [dgf, loss]
    riders, sent = send_pending()
    _, got = _call(None, "tail_exchange", None, [], [], [], [], [], riders + [_gather_rider(small)])
    record(sent, got)
    parts_all = got[1]

    out = {}
    for n in _BIG:
        res = [_adamw_shard(big[n][l][0], big[n][l][1], dev1, w[n][l], m[n][l], v[n][l], f"adamw_{n}{l}")
               for l in range(depth)]
        out[n] = [jnp.stack([res[l][j] for l in range(depth)]) for j in range(4)]
    shard = lambda a: lax.dynamic_slice_in_dim(a, dev * (a.shape[-1] // N_DEV), a.shape[-1] // N_DEV, axis=a.ndim - 1)
    flat = {"lru_wa": (depth, LRU_W, 64), "lru_wx": (depth, LRU_W, 64), "final_norm": (1, D_MODEL)}
    parts, ws, ms, vs = [], [], [], []
    for n, g in zip(_SMALL + ["final_norm"], parts_all[:-1]):
        parts.append(shard(g) if n in ("conv_dw_w", "lru_conv_w") else g)
        shp = flat.get(n, w[n].shape)
        ws.append(w[n].reshape(shp))
        ms.append(m[n].reshape(shp))
        vs.append(v[n].reshape(shp))
    sg, sd, sm, sv = _adamw_small(parts, ws, ms, vs, "adamw_small")
    for j, n in enumerate(_SMALL + ["final_norm"]):
        out[n] = [a.reshape(w[n].shape) for a in (sg[j], sd[j], sm[j], sv[j])]
    loss_total = _sum_parts(parts_all[-1], "loss_sum")[0, 0]

    result = [loss_total, dh[None]]
    for j in range(4):
        result += [out[n][j] for n in _WEIGHTS]
    return tuple(result)
```

```python
import types

import jax
import jax.numpy as jnp
from jax import lax
from jax.experimental import pallas as pl
from jax.experimental.pallas import tpu as pltpu

F32 = jnp.float32
MX = jnp.bfloat16
WIRE = jnp.bfloat16

D_MODEL = 1024
HEAD_DIM = 64
ATTN_W = 512
KV_W = 128
BLK = 128
CONV_W = 256
CONV_K = 31
LRU_W = 256
LRU_K = 4
LRU_C = 8.0
IN_W = 1792
D_FF = 4096
FF_BLK = 512
N_DEV = 8
IN_SHARD = IN_W // N_DEV
RMS_EPS = 1e-6
LN_EPS = 1e-5
MASK_VALUE = -1e30
SCALE = HEAD_DIM ** -0.5
CONV_HALO = 32
LRU_HALO = 8
CONV_CHUNK = 64
POST_TILE = 512
Q0, K0, V0, CV0, CG0, RX0, RG0 = 0, 512, 640, 768, 1024, 1280, 1536
R_CONV_B, R_LN_G, R_LN_B, R_LCONV_B, R_BA, R_BX, R_LAM, R_LCW = 0, 1, 2, 3, 4, 5, 6, 8

ADAM_LR, ADAM_B1, ADAM_B2, ADAM_EPS, ADAM_WD, ADAM_STEP = 0.001, 0.9, 0.999, 1e-08, 0.01, 10

VMEM_LIMIT = 56 * 1024 * 1024
MESH = pl.DeviceIdType.MESH
ANY = pl.BlockSpec(memory_space=pl.ANY)


def _tile(t, cap=512):
    return min(cap, t)


def _dot(a, b):
    return jnp.dot(a.astype(MX), b.astype(MX), preferred_element_type=F32)


def _dot_nt(a, b):
    return lax.dot_general(a.astype(MX), b.astype(MX), (((1,), (1,)), ((), ())), preferred_element_type=F32)


def _dot_tn(a, b):
    return lax.dot_general(a.astype(MX), b.astype(MX), (((0,), (0,)), ((), ())), preferred_element_type=F32)


def _const_spec(shape):
    nd = len(shape)
    return pl.BlockSpec(shape, lambda *_: (0,) * nd, pipeline_mode=pl.Buffered(1))


def _acc_spec(shape):
    nd = len(shape)
    return pl.BlockSpec(shape, lambda *_: (0,) * nd)


def _sds(shape, dtype):
    return jax.ShapeDtypeStruct(shape, dtype)


def _sigmoid(x):
    return jax.nn.sigmoid(x)


def _rms_fwd(x, g):
    r = lax.rsqrt(jnp.mean(x * x, axis=-1, keepdims=True) + RMS_EPS)
    xh = x * r
    return xh * g, xh, r


def _rms_bwd(dy, xh, r, g):
    t = dy * g
    dx = r * (t - xh * jnp.mean(t * xh, axis=-1, keepdims=True))
    return dx, jnp.sum(dy * xh, axis=0, keepdims=True)


_GROUPS = ((0, 512), (512, 768), (768, 1024))


def _group_rms_fwd(y, g):
    parts = [_rms_fwd(y[:, a:b], g[:, a:b]) for a, b in _GROUPS]
    return (jnp.concatenate([p[0] for p in parts], axis=1),
            jnp.concatenate([p[1] for p in parts], axis=1),
            [p[2] for p in parts])


def _gelu(x):
    c = 0.7978845608028654
    u = c * (x + 0.044715 * x * x * x)
    th = jnp.tanh(u)
    val = 0.5 * x * (1.0 + th)
    grad = 0.5 * (1.0 + th) + 0.5 * x * (1.0 - th * th) * c * (1.0 + 3.0 * 0.044715 * x * x)
    return val, grad


def _neg_expm1(x):
    series = -x * (1.0 + x * (0.5 + x * (1.0 / 6.0 + x * (1.0 / 24.0))))
    return jnp.where(x > -0.02, series, 1.0 - jnp.exp(x))


def _me():
    return lax.axis_index("x"), lax.axis_index("y"), lax.axis_index("c")


def _gather_rider(arrays):
    arrays = list(arrays)
    n = len(arrays)

    def plan(ins, outs, sems):
        ssem, rsem, lsem = sems
        x, y, c = _me()
        chips = [(1 - x, y), (x, 1 - y), (1 - x, 1 - y)]

        def copy(a, k, block, to, own=False):
            dst = outs[a].at[4 * block[0] + 2 * block[1] + block[2]]
            return pltpu.make_async_remote_copy(
                src_ref=ins[a] if own else dst, dst_ref=dst, send_sem=ssem.at[7 * a + k],
                recv_sem=rsem.at[7 * a + k], device_id=to, device_id_type=MESH)

        return x, y, c, chips, copy, lsem

    def start(ins, outs, sems):
        x, y, c, chips, copy, lsem = plan(ins, outs, sems)
        for a in range(n):
            pltpu.make_async_copy(ins[a], outs[a].at[4 * x + 2 * y + c], lsem.at[a]).start()
            copy(a, 0, (x, y, c), (x, y, 1 - c), own=True).start()
            for j, chip in enumerate(chips):
                copy(a, 1 + j, (x, y, c), (*chip, c), own=True).start()

    def mid(ins, outs, sems):
        x, y, c, chips, copy, _ = plan(ins, outs, sems)
        for a in range(n):
            for j, chip in enumerate(chips):
                copy(a, 1 + j, (*chip, c), (x, y, c)).wait_recv()
                copy(a, 4 + j, (*chip, c), (x, y, 1 - c)).start()

    def finish(ins, outs, sems):
        x, y, c, chips, copy, lsem = plan(ins, outs, sems)
        for a in range(n):
            copy(a, 0, (x, y, 1 - c), (x, y, c)).wait_recv()
            for j, chip in enumerate(chips):
                copy(a, 4 + j, (*chip, 1 - c), (x, y, c)).wait_recv()
        for a in range(n):
            copy(a, 0, (x, y, c), (x, y, 1 - c), own=True).wait_send()
            for j, chip in enumerate(chips):
                copy(a, 1 + j, (x, y, c), (*chip, c), own=True).wait_send()
                copy(a, 4 + j, (*chip, c), (x, y, 1 - c)).wait_send()
            pltpu.make_async_copy(ins[a], outs[a].at[4 * x + 2 * y + c], lsem.at[a]).wait()

    return types.SimpleNamespace(
        arrays=arrays, out_shape=[_sds((N_DEV,) + a.shape, a.dtype) for a in arrays],
        scratch=[pltpu.SemaphoreType.DMA((7 * n,)), pltpu.SemaphoreType.DMA((7 * n,)), pltpu.SemaphoreType.DMA((n,))],
        start=start, mid=mid, finish=finish)


def _bcast_rider(arrays):
    arrays = list(arrays)
    n = len(arrays)

    def copies(ins, outs, sems, landing):
        ssem, rsem, lsem = sems
        x, y, c = _me()
        out = []
        for a in range(n):
            out.append(pltpu.make_async_copy(ins[a], outs[a].at[4 * x + 2 * y + c], lsem.at[a]))
            for f in range(1, N_DEV):
                px = 1 - x if f & 4 else x
                py = 1 - y if f & 2 else y
                pc = 1 - c if f & 1 else c
                slot = 4 * px + 2 * py + pc if landing else 4 * x + 2 * y + c
                out.append(pltpu.make_async_remote_copy(
                    src_ref=ins[a], dst_ref=outs[a].at[slot], send_sem=ssem.at[7 * a + f - 1],
                    recv_sem=rsem.at[7 * a + f - 1], device_id=(px, py, pc), device_id_type=MESH))
        return out

    def start(ins, outs, sems):
        for cp in copies(ins, outs, sems, landing=False):
            cp.start()

    def finish(ins, outs, sems):
        for cp in copies(ins, outs, sems, landing=True):
            cp.wait()

    return types.SimpleNamespace(
        arrays=arrays, out_shape=[_sds((N_DEV,) + a.shape, a.dtype) for a in arrays],
        scratch=[pltpu.SemaphoreType.DMA((7 * n,)), pltpu.SemaphoreType.DMA((7 * n,)), pltpu.SemaphoreType.DMA((n,))],
        start=start, mid=None, finish=finish)


def _scatter_rider(arrays):
    arrays = list(arrays)
    n = len(arrays)

    def copies(ins, outs, sems):
        ssem, rsem = sems
        x, y, c = _me()
        out = []
        for a in range(n):
            for f in range(1, N_DEV):
                px = 1 - x if f & 4 else x
                py = 1 - y if f & 2 else y
                pc = 1 - c if f & 1 else c
                out.append(pltpu.make_async_remote_copy(
                    src_ref=ins[a].at[4 * px + 2 * py + pc], dst_ref=outs[a].at[f - 1], send_sem=ssem.at[7 * a + f - 1],
                    recv_sem=rsem.at[7 * a + f - 1], device_id=(px, py, pc), device_id_type=MESH))
        return out

    def start(ins, outs, sems):
        for cp in copies(ins, outs, sems):
            cp.start()

    def finish(ins, outs, sems):
        for cp in copies(ins, outs, sems):
            cp.wait()

    return types.SimpleNamespace(
        arrays=arrays, out_shape=[_sds((N_DEV - 1,) + a.shape[1:], a.dtype) for a in arrays],
        scratch=[pltpu.SemaphoreType.DMA((7 * n,)), pltpu.SemaphoreType.DMA((7 * n,))],
        start=start, mid=None, finish=finish)


def _call(body, name, grid, in_specs, out_specs, out_shape, scratch, operands, riders=()):
    n_in, n_out, n_scr = len(operands), len(out_shape), len(scratch)
    nsteps = grid[0] if grid else 1
    sizes = [(len(r.arrays), len(r.out_shape), len(r.scratch)) for r in riders]

    def wrapped(*refs):
        pos = n_in
        r_ins = []
        for ri, _, _ in sizes:
            r_ins.append(refs[pos:pos + ri])
            pos += ri
        outs = refs[pos:pos + n_out]
        pos += n_out
        r_outs = []
        for _, ro, _ in sizes:
            r_outs.append(refs[pos:pos + ro])
            pos += ro
        scr = refs[pos:pos + n_scr]
        pos += n_scr
        r_sems = []
        for _, _, rs in sizes:
            r_sems.append(refs[pos:pos + rs])
            pos += rs
        step = pl.program_id(0) if grid else 0

        def at(s, fn):
            if grid:
                pl.when(step == s)(fn)
            else:
                fn()

        for r, a, b, c in zip(riders, r_ins, r_outs, r_sems):
            at(0, lambda r=r, a=a, b=b, c=c: r.start(a, b, c))
        for r, a, b, c in zip(riders, r_ins, r_outs, r_sems):
            if r.mid is not None:
                at((3 * nsteps) // 4, lambda r=r, a=a, b=b, c=c: r.mid(a, b, c))
        if body is not None:
            body(*refs[:n_in], *outs, *scr)
        for r, a, b, c in zip(riders, r_ins, r_outs, r_sems):
            at(nsteps - 1, lambda r=r, a=a, b=b, c=c: r.finish(a, b, c))

    r_arrays = [a for r in riders for a in r.arrays]
    r_shapes = [s for r in riders for s in r.out_shape]
    kwargs = {}
    if grid:
        kwargs = dict(grid=grid, compiler_params=pltpu.CompilerParams(
            dimension_semantics=("arbitrary",) * len(grid), vmem_limit_bytes=VMEM_LIMIT))
    res = pl.pallas_call(
        wrapped, name=name,
        in_specs=list(in_specs) + [ANY] * len(r_arrays),
        out_specs=list(out_specs) + [ANY] * len(r_shapes),
        out_shape=list(out_shape) + r_shapes,
        scratch_shapes=list(scratch) + [s for r in riders for s in r.scratch],
        **kwargs,
    )(*operands, *r_arrays)
    host, rest = res[:n_out], res[n_out:]
    r_res = []
    for _, ro, _ in sizes:
        r_res.append(rest[:ro])
        rest = rest[ro:]
    return host, r_res


def _assemble_w_in(wg_ref, w_scr):
    for j in range(N_DEV):
        w_scr[:, j * IN_SHARD:(j + 1) * IN_SHARD] = wg_ref[j]


def _ln_in(h, g1, w_in, name):
    t = h.shape[0]
    tm = _tile(t)

    def body(h_ref, g_ref, wg_ref, z_ref, hn_ref, w_scr):
        @pl.when(pl.program_id(0) == 0)
        def _():
            _assemble_w_in(wg_ref, w_scr)

        y, _, _ = _rms_fwd(h_ref[...], g_ref[...])
        hn = y.astype(MX)
        hn_ref[...] = hn
        z_ref[...] = jnp.dot(hn, w_scr[...], preferred_element_type=F32)

    tile = lambda w: pl.BlockSpec((tm, w), lambda i: (i, 0))
    (z, hn), _ = _call(
        body, name, (t // tm,),
        [tile(D_MODEL), _const_spec((1, D_MODEL)), _const_spec((N_DEV, D_MODEL, IN_SHARD))],
        [tile(IN_W), tile(D_MODEL)], [_sds((t, IN_W), F32), _sds((t, D_MODEL), MX)],
        [pltpu.VMEM((D_MODEL, IN_W), MX)], [h, g1, w_in])
    return z, hn


def _band2(kb, g):
    lo = lax.broadcasted_iota(jnp.int32, kb.shape, 1) < HEAD_DIM
    kr = pltpu.roll(kb, HEAD_DIM, 1)
    if g == 0:
        top, bot = jnp.where(lo, kb, 0.0), jnp.where(lo, 0.0, kr)
    else:
        top, bot = jnp.where(lo, kr, 0.0), jnp.where(lo, 0.0, kb)
    return jnp.concatenate([top, bot], axis=0)


def _attn_block(z_ref, zh_ref, sink_ref, b, first):
    rows = slice(b * BLK, (b + 1) * BLK)
    prev = zh_ref if b == 0 else z_ref
    prow = slice(0, BLK) if b == 0 else slice((b - 1) * BLK, b * BLK)
    kb = jnp.concatenate([prev[prow, K0:K0 + KV_W], z_ref[rows, K0:K0 + KV_W]], axis=0)
    vb = jnp.concatenate([prev[prow, V0:V0 + KV_W], z_ref[rows, V0:V0 + KV_W]], axis=0)
    k2 = [_band2(kb, g) for g in range(2)]
    v2 = [_band2(vb, g) for g in range(2)]
    q2 = [jnp.concatenate([z_ref[rows, (2 * g) * BLK:(2 * g + 1) * BLK], z_ref[rows, (2 * g + 1) * BLK:(2 * g + 2) * BLK]],
                          axis=0) for g in range(2)]
    rr = lax.broadcasted_iota(jnp.int32, (4 * BLK, 2 * BLK), 0) & (BLK - 1)
    cc = lax.broadcasted_iota(jnp.int32, (4 * BLK, 2 * BLK), 1)
    first_block = jnp.logical_and(first, b == 0).astype(jnp.int32)
    mask = jnp.logical_and(jnp.logical_and(cc > rr, cc <= rr + BLK), cc >= BLK * first_block)
    s = jnp.concatenate([_dot_nt(q2[g], k2[g]) for g in range(2)], axis=0) * SCALE
    w = 2 * BLK
    out, psink = [], []
    for hh in range(2):
        sh = jnp.where(mask, s[:, hh * w:(hh + 1) * w], MASK_VALUE)
        sk = jnp.concatenate([jnp.broadcast_to(sink_ref[p:p + 1, hh * w:hh * w + 1], (BLK, 1)) for p in range(4)], axis=0)
        m = jnp.maximum(jnp.max(sh, axis=1, keepdims=True), sk)
        p = jnp.exp(sh - m)
        es = jnp.exp(sk - m)
        inv = 1.0 / (jnp.sum(p, axis=1, keepdims=True) + es)
        out.append(p * inv)
        psink.append(es * inv)
    return q2, k2, v2, jnp.concatenate(out, axis=1), psink


def _scan_fwd(a, b, tm):
    rows = lax.broadcasted_iota(jnp.int32, a.shape, 0)
    d = 1
    while d < tm:
        keep = rows >= d
        a_sh = jnp.where(keep, pltpu.roll(a, d, 0), 1.0)
        b_sh = jnp.where(keep, pltpu.roll(b, d, 0), 0.0)
        b = a * b_sh + b
        a = a * a_sh
        d *= 2
    return a, b


def _scan_bwd(c, b, tm):
    rows = lax.broadcasted_iota(jnp.int32, c.shape, 0)
    d = 1
    while d < tm:
        keep = rows < tm - d
        c_sh = jnp.where(keep, pltpu.roll(c, tm - d, 0), 1.0)
        b_sh = jnp.where(keep, pltpu.roll(b, tm - d, 0), 0.0)
        b = c * b_sh + b
        c = c * c_sh
        d *= 2
    return b


def _shifted_copies(ext, shifts, tm):
    rows = tm + CONV_HALO - 8
    for r in range(1, 8):
        shifts[r - 1, 0:rows, :] = ext[pl.ds(r, rows), :]


def _tap(ext, shifts, off, r0, n):
    a, r = divmod(off, 8)
    lo = 8 * a + r0
    if r == 0:
        return ext[lo:lo + n, :]
    return shifts[r - 1, lo:lo + n, :]


def _glu_fill(z_ref, zh_ref, uext, ush, first, tm, sg_out=None):
    cv = z_ref[:, CV0:CV0 + CONV_W]
    sg = _sigmoid(z_ref[:, CG0:CG0 + CONV_W])
    if sg_out is not None:
        sg_out[...] = sg
    hrow = BLK - CONV_HALO
    uh = zh_ref[hrow:BLK, CV0:CV0 + CONV_W] * _sigmoid(zh_ref[hrow:BLK, CG0:CG0 + CONV_W])
    uext[0:CONV_HALO, :] = jnp.where(first, 0.0, uh)
    uext[CONV_HALO:CONV_HALO + tm, :] = cv * sg
    _shifted_copies(uext, ush, tm)


def _conv_taps(cw_ref, pv_ref, uext, ush, out_ref, tm):
    for r0 in range(0, tm, CONV_CHUNK):
        acc = jnp.broadcast_to(pv_ref[R_CONV_B:R_CONV_B + 1, :], (CONV_CHUNK, CONV_W))
        for k in range(CONV_K):
            acc = acc + cw_ref[k:k + 1, :] * _tap(uext, ush, CONV_HALO - (CONV_K - 1) + k, r0, CONV_CHUNK)
        out_ref[r0:r0 + CONV_CHUNK, :] = acc


def _ln_silu(uc, pv_ref):
    mu = jnp.mean(uc, axis=-1, keepdims=True)
    xc = uc - mu
    rs = lax.rsqrt(jnp.mean(xc * xc, axis=-1, keepdims=True) + LN_EPS)
    xh = xc * rs
    ln = xh * pv_ref[R_LN_G:R_LN_G + 1, :] + pv_ref[R_LN_B:R_LN_B + 1, :]
    sg = _sigmoid(ln)
    return xh, rs, ln, sg


def _lru_gates(z_ref, zh_ref, pv_ref, wa_ref, wx_ref, rxext, first, tm):
    rxext[0:LRU_HALO, :] = jnp.where(first, 0.0, zh_ref[BLK - LRU_HALO:BLK, RX0:RX0 + LRU_W])
    rxext[LRU_HALO:LRU_HALO + tm, :] = z_ref[:, RX0:RX0 + LRU_W]
    xc = jnp.broadcast_to(pv_ref[R_LCONV_B:R_LCONV_B + 1, :], (tm, LRU_W))
    for k in range(LRU_K):
        xc = xc + pv_ref[R_LCW + k:R_LCW + k + 1, :] * rxext[pl.ds(LRU_HALO - (LRU_K - 1) + k, tm), :]
    r = _sigmoid(_dot(xc, wa_ref[...]) + pv_ref[R_BA:R_BA + 1, :])
    ig = _sigmoid(_dot(xc, wx_ref[...]) + pv_ref[R_BX:R_BX + 1, :])
    lam = pv_ref[R_LAM:R_LAM + 1, :]
    sp = jnp.log1p(jnp.exp(-lam))
    la = (-LRU_C * r) * sp
    a = jnp.exp(la)
    mult = jnp.sqrt(_neg_expm1(2.0 * la))
    return xc, r, ig, sp, la, a, mult


def _mixer_in_specs(tm, tile_of):
    hb = tm // BLK
    return [
        pl.BlockSpec((tm, IN_W), lambda i: (tile_of(i), 0)),
        pl.BlockSpec((BLK, IN_W), lambda i: (jnp.maximum(tile_of(i) * hb - 1, 0), 0)),
        _const_spec((8, 4 * BLK)),
        _const_spec((32, CONV_W)),
        _const_spec((16, CONV_W)),
        _const_spec((LRU_W, LRU_W)),
        _const_spec((LRU_W, LRU_W)),
    ]


def _mixer_fwd(z, sink, cw, pv, wa, wx, name, riders=()):
    t = z.shape[0]
    tm = _tile(t)
    nb = tm // BLK

    def body(z_ref, zh_ref, sink_ref, cw_ref, pv_ref, wa_ref, wx_ref, y_ref, hl_ref, uc_ref, uext, ush, rxext, hcar):
        i = pl.program_id(0)
        first = i == 0

        @pl.when(first)
        def _():
            hcar[...] = jnp.zeros_like(hcar)

        for b in range(nb):
            rows = slice(b * BLK, (b + 1) * BLK)
            _, _, v2, prob, _ = _attn_block(z_ref, zh_ref, sink_ref, b, first)
            for g in range(2):
                o = _dot(prob[2 * g * BLK:(2 * g + 2) * BLK], v2[g])
                y_ref[rows, (2 * g) * BLK:(2 * g + 1) * BLK] = o[0:BLK]
                y_ref[rows, (2 * g + 1) * BLK:(2 * g + 2) * BLK] = o[BLK:2 * BLK]
        _glu_fill(z_ref, zh_ref, uext, ush, first, tm)
        _conv_taps(cw_ref, pv_ref, uext, ush, uc_ref, tm)
        _, _, ln, sg = _ln_silu(uc_ref[...], pv_ref)
        y_ref[:, ATTN_W:ATTN_W + CONV_W] = ln * sg
        xc, _, ig, _, _, a, mult = _lru_gates(z_ref, zh_ref, pv_ref, wa_ref, wx_ref, rxext, first, tm)
        acum, h = _scan_fwd(a, mult * (ig * xc), tm)
        h = h + acum * hcar[0:1, :]
        hl_ref[...] = h
        hcar[0:1, :] = h[tm - 1:tm, :]
        gl, _ = _gelu(z_ref[:, RG0:RG0 + LRU_W])
        y_ref[:, ATTN_W + CONV_W:ATTN_W + CONV_W + LRU_W] = h * gl

    tile = lambda w: pl.BlockSpec((tm, w), lambda i: (i, 0))
    return _call(
        body, name, (t // tm,), _mixer_in_specs(tm, lambda i: i),
        [tile(D_MODEL), tile(LRU_W), tile(CONV_W)],
        [_sds((t, D_MODEL), F32), _sds((t, LRU_W), F32), _sds((t, CONV_W), F32)],
        [pltpu.VMEM((tm + CONV_HALO, CONV_W), F32), pltpu.VMEM((7, tm + CONV_HALO - 8, CONV_W), F32),
         pltpu.VMEM((tm + LRU_HALO, LRU_W), F32), pltpu.VMEM((8, LRU_W), F32)],
        [z, z, sink, cw, pv, wa, wx], riders)


def _mixer_bwd(dy, z, ycat, hl, uc, sink, cw, pv, wa, wx, name, riders=()):
    t = z.shape[0]
    tm = _tile(t)
    nt = t // tm
    nb = tm // BLK
    rev = lambda i: nt - 1 - i

    def body(dy_ref, z_ref, zh_ref, sink_ref, cw_ref, pv_ref, wa_ref, wx_ref, y_ref, hl_ref, hlh_ref, uc_ref,
             dz_ref, dsink_ref, dcw_ref, dpv_ref, dwa_ref, dwx_ref,
             uext, ush, sgs, rxext, dkext, dvext, ducext, dsh, dcw8, dxcext, kcar, vcar, uccar, xccar, gcar):
        i = pl.program_id(0)
        first = i == nt - 1

        @pl.when(i == 0)
        def _():
            for car in (kcar, vcar, uccar, xccar, gcar, dcw8):
                car[...] = jnp.zeros_like(car)
            dsink_ref[...] = jnp.zeros_like(dsink_ref)
            dpv_ref[...] = jnp.zeros_like(dpv_ref)
            dwa_ref[...] = jnp.zeros_like(dwa_ref)
            dwx_ref[...] = jnp.zeros_like(dwx_ref)

        def addrow(r, val):
            dpv_ref[r:r + 1, :] += jnp.sum(val, axis=0, keepdims=True)

        dkext[:, 0:tm] = jnp.zeros((KV_W, tm), F32)
        dvext[:, 0:tm] = jnp.zeros((KV_W, tm), F32)
        dkext[:, tm:tm + BLK] = kcar[...]
        dvext[:, tm:tm + BLK] = vcar[...]
        lane512 = lax.broadcasted_iota(jnp.int32, (1, 4 * BLK), 1) < 2 * BLK
        lo = lax.broadcasted_iota(jnp.int32, (4 * BLK, BLK), 1) < HEAD_DIM
        hd, w2 = HEAD_DIM, 2 * BLK
        for b in range(nb):
            rows = slice(b * BLK, (b + 1) * BLK)
            band = slice(b * BLK, (b + 2) * BLK)
            q2, k2, v2, prob, psink = _attn_block(z_ref, zh_ref, sink_ref, b, first)
            stack = lambda ref: jnp.concatenate([ref[rows, p * BLK:(p + 1) * BLK] for p in range(4)], axis=0)
            do4 = stack(dy_ref)
            dlt = do4 * stack(y_ref)
            d0 = jnp.sum(jnp.where(lo, dlt, 0.0), axis=1, keepdims=True)
            d1 = jnp.sum(jnp.where(lo, 0.0, dlt), axis=1, keepdims=True)
            dp = jnp.concatenate([_dot_nt(do4[g * w2:(g + 1) * w2], v2[g]) for g in range(2)], axis=0)
            dl = jnp.concatenate([jnp.broadcast_to(d0, (4 * BLK, w2)), jnp.broadcast_to(d1, (4 * BLK, w2))], axis=1)
            draw = (prob * (dp - dl)) * SCALE
            e0, e1 = psink[0] * d0, psink[1] * d1
            for p in range(4):
                prs = slice(p * BLK, (p + 1) * BLK)
                s0 = jnp.sum(e0[prs], axis=0, keepdims=True)
                s1 = jnp.sum(e1[prs], axis=0, keepdims=True)
                dsink_ref[p:p + 1, :] += -jnp.where(lane512, s0, s1)
            for g in range(2):
                grs = slice(g * w2, (g + 1) * w2)
                dq = _dot(draw[grs], k2[g])
                dz_ref[rows, (2 * g) * BLK:(2 * g + 1) * BLK] = dq[0:BLK].astype(dz_ref.dtype)
                dz_ref[rows, (2 * g + 1) * BLK:(2 * g + 2) * BLK] = dq[BLK:2 * BLK].astype(dz_ref.dtype)
                tk = _dot_tn(q2[g], draw[grs])
                tv = _dot_tn(do4[grs], prob[grs])
                dkext[g * hd:(g + 1) * hd, band] += tk[0:hd, 0:w2] + tk[hd:2 * hd, w2:2 * w2]
                dvext[g * hd:(g + 1) * hd, band] += tv[0:hd, 0:w2] + tv[hd:2 * hd, w2:2 * w2]
        dz_ref[:, K0:K0 + KV_W] = jnp.transpose(dkext[:, BLK:BLK + tm]).astype(dz_ref.dtype)
        dz_ref[:, V0:V0 + KV_W] = jnp.transpose(dvext[:, BLK:BLK + tm]).astype(dz_ref.dtype)
        kcar[...] = dkext[:, 0:BLK]
        vcar[...] = dvext[:, 0:BLK]

        _glu_fill(z_ref, zh_ref, uext, ush, first, tm, sg_out=sgs)
        xh, rs, ln, sg = _ln_silu(uc_ref[...], pv_ref)
        dln = dy_ref[:, ATTN_W:ATTN_W + CONV_W] * (sg * (1.0 + ln * (1.0 - sg)))
        addrow(R_LN_G, dln * xh)
        addrow(R_LN_B, dln)
        dxh = dln * pv_ref[R_LN_G:R_LN_G + 1, :]
        duc = rs * (dxh - jnp.mean(dxh, axis=-1, keepdims=True) - xh * jnp.mean(dxh * xh, axis=-1, keepdims=True))
        addrow(R_CONV_B, duc)
        ducext[0:tm, :] = duc
        ducext[tm:tm + CONV_HALO, :] = uccar[...]
        uccar[...] = duc[0:CONV_HALO, :]
        _shifted_copies(ducext, dsh, tm)
        for r0 in range(0, tm, CONV_CHUNK):
            crow = slice(r0, r0 + CONV_CHUNK)
            duc_c = ducext[crow, :]
            du = jnp.zeros((CONV_CHUNK, CONV_W), F32)
            for k in range(CONV_K):
                prod = duc_c * _tap(uext, ush, CONV_HALO - (CONV_K - 1) + k, r0, CONV_CHUNK)
                part = prod[0:8]
                for s in range(8, CONV_CHUNK, 8):
                    part = part + prod[s:s + 8]
                dcw8[k] += part
                du = du + cw_ref[k:k + 1, :] * _tap(ducext, dsh, CONV_K - 1 - k, r0, CONV_CHUNK)
            sgc = sgs[crow, :]
            dz_ref[crow, CV0:CV0 + CONV_W] = (du * sgc).astype(dz_ref.dtype)
            u_c = uext[CONV_HALO + r0:CONV_HALO + r0 + CONV_CHUNK, :]
            dz_ref[crow, CG0:CG0 + CONV_W] = (du * u_c * (1.0 - sgc)).astype(dz_ref.dtype)

        @pl.when(i == nt - 1)
        def _():
            dcw_ref[...] = jnp.sum(dcw8[...], axis=1)

        xc, r, ig, sp, la, a, mult = _lru_gates(z_ref, zh_ref, pv_ref, wa_ref, wx_ref, rxext, first, tm)
        h = hl_ref[...]
        rowi = lax.broadcasted_iota(jnp.int32, (tm, LRU_W), 0)
        hlast = jnp.where(first, 0.0, hlh_ref[7:8, :])
        hprev = jnp.where(rowi == 0, hlast, pltpu.roll(h, 1, 0))
        dyl = dy_ref[:, ATTN_W + CONV_W:ATTN_W + CONV_W + LRU_W]
        gl, dgl = _gelu(z_ref[:, RG0:RG0 + LRU_W])
        dz_ref[:, RG0:RG0 + LRU_W] = (dyl * h * dgl).astype(dz_ref.dtype)
        dh = dyl * gl + jnp.where(rowi == tm - 1, gcar[0:1, :], 0.0)
        c = jnp.where(rowi == tm - 1, 0.0, pltpu.roll(a, tm - 1, 0))
        gg = _scan_bwd(c, dh, tm)
        gcar[0:1, :] = a[0:1, :] * gg[0:1, :]
        dmult = gg * (ig * xc)
        dig = gg * mult * xc
        dxc = gg * mult * ig
        dla = gg * hprev * a - dmult * a * a / mult
        dr = dla * (-LRU_C * sp)
        lam = pv_ref[R_LAM:R_LAM + 1, :]
        dpv_ref[R_LAM:R_LAM + 1, :] += jnp.sum(dla * (-LRU_C * r), axis=0, keepdims=True) * (-_sigmoid(-lam))
        dpa = dr * r * (1.0 - r)
        dpx = dig * ig * (1.0 - ig)
        addrow(R_BA, dpa)
        addrow(R_BX, dpx)
        dxc = dxc + _dot_nt(dpa, wa_ref[...]) + _dot_nt(dpx, wx_ref[...])
        dwa_ref[...] += _dot_tn(xc, dpa)
        dwx_ref[...] += _dot_tn(xc, dpx)
        addrow(R_LCONV_B, dxc)
        dxcext[0:tm, :] = dxc
        dxcext[tm:tm + LRU_HALO, :] = xccar[...]
        xccar[...] = dxc[0:LRU_HALO, :]
        drx = jnp.zeros((tm, LRU_W), F32)
        for k in range(LRU_K):
            addrow(R_LCW + k, dxc * rxext[pl.ds(LRU_HALO - (LRU_K - 1) + k, tm), :])
            drx = drx + pv_ref[R_LCW + k:R_LCW + k + 1, :] * dxcext[pl.ds(LRU_K - 1 - k, tm), :]
        dz_ref[:, RX0:RX0 + LRU_W] = drx.astype(dz_ref.dtype)

    tile = lambda w: pl.BlockSpec((tm, w), lambda i: (rev(i), 0))
    in_specs = [tile(D_MODEL)] + _mixer_in_specs(tm, rev) + [
        tile(D_MODEL), tile(LRU_W),
        pl.BlockSpec((8, LRU_W), lambda i: (jnp.maximum(rev(i) * (tm // 8) - 1, 0), 0)),
        tile(CONV_W)]
    return _call(
        body, name, (nt,), in_specs,
        [tile(IN_W), _acc_spec((8, 4 * BLK)), _acc_spec((32, CONV_W)), _acc_spec((16, CONV_W)),
         _acc_spec((LRU_W, LRU_W)), _acc_spec((LRU_W, LRU_W))],
        [_sds((t, IN_W), MX), _sds((8, 4 * BLK), F32), _sds((32, CONV_W), F32), _sds((16, CONV_W), F32),
         _sds((LRU_W, LRU_W), F32), _sds((LRU_W, LRU_W), F32)],
        [pltpu.VMEM((tm + CONV_HALO, CONV_W), F32), pltpu.VMEM((7, tm + CONV_HALO - 8, CONV_W), F32),
         pltpu.VMEM((tm, CONV_W), F32), pltpu.VMEM((tm + LRU_HALO, LRU_W), F32),
         pltpu.VMEM((KV_W, tm + BLK), F32), pltpu.VMEM((KV_W, tm + BLK), F32),
         pltpu.VMEM((tm + CONV_HALO, CONV_W), F32), pltpu.VMEM((7, tm + CONV_HALO - 8, CONV_W), F32),
         pltpu.VMEM((32, 8, CONV_W), F32), pltpu.VMEM((tm + LRU_HALO, LRU_W), F32),
         pltpu.VMEM((KV_W, BLK), F32), pltpu.VMEM((KV_W, BLK), F32),
         pltpu.VMEM((CONV_HALO, CONV_W), F32), pltpu.VMEM((LRU_HALO, LRU_W), F32), pltpu.VMEM((8, LRU_W), F32)],
        [dy, z, z, sink, cw, pv, wa, wx, ycat, hl, hl, uc], riders)


def _post_fwd(ycat, h0, gmix, w_out, g2, w_up, w_down, name, riders=()):
    t = h0.shape[0]
    tm = _tile(t, POST_TILE)
    nj = D_FF // FF_BLK

    def body(y_ref, h_ref, gm_ref, wo_ref, g2_ref, wu_ref, wd_ref, h1_ref, a_ref, h2_ref, ym_ref, hn_ref):
        ym, _, _ = _group_rms_fwd(y_ref[...], gm_ref[...])
        ym = ym.astype(MX)
        ym_ref[...] = ym
        h1 = h_ref[...] + jnp.dot(ym, wo_ref[...], preferred_element_type=F32)
        h1_ref[...] = h1
        hn, _, _ = _rms_fwd(h1, g2_ref[...])
        hn = hn.astype(MX)
        hn_ref[...] = hn
        acc = h1
        for j in range(nj):
            u = jnp.dot(hn, wu_ref[j], preferred_element_type=F32)
            act = jnp.square(jnp.maximum(u, 0.0)).astype(MX)
            a_ref[:, j * FF_BLK:(j + 1) * FF_BLK] = act
            acc = acc + jnp.dot(act, wd_ref[j], preferred_element_type=F32)
        h2_ref[...] = acc

    tile = lambda w: pl.BlockSpec((tm, w), lambda i: (i, 0))
    return _call(
        body, name, (t // tm,),
        [tile(D_MODEL), tile(D_MODEL), _const_spec((1, D_MODEL)), _const_spec((D_MODEL, D_MODEL)),
         _const_spec((1, D_MODEL)), _const_spec((nj, D_MODEL, FF_BLK)), _const_spec((nj, FF_BLK, D_MODEL))],
        [tile(D_MODEL), tile(D_FF), tile(D_MODEL), tile(D_MODEL), tile(D_MODEL)],
        [_sds((t, D_MODEL), F32), _sds((t, D_FF), MX), _sds((t, D_MODEL), F32), _sds((t, D_MODEL), MX),
         _sds((t, D_MODEL), MX)],
        [], [ycat, h0, gmix, w_out, g2, w_up, w_down], riders)


def _ffn_bwd(dh2, act, h1, g2, w_up, w_down, name, riders=()):
    t = h1.shape[0]
    tm = _tile(t, POST_TILE)
    nj = D_FF // FF_BLK

    def body(dh2_ref, a_ref, h1_ref, g2_ref, wu_ref, wd_ref, dh1_ref, du_ref, dg2_ref):
        @pl.when(pl.program_id(0) == 0)
        def _():
            dg2_ref[...] = jnp.zeros_like(dg2_ref)

        dh2 = dh2_ref[...]
        dh2b = dh2.astype(MX)
        dhn = jnp.zeros((tm, D_MODEL), F32)
        for j in range(nj):
            cols = slice(j * FF_BLK, (j + 1) * FF_BLK)
            da = _dot_nt(dh2b, wd_ref[j])
            du = (da * (2.0 * jnp.sqrt(a_ref[:, cols].astype(F32)))).astype(MX)
            du_ref[:, cols] = du
            dhn = dhn + _dot_nt(du, wu_ref[j])
        _, xh, r = _rms_fwd(h1_ref[...], g2_ref[...])
        dx, dg = _rms_bwd(dhn, xh, r, g2_ref[...])
        dg2_ref[...] += dg
        dh1_ref[...] = dh2 + dx

    tile = lambda w: pl.BlockSpec((tm, w), lambda i: (i, 0))
    return _call(
        body, name, (t // tm,),
        [tile(D_MODEL), tile(D_FF), tile(D_MODEL), _const_spec((1, D_MODEL)),
         _const_spec((nj, D_MODEL, FF_BLK)), _const_spec((nj, FF_BLK, D_MODEL))],
        [tile(D_MODEL), tile(D_FF), _acc_spec((1, D_MODEL))],
        [_sds((t, D_MODEL), F32), _sds((t, D_FF), MX), _sds((1, D_MODEL), F32)],
        [], [dh2, act, h1, g2, w_up, w_down], riders)


def _mix_bwd(dh1, ycat, gmix, w_out, name):
    t = dh1.shape[0]
    tm = _tile(t)

    def body(dh1_ref, y_ref, gm_ref, wo_ref, dy_ref, dgm_ref):
        @pl.when(pl.program_id(0) == 0)
        def _():
            dgm_ref[...] = jnp.zeros_like(dgm_ref)

        dym = _dot_nt(dh1_ref[...], wo_ref[...])
        gm = gm_ref[...]
        _, yh, rr = _group_rms_fwd(y_ref[...], gm)
        outs, dgs = [], []
        for (a, b), rg in zip(_GROUPS, rr):
            dxg, dgg = _rms_bwd(dym[:, a:b], yh[:, a:b], rg, gm[:, a:b])
            outs.append(dxg)
            dgs.append(dgg)
        dy_ref[...] = jnp.concatenate(outs, axis=1)
        dgm_ref[...] += jnp.concatenate(dgs, axis=1)

    tile = pl.BlockSpec((tm, D_MODEL), lambda i: (i, 0))
    (dy, dgm), _ = _call(
        body, name, (t // tm,), [tile, tile, _const_spec((1, D_MODEL)), _const_spec((D_MODEL, D_MODEL))],
        [tile, _acc_spec((1, D_MODEL))], [_sds((t, D_MODEL), F32), _sds((1, D_MODEL), F32)],
        [], [dh1, ycat, gmix, w_out])
    return dy, dgm


def _in_bwd(dz, h0, dh1, g1, w_in, name):
    t = h0.shape[0]
    tm = _tile(t)

    def body(dz_ref, h_ref, dh1_ref, g_ref, wg_ref, dh0_ref, dg_ref, w_scr):
        @pl.when(pl.program_id(0) == 0)
        def _():
            dg_ref[...] = jnp.zeros_like(dg_ref)
            _assemble_w_in(wg_ref, w_scr)

        dhn = _dot_nt(dz_ref[...], w_scr[...])
        _, xh, r = _rms_fwd(h_ref[...], g_ref[...])
        dx, dg = _rms_bwd(dhn, xh, r, g_ref[...])
        dg_ref[...] += dg
        dh0_ref[...] = dh1_ref[...] + dx

    tile = lambda w: pl.BlockSpec((tm, w), lambda i: (i, 0))
    (dh0, dg), _ = _call(
        body, name, (t // tm,),
        [tile(IN_W), tile(D_MODEL), tile(D_MODEL), _const_spec((1, D_MODEL)), _const_spec((N_DEV, D_MODEL, IN_SHARD))],
        [tile(D_MODEL), _acc_spec((1, D_MODEL))], [_sds((t, D_MODEL), F32), _sds((1, D_MODEL), F32)],
        [pltpu.VMEM((D_MODEL, IN_W), MX)], [dz, h0, dh1, g1, w_in])
    return dh0, dg


def _loss_head(h, gf, target, name):
    t = h.shape[0]
    tm = _tile(t)

    def body(h_ref, g_ref, t_ref, dh_ref, loss_ref, dg_ref):
        @pl.when(pl.program_id(0) == 0)
        def _():
            loss_ref[...] = jnp.zeros_like(loss_ref)
            dg_ref[...] = jnp.zeros_like(dg_ref)

        g = g_ref[...]
        y, xh, r = _rms_fwd(h_ref[...], g)
        err = y - t_ref[...]
        part = 0.5 * jnp.sum(jnp.mean(err * err, axis=-1, keepdims=True), axis=0, keepdims=True)
        loss_ref[...] += jnp.broadcast_to(part, loss_ref.shape)
        dx, dg = _rms_bwd(err * (1.0 / D_MODEL), xh, r, g)
        dg_ref[...] += dg
        dh_ref[...] = dx

    tile = pl.BlockSpec((tm, D_MODEL), lambda i: (i, 0))
    (dh, loss, dg), _ = _call(
        body, name, (t // tm,), [tile, _const_spec((1, D_MODEL)), tile],
        [tile, _acc_spec((1, 128)), _acc_spec((1, D_MODEL))],
        [_sds((t, D_MODEL), F32), _sds((1, 128), F32), _sds((1, D_MODEL), F32)], [], [h, gf, target])
    return dh, loss, dg


def _dw(x, y, name, split, bm, bn):
    t, m = x.shape
    n = y.shape[1]
    tk = _tile(t)
    nk = t // tk
    if split == "rows":
        assert bn == n
        r, c = m // N_DEV, n
        per = bm // r
        out_block = pl.BlockSpec((per, r, c), lambda a, b, k: (a, 0, 0))
    else:
        assert bm == m
        r, c = m, n // N_DEV
        per = bn // c
        out_block = pl.BlockSpec((per, r, c), lambda a, b, k: (b, 0, 0))

    def body(x_ref, y_ref, o_ref, o16_ref, acc):
        k = pl.program_id(2)

        @pl.when(k == 0)
        def _():
            acc[...] = jnp.zeros_like(acc)

        acc[...] += _dot_tn(x_ref[...], y_ref[...])

        @pl.when(k == nk - 1)
        def _():
            for d in range(per):
                v = acc[d * r:(d + 1) * r, :] if split == "rows" else acc[:, d * c:(d + 1) * c]
                o_ref[d] = v
                o16_ref[d] = v.astype(o16_ref.dtype)

    return pl.pallas_call(
        body, name=name, grid=(m // bm, n // bn, nk),
        in_specs=[pl.BlockSpec((tk, bm), lambda a, b, k: (k, a)), pl.BlockSpec((tk, bn), lambda a, b, k: (k, b))],
        out_specs=[out_block, out_block],
        out_shape=[_sds((N_DEV, r, c), F32), _sds((N_DEV, r, c), WIRE)],
        scratch_shapes=[pltpu.VMEM((bm, bn), F32)],
        compiler_params=pltpu.CompilerParams(dimension_semantics=("arbitrary",) * 3, vmem_limit_bytes=VMEM_LIMIT),
    )(x, y)


def _adamw_math(w, g, m, v):
    m = ADAM_B1 * m + (1.0 - ADAM_B1) * g
    v = ADAM_B2 * v + (1.0 - ADAM_B2) * jnp.square(g)
    m_hat = m / (1.0 - ADAM_B1 ** ADAM_STEP)
    v_hat = v / (1.0 - ADAM_B2 ** ADAM_STEP)
    delta = -ADAM_LR * (m_hat / (jnp.sqrt(v_hat) + ADAM_EPS) + ADAM_WD * w)
    return delta, m, v


def _adamw_shard(g_own, g_recv, dev, w, m, v, name):
    r, c = w.shape
    br = r
    for cand in (256, 128, 112, 64, 56, 32, 16, 8):
        if r % cand == 0:
            br = cand
            break

    def body(dev_ref, go_ref, gr_ref, w_ref, m_ref, v_ref, g_out, d_out, m_out, v_out):
        g = go_ref[0]
        for j in range(N_DEV - 1):
            g = g + gr_ref[j].astype(F32)
        delta, mn, vn = _adamw_math(w_ref[...], g, m_ref[...], v_ref[...])
        g_out[...] = g
        d_out[...] = delta
        m_out[...] = mn
        v_out[...] = vn

    tile = pl.BlockSpec((br, c), lambda i, dev_ref: (i, 0))
    return pl.pallas_call(
        body, name=name,
        grid_spec=pltpu.PrefetchScalarGridSpec(
            num_scalar_prefetch=1, grid=(r // br,),
            in_specs=[pl.BlockSpec((1, br, c), lambda i, dev_ref: (dev_ref[0], i, 0)),
                      pl.BlockSpec((N_DEV - 1, br, c), lambda i, dev_ref: (0, i, 0)),
                      tile, tile, tile],
            out_specs=[tile, tile, tile, tile]),
        out_shape=[_sds((r, c), F32)] * 4,
        compiler_params=pltpu.CompilerParams(dimension_semantics=("arbitrary",), vmem_limit_bytes=VMEM_LIMIT),
    )(dev, g_own, g_recv, w, m, v)


def _adamw_small(parts, ws, ms, vs, name):
    n = len(parts)

    def body(*refs):
        p_refs, w_refs, m_refs, v_refs = (refs[k * n:(k + 1) * n] for k in range(4))
        outs = refs[4 * n:]
        for k in range(n):
            g = p_refs[k][0]
            for d in range(1, N_DEV):
                g = g + p_refs[k][d]
            delta, mn, vn = _adamw_math(w_refs[k][...], g, m_refs[k][...], v_refs[k][...])
            outs[k][...] = g
            outs[n + k][...] = delta
            outs[2 * n + k][...] = mn
            outs[3 * n + k][...] = vn

    shapes = [_sds(w.shape, F32) for w in ws]
    res = pl.pallas_call(body, name=name, out_shape=shapes * 4,
                         compiler_params=pltpu.CompilerParams(vmem_limit_bytes=VMEM_LIMIT))(*parts, *ws, *ms, *vs)
    return res[:n], res[n:2 * n], res[2 * n:3 * n], res[3 * n:]


def _sum_parts(part, name):
    def body(p_ref, o_ref):
        g = p_ref[0]
        for d in range(1, N_DEV):
            g = g + p_ref[d]
        o_ref[...] = g

    return pl.pallas_call(body, name=name, out_shape=_sds(part.shape[1:], F32))(part)


def _block_diag(w):
    out = jnp.zeros((LRU_W, LRU_W), w.dtype)
    for h in range(4):
        out = lax.dynamic_update_slice(out, w[h], (h * 64, h * 64))
    return out


def _unblock_diag(w):
    return jnp.concatenate([w[h * 64:(h + 1) * 64, h * 64:(h + 1) * 64] for h in range(4)], axis=0)


def _layer_params(p, l):
    row = lambda a: a[l].reshape(1, -1)
    sink_rows = jnp.repeat(p["attn_sinks"][l].reshape(4, 2), 2 * BLK, axis=1)
    sink_rows = jnp.concatenate([sink_rows, jnp.zeros((4, 4 * BLK), F32)], axis=0)
    cw = jnp.concatenate([p["conv_dw_w"][l], jnp.zeros((1, CONV_W), F32)], axis=0)
    pv = jnp.concatenate([
        row(p["conv_dw_b"]), row(p["conv_ln_g"]), row(p["conv_ln_b"]), row(p["lru_conv_b"]), row(p["lru_ba"]),
        row(p["lru_bx"]), row(p["lru_lambda"]), jnp.zeros((1, LRU_W), F32), p["lru_conv_w"][l],
        jnp.zeros((4, LRU_W), F32)], axis=0)
    return dict(
        g1=row(p["norm1"]), sink=sink_rows, cw=cw, pv=pv,
        wa=_block_diag(p["lru_wa"][l]).astype(MX), wx=_block_diag(p["lru_wx"][l]).astype(MX),
        gmix=row(p["mix_norm"]), g2=row(p["norm2"]))


_SMALL = ["norm1", "attn_sinks", "conv_dw_w", "conv_dw_b", "conv_ln_g", "conv_ln_b", "lru_conv_w", "lru_conv_b",
          "lru_wa", "lru_ba", "lru_wx", "lru_bx", "lru_lambda", "mix_norm", "norm2"]
_BIG = ["w_in", "w_out", "w_up", "w_down"]
_WEIGHTS = ["norm1", "w_in", "attn_sinks", "conv_dw_w", "conv_dw_b", "conv_ln_g", "conv_ln_b", "lru_conv_w",
            "lru_conv_b", "lru_wa", "lru_ba", "lru_wx", "lru_bx", "lru_lambda", "mix_norm", "w_out", "norm2", "w_up",
            "w_down", "final_norm"]


def kernel(x, norm1, w_in, attn_sinks, conv_dw_w, conv_dw_b, conv_ln_g, conv_ln_b, lru_conv_w, lru_conv_b, lru_wa, lru_ba, lru_wx, lru_bx, lru_lambda, mix_norm, w_out, norm2, w_up, w_down, final_norm, loss_target, m_norm1, m_w_in, m_attn_sinks, m_conv_dw_w, m_conv_dw_b, m_conv_ln_g, m_conv_ln_b, m_lru_conv_w, m_lru_conv_b, m_lru_wa, m_lru_ba, m_lru_wx, m_lru_bx, m_lru_lambda, m_mix_norm, m_w_out, m_norm2, m_w_up, m_w_down, m_final_norm, v_norm1, v_w_in, v_attn_sinks, v_conv_dw_w, v_conv_dw_b, v_conv_ln_g, v_conv_ln_b, v_lru_conv_w, v_lru_conv_b, v_lru_wa, v_lru_ba, v_lru_wx, v_lru_bx, v_lru_lambda, v_mix_norm, v_w_out, v_norm2, v_w_up, v_w_down, v_final_norm):
    w = dict(norm1=norm1, w_in=w_in, attn_sinks=attn_sinks, conv_dw_w=conv_dw_w, conv_dw_b=conv_dw_b,
             conv_ln_g=conv_ln_g, conv_ln_b=conv_ln_b, lru_conv_w=lru_conv_w, lru_conv_b=lru_conv_b, lru_wa=lru_wa,
             lru_ba=lru_ba, lru_wx=lru_wx, lru_bx=lru_bx, lru_lambda=lru_lambda, mix_norm=mix_norm, w_out=w_out,
             norm2=norm2, w_up=w_up, w_down=w_down, final_norm=final_norm)
    m = dict(norm1=m_norm1, w_in=m_w_in, attn_sinks=m_attn_sinks, conv_dw_w=m_conv_dw_w, conv_dw_b=m_conv_dw_b,
             conv_ln_g=m_conv_ln_g, conv_ln_b=m_conv_ln_b, lru_conv_w=m_lru_conv_w, lru_conv_b=m_lru_conv_b,
             lru_wa=m_lru_wa, lru_ba=m_lru_ba, lru_wx=m_lru_wx, lru_bx=m_lru_bx, lru_lambda=m_lru_lambda,
             mix_norm=m_mix_norm, w_out=m_w_out, norm2=m_norm2, w_up=m_w_up, w_down=m_w_down, final_norm=m_final_norm)
    v = dict(norm1=v_norm1, w_in=v_w_in, attn_sinks=v_attn_sinks, conv_dw_w=v_conv_dw_w, conv_dw_b=v_conv_dw_b,
             conv_ln_g=v_conv_ln_g, conv_ln_b=v_conv_ln_b, lru_conv_w=v_lru_conv_w, lru_conv_b=v_lru_conv_b,
             lru_wa=v_lru_wa, lru_ba=v_lru_ba, lru_wx=v_lru_wx, lru_bx=v_lru_bx, lru_lambda=v_lru_lambda,
             mix_norm=v_mix_norm, w_out=v_w_out, norm2=v_norm2, w_up=v_w_up, w_down=v_w_down, final_norm=v_final_norm)
    depth = w_in.shape[0]
    xi, yi, ci = _me()
    dev = (4 * xi + 2 * yi + ci).astype(jnp.int32)
    dev1 = dev.reshape(1)
    wb = {n: w[n].astype(MX) for n in _BIG}
    layer_shards = lambda l: [wb["w_out"][l], wb["w_up"][l], wb["w_down"][l]]

    _, ((g_in0, g_cw, g_lcw),) = _call(None, "gather_first", None, [], [], [], [], [],
                                        [_gather_rider([wb["w_in"][0], conv_dw_w, lru_conv_w])])
    cols = lambda g: jnp.moveaxis(g, 0, -2).reshape(g.shape[1:-1] + (N_DEV * g.shape[-1],))
    p = dict(w)
    p["conv_dw_w"] = cols(g_cw)
    p["lru_conv_w"] = cols(g_lcw)
    lp = [_layer_params(p, l) for l in range(depth)]

    gathered = [dict(w_in=g_in0), dict()]
    saved = []
    h = x[0]
    for l in range(depth):
        q, gw = lp[l], gathered[l]
        z, hn1 = _ln_in(h, q["g1"], gw["w_in"], f"ln_in{l}")
        riders = [_gather_rider(layer_shards(0))] if l == 0 else []
        (ycat, hl, uc), got = _mixer_fwd(z, q["sink"], q["cw"], q["pv"], q["wa"], q["wx"], f"mixer_fwd{l}", riders)
        if l == 0:
            gw["w_out"], gw["w_up"], gw["w_down"] = got[0]
            gw["w_out"] = gw["w_out"].reshape(D_MODEL, D_MODEL)
        riders = [_gather_rider([wb["w_in"][1]] + layer_shards(1))] if l == 0 else []
        (h1, act, h2, ym, hn2), got = _post_fwd(ycat, h, q["gmix"], gw["w_out"], q["g2"], gw["w_up"], gw["w_down"],
                                                f"post_fwd{l}", riders)
        if l == 0:
            nxt = gathered[1]
            nxt["w_in"], nxt["w_out"], nxt["w_up"], nxt["w_down"] = got[0]
            nxt["w_out"] = nxt["w_out"].reshape(D_MODEL, D_MODEL)
        saved.append(dict(h0=h, z=z, hn1=hn1, ycat=ycat, hl=hl, uc=uc, h1=h1, act=act, ym=ym, hn2=hn2))
        h = h2
    dh, loss, dgf = _loss_head(h, final_norm.reshape(1, -1), loss_target[0], "loss_head")

    grads = [None] * depth
    big = {n: [None] * depth for n in _BIG}
    pending = []

    def send_pending():
        riders = [_scatter_rider([item[3] for item in pending])] if pending else []
        return riders, list(pending)

    def record(sent, got):
        for item, recv in zip(sent, got[0] if sent else []):
            big[item[0]][item[1]] = (item[2], recv)
        del pending[:len(sent)]

    for l in reversed(range(depth)):
        q, s, gw = lp[l], saved[l], gathered[l]
        riders, sent = send_pending()
        (dh1, du, dg2), got = _ffn_bwd(dh, s["act"], s["h1"], q["g2"], gw["w_up"], gw["w_down"], f"ffn_bwd{l}", riders)
        record(sent, got)
        dycat, dgm = _mix_bwd(dh1, s["ycat"], q["gmix"], gw["w_out"], f"mix_bwd{l}")
        pending.append(("w_down", l) + tuple(_dw(s["act"], dh, f"dw_down{l}", "rows", 2048, D_MODEL)))
        pending.append(("w_up", l) + tuple(_dw(s["hn2"], du, f"dw_up{l}", "cols", D_MODEL, 2048)))
        pending.append(("w_out", l) + tuple(_dw(s["ym"], dh1, f"dw_out{l}", "rows", D_MODEL, D_MODEL)))
        riders, sent = send_pending()
        (dz, dsink, dcw, dpv, dwa, dwx), got = _mixer_bwd(dycat, s["z"], s["ycat"], s["hl"], s["uc"], q["sink"],
                                                          q["cw"], q["pv"], q["wa"], q["wx"], f"mixer_bwd{l}", riders)
        record(sent, got)
        pending.append(("w_in", l) + tuple(_dw(s["hn1"], dz, f"dw_in{l}", "cols", D_MODEL, IN_W)))
        dh, dg1 = _in_bwd(dz, s["h0"], dh1, q["g1"], gw["w_in"], f"in_bwd{l}")
        grads[l] = dict(
            norm1=dg1[0], attn_sinks=jnp.stack([dsink[0:4, 0], dsink[0:4, 2 * BLK]], axis=1).reshape(8),
            conv_dw_w=dcw[0:CONV_K], conv_dw_b=dpv[R_CONV_B], conv_ln_g=dpv[R_LN_G], conv_ln_b=dpv[R_LN_B],
            lru_conv_w=dpv[R_LCW:R_LCW + LRU_K], lru_conv_b=dpv[R_LCONV_B], lru_wa=_unblock_diag(dwa),
            lru_ba=dpv[R_BA].reshape(4, 64), lru_wx=_unblock_diag(dwx), lru_bx=dpv[R_BX].reshape(4, 64),
            lru_lambda=dpv[R_LAM], mix_norm=dgm[0], norm2=dg2[0])

    small = [jnp.stack([grads[l][n] for l in range(depth)]) for n in _SMALL] + [dgf, loss]
    riders, sent = send_pending()
    _, got = _call(None, "tail_exchange", None, [], [], [], [], [], riders + [_bcast_rider(small)])
    record(sent, got)
    parts_all = got[1]

    out = {}
    for n in _BIG:
        res = [_adamw_shard(big[n][l][0], big[n][l][1], dev1, w[n][l], m[n][l], v[n][l], f"adamw_{n}{l}")
               for l in range(depth)]
        out[n] = [jnp.stack([res[l][j] for l in range(depth)]) for j in range(4)]
    shard = lambda a: lax.dynamic_slice_in_dim(a, dev * (a.shape[-1] // N_DEV), a.shape[-1] // N_DEV, axis=a.ndim - 1)
    flat = {"lru_wa": (depth, LRU_W, 64), "lru_wx": (depth, LRU_W, 64), "final_norm": (1, D_MODEL)}
    parts, ws, ms, vs = [], [], [], []
    for n, g in zip(_SMALL + ["final_norm"], parts_all[:-1]):
        parts.append(shard(g) if n in ("conv_dw_w", "lru_conv_w") else g)
        shp = flat.get(n, w[n].shape)
        ws.append(w[n].reshape(shp))
        ms.append(m[n].reshape(shp))
        vs.append(v[n].reshape(shp))
    sg, sd, sm, sv = _adamw_small(parts, ws, ms, vs, "adamw_small")
    for j, n in enumerate(_SMALL + ["final_norm"]):
        out[n] = [a.reshape(w[n].shape) for a in (sg[j], sd[j], sm[j], sv[j])]
    loss_total = _sum_parts(parts_all[-1], "loss_sum")[0, 0]

    result = [loss_total, dh[None]]
    for j in range(4):
        result += [out[n][j] for n in _WEIGHTS]
    return tuple(result)
```

```python
import types

import jax
import jax.numpy as jnp
from jax import lax
from jax.experimental import pallas as pl
from jax.experimental.pallas import tpu as pltpu

F32 = jnp.float32
MX = jnp.bfloat16
WIRE = jnp.bfloat16

D_MODEL = 1024
HEAD_DIM = 64
ATTN_W = 512
KV_W = 128
BLK = 128
CONV_W = 256
CONV_K = 31
LRU_W = 256
LRU_K = 4
LRU_C = 8.0
IN_W = 1792
D_FF = 4096
FF_BLK = 512
N_DEV = 8
IN_SHARD = IN_W // N_DEV
RMS_EPS = 1e-6
LN_EPS = 1e-5
MASK_VALUE = -1e30
SCALE = HEAD_DIM ** -0.5
CONV_HALO = 32
LRU_HALO = 8
CONV_CHUNK = 64
POST_TILE = 512
DW_TILE = 1024
Q0, K0, V0, CV0, CG0, RX0, RG0 = 0, 512, 640, 768, 1024, 1280, 1536
R_CONV_B, R_LN_G, R_LN_B, R_LCONV_B, R_BA, R_BX, R_LAM, R_LCW = 0, 1, 2, 3, 4, 5, 6, 8

ADAM_LR, ADAM_B1, ADAM_B2, ADAM_EPS, ADAM_WD, ADAM_STEP = 0.001, 0.9, 0.999, 1e-08, 0.01, 10

VMEM_LIMIT = 56 * 1024 * 1024
MESH = pl.DeviceIdType.MESH
ANY = pl.BlockSpec(memory_space=pl.ANY)


def _tile(t, cap=512):
    return min(cap, t)


def _dot(a, b):
    return jnp.dot(a.astype(MX), b.astype(MX), preferred_element_type=F32)


def _dot_nt(a, b):
    return lax.dot_general(a.astype(MX), b.astype(MX), (((1,), (1,)), ((), ())), preferred_element_type=F32)


def _dot_tn(a, b):
    return lax.dot_general(a.astype(MX), b.astype(MX), (((0,), (0,)), ((), ())), preferred_element_type=F32)


def _const_spec(shape):
    nd = len(shape)
    return pl.BlockSpec(shape, lambda *_: (0,) * nd, pipeline_mode=pl.Buffered(1))


def _acc_spec(shape):
    nd = len(shape)
    return pl.BlockSpec(shape, lambda *_: (0,) * nd)


def _sds(shape, dtype):
    return jax.ShapeDtypeStruct(shape, dtype)


def _sigmoid(x):
    return jax.nn.sigmoid(x)


def _rms_fwd(x, g):
    r = lax.rsqrt(jnp.mean(x * x, axis=-1, keepdims=True) + RMS_EPS)
    xh = x * r
    return xh * g, xh, r


def _rms_bwd(dy, xh, r, g):
    t = dy * g
    dx = r * (t - xh * jnp.mean(t * xh, axis=-1, keepdims=True))
    return dx, jnp.sum(dy * xh, axis=0, keepdims=True)


_GROUPS = ((0, 512), (512, 768), (768, 1024))


def _group_rms_fwd(y, g):
    parts = [_rms_fwd(y[:, a:b], g[:, a:b]) for a, b in _GROUPS]
    return (jnp.concatenate([p[0] for p in parts], axis=1),
            jnp.concatenate([p[1] for p in parts], axis=1),
            [p[2] for p in parts])


def _gelu(x):
    c = 0.7978845608028654
    u = c * (x + 0.044715 * x * x * x)
    th = jnp.tanh(u)
    val = 0.5 * x * (1.0 + th)
    grad = 0.5 * (1.0 + th) + 0.5 * x * (1.0 - th * th) * c * (1.0 + 3.0 * 0.044715 * x * x)
    return val, grad


def _neg_expm1(x):
    series = -x * (1.0 + x * (0.5 + x * (1.0 / 6.0 + x * (1.0 / 24.0))))
    return jnp.where(x > -0.02, series, 1.0 - jnp.exp(x))


def _me():
    return lax.axis_index("x"), lax.axis_index("y"), lax.axis_index("c")


def _gather_rider(arrays):
    arrays = list(arrays)
    n = len(arrays)

    def plan(ins, outs, sems):
        ssem, rsem, lsem = sems
        x, y, c = _me()
        chips = [(1 - x, y), (x, 1 - y), (1 - x, 1 - y)]

        def copy(a, k, block, to, own=False):
            dst = outs[a].at[4 * block[0] + 2 * block[1] + block[2]]
            return pltpu.make_async_remote_copy(
                src_ref=ins[a] if own else dst, dst_ref=dst, send_sem=ssem.at[7 * a + k],
                recv_sem=rsem.at[7 * a + k], device_id=to, device_id_type=MESH)

        return x, y, c, chips, copy, lsem

    def start(ins, outs, sems):
        x, y, c, chips, copy, lsem = plan(ins, outs, sems)
        for a in range(n):
            pltpu.make_async_copy(ins[a], outs[a].at[4 * x + 2 * y + c], lsem.at[a]).start()
            copy(a, 0, (x, y, c), (x, y, 1 - c), own=True).start()
            for j, chip in enumerate(chips):
                copy(a, 1 + j, (x, y, c), (*chip, c), own=True).start()

    def mid(ins, outs, sems):
        x, y, c, chips, copy, _ = plan(ins, outs, sems)
        for a in range(n):
            for j, chip in enumerate(chips):
                copy(a, 1 + j, (*chip, c), (x, y, c)).wait_recv()
                copy(a, 4 + j, (*chip, c), (x, y, 1 - c)).start()

    def finish(ins, outs, sems):
        x, y, c, chips, copy, lsem = plan(ins, outs, sems)
        for a in range(n):
            copy(a, 0, (x, y, 1 - c), (x, y, c)).wait_recv()
            for j, chip in enumerate(chips):
                copy(a, 4 + j, (*chip, 1 - c), (x, y, c)).wait_recv()
        for a in range(n):
            copy(a, 0, (x, y, c), (x, y, 1 - c), own=True).wait_send()
            for j, chip in enumerate(chips):
                copy(a, 1 + j, (x, y, c), (*chip, c), own=True).wait_send()
                copy(a, 4 + j, (*chip, c), (x, y, 1 - c)).wait_send()
            pltpu.make_async_copy(ins[a], outs[a].at[4 * x + 2 * y + c], lsem.at[a]).wait()

    return types.SimpleNamespace(
        arrays=arrays, out_shape=[_sds((N_DEV,) + a.shape, a.dtype) for a in arrays],
        scratch=[pltpu.SemaphoreType.DMA((7 * n,)), pltpu.SemaphoreType.DMA((7 * n,)), pltpu.SemaphoreType.DMA((n,))],
        start=start, mid=mid, finish=finish)


def _bcast_rider(arrays):
    arrays = list(arrays)
    n = len(arrays)

    def copies(ins, outs, sems, landing):
        ssem, rsem, lsem = sems
        x, y, c = _me()
        out = []
        for a in range(n):
            out.append(pltpu.make_async_copy(ins[a], outs[a].at[4 * x + 2 * y + c], lsem.at[a]))
            for f in range(1, N_DEV):
                px = 1 - x if f & 4 else x
                py = 1 - y if f & 2 else y
                pc = 1 - c if f & 1 else c
                slot = 4 * px + 2 * py + pc if landing else 4 * x + 2 * y + c
                out.append(pltpu.make_async_remote_copy(
                    src_ref=ins[a], dst_ref=outs[a].at[slot], send_sem=ssem.at[7 * a + f - 1],
                    recv_sem=rsem.at[7 * a + f - 1], device_id=(px, py, pc), device_id_type=MESH))
        return out

    def start(ins, outs, sems):
        for cp in copies(ins, outs, sems, landing=False):
            cp.start()

    def finish(ins, outs, sems):
        for cp in copies(ins, outs, sems, landing=True):
            cp.wait()

    return types.SimpleNamespace(
        arrays=arrays, out_shape=[_sds((N_DEV,) + a.shape, a.dtype) for a in arrays],
        scratch=[pltpu.SemaphoreType.DMA((7 * n,)), pltpu.SemaphoreType.DMA((7 * n,)), pltpu.SemaphoreType.DMA((n,))],
        start=start, mid=None, finish=finish)


def _scatter_rider(arrays):
    arrays = list(arrays)
    n = len(arrays)

    def copies(ins, outs, sems):
        ssem, rsem = sems
        x, y, c = _me()
        out = []
        for a in range(n):
            for f in range(1, N_DEV):
                px = 1 - x if f & 4 else x
                py = 1 - y if f & 2 else y
                pc = 1 - c if f & 1 else c
                out.append(pltpu.make_async_remote_copy(
                    src_ref=ins[a].at[4 * px + 2 * py + pc], dst_ref=outs[a].at[f - 1], send_sem=ssem.at[7 * a + f - 1],
                    recv_sem=rsem.at[7 * a + f - 1], device_id=(px, py, pc), device_id_type=MESH))
        return out

    def start(ins, outs, sems):
        for cp in copies(ins, outs, sems):
            cp.start()

    def finish(ins, outs, sems):
        for cp in copies(ins, outs, sems):
            cp.wait()

    return types.SimpleNamespace(
        arrays=arrays, out_shape=[_sds((N_DEV - 1,) + a.shape[1:], a.dtype) for a in arrays],
        scratch=[pltpu.SemaphoreType.DMA((7 * n,)), pltpu.SemaphoreType.DMA((7 * n,))],
        start=start, mid=None, finish=finish)


def _call(body, name, grid, in_specs, out_specs, out_shape, scratch, operands, riders=()):
    n_in, n_out, n_scr = len(operands), len(out_shape), len(scratch)
    nsteps = grid[0] if grid else 1
    sizes = [(len(r.arrays), len(r.out_shape), len(r.scratch)) for r in riders]

    def wrapped(*refs):
        pos = n_in
        r_ins = []
        for ri, _, _ in sizes:
            r_ins.append(refs[pos:pos + ri])
            pos += ri
        outs = refs[pos:pos + n_out]
        pos += n_out
        r_outs = []
        for _, ro, _ in sizes:
            r_outs.append(refs[pos:pos + ro])
            pos += ro
        scr = refs[pos:pos + n_scr]
        pos += n_scr
        r_sems = []
        for _, _, rs in sizes:
            r_sems.append(refs[pos:pos + rs])
            pos += rs
        step = pl.program_id(0) if grid else 0

        def at(s, fn):
            if grid:
                pl.when(step == s)(fn)
            else:
                fn()

        for r, a, b, c in zip(riders, r_ins, r_outs, r_sems):
            at(0, lambda r=r, a=a, b=b, c=c: r.start(a, b, c))
        for r, a, b, c in zip(riders, r_ins, r_outs, r_sems):
            if r.mid is not None:
                at((3 * nsteps) // 4, lambda r=r, a=a, b=b, c=c: r.mid(a, b, c))
        if body is not None:
            body(*refs[:n_in], *outs, *scr)
        for r, a, b, c in zip(riders, r_ins, r_outs, r_sems):
            at(nsteps - 1, lambda r=r, a=a, b=b, c=c: r.finish(a, b, c))

    r_arrays = [a for r in riders for a in r.arrays]
    r_shapes = [s for r in riders for s in r.out_shape]
    kwargs = {}
    if grid:
        kwargs = dict(grid=grid, compiler_params=pltpu.CompilerParams(
            dimension_semantics=("arbitrary",) * len(grid), vmem_limit_bytes=VMEM_LIMIT))
    res = pl.pallas_call(
        wrapped, name=name,
        in_specs=list(in_specs) + [ANY] * len(r_arrays),
        out_specs=list(out_specs) + [ANY] * len(r_shapes),
        out_shape=list(out_shape) + r_shapes,
        scratch_shapes=list(scratch) + [s for r in riders for s in r.scratch],
        **kwargs,
    )(*operands, *r_arrays)
    host, rest = res[:n_out], res[n_out:]
    r_res = []
    for _, ro, _ in sizes:
        r_res.append(rest[:ro])
        rest = rest[ro:]
    return host, r_res


def _assemble_w_in(wg_ref, w_scr):
    for j in range(N_DEV):
        w_scr[:, j * IN_SHARD:(j + 1) * IN_SHARD] = wg_ref[j]


def _ln_in(h, g1, w_in, name):
    t = h.shape[0]
    tm = _tile(t)

    def body(h_ref, g_ref, wg_ref, z_ref, hn_ref, w_scr):
        @pl.when(pl.program_id(0) == 0)
        def _():
            _assemble_w_in(wg_ref, w_scr)

        y, _, _ = _rms_fwd(h_ref[...], g_ref[...])
        hn = y.astype(MX)
        hn_ref[...] = hn
        z_ref[...] = jnp.dot(hn, w_scr[...], preferred_element_type=F32)

    tile = lambda w: pl.BlockSpec((tm, w), lambda i: (i, 0))
    (z, hn), _ = _call(
        body, name, (t // tm,),
        [tile(D_MODEL), _const_spec((1, D_MODEL)), _const_spec((N_DEV, D_MODEL, IN_SHARD))],
        [tile(IN_W), tile(D_MODEL)], [_sds((t, IN_W), F32), _sds((t, D_MODEL), MX)],
        [pltpu.VMEM((D_MODEL, IN_W), MX)], [h, g1, w_in])
    return z, hn


def _band2(kb, g):
    lo = lax.broadcasted_iota(jnp.int32, kb.shape, 1) < HEAD_DIM
    kr = pltpu.roll(kb, HEAD_DIM, 1)
    if g == 0:
        top, bot = jnp.where(lo, kb, 0.0), jnp.where(lo, 0.0, kr)
    else:
        top, bot = jnp.where(lo, kr, 0.0), jnp.where(lo, 0.0, kb)
    return jnp.concatenate([top, bot], axis=0)


def _attn_block(z_ref, zh_ref, sink_ref, b, first):
    rows = slice(b * BLK, (b + 1) * BLK)
    prev = zh_ref if b == 0 else z_ref
    prow = slice(0, BLK) if b == 0 else slice((b - 1) * BLK, b * BLK)
    kb = jnp.concatenate([prev[prow, K0:K0 + KV_W], z_ref[rows, K0:K0 + KV_W]], axis=0)
    vb = jnp.concatenate([prev[prow, V0:V0 + KV_W], z_ref[rows, V0:V0 + KV_W]], axis=0)
    k2 = [_band2(kb, g) for g in range(2)]
    v2 = [_band2(vb, g) for g in range(2)]
    q2 = [jnp.concatenate([z_ref[rows, (2 * g) * BLK:(2 * g + 1) * BLK], z_ref[rows, (2 * g + 1) * BLK:(2 * g + 2) * BLK]],
                          axis=0) for g in range(2)]
    rr = lax.broadcasted_iota(jnp.int32, (4 * BLK, 2 * BLK), 0) & (BLK - 1)
    cc = lax.broadcasted_iota(jnp.int32, (4 * BLK, 2 * BLK), 1)
    first_block = jnp.logical_and(first, b == 0).astype(jnp.int32)
    mask = jnp.logical_and(jnp.logical_and(cc > rr, cc <= rr + BLK), cc >= BLK * first_block)
    s = jnp.concatenate([_dot_nt(q2[g], k2[g]) for g in range(2)], axis=0) * SCALE
    w = 2 * BLK
    out, psink = [], []
    for hh in range(2):
        sh = jnp.where(mask, s[:, hh * w:(hh + 1) * w], MASK_VALUE)
        sk = jnp.concatenate([jnp.broadcast_to(sink_ref[p:p + 1, hh * w:hh * w + 1], (BLK, 1)) for p in range(4)], axis=0)
        m = jnp.maximum(jnp.max(sh, axis=1, keepdims=True), sk)
        p = jnp.exp(sh - m)
        es = jnp.exp(sk - m)
        inv = 1.0 / (jnp.sum(p, axis=1, keepdims=True) + es)
        out.append(p * inv)
        psink.append(es * inv)
    return q2, k2, v2, jnp.concatenate(out, axis=1), psink


def _scan_fwd(a, b, tm):
    rows = lax.broadcasted_iota(jnp.int32, a.shape, 0)
    d = 1
    while d < tm:
        keep = rows >= d
        a_sh = jnp.where(keep, pltpu.roll(a, d, 0), 1.0)
        b_sh = jnp.where(keep, pltpu.roll(b, d, 0), 0.0)
        b = a * b_sh + b
        a = a * a_sh
        d *= 2
    return a, b


def _scan_bwd(c, b, tm):
    rows = lax.broadcasted_iota(jnp.int32, c.shape, 0)
    d = 1
    while d < tm:
        keep = rows < tm - d
        c_sh = jnp.where(keep, pltpu.roll(c, tm - d, 0), 1.0)
        b_sh = jnp.where(keep, pltpu.roll(b, tm - d, 0), 0.0)
        b = c * b_sh + b
        c = c * c_sh
        d *= 2
    return b


def _shifted_copies(ext, shifts, tm):
    rows = tm + CONV_HALO - 8
    for r in range(1, 8):
        shifts[r - 1, 0:rows, :] = ext[pl.ds(r, rows), :]


def _tap(ext, shifts, off, r0, n):
    a, r = divmod(off, 8)
    lo = 8 * a + r0
    if r == 0:
        return ext[lo:lo + n, :]
    return shifts[r - 1, lo:lo + n, :]


def _glu_fill(z_ref, zh_ref, uext, ush, first, tm, sg_out=None):
    cv = z_ref[:, CV0:CV0 + CONV_W]
    sg = _sigmoid(z_ref[:, CG0:CG0 + CONV_W])
    if sg_out is not None:
        sg_out[...] = sg
    hrow = BLK - CONV_HALO
    uh = zh_ref[hrow:BLK, CV0:CV0 + CONV_W] * _sigmoid(zh_ref[hrow:BLK, CG0:CG0 + CONV_W])
    uext[0:CONV_HALO, :] = jnp.where(first, 0.0, uh)
    uext[CONV_HALO:CONV_HALO + tm, :] = cv * sg
    _shifted_copies(uext, ush, tm)


def _conv_taps(cw_ref, pv_ref, uext, ush, out_ref, tm):
    for r0 in range(0, tm, CONV_CHUNK):
        acc = jnp.broadcast_to(pv_ref[R_CONV_B:R_CONV_B + 1, :], (CONV_CHUNK, CONV_W))
        for k in range(CONV_K):
            acc = acc + cw_ref[k:k + 1, :] * _tap(uext, ush, CONV_HALO - (CONV_K - 1) + k, r0, CONV_CHUNK)
        out_ref[r0:r0 + CONV_CHUNK, :] = acc


def _ln_silu(uc, pv_ref):
    mu = jnp.mean(uc, axis=-1, keepdims=True)
    xc = uc - mu
    rs = lax.rsqrt(jnp.mean(xc * xc, axis=-1, keepdims=True) + LN_EPS)
    xh = xc * rs
    ln = xh * pv_ref[R_LN_G:R_LN_G + 1, :] + pv_ref[R_LN_B:R_LN_B + 1, :]
    sg = _sigmoid(ln)
    return xh, rs, ln, sg


def _lru_gates(z_ref, zh_ref, pv_ref, wa_ref, wx_ref, rxext, first, tm):
    rxext[0:LRU_HALO, :] = jnp.where(first, 0.0, zh_ref[BLK - LRU_HALO:BLK, RX0:RX0 + LRU_W])
    rxext[LRU_HALO:LRU_HALO + tm, :] = z_ref[:, RX0:RX0 + LRU_W]
    xc = jnp.broadcast_to(pv_ref[R_LCONV_B:R_LCONV_B + 1, :], (tm, LRU_W))
    for k in range(LRU_K):
        xc = xc + pv_ref[R_LCW + k:R_LCW + k + 1, :] * rxext[pl.ds(LRU_HALO - (LRU_K - 1) + k, tm), :]
    r = _sigmoid(_dot(xc, wa_ref[...]) + pv_ref[R_BA:R_BA + 1, :])
    ig = _sigmoid(_dot(xc, wx_ref[...]) + pv_ref[R_BX:R_BX + 1, :])
    lam = pv_ref[R_LAM:R_LAM + 1, :]
    sp = jnp.log1p(jnp.exp(-lam))
    la = (-LRU_C * r) * sp
    a = jnp.exp(la)
    mult = jnp.sqrt(_neg_expm1(2.0 * la))
    return xc, r, ig, sp, la, a, mult


def _mixer_in_specs(tm, tile_of):
    hb = tm // BLK
    return [
        pl.BlockSpec((tm, IN_W), lambda i: (tile_of(i), 0)),
        pl.BlockSpec((BLK, IN_W), lambda i: (jnp.maximum(tile_of(i) * hb - 1, 0), 0)),
        _const_spec((8, 4 * BLK)),
        _const_spec((32, CONV_W)),
        _const_spec((16, CONV_W)),
        _const_spec((LRU_W, LRU_W)),
        _const_spec((LRU_W, LRU_W)),
    ]


def _mixer_fwd(z, sink, cw, pv, wa, wx, name, riders=()):
    t = z.shape[0]
    tm = _tile(t)
    nb = tm // BLK

    def body(z_ref, zh_ref, sink_ref, cw_ref, pv_ref, wa_ref, wx_ref, y_ref, hl_ref, uc_ref, uext, ush, rxext, hcar):
        i = pl.program_id(0)
        first = i == 0

        @pl.when(first)
        def _():
            hcar[...] = jnp.zeros_like(hcar)

        for b in range(nb):
            rows = slice(b * BLK, (b + 1) * BLK)
            _, _, v2, prob, _ = _attn_block(z_ref, zh_ref, sink_ref, b, first)
            for g in range(2):
                o = _dot(prob[2 * g * BLK:(2 * g + 2) * BLK], v2[g])
                y_ref[rows, (2 * g) * BLK:(2 * g + 1) * BLK] = o[0:BLK]
                y_ref[rows, (2 * g + 1) * BLK:(2 * g + 2) * BLK] = o[BLK:2 * BLK]
        _glu_fill(z_ref, zh_ref, uext, ush, first, tm)
        _conv_taps(cw_ref, pv_ref, uext, ush, uc_ref, tm)
        _, _, ln, sg = _ln_silu(uc_ref[...], pv_ref)
        y_ref[:, ATTN_W:ATTN_W + CONV_W] = ln * sg
        xc, _, ig, _, _, a, mult = _lru_gates(z_ref, zh_ref, pv_ref, wa_ref, wx_ref, rxext, first, tm)
        acum, h = _scan_fwd(a, mult * (ig * xc), tm)
        h = h + acum * hcar[0:1, :]
        hl_ref[...] = h
        hcar[0:1, :] = h[tm - 1:tm, :]
        gl, _ = _gelu(z_ref[:, RG0:RG0 + LRU_W])
        y_ref[:, ATTN_W + CONV_W:ATTN_W + CONV_W + LRU_W] = h * gl

    tile = lambda w: pl.BlockSpec((tm, w), lambda i: (i, 0))
    return _call(
        body, name, (t // tm,), _mixer_in_specs(tm, lambda i: i),
        [tile(D_MODEL), tile(LRU_W), tile(CONV_W)],
        [_sds((t, D_MODEL), F32), _sds((t, LRU_W), F32), _sds((t, CONV_W), F32)],
        [pltpu.VMEM((tm + CONV_HALO, CONV_W), F32), pltpu.VMEM((7, tm + CONV_HALO - 8, CONV_W), F32),
         pltpu.VMEM((tm + LRU_HALO, LRU_W), F32), pltpu.VMEM((8, LRU_W), F32)],
        [z, z, sink, cw, pv, wa, wx], riders)


def _mixer_bwd(dy, z, ycat, hl, uc, sink, cw, pv, wa, wx, name, riders=()):
    t = z.shape[0]
    tm = _tile(t)
    nt = t // tm
    nb = tm // BLK
    rev = lambda i: nt - 1 - i

    def body(dy_ref, z_ref, zh_ref, sink_ref, cw_ref, pv_ref, wa_ref, wx_ref, y_ref, hl_ref, hlh_ref, uc_ref,
             dz_ref, dsink_ref, dcw_ref, dpv_ref, dwa_ref, dwx_ref,
             uext, ush, sgs, rxext, dkext, dvext, ducext, dsh, dcw8, dxcext, kcar, vcar, uccar, xccar, gcar):
        i = pl.program_id(0)
        first = i == nt - 1

        @pl.when(i == 0)
        def _():
            for car in (kcar, vcar, uccar, xccar, gcar, dcw8):
                car[...] = jnp.zeros_like(car)
            dsink_ref[...] = jnp.zeros_like(dsink_ref)
            dpv_ref[...] = jnp.zeros_like(dpv_ref)
            dwa_ref[...] = jnp.zeros_like(dwa_ref)
            dwx_ref[...] = jnp.zeros_like(dwx_ref)

        def addrow(r, val):
            dpv_ref[r:r + 1, :] += jnp.sum(val, axis=0, keepdims=True)

        dkext[:, 0:tm] = jnp.zeros((KV_W, tm), F32)
        dvext[:, 0:tm] = jnp.zeros((KV_W, tm), F32)
        dkext[:, tm:tm + BLK] = kcar[...]
        dvext[:, tm:tm + BLK] = vcar[...]
        lane512 = lax.broadcasted_iota(jnp.int32, (1, 4 * BLK), 1) < 2 * BLK
        lo = lax.broadcasted_iota(jnp.int32, (4 * BLK, BLK), 1) < HEAD_DIM
        hd, w2 = HEAD_DIM, 2 * BLK
        for b in range(nb):
            rows = slice(b * BLK, (b + 1) * BLK)
            band = slice(b * BLK, (b + 2) * BLK)
            q2, k2, v2, prob, psink = _attn_block(z_ref, zh_ref, sink_ref, b, first)
            stack = lambda ref: jnp.concatenate([ref[rows, p * BLK:(p + 1) * BLK] for p in range(4)], axis=0)
            do4 = stack(dy_ref)
            dlt = do4 * stack(y_ref)
            d0 = jnp.sum(jnp.where(lo, dlt, 0.0), axis=1, keepdims=True)
            d1 = jnp.sum(jnp.where(lo, 0.0, dlt), axis=1, keepdims=True)
            dp = jnp.concatenate([_dot_nt(do4[g * w2:(g + 1) * w2], v2[g]) for g in range(2)], axis=0)
            dl = jnp.concatenate([jnp.broadcast_to(d0, (4 * BLK, w2)), jnp.broadcast_to(d1, (4 * BLK, w2))], axis=1)
            draw = (prob * (dp - dl)) * SCALE
            e0, e1 = psink[0] * d0, psink[1] * d1
            for p in range(4):
                prs = slice(p * BLK, (p + 1) * BLK)
                s0 = jnp.sum(e0[prs], axis=0, keepdims=True)
                s1 = jnp.sum(e1[prs], axis=0, keepdims=True)
                dsink_ref[p:p + 1, :] += -jnp.where(lane512, s0, s1)
            for g in range(2):
                grs = slice(g * w2, (g + 1) * w2)
                dq = _dot(draw[grs], k2[g])
                dz_ref[rows, (2 * g) * BLK:(2 * g + 1) * BLK] = dq[0:BLK].astype(dz_ref.dtype)
                dz_ref[rows, (2 * g + 1) * BLK:(2 * g + 2) * BLK] = dq[BLK:2 * BLK].astype(dz_ref.dtype)
                tk = _dot_tn(q2[g], draw[grs])
                tv = _dot_tn(do4[grs], prob[grs])
                dkext[g * hd:(g + 1) * hd, band] += tk[0:hd, 0:w2] + tk[hd:2 * hd, w2:2 * w2]
                dvext[g * hd:(g + 1) * hd, band] += tv[0:hd, 0:w2] + tv[hd:2 * hd, w2:2 * w2]
        dz_ref[:, K0:K0 + KV_W] = jnp.transpose(dkext[:, BLK:BLK + tm]).astype(dz_ref.dtype)
        dz_ref[:, V0:V0 + KV_W] = jnp.transpose(dvext[:, BLK:BLK + tm]).astype(dz_ref.dtype)
        kcar[...] = dkext[:, 0:BLK]
        vcar[...] = dvext[:, 0:BLK]

        _glu_fill(z_ref, zh_ref, uext, ush, first, tm, sg_out=sgs)
        xh, rs, ln, sg = _ln_silu(uc_ref[...], pv_ref)
        dln = dy_ref[:, ATTN_W:ATTN_W + CONV_W] * (sg * (1.0 + ln * (1.0 - sg)))
        addrow(R_LN_G, dln * xh)
        addrow(R_LN_B, dln)
        dxh = dln * pv_ref[R_LN_G:R_LN_G + 1, :]
        duc = rs * (dxh - jnp.mean(dxh, axis=-1, keepdims=True) - xh * jnp.mean(dxh * xh, axis=-1, keepdims=True))
        addrow(R_CONV_B, duc)
        ducext[0:tm, :] = duc
        ducext[tm:tm + CONV_HALO, :] = uccar[...]
        uccar[...] = duc[0:CONV_HALO, :]
        _shifted_copies(ducext, dsh, tm)
        for r0 in range(0, tm, CONV_CHUNK):
            crow = slice(r0, r0 + CONV_CHUNK)
            duc_c = ducext[crow, :]
            du = jnp.zeros((CONV_CHUNK, CONV_W), F32)
            for k in range(CONV_K):
                prod = duc_c * _tap(uext, ush, CONV_HALO - (CONV_K - 1) + k, r0, CONV_CHUNK)
                part = prod[0:8]
                for s in range(8, CONV_CHUNK, 8):
                    part = part + prod[s:s + 8]
                dcw8[k] += part
                du = du + cw_ref[k:k + 1, :] * _tap(ducext, dsh, CONV_K - 1 - k, r0, CONV_CHUNK)
            sgc = sgs[crow, :]
            dz_ref[crow, CV0:CV0 + CONV_W] = (du * sgc).astype(dz_ref.dtype)
            u_c = uext[CONV_HALO + r0:CONV_HALO + r0 + CONV_CHUNK, :]
            dz_ref[crow, CG0:CG0 + CONV_W] = (du * u_c * (1.0 - sgc)).astype(dz_ref.dtype)

        @pl.when(i == nt - 1)
        def _():
            dcw_ref[...] = jnp.sum(dcw8[...], axis=1)

        xc, r, ig, sp, la, a, mult = _lru_gates(z_ref, zh_ref, pv_ref, wa_ref, wx_ref, rxext, first, tm)
        h = hl_ref[...]
        rowi = lax.broadcasted_iota(jnp.int32, (tm, LRU_W), 0)
        hlast = jnp.where(first, 0.0, hlh_ref[7:8, :])
        hprev = jnp.where(rowi == 0, hlast, pltpu.roll(h, 1, 0))
        dyl = dy_ref[:, ATTN_W + CONV_W:ATTN_W + CONV_W + LRU_W]
        gl, dgl = _gelu(z_ref[:, RG0:RG0 + LRU_W])
        dz_ref[:, RG0:RG0 + LRU_W] = (dyl * h * dgl).astype(dz_ref.dtype)
        dh = dyl * gl + jnp.where(rowi == tm - 1, gcar[0:1, :], 0.0)
        c = jnp.where(rowi == tm - 1, 0.0, pltpu.roll(a, tm - 1, 0))
        gg = _scan_bwd(c, dh, tm)
        gcar[0:1, :] = a[0:1, :] * gg[0:1, :]
        dmult = gg * (ig * xc)
        dig = gg * mult * xc
        dxc = gg * mult * ig
        dla = gg * hprev * a - dmult * a * a / mult
        dr = dla * (-LRU_C * sp)
        lam = pv_ref[R_LAM:R_LAM + 1, :]
        dpv_ref[R_LAM:R_LAM + 1, :] += jnp.sum(dla * (-LRU_C * r), axis=0, keepdims=True) * (-_sigmoid(-lam))
        dpa = dr * r * (1.0 - r)
        dpx = dig * ig * (1.0 - ig)
        addrow(R_BA, dpa)
        addrow(R_BX, dpx)
        dxc = dxc + _dot_nt(dpa, wa_ref[...]) + _dot_nt(dpx, wx_ref[...])
        dwa_ref[...] += _dot_tn(xc, dpa)
        dwx_ref[...] += _dot_tn(xc, dpx)
        addrow(R_LCONV_B, dxc)
        dxcext[0:tm, :] = dxc
        dxcext[tm:tm + LRU_HALO, :] = xccar[...]
        xccar[...] = dxc[0:LRU_HALO, :]
        drx = jnp.zeros((tm, LRU_W), F32)
        for k in range(LRU_K):
            addrow(R_LCW + k, dxc * rxext[pl.ds(LRU_HALO - (LRU_K - 1) + k, tm), :])
            drx = drx + pv_ref[R_LCW + k:R_LCW + k + 1, :] * dxcext[pl.ds(LRU_K - 1 - k, tm), :]
        dz_ref[:, RX0:RX0 + LRU_W] = drx.astype(dz_ref.dtype)

    tile = lambda w: pl.BlockSpec((tm, w), lambda i: (rev(i), 0))
    in_specs = [tile(D_MODEL)] + _mixer_in_specs(tm, rev) + [
        tile(D_MODEL), tile(LRU_W),
        pl.BlockSpec((8, LRU_W), lambda i: (jnp.maximum(rev(i) * (tm // 8) - 1, 0), 0)),
        tile(CONV_W)]
    return _call(
        body, name, (nt,), in_specs,
        [tile(IN_W), _acc_spec((8, 4 * BLK)), _acc_spec((32, CONV_W)), _acc_spec((16, CONV_W)),
         _acc_spec((LRU_W, LRU_W)), _acc_spec((LRU_W, LRU_W))],
        [_sds((t, IN_W), MX), _sds((8, 4 * BLK), F32), _sds((32, CONV_W), F32), _sds((16, CONV_W), F32),
         _sds((LRU_W, LRU_W), F32), _sds((LRU_W, LRU_W), F32)],
        [pltpu.VMEM((tm + CONV_HALO, CONV_W), F32), pltpu.VMEM((7, tm + CONV_HALO - 8, CONV_W), F32),
         pltpu.VMEM((tm, CONV_W), F32), pltpu.VMEM((tm + LRU_HALO, LRU_W), F32),
         pltpu.VMEM((KV_W, tm + BLK), F32), pltpu.VMEM((KV_W, tm + BLK), F32),
         pltpu.VMEM((tm + CONV_HALO, CONV_W), F32), pltpu.VMEM((7, tm + CONV_HALO - 8, CONV_W), F32),
         pltpu.VMEM((32, 8, CONV_W), F32), pltpu.VMEM((tm + LRU_HALO, LRU_W), F32),
         pltpu.VMEM((KV_W, BLK), F32), pltpu.VMEM((KV_W, BLK), F32),
         pltpu.VMEM((CONV_HALO, CONV_W), F32), pltpu.VMEM((LRU_HALO, LRU_W), F32), pltpu.VMEM((8, LRU_W), F32)],
        [dy, z, z, sink, cw, pv, wa, wx, ycat, hl, hl, uc], riders)


def _post_fwd(ycat, h0, gmix, w_out, g2, w_up, w_down, name, riders=()):
    t = h0.shape[0]
    tm = _tile(t, POST_TILE)
    nj = D_FF // FF_BLK

    def body(y_ref, h_ref, gm_ref, wo_ref, g2_ref, wu_ref, wd_ref, h1_ref, a_ref, h2_ref, ym_ref, hn_ref):
        ym, _, _ = _group_rms_fwd(y_ref[...], gm_ref[...])
        ym = ym.astype(MX)
        ym_ref[...] = ym
        h1 = h_ref[...] + jnp.dot(ym, wo_ref[...], preferred_element_type=F32)
        h1_ref[...] = h1
        hn, _, _ = _rms_fwd(h1, g2_ref[...])
        hn = hn.astype(MX)
        hn_ref[...] = hn
        for j in range(nj):
            u = jnp.dot(hn, wu_ref[j], preferred_element_type=F32)
            a_ref[:, j * FF_BLK:(j + 1) * FF_BLK] = jnp.square(jnp.maximum(u, 0.0)).astype(MX)
        h2_ref[...] = h1 + jnp.dot(a_ref[...], wd_ref[...], preferred_element_type=F32)

    tile = lambda w: pl.BlockSpec((tm, w), lambda i: (i, 0))
    return _call(
        body, name, (t // tm,),
        [tile(D_MODEL), tile(D_MODEL), _const_spec((1, D_MODEL)), _const_spec((D_MODEL, D_MODEL)),
         _const_spec((1, D_MODEL)), _const_spec((nj, D_MODEL, FF_BLK)), _const_spec((D_FF, D_MODEL))],
        [tile(D_MODEL), tile(D_FF), tile(D_MODEL), tile(D_MODEL), tile(D_MODEL)],
        [_sds((t, D_MODEL), F32), _sds((t, D_FF), MX), _sds((t, D_MODEL), F32), _sds((t, D_MODEL), MX),
         _sds((t, D_MODEL), MX)],
        [], [ycat, h0, gmix, w_out, g2, w_up, w_down], riders)


def _ffn_bwd(dh2, act, h1, g2, w_up_t, w_down, name, riders=()):
    t = h1.shape[0]
    tm = _tile(t, POST_TILE)
    nj = D_FF // FF_BLK

    def body(dh2_ref, a_ref, h1_ref, g2_ref, wut_ref, wd_ref, dh1_ref, dh1b_ref, dh2b_ref, du_ref, dg2_ref):
        @pl.when(pl.program_id(0) == 0)
        def _():
            dg2_ref[...] = jnp.zeros_like(dg2_ref)

        dh2 = dh2_ref[...]
        dh2b = dh2.astype(MX)
        dh2b_ref[...] = dh2b
        for j in range(nj):
            cols = slice(j * FF_BLK, (j + 1) * FF_BLK)
            da = _dot_nt(dh2b, wd_ref[j])
            du_ref[:, cols] = (da * (2.0 * jnp.sqrt(a_ref[:, cols].astype(F32)))).astype(MX)
        dhn = jnp.dot(du_ref[...], wut_ref[...], preferred_element_type=F32)
        _, xh, r = _rms_fwd(h1_ref[...], g2_ref[...])
        dx, dg = _rms_bwd(dhn, xh, r, g2_ref[...])
        dg2_ref[...] += dg
        dh1 = dh2 + dx
        dh1_ref[...] = dh1
        dh1b_ref[...] = dh1.astype(MX)

    tile = lambda w: pl.BlockSpec((tm, w), lambda i: (i, 0))
    return _call(
        body, name, (t // tm,),
        [tile(D_MODEL), tile(D_FF), tile(D_MODEL), _const_spec((1, D_MODEL)),
         _const_spec((D_FF, D_MODEL)), _const_spec((nj, FF_BLK, D_MODEL))],
        [tile(D_MODEL), tile(D_MODEL), tile(D_MODEL), tile(D_FF), _acc_spec((1, D_MODEL))],
        [_sds((t, D_MODEL), F32), _sds((t, D_MODEL), MX), _sds((t, D_MODEL), MX), _sds((t, D_FF), MX),
         _sds((1, D_MODEL), F32)],
        [], [dh2, act, h1, g2, w_up_t, w_down], riders)


def _mix_bwd(dh1, ycat, gmix, w_out, name):
    t = dh1.shape[0]
    tm = _tile(t)

    def body(dh1_ref, y_ref, gm_ref, wo_ref, dy_ref, dgm_ref):
        @pl.when(pl.program_id(0) == 0)
        def _():
            dgm_ref[...] = jnp.zeros_like(dgm_ref)

        dym = _dot_nt(dh1_ref[...], wo_ref[...])
        gm = gm_ref[...]
        _, yh, rr = _group_rms_fwd(y_ref[...], gm)
        outs, dgs = [], []
        for (a, b), rg in zip(_GROUPS, rr):
            dxg, dgg = _rms_bwd(dym[:, a:b], yh[:, a:b], rg, gm[:, a:b])
            outs.append(dxg)
            dgs.append(dgg)
        dy_ref[...] = jnp.concatenate(outs, axis=1)
        dgm_ref[...] += jnp.concatenate(dgs, axis=1)

    tile = pl.BlockSpec((tm, D_MODEL), lambda i: (i, 0))
    (dy, dgm), _ = _call(
        body, name, (t // tm,), [tile, tile, _const_spec((1, D_MODEL)), _const_spec((D_MODEL, D_MODEL))],
        [tile, _acc_spec((1, D_MODEL))], [_sds((t, D_MODEL), F32), _sds((1, D_MODEL), F32)],
        [], [dh1, ycat, gmix, w_out])
    return dy, dgm


def _in_bwd(dz, h0, dh1, g1, w_in, name):
    t = h0.shape[0]
    tm = _tile(t)

    def body(dz_ref, h_ref, dh1_ref, g_ref, wg_ref, dh0_ref, dg_ref, w_scr):
        @pl.when(pl.program_id(0) == 0)
        def _():
            dg_ref[...] = jnp.zeros_like(dg_ref)
            _assemble_w_in(wg_ref, w_scr)

        dhn = _dot_nt(dz_ref[...], w_scr[...])
        _, xh, r = _rms_fwd(h_ref[...], g_ref[...])
        dx, dg = _rms_bwd(dhn, xh, r, g_ref[...])
        dg_ref[...] += dg
        dh0_ref[...] = dh1_ref[...] + dx

    tile = lambda w: pl.BlockSpec((tm, w), lambda i: (i, 0))
    (dh0, dg), _ = _call(
        body, name, (t // tm,),
        [tile(IN_W), tile(D_MODEL), tile(D_MODEL), _const_spec((1, D_MODEL)), _const_spec((N_DEV, D_MODEL, IN_SHARD))],
        [tile(D_MODEL), _acc_spec((1, D_MODEL))], [_sds((t, D_MODEL), F32), _sds((1, D_MODEL), F32)],
        [pltpu.VMEM((D_MODEL, IN_W), MX)], [dz, h0, dh1, g1, w_in])
    return dh0, dg


def _loss_head(h, gf, target, name):
    t = h.shape[0]
    tm = _tile(t)

    def body(h_ref, g_ref, t_ref, dh_ref, loss_ref, dg_ref):
        @pl.when(pl.program_id(0) == 0)
        def _():
            loss_ref[...] = jnp.zeros_like(loss_ref)
            dg_ref[...] = jnp.zeros_like(dg_ref)

        g = g_ref[...]
        y, xh, r = _rms_fwd(h_ref[...], g)
        err = y - t_ref[...]
        part = 0.5 * jnp.sum(jnp.mean(err * err, axis=-1, keepdims=True), axis=0, keepdims=True)
        loss_ref[...] += jnp.broadcast_to(part, loss_ref.shape)
        dx, dg = _rms_bwd(err * (1.0 / D_MODEL), xh, r, g)
        dg_ref[...] += dg
        dh_ref[...] = dx

    tile = pl.BlockSpec((tm, D_MODEL), lambda i: (i, 0))
    (dh, loss, dg), _ = _call(
        body, name, (t // tm,), [tile, _const_spec((1, D_MODEL)), tile],
        [tile, _acc_spec((1, 128)), _acc_spec((1, D_MODEL))],
        [_sds((t, D_MODEL), F32), _sds((1, 128), F32), _sds((1, D_MODEL), F32)], [], [h, gf, target])
    return dh, loss, dg


def _dw(x, y, name, split, bm, bn):
    t, m = x.shape
    n = y.shape[1]
    tk = _tile(t, DW_TILE)
    nk = t // tk
    if split == "rows":
        assert bn == n
        r, c = m // N_DEV, n
        per = bm // r
        out_block = pl.BlockSpec((per, r, c), lambda a, b, k: (a, 0, 0))
    else:
        assert bm == m
        r, c = m, n // N_DEV
        per = bn // c
        out_block = pl.BlockSpec((per, r, c), lambda a, b, k: (b, 0, 0))

    def body(x_ref, y_ref, o_ref, o16_ref, acc):
        k = pl.program_id(2)

        @pl.when(k == 0)
        def _():
            acc[...] = jnp.zeros_like(acc)

        acc[...] += _dot_tn(x_ref[...], y_ref[...])

        @pl.when(k == nk - 1)
        def _():
            for d in range(per):
                v = acc[d * r:(d + 1) * r, :] if split == "rows" else acc[:, d * c:(d + 1) * c]
                o_ref[d] = v
                o16_ref[d] = v.astype(o16_ref.dtype)

    return pl.pallas_call(
        body, name=name, grid=(m // bm, n // bn, nk),
        in_specs=[pl.BlockSpec((tk, bm), lambda a, b, k: (k, a)), pl.BlockSpec((tk, bn), lambda a, b, k: (k, b))],
        out_specs=[out_block, out_block],
        out_shape=[_sds((N_DEV, r, c), F32), _sds((N_DEV, r, c), WIRE)],
        scratch_shapes=[pltpu.VMEM((bm, bn), F32)],
        compiler_params=pltpu.CompilerParams(dimension_semantics=("arbitrary",) * 3, vmem_limit_bytes=VMEM_LIMIT),
    )(x, y)


def _adamw_math(w, g, m, v):
    m = ADAM_B1 * m + (1.0 - ADAM_B1) * g
    v = ADAM_B2 * v + (1.0 - ADAM_B2) * jnp.square(g)
    m_hat = m / (1.0 - ADAM_B1 ** ADAM_STEP)
    v_hat = v / (1.0 - ADAM_B2 ** ADAM_STEP)
    delta = -ADAM_LR * (m_hat / (jnp.sqrt(v_hat) + ADAM_EPS) + ADAM_WD * w)
    return delta, m, v


def _adamw_shard(g_own, g_recv, dev, w, m, v, name):
    r, c = w.shape
    br = r
    for cand in (256, 128, 112, 64, 56, 32, 16, 8):
        if r % cand == 0:
            br = cand
            break

    def body(dev_ref, go_ref, gr_ref, w_ref, m_ref, v_ref, g_out, d_out, m_out, v_out):
        g = go_ref[0]
        for j in range(N_DEV - 1):
            g = g + gr_ref[j].astype(F32)
        delta, mn, vn = _adamw_math(w_ref[...], g, m_ref[...], v_ref[...])
        g_out[...] = g
        d_out[...] = delta
        m_out[...] = mn
        v_out[...] = vn

    tile = pl.BlockSpec((br, c), lambda i, dev_ref: (i, 0))
    return pl.pallas_call(
        body, name=name,
        grid_spec=pltpu.PrefetchScalarGridSpec(
            num_scalar_prefetch=1, grid=(r // br,),
            in_specs=[pl.BlockSpec((1, br, c), lambda i, dev_ref: (dev_ref[0], i, 0)),
                      pl.BlockSpec((N_DEV - 1, br, c), lambda i, dev_ref: (0, i, 0)),
                      tile, tile, tile],
            out_specs=[tile, tile, tile, tile]),
        out_shape=[_sds((r, c), F32)] * 4,
        compiler_params=pltpu.CompilerParams(dimension_semantics=("arbitrary",), vmem_limit_bytes=VMEM_LIMIT),
    )(dev, g_own, g_recv, w, m, v)


def _adamw_small(gs, ws, ms, vs, name):
    n = len(gs)

    def body(*refs):
        g_refs, w_refs, m_refs, v_refs = (refs[k * n:(k + 1) * n] for k in range(4))
        outs = refs[4 * n:]
        for k in range(n):
            delta, mn, vn = _adamw_math(w_refs[k][...], g_refs[k][...], m_refs[k][...], v_refs[k][...])
            outs[k][...] = delta
            outs[n + k][...] = mn
            outs[2 * n + k][...] = vn

    shapes = [_sds(w.shape, F32) for w in ws]
    res = pl.pallas_call(body, name=name, out_shape=shapes * 3,
                         compiler_params=pltpu.CompilerParams(vmem_limit_bytes=VMEM_LIMIT))(*gs, *ws, *ms, *vs)
    return res[:n], res[n:2 * n], res[2 * n:]


def _sum_parts(part, name):
    def body(p_ref, o_ref):
        g = p_ref[0]
        for d in range(1, N_DEV):
            g = g + p_ref[d]
        o_ref[...] = g

    return pl.pallas_call(body, name=name, out_shape=_sds(part.shape[1:], F32))(part)


def _block_diag(w):
    out = jnp.zeros((LRU_W, LRU_W), w.dtype)
    for h in range(4):
        out = lax.dynamic_update_slice(out, w[h], (h * 64, h * 64))
    return out


def _unblock_diag(w):
    return jnp.concatenate([w[h * 64:(h + 1) * 64, h * 64:(h + 1) * 64] for h in range(4)], axis=0)


def _layer_params(p, l):
    row = lambda a: a[l].reshape(1, -1)
    sink_rows = jnp.repeat(p["attn_sinks"][l].reshape(4, 2), 2 * BLK, axis=1)
    sink_rows = jnp.concatenate([sink_rows, jnp.zeros((4, 4 * BLK), F32)], axis=0)
    cw = jnp.concatenate([p["conv_dw_w"][l], jnp.zeros((1, CONV_W), F32)], axis=0)
    pv = jnp.concatenate([
        row(p["conv_dw_b"]), row(p["conv_ln_g"]), row(p["conv_ln_b"]), row(p["lru_conv_b"]), row(p["lru_ba"]),
        row(p["lru_bx"]), row(p["lru_lambda"]), jnp.zeros((1, LRU_W), F32), p["lru_conv_w"][l],
        jnp.zeros((4, LRU_W), F32)], axis=0)
    return dict(
        g1=row(p["norm1"]), sink=sink_rows, cw=cw, pv=pv,
        wa=_block_diag(p["lru_wa"][l]).astype(MX), wx=_block_diag(p["lru_wx"][l]).astype(MX),
        gmix=row(p["mix_norm"]), g2=row(p["norm2"]))


_SMALL = ["norm1", "attn_sinks", "conv_dw_w", "conv_dw_b", "conv_ln_g", "conv_ln_b", "lru_conv_w", "lru_conv_b",
          "lru_wa", "lru_ba", "lru_wx", "lru_bx", "lru_lambda", "mix_norm", "norm2"]
_BIG = ["w_in", "w_out", "w_up", "w_down"]
_WEIGHTS = ["norm1", "w_in", "attn_sinks", "conv_dw_w", "conv_dw_b", "conv_ln_g", "conv_ln_b", "lru_conv_w",
            "lru_conv_b", "lru_wa", "lru_ba", "lru_wx", "lru_bx", "lru_lambda", "mix_norm", "w_out", "norm2", "w_up",
            "w_down", "final_norm"]


def kernel(x, norm1, w_in, attn_sinks, conv_dw_w, conv_dw_b, conv_ln_g, conv_ln_b, lru_conv_w, lru_conv_b, lru_wa, lru_ba, lru_wx, lru_bx, lru_lambda, mix_norm, w_out, norm2, w_up, w_down, final_norm, loss_target, m_norm1, m_w_in, m_attn_sinks, m_conv_dw_w, m_conv_dw_b, m_conv_ln_g, m_conv_ln_b, m_lru_conv_w, m_lru_conv_b, m_lru_wa, m_lru_ba, m_lru_wx, m_lru_bx, m_lru_lambda, m_mix_norm, m_w_out, m_norm2, m_w_up, m_w_down, m_final_norm, v_norm1, v_w_in, v_attn_sinks, v_conv_dw_w, v_conv_dw_b, v_conv_ln_g, v_conv_ln_b, v_lru_conv_w, v_lru_conv_b, v_lru_wa, v_lru_ba, v_lru_wx, v_lru_bx, v_lru_lambda, v_mix_norm, v_w_out, v_norm2, v_w_up, v_w_down, v_final_norm):
    w = dict(norm1=norm1, w_in=w_in, attn_sinks=attn_sinks, conv_dw_w=conv_dw_w, conv_dw_b=conv_dw_b,
             conv_ln_g=conv_ln_g, conv_ln_b=conv_ln_b, lru_conv_w=lru_conv_w, lru_conv_b=lru_conv_b, lru_wa=lru_wa,
             lru_ba=lru_ba, lru_wx=lru_wx, lru_bx=lru_bx, lru_lambda=lru_lambda, mix_norm=mix_norm, w_out=w_out,
             norm2=norm2, w_up=w_up, w_down=w_down, final_norm=final_norm)
    m = dict(norm1=m_norm1, w_in=m_w_in, attn_sinks=m_attn_sinks, conv_dw_w=m_conv_dw_w, conv_dw_b=m_conv_dw_b,
             conv_ln_g=m_conv_ln_g, conv_ln_b=m_conv_ln_b, lru_conv_w=m_lru_conv_w, lru_conv_b=m_lru_conv_b,
             lru_wa=m_lru_wa, lru_ba=m_lru_ba, lru_wx=m_lru_wx, lru_bx=m_lru_bx, lru_lambda=m_lru_lambda,
             mix_norm=m_mix_norm, w_out=m_w_out, norm2=m_norm2, w_up=m_w_up, w_down=m_w_down, final_norm=m_final_norm)
    v = dict(norm1=v_norm1, w_in=v_w_in, attn_sinks=v_attn_sinks, conv_dw_w=v_conv_dw_w, conv_dw_b=v_conv_dw_b,
             conv_ln_g=v_conv_ln_g, conv_ln_b=v_conv_ln_b, lru_conv_w=v_lru_conv_w, lru_conv_b=v_lru_conv_b,
             lru_wa=v_lru_wa, lru_ba=v_lru_ba, lru_wx=v_lru_wx, lru_bx=v_lru_bx, lru_lambda=v_lru_lambda,
             mix_norm=v_mix_norm, w_out=v_w_out, norm2=v_norm2, w_up=v_w_up, w_down=v_w_down, final_norm=v_final_norm)
    depth = w_in.shape[0]
    xi, yi, ci = _me()
    dev = (4 * xi + 2 * yi + ci).astype(jnp.int32)
    dev1 = dev.reshape(1)
    wb = {n: w[n].astype(MX) for n in _BIG}
    layer_shards = lambda l: [wb["w_out"][l], wb["w_up"][l], wb["w_down"][l]]

    _, ((g_in0, g_cw, g_lcw),) = _call(None, "gather_first", None, [], [], [], [], [],
                                        [_gather_rider([wb["w_in"][0], conv_dw_w, lru_conv_w])])
    cols = lambda g: jnp.moveaxis(g, 0, -2).reshape(g.shape[1:-1] + (N_DEV * g.shape[-1],))
    p = dict(w)
    p["conv_dw_w"] = cols(g_cw)
    p["lru_conv_w"] = cols(g_lcw)
    lp = [_layer_params(p, l) for l in range(depth)]

    gathered = [dict(w_in=g_in0), dict()]
    saved = []
    h = x[0]
    for l in range(depth):
        q, gw = lp[l], gathered[l]
        z, hn1 = _ln_in(h, q["g1"], gw["w_in"], f"ln_in{l}")
        riders = [_gather_rider(layer_shards(0))] if l == 0 else []
        (ycat, hl, uc), got = _mixer_fwd(z, q["sink"], q["cw"], q["pv"], q["wa"], q["wx"], f"mixer_fwd{l}", riders)
        if l == 0:
            gw["w_out"], gw["w_up"], gw["w_down"] = got[0]
            gw["w_out"] = gw["w_out"].reshape(D_MODEL, D_MODEL)
        riders = [_gather_rider([wb["w_in"][1]] + layer_shards(1))] if l == 0 else []
        (h1, act, h2, ym, hn2), got = _post_fwd(ycat, h, q["gmix"], gw["w_out"], q["g2"], gw["w_up"],
                                                gw["w_down"].reshape(D_FF, D_MODEL), f"post_fwd{l}", riders)
        if l == 0:
            nxt = gathered[1]
            nxt["w_in"], nxt["w_out"], nxt["w_up"], nxt["w_down"] = got[0]
            nxt["w_out"] = nxt["w_out"].reshape(D_MODEL, D_MODEL)
        saved.append(dict(h0=h, z=z, hn1=hn1, ycat=ycat, hl=hl, uc=uc, h1=h1, act=act, ym=ym, hn2=hn2))
        h = h2
    dh, loss, dgf = _loss_head(h, final_norm.reshape(1, -1), loss_target[0], "loss_head")

    grads = [None] * depth
    big = {n: [None] * depth for n in _BIG}
    pending = []

    def send_pending():
        riders = [_scatter_rider([item[3] for item in pending])] if pending else []
        return riders, list(pending)

    def record(sent, got):
        for item, recv in zip(sent, got[0] if sent else []):
            big[item[0]][item[1]] = (item[2], recv)
        del pending[:len(sent)]

    for l in reversed(range(depth)):
        q, s, gw = lp[l], saved[l], gathered[l]
        riders, sent = send_pending()
        w_up_t = jnp.swapaxes(gw["w_up"], 1, 2).reshape(D_FF, D_MODEL)
        (dh1, dh1b, dhb, du, dg2), got = _ffn_bwd(dh, s["act"], s["h1"], q["g2"], w_up_t, gw["w_down"],
                                                  f"ffn_bwd{l}", riders)
        record(sent, got)
        dycat, dgm = _mix_bwd(dh1b, s["ycat"], q["gmix"], gw["w_out"], f"mix_bwd{l}")
        pending.append(("w_down", l) + tuple(_dw(s["act"], dhb, f"dw_down{l}", "rows", 2048, D_MODEL)))
        pending.append(("w_up", l) + tuple(_dw(s["hn2"], du, f"dw_up{l}", "cols", D_MODEL, 2048)))
        pending.append(("w_out", l) + tuple(_dw(s["ym"], dh1b, f"dw_out{l}", "rows", D_MODEL, D_MODEL)))
        riders, sent = send_pending()
        (dz, dsink, dcw, dpv, dwa, dwx), got = _mixer_bwd(dycat, s["z"], s["ycat"], s["hl"], s["uc"], q["sink"],
                                                          q["cw"], q["pv"], q["wa"], q["wx"], f"mixer_bwd{l}", riders)
        record(sent, got)
        pending.append(("w_in", l) + tuple(_dw(s["hn1"], dz, f"dw_in{l}", "cols", D_MODEL, IN_W)))
        dh, dg1 = _in_bwd(dz, s["h0"], dh1, q["g1"], gw["w_in"], f"in_bwd{l}")
        grads[l] = dict(
            norm1=dg1[0], attn_sinks=jnp.stack([dsink[0:4, 0], dsink[0:4, 2 * BLK]], axis=1).reshape(8),
            conv_dw_w=dcw[0:CONV_K], conv_dw_b=dpv[R_CONV_B], conv_ln_g=dpv[R_LN_G], conv_ln_b=dpv[R_LN_B],
            lru_conv_w=dpv[R_LCW:R_LCW + LRU_K], lru_conv_b=dpv[R_LCONV_B], lru_wa=_unblock_diag(dwa),
            lru_ba=dpv[R_BA].reshape(4, 64), lru_wx=_unblock_diag(dwx), lru_bx=dpv[R_BX].reshape(4, 64),
            lru_lambda=dpv[R_LAM], mix_norm=dgm[0], norm2=dg2[0])

    small = [jnp.stack([grads[l][n] for l in range(depth)]) for n in _SMALL] + [dgf, loss[:, 0:1]]
    sizes = [a.size for a in small]
    total = -(-sum(sizes) // 1024) * 1024
    packed = jnp.concatenate([a.reshape(-1) for a in small] + [jnp.zeros((total - sum(sizes),), F32)])
    riders, sent = send_pending()
    _, got = _call(None, "tail_exchange", None, [], [], [], [], [],
                   riders + [_bcast_rider([packed.reshape(total // 128, 128)])])
    record(sent, got)
    summed = _sum_parts(got[1][0], "sum_small_grads").reshape(-1)
    small_sums, pos = [], 0
    for a, size in zip(small, sizes):
        small_sums.append(summed[pos:pos + size].reshape(a.shape))
        pos += size

    out = {}
    for n in _BIG:
        res = [_adamw_shard(big[n][l][0], big[n][l][1], dev1, w[n][l], m[n][l], v[n][l], f"adamw_{n}{l}")
               for l in range(depth)]
        out[n] = [jnp.stack([res[l][j] for l in range(depth)]) for j in range(4)]
    shard = lambda a: lax.dynamic_slice_in_dim(a, dev * (a.shape[-1] // N_DEV), a.shape[-1] // N_DEV, axis=a.ndim - 1)
    flat = {"lru_wa": (depth, LRU_W, 64), "lru_wx": (depth, LRU_W, 64), "final_norm": (1, D_MODEL)}
    gs, ws, ms, vs = [], [], [], []
    for n, g in zip(_SMALL + ["final_norm"], small_sums[:-1]):
        shp = flat.get(n, w[n].shape)
        gs.append((shard(g) if n in ("conv_dw_w", "lru_conv_w") else g).reshape(shp))
        ws.append(w[n].reshape(shp))
        ms.append(m[n].reshape(shp))
        vs.append(v[n].reshape(shp))
    sd, sm, sv = _adamw_small(gs, ws, ms, vs, "adamw_small")
    for j, n in enumerate(_SMALL + ["final_norm"]):
        out[n] = [a.reshape(w[n].shape) for a in (gs[j], sd[j], sm[j], sv[j])]
    loss_total = small_sums[-1][0, 0]

    result = [loss_total, dh[None]]
    for j in range(4):
        result += [out[n][j] for n in _WEIGHTS]
    return tuple(result)
```

```python
import types

import jax
import jax.numpy as jnp
from jax import lax
from jax.experimental import pallas as pl
from jax.experimental.pallas import tpu as pltpu

F32 = jnp.float32
MX = jnp.bfloat16
WIRE = jnp.bfloat16

D_MODEL = 1024
HEAD_DIM = 64
ATTN_W = 512
KV_W = 128
BLK = 128
CONV_W = 256
CONV_K = 31
LRU_W = 256
LRU_K = 4
LRU_C = 8.0
IN_W = 1792
D_FF = 4096
FF_BLK = 512
N_DEV = 8
IN_SHARD = IN_W // N_DEV
RMS_EPS = 1e-6
LN_EPS = 1e-5
MASK_VALUE = -1e30
SCALE = HEAD_DIM ** -0.5
CONV_HALO = 32
LRU_HALO = 8
CONV_CHUNK = 64
POST_TILE = 512
DW_TILE = 1024
Q0, K0, V0, CV0, CG0, RX0, RG0 = 0, 512, 640, 768, 1024, 1280, 1536
R_CONV_B, R_LN_G, R_LN_B, R_LCONV_B, R_BA, R_BX, R_LAM, R_LCW = 0, 1, 2, 3, 4, 5, 6, 8

ADAM_LR, ADAM_B1, ADAM_B2, ADAM_EPS, ADAM_WD, ADAM_STEP = 0.001, 0.9, 0.999, 1e-08, 0.01, 10

VMEM_LIMIT = 56 * 1024 * 1024
MESH = pl.DeviceIdType.MESH
ANY = pl.BlockSpec(memory_space=pl.ANY)


def _tile(t, cap=512):
    return min(cap, t)


def _dot(a, b):
    return jnp.dot(a.astype(MX), b.astype(MX), preferred_element_type=F32)


def _dot_nt(a, b):
    return lax.dot_general(a.astype(MX), b.astype(MX), (((1,), (1,)), ((), ())), preferred_element_type=F32)


def _dot_tn(a, b):
    return lax.dot_general(a.astype(MX), b.astype(MX), (((0,), (0,)), ((), ())), preferred_element_type=F32)


def _const_spec(shape):
    nd = len(shape)
    return pl.BlockSpec(shape, lambda *_: (0,) * nd, pipeline_mode=pl.Buffered(1))


def _acc_spec(shape):
    nd = len(shape)
    return pl.BlockSpec(shape, lambda *_: (0,) * nd)


def _sds(shape, dtype):
    return jax.ShapeDtypeStruct(shape, dtype)


def _sigmoid(x):
    return jax.nn.sigmoid(x)


def _rms_fwd(x, g):
    r = lax.rsqrt(jnp.mean(x * x, axis=-1, keepdims=True) + RMS_EPS)
    xh = x * r
    return xh * g, xh, r


def _rms_bwd(dy, xh, r, g):
    t = dy * g
    dx = r * (t - xh * jnp.mean(t * xh, axis=-1, keepdims=True))
    return dx, jnp.sum(dy * xh, axis=0, keepdims=True)


_GROUPS = ((0, 512), (512, 768), (768, 1024))


def _group_rms_fwd(y, g):
    parts = [_rms_fwd(y[:, a:b], g[:, a:b]) for a, b in _GROUPS]
    return (jnp.concatenate([p[0] for p in parts], axis=1),
            jnp.concatenate([p[1] for p in parts], axis=1),
            [p[2] for p in parts])


def _gelu(x):
    c = 0.7978845608028654
    u = c * (x + 0.044715 * x * x * x)
    th = jnp.tanh(u)
    val = 0.5 * x * (1.0 + th)
    grad = 0.5 * (1.0 + th) + 0.5 * x * (1.0 - th * th) * c * (1.0 + 3.0 * 0.044715 * x * x)
    return val, grad


def _neg_expm1(x):
    series = -x * (1.0 + x * (0.5 + x * (1.0 / 6.0 + x * (1.0 / 24.0))))
    return jnp.where(x > -0.02, series, 1.0 - jnp.exp(x))


def _me():
    return lax.axis_index("x"), lax.axis_index("y"), lax.axis_index("c")


def _gather_rider(arrays):
    arrays = list(arrays)
    n = len(arrays)

    def plan(ins, outs, sems):
        ssem, rsem, lsem = sems
        x, y, c = _me()
        chips = [(1 - x, y), (x, 1 - y), (1 - x, 1 - y)]

        def copy(a, k, block, to, own=False):
            dst = outs[a].at[4 * block[0] + 2 * block[1] + block[2]]
            return pltpu.make_async_remote_copy(
                src_ref=ins[a] if own else dst, dst_ref=dst, send_sem=ssem.at[7 * a + k],
                recv_sem=rsem.at[7 * a + k], device_id=to, device_id_type=MESH)

        return x, y, c, chips, copy, lsem

    def start(ins, outs, sems):
        x, y, c, chips, copy, lsem = plan(ins, outs, sems)
        for a in range(n):
            pltpu.make_async_copy(ins[a], outs[a].at[4 * x + 2 * y + c], lsem.at[a]).start()
            copy(a, 0, (x, y, c), (x, y, 1 - c), own=True).start()
            for j, chip in enumerate(chips):
                copy(a, 1 + j, (x, y, c), (*chip, c), own=True).start()

    def mid(ins, outs, sems):
        x, y, c, chips, copy, _ = plan(ins, outs, sems)
        for a in range(n):
            for j, chip in enumerate(chips):
                copy(a, 1 + j, (*chip, c), (x, y, c)).wait_recv()
                copy(a, 4 + j, (*chip, c), (x, y, 1 - c)).start()

    def finish(ins, outs, sems):
        x, y, c, chips, copy, lsem = plan(ins, outs, sems)
        for a in range(n):
            copy(a, 0, (x, y, 1 - c), (x, y, c)).wait_recv()
            for j, chip in enumerate(chips):
                copy(a, 4 + j, (*chip, 1 - c), (x, y, c)).wait_recv()
        for a in range(n):
            copy(a, 0, (x, y, c), (x, y, 1 - c), own=True).wait_send()
            for j, chip in enumerate(chips):
                copy(a, 1 + j, (x, y, c), (*chip, c), own=True).wait_send()
                copy(a, 4 + j, (*chip, c), (x, y, 1 - c)).wait_send()
            pltpu.make_async_copy(ins[a], outs[a].at[4 * x + 2 * y + c], lsem.at[a]).wait()

    return types.SimpleNamespace(
        arrays=arrays, out_shape=[_sds((N_DEV,) + a.shape, a.dtype) for a in arrays],
        scratch=[pltpu.SemaphoreType.DMA((7 * n,)), pltpu.SemaphoreType.DMA((7 * n,)), pltpu.SemaphoreType.DMA((n,))],
        start=start, mid=mid, finish=finish)


def _bcast_rider(arrays):
    arrays = list(arrays)
    n = len(arrays)

    def copies(ins, outs, sems, landing):
        ssem, rsem, lsem = sems
        x, y, c = _me()
        out = []
        for a in range(n):
            out.append(pltpu.make_async_copy(ins[a], outs[a].at[4 * x + 2 * y + c], lsem.at[a]))
            for f in range(1, N_DEV):
                px = 1 - x if f & 4 else x
                py = 1 - y if f & 2 else y
                pc = 1 - c if f & 1 else c
                slot = 4 * px + 2 * py + pc if landing else 4 * x + 2 * y + c
                out.append(pltpu.make_async_remote_copy(
                    src_ref=ins[a], dst_ref=outs[a].at[slot], send_sem=ssem.at[7 * a + f - 1],
                    recv_sem=rsem.at[7 * a + f - 1], device_id=(px, py, pc), device_id_type=MESH))
        return out

    def start(ins, outs, sems):
        for cp in copies(ins, outs, sems, landing=False):
            cp.start()

    def finish(ins, outs, sems):
        for cp in copies(ins, outs, sems, landing=True):
            cp.wait()

    return types.SimpleNamespace(
        arrays=arrays, out_shape=[_sds((N_DEV,) + a.shape, a.dtype) for a in arrays],
        scratch=[pltpu.SemaphoreType.DMA((7 * n,)), pltpu.SemaphoreType.DMA((7 * n,)), pltpu.SemaphoreType.DMA((n,))],
        start=start, mid=None, finish=finish)


def _scatter_rider(arrays):
    arrays = list(arrays)
    n = len(arrays)

    def copies(ins, outs, sems):
        ssem, rsem = sems
        x, y, c = _me()
        out = []
        for a in range(n):
            for f in range(1, N_DEV):
                px = 1 - x if f & 4 else x
                py = 1 - y if f & 2 else y
                pc = 1 - c if f & 1 else c
                out.append(pltpu.make_async_remote_copy(
                    src_ref=ins[a].at[4 * px + 2 * py + pc], dst_ref=outs[a].at[f - 1], send_sem=ssem.at[7 * a + f - 1],
                    recv_sem=rsem.at[7 * a + f - 1], device_id=(px, py, pc), device_id_type=MESH))
        return out

    def start(ins, outs, sems):
        for cp in copies(ins, outs, sems):
            cp.start()

    def finish(ins, outs, sems):
        for cp in copies(ins, outs, sems):
            cp.wait()

    return types.SimpleNamespace(
        arrays=arrays, out_shape=[_sds((N_DEV - 1,) + a.shape[1:], a.dtype) for a in arrays],
        scratch=[pltpu.SemaphoreType.DMA((7 * n,)), pltpu.SemaphoreType.DMA((7 * n,))],
        start=start, mid=None, finish=finish)


def _call(body, name, grid, in_specs, out_specs, out_shape, scratch, operands, riders=()):
    n_in, n_out, n_scr = len(operands), len(out_shape), len(scratch)
    nsteps = grid[0] if grid else 1
    sizes = [(len(r.arrays), len(r.out_shape), len(r.scratch)) for r in riders]

    def wrapped(*refs):
        pos = n_in
        r_ins = []
        for ri, _, _ in sizes:
            r_ins.append(refs[pos:pos + ri])
            pos += ri
        outs = refs[pos:pos + n_out]
        pos += n_out
        r_outs = []
        for _, ro, _ in sizes:
            r_outs.append(refs[pos:pos + ro])
            pos += ro
        scr = refs[pos:pos + n_scr]
        pos += n_scr
        r_sems = []
        for _, _, rs in sizes:
            r_sems.append(refs[pos:pos + rs])
            pos += rs
        step = pl.program_id(0) if grid else 0

        def at(s, fn):
            if grid:
                pl.when(step == s)(fn)
            else:
                fn()

        for r, a, b, c in zip(riders, r_ins, r_outs, r_sems):
            at(0, lambda r=r, a=a, b=b, c=c: r.start(a, b, c))
        for r, a, b, c in zip(riders, r_ins, r_outs, r_sems):
            if r.mid is not None:
                at((3 * nsteps) // 4, lambda r=r, a=a, b=b, c=c: r.mid(a, b, c))
        if body is not None:
            body(*refs[:n_in], *outs, *scr)
        for r, a, b, c in zip(riders, r_ins, r_outs, r_sems):
            at(nsteps - 1, lambda r=r, a=a, b=b, c=c: r.finish(a, b, c))

    r_arrays = [a for r in riders for a in r.arrays]
    r_shapes = [s for r in riders for s in r.out_shape]
    kwargs = {}
    if grid:
        kwargs = dict(grid=grid, compiler_params=pltpu.CompilerParams(
            dimension_semantics=("arbitrary",) * len(grid), vmem_limit_bytes=VMEM_LIMIT))
    res = pl.pallas_call(
        wrapped, name=name,
        in_specs=list(in_specs) + [ANY] * len(r_arrays),
        out_specs=list(out_specs) + [ANY] * len(r_shapes),
        out_shape=list(out_shape) + r_shapes,
        scratch_shapes=list(scratch) + [s for r in riders for s in r.scratch],
        **kwargs,
    )(*operands, *r_arrays)
    host, rest = res[:n_out], res[n_out:]
    r_res = []
    for _, ro, _ in sizes:
        r_res.append(rest[:ro])
        rest = rest[ro:]
    return host, r_res


def _ln_in(h, g1, w_in, name):
    t = h.shape[0]
    tm = _tile(t)

    def body(h_ref, g_ref, w_ref, z_ref, hn_ref):
        y, _, _ = _rms_fwd(h_ref[...], g_ref[...])
        hn = y.astype(MX)
        hn_ref[...] = hn
        z_ref[...] = jnp.dot(hn, w_ref[...], preferred_element_type=F32)

    tile = lambda w: pl.BlockSpec((tm, w), lambda i: (i, 0))
    (z, hn), _ = _call(
        body, name, (t // tm,),
        [tile(D_MODEL), _const_spec((1, D_MODEL)), _const_spec((D_MODEL, IN_W))],
        [tile(IN_W), tile(D_MODEL)], [_sds((t, IN_W), F32), _sds((t, D_MODEL), MX)], [], [h, g1, w_in])
    return z, hn


def _band2(kb, g):
    lo = lax.broadcasted_iota(jnp.int32, kb.shape, 1) < HEAD_DIM
    kr = pltpu.roll(kb, HEAD_DIM, 1)
    if g == 0:
        top, bot = jnp.where(lo, kb, 0.0), jnp.where(lo, 0.0, kr)
    else:
        top, bot = jnp.where(lo, kr, 0.0), jnp.where(lo, 0.0, kb)
    return jnp.concatenate([top, bot], axis=0)


def _attn_block(z_ref, zh_ref, sink_ref, b, first):
    rows = slice(b * BLK, (b + 1) * BLK)
    prev = zh_ref if b == 0 else z_ref
    prow = slice(0, BLK) if b == 0 else slice((b - 1) * BLK, b * BLK)
    kb = jnp.concatenate([prev[prow, K0:K0 + KV_W], z_ref[rows, K0:K0 + KV_W]], axis=0)
    vb = jnp.concatenate([prev[prow, V0:V0 + KV_W], z_ref[rows, V0:V0 + KV_W]], axis=0)
    k2 = [_band2(kb, g) for g in range(2)]
    v2 = [_band2(vb, g) for g in range(2)]
    q2 = [jnp.concatenate([z_ref[rows, (2 * g) * BLK:(2 * g + 1) * BLK], z_ref[rows, (2 * g + 1) * BLK:(2 * g + 2) * BLK]],
                          axis=0) for g in range(2)]
    rr = lax.broadcasted_iota(jnp.int32, (4 * BLK, 2 * BLK), 0) & (BLK - 1)
    cc = lax.broadcasted_iota(jnp.int32, (4 * BLK, 2 * BLK), 1)
    first_block = jnp.logical_and(first, b == 0).astype(jnp.int32)
    mask = jnp.logical_and(jnp.logical_and(cc > rr, cc <= rr + BLK), cc >= BLK * first_block)
    s = jnp.concatenate([_dot_nt(q2[g], k2[g]) for g in range(2)], axis=0) * SCALE
    w = 2 * BLK
    out, psink = [], []
    for hh in range(2):
        sh = jnp.where(mask, s[:, hh * w:(hh + 1) * w], MASK_VALUE)
        sk = jnp.concatenate([jnp.broadcast_to(sink_ref[p:p + 1, hh * w:hh * w + 1], (BLK, 1)) for p in range(4)], axis=0)
        m = jnp.maximum(jnp.max(sh, axis=1, keepdims=True), sk)
        p = jnp.exp(sh - m)
        es = jnp.exp(sk - m)
        inv = 1.0 / (jnp.sum(p, axis=1, keepdims=True) + es)
        out.append(p * inv)
        psink.append(es * inv)
    return q2, k2, v2, jnp.concatenate(out, axis=1), psink


def _scan_fwd(a, b, tm):
    rows = lax.broadcasted_iota(jnp.int32, a.shape, 0)
    d = 1
    while d < tm:
        keep = rows >= d
        a_sh = jnp.where(keep, pltpu.roll(a, d, 0), 1.0)
        b_sh = jnp.where(keep, pltpu.roll(b, d, 0), 0.0)
        b = a * b_sh + b
        a = a * a_sh
        d *= 2
    return a, b


def _scan_bwd(c, b, tm):
    rows = lax.broadcasted_iota(jnp.int32, c.shape, 0)
    d = 1
    while d < tm:
        keep = rows < tm - d
        c_sh = jnp.where(keep, pltpu.roll(c, tm - d, 0), 1.0)
        b_sh = jnp.where(keep, pltpu.roll(b, tm - d, 0), 0.0)
        b = c * b_sh + b
        c = c * c_sh
        d *= 2
    return b


def _shifted_copies(ext, shifts, tm):
    rows = tm + CONV_HALO - 8
    for r in range(1, 8):
        shifts[r - 1, 0:rows, :] = ext[pl.ds(r, rows), :]


def _tap(ext, shifts, off, r0, n):
    a, r = divmod(off, 8)
    lo = 8 * a + r0
    if r == 0:
        return ext[lo:lo + n, :]
    return shifts[r - 1, lo:lo + n, :]


def _glu_fill(z_ref, zh_ref, uext, ush, first, tm, sg_out=None):
    cv = z_ref[:, CV0:CV0 + CONV_W]
    sg = _sigmoid(z_ref[:, CG0:CG0 + CONV_W])
    if sg_out is not None:
        sg_out[...] = sg
    hrow = BLK - CONV_HALO
    uh = zh_ref[hrow:BLK, CV0:CV0 + CONV_W] * _sigmoid(zh_ref[hrow:BLK, CG0:CG0 + CONV_W])
    uext[0:CONV_HALO, :] = jnp.where(first, 0.0, uh)
    uext[CONV_HALO:CONV_HALO + tm, :] = cv * sg
    _shifted_copies(uext, ush, tm)


def _conv_taps(cw_ref, pv_ref, uext, ush, out_ref, tm):
    for r0 in range(0, tm, CONV_CHUNK):
        acc = jnp.broadcast_to(pv_ref[R_CONV_B:R_CONV_B + 1, :], (CONV_CHUNK, CONV_W))
        for k in range(CONV_K):
            acc = acc + cw_ref[k:k + 1, :] * _tap(uext, ush, CONV_HALO - (CONV_K - 1) + k, r0, CONV_CHUNK)
        out_ref[r0:r0 + CONV_CHUNK, :] = acc


def _ln_silu(uc, pv_ref):
    mu = jnp.mean(uc, axis=-1, keepdims=True)
    xc = uc - mu
    rs = lax.rsqrt(jnp.mean(xc * xc, axis=-1, keepdims=True) + LN_EPS)
    xh = xc * rs
    ln = xh * pv_ref[R_LN_G:R_LN_G + 1, :] + pv_ref[R_LN_B:R_LN_B + 1, :]
    sg = _sigmoid(ln)
    return xh, rs, ln, sg


def _lru_gates(z_ref, zh_ref, pv_ref, wa_ref, wx_ref, rxext, first, tm):
    rxext[0:LRU_HALO, :] = jnp.where(first, 0.0, zh_ref[BLK - LRU_HALO:BLK, RX0:RX0 + LRU_W])
    rxext[LRU_HALO:LRU_HALO + tm, :] = z_ref[:, RX0:RX0 + LRU_W]
    xc = jnp.broadcast_to(pv_ref[R_LCONV_B:R_LCONV_B + 1, :], (tm, LRU_W))
    for k in range(LRU_K):
        xc = xc + pv_ref[R_LCW + k:R_LCW + k + 1, :] * rxext[pl.ds(LRU_HALO - (LRU_K - 1) + k, tm), :]
    r = _sigmoid(_dot(xc, wa_ref[...]) + pv_ref[R_BA:R_BA + 1, :])
    ig = _sigmoid(_dot(xc, wx_ref[...]) + pv_ref[R_BX:R_BX + 1, :])
    lam = pv_ref[R_LAM:R_LAM + 1, :]
    sp = jnp.log1p(jnp.exp(-lam))
    la = (-LRU_C * r) * sp
    a = jnp.exp(la)
    mult = jnp.sqrt(_neg_expm1(2.0 * la))
    return xc, r, ig, sp, la, a, mult


def _mixer_in_specs(tm, tile_of):
    hb = tm // BLK
    return [
        pl.BlockSpec((tm, IN_W), lambda i: (tile_of(i), 0)),
        pl.BlockSpec((BLK, IN_W), lambda i: (jnp.maximum(tile_of(i) * hb - 1, 0), 0)),
        _const_spec((8, 4 * BLK)),
        _const_spec((32, CONV_W)),
        _const_spec((16, CONV_W)),
        _const_spec((LRU_W, LRU_W)),
        _const_spec((LRU_W, LRU_W)),
    ]


def _mixer_fwd(z, sink, cw, pv, wa, wx, name, riders=()):
    t = z.shape[0]
    tm = _tile(t)
    nb = tm // BLK

    def body(z_ref, zh_ref, sink_ref, cw_ref, pv_ref, wa_ref, wx_ref, y_ref, hl_ref, uc_ref, uext, ush, rxext, hcar):
        i = pl.program_id(0)
        first = i == 0

        @pl.when(first)
        def _():
            hcar[...] = jnp.zeros_like(hcar)

        for b in range(nb):
            rows = slice(b * BLK, (b + 1) * BLK)
            _, _, v2, prob, _ = _attn_block(z_ref, zh_ref, sink_ref, b, first)
            for g in range(2):
                o = _dot(prob[2 * g * BLK:(2 * g + 2) * BLK], v2[g])
                y_ref[rows, (2 * g) * BLK:(2 * g + 1) * BLK] = o[0:BLK]
                y_ref[rows, (2 * g + 1) * BLK:(2 * g + 2) * BLK] = o[BLK:2 * BLK]
        _glu_fill(z_ref, zh_ref, uext, ush, first, tm)
        _conv_taps(cw_ref, pv_ref, uext, ush, uc_ref, tm)
        _, _, ln, sg = _ln_silu(uc_ref[...], pv_ref)
        y_ref[:, ATTN_W:ATTN_W + CONV_W] = ln * sg
        xc, _, ig, _, _, a, mult = _lru_gates(z_ref, zh_ref, pv_ref, wa_ref, wx_ref, rxext, first, tm)
        acum, h = _scan_fwd(a, mult * (ig * xc), tm)
        h = h + acum * hcar[0:1, :]
        hl_ref[...] = h
        hcar[0:1, :] = h[tm - 1:tm, :]
        gl, _ = _gelu(z_ref[:, RG0:RG0 + LRU_W])
        y_ref[:, ATTN_W + CONV_W:ATTN_W + CONV_W + LRU_W] = h * gl

    tile = lambda w: pl.BlockSpec((tm, w), lambda i: (i, 0))
    return _call(
        body, name, (t // tm,), _mixer_in_specs(tm, lambda i: i),
        [tile(D_MODEL), tile(LRU_W), tile(CONV_W)],
        [_sds((t, D_MODEL), F32), _sds((t, LRU_W), F32), _sds((t, CONV_W), F32)],
        [pltpu.VMEM((tm + CONV_HALO, CONV_W), F32), pltpu.VMEM((7, tm + CONV_HALO - 8, CONV_W), F32),
         pltpu.VMEM((tm + LRU_HALO, LRU_W), F32), pltpu.VMEM((8, LRU_W), F32)],
        [z, z, sink, cw, pv, wa, wx], riders)


def _mixer_bwd(dh1b, dh1, h0, z, ycat, hl, uc, sink, cw, pv, wa, wx, gmix, w_out, g1, w_in, name, riders=()):
    t = z.shape[0]
    tm = _tile(t)
    nt = t // tm
    nb = tm // BLK
    rev = lambda i: nt - 1 - i

    def body(dh1b_ref, z_ref, zh_ref, sink_ref, cw_ref, pv_ref, wa_ref, wx_ref, y_ref, hl_ref, hlh_ref, uc_ref,
             dh1_ref, h0_ref, gm_ref, wo_ref, g1_ref, win_ref,
             dz_ref, dsink_ref, dcw_ref, dpv_ref, dwa_ref, dwx_ref, dh0_ref, dgm_ref, dg1_ref,
             uext, ush, sgs, rxext, dkext, dvext, ducext, dsh, dcw8, dxcext, kcar, vcar, uccar, xccar, gcar, dy_ref):
        i = pl.program_id(0)
        first = i == nt - 1

        @pl.when(i == 0)
        def _():
            for car in (kcar, vcar, uccar, xccar, gcar, dcw8):
                car[...] = jnp.zeros_like(car)
            for acc in (dsink_ref, dpv_ref, dwa_ref, dwx_ref, dgm_ref, dg1_ref):
                acc[...] = jnp.zeros_like(acc)

        def addrow(r, val):
            dpv_ref[r:r + 1, :] += jnp.sum(val, axis=0, keepdims=True)

        dym = _dot_nt(dh1b_ref[...], wo_ref[...])
        gm = gm_ref[...]
        _, yh, rr = _group_rms_fwd(y_ref[...], gm)
        dgs = []
        for (ga, gb), rg in zip(_GROUPS, rr):
            dxg, dgg = _rms_bwd(dym[:, ga:gb], yh[:, ga:gb], rg, gm[:, ga:gb])
            dy_ref[:, ga:gb] = dxg
            dgs.append(dgg)
        dgm_ref[...] += jnp.concatenate(dgs, axis=1)

        dkext[:, 0:tm] = jnp.zeros((KV_W, tm), F32)
        dvext[:, 0:tm] = jnp.zeros((KV_W, tm), F32)
        dkext[:, tm:tm + BLK] = kcar[...]
        dvext[:, tm:tm + BLK] = vcar[...]
        lane512 = lax.broadcasted_iota(jnp.int32, (1, 4 * BLK), 1) < 2 * BLK
        lo = lax.broadcasted_iota(jnp.int32, (4 * BLK, BLK), 1) < HEAD_DIM
        hd, w2 = HEAD_DIM, 2 * BLK
        for b in range(nb):
            rows = slice(b * BLK, (b + 1) * BLK)
            band = slice(b * BLK, (b + 2) * BLK)
            q2, k2, v2, prob, psink = _attn_block(z_ref, zh_ref, sink_ref, b, first)
            stack = lambda ref: jnp.concatenate([ref[rows, p * BLK:(p + 1) * BLK] for p in range(4)], axis=0)
            do4 = stack(dy_ref)
            dlt = do4 * stack(y_ref)
            d0 = jnp.sum(jnp.where(lo, dlt, 0.0), axis=1, keepdims=True)
            d1 = jnp.sum(jnp.where(lo, 0.0, dlt), axis=1, keepdims=True)
            dp = jnp.concatenate([_dot_nt(do4[g * w2:(g + 1) * w2], v2[g]) for g in range(2)], axis=0)
            dl = jnp.concatenate([jnp.broadcast_to(d0, (4 * BLK, w2)), jnp.broadcast_to(d1, (4 * BLK, w2))], axis=1)
            draw = (prob * (dp - dl)) * SCALE
            e0, e1 = psink[0] * d0, psink[1] * d1
            for p in range(4):
                prs = slice(p * BLK, (p + 1) * BLK)
                s0 = jnp.sum(e0[prs], axis=0, keepdims=True)
                s1 = jnp.sum(e1[prs], axis=0, keepdims=True)
                dsink_ref[p:p + 1, :] += -jnp.where(lane512, s0, s1)
            for g in range(2):
                grs = slice(g * w2, (g + 1) * w2)
                dq = _dot(draw[grs], k2[g])
                dz_ref[rows, (2 * g) * BLK:(2 * g + 1) * BLK] = dq[0:BLK].astype(dz_ref.dtype)
                dz_ref[rows, (2 * g + 1) * BLK:(2 * g + 2) * BLK] = dq[BLK:2 * BLK].astype(dz_ref.dtype)
                tk = _dot_tn(q2[g], draw[grs])
                tv = _dot_tn(do4[grs], prob[grs])
                dkext[g * hd:(g + 1) * hd, band] += tk[0:hd, 0:w2] + tk[hd:2 * hd, w2:2 * w2]
                dvext[g * hd:(g + 1) * hd, band] += tv[0:hd, 0:w2] + tv[hd:2 * hd, w2:2 * w2]
        dz_ref[:, K0:K0 + KV_W] = jnp.transpose(dkext[:, BLK:BLK + tm]).astype(dz_ref.dtype)
        dz_ref[:, V0:V0 + KV_W] = jnp.transpose(dvext[:, BLK:BLK + tm]).astype(dz_ref.dtype)
        kcar[...] = dkext[:, 0:BLK]
        vcar[...] = dvext[:, 0:BLK]

        _glu_fill(z_ref, zh_ref, uext, ush, first, tm, sg_out=sgs)
        xh, rs, ln, sg = _ln_silu(uc_ref[...], pv_ref)
        dln = dy_ref[:, ATTN_W:ATTN_W + CONV_W] * (sg * (1.0 + ln * (1.0 - sg)))
        addrow(R_LN_G, dln * xh)
        addrow(R_LN_B, dln)
        dxh = dln * pv_ref[R_LN_G:R_LN_G + 1, :]
        duc = rs * (dxh - jnp.mean(dxh, axis=-1, keepdims=True) - xh * jnp.mean(dxh * xh, axis=-1, keepdims=True))
        addrow(R_CONV_B, duc)
        ducext[0:tm, :] = duc
        ducext[tm:tm + CONV_HALO, :] = uccar[...]
        uccar[...] = duc[0:CONV_HALO, :]
        _shifted_copies(ducext, dsh, tm)
        for r0 in range(0, tm, CONV_CHUNK):
            crow = slice(r0, r0 + CONV_CHUNK)
            duc_c = ducext[crow, :]
            du = jnp.zeros((CONV_CHUNK, CONV_W), F32)
            for k in range(CONV_K):
                prod = duc_c * _tap(uext, ush, CONV_HALO - (CONV_K - 1) + k, r0, CONV_CHUNK)
                part = prod[0:8]
                for s in range(8, CONV_CHUNK, 8):
                    part = part + prod[s:s + 8]
                dcw8[k] += part
                du = du + cw_ref[k:k + 1, :] * _tap(ducext, dsh, CONV_K - 1 - k, r0, CONV_CHUNK)
            sgc = sgs[crow, :]
            dz_ref[crow, CV0:CV0 + CONV_W] = (du * sgc).astype(dz_ref.dtype)
            u_c = uext[CONV_HALO + r0:CONV_HALO + r0 + CONV_CHUNK, :]
            dz_ref[crow, CG0:CG0 + CONV_W] = (du * u_c * (1.0 - sgc)).astype(dz_ref.dtype)

        @pl.when(i == nt - 1)
        def _():
            dcw_ref[...] = jnp.sum(dcw8[...], axis=1)

        xc, r, ig, sp, la, a, mult = _lru_gates(z_ref, zh_ref, pv_ref, wa_ref, wx_ref, rxext, first, tm)
        h = hl_ref[...]
        rowi = lax.broadcasted_iota(jnp.int32, (tm, LRU_W), 0)
        hlast = jnp.where(first, 0.0, hlh_ref[7:8, :])
        hprev = jnp.where(rowi == 0, hlast, pltpu.roll(h, 1, 0))
        dyl = dy_ref[:, ATTN_W + CONV_W:ATTN_W + CONV_W + LRU_W]
        gl, dgl = _gelu(z_ref[:, RG0:RG0 + LRU_W])
        dz_ref[:, RG0:RG0 + LRU_W] = (dyl * h * dgl).astype(dz_ref.dtype)
        dh = dyl * gl + jnp.where(rowi == tm - 1, gcar[0:1, :], 0.0)
        c = jnp.where(rowi == tm - 1, 0.0, pltpu.roll(a, tm - 1, 0))
        gg = _scan_bwd(c, dh, tm)
        gcar[0:1, :] = a[0:1, :] * gg[0:1, :]
        dmult = gg * (ig * xc)
        dig = gg * mult * xc
        dxc = gg * mult * ig
        dla = gg * hprev * a - dmult * a * a / mult
        dr = dla * (-LRU_C * sp)
        lam = pv_ref[R_LAM:R_LAM + 1, :]
        dpv_ref[R_LAM:R_LAM + 1, :] += jnp.sum(dla * (-LRU_C * r), axis=0, keepdims=True) * (-_sigmoid(-lam))
        dpa = dr * r * (1.0 - r)
        dpx = dig * ig * (1.0 - ig)
        addrow(R_BA, dpa)
        addrow(R_BX, dpx)
        dxc = dxc + _dot_nt(dpa, wa_ref[...]) + _dot_nt(dpx, wx_ref[...])
        dwa_ref[...] += _dot_tn(xc, dpa)
        dwx_ref[...] += _dot_tn(xc, dpx)
        addrow(R_LCONV_B, dxc)
        dxcext[0:tm, :] = dxc
        dxcext[tm:tm + LRU_HALO, :] = xccar[...]
        xccar[...] = dxc[0:LRU_HALO, :]
        drx = jnp.zeros((tm, LRU_W), F32)
        for k in range(LRU_K):
            addrow(R_LCW + k, dxc * rxext[pl.ds(LRU_HALO - (LRU_K - 1) + k, tm), :])
            drx = drx + pv_ref[R_LCW + k:R_LCW + k + 1, :] * dxcext[pl.ds(LRU_K - 1 - k, tm), :]
        dz_ref[:, RX0:RX0 + LRU_W] = drx.astype(dz_ref.dtype)

        dhn = _dot_nt(dz_ref[...], win_ref[...])
        _, xh1, r1 = _rms_fwd(h0_ref[...], g1_ref[...])
        dx, dg = _rms_bwd(dhn, xh1, r1, g1_ref[...])
        dg1_ref[...] += dg
        dh0_ref[...] = dh1_ref[...] + dx

    tile = lambda w: pl.BlockSpec((tm, w), lambda i: (rev(i), 0))
    in_specs = [tile(D_MODEL)] + _mixer_in_specs(tm, rev) + [
        tile(D_MODEL), tile(LRU_W),
        pl.BlockSpec((8, LRU_W), lambda i: (jnp.maximum(rev(i) * (tm // 8) - 1, 0), 0)),
        tile(CONV_W), tile(D_MODEL), tile(D_MODEL), _const_spec((1, D_MODEL)), _const_spec((D_MODEL, D_MODEL)),
        _const_spec((1, D_MODEL)), _const_spec((D_MODEL, IN_W))]
    return _call(
        body, name, (nt,), in_specs,
        [tile(IN_W), _acc_spec((8, 4 * BLK)), _acc_spec((32, CONV_W)), _acc_spec((16, CONV_W)),
         _acc_spec((LRU_W, LRU_W)), _acc_spec((LRU_W, LRU_W)), tile(D_MODEL), _acc_spec((1, D_MODEL)),
         _acc_spec((1, D_MODEL))],
        [_sds((t, IN_W), MX), _sds((8, 4 * BLK), F32), _sds((32, CONV_W), F32), _sds((16, CONV_W), F32),
         _sds((LRU_W, LRU_W), F32), _sds((LRU_W, LRU_W), F32), _sds((t, D_MODEL), F32), _sds((1, D_MODEL), F32),
         _sds((1, D_MODEL), F32)],
        [pltpu.VMEM((tm + CONV_HALO, CONV_W), F32), pltpu.VMEM((7, tm + CONV_HALO - 8, CONV_W), F32),
         pltpu.VMEM((tm, CONV_W), F32), pltpu.VMEM((tm + LRU_HALO, LRU_W), F32),
         pltpu.VMEM((KV_W, tm + BLK), F32), pltpu.VMEM((KV_W, tm + BLK), F32),
         pltpu.VMEM((tm + CONV_HALO, CONV_W), F32), pltpu.VMEM((7, tm + CONV_HALO - 8, CONV_W), F32),
         pltpu.VMEM((32, 8, CONV_W), F32), pltpu.VMEM((tm + LRU_HALO, LRU_W), F32),
         pltpu.VMEM((KV_W, BLK), F32), pltpu.VMEM((KV_W, BLK), F32),
         pltpu.VMEM((CONV_HALO, CONV_W), F32), pltpu.VMEM((LRU_HALO, LRU_W), F32), pltpu.VMEM((8, LRU_W), F32),
         pltpu.VMEM((tm, D_MODEL), F32)],
        [dh1b, z, z, sink, cw, pv, wa, wx, ycat, hl, hl, uc, dh1, h0, gmix, w_out, g1, w_in], riders)


def _post_fwd(ycat, h0, gmix, w_out, g2, w_up, w_down, name, riders=()):
    t = h0.shape[0]
    tm = _tile(t, POST_TILE)
    nj = D_FF // FF_BLK

    def body(y_ref, h_ref, gm_ref, wo_ref, g2_ref, wu_ref, wd_ref, h1_ref, a_ref, h2_ref, ym_ref, hn_ref):
        ym, _, _ = _group_rms_fwd(y_ref[...], gm_ref[...])
        ym = ym.astype(MX)
        ym_ref[...] = ym
        h1 = h_ref[...] + jnp.dot(ym, wo_ref[...], preferred_element_type=F32)
        h1_ref[...] = h1
        hn, _, _ = _rms_fwd(h1, g2_ref[...])
        hn = hn.astype(MX)
        hn_ref[...] = hn
        for j in range(nj):
            u = jnp.dot(hn, wu_ref[j], preferred_element_type=F32)
            a_ref[:, j * FF_BLK:(j + 1) * FF_BLK] = jnp.square(jnp.maximum(u, 0.0)).astype(MX)
        h2_ref[...] = h1 + jnp.dot(a_ref[...], wd_ref[...], preferred_element_type=F32)

    tile = lambda w: pl.BlockSpec((tm, w), lambda i: (i, 0))
    return _call(
        body, name, (t // tm,),
        [tile(D_MODEL), tile(D_MODEL), _const_spec((1, D_MODEL)), _const_spec((D_MODEL, D_MODEL)),
         _const_spec((1, D_MODEL)), _const_spec((nj, D_MODEL, FF_BLK)), _const_spec((D_FF, D_MODEL))],
        [tile(D_MODEL), tile(D_FF), tile(D_MODEL), tile(D_MODEL), tile(D_MODEL)],
        [_sds((t, D_MODEL), F32), _sds((t, D_FF), MX), _sds((t, D_MODEL), F32), _sds((t, D_MODEL), MX),
         _sds((t, D_MODEL), MX)],
        [], [ycat, h0, gmix, w_out, g2, w_up, w_down], riders)


def _ffn_bwd(dh2, act, h1, g2, w_up_t, w_down, name, riders=()):
    t = h1.shape[0]
    tm = _tile(t, POST_TILE)
    nj = D_FF // FF_BLK

    def body(dh2_ref, a_ref, h1_ref, g2_ref, wut_ref, wd_ref, dh1_ref, dh1b_ref, dh2b_ref, du_ref, dg2_ref):
        @pl.when(pl.program_id(0) == 0)
        def _():
            dg2_ref[...] = jnp.zeros_like(dg2_ref)

        dh2 = dh2_ref[...]
        dh2b = dh2.astype(MX)
        dh2b_ref[...] = dh2b
        for j in range(nj):
            cols = slice(j * FF_BLK, (j + 1) * FF_BLK)
            da = _dot_nt(dh2b, wd_ref[j])
            du_ref[:, cols] = (da * (2.0 * jnp.sqrt(a_ref[:, cols].astype(F32)))).astype(MX)
        dhn = jnp.dot(du_ref[...], wut_ref[...], preferred_element_type=F32)
        _, xh, r = _rms_fwd(h1_ref[...], g2_ref[...])
        dx, dg = _rms_bwd(dhn, xh, r, g2_ref[...])
        dg2_ref[...] += dg
        dh1 = dh2 + dx
        dh1_ref[...] = dh1
        dh1b_ref[...] = dh1.astype(MX)

    tile = lambda w: pl.BlockSpec((tm, w), lambda i: (i, 0))
    return _call(
        body, name, (t // tm,),
        [tile(D_MODEL), tile(D_FF), tile(D_MODEL), _const_spec((1, D_MODEL)),
         _const_spec((D_FF, D_MODEL)), _const_spec((nj, FF_BLK, D_MODEL))],
        [tile(D_MODEL), tile(D_MODEL), tile(D_MODEL), tile(D_FF), _acc_spec((1, D_MODEL))],
        [_sds((t, D_MODEL), F32), _sds((t, D_MODEL), MX), _sds((t, D_MODEL), MX), _sds((t, D_FF), MX),
         _sds((1, D_MODEL), F32)],
        [], [dh2, act, h1, g2, w_up_t, w_down], riders)


def _loss_head(h, gf, target, name):
    t = h.shape[0]
    tm = _tile(t)

    def body(h_ref, g_ref, t_ref, dh_ref, loss_ref, dg_ref):
        @pl.when(pl.program_id(0) == 0)
        def _():
            loss_ref[...] = jnp.zeros_like(loss_ref)
            dg_ref[...] = jnp.zeros_like(dg_ref)

        g = g_ref[...]
        y, xh, r = _rms_fwd(h_ref[...], g)
        err = y - t_ref[...]
        part = 0.5 * jnp.sum(jnp.mean(err * err, axis=-1, keepdims=True), axis=0, keepdims=True)
        loss_ref[...] += jnp.broadcast_to(part, loss_ref.shape)
        dx, dg = _rms_bwd(err * (1.0 / D_MODEL), xh, r, g)
        dg_ref[...] += dg
        dh_ref[...] = dx

    tile = pl.BlockSpec((tm, D_MODEL), lambda i: (i, 0))
    (dh, loss, dg), _ = _call(
        body, name, (t // tm,), [tile, _const_spec((1, D_MODEL)), tile],
        [tile, _acc_spec((1, 128)), _acc_spec((1, D_MODEL))],
        [_sds((t, D_MODEL), F32), _sds((1, 128), F32), _sds((1, D_MODEL), F32)], [], [h, gf, target])
    return dh, loss, dg


def _dw(x, y, name, split, bm, bn):
    t, m = x.shape
    n = y.shape[1]
    tk = _tile(t, DW_TILE)
    nk = t // tk
    if split == "rows":
        assert bn == n
        r, c = m // N_DEV, n
        per = bm // r
        out_block = pl.BlockSpec((per, r, c), lambda a, b, k: (a, 0, 0))
    else:
        assert bm == m
        r, c = m, n // N_DEV
        per = bn // c
        out_block = pl.BlockSpec((per, r, c), lambda a, b, k: (b, 0, 0))

    def body(x_ref, y_ref, o_ref, o16_ref, acc):
        k = pl.program_id(2)

        @pl.when(k == 0)
        def _():
            acc[...] = jnp.zeros_like(acc)

        acc[...] += _dot_tn(x_ref[...], y_ref[...])

        @pl.when(k == nk - 1)
        def _():
            for d in range(per):
                v = acc[d * r:(d + 1) * r, :] if split == "rows" else acc[:, d * c:(d + 1) * c]
                o_ref[d] = v
                o16_ref[d] = v.astype(o16_ref.dtype)

    return pl.pallas_call(
        body, name=name, grid=(m // bm, n // bn, nk),
        in_specs=[pl.BlockSpec((tk, bm), lambda a, b, k: (k, a)), pl.BlockSpec((tk, bn), lambda a, b, k: (k, b))],
        out_specs=[out_block, out_block],
        out_shape=[_sds((N_DEV, r, c), F32), _sds((N_DEV, r, c), WIRE)],
        scratch_shapes=[pltpu.VMEM((bm, bn), F32)],
        compiler_params=pltpu.CompilerParams(dimension_semantics=("arbitrary",) * 3, vmem_limit_bytes=VMEM_LIMIT),
    )(x, y)


def _adamw_math(w, g, m, v):
    m = ADAM_B1 * m + (1.0 - ADAM_B1) * g
    v = ADAM_B2 * v + (1.0 - ADAM_B2) * jnp.square(g)
    m_hat = m / (1.0 - ADAM_B1 ** ADAM_STEP)
    v_hat = v / (1.0 - ADAM_B2 ** ADAM_STEP)
    delta = -ADAM_LR * (m_hat / (jnp.sqrt(v_hat) + ADAM_EPS) + ADAM_WD * w)
    return delta, m, v


def _adamw_shard(g_own, g_recv, dev, w, m, v, name):
    r, c = w.shape
    br = r
    for cand in (256, 128, 112, 64, 56, 32, 16, 8):
        if r % cand == 0:
            br = cand
            break

    def body(dev_ref, go_ref, gr_ref, w_ref, m_ref, v_ref, g_out, d_out, m_out, v_out):
        g = go_ref[0]
        for j in range(N_DEV - 1):
            g = g + gr_ref[j].astype(F32)
        delta, mn, vn = _adamw_math(w_ref[...], g, m_ref[...], v_ref[...])
        g_out[...] = g
        d_out[...] = delta
        m_out[...] = mn
        v_out[...] = vn

    tile = pl.BlockSpec((br, c), lambda i, dev_ref: (i, 0))
    return pl.pallas_call(
        body, name=name,
        grid_spec=pltpu.PrefetchScalarGridSpec(
            num_scalar_prefetch=1, grid=(r // br,),
            in_specs=[pl.BlockSpec((1, br, c), lambda i, dev_ref: (dev_ref[0], i, 0)),
                      pl.BlockSpec((N_DEV - 1, br, c), lambda i, dev_ref: (0, i, 0)),
                      tile, tile, tile],
            out_specs=[tile, tile, tile, tile]),
        out_shape=[_sds((r, c), F32)] * 4,
        compiler_params=pltpu.CompilerParams(dimension_semantics=("arbitrary",), vmem_limit_bytes=VMEM_LIMIT),
    )(dev, g_own, g_recv, w, m, v)


def _adamw_small(gs, ws, ms, vs, name):
    n = len(gs)

    def body(*refs):
        g_refs, w_refs, m_refs, v_refs = (refs[k * n:(k + 1) * n] for k in range(4))
        outs = refs[4 * n:]
        for k in range(n):
            delta, mn, vn = _adamw_math(w_refs[k][...], g_refs[k][...], m_refs[k][...], v_refs[k][...])
            outs[k][...] = delta
            outs[n + k][...] = mn
            outs[2 * n + k][...] = vn

    shapes = [_sds(w.shape, F32) for w in ws]
    res = pl.pallas_call(body, name=name, out_shape=shapes * 3,
                         compiler_params=pltpu.CompilerParams(vmem_limit_bytes=VMEM_LIMIT))(*gs, *ws, *ms, *vs)
    return res[:n], res[n:2 * n], res[2 * n:]


def _sum_parts(part, name):
    def body(p_ref, o_ref):
        g = p_ref[0]
        for d in range(1, N_DEV):
            g = g + p_ref[d]
        o_ref[...] = g

    return pl.pallas_call(body, name=name, out_shape=_sds(part.shape[1:], F32))(part)


def _block_diag(w):
    out = jnp.zeros((LRU_W, LRU_W), w.dtype)
    for h in range(4):
        out = lax.dynamic_update_slice(out, w[h], (h * 64, h * 64))
    return out


def _unblock_diag(w):
    return jnp.concatenate([w[h * 64:(h + 1) * 64, h * 64:(h + 1) * 64] for h in range(4)], axis=0)


def _layer_params(p, l):
    row = lambda a: a[l].reshape(1, -1)
    sink_rows = jnp.repeat(p["attn_sinks"][l].reshape(4, 2), 2 * BLK, axis=1)
    sink_rows = jnp.concatenate([sink_rows, jnp.zeros((4, 4 * BLK), F32)], axis=0)
    cw = jnp.concatenate([p["conv_dw_w"][l], jnp.zeros((1, CONV_W), F32)], axis=0)
    pv = jnp.concatenate([
        row(p["conv_dw_b"]), row(p["conv_ln_g"]), row(p["conv_ln_b"]), row(p["lru_conv_b"]), row(p["lru_ba"]),
        row(p["lru_bx"]), row(p["lru_lambda"]), jnp.zeros((1, LRU_W), F32), p["lru_conv_w"][l],
        jnp.zeros((4, LRU_W), F32)], axis=0)
    return dict(
        g1=row(p["norm1"]), sink=sink_rows, cw=cw, pv=pv,
        wa=_block_diag(p["lru_wa"][l]).astype(MX), wx=_block_diag(p["lru_wx"][l]).astype(MX),
        gmix=row(p["mix_norm"]), g2=row(p["norm2"]))


_SMALL = ["norm1", "attn_sinks", "conv_dw_w", "conv_dw_b", "conv_ln_g", "conv_ln_b", "lru_conv_w", "lru_conv_b",
          "lru_wa", "lru_ba", "lru_wx", "lru_bx", "lru_lambda", "mix_norm", "norm2"]
_BIG = ["w_in", "w_out", "w_up", "w_down"]
_WEIGHTS = ["norm1", "w_in", "attn_sinks", "conv_dw_w", "conv_dw_b", "conv_ln_g", "conv_ln_b", "lru_conv_w",
            "lru_conv_b", "lru_wa", "lru_ba", "lru_wx", "lru_bx", "lru_lambda", "mix_norm", "w_out", "norm2", "w_up",
            "w_down", "final_norm"]


def kernel(x, norm1, w_in, attn_sinks, conv_dw_w, conv_dw_b, conv_ln_g, conv_ln_b, lru_conv_w, lru_conv_b, lru_wa, lru_ba, lru_wx, lru_bx, lru_lambda, mix_norm, w_out, norm2, w_up, w_down, final_norm, loss_target, m_norm1, m_w_in, m_attn_sinks, m_conv_dw_w, m_conv_dw_b, m_conv_ln_g, m_conv_ln_b, m_lru_conv_w, m_lru_conv_b, m_lru_wa, m_lru_ba, m_lru_wx, m_lru_bx, m_lru_lambda, m_mix_norm, m_w_out, m_norm2, m_w_up, m_w_down, m_final_norm, v_norm1, v_w_in, v_attn_sinks, v_conv_dw_w, v_conv_dw_b, v_conv_ln_g, v_conv_ln_b, v_lru_conv_w, v_lru_conv_b, v_lru_wa, v_lru_ba, v_lru_wx, v_lru_bx, v_lru_lambda, v_mix_norm, v_w_out, v_norm2, v_w_up, v_w_down, v_final_norm):
    w = dict(norm1=norm1, w_in=w_in, attn_sinks=attn_sinks, conv_dw_w=conv_dw_w, conv_dw_b=conv_dw_b,
             conv_ln_g=conv_ln_g, conv_ln_b=conv_ln_b, lru_conv_w=lru_conv_w, lru_conv_b=lru_conv_b, lru_wa=lru_wa,
             lru_ba=lru_ba, lru_wx=lru_wx, lru_bx=lru_bx, lru_lambda=lru_lambda, mix_norm=mix_norm, w_out=w_out,
             norm2=norm2, w_up=w_up, w_down=w_down, final_norm=final_norm)
    m = dict(norm1=m_norm1, w_in=m_w_in, attn_sinks=m_attn_sinks, conv_dw_w=m_conv_dw_w, conv_dw_b=m_conv_dw_b,
             conv_ln_g=m_conv_ln_g, conv_ln_b=m_conv_ln_b, lru_conv_w=m_lru_conv_w, lru_conv_b=m_lru_conv_b,
             lru_wa=m_lru_wa, lru_ba=m_lru_ba, lru_wx=m_lru_wx, lru_bx=m_lru_bx, lru_lambda=m_lru_lambda,
             mix_norm=m_mix_norm, w_out=m_w_out, norm2=m_norm2, w_up=m_w_up, w_down=m_w_down, final_norm=m_final_norm)
    v = dict(norm1=v_norm1, w_in=v_w_in, attn_sinks=v_attn_sinks, conv_dw_w=v_conv_dw_w, conv_dw_b=v_conv_dw_b,
             conv_ln_g=v_conv_ln_g, conv_ln_b=v_conv_ln_b, lru_conv_w=v_lru_conv_w, lru_conv_b=v_lru_conv_b,
             lru_wa=v_lru_wa, lru_ba=v_lru_ba, lru_wx=v_lru_wx, lru_bx=v_lru_bx, lru_lambda=v_lru_lambda,
             mix_norm=v_mix_norm, w_out=v_w_out, norm2=v_norm2, w_up=v_w_up, w_down=v_w_down, final_norm=v_final_norm)
    depth = w_in.shape[0]
    xi, yi, ci = _me()
    dev = (4 * xi + 2 * yi + ci).astype(jnp.int32)
    dev1 = dev.reshape(1)
    wb = {n: w[n].astype(MX) for n in _BIG}
    layer_shards = lambda l: [wb["w_out"][l], wb["w_up"][l], wb["w_down"][l]]

    _, ((g_in0, g_cw, g_lcw),) = _call(None, "gather_first", None, [], [], [], [], [],
                                        [_gather_rider([wb["w_in"][0], conv_dw_w, lru_conv_w])])
    cols = lambda g: jnp.moveaxis(g, 0, -2).reshape(g.shape[1:-1] + (N_DEV * g.shape[-1],))
    p = dict(w)
    p["conv_dw_w"] = cols(g_cw)
    p["lru_conv_w"] = cols(g_lcw)
    lp = [_layer_params(p, l) for l in range(depth)]

    gathered = [dict(w_in=cols(g_in0)), dict()]
    saved = []
    h = x[0]
    for l in range(depth):
        q, gw = lp[l], gathered[l]
        z, hn1 = _ln_in(h, q["g1"], gw["w_in"], f"ln_in{l}")
        riders = [_gather_rider(layer_shards(0))] if l == 0 else []
        (ycat, hl, uc), got = _mixer_fwd(z, q["sink"], q["cw"], q["pv"], q["wa"], q["wx"], f"mixer_fwd{l}", riders)
        if l == 0:
            gw["w_out"], gw["w_up"], gw["w_down"] = got[0]
            gw["w_out"] = gw["w_out"].reshape(D_MODEL, D_MODEL)
        riders = [_gather_rider([wb["w_in"][1]] + layer_shards(1))] if l == 0 else []
        (h1, act, h2, ym, hn2), got = _post_fwd(ycat, h, q["gmix"], gw["w_out"], q["g2"], gw["w_up"],
                                                gw["w_down"].reshape(D_FF, D_MODEL), f"post_fwd{l}", riders)
        if l == 0:
            nxt = gathered[1]
            nxt["w_in"], nxt["w_out"], nxt["w_up"], nxt["w_down"] = got[0]
            nxt["w_in"] = cols(nxt["w_in"])
            nxt["w_out"] = nxt["w_out"].reshape(D_MODEL, D_MODEL)
        saved.append(dict(h0=h, z=z, hn1=hn1, ycat=ycat, hl=hl, uc=uc, h1=h1, act=act, ym=ym, hn2=hn2))
        h = h2
    dh, loss, dgf = _loss_head(h, final_norm.reshape(1, -1), loss_target[0], "loss_head")

    grads = [None] * depth
    big = {n: [None] * depth for n in _BIG}
    pending = []

    def send_pending():
        riders = [_scatter_rider([item[3] for item in pending])] if pending else []
        return riders, list(pending)

    def record(sent, got):
        for item, recv in zip(sent, got[0] if sent else []):
            big[item[0]][item[1]] = (item[2], recv)
        del pending[:len(sent)]

    for l in reversed(range(depth)):
        q, s, gw = lp[l], saved[l], gathered[l]
        riders, sent = send_pending()
        w_up_t = jnp.swapaxes(gw["w_up"], 1, 2).reshape(D_FF, D_MODEL)
        (dh1, dh1b, dhb, du, dg2), got = _ffn_bwd(dh, s["act"], s["h1"], q["g2"], w_up_t, gw["w_down"],
                                                  f"ffn_bwd{l}", riders)
        record(sent, got)
        pending.append(("w_down", l) + tuple(_dw(s["act"], dhb, f"dw_down{l}", "rows", 2048, D_MODEL)))
        pending.append(("w_up", l) + tuple(_dw(s["hn2"], du, f"dw_up{l}", "cols", D_MODEL, 2048)))
        pending.append(("w_out", l) + tuple(_dw(s["ym"], dh1b, f"dw_out{l}", "rows", D_MODEL, D_MODEL)))
        riders, sent = send_pending()
        (dz, dsink, dcw, dpv, dwa, dwx, dh, dgm, dg1), got = _mixer_bwd(
            dh1b, dh1, s["h0"], s["z"], s["ycat"], s["hl"], s["uc"], q["sink"], q["cw"], q["pv"], q["wa"], q["wx"],
            q["gmix"], gw["w_out"], q["g1"], gw["w_in"], f"mixer_bwd{l}", riders)
        record(sent, got)
        pending.append(("w_in", l) + tuple(_dw(s["hn1"], dz, f"dw_in{l}", "cols", D_MODEL, IN_W)))
        grads[l] = dict(
            norm1=dg1[0], attn_sinks=jnp.stack([dsink[0:4, 0], dsink[0:4, 2 * BLK]], axis=1).reshape(8),
            conv_dw_w=dcw[0:CONV_K], conv_dw_b=dpv[R_CONV_B], conv_ln_g=dpv[R_LN_G], conv_ln_b=dpv[R_LN_B],
            lru_conv_w=dpv[R_LCW:R_LCW + LRU_K], lru_conv_b=dpv[R_LCONV_B], lru_wa=_unblock_diag(dwa),
            lru_ba=dpv[R_BA].reshape(4, 64), lru_wx=_unblock_diag(dwx), lru_bx=dpv[R_BX].reshape(4, 64),
            lru_lambda=dpv[R_LAM], mix_norm=dgm[0], norm2=dg2[0])

    small = [jnp.stack([grads[l][n] for l in range(depth)]) for n in _SMALL] + [dgf, loss[:, 0:1]]
    sizes = [a.size for a in small]
    total = -(-sum(sizes) // 1024) * 1024
    packed = jnp.concatenate([a.reshape(-1) for a in small] + [jnp.zeros((total - sum(sizes),), F32)])
    riders, sent = send_pending()
    _, got = _call(None, "tail_exchange", None, [], [], [], [], [],
                   riders + [_bcast_rider([packed.reshape(total // 128, 128)])])
    record(sent, got)
    summed = _sum_parts(got[1][0], "sum_small_grads").reshape(-1)
    small_sums, pos = [], 0
    for a, size in zip(small, sizes):
        small_sums.append(summed[pos:pos + size].reshape(a.shape))
        pos += size

    out = {}
    for n in _BIG:
        res = [_adamw_shard(big[n][l][0], big[n][l][1], dev1, w[n][l], m[n][l], v[n][l], f"adamw_{n}{l}")
               for l in range(depth)]
        out[n] = [jnp.stack([res[l][j] for l in range(depth)]) for j in range(4)]
    shard = lambda a: lax.dynamic_slice_in_dim(a, dev * (a.shape[-1] // N_DEV), a.shape[-1] // N_DEV, axis=a.ndim - 1)
    flat = {"lru_wa": (depth, LRU_W, 64), "lru_wx": (depth, LRU_W, 64), "final_norm": (1, D_MODEL)}
    gs, ws, ms, vs = [], [], [], []
    for n, g in zip(_SMALL + ["final_norm"], small_sums[:-1]):
        shp = flat.get(n, w[n].shape)
        gs.append((shard(g) if n in ("conv_dw_w", "lru_conv_w") else g).reshape(shp))
        ws.append(w[n].reshape(shp))
        ms.append(m[n].reshape(shp))
        vs.append(v[n].reshape(shp))
    sd, sm, sv = _adamw_small(gs, ws, ms, vs, "adamw_small")
    for j, n in enumerate(_SMALL + ["final_norm"]):
        out[n] = [a.reshape(w[n].shape) for a in (gs[j], sd[j], sm[j], sv[j])]
    loss_total = small_sums[-1][0, 0]

    result = [loss_total, dh[None]]
    for j in range(4):
        result += [out[n][j] for n in _WEIGHTS]
    return tuple(result)
```

```python
import types

import jax
import jax.numpy as jnp
from jax import lax
from jax.experimental import pallas as pl
from jax.experimental.pallas import tpu as pltpu

F32 = jnp.float32
MX = jnp.bfloat16
WIRE = jnp.bfloat16

D_MODEL = 1024
HEAD_DIM = 64
ATTN_W = 512
KV_W = 128
BLK = 128
CONV_W = 256
CONV_K = 31
LRU_W = 256
LRU_K = 4
LRU_C = 8.0
IN_W = 1792
D_FF = 4096
FF_BLK = 512
N_DEV = 8
IN_SHARD = IN_W // N_DEV
RMS_EPS = 1e-6
LN_EPS = 1e-5
MASK_VALUE = -1e30
SCALE = HEAD_DIM ** -0.5
CONV_HALO = 32
LRU_HALO = 8
CONV_CHUNK = 64
POST_TILE = 512
DW_TILE = 1024
Q0, K0, V0, CV0, CG0, RX0, RG0 = 0, 512, 640, 768, 1024, 1280, 1536
R_CONV_B, R_LN_G, R_LN_B, R_LCONV_B, R_BA, R_BX, R_LAM, R_LCW = 0, 1, 2, 3, 4, 5, 6, 8

ADAM_LR, ADAM_B1, ADAM_B2, ADAM_EPS, ADAM_WD, ADAM_STEP = 0.001, 0.9, 0.999, 1e-08, 0.01, 10

VMEM_LIMIT = 56 * 1024 * 1024
MESH = pl.DeviceIdType.MESH
ANY = pl.BlockSpec(memory_space=pl.ANY)


def _tile(t, cap=512):
    return min(cap, t)


def _dot(a, b):
    return jnp.dot(a.astype(MX), b.astype(MX), preferred_element_type=F32)


def _dot_nt(a, b):
    return lax.dot_general(a.astype(MX), b.astype(MX), (((1,), (1,)), ((), ())), preferred_element_type=F32)


def _dot_tn(a, b):
    return lax.dot_general(a.astype(MX), b.astype(MX), (((0,), (0,)), ((), ())), preferred_element_type=F32)


def _const_spec(shape):
    nd = len(shape)
    return pl.BlockSpec(shape, lambda *_: (0,) * nd, pipeline_mode=pl.Buffered(1))


def _acc_spec(shape):
    nd = len(shape)
    return pl.BlockSpec(shape, lambda *_: (0,) * nd)


def _sds(shape, dtype):
    return jax.ShapeDtypeStruct(shape, dtype)


def _sigmoid(x):
    return jax.nn.sigmoid(x)


def _rms_fwd(x, g):
    r = lax.rsqrt(jnp.mean(x * x, axis=-1, keepdims=True) + RMS_EPS)
    xh = x * r
    return xh * g, xh, r


def _rms_bwd(dy, xh, r, g):
    t = dy * g
    dx = r * (t - xh * jnp.mean(t * xh, axis=-1, keepdims=True))
    return dx, jnp.sum(dy * xh, axis=0, keepdims=True)


_GROUPS = ((0, 512), (512, 768), (768, 1024))


def _group_rms_fwd(y, g):
    parts = [_rms_fwd(y[:, a:b], g[:, a:b]) for a, b in _GROUPS]
    return (jnp.concatenate([p[0] for p in parts], axis=1),
            jnp.concatenate([p[1] for p in parts], axis=1),
            [p[2] for p in parts])


def _gelu(x):
    c = 0.7978845608028654
    u = c * (x + 0.044715 * x * x * x)
    th = jnp.tanh(u)
    val = 0.5 * x * (1.0 + th)
    grad = 0.5 * (1.0 + th) + 0.5 * x * (1.0 - th * th) * c * (1.0 + 3.0 * 0.044715 * x * x)
    return val, grad


def _neg_expm1(x):
    series = -x * (1.0 + x * (0.5 + x * (1.0 / 6.0 + x * (1.0 / 24.0))))
    return jnp.where(x > -0.02, series, 1.0 - jnp.exp(x))


def _me():
    return lax.axis_index("x"), lax.axis_index("y"), lax.axis_index("c")


def _gather_rider(arrays):
    arrays = list(arrays)
    n = len(arrays)

    def plan(ins, outs, sems):
        ssem, rsem, lsem = sems
        x, y, c = _me()
        chips = [(1 - x, y), (x, 1 - y), (1 - x, 1 - y)]

        def copy(a, k, block, to, own=False):
            dst = outs[a].at[4 * block[0] + 2 * block[1] + block[2]]
            return pltpu.make_async_remote_copy(
                src_ref=ins[a] if own else dst, dst_ref=dst, send_sem=ssem.at[7 * a + k],
                recv_sem=rsem.at[7 * a + k], device_id=to, device_id_type=MESH)

        return x, y, c, chips, copy, lsem

    def start(ins, outs, sems):
        x, y, c, chips, copy, lsem = plan(ins, outs, sems)
        for a in range(n):
            pltpu.make_async_copy(ins[a], outs[a].at[4 * x + 2 * y + c], lsem.at[a]).start()
            copy(a, 0, (x, y, c), (x, y, 1 - c), own=True).start()
            for j, chip in enumerate(chips):
                copy(a, 1 + j, (x, y, c), (*chip, c), own=True).start()

    def mid(ins, outs, sems):
        x, y, c, chips, copy, _ = plan(ins, outs, sems)
        for a in range(n):
            for j, chip in enumerate(chips):
                copy(a, 1 + j, (*chip, c), (x, y, c)).wait_recv()
                copy(a, 4 + j, (*chip, c), (x, y, 1 - c)).start()

    def finish(ins, outs, sems):
        x, y, c, chips, copy, lsem = plan(ins, outs, sems)
        for a in range(n):
            copy(a, 0, (x, y, 1 - c), (x, y, c)).wait_recv()
            for j, chip in enumerate(chips):
                copy(a, 4 + j, (*chip, 1 - c), (x, y, c)).wait_recv()
        for a in range(n):
            copy(a, 0, (x, y, c), (x, y, 1 - c), own=True).wait_send()
            for j, chip in enumerate(chips):
                copy(a, 1 + j, (x, y, c), (*chip, c), own=True).wait_send()
                copy(a, 4 + j, (*chip, c), (x, y, 1 - c)).wait_send()
            pltpu.make_async_copy(ins[a], outs[a].at[4 * x + 2 * y + c], lsem.at[a]).wait()

    return types.SimpleNamespace(
        arrays=arrays, out_shape=[_sds((N_DEV,) + a.shape, a.dtype) for a in arrays],
        scratch=[pltpu.SemaphoreType.DMA((7 * n,)), pltpu.SemaphoreType.DMA((7 * n,)), pltpu.SemaphoreType.DMA((n,))],
        start=start, mid=mid, finish=finish)


def _bcast_rider(arrays):
    arrays = list(arrays)
    n = len(arrays)

    def copies(ins, outs, sems, landing):
        ssem, rsem, lsem = sems
        x, y, c = _me()
        out = []
        for a in range(n):
            out.append(pltpu.make_async_copy(ins[a], outs[a].at[4 * x + 2 * y + c], lsem.at[a]))
            for f in range(1, N_DEV):
                px = 1 - x if f & 4 else x
                py = 1 - y if f & 2 else y
                pc = 1 - c if f & 1 else c
                slot = 4 * px + 2 * py + pc if landing else 4 * x + 2 * y + c
                out.append(pltpu.make_async_remote_copy(
                    src_ref=ins[a], dst_ref=outs[a].at[slot], send_sem=ssem.at[7 * a + f - 1],
                    recv_sem=rsem.at[7 * a + f - 1], device_id=(px, py, pc), device_id_type=MESH))
        return out

    def start(ins, outs, sems):
        for cp in copies(ins, outs, sems, landing=False):
            cp.start()

    def finish(ins, outs, sems):
        for cp in copies(ins, outs, sems, landing=True):
            cp.wait()

    return types.SimpleNamespace(
        arrays=arrays, out_shape=[_sds((N_DEV,) + a.shape, a.dtype) for a in arrays],
        scratch=[pltpu.SemaphoreType.DMA((7 * n,)), pltpu.SemaphoreType.DMA((7 * n,)), pltpu.SemaphoreType.DMA((n,))],
        start=start, mid=None, finish=finish)


def _scatter_rider(arrays):
    arrays = list(arrays)
    n = len(arrays)

    def copies(ins, outs, sems):
        ssem, rsem = sems
        x, y, c = _me()
        out = []
        for a in range(n):
            for f in range(1, N_DEV):
                px = 1 - x if f & 4 else x
                py = 1 - y if f & 2 else y
                pc = 1 - c if f & 1 else c
                out.append(pltpu.make_async_remote_copy(
                    src_ref=ins[a].at[4 * px + 2 * py + pc], dst_ref=outs[a].at[f - 1], send_sem=ssem.at[7 * a + f - 1],
                    recv_sem=rsem.at[7 * a + f - 1], device_id=(px, py, pc), device_id_type=MESH))
        return out

    def start(ins, outs, sems):
        for cp in copies(ins, outs, sems):
            cp.start()

    def finish(ins, outs, sems):
        for cp in copies(ins, outs, sems):
            cp.wait()

    return types.SimpleNamespace(
        arrays=arrays, out_shape=[_sds((N_DEV - 1,) + a.shape[1:], a.dtype) for a in arrays],
        scratch=[pltpu.SemaphoreType.DMA((7 * n,)), pltpu.SemaphoreType.DMA((7 * n,))],
        start=start, mid=None, finish=finish)


def _call(body, name, grid, in_specs, out_specs, out_shape, scratch, operands, riders=()):
    n_in, n_out, n_scr = len(operands), len(out_shape), len(scratch)
    nsteps = grid[0] if grid else 1
    sizes = [(len(r.arrays), len(r.out_shape), len(r.scratch)) for r in riders]

    def wrapped(*refs):
        pos = n_in
        r_ins = []
        for ri, _, _ in sizes:
            r_ins.append(refs[pos:pos + ri])
            pos += ri
        outs = refs[pos:pos + n_out]
        pos += n_out
        r_outs = []
        for _, ro, _ in sizes:
            r_outs.append(refs[pos:pos + ro])
            pos += ro
        scr = refs[pos:pos + n_scr]
        pos += n_scr
        r_sems = []
        for _, _, rs in sizes:
            r_sems.append(refs[pos:pos + rs])
            pos += rs
        step = pl.program_id(0) if grid else 0

        def at(s, fn):
            if grid:
                pl.when(step == s)(fn)
            else:
                fn()

        for r, a, b, c in zip(riders, r_ins, r_outs, r_sems):
            at(0, lambda r=r, a=a, b=b, c=c: r.start(a, b, c))
        for r, a, b, c in zip(riders, r_ins, r_outs, r_sems):
            if r.mid is not None:
                at((3 * nsteps) // 4, lambda r=r, a=a, b=b, c=c: r.mid(a, b, c))
        if body is not None:
            body(*refs[:n_in], *outs, *scr)
        for r, a, b, c in zip(riders, r_ins, r_outs, r_sems):
            at(nsteps - 1, lambda r=r, a=a, b=b, c=c: r.finish(a, b, c))

    r_arrays = [a for r in riders for a in r.arrays]
    r_shapes = [s for r in riders for s in r.out_shape]
    kwargs = {}
    if grid:
        kwargs = dict(grid=grid, compiler_params=pltpu.CompilerParams(
            dimension_semantics=("arbitrary",) * len(grid), vmem_limit_bytes=VMEM_LIMIT))
    res = pl.pallas_call(
        wrapped, name=name,
        in_specs=list(in_specs) + [ANY] * len(r_arrays),
        out_specs=list(out_specs) + [ANY] * len(r_shapes),
        out_shape=list(out_shape) + r_shapes,
        scratch_shapes=list(scratch) + [s for r in riders for s in r.scratch],
        **kwargs,
    )(*operands, *r_arrays)
    host, rest = res[:n_out], res[n_out:]
    r_res = []
    for _, ro, _ in sizes:
        r_res.append(rest[:ro])
        rest = rest[ro:]
    return host, r_res


def _ln_in(h, g1, w_in, name):
    t = h.shape[0]
    tm = _tile(t)

    def body(h_ref, g_ref, w_ref, z_ref, hn_ref):
        y, _, _ = _rms_fwd(h_ref[...], g_ref[...])
        hn = y.astype(MX)
        hn_ref[...] = hn
        z_ref[...] = jnp.dot(hn, w_ref[...], preferred_element_type=F32)

    tile = lambda w: pl.BlockSpec((tm, w), lambda i: (i, 0))
    (z, hn), _ = _call(
        body, name, (t // tm,),
        [tile(D_MODEL), _const_spec((1, D_MODEL)), _const_spec((D_MODEL, IN_W))],
        [tile(IN_W), tile(D_MODEL)], [_sds((t, IN_W), F32), _sds((t, D_MODEL), MX)], [], [h, g1, w_in])
    return z, hn


def _band2(kb, g):
    lo = lax.broadcasted_iota(jnp.int32, kb.shape, 1) < HEAD_DIM
    kr = pltpu.roll(kb, HEAD_DIM, 1)
    if g == 0:
        top, bot = jnp.where(lo, kb, 0.0), jnp.where(lo, 0.0, kr)
    else:
        top, bot = jnp.where(lo, kr, 0.0), jnp.where(lo, 0.0, kb)
    return jnp.concatenate([top, bot], axis=0)


def _attn_block(z_ref, zh_ref, sink_ref, b, first):
    rows = slice(b * BLK, (b + 1) * BLK)
    prev = zh_ref if b == 0 else z_ref
    prow = slice(0, BLK) if b == 0 else slice((b - 1) * BLK, b * BLK)
    kb = jnp.concatenate([prev[prow, K0:K0 + KV_W], z_ref[rows, K0:K0 + KV_W]], axis=0)
    vb = jnp.concatenate([prev[prow, V0:V0 + KV_W], z_ref[rows, V0:V0 + KV_W]], axis=0)
    k2 = [_band2(kb, g) for g in range(2)]
    v2 = [_band2(vb, g) for g in range(2)]
    q2 = [jnp.concatenate([z_ref[rows, (2 * g) * BLK:(2 * g + 1) * BLK], z_ref[rows, (2 * g + 1) * BLK:(2 * g + 2) * BLK]],
                          axis=0) for g in range(2)]
    rr = lax.broadcasted_iota(jnp.int32, (4 * BLK, 2 * BLK), 0) & (BLK - 1)
    cc = lax.broadcasted_iota(jnp.int32, (4 * BLK, 2 * BLK), 1)
    first_block = jnp.logical_and(first, b == 0).astype(jnp.int32)
    mask = jnp.logical_and(jnp.logical_and(cc > rr, cc <= rr + BLK), cc >= BLK * first_block)
    s = jnp.concatenate([_dot_nt(q2[g], k2[g]) for g in range(2)], axis=0) * SCALE
    w = 2 * BLK
    out, psink = [], []
    for hh in range(2):
        sh = jnp.where(mask, s[:, hh * w:(hh + 1) * w], MASK_VALUE)
        sk = jnp.concatenate([jnp.broadcast_to(sink_ref[p:p + 1, hh * w:hh * w + 1], (BLK, 1)) for p in range(4)], axis=0)
        m = jnp.maximum(jnp.max(sh, axis=1, keepdims=True), sk)
        p = jnp.exp(sh - m)
        es = jnp.exp(sk - m)
        inv = 1.0 / (jnp.sum(p, axis=1, keepdims=True) + es)
        out.append(p * inv)
        psink.append(es * inv)
    return q2, k2, v2, jnp.concatenate(out, axis=1), psink


def _scan_steps(a, b, n, span, reverse):
    pos = lax.broadcasted_iota(jnp.int32, a.shape, 0) & (span - 1)
    d = 1
    while d < span:
        keep = pos < span - d if reverse else pos >= d
        shift = n - d if reverse else d
        a_sh = jnp.where(keep, pltpu.roll(a, shift, 0), 1.0)
        b_sh = jnp.where(keep, pltpu.roll(b, shift, 0), 0.0)
        b = a * b_sh + b
        a = a * a_sh
        d *= 2
    return a, b


def _scan(a, b, tm, reverse):
    w = a.shape[1]
    ng = tm // 8
    a, b = _scan_steps(a, b, tm, 8, reverse)
    edge = 0 if reverse else 7
    ga, gb = _scan_steps(a.reshape(ng, 8, w)[:, edge, :], b.reshape(ng, 8, w)[:, edge, :], ng, ng, reverse)
    gpos = lax.broadcasted_iota(jnp.int32, ga.shape, 0)
    inner = gpos < ng - 1 if reverse else gpos >= 1
    shift = ng - 1 if reverse else 1
    pa = jnp.where(inner, pltpu.roll(ga, shift, 0), 1.0)
    pb = jnp.where(inner, pltpu.roll(gb, shift, 0), 0.0)
    expand = lambda v: jnp.broadcast_to(v[:, None, :], (ng, 8, w)).reshape(tm, w)
    return a * expand(pa), b + a * expand(pb)


def _shifted_copies(ext, shifts, tm):
    rows = tm + CONV_HALO - 8
    for r in range(1, 8):
        shifts[r - 1, 0:rows, :] = ext[pl.ds(r, rows), :]


def _tap(ext, shifts, off, r0, n):
    a, r = divmod(off, 8)
    lo = 8 * a + r0
    if r == 0:
        return ext[lo:lo + n, :]
    return shifts[r - 1, lo:lo + n, :]


def _glu_fill(z_ref, zh_ref, uext, ush, first, tm, sg_out=None):
    cv = z_ref[:, CV0:CV0 + CONV_W]
    sg = _sigmoid(z_ref[:, CG0:CG0 + CONV_W])
    if sg_out is not None:
        sg_out[...] = sg
    hrow = BLK - CONV_HALO
    uh = zh_ref[hrow:BLK, CV0:CV0 + CONV_W] * _sigmoid(zh_ref[hrow:BLK, CG0:CG0 + CONV_W])
    uext[0:CONV_HALO, :] = jnp.where(first, 0.0, uh)
    uext[CONV_HALO:CONV_HALO + tm, :] = cv * sg
    _shifted_copies(uext, ush, tm)


def _conv_taps(cw_ref, pv_ref, uext, ush, out_ref, tm):
    for r0 in range(0, tm, CONV_CHUNK):
        acc = jnp.broadcast_to(pv_ref[R_CONV_B:R_CONV_B + 1, :], (CONV_CHUNK, CONV_W))
        for k in range(CONV_K):
            acc = acc + cw_ref[k:k + 1, :] * _tap(uext, ush, CONV_HALO - (CONV_K - 1) + k, r0, CONV_CHUNK)
        out_ref[r0:r0 + CONV_CHUNK, :] = acc


def _ln_silu(uc, pv_ref):
    mu = jnp.mean(uc, axis=-1, keepdims=True)
    xc = uc - mu
    rs = lax.rsqrt(jnp.mean(xc * xc, axis=-1, keepdims=True) + LN_EPS)
    xh = xc * rs
    ln = xh * pv_ref[R_LN_G:R_LN_G + 1, :] + pv_ref[R_LN_B:R_LN_B + 1, :]
    sg = _sigmoid(ln)
    return xh, rs, ln, sg


def _lru_gates(z_ref, zh_ref, pv_ref, wa_ref, wx_ref, rxext, first, tm):
    rxext[0:LRU_HALO, :] = jnp.where(first, 0.0, zh_ref[BLK - LRU_HALO:BLK, RX0:RX0 + LRU_W])
    rxext[LRU_HALO:LRU_HALO + tm, :] = z_ref[:, RX0:RX0 + LRU_W]
    xc = jnp.broadcast_to(pv_ref[R_LCONV_B:R_LCONV_B + 1, :], (tm, LRU_W))
    for k in range(LRU_K):
        xc = xc + pv_ref[R_LCW + k:R_LCW + k + 1, :] * rxext[pl.ds(LRU_HALO - (LRU_K - 1) + k, tm), :]
    r = _sigmoid(_dot(xc, wa_ref[...]) + pv_ref[R_BA:R_BA + 1, :])
    ig = _sigmoid(_dot(xc, wx_ref[...]) + pv_ref[R_BX:R_BX + 1, :])
    lam = pv_ref[R_LAM:R_LAM + 1, :]
    sp = jnp.log1p(jnp.exp(-lam))
    la = (-LRU_C * r) * sp
    a = jnp.exp(la)
    mult = jnp.sqrt(_neg_expm1(2.0 * la))
    return xc, r, ig, sp, la, a, mult


def _mixer_in_specs(tm, tile_of):
    hb = tm // BLK
    return [
        pl.BlockSpec((tm, IN_W), lambda i: (tile_of(i), 0)),
        pl.BlockSpec((BLK, IN_W), lambda i: (jnp.maximum(tile_of(i) * hb - 1, 0), 0)),
        _const_spec((8, 4 * BLK)),
        _const_spec((32, CONV_W)),
        _const_spec((16, CONV_W)),
        _const_spec((LRU_W, LRU_W)),
        _const_spec((LRU_W, LRU_W)),
    ]


def _mixer_fwd(z, sink, cw, pv, wa, wx, name, riders=()):
    t = z.shape[0]
    tm = _tile(t)
    nb = tm // BLK

    def body(z_ref, zh_ref, sink_ref, cw_ref, pv_ref, wa_ref, wx_ref, y_ref, hl_ref, uc_ref, uext, ush, rxext, hcar):
        i = pl.program_id(0)
        first = i == 0

        @pl.when(first)
        def _():
            hcar[...] = jnp.zeros_like(hcar)

        for b in range(nb):
            rows = slice(b * BLK, (b + 1) * BLK)
            _, _, v2, prob, _ = _attn_block(z_ref, zh_ref, sink_ref, b, first)
            for g in range(2):
                o = _dot(prob[2 * g * BLK:(2 * g + 2) * BLK], v2[g])
                y_ref[rows, (2 * g) * BLK:(2 * g + 1) * BLK] = o[0:BLK]
                y_ref[rows, (2 * g + 1) * BLK:(2 * g + 2) * BLK] = o[BLK:2 * BLK]
        _glu_fill(z_ref, zh_ref, uext, ush, first, tm)
        _conv_taps(cw_ref, pv_ref, uext, ush, uc_ref, tm)
        _, _, ln, sg = _ln_silu(uc_ref[...], pv_ref)
        y_ref[:, ATTN_W:ATTN_W + CONV_W] = ln * sg
        xc, _, ig, _, _, a, mult = _lru_gates(z_ref, zh_ref, pv_ref, wa_ref, wx_ref, rxext, first, tm)
        acum, h = _scan(a, mult * (ig * xc), tm, reverse=False)
        h = h + acum * hcar[0:1, :]
        hl_ref[...] = h
        hcar[0:1, :] = h[tm - 1:tm, :]
        gl, _ = _gelu(z_ref[:, RG0:RG0 + LRU_W])
        y_ref[:, ATTN_W + CONV_W:ATTN_W + CONV_W + LRU_W] = h * gl

    tile = lambda w: pl.BlockSpec((tm, w), lambda i: (i, 0))
    return _call(
        body, name, (t // tm,), _mixer_in_specs(tm, lambda i: i),
        [tile(D_MODEL), tile(LRU_W), tile(CONV_W)],
        [_sds((t, D_MODEL), F32), _sds((t, LRU_W), F32), _sds((t, CONV_W), F32)],
        [pltpu.VMEM((tm + CONV_HALO, CONV_W), F32), pltpu.VMEM((7, tm + CONV_HALO - 8, CONV_W), F32),
         pltpu.VMEM((tm + LRU_HALO, LRU_W), F32), pltpu.VMEM((8, LRU_W), F32)],
        [z, z, sink, cw, pv, wa, wx], riders)


def _mixer_bwd(dy, z, ycat, hl, uc, sink, cw, pv, wa, wx, name, riders=()):
    t = z.shape[0]
    tm = _tile(t)
    nt = t // tm
    nb = tm // BLK
    rev = lambda i: nt - 1 - i

    def body(dy_ref, z_ref, zh_ref, sink_ref, cw_ref, pv_ref, wa_ref, wx_ref, y_ref, hl_ref, hlh_ref, uc_ref,
             dz_ref, dsink_ref, dcw_ref, dpv_ref, dwa_ref, dwx_ref,
             uext, ush, sgs, rxext, dkext, dvext, ducext, dsh, dcw8, dxcext, kcar, vcar, uccar, xccar, gcar):
        i = pl.program_id(0)
        first = i == nt - 1

        @pl.when(i == 0)
        def _():
            for car in (kcar, vcar, uccar, xccar, gcar, dcw8):
                car[...] = jnp.zeros_like(car)
            for acc in (dsink_ref, dpv_ref, dwa_ref, dwx_ref):
                acc[...] = jnp.zeros_like(acc)

        def addrow(r, val):
            dpv_ref[r:r + 1, :] += jnp.sum(val, axis=0, keepdims=True)

        dkext[:, 0:tm] = jnp.zeros((KV_W, tm), F32)
        dvext[:, 0:tm] = jnp.zeros((KV_W, tm), F32)
        dkext[:, tm:tm + BLK] = kcar[...]
        dvext[:, tm:tm + BLK] = vcar[...]
        lane512 = lax.broadcasted_iota(jnp.int32, (1, 4 * BLK), 1) < 2 * BLK
        lo = lax.broadcasted_iota(jnp.int32, (4 * BLK, BLK), 1) < HEAD_DIM
        hd, w2 = HEAD_DIM, 2 * BLK
        for b in range(nb):
            rows = slice(b * BLK, (b + 1) * BLK)
            band = slice(b * BLK, (b + 2) * BLK)
            q2, k2, v2, prob, psink = _attn_block(z_ref, zh_ref, sink_ref, b, first)
            stack = lambda ref: jnp.concatenate([ref[rows, p * BLK:(p + 1) * BLK] for p in range(4)], axis=0)
            do4 = stack(dy_ref)
            dlt = do4 * stack(y_ref)
            d0 = jnp.sum(jnp.where(lo, dlt, 0.0), axis=1, keepdims=True)
            d1 = jnp.sum(jnp.where(lo, 0.0, dlt), axis=1, keepdims=True)
            dp = jnp.concatenate([_dot_nt(do4[g * w2:(g + 1) * w2], v2[g]) for g in range(2)], axis=0)
            dl = jnp.concatenate([jnp.broadcast_to(d0, (4 * BLK, w2)), jnp.broadcast_to(d1, (4 * BLK, w2))], axis=1)
            draw = (prob * (dp - dl)) * SCALE
            e0, e1 = psink[0] * d0, psink[1] * d1
            for p in range(4):
                prs = slice(p * BLK, (p + 1) * BLK)
                s0 = jnp.sum(e0[prs], axis=0, keepdims=True)
                s1 = jnp.sum(e1[prs], axis=0, keepdims=True)
                dsink_ref[p:p + 1, :] += -jnp.where(lane512, s0, s1)
            for g in range(2):
                grs = slice(g * w2, (g + 1) * w2)
                dq = _dot(draw[grs], k2[g])
                dz_ref[rows, (2 * g) * BLK:(2 * g + 1) * BLK] = dq[0:BLK].astype(dz_ref.dtype)
                dz_ref[rows, (2 * g + 1) * BLK:(2 * g + 2) * BLK] = dq[BLK:2 * BLK].astype(dz_ref.dtype)
                tk = _dot_tn(q2[g], draw[grs])
                tv = _dot_tn(do4[grs], prob[grs])
                dkext[g * hd:(g + 1) * hd, band] += tk[0:hd, 0:w2] + tk[hd:2 * hd, w2:2 * w2]
                dvext[g * hd:(g + 1) * hd, band] += tv[0:hd, 0:w2] + tv[hd:2 * hd, w2:2 * w2]
        dz_ref[:, K0:K0 + KV_W] = jnp.transpose(dkext[:, BLK:BLK + tm]).astype(dz_ref.dtype)
        dz_ref[:, V0:V0 + KV_W] = jnp.transpose(dvext[:, BLK:BLK + tm]).astype(dz_ref.dtype)
        kcar[...] = dkext[:, 0:BLK]
        vcar[...] = dvext[:, 0:BLK]

        _glu_fill(z_ref, zh_ref, uext, ush, first, tm, sg_out=sgs)
        xh, rs, ln, sg = _ln_silu(uc_ref[...], pv_ref)
        dln = dy_ref[:, ATTN_W:ATTN_W + CONV_W] * (sg * (1.0 + ln * (1.0 - sg)))
        addrow(R_LN_G, dln * xh)
        addrow(R_LN_B, dln)
        dxh = dln * pv_ref[R_LN_G:R_LN_G + 1, :]
        duc = rs * (dxh - jnp.mean(dxh, axis=-1, keepdims=True) - xh * jnp.mean(dxh * xh, axis=-1, keepdims=True))
        addrow(R_CONV_B, duc)
        ducext[0:tm, :] = duc
        ducext[tm:tm + CONV_HALO, :] = uccar[...]
        uccar[...] = duc[0:CONV_HALO, :]
        _shifted_copies(ducext, dsh, tm)
        for r0 in range(0, tm, CONV_CHUNK):
            crow = slice(r0, r0 + CONV_CHUNK)
            duc_c = ducext[crow, :]
            du = jnp.zeros((CONV_CHUNK, CONV_W), F32)
            for k in range(CONV_K):
                prod = duc_c * _tap(uext, ush, CONV_HALO - (CONV_K - 1) + k, r0, CONV_CHUNK)
                part = prod[0:8]
                for s in range(8, CONV_CHUNK, 8):
                    part = part + prod[s:s + 8]
                dcw8[k] += part
                du = du + cw_ref[k:k + 1, :] * _tap(ducext, dsh, CONV_K - 1 - k, r0, CONV_CHUNK)
            sgc = sgs[crow, :]
            dz_ref[crow, CV0:CV0 + CONV_W] = (du * sgc).astype(dz_ref.dtype)
            u_c = uext[CONV_HALO + r0:CONV_HALO + r0 + CONV_CHUNK, :]
            dz_ref[crow, CG0:CG0 + CONV_W] = (du * u_c * (1.0 - sgc)).astype(dz_ref.dtype)

        @pl.when(i == nt - 1)
        def _():
            dcw_ref[...] = jnp.sum(dcw8[...], axis=1)

        xc, r, ig, sp, la, a, mult = _lru_gates(z_ref, zh_ref, pv_ref, wa_ref, wx_ref, rxext, first, tm)
        h = hl_ref[...]
        rowi = lax.broadcasted_iota(jnp.int32, (tm, LRU_W), 0)
        hlast = jnp.where(first, 0.0, hlh_ref[7:8, :])
        hprev = jnp.where(rowi == 0, hlast, pltpu.roll(h, 1, 0))
        dyl = dy_ref[:, ATTN_W + CONV_W:ATTN_W + CONV_W + LRU_W]
        gl, dgl = _gelu(z_ref[:, RG0:RG0 + LRU_W])
        dz_ref[:, RG0:RG0 + LRU_W] = (dyl * h * dgl).astype(dz_ref.dtype)
        dh = dyl * gl + jnp.where(rowi == tm - 1, gcar[0:1, :], 0.0)
        c = jnp.where(rowi == tm - 1, 0.0, pltpu.roll(a, tm - 1, 0))
        _, gg = _scan(c, dh, tm, reverse=True)
        gcar[0:1, :] = a[0:1, :] * gg[0:1, :]
        dmult = gg * (ig * xc)
        dig = gg * mult * xc
        dxc = gg * mult * ig
        dla = gg * hprev * a - dmult * a * a / mult
        dr = dla * (-LRU_C * sp)
        lam = pv_ref[R_LAM:R_LAM + 1, :]
        dpv_ref[R_LAM:R_LAM + 1, :] += jnp.sum(dla * (-LRU_C * r), axis=0, keepdims=True) * (-_sigmoid(-lam))
        dpa = dr * r * (1.0 - r)
        dpx = dig * ig * (1.0 - ig)
        addrow(R_BA, dpa)
        addrow(R_BX, dpx)
        dxc = dxc + _dot_nt(dpa, wa_ref[...]) + _dot_nt(dpx, wx_ref[...])
        dwa_ref[...] += _dot_tn(xc, dpa)
        dwx_ref[...] += _dot_tn(xc, dpx)
        addrow(R_LCONV_B, dxc)
        dxcext[0:tm, :] = dxc
        dxcext[tm:tm + LRU_HALO, :] = xccar[...]
        xccar[...] = dxc[0:LRU_HALO, :]
        drx = jnp.zeros((tm, LRU_W), F32)
        for k in range(LRU_K):
            addrow(R_LCW + k, dxc * rxext[pl.ds(LRU_HALO - (LRU_K - 1) + k, tm), :])
            drx = drx + pv_ref[R_LCW + k:R_LCW + k + 1, :] * dxcext[pl.ds(LRU_K - 1 - k, tm), :]
        dz_ref[:, RX0:RX0 + LRU_W] = drx.astype(dz_ref.dtype)

    tile = lambda w: pl.BlockSpec((tm, w), lambda i: (rev(i), 0))
    in_specs = [tile(D_MODEL)] + _mixer_in_specs(tm, rev) + [
        tile(D_MODEL), tile(LRU_W),
        pl.BlockSpec((8, LRU_W), lambda i: (jnp.maximum(rev(i) * (tm // 8) - 1, 0), 0)),
        tile(CONV_W)]
    return _call(
        body, name, (nt,), in_specs,
        [tile(IN_W), _acc_spec((8, 4 * BLK)), _acc_spec((32, CONV_W)), _acc_spec((16, CONV_W)),
         _acc_spec((LRU_W, LRU_W)), _acc_spec((LRU_W, LRU_W))],
        [_sds((t, IN_W), MX), _sds((8, 4 * BLK), F32), _sds((32, CONV_W), F32), _sds((16, CONV_W), F32),
         _sds((LRU_W, LRU_W), F32), _sds((LRU_W, LRU_W), F32)],
        [pltpu.VMEM((tm + CONV_HALO, CONV_W), F32), pltpu.VMEM((7, tm + CONV_HALO - 8, CONV_W), F32),
         pltpu.VMEM((tm, CONV_W), F32), pltpu.VMEM((tm + LRU_HALO, LRU_W), F32),
         pltpu.VMEM((KV_W, tm + BLK), F32), pltpu.VMEM((KV_W, tm + BLK), F32),
         pltpu.VMEM((tm + CONV_HALO, CONV_W), F32), pltpu.VMEM((7, tm + CONV_HALO - 8, CONV_W), F32),
         pltpu.VMEM((32, 8, CONV_W), F32), pltpu.VMEM((tm + LRU_HALO, LRU_W), F32),
         pltpu.VMEM((KV_W, BLK), F32), pltpu.VMEM((KV_W, BLK), F32),
         pltpu.VMEM((CONV_HALO, CONV_W), F32), pltpu.VMEM((LRU_HALO, LRU_W), F32), pltpu.VMEM((8, LRU_W), F32)],
        [dy, z, z, sink, cw, pv, wa, wx, ycat, hl, hl, uc], riders)


def _post_fwd(ycat, h0, gmix, w_out, g2, w_up, w_down, name, riders=()):
    t = h0.shape[0]
    tm = _tile(t, POST_TILE)
    nj = D_FF // FF_BLK

    def body(y_ref, h_ref, gm_ref, wo_ref, g2_ref, wu_ref, wd_ref, h1_ref, a_ref, h2_ref, ym_ref, hn_ref):
        ym, _, _ = _group_rms_fwd(y_ref[...], gm_ref[...])
        ym = ym.astype(MX)
        ym_ref[...] = ym
        h1 = h_ref[...] + jnp.dot(ym, wo_ref[...], preferred_element_type=F32)
        h1_ref[...] = h1
        hn, _, _ = _rms_fwd(h1, g2_ref[...])
        hn = hn.astype(MX)
        hn_ref[...] = hn
        for j in range(nj):
            u = jnp.dot(hn, wu_ref[j], preferred_element_type=F32)
            a_ref[:, j * FF_BLK:(j + 1) * FF_BLK] = jnp.square(jnp.maximum(u, 0.0)).astype(MX)
        h2_ref[...] = h1 + jnp.dot(a_ref[...], wd_ref[...], preferred_element_type=F32)

    tile = lambda w: pl.BlockSpec((tm, w), lambda i: (i, 0))
    return _call(
        body, name, (t // tm,),
        [tile(D_MODEL), tile(D_MODEL), _const_spec((1, D_MODEL)), _const_spec((D_MODEL, D_MODEL)),
         _const_spec((1, D_MODEL)), _const_spec((nj, D_MODEL, FF_BLK)), _const_spec((D_FF, D_MODEL))],
        [tile(D_MODEL), tile(D_FF), tile(D_MODEL), tile(D_MODEL), tile(D_MODEL)],
        [_sds((t, D_MODEL), F32), _sds((t, D_FF), MX), _sds((t, D_MODEL), F32), _sds((t, D_MODEL), MX),
         _sds((t, D_MODEL), MX)],
        [], [ycat, h0, gmix, w_out, g2, w_up, w_down], riders)


def _ffn_bwd(dh2, act, h1, g2, w_up_t, w_down, name, riders=()):
    t = h1.shape[0]
    tm = _tile(t, POST_TILE)
    nj = D_FF // FF_BLK

    def body(dh2_ref, a_ref, h1_ref, g2_ref, wut_ref, wd_ref, dh1_ref, dh1b_ref, dh2b_ref, du_ref, dg2_ref):
        @pl.when(pl.program_id(0) == 0)
        def _():
            dg2_ref[...] = jnp.zeros_like(dg2_ref)

        dh2 = dh2_ref[...]
        dh2b = dh2.astype(MX)
        dh2b_ref[...] = dh2b
        for j in range(nj):
            cols = slice(j * FF_BLK, (j + 1) * FF_BLK)
            da = _dot_nt(dh2b, wd_ref[j])
            du_ref[:, cols] = (da * (2.0 * jnp.sqrt(a_ref[:, cols].astype(F32)))).astype(MX)
        dhn = jnp.dot(du_ref[...], wut_ref[...], preferred_element_type=F32)
        _, xh, r = _rms_fwd(h1_ref[...], g2_ref[...])
        dx, dg = _rms_bwd(dhn, xh, r, g2_ref[...])
        dg2_ref[...] += dg
        dh1 = dh2 + dx
        dh1_ref[...] = dh1
        dh1b_ref[...] = dh1.astype(MX)

    tile = lambda w: pl.BlockSpec((tm, w), lambda i: (i, 0))
    return _call(
        body, name, (t // tm,),
        [tile(D_MODEL), tile(D_FF), tile(D_MODEL), _const_spec((1, D_MODEL)),
         _const_spec((D_FF, D_MODEL)), _const_spec((nj, FF_BLK, D_MODEL))],
        [tile(D_MODEL), tile(D_MODEL), tile(D_MODEL), tile(D_FF), _acc_spec((1, D_MODEL))],
        [_sds((t, D_MODEL), F32), _sds((t, D_MODEL), MX), _sds((t, D_MODEL), MX), _sds((t, D_FF), MX),
         _sds((1, D_MODEL), F32)],
        [], [dh2, act, h1, g2, w_up_t, w_down], riders)


def _mix_bwd(dh1, ycat, gmix, w_out, name):
    t = dh1.shape[0]
    tm = _tile(t)

    def body(dh1_ref, y_ref, gm_ref, wo_ref, dy_ref, dgm_ref):
        @pl.when(pl.program_id(0) == 0)
        def _():
            dgm_ref[...] = jnp.zeros_like(dgm_ref)

        dym = _dot_nt(dh1_ref[...], wo_ref[...])
        gm = gm_ref[...]
        _, yh, rr = _group_rms_fwd(y_ref[...], gm)
        outs, dgs = [], []
        for (a, b), rg in zip(_GROUPS, rr):
            dxg, dgg = _rms_bwd(dym[:, a:b], yh[:, a:b], rg, gm[:, a:b])
            outs.append(dxg)
            dgs.append(dgg)
        dy_ref[...] = jnp.concatenate(outs, axis=1)
        dgm_ref[...] += jnp.concatenate(dgs, axis=1)

    tile = pl.BlockSpec((tm, D_MODEL), lambda i: (i, 0))
    (dy, dgm), _ = _call(
        body, name, (t // tm,), [tile, tile, _const_spec((1, D_MODEL)), _const_spec((D_MODEL, D_MODEL))],
        [tile, _acc_spec((1, D_MODEL))], [_sds((t, D_MODEL), F32), _sds((1, D_MODEL), F32)],
        [], [dh1, ycat, gmix, w_out])
    return dy, dgm


def _in_bwd(dz, h0, dh1, g1, w_in, name):
    t = h0.shape[0]
    tm = _tile(t)

    def body(dz_ref, h_ref, dh1_ref, g_ref, w_ref, dh0_ref, dg_ref):
        @pl.when(pl.program_id(0) == 0)
        def _():
            dg_ref[...] = jnp.zeros_like(dg_ref)

        dhn = _dot_nt(dz_ref[...], w_ref[...])
        _, xh, r = _rms_fwd(h_ref[...], g_ref[...])
        dx, dg = _rms_bwd(dhn, xh, r, g_ref[...])
        dg_ref[...] += dg
        dh0_ref[...] = dh1_ref[...] + dx

    tile = lambda w: pl.BlockSpec((tm, w), lambda i: (i, 0))
    (dh0, dg), _ = _call(
        body, name, (t // tm,),
        [tile(IN_W), tile(D_MODEL), tile(D_MODEL), _const_spec((1, D_MODEL)), _const_spec((D_MODEL, IN_W))],
        [tile(D_MODEL), _acc_spec((1, D_MODEL))], [_sds((t, D_MODEL), F32), _sds((1, D_MODEL), F32)],
        [], [dz, h0, dh1, g1, w_in])
    return dh0, dg


def _loss_head(h, gf, target, name):
    t = h.shape[0]
    tm = _tile(t)

    def body(h_ref, g_ref, t_ref, dh_ref, loss_ref, dg_ref):
        @pl.when(pl.program_id(0) == 0)
        def _():
            loss_ref[...] = jnp.zeros_like(loss_ref)
            dg_ref[...] = jnp.zeros_like(dg_ref)

        g = g_ref[...]
        y, xh, r = _rms_fwd(h_ref[...], g)
        err = y - t_ref[...]
        part = 0.5 * jnp.sum(jnp.mean(err * err, axis=-1, keepdims=True), axis=0, keepdims=True)
        loss_ref[...] += jnp.broadcast_to(part, loss_ref.shape)
        dx, dg = _rms_bwd(err * (1.0 / D_MODEL), xh, r, g)
        dg_ref[...] += dg
        dh_ref[...] = dx

    tile = pl.BlockSpec((tm, D_MODEL), lambda i: (i, 0))
    (dh, loss, dg), _ = _call(
        body, name, (t // tm,), [tile, _const_spec((1, D_MODEL)), tile],
        [tile, _acc_spec((1, 128)), _acc_spec((1, D_MODEL))],
        [_sds((t, D_MODEL), F32), _sds((1, 128), F32), _sds((1, D_MODEL), F32)], [], [h, gf, target])
    return dh, loss, dg


def _dw(x, y, name, split, bm, bn):
    t, m = x.shape
    n = y.shape[1]
    tk = _tile(t, DW_TILE)
    nk = t // tk
    if split == "rows":
        assert bn == n
        r, c = m // N_DEV, n
        per = bm // r
        out_block = pl.BlockSpec((per, r, c), lambda a, b, k: (a, 0, 0))
    else:
        assert bm == m
        r, c = m, n // N_DEV
        per = bn // c
        out_block = pl.BlockSpec((per, r, c), lambda a, b, k: (b, 0, 0))

    def body(x_ref, y_ref, o_ref, o16_ref, acc):
        k = pl.program_id(2)

        @pl.when(k == 0)
        def _():
            acc[...] = jnp.zeros_like(acc)

        acc[...] += _dot_tn(x_ref[...], y_ref[...])

        @pl.when(k == nk - 1)
        def _():
            for d in range(per):
                v = acc[d * r:(d + 1) * r, :] if split == "rows" else acc[:, d * c:(d + 1) * c]
                o_ref[d] = v
                o16_ref[d] = v.astype(o16_ref.dtype)

    return pl.pallas_call(
        body, name=name, grid=(m // bm, n // bn, nk),
        in_specs=[pl.BlockSpec((tk, bm), lambda a, b, k: (k, a)), pl.BlockSpec((tk, bn), lambda a, b, k: (k, b))],
        out_specs=[out_block, out_block],
        out_shape=[_sds((N_DEV, r, c), F32), _sds((N_DEV, r, c), WIRE)],
        scratch_shapes=[pltpu.VMEM((bm, bn), F32)],
        compiler_params=pltpu.CompilerParams(dimension_semantics=("arbitrary",) * 3, vmem_limit_bytes=VMEM_LIMIT),
    )(x, y)


def _adamw_math(w, g, m, v):
    m = ADAM_B1 * m + (1.0 - ADAM_B1) * g
    v = ADAM_B2 * v + (1.0 - ADAM_B2) * jnp.square(g)
    m_hat = m / (1.0 - ADAM_B1 ** ADAM_STEP)
    v_hat = v / (1.0 - ADAM_B2 ** ADAM_STEP)
    delta = -ADAM_LR * (m_hat / (jnp.sqrt(v_hat) + ADAM_EPS) + ADAM_WD * w)
    return delta, m, v


def _adamw_shard(g_own, g_recv, dev, w, m, v, name):
    _, r, c = w.shape
    br = r
    for cand in (256, 128, 112, 64, 56, 32, 16, 8):
        if r % cand == 0:
            br = cand
            break
    nr = r // br
    own = lambda l: pl.BlockSpec((1, br, c), lambda ll, i, d: (d[0], jnp.where(ll == l, i, (nr - 1) * (1 - l)), 0))
    recv = lambda l: pl.BlockSpec((N_DEV - 1, br, c), lambda ll, i, d: (0, jnp.where(ll == l, i, (nr - 1) * (1 - l)), 0))

    def body(dev_ref, go0, gr0, go1, gr1, w_ref, m_ref, v_ref, g_out, d_out, m_out, v_out):
        def update(go_ref, gr_ref):
            g = go_ref[0]
            for j in range(N_DEV - 1):
                g = g + gr_ref[j].astype(F32)
            delta, mn, vn = _adamw_math(w_ref[0], g, m_ref[0], v_ref[0])
            g_out[0] = g
            d_out[0] = delta
            m_out[0] = mn
            v_out[0] = vn

        layer = pl.program_id(0)
        pl.when(layer == 0)(lambda: update(go0, gr0))
        pl.when(layer == 1)(lambda: update(go1, gr1))

    tile = pl.BlockSpec((1, br, c), lambda ll, i, d: (ll, i, 0))
    return pl.pallas_call(
        body, name=name,
        grid_spec=pltpu.PrefetchScalarGridSpec(
            num_scalar_prefetch=1, grid=(2, nr),
            in_specs=[own(0), recv(0), own(1), recv(1), tile, tile, tile],
            out_specs=[tile, tile, tile, tile]),
        out_shape=[_sds((2, r, c), F32)] * 4,
        compiler_params=pltpu.CompilerParams(dimension_semantics=("arbitrary",) * 2, vmem_limit_bytes=VMEM_LIMIT),
    )(dev, g_own[0], g_recv[0], g_own[1], g_recv[1], w, m, v)


def _adamw_small(gs, ws, ms, vs, name):
    n = len(gs)

    def body(*refs):
        g_refs, w_refs, m_refs, v_refs = (refs[k * n:(k + 1) * n] for k in range(4))
        outs = refs[4 * n:]
        for k in range(n):
            delta, mn, vn = _adamw_math(w_refs[k][...], g_refs[k][...], m_refs[k][...], v_refs[k][...])
            outs[k][...] = delta
            outs[n + k][...] = mn
            outs[2 * n + k][...] = vn

    shapes = [_sds(w.shape, F32) for w in ws]
    res = pl.pallas_call(body, name=name, out_shape=shapes * 3,
                         compiler_params=pltpu.CompilerParams(vmem_limit_bytes=VMEM_LIMIT))(*gs, *ws, *ms, *vs)
    return res[:n], res[n:2 * n], res[2 * n:]


def _sum_parts(part, name):
    def body(p_ref, o_ref):
        g = p_ref[0]
        for d in range(1, N_DEV):
            g = g + p_ref[d]
        o_ref[...] = g

    return pl.pallas_call(body, name=name, out_shape=_sds(part.shape[1:], F32))(part)


def _block_diag(w):
    out = jnp.zeros((LRU_W, LRU_W), w.dtype)
    for h in range(4):
        out = lax.dynamic_update_slice(out, w[h], (h * 64, h * 64))
    return out


def _unblock_diag(w):
    return jnp.concatenate([w[h * 64:(h + 1) * 64, h * 64:(h + 1) * 64] for h in range(4)], axis=0)


def _layer_params(p, l):
    row = lambda a: a[l].reshape(1, -1)
    sink_rows = jnp.repeat(p["attn_sinks"][l].reshape(4, 2), 2 * BLK, axis=1)
    sink_rows = jnp.concatenate([sink_rows, jnp.zeros((4, 4 * BLK), F32)], axis=0)
    cw = jnp.concatenate([p["conv_dw_w"][l], jnp.zeros((1, CONV_W), F32)], axis=0)
    pv = jnp.concatenate([
        row(p["conv_dw_b"]), row(p["conv_ln_g"]), row(p["conv_ln_b"]), row(p["lru_conv_b"]), row(p["lru_ba"]),
        row(p["lru_bx"]), row(p["lru_lambda"]), jnp.zeros((1, LRU_W), F32), p["lru_conv_w"][l],
        jnp.zeros((4, LRU_W), F32)], axis=0)
    return dict(
        g1=row(p["norm1"]), sink=sink_rows, cw=cw, pv=pv,
        wa=_block_diag(p["lru_wa"][l]).astype(MX), wx=_block_diag(p["lru_wx"][l]).astype(MX),
        gmix=row(p["mix_norm"]), g2=row(p["norm2"]))


_SMALL = ["norm1", "attn_sinks", "conv_dw_w", "conv_dw_b", "conv_ln_g", "conv_ln_b", "lru_conv_w", "lru_conv_b",
          "lru_wa", "lru_ba", "lru_wx", "lru_bx", "lru_lambda", "mix_norm", "norm2"]
_BIG = ["w_in", "w_out", "w_up", "w_down"]
_WEIGHTS = ["norm1", "w_in", "attn_sinks", "conv_dw_w", "conv_dw_b", "conv_ln_g", "conv_ln_b", "lru_conv_w",
            "lru_conv_b", "lru_wa", "lru_ba", "lru_wx", "lru_bx", "lru_lambda", "mix_norm", "w_out", "norm2", "w_up",
            "w_down", "final_norm"]


def kernel(x, norm1, w_in, attn_sinks, conv_dw_w, conv_dw_b, conv_ln_g, conv_ln_b, lru_conv_w, lru_conv_b, lru_wa, lru_ba, lru_wx, lru_bx, lru_lambda, mix_norm, w_out, norm2, w_up, w_down, final_norm, loss_target, m_norm1, m_w_in, m_attn_sinks, m_conv_dw_w, m_conv_dw_b, m_conv_ln_g, m_conv_ln_b, m_lru_conv_w, m_lru_conv_b, m_lru_wa, m_lru_ba, m_lru_wx, m_lru_bx, m_lru_lambda, m_mix_norm, m_w_out, m_norm2, m_w_up, m_w_down, m_final_norm, v_norm1, v_w_in, v_attn_sinks, v_conv_dw_w, v_conv_dw_b, v_conv_ln_g, v_conv_ln_b, v_lru_conv_w, v_lru_conv_b, v_lru_wa, v_lru_ba, v_lru_wx, v_lru_bx, v_lru_lambda, v_mix_norm, v_w_out, v_norm2, v_w_up, v_w_down, v_final_norm):
    w = dict(norm1=norm1, w_in=w_in, attn_sinks=attn_sinks, conv_dw_w=conv_dw_w, conv_dw_b=conv_dw_b,
             conv_ln_g=conv_ln_g, conv_ln_b=conv_ln_b, lru_conv_w=lru_conv_w, lru_conv_b=lru_conv_b, lru_wa=lru_wa,
             lru_ba=lru_ba, lru_wx=lru_wx, lru_bx=lru_bx, lru_lambda=lru_lambda, mix_norm=mix_norm, w_out=w_out,
             norm2=norm2, w_up=w_up, w_down=w_down, final_norm=final_norm)
    m = dict(norm1=m_norm1, w_in=m_w_in, attn_sinks=m_attn_sinks, conv_dw_w=m_conv_dw_w, conv_dw_b=m_conv_dw_b,
             conv_ln_g=m_conv_ln_g, conv_ln_b=m_conv_ln_b, lru_conv_w=m_lru_conv_w, lru_conv_b=m_lru_conv_b,
             lru_wa=m_lru_wa, lru_ba=m_lru_ba, lru_wx=m_lru_wx, lru_bx=m_lru_bx, lru_lambda=m_lru_lambda,
             mix_norm=m_mix_norm, w_out=m_w_out, norm2=m_norm2, w_up=m_w_up, w_down=m_w_down, final_norm=m_final_norm)
    v = dict(norm1=v_norm1, w_in=v_w_in, attn_sinks=v_attn_sinks, conv_dw_w=v_conv_dw_w, conv_dw_b=v_conv_dw_b,
             conv_ln_g=v_conv_ln_g, conv_ln_b=v_conv_ln_b, lru_conv_w=v_lru_conv_w, lru_conv_b=v_lru_conv_b,
             lru_wa=v_lru_wa, lru_ba=v_lru_ba, lru_wx=v_lru_wx, lru_bx=v_lru_bx, lru_lambda=v_lru_lambda,
             mix_norm=v_mix_norm, w_out=v_w_out, norm2=v_norm2, w_up=v_w_up, w_down=v_w_down, final_norm=v_final_norm)
    depth = w_in.shape[0]
    xi, yi, ci = _me()
    dev = (4 * xi + 2 * yi + ci).astype(jnp.int32)
    dev1 = dev.reshape(1)
    wb = {n: w[n].astype(MX) for n in _BIG}
    layer_shards = lambda l: [wb["w_out"][l], wb["w_up"][l], wb["w_down"][l]]

    _, ((g_in0, g_cw, g_lcw),) = _call(None, "gather_first", None, [], [], [], [], [],
                                        [_gather_rider([wb["w_in"][0], conv_dw_w, lru_conv_w])])
    cols = lambda g: jnp.moveaxis(g, 0, -2).reshape(g.shape[1:-1] + (N_DEV * g.shape[-1],))
    p = dict(w)
    p["conv_dw_w"] = cols(g_cw)
    p["lru_conv_w"] = cols(g_lcw)
    lp = [_layer_params(p, l) for l in range(depth)]

    gathered = [dict(w_in=cols(g_in0)), dict()]
    saved = []
    h = x[0]
    for l in range(depth):
        q, gw = lp[l], gathered[l]
        z, hn1 = _ln_in(h, q["g1"], gw["w_in"], f"ln_in{l}")
        riders = [_gather_rider(layer_shards(0))] if l == 0 else []
        (ycat, hl, uc), got = _mixer_fwd(z, q["sink"], q["cw"], q["pv"], q["wa"], q["wx"], f"mixer_fwd{l}", riders)
        if l == 0:
            gw["w_out"], gw["w_up"], gw["w_down"] = got[0]
            gw["w_out"] = gw["w_out"].reshape(D_MODEL, D_MODEL)
        riders = [_gather_rider([wb["w_in"][1]] + layer_shards(1))] if l == 0 else []
        (h1, act, h2, ym, hn2), got = _post_fwd(ycat, h, q["gmix"], gw["w_out"], q["g2"], gw["w_up"],
                                                gw["w_down"].reshape(D_FF, D_MODEL), f"post_fwd{l}", riders)
        if l == 0:
            nxt = gathered[1]
            nxt["w_in"], nxt["w_out"], nxt["w_up"], nxt["w_down"] = got[0]
            nxt["w_in"] = cols(nxt["w_in"])
            nxt["w_out"] = nxt["w_out"].reshape(D_MODEL, D_MODEL)
        saved.append(dict(h0=h, z=z, hn1=hn1, ycat=ycat, hl=hl, uc=uc, h1=h1, act=act, ym=ym, hn2=hn2))
        h = h2
    dh, loss, dgf = _loss_head(h, final_norm.reshape(1, -1), loss_target[0], "loss_head")

    grads = [None] * depth
    big = {n: [None] * depth for n in _BIG}
    pending = []

    def send_pending():
        riders = [_scatter_rider([item[3] for item in pending])] if pending else []
        return riders, list(pending)

    def record(sent, got):
        for item, recv in zip(sent, got[0] if sent else []):
            big[item[0]][item[1]] = (item[2], recv)
        del pending[:len(sent)]

    for l in reversed(range(depth)):
        q, s, gw = lp[l], saved[l], gathered[l]
        riders, sent = send_pending()
        w_up_t = jnp.swapaxes(gw["w_up"], 1, 2).reshape(D_FF, D_MODEL)
        (dh1, dh1b, dhb, du, dg2), got = _ffn_bwd(dh, s["act"], s["h1"], q["g2"], w_up_t, gw["w_down"],
                                                  f"ffn_bwd{l}", riders)
        record(sent, got)
        dycat, dgm = _mix_bwd(dh1b, s["ycat"], q["gmix"], gw["w_out"], f"mix_bwd{l}")
        pending.append(("w_down", l) + tuple(_dw(s["act"], dhb, f"dw_down{l}", "rows", 2048, D_MODEL)))
        pending.append(("w_up", l) + tuple(_dw(s["hn2"], du, f"dw_up{l}", "cols", D_MODEL, 2048)))
        pending.append(("w_out", l) + tuple(_dw(s["ym"], dh1b, f"dw_out{l}", "rows", D_MODEL, D_MODEL)))
        riders, sent = send_pending()
        (dz, dsink, dcw, dpv, dwa, dwx), got = _mixer_bwd(dycat, s["z"], s["ycat"], s["hl"], s["uc"], q["sink"],
                                                          q["cw"], q["pv"], q["wa"], q["wx"], f"mixer_bwd{l}", riders)
        record(sent, got)
        pending.append(("w_in", l) + tuple(_dw(s["hn1"], dz, f"dw_in{l}", "cols", D_MODEL, IN_W)))
        dh, dg1 = _in_bwd(dz, s["h0"], dh1, q["g1"], gw["w_in"], f"in_bwd{l}")
        grads[l] = dict(
            norm1=dg1[0], attn_sinks=jnp.stack([dsink[0:4, 0], dsink[0:4, 2 * BLK]], axis=1).reshape(8),
            conv_dw_w=dcw[0:CONV_K], conv_dw_b=dpv[R_CONV_B], conv_ln_g=dpv[R_LN_G], conv_ln_b=dpv[R_LN_B],
            lru_conv_w=dpv[R_LCW:R_LCW + LRU_K], lru_conv_b=dpv[R_LCONV_B], lru_wa=_unblock_diag(dwa),
            lru_ba=dpv[R_BA].reshape(4, 64), lru_wx=_unblock_diag(dwx), lru_bx=dpv[R_BX].reshape(4, 64),
            lru_lambda=dpv[R_LAM], mix_norm=dgm[0], norm2=dg2[0])

    small = [jnp.stack([grads[l][n] for l in range(depth)]) for n in _SMALL] + [dgf, loss[:, 0:1]]
    sizes = [a.size for a in small]
    total = -(-sum(sizes) // 1024) * 1024
    packed = jnp.concatenate([a.reshape(-1) for a in small] + [jnp.zeros((total - sum(sizes),), F32)])
    riders, sent = send_pending()
    _, got = _call(None, "tail_exchange", None, [], [], [], [], [],
                   riders + [_bcast_rider([packed.reshape(total // 128, 128)])])
    record(sent, got)
    summed = _sum_parts(got[1][0], "sum_small_grads").reshape(-1)
    small_sums, pos = [], 0
    for a, size in zip(small, sizes):
        small_sums.append(summed[pos:pos + size].reshape(a.shape))
        pos += size

    out = {}
    for n in _BIG:
        out[n] = list(_adamw_shard([big[n][l][0] for l in range(depth)], [big[n][l][1] for l in range(depth)],
                                   dev1, w[n], m[n], v[n], f"adamw_{n}"))
    shard = lambda a: lax.dynamic_slice_in_dim(a, dev * (a.shape[-1] // N_DEV), a.shape[-1] // N_DEV, axis=a.ndim - 1)
    flat = {"lru_wa": (depth, LRU_W, 64), "lru_wx": (depth, LRU_W, 64), "final_norm": (1, D_MODEL)}
    gs, ws, ms, vs = [], [], [], []
    for n, g in zip(_SMALL + ["final_norm"], small_sums[:-1]):
        shp = flat.get(n, w[n].shape)
        gs.append((shard(g) if n in ("conv_dw_w", "lru_conv_w") else g).reshape(shp))
        ws.append(w[n].reshape(shp))
        ms.append(m[n].reshape(shp))
        vs.append(v[n].reshape(shp))
    sd, sm, sv = _adamw_small(gs, ws, ms, vs, "adamw_small")
    for j, n in enumerate(_SMALL + ["final_norm"]):
        out[n] = [a.reshape(w[n].shape) for a in (gs[j], sd[j], sm[j], sv[j])]
    loss_total = small_sums[-1][0, 0]

    result = [loss_total, dh[None]]
    for j in range(4):
        result += [out[n][j] for n in _WEIGHTS]
    return tuple(result)
```

```python
import types

import jax
import jax.numpy as jnp
from jax import lax
from jax.experimental import pallas as pl
from jax.experimental.pallas import tpu as pltpu

F32 = jnp.float32
MX = jnp.bfloat16
WIRE = jnp.bfloat16

D_MODEL = 1024
HEAD_DIM = 64
ATTN_W = 512
KV_W = 128
BLK = 128
CONV_W = 256
CONV_K = 31
LRU_W = 256
LRU_K = 4
LRU_C = 8.0
IN_W = 1792
D_FF = 4096
FF_BLK = 512
N_DEV = 8
IN_SHARD = IN_W // N_DEV
RMS_EPS = 1e-6
LN_EPS = 1e-5
MASK_VALUE = -1e30
SCALE = HEAD_DIM ** -0.5
CONV_HALO = 32
LRU_HALO = 8
CONV_CHUNK = 64
POST_TILE = 512
DW_TILE = 1024
Q0, K0, V0, CV0, CG0, RX0, RG0 = 0, 512, 640, 768, 1024, 1280, 1536
R_CONV_B, R_LN_G, R_LN_B, R_LCONV_B, R_BA, R_BX, R_LAM, R_LCW = 0, 1, 2, 3, 4, 5, 6, 8

ADAM_LR, ADAM_B1, ADAM_B2, ADAM_EPS, ADAM_WD, ADAM_STEP = 0.001, 0.9, 0.999, 1e-08, 0.01, 10

VMEM_LIMIT = 56 * 1024 * 1024
MESH = pl.DeviceIdType.MESH
ANY = pl.BlockSpec(memory_space=pl.ANY)


def _tile(t, cap=512):
    return min(cap, t)


def _dot(a, b):
    return jnp.dot(a.astype(MX), b.astype(MX), preferred_element_type=F32)


def _dot_nt(a, b):
    return lax.dot_general(a.astype(MX), b.astype(MX), (((1,), (1,)), ((), ())), preferred_element_type=F32)


def _dot_tn(a, b):
    return lax.dot_general(a.astype(MX), b.astype(MX), (((0,), (0,)), ((), ())), preferred_element_type=F32)


def _const_spec(shape):
    nd = len(shape)
    return pl.BlockSpec(shape, lambda *_: (0,) * nd, pipeline_mode=pl.Buffered(1))


def _acc_spec(shape):
    nd = len(shape)
    return pl.BlockSpec(shape, lambda *_: (0,) * nd)


def _sds(shape, dtype):
    return jax.ShapeDtypeStruct(shape, dtype)


def _sigmoid(x):
    return jax.nn.sigmoid(x)


def _rms_fwd(x, g):
    r = lax.rsqrt(jnp.mean(x * x, axis=-1, keepdims=True) + RMS_EPS)
    xh = x * r
    return xh * g, xh, r


def _rms_bwd(dy, xh, r, g):
    t = dy * g
    dx = r * (t - xh * jnp.mean(t * xh, axis=-1, keepdims=True))
    return dx, jnp.sum(dy * xh, axis=0, keepdims=True)


_GROUPS = ((0, 512), (512, 768), (768, 1024))


def _group_rms_fwd(y, g):
    parts = [_rms_fwd(y[:, a:b], g[:, a:b]) for a, b in _GROUPS]
    return (jnp.concatenate([p[0] for p in parts], axis=1),
            jnp.concatenate([p[1] for p in parts], axis=1),
            [p[2] for p in parts])


def _gelu(x):
    c = 0.7978845608028654
    u = c * (x + 0.044715 * x * x * x)
    th = jnp.tanh(u)
    val = 0.5 * x * (1.0 + th)
    grad = 0.5 * (1.0 + th) + 0.5 * x * (1.0 - th * th) * c * (1.0 + 3.0 * 0.044715 * x * x)
    return val, grad


def _neg_expm1(x):
    series = -x * (1.0 + x * (0.5 + x * (1.0 / 6.0 + x * (1.0 / 24.0))))
    return jnp.where(x > -0.02, series, 1.0 - jnp.exp(x))


def _me():
    return lax.axis_index("x"), lax.axis_index("y"), lax.axis_index("c")


def _gather_rider(arrays):
    arrays = list(arrays)
    n = len(arrays)

    def plan(ins, outs, sems):
        ssem, rsem, lsem = sems
        x, y, c = _me()
        chips = [(1 - x, y), (x, 1 - y), (1 - x, 1 - y)]

        def copy(a, k, block, to, own=False):
            dst = outs[a].at[4 * block[0] + 2 * block[1] + block[2]]
            return pltpu.make_async_remote_copy(
                src_ref=ins[a] if own else dst, dst_ref=dst, send_sem=ssem.at[7 * a + k],
                recv_sem=rsem.at[7 * a + k], device_id=to, device_id_type=MESH)

        return x, y, c, chips, copy, lsem

    def start(ins, outs, sems):
        x, y, c, chips, copy, lsem = plan(ins, outs, sems)
        for a in range(n):
            pltpu.make_async_copy(ins[a], outs[a].at[4 * x + 2 * y + c], lsem.at[a]).start()
            copy(a, 0, (x, y, c), (x, y, 1 - c), own=True).start()
            for j, chip in enumerate(chips):
                copy(a, 1 + j, (x, y, c), (*chip, c), own=True).start()

    def mid(ins, outs, sems):
        x, y, c, chips, copy, _ = plan(ins, outs, sems)
        for a in range(n):
            for j, chip in enumerate(chips):
                copy(a, 1 + j, (*chip, c), (x, y, c)).wait_recv()
                copy(a, 4 + j, (*chip, c), (x, y, 1 - c)).start()

    def finish(ins, outs, sems):
        x, y, c, chips, copy, lsem = plan(ins, outs, sems)
        for a in range(n):
            copy(a, 0, (x, y, 1 - c), (x, y, c)).wait_recv()
            for j, chip in enumerate(chips):
                copy(a, 4 + j, (*chip, 1 - c), (x, y, c)).wait_recv()
        for a in range(n):
            copy(a, 0, (x, y, c), (x, y, 1 - c), own=True).wait_send()
            for j, chip in enumerate(chips):
                copy(a, 1 + j, (x, y, c), (*chip, c), own=True).wait_send()
                copy(a, 4 + j, (*chip, c), (x, y, 1 - c)).wait_send()
            pltpu.make_async_copy(ins[a], outs[a].at[4 * x + 2 * y + c], lsem.at[a]).wait()

    return types.SimpleNamespace(
        arrays=arrays, out_shape=[_sds((N_DEV,) + a.shape, a.dtype) for a in arrays],
        scratch=[pltpu.SemaphoreType.DMA((7 * n,)), pltpu.SemaphoreType.DMA((7 * n,)), pltpu.SemaphoreType.DMA((n,))],
        start=start, mid=mid, finish=finish)


def _bcast_rider(arrays):
    arrays = list(arrays)
    n = len(arrays)

    def copies(ins, outs, sems, landing):
        ssem, rsem, lsem = sems
        x, y, c = _me()
        out = []
        for a in range(n):
            out.append(pltpu.make_async_copy(ins[a], outs[a].at[4 * x + 2 * y + c], lsem.at[a]))
            for f in range(1, N_DEV):
                px = 1 - x if f & 4 else x
                py = 1 - y if f & 2 else y
                pc = 1 - c if f & 1 else c
                slot = 4 * px + 2 * py + pc if landing else 4 * x + 2 * y + c
                out.append(pltpu.make_async_remote_copy(
                    src_ref=ins[a], dst_ref=outs[a].at[slot], send_sem=ssem.at[7 * a + f - 1],
                    recv_sem=rsem.at[7 * a + f - 1], device_id=(px, py, pc), device_id_type=MESH))
        return out

    def start(ins, outs, sems):
        for cp in copies(ins, outs, sems, landing=False):
            cp.start()

    def finish(ins, outs, sems):
        for cp in copies(ins, outs, sems, landing=True):
            cp.wait()

    return types.SimpleNamespace(
        arrays=arrays, out_shape=[_sds((N_DEV,) + a.shape, a.dtype) for a in arrays],
        scratch=[pltpu.SemaphoreType.DMA((7 * n,)), pltpu.SemaphoreType.DMA((7 * n,)), pltpu.SemaphoreType.DMA((n,))],
        start=start, mid=None, finish=finish)


def _scatter_rider(arrays):
    arrays = list(arrays)
    n = len(arrays)

    def copies(ins, outs, sems):
        ssem, rsem = sems
        x, y, c = _me()
        out = []
        for a in range(n):
            for f in range(1, N_DEV):
                px = 1 - x if f & 4 else x
                py = 1 - y if f & 2 else y
                pc = 1 - c if f & 1 else c
                out.append(pltpu.make_async_remote_copy(
                    src_ref=ins[a].at[4 * px + 2 * py + pc], dst_ref=outs[a].at[f - 1], send_sem=ssem.at[7 * a + f - 1],
                    recv_sem=rsem.at[7 * a + f - 1], device_id=(px, py, pc), device_id_type=MESH))
        return out

    def start(ins, outs, sems):
        for cp in copies(ins, outs, sems):
            cp.start()

    def finish(ins, outs, sems):
        for cp in copies(ins, outs, sems):
            cp.wait()

    return types.SimpleNamespace(
        arrays=arrays, out_shape=[_sds((N_DEV - 1,) + a.shape[1:], a.dtype) for a in arrays],
        scratch=[pltpu.SemaphoreType.DMA((7 * n,)), pltpu.SemaphoreType.DMA((7 * n,))],
        start=start, mid=None, finish=finish)


def _call(body, name, grid, in_specs, out_specs, out_shape, scratch, operands, riders=()):
    n_in, n_out, n_scr = len(operands), len(out_shape), len(scratch)
    nsteps = grid[0] if grid else 1
    sizes = [(len(r.arrays), len(r.out_shape), len(r.scratch)) for r in riders]

    def wrapped(*refs):
        pos = n_in
        r_ins = []
        for ri, _, _ in sizes:
            r_ins.append(refs[pos:pos + ri])
            pos += ri
        outs = refs[pos:pos + n_out]
        pos += n_out
        r_outs = []
        for _, ro, _ in sizes:
            r_outs.append(refs[pos:pos + ro])
            pos += ro
        scr = refs[pos:pos + n_scr]
        pos += n_scr
        r_sems = []
        for _, _, rs in sizes:
            r_sems.append(refs[pos:pos + rs])
            pos += rs
        step = pl.program_id(0) if grid else 0

        def at(s, fn):
            if grid:
                pl.when(step == s)(fn)
            else:
                fn()

        for r, a, b, c in zip(riders, r_ins, r_outs, r_sems):
            at(0, lambda r=r, a=a, b=b, c=c: r.start(a, b, c))
        for r, a, b, c in zip(riders, r_ins, r_outs, r_sems):
            if r.mid is not None:
                at((3 * nsteps) // 4, lambda r=r, a=a, b=b, c=c: r.mid(a, b, c))
        if body is not None:
            body(*refs[:n_in], *outs, *scr)
        for r, a, b, c in zip(riders, r_ins, r_outs, r_sems):
            at(nsteps - 1, lambda r=r, a=a, b=b, c=c: r.finish(a, b, c))

    r_arrays = [a for r in riders for a in r.arrays]
    r_shapes = [s for r in riders for s in r.out_shape]
    kwargs = {}
    if grid:
        kwargs = dict(grid=grid, compiler_params=pltpu.CompilerParams(
            dimension_semantics=("arbitrary",) * len(grid), vmem_limit_bytes=VMEM_LIMIT))
    res = pl.pallas_call(
        wrapped, name=name,
        in_specs=list(in_specs) + [ANY] * len(r_arrays),
        out_specs=list(out_specs) + [ANY] * len(r_shapes),
        out_shape=list(out_shape) + r_shapes,
        scratch_shapes=list(scratch) + [s for r in riders for s in r.scratch],
        **kwargs,
    )(*operands, *r_arrays)
    host, rest = res[:n_out], res[n_out:]
    r_res = []
    for _, ro, _ in sizes:
        r_res.append(rest[:ro])
        rest = rest[ro:]
    return host, r_res


def _ln_in(h, g1, w_in, name):
    t = h.shape[0]
    tm = _tile(t)

    def body(h_ref, g_ref, w_ref, z_ref, hn_ref):
        y, _, _ = _rms_fwd(h_ref[...], g_ref[...])
        hn = y.astype(MX)
        hn_ref[...] = hn
        z_ref[...] = jnp.dot(hn, w_ref[...], preferred_element_type=F32)

    tile = lambda w: pl.BlockSpec((tm, w), lambda i: (i, 0))
    (z, hn), _ = _call(
        body, name, (t // tm,),
        [tile(D_MODEL), _const_spec((1, D_MODEL)), _const_spec((D_MODEL, IN_W))],
        [tile(IN_W), tile(D_MODEL)], [_sds((t, IN_W), F32), _sds((t, D_MODEL), MX)], [], [h, g1, w_in])
    return z, hn


def _band2(kb, g):
    lo = lax.broadcasted_iota(jnp.int32, kb.shape, 1) < HEAD_DIM
    kr = pltpu.roll(kb, HEAD_DIM, 1)
    if g == 0:
        top, bot = jnp.where(lo, kb, 0.0), jnp.where(lo, 0.0, kr)
    else:
        top, bot = jnp.where(lo, kr, 0.0), jnp.where(lo, 0.0, kb)
    return jnp.concatenate([top, bot], axis=0)


def _attn_block(z_ref, zh_ref, sink_ref, b, first):
    rows = slice(b * BLK, (b + 1) * BLK)
    prev = zh_ref if b == 0 else z_ref
    prow = slice(0, BLK) if b == 0 else slice((b - 1) * BLK, b * BLK)
    kb = jnp.concatenate([prev[prow, K0:K0 + KV_W], z_ref[rows, K0:K0 + KV_W]], axis=0)
    vb = jnp.concatenate([prev[prow, V0:V0 + KV_W], z_ref[rows, V0:V0 + KV_W]], axis=0)
    k2 = [_band2(kb, g) for g in range(2)]
    v2 = [_band2(vb, g) for g in range(2)]
    q2 = [jnp.concatenate([z_ref[rows, (2 * g) * BLK:(2 * g + 1) * BLK], z_ref[rows, (2 * g + 1) * BLK:(2 * g + 2) * BLK]],
                          axis=0) for g in range(2)]
    rr = lax.broadcasted_iota(jnp.int32, (4 * BLK, 2 * BLK), 0) & (BLK - 1)
    cc = lax.broadcasted_iota(jnp.int32, (4 * BLK, 2 * BLK), 1)
    first_block = jnp.logical_and(first, b == 0).astype(jnp.int32)
    mask = jnp.logical_and(jnp.logical_and(cc > rr, cc <= rr + BLK), cc >= BLK * first_block)
    s = jnp.concatenate([_dot_nt(q2[g], k2[g]) for g in range(2)], axis=0) * SCALE
    w = 2 * BLK
    out, psink = [], []
    for hh in range(2):
        sh = jnp.where(mask, s[:, hh * w:(hh + 1) * w], MASK_VALUE)
        sk = jnp.concatenate([jnp.broadcast_to(sink_ref[p:p + 1, hh * w:hh * w + 1], (BLK, 1)) for p in range(4)], axis=0)
        m = jnp.maximum(jnp.max(sh, axis=1, keepdims=True), sk)
        p = jnp.exp(sh - m)
        es = jnp.exp(sk - m)
        inv = 1.0 / (jnp.sum(p, axis=1, keepdims=True) + es)
        out.append(p * inv)
        psink.append(es * inv)
    return q2, k2, v2, jnp.concatenate(out, axis=1), psink


def _scan_steps(a, b, n, span, reverse):
    pos = lax.broadcasted_iota(jnp.int32, a.shape, 0) & (span - 1)
    d = 1
    while d < span:
        keep = pos < span - d if reverse else pos >= d
        shift = n - d if reverse else d
        a_sh = jnp.where(keep, pltpu.roll(a, shift, 0), 1.0)
        b_sh = jnp.where(keep, pltpu.roll(b, shift, 0), 0.0)
        b = a * b_sh + b
        a = a * a_sh
        d *= 2
    return a, b


def _scan(a, b, tm, reverse):
    return _scan_steps(a, b, tm, tm, reverse)


def _shifted_copies(ext, shifts, tm):
    rows = tm + CONV_HALO - 8
    for r in range(1, 8):
        shifts[r - 1, 0:rows, :] = ext[pl.ds(r, rows), :]


def _tap(ext, shifts, off, r0, n):
    a, r = divmod(off, 8)
    lo = 8 * a + r0
    if r == 0:
        return ext[lo:lo + n, :]
    return shifts[r - 1, lo:lo + n, :]


def _glu_fill(z_ref, zh_ref, uext, ush, first, tm, sg_out=None):
    cv = z_ref[:, CV0:CV0 + CONV_W]
    sg = _sigmoid(z_ref[:, CG0:CG0 + CONV_W])
    if sg_out is not None:
        sg_out[...] = sg
    hrow = BLK - CONV_HALO
    uh = zh_ref[hrow:BLK, CV0:CV0 + CONV_W] * _sigmoid(zh_ref[hrow:BLK, CG0:CG0 + CONV_W])
    uext[0:CONV_HALO, :] = jnp.where(first, 0.0, uh)
    uext[CONV_HALO:CONV_HALO + tm, :] = cv * sg
    _shifted_copies(uext, ush, tm)


def _conv_taps(cw_ref, pv_ref, uext, ush, out_ref, tm):
    for r0 in range(0, tm, CONV_CHUNK):
        acc = jnp.broadcast_to(pv_ref[R_CONV_B:R_CONV_B + 1, :], (CONV_CHUNK, CONV_W))
        for k in range(CONV_K):
            acc = acc + cw_ref[k:k + 1, :] * _tap(uext, ush, CONV_HALO - (CONV_K - 1) + k, r0, CONV_CHUNK)
        out_ref[r0:r0 + CONV_CHUNK, :] = acc


def _ln_silu(uc, pv_ref):
    mu = jnp.mean(uc, axis=-1, keepdims=True)
    xc = uc - mu
    rs = lax.rsqrt(jnp.mean(xc * xc, axis=-1, keepdims=True) + LN_EPS)
    xh = xc * rs
    ln = xh * pv_ref[R_LN_G:R_LN_G + 1, :] + pv_ref[R_LN_B:R_LN_B + 1, :]
    sg = _sigmoid(ln)
    return xh, rs, ln, sg


def _lru_gates(z_ref, zh_ref, pv_ref, wa_ref, wx_ref, rxext, first, tm):
    rxext[0:LRU_HALO, :] = jnp.where(first, 0.0, zh_ref[BLK - LRU_HALO:BLK, RX0:RX0 + LRU_W])
    rxext[LRU_HALO:LRU_HALO + tm, :] = z_ref[:, RX0:RX0 + LRU_W]
    xc = jnp.broadcast_to(pv_ref[R_LCONV_B:R_LCONV_B + 1, :], (tm, LRU_W))
    for k in range(LRU_K):
        xc = xc + pv_ref[R_LCW + k:R_LCW + k + 1, :] * rxext[pl.ds(LRU_HALO - (LRU_K - 1) + k, tm), :]
    r = _sigmoid(_dot(xc, wa_ref[...]) + pv_ref[R_BA:R_BA + 1, :])
    ig = _sigmoid(_dot(xc, wx_ref[...]) + pv_ref[R_BX:R_BX + 1, :])
    lam = pv_ref[R_LAM:R_LAM + 1, :]
    sp = jnp.log1p(jnp.exp(-lam))
    la = (-LRU_C * r) * sp
    a = jnp.exp(la)
    mult = jnp.sqrt(_neg_expm1(2.0 * la))
    return xc, r, ig, sp, la, a, mult


def _mixer_in_specs(tm, tile_of):
    hb = tm // BLK
    return [
        pl.BlockSpec((tm, IN_W), lambda i: (tile_of(i), 0)),
        pl.BlockSpec((BLK, IN_W), lambda i: (jnp.maximum(tile_of(i) * hb - 1, 0), 0)),
        _const_spec((8, 4 * BLK)),
        _const_spec((32, CONV_W)),
        _const_spec((16, CONV_W)),
        _const_spec((LRU_W, LRU_W)),
        _const_spec((LRU_W, LRU_W)),
    ]


def _mixer_fwd(z, sink, cw, pv, wa, wx, name, riders=()):
    t = z.shape[0]
    tm = _tile(t)
    nb = tm // BLK

    def body(z_ref, zh_ref, sink_ref, cw_ref, pv_ref, wa_ref, wx_ref, y_ref, hl_ref, uc_ref, uext, ush, rxext, hcar):
        i = pl.program_id(0)
        first = i == 0

        @pl.when(first)
        def _():
            hcar[...] = jnp.zeros_like(hcar)

        for b in range(nb):
            rows = slice(b * BLK, (b + 1) * BLK)
            _, _, v2, prob, _ = _attn_block(z_ref, zh_ref, sink_ref, b, first)
            for g in range(2):
                o = _dot(prob[2 * g * BLK:(2 * g + 2) * BLK], v2[g])
                y_ref[rows, (2 * g) * BLK:(2 * g + 1) * BLK] = o[0:BLK]
                y_ref[rows, (2 * g + 1) * BLK:(2 * g + 2) * BLK] = o[BLK:2 * BLK]
        _glu_fill(z_ref, zh_ref, uext, ush, first, tm)
        _conv_taps(cw_ref, pv_ref, uext, ush, uc_ref, tm)
        _, _, ln, sg = _ln_silu(uc_ref[...], pv_ref)
        y_ref[:, ATTN_W:ATTN_W + CONV_W] = ln * sg
        xc, _, ig, _, _, a, mult = _lru_gates(z_ref, zh_ref, pv_ref, wa_ref, wx_ref, rxext, first, tm)
        acum, h = _scan(a, mult * (ig * xc), tm, reverse=False)
        h = h + acum * hcar[0:1, :]
        hl_ref[...] = h
        hcar[0:1, :] = h[tm - 1:tm, :]
        gl, _ = _gelu(z_ref[:, RG0:RG0 + LRU_W])
        y_ref[:, ATTN_W + CONV_W:ATTN_W + CONV_W + LRU_W] = h * gl

    tile = lambda w: pl.BlockSpec((tm, w), lambda i: (i, 0))
    return _call(
        body, name, (t // tm,), _mixer_in_specs(tm, lambda i: i),
        [tile(D_MODEL), tile(LRU_W), tile(CONV_W)],
        [_sds((t, D_MODEL), F32), _sds((t, LRU_W), F32), _sds((t, CONV_W), F32)],
        [pltpu.VMEM((tm + CONV_HALO, CONV_W), F32), pltpu.VMEM((7, tm + CONV_HALO - 8, CONV_W), F32),
         pltpu.VMEM((tm + LRU_HALO, LRU_W), F32), pltpu.VMEM((8, LRU_W), F32)],
        [z, z, sink, cw, pv, wa, wx], riders)


def _mixer_bwd(dy, z, ycat, hl, uc, sink, cw, pv, wa, wx, name, riders=()):
    t = z.shape[0]
    tm = _tile(t)
    nt = t // tm
    nb = tm // BLK
    rev = lambda i: nt - 1 - i

    def body(dy_ref, z_ref, zh_ref, sink_ref, cw_ref, pv_ref, wa_ref, wx_ref, y_ref, hl_ref, hlh_ref, uc_ref,
             dz_ref, dsink_ref, dcw_ref, dpv_ref, dwa_ref, dwx_ref,
             uext, ush, sgs, rxext, dkext, dvext, ducext, dsh, dcw8, dxcext, kcar, vcar, uccar, xccar, gcar):
        i = pl.program_id(0)
        first = i == nt - 1

        @pl.when(i == 0)
        def _():
            for car in (kcar, vcar, uccar, xccar, gcar, dcw8):
                car[...] = jnp.zeros_like(car)
            for acc in (dsink_ref, dpv_ref, dwa_ref, dwx_ref):
                acc[...] = jnp.zeros_like(acc)

        def addrow(r, val):
            dpv_ref[r:r + 1, :] += jnp.sum(val, axis=0, keepdims=True)

        dkext[:, 0:tm] = jnp.zeros((KV_W, tm), F32)
        dvext[:, 0:tm] = jnp.zeros((KV_W, tm), F32)
        dkext[:, tm:tm + BLK] = kcar[...]
        dvext[:, tm:tm + BLK] = vcar[...]
        lane512 = lax.broadcasted_iota(jnp.int32, (1, 4 * BLK), 1) < 2 * BLK
        lo = lax.broadcasted_iota(jnp.int32, (4 * BLK, BLK), 1) < HEAD_DIM
        hd, w2 = HEAD_DIM, 2 * BLK
        for b in range(nb):
            rows = slice(b * BLK, (b + 1) * BLK)
            band = slice(b * BLK, (b + 2) * BLK)
            q2, k2, v2, prob, psink = _attn_block(z_ref, zh_ref, sink_ref, b, first)
            stack = lambda ref: jnp.concatenate([ref[rows, p * BLK:(p + 1) * BLK] for p in range(4)], axis=0)
            do4 = stack(dy_ref)
            dlt = do4 * stack(y_ref)
            d0 = jnp.sum(jnp.where(lo, dlt, 0.0), axis=1, keepdims=True)
            d1 = jnp.sum(jnp.where(lo, 0.0, dlt), axis=1, keepdims=True)
            dp = jnp.concatenate([_dot_nt(do4[g * w2:(g + 1) * w2], v2[g]) for g in range(2)], axis=0)
            dl = jnp.concatenate([jnp.broadcast_to(d0, (4 * BLK, w2)), jnp.broadcast_to(d1, (4 * BLK, w2))], axis=1)
            draw = (prob * (dp - dl)) * SCALE
            e0, e1 = psink[0] * d0, psink[1] * d1
            for p in range(4):
                prs = slice(p * BLK, (p + 1) * BLK)
                s0 = jnp.sum(e0[prs], axis=0, keepdims=True)
                s1 = jnp.sum(e1[prs], axis=0, keepdims=True)
                dsink_ref[p:p + 1, :] += -jnp.where(lane512, s0, s1)
            for g in range(2):
                grs = slice(g * w2, (g + 1) * w2)
                dq = _dot(draw[grs], k2[g])
                dz_ref[rows, (2 * g) * BLK:(2 * g + 1) * BLK] = dq[0:BLK].astype(dz_ref.dtype)
                dz_ref[rows, (2 * g + 1) * BLK:(2 * g + 2) * BLK] = dq[BLK:2 * BLK].astype(dz_ref.dtype)
                tk = _dot_tn(q2[g], draw[grs])
                tv = _dot_tn(do4[grs], prob[grs])
                dkext[g * hd:(g + 1) * hd, band] += tk[0:hd, 0:w2] + tk[hd:2 * hd, w2:2 * w2]
                dvext[g * hd:(g + 1) * hd, band] += tv[0:hd, 0:w2] + tv[hd:2 * hd, w2:2 * w2]
        dz_ref[:, K0:K0 + KV_W] = jnp.transpose(dkext[:, BLK:BLK + tm]).astype(dz_ref.dtype)
        dz_ref[:, V0:V0 + KV_W] = jnp.transpose(dvext[:, BLK:BLK + tm]).astype(dz_ref.dtype)
        kcar[...] = dkext[:, 0:BLK]
        vcar[...] = dvext[:, 0:BLK]

        _glu_fill(z_ref, zh_ref, uext, ush, first, tm, sg_out=sgs)
        xh, rs, ln, sg = _ln_silu(uc_ref[...], pv_ref)
        dln = dy_ref[:, ATTN_W:ATTN_W + CONV_W] * (sg * (1.0 + ln * (1.0 - sg)))
        addrow(R_LN_G, dln * xh)
        addrow(R_LN_B, dln)
        dxh = dln * pv_ref[R_LN_G:R_LN_G + 1, :]
        duc = rs * (dxh - jnp.mean(dxh, axis=-1, keepdims=True) - xh * jnp.mean(dxh * xh, axis=-1, keepdims=True))
        addrow(R_CONV_B, duc)
        ducext[0:tm, :] = duc
        ducext[tm:tm + CONV_HALO, :] = uccar[...]
        uccar[...] = duc[0:CONV_HALO, :]
        _shifted_copies(ducext, dsh, tm)
        for r0 in range(0, tm, CONV_CHUNK):
            crow = slice(r0, r0 + CONV_CHUNK)
            duc_c = ducext[crow, :]
            du = jnp.zeros((CONV_CHUNK, CONV_W), F32)
            for k in range(CONV_K):
                prod = duc_c * _tap(uext, ush, CONV_HALO - (CONV_K - 1) + k, r0, CONV_CHUNK)
                part = prod[0:8]
                for s in range(8, CONV_CHUNK, 8):
                    part = part + prod[s:s + 8]
                dcw8[k] += part
                du = du + cw_ref[k:k + 1, :] * _tap(ducext, dsh, CONV_K - 1 - k, r0, CONV_CHUNK)
            sgc = sgs[crow, :]
            dz_ref[crow, CV0:CV0 + CONV_W] = (du * sgc).astype(dz_ref.dtype)
            u_c = uext[CONV_HALO + r0:CONV_HALO + r0 + CONV_CHUNK, :]
            dz_ref[crow, CG0:CG0 + CONV_W] = (du * u_c * (1.0 - sgc)).astype(dz_ref.dtype)

        @pl.when(i == nt - 1)
        def _():
            dcw_ref[...] = jnp.sum(dcw8[...], axis=1)

        xc, r, ig, sp, la, a, mult = _lru_gates(z_ref, zh_ref, pv_ref, wa_ref, wx_ref, rxext, first, tm)
        h = hl_ref[...]
        rowi = lax.broadcasted_iota(jnp.int32, (tm, LRU_W), 0)
        hlast = jnp.where(first, 0.0, hlh_ref[7:8, :])
        hprev = jnp.where(rowi == 0, hlast, pltpu.roll(h, 1, 0))
        dyl = dy_ref[:, ATTN_W + CONV_W:ATTN_W + CONV_W + LRU_W]
        gl, dgl = _gelu(z_ref[:, RG0:RG0 + LRU_W])
        dz_ref[:, RG0:RG0 + LRU_W] = (dyl * h * dgl).astype(dz_ref.dtype)
        dh = dyl * gl + jnp.where(rowi == tm - 1, gcar[0:1, :], 0.0)
        c = jnp.where(rowi == tm - 1, 0.0, pltpu.roll(a, tm - 1, 0))
        _, gg = _scan(c, dh, tm, reverse=True)
        gcar[0:1, :] = a[0:1, :] * gg[0:1, :]
        dmult = gg * (ig * xc)
        dig = gg * mult * xc
        dxc = gg * mult * ig
        dla = gg * hprev * a - dmult * a * a / mult
        dr = dla * (-LRU_C * sp)
        lam = pv_ref[R_LAM:R_LAM + 1, :]
        dpv_ref[R_LAM:R_LAM + 1, :] += jnp.sum(dla * (-LRU_C * r), axis=0, keepdims=True) * (-_sigmoid(-lam))
        dpa = dr * r * (1.0 - r)
        dpx = dig * ig * (1.0 - ig)
        addrow(R_BA, dpa)
        addrow(R_BX, dpx)
        dxc = dxc + _dot_nt(dpa, wa_ref[...]) + _dot_nt(dpx, wx_ref[...])
        dwa_ref[...] += _dot_tn(xc, dpa)
        dwx_ref[...] += _dot_tn(xc, dpx)
        addrow(R_LCONV_B, dxc)
        dxcext[0:tm, :] = dxc
        dxcext[tm:tm + LRU_HALO, :] = xccar[...]
        xccar[...] = dxc[0:LRU_HALO, :]
        drx = jnp.zeros((tm, LRU_W), F32)
        for k in range(LRU_K):
            addrow(R_LCW + k, dxc * rxext[pl.ds(LRU_HALO - (LRU_K - 1) + k, tm), :])
            drx = drx + pv_ref[R_LCW + k:R_LCW + k + 1, :] * dxcext[pl.ds(LRU_K - 1 - k, tm), :]
        dz_ref[:, RX0:RX0 + LRU_W] = drx.astype(dz_ref.dtype)

    tile = lambda w: pl.BlockSpec((tm, w), lambda i: (rev(i), 0))
    in_specs = [tile(D_MODEL)] + _mixer_in_specs(tm, rev) + [
        tile(D_MODEL), tile(LRU_W),
        pl.BlockSpec((8, LRU_W), lambda i: (jnp.maximum(rev(i) * (tm // 8) - 1, 0), 0)),
        tile(CONV_W)]
    return _call(
        body, name, (nt,), in_specs,
        [tile(IN_W), _acc_spec((8, 4 * BLK)), _acc_spec((32, CONV_W)), _acc_spec((16, CONV_W)),
         _acc_spec((LRU_W, LRU_W)), _acc_spec((LRU_W, LRU_W))],
        [_sds((t, IN_W), MX), _sds((8, 4 * BLK), F32), _sds((32, CONV_W), F32), _sds((16, CONV_W), F32),
         _sds((LRU_W, LRU_W), F32), _sds((LRU_W, LRU_W), F32)],
        [pltpu.VMEM((tm + CONV_HALO, CONV_W), F32), pltpu.VMEM((7, tm + CONV_HALO - 8, CONV_W), F32),
         pltpu.VMEM((tm, CONV_W), F32), pltpu.VMEM((tm + LRU_HALO, LRU_W), F32),
         pltpu.VMEM((KV_W, tm + BLK), F32), pltpu.VMEM((KV_W, tm + BLK), F32),
         pltpu.VMEM((tm + CONV_HALO, CONV_W), F32), pltpu.VMEM((7, tm + CONV_HALO - 8, CONV_W), F32),
         pltpu.VMEM((32, 8, CONV_W), F32), pltpu.VMEM((tm + LRU_HALO, LRU_W), F32),
         pltpu.VMEM((KV_W, BLK), F32), pltpu.VMEM((KV_W, BLK), F32),
         pltpu.VMEM((CONV_HALO, CONV_W), F32), pltpu.VMEM((LRU_HALO, LRU_W), F32), pltpu.VMEM((8, LRU_W), F32)],
        [dy, z, z, sink, cw, pv, wa, wx, ycat, hl, hl, uc], riders)


def _post_fwd(ycat, h0, gmix, w_out, g2, w_up, w_down, name, riders=()):
    t = h0.shape[0]
    tm = _tile(t, POST_TILE)
    nj = D_FF // FF_BLK

    def body(y_ref, h_ref, gm_ref, wo_ref, g2_ref, wu_ref, wd_ref, h1_ref, a_ref, h2_ref, ym_ref, hn_ref):
        ym, _, _ = _group_rms_fwd(y_ref[...], gm_ref[...])
        ym = ym.astype(MX)
        ym_ref[...] = ym
        h1 = h_ref[...] + jnp.dot(ym, wo_ref[...], preferred_element_type=F32)
        h1_ref[...] = h1
        hn, _, _ = _rms_fwd(h1, g2_ref[...])
        hn = hn.astype(MX)
        hn_ref[...] = hn
        for j in range(nj):
            u = jnp.dot(hn, wu_ref[j], preferred_element_type=F32)
            a_ref[:, j * FF_BLK:(j + 1) * FF_BLK] = jnp.square(jnp.maximum(u, 0.0)).astype(MX)
        h2_ref[...] = h1 + jnp.dot(a_ref[...], wd_ref[...], preferred_element_type=F32)

    tile = lambda w: pl.BlockSpec((tm, w), lambda i: (i, 0))
    return _call(
        body, name, (t // tm,),
        [tile(D_MODEL), tile(D_MODEL), _const_spec((1, D_MODEL)), _const_spec((D_MODEL, D_MODEL)),
         _const_spec((1, D_MODEL)), _const_spec((nj, D_MODEL, FF_BLK)), _const_spec((D_FF, D_MODEL))],
        [tile(D_MODEL), tile(D_FF), tile(D_MODEL), tile(D_MODEL), tile(D_MODEL)],
        [_sds((t, D_MODEL), F32), _sds((t, D_FF), MX), _sds((t, D_MODEL), F32), _sds((t, D_MODEL), MX),
         _sds((t, D_MODEL), MX)],
        [], [ycat, h0, gmix, w_out, g2, w_up, w_down], riders)


def _ffn_bwd(dh2, act, h1, g2, w_up_t, w_down, name, riders=()):
    t = h1.shape[0]
    tm = _tile(t, POST_TILE)
    nj = D_FF // FF_BLK

    def body(dh2_ref, a_ref, h1_ref, g2_ref, wut_ref, wd_ref, dh1_ref, dh1b_ref, dh2b_ref, du_ref, dg2_ref):
        @pl.when(pl.program_id(0) == 0)
        def _():
            dg2_ref[...] = jnp.zeros_like(dg2_ref)

        dh2 = dh2_ref[...]
        dh2b = dh2.astype(MX)
        dh2b_ref[...] = dh2b
        for j in range(nj):
            cols = slice(j * FF_BLK, (j + 1) * FF_BLK)
            da = _dot_nt(dh2b, wd_ref[j])
            du_ref[:, cols] = (da * (2.0 * jnp.sqrt(a_ref[:, cols].astype(F32)))).astype(MX)
        dhn = jnp.dot(du_ref[...], wut_ref[...], preferred_element_type=F32)
        _, xh, r = _rms_fwd(h1_ref[...], g2_ref[...])
        dx, dg = _rms_bwd(dhn, xh, r, g2_ref[...])
        dg2_ref[...] += dg
        dh1 = dh2 + dx
        dh1_ref[...] = dh1
        dh1b_ref[...] = dh1.astype(MX)

    tile = lambda w: pl.BlockSpec((tm, w), lambda i: (i, 0))
    return _call(
        body, name, (t // tm,),
        [tile(D_MODEL), tile(D_FF), tile(D_MODEL), _const_spec((1, D_MODEL)),
         _const_spec((D_FF, D_MODEL)), _const_spec((nj, FF_BLK, D_MODEL))],
        [tile(D_MODEL), tile(D_MODEL), tile(D_MODEL), tile(D_FF), _acc_spec((1, D_MODEL))],
        [_sds((t, D_MODEL), F32), _sds((t, D_MODEL), MX), _sds((t, D_MODEL), MX), _sds((t, D_FF), MX),
         _sds((1, D_MODEL), F32)],
        [], [dh2, act, h1, g2, w_up_t, w_down], riders)


def _mix_bwd(dh1, ycat, gmix, w_out, name):
    t = dh1.shape[0]
    tm = _tile(t)

    def body(dh1_ref, y_ref, gm_ref, wo_ref, dy_ref, dgm_ref):
        @pl.when(pl.program_id(0) == 0)
        def _():
            dgm_ref[...] = jnp.zeros_like(dgm_ref)

        dym = _dot_nt(dh1_ref[...], wo_ref[...])
        gm = gm_ref[...]
        _, yh, rr = _group_rms_fwd(y_ref[...], gm)
        outs, dgs = [], []
        for (a, b), rg in zip(_GROUPS, rr):
            dxg, dgg = _rms_bwd(dym[:, a:b], yh[:, a:b], rg, gm[:, a:b])
            outs.append(dxg)
            dgs.append(dgg)
        dy_ref[...] = jnp.concatenate(outs, axis=1)
        dgm_ref[...] += jnp.concatenate(dgs, axis=1)

    tile = pl.BlockSpec((tm, D_MODEL), lambda i: (i, 0))
    (dy, dgm), _ = _call(
        body, name, (t // tm,), [tile, tile, _const_spec((1, D_MODEL)), _const_spec((D_MODEL, D_MODEL))],
        [tile, _acc_spec((1, D_MODEL))], [_sds((t, D_MODEL), F32), _sds((1, D_MODEL), F32)],
        [], [dh1, ycat, gmix, w_out])
    return dy, dgm


def _in_bwd(dz, h0, dh1, g1, w_in, after, name):
    t = h0.shape[0]
    tm = _tile(t)

    def body(dz_ref, h_ref, dh1_ref, g_ref, w_ref, after_ref, dh0_ref, dg_ref):
        @pl.when(pl.program_id(0) == 0)
        def _():
            dg_ref[...] = jnp.zeros_like(dg_ref)

        dhn = _dot_nt(dz_ref[...], w_ref[...])
        _, xh, r = _rms_fwd(h_ref[...], g_ref[...])
        dx, dg = _rms_bwd(dhn, xh, r, g_ref[...])
        dg_ref[...] += dg
        dh0_ref[...] = dh1_ref[...] + dx

    tile = lambda w: pl.BlockSpec((tm, w), lambda i: (i, 0))
    (dh0, dg), _ = _call(
        body, name, (t // tm,),
        [tile(IN_W), tile(D_MODEL), tile(D_MODEL), _const_spec((1, D_MODEL)), _const_spec((D_MODEL, IN_W)),
         _const_spec((8, 128))],
        [tile(D_MODEL), _acc_spec((1, D_MODEL))], [_sds((t, D_MODEL), F32), _sds((1, D_MODEL), F32)],
        [], [dz, h0, dh1, g1, w_in, after])
    return dh0, dg


def _loss_head(h, gf, target, name):
    t = h.shape[0]
    tm = _tile(t)

    def body(h_ref, g_ref, t_ref, dh_ref, loss_ref, dg_ref):
        @pl.when(pl.program_id(0) == 0)
        def _():
            loss_ref[...] = jnp.zeros_like(loss_ref)
            dg_ref[...] = jnp.zeros_like(dg_ref)

        g = g_ref[...]
        y, xh, r = _rms_fwd(h_ref[...], g)
        err = y - t_ref[...]
        part = 0.5 * jnp.sum(jnp.mean(err * err, axis=-1, keepdims=True), axis=0, keepdims=True)
        loss_ref[...] += jnp.broadcast_to(part, loss_ref.shape)
        dx, dg = _rms_bwd(err * (1.0 / D_MODEL), xh, r, g)
        dg_ref[...] += dg
        dh_ref[...] = dx

    tile = pl.BlockSpec((tm, D_MODEL), lambda i: (i, 0))
    (dh, loss, dg), _ = _call(
        body, name, (t // tm,), [tile, _const_spec((1, D_MODEL)), tile],
        [tile, _acc_spec((1, 128)), _acc_spec((1, D_MODEL))],
        [_sds((t, D_MODEL), F32), _sds((1, 128), F32), _sds((1, D_MODEL), F32)], [], [h, gf, target])
    return dh, loss, dg


def _dw(x, y, name, split, bm, bn):
    t, m = x.shape
    n = y.shape[1]
    tk = _tile(t, DW_TILE)
    nk = t // tk
    if split == "rows":
        assert bn == n
        r, c = m // N_DEV, n
        per = bm // r
        out_block = pl.BlockSpec((per, r, c), lambda a, b, k: (a, 0, 0))
    else:
        assert bm == m
        r, c = m, n // N_DEV
        per = bn // c
        out_block = pl.BlockSpec((per, r, c), lambda a, b, k: (b, 0, 0))

    def body(x_ref, y_ref, o_ref, o16_ref, acc):
        k = pl.program_id(2)

        @pl.when(k == 0)
        def _():
            acc[...] = jnp.zeros_like(acc)

        acc[...] += _dot_tn(x_ref[...], y_ref[...])

        @pl.when(k == nk - 1)
        def _():
            for d in range(per):
                v = acc[d * r:(d + 1) * r, :] if split == "rows" else acc[:, d * c:(d + 1) * c]
                o_ref[d] = v
                o16_ref[d] = v.astype(o16_ref.dtype)

    return pl.pallas_call(
        body, name=name, grid=(m // bm, n // bn, nk),
        in_specs=[pl.BlockSpec((tk, bm), lambda a, b, k: (k, a)), pl.BlockSpec((tk, bn), lambda a, b, k: (k, b))],
        out_specs=[out_block, out_block],
        out_shape=[_sds((N_DEV, r, c), F32), _sds((N_DEV, r, c), WIRE)],
        scratch_shapes=[pltpu.VMEM((bm, bn), F32)],
        compiler_params=pltpu.CompilerParams(dimension_semantics=("arbitrary",) * 3, vmem_limit_bytes=VMEM_LIMIT),
    )(x, y)


def _adamw_math(w, g, m, v):
    m = ADAM_B1 * m + (1.0 - ADAM_B1) * g
    v = ADAM_B2 * v + (1.0 - ADAM_B2) * jnp.square(g)
    m_hat = m / (1.0 - ADAM_B1 ** ADAM_STEP)
    v_hat = v / (1.0 - ADAM_B2 ** ADAM_STEP)
    delta = -ADAM_LR * (m_hat / (jnp.sqrt(v_hat) + ADAM_EPS) + ADAM_WD * w)
    return delta, m, v


def _adamw_shard(g_own, g_recv, dev, w, m, v, after, name):
    _, r, c = w.shape
    br = r
    for cand in (256, 128, 112, 64, 56, 32, 16, 8):
        if r % cand == 0:
            br = cand
            break
    nr = r // br
    own = lambda l: pl.BlockSpec((1, br, c), lambda ll, i, d: (d[0], jnp.where(ll == l, i, (nr - 1) * (1 - l)), 0))
    recv = lambda l: pl.BlockSpec((N_DEV - 1, br, c), lambda ll, i, d: (0, jnp.where(ll == l, i, (nr - 1) * (1 - l)), 0))

    def body(dev_ref, go0, gr0, go1, gr1, w_ref, m_ref, v_ref, after_ref, g_out, d_out, m_out, v_out):
        def update(go_ref, gr_ref):
            g = go_ref[0]
            for j in range(N_DEV - 1):
                g = g + gr_ref[j].astype(F32)
            delta, mn, vn = _adamw_math(w_ref[0], g, m_ref[0], v_ref[0])
            g_out[0] = g
            d_out[0] = delta
            m_out[0] = mn
            v_out[0] = vn

        layer = pl.program_id(0)
        pl.when(layer == 0)(lambda: update(go0, gr0))
        pl.when(layer == 1)(lambda: update(go1, gr1))

    tile = pl.BlockSpec((1, br, c), lambda ll, i, d: (ll, i, 0))
    return pl.pallas_call(
        body, name=name,
        grid_spec=pltpu.PrefetchScalarGridSpec(
            num_scalar_prefetch=1, grid=(2, nr),
            in_specs=[own(0), recv(0), own(1), recv(1), tile, tile, tile,
                      pl.BlockSpec((8, 128), lambda ll, i, d: (0, 0))],
            out_specs=[tile, tile, tile, tile]),
        out_shape=[_sds((2, r, c), F32)] * 4,
        compiler_params=pltpu.CompilerParams(dimension_semantics=("arbitrary",) * 2, vmem_limit_bytes=VMEM_LIMIT),
    )(dev, g_own[0], g_recv[0], g_own[1], g_recv[1], w, m, v, after)


def _adamw_small(gs, ws, ms, vs, name):
    n = len(gs)

    def body(*refs):
        g_refs, w_refs, m_refs, v_refs = (refs[k * n:(k + 1) * n] for k in range(4))
        outs = refs[4 * n:]
        for k in range(n):
            delta, mn, vn = _adamw_math(w_refs[k][...], g_refs[k][...], m_refs[k][...], v_refs[k][...])
            outs[k][...] = delta
            outs[n + k][...] = mn
            outs[2 * n + k][...] = vn

    shapes = [_sds(w.shape, F32) for w in ws]
    res = pl.pallas_call(body, name=name, out_shape=shapes * 3,
                         compiler_params=pltpu.CompilerParams(vmem_limit_bytes=VMEM_LIMIT))(*gs, *ws, *ms, *vs)
    return res[:n], res[n:2 * n], res[2 * n:]


def _sum_parts(part, dev, name):
    def body(dev_ref, p_ref, o_ref):
        me = dev_ref[0]
        g = p_ref[me]
        for d in range(1, N_DEV):
            g = g + p_ref[jnp.bitwise_xor(me, d)]
        o_ref[...] = g

    full = pl.BlockSpec(part.shape, lambda i, d: (0, 0, 0))
    return pl.pallas_call(
        body, name=name,
        grid_spec=pltpu.PrefetchScalarGridSpec(
            num_scalar_prefetch=1, grid=(1,), in_specs=[full],
            out_specs=pl.BlockSpec(part.shape[1:], lambda i, d: (0, 0))),
        out_shape=_sds(part.shape[1:], F32))(dev, part)


HBM = pl.BlockSpec(memory_space=pltpu.HBM)
SEM = pl.BlockSpec(memory_space=pltpu.SEMAPHORE)
EFFECT = pltpu.SideEffectType.DATAFLOW_SIDE_EFFECTING


def _direct_copies(srcs, lands, ssem, rsem, scatter):
    x, y, c = _me()
    out = []
    for a in range(len(srcs)):
        for f in range(1, N_DEV):
            px = 1 - x if f & 4 else x
            py = 1 - y if f & 2 else y
            pc = 1 - c if f & 1 else c
            out.append(pltpu.make_async_remote_copy(
                src_ref=srcs[a].at[4 * px + 2 * py + pc] if scatter else srcs[a], dst_ref=lands[a].at[f - 1],
                send_sem=ssem.at[7 * a + f - 1], recv_sem=rsem.at[7 * a + f - 1],
                device_id=(px, py, pc), device_id_type=MESH))
    return out


def _send_start(arrays, scatter, name):
    arrays = list(arrays)
    n = len(arrays)
    lands = [lax.empty((N_DEV - 1,) + (a.shape[1:] if scatter else a.shape), a.dtype) for a in arrays]

    def body(*refs):
        srcs, lnds, ssem, rsem, token = refs[:n], refs[n:2 * n], refs[2 * n], refs[2 * n + 1], refs[-1]
        for cp in _direct_copies(srcs, lnds, ssem, rsem, scatter):
            cp.start()
        token[...] = jnp.zeros_like(token)

    hbm = lambda a: pltpu.HBM(a.shape, a.dtype)
    res = pl.pallas_call(
        body, name=name,
        out_shape=(pltpu.SemaphoreType.DMA((7 * n,)), pltpu.SemaphoreType.DMA((7 * n,)),
                   *[hbm(a) for a in arrays + lands], _sds((8, 128), F32)),
        in_specs=[HBM] * (2 * n),
        out_specs=(SEM, SEM, *[HBM] * (2 * n), pl.BlockSpec(memory_space=pltpu.VMEM)),
        input_output_aliases={i: 2 + i for i in range(2 * n)},
        compiler_params=pltpu.CompilerParams(has_side_effects=EFFECT),
    )(*[pltpu.with_memory_space_constraint(a, pltpu.HBM) for a in arrays + lands])
    return types.SimpleNamespace(ssem=res[0], rsem=res[1], srcs=list(res[2:2 + n]), lands=list(res[2 + n:2 + 2 * n]),
                                 token=res[-1], scatter=scatter)


def _send_wait(h, after, name):
    n = len(h.srcs)

    def body(*refs):
        srcs, lnds, ssem, rsem = refs[:n], refs[n:2 * n], refs[2 * n], refs[2 * n + 1]
        for cp in _direct_copies(srcs, lnds, ssem, rsem, h.scatter):
            cp.wait_send()
            cp.wait_recv()

    hbm = lambda a: pltpu.HBM(a.shape, a.dtype)
    res = pl.pallas_call(
        body, name=name,
        out_shape=tuple(hbm(a) for a in h.srcs + h.lands),
        in_specs=[HBM] * (2 * n) + [SEM, SEM, ANY], out_specs=[HBM] * (2 * n),
        input_output_aliases={i: i for i in range(2 * n)},
        compiler_params=pltpu.CompilerParams(has_side_effects=EFFECT),
    )(*h.srcs, *h.lands, h.ssem, h.rsem, after)
    return list(res[:n]), list(res[n:])


def _block_diag(w):
    out = jnp.zeros((LRU_W, LRU_W), w.dtype)
    for h in range(4):
        out = lax.dynamic_update_slice(out, w[h], (h * 64, h * 64))
    return out


def _unblock_diag(w):
    return jnp.concatenate([w[h * 64:(h + 1) * 64, h * 64:(h + 1) * 64] for h in range(4)], axis=0)


def _layer_params(p, l):
    row = lambda a: a[l].reshape(1, -1)
    sink_rows = jnp.repeat(p["attn_sinks"][l].reshape(4, 2), 2 * BLK, axis=1)
    sink_rows = jnp.concatenate([sink_rows, jnp.zeros((4, 4 * BLK), F32)], axis=0)
    cw = jnp.concatenate([p["conv_dw_w"][l], jnp.zeros((1, CONV_W), F32)], axis=0)
    pv = jnp.concatenate([
        row(p["conv_dw_b"]), row(p["conv_ln_g"]), row(p["conv_ln_b"]), row(p["lru_conv_b"]), row(p["lru_ba"]),
        row(p["lru_bx"]), row(p["lru_lambda"]), jnp.zeros((1, LRU_W), F32), p["lru_conv_w"][l],
        jnp.zeros((4, LRU_W), F32)], axis=0)
    return dict(
        g1=row(p["norm1"]), sink=sink_rows, cw=cw, pv=pv,
        wa=_block_diag(p["lru_wa"][l]).astype(MX), wx=_block_diag(p["lru_wx"][l]).astype(MX),
        gmix=row(p["mix_norm"]), g2=row(p["norm2"]))


_SMALL = ["norm1", "attn_sinks", "conv_dw_w", "conv_dw_b", "conv_ln_g", "conv_ln_b", "lru_conv_w", "lru_conv_b",
          "lru_wa", "lru_ba", "lru_wx", "lru_bx", "lru_lambda", "mix_norm", "norm2"]
_BIG = ["w_in", "w_out", "w_up", "w_down"]
_WEIGHTS = ["norm1", "w_in", "attn_sinks", "conv_dw_w", "conv_dw_b", "conv_ln_g", "conv_ln_b", "lru_conv_w",
            "lru_conv_b", "lru_wa", "lru_ba", "lru_wx", "lru_bx", "lru_lambda", "mix_norm", "w_out", "norm2", "w_up",
            "w_down", "final_norm"]


def kernel(x, norm1, w_in, attn_sinks, conv_dw_w, conv_dw_b, conv_ln_g, conv_ln_b, lru_conv_w, lru_conv_b, lru_wa, lru_ba, lru_wx, lru_bx, lru_lambda, mix_norm, w_out, norm2, w_up, w_down, final_norm, loss_target, m_norm1, m_w_in, m_attn_sinks, m_conv_dw_w, m_conv_dw_b, m_conv_ln_g, m_conv_ln_b, m_lru_conv_w, m_lru_conv_b, m_lru_wa, m_lru_ba, m_lru_wx, m_lru_bx, m_lru_lambda, m_mix_norm, m_w_out, m_norm2, m_w_up, m_w_down, m_final_norm, v_norm1, v_w_in, v_attn_sinks, v_conv_dw_w, v_conv_dw_b, v_conv_ln_g, v_conv_ln_b, v_lru_conv_w, v_lru_conv_b, v_lru_wa, v_lru_ba, v_lru_wx, v_lru_bx, v_lru_lambda, v_mix_norm, v_w_out, v_norm2, v_w_up, v_w_down, v_final_norm):
    w = dict(norm1=norm1, w_in=w_in, attn_sinks=attn_sinks, conv_dw_w=conv_dw_w, conv_dw_b=conv_dw_b,
             conv_ln_g=conv_ln_g, conv_ln_b=conv_ln_b, lru_conv_w=lru_conv_w, lru_conv_b=lru_conv_b, lru_wa=lru_wa,
             lru_ba=lru_ba, lru_wx=lru_wx, lru_bx=lru_bx, lru_lambda=lru_lambda, mix_norm=mix_norm, w_out=w_out,
             norm2=norm2, w_up=w_up, w_down=w_down, final_norm=final_norm)
    m = dict(norm1=m_norm1, w_in=m_w_in, attn_sinks=m_attn_sinks, conv_dw_w=m_conv_dw_w, conv_dw_b=m_conv_dw_b,
             conv_ln_g=m_conv_ln_g, conv_ln_b=m_conv_ln_b, lru_conv_w=m_lru_conv_w, lru_conv_b=m_lru_conv_b,
             lru_wa=m_lru_wa, lru_ba=m_lru_ba, lru_wx=m_lru_wx, lru_bx=m_lru_bx, lru_lambda=m_lru_lambda,
             mix_norm=m_mix_norm, w_out=m_w_out, norm2=m_norm2, w_up=m_w_up, w_down=m_w_down, final_norm=m_final_norm)
    v = dict(norm1=v_norm1, w_in=v_w_in, attn_sinks=v_attn_sinks, conv_dw_w=v_conv_dw_w, conv_dw_b=v_conv_dw_b,
             conv_ln_g=v_conv_ln_g, conv_ln_b=v_conv_ln_b, lru_conv_w=v_lru_conv_w, lru_conv_b=v_lru_conv_b,
             lru_wa=v_lru_wa, lru_ba=v_lru_ba, lru_wx=v_lru_wx, lru_bx=v_lru_bx, lru_lambda=v_lru_lambda,
             mix_norm=v_mix_norm, w_out=v_w_out, norm2=v_norm2, w_up=v_w_up, w_down=v_w_down, final_norm=v_final_norm)
    depth = w_in.shape[0]
    xi, yi, ci = _me()
    dev = (4 * xi + 2 * yi + ci).astype(jnp.int32)
    dev1 = dev.reshape(1)
    wb = {n: w[n].astype(MX) for n in _BIG}
    layer_shards = lambda l: [wb["w_out"][l], wb["w_up"][l], wb["w_down"][l]]

    _, ((g_in0, g_cw, g_lcw),) = _call(None, "gather_first", None, [], [], [], [], [],
                                        [_gather_rider([wb["w_in"][0], conv_dw_w, lru_conv_w])])
    cols = lambda g: jnp.moveaxis(g, 0, -2).reshape(g.shape[1:-1] + (N_DEV * g.shape[-1],))
    p = dict(w)
    p["conv_dw_w"] = cols(g_cw)
    p["lru_conv_w"] = cols(g_lcw)
    lp = [_layer_params(p, l) for l in range(depth)]

    gathered = [dict(w_in=cols(g_in0)), dict()]
    saved = []
    h = x[0]
    for l in range(depth):
        q, gw = lp[l], gathered[l]
        z, hn1 = _ln_in(h, q["g1"], gw["w_in"], f"ln_in{l}")
        riders = [_gather_rider(layer_shards(0))] if l == 0 else []
        (ycat, hl, uc), got = _mixer_fwd(z, q["sink"], q["cw"], q["pv"], q["wa"], q["wx"], f"mixer_fwd{l}", riders)
        if l == 0:
            gw["w_out"], gw["w_up"], gw["w_down"] = got[0]
            gw["w_out"] = gw["w_out"].reshape(D_MODEL, D_MODEL)
        riders = [_gather_rider([wb["w_in"][1]] + layer_shards(1))] if l == 0 else []
        (h1, act, h2, ym, hn2), got = _post_fwd(ycat, h, q["gmix"], gw["w_out"], q["g2"], gw["w_up"],
                                                gw["w_down"].reshape(D_FF, D_MODEL), f"post_fwd{l}", riders)
        if l == 0:
            nxt = gathered[1]
            nxt["w_in"], nxt["w_out"], nxt["w_up"], nxt["w_down"] = got[0]
            nxt["w_in"] = cols(nxt["w_in"])
            nxt["w_out"] = nxt["w_out"].reshape(D_MODEL, D_MODEL)
        saved.append(dict(h0=h, z=z, hn1=hn1, ycat=ycat, hl=hl, uc=uc, h1=h1, act=act, ym=ym, hn2=hn2))
        h = h2
    dh, loss, dgf = _loss_head(h, final_norm.reshape(1, -1), loss_target[0], "loss_head")

    grads = [None] * depth
    big = {n: [None] * depth for n in _BIG}
    pending = []

    def send_pending():
        riders = [_scatter_rider([item[3] for item in pending])] if pending else []
        return riders, list(pending)

    def record(sent, got):
        for item, recv in zip(sent, got[0] if sent else []):
            big[item[0]][item[1]] = (item[2], recv)
        del pending[:len(sent)]

    for l in reversed(range(depth)):
        q, s, gw = lp[l], saved[l], gathered[l]
        riders, sent = send_pending()
        w_up_t = jnp.swapaxes(gw["w_up"], 1, 2).reshape(D_FF, D_MODEL)
        (dh1, dh1b, dhb, du, dg2), got = _ffn_bwd(dh, s["act"], s["h1"], q["g2"], w_up_t, gw["w_down"],
                                                  f"ffn_bwd{l}", riders)
        record(sent, got)
        dycat, dgm = _mix_bwd(dh1b, s["ycat"], q["gmix"], gw["w_out"], f"mix_bwd{l}")
        pending.append(("w_down", l) + tuple(_dw(s["act"], dhb, f"dw_down{l}", "rows", 2048, D_MODEL)))
        pending.append(("w_up", l) + tuple(_dw(s["hn2"], du, f"dw_up{l}", "cols", D_MODEL, 2048)))
        pending.append(("w_out", l) + tuple(_dw(s["ym"], dh1b, f"dw_out{l}", "rows", D_MODEL, D_MODEL)))
        riders, sent = send_pending()
        (dz, dsink, dcw, dpv, dwa, dwx), got = _mixer_bwd(dycat, s["z"], s["ycat"], s["hl"], s["uc"], q["sink"],
                                                          q["cw"], q["pv"], q["wa"], q["wx"], f"mixer_bwd{l}", riders)
        record(sent, got)
        d_win = _dw(s["hn1"], dz, f"dw_in{l}", "cols", D_MODEL, IN_W)
        if l > 0:
            pending.append(("w_in", l) + tuple(d_win))
            after = jnp.zeros((8, 128), F32)
        else:
            win_sends = _send_start([d_win[1]], True, "scatter_w_in0_start")
            after = win_sends.token
        dh, dg1 = _in_bwd(dz, s["h0"], dh1, q["g1"], gw["w_in"], after, f"in_bwd{l}")
        grads[l] = dict(
            norm1=dg1[0], attn_sinks=jnp.stack([dsink[0:4, 0], dsink[0:4, 2 * BLK]], axis=1).reshape(8),
            conv_dw_w=dcw[0:CONV_K], conv_dw_b=dpv[R_CONV_B], conv_ln_g=dpv[R_LN_G], conv_ln_b=dpv[R_LN_B],
            lru_conv_w=dpv[R_LCW:R_LCW + LRU_K], lru_conv_b=dpv[R_LCONV_B], lru_wa=_unblock_diag(dwa),
            lru_ba=dpv[R_BA].reshape(4, 64), lru_wx=_unblock_diag(dwx), lru_bx=dpv[R_BX].reshape(4, 64),
            lru_lambda=dpv[R_LAM], mix_norm=dgm[0], norm2=dg2[0])

    small = [jnp.stack([grads[l][n] for l in range(depth)]) for n in _SMALL] + [dgf, loss[:, 0:1]]
    sizes = [a.size for a in small]
    total = -(-sum(sizes) // 1024) * 1024
    packed = jnp.concatenate([a.reshape(-1) for a in small] + [jnp.zeros((total - sum(sizes),), F32)])
    packed = packed.reshape(total // 128, 128)
    small_sends = _send_start([packed], False, "bcast_small_start")

    out = {}
    shard_update = lambda n, after: list(_adamw_shard(
        [big[n][l][0] for l in range(depth)], [big[n][l][1] for l in range(depth)], dev1, w[n], m[n], v[n], after,
        f"adamw_{n}"))
    for n in ("w_out", "w_up", "w_down"):
        out[n] = shard_update(n, small_sends.token)
    _, (win_recv,) = _send_wait(win_sends, out["w_down"][1], "scatter_w_in0_wait")
    big["w_in"][0] = (d_win[0], win_recv)
    (packed,), (small_recv,) = _send_wait(small_sends, win_recv, "bcast_small_wait")
    out["w_in"] = shard_update("w_in", jnp.zeros((8, 128), F32))
    parts = jnp.concatenate([packed[None], small_recv], axis=0)
    summed = _sum_parts(parts, dev1, "sum_small_grads").reshape(-1)
    small_sums, pos = [], 0
    for a, size in zip(small, sizes):
        small_sums.append(summed[pos:pos + size].reshape(a.shape))
        pos += size
    shard = lambda a: lax.dynamic_slice_in_dim(a, dev * (a.shape[-1] // N_DEV), a.shape[-1] // N_DEV, axis=a.ndim - 1)
    flat = {"lru_wa": (depth, LRU_W, 64), "lru_wx": (depth, LRU_W, 64), "final_norm": (1, D_MODEL)}
    gs, ws, ms, vs = [], [], [], []
    for n, g in zip(_SMALL + ["final_norm"], small_sums[:-1]):
        shp = flat.get(n, w[n].shape)
        gs.append((shard(g) if n in ("conv_dw_w", "lru_conv_w") else g).reshape(shp))
        ws.append(w[n].reshape(shp))
        ms.append(m[n].reshape(shp))
        vs.append(v[n].reshape(shp))
    sd, sm, sv = _adamw_small(gs, ws, ms, vs, "adamw_small")
    for j, n in enumerate(_SMALL + ["final_norm"]):
        out[n] = [a.reshape(w[n].shape) for a in (gs[j], sd[j], sm[j], sv[j])]
    loss_total = small_sums[-1][0, 0]

    result = [loss_total, dh[None]]
    for j in range(4):
        result += [out[n][j] for n in _WEIGHTS]
    return tuple(result)
```

```python
import types

import jax
import jax.numpy as jnp
from jax import lax
from jax.experimental import pallas as pl
from jax.experimental.pallas import tpu as pltpu

F32 = jnp.float32
MX = jnp.bfloat16
WIRE = jnp.bfloat16

D_MODEL = 1024
HEAD_DIM = 64
ATTN_W = 512
KV_W = 128
BLK = 128
CONV_W = 256
CONV_K = 31
LRU_W = 256
LRU_K = 4
LRU_C = 8.0
IN_W = 1792
D_FF = 4096
FF_BLK = 512
N_DEV = 8
IN_SHARD = IN_W // N_DEV
RMS_EPS = 1e-6
LN_EPS = 1e-5
MASK_VALUE = -1e30
SCALE = HEAD_DIM ** -0.5
CONV_HALO = 32
LRU_HALO = 8
CONV_CHUNK = 64
POST_TILE = 512
DW_TILE = 1024
Q0, K0, V0, CV0, CG0, RX0, RG0 = 0, 512, 640, 768, 1024, 1280, 1536
R_CONV_B, R_LN_G, R_LN_B, R_LCONV_B, R_BA, R_BX, R_LAM, R_LCW = 0, 1, 2, 3, 4, 5, 6, 8

ADAM_LR, ADAM_B1, ADAM_B2, ADAM_EPS, ADAM_WD, ADAM_STEP = 0.001, 0.9, 0.999, 1e-08, 0.01, 10

VMEM_LIMIT = 56 * 1024 * 1024
MESH = pl.DeviceIdType.MESH
ANY = pl.BlockSpec(memory_space=pl.ANY)


def _tile(t, cap=512):
    return min(cap, t)


def _dot(a, b):
    return jnp.dot(a.astype(MX), b.astype(MX), preferred_element_type=F32)


def _dot_nt(a, b):
    return lax.dot_general(a.astype(MX), b.astype(MX), (((1,), (1,)), ((), ())), preferred_element_type=F32)


def _dot_tn(a, b):
    return lax.dot_general(a.astype(MX), b.astype(MX), (((0,), (0,)), ((), ())), preferred_element_type=F32)


def _const_spec(shape):
    nd = len(shape)
    return pl.BlockSpec(shape, lambda *_: (0,) * nd, pipeline_mode=pl.Buffered(1))


def _acc_spec(shape):
    nd = len(shape)
    return pl.BlockSpec(shape, lambda *_: (0,) * nd)


def _sds(shape, dtype):
    return jax.ShapeDtypeStruct(shape, dtype)


def _sigmoid(x):
    return jax.nn.sigmoid(x)


def _rms_fwd(x, g):
    r = lax.rsqrt(jnp.mean(x * x, axis=-1, keepdims=True) + RMS_EPS)
    xh = x * r
    return xh * g, xh, r


def _rms_bwd(dy, xh, r, g):
    t = dy * g
    dx = r * (t - xh * jnp.mean(t * xh, axis=-1, keepdims=True))
    return dx, jnp.sum(dy * xh, axis=0, keepdims=True)


_GROUPS = ((0, 512), (512, 768), (768, 1024))


def _group_rms_fwd(y, g):
    parts = [_rms_fwd(y[:, a:b], g[:, a:b]) for a, b in _GROUPS]
    return (jnp.concatenate([p[0] for p in parts], axis=1),
            jnp.concatenate([p[1] for p in parts], axis=1),
            [p[2] for p in parts])


def _gelu(x):
    c = 0.7978845608028654
    u = c * (x + 0.044715 * x * x * x)
    th = jnp.tanh(u)
    val = 0.5 * x * (1.0 + th)
    grad = 0.5 * (1.0 + th) + 0.5 * x * (1.0 - th * th) * c * (1.0 + 3.0 * 0.044715 * x * x)
    return val, grad


def _neg_expm1(x):
    series = -x * (1.0 + x * (0.5 + x * (1.0 / 6.0 + x * (1.0 / 24.0))))
    return jnp.where(x > -0.02, series, 1.0 - jnp.exp(x))


def _me():
    return lax.axis_index("x"), lax.axis_index("y"), lax.axis_index("c")


def _gather_rider(arrays):
    arrays = list(arrays)
    n = len(arrays)

    def plan(ins, outs, sems):
        ssem, rsem, lsem = sems
        x, y, c = _me()
        chips = [(1 - x, y), (x, 1 - y), (1 - x, 1 - y)]

        def copy(a, k, block, to, own=False):
            dst = outs[a].at[4 * block[0] + 2 * block[1] + block[2]]
            return pltpu.make_async_remote_copy(
                src_ref=ins[a] if own else dst, dst_ref=dst, send_sem=ssem.at[7 * a + k],
                recv_sem=rsem.at[7 * a + k], device_id=to, device_id_type=MESH)

        return x, y, c, chips, copy, lsem

    def start(ins, outs, sems):
        x, y, c, chips, copy, lsem = plan(ins, outs, sems)
        for a in range(n):
            pltpu.make_async_copy(ins[a], outs[a].at[4 * x + 2 * y + c], lsem.at[a]).start()
            copy(a, 0, (x, y, c), (x, y, 1 - c), own=True).start()
            for j, chip in enumerate(chips):
                copy(a, 1 + j, (x, y, c), (*chip, c), own=True).start()

    def mid(ins, outs, sems):
        x, y, c, chips, copy, _ = plan(ins, outs, sems)
        for a in range(n):
            for j, chip in enumerate(chips):
                copy(a, 1 + j, (*chip, c), (x, y, c)).wait_recv()
                copy(a, 4 + j, (*chip, c), (x, y, 1 - c)).start()

    def finish(ins, outs, sems):
        x, y, c, chips, copy, lsem = plan(ins, outs, sems)
        for a in range(n):
            copy(a, 0, (x, y, 1 - c), (x, y, c)).wait_recv()
            for j, chip in enumerate(chips):
                copy(a, 4 + j, (*chip, 1 - c), (x, y, c)).wait_recv()
        for a in range(n):
            copy(a, 0, (x, y, c), (x, y, 1 - c), own=True).wait_send()
            for j, chip in enumerate(chips):
                copy(a, 1 + j, (x, y, c), (*chip, c), own=True).wait_send()
                copy(a, 4 + j, (*chip, c), (x, y, 1 - c)).wait_send()
            pltpu.make_async_copy(ins[a], outs[a].at[4 * x + 2 * y + c], lsem.at[a]).wait()

    return types.SimpleNamespace(
        arrays=arrays, out_shape=[_sds((N_DEV,) + a.shape, a.dtype) for a in arrays],
        scratch=[pltpu.SemaphoreType.DMA((7 * n,)), pltpu.SemaphoreType.DMA((7 * n,)), pltpu.SemaphoreType.DMA((n,))],
        start=start, mid=mid, finish=finish)


def _bcast_rider(arrays):
    arrays = list(arrays)
    n = len(arrays)

    def copies(ins, outs, sems, landing):
        ssem, rsem, lsem = sems
        x, y, c = _me()
        out = []
        for a in range(n):
            out.append(pltpu.make_async_copy(ins[a], outs[a].at[4 * x + 2 * y + c], lsem.at[a]))
            for f in range(1, N_DEV):
                px = 1 - x if f & 4 else x
                py = 1 - y if f & 2 else y
                pc = 1 - c if f & 1 else c
                slot = 4 * px + 2 * py + pc if landing else 4 * x + 2 * y + c
                out.append(pltpu.make_async_remote_copy(
                    src_ref=ins[a], dst_ref=outs[a].at[slot], send_sem=ssem.at[7 * a + f - 1],
                    recv_sem=rsem.at[7 * a + f - 1], device_id=(px, py, pc), device_id_type=MESH))
        return out

    def start(ins, outs, sems):
        for cp in copies(ins, outs, sems, landing=False):
            cp.start()

    def finish(ins, outs, sems):
        for cp in copies(ins, outs, sems, landing=True):
            cp.wait()

    return types.SimpleNamespace(
        arrays=arrays, out_shape=[_sds((N_DEV,) + a.shape, a.dtype) for a in arrays],
        scratch=[pltpu.SemaphoreType.DMA((7 * n,)), pltpu.SemaphoreType.DMA((7 * n,)), pltpu.SemaphoreType.DMA((n,))],
        start=start, mid=None, finish=finish)


def _scatter_rider(arrays):
    arrays = list(arrays)
    n = len(arrays)

    def copies(ins, outs, sems):
        ssem, rsem = sems
        x, y, c = _me()
        out = []
        for a in range(n):
            for f in range(1, N_DEV):
                px = 1 - x if f & 4 else x
                py = 1 - y if f & 2 else y
                pc = 1 - c if f & 1 else c
                out.append(pltpu.make_async_remote_copy(
                    src_ref=ins[a].at[4 * px + 2 * py + pc], dst_ref=outs[a].at[f - 1], send_sem=ssem.at[7 * a + f - 1],
                    recv_sem=rsem.at[7 * a + f - 1], device_id=(px, py, pc), device_id_type=MESH))
        return out

    def start(ins, outs, sems):
        for cp in copies(ins, outs, sems):
            cp.start()

    def finish(ins, outs, sems):
        for cp in copies(ins, outs, sems):
            cp.wait()

    return types.SimpleNamespace(
        arrays=arrays, out_shape=[_sds((N_DEV - 1,) + a.shape[1:], a.dtype) for a in arrays],
        scratch=[pltpu.SemaphoreType.DMA((7 * n,)), pltpu.SemaphoreType.DMA((7 * n,))],
        start=start, mid=None, finish=finish)


def _call(body, name, grid, in_specs, out_specs, out_shape, scratch, operands, riders=()):
    n_in, n_out, n_scr = len(operands), len(out_shape), len(scratch)
    nsteps = grid[0] if grid else 1
    sizes = [(len(r.arrays), len(r.out_shape), len(r.scratch)) for r in riders]

    def wrapped(*refs):
        pos = n_in
        r_ins = []
        for ri, _, _ in sizes:
            r_ins.append(refs[pos:pos + ri])
            pos += ri
        outs = refs[pos:pos + n_out]
        pos += n_out
        r_outs = []
        for _, ro, _ in sizes:
            r_outs.append(refs[pos:pos + ro])
            pos += ro
        scr = refs[pos:pos + n_scr]
        pos += n_scr
        r_sems = []
        for _, _, rs in sizes:
            r_sems.append(refs[pos:pos + rs])
            pos += rs
        step = pl.program_id(0) if grid else 0

        def at(s, fn):
            if grid:
                pl.when(step == s)(fn)
            else:
                fn()

        for r, a, b, c in zip(riders, r_ins, r_outs, r_sems):
            at(0, lambda r=r, a=a, b=b, c=c: r.start(a, b, c))
        for r, a, b, c in zip(riders, r_ins, r_outs, r_sems):
            if r.mid is not None:
                at((3 * nsteps) // 4, lambda r=r, a=a, b=b, c=c: r.mid(a, b, c))
        if body is not None:
            body(*refs[:n_in], *outs, *scr)
        for r, a, b, c in zip(riders, r_ins, r_outs, r_sems):
            at(nsteps - 1, lambda r=r, a=a, b=b, c=c: r.finish(a, b, c))

    r_arrays = [a for r in riders for a in r.arrays]
    r_shapes = [s for r in riders for s in r.out_shape]
    kwargs = {}
    if grid:
        kwargs = dict(grid=grid, compiler_params=pltpu.CompilerParams(
            dimension_semantics=("arbitrary",) * len(grid), vmem_limit_bytes=VMEM_LIMIT))
    res = pl.pallas_call(
        wrapped, name=name,
        in_specs=list(in_specs) + [ANY] * len(r_arrays),
        out_specs=list(out_specs) + [ANY] * len(r_shapes),
        out_shape=list(out_shape) + r_shapes,
        scratch_shapes=list(scratch) + [s for r in riders for s in r.scratch],
        **kwargs,
    )(*operands, *r_arrays)
    host, rest = res[:n_out], res[n_out:]
    r_res = []
    for _, ro, _ in sizes:
        r_res.append(rest[:ro])
        rest = rest[ro:]
    return host, r_res


def _ln_in(h, g1, w_in, name):
    t = h.shape[0]
    tm = _tile(t)

    def body(h_ref, g_ref, w_ref, z_ref, hn_ref):
        y, _, _ = _rms_fwd(h_ref[...], g_ref[...])
        hn = y.astype(MX)
        hn_ref[...] = hn
        z_ref[...] = jnp.dot(hn, w_ref[...], preferred_element_type=F32)

    tile = lambda w: pl.BlockSpec((tm, w), lambda i: (i, 0))
    (z, hn), _ = _call(
        body, name, (t // tm,),
        [tile(D_MODEL), _const_spec((1, D_MODEL)), _const_spec((D_MODEL, IN_W))],
        [tile(IN_W), tile(D_MODEL)], [_sds((t, IN_W), F32), _sds((t, D_MODEL), MX)], [], [h, g1, w_in])
    return z, hn


def _band2(kb, g):
    lo = lax.broadcasted_iota(jnp.int32, kb.shape, 1) < HEAD_DIM
    kr = pltpu.roll(kb, HEAD_DIM, 1)
    if g == 0:
        top, bot = jnp.where(lo, kb, 0.0), jnp.where(lo, 0.0, kr)
    else:
        top, bot = jnp.where(lo, kr, 0.0), jnp.where(lo, 0.0, kb)
    return jnp.concatenate([top, bot], axis=0)


def _attn_block(z_ref, zh_ref, sink_ref, b, first):
    rows = slice(b * BLK, (b + 1) * BLK)
    prev = zh_ref if b == 0 else z_ref
    prow = slice(0, BLK) if b == 0 else slice((b - 1) * BLK, b * BLK)
    kb = jnp.concatenate([prev[prow, K0:K0 + KV_W], z_ref[rows, K0:K0 + KV_W]], axis=0)
    vb = jnp.concatenate([prev[prow, V0:V0 + KV_W], z_ref[rows, V0:V0 + KV_W]], axis=0)
    k2 = [_band2(kb, g) for g in range(2)]
    v2 = [_band2(vb, g) for g in range(2)]
    q2 = [jnp.concatenate([z_ref[rows, (2 * g) * BLK:(2 * g + 1) * BLK], z_ref[rows, (2 * g + 1) * BLK:(2 * g + 2) * BLK]],
                          axis=0) for g in range(2)]
    rr = lax.broadcasted_iota(jnp.int32, (4 * BLK, 2 * BLK), 0) & (BLK - 1)
    cc = lax.broadcasted_iota(jnp.int32, (4 * BLK, 2 * BLK), 1)
    first_block = jnp.logical_and(first, b == 0).astype(jnp.int32)
    mask = jnp.logical_and(jnp.logical_and(cc > rr, cc <= rr + BLK), cc >= BLK * first_block)
    s = jnp.concatenate([_dot_nt(q2[g], k2[g]) for g in range(2)], axis=0) * SCALE
    w = 2 * BLK
    out, psink = [], []
    for hh in range(2):
        sh = jnp.where(mask, s[:, hh * w:(hh + 1) * w], MASK_VALUE)
        sk = jnp.concatenate([jnp.broadcast_to(sink_ref[p:p + 1, hh * w:hh * w + 1], (BLK, 1)) for p in range(4)], axis=0)
        m = jnp.maximum(jnp.max(sh, axis=1, keepdims=True), sk)
        p = jnp.exp(sh - m)
        es = jnp.exp(sk - m)
        inv = 1.0 / (jnp.sum(p, axis=1, keepdims=True) + es)
        out.append(p * inv)
        psink.append(es * inv)
    return q2, k2, v2, jnp.concatenate(out, axis=1), psink


def _scan_steps(a, b, n, span, reverse):
    pos = lax.broadcasted_iota(jnp.int32, a.shape, 0) & (span - 1)
    d = 1
    while d < span:
        keep = pos < span - d if reverse else pos >= d
        shift = n - d if reverse else d
        a_sh = jnp.where(keep, pltpu.roll(a, shift, 0), 1.0)
        b_sh = jnp.where(keep, pltpu.roll(b, shift, 0), 0.0)
        b = a * b_sh + b
        a = a * a_sh
        d *= 2
    return a, b


def _scan(a, b, tm, reverse):
    return _scan_steps(a, b, tm, tm, reverse)


def _shifted_copies(ext, shifts, tm):
    rows = tm + CONV_HALO - 8
    for r in range(1, 8):
        shifts[r - 1, 0:rows, :] = ext[pl.ds(r, rows), :]


def _tap(ext, shifts, off, r0, n):
    a, r = divmod(off, 8)
    lo = 8 * a + r0
    if r == 0:
        return ext[lo:lo + n, :]
    return shifts[r - 1, lo:lo + n, :]


def _glu_fill(z_ref, zh_ref, uext, ush, first, tm, sg_out=None):
    cv = z_ref[:, CV0:CV0 + CONV_W]
    sg = _sigmoid(z_ref[:, CG0:CG0 + CONV_W])
    if sg_out is not None:
        sg_out[...] = sg
    hrow = BLK - CONV_HALO
    uh = zh_ref[hrow:BLK, CV0:CV0 + CONV_W] * _sigmoid(zh_ref[hrow:BLK, CG0:CG0 + CONV_W])
    uext[0:CONV_HALO, :] = jnp.where(first, 0.0, uh)
    uext[CONV_HALO:CONV_HALO + tm, :] = cv * sg
    _shifted_copies(uext, ush, tm)


def _conv_taps(cw_ref, pv_ref, uext, ush, out_ref, tm):
    for r0 in range(0, tm, CONV_CHUNK):
        acc = jnp.broadcast_to(pv_ref[R_CONV_B:R_CONV_B + 1, :], (CONV_CHUNK, CONV_W))
        for k in range(CONV_K):
            acc = acc + cw_ref[k:k + 1, :] * _tap(uext, ush, CONV_HALO - (CONV_K - 1) + k, r0, CONV_CHUNK)
        out_ref[r0:r0 + CONV_CHUNK, :] = acc


def _ln_silu(uc, pv_ref):
    mu = jnp.mean(uc, axis=-1, keepdims=True)
    xc = uc - mu
    rs = lax.rsqrt(jnp.mean(xc * xc, axis=-1, keepdims=True) + LN_EPS)
    xh = xc * rs
    ln = xh * pv_ref[R_LN_G:R_LN_G + 1, :] + pv_ref[R_LN_B:R_LN_B + 1, :]
    sg = _sigmoid(ln)
    return xh, rs, ln, sg


def _lru_gates(z_ref, zh_ref, pv_ref, wa_ref, wx_ref, rxext, first, tm):
    rxext[0:LRU_HALO, :] = jnp.where(first, 0.0, zh_ref[BLK - LRU_HALO:BLK, RX0:RX0 + LRU_W])
    rxext[LRU_HALO:LRU_HALO + tm, :] = z_ref[:, RX0:RX0 + LRU_W]
    xc = jnp.broadcast_to(pv_ref[R_LCONV_B:R_LCONV_B + 1, :], (tm, LRU_W))
    for k in range(LRU_K):
        xc = xc + pv_ref[R_LCW + k:R_LCW + k + 1, :] * rxext[pl.ds(LRU_HALO - (LRU_K - 1) + k, tm), :]
    r = _sigmoid(_dot(xc, wa_ref[...]) + pv_ref[R_BA:R_BA + 1, :])
    ig = _sigmoid(_dot(xc, wx_ref[...]) + pv_ref[R_BX:R_BX + 1, :])
    lam = pv_ref[R_LAM:R_LAM + 1, :]
    sp = jnp.log1p(jnp.exp(-lam))
    la = (-LRU_C * r) * sp
    a = jnp.exp(la)
    mult = jnp.sqrt(_neg_expm1(2.0 * la))
    return xc, r, ig, sp, la, a, mult


def _mixer_in_specs(tm, tile_of):
    hb = tm // BLK
    return [
        pl.BlockSpec((tm, IN_W), lambda i: (tile_of(i), 0)),
        pl.BlockSpec((BLK, IN_W), lambda i: (jnp.maximum(tile_of(i) * hb - 1, 0), 0)),
        _const_spec((8, 4 * BLK)),
        _const_spec((32, CONV_W)),
        _const_spec((16, CONV_W)),
        _const_spec((LRU_W, LRU_W)),
        _const_spec((LRU_W, LRU_W)),
    ]


def _mixer_fwd(z, sink, cw, pv, wa, wx, name, riders=()):
    t = z.shape[0]
    tm = _tile(t)
    nb = tm // BLK

    def body(z_ref, zh_ref, sink_ref, cw_ref, pv_ref, wa_ref, wx_ref, y_ref, hl_ref, uc_ref, uext, ush, rxext, hcar):
        i = pl.program_id(0)
        first = i == 0

        @pl.when(first)
        def _():
            hcar[...] = jnp.zeros_like(hcar)

        for b in range(nb):
            rows = slice(b * BLK, (b + 1) * BLK)
            _, _, v2, prob, _ = _attn_block(z_ref, zh_ref, sink_ref, b, first)
            for g in range(2):
                o = _dot(prob[2 * g * BLK:(2 * g + 2) * BLK], v2[g])
                y_ref[rows, (2 * g) * BLK:(2 * g + 1) * BLK] = o[0:BLK]
                y_ref[rows, (2 * g + 1) * BLK:(2 * g + 2) * BLK] = o[BLK:2 * BLK]
        _glu_fill(z_ref, zh_ref, uext, ush, first, tm)
        _conv_taps(cw_ref, pv_ref, uext, ush, uc_ref, tm)
        _, _, ln, sg = _ln_silu(uc_ref[...], pv_ref)
        y_ref[:, ATTN_W:ATTN_W + CONV_W] = ln * sg
        xc, _, ig, _, _, a, mult = _lru_gates(z_ref, zh_ref, pv_ref, wa_ref, wx_ref, rxext, first, tm)
        acum, h = _scan(a, mult * (ig * xc), tm, reverse=False)
        h = h + acum * hcar[0:1, :]
        hl_ref[...] = h
        hcar[0:1, :] = h[tm - 1:tm, :]
        gl, _ = _gelu(z_ref[:, RG0:RG0 + LRU_W])
        y_ref[:, ATTN_W + CONV_W:ATTN_W + CONV_W + LRU_W] = h * gl

    tile = lambda w: pl.BlockSpec((tm, w), lambda i: (i, 0))
    return _call(
        body, name, (t // tm,), _mixer_in_specs(tm, lambda i: i),
        [tile(D_MODEL), tile(LRU_W), tile(CONV_W)],
        [_sds((t, D_MODEL), F32), _sds((t, LRU_W), F32), _sds((t, CONV_W), F32)],
        [pltpu.VMEM((tm + CONV_HALO, CONV_W), F32), pltpu.VMEM((7, tm + CONV_HALO - 8, CONV_W), F32),
         pltpu.VMEM((tm + LRU_HALO, LRU_W), F32), pltpu.VMEM((8, LRU_W), F32)],
        [z, z, sink, cw, pv, wa, wx], riders)


def _mixer_bwd(dy, z, ycat, hl, uc, sink, cw, pv, wa, wx, name, riders=()):
    t = z.shape[0]
    tm = _tile(t)
    nt = t // tm
    nb = tm // BLK
    rev = lambda i: nt - 1 - i

    def body(dy_ref, z_ref, zh_ref, sink_ref, cw_ref, pv_ref, wa_ref, wx_ref, y_ref, hl_ref, hlh_ref, uc_ref,
             dz_ref, dsink_ref, dcw_ref, dpv_ref, dwa_ref, dwx_ref,
             uext, ush, sgs, rxext, dkext, dvext, ducext, dsh, dcw8, dxcext, kcar, vcar, uccar, xccar, gcar):
        i = pl.program_id(0)
        first = i == nt - 1

        @pl.when(i == 0)
        def _():
            for car in (kcar, vcar, uccar, xccar, gcar, dcw8):
                car[...] = jnp.zeros_like(car)
            for acc in (dsink_ref, dpv_ref, dwa_ref, dwx_ref):
                acc[...] = jnp.zeros_like(acc)

        def addrow(r, val):
            dpv_ref[r:r + 1, :] += jnp.sum(val, axis=0, keepdims=True)

        dkext[:, 0:tm] = jnp.zeros((KV_W, tm), F32)
        dvext[:, 0:tm] = jnp.zeros((KV_W, tm), F32)
        dkext[:, tm:tm + BLK] = kcar[...]
        dvext[:, tm:tm + BLK] = vcar[...]
        lane512 = lax.broadcasted_iota(jnp.int32, (1, 4 * BLK), 1) < 2 * BLK
        lo = lax.broadcasted_iota(jnp.int32, (4 * BLK, BLK), 1) < HEAD_DIM
        hd, w2 = HEAD_DIM, 2 * BLK
        for b in range(nb):
            rows = slice(b * BLK, (b + 1) * BLK)
            band = slice(b * BLK, (b + 2) * BLK)
            q2, k2, v2, prob, psink = _attn_block(z_ref, zh_ref, sink_ref, b, first)
            stack = lambda ref: jnp.concatenate([ref[rows, p * BLK:(p + 1) * BLK] for p in range(4)], axis=0)
            do4 = stack(dy_ref)
            dlt = do4 * stack(y_ref)
            d0 = jnp.sum(jnp.where(lo, dlt, 0.0), axis=1, keepdims=True)
            d1 = jnp.sum(jnp.where(lo, 0.0, dlt), axis=1, keepdims=True)
            dp = jnp.concatenate([_dot_nt(do4[g * w2:(g + 1) * w2], v2[g]) for g in range(2)], axis=0)
            dl = jnp.concatenate([jnp.broadcast_to(d0, (4 * BLK, w2)), jnp.broadcast_to(d1, (4 * BLK, w2))], axis=1)
            draw = (prob * (dp - dl)) * SCALE
            e0, e1 = psink[0] * d0, psink[1] * d1
            for p in range(4):
                prs = slice(p * BLK, (p + 1) * BLK)
                s0 = jnp.sum(e0[prs], axis=0, keepdims=True)
                s1 = jnp.sum(e1[prs], axis=0, keepdims=True)
                dsink_ref[p:p + 1, :] += -jnp.where(lane512, s0, s1)
            for g in range(2):
                grs = slice(g * w2, (g + 1) * w2)
                dq = _dot(draw[grs], k2[g])
                dz_ref[rows, (2 * g) * BLK:(2 * g + 1) * BLK] = dq[0:BLK].astype(dz_ref.dtype)
                dz_ref[rows, (2 * g + 1) * BLK:(2 * g + 2) * BLK] = dq[BLK:2 * BLK].astype(dz_ref.dtype)
                tk = _dot_tn(q2[g], draw[grs])
                tv = _dot_tn(do4[grs], prob[grs])
                dkext[g * hd:(g + 1) * hd, band] += tk[0:hd, 0:w2] + tk[hd:2 * hd, w2:2 * w2]
                dvext[g * hd:(g + 1) * hd, band] += tv[0:hd, 0:w2] + tv[hd:2 * hd, w2:2 * w2]
        dz_ref[:, K0:K0 + KV_W] = jnp.transpose(dkext[:, BLK:BLK + tm]).astype(dz_ref.dtype)
        dz_ref[:, V0:V0 + KV_W] = jnp.transpose(dvext[:, BLK:BLK + tm]).astype(dz_ref.dtype)
        kcar[...] = dkext[:, 0:BLK]
        vcar[...] = dvext[:, 0:BLK]

        _glu_fill(z_ref, zh_ref, uext, ush, first, tm, sg_out=sgs)
        xh, rs, ln, sg = _ln_silu(uc_ref[...], pv_ref)
        dln = dy_ref[:, ATTN_W:ATTN_W + CONV_W] * (sg * (1.0 + ln * (1.0 - sg)))
        addrow(R_LN_G, dln * xh)
        addrow(R_LN_B, dln)
        dxh = dln * pv_ref[R_LN_G:R_LN_G + 1, :]
        duc = rs * (dxh - jnp.mean(dxh, axis=-1, keepdims=True) - xh * jnp.mean(dxh * xh, axis=-1, keepdims=True))
        addrow(R_CONV_B, duc)
        ducext[0:tm, :] = duc
        ducext[tm:tm + CONV_HALO, :] = uccar[...]
        uccar[...] = duc[0:CONV_HALO, :]
        _shifted_copies(ducext, dsh, tm)
        for r0 in range(0, tm, CONV_CHUNK):
            crow = slice(r0, r0 + CONV_CHUNK)
            duc_c = ducext[crow, :]
            du = jnp.zeros((CONV_CHUNK, CONV_W), F32)
            for k in range(CONV_K):
                prod = duc_c * _tap(uext, ush, CONV_HALO - (CONV_K - 1) + k, r0, CONV_CHUNK)
                part = prod[0:8]
                for s in range(8, CONV_CHUNK, 8):
                    part = part + prod[s:s + 8]
                dcw8[k] += part
                du = du + cw_ref[k:k + 1, :] * _tap(ducext, dsh, CONV_K - 1 - k, r0, CONV_CHUNK)
            sgc = sgs[crow, :]
            dz_ref[crow, CV0:CV0 + CONV_W] = (du * sgc).astype(dz_ref.dtype)
            u_c = uext[CONV_HALO + r0:CONV_HALO + r0 + CONV_CHUNK, :]
            dz_ref[crow, CG0:CG0 + CONV_W] = (du * u_c * (1.0 - sgc)).astype(dz_ref.dtype)

        @pl.when(i == nt - 1)
        def _():
            dcw_ref[...] = jnp.sum(dcw8[...], axis=1)

        xc, r, ig, sp, la, a, mult = _lru_gates(z_ref, zh_ref, pv_ref, wa_ref, wx_ref, rxext, first, tm)
        h = hl_ref[...]
        rowi = lax.broadcasted_iota(jnp.int32, (tm, LRU_W), 0)
        hlast = jnp.where(first, 0.0, hlh_ref[7:8, :])
        hprev = jnp.where(rowi == 0, hlast, pltpu.roll(h, 1, 0))
        dyl = dy_ref[:, ATTN_W + CONV_W:ATTN_W + CONV_W + LRU_W]
        gl, dgl = _gelu(z_ref[:, RG0:RG0 + LRU_W])
        dz_ref[:, RG0:RG0 + LRU_W] = (dyl * h * dgl).astype(dz_ref.dtype)
        dh = dyl * gl + jnp.where(rowi == tm - 1, gcar[0:1, :], 0.0)
        c = jnp.where(rowi == tm - 1, 0.0, pltpu.roll(a, tm - 1, 0))
        _, gg = _scan(c, dh, tm, reverse=True)
        gcar[0:1, :] = a[0:1, :] * gg[0:1, :]
        dmult = gg * (ig * xc)
        dig = gg * mult * xc
        dxc = gg * mult * ig
        dla = gg * hprev * a - dmult * a * a / mult
        dr = dla * (-LRU_C * sp)
        lam = pv_ref[R_LAM:R_LAM + 1, :]
        dpv_ref[R_LAM:R_LAM + 1, :] += jnp.sum(dla * (-LRU_C * r), axis=0, keepdims=True) * (-_sigmoid(-lam))
        dpa = dr * r * (1.0 - r)
        dpx = dig * ig * (1.0 - ig)
        addrow(R_BA, dpa)
        addrow(R_BX, dpx)
        dxc = dxc + _dot_nt(dpa, wa_ref[...]) + _dot_nt(dpx, wx_ref[...])
        dwa_ref[...] += _dot_tn(xc, dpa)
        dwx_ref[...] += _dot_tn(xc, dpx)
        addrow(R_LCONV_B, dxc)
        dxcext[0:tm, :] = dxc
        dxcext[tm:tm + LRU_HALO, :] = xccar[...]
        xccar[...] = dxc[0:LRU_HALO, :]
        drx = jnp.zeros((tm, LRU_W), F32)
        for k in range(LRU_K):
            addrow(R_LCW + k, dxc * rxext[pl.ds(LRU_HALO - (LRU_K - 1) + k, tm), :])
            drx = drx + pv_ref[R_LCW + k:R_LCW + k + 1, :] * dxcext[pl.ds(LRU_K - 1 - k, tm), :]
        dz_ref[:, RX0:RX0 + LRU_W] = drx.astype(dz_ref.dtype)

    tile = lambda w: pl.BlockSpec((tm, w), lambda i: (rev(i), 0))
    in_specs = [tile(D_MODEL)] + _mixer_in_specs(tm, rev) + [
        tile(D_MODEL), tile(LRU_W),
        pl.BlockSpec((8, LRU_W), lambda i: (jnp.maximum(rev(i) * (tm // 8) - 1, 0), 0)),
        tile(CONV_W)]
    return _call(
        body, name, (nt,), in_specs,
        [tile(IN_W), _acc_spec((8, 4 * BLK)), _acc_spec((32, CONV_W)), _acc_spec((16, CONV_W)),
         _acc_spec((LRU_W, LRU_W)), _acc_spec((LRU_W, LRU_W))],
        [_sds((t, IN_W), MX), _sds((8, 4 * BLK), F32), _sds((32, CONV_W), F32), _sds((16, CONV_W), F32),
         _sds((LRU_W, LRU_W), F32), _sds((LRU_W, LRU_W), F32)],
        [pltpu.VMEM((tm + CONV_HALO, CONV_W), F32), pltpu.VMEM((7, tm + CONV_HALO - 8, CONV_W), F32),
         pltpu.VMEM((tm, CONV_W), F32), pltpu.VMEM((tm + LRU_HALO, LRU_W), F32),
         pltpu.VMEM((KV_W, tm + BLK), F32), pltpu.VMEM((KV_W, tm + BLK), F32),
         pltpu.VMEM((tm + CONV_HALO, CONV_W), F32), pltpu.VMEM((7, tm + CONV_HALO - 8, CONV_W), F32),
         pltpu.VMEM((32, 8, CONV_W), F32), pltpu.VMEM((tm + LRU_HALO, LRU_W), F32),
         pltpu.VMEM((KV_W, BLK), F32), pltpu.VMEM((KV_W, BLK), F32),
         pltpu.VMEM((CONV_HALO, CONV_W), F32), pltpu.VMEM((LRU_HALO, LRU_W), F32), pltpu.VMEM((8, LRU_W), F32)],
        [dy, z, z, sink, cw, pv, wa, wx, ycat, hl, hl, uc], riders)


def _post_fwd(ycat, h0, gmix, w_out, g2, w_up, w_down, name, riders=()):
    t = h0.shape[0]
    tm = _tile(t, POST_TILE)
    nj = D_FF // FF_BLK

    def body(y_ref, h_ref, gm_ref, wo_ref, g2_ref, wu_ref, wd_ref, h1_ref, a_ref, h2_ref, ym_ref, hn_ref):
        ym, _, _ = _group_rms_fwd(y_ref[...], gm_ref[...])
        ym = ym.astype(MX)
        ym_ref[...] = ym
        h1 = h_ref[...] + jnp.dot(ym, wo_ref[...], preferred_element_type=F32)
        h1_ref[...] = h1
        hn, _, _ = _rms_fwd(h1, g2_ref[...])
        hn = hn.astype(MX)
        hn_ref[...] = hn
        for j in range(nj):
            u = jnp.dot(hn, wu_ref[j], preferred_element_type=F32)
            a_ref[:, j * FF_BLK:(j + 1) * FF_BLK] = jnp.square(jnp.maximum(u, 0.0)).astype(MX)
        h2_ref[...] = h1 + jnp.dot(a_ref[...], wd_ref[...], preferred_element_type=F32)

    tile = lambda w: pl.BlockSpec((tm, w), lambda i: (i, 0))
    return _call(
        body, name, (t // tm,),
        [tile(D_MODEL), tile(D_MODEL), _const_spec((1, D_MODEL)), _const_spec((D_MODEL, D_MODEL)),
         _const_spec((1, D_MODEL)), _const_spec((nj, D_MODEL, FF_BLK)), _const_spec((D_FF, D_MODEL))],
        [tile(D_MODEL), tile(D_FF), tile(D_MODEL), tile(D_MODEL), tile(D_MODEL)],
        [_sds((t, D_MODEL), F32), _sds((t, D_FF), MX), _sds((t, D_MODEL), F32), _sds((t, D_MODEL), MX),
         _sds((t, D_MODEL), MX)],
        [], [ycat, h0, gmix, w_out, g2, w_up, w_down], riders)


def _ffn_bwd(dh2, act, h1, g2, w_up_t, w_down_t, name, riders=()):
    t = h1.shape[0]
    tm = _tile(t, POST_TILE)
    nj = D_FF // FF_BLK

    def body(dh2_ref, a_ref, h1_ref, g2_ref, wut_ref, wdt_ref, dh1_ref, dh1b_ref, dh2b_ref, du_ref, dg2_ref):
        @pl.when(pl.program_id(0) == 0)
        def _():
            dg2_ref[...] = jnp.zeros_like(dg2_ref)

        dh2 = dh2_ref[...]
        dh2b = dh2.astype(MX)
        dh2b_ref[...] = dh2b
        for j in range(nj):
            cols = slice(j * FF_BLK, (j + 1) * FF_BLK)
            da = jnp.dot(dh2b, wdt_ref[:, cols], preferred_element_type=F32)
            du_ref[:, cols] = (da * (2.0 * jnp.sqrt(a_ref[:, cols].astype(F32)))).astype(MX)
        dhn = jnp.dot(du_ref[...], wut_ref[...], preferred_element_type=F32)
        _, xh, r = _rms_fwd(h1_ref[...], g2_ref[...])
        dx, dg = _rms_bwd(dhn, xh, r, g2_ref[...])
        dg2_ref[...] += dg
        dh1 = dh2 + dx
        dh1_ref[...] = dh1
        dh1b_ref[...] = dh1.astype(MX)

    tile = lambda w: pl.BlockSpec((tm, w), lambda i: (i, 0))
    return _call(
        body, name, (t // tm,),
        [tile(D_MODEL), tile(D_FF), tile(D_MODEL), _const_spec((1, D_MODEL)),
         _const_spec((D_FF, D_MODEL)), _const_spec((D_MODEL, D_FF))],
        [tile(D_MODEL), tile(D_MODEL), tile(D_MODEL), tile(D_FF), _acc_spec((1, D_MODEL))],
        [_sds((t, D_MODEL), F32), _sds((t, D_MODEL), MX), _sds((t, D_MODEL), MX), _sds((t, D_FF), MX),
         _sds((1, D_MODEL), F32)],
        [], [dh2, act, h1, g2, w_up_t, w_down_t], riders)


def _mix_bwd(dh1, ycat, gmix, w_out_t, name):
    t = dh1.shape[0]
    tm = _tile(t)

    def body(dh1_ref, y_ref, gm_ref, wot_ref, dy_ref, dgm_ref):
        @pl.when(pl.program_id(0) == 0)
        def _():
            dgm_ref[...] = jnp.zeros_like(dgm_ref)

        dym = _dot(dh1_ref[...], wot_ref[...])
        gm = gm_ref[...]
        _, yh, rr = _group_rms_fwd(y_ref[...], gm)
        outs, dgs = [], []
        for (a, b), rg in zip(_GROUPS, rr):
            dxg, dgg = _rms_bwd(dym[:, a:b], yh[:, a:b], rg, gm[:, a:b])
            outs.append(dxg)
            dgs.append(dgg)
        dy_ref[...] = jnp.concatenate(outs, axis=1)
        dgm_ref[...] += jnp.concatenate(dgs, axis=1)

    tile = pl.BlockSpec((tm, D_MODEL), lambda i: (i, 0))
    (dy, dgm), _ = _call(
        body, name, (t // tm,), [tile, tile, _const_spec((1, D_MODEL)), _const_spec((D_MODEL, D_MODEL))],
        [tile, _acc_spec((1, D_MODEL))], [_sds((t, D_MODEL), F32), _sds((1, D_MODEL), F32)],
        [], [dh1, ycat, gmix, w_out_t])
    return dy, dgm


def _in_bwd(dz, h0, dh1, g1, w_in_t, after, name):
    t = h0.shape[0]
    tm = _tile(t)

    def body(dz_ref, h_ref, dh1_ref, g_ref, wt_ref, after_ref, dh0_ref, dg_ref):
        @pl.when(pl.program_id(0) == 0)
        def _():
            dg_ref[...] = jnp.zeros_like(dg_ref)

        dhn = _dot(dz_ref[...], wt_ref[...])
        _, xh, r = _rms_fwd(h_ref[...], g_ref[...])
        dx, dg = _rms_bwd(dhn, xh, r, g_ref[...])
        dg_ref[...] += dg
        dh0_ref[...] = dh1_ref[...] + dx

    tile = lambda w: pl.BlockSpec((tm, w), lambda i: (i, 0))
    (dh0, dg), _ = _call(
        body, name, (t // tm,),
        [tile(IN_W), tile(D_MODEL), tile(D_MODEL), _const_spec((1, D_MODEL)), _const_spec((IN_W, D_MODEL)),
         _const_spec((8, 128))],
        [tile(D_MODEL), _acc_spec((1, D_MODEL))], [_sds((t, D_MODEL), F32), _sds((1, D_MODEL), F32)],
        [], [dz, h0, dh1, g1, w_in_t, after])
    return dh0, dg


def _loss_head(h, gf, target, name):
    t = h.shape[0]
    tm = _tile(t)

    def body(h_ref, g_ref, t_ref, dh_ref, loss_ref, dg_ref):
        @pl.when(pl.program_id(0) == 0)
        def _():
            loss_ref[...] = jnp.zeros_like(loss_ref)
            dg_ref[...] = jnp.zeros_like(dg_ref)

        g = g_ref[...]
        y, xh, r = _rms_fwd(h_ref[...], g)
        err = y - t_ref[...]
        part = 0.5 * jnp.sum(jnp.mean(err * err, axis=-1, keepdims=True), axis=0, keepdims=True)
        loss_ref[...] += jnp.broadcast_to(part, loss_ref.shape)
        dx, dg = _rms_bwd(err * (1.0 / D_MODEL), xh, r, g)
        dg_ref[...] += dg
        dh_ref[...] = dx

    tile = pl.BlockSpec((tm, D_MODEL), lambda i: (i, 0))
    (dh, loss, dg), _ = _call(
        body, name, (t // tm,), [tile, _const_spec((1, D_MODEL)), tile],
        [tile, _acc_spec((1, 128)), _acc_spec((1, D_MODEL))],
        [_sds((t, D_MODEL), F32), _sds((1, 128), F32), _sds((1, D_MODEL), F32)], [], [h, gf, target])
    return dh, loss, dg


def _dw(x, y, name, split, bm, bn):
    t, m = x.shape
    n = y.shape[1]
    tk = _tile(t, DW_TILE)
    nk = t // tk
    if split == "rows":
        assert bn == n
        r, c = m // N_DEV, n
        per = bm // r
        out_block = pl.BlockSpec((per, r, c), lambda a, b, k: (a, 0, 0))
    else:
        assert bm == m
        r, c = m, n // N_DEV
        per = bn // c
        out_block = pl.BlockSpec((per, r, c), lambda a, b, k: (b, 0, 0))

    def body(x_ref, y_ref, o_ref, o16_ref, acc):
        k = pl.program_id(2)

        @pl.when(k == 0)
        def _():
            acc[...] = jnp.zeros_like(acc)

        acc[...] += _dot_tn(x_ref[...], y_ref[...])

        @pl.when(k == nk - 1)
        def _():
            for d in range(per):
                v = acc[d * r:(d + 1) * r, :] if split == "rows" else acc[:, d * c:(d + 1) * c]
                o_ref[d] = v
                o16_ref[d] = v.astype(o16_ref.dtype)

    return pl.pallas_call(
        body, name=name, grid=(m // bm, n // bn, nk),
        in_specs=[pl.BlockSpec((tk, bm), lambda a, b, k: (k, a)), pl.BlockSpec((tk, bn), lambda a, b, k: (k, b))],
        out_specs=[out_block, out_block],
        out_shape=[_sds((N_DEV, r, c), F32), _sds((N_DEV, r, c), WIRE)],
        scratch_shapes=[pltpu.VMEM((bm, bn), F32)],
        compiler_params=pltpu.CompilerParams(dimension_semantics=("arbitrary",) * 3, vmem_limit_bytes=VMEM_LIMIT),
    )(x, y)


def _adamw_math(w, g, m, v):
    m = ADAM_B1 * m + (1.0 - ADAM_B1) * g
    v = ADAM_B2 * v + (1.0 - ADAM_B2) * jnp.square(g)
    m_hat = m / (1.0 - ADAM_B1 ** ADAM_STEP)
    v_hat = v / (1.0 - ADAM_B2 ** ADAM_STEP)
    delta = -ADAM_LR * (m_hat / (jnp.sqrt(v_hat) + ADAM_EPS) + ADAM_WD * w)
    return delta, m, v


def _adamw_shard(g_own, g_recv, dev, w, m, v, after, name):
    _, r, c = w.shape
    br = r
    for cand in (256, 128, 112, 64, 56, 32, 16, 8):
        if r % cand == 0:
            br = cand
            break
    nr = r // br
    own = lambda l: pl.BlockSpec((1, br, c), lambda ll, i, d: (d[0], jnp.where(ll == l, i, (nr - 1) * (1 - l)), 0))
    recv = lambda l: pl.BlockSpec((N_DEV - 1, br, c), lambda ll, i, d: (0, jnp.where(ll == l, i, (nr - 1) * (1 - l)), 0))

    def body(dev_ref, go0, gr0, go1, gr1, w_ref, m_ref, v_ref, after_ref, g_out, d_out, m_out, v_out):
        def update(go_ref, gr_ref):
            g = go_ref[0]
            for j in range(N_DEV - 1):
                g = g + gr_ref[j].astype(F32)
            delta, mn, vn = _adamw_math(w_ref[0], g, m_ref[0], v_ref[0])
            g_out[0] = g
            d_out[0] = delta
            m_out[0] = mn
            v_out[0] = vn

        layer = pl.program_id(0)
        pl.when(layer == 0)(lambda: update(go0, gr0))
        pl.when(layer == 1)(lambda: update(go1, gr1))

    tile = pl.BlockSpec((1, br, c), lambda ll, i, d: (ll, i, 0))
    return pl.pallas_call(
        body, name=name,
        grid_spec=pltpu.PrefetchScalarGridSpec(
            num_scalar_prefetch=1, grid=(2, nr),
            in_specs=[own(0), recv(0), own(1), recv(1), tile, tile, tile,
                      pl.BlockSpec((8, 128), lambda ll, i, d: (0, 0))],
            out_specs=[tile, tile, tile, tile]),
        out_shape=[_sds((2, r, c), F32)] * 4,
        compiler_params=pltpu.CompilerParams(dimension_semantics=("arbitrary",) * 2, vmem_limit_bytes=VMEM_LIMIT),
    )(dev, g_own[0], g_recv[0], g_own[1], g_recv[1], w, m, v, after)


def _adamw_small(gs, ws, ms, vs, name):
    n = len(gs)

    def body(*refs):
        g_refs, w_refs, m_refs, v_refs = (refs[k * n:(k + 1) * n] for k in range(4))
        outs = refs[4 * n:]
        for k in range(n):
            delta, mn, vn = _adamw_math(w_refs[k][...], g_refs[k][...], m_refs[k][...], v_refs[k][...])
            outs[k][...] = delta
            outs[n + k][...] = mn
            outs[2 * n + k][...] = vn

    shapes = [_sds(w.shape, F32) for w in ws]
    res = pl.pallas_call(body, name=name, out_shape=shapes * 3,
                         compiler_params=pltpu.CompilerParams(vmem_limit_bytes=VMEM_LIMIT))(*gs, *ws, *ms, *vs)
    return res[:n], res[n:2 * n], res[2 * n:]


def _sum_parts(part, dev, name):
    def body(dev_ref, p_ref, o_ref):
        me = dev_ref[0]
        g = p_ref[me]
        for d in range(1, N_DEV):
            g = g + p_ref[jnp.bitwise_xor(me, d)]
        o_ref[...] = g

    full = pl.BlockSpec(part.shape, lambda i, d: (0, 0, 0))
    return pl.pallas_call(
        body, name=name,
        grid_spec=pltpu.PrefetchScalarGridSpec(
            num_scalar_prefetch=1, grid=(1,), in_specs=[full],
            out_specs=pl.BlockSpec(part.shape[1:], lambda i, d: (0, 0))),
        out_shape=_sds(part.shape[1:], F32))(dev, part)


HBM = pl.BlockSpec(memory_space=pltpu.HBM)
SEM = pl.BlockSpec(memory_space=pltpu.SEMAPHORE)
EFFECT = pltpu.SideEffectType.DATAFLOW_SIDE_EFFECTING


def _direct_copies(srcs, lands, ssem, rsem, scatter):
    x, y, c = _me()
    out = []
    for a in range(len(srcs)):
        for f in range(1, N_DEV):
            px = 1 - x if f & 4 else x
            py = 1 - y if f & 2 else y
            pc = 1 - c if f & 1 else c
            out.append(pltpu.make_async_remote_copy(
                src_ref=srcs[a].at[4 * px + 2 * py + pc] if scatter else srcs[a], dst_ref=lands[a].at[f - 1],
                send_sem=ssem.at[7 * a + f - 1], recv_sem=rsem.at[7 * a + f - 1],
                device_id=(px, py, pc), device_id_type=MESH))
    return out


def _send_start(arrays, scatter, name):
    arrays = list(arrays)
    n = len(arrays)
    lands = [lax.empty((N_DEV - 1,) + (a.shape[1:] if scatter else a.shape), a.dtype) for a in arrays]

    def body(*refs):
        srcs, lnds, ssem, rsem, token = refs[:n], refs[n:2 * n], refs[2 * n], refs[2 * n + 1], refs[-1]
        for cp in _direct_copies(srcs, lnds, ssem, rsem, scatter):
            cp.start()
        token[...] = jnp.zeros_like(token)

    hbm = lambda a: pltpu.HBM(a.shape, a.dtype)
    res = pl.pallas_call(
        body, name=name,
        out_shape=(pltpu.SemaphoreType.DMA((7 * n,)), pltpu.SemaphoreType.DMA((7 * n,)),
                   *[hbm(a) for a in arrays + lands], _sds((8, 128), F32)),
        in_specs=[HBM] * (2 * n),
        out_specs=(SEM, SEM, *[HBM] * (2 * n), pl.BlockSpec(memory_space=pltpu.VMEM)),
        input_output_aliases={i: 2 + i for i in range(2 * n)},
        compiler_params=pltpu.CompilerParams(has_side_effects=EFFECT),
    )(*[pltpu.with_memory_space_constraint(a, pltpu.HBM) for a in arrays + lands])
    return types.SimpleNamespace(ssem=res[0], rsem=res[1], srcs=list(res[2:2 + n]), lands=list(res[2 + n:2 + 2 * n]),
                                 token=res[-1], scatter=scatter)


def _send_wait(h, after, name):
    n = len(h.srcs)

    def body(*refs):
        srcs, lnds, ssem, rsem = refs[:n], refs[n:2 * n], refs[2 * n], refs[2 * n + 1]
        for cp in _direct_copies(srcs, lnds, ssem, rsem, h.scatter):
            cp.wait_send()
            cp.wait_recv()

    hbm = lambda a: pltpu.HBM(a.shape, a.dtype)
    res = pl.pallas_call(
        body, name=name,
        out_shape=tuple(hbm(a) for a in h.srcs + h.lands),
        in_specs=[HBM] * (2 * n) + [SEM, SEM, ANY], out_specs=[HBM] * (2 * n),
        input_output_aliases={i: i for i in range(2 * n)},
        compiler_params=pltpu.CompilerParams(has_side_effects=EFFECT),
    )(*h.srcs, *h.lands, h.ssem, h.rsem, after)
    return list(res[:n]), list(res[n:])


def _block_diag(w):
    out = jnp.zeros((LRU_W, LRU_W), w.dtype)
    for h in range(4):
        out = lax.dynamic_update_slice(out, w[h], (h * 64, h * 64))
    return out


def _unblock_diag(w):
    return jnp.concatenate([w[h * 64:(h + 1) * 64, h * 64:(h + 1) * 64] for h in range(4)], axis=0)


def _layer_params(p, l):
    row = lambda a: a[l].reshape(1, -1)
    sink_rows = jnp.repeat(p["attn_sinks"][l].reshape(4, 2), 2 * BLK, axis=1)
    sink_rows = jnp.concatenate([sink_rows, jnp.zeros((4, 4 * BLK), F32)], axis=0)
    cw = jnp.concatenate([p["conv_dw_w"][l], jnp.zeros((1, CONV_W), F32)], axis=0)
    pv = jnp.concatenate([
        row(p["conv_dw_b"]), row(p["conv_ln_g"]), row(p["conv_ln_b"]), row(p["lru_conv_b"]), row(p["lru_ba"]),
        row(p["lru_bx"]), row(p["lru_lambda"]), jnp.zeros((1, LRU_W), F32), p["lru_conv_w"][l],
        jnp.zeros((4, LRU_W), F32)], axis=0)
    return dict(
        g1=row(p["norm1"]), sink=sink_rows, cw=cw, pv=pv,
        wa=_block_diag(p["lru_wa"][l]).astype(MX), wx=_block_diag(p["lru_wx"][l]).astype(MX),
        gmix=row(p["mix_norm"]), g2=row(p["norm2"]))


_SMALL = ["norm1", "attn_sinks", "conv_dw_w", "conv_dw_b", "conv_ln_g", "conv_ln_b", "lru_conv_w", "lru_conv_b",
          "lru_wa", "lru_ba", "lru_wx", "lru_bx", "lru_lambda", "mix_norm", "norm2"]
_BIG = ["w_in", "w_out", "w_up", "w_down"]
_WEIGHTS = ["norm1", "w_in", "attn_sinks", "conv_dw_w", "conv_dw_b", "conv_ln_g", "conv_ln_b", "lru_conv_w",
            "lru_conv_b", "lru_wa", "lru_ba", "lru_wx", "lru_bx", "lru_lambda", "mix_norm", "w_out", "norm2", "w_up",
            "w_down", "final_norm"]


def kernel(x, norm1, w_in, attn_sinks, conv_dw_w, conv_dw_b, conv_ln_g, conv_ln_b, lru_conv_w, lru_conv_b, lru_wa, lru_ba, lru_wx, lru_bx, lru_lambda, mix_norm, w_out, norm2, w_up, w_down, final_norm, loss_target, m_norm1, m_w_in, m_attn_sinks, m_conv_dw_w, m_conv_dw_b, m_conv_ln_g, m_conv_ln_b, m_lru_conv_w, m_lru_conv_b, m_lru_wa, m_lru_ba, m_lru_wx, m_lru_bx, m_lru_lambda, m_mix_norm, m_w_out, m_norm2, m_w_up, m_w_down, m_final_norm, v_norm1, v_w_in, v_attn_sinks, v_conv_dw_w, v_conv_dw_b, v_conv_ln_g, v_conv_ln_b, v_lru_conv_w, v_lru_conv_b, v_lru_wa, v_lru_ba, v_lru_wx, v_lru_bx, v_lru_lambda, v_mix_norm, v_w_out, v_norm2, v_w_up, v_w_down, v_final_norm):
    w = dict(norm1=norm1, w_in=w_in, attn_sinks=attn_sinks, conv_dw_w=conv_dw_w, conv_dw_b=conv_dw_b,
             conv_ln_g=conv_ln_g, conv_ln_b=conv_ln_b, lru_conv_w=lru_conv_w, lru_conv_b=lru_conv_b, lru_wa=lru_wa,
             lru_ba=lru_ba, lru_wx=lru_wx, lru_bx=lru_bx, lru_lambda=lru_lambda, mix_norm=mix_norm, w_out=w_out,
             norm2=norm2, w_up=w_up, w_down=w_down, final_norm=final_norm)
    m = dict(norm1=m_norm1, w_in=m_w_in, attn_sinks=m_attn_sinks, conv_dw_w=m_conv_dw_w, conv_dw_b=m_conv_dw_b,
             conv_ln_g=m_conv_ln_g, conv_ln_b=m_conv_ln_b, lru_conv_w=m_lru_conv_w, lru_conv_b=m_lru_conv_b,
             lru_wa=m_lru_wa, lru_ba=m_lru_ba, lru_wx=m_lru_wx, lru_bx=m_lru_bx, lru_lambda=m_lru_lambda,
             mix_norm=m_mix_norm, w_out=m_w_out, norm2=m_norm2, w_up=m_w_up, w_down=m_w_down, final_norm=m_final_norm)
    v = dict(norm1=v_norm1, w_in=v_w_in, attn_sinks=v_attn_sinks, conv_dw_w=v_conv_dw_w, conv_dw_b=v_conv_dw_b,
             conv_ln_g=v_conv_ln_g, conv_ln_b=v_conv_ln_b, lru_conv_w=v_lru_conv_w, lru_conv_b=v_lru_conv_b,
             lru_wa=v_lru_wa, lru_ba=v_lru_ba, lru_wx=v_lru_wx, lru_bx=v_lru_bx, lru_lambda=v_lru_lambda,
             mix_norm=v_mix_norm, w_out=v_w_out, norm2=v_norm2, w_up=v_w_up, w_down=v_w_down, final_norm=v_final_norm)
    depth = w_in.shape[0]
    xi, yi, ci = _me()
    dev = (4 * xi + 2 * yi + ci).astype(jnp.int32)
    dev1 = dev.reshape(1)
    wb = {n: w[n].astype(MX) for n in _BIG}
    layer_shards = lambda l: [wb["w_out"][l], wb["w_up"][l], wb["w_down"][l]]

    _, ((g_in0, g_cw, g_lcw),) = _call(None, "gather_first", None, [], [], [], [], [],
                                        [_gather_rider([wb["w_in"][0], conv_dw_w, lru_conv_w])])
    cols = lambda g: jnp.moveaxis(g, 0, -2).reshape(g.shape[1:-1] + (N_DEV * g.shape[-1],))
    p = dict(w)
    p["conv_dw_w"] = cols(g_cw)
    p["lru_conv_w"] = cols(g_lcw)
    lp = [_layer_params(p, l) for l in range(depth)]

    gathered = [dict(w_in=cols(g_in0), w_in_shards=g_in0), dict()]
    saved = []
    h = x[0]
    for l in range(depth):
        q, gw = lp[l], gathered[l]
        z, hn1 = _ln_in(h, q["g1"], gw["w_in"], f"ln_in{l}")
        riders = [_gather_rider(layer_shards(0))] if l == 0 else []
        (ycat, hl, uc), got = _mixer_fwd(z, q["sink"], q["cw"], q["pv"], q["wa"], q["wx"], f"mixer_fwd{l}", riders)
        if l == 0:
            gw["w_out"], gw["w_up"], gw["w_down"] = got[0]
            gw["w_out"] = gw["w_out"].reshape(D_MODEL, D_MODEL)
        riders = [_gather_rider([wb["w_in"][1]] + layer_shards(1))] if l == 0 else []
        (h1, act, h2, ym, hn2), got = _post_fwd(ycat, h, q["gmix"], gw["w_out"], q["g2"], gw["w_up"],
                                                gw["w_down"].reshape(D_FF, D_MODEL), f"post_fwd{l}", riders)
        if l == 0:
            nxt = gathered[1]
            nxt["w_in_shards"], nxt["w_out"], nxt["w_up"], nxt["w_down"] = got[0]
            nxt["w_in"] = cols(nxt["w_in_shards"])
            nxt["w_out"] = nxt["w_out"].reshape(D_MODEL, D_MODEL)
        saved.append(dict(h0=h, z=z, hn1=hn1, ycat=ycat, hl=hl, uc=uc, h1=h1, act=act, ym=ym, hn2=hn2))
        h = h2
    dh, loss, dgf = _loss_head(h, final_norm.reshape(1, -1), loss_target[0], "loss_head")

    grads = [None] * depth
    big = {n: [None] * depth for n in _BIG}
    pending = []

    def send_pending():
        riders = [_scatter_rider([item[3] for item in pending])] if pending else []
        return riders, list(pending)

    def record(sent, got):
        for item, recv in zip(sent, got[0] if sent else []):
            big[item[0]][item[1]] = (item[2], recv)
        del pending[:len(sent)]

    for l in reversed(range(depth)):
        q, s, gw = lp[l], saved[l], gathered[l]
        riders, sent = send_pending()
        w_up_t = jnp.swapaxes(gw["w_up"], 1, 2).reshape(D_FF, D_MODEL)
        w_down_t = gw["w_down"].reshape(D_FF, D_MODEL).T
        w_in_t = jnp.swapaxes(gw["w_in_shards"], 1, 2).reshape(IN_W, D_MODEL)
        (dh1, dh1b, dhb, du, dg2), got = _ffn_bwd(dh, s["act"], s["h1"], q["g2"], w_up_t, w_down_t,
                                                  f"ffn_bwd{l}", riders)
        record(sent, got)
        dycat, dgm = _mix_bwd(dh1b, s["ycat"], q["gmix"], gw["w_out"].T, f"mix_bwd{l}")
        pending.append(("w_down", l) + tuple(_dw(s["act"], dhb, f"dw_down{l}", "rows", 2048, D_MODEL)))
        pending.append(("w_up", l) + tuple(_dw(s["hn2"], du, f"dw_up{l}", "cols", D_MODEL, 2048)))
        pending.append(("w_out", l) + tuple(_dw(s["ym"], dh1b, f"dw_out{l}", "rows", D_MODEL, D_MODEL)))
        riders, sent = send_pending()
        (dz, dsink, dcw, dpv, dwa, dwx), got = _mixer_bwd(dycat, s["z"], s["ycat"], s["hl"], s["uc"], q["sink"],
                                                          q["cw"], q["pv"], q["wa"], q["wx"], f"mixer_bwd{l}", riders)
        record(sent, got)
        d_win = _dw(s["hn1"], dz, f"dw_in{l}", "cols", D_MODEL, IN_W)
        if l > 0:
            pending.append(("w_in", l) + tuple(d_win))
            after = jnp.zeros((8, 128), F32)
        else:
            win_sends = _send_start([d_win[1]], True, "scatter_w_in0_start")
            after = win_sends.token
        dh, dg1 = _in_bwd(dz, s["h0"], dh1, q["g1"], w_in_t, after, f"in_bwd{l}")
        grads[l] = dict(
            norm1=dg1[0], attn_sinks=jnp.stack([dsink[0:4, 0], dsink[0:4, 2 * BLK]], axis=1).reshape(8),
            conv_dw_w=dcw[0:CONV_K], conv_dw_b=dpv[R_CONV_B], conv_ln_g=dpv[R_LN_G], conv_ln_b=dpv[R_LN_B],
            lru_conv_w=dpv[R_LCW:R_LCW + LRU_K], lru_conv_b=dpv[R_LCONV_B], lru_wa=_unblock_diag(dwa),
            lru_ba=dpv[R_BA].reshape(4, 64), lru_wx=_unblock_diag(dwx), lru_bx=dpv[R_BX].reshape(4, 64),
            lru_lambda=dpv[R_LAM], mix_norm=dgm[0], norm2=dg2[0])

    small = [jnp.stack([grads[l][n] for l in range(depth)]) for n in _SMALL] + [dgf, loss[:, 0:1]]
    sizes = [a.size for a in small]
    total = -(-sum(sizes) // 1024) * 1024
    packed = jnp.concatenate([a.reshape(-1) for a in small] + [jnp.zeros((total - sum(sizes),), F32)])
    packed = packed.reshape(total // 128, 128)
    small_sends = _send_start([packed], False, "bcast_small_start")

    out = {}
    shard_update = lambda n, after: list(_adamw_shard(
        [big[n][l][0] for l in range(depth)], [big[n][l][1] for l in range(depth)], dev1, w[n], m[n], v[n], after,
        f"adamw_{n}"))
    for n in ("w_out", "w_up", "w_down"):
        out[n] = shard_update(n, small_sends.token)
    _, (win_recv,) = _send_wait(win_sends, out["w_down"][1], "scatter_w_in0_wait")
    big["w_in"][0] = (d_win[0], win_recv)
    (packed,), (small_recv,) = _send_wait(small_sends, win_recv, "bcast_small_wait")
    out["w_in"] = shard_update("w_in", jnp.zeros((8, 128), F32))
    parts = jnp.concatenate([packed[None], small_recv], axis=0)
    summed = _sum_parts(parts, dev1, "sum_small_grads").reshape(-1)
    small_sums, pos = [], 0
    for a, size in zip(small, sizes):
        small_sums.append(summed[pos:pos + size].reshape(a.shape))
        pos += size
    shard = lambda a: lax.dynamic_slice_in_dim(a, dev * (a.shape[-1] // N_DEV), a.shape[-1] // N_DEV, axis=a.ndim - 1)
    flat = {"lru_wa": (depth, LRU_W, 64), "lru_wx": (depth, LRU_W, 64), "final_norm": (1, D_MODEL)}
    gs, ws, ms, vs = [], [], [], []
    for n, g in zip(_SMALL + ["final_norm"], small_sums[:-1]):
        shp = flat.get(n, w[n].shape)
        gs.append((shard(g) if n in ("conv_dw_w", "lru_conv_w") else g).reshape(shp))
        ws.append(w[n].reshape(shp))
        ms.append(m[n].reshape(shp))
        vs.append(v[n].reshape(shp))
    sd, sm, sv = _adamw_small(gs, ws, ms, vs, "adamw_small")
    for j, n in enumerate(_SMALL + ["final_norm"]):
        out[n] = [a.reshape(w[n].shape) for a in (gs[j], sd[j], sm[j], sv[j])]
    loss_total = small_sums[-1][0, 0]

    result = [loss_total, dh[None]]
    for j in range(4):
        result += [out[n][j] for n in _WEIGHTS]
    return tuple(result)
```

```python
import types

import jax
import jax.numpy as jnp
from jax import lax
from jax.experimental import pallas as pl
from jax.experimental.pallas import tpu as pltpu

F32 = jnp.float32
MX = jnp.bfloat16
WIRE = jnp.bfloat16

D_MODEL = 1024
HEAD_DIM = 64
ATTN_W = 512
KV_W = 128
BLK = 128
CONV_W = 256
CONV_K = 31
LRU_W = 256
LRU_K = 4
LRU_C = 8.0
IN_W = 1792
D_FF = 4096
FF_BLK = 512
N_DEV = 8
IN_SHARD = IN_W // N_DEV
RMS_EPS = 1e-6
LN_EPS = 1e-5
MASK_VALUE = -1e30
SCALE = HEAD_DIM ** -0.5
CONV_HALO = 32
LRU_HALO = 8
CONV_CHUNK = 64
POST_TILE = 512
DW_TILE = 1024
Q0, K0, V0, CV0, CG0, RX0, RG0 = 0, 512, 640, 768, 1024, 1280, 1536
R_CONV_B, R_LN_G, R_LN_B, R_LCONV_B, R_BA, R_BX, R_LAM, R_LCW = 0, 1, 2, 3, 4, 5, 6, 8

ADAM_LR, ADAM_B1, ADAM_B2, ADAM_EPS, ADAM_WD, ADAM_STEP = 0.001, 0.9, 0.999, 1e-08, 0.01, 10

VMEM_LIMIT = 56 * 1024 * 1024
MESH = pl.DeviceIdType.MESH
ANY = pl.BlockSpec(memory_space=pl.ANY)


def _tile(t, cap=512):
    return min(cap, t)


def _dot(a, b):
    return jnp.dot(a.astype(MX), b.astype(MX), preferred_element_type=F32)


def _dot_nt(a, b):
    return lax.dot_general(a.astype(MX), b.astype(MX), (((1,), (1,)), ((), ())), preferred_element_type=F32)


def _dot_tn(a, b):
    return lax.dot_general(a.astype(MX), b.astype(MX), (((0,), (0,)), ((), ())), preferred_element_type=F32)


def _const_spec(shape):
    nd = len(shape)
    return pl.BlockSpec(shape, lambda *_: (0,) * nd, pipeline_mode=pl.Buffered(1))


def _acc_spec(shape):
    nd = len(shape)
    return pl.BlockSpec(shape, lambda *_: (0,) * nd)


def _sds(shape, dtype):
    return jax.ShapeDtypeStruct(shape, dtype)


def _sigmoid(x):
    return jax.nn.sigmoid(x)


def _rms_fwd(x, g):
    r = lax.rsqrt(jnp.mean(x * x, axis=-1, keepdims=True) + RMS_EPS)
    xh = x * r
    return xh * g, xh, r


def _rms_bwd(dy, xh, r, g):
    t = dy * g
    dx = r * (t - xh * jnp.mean(t * xh, axis=-1, keepdims=True))
    return dx, jnp.sum(dy * xh, axis=0, keepdims=True)


_GROUPS = ((0, 512), (512, 768), (768, 1024))


def _group_rms_fwd(y, g):
    parts = [_rms_fwd(y[:, a:b], g[:, a:b]) for a, b in _GROUPS]
    return (jnp.concatenate([p[0] for p in parts], axis=1),
            jnp.concatenate([p[1] for p in parts], axis=1),
            [p[2] for p in parts])


def _gelu(x):
    c = 0.7978845608028654
    u = c * (x + 0.044715 * x * x * x)
    th = jnp.tanh(u)
    val = 0.5 * x * (1.0 + th)
    grad = 0.5 * (1.0 + th) + 0.5 * x * (1.0 - th * th) * c * (1.0 + 3.0 * 0.044715 * x * x)
    return val, grad


def _neg_expm1(x):
    series = -x * (1.0 + x * (0.5 + x * (1.0 / 6.0 + x * (1.0 / 24.0))))
    return jnp.where(x > -0.02, series, 1.0 - jnp.exp(x))


def _me():
    return lax.axis_index("x"), lax.axis_index("y"), lax.axis_index("c")


def _gather_rider(arrays):
    arrays = list(arrays)
    n = len(arrays)

    def plan(ins, outs, sems):
        ssem, rsem, lsem = sems
        x, y, c = _me()
        chips = [(1 - x, y), (x, 1 - y), (1 - x, 1 - y)]

        def copy(a, k, block, to, own=False):
            dst = outs[a].at[4 * block[0] + 2 * block[1] + block[2]]
            return pltpu.make_async_remote_copy(
                src_ref=ins[a] if own else dst, dst_ref=dst, send_sem=ssem.at[7 * a + k],
                recv_sem=rsem.at[7 * a + k], device_id=to, device_id_type=MESH)

        return x, y, c, chips, copy, lsem

    def start(ins, outs, sems):
        x, y, c, chips, copy, lsem = plan(ins, outs, sems)
        for a in range(n):
            pltpu.make_async_copy(ins[a], outs[a].at[4 * x + 2 * y + c], lsem.at[a]).start()
            copy(a, 0, (x, y, c), (x, y, 1 - c), own=True).start()
            for j, chip in enumerate(chips):
                copy(a, 1 + j, (x, y, c), (*chip, c), own=True).start()

    def mid(ins, outs, sems):
        x, y, c, chips, copy, _ = plan(ins, outs, sems)
        for a in range(n):
            for j, chip in enumerate(chips):
                copy(a, 1 + j, (*chip, c), (x, y, c)).wait_recv()
                copy(a, 4 + j, (*chip, c), (x, y, 1 - c)).start()

    def finish(ins, outs, sems):
        x, y, c, chips, copy, lsem = plan(ins, outs, sems)
        for a in range(n):
            copy(a, 0, (x, y, 1 - c), (x, y, c)).wait_recv()
            for j, chip in enumerate(chips):
                copy(a, 4 + j, (*chip, 1 - c), (x, y, c)).wait_recv()
        for a in range(n):
            copy(a, 0, (x, y, c), (x, y, 1 - c), own=True).wait_send()
            for j, chip in enumerate(chips):
                copy(a, 1 + j, (x, y, c), (*chip, c), own=True).wait_send()
                copy(a, 4 + j, (*chip, c), (x, y, 1 - c)).wait_send()
            pltpu.make_async_copy(ins[a], outs[a].at[4 * x + 2 * y + c], lsem.at[a]).wait()

    return types.SimpleNamespace(
        arrays=arrays, out_shape=[_sds((N_DEV,) + a.shape, a.dtype) for a in arrays],
        scratch=[pltpu.SemaphoreType.DMA((7 * n,)), pltpu.SemaphoreType.DMA((7 * n,)), pltpu.SemaphoreType.DMA((n,))],
        start=start, mid=mid, finish=finish)


def _bcast_rider(arrays):
    arrays = list(arrays)
    n = len(arrays)

    def copies(ins, outs, sems, landing):
        ssem, rsem, lsem = sems
        x, y, c = _me()
        out = []
        for a in range(n):
            out.append(pltpu.make_async_copy(ins[a], outs[a].at[4 * x + 2 * y + c], lsem.at[a]))
            for f in range(1, N_DEV):
                px = 1 - x if f & 4 else x
                py = 1 - y if f & 2 else y
                pc = 1 - c if f & 1 else c
                slot = 4 * px + 2 * py + pc if landing else 4 * x + 2 * y + c
                out.append(pltpu.make_async_remote_copy(
                    src_ref=ins[a], dst_ref=outs[a].at[slot], send_sem=ssem.at[7 * a + f - 1],
                    recv_sem=rsem.at[7 * a + f - 1], device_id=(px, py, pc), device_id_type=MESH))
        return out

    def start(ins, outs, sems):
        for cp in copies(ins, outs, sems, landing=False):
            cp.start()

    def finish(ins, outs, sems):
        for cp in copies(ins, outs, sems, landing=True):
            cp.wait()

    return types.SimpleNamespace(
        arrays=arrays, out_shape=[_sds((N_DEV,) + a.shape, a.dtype) for a in arrays],
        scratch=[pltpu.SemaphoreType.DMA((7 * n,)), pltpu.SemaphoreType.DMA((7 * n,)), pltpu.SemaphoreType.DMA((n,))],
        start=start, mid=None, finish=finish)


def _scatter_rider(arrays):
    arrays = list(arrays)
    n = len(arrays)

    def copies(ins, outs, sems):
        ssem, rsem = sems
        x, y, c = _me()
        out = []
        for a in range(n):
            for f in range(1, N_DEV):
                px = 1 - x if f & 4 else x
                py = 1 - y if f & 2 else y
                pc = 1 - c if f & 1 else c
                out.append(pltpu.make_async_remote_copy(
                    src_ref=ins[a].at[4 * px + 2 * py + pc], dst_ref=outs[a].at[f - 1], send_sem=ssem.at[7 * a + f - 1],
                    recv_sem=rsem.at[7 * a + f - 1], device_id=(px, py, pc), device_id_type=MESH))
        return out

    def start(ins, outs, sems):
        for cp in copies(ins, outs, sems):
            cp.start()

    def finish(ins, outs, sems):
        for cp in copies(ins, outs, sems):
            cp.wait()

    return types.SimpleNamespace(
        arrays=arrays, out_shape=[_sds((N_DEV - 1,) + a.shape[1:], a.dtype) for a in arrays],
        scratch=[pltpu.SemaphoreType.DMA((7 * n,)), pltpu.SemaphoreType.DMA((7 * n,))],
        start=start, mid=None, finish=finish)


def _call(body, name, grid, in_specs, out_specs, out_shape, scratch, operands, riders=()):
    n_in, n_out, n_scr = len(operands), len(out_shape), len(scratch)
    nsteps = grid[0] if grid else 1
    sizes = [(len(r.arrays), len(r.out_shape), len(r.scratch)) for r in riders]

    def wrapped(*refs):
        pos = n_in
        r_ins = []
        for ri, _, _ in sizes:
            r_ins.append(refs[pos:pos + ri])
            pos += ri
        outs = refs[pos:pos + n_out]
        pos += n_out
        r_outs = []
        for _, ro, _ in sizes:
            r_outs.append(refs[pos:pos + ro])
            pos += ro
        scr = refs[pos:pos + n_scr]
        pos += n_scr
        r_sems = []
        for _, _, rs in sizes:
            r_sems.append(refs[pos:pos + rs])
            pos += rs
        step = pl.program_id(0) if grid else 0

        def at(s, fn):
            if grid:
                pl.when(step == s)(fn)
            else:
                fn()

        for r, a, b, c in zip(riders, r_ins, r_outs, r_sems):
            at(0, lambda r=r, a=a, b=b, c=c: r.start(a, b, c))
        for r, a, b, c in zip(riders, r_ins, r_outs, r_sems):
            if r.mid is not None:
                at((3 * nsteps) // 4, lambda r=r, a=a, b=b, c=c: r.mid(a, b, c))
        if body is not None:
            body(*refs[:n_in], *outs, *scr)
        for r, a, b, c in zip(riders, r_ins, r_outs, r_sems):
            at(nsteps - 1, lambda r=r, a=a, b=b, c=c: r.finish(a, b, c))

    r_arrays = [a for r in riders for a in r.arrays]
    r_shapes = [s for r in riders for s in r.out_shape]
    kwargs = {}
    if grid:
        kwargs = dict(grid=grid, compiler_params=pltpu.CompilerParams(
            dimension_semantics=("arbitrary",) * len(grid), vmem_limit_bytes=VMEM_LIMIT))
    res = pl.pallas_call(
        wrapped, name=name,
        in_specs=list(in_specs) + [ANY] * len(r_arrays),
        out_specs=list(out_specs) + [ANY] * len(r_shapes),
        out_shape=list(out_shape) + r_shapes,
        scratch_shapes=list(scratch) + [s for r in riders for s in r.scratch],
        **kwargs,
    )(*operands, *r_arrays)
    host, rest = res[:n_out], res[n_out:]
    r_res = []
    for _, ro, _ in sizes:
        r_res.append(rest[:ro])
        rest = rest[ro:]
    return host, r_res


def _ln_in(h, g1, w_in, name):
    t = h.shape[0]
    tm = _tile(t)

    def body(h_ref, g_ref, w_ref, z_ref, hn_ref):
        y, _, _ = _rms_fwd(h_ref[...], g_ref[...])
        hn = y.astype(MX)
        hn_ref[...] = hn
        z_ref[...] = jnp.dot(hn, w_ref[...], preferred_element_type=F32)

    tile = lambda w: pl.BlockSpec((tm, w), lambda i: (i, 0))
    (z, hn), _ = _call(
        body, name, (t // tm,),
        [tile(D_MODEL), _const_spec((1, D_MODEL)), _const_spec((D_MODEL, IN_W))],
        [tile(IN_W), tile(D_MODEL)], [_sds((t, IN_W), F32), _sds((t, D_MODEL), MX)], [], [h, g1, w_in])
    return z, hn


def _band2(kb, g):
    lo = lax.broadcasted_iota(jnp.int32, kb.shape, 1) < HEAD_DIM
    kr = pltpu.roll(kb, HEAD_DIM, 1)
    if g == 0:
        top, bot = jnp.where(lo, kb, 0.0), jnp.where(lo, 0.0, kr)
    else:
        top, bot = jnp.where(lo, kr, 0.0), jnp.where(lo, 0.0, kb)
    return jnp.concatenate([top, bot], axis=0)


def _attn_operands(z_ref, zh_ref, b):
    rows = slice(b * BLK, (b + 1) * BLK)
    prev = zh_ref if b == 0 else z_ref
    prow = slice(0, BLK) if b == 0 else slice((b - 1) * BLK, b * BLK)
    kb = jnp.concatenate([prev[prow, K0:K0 + KV_W], z_ref[rows, K0:K0 + KV_W]], axis=0)
    vb = jnp.concatenate([prev[prow, V0:V0 + KV_W], z_ref[rows, V0:V0 + KV_W]], axis=0)
    k2 = [_band2(kb, g) for g in range(2)]
    v2 = [_band2(vb, g) for g in range(2)]
    q2 = [jnp.concatenate([z_ref[rows, (2 * g) * BLK:(2 * g + 1) * BLK], z_ref[rows, (2 * g + 1) * BLK:(2 * g + 2) * BLK]],
                          axis=0) for g in range(2)]
    return q2, k2, v2


def _attn_block(z_ref, zh_ref, sink_ref, b, first):
    q2, k2, v2 = _attn_operands(z_ref, zh_ref, b)
    rr = lax.broadcasted_iota(jnp.int32, (4 * BLK, 2 * BLK), 0) & (BLK - 1)
    cc = lax.broadcasted_iota(jnp.int32, (4 * BLK, 2 * BLK), 1)
    first_block = jnp.logical_and(first, b == 0).astype(jnp.int32)
    mask = jnp.logical_and(jnp.logical_and(cc > rr, cc <= rr + BLK), cc >= BLK * first_block)
    s = jnp.concatenate([_dot_nt(q2[g], k2[g]) for g in range(2)], axis=0) * SCALE
    w = 2 * BLK
    out, psink = [], []
    for hh in range(2):
        sh = jnp.where(mask, s[:, hh * w:(hh + 1) * w], MASK_VALUE)
        sk = jnp.concatenate([jnp.broadcast_to(sink_ref[p:p + 1, hh * w:hh * w + 1], (BLK, 1)) for p in range(4)], axis=0)
        m = jnp.maximum(jnp.max(sh, axis=1, keepdims=True), sk)
        p = jnp.exp(sh - m)
        es = jnp.exp(sk - m)
        inv = 1.0 / (jnp.sum(p, axis=1, keepdims=True) + es)
        out.append(p * inv)
        psink.append(es * inv)
    return v2, jnp.concatenate(out, axis=1), psink


def _scan_steps(a, b, n, span, reverse):
    pos = lax.broadcasted_iota(jnp.int32, a.shape, 0) & (span - 1)
    d = 1
    while d < span:
        keep = pos < span - d if reverse else pos >= d
        shift = n - d if reverse else d
        a_sh = jnp.where(keep, pltpu.roll(a, shift, 0), 1.0)
        b_sh = jnp.where(keep, pltpu.roll(b, shift, 0), 0.0)
        b = a * b_sh + b
        a = a * a_sh
        d *= 2
    return a, b


def _scan(a, b, tm, reverse):
    return _scan_steps(a, b, tm, tm, reverse)


def _shifted_copies(ext, shifts, tm):
    rows = tm + CONV_HALO - 8
    for r in range(1, 8):
        shifts[r - 1, 0:rows, :] = ext[pl.ds(r, rows), :]


def _tap(ext, shifts, off, r0, n):
    a, r = divmod(off, 8)
    lo = 8 * a + r0
    if r == 0:
        return ext[lo:lo + n, :]
    return shifts[r - 1, lo:lo + n, :]


def _glu_fill(z_ref, zh_ref, uext, ush, first, tm, sg_out=None):
    cv = z_ref[:, CV0:CV0 + CONV_W]
    sg = _sigmoid(z_ref[:, CG0:CG0 + CONV_W])
    if sg_out is not None:
        sg_out[...] = sg
    hrow = BLK - CONV_HALO
    uh = zh_ref[hrow:BLK, CV0:CV0 + CONV_W] * _sigmoid(zh_ref[hrow:BLK, CG0:CG0 + CONV_W])
    uext[0:CONV_HALO, :] = jnp.where(first, 0.0, uh)
    uext[CONV_HALO:CONV_HALO + tm, :] = cv * sg
    _shifted_copies(uext, ush, tm)


def _conv_taps(cw_ref, pv_ref, uext, ush, out_ref, tm):
    for r0 in range(0, tm, CONV_CHUNK):
        acc = jnp.broadcast_to(pv_ref[R_CONV_B:R_CONV_B + 1, :], (CONV_CHUNK, CONV_W))
        for k in range(CONV_K):
            acc = acc + cw_ref[k:k + 1, :] * _tap(uext, ush, CONV_HALO - (CONV_K - 1) + k, r0, CONV_CHUNK)
        out_ref[r0:r0 + CONV_CHUNK, :] = acc


def _ln_silu(uc, pv_ref):
    mu = jnp.mean(uc, axis=-1, keepdims=True)
    xc = uc - mu
    rs = lax.rsqrt(jnp.mean(xc * xc, axis=-1, keepdims=True) + LN_EPS)
    xh = xc * rs
    ln = xh * pv_ref[R_LN_G:R_LN_G + 1, :] + pv_ref[R_LN_B:R_LN_B + 1, :]
    sg = _sigmoid(ln)
    return xh, rs, ln, sg


def _lru_gates(z_ref, zh_ref, pv_ref, wa_ref, wx_ref, rxext, first, tm):
    rxext[0:LRU_HALO, :] = jnp.where(first, 0.0, zh_ref[BLK - LRU_HALO:BLK, RX0:RX0 + LRU_W])
    rxext[LRU_HALO:LRU_HALO + tm, :] = z_ref[:, RX0:RX0 + LRU_W]
    xc = jnp.broadcast_to(pv_ref[R_LCONV_B:R_LCONV_B + 1, :], (tm, LRU_W))
    for k in range(LRU_K):
        xc = xc + pv_ref[R_LCW + k:R_LCW + k + 1, :] * rxext[pl.ds(LRU_HALO - (LRU_K - 1) + k, tm), :]
    r = _sigmoid(_dot(xc, wa_ref[...]) + pv_ref[R_BA:R_BA + 1, :])
    ig = _sigmoid(_dot(xc, wx_ref[...]) + pv_ref[R_BX:R_BX + 1, :])
    lam = pv_ref[R_LAM:R_LAM + 1, :]
    sp = jnp.log1p(jnp.exp(-lam))
    la = (-LRU_C * r) * sp
    a = jnp.exp(la)
    mult = jnp.sqrt(_neg_expm1(2.0 * la))
    return xc, r, ig, sp, la, a, mult


def _mixer_in_specs(tm, tile_of):
    hb = tm // BLK
    return [
        pl.BlockSpec((tm, IN_W), lambda i: (tile_of(i), 0)),
        pl.BlockSpec((BLK, IN_W), lambda i: (jnp.maximum(tile_of(i) * hb - 1, 0), 0)),
        _const_spec((8, 4 * BLK)),
        _const_spec((32, CONV_W)),
        _const_spec((16, CONV_W)),
        _const_spec((LRU_W, LRU_W)),
        _const_spec((LRU_W, LRU_W)),
    ]


def _mixer_fwd(z, sink, cw, pv, wa, wx, name, riders=()):
    t = z.shape[0]
    tm = _tile(t)
    nb = tm // BLK

    def body(z_ref, zh_ref, sink_ref, cw_ref, pv_ref, wa_ref, wx_ref, y_ref, hl_ref, uc_ref, p_ref, ps_ref,
             uext, ush, rxext, hcar):
        i = pl.program_id(0)
        first = i == 0

        @pl.when(first)
        def _():
            hcar[...] = jnp.zeros_like(hcar)

        lo = lax.broadcasted_iota(jnp.int32, (4 * BLK, BLK), 1) < HEAD_DIM
        for b in range(nb):
            rows = slice(b * BLK, (b + 1) * BLK)
            v2, prob, psink = _attn_block(z_ref, zh_ref, sink_ref, b, first)
            prob = prob.astype(MX)
            p_ref[b] = prob
            ps_ref[b] = jnp.where(lo, psink[0], psink[1])
            for g in range(2):
                o = _dot(prob[2 * g * BLK:(2 * g + 2) * BLK], v2[g])
                y_ref[rows, (2 * g) * BLK:(2 * g + 1) * BLK] = o[0:BLK]
                y_ref[rows, (2 * g + 1) * BLK:(2 * g + 2) * BLK] = o[BLK:2 * BLK]
        _glu_fill(z_ref, zh_ref, uext, ush, first, tm)
        _conv_taps(cw_ref, pv_ref, uext, ush, uc_ref, tm)
        _, _, ln, sg = _ln_silu(uc_ref[...], pv_ref)
        y_ref[:, ATTN_W:ATTN_W + CONV_W] = ln * sg
        xc, _, ig, _, _, a, mult = _lru_gates(z_ref, zh_ref, pv_ref, wa_ref, wx_ref, rxext, first, tm)
        acum, h = _scan(a, mult * (ig * xc), tm, reverse=False)
        h = h + acum * hcar[0:1, :]
        hl_ref[...] = h
        hcar[0:1, :] = h[tm - 1:tm, :]
        gl, _ = _gelu(z_ref[:, RG0:RG0 + LRU_W])
        y_ref[:, ATTN_W + CONV_W:ATTN_W + CONV_W + LRU_W] = h * gl

    tile = lambda w: pl.BlockSpec((tm, w), lambda i: (i, 0))
    return _call(
        body, name, (t // tm,), _mixer_in_specs(tm, lambda i: i),
        [tile(D_MODEL), tile(LRU_W), tile(CONV_W), pl.BlockSpec((nb, 4 * BLK, 4 * BLK), lambda i: (i, 0, 0)),
         pl.BlockSpec((nb, 4 * BLK, BLK), lambda i: (i, 0, 0))],
        [_sds((t, D_MODEL), F32), _sds((t, LRU_W), F32), _sds((t, CONV_W), F32),
         _sds((t // BLK, 4 * BLK, 4 * BLK), MX), _sds((t // BLK, 4 * BLK, BLK), F32)],
        [pltpu.VMEM((tm + CONV_HALO, CONV_W), F32), pltpu.VMEM((7, tm + CONV_HALO - 8, CONV_W), F32),
         pltpu.VMEM((tm + LRU_HALO, LRU_W), F32), pltpu.VMEM((8, LRU_W), F32)],
        [z, z, sink, cw, pv, wa, wx], riders)


def _mixer_bwd(dy, z, ycat, hl, uc, probs, psinks, sink, cw, pv, wa, wx, name, riders=()):
    t = z.shape[0]
    tm = _tile(t)
    nt = t // tm
    nb = tm // BLK
    rev = lambda i: nt - 1 - i

    def body(dy_ref, z_ref, zh_ref, sink_ref, cw_ref, pv_ref, wa_ref, wx_ref, y_ref, hl_ref, hlh_ref, uc_ref,
             p_ref, ps_ref, dz_ref, dsink_ref, dcw_ref, dpv_ref, dwa_ref, dwx_ref,
             uext, ush, sgs, rxext, dkext, dvext, ducext, dsh, dcw8, dxcext, kcar, vcar, uccar, xccar, gcar):
        i = pl.program_id(0)
        first = i == nt - 1

        @pl.when(i == 0)
        def _():
            for car in (kcar, vcar, uccar, xccar, gcar, dcw8):
                car[...] = jnp.zeros_like(car)
            for acc in (dsink_ref, dpv_ref, dwa_ref, dwx_ref):
                acc[...] = jnp.zeros_like(acc)

        def addrow(r, val):
            dpv_ref[r:r + 1, :] += jnp.sum(val, axis=0, keepdims=True)

        dkext[:, 0:tm] = jnp.zeros((KV_W, tm), F32)
        dvext[:, 0:tm] = jnp.zeros((KV_W, tm), F32)
        dkext[:, tm:tm + BLK] = kcar[...]
        dvext[:, tm:tm + BLK] = vcar[...]
        lane512 = lax.broadcasted_iota(jnp.int32, (1, 4 * BLK), 1) < 2 * BLK
        lo = lax.broadcasted_iota(jnp.int32, (4 * BLK, BLK), 1) < HEAD_DIM
        hd, w2 = HEAD_DIM, 2 * BLK
        for b in range(nb):
            rows = slice(b * BLK, (b + 1) * BLK)
            band = slice(b * BLK, (b + 2) * BLK)
            q2, k2, v2 = _attn_operands(z_ref, zh_ref, b)
            prob = p_ref[b]
            psink = [ps_ref[b, :, 0:1], ps_ref[b, :, HEAD_DIM:HEAD_DIM + 1]]
            stack = lambda ref: jnp.concatenate([ref[rows, p * BLK:(p + 1) * BLK] for p in range(4)], axis=0)
            do4 = stack(dy_ref)
            dlt = do4 * stack(y_ref)
            d0 = jnp.sum(jnp.where(lo, dlt, 0.0), axis=1, keepdims=True)
            d1 = jnp.sum(jnp.where(lo, 0.0, dlt), axis=1, keepdims=True)
            dp = jnp.concatenate([_dot_nt(do4[g * w2:(g + 1) * w2], v2[g]) for g in range(2)], axis=0)
            dl = jnp.concatenate([jnp.broadcast_to(d0, (4 * BLK, w2)), jnp.broadcast_to(d1, (4 * BLK, w2))], axis=1)
            draw = (prob * (dp - dl)) * SCALE
            e0, e1 = psink[0] * d0, psink[1] * d1
            for p in range(4):
                prs = slice(p * BLK, (p + 1) * BLK)
                s0 = jnp.sum(e0[prs], axis=0, keepdims=True)
                s1 = jnp.sum(e1[prs], axis=0, keepdims=True)
                dsink_ref[p:p + 1, :] += -jnp.where(lane512, s0, s1)
            for g in range(2):
                grs = slice(g * w2, (g + 1) * w2)
                dq = _dot(draw[grs], k2[g])
                dz_ref[rows, (2 * g) * BLK:(2 * g + 1) * BLK] = dq[0:BLK].astype(dz_ref.dtype)
                dz_ref[rows, (2 * g + 1) * BLK:(2 * g + 2) * BLK] = dq[BLK:2 * BLK].astype(dz_ref.dtype)
                tk = _dot_tn(q2[g], draw[grs])
                tv = _dot_tn(do4[grs], prob[grs])
                dkext[g * hd:(g + 1) * hd, band] += tk[0:hd, 0:w2] + tk[hd:2 * hd, w2:2 * w2]
                dvext[g * hd:(g + 1) * hd, band] += tv[0:hd, 0:w2] + tv[hd:2 * hd, w2:2 * w2]
        dz_ref[:, K0:K0 + KV_W] = jnp.transpose(dkext[:, BLK:BLK + tm]).astype(dz_ref.dtype)
        dz_ref[:, V0:V0 + KV_W] = jnp.transpose(dvext[:, BLK:BLK + tm]).astype(dz_ref.dtype)
        kcar[...] = dkext[:, 0:BLK]
        vcar[...] = dvext[:, 0:BLK]

        _glu_fill(z_ref, zh_ref, uext, ush, first, tm, sg_out=sgs)
        xh, rs, ln, sg = _ln_silu(uc_ref[...], pv_ref)
        dln = dy_ref[:, ATTN_W:ATTN_W + CONV_W] * (sg * (1.0 + ln * (1.0 - sg)))
        addrow(R_LN_G, dln * xh)
        addrow(R_LN_B, dln)
        dxh = dln * pv_ref[R_LN_G:R_LN_G + 1, :]
        duc = rs * (dxh - jnp.mean(dxh, axis=-1, keepdims=True) - xh * jnp.mean(dxh * xh, axis=-1, keepdims=True))
        addrow(R_CONV_B, duc)
        ducext[0:tm, :] = duc
        ducext[tm:tm + CONV_HALO, :] = uccar[...]
        uccar[...] = duc[0:CONV_HALO, :]
        _shifted_copies(ducext, dsh, tm)
        for r0 in range(0, tm, CONV_CHUNK):
            crow = slice(r0, r0 + CONV_CHUNK)
            duc_c = ducext[crow, :]
            du = jnp.zeros((CONV_CHUNK, CONV_W), F32)
            for k in range(CONV_K):
                prod = duc_c * _tap(uext, ush, CONV_HALO - (CONV_K - 1) + k, r0, CONV_CHUNK)
                part = prod[0:8]
                for s in range(8, CONV_CHUNK, 8):
                    part = part + prod[s:s + 8]
                dcw8[k] += part
                du = du + cw_ref[k:k + 1, :] * _tap(ducext, dsh, CONV_K - 1 - k, r0, CONV_CHUNK)
            sgc = sgs[crow, :]
            dz_ref[crow, CV0:CV0 + CONV_W] = (du * sgc).astype(dz_ref.dtype)
            u_c = uext[CONV_HALO + r0:CONV_HALO + r0 + CONV_CHUNK, :]
            dz_ref[crow, CG0:CG0 + CONV_W] = (du * u_c * (1.0 - sgc)).astype(dz_ref.dtype)

        @pl.when(i == nt - 1)
        def _():
            dcw_ref[...] = jnp.sum(dcw8[...], axis=1)

        xc, r, ig, sp, la, a, mult = _lru_gates(z_ref, zh_ref, pv_ref, wa_ref, wx_ref, rxext, first, tm)
        h = hl_ref[...]
        rowi = lax.broadcasted_iota(jnp.int32, (tm, LRU_W), 0)
        hlast = jnp.where(first, 0.0, hlh_ref[7:8, :])
        hprev = jnp.where(rowi == 0, hlast, pltpu.roll(h, 1, 0))
        dyl = dy_ref[:, ATTN_W + CONV_W:ATTN_W + CONV_W + LRU_W]
        gl, dgl = _gelu(z_ref[:, RG0:RG0 + LRU_W])
        dz_ref[:, RG0:RG0 + LRU_W] = (dyl * h * dgl).astype(dz_ref.dtype)
        dh = dyl * gl + jnp.where(rowi == tm - 1, gcar[0:1, :], 0.0)
        c = jnp.where(rowi == tm - 1, 0.0, pltpu.roll(a, tm - 1, 0))
        _, gg = _scan(c, dh, tm, reverse=True)
        gcar[0:1, :] = a[0:1, :] * gg[0:1, :]
        dmult = gg * (ig * xc)
        dig = gg * mult * xc
        dxc = gg * mult * ig
        dla = gg * hprev * a - dmult * a * a / mult
        dr = dla * (-LRU_C * sp)
        lam = pv_ref[R_LAM:R_LAM + 1, :]
        dpv_ref[R_LAM:R_LAM + 1, :] += jnp.sum(dla * (-LRU_C * r), axis=0, keepdims=True) * (-_sigmoid(-lam))
        dpa = dr * r * (1.0 - r)
        dpx = dig * ig * (1.0 - ig)
        addrow(R_BA, dpa)
        addrow(R_BX, dpx)
        dxc = dxc + _dot_nt(dpa, wa_ref[...]) + _dot_nt(dpx, wx_ref[...])
        dwa_ref[...] += _dot_tn(xc, dpa)
        dwx_ref[...] += _dot_tn(xc, dpx)
        addrow(R_LCONV_B, dxc)
        dxcext[0:tm, :] = dxc
        dxcext[tm:tm + LRU_HALO, :] = xccar[...]
        xccar[...] = dxc[0:LRU_HALO, :]
        drx = jnp.zeros((tm, LRU_W), F32)
        for k in range(LRU_K):
            addrow(R_LCW + k, dxc * rxext[pl.ds(LRU_HALO - (LRU_K - 1) + k, tm), :])
            drx = drx + pv_ref[R_LCW + k:R_LCW + k + 1, :] * dxcext[pl.ds(LRU_K - 1 - k, tm), :]
        dz_ref[:, RX0:RX0 + LRU_W] = drx.astype(dz_ref.dtype)

    tile = lambda w: pl.BlockSpec((tm, w), lambda i: (rev(i), 0))
    in_specs = [tile(D_MODEL)] + _mixer_in_specs(tm, rev) + [
        tile(D_MODEL), tile(LRU_W),
        pl.BlockSpec((8, LRU_W), lambda i: (jnp.maximum(rev(i) * (tm // 8) - 1, 0), 0)),
        tile(CONV_W), pl.BlockSpec((nb, 4 * BLK, 4 * BLK), lambda i: (rev(i), 0, 0)),
        pl.BlockSpec((nb, 4 * BLK, BLK), lambda i: (rev(i), 0, 0))]
    return _call(
        body, name, (nt,), in_specs,
        [tile(IN_W), _acc_spec((8, 4 * BLK)), _acc_spec((32, CONV_W)), _acc_spec((16, CONV_W)),
         _acc_spec((LRU_W, LRU_W)), _acc_spec((LRU_W, LRU_W))],
        [_sds((t, IN_W), MX), _sds((8, 4 * BLK), F32), _sds((32, CONV_W), F32), _sds((16, CONV_W), F32),
         _sds((LRU_W, LRU_W), F32), _sds((LRU_W, LRU_W), F32)],
        [pltpu.VMEM((tm + CONV_HALO, CONV_W), F32), pltpu.VMEM((7, tm + CONV_HALO - 8, CONV_W), F32),
         pltpu.VMEM((tm, CONV_W), F32), pltpu.VMEM((tm + LRU_HALO, LRU_W), F32),
         pltpu.VMEM((KV_W, tm + BLK), F32), pltpu.VMEM((KV_W, tm + BLK), F32),
         pltpu.VMEM((tm + CONV_HALO, CONV_W), F32), pltpu.VMEM((7, tm + CONV_HALO - 8, CONV_W), F32),
         pltpu.VMEM((32, 8, CONV_W), F32), pltpu.VMEM((tm + LRU_HALO, LRU_W), F32),
         pltpu.VMEM((KV_W, BLK), F32), pltpu.VMEM((KV_W, BLK), F32),
         pltpu.VMEM((CONV_HALO, CONV_W), F32), pltpu.VMEM((LRU_HALO, LRU_W), F32), pltpu.VMEM((8, LRU_W), F32)],
        [dy, z, z, sink, cw, pv, wa, wx, ycat, hl, hl, uc, probs, psinks], riders)


def _post_fwd(ycat, h0, gmix, w_out, g2, w_up, w_down, name, riders=()):
    t = h0.shape[0]
    tm = _tile(t, POST_TILE)
    nj = D_FF // FF_BLK

    def body(y_ref, h_ref, gm_ref, wo_ref, g2_ref, wu_ref, wd_ref, h1_ref, a_ref, h2_ref, ym_ref, hn_ref):
        ym, _, _ = _group_rms_fwd(y_ref[...], gm_ref[...])
        ym = ym.astype(MX)
        ym_ref[...] = ym
        h1 = h_ref[...] + jnp.dot(ym, wo_ref[...], preferred_element_type=F32)
        h1_ref[...] = h1
        hn, _, _ = _rms_fwd(h1, g2_ref[...])
        hn = hn.astype(MX)
        hn_ref[...] = hn
        for j in range(nj):
            u = jnp.dot(hn, wu_ref[j], preferred_element_type=F32)
            a_ref[:, j * FF_BLK:(j + 1) * FF_BLK] = jnp.square(jnp.maximum(u, 0.0)).astype(MX)
        h2_ref[...] = h1 + jnp.dot(a_ref[...], wd_ref[...], preferred_element_type=F32)

    tile = lambda w: pl.BlockSpec((tm, w), lambda i: (i, 0))
    return _call(
        body, name, (t // tm,),
        [tile(D_MODEL), tile(D_MODEL), _const_spec((1, D_MODEL)), _const_spec((D_MODEL, D_MODEL)),
         _const_spec((1, D_MODEL)), _const_spec((nj, D_MODEL, FF_BLK)), _const_spec((D_FF, D_MODEL))],
        [tile(D_MODEL), tile(D_FF), tile(D_MODEL), tile(D_MODEL), tile(D_MODEL)],
        [_sds((t, D_MODEL), F32), _sds((t, D_FF), MX), _sds((t, D_MODEL), F32), _sds((t, D_MODEL), MX),
         _sds((t, D_MODEL), MX)],
        [], [ycat, h0, gmix, w_out, g2, w_up, w_down], riders)


def _ffn_bwd(dh2, act, h1, g2, w_up_t, w_down, name, riders=()):
    t = h1.shape[0]
    tm = _tile(t, POST_TILE)
    nj = D_FF // FF_BLK

    def body(dh2_ref, a_ref, h1_ref, g2_ref, wut_ref, wd_ref, dh1_ref, dh1b_ref, dh2b_ref, du_ref, dg2_ref):
        @pl.when(pl.program_id(0) == 0)
        def _():
            dg2_ref[...] = jnp.zeros_like(dg2_ref)

        dh2 = dh2_ref[...]
        dh2b = dh2.astype(MX)
        dh2b_ref[...] = dh2b
        for j in range(nj):
            cols = slice(j * FF_BLK, (j + 1) * FF_BLK)
            da = _dot_nt(dh2b, wd_ref[j])
            du_ref[:, cols] = (da * (2.0 * jnp.sqrt(a_ref[:, cols].astype(F32)))).astype(MX)
        dhn = jnp.dot(du_ref[...], wut_ref[...], preferred_element_type=F32)
        _, xh, r = _rms_fwd(h1_ref[...], g2_ref[...])
        dx, dg = _rms_bwd(dhn, xh, r, g2_ref[...])
        dg2_ref[...] += dg
        dh1 = dh2 + dx
        dh1_ref[...] = dh1
        dh1b_ref[...] = dh1.astype(MX)

    tile = lambda w: pl.BlockSpec((tm, w), lambda i: (i, 0))
    return _call(
        body, name, (t // tm,),
        [tile(D_MODEL), tile(D_FF), tile(D_MODEL), _const_spec((1, D_MODEL)),
         _const_spec((D_FF, D_MODEL)), _const_spec((nj, FF_BLK, D_MODEL))],
        [tile(D_MODEL), tile(D_MODEL), tile(D_MODEL), tile(D_FF), _acc_spec((1, D_MODEL))],
        [_sds((t, D_MODEL), F32), _sds((t, D_MODEL), MX), _sds((t, D_MODEL), MX), _sds((t, D_FF), MX),
         _sds((1, D_MODEL), F32)],
        [], [dh2, act, h1, g2, w_up_t, w_down], riders)


def _mix_bwd(dh1, ycat, gmix, w_out, name):
    t = dh1.shape[0]
    tm = _tile(t)

    def body(dh1_ref, y_ref, gm_ref, wo_ref, dy_ref, dgm_ref):
        @pl.when(pl.program_id(0) == 0)
        def _():
            dgm_ref[...] = jnp.zeros_like(dgm_ref)

        dym = _dot_nt(dh1_ref[...], wo_ref[...])
        gm = gm_ref[...]
        _, yh, rr = _group_rms_fwd(y_ref[...], gm)
        outs, dgs = [], []
        for (a, b), rg in zip(_GROUPS, rr):
            dxg, dgg = _rms_bwd(dym[:, a:b], yh[:, a:b], rg, gm[:, a:b])
            outs.append(dxg)
            dgs.append(dgg)
        dy_ref[...] = jnp.concatenate(outs, axis=1)
        dgm_ref[...] += jnp.concatenate(dgs, axis=1)

    tile = pl.BlockSpec((tm, D_MODEL), lambda i: (i, 0))
    (dy, dgm), _ = _call(
        body, name, (t // tm,), [tile, tile, _const_spec((1, D_MODEL)), _const_spec((D_MODEL, D_MODEL))],
        [tile, _acc_spec((1, D_MODEL))], [_sds((t, D_MODEL), F32), _sds((1, D_MODEL), F32)],
        [], [dh1, ycat, gmix, w_out])
    return dy, dgm


def _in_bwd(dz, h0, dh1, g1, w_in, after, name):
    t = h0.shape[0]
    tm = _tile(t)

    def body(dz_ref, h_ref, dh1_ref, g_ref, w_ref, after_ref, dh0_ref, dg_ref):
        @pl.when(pl.program_id(0) == 0)
        def _():
            dg_ref[...] = jnp.zeros_like(dg_ref)

        dhn = _dot_nt(dz_ref[...], w_ref[...])
        _, xh, r = _rms_fwd(h_ref[...], g_ref[...])
        dx, dg = _rms_bwd(dhn, xh, r, g_ref[...])
        dg_ref[...] += dg
        dh0_ref[...] = dh1_ref[...] + dx

    tile = lambda w: pl.BlockSpec((tm, w), lambda i: (i, 0))
    (dh0, dg), _ = _call(
        body, name, (t // tm,),
        [tile(IN_W), tile(D_MODEL), tile(D_MODEL), _const_spec((1, D_MODEL)), _const_spec((D_MODEL, IN_W)),
         _const_spec((8, 128))],
        [tile(D_MODEL), _acc_spec((1, D_MODEL))], [_sds((t, D_MODEL), F32), _sds((1, D_MODEL), F32)],
        [], [dz, h0, dh1, g1, w_in, after])
    return dh0, dg


def _loss_head(h, gf, target, name):
    t = h.shape[0]
    tm = _tile(t)

    def body(h_ref, g_ref, t_ref, dh_ref, loss_ref, dg_ref):
        @pl.when(pl.program_id(0) == 0)
        def _():
            loss_ref[...] = jnp.zeros_like(loss_ref)
            dg_ref[...] = jnp.zeros_like(dg_ref)

        g = g_ref[...]
        y, xh, r = _rms_fwd(h_ref[...], g)
        err = y - t_ref[...]
        part = 0.5 * jnp.sum(jnp.mean(err * err, axis=-1, keepdims=True), axis=0, keepdims=True)
        loss_ref[...] += jnp.broadcast_to(part, loss_ref.shape)
        dx, dg = _rms_bwd(err * (1.0 / D_MODEL), xh, r, g)
        dg_ref[...] += dg
        dh_ref[...] = dx

    tile = pl.BlockSpec((tm, D_MODEL), lambda i: (i, 0))
    (dh, loss, dg), _ = _call(
        body, name, (t // tm,), [tile, _const_spec((1, D_MODEL)), tile],
        [tile, _acc_spec((1, 128)), _acc_spec((1, D_MODEL))],
        [_sds((t, D_MODEL), F32), _sds((1, 128), F32), _sds((1, D_MODEL), F32)], [], [h, gf, target])
    return dh, loss, dg


def _dw(x, y, name, split, bm, bn):
    t, m = x.shape
    n = y.shape[1]
    tk = _tile(t, DW_TILE)
    nk = t // tk
    if split == "rows":
        assert bn == n
        r, c = m // N_DEV, n
        per = bm // r
        out_block = pl.BlockSpec((per, r, c), lambda a, b, k: (a, 0, 0))
    else:
        assert bm == m
        r, c = m, n // N_DEV
        per = bn // c
        out_block = pl.BlockSpec((per, r, c), lambda a, b, k: (b, 0, 0))

    def body(x_ref, y_ref, o_ref, o16_ref, acc):
        k = pl.program_id(2)

        @pl.when(k == 0)
        def _():
            acc[...] = jnp.zeros_like(acc)

        acc[...] += _dot_tn(x_ref[...], y_ref[...])

        @pl.when(k == nk - 1)
        def _():
            for d in range(per):
                v = acc[d * r:(d + 1) * r, :] if split == "rows" else acc[:, d * c:(d + 1) * c]
                o_ref[d] = v
                o16_ref[d] = v.astype(o16_ref.dtype)

    return pl.pallas_call(
        body, name=name, grid=(m // bm, n // bn, nk),
        in_specs=[pl.BlockSpec((tk, bm), lambda a, b, k: (k, a)), pl.BlockSpec((tk, bn), lambda a, b, k: (k, b))],
        out_specs=[out_block, out_block],
        out_shape=[_sds((N_DEV, r, c), F32), _sds((N_DEV, r, c), WIRE)],
        scratch_shapes=[pltpu.VMEM((bm, bn), F32)],
        compiler_params=pltpu.CompilerParams(dimension_semantics=("arbitrary",) * 3, vmem_limit_bytes=VMEM_LIMIT),
    )(x, y)


def _adamw_math(w, g, m, v):
    m = ADAM_B1 * m + (1.0 - ADAM_B1) * g
    v = ADAM_B2 * v + (1.0 - ADAM_B2) * jnp.square(g)
    m_hat = m / (1.0 - ADAM_B1 ** ADAM_STEP)
    v_hat = v / (1.0 - ADAM_B2 ** ADAM_STEP)
    delta = -ADAM_LR * (m_hat / (jnp.sqrt(v_hat) + ADAM_EPS) + ADAM_WD * w)
    return delta, m, v


def _adamw_shard(g_own, g_recv, dev, w, m, v, after, name):
    _, r, c = w.shape
    br = r
    for cand in (256, 128, 112, 64, 56, 32, 16, 8):
        if r % cand == 0:
            br = cand
            break
    nr = r // br
    own = lambda l: pl.BlockSpec((1, br, c), lambda ll, i, d: (d[0], jnp.where(ll == l, i, (nr - 1) * (1 - l)), 0))
    recv = lambda l: pl.BlockSpec((N_DEV - 1, br, c), lambda ll, i, d: (0, jnp.where(ll == l, i, (nr - 1) * (1 - l)), 0))

    def body(dev_ref, go0, gr0, go1, gr1, w_ref, m_ref, v_ref, after_ref, g_out, d_out, m_out, v_out):
        def update(go_ref, gr_ref):
            g = go_ref[0]
            for j in range(N_DEV - 1):
                g = g + gr_ref[j].astype(F32)
            delta, mn, vn = _adamw_math(w_ref[0], g, m_ref[0], v_ref[0])
            g_out[0] = g
            d_out[0] = delta
            m_out[0] = mn
            v_out[0] = vn

        layer = pl.program_id(0)
        pl.when(layer == 0)(lambda: update(go0, gr0))
        pl.when(layer == 1)(lambda: update(go1, gr1))

    tile = pl.BlockSpec((1, br, c), lambda ll, i, d: (ll, i, 0))
    return pl.pallas_call(
        body, name=name,
        grid_spec=pltpu.PrefetchScalarGridSpec(
            num_scalar_prefetch=1, grid=(2, nr),
            in_specs=[own(0), recv(0), own(1), recv(1), tile, tile, tile,
                      pl.BlockSpec((8, 128), lambda ll, i, d: (0, 0))],
            out_specs=[tile, tile, tile, tile]),
        out_shape=[_sds((2, r, c), F32)] * 4,
        compiler_params=pltpu.CompilerParams(dimension_semantics=("arbitrary",) * 2, vmem_limit_bytes=VMEM_LIMIT),
    )(dev, g_own[0], g_recv[0], g_own[1], g_recv[1], w, m, v, after)


def _adamw_small(gs, ws, ms, vs, name):
    n = len(gs)

    def body(*refs):
        g_refs, w_refs, m_refs, v_refs = (refs[k * n:(k + 1) * n] for k in range(4))
        outs = refs[4 * n:]
        for k in range(n):
            delta, mn, vn = _adamw_math(w_refs[k][...], g_refs[k][...], m_refs[k][...], v_refs[k][...])
            outs[k][...] = delta
            outs[n + k][...] = mn
            outs[2 * n + k][...] = vn

    shapes = [_sds(w.shape, F32) for w in ws]
    res = pl.pallas_call(body, name=name, out_shape=shapes * 3,
                         compiler_params=pltpu.CompilerParams(vmem_limit_bytes=VMEM_LIMIT))(*gs, *ws, *ms, *vs)
    return res[:n], res[n:2 * n], res[2 * n:]


def _sum_parts(part, dev, name):
    def body(dev_ref, p_ref, o_ref):
        me = dev_ref[0]
        g = p_ref[me]
        for d in range(1, N_DEV):
            g = g + p_ref[jnp.bitwise_xor(me, d)]
        o_ref[...] = g

    full = pl.BlockSpec(part.shape, lambda i, d: (0, 0, 0))
    return pl.pallas_call(
        body, name=name,
        grid_spec=pltpu.PrefetchScalarGridSpec(
            num_scalar_prefetch=1, grid=(1,), in_specs=[full],
            out_specs=pl.BlockSpec(part.shape[1:], lambda i, d: (0, 0))),
        out_shape=_sds(part.shape[1:], F32))(dev, part)


HBM = pl.BlockSpec(memory_space=pltpu.HBM)
SEM = pl.BlockSpec(memory_space=pltpu.SEMAPHORE)
EFFECT = pltpu.SideEffectType.DATAFLOW_SIDE_EFFECTING


def _direct_copies(srcs, lands, ssem, rsem, scatter):
    x, y, c = _me()
    out = []
    for a in range(len(srcs)):
        for f in range(1, N_DEV):
            px = 1 - x if f & 4 else x
            py = 1 - y if f & 2 else y
            pc = 1 - c if f & 1 else c
            out.append(pltpu.make_async_remote_copy(
                src_ref=srcs[a].at[4 * px + 2 * py + pc] if scatter else srcs[a], dst_ref=lands[a].at[f - 1],
                send_sem=ssem.at[7 * a + f - 1], recv_sem=rsem.at[7 * a + f - 1],
                device_id=(px, py, pc), device_id_type=MESH))
    return out


def _send_start(arrays, scatter, name):
    arrays = list(arrays)
    n = len(arrays)
    lands = [lax.empty((N_DEV - 1,) + (a.shape[1:] if scatter else a.shape), a.dtype) for a in arrays]

    def body(*refs):
        srcs, lnds, ssem, rsem, token = refs[:n], refs[n:2 * n], refs[2 * n], refs[2 * n + 1], refs[-1]
        for cp in _direct_copies(srcs, lnds, ssem, rsem, scatter):
            cp.start()
        token[...] = jnp.zeros_like(token)

    hbm = lambda a: pltpu.HBM(a.shape, a.dtype)
    res = pl.pallas_call(
        body, name=name,
        out_shape=(pltpu.SemaphoreType.DMA((7 * n,)), pltpu.SemaphoreType.DMA((7 * n,)),
                   *[hbm(a) for a in arrays + lands], _sds((8, 128), F32)),
        in_specs=[HBM] * (2 * n),
        out_specs=(SEM, SEM, *[HBM] * (2 * n), pl.BlockSpec(memory_space=pltpu.VMEM)),
        input_output_aliases={i: 2 + i for i in range(2 * n)},
        compiler_params=pltpu.CompilerParams(has_side_effects=EFFECT),
    )(*[pltpu.with_memory_space_constraint(a, pltpu.HBM) for a in arrays + lands])
    return types.SimpleNamespace(ssem=res[0], rsem=res[1], srcs=list(res[2:2 + n]), lands=list(res[2 + n:2 + 2 * n]),
                                 token=res[-1], scatter=scatter)


def _send_wait(h, after, name):
    n = len(h.srcs)

    def body(*refs):
        srcs, lnds, ssem, rsem = refs[:n], refs[n:2 * n], refs[2 * n], refs[2 * n + 1]
        for cp in _direct_copies(srcs, lnds, ssem, rsem, h.scatter):
            cp.wait_send()
            cp.wait_recv()

    hbm = lambda a: pltpu.HBM(a.shape, a.dtype)
    res = pl.pallas_call(
        body, name=name,
        out_shape=tuple(hbm(a) for a in h.srcs + h.lands),
        in_specs=[HBM] * (2 * n) + [SEM, SEM, ANY], out_specs=[HBM] * (2 * n),
        input_output_aliases={i: i for i in range(2 * n)},
        compiler_params=pltpu.CompilerParams(has_side_effects=EFFECT),
    )(*h.srcs, *h.lands, h.ssem, h.rsem, after)
    return list(res[:n]), list(res[n:])


def _block_diag(w):
    out = jnp.zeros((LRU_W, LRU_W), w.dtype)
    for h in range(4):
        out = lax.dynamic_update_slice(out, w[h], (h * 64, h * 64))
    return out


def _unblock_diag(w):
    return jnp.concatenate([w[h * 64:(h + 1) * 64, h * 64:(h + 1) * 64] for h in range(4)], axis=0)


def _layer_params(p, l):
    row = lambda a: a[l].reshape(1, -1)
    sink_rows = jnp.repeat(p["attn_sinks"][l].reshape(4, 2), 2 * BLK, axis=1)
    sink_rows = jnp.concatenate([sink_rows, jnp.zeros((4, 4 * BLK), F32)], axis=0)
    cw = jnp.concatenate([p["conv_dw_w"][l], jnp.zeros((1, CONV_W), F32)], axis=0)
    pv = jnp.concatenate([
        row(p["conv_dw_b"]), row(p["conv_ln_g"]), row(p["conv_ln_b"]), row(p["lru_conv_b"]), row(p["lru_ba"]),
        row(p["lru_bx"]), row(p["lru_lambda"]), jnp.zeros((1, LRU_W), F32), p["lru_conv_w"][l],
        jnp.zeros((4, LRU_W), F32)], axis=0)
    return dict(
        g1=row(p["norm1"]), sink=sink_rows, cw=cw, pv=pv,
        wa=_block_diag(p["lru_wa"][l]).astype(MX), wx=_block_diag(p["lru_wx"][l]).astype(MX),
        gmix=row(p["mix_norm"]), g2=row(p["norm2"]))


_SMALL = ["norm1", "attn_sinks", "conv_dw_w", "conv_dw_b", "conv_ln_g", "conv_ln_b", "lru_conv_w", "lru_conv_b",
          "lru_wa", "lru_ba", "lru_wx", "lru_bx", "lru_lambda", "mix_norm", "norm2"]
_BIG = ["w_in", "w_out", "w_up", "w_down"]
_WEIGHTS = ["norm1", "w_in", "attn_sinks", "conv_dw_w", "conv_dw_b", "conv_ln_g", "conv_ln_b", "lru_conv_w",
            "lru_conv_b", "lru_wa", "lru_ba", "lru_wx", "lru_bx", "lru_lambda", "mix_norm", "w_out", "norm2", "w_up",
            "w_down", "final_norm"]


def kernel(x, norm1, w_in, attn_sinks, conv_dw_w, conv_dw_b, conv_ln_g, conv_ln_b, lru_conv_w, lru_conv_b, lru_wa, lru_ba, lru_wx, lru_bx, lru_lambda, mix_norm, w_out, norm2, w_up, w_down, final_norm, loss_target, m_norm1, m_w_in, m_attn_sinks, m_conv_dw_w, m_conv_dw_b, m_conv_ln_g, m_conv_ln_b, m_lru_conv_w, m_lru_conv_b, m_lru_wa, m_lru_ba, m_lru_wx, m_lru_bx, m_lru_lambda, m_mix_norm, m_w_out, m_norm2, m_w_up, m_w_down, m_final_norm, v_norm1, v_w_in, v_attn_sinks, v_conv_dw_w, v_conv_dw_b, v_conv_ln_g, v_conv_ln_b, v_lru_conv_w, v_lru_conv_b, v_lru_wa, v_lru_ba, v_lru_wx, v_lru_bx, v_lru_lambda, v_mix_norm, v_w_out, v_norm2, v_w_up, v_w_down, v_final_norm):
    w = dict(norm1=norm1, w_in=w_in, attn_sinks=attn_sinks, conv_dw_w=conv_dw_w, conv_dw_b=conv_dw_b,
             conv_ln_g=conv_ln_g, conv_ln_b=conv_ln_b, lru_conv_w=lru_conv_w, lru_conv_b=lru_conv_b, lru_wa=lru_wa,
             lru_ba=lru_ba, lru_wx=lru_wx, lru_bx=lru_bx, lru_lambda=lru_lambda, mix_norm=mix_norm, w_out=w_out,
             norm2=norm2, w_up=w_up, w_down=w_down, final_norm=final_norm)
    m = dict(norm1=m_norm1, w_in=m_w_in, attn_sinks=m_attn_sinks, conv_dw_w=m_conv_dw_w, conv_dw_b=m_conv_dw_b,
             conv_ln_g=m_conv_ln_g, conv_ln_b=m_conv_ln_b, lru_conv_w=m_lru_conv_w, lru_conv_b=m_lru_conv_b,
             lru_wa=m_lru_wa, lru_ba=m_lru_ba, lru_wx=m_lru_wx, lru_bx=m_lru_bx, lru_lambda=m_lru_lambda,
             mix_norm=m_mix_norm, w_out=m_w_out, norm2=m_norm2, w_up=m_w_up, w_down=m_w_down, final_norm=m_final_norm)
    v = dict(norm1=v_norm1, w_in=v_w_in, attn_sinks=v_attn_sinks, conv_dw_w=v_conv_dw_w, conv_dw_b=v_conv_dw_b,
             conv_ln_g=v_conv_ln_g, conv_ln_b=v_conv_ln_b, lru_conv_w=v_lru_conv_w, lru_conv_b=v_lru_conv_b,
             lru_wa=v_lru_wa, lru_ba=v_lru_ba, lru_wx=v_lru_wx, lru_bx=v_lru_bx, lru_lambda=v_lru_lambda,
             mix_norm=v_mix_norm, w_out=v_w_out, norm2=v_norm2, w_up=v_w_up, w_down=v_w_down, final_norm=v_final_norm)
    depth = w_in.shape[0]
    xi, yi, ci = _me()
    dev = (4 * xi + 2 * yi + ci).astype(jnp.int32)
    dev1 = dev.reshape(1)
    wb = {n: w[n].astype(MX) for n in _BIG}
    layer_shards = lambda l: [wb["w_out"][l], wb["w_up"][l], wb["w_down"][l]]

    _, ((g_in0, g_cw, g_lcw),) = _call(None, "gather_first", None, [], [], [], [], [],
                                        [_gather_rider([wb["w_in"][0], conv_dw_w, lru_conv_w])])
    cols = lambda g: jnp.moveaxis(g, 0, -2).reshape(g.shape[1:-1] + (N_DEV * g.shape[-1],))
    p = dict(w)
    p["conv_dw_w"] = cols(g_cw)
    p["lru_conv_w"] = cols(g_lcw)
    lp = [_layer_params(p, l) for l in range(depth)]

    gathered = [dict(w_in=cols(g_in0)), dict()]
    saved = []
    h = x[0]
    for l in range(depth):
        q, gw = lp[l], gathered[l]
        z, hn1 = _ln_in(h, q["g1"], gw["w_in"], f"ln_in{l}")
        riders = [_gather_rider(layer_shards(0))] if l == 0 else []
        (ycat, hl, uc, probs, psinks), got = _mixer_fwd(z, q["sink"], q["cw"], q["pv"], q["wa"], q["wx"],
                                                        f"mixer_fwd{l}", riders)
        if l == 0:
            gw["w_out"], gw["w_up"], gw["w_down"] = got[0]
            gw["w_out"] = gw["w_out"].reshape(D_MODEL, D_MODEL)
        riders = [_gather_rider([wb["w_in"][1]] + layer_shards(1))] if l == 0 else []
        (h1, act, h2, ym, hn2), got = _post_fwd(ycat, h, q["gmix"], gw["w_out"], q["g2"], gw["w_up"],
                                                gw["w_down"].reshape(D_FF, D_MODEL), f"post_fwd{l}", riders)
        if l == 0:
            nxt = gathered[1]
            nxt["w_in"], nxt["w_out"], nxt["w_up"], nxt["w_down"] = got[0]
            nxt["w_in"] = cols(nxt["w_in"])
            nxt["w_out"] = nxt["w_out"].reshape(D_MODEL, D_MODEL)
        saved.append(dict(h0=h, z=z, hn1=hn1, ycat=ycat, hl=hl, uc=uc, probs=probs, psinks=psinks, h1=h1, act=act,
                          ym=ym, hn2=hn2))
        h = h2
    dh, loss, dgf = _loss_head(h, final_norm.reshape(1, -1), loss_target[0], "loss_head")

    grads = [None] * depth
    big = {n: [None] * depth for n in _BIG}
    pending = []

    def send_pending():
        riders = [_scatter_rider([item[3] for item in pending])] if pending else []
        return riders, list(pending)

    def record(sent, got):
        for item, recv in zip(sent, got[0] if sent else []):
            big[item[0]][item[1]] = (item[2], recv)
        del pending[:len(sent)]

    for l in reversed(range(depth)):
        q, s, gw = lp[l], saved[l], gathered[l]
        riders, sent = send_pending()
        w_up_t = jnp.swapaxes(gw["w_up"], 1, 2).reshape(D_FF, D_MODEL)
        (dh1, dh1b, dhb, du, dg2), got = _ffn_bwd(dh, s["act"], s["h1"], q["g2"], w_up_t, gw["w_down"],
                                                  f"ffn_bwd{l}", riders)
        record(sent, got)
        dycat, dgm = _mix_bwd(dh1b, s["ycat"], q["gmix"], gw["w_out"], f"mix_bwd{l}")
        pending.append(("w_down", l) + tuple(_dw(s["act"], dhb, f"dw_down{l}", "rows", 2048, D_MODEL)))
        pending.append(("w_up", l) + tuple(_dw(s["hn2"], du, f"dw_up{l}", "cols", D_MODEL, 2048)))
        pending.append(("w_out", l) + tuple(_dw(s["ym"], dh1b, f"dw_out{l}", "rows", D_MODEL, D_MODEL)))
        riders, sent = send_pending()
        (dz, dsink, dcw, dpv, dwa, dwx), got = _mixer_bwd(
            dycat, s["z"], s["ycat"], s["hl"], s["uc"], s["probs"], s["psinks"], q["sink"], q["cw"], q["pv"], q["wa"],
            q["wx"], f"mixer_bwd{l}", riders)
        record(sent, got)
        d_win = _dw(s["hn1"], dz, f"dw_in{l}", "cols", D_MODEL, IN_W)
        if l > 0:
            pending.append(("w_in", l) + tuple(d_win))
            after = jnp.zeros((8, 128), F32)
        else:
            win_sends = _send_start([d_win[1]], True, "scatter_w_in0_start")
            after = win_sends.token
        dh, dg1 = _in_bwd(dz, s["h0"], dh1, q["g1"], gw["w_in"], after, f"in_bwd{l}")
        grads[l] = dict(
            norm1=dg1[0], attn_sinks=jnp.stack([dsink[0:4, 0], dsink[0:4, 2 * BLK]], axis=1).reshape(8),
            conv_dw_w=dcw[0:CONV_K], conv_dw_b=dpv[R_CONV_B], conv_ln_g=dpv[R_LN_G], conv_ln_b=dpv[R_LN_B],
            lru_conv_w=dpv[R_LCW:R_LCW + LRU_K], lru_conv_b=dpv[R_LCONV_B], lru_wa=_unblock_diag(dwa),
            lru_ba=dpv[R_BA].reshape(4, 64), lru_wx=_unblock_diag(dwx), lru_bx=dpv[R_BX].reshape(4, 64),
            lru_lambda=dpv[R_LAM], mix_norm=dgm[0], norm2=dg2[0])

    small = [jnp.stack([grads[l][n] for l in range(depth)]) for n in _SMALL] + [dgf, loss[:, 0:1]]
    sizes = [a.size for a in small]
    total = -(-sum(sizes) // 1024) * 1024
    packed = jnp.concatenate([a.reshape(-1) for a in small] + [jnp.zeros((total - sum(sizes),), F32)])
    packed = packed.reshape(total // 128, 128)
    small_sends = _send_start([packed], False, "bcast_small_start")

    out = {}
    shard_update = lambda n, after: list(_adamw_shard(
        [big[n][l][0] for l in range(depth)], [big[n][l][1] for l in range(depth)], dev1, w[n], m[n], v[n], after,
        f"adamw_{n}"))
    for n in ("w_out", "w_up", "w_down"):
        out[n] = shard_update(n, small_sends.token)
    _, (win_recv,) = _send_wait(win_sends, out["w_down"][1], "scatter_w_in0_wait")
    big["w_in"][0] = (d_win[0], win_recv)
    (packed,), (small_recv,) = _send_wait(small_sends, win_recv, "bcast_small_wait")
    out["w_in"] = shard_update("w_in", jnp.zeros((8, 128), F32))
    parts = jnp.concatenate([packed[None], small_recv], axis=0)
    summed = _sum_parts(parts, dev1, "sum_small_grads").reshape(-1)
    small_sums, pos = [], 0
    for a, size in zip(small, sizes):
        small_sums.append(summed[pos:pos + size].reshape(a.shape))
        pos += size
    shard = lambda a: lax.dynamic_slice_in_dim(a, dev * (a.shape[-1] // N_DEV), a.shape[-1] // N_DEV, axis=a.ndim - 1)
    flat = {"lru_wa": (depth, LRU_W, 64), "lru_wx": (depth, LRU_W, 64), "final_norm": (1, D_MODEL)}
    gs, ws, ms, vs = [], [], [], []
    for n, g in zip(_SMALL + ["final_norm"], small_sums[:-1]):
        shp = flat.get(n, w[n].shape)
        gs.append((shard(g) if n in ("conv_dw_w", "lru_conv_w") else g).reshape(shp))
        ws.append(w[n].reshape(shp))
        ms.append(m[n].reshape(shp))
        vs.append(v[n].reshape(shp))
    sd, sm, sv = _adamw_small(gs, ws, ms, vs, "adamw_small")
    for j, n in enumerate(_SMALL + ["final_norm"]):
        out[n] = [a.reshape(w[n].shape) for a in (gs[j], sd[j], sm[j], sv[j])]
    loss_total = small_sums[-1][0, 0]

    result = [loss_total, dh[None]]
    for j in range(4):
        result += [out[n][j] for n in _WEIGHTS]
    return tuple(result)
```

```python
import types

import jax
import jax.numpy as jnp
from jax import lax
from jax.experimental import pallas as pl
from jax.experimental.pallas import tpu as pltpu

F32 = jnp.float32
MX = jnp.bfloat16
WIRE = jnp.bfloat16

D_MODEL = 1024
HEAD_DIM = 64
ATTN_W = 512
KV_W = 128
BLK = 128
CONV_W = 256
CONV_K = 31
LRU_W = 256
LRU_K = 4
LRU_C = 8.0
IN_W = 1792
D_FF = 4096
FF_BLK = 512
N_DEV = 8
IN_SHARD = IN_W // N_DEV
RMS_EPS = 1e-6
LN_EPS = 1e-5
MASK_VALUE = -1e30
SCALE = HEAD_DIM ** -0.5
CONV_HALO = 32
LRU_HALO = 8
CONV_CHUNK = 64
POST_TILE = 512
DW_TILE = 1024
Q0, K0, V0, CV0, CG0, RX0, RG0 = 0, 512, 640, 768, 1024, 1280, 1536
SV_U, SV_SG, SV_XC, SV_R, SV_IG, SV_A, SV_MULT, SV_W = 0, 256, 512, 768, 1024, 1280, 1536, 1792
R_CONV_B, R_LN_G, R_LN_B, R_LCONV_B, R_BA, R_BX, R_LAM, R_LCW = 0, 1, 2, 3, 4, 5, 6, 8

ADAM_LR, ADAM_B1, ADAM_B2, ADAM_EPS, ADAM_WD, ADAM_STEP = 0.001, 0.9, 0.999, 1e-08, 0.01, 10

VMEM_LIMIT = 56 * 1024 * 1024
MESH = pl.DeviceIdType.MESH
ANY = pl.BlockSpec(memory_space=pl.ANY)


def _tile(t, cap=512):
    return min(cap, t)


def _dot(a, b):
    return jnp.dot(a.astype(MX), b.astype(MX), preferred_element_type=F32)


def _dot_nt(a, b):
    return lax.dot_general(a.astype(MX), b.astype(MX), (((1,), (1,)), ((), ())), preferred_element_type=F32)


def _dot_tn(a, b):
    return lax.dot_general(a.astype(MX), b.astype(MX), (((0,), (0,)), ((), ())), preferred_element_type=F32)


def _const_spec(shape):
    nd = len(shape)
    return pl.BlockSpec(shape, lambda *_: (0,) * nd, pipeline_mode=pl.Buffered(1))


def _acc_spec(shape):
    nd = len(shape)
    return pl.BlockSpec(shape, lambda *_: (0,) * nd)


def _sds(shape, dtype):
    return jax.ShapeDtypeStruct(shape, dtype)


def _sigmoid(x):
    return jax.nn.sigmoid(x)


def _rms_fwd(x, g):
    r = lax.rsqrt(jnp.mean(x * x, axis=-1, keepdims=True) + RMS_EPS)
    xh = x * r
    return xh * g, xh, r


def _rms_bwd(dy, xh, r, g):
    t = dy * g
    dx = r * (t - xh * jnp.mean(t * xh, axis=-1, keepdims=True))
    return dx, jnp.sum(dy * xh, axis=0, keepdims=True)


_GROUPS = ((0, 512), (512, 768), (768, 1024))


def _group_rms_fwd(y, g):
    parts = [_rms_fwd(y[:, a:b], g[:, a:b]) for a, b in _GROUPS]
    return (jnp.concatenate([p[0] for p in parts], axis=1),
            jnp.concatenate([p[1] for p in parts], axis=1),
            [p[2] for p in parts])


def _gelu(x):
    c = 0.7978845608028654
    u = c * (x + 0.044715 * x * x * x)
    th = jnp.tanh(u)
    val = 0.5 * x * (1.0 + th)
    grad = 0.5 * (1.0 + th) + 0.5 * x * (1.0 - th * th) * c * (1.0 + 3.0 * 0.044715 * x * x)
    return val, grad


def _neg_expm1(x):
    series = -x * (1.0 + x * (0.5 + x * (1.0 / 6.0 + x * (1.0 / 24.0))))
    return jnp.where(x > -0.02, series, 1.0 - jnp.exp(x))


def _me():
    return lax.axis_index("x"), lax.axis_index("y"), lax.axis_index("c")


def _gather_rider(arrays):
    arrays = list(arrays)
    n = len(arrays)

    def plan(ins, outs, sems):
        ssem, rsem, lsem = sems
        x, y, c = _me()
        chips = [(1 - x, y), (x, 1 - y), (1 - x, 1 - y)]

        def copy(a, k, block, to, own=False):
            dst = outs[a].at[4 * block[0] + 2 * block[1] + block[2]]
            return pltpu.make_async_remote_copy(
                src_ref=ins[a] if own else dst, dst_ref=dst, send_sem=ssem.at[7 * a + k],
                recv_sem=rsem.at[7 * a + k], device_id=to, device_id_type=MESH)

        return x, y, c, chips, copy, lsem

    def start(ins, outs, sems):
        x, y, c, chips, copy, lsem = plan(ins, outs, sems)
        for a in range(n):
            pltpu.make_async_copy(ins[a], outs[a].at[4 * x + 2 * y + c], lsem.at[a]).start()
            copy(a, 0, (x, y, c), (x, y, 1 - c), own=True).start()
            for j, chip in enumerate(chips):
                copy(a, 1 + j, (x, y, c), (*chip, c), own=True).start()

    def mid(ins, outs, sems):
        x, y, c, chips, copy, _ = plan(ins, outs, sems)
        for a in range(n):
            for j, chip in enumerate(chips):
                copy(a, 1 + j, (*chip, c), (x, y, c)).wait_recv()
                copy(a, 4 + j, (*chip, c), (x, y, 1 - c)).start()

    def finish(ins, outs, sems):
        x, y, c, chips, copy, lsem = plan(ins, outs, sems)
        for a in range(n):
            copy(a, 0, (x, y, 1 - c), (x, y, c)).wait_recv()
            for j, chip in enumerate(chips):
                copy(a, 4 + j, (*chip, 1 - c), (x, y, c)).wait_recv()
        for a in range(n):
            copy(a, 0, (x, y, c), (x, y, 1 - c), own=True).wait_send()
            for j, chip in enumerate(chips):
                copy(a, 1 + j, (x, y, c), (*chip, c), own=True).wait_send()
                copy(a, 4 + j, (*chip, c), (x, y, 1 - c)).wait_send()
            pltpu.make_async_copy(ins[a], outs[a].at[4 * x + 2 * y + c], lsem.at[a]).wait()

    return types.SimpleNamespace(
        arrays=arrays, out_shape=[_sds((N_DEV,) + a.shape, a.dtype) for a in arrays],
        scratch=[pltpu.SemaphoreType.DMA((7 * n,)), pltpu.SemaphoreType.DMA((7 * n,)), pltpu.SemaphoreType.DMA((n,))],
        start=start, mid=mid, finish=finish)


def _bcast_rider(arrays):
    arrays = list(arrays)
    n = len(arrays)

    def copies(ins, outs, sems, landing):
        ssem, rsem, lsem = sems
        x, y, c = _me()
        out = []
        for a in range(n):
            out.append(pltpu.make_async_copy(ins[a], outs[a].at[4 * x + 2 * y + c], lsem.at[a]))
            for f in range(1, N_DEV):
                px = 1 - x if f & 4 else x
                py = 1 - y if f & 2 else y
                pc = 1 - c if f & 1 else c
                slot = 4 * px + 2 * py + pc if landing else 4 * x + 2 * y + c
                out.append(pltpu.make_async_remote_copy(
                    src_ref=ins[a], dst_ref=outs[a].at[slot], send_sem=ssem.at[7 * a + f - 1],
                    recv_sem=rsem.at[7 * a + f - 1], device_id=(px, py, pc), device_id_type=MESH))
        return out

    def start(ins, outs, sems):
        for cp in copies(ins, outs, sems, landing=False):
            cp.start()

    def finish(ins, outs, sems):
        for cp in copies(ins, outs, sems, landing=True):
            cp.wait()

    return types.SimpleNamespace(
        arrays=arrays, out_shape=[_sds((N_DEV,) + a.shape, a.dtype) for a in arrays],
        scratch=[pltpu.SemaphoreType.DMA((7 * n,)), pltpu.SemaphoreType.DMA((7 * n,)), pltpu.SemaphoreType.DMA((n,))],
        start=start, mid=None, finish=finish)


def _scatter_rider(arrays):
    arrays = list(arrays)
    n = len(arrays)

    def copies(ins, outs, sems):
        ssem, rsem = sems
        x, y, c = _me()
        out = []
        for a in range(n):
            for f in range(1, N_DEV):
                px = 1 - x if f & 4 else x
                py = 1 - y if f & 2 else y
                pc = 1 - c if f & 1 else c
                out.append(pltpu.make_async_remote_copy(
                    src_ref=ins[a].at[4 * px + 2 * py + pc], dst_ref=outs[a].at[f - 1], send_sem=ssem.at[7 * a + f - 1],
                    recv_sem=rsem.at[7 * a + f - 1], device_id=(px, py, pc), device_id_type=MESH))
        return out

    def start(ins, outs, sems):
        for cp in copies(ins, outs, sems):
            cp.start()

    def finish(ins, outs, sems):
        for cp in copies(ins, outs, sems):
            cp.wait()

    return types.SimpleNamespace(
        arrays=arrays, out_shape=[_sds((N_DEV - 1,) + a.shape[1:], a.dtype) for a in arrays],
        scratch=[pltpu.SemaphoreType.DMA((7 * n,)), pltpu.SemaphoreType.DMA((7 * n,))],
        start=start, mid=None, finish=finish)


def _call(body, name, grid, in_specs, out_specs, out_shape, scratch, operands, riders=()):
    n_in, n_out, n_scr = len(operands), len(out_shape), len(scratch)
    nsteps = grid[0] if grid else 1
    sizes = [(len(r.arrays), len(r.out_shape), len(r.scratch)) for r in riders]

    def wrapped(*refs):
        pos = n_in
        r_ins = []
        for ri, _, _ in sizes:
            r_ins.append(refs[pos:pos + ri])
            pos += ri
        outs = refs[pos:pos + n_out]
        pos += n_out
        r_outs = []
        for _, ro, _ in sizes:
            r_outs.append(refs[pos:pos + ro])
            pos += ro
        scr = refs[pos:pos + n_scr]
        pos += n_scr
        r_sems = []
        for _, _, rs in sizes:
            r_sems.append(refs[pos:pos + rs])
            pos += rs
        step = pl.program_id(0) if grid else 0

        def at(s, fn):
            if grid:
                pl.when(step == s)(fn)
            else:
                fn()

        for r, a, b, c in zip(riders, r_ins, r_outs, r_sems):
            at(0, lambda r=r, a=a, b=b, c=c: r.start(a, b, c))
        for r, a, b, c in zip(riders, r_ins, r_outs, r_sems):
            if r.mid is not None:
                at((3 * nsteps) // 4, lambda r=r, a=a, b=b, c=c: r.mid(a, b, c))
        if body is not None:
            body(*refs[:n_in], *outs, *scr)
        for r, a, b, c in zip(riders, r_ins, r_outs, r_sems):
            at(nsteps - 1, lambda r=r, a=a, b=b, c=c: r.finish(a, b, c))

    r_arrays = [a for r in riders for a in r.arrays]
    r_shapes = [s for r in riders for s in r.out_shape]
    kwargs = {}
    if grid:
        kwargs = dict(grid=grid, compiler_params=pltpu.CompilerParams(
            dimension_semantics=("arbitrary",) * len(grid), vmem_limit_bytes=VMEM_LIMIT))
    res = pl.pallas_call(
        wrapped, name=name,
        in_specs=list(in_specs) + [ANY] * len(r_arrays),
        out_specs=list(out_specs) + [ANY] * len(r_shapes),
        out_shape=list(out_shape) + r_shapes,
        scratch_shapes=list(scratch) + [s for r in riders for s in r.scratch],
        **kwargs,
    )(*operands, *r_arrays)
    host, rest = res[:n_out], res[n_out:]
    r_res = []
    for _, ro, _ in sizes:
        r_res.append(rest[:ro])
        rest = rest[ro:]
    return host, r_res


def _ln_in(h, g1, w_in, name):
    t = h.shape[0]
    tm = _tile(t)

    def body(h_ref, g_ref, w_ref, z_ref, hn_ref):
        y, _, _ = _rms_fwd(h_ref[...], g_ref[...])
        hn = y.astype(MX)
        hn_ref[...] = hn
        z_ref[...] = jnp.dot(hn, w_ref[...], preferred_element_type=F32)

    tile = lambda w: pl.BlockSpec((tm, w), lambda i: (i, 0))
    (z, hn), _ = _call(
        body, name, (t // tm,),
        [tile(D_MODEL), _const_spec((1, D_MODEL)), _const_spec((D_MODEL, IN_W))],
        [tile(IN_W), tile(D_MODEL)], [_sds((t, IN_W), F32), _sds((t, D_MODEL), MX)], [], [h, g1, w_in])
    return z, hn


def _band2(kb, g):
    lo = lax.broadcasted_iota(jnp.int32, kb.shape, 1) < HEAD_DIM
    kr = pltpu.roll(kb, HEAD_DIM, 1)
    if g == 0:
        top, bot = jnp.where(lo, kb, 0.0), jnp.where(lo, 0.0, kr)
    else:
        top, bot = jnp.where(lo, kr, 0.0), jnp.where(lo, 0.0, kb)
    return jnp.concatenate([top, bot], axis=0)


def _attn_operands(z_ref, zh_ref, b):
    rows = slice(b * BLK, (b + 1) * BLK)
    prev = zh_ref if b == 0 else z_ref
    prow = slice(0, BLK) if b == 0 else slice((b - 1) * BLK, b * BLK)
    kb = jnp.concatenate([prev[prow, K0:K0 + KV_W], z_ref[rows, K0:K0 + KV_W]], axis=0)
    vb = jnp.concatenate([prev[prow, V0:V0 + KV_W], z_ref[rows, V0:V0 + KV_W]], axis=0)
    k2 = [_band2(kb, g) for g in range(2)]
    v2 = [_band2(vb, g) for g in range(2)]
    q2 = [jnp.concatenate([z_ref[rows, (2 * g) * BLK:(2 * g + 1) * BLK], z_ref[rows, (2 * g + 1) * BLK:(2 * g + 2) * BLK]],
                          axis=0) for g in range(2)]
    return q2, k2, v2


def _attn_block(z_ref, zh_ref, sink_ref, b, first):
    q2, k2, v2 = _attn_operands(z_ref, zh_ref, b)
    rr = lax.broadcasted_iota(jnp.int32, (4 * BLK, 2 * BLK), 0) & (BLK - 1)
    cc = lax.broadcasted_iota(jnp.int32, (4 * BLK, 2 * BLK), 1)
    first_block = jnp.logical_and(first, b == 0).astype(jnp.int32)
    mask = jnp.logical_and(jnp.logical_and(cc > rr, cc <= rr + BLK), cc >= BLK * first_block)
    s = jnp.concatenate([_dot_nt(q2[g], k2[g]) for g in range(2)], axis=0) * SCALE
    w = 2 * BLK
    out, psink = [], []
    for hh in range(2):
        sh = jnp.where(mask, s[:, hh * w:(hh + 1) * w], MASK_VALUE)
        sk = jnp.concatenate([jnp.broadcast_to(sink_ref[p:p + 1, hh * w:hh * w + 1], (BLK, 1)) for p in range(4)], axis=0)
        m = jnp.maximum(jnp.max(sh, axis=1, keepdims=True), sk)
        p = jnp.exp(sh - m)
        es = jnp.exp(sk - m)
        inv = 1.0 / (jnp.sum(p, axis=1, keepdims=True) + es)
        out.append(p * inv)
        psink.append(es * inv)
    return v2, jnp.concatenate(out, axis=1), psink


def _scan_steps(a, b, n, span, reverse):
    pos = lax.broadcasted_iota(jnp.int32, a.shape, 0) & (span - 1)
    d = 1
    while d < span:
        keep = pos < span - d if reverse else pos >= d
        shift = n - d if reverse else d
        a_sh = jnp.where(keep, pltpu.roll(a, shift, 0), 1.0)
        b_sh = jnp.where(keep, pltpu.roll(b, shift, 0), 0.0)
        b = a * b_sh + b
        a = a * a_sh
        d *= 2
    return a, b


def _scan(a, b, tm, reverse):
    return _scan_steps(a, b, tm, tm, reverse)


def _shifted_copies(ext, shifts, tm):
    rows = tm + CONV_HALO - 8
    for r in range(1, 8):
        shifts[r - 1, 0:rows, :] = ext[pl.ds(r, rows), :]


def _tap(ext, shifts, off, r0, n):
    a, r = divmod(off, 8)
    lo = 8 * a + r0
    if r == 0:
        return ext[lo:lo + n, :]
    return shifts[r - 1, lo:lo + n, :]


def _glu_fill(z_ref, zh_ref, uext, ush, first, tm, sv_ref):
    cv = z_ref[:, CV0:CV0 + CONV_W]
    sg = _sigmoid(z_ref[:, CG0:CG0 + CONV_W])
    u = cv * sg
    sv_ref[:, SV_SG:SV_SG + CONV_W] = sg
    sv_ref[:, SV_U:SV_U + CONV_W] = u
    hrow = BLK - CONV_HALO
    uh = zh_ref[hrow:BLK, CV0:CV0 + CONV_W] * _sigmoid(zh_ref[hrow:BLK, CG0:CG0 + CONV_W])
    uext[0:CONV_HALO, :] = jnp.where(first, 0.0, uh)
    uext[CONV_HALO:CONV_HALO + tm, :] = u
    _shifted_copies(uext, ush, tm)


def _conv_taps(cw_ref, pv_ref, uext, ush, out_ref, tm):
    for r0 in range(0, tm, CONV_CHUNK):
        acc = jnp.broadcast_to(pv_ref[R_CONV_B:R_CONV_B + 1, :], (CONV_CHUNK, CONV_W))
        for k in range(CONV_K):
            acc = acc + cw_ref[k:k + 1, :] * _tap(uext, ush, CONV_HALO - (CONV_K - 1) + k, r0, CONV_CHUNK)
        out_ref[r0:r0 + CONV_CHUNK, :] = acc


def _ln_silu(uc, pv_ref):
    mu = jnp.mean(uc, axis=-1, keepdims=True)
    xc = uc - mu
    rs = lax.rsqrt(jnp.mean(xc * xc, axis=-1, keepdims=True) + LN_EPS)
    xh = xc * rs
    ln = xh * pv_ref[R_LN_G:R_LN_G + 1, :] + pv_ref[R_LN_B:R_LN_B + 1, :]
    sg = _sigmoid(ln)
    return xh, rs, ln, sg


def _rx_fill(z_ref, zh_ref, rxext, first, tm):
    rxext[0:LRU_HALO, :] = jnp.where(first, 0.0, zh_ref[BLK - LRU_HALO:BLK, RX0:RX0 + LRU_W])
    rxext[LRU_HALO:LRU_HALO + tm, :] = z_ref[:, RX0:RX0 + LRU_W]


def _lru_gates(z_ref, zh_ref, pv_ref, wa_ref, wx_ref, rxext, first, tm):
    _rx_fill(z_ref, zh_ref, rxext, first, tm)
    xc = jnp.broadcast_to(pv_ref[R_LCONV_B:R_LCONV_B + 1, :], (tm, LRU_W))
    for k in range(LRU_K):
        xc = xc + pv_ref[R_LCW + k:R_LCW + k + 1, :] * rxext[pl.ds(LRU_HALO - (LRU_K - 1) + k, tm), :]
    r = _sigmoid(_dot(xc, wa_ref[...]) + pv_ref[R_BA:R_BA + 1, :])
    ig = _sigmoid(_dot(xc, wx_ref[...]) + pv_ref[R_BX:R_BX + 1, :])
    lam = pv_ref[R_LAM:R_LAM + 1, :]
    sp = jnp.log1p(jnp.exp(-lam))
    la = (-LRU_C * r) * sp
    a = jnp.exp(la)
    mult = jnp.sqrt(_neg_expm1(2.0 * la))
    return xc, r, ig, sp, la, a, mult


def _mixer_in_specs(tm, tile_of):
    hb = tm // BLK
    return [
        pl.BlockSpec((tm, IN_W), lambda i: (tile_of(i), 0)),
        pl.BlockSpec((BLK, IN_W), lambda i: (jnp.maximum(tile_of(i) * hb - 1, 0), 0)),
        _const_spec((8, 4 * BLK)),
        _const_spec((32, CONV_W)),
        _const_spec((16, CONV_W)),
        _const_spec((LRU_W, LRU_W)),
        _const_spec((LRU_W, LRU_W)),
    ]


def _mixer_fwd(z, sink, cw, pv, wa, wx, name, riders=()):
    t = z.shape[0]
    tm = _tile(t)
    nb = tm // BLK

    def body(z_ref, zh_ref, sink_ref, cw_ref, pv_ref, wa_ref, wx_ref, y_ref, hl_ref, uc_ref, p_ref, ps_ref, sv_ref,
             uext, ush, rxext, hcar):
        i = pl.program_id(0)
        first = i == 0

        @pl.when(first)
        def _():
            hcar[...] = jnp.zeros_like(hcar)

        lo = lax.broadcasted_iota(jnp.int32, (4 * BLK, BLK), 1) < HEAD_DIM
        for b in range(nb):
            rows = slice(b * BLK, (b + 1) * BLK)
            v2, prob, psink = _attn_block(z_ref, zh_ref, sink_ref, b, first)
            prob = prob.astype(MX)
            p_ref[b] = prob
            ps_ref[b] = jnp.where(lo, psink[0], psink[1])
            for g in range(2):
                o = _dot(prob[2 * g * BLK:(2 * g + 2) * BLK], v2[g])
                y_ref[rows, (2 * g) * BLK:(2 * g + 1) * BLK] = o[0:BLK]
                y_ref[rows, (2 * g + 1) * BLK:(2 * g + 2) * BLK] = o[BLK:2 * BLK]
        _glu_fill(z_ref, zh_ref, uext, ush, first, tm, sv_ref)
        _conv_taps(cw_ref, pv_ref, uext, ush, uc_ref, tm)
        _, _, ln, sg = _ln_silu(uc_ref[...], pv_ref)
        y_ref[:, ATTN_W:ATTN_W + CONV_W] = ln * sg
        xc, r, ig, _, _, a, mult = _lru_gates(z_ref, zh_ref, pv_ref, wa_ref, wx_ref, rxext, first, tm)
        for col, val in ((SV_XC, xc), (SV_R, r), (SV_IG, ig), (SV_A, a), (SV_MULT, mult)):
            sv_ref[:, col:col + LRU_W] = val
        acum, h = _scan(a, mult * (ig * xc), tm, reverse=False)
        h = h + acum * hcar[0:1, :]
        hl_ref[...] = h
        hcar[0:1, :] = h[tm - 1:tm, :]
        gl, _ = _gelu(z_ref[:, RG0:RG0 + LRU_W])
        y_ref[:, ATTN_W + CONV_W:ATTN_W + CONV_W + LRU_W] = h * gl

    tile = lambda w: pl.BlockSpec((tm, w), lambda i: (i, 0))
    return _call(
        body, name, (t // tm,), _mixer_in_specs(tm, lambda i: i),
        [tile(D_MODEL), tile(LRU_W), tile(CONV_W), pl.BlockSpec((nb, 4 * BLK, 4 * BLK), lambda i: (i, 0, 0)),
         pl.BlockSpec((nb, 4 * BLK, BLK), lambda i: (i, 0, 0)), tile(SV_W)],
        [_sds((t, D_MODEL), F32), _sds((t, LRU_W), F32), _sds((t, CONV_W), F32),
         _sds((t // BLK, 4 * BLK, 4 * BLK), MX), _sds((t // BLK, 4 * BLK, BLK), F32), _sds((t, SV_W), F32)],
        [pltpu.VMEM((tm + CONV_HALO, CONV_W), F32), pltpu.VMEM((7, tm + CONV_HALO - 8, CONV_W), F32),
         pltpu.VMEM((tm + LRU_HALO, LRU_W), F32), pltpu.VMEM((8, LRU_W), F32)],
        [z, z, sink, cw, pv, wa, wx], riders)


def _mixer_bwd(dy, z, ycat, hl, uc, probs, psinks, kept, sink, cw, pv, wa, wx, name, riders=()):
    t = z.shape[0]
    tm = _tile(t)
    nt = t // tm
    nb = tm // BLK
    rev = lambda i: nt - 1 - i

    def body(dy_ref, z_ref, zh_ref, sink_ref, cw_ref, pv_ref, wa_ref, wx_ref, y_ref, hl_ref, hlh_ref, uc_ref,
             p_ref, ps_ref, sv_ref, svh_ref, dz_ref, dsink_ref, dcw_ref, dpv_ref, dwa_ref, dwx_ref,
             uext, ush, rxext, dkext, dvext, ducext, dsh, dcw8, dxcext, kcar, vcar, uccar, xccar, gcar):
        i = pl.program_id(0)
        first = i == nt - 1

        @pl.when(i == 0)
        def _():
            for car in (kcar, vcar, uccar, xccar, gcar, dcw8):
                car[...] = jnp.zeros_like(car)
            for acc in (dsink_ref, dpv_ref, dwa_ref, dwx_ref):
                acc[...] = jnp.zeros_like(acc)

        def addrow(r, val):
            dpv_ref[r:r + 1, :] += jnp.sum(val, axis=0, keepdims=True)

        dkext[:, 0:tm] = jnp.zeros((KV_W, tm), F32)
        dvext[:, 0:tm] = jnp.zeros((KV_W, tm), F32)
        dkext[:, tm:tm + BLK] = kcar[...]
        dvext[:, tm:tm + BLK] = vcar[...]
        lane512 = lax.broadcasted_iota(jnp.int32, (1, 4 * BLK), 1) < 2 * BLK
        lo = lax.broadcasted_iota(jnp.int32, (4 * BLK, BLK), 1) < HEAD_DIM
        hd, w2 = HEAD_DIM, 2 * BLK
        for b in range(nb):
            rows = slice(b * BLK, (b + 1) * BLK)
            band = slice(b * BLK, (b + 2) * BLK)
            q2, k2, v2 = _attn_operands(z_ref, zh_ref, b)
            prob = p_ref[b]
            psink = [ps_ref[b, :, 0:1], ps_ref[b, :, HEAD_DIM:HEAD_DIM + 1]]
            stack = lambda ref: jnp.concatenate([ref[rows, p * BLK:(p + 1) * BLK] for p in range(4)], axis=0)
            do4 = stack(dy_ref)
            dlt = do4 * stack(y_ref)
            d0 = jnp.sum(jnp.where(lo, dlt, 0.0), axis=1, keepdims=True)
            d1 = jnp.sum(jnp.where(lo, 0.0, dlt), axis=1, keepdims=True)
            dp = jnp.concatenate([_dot_nt(do4[g * w2:(g + 1) * w2], v2[g]) for g in range(2)], axis=0)
            dl = jnp.concatenate([jnp.broadcast_to(d0, (4 * BLK, w2)), jnp.broadcast_to(d1, (4 * BLK, w2))], axis=1)
            draw = (prob * (dp - dl)) * SCALE
            e0, e1 = psink[0] * d0, psink[1] * d1
            for p in range(4):
                prs = slice(p * BLK, (p + 1) * BLK)
                s0 = jnp.sum(e0[prs], axis=0, keepdims=True)
                s1 = jnp.sum(e1[prs], axis=0, keepdims=True)
                dsink_ref[p:p + 1, :] += -jnp.where(lane512, s0, s1)
            for g in range(2):
                grs = slice(g * w2, (g + 1) * w2)
                dq = _dot(draw[grs], k2[g])
                dz_ref[rows, (2 * g) * BLK:(2 * g + 1) * BLK] = dq[0:BLK].astype(dz_ref.dtype)
                dz_ref[rows, (2 * g + 1) * BLK:(2 * g + 2) * BLK] = dq[BLK:2 * BLK].astype(dz_ref.dtype)
                tk = _dot_tn(q2[g], draw[grs])
                tv = _dot_tn(do4[grs], prob[grs])
                dkext[g * hd:(g + 1) * hd, band] += tk[0:hd, 0:w2] + tk[hd:2 * hd, w2:2 * w2]
                dvext[g * hd:(g + 1) * hd, band] += tv[0:hd, 0:w2] + tv[hd:2 * hd, w2:2 * w2]
        dz_ref[:, K0:K0 + KV_W] = jnp.transpose(dkext[:, BLK:BLK + tm]).astype(dz_ref.dtype)
        dz_ref[:, V0:V0 + KV_W] = jnp.transpose(dvext[:, BLK:BLK + tm]).astype(dz_ref.dtype)
        kcar[...] = dkext[:, 0:BLK]
        vcar[...] = dvext[:, 0:BLK]

        uext[0:CONV_HALO, :] = jnp.where(first, 0.0, svh_ref[...])
        uext[CONV_HALO:CONV_HALO + tm, :] = sv_ref[:, SV_U:SV_U + CONV_W]
        _shifted_copies(uext, ush, tm)
        xh, rs, ln, sg = _ln_silu(uc_ref[...], pv_ref)
        dln = dy_ref[:, ATTN_W:ATTN_W + CONV_W] * (sg * (1.0 + ln * (1.0 - sg)))
        addrow(R_LN_G, dln * xh)
        addrow(R_LN_B, dln)
        dxh = dln * pv_ref[R_LN_G:R_LN_G + 1, :]
        duc = rs * (dxh - jnp.mean(dxh, axis=-1, keepdims=True) - xh * jnp.mean(dxh * xh, axis=-1, keepdims=True))
        addrow(R_CONV_B, duc)
        ducext[0:tm, :] = duc
        ducext[tm:tm + CONV_HALO, :] = uccar[...]
        uccar[...] = duc[0:CONV_HALO, :]
        _shifted_copies(ducext, dsh, tm)
        for r0 in range(0, tm, CONV_CHUNK):
            crow = slice(r0, r0 + CONV_CHUNK)
            duc_c = ducext[crow, :]
            du = jnp.zeros((CONV_CHUNK, CONV_W), F32)
            for k in range(CONV_K):
                prod = duc_c * _tap(uext, ush, CONV_HALO - (CONV_K - 1) + k, r0, CONV_CHUNK)
                part = prod[0:8]
                for s in range(8, CONV_CHUNK, 8):
                    part = part + prod[s:s + 8]
                dcw8[k] += part
                du = du + cw_ref[k:k + 1, :] * _tap(ducext, dsh, CONV_K - 1 - k, r0, CONV_CHUNK)
            sgc = sv_ref[crow, SV_SG:SV_SG + CONV_W]
            dz_ref[crow, CV0:CV0 + CONV_W] = (du * sgc).astype(dz_ref.dtype)
            u_c = uext[CONV_HALO + r0:CONV_HALO + r0 + CONV_CHUNK, :]
            dz_ref[crow, CG0:CG0 + CONV_W] = (du * u_c * (1.0 - sgc)).astype(dz_ref.dtype)

        @pl.when(i == nt - 1)
        def _():
            dcw_ref[...] = jnp.sum(dcw8[...], axis=1)

        _rx_fill(z_ref, zh_ref, rxext, first, tm)
        xc, r, ig, a, mult = (sv_ref[:, col:col + LRU_W] for col in (SV_XC, SV_R, SV_IG, SV_A, SV_MULT))
        sp = jnp.log1p(jnp.exp(-pv_ref[R_LAM:R_LAM + 1, :]))
        h = hl_ref[...]
        rowi = lax.broadcasted_iota(jnp.int32, (tm, LRU_W), 0)
        hlast = jnp.where(first, 0.0, hlh_ref[7:8, :])
        hprev = jnp.where(rowi == 0, hlast, pltpu.roll(h, 1, 0))
        dyl = dy_ref[:, ATTN_W + CONV_W:ATTN_W + CONV_W + LRU_W]
        gl, dgl = _gelu(z_ref[:, RG0:RG0 + LRU_W])
        dz_ref[:, RG0:RG0 + LRU_W] = (dyl * h * dgl).astype(dz_ref.dtype)
        dh = dyl * gl + jnp.where(rowi == tm - 1, gcar[0:1, :], 0.0)
        c = jnp.where(rowi == tm - 1, 0.0, pltpu.roll(a, tm - 1, 0))
        _, gg = _scan(c, dh, tm, reverse=True)
        gcar[0:1, :] = a[0:1, :] * gg[0:1, :]
        dmult = gg * (ig * xc)
        dig = gg * mult * xc
        dxc = gg * mult * ig
        dla = gg * hprev * a - dmult * a * a / mult
        dr = dla * (-LRU_C * sp)
        lam = pv_ref[R_LAM:R_LAM + 1, :]
        dpv_ref[R_LAM:R_LAM + 1, :] += jnp.sum(dla * (-LRU_C * r), axis=0, keepdims=True) * (-_sigmoid(-lam))
        dpa = dr * r * (1.0 - r)
        dpx = dig * ig * (1.0 - ig)
        addrow(R_BA, dpa)
        addrow(R_BX, dpx)
        dxc = dxc + _dot_nt(dpa, wa_ref[...]) + _dot_nt(dpx, wx_ref[...])
        dwa_ref[...] += _dot_tn(xc, dpa)
        dwx_ref[...] += _dot_tn(xc, dpx)
        addrow(R_LCONV_B, dxc)
        dxcext[0:tm, :] = dxc
        dxcext[tm:tm + LRU_HALO, :] = xccar[...]
        xccar[...] = dxc[0:LRU_HALO, :]
        drx = jnp.zeros((tm, LRU_W), F32)
        for k in range(LRU_K):
            addrow(R_LCW + k, dxc * rxext[pl.ds(LRU_HALO - (LRU_K - 1) + k, tm), :])
            drx = drx + pv_ref[R_LCW + k:R_LCW + k + 1, :] * dxcext[pl.ds(LRU_K - 1 - k, tm), :]
        dz_ref[:, RX0:RX0 + LRU_W] = drx.astype(dz_ref.dtype)

    tile = lambda w: pl.BlockSpec((tm, w), lambda i: (rev(i), 0))
    in_specs = [tile(D_MODEL)] + _mixer_in_specs(tm, rev) + [
        tile(D_MODEL), tile(LRU_W),
        pl.BlockSpec((8, LRU_W), lambda i: (jnp.maximum(rev(i) * (tm // 8) - 1, 0), 0)),
        tile(CONV_W), pl.BlockSpec((nb, 4 * BLK, 4 * BLK), lambda i: (rev(i), 0, 0)),
        pl.BlockSpec((nb, 4 * BLK, BLK), lambda i: (rev(i), 0, 0)), tile(SV_W),
        pl.BlockSpec((CONV_HALO, CONV_W), lambda i: (jnp.maximum(rev(i) * (tm // CONV_HALO) - 1, 0), 0))]
    return _call(
        body, name, (nt,), in_specs,
        [tile(IN_W), _acc_spec((8, 4 * BLK)), _acc_spec((32, CONV_W)), _acc_spec((16, CONV_W)),
         _acc_spec((LRU_W, LRU_W)), _acc_spec((LRU_W, LRU_W))],
        [_sds((t, IN_W), MX), _sds((8, 4 * BLK), F32), _sds((32, CONV_W), F32), _sds((16, CONV_W), F32),
         _sds((LRU_W, LRU_W), F32), _sds((LRU_W, LRU_W), F32)],
        [pltpu.VMEM((tm + CONV_HALO, CONV_W), F32), pltpu.VMEM((7, tm + CONV_HALO - 8, CONV_W), F32),
         pltpu.VMEM((tm + LRU_HALO, LRU_W), F32),
         pltpu.VMEM((KV_W, tm + BLK), F32), pltpu.VMEM((KV_W, tm + BLK), F32),
         pltpu.VMEM((tm + CONV_HALO, CONV_W), F32), pltpu.VMEM((7, tm + CONV_HALO - 8, CONV_W), F32),
         pltpu.VMEM((32, 8, CONV_W), F32), pltpu.VMEM((tm + LRU_HALO, LRU_W), F32),
         pltpu.VMEM((KV_W, BLK), F32), pltpu.VMEM((KV_W, BLK), F32),
         pltpu.VMEM((CONV_HALO, CONV_W), F32), pltpu.VMEM((LRU_HALO, LRU_W), F32), pltpu.VMEM((8, LRU_W), F32)],
        [dy, z, z, sink, cw, pv, wa, wx, ycat, hl, hl, uc, probs, psinks, kept, kept], riders)


def _post_fwd(ycat, h0, gmix, w_out, g2, w_up, w_down, name, riders=()):
    t = h0.shape[0]
    tm = _tile(t, POST_TILE)
    nj = D_FF // FF_BLK

    def body(y_ref, h_ref, gm_ref, wo_ref, g2_ref, wu_ref, wd_ref, h1_ref, a_ref, h2_ref, ym_ref, hn_ref):
        ym, _, _ = _group_rms_fwd(y_ref[...], gm_ref[...])
        ym = ym.astype(MX)
        ym_ref[...] = ym
        h1 = h_ref[...] + jnp.dot(ym, wo_ref[...], preferred_element_type=F32)
        h1_ref[...] = h1
        hn, _, _ = _rms_fwd(h1, g2_ref[...])
        hn = hn.astype(MX)
        hn_ref[...] = hn
        for j in range(nj):
            u = jnp.dot(hn, wu_ref[j], preferred_element_type=F32)
            a_ref[:, j * FF_BLK:(j + 1) * FF_BLK] = jnp.square(jnp.maximum(u, 0.0)).astype(MX)
        h2_ref[...] = h1 + jnp.dot(a_ref[...], wd_ref[...], preferred_element_type=F32)

    tile = lambda w: pl.BlockSpec((tm, w), lambda i: (i, 0))
    return _call(
        body, name, (t // tm,),
        [tile(D_MODEL), tile(D_MODEL), _const_spec((1, D_MODEL)), _const_spec((D_MODEL, D_MODEL)),
         _const_spec((1, D_MODEL)), _const_spec((nj, D_MODEL, FF_BLK)), _const_spec((D_FF, D_MODEL))],
        [tile(D_MODEL), tile(D_FF), tile(D_MODEL), tile(D_MODEL), tile(D_MODEL)],
        [_sds((t, D_MODEL), F32), _sds((t, D_FF), MX), _sds((t, D_MODEL), F32), _sds((t, D_MODEL), MX),
         _sds((t, D_MODEL), MX)],
        [], [ycat, h0, gmix, w_out, g2, w_up, w_down], riders)


def _ffn_bwd(dh2, act, h1, g2, w_up_t, w_down, name, riders=()):
    t = h1.shape[0]
    tm = _tile(t, POST_TILE)
    nj = D_FF // FF_BLK

    def body(dh2_ref, a_ref, h1_ref, g2_ref, wut_ref, wd_ref, dh1_ref, dh1b_ref, dh2b_ref, du_ref, dg2_ref):
        @pl.when(pl.program_id(0) == 0)
        def _():
            dg2_ref[...] = jnp.zeros_like(dg2_ref)

        dh2 = dh2_ref[...]
        dh2b = dh2.astype(MX)
        dh2b_ref[...] = dh2b
        for j in range(nj):
            cols = slice(j * FF_BLK, (j + 1) * FF_BLK)
            da = _dot_nt(dh2b, wd_ref[j])
            du_ref[:, cols] = (da * (2.0 * jnp.sqrt(a_ref[:, cols].astype(F32)))).astype(MX)
        dhn = jnp.dot(du_ref[...], wut_ref[...], preferred_element_type=F32)
        _, xh, r = _rms_fwd(h1_ref[...], g2_ref[...])
        dx, dg = _rms_bwd(dhn, xh, r, g2_ref[...])
        dg2_ref[...] += dg
        dh1 = dh2 + dx
        dh1_ref[...] = dh1
        dh1b_ref[...] = dh1.astype(MX)

    tile = lambda w: pl.BlockSpec((tm, w), lambda i: (i, 0))
    return _call(
        body, name, (t // tm,),
        [tile(D_MODEL), tile(D_FF), tile(D_MODEL), _const_spec((1, D_MODEL)),
         _const_spec((D_FF, D_MODEL)), _const_spec((nj, FF_BLK, D_MODEL))],
        [tile(D_MODEL), tile(D_MODEL), tile(D_MODEL), tile(D_FF), _acc_spec((1, D_MODEL))],
        [_sds((t, D_MODEL), F32), _sds((t, D_MODEL), MX), _sds((t, D_MODEL), MX), _sds((t, D_FF), MX),
         _sds((1, D_MODEL), F32)],
        [], [dh2, act, h1, g2, w_up_t, w_down], riders)


def _mix_bwd(dh1, ycat, gmix, w_out, name):
    t = dh1.shape[0]
    tm = _tile(t)

    def body(dh1_ref, y_ref, gm_ref, wo_ref, dy_ref, dgm_ref):
        @pl.when(pl.program_id(0) == 0)
        def _():
            dgm_ref[...] = jnp.zeros_like(dgm_ref)

        dym = _dot_nt(dh1_ref[...], wo_ref[...])
        gm = gm_ref[...]
        _, yh, rr = _group_rms_fwd(y_ref[...], gm)
        outs, dgs = [], []
        for (a, b), rg in zip(_GROUPS, rr):
            dxg, dgg = _rms_bwd(dym[:, a:b], yh[:, a:b], rg, gm[:, a:b])
            outs.append(dxg)
            dgs.append(dgg)
        dy_ref[...] = jnp.concatenate(outs, axis=1)
        dgm_ref[...] += jnp.concatenate(dgs, axis=1)

    tile = pl.BlockSpec((tm, D_MODEL), lambda i: (i, 0))
    (dy, dgm), _ = _call(
        body, name, (t // tm,), [tile, tile, _const_spec((1, D_MODEL)), _const_spec((D_MODEL, D_MODEL))],
        [tile, _acc_spec((1, D_MODEL))], [_sds((t, D_MODEL), F32), _sds((1, D_MODEL), F32)],
        [], [dh1, ycat, gmix, w_out])
    return dy, dgm


def _in_bwd(dz, h0, dh1, g1, w_in, after, name):
    t = h0.shape[0]
    tm = _tile(t)

    def body(dz_ref, h_ref, dh1_ref, g_ref, w_ref, after_ref, dh0_ref, dg_ref):
        @pl.when(pl.program_id(0) == 0)
        def _():
            dg_ref[...] = jnp.zeros_like(dg_ref)

        dhn = _dot_nt(dz_ref[...], w_ref[...])
        _, xh, r = _rms_fwd(h_ref[...], g_ref[...])
        dx, dg = _rms_bwd(dhn, xh, r, g_ref[...])
        dg_ref[...] += dg
        dh0_ref[...] = dh1_ref[...] + dx

    tile = lambda w: pl.BlockSpec((tm, w), lambda i: (i, 0))
    (dh0, dg), _ = _call(
        body, name, (t // tm,),
        [tile(IN_W), tile(D_MODEL), tile(D_MODEL), _const_spec((1, D_MODEL)), _const_spec((D_MODEL, IN_W)),
         _const_spec((8, 128))],
        [tile(D_MODEL), _acc_spec((1, D_MODEL))], [_sds((t, D_MODEL), F32), _sds((1, D_MODEL), F32)],
        [], [dz, h0, dh1, g1, w_in, after])
    return dh0, dg


def _loss_head(h, gf, target, name):
    t = h.shape[0]
    tm = _tile(t)

    def body(h_ref, g_ref, t_ref, dh_ref, loss_ref, dg_ref):
        @pl.when(pl.program_id(0) == 0)
        def _():
            loss_ref[...] = jnp.zeros_like(loss_ref)
            dg_ref[...] = jnp.zeros_like(dg_ref)

        g = g_ref[...]
        y, xh, r = _rms_fwd(h_ref[...], g)
        err = y - t_ref[...]
        part = 0.5 * jnp.sum(jnp.mean(err * err, axis=-1, keepdims=True), axis=0, keepdims=True)
        loss_ref[...] += jnp.broadcast_to(part, loss_ref.shape)
        dx, dg = _rms_bwd(err * (1.0 / D_MODEL), xh, r, g)
        dg_ref[...] += dg
        dh_ref[...] = dx

    tile = pl.BlockSpec((tm, D_MODEL), lambda i: (i, 0))
    (dh, loss, dg), _ = _call(
        body, name, (t // tm,), [tile, _const_spec((1, D_MODEL)), tile],
        [tile, _acc_spec((1, 128)), _acc_spec((1, D_MODEL))],
        [_sds((t, D_MODEL), F32), _sds((1, 128), F32), _sds((1, D_MODEL), F32)], [], [h, gf, target])
    return dh, loss, dg


def _dw(x, y, name, split, bm, bn):
    t, m = x.shape
    n = y.shape[1]
    tk = _tile(t, DW_TILE)
    nk = t // tk
    if split == "rows":
        assert bn == n
        r, c = m // N_DEV, n
        per = bm // r
        out_block = pl.BlockSpec((per, r, c), lambda a, b, k: (a, 0, 0))
    else:
        assert bm == m
        r, c = m, n // N_DEV
        per = bn // c
        out_block = pl.BlockSpec((per, r, c), lambda a, b, k: (b, 0, 0))

    def body(x_ref, y_ref, o_ref, o16_ref, acc):
        k = pl.program_id(2)

        @pl.when(k == 0)
        def _():
            acc[...] = jnp.zeros_like(acc)

        acc[...] += _dot_tn(x_ref[...], y_ref[...])

        @pl.when(k == nk - 1)
        def _():
            for d in range(per):
                v = acc[d * r:(d + 1) * r, :] if split == "rows" else acc[:, d * c:(d + 1) * c]
                o_ref[d] = v
                o16_ref[d] = v.astype(o16_ref.dtype)

    return pl.pallas_call(
        body, name=name, grid=(m // bm, n // bn, nk),
        in_specs=[pl.BlockSpec((tk, bm), lambda a, b, k: (k, a)), pl.BlockSpec((tk, bn), lambda a, b, k: (k, b))],
        out_specs=[out_block, out_block],
        out_shape=[_sds((N_DEV, r, c), F32), _sds((N_DEV, r, c), WIRE)],
        scratch_shapes=[pltpu.VMEM((bm, bn), F32)],
        compiler_params=pltpu.CompilerParams(dimension_semantics=("arbitrary",) * 3, vmem_limit_bytes=VMEM_LIMIT),
    )(x, y)


def _adamw_math(w, g, m, v):
    m = ADAM_B1 * m + (1.0 - ADAM_B1) * g
    v = ADAM_B2 * v + (1.0 - ADAM_B2) * jnp.square(g)
    m_hat = m / (1.0 - ADAM_B1 ** ADAM_STEP)
    v_hat = v / (1.0 - ADAM_B2 ** ADAM_STEP)
    delta = -ADAM_LR * (m_hat / (jnp.sqrt(v_hat) + ADAM_EPS) + ADAM_WD * w)
    return delta, m, v


def _adamw_shard(g_own, g_recv, dev, w, m, v, after, name):
    _, r, c = w.shape
    br = r
    for cand in (256, 128, 112, 64, 56, 32, 16, 8):
        if r % cand == 0:
            br = cand
            break
    nr = r // br
    own = lambda l: pl.BlockSpec((1, br, c), lambda ll, i, d: (d[0], jnp.where(ll == l, i, (nr - 1) * (1 - l)), 0))
    recv = lambda l: pl.BlockSpec((N_DEV - 1, br, c), lambda ll, i, d: (0, jnp.where(ll == l, i, (nr - 1) * (1 - l)), 0))

    def body(dev_ref, go0, gr0, go1, gr1, w_ref, m_ref, v_ref, after_ref, g_out, d_out, m_out, v_out):
        def update(go_ref, gr_ref):
            g = go_ref[0]
            for j in range(N_DEV - 1):
                g = g + gr_ref[j].astype(F32)
            delta, mn, vn = _adamw_math(w_ref[0], g, m_ref[0], v_ref[0])
            g_out[0] = g
            d_out[0] = delta
            m_out[0] = mn
            v_out[0] = vn

        layer = pl.program_id(0)
        pl.when(layer == 0)(lambda: update(go0, gr0))
        pl.when(layer == 1)(lambda: update(go1, gr1))

    tile = pl.BlockSpec((1, br, c), lambda ll, i, d: (ll, i, 0))
    return pl.pallas_call(
        body, name=name,
        grid_spec=pltpu.PrefetchScalarGridSpec(
            num_scalar_prefetch=1, grid=(2, nr),
            in_specs=[own(0), recv(0), own(1), recv(1), tile, tile, tile,
                      pl.BlockSpec((8, 128), lambda ll, i, d: (0, 0))],
            out_specs=[tile, tile, tile, tile]),
        out_shape=[_sds((2, r, c), F32)] * 4,
        compiler_params=pltpu.CompilerParams(dimension_semantics=("arbitrary",) * 2, vmem_limit_bytes=VMEM_LIMIT),
    )(dev, g_own[0], g_recv[0], g_own[1], g_recv[1], w, m, v, after)


def _adamw_small(gs, ws, ms, vs, name):
    n = len(gs)

    def body(*refs):
        g_refs, w_refs, m_refs, v_refs = (refs[k * n:(k + 1) * n] for k in range(4))
        outs = refs[4 * n:]
        for k in range(n):
            delta, mn, vn = _adamw_math(w_refs[k][...], g_refs[k][...], m_refs[k][...], v_refs[k][...])
            outs[k][...] = delta
            outs[n + k][...] = mn
            outs[2 * n + k][...] = vn

    shapes = [_sds(w.shape, F32) for w in ws]
    res = pl.pallas_call(body, name=name, out_shape=shapes * 3,
                         compiler_params=pltpu.CompilerParams(vmem_limit_bytes=VMEM_LIMIT))(*gs, *ws, *ms, *vs)
    return res[:n], res[n:2 * n], res[2 * n:]


def _sum_parts(part, dev, name):
    def body(dev_ref, p_ref, o_ref):
        me = dev_ref[0]
        g = p_ref[me]
        for d in range(1, N_DEV):
            g = g + p_ref[jnp.bitwise_xor(me, d)]
        o_ref[...] = g

    full = pl.BlockSpec(part.shape, lambda i, d: (0, 0, 0))
    return pl.pallas_call(
        body, name=name,
        grid_spec=pltpu.PrefetchScalarGridSpec(
            num_scalar_prefetch=1, grid=(1,), in_specs=[full],
            out_specs=pl.BlockSpec(part.shape[1:], lambda i, d: (0, 0))),
        out_shape=_sds(part.shape[1:], F32))(dev, part)


HBM = pl.BlockSpec(memory_space=pltpu.HBM)
SEM = pl.BlockSpec(memory_space=pltpu.SEMAPHORE)
EFFECT = pltpu.SideEffectType.DATAFLOW_SIDE_EFFECTING


def _direct_copies(srcs, lands, ssem, rsem, scatter):
    x, y, c = _me()
    out = []
    for a in range(len(srcs)):
        for f in range(1, N_DEV):
            px = 1 - x if f & 4 else x
            py = 1 - y if f & 2 else y
            pc = 1 - c if f & 1 else c
            out.append(pltpu.make_async_remote_copy(
                src_ref=srcs[a].at[4 * px + 2 * py + pc] if scatter else srcs[a], dst_ref=lands[a].at[f - 1],
                send_sem=ssem.at[7 * a + f - 1], recv_sem=rsem.at[7 * a + f - 1],
                device_id=(px, py, pc), device_id_type=MESH))
    return out


def _send_start(arrays, scatter, name):
    arrays = list(arrays)
    n = len(arrays)
    lands = [lax.empty((N_DEV - 1,) + (a.shape[1:] if scatter else a.shape), a.dtype) for a in arrays]

    def body(*refs):
        srcs, lnds, ssem, rsem, token = refs[:n], refs[n:2 * n], refs[2 * n], refs[2 * n + 1], refs[-1]
        for cp in _direct_copies(srcs, lnds, ssem, rsem, scatter):
            cp.start()
        token[...] = jnp.zeros_like(token)

    hbm = lambda a: pltpu.HBM(a.shape, a.dtype)
    res = pl.pallas_call(
        body, name=name,
        out_shape=(pltpu.SemaphoreType.DMA((7 * n,)), pltpu.SemaphoreType.DMA((7 * n,)),
                   *[hbm(a) for a in arrays + lands], _sds((8, 128), F32)),
        in_specs=[HBM] * (2 * n),
        out_specs=(SEM, SEM, *[HBM] * (2 * n), pl.BlockSpec(memory_space=pltpu.VMEM)),
        input_output_aliases={i: 2 + i for i in range(2 * n)},
        compiler_params=pltpu.CompilerParams(has_side_effects=EFFECT),
    )(*[pltpu.with_memory_space_constraint(a, pltpu.HBM) for a in arrays + lands])
    return types.SimpleNamespace(ssem=res[0], rsem=res[1], srcs=list(res[2:2 + n]), lands=list(res[2 + n:2 + 2 * n]),
                                 token=res[-1], scatter=scatter)


def _send_wait(h, after, name):
    n = len(h.srcs)

    def body(*refs):
        srcs, lnds, ssem, rsem = refs[:n], refs[n:2 * n], refs[2 * n], refs[2 * n + 1]
        for cp in _direct_copies(srcs, lnds, ssem, rsem, h.scatter):
            cp.wait_send()
            cp.wait_recv()

    hbm = lambda a: pltpu.HBM(a.shape, a.dtype)
    res = pl.pallas_call(
        body, name=name,
        out_shape=tuple(hbm(a) for a in h.srcs + h.lands),
        in_specs=[HBM] * (2 * n) + [SEM, SEM, ANY], out_specs=[HBM] * (2 * n),
        input_output_aliases={i: i for i in range(2 * n)},
        compiler_params=pltpu.CompilerParams(has_side_effects=EFFECT),
    )(*h.srcs, *h.lands, h.ssem, h.rsem, after)
    return list(res[:n]), list(res[n:])


def _block_diag(w):
    out = jnp.zeros((LRU_W, LRU_W), w.dtype)
    for h in range(4):
        out = lax.dynamic_update_slice(out, w[h], (h * 64, h * 64))
    return out


def _unblock_diag(w):
    return jnp.concatenate([w[h * 64:(h + 1) * 64, h * 64:(h + 1) * 64] for h in range(4)], axis=0)


def _layer_params(p, l):
    row = lambda a: a[l].reshape(1, -1)
    sink_rows = jnp.repeat(p["attn_sinks"][l].reshape(4, 2), 2 * BLK, axis=1)
    sink_rows = jnp.concatenate([sink_rows, jnp.zeros((4, 4 * BLK), F32)], axis=0)
    cw = jnp.concatenate([p["conv_dw_w"][l], jnp.zeros((1, CONV_W), F32)], axis=0)
    pv = jnp.concatenate([
        row(p["conv_dw_b"]), row(p["conv_ln_g"]), row(p["conv_ln_b"]), row(p["lru_conv_b"]), row(p["lru_ba"]),
        row(p["lru_bx"]), row(p["lru_lambda"]), jnp.zeros((1, LRU_W), F32), p["lru_conv_w"][l],
        jnp.zeros((4, LRU_W), F32)], axis=0)
    return dict(
        g1=row(p["norm1"]), sink=sink_rows, cw=cw, pv=pv,
        wa=_block_diag(p["lru_wa"][l]).astype(MX), wx=_block_diag(p["lru_wx"][l]).astype(MX),
        gmix=row(p["mix_norm"]), g2=row(p["norm2"]))


_SMALL = ["norm1", "attn_sinks", "conv_dw_w", "conv_dw_b", "conv_ln_g", "conv_ln_b", "lru_conv_w", "lru_conv_b",
          "lru_wa", "lru_ba", "lru_wx", "lru_bx", "lru_lambda", "mix_norm", "norm2"]
_BIG = ["w_in", "w_out", "w_up", "w_down"]
_WEIGHTS = ["norm1", "w_in", "attn_sinks", "conv_dw_w", "conv_dw_b", "conv_ln_g", "conv_ln_b", "lru_conv_w",
            "lru_conv_b", "lru_wa", "lru_ba", "lru_wx", "lru_bx", "lru_lambda", "mix_norm", "w_out", "norm2", "w_up",
            "w_down", "final_norm"]


def kernel(x, norm1, w_in, attn_sinks, conv_dw_w, conv_dw_b, conv_ln_g, conv_ln_b, lru_conv_w, lru_conv_b, lru_wa, lru_ba, lru_wx, lru_bx, lru_lambda, mix_norm, w_out, norm2, w_up, w_down, final_norm, loss_target, m_norm1, m_w_in, m_attn_sinks, m_conv_dw_w, m_conv_dw_b, m_conv_ln_g, m_conv_ln_b, m_lru_conv_w, m_lru_conv_b, m_lru_wa, m_lru_ba, m_lru_wx, m_lru_bx, m_lru_lambda, m_mix_norm, m_w_out, m_norm2, m_w_up, m_w_down, m_final_norm, v_norm1, v_w_in, v_attn_sinks, v_conv_dw_w, v_conv_dw_b, v_conv_ln_g, v_conv_ln_b, v_lru_conv_w, v_lru_conv_b, v_lru_wa, v_lru_ba, v_lru_wx, v_lru_bx, v_lru_lambda, v_mix_norm, v_w_out, v_norm2, v_w_up, v_w_down, v_final_norm):
    w = dict(norm1=norm1, w_in=w_in, attn_sinks=attn_sinks, conv_dw_w=conv_dw_w, conv_dw_b=conv_dw_b,
             conv_ln_g=conv_ln_g, conv_ln_b=conv_ln_b, lru_conv_w=lru_conv_w, lru_conv_b=lru_conv_b, lru_wa=lru_wa,
             lru_ba=lru_ba, lru_wx=lru_wx, lru_bx=lru_bx, lru_lambda=lru_lambda, mix_norm=mix_norm, w_out=w_out,
             norm2=norm2, w_up=w_up, w_down=w_down, final_norm=final_norm)
    m = dict(norm1=m_norm1, w_in=m_w_in, attn_sinks=m_attn_sinks, conv_dw_w=m_conv_dw_w, conv_dw_b=m_conv_dw_b,
             conv_ln_g=m_conv_ln_g, conv_ln_b=m_conv_ln_b, lru_conv_w=m_lru_conv_w, lru_conv_b=m_lru_conv_b,
             lru_wa=m_lru_wa, lru_ba=m_lru_ba, lru_wx=m_lru_wx, lru_bx=m_lru_bx, lru_lambda=m_lru_lambda,
             mix_norm=m_mix_norm, w_out=m_w_out, norm2=m_norm2, w_up=m_w_up, w_down=m_w_down, final_norm=m_final_norm)
    v = dict(norm1=v_norm1, w_in=v_w_in, attn_sinks=v_attn_sinks, conv_dw_w=v_conv_dw_w, conv_dw_b=v_conv_dw_b,
             conv_ln_g=v_conv_ln_g, conv_ln_b=v_conv_ln_b, lru_conv_w=v_lru_conv_w, lru_conv_b=v_lru_conv_b,
             lru_wa=v_lru_wa, lru_ba=v_lru_ba, lru_wx=v_lru_wx, lru_bx=v_lru_bx, lru_lambda=v_lru_lambda,
             mix_norm=v_mix_norm, w_out=v_w_out, norm2=v_norm2, w_up=v_w_up, w_down=v_w_down, final_norm=v_final_norm)
    depth = w_in.shape[0]
    xi, yi, ci = _me()
    dev = (4 * xi + 2 * yi + ci).astype(jnp.int32)
    dev1 = dev.reshape(1)
    wb = {n: w[n].astype(MX) for n in _BIG}
    layer_shards = lambda l: [wb["w_out"][l], wb["w_up"][l], wb["w_down"][l]]

    _, ((g_in0, g_cw, g_lcw),) = _call(None, "gather_first", None, [], [], [], [], [],
                                        [_gather_rider([wb["w_in"][0], conv_dw_w, lru_conv_w])])
    cols = lambda g: jnp.moveaxis(g, 0, -2).reshape(g.shape[1:-1] + (N_DEV * g.shape[-1],))
    p = dict(w)
    p["conv_dw_w"] = cols(g_cw)
    p["lru_conv_w"] = cols(g_lcw)
    lp = [_layer_params(p, l) for l in range(depth)]

    gathered = [dict(w_in=cols(g_in0)), dict()]
    saved = []
    h = x[0]
    for l in range(depth):
        q, gw = lp[l], gathered[l]
        z, hn1 = _ln_in(h, q["g1"], gw["w_in"], f"ln_in{l}")
        riders = [_gather_rider(layer_shards(0))] if l == 0 else []
        (ycat, hl, uc, probs, psinks, kept), got = _mixer_fwd(z, q["sink"], q["cw"], q["pv"], q["wa"], q["wx"],
                                                              f"mixer_fwd{l}", riders)
        if l == 0:
            gw["w_out"], gw["w_up"], gw["w_down"] = got[0]
            gw["w_out"] = gw["w_out"].reshape(D_MODEL, D_MODEL)
        riders = [_gather_rider([wb["w_in"][1]] + layer_shards(1))] if l == 0 else []
        (h1, act, h2, ym, hn2), got = _post_fwd(ycat, h, q["gmix"], gw["w_out"], q["g2"], gw["w_up"],
                                                gw["w_down"].reshape(D_FF, D_MODEL), f"post_fwd{l}", riders)
        if l == 0:
            nxt = gathered[1]
            nxt["w_in"], nxt["w_out"], nxt["w_up"], nxt["w_down"] = got[0]
            nxt["w_in"] = cols(nxt["w_in"])
            nxt["w_out"] = nxt["w_out"].reshape(D_MODEL, D_MODEL)
        saved.append(dict(h0=h, z=z, hn1=hn1, ycat=ycat, hl=hl, uc=uc, probs=probs, psinks=psinks, kept=kept, h1=h1,
                          act=act, ym=ym, hn2=hn2))
        h = h2
    dh, loss, dgf = _loss_head(h, final_norm.reshape(1, -1), loss_target[0], "loss_head")

    grads = [None] * depth
    big = {n: [None] * depth for n in _BIG}
    pending = []

    def send_pending():
        riders = [_scatter_rider([item[3] for item in pending])] if pending else []
        return riders, list(pending)

    def record(sent, got):
        for item, recv in zip(sent, got[0] if sent else []):
            big[item[0]][item[1]] = (item[2], recv)
        del pending[:len(sent)]

    for l in reversed(range(depth)):
        q, s, gw = lp[l], saved[l], gathered[l]
        riders, sent = send_pending()
        w_up_t = jnp.swapaxes(gw["w_up"], 1, 2).reshape(D_FF, D_MODEL)
        (dh1, dh1b, dhb, du, dg2), got = _ffn_bwd(dh, s["act"], s["h1"], q["g2"], w_up_t, gw["w_down"],
                                                  f"ffn_bwd{l}", riders)
        record(sent, got)
        dycat, dgm = _mix_bwd(dh1b, s["ycat"], q["gmix"], gw["w_out"], f"mix_bwd{l}")
        pending.append(("w_down", l) + tuple(_dw(s["act"], dhb, f"dw_down{l}", "rows", 2048, D_MODEL)))
        pending.append(("w_up", l) + tuple(_dw(s["hn2"], du, f"dw_up{l}", "cols", D_MODEL, 2048)))
        pending.append(("w_out", l) + tuple(_dw(s["ym"], dh1b, f"dw_out{l}", "rows", D_MODEL, D_MODEL)))
        riders, sent = send_pending()
        (dz, dsink, dcw, dpv, dwa, dwx), got = _mixer_bwd(
            dycat, s["z"], s["ycat"], s["hl"], s["uc"], s["probs"], s["psinks"], s["kept"], q["sink"], q["cw"],
            q["pv"], q["wa"], q["wx"], f"mixer_bwd{l}", riders)
        record(sent, got)
        d_win = _dw(s["hn1"], dz, f"dw_in{l}", "cols", D_MODEL, IN_W)
        if l > 0:
            pending.append(("w_in", l) + tuple(d_win))
            after = jnp.zeros((8, 128), F32)
        else:
            win_sends = _send_start([d_win[1]], True, "scatter_w_in0_start")
            after = win_sends.token
        dh, dg1 = _in_bwd(dz, s["h0"], dh1, q["g1"], gw["w_in"], after, f"in_bwd{l}")
        grads[l] = dict(
            norm1=dg1[0], attn_sinks=jnp.stack([dsink[0:4, 0], dsink[0:4, 2 * BLK]], axis=1).reshape(8),
            conv_dw_w=dcw[0:CONV_K], conv_dw_b=dpv[R_CONV_B], conv_ln_g=dpv[R_LN_G], conv_ln_b=dpv[R_LN_B],
            lru_conv_w=dpv[R_LCW:R_LCW + LRU_K], lru_conv_b=dpv[R_LCONV_B], lru_wa=_unblock_diag(dwa),
            lru_ba=dpv[R_BA].reshape(4, 64), lru_wx=_unblock_diag(dwx), lru_bx=dpv[R_BX].reshape(4, 64),
            lru_lambda=dpv[R_LAM], mix_norm=dgm[0], norm2=dg2[0])

    small = [jnp.stack([grads[l][n] for l in range(depth)]) for n in _SMALL] + [dgf, loss[:, 0:1]]
    sizes = [a.size for a in small]
    total = -(-sum(sizes) // 1024) * 1024
    packed = jnp.concatenate([a.reshape(-1) for a in small] + [jnp.zeros((total - sum(sizes),), F32)])
    packed = packed.reshape(total // 128, 128)
    small_sends = _send_start([packed], False, "bcast_small_start")

    out = {}
    shard_update = lambda n, after: list(_adamw_shard(
        [big[n][l][0] for l in range(depth)], [big[n][l][1] for l in range(depth)], dev1, w[n], m[n], v[n], after,
        f"adamw_{n}"))
    for n in ("w_out", "w_up", "w_down"):
        out[n] = shard_update(n, small_sends.token)
    _, (win_recv,) = _send_wait(win_sends, out["w_down"][1], "scatter_w_in0_wait")
    big["w_in"][0] = (d_win[0], win_recv)
    (packed,), (small_recv,) = _send_wait(small_sends, win_recv, "bcast_small_wait")
    out["w_in"] = shard_update("w_in", jnp.zeros((8, 128), F32))
    parts = jnp.concatenate([packed[None], small_recv], axis=0)
    summed = _sum_parts(parts, dev1, "sum_small_grads").reshape(-1)
    small_sums, pos = [], 0
    for a, size in zip(small, sizes):
        small_sums.append(summed[pos:pos + size].reshape(a.shape))
        pos += size
    shard = lambda a: lax.dynamic_slice_in_dim(a, dev * (a.shape[-1] // N_DEV), a.shape[-1] // N_DEV, axis=a.ndim - 1)
    flat = {"lru_wa": (depth, LRU_W, 64), "lru_wx": (depth, LRU_W, 64), "final_norm": (1, D_MODEL)}
    gs, ws, ms, vs = [], [], [], []
    for n, g in zip(_SMALL + ["final_norm"], small_sums[:-1]):
        shp = flat.get(n, w[n].shape)
        gs.append((shard(g) if n in ("conv_dw_w", "lru_conv_w") else g).reshape(shp))
        ws.append(w[n].reshape(shp))
        ms.append(m[n].reshape(shp))
        vs.append(v[n].reshape(shp))
    sd, sm, sv = _adamw_small(gs, ws, ms, vs, "adamw_small")
    for j, n in enumerate(_SMALL + ["final_norm"]):
        out[n] = [a.reshape(w[n].shape) for a in (gs[j], sd[j], sm[j], sv[j])]
    loss_total = small_sums[-1][0, 0]

    result = [loss_total, dh[None]]
    for j in range(4):
        result += [out[n][j] for n in _WEIGHTS]
    return tuple(result)
```

```python
import types

import jax
import jax.numpy as jnp
from jax import lax
from jax.experimental import pallas as pl
from jax.experimental.pallas import tpu as pltpu

F32 = jnp.float32
MX = jnp.bfloat16
WIRE = jnp.bfloat16

D_MODEL = 1024
HEAD_DIM = 64
ATTN_W = 512
KV_W = 128
BLK = 128
CONV_W = 256
CONV_K = 31
LRU_W = 256
LRU_K = 4
LRU_C = 8.0
IN_W = 1792
D_FF = 4096
FF_BLK = 512
N_DEV = 8
IN_SHARD = IN_W // N_DEV
RMS_EPS = 1e-6
LN_EPS = 1e-5
MASK_VALUE = -1e30
SCALE = HEAD_DIM ** -0.5
CONV_HALO = 32
LRU_HALO = 8
CONV_CHUNK = 64
POST_TILE = 512
DW_TILE = 1024
Q0, K0, V0, CV0, CG0, RX0, RG0 = 0, 512, 640, 768, 1024, 1280, 1536
R_CONV_B, R_LN_G, R_LN_B, R_LCONV_B, R_BA, R_BX, R_LAM, R_LCW = 0, 1, 2, 3, 4, 5, 6, 8

ADAM_LR, ADAM_B1, ADAM_B2, ADAM_EPS, ADAM_WD, ADAM_STEP = 0.001, 0.9, 0.999, 1e-08, 0.01, 10

VMEM_LIMIT = 56 * 1024 * 1024
MESH = pl.DeviceIdType.MESH
ANY = pl.BlockSpec(memory_space=pl.ANY)


def _tile(t, cap=512):
    return min(cap, t)


def _dot(a, b):
    return jnp.dot(a.astype(MX), b.astype(MX), preferred_element_type=F32)


def _dot_nt(a, b):
    return lax.dot_general(a.astype(MX), b.astype(MX), (((1,), (1,)), ((), ())), preferred_element_type=F32)


def _dot_tn(a, b):
    return lax.dot_general(a.astype(MX), b.astype(MX), (((0,), (0,)), ((), ())), preferred_element_type=F32)


def _const_spec(shape):
    nd = len(shape)
    return pl.BlockSpec(shape, lambda *_: (0,) * nd, pipeline_mode=pl.Buffered(1))


def _acc_spec(shape):
    nd = len(shape)
    return pl.BlockSpec(shape, lambda *_: (0,) * nd)


def _sds(shape, dtype):
    return jax.ShapeDtypeStruct(shape, dtype)


def _sigmoid(x):
    return jax.nn.sigmoid(x)


def _rms_fwd(x, g):
    r = lax.rsqrt(jnp.mean(x * x, axis=-1, keepdims=True) + RMS_EPS)
    xh = x * r
    return xh * g, xh, r


def _rms_bwd(dy, xh, r, g):
    t = dy * g
    dx = r * (t - xh * jnp.mean(t * xh, axis=-1, keepdims=True))
    return dx, jnp.sum(dy * xh, axis=0, keepdims=True)


_GROUPS = ((0, 512), (512, 768), (768, 1024))


def _group_rms_fwd(y, g):
    parts = [_rms_fwd(y[:, a:b], g[:, a:b]) for a, b in _GROUPS]
    return (jnp.concatenate([p[0] for p in parts], axis=1),
            jnp.concatenate([p[1] for p in parts], axis=1),
            [p[2] for p in parts])


def _gelu(x):
    c = 0.7978845608028654
    u = c * (x + 0.044715 * x * x * x)
    th = jnp.tanh(u)
    val = 0.5 * x * (1.0 + th)
    grad = 0.5 * (1.0 + th) + 0.5 * x * (1.0 - th * th) * c * (1.0 + 3.0 * 0.044715 * x * x)
    return val, grad


def _neg_expm1(x):
    series = -x * (1.0 + x * (0.5 + x * (1.0 / 6.0 + x * (1.0 / 24.0))))
    return jnp.where(x > -0.02, series, 1.0 - jnp.exp(x))


def _me():
    return lax.axis_index("x"), lax.axis_index("y"), lax.axis_index("c")


def _gather_rider(arrays):
    arrays = list(arrays)
    n = len(arrays)

    def plan(ins, outs, sems):
        ssem, rsem, lsem = sems
        x, y, c = _me()
        chips = [(1 - x, y), (x, 1 - y), (1 - x, 1 - y)]

        def copy(a, k, block, to, own=False):
            dst = outs[a].at[4 * block[0] + 2 * block[1] + block[2]]
            return pltpu.make_async_remote_copy(
                src_ref=ins[a] if own else dst, dst_ref=dst, send_sem=ssem.at[7 * a + k],
                recv_sem=rsem.at[7 * a + k], device_id=to, device_id_type=MESH)

        return x, y, c, chips, copy, lsem

    def start(ins, outs, sems):
        x, y, c, chips, copy, lsem = plan(ins, outs, sems)
        for a in range(n):
            pltpu.make_async_copy(ins[a], outs[a].at[4 * x + 2 * y + c], lsem.at[a]).start()
            copy(a, 0, (x, y, c), (x, y, 1 - c), own=True).start()
            for j, chip in enumerate(chips):
                copy(a, 1 + j, (x, y, c), (*chip, c), own=True).start()

    def mid(ins, outs, sems):
        x, y, c, chips, copy, _ = plan(ins, outs, sems)
        for a in range(n):
            for j, chip in enumerate(chips):
                copy(a, 1 + j, (*chip, c), (x, y, c)).wait_recv()
                copy(a, 4 + j, (*chip, c), (x, y, 1 - c)).start()

    def finish(ins, outs, sems):
        x, y, c, chips, copy, lsem = plan(ins, outs, sems)
        for a in range(n):
            copy(a, 0, (x, y, 1 - c), (x, y, c)).wait_recv()
            for j, chip in enumerate(chips):
                copy(a, 4 + j, (*chip, 1 - c), (x, y, c)).wait_recv()
        for a in range(n):
            copy(a, 0, (x, y, c), (x, y, 1 - c), own=True).wait_send()
            for j, chip in enumerate(chips):
                copy(a, 1 + j, (x, y, c), (*chip, c), own=True).wait_send()
                copy(a, 4 + j, (*chip, c), (x, y, 1 - c)).wait_send()
            pltpu.make_async_copy(ins[a], outs[a].at[4 * x + 2 * y + c], lsem.at[a]).wait()

    return types.SimpleNamespace(
        arrays=arrays, out_shape=[_sds((N_DEV,) + a.shape, a.dtype) for a in arrays],
        scratch=[pltpu.SemaphoreType.DMA((7 * n,)), pltpu.SemaphoreType.DMA((7 * n,)), pltpu.SemaphoreType.DMA((n,))],
        start=start, mid=mid, finish=finish)


def _bcast_rider(arrays):
    arrays = list(arrays)
    n = len(arrays)

    def copies(ins, outs, sems, landing):
        ssem, rsem, lsem = sems
        x, y, c = _me()
        out = []
        for a in range(n):
            out.append(pltpu.make_async_copy(ins[a], outs[a].at[4 * x + 2 * y + c], lsem.at[a]))
            for f in range(1, N_DEV):
                px = 1 - x if f & 4 else x
                py = 1 - y if f & 2 else y
                pc = 1 - c if f & 1 else c
                slot = 4 * px + 2 * py + pc if landing else 4 * x + 2 * y + c
                out.append(pltpu.make_async_remote_copy(
                    src_ref=ins[a], dst_ref=outs[a].at[slot], send_sem=ssem.at[7 * a + f - 1],
                    recv_sem=rsem.at[7 * a + f - 1], device_id=(px, py, pc), device_id_type=MESH))
        return out

    def start(ins, outs, sems):
        for cp in copies(ins, outs, sems, landing=False):
            cp.start()

    def finish(ins, outs, sems):
        for cp in copies(ins, outs, sems, landing=True):
            cp.wait()

    return types.SimpleNamespace(
        arrays=arrays, out_shape=[_sds((N_DEV,) + a.shape, a.dtype) for a in arrays],
        scratch=[pltpu.SemaphoreType.DMA((7 * n,)), pltpu.SemaphoreType.DMA((7 * n,)), pltpu.SemaphoreType.DMA((n,))],
        start=start, mid=None, finish=finish)


def _scatter_rider(arrays):
    arrays = list(arrays)
    n = len(arrays)

    def copies(ins, outs, sems):
        ssem, rsem = sems
        x, y, c = _me()
        out = []
        for a in range(n):
            for f in range(1, N_DEV):
                px = 1 - x if f & 4 else x
                py = 1 - y if f & 2 else y
                pc = 1 - c if f & 1 else c
                out.append(pltpu.make_async_remote_copy(
                    src_ref=ins[a].at[4 * px + 2 * py + pc], dst_ref=outs[a].at[f - 1], send_sem=ssem.at[7 * a + f - 1],
                    recv_sem=rsem.at[7 * a + f - 1], device_id=(px, py, pc), device_id_type=MESH))
        return out

    def start(ins, outs, sems):
        for cp in copies(ins, outs, sems):
            cp.start()

    def finish(ins, outs, sems):
        for cp in copies(ins, outs, sems):
            cp.wait()

    return types.SimpleNamespace(
        arrays=arrays, out_shape=[_sds((N_DEV - 1,) + a.shape[1:], a.dtype) for a in arrays],
        scratch=[pltpu.SemaphoreType.DMA((7 * n,)), pltpu.SemaphoreType.DMA((7 * n,))],
        start=start, mid=None, finish=finish)


def _call(body, name, grid, in_specs, out_specs, out_shape, scratch, operands, riders=()):
    n_in, n_out, n_scr = len(operands), len(out_shape), len(scratch)
    nsteps = grid[0] if grid else 1
    sizes = [(len(r.arrays), len(r.out_shape), len(r.scratch)) for r in riders]

    def wrapped(*refs):
        pos = n_in
        r_ins = []
        for ri, _, _ in sizes:
            r_ins.append(refs[pos:pos + ri])
            pos += ri
        outs = refs[pos:pos + n_out]
        pos += n_out
        r_outs = []
        for _, ro, _ in sizes:
            r_outs.append(refs[pos:pos + ro])
            pos += ro
        scr = refs[pos:pos + n_scr]
        pos += n_scr
        r_sems = []
        for _, _, rs in sizes:
            r_sems.append(refs[pos:pos + rs])
            pos += rs
        step = pl.program_id(0) if grid else 0

        def at(s, fn):
            if grid:
                pl.when(step == s)(fn)
            else:
                fn()

        for r, a, b, c in zip(riders, r_ins, r_outs, r_sems):
            at(0, lambda r=r, a=a, b=b, c=c: r.start(a, b, c))
        for r, a, b, c in zip(riders, r_ins, r_outs, r_sems):
            if r.mid is not None:
                at((3 * nsteps) // 4, lambda r=r, a=a, b=b, c=c: r.mid(a, b, c))
        if body is not None:
            body(*refs[:n_in], *outs, *scr)
        for r, a, b, c in zip(riders, r_ins, r_outs, r_sems):
            at(nsteps - 1, lambda r=r, a=a, b=b, c=c: r.finish(a, b, c))

    r_arrays = [a for r in riders for a in r.arrays]
    r_shapes = [s for r in riders for s in r.out_shape]
    kwargs = {}
    if grid:
        kwargs = dict(grid=grid, compiler_params=pltpu.CompilerParams(
            dimension_semantics=("arbitrary",) * len(grid), vmem_limit_bytes=VMEM_LIMIT))
    res = pl.pallas_call(
        wrapped, name=name,
        in_specs=list(in_specs) + [ANY] * len(r_arrays),
        out_specs=list(out_specs) + [ANY] * len(r_shapes),
        out_shape=list(out_shape) + r_shapes,
        scratch_shapes=list(scratch) + [s for r in riders for s in r.scratch],
        **kwargs,
    )(*operands, *r_arrays)
    host, rest = res[:n_out], res[n_out:]
    r_res = []
    for _, ro, _ in sizes:
        r_res.append(rest[:ro])
        rest = rest[ro:]
    return host, r_res


def _ln_in(h, g1, w_in, name):
    t = h.shape[0]
    tm = _tile(t)

    def body(h_ref, g_ref, w_ref, z_ref, hn_ref):
        y, _, _ = _rms_fwd(h_ref[...], g_ref[...])
        hn = y.astype(MX)
        hn_ref[...] = hn
        z_ref[...] = jnp.dot(hn, w_ref[...], preferred_element_type=F32)

    tile = lambda w: pl.BlockSpec((tm, w), lambda i: (i, 0))
    (z, hn), _ = _call(
        body, name, (t // tm,),
        [tile(D_MODEL), _const_spec((1, D_MODEL)), _const_spec((D_MODEL, IN_W))],
        [tile(IN_W), tile(D_MODEL)], [_sds((t, IN_W), F32), _sds((t, D_MODEL), MX)], [], [h, g1, w_in])
    return z, hn


def _band2(kb, g):
    lo = lax.broadcasted_iota(jnp.int32, kb.shape, 1) < HEAD_DIM
    kr = pltpu.roll(kb, HEAD_DIM, 1)
    if g == 0:
        top, bot = jnp.where(lo, kb, 0.0), jnp.where(lo, 0.0, kr)
    else:
        top, bot = jnp.where(lo, kr, 0.0), jnp.where(lo, 0.0, kb)
    return jnp.concatenate([top, bot], axis=0)


def _attn_operands(z_ref, zh_ref, b):
    rows = slice(b * BLK, (b + 1) * BLK)
    prev = zh_ref if b == 0 else z_ref
    prow = slice(0, BLK) if b == 0 else slice((b - 1) * BLK, b * BLK)
    kb = jnp.concatenate([prev[prow, K0:K0 + KV_W], z_ref[rows, K0:K0 + KV_W]], axis=0)
    vb = jnp.concatenate([prev[prow, V0:V0 + KV_W], z_ref[rows, V0:V0 + KV_W]], axis=0)
    k2 = [_band2(kb, g) for g in range(2)]
    v2 = [_band2(vb, g) for g in range(2)]
    q2 = [jnp.concatenate([z_ref[rows, (2 * g) * BLK:(2 * g + 1) * BLK], z_ref[rows, (2 * g + 1) * BLK:(2 * g + 2) * BLK]],
                          axis=0) for g in range(2)]
    return q2, k2, v2


def _attn_block(z_ref, zh_ref, sink_ref, b, first):
    q2, k2, v2 = _attn_operands(z_ref, zh_ref, b)
    rr = lax.broadcasted_iota(jnp.int32, (4 * BLK, 2 * BLK), 0) & (BLK - 1)
    cc = lax.broadcasted_iota(jnp.int32, (4 * BLK, 2 * BLK), 1)
    first_block = jnp.logical_and(first, b == 0).astype(jnp.int32)
    mask = jnp.logical_and(jnp.logical_and(cc > rr, cc <= rr + BLK), cc >= BLK * first_block)
    s = jnp.concatenate([_dot_nt(q2[g], k2[g]) for g in range(2)], axis=0) * SCALE
    w = 2 * BLK
    out, psink = [], []
    for hh in range(2):
        sh = jnp.where(mask, s[:, hh * w:(hh + 1) * w], MASK_VALUE)
        sk = jnp.concatenate([jnp.broadcast_to(sink_ref[p:p + 1, hh * w:hh * w + 1], (BLK, 1)) for p in range(4)], axis=0)
        m = jnp.maximum(jnp.max(sh, axis=1, keepdims=True), sk)
        p = jnp.exp(sh - m)
        es = jnp.exp(sk - m)
        inv = 1.0 / (jnp.sum(p, axis=1, keepdims=True) + es)
        out.append(p * inv)
        psink.append(es * inv)
    return v2, jnp.concatenate(out, axis=1), psink


def _scan_steps(a, b, n, span, reverse):
    pos = lax.broadcasted_iota(jnp.int32, a.shape, 0) & (span - 1)
    d = 1
    while d < span:
        keep = pos < span - d if reverse else pos >= d
        shift = n - d if reverse else d
        a_sh = jnp.where(keep, pltpu.roll(a, shift, 0), 1.0)
        b_sh = jnp.where(keep, pltpu.roll(b, shift, 0), 0.0)
        b = a * b_sh + b
        a = a * a_sh
        d *= 2
    return a, b


def _scan(a, b, tm, reverse):
    return _scan_steps(a, b, tm, tm, reverse)


def _shifted_copies(ext, shifts, tm):
    rows = tm + CONV_HALO - 8
    for r in range(1, 8):
        shifts[r - 1, 0:rows, :] = ext[pl.ds(r, rows), :]


def _tap(ext, shifts, off, r0, n):
    a, r = divmod(off, 8)
    lo = 8 * a + r0
    if r == 0:
        return ext[lo:lo + n, :]
    return shifts[r - 1, lo:lo + n, :]


def _glu_fill(z_ref, zh_ref, uext, ush, first, tm, sg_out=None):
    cv = z_ref[:, CV0:CV0 + CONV_W]
    sg = _sigmoid(z_ref[:, CG0:CG0 + CONV_W])
    if sg_out is not None:
        sg_out[...] = sg
    hrow = BLK - CONV_HALO
    uh = zh_ref[hrow:BLK, CV0:CV0 + CONV_W] * _sigmoid(zh_ref[hrow:BLK, CG0:CG0 + CONV_W])
    uext[0:CONV_HALO, :] = jnp.where(first, 0.0, uh)
    uext[CONV_HALO:CONV_HALO + tm, :] = cv * sg
    _shifted_copies(uext, ush, tm)


def _conv_taps(cw_ref, pv_ref, uext, ush, out_ref, tm):
    for r0 in range(0, tm, CONV_CHUNK):
        acc = jnp.broadcast_to(pv_ref[R_CONV_B:R_CONV_B + 1, :], (CONV_CHUNK, CONV_W))
        for k in range(CONV_K):
            acc = acc + cw_ref[k:k + 1, :] * _tap(uext, ush, CONV_HALO - (CONV_K - 1) + k, r0, CONV_CHUNK)
        out_ref[r0:r0 + CONV_CHUNK, :] = acc


def _ln_silu(uc, pv_ref):
    mu = jnp.mean(uc, axis=-1, keepdims=True)
    xc = uc - mu
    rs = lax.rsqrt(jnp.mean(xc * xc, axis=-1, keepdims=True) + LN_EPS)
    xh = xc * rs
    ln = xh * pv_ref[R_LN_G:R_LN_G + 1, :] + pv_ref[R_LN_B:R_LN_B + 1, :]
    sg = _sigmoid(ln)
    return xh, rs, ln, sg


def _lru_gates(z_ref, zh_ref, pv_ref, wa_ref, wx_ref, rxext, first, tm):
    rxext[0:LRU_HALO, :] = jnp.where(first, 0.0, zh_ref[BLK - LRU_HALO:BLK, RX0:RX0 + LRU_W])
    rxext[LRU_HALO:LRU_HALO + tm, :] = z_ref[:, RX0:RX0 + LRU_W]
    xc = jnp.broadcast_to(pv_ref[R_LCONV_B:R_LCONV_B + 1, :], (tm, LRU_W))
    for k in range(LRU_K):
        xc = xc + pv_ref[R_LCW + k:R_LCW + k + 1, :] * rxext[pl.ds(LRU_HALO - (LRU_K - 1) + k, tm), :]
    r = _sigmoid(_dot(xc, wa_ref[...]) + pv_ref[R_BA:R_BA + 1, :])
    ig = _sigmoid(_dot(xc, wx_ref[...]) + pv_ref[R_BX:R_BX + 1, :])
    lam = pv_ref[R_LAM:R_LAM + 1, :]
    sp = jnp.log1p(jnp.exp(-lam))
    la = (-LRU_C * r) * sp
    a = jnp.exp(la)
    mult = jnp.sqrt(_neg_expm1(2.0 * la))
    return xc, r, ig, sp, la, a, mult


def _mixer_in_specs(tm, tile_of):
    hb = tm // BLK
    return [
        pl.BlockSpec((tm, IN_W), lambda i: (tile_of(i), 0)),
        pl.BlockSpec((BLK, IN_W), lambda i: (jnp.maximum(tile_of(i) * hb - 1, 0), 0)),
        _const_spec((8, 4 * BLK)),
        _const_spec((32, CONV_W)),
        _const_spec((16, CONV_W)),
        _const_spec((LRU_W, LRU_W)),
        _const_spec((LRU_W, LRU_W)),
    ]


def _mixer_fwd(z, sink, cw, pv, wa, wx, name, riders=()):
    t = z.shape[0]
    tm = _tile(t)
    nb = tm // BLK

    def body(z_ref, zh_ref, sink_ref, cw_ref, pv_ref, wa_ref, wx_ref, y_ref, hl_ref, uc_ref, p_ref, ps_ref,
             uext, ush, rxext, hcar):
        i = pl.program_id(0)
        first = i == 0

        @pl.when(first)
        def _():
            hcar[...] = jnp.zeros_like(hcar)

        lo = lax.broadcasted_iota(jnp.int32, (4 * BLK, BLK), 1) < HEAD_DIM
        for b in range(nb):
            rows = slice(b * BLK, (b + 1) * BLK)
            v2, prob, psink = _attn_block(z_ref, zh_ref, sink_ref, b, first)
            prob = prob.astype(MX)
            p_ref[b] = prob
            ps_ref[b] = jnp.where(lo, psink[0], psink[1])
            for g in range(2):
                o = _dot(prob[2 * g * BLK:(2 * g + 2) * BLK], v2[g])
                y_ref[rows, (2 * g) * BLK:(2 * g + 1) * BLK] = o[0:BLK]
                y_ref[rows, (2 * g + 1) * BLK:(2 * g + 2) * BLK] = o[BLK:2 * BLK]
        _glu_fill(z_ref, zh_ref, uext, ush, first, tm)
        _conv_taps(cw_ref, pv_ref, uext, ush, uc_ref, tm)
        _, _, ln, sg = _ln_silu(uc_ref[...], pv_ref)
        y_ref[:, ATTN_W:ATTN_W + CONV_W] = ln * sg
        xc, _, ig, _, _, a, mult = _lru_gates(z_ref, zh_ref, pv_ref, wa_ref, wx_ref, rxext, first, tm)
        acum, h = _scan(a, mult * (ig * xc), tm, reverse=False)
        h = h + acum * hcar[0:1, :]
        hl_ref[...] = h
        hcar[0:1, :] = h[tm - 1:tm, :]
        gl, _ = _gelu(z_ref[:, RG0:RG0 + LRU_W])
        y_ref[:, ATTN_W + CONV_W:ATTN_W + CONV_W + LRU_W] = h * gl

    tile = lambda w: pl.BlockSpec((tm, w), lambda i: (i, 0))
    return _call(
        body, name, (t // tm,), _mixer_in_specs(tm, lambda i: i),
        [tile(D_MODEL), tile(LRU_W), tile(CONV_W), pl.BlockSpec((nb, 4 * BLK, 4 * BLK), lambda i: (i, 0, 0)),
         pl.BlockSpec((nb, 4 * BLK, BLK), lambda i: (i, 0, 0))],
        [_sds((t, D_MODEL), F32), _sds((t, LRU_W), F32), _sds((t, CONV_W), F32),
         _sds((t // BLK, 4 * BLK, 4 * BLK), MX), _sds((t // BLK, 4 * BLK, BLK), F32)],
        [pltpu.VMEM((tm + CONV_HALO, CONV_W), F32), pltpu.VMEM((7, tm + CONV_HALO - 8, CONV_W), F32),
         pltpu.VMEM((tm + LRU_HALO, LRU_W), F32), pltpu.VMEM((8, LRU_W), F32)],
        [z, z, sink, cw, pv, wa, wx], riders)


def _mixer_bwd(dy, z, ycat, hl, uc, probs, psinks, sink, cw, pv, wa, wx, name, riders=()):
    t = z.shape[0]
    tm = _tile(t)
    nt = t // tm
    nb = tm // BLK
    rev = lambda i: nt - 1 - i

    def body(dy_ref, z_ref, zh_ref, sink_ref, cw_ref, pv_ref, wa_ref, wx_ref, y_ref, hl_ref, hlh_ref, uc_ref,
             p_ref, ps_ref, dz_ref, dsink_ref, dcw_ref, dpv_ref, dwa_ref, dwx_ref,
             uext, ush, sgs, rxext, dkext, dvext, ducext, dsh, dcw8, dxcext, kcar, vcar, uccar, xccar, gcar):
        i = pl.program_id(0)
        first = i == nt - 1

        @pl.when(i == 0)
        def _():
            for car in (kcar, vcar, uccar, xccar, gcar, dcw8):
                car[...] = jnp.zeros_like(car)
            for acc in (dsink_ref, dpv_ref, dwa_ref, dwx_ref):
                acc[...] = jnp.zeros_like(acc)

        def addrow(r, val):
            dpv_ref[r:r + 1, :] += jnp.sum(val, axis=0, keepdims=True)

        dkext[:, 0:tm] = jnp.zeros((KV_W, tm), F32)
        dvext[:, 0:tm] = jnp.zeros((KV_W, tm), F32)
        dkext[:, tm:tm + BLK] = kcar[...]
        dvext[:, tm:tm + BLK] = vcar[...]
        lane512 = lax.broadcasted_iota(jnp.int32, (1, 4 * BLK), 1) < 2 * BLK
        lo = lax.broadcasted_iota(jnp.int32, (4 * BLK, BLK), 1) < HEAD_DIM
        hd, w2 = HEAD_DIM, 2 * BLK
        for b in range(nb):
            rows = slice(b * BLK, (b + 1) * BLK)
            band = slice(b * BLK, (b + 2) * BLK)
            q2, k2, v2 = _attn_operands(z_ref, zh_ref, b)
            prob = p_ref[b]
            psink = [ps_ref[b, :, 0:1], ps_ref[b, :, HEAD_DIM:HEAD_DIM + 1]]
            stack = lambda ref: jnp.concatenate([ref[rows, p * BLK:(p + 1) * BLK] for p in range(4)], axis=0)
            do4 = stack(dy_ref)
            dlt = do4 * stack(y_ref)
            d0 = jnp.sum(jnp.where(lo, dlt, 0.0), axis=1, keepdims=True)
            d1 = jnp.sum(jnp.where(lo, 0.0, dlt), axis=1, keepdims=True)
            dp = jnp.concatenate([_dot_nt(do4[g * w2:(g + 1) * w2], v2[g]) for g in range(2)], axis=0)
            dl = jnp.concatenate([jnp.broadcast_to(d0, (4 * BLK, w2)), jnp.broadcast_to(d1, (4 * BLK, w2))], axis=1)
            draw = (prob * (dp - dl)) * SCALE
            e0, e1 = psink[0] * d0, psink[1] * d1
            for p in range(4):
                prs = slice(p * BLK, (p + 1) * BLK)
                s0 = jnp.sum(e0[prs], axis=0, keepdims=True)
                s1 = jnp.sum(e1[prs], axis=0, keepdims=True)
                dsink_ref[p:p + 1, :] += -jnp.where(lane512, s0, s1)
            for g in range(2):
                grs = slice(g * w2, (g + 1) * w2)
                dq = _dot(draw[grs], k2[g])
                dz_ref[rows, (2 * g) * BLK:(2 * g + 1) * BLK] = dq[0:BLK].astype(dz_ref.dtype)
                dz_ref[rows, (2 * g + 1) * BLK:(2 * g + 2) * BLK] = dq[BLK:2 * BLK].astype(dz_ref.dtype)
                tk = _dot_tn(q2[g], draw[grs])
                tv = _dot_tn(do4[grs], prob[grs])
                dkext[g * hd:(g + 1) * hd, band] += tk[0:hd, 0:w2] + tk[hd:2 * hd, w2:2 * w2]
                dvext[g * hd:(g + 1) * hd, band] += tv[0:hd, 0:w2] + tv[hd:2 * hd, w2:2 * w2]
        dz_ref[:, K0:K0 + KV_W] = jnp.transpose(dkext[:, BLK:BLK + tm]).astype(dz_ref.dtype)
        dz_ref[:, V0:V0 + KV_W] = jnp.transpose(dvext[:, BLK:BLK + tm]).astype(dz_ref.dtype)
        kcar[...] = dkext[:, 0:BLK]
        vcar[...] = dvext[:, 0:BLK]

        _glu_fill(z_ref, zh_ref, uext, ush, first, tm, sg_out=sgs)
        xh, rs, ln, sg = _ln_silu(uc_ref[...], pv_ref)
        dln = dy_ref[:, ATTN_W:ATTN_W + CONV_W] * (sg * (1.0 + ln * (1.0 - sg)))
        addrow(R_LN_G, dln * xh)
        addrow(R_LN_B, dln)
        dxh = dln * pv_ref[R_LN_G:R_LN_G + 1, :]
        duc = rs * (dxh - jnp.mean(dxh, axis=-1, keepdims=True) - xh * jnp.mean(dxh * xh, axis=-1, keepdims=True))
        addrow(R_CONV_B, duc)
        ducext[0:tm, :] = duc
        ducext[tm:tm + CONV_HALO, :] = uccar[...]
        uccar[...] = duc[0:CONV_HALO, :]
        _shifted_copies(ducext, dsh, tm)
        for r0 in range(0, tm, CONV_CHUNK):
            crow = slice(r0, r0 + CONV_CHUNK)
            duc_c = ducext[crow, :]
            du = jnp.zeros((CONV_CHUNK, CONV_W), F32)
            for k in range(CONV_K):
                prod = duc_c * _tap(uext, ush, CONV_HALO - (CONV_K - 1) + k, r0, CONV_CHUNK)
                part = prod[0:8]
                for s in range(8, CONV_CHUNK, 8):
                    part = part + prod[s:s + 8]
                dcw8[k] += part
                du = du + cw_ref[k:k + 1, :] * _tap(ducext, dsh, CONV_K - 1 - k, r0, CONV_CHUNK)
            sgc = sgs[crow, :]
            dz_ref[crow, CV0:CV0 + CONV_W] = (du * sgc).astype(dz_ref.dtype)
            u_c = uext[CONV_HALO + r0:CONV_HALO + r0 + CONV_CHUNK, :]
            dz_ref[crow, CG0:CG0 + CONV_W] = (du * u_c * (1.0 - sgc)).astype(dz_ref.dtype)

        @pl.when(i == nt - 1)
        def _():
            dcw_ref[...] = jnp.sum(dcw8[...], axis=1)

        xc, r, ig, sp, la, a, mult = _lru_gates(z_ref, zh_ref, pv_ref, wa_ref, wx_ref, rxext, first, tm)
        h = hl_ref[...]
        rowi = lax.broadcasted_iota(jnp.int32, (tm, LRU_W), 0)
        hlast = jnp.where(first, 0.0, hlh_ref[7:8, :])
        hprev = jnp.where(rowi == 0, hlast, pltpu.roll(h, 1, 0))
        dyl = dy_ref[:, ATTN_W + CONV_W:ATTN_W + CONV_W + LRU_W]
        gl, dgl = _gelu(z_ref[:, RG0:RG0 + LRU_W])
        dz_ref[:, RG0:RG0 + LRU_W] = (dyl * h * dgl).astype(dz_ref.dtype)
        dh = dyl * gl + jnp.where(rowi == tm - 1, gcar[0:1, :], 0.0)
        c = jnp.where(rowi == tm - 1, 0.0, pltpu.roll(a, tm - 1, 0))
        _, gg = _scan(c, dh, tm, reverse=True)
        gcar[0:1, :] = a[0:1, :] * gg[0:1, :]
        dmult = gg * (ig * xc)
        dig = gg * mult * xc
        dxc = gg * mult * ig
        dla = gg * hprev * a - dmult * a * a / mult
        dr = dla * (-LRU_C * sp)
        lam = pv_ref[R_LAM:R_LAM + 1, :]
        dpv_ref[R_LAM:R_LAM + 1, :] += jnp.sum(dla * (-LRU_C * r), axis=0, keepdims=True) * (-_sigmoid(-lam))
        dpa = dr * r * (1.0 - r)
        dpx = dig * ig * (1.0 - ig)
        addrow(R_BA, dpa)
        addrow(R_BX, dpx)
        dxc = dxc + _dot_nt(dpa, wa_ref[...]) + _dot_nt(dpx, wx_ref[...])
        dwa_ref[...] += _dot_tn(xc, dpa)
        dwx_ref[...] += _dot_tn(xc, dpx)
        addrow(R_LCONV_B, dxc)
        dxcext[0:tm, :] = dxc
        dxcext[tm:tm + LRU_HALO, :] = xccar[...]
        xccar[...] = dxc[0:LRU_HALO, :]
        drx = jnp.zeros((tm, LRU_W), F32)
        for k in range(LRU_K):
            addrow(R_LCW + k, dxc * rxext[pl.ds(LRU_HALO - (LRU_K - 1) + k, tm), :])
            drx = drx + pv_ref[R_LCW + k:R_LCW + k + 1, :] * dxcext[pl.ds(LRU_K - 1 - k, tm), :]
        dz_ref[:, RX0:RX0 + LRU_W] = drx.astype(dz_ref.dtype)

    tile = lambda w: pl.BlockSpec((tm, w), lambda i: (rev(i), 0))
    in_specs = [tile(D_MODEL)] + _mixer_in_specs(tm, rev) + [
        tile(D_MODEL), tile(LRU_W),
        pl.BlockSpec((8, LRU_W), lambda i: (jnp.maximum(rev(i) * (tm // 8) - 1, 0), 0)),
        tile(CONV_W), pl.BlockSpec((nb, 4 * BLK, 4 * BLK), lambda i: (rev(i), 0, 0)),
        pl.BlockSpec((nb, 4 * BLK, BLK), lambda i: (rev(i), 0, 0))]
    return _call(
        body, name, (nt,), in_specs,
        [tile(IN_W), _acc_spec((8, 4 * BLK)), _acc_spec((32, CONV_W)), _acc_spec((16, CONV_W)),
         _acc_spec((LRU_W, LRU_W)), _acc_spec((LRU_W, LRU_W))],
        [_sds((t, IN_W), MX), _sds((8, 4 * BLK), F32), _sds((32, CONV_W), F32), _sds((16, CONV_W), F32),
         _sds((LRU_W, LRU_W), F32), _sds((LRU_W, LRU_W), F32)],
        [pltpu.VMEM((tm + CONV_HALO, CONV_W), F32), pltpu.VMEM((7, tm + CONV_HALO - 8, CONV_W), F32),
         pltpu.VMEM((tm, CONV_W), F32), pltpu.VMEM((tm + LRU_HALO, LRU_W), F32),
         pltpu.VMEM((KV_W, tm + BLK), F32), pltpu.VMEM((KV_W, tm + BLK), F32),
         pltpu.VMEM((tm + CONV_HALO, CONV_W), F32), pltpu.VMEM((7, tm + CONV_HALO - 8, CONV_W), F32),
         pltpu.VMEM((32, 8, CONV_W), F32), pltpu.VMEM((tm + LRU_HALO, LRU_W), F32),
         pltpu.VMEM((KV_W, BLK), F32), pltpu.VMEM((KV_W, BLK), F32),
         pltpu.VMEM((CONV_HALO, CONV_W), F32), pltpu.VMEM((LRU_HALO, LRU_W), F32), pltpu.VMEM((8, LRU_W), F32)],
        [dy, z, z, sink, cw, pv, wa, wx, ycat, hl, hl, uc, probs, psinks], riders)


def _post_fwd(ycat, h0, gmix, w_out, g2, w_up, w_down, name, riders=()):
    t = h0.shape[0]
    tm = _tile(t, POST_TILE)
    nj = D_FF // FF_BLK

    def body(y_ref, h_ref, gm_ref, wo_ref, g2_ref, wu_ref, wd_ref, h1_ref, a_ref, h2_ref, ym_ref, hn_ref):
        ym, _, _ = _group_rms_fwd(y_ref[...], gm_ref[...])
        ym = ym.astype(MX)
        ym_ref[...] = ym
        h1 = h_ref[...] + jnp.dot(ym, wo_ref[...], preferred_element_type=F32)
        h1_ref[...] = h1
        hn, _, _ = _rms_fwd(h1, g2_ref[...])
        hn = hn.astype(MX)
        hn_ref[...] = hn
        for j in range(nj):
            u = jnp.dot(hn, wu_ref[j], preferred_element_type=F32)
            a_ref[:, j * FF_BLK:(j + 1) * FF_BLK] = jnp.square(jnp.maximum(u, 0.0)).astype(MX)
        h2_ref[...] = h1 + jnp.dot(a_ref[...], wd_ref[...], preferred_element_type=F32)

    tile = lambda w: pl.BlockSpec((tm, w), lambda i: (i, 0))
    return _call(
        body, name, (t // tm,),
        [tile(D_MODEL), tile(D_MODEL), _const_spec((1, D_MODEL)), _const_spec((D_MODEL, D_MODEL)),
         _const_spec((1, D_MODEL)), _const_spec((nj, D_MODEL, FF_BLK)), _const_spec((D_FF, D_MODEL))],
        [tile(D_MODEL), tile(D_FF), tile(D_MODEL), tile(D_MODEL), tile(D_MODEL)],
        [_sds((t, D_MODEL), F32), _sds((t, D_FF), MX), _sds((t, D_MODEL), F32), _sds((t, D_MODEL), MX),
         _sds((t, D_MODEL), MX)],
        [], [ycat, h0, gmix, w_out, g2, w_up, w_down], riders)


def _ffn_bwd(dh2, act, h1, g2, w_up_t, w_down, name, riders=()):
    t = h1.shape[0]
    tm = _tile(t, POST_TILE)
    nj = D_FF // FF_BLK

    def body(dh2_ref, a_ref, h1_ref, g2_ref, wut_ref, wd_ref, dh1_ref, dh1b_ref, dh2b_ref, du_ref, dg2_ref):
        @pl.when(pl.program_id(0) == 0)
        def _():
            dg2_ref[...] = jnp.zeros_like(dg2_ref)

        dh2 = dh2_ref[...]
        dh2b = dh2.astype(MX)
        dh2b_ref[...] = dh2b
        for j in range(nj):
            cols = slice(j * FF_BLK, (j + 1) * FF_BLK)
            da = _dot_nt(dh2b, wd_ref[j])
            du_ref[:, cols] = (da * (2.0 * jnp.sqrt(a_ref[:, cols].astype(F32)))).astype(MX)
        dhn = jnp.dot(du_ref[...], wut_ref[...], preferred_element_type=F32)
        _, xh, r = _rms_fwd(h1_ref[...], g2_ref[...])
        dx, dg = _rms_bwd(dhn, xh, r, g2_ref[...])
        dg2_ref[...] += dg
        dh1 = dh2 + dx
        dh1_ref[...] = dh1
        dh1b_ref[...] = dh1.astype(MX)

    tile = lambda w: pl.BlockSpec((tm, w), lambda i: (i, 0))
    return _call(
        body, name, (t // tm,),
        [tile(D_MODEL), tile(D_FF), tile(D_MODEL), _const_spec((1, D_MODEL)),
         _const_spec((D_FF, D_MODEL)), _const_spec((nj, FF_BLK, D_MODEL))],
        [tile(D_MODEL), tile(D_MODEL), tile(D_MODEL), tile(D_FF), _acc_spec((1, D_MODEL))],
        [_sds((t, D_MODEL), F32), _sds((t, D_MODEL), MX), _sds((t, D_MODEL), MX), _sds((t, D_FF), MX),
         _sds((1, D_MODEL), F32)],
        [], [dh2, act, h1, g2, w_up_t, w_down], riders)


def _mix_bwd(dh1, ycat, ym, gmix, w_out, name):
    t = dh1.shape[0]
    tm = _tile(t)
    nk = t // tm
    r = D_MODEL // N_DEV

    def body(dh1_ref, y_ref, ym_ref, gm_ref, wo_ref, dy_ref, dgm_ref, o_ref, o16_ref, acc):
        k = pl.program_id(0)

        @pl.when(k == 0)
        def _():
            dgm_ref[...] = jnp.zeros_like(dgm_ref)
            acc[...] = jnp.zeros_like(acc)

        dh = dh1_ref[...]
        acc[...] += _dot_tn(ym_ref[...], dh)
        dym = _dot_nt(dh, wo_ref[...])
        gm = gm_ref[...]
        _, yh, rr = _group_rms_fwd(y_ref[...], gm)
        outs, dgs = [], []
        for (a, b), rg in zip(_GROUPS, rr):
            dxg, dgg = _rms_bwd(dym[:, a:b], yh[:, a:b], rg, gm[:, a:b])
            outs.append(dxg)
            dgs.append(dgg)
        dy_ref[...] = jnp.concatenate(outs, axis=1)
        dgm_ref[...] += jnp.concatenate(dgs, axis=1)

        @pl.when(k == nk - 1)
        def _():
            for d in range(N_DEV):
                v = acc[d * r:(d + 1) * r, :]
                o_ref[d] = v
                o16_ref[d] = v.astype(o16_ref.dtype)

    tile = pl.BlockSpec((tm, D_MODEL), lambda i: (i, 0))
    slabs = _const_spec((N_DEV, r, D_MODEL))
    (dy, dgm, dw, dw16), _ = _call(
        body, name, (nk,), [tile, tile, tile, _const_spec((1, D_MODEL)), _const_spec((D_MODEL, D_MODEL))],
        [tile, _acc_spec((1, D_MODEL)), slabs, slabs],
        [_sds((t, D_MODEL), F32), _sds((1, D_MODEL), F32), _sds((N_DEV, r, D_MODEL), F32),
         _sds((N_DEV, r, D_MODEL), WIRE)],
        [pltpu.VMEM((D_MODEL, D_MODEL), F32)], [dh1, ycat, ym, gmix, w_out])
    return dy, dgm, (dw, dw16)


def _in_bwd(dz, h0, dh1, g1, w_in, after, name):
    t = h0.shape[0]
    tm = _tile(t)

    def body(dz_ref, h_ref, dh1_ref, g_ref, w_ref, after_ref, dh0_ref, dg_ref):
        @pl.when(pl.program_id(0) == 0)
        def _():
            dg_ref[...] = jnp.zeros_like(dg_ref)

        dhn = _dot_nt(dz_ref[...], w_ref[...])
        _, xh, r = _rms_fwd(h_ref[...], g_ref[...])
        dx, dg = _rms_bwd(dhn, xh, r, g_ref[...])
        dg_ref[...] += dg
        dh0_ref[...] = dh1_ref[...] + dx

    tile = lambda w: pl.BlockSpec((tm, w), lambda i: (i, 0))
    (dh0, dg), _ = _call(
        body, name, (t // tm,),
        [tile(IN_W), tile(D_MODEL), tile(D_MODEL), _const_spec((1, D_MODEL)), _const_spec((D_MODEL, IN_W)),
         _const_spec((8, 128))],
        [tile(D_MODEL), _acc_spec((1, D_MODEL))], [_sds((t, D_MODEL), F32), _sds((1, D_MODEL), F32)],
        [], [dz, h0, dh1, g1, w_in, after])
    return dh0, dg


def _in_bwd_dw(dz, h0, dh1, hn1, g1, w_in, name):
    t = h0.shape[0]
    tm = _tile(t)
    nk = t // tm

    def body(dz_ref, h_ref, dh1_ref, hn_ref, g_ref, w_ref, dh0_ref, dg_ref, o_ref, o16_ref, acc):
        k = pl.program_id(0)

        @pl.when(k == 0)
        def _():
            dg_ref[...] = jnp.zeros_like(dg_ref)
            acc[...] = jnp.zeros_like(acc)

        dz_t = dz_ref[...]
        acc[...] += _dot_tn(hn_ref[...], dz_t)
        dhn = _dot_nt(dz_t, w_ref[...])
        _, xh, r = _rms_fwd(h_ref[...], g_ref[...])
        dx, dg = _rms_bwd(dhn, xh, r, g_ref[...])
        dg_ref[...] += dg
        dh0_ref[...] = dh1_ref[...] + dx

        @pl.when(k == nk - 1)
        def _():
            for d in range(N_DEV):
                v = acc[:, d * IN_SHARD:(d + 1) * IN_SHARD]
                o_ref[d] = v
                o16_ref[d] = v.astype(o16_ref.dtype)

    tile = lambda w: pl.BlockSpec((tm, w), lambda i: (i, 0))
    slabs = _const_spec((N_DEV, D_MODEL, IN_SHARD))
    (dh0, dg, dw, dw16), _ = _call(
        body, name, (nk,),
        [tile(IN_W), tile(D_MODEL), tile(D_MODEL), tile(D_MODEL), _const_spec((1, D_MODEL)),
         _const_spec((D_MODEL, IN_W))],
        [tile(D_MODEL), _acc_spec((1, D_MODEL)), slabs, slabs],
        [_sds((t, D_MODEL), F32), _sds((1, D_MODEL), F32), _sds((N_DEV, D_MODEL, IN_SHARD), F32),
         _sds((N_DEV, D_MODEL, IN_SHARD), WIRE)],
        [pltpu.VMEM((D_MODEL, IN_W), F32)], [dz, h0, dh1, hn1, g1, w_in])
    return dh0, dg, (dw, dw16)


def _loss_head(h, gf, target, name):
    t = h.shape[0]
    tm = _tile(t)

    def body(h_ref, g_ref, t_ref, dh_ref, loss_ref, dg_ref):
        @pl.when(pl.program_id(0) == 0)
        def _():
            loss_ref[...] = jnp.zeros_like(loss_ref)
            dg_ref[...] = jnp.zeros_like(dg_ref)

        g = g_ref[...]
        y, xh, r = _rms_fwd(h_ref[...], g)
        err = y - t_ref[...]
        part = 0.5 * jnp.sum(jnp.mean(err * err, axis=-1, keepdims=True), axis=0, keepdims=True)
        loss_ref[...] += jnp.broadcast_to(part, loss_ref.shape)
        dx, dg = _rms_bwd(err * (1.0 / D_MODEL), xh, r, g)
        dg_ref[...] += dg
        dh_ref[...] = dx

    tile = pl.BlockSpec((tm, D_MODEL), lambda i: (i, 0))
    (dh, loss, dg), _ = _call(
        body, name, (t // tm,), [tile, _const_spec((1, D_MODEL)), tile],
        [tile, _acc_spec((1, 128)), _acc_spec((1, D_MODEL))],
        [_sds((t, D_MODEL), F32), _sds((1, 128), F32), _sds((1, D_MODEL), F32)], [], [h, gf, target])
    return dh, loss, dg


def _dw(x, y, name, split, bm, bn):
    t, m = x.shape
    n = y.shape[1]
    tk = _tile(t, DW_TILE)
    nk = t // tk
    if split == "rows":
        assert bn == n
        r, c = m // N_DEV, n
        per = bm // r
        out_block = pl.BlockSpec((per, r, c), lambda a, b, k: (a, 0, 0))
    else:
        assert bm == m
        r, c = m, n // N_DEV
        per = bn // c
        out_block = pl.BlockSpec((per, r, c), lambda a, b, k: (b, 0, 0))

    def body(x_ref, y_ref, o_ref, o16_ref, acc):
        k = pl.program_id(2)

        @pl.when(k == 0)
        def _():
            acc[...] = jnp.zeros_like(acc)

        acc[...] += _dot_tn(x_ref[...], y_ref[...])

        @pl.when(k == nk - 1)
        def _():
            for d in range(per):
                v = acc[d * r:(d + 1) * r, :] if split == "rows" else acc[:, d * c:(d + 1) * c]
                o_ref[d] = v
                o16_ref[d] = v.astype(o16_ref.dtype)

    return pl.pallas_call(
        body, name=name, grid=(m // bm, n // bn, nk),
        in_specs=[pl.BlockSpec((tk, bm), lambda a, b, k: (k, a)), pl.BlockSpec((tk, bn), lambda a, b, k: (k, b))],
        out_specs=[out_block, out_block],
        out_shape=[_sds((N_DEV, r, c), F32), _sds((N_DEV, r, c), WIRE)],
        scratch_shapes=[pltpu.VMEM((bm, bn), F32)],
        compiler_params=pltpu.CompilerParams(dimension_semantics=("arbitrary",) * 3, vmem_limit_bytes=VMEM_LIMIT),
    )(x, y)


def _adamw_math(w, g, m, v):
    m = ADAM_B1 * m + (1.0 - ADAM_B1) * g
    v = ADAM_B2 * v + (1.0 - ADAM_B2) * jnp.square(g)
    m_hat = m / (1.0 - ADAM_B1 ** ADAM_STEP)
    v_hat = v / (1.0 - ADAM_B2 ** ADAM_STEP)
    delta = -ADAM_LR * (m_hat / (jnp.sqrt(v_hat) + ADAM_EPS) + ADAM_WD * w)
    return delta, m, v


def _adamw_shard(g_own, g_recv, dev, w, m, v, after, name):
    _, r, c = w.shape
    br = r
    for cand in (256, 128, 112, 64, 56, 32, 16, 8):
        if r % cand == 0:
            br = cand
            break
    nr = r // br
    own = lambda l: pl.BlockSpec((1, br, c), lambda ll, i, d: (d[0], jnp.where(ll == l, i, (nr - 1) * (1 - l)), 0))
    recv = lambda l: pl.BlockSpec((N_DEV - 1, br, c), lambda ll, i, d: (0, jnp.where(ll == l, i, (nr - 1) * (1 - l)), 0))

    def body(dev_ref, go0, gr0, go1, gr1, w_ref, m_ref, v_ref, after_ref, g_out, d_out, m_out, v_out):
        def update(go_ref, gr_ref):
            g = go_ref[0]
            for j in range(N_DEV - 1):
                g = g + gr_ref[j].astype(F32)
            delta, mn, vn = _adamw_math(w_ref[0], g, m_ref[0], v_ref[0])
            g_out[0] = g
            d_out[0] = delta
            m_out[0] = mn
            v_out[0] = vn

        layer = pl.program_id(0)
        pl.when(layer == 0)(lambda: update(go0, gr0))
        pl.when(layer == 1)(lambda: update(go1, gr1))

    tile = pl.BlockSpec((1, br, c), lambda ll, i, d: (ll, i, 0))
    return pl.pallas_call(
        body, name=name,
        grid_spec=pltpu.PrefetchScalarGridSpec(
            num_scalar_prefetch=1, grid=(2, nr),
            in_specs=[own(0), recv(0), own(1), recv(1), tile, tile, tile,
                      pl.BlockSpec((8, 128), lambda ll, i, d: (0, 0))],
            out_specs=[tile, tile, tile, tile]),
        out_shape=[_sds((2, r, c), F32)] * 4,
        compiler_params=pltpu.CompilerParams(dimension_semantics=("arbitrary",) * 2, vmem_limit_bytes=VMEM_LIMIT),
    )(dev, g_own[0], g_recv[0], g_own[1], g_recv[1], w, m, v, after)


def _adamw_small(gs, ws, ms, vs, name):
    n = len(gs)

    def body(*refs):
        g_refs, w_refs, m_refs, v_refs = (refs[k * n:(k + 1) * n] for k in range(4))
        outs = refs[4 * n:]
        for k in range(n):
            delta, mn, vn = _adamw_math(w_refs[k][...], g_refs[k][...], m_refs[k][...], v_refs[k][...])
            outs[k][...] = delta
            outs[n + k][...] = mn
            outs[2 * n + k][...] = vn

    shapes = [_sds(w.shape, F32) for w in ws]
    res = pl.pallas_call(body, name=name, out_shape=shapes * 3,
                         compiler_params=pltpu.CompilerParams(vmem_limit_bytes=VMEM_LIMIT))(*gs, *ws, *ms, *vs)
    return res[:n], res[n:2 * n], res[2 * n:]


def _sum_parts(part, dev, name):
    def body(dev_ref, p_ref, o_ref):
        me = dev_ref[0]
        g = p_ref[me]
        for d in range(1, N_DEV):
            g = g + p_ref[jnp.bitwise_xor(me, d)]
        o_ref[...] = g

    full = pl.BlockSpec(part.shape, lambda i, d: (0, 0, 0))
    return pl.pallas_call(
        body, name=name,
        grid_spec=pltpu.PrefetchScalarGridSpec(
            num_scalar_prefetch=1, grid=(1,), in_specs=[full],
            out_specs=pl.BlockSpec(part.shape[1:], lambda i, d: (0, 0))),
        out_shape=_sds(part.shape[1:], F32))(dev, part)


HBM = pl.BlockSpec(memory_space=pltpu.HBM)
SEM = pl.BlockSpec(memory_space=pltpu.SEMAPHORE)
EFFECT = pltpu.SideEffectType.DATAFLOW_SIDE_EFFECTING


def _direct_copies(srcs, lands, ssem, rsem, scatter):
    x, y, c = _me()
    out = []
    for a in range(len(srcs)):
        for f in range(1, N_DEV):
            px = 1 - x if f & 4 else x
            py = 1 - y if f & 2 else y
            pc = 1 - c if f & 1 else c
            out.append(pltpu.make_async_remote_copy(
                src_ref=srcs[a].at[4 * px + 2 * py + pc] if scatter else srcs[a], dst_ref=lands[a].at[f - 1],
                send_sem=ssem.at[7 * a + f - 1], recv_sem=rsem.at[7 * a + f - 1],
                device_id=(px, py, pc), device_id_type=MESH))
    return out


def _send_start(arrays, scatter, name):
    arrays = list(arrays)
    n = len(arrays)
    lands = [lax.empty((N_DEV - 1,) + (a.shape[1:] if scatter else a.shape), a.dtype) for a in arrays]

    def body(*refs):
        srcs, lnds, ssem, rsem, token = refs[:n], refs[n:2 * n], refs[2 * n], refs[2 * n + 1], refs[-1]
        for cp in _direct_copies(srcs, lnds, ssem, rsem, scatter):
            cp.start()
        token[...] = jnp.zeros_like(token)

    hbm = lambda a: pltpu.HBM(a.shape, a.dtype)
    res = pl.pallas_call(
        body, name=name,
        out_shape=(pltpu.SemaphoreType.DMA((7 * n,)), pltpu.SemaphoreType.DMA((7 * n,)),
                   *[hbm(a) for a in arrays + lands], _sds((8, 128), F32)),
        in_specs=[HBM] * (2 * n),
        out_specs=(SEM, SEM, *[HBM] * (2 * n), pl.BlockSpec(memory_space=pltpu.VMEM)),
        input_output_aliases={i: 2 + i for i in range(2 * n)},
        compiler_params=pltpu.CompilerParams(has_side_effects=EFFECT),
    )(*[pltpu.with_memory_space_constraint(a, pltpu.HBM) for a in arrays + lands])
    return types.SimpleNamespace(ssem=res[0], rsem=res[1], srcs=list(res[2:2 + n]), lands=list(res[2 + n:2 + 2 * n]),
                                 token=res[-1], scatter=scatter)


def _send_wait(h, after, name):
    n = len(h.srcs)

    def body(*refs):
        srcs, lnds, ssem, rsem = refs[:n], refs[n:2 * n], refs[2 * n], refs[2 * n + 1]
        for cp in _direct_copies(srcs, lnds, ssem, rsem, h.scatter):
            cp.wait_send()
            cp.wait_recv()

    hbm = lambda a: pltpu.HBM(a.shape, a.dtype)
    res = pl.pallas_call(
        body, name=name,
        out_shape=tuple(hbm(a) for a in h.srcs + h.lands),
        in_specs=[HBM] * (2 * n) + [SEM, SEM, ANY], out_specs=[HBM] * (2 * n),
        input_output_aliases={i: i for i in range(2 * n)},
        compiler_params=pltpu.CompilerParams(has_side_effects=EFFECT),
    )(*h.srcs, *h.lands, h.ssem, h.rsem, after)
    return list(res[:n]), list(res[n:])


def _block_diag(w):
    out = jnp.zeros((LRU_W, LRU_W), w.dtype)
    for h in range(4):
        out = lax.dynamic_update_slice(out, w[h], (h * 64, h * 64))
    return out


def _unblock_diag(w):
    return jnp.concatenate([w[h * 64:(h + 1) * 64, h * 64:(h + 1) * 64] for h in range(4)], axis=0)


def _layer_params(p, l):
    row = lambda a: a[l].reshape(1, -1)
    sink_rows = jnp.repeat(p["attn_sinks"][l].reshape(4, 2), 2 * BLK, axis=1)
    sink_rows = jnp.concatenate([sink_rows, jnp.zeros((4, 4 * BLK), F32)], axis=0)
    cw = jnp.concatenate([p["conv_dw_w"][l], jnp.zeros((1, CONV_W), F32)], axis=0)
    pv = jnp.concatenate([
        row(p["conv_dw_b"]), row(p["conv_ln_g"]), row(p["conv_ln_b"]), row(p["lru_conv_b"]), row(p["lru_ba"]),
        row(p["lru_bx"]), row(p["lru_lambda"]), jnp.zeros((1, LRU_W), F32), p["lru_conv_w"][l],
        jnp.zeros((4, LRU_W), F32)], axis=0)
    return dict(
        g1=row(p["norm1"]), sink=sink_rows, cw=cw, pv=pv,
        wa=_block_diag(p["lru_wa"][l]).astype(MX), wx=_block_diag(p["lru_wx"][l]).astype(MX),
        gmix=row(p["mix_norm"]), g2=row(p["norm2"]))


_SMALL = ["norm1", "attn_sinks", "conv_dw_w", "conv_dw_b", "conv_ln_g", "conv_ln_b", "lru_conv_w", "lru_conv_b",
          "lru_wa", "lru_ba", "lru_wx", "lru_bx", "lru_lambda", "mix_norm", "norm2"]
_BIG = ["w_in", "w_out", "w_up", "w_down"]
_WEIGHTS = ["norm1", "w_in", "attn_sinks", "conv_dw_w", "conv_dw_b", "conv_ln_g", "conv_ln_b", "lru_conv_w",
            "lru_conv_b", "lru_wa", "lru_ba", "lru_wx", "lru_bx", "lru_lambda", "mix_norm", "w_out", "norm2", "w_up",
            "w_down", "final_norm"]


def kernel(x, norm1, w_in, attn_sinks, conv_dw_w, conv_dw_b, conv_ln_g, conv_ln_b, lru_conv_w, lru_conv_b, lru_wa, lru_ba, lru_wx, lru_bx, lru_lambda, mix_norm, w_out, norm2, w_up, w_down, final_norm, loss_target, m_norm1, m_w_in, m_attn_sinks, m_conv_dw_w, m_conv_dw_b, m_conv_ln_g, m_conv_ln_b, m_lru_conv_w, m_lru_conv_b, m_lru_wa, m_lru_ba, m_lru_wx, m_lru_bx, m_lru_lambda, m_mix_norm, m_w_out, m_norm2, m_w_up, m_w_down, m_final_norm, v_norm1, v_w_in, v_attn_sinks, v_conv_dw_w, v_conv_dw_b, v_conv_ln_g, v_conv_ln_b, v_lru_conv_w, v_lru_conv_b, v_lru_wa, v_lru_ba, v_lru_wx, v_lru_bx, v_lru_lambda, v_mix_norm, v_w_out, v_norm2, v_w_up, v_w_down, v_final_norm):
    w = dict(norm1=norm1, w_in=w_in, attn_sinks=attn_sinks, conv_dw_w=conv_dw_w, conv_dw_b=conv_dw_b,
             conv_ln_g=conv_ln_g, conv_ln_b=conv_ln_b, lru_conv_w=lru_conv_w, lru_conv_b=lru_conv_b, lru_wa=lru_wa,
             lru_ba=lru_ba, lru_wx=lru_wx, lru_bx=lru_bx, lru_lambda=lru_lambda, mix_norm=mix_norm, w_out=w_out,
             norm2=norm2, w_up=w_up, w_down=w_down, final_norm=final_norm)
    m = dict(norm1=m_norm1, w_in=m_w_in, attn_sinks=m_attn_sinks, conv_dw_w=m_conv_dw_w, conv_dw_b=m_conv_dw_b,
             conv_ln_g=m_conv_ln_g, conv_ln_b=m_conv_ln_b, lru_conv_w=m_lru_conv_w, lru_conv_b=m_lru_conv_b,
             lru_wa=m_lru_wa, lru_ba=m_lru_ba, lru_wx=m_lru_wx, lru_bx=m_lru_bx, lru_lambda=m_lru_lambda,
             mix_norm=m_mix_norm, w_out=m_w_out, norm2=m_norm2, w_up=m_w_up, w_down=m_w_down, final_norm=m_final_norm)
    v = dict(norm1=v_norm1, w_in=v_w_in, attn_sinks=v_attn_sinks, conv_dw_w=v_conv_dw_w, conv_dw_b=v_conv_dw_b,
             conv_ln_g=v_conv_ln_g, conv_ln_b=v_conv_ln_b, lru_conv_w=v_lru_conv_w, lru_conv_b=v_lru_conv_b,
             lru_wa=v_lru_wa, lru_ba=v_lru_ba, lru_wx=v_lru_wx, lru_bx=v_lru_bx, lru_lambda=v_lru_lambda,
             mix_norm=v_mix_norm, w_out=v_w_out, norm2=v_norm2, w_up=v_w_up, w_down=v_w_down, final_norm=v_final_norm)
    depth = w_in.shape[0]
    xi, yi, ci = _me()
    dev = (4 * xi + 2 * yi + ci).astype(jnp.int32)
    dev1 = dev.reshape(1)
    wb = {n: w[n].astype(MX) for n in _BIG}
    layer_shards = lambda l: [wb["w_out"][l], wb["w_up"][l], wb["w_down"][l]]

    _, ((g_in0, g_cw, g_lcw),) = _call(None, "gather_first", None, [], [], [], [], [],
                                        [_gather_rider([wb["w_in"][0], conv_dw_w, lru_conv_w])])
    cols = lambda g: jnp.moveaxis(g, 0, -2).reshape(g.shape[1:-1] + (N_DEV * g.shape[-1],))
    p = dict(w)
    p["conv_dw_w"] = cols(g_cw)
    p["lru_conv_w"] = cols(g_lcw)
    lp = [_layer_params(p, l) for l in range(depth)]

    gathered = [dict(w_in=cols(g_in0)), dict()]
    saved = []
    h = x[0]
    for l in range(depth):
        q, gw = lp[l], gathered[l]
        z, hn1 = _ln_in(h, q["g1"], gw["w_in"], f"ln_in{l}")
        riders = [_gather_rider(layer_shards(0))] if l == 0 else []
        (ycat, hl, uc, probs, psinks), got = _mixer_fwd(z, q["sink"], q["cw"], q["pv"], q["wa"], q["wx"],
                                                        f"mixer_fwd{l}", riders)
        if l == 0:
            gw["w_out"], gw["w_up"], gw["w_down"] = got[0]
            gw["w_out"] = gw["w_out"].reshape(D_MODEL, D_MODEL)
        riders = [_gather_rider([wb["w_in"][1]] + layer_shards(1))] if l == 0 else []
        (h1, act, h2, ym, hn2), got = _post_fwd(ycat, h, q["gmix"], gw["w_out"], q["g2"], gw["w_up"],
                                                gw["w_down"].reshape(D_FF, D_MODEL), f"post_fwd{l}", riders)
        if l == 0:
            nxt = gathered[1]
            nxt["w_in"], nxt["w_out"], nxt["w_up"], nxt["w_down"] = got[0]
            nxt["w_in"] = cols(nxt["w_in"])
            nxt["w_out"] = nxt["w_out"].reshape(D_MODEL, D_MODEL)
        saved.append(dict(h0=h, z=z, hn1=hn1, ycat=ycat, hl=hl, uc=uc, probs=probs, psinks=psinks, h1=h1, act=act,
                          ym=ym, hn2=hn2))
        h = h2
    dh, loss, dgf = _loss_head(h, final_norm.reshape(1, -1), loss_target[0], "loss_head")

    grads = [None] * depth
    big = {n: [None] * depth for n in _BIG}
    pending = []

    def send_pending():
        riders = [_scatter_rider([item[3] for item in pending])] if pending else []
        return riders, list(pending)

    def record(sent, got):
        for item, recv in zip(sent, got[0] if sent else []):
            big[item[0]][item[1]] = (item[2], recv)
        del pending[:len(sent)]

    for l in reversed(range(depth)):
        q, s, gw = lp[l], saved[l], gathered[l]
        riders, sent = send_pending()
        w_up_t = jnp.swapaxes(gw["w_up"], 1, 2).reshape(D_FF, D_MODEL)
        (dh1, dh1b, dhb, du, dg2), got = _ffn_bwd(dh, s["act"], s["h1"], q["g2"], w_up_t, gw["w_down"],
                                                  f"ffn_bwd{l}", riders)
        record(sent, got)
        dycat, dgm, d_wout = _mix_bwd(dh1b, s["ycat"], s["ym"], q["gmix"], gw["w_out"], f"mix_bwd{l}")
        pending.append(("w_down", l) + tuple(_dw(s["act"], dhb, f"dw_down{l}", "rows", 2048, D_MODEL)))
        pending.append(("w_up", l) + tuple(_dw(s["hn2"], du, f"dw_up{l}", "cols", D_MODEL, 2048)))
        pending.append(("w_out", l) + tuple(d_wout))
        riders, sent = send_pending()
        (dz, dsink, dcw, dpv, dwa, dwx), got = _mixer_bwd(
            dycat, s["z"], s["ycat"], s["hl"], s["uc"], s["probs"], s["psinks"], q["sink"], q["cw"], q["pv"], q["wa"],
            q["wx"], f"mixer_bwd{l}", riders)
        record(sent, got)
        if l > 0:
            dh, dg1, d_win = _in_bwd_dw(dz, s["h0"], dh1, s["hn1"], q["g1"], gw["w_in"], f"in_bwd{l}")
            pending.append(("w_in", l) + tuple(d_win))
        else:
            d_win = _dw(s["hn1"], dz, f"dw_in{l}", "cols", D_MODEL, IN_W)
            win_sends = _send_start([d_win[1]], True, "scatter_w_in0_start")
            dh, dg1 = _in_bwd(dz, s["h0"], dh1, q["g1"], gw["w_in"], win_sends.token, f"in_bwd{l}")
        grads[l] = dict(
            norm1=dg1[0], attn_sinks=jnp.stack([dsink[0:4, 0], dsink[0:4, 2 * BLK]], axis=1).reshape(8),
            conv_dw_w=dcw[0:CONV_K], conv_dw_b=dpv[R_CONV_B], conv_ln_g=dpv[R_LN_G], conv_ln_b=dpv[R_LN_B],
            lru_conv_w=dpv[R_LCW:R_LCW + LRU_K], lru_conv_b=dpv[R_LCONV_B], lru_wa=_unblock_diag(dwa),
            lru_ba=dpv[R_BA].reshape(4, 64), lru_wx=_unblock_diag(dwx), lru_bx=dpv[R_BX].reshape(4, 64),
            lru_lambda=dpv[R_LAM], mix_norm=dgm[0], norm2=dg2[0])

    small = [jnp.stack([grads[l][n] for l in range(depth)]) for n in _SMALL] + [dgf, loss[:, 0:1]]
    sizes = [a.size for a in small]
    total = -(-sum(sizes) // 1024) * 1024
    packed = jnp.concatenate([a.reshape(-1) for a in small] + [jnp.zeros((total - sum(sizes),), F32)])
    packed = packed.reshape(total // 128, 128)
    small_sends = _send_start([packed], False, "bcast_small_start")

    out = {}
    shard_update = lambda n, after: list(_adamw_shard(
        [big[n][l][0] for l in range(depth)], [big[n][l][1] for l in range(depth)], dev1, w[n], m[n], v[n], after,
        f"adamw_{n}"))
    for n in ("w_out", "w_up", "w_down"):
        out[n] = shard_update(n, small_sends.token)
    _, (win_recv,) = _send_wait(win_sends, out["w_down"][1], "scatter_w_in0_wait")
    big["w_in"][0] = (d_win[0], win_recv)
    (packed,), (small_recv,) = _send_wait(small_sends, win_recv, "bcast_small_wait")
    out["w_in"] = shard_update("w_in", jnp.zeros((8, 128), F32))
    parts = jnp.concatenate([packed[None], small_recv], axis=0)
    summed = _sum_parts(parts, dev1, "sum_small_grads").reshape(-1)
    small_sums, pos = [], 0
    for a, size in zip(small, sizes):
        small_sums.append(summed[pos:pos + size].reshape(a.shape))
        pos += size
    shard = lambda a: lax.dynamic_slice_in_dim(a, dev * (a.shape[-1] // N_DEV), a.shape[-1] // N_DEV, axis=a.ndim - 1)
    flat = {"lru_wa": (depth, LRU_W, 64), "lru_wx": (depth, LRU_W, 64), "final_norm": (1, D_MODEL)}
    gs, ws, ms, vs = [], [], [], []
    for n, g in zip(_SMALL + ["final_norm"], small_sums[:-1]):
        shp = flat.get(n, w[n].shape)
        gs.append((shard(g) if n in ("conv_dw_w", "lru_conv_w") else g).reshape(shp))
        ws.append(w[n].reshape(shp))
        ms.append(m[n].reshape(shp))
        vs.append(v[n].reshape(shp))
    sd, sm, sv = _adamw_small(gs, ws, ms, vs, "adamw_small")
    for j, n in enumerate(_SMALL + ["final_norm"]):
        out[n] = [a.reshape(w[n].shape) for a in (gs[j], sd[j], sm[j], sv[j])]
    loss_total = small_sums[-1][0, 0]

    result = [loss_total, dh[None]]
    for j in range(4):
        result += [out[n][j] for n in _WEIGHTS]
    return tuple(result)
```

```python
import types

import jax
import jax.numpy as jnp
from jax import lax
from jax.experimental import pallas as pl
from jax.experimental.pallas import tpu as pltpu

F32 = jnp.float32
MX = jnp.bfloat16
WIRE = jnp.bfloat16

D_MODEL = 1024
HEAD_DIM = 64
ATTN_W = 512
KV_W = 128
BLK = 128
CONV_W = 256
CONV_K = 31
LRU_W = 256
LRU_K = 4
LRU_C = 8.0
IN_W = 1792
D_FF = 4096
FF_BLK = 512
N_DEV = 8
IN_SHARD = IN_W // N_DEV
RMS_EPS = 1e-6
LN_EPS = 1e-5
MASK_VALUE = -1e30
SCALE = HEAD_DIM ** -0.5
CONV_HALO = 32
LRU_HALO = 8
CONV_CHUNK = 64
POST_TILE = 512
DW_TILE = 2048
Q0, K0, V0, CV0, CG0, RX0, RG0 = 0, 512, 640, 768, 1024, 1280, 1536
R_CONV_B, R_LN_G, R_LN_B, R_LCONV_B, R_BA, R_BX, R_LAM, R_LCW = 0, 1, 2, 3, 4, 5, 6, 8

ADAM_LR, ADAM_B1, ADAM_B2, ADAM_EPS, ADAM_WD, ADAM_STEP = 0.001, 0.9, 0.999, 1e-08, 0.01, 10

VMEM_LIMIT = 56 * 1024 * 1024
MESH = pl.DeviceIdType.MESH
ANY = pl.BlockSpec(memory_space=pl.ANY)


def _tile(t, cap=512):
    return min(cap, t)


def _dot(a, b):
    return jnp.dot(a.astype(MX), b.astype(MX), preferred_element_type=F32)


def _dot_nt(a, b):
    return lax.dot_general(a.astype(MX), b.astype(MX), (((1,), (1,)), ((), ())), preferred_element_type=F32)


def _dot_tn(a, b):
    return lax.dot_general(a.astype(MX), b.astype(MX), (((0,), (0,)), ((), ())), preferred_element_type=F32)


def _const_spec(shape):
    nd = len(shape)
    return pl.BlockSpec(shape, lambda *_: (0,) * nd, pipeline_mode=pl.Buffered(1))


def _acc_spec(shape):
    nd = len(shape)
    return pl.BlockSpec(shape, lambda *_: (0,) * nd)


def _sds(shape, dtype):
    return jax.ShapeDtypeStruct(shape, dtype)


def _sigmoid(x):
    return jax.nn.sigmoid(x)


def _rms_fwd(x, g):
    r = lax.rsqrt(jnp.mean(x * x, axis=-1, keepdims=True) + RMS_EPS)
    xh = x * r
    return xh * g, xh, r


def _rms_bwd(dy, xh, r, g):
    t = dy * g
    dx = r * (t - xh * jnp.mean(t * xh, axis=-1, keepdims=True))
    return dx, jnp.sum(dy * xh, axis=0, keepdims=True)


_GROUPS = ((0, 512), (512, 768), (768, 1024))


def _group_rms_fwd(y, g):
    parts = [_rms_fwd(y[:, a:b], g[:, a:b]) for a, b in _GROUPS]
    return (jnp.concatenate([p[0] for p in parts], axis=1),
            jnp.concatenate([p[1] for p in parts], axis=1),
            [p[2] for p in parts])


def _gelu(x):
    c = 0.7978845608028654
    u = c * (x + 0.044715 * x * x * x)
    th = jnp.tanh(u)
    val = 0.5 * x * (1.0 + th)
    grad = 0.5 * (1.0 + th) + 0.5 * x * (1.0 - th * th) * c * (1.0 + 3.0 * 0.044715 * x * x)
    return val, grad


def _neg_expm1(x):
    series = -x * (1.0 + x * (0.5 + x * (1.0 / 6.0 + x * (1.0 / 24.0))))
    return jnp.where(x > -0.02, series, 1.0 - jnp.exp(x))


def _me():
    return lax.axis_index("x"), lax.axis_index("y"), lax.axis_index("c")


def _gather_rider(arrays):
    arrays = list(arrays)
    n = len(arrays)

    def plan(ins, outs, sems):
        ssem, rsem, lsem = sems
        x, y, c = _me()
        chips = [(1 - x, y), (x, 1 - y), (1 - x, 1 - y)]

        def copy(a, k, block, to, own=False):
            dst = outs[a].at[4 * block[0] + 2 * block[1] + block[2]]
            return pltpu.make_async_remote_copy(
                src_ref=ins[a] if own else dst, dst_ref=dst, send_sem=ssem.at[7 * a + k],
                recv_sem=rsem.at[7 * a + k], device_id=to, device_id_type=MESH)

        return x, y, c, chips, copy, lsem

    def start(ins, outs, sems):
        x, y, c, chips, copy, lsem = plan(ins, outs, sems)
        for a in range(n):
            pltpu.make_async_copy(ins[a], outs[a].at[4 * x + 2 * y + c], lsem.at[a]).start()
            copy(a, 0, (x, y, c), (x, y, 1 - c), own=True).start()
            for j, chip in enumerate(chips):
                copy(a, 1 + j, (x, y, c), (*chip, c), own=True).start()

    def mid(ins, outs, sems):
        x, y, c, chips, copy, _ = plan(ins, outs, sems)
        for a in range(n):
            for j, chip in enumerate(chips):
                copy(a, 1 + j, (*chip, c), (x, y, c)).wait_recv()
                copy(a, 4 + j, (*chip, c), (x, y, 1 - c)).start()

    def finish(ins, outs, sems):
        x, y, c, chips, copy, lsem = plan(ins, outs, sems)
        for a in range(n):
            copy(a, 0, (x, y, 1 - c), (x, y, c)).wait_recv()
            for j, chip in enumerate(chips):
                copy(a, 4 + j, (*chip, 1 - c), (x, y, c)).wait_recv()
        for a in range(n):
            copy(a, 0, (x, y, c), (x, y, 1 - c), own=True).wait_send()
            for j, chip in enumerate(chips):
                copy(a, 1 + j, (x, y, c), (*chip, c), own=True).wait_send()
                copy(a, 4 + j, (*chip, c), (x, y, 1 - c)).wait_send()
            pltpu.make_async_copy(ins[a], outs[a].at[4 * x + 2 * y + c], lsem.at[a]).wait()

    return types.SimpleNamespace(
        arrays=arrays, out_shape=[_sds((N_DEV,) + a.shape, a.dtype) for a in arrays],
        scratch=[pltpu.SemaphoreType.DMA((7 * n,)), pltpu.SemaphoreType.DMA((7 * n,)), pltpu.SemaphoreType.DMA((n,))],
        start=start, mid=mid, finish=finish)


def _bcast_rider(arrays):
    arrays = list(arrays)
    n = len(arrays)

    def copies(ins, outs, sems, landing):
        ssem, rsem, lsem = sems
        x, y, c = _me()
        out = []
        for a in range(n):
            out.append(pltpu.make_async_copy(ins[a], outs[a].at[4 * x + 2 * y + c], lsem.at[a]))
            for f in range(1, N_DEV):
                px = 1 - x if f & 4 else x
                py = 1 - y if f & 2 else y
                pc = 1 - c if f & 1 else c
                slot = 4 * px + 2 * py + pc if landing else 4 * x + 2 * y + c
                out.append(pltpu.make_async_remote_copy(
                    src_ref=ins[a], dst_ref=outs[a].at[slot], send_sem=ssem.at[7 * a + f - 1],
                    recv_sem=rsem.at[7 * a + f - 1], device_id=(px, py, pc), device_id_type=MESH))
        return out

    def start(ins, outs, sems):
        for cp in copies(ins, outs, sems, landing=False):
            cp.start()

    def finish(ins, outs, sems):
        for cp in copies(ins, outs, sems, landing=True):
            cp.wait()

    return types.SimpleNamespace(
        arrays=arrays, out_shape=[_sds((N_DEV,) + a.shape, a.dtype) for a in arrays],
        scratch=[pltpu.SemaphoreType.DMA((7 * n,)), pltpu.SemaphoreType.DMA((7 * n,)), pltpu.SemaphoreType.DMA((n,))],
        start=start, mid=None, finish=finish)


def _scatter_rider(arrays):
    arrays = list(arrays)
    n = len(arrays)

    def copies(ins, outs, sems):
        ssem, rsem = sems
        x, y, c = _me()
        out = []
        for a in range(n):
            for f in range(1, N_DEV):
                px = 1 - x if f & 4 else x
                py = 1 - y if f & 2 else y
                pc = 1 - c if f & 1 else c
                out.append(pltpu.make_async_remote_copy(
                    src_ref=ins[a].at[4 * px + 2 * py + pc], dst_ref=outs[a].at[f - 1], send_sem=ssem.at[7 * a + f - 1],
                    recv_sem=rsem.at[7 * a + f - 1], device_id=(px, py, pc), device_id_type=MESH))
        return out

    def start(ins, outs, sems):
        for cp in copies(ins, outs, sems):
            cp.start()

    def finish(ins, outs, sems):
        for cp in copies(ins, outs, sems):
            cp.wait()

    return types.SimpleNamespace(
        arrays=arrays, out_shape=[_sds((N_DEV - 1,) + a.shape[1:], a.dtype) for a in arrays],
        scratch=[pltpu.SemaphoreType.DMA((7 * n,)), pltpu.SemaphoreType.DMA((7 * n,))],
        start=start, mid=None, finish=finish)


def _call(body, name, grid, in_specs, out_specs, out_shape, scratch, operands, riders=()):
    n_in, n_out, n_scr = len(operands), len(out_shape), len(scratch)
    nsteps = grid[0] if grid else 1
    sizes = [(len(r.arrays), len(r.out_shape), len(r.scratch)) for r in riders]

    def wrapped(*refs):
        pos = n_in
        r_ins = []
        for ri, _, _ in sizes:
            r_ins.append(refs[pos:pos + ri])
            pos += ri
        outs = refs[pos:pos + n_out]
        pos += n_out
        r_outs = []
        for _, ro, _ in sizes:
            r_outs.append(refs[pos:pos + ro])
            pos += ro
        scr = refs[pos:pos + n_scr]
        pos += n_scr
        r_sems = []
        for _, _, rs in sizes:
            r_sems.append(refs[pos:pos + rs])
            pos += rs
        step = pl.program_id(0) if grid else 0

        def at(s, fn):
            if grid:
                pl.when(step == s)(fn)
            else:
                fn()

        for r, a, b, c in zip(riders, r_ins, r_outs, r_sems):
            at(0, lambda r=r, a=a, b=b, c=c: r.start(a, b, c))
        for r, a, b, c in zip(riders, r_ins, r_outs, r_sems):
            if r.mid is not None:
                at((3 * nsteps) // 4, lambda r=r, a=a, b=b, c=c: r.mid(a, b, c))
        if body is not None:
            body(*refs[:n_in], *outs, *scr)
        for r, a, b, c in zip(riders, r_ins, r_outs, r_sems):
            at(nsteps - 1, lambda r=r, a=a, b=b, c=c: r.finish(a, b, c))

    r_arrays = [a for r in riders for a in r.arrays]
    r_shapes = [s for r in riders for s in r.out_shape]
    kwargs = {}
    if grid:
        kwargs = dict(grid=grid, compiler_params=pltpu.CompilerParams(
            dimension_semantics=("arbitrary",) * len(grid), vmem_limit_bytes=VMEM_LIMIT))
    res = pl.pallas_call(
        wrapped, name=name,
        in_specs=list(in_specs) + [ANY] * len(r_arrays),
        out_specs=list(out_specs) + [ANY] * len(r_shapes),
        out_shape=list(out_shape) + r_shapes,
        scratch_shapes=list(scratch) + [s for r in riders for s in r.scratch],
        **kwargs,
    )(*operands, *r_arrays)
    host, rest = res[:n_out], res[n_out:]
    r_res = []
    for _, ro, _ in sizes:
        r_res.append(rest[:ro])
        rest = rest[ro:]
    return host, r_res


def _ln_in(h, g1, w_in, name):
    t = h.shape[0]
    tm = _tile(t)

    def body(h_ref, g_ref, w_ref, z_ref, hn_ref):
        y, _, _ = _rms_fwd(h_ref[...], g_ref[...])
        hn = y.astype(MX)
        hn_ref[...] = hn
        z_ref[...] = jnp.dot(hn, w_ref[...], preferred_element_type=F32)

    tile = lambda w: pl.BlockSpec((tm, w), lambda i: (i, 0))
    (z, hn), _ = _call(
        body, name, (t // tm,),
        [tile(D_MODEL), _const_spec((1, D_MODEL)), _const_spec((D_MODEL, IN_W))],
        [tile(IN_W), tile(D_MODEL)], [_sds((t, IN_W), F32), _sds((t, D_MODEL), MX)], [], [h, g1, w_in])
    return z, hn


def _band2(kb, g):
    lo = lax.broadcasted_iota(jnp.int32, kb.shape, 1) < HEAD_DIM
    kr = pltpu.roll(kb, HEAD_DIM, 1)
    if g == 0:
        top, bot = jnp.where(lo, kb, 0.0), jnp.where(lo, 0.0, kr)
    else:
        top, bot = jnp.where(lo, kr, 0.0), jnp.where(lo, 0.0, kb)
    return jnp.concatenate([top, bot], axis=0)


def _attn_operands(z_ref, zh_ref, b):
    rows = slice(b * BLK, (b + 1) * BLK)
    prev = zh_ref if b == 0 else z_ref
    prow = slice(0, BLK) if b == 0 else slice((b - 1) * BLK, b * BLK)
    kb = jnp.concatenate([prev[prow, K0:K0 + KV_W], z_ref[rows, K0:K0 + KV_W]], axis=0)
    vb = jnp.concatenate([prev[prow, V0:V0 + KV_W], z_ref[rows, V0:V0 + KV_W]], axis=0)
    k2 = [_band2(kb, g) for g in range(2)]
    v2 = [_band2(vb, g) for g in range(2)]
    q2 = [jnp.concatenate([z_ref[rows, (2 * g) * BLK:(2 * g + 1) * BLK], z_ref[rows, (2 * g + 1) * BLK:(2 * g + 2) * BLK]],
                          axis=0) for g in range(2)]
    return q2, k2, v2


def _attn_block(z_ref, zh_ref, sink_ref, b, first):
    q2, k2, v2 = _attn_operands(z_ref, zh_ref, b)
    rr = lax.broadcasted_iota(jnp.int32, (4 * BLK, 2 * BLK), 0) & (BLK - 1)
    cc = lax.broadcasted_iota(jnp.int32, (4 * BLK, 2 * BLK), 1)
    first_block = jnp.logical_and(first, b == 0).astype(jnp.int32)
    mask = jnp.logical_and(jnp.logical_and(cc > rr, cc <= rr + BLK), cc >= BLK * first_block)
    s = jnp.concatenate([_dot_nt(q2[g], k2[g]) for g in range(2)], axis=0) * SCALE
    w = 2 * BLK
    out, psink = [], []
    for hh in range(2):
        sh = jnp.where(mask, s[:, hh * w:(hh + 1) * w], MASK_VALUE)
        sk = jnp.concatenate([jnp.broadcast_to(sink_ref[p:p + 1, hh * w:hh * w + 1], (BLK, 1)) for p in range(4)], axis=0)
        m = jnp.maximum(jnp.max(sh, axis=1, keepdims=True), sk)
        p = jnp.exp(sh - m)
        es = jnp.exp(sk - m)
        inv = 1.0 / (jnp.sum(p, axis=1, keepdims=True) + es)
        out.append(p * inv)
        psink.append(es * inv)
    return v2, jnp.concatenate(out, axis=1), psink


def _scan_steps(a, b, n, span, reverse):
    pos = lax.broadcasted_iota(jnp.int32, a.shape, 0) & (span - 1)
    d = 1
    while d < span:
        keep = pos < span - d if reverse else pos >= d
        shift = n - d if reverse else d
        a_sh = jnp.where(keep, pltpu.roll(a, shift, 0), 1.0)
        b_sh = jnp.where(keep, pltpu.roll(b, shift, 0), 0.0)
        b = a * b_sh + b
        a = a * a_sh
        d *= 2
    return a, b


def _scan(a, b, tm, reverse):
    return _scan_steps(a, b, tm, tm, reverse)


def _shifted_copies(ext, shifts, tm):
    rows = tm + CONV_HALO - 8
    for r in range(1, 8):
        shifts[r - 1, 0:rows, :] = ext[pl.ds(r, rows), :]


def _tap(ext, shifts, off, r0, n):
    a, r = divmod(off, 8)
    lo = 8 * a + r0
    if r == 0:
        return ext[lo:lo + n, :]
    return shifts[r - 1, lo:lo + n, :]


def _glu_fill(z_ref, zh_ref, uext, ush, first, tm, sg_out=None):
    cv = z_ref[:, CV0:CV0 + CONV_W]
    sg = _sigmoid(z_ref[:, CG0:CG0 + CONV_W])
    if sg_out is not None:
        sg_out[...] = sg
    hrow = BLK - CONV_HALO
    uh = zh_ref[hrow:BLK, CV0:CV0 + CONV_W] * _sigmoid(zh_ref[hrow:BLK, CG0:CG0 + CONV_W])
    uext[0:CONV_HALO, :] = jnp.where(first, 0.0, uh)
    uext[CONV_HALO:CONV_HALO + tm, :] = cv * sg
    _shifted_copies(uext, ush, tm)


def _conv_taps(cw_ref, pv_ref, uext, ush, out_ref, tm):
    for r0 in range(0, tm, CONV_CHUNK):
        acc = jnp.broadcast_to(pv_ref[R_CONV_B:R_CONV_B + 1, :], (CONV_CHUNK, CONV_W))
        for k in range(CONV_K):
            acc = acc + cw_ref[k:k + 1, :] * _tap(uext, ush, CONV_HALO - (CONV_K - 1) + k, r0, CONV_CHUNK)
        out_ref[r0:r0 + CONV_CHUNK, :] = acc


def _ln_silu(uc, pv_ref):
    mu = jnp.mean(uc, axis=-1, keepdims=True)
    xc = uc - mu
    rs = lax.rsqrt(jnp.mean(xc * xc, axis=-1, keepdims=True) + LN_EPS)
    xh = xc * rs
    ln = xh * pv_ref[R_LN_G:R_LN_G + 1, :] + pv_ref[R_LN_B:R_LN_B + 1, :]
    sg = _sigmoid(ln)
    return xh, rs, ln, sg


def _lru_gates(z_ref, zh_ref, pv_ref, wa_ref, wx_ref, rxext, first, tm):
    rxext[0:LRU_HALO, :] = jnp.where(first, 0.0, zh_ref[BLK - LRU_HALO:BLK, RX0:RX0 + LRU_W])
    rxext[LRU_HALO:LRU_HALO + tm, :] = z_ref[:, RX0:RX0 + LRU_W]
    xc = jnp.broadcast_to(pv_ref[R_LCONV_B:R_LCONV_B + 1, :], (tm, LRU_W))
    for k in range(LRU_K):
        xc = xc + pv_ref[R_LCW + k:R_LCW + k + 1, :] * rxext[pl.ds(LRU_HALO - (LRU_K - 1) + k, tm), :]
    r = _sigmoid(_dot(xc, wa_ref[...]) + pv_ref[R_BA:R_BA + 1, :])
    ig = _sigmoid(_dot(xc, wx_ref[...]) + pv_ref[R_BX:R_BX + 1, :])
    lam = pv_ref[R_LAM:R_LAM + 1, :]
    sp = jnp.log1p(jnp.exp(-lam))
    la = (-LRU_C * r) * sp
    a = jnp.exp(la)
    mult = jnp.sqrt(_neg_expm1(2.0 * la))
    return xc, r, ig, sp, la, a, mult


def _mixer_in_specs(tm, tile_of):
    hb = tm // BLK
    return [
        pl.BlockSpec((tm, IN_W), lambda i: (tile_of(i), 0)),
        pl.BlockSpec((BLK, IN_W), lambda i: (jnp.maximum(tile_of(i) * hb - 1, 0), 0)),
        _const_spec((8, 4 * BLK)),
        _const_spec((32, CONV_W)),
        _const_spec((16, CONV_W)),
        _const_spec((LRU_W, LRU_W)),
        _const_spec((LRU_W, LRU_W)),
    ]


def _mixer_fwd(z, sink, cw, pv, wa, wx, name, riders=()):
    t = z.shape[0]
    tm = _tile(t)
    nb = tm // BLK

    def body(z_ref, zh_ref, sink_ref, cw_ref, pv_ref, wa_ref, wx_ref, y_ref, hl_ref, uc_ref, p_ref, ps_ref,
             uext, ush, rxext, hcar):
        i = pl.program_id(0)
        first = i == 0

        @pl.when(first)
        def _():
            hcar[...] = jnp.zeros_like(hcar)

        lo = lax.broadcasted_iota(jnp.int32, (4 * BLK, BLK), 1) < HEAD_DIM
        for b in range(nb):
            rows = slice(b * BLK, (b + 1) * BLK)
            v2, prob, psink = _attn_block(z_ref, zh_ref, sink_ref, b, first)
            prob = prob.astype(MX)
            p_ref[b] = prob
            ps_ref[b] = jnp.where(lo, psink[0], psink[1])
            for g in range(2):
                o = _dot(prob[2 * g * BLK:(2 * g + 2) * BLK], v2[g])
                y_ref[rows, (2 * g) * BLK:(2 * g + 1) * BLK] = o[0:BLK]
                y_ref[rows, (2 * g + 1) * BLK:(2 * g + 2) * BLK] = o[BLK:2 * BLK]
        _glu_fill(z_ref, zh_ref, uext, ush, first, tm)
        _conv_taps(cw_ref, pv_ref, uext, ush, uc_ref, tm)
        _, _, ln, sg = _ln_silu(uc_ref[...], pv_ref)
        y_ref[:, ATTN_W:ATTN_W + CONV_W] = ln * sg
        xc, _, ig, _, _, a, mult = _lru_gates(z_ref, zh_ref, pv_ref, wa_ref, wx_ref, rxext, first, tm)
        acum, h = _scan(a, mult * (ig * xc), tm, reverse=False)
        h = h + acum * hcar[0:1, :]
        hl_ref[...] = h
        hcar[0:1, :] = h[tm - 1:tm, :]
        gl, _ = _gelu(z_ref[:, RG0:RG0 + LRU_W])
        y_ref[:, ATTN_W + CONV_W:ATTN_W + CONV_W + LRU_W] = h * gl

    tile = lambda w: pl.BlockSpec((tm, w), lambda i: (i, 0))
    return _call(
        body, name, (t // tm,), _mixer_in_specs(tm, lambda i: i),
        [tile(D_MODEL), tile(LRU_W), tile(CONV_W), pl.BlockSpec((nb, 4 * BLK, 4 * BLK), lambda i: (i, 0, 0)),
         pl.BlockSpec((nb, 4 * BLK, BLK), lambda i: (i, 0, 0))],
        [_sds((t, D_MODEL), F32), _sds((t, LRU_W), F32), _sds((t, CONV_W), F32),
         _sds((t // BLK, 4 * BLK, 4 * BLK), MX), _sds((t // BLK, 4 * BLK, BLK), F32)],
        [pltpu.VMEM((tm + CONV_HALO, CONV_W), F32), pltpu.VMEM((7, tm + CONV_HALO - 8, CONV_W), F32),
         pltpu.VMEM((tm + LRU_HALO, LRU_W), F32), pltpu.VMEM((8, LRU_W), F32)],
        [z, z, sink, cw, pv, wa, wx], riders)


def _mixer_bwd(dy, z, ycat, hl, uc, probs, psinks, sink, cw, pv, wa, wx, name, riders=()):
    t = z.shape[0]
    tm = _tile(t)
    nt = t // tm
    nb = tm // BLK
    rev = lambda i: nt - 1 - i

    def body(dy_ref, z_ref, zh_ref, sink_ref, cw_ref, pv_ref, wa_ref, wx_ref, y_ref, hl_ref, hlh_ref, uc_ref,
             p_ref, ps_ref, dz_ref, dsink_ref, dcw_ref, dpv_ref, dwa_ref, dwx_ref,
             uext, ush, sgs, rxext, dkext, dvext, ducext, dsh, dcw8, dxcext, kcar, vcar, uccar, xccar, gcar):
        i = pl.program_id(0)
        first = i == nt - 1

        @pl.when(i == 0)
        def _():
            for car in (kcar, vcar, uccar, xccar, gcar, dcw8):
                car[...] = jnp.zeros_like(car)
            for acc in (dsink_ref, dpv_ref, dwa_ref, dwx_ref):
                acc[...] = jnp.zeros_like(acc)

        def addrow(r, val):
            dpv_ref[r:r + 1, :] += jnp.sum(val, axis=0, keepdims=True)

        dkext[:, 0:tm] = jnp.zeros((KV_W, tm), F32)
        dvext[:, 0:tm] = jnp.zeros((KV_W, tm), F32)
        dkext[:, tm:tm + BLK] = kcar[...]
        dvext[:, tm:tm + BLK] = vcar[...]
        lane512 = lax.broadcasted_iota(jnp.int32, (1, 4 * BLK), 1) < 2 * BLK
        lo = lax.broadcasted_iota(jnp.int32, (4 * BLK, BLK), 1) < HEAD_DIM
        hd, w2 = HEAD_DIM, 2 * BLK
        for b in range(nb):
            rows = slice(b * BLK, (b + 1) * BLK)
            band = slice(b * BLK, (b + 2) * BLK)
            q2, k2, v2 = _attn_operands(z_ref, zh_ref, b)
            prob = p_ref[b]
            psink = [ps_ref[b, :, 0:1], ps_ref[b, :, HEAD_DIM:HEAD_DIM + 1]]
            stack = lambda ref: jnp.concatenate([ref[rows, p * BLK:(p + 1) * BLK] for p in range(4)], axis=0)
            do4 = stack(dy_ref)
            dlt = do4 * stack(y_ref)
            d0 = jnp.sum(jnp.where(lo, dlt, 0.0), axis=1, keepdims=True)
            d1 = jnp.sum(jnp.where(lo, 0.0, dlt), axis=1, keepdims=True)
            dp = jnp.concatenate([_dot_nt(do4[g * w2:(g + 1) * w2], v2[g]) for g in range(2)], axis=0)
            dl = jnp.concatenate([jnp.broadcast_to(d0, (4 * BLK, w2)), jnp.broadcast_to(d1, (4 * BLK, w2))], axis=1)
            draw = (prob * (dp - dl)) * SCALE
            e0, e1 = psink[0] * d0, psink[1] * d1
            for p in range(4):
                prs = slice(p * BLK, (p + 1) * BLK)
                s0 = jnp.sum(e0[prs], axis=0, keepdims=True)
                s1 = jnp.sum(e1[prs], axis=0, keepdims=True)
                dsink_ref[p:p + 1, :] += -jnp.where(lane512, s0, s1)
            for g in range(2):
                grs = slice(g * w2, (g + 1) * w2)
                dq = _dot(draw[grs], k2[g])
                dz_ref[rows, (2 * g) * BLK:(2 * g + 1) * BLK] = dq[0:BLK].astype(dz_ref.dtype)
                dz_ref[rows, (2 * g + 1) * BLK:(2 * g + 2) * BLK] = dq[BLK:2 * BLK].astype(dz_ref.dtype)
                tk = _dot_tn(q2[g], draw[grs])
                tv = _dot_tn(do4[grs], prob[grs])
                dkext[g * hd:(g + 1) * hd, band] += tk[0:hd, 0:w2] + tk[hd:2 * hd, w2:2 * w2]
                dvext[g * hd:(g + 1) * hd, band] += tv[0:hd, 0:w2] + tv[hd:2 * hd, w2:2 * w2]
        dz_ref[:, K0:K0 + KV_W] = jnp.transpose(dkext[:, BLK:BLK + tm]).astype(dz_ref.dtype)
        dz_ref[:, V0:V0 + KV_W] = jnp.transpose(dvext[:, BLK:BLK + tm]).astype(dz_ref.dtype)
        kcar[...] = dkext[:, 0:BLK]
        vcar[...] = dvext[:, 0:BLK]

        _glu_fill(z_ref, zh_ref, uext, ush, first, tm, sg_out=sgs)
        xh, rs, ln, sg = _ln_silu(uc_ref[...], pv_ref)
        dln = dy_ref[:, ATTN_W:ATTN_W + CONV_W] * (sg * (1.0 + ln * (1.0 - sg)))
        addrow(R_LN_G, dln * xh)
        addrow(R_LN_B, dln)
        dxh = dln * pv_ref[R_LN_G:R_LN_G + 1, :]
        duc = rs * (dxh - jnp.mean(dxh, axis=-1, keepdims=True) - xh * jnp.mean(dxh * xh, axis=-1, keepdims=True))
        addrow(R_CONV_B, duc)
        ducext[0:tm, :] = duc
        ducext[tm:tm + CONV_HALO, :] = uccar[...]
        uccar[...] = duc[0:CONV_HALO, :]
        _shifted_copies(ducext, dsh, tm)
        for r0 in range(0, tm, CONV_CHUNK):
            crow = slice(r0, r0 + CONV_CHUNK)
            duc_c = ducext[crow, :]
            du = jnp.zeros((CONV_CHUNK, CONV_W), F32)
            for k in range(CONV_K):
                prod = duc_c * _tap(uext, ush, CONV_HALO - (CONV_K - 1) + k, r0, CONV_CHUNK)
                part = prod[0:8]
                for s in range(8, CONV_CHUNK, 8):
                    part = part + prod[s:s + 8]
                dcw8[k] += part
                du = du + cw_ref[k:k + 1, :] * _tap(ducext, dsh, CONV_K - 1 - k, r0, CONV_CHUNK)
            sgc = sgs[crow, :]
            dz_ref[crow, CV0:CV0 + CONV_W] = (du * sgc).astype(dz_ref.dtype)
            u_c = uext[CONV_HALO + r0:CONV_HALO + r0 + CONV_CHUNK, :]
            dz_ref[crow, CG0:CG0 + CONV_W] = (du * u_c * (1.0 - sgc)).astype(dz_ref.dtype)

        @pl.when(i == nt - 1)
        def _():
            dcw_ref[...] = jnp.sum(dcw8[...], axis=1)

        xc, r, ig, sp, la, a, mult = _lru_gates(z_ref, zh_ref, pv_ref, wa_ref, wx_ref, rxext, first, tm)
        h = hl_ref[...]
        rowi = lax.broadcasted_iota(jnp.int32, (tm, LRU_W), 0)
        hlast = jnp.where(first, 0.0, hlh_ref[7:8, :])
        hprev = jnp.where(rowi == 0, hlast, pltpu.roll(h, 1, 0))
        dyl = dy_ref[:, ATTN_W + CONV_W:ATTN_W + CONV_W + LRU_W]
        gl, dgl = _gelu(z_ref[:, RG0:RG0 + LRU_W])
        dz_ref[:, RG0:RG0 + LRU_W] = (dyl * h * dgl).astype(dz_ref.dtype)
        dh = dyl * gl + jnp.where(rowi == tm - 1, gcar[0:1, :], 0.0)
        c = jnp.where(rowi == tm - 1, 0.0, pltpu.roll(a, tm - 1, 0))
        _, gg = _scan(c, dh, tm, reverse=True)
        gcar[0:1, :] = a[0:1, :] * gg[0:1, :]
        dmult = gg * (ig * xc)
        dig = gg * mult * xc
        dxc = gg * mult * ig
        dla = gg * hprev * a - dmult * a * a / mult
        dr = dla * (-LRU_C * sp)
        lam = pv_ref[R_LAM:R_LAM + 1, :]
        dpv_ref[R_LAM:R_LAM + 1, :] += jnp.sum(dla * (-LRU_C * r), axis=0, keepdims=True) * (-_sigmoid(-lam))
        dpa = dr * r * (1.0 - r)
        dpx = dig * ig * (1.0 - ig)
        addrow(R_BA, dpa)
        addrow(R_BX, dpx)
        dxc = dxc + _dot_nt(dpa, wa_ref[...]) + _dot_nt(dpx, wx_ref[...])
        dwa_ref[...] += _dot_tn(xc, dpa)
        dwx_ref[...] += _dot_tn(xc, dpx)
        addrow(R_LCONV_B, dxc)
        dxcext[0:tm, :] = dxc
        dxcext[tm:tm + LRU_HALO, :] = xccar[...]
        xccar[...] = dxc[0:LRU_HALO, :]
        drx = jnp.zeros((tm, LRU_W), F32)
        for k in range(LRU_K):
            addrow(R_LCW + k, dxc * rxext[pl.ds(LRU_HALO - (LRU_K - 1) + k, tm), :])
            drx = drx + pv_ref[R_LCW + k:R_LCW + k + 1, :] * dxcext[pl.ds(LRU_K - 1 - k, tm), :]
        dz_ref[:, RX0:RX0 + LRU_W] = drx.astype(dz_ref.dtype)

    tile = lambda w: pl.BlockSpec((tm, w), lambda i: (rev(i), 0))
    in_specs = [tile(D_MODEL)] + _mixer_in_specs(tm, rev) + [
        tile(D_MODEL), tile(LRU_W),
        pl.BlockSpec((8, LRU_W), lambda i: (jnp.maximum(rev(i) * (tm // 8) - 1, 0), 0)),
        tile(CONV_W), pl.BlockSpec((nb, 4 * BLK, 4 * BLK), lambda i: (rev(i), 0, 0)),
        pl.BlockSpec((nb, 4 * BLK, BLK), lambda i: (rev(i), 0, 0))]
    return _call(
        body, name, (nt,), in_specs,
        [tile(IN_W), _acc_spec((8, 4 * BLK)), _acc_spec((32, CONV_W)), _acc_spec((16, CONV_W)),
         _acc_spec((LRU_W, LRU_W)), _acc_spec((LRU_W, LRU_W))],
        [_sds((t, IN_W), MX), _sds((8, 4 * BLK), F32), _sds((32, CONV_W), F32), _sds((16, CONV_W), F32),
         _sds((LRU_W, LRU_W), F32), _sds((LRU_W, LRU_W), F32)],
        [pltpu.VMEM((tm + CONV_HALO, CONV_W), F32), pltpu.VMEM((7, tm + CONV_HALO - 8, CONV_W), F32),
         pltpu.VMEM((tm, CONV_W), F32), pltpu.VMEM((tm + LRU_HALO, LRU_W), F32),
         pltpu.VMEM((KV_W, tm + BLK), F32), pltpu.VMEM((KV_W, tm + BLK), F32),
         pltpu.VMEM((tm + CONV_HALO, CONV_W), F32), pltpu.VMEM((7, tm + CONV_HALO - 8, CONV_W), F32),
         pltpu.VMEM((32, 8, CONV_W), F32), pltpu.VMEM((tm + LRU_HALO, LRU_W), F32),
         pltpu.VMEM((KV_W, BLK), F32), pltpu.VMEM((KV_W, BLK), F32),
         pltpu.VMEM((CONV_HALO, CONV_W), F32), pltpu.VMEM((LRU_HALO, LRU_W), F32), pltpu.VMEM((8, LRU_W), F32)],
        [dy, z, z, sink, cw, pv, wa, wx, ycat, hl, hl, uc, probs, psinks], riders)


def _post_fwd(ycat, h0, gmix, w_out, g2, w_up, w_down, name, riders=()):
    t = h0.shape[0]
    tm = _tile(t, POST_TILE)
    nj = D_FF // FF_BLK

    def body(y_ref, h_ref, gm_ref, wo_ref, g2_ref, wu_ref, wd_ref, h1_ref, a_ref, h2_ref, ym_ref, hn_ref):
        ym, _, _ = _group_rms_fwd(y_ref[...], gm_ref[...])
        ym = ym.astype(MX)
        ym_ref[...] = ym
        h1 = h_ref[...] + jnp.dot(ym, wo_ref[...], preferred_element_type=F32)
        h1_ref[...] = h1
        hn, _, _ = _rms_fwd(h1, g2_ref[...])
        hn = hn.astype(MX)
        hn_ref[...] = hn
        for j in range(nj):
            u = jnp.dot(hn, wu_ref[j], preferred_element_type=F32)
            a_ref[:, j * FF_BLK:(j + 1) * FF_BLK] = jnp.square(jnp.maximum(u, 0.0)).astype(MX)
        h2_ref[...] = h1 + jnp.dot(a_ref[...], wd_ref[...], preferred_element_type=F32)

    tile = lambda w: pl.BlockSpec((tm, w), lambda i: (i, 0))
    return _call(
        body, name, (t // tm,),
        [tile(D_MODEL), tile(D_MODEL), _const_spec((1, D_MODEL)), _const_spec((D_MODEL, D_MODEL)),
         _const_spec((1, D_MODEL)), _const_spec((nj, D_MODEL, FF_BLK)), _const_spec((D_FF, D_MODEL))],
        [tile(D_MODEL), tile(D_FF), tile(D_MODEL), tile(D_MODEL), tile(D_MODEL)],
        [_sds((t, D_MODEL), F32), _sds((t, D_FF), MX), _sds((t, D_MODEL), F32), _sds((t, D_MODEL), MX),
         _sds((t, D_MODEL), MX)],
        [], [ycat, h0, gmix, w_out, g2, w_up, w_down], riders)


def _ffn_bwd(dh2, act, h1, g2, w_up_t, w_down, name, riders=()):
    t = h1.shape[0]
    tm = _tile(t, POST_TILE)
    nj = D_FF // FF_BLK

    def body(dh2_ref, a_ref, h1_ref, g2_ref, wut_ref, wd_ref, dh1_ref, dh1b_ref, dh2b_ref, du_ref, dg2_ref):
        @pl.when(pl.program_id(0) == 0)
        def _():
            dg2_ref[...] = jnp.zeros_like(dg2_ref)

        dh2 = dh2_ref[...]
        dh2b = dh2.astype(MX)
        dh2b_ref[...] = dh2b
        for j in range(nj):
            cols = slice(j * FF_BLK, (j + 1) * FF_BLK)
            da = _dot_nt(dh2b, wd_ref[j])
            du_ref[:, cols] = (da * (2.0 * jnp.sqrt(a_ref[:, cols].astype(F32)))).astype(MX)
        dhn = jnp.dot(du_ref[...], wut_ref[...], preferred_element_type=F32)
        _, xh, r = _rms_fwd(h1_ref[...], g2_ref[...])
        dx, dg = _rms_bwd(dhn, xh, r, g2_ref[...])
        dg2_ref[...] += dg
        dh1 = dh2 + dx
        dh1_ref[...] = dh1
        dh1b_ref[...] = dh1.astype(MX)

    tile = lambda w: pl.BlockSpec((tm, w), lambda i: (i, 0))
    return _call(
        body, name, (t // tm,),
        [tile(D_MODEL), tile(D_FF), tile(D_MODEL), _const_spec((1, D_MODEL)),
         _const_spec((D_FF, D_MODEL)), _const_spec((nj, FF_BLK, D_MODEL))],
        [tile(D_MODEL), tile(D_MODEL), tile(D_MODEL), tile(D_FF), _acc_spec((1, D_MODEL))],
        [_sds((t, D_MODEL), F32), _sds((t, D_MODEL), MX), _sds((t, D_MODEL), MX), _sds((t, D_FF), MX),
         _sds((1, D_MODEL), F32)],
        [], [dh2, act, h1, g2, w_up_t, w_down], riders)


def _mix_bwd(dh1, ycat, ym, gmix, w_out, name):
    t = dh1.shape[0]
    tm = _tile(t)
    nk = t // tm
    r = D_MODEL // N_DEV

    def body(dh1_ref, y_ref, ym_ref, gm_ref, wo_ref, dy_ref, dgm_ref, o_ref, o16_ref, acc):
        k = pl.program_id(0)

        @pl.when(k == 0)
        def _():
            dgm_ref[...] = jnp.zeros_like(dgm_ref)
            acc[...] = jnp.zeros_like(acc)

        dh = dh1_ref[...]
        acc[...] += _dot_tn(ym_ref[...], dh)
        dym = _dot_nt(dh, wo_ref[...])
        gm = gm_ref[...]
        _, yh, rr = _group_rms_fwd(y_ref[...], gm)
        outs, dgs = [], []
        for (a, b), rg in zip(_GROUPS, rr):
            dxg, dgg = _rms_bwd(dym[:, a:b], yh[:, a:b], rg, gm[:, a:b])
            outs.append(dxg)
            dgs.append(dgg)
        dy_ref[...] = jnp.concatenate(outs, axis=1)
        dgm_ref[...] += jnp.concatenate(dgs, axis=1)

        @pl.when(k == nk - 1)
        def _():
            for d in range(N_DEV):
                v = acc[d * r:(d + 1) * r, :]
                o_ref[d] = v
                o16_ref[d] = v.astype(o16_ref.dtype)

    tile = pl.BlockSpec((tm, D_MODEL), lambda i: (i, 0))
    slabs = _const_spec((N_DEV, r, D_MODEL))
    (dy, dgm, dw, dw16), _ = _call(
        body, name, (nk,), [tile, tile, tile, _const_spec((1, D_MODEL)), _const_spec((D_MODEL, D_MODEL))],
        [tile, _acc_spec((1, D_MODEL)), slabs, slabs],
        [_sds((t, D_MODEL), F32), _sds((1, D_MODEL), F32), _sds((N_DEV, r, D_MODEL), F32),
         _sds((N_DEV, r, D_MODEL), WIRE)],
        [pltpu.VMEM((D_MODEL, D_MODEL), F32)], [dh1, ycat, ym, gmix, w_out])
    return dy, dgm, (dw, dw16)


def _in_bwd(dz, h0, dh1, g1, w_in, after, name):
    t = h0.shape[0]
    tm = _tile(t)

    def body(dz_ref, h_ref, dh1_ref, g_ref, w_ref, after_ref, dh0_ref, dg_ref):
        @pl.when(pl.program_id(0) == 0)
        def _():
            dg_ref[...] = jnp.zeros_like(dg_ref)

        dhn = _dot_nt(dz_ref[...], w_ref[...])
        _, xh, r = _rms_fwd(h_ref[...], g_ref[...])
        dx, dg = _rms_bwd(dhn, xh, r, g_ref[...])
        dg_ref[...] += dg
        dh0_ref[...] = dh1_ref[...] + dx

    tile = lambda w: pl.BlockSpec((tm, w), lambda i: (i, 0))
    (dh0, dg), _ = _call(
        body, name, (t // tm,),
        [tile(IN_W), tile(D_MODEL), tile(D_MODEL), _const_spec((1, D_MODEL)), _const_spec((D_MODEL, IN_W)),
         _const_spec((8, 128))],
        [tile(D_MODEL), _acc_spec((1, D_MODEL))], [_sds((t, D_MODEL), F32), _sds((1, D_MODEL), F32)],
        [], [dz, h0, dh1, g1, w_in, after])
    return dh0, dg


def _in_bwd_dw(dz, h0, dh1, hn1, g1, w_in, name):
    t = h0.shape[0]
    tm = _tile(t)
    nk = t // tm

    def body(dz_ref, h_ref, dh1_ref, hn_ref, g_ref, w_ref, dh0_ref, dg_ref, o_ref, o16_ref, acc):
        k = pl.program_id(0)

        @pl.when(k == 0)
        def _():
            dg_ref[...] = jnp.zeros_like(dg_ref)
            acc[...] = jnp.zeros_like(acc)

        dz_t = dz_ref[...]
        acc[...] += _dot_tn(hn_ref[...], dz_t)
        dhn = _dot_nt(dz_t, w_ref[...])
        _, xh, r = _rms_fwd(h_ref[...], g_ref[...])
        dx, dg = _rms_bwd(dhn, xh, r, g_ref[...])
        dg_ref[...] += dg
        dh0_ref[...] = dh1_ref[...] + dx

        @pl.when(k == nk - 1)
        def _():
            for d in range(N_DEV):
                v = acc[:, d * IN_SHARD:(d + 1) * IN_SHARD]
                o_ref[d] = v
                o16_ref[d] = v.astype(o16_ref.dtype)

    tile = lambda w: pl.BlockSpec((tm, w), lambda i: (i, 0))
    slabs = _const_spec((N_DEV, D_MODEL, IN_SHARD))
    (dh0, dg, dw, dw16), _ = _call(
        body, name, (nk,),
        [tile(IN_W), tile(D_MODEL), tile(D_MODEL), tile(D_MODEL), _const_spec((1, D_MODEL)),
         _const_spec((D_MODEL, IN_W))],
        [tile(D_MODEL), _acc_spec((1, D_MODEL)), slabs, slabs],
        [_sds((t, D_MODEL), F32), _sds((1, D_MODEL), F32), _sds((N_DEV, D_MODEL, IN_SHARD), F32),
         _sds((N_DEV, D_MODEL, IN_SHARD), WIRE)],
        [pltpu.VMEM((D_MODEL, IN_W), F32)], [dz, h0, dh1, hn1, g1, w_in])
    return dh0, dg, (dw, dw16)


def _loss_head(h, gf, target, name):
    t = h.shape[0]
    tm = _tile(t)

    def body(h_ref, g_ref, t_ref, dh_ref, loss_ref, dg_ref):
        @pl.when(pl.program_id(0) == 0)
        def _():
            loss_ref[...] = jnp.zeros_like(loss_ref)
            dg_ref[...] = jnp.zeros_like(dg_ref)

        g = g_ref[...]
        y, xh, r = _rms_fwd(h_ref[...], g)
        err = y - t_ref[...]
        part = 0.5 * jnp.sum(jnp.mean(err * err, axis=-1, keepdims=True), axis=0, keepdims=True)
        loss_ref[...] += jnp.broadcast_to(part, loss_ref.shape)
        dx, dg = _rms_bwd(err * (1.0 / D_MODEL), xh, r, g)
        dg_ref[...] += dg
        dh_ref[...] = dx

    tile = pl.BlockSpec((tm, D_MODEL), lambda i: (i, 0))
    (dh, loss, dg), _ = _call(
        body, name, (t // tm,), [tile, _const_spec((1, D_MODEL)), tile],
        [tile, _acc_spec((1, 128)), _acc_spec((1, D_MODEL))],
        [_sds((t, D_MODEL), F32), _sds((1, 128), F32), _sds((1, D_MODEL), F32)], [], [h, gf, target])
    return dh, loss, dg


def _dw(x, y, name, split, bm, bn):
    t, m = x.shape
    n = y.shape[1]
    tk = _tile(t, DW_TILE)
    nk = t // tk
    if split == "rows":
        assert bn == n
        r, c = m // N_DEV, n
        per = bm // r
        out_block = pl.BlockSpec((per, r, c), lambda a, b, k: (a, 0, 0), pipeline_mode=pl.Buffered(1))
    else:
        assert bm == m
        r, c = m, n // N_DEV
        per = bn // c
        out_block = pl.BlockSpec((per, r, c), lambda a, b, k: (b, 0, 0), pipeline_mode=pl.Buffered(1))

    def body(x_ref, y_ref, o_ref, o16_ref, acc):
        k = pl.program_id(2)

        @pl.when(k == 0)
        def _():
            acc[...] = jnp.zeros_like(acc)

        acc[...] += _dot_tn(x_ref[...], y_ref[...])

        @pl.when(k == nk - 1)
        def _():
            for d in range(per):
                v = acc[d * r:(d + 1) * r, :] if split == "rows" else acc[:, d * c:(d + 1) * c]
                o_ref[d] = v
                o16_ref[d] = v.astype(o16_ref.dtype)

    return pl.pallas_call(
        body, name=name, grid=(m // bm, n // bn, nk),
        in_specs=[pl.BlockSpec((tk, bm), lambda a, b, k: (k, a)), pl.BlockSpec((tk, bn), lambda a, b, k: (k, b))],
        out_specs=[out_block, out_block],
        out_shape=[_sds((N_DEV, r, c), F32), _sds((N_DEV, r, c), WIRE)],
        scratch_shapes=[pltpu.VMEM((bm, bn), F32)],
        compiler_params=pltpu.CompilerParams(dimension_semantics=("arbitrary",) * 3, vmem_limit_bytes=VMEM_LIMIT),
    )(x, y)


def _adamw_math(w, g, m, v):
    m = ADAM_B1 * m + (1.0 - ADAM_B1) * g
    v = ADAM_B2 * v + (1.0 - ADAM_B2) * jnp.square(g)
    m_hat = m / (1.0 - ADAM_B1 ** ADAM_STEP)
    v_hat = v / (1.0 - ADAM_B2 ** ADAM_STEP)
    delta = -ADAM_LR * (m_hat / (jnp.sqrt(v_hat) + ADAM_EPS) + ADAM_WD * w)
    return delta, m, v


def _adamw_shard(g_own, g_recv, dev, w, m, v, after, name):
    _, r, c = w.shape
    br = r
    for cand in (256, 128, 112, 64, 56, 32, 16, 8):
        if r % cand == 0:
            br = cand
            break
    nr = r // br
    own = lambda l: pl.BlockSpec((1, br, c), lambda ll, i, d: (d[0], jnp.where(ll == l, i, (nr - 1) * (1 - l)), 0))
    recv = lambda l: pl.BlockSpec((N_DEV - 1, br, c), lambda ll, i, d: (0, jnp.where(ll == l, i, (nr - 1) * (1 - l)), 0))

    def body(dev_ref, go0, gr0, go1, gr1, w_ref, m_ref, v_ref, after_ref, g_out, d_out, m_out, v_out):
        def update(go_ref, gr_ref):
            g = go_ref[0]
            for j in range(N_DEV - 1):
                g = g + gr_ref[j].astype(F32)
            delta, mn, vn = _adamw_math(w_ref[0], g, m_ref[0], v_ref[0])
            g_out[0] = g
            d_out[0] = delta
            m_out[0] = mn
            v_out[0] = vn

        layer = pl.program_id(0)
        pl.when(layer == 0)(lambda: update(go0, gr0))
        pl.when(layer == 1)(lambda: update(go1, gr1))

    tile = pl.BlockSpec((1, br, c), lambda ll, i, d: (ll, i, 0))
    return pl.pallas_call(
        body, name=name,
        grid_spec=pltpu.PrefetchScalarGridSpec(
            num_scalar_prefetch=1, grid=(2, nr),
            in_specs=[own(0), recv(0), own(1), recv(1), tile, tile, tile,
                      pl.BlockSpec((8, 128), lambda ll, i, d: (0, 0))],
            out_specs=[tile, tile, tile, tile]),
        out_shape=[_sds((2, r, c), F32)] * 4,
        compiler_params=pltpu.CompilerParams(dimension_semantics=("arbitrary",) * 2, vmem_limit_bytes=VMEM_LIMIT),
    )(dev, g_own[0], g_recv[0], g_own[1], g_recv[1], w, m, v, after)


def _adamw_small(gs, ws, ms, vs, name):
    n = len(gs)

    def body(*refs):
        g_refs, w_refs, m_refs, v_refs = (refs[k * n:(k + 1) * n] for k in range(4))
        outs = refs[4 * n:]
        for k in range(n):
            delta, mn, vn = _adamw_math(w_refs[k][...], g_refs[k][...], m_refs[k][...], v_refs[k][...])
            outs[k][...] = delta
            outs[n + k][...] = mn
            outs[2 * n + k][...] = vn

    shapes = [_sds(w.shape, F32) for w in ws]
    res = pl.pallas_call(body, name=name, out_shape=shapes * 3,
                         compiler_params=pltpu.CompilerParams(vmem_limit_bytes=VMEM_LIMIT))(*gs, *ws, *ms, *vs)
    return res[:n], res[n:2 * n], res[2 * n:]


def _sum_parts(part, dev, name):
    def body(dev_ref, p_ref, o_ref):
        me = dev_ref[0]
        g = p_ref[me]
        for d in range(1, N_DEV):
            g = g + p_ref[jnp.bitwise_xor(me, d)]
        o_ref[...] = g

    full = pl.BlockSpec(part.shape, lambda i, d: (0, 0, 0))
    return pl.pallas_call(
        body, name=name,
        grid_spec=pltpu.PrefetchScalarGridSpec(
            num_scalar_prefetch=1, grid=(1,), in_specs=[full],
            out_specs=pl.BlockSpec(part.shape[1:], lambda i, d: (0, 0))),
        out_shape=_sds(part.shape[1:], F32))(dev, part)


HBM = pl.BlockSpec(memory_space=pltpu.HBM)
SEM = pl.BlockSpec(memory_space=pltpu.SEMAPHORE)
EFFECT = pltpu.SideEffectType.DATAFLOW_SIDE_EFFECTING


def _direct_copies(srcs, lands, ssem, rsem, scatter):
    x, y, c = _me()
    out = []
    for a in range(len(srcs)):
        for f in range(1, N_DEV):
            px = 1 - x if f & 4 else x
            py = 1 - y if f & 2 else y
            pc = 1 - c if f & 1 else c
            out.append(pltpu.make_async_remote_copy(
                src_ref=srcs[a].at[4 * px + 2 * py + pc] if scatter else srcs[a], dst_ref=lands[a].at[f - 1],
                send_sem=ssem.at[7 * a + f - 1], recv_sem=rsem.at[7 * a + f - 1],
                device_id=(px, py, pc), device_id_type=MESH))
    return out


def _send_start(arrays, scatter, name):
    arrays = list(arrays)
    n = len(arrays)
    lands = [lax.empty((N_DEV - 1,) + (a.shape[1:] if scatter else a.shape), a.dtype) for a in arrays]

    def body(*refs):
        srcs, lnds, ssem, rsem, token = refs[:n], refs[n:2 * n], refs[2 * n], refs[2 * n + 1], refs[-1]
        for cp in _direct_copies(srcs, lnds, ssem, rsem, scatter):
            cp.start()
        token[...] = jnp.zeros_like(token)

    hbm = lambda a: pltpu.HBM(a.shape, a.dtype)
    res = pl.pallas_call(
        body, name=name,
        out_shape=(pltpu.SemaphoreType.DMA((7 * n,)), pltpu.SemaphoreType.DMA((7 * n,)),
                   *[hbm(a) for a in arrays + lands], _sds((8, 128), F32)),
        in_specs=[HBM] * (2 * n),
        out_specs=(SEM, SEM, *[HBM] * (2 * n), pl.BlockSpec(memory_space=pltpu.VMEM)),
        input_output_aliases={i: 2 + i for i in range(2 * n)},
        compiler_params=pltpu.CompilerParams(has_side_effects=EFFECT),
    )(*[pltpu.with_memory_space_constraint(a, pltpu.HBM) for a in arrays + lands])
    return types.SimpleNamespace(ssem=res[0], rsem=res[1], srcs=list(res[2:2 + n]), lands=list(res[2 + n:2 + 2 * n]),
                                 token=res[-1], scatter=scatter)


def _send_wait(h, after, name):
    n = len(h.srcs)

    def body(*refs):
        srcs, lnds, ssem, rsem = refs[:n], refs[n:2 * n], refs[2 * n], refs[2 * n + 1]
        for cp in _direct_copies(srcs, lnds, ssem, rsem, h.scatter):
            cp.wait_send()
            cp.wait_recv()

    hbm = lambda a: pltpu.HBM(a.shape, a.dtype)
    res = pl.pallas_call(
        body, name=name,
        out_shape=tuple(hbm(a) for a in h.srcs + h.lands),
        in_specs=[HBM] * (2 * n) + [SEM, SEM, ANY], out_specs=[HBM] * (2 * n),
        input_output_aliases={i: i for i in range(2 * n)},
        compiler_params=pltpu.CompilerParams(has_side_effects=EFFECT),
    )(*h.srcs, *h.lands, h.ssem, h.rsem, after)
    return list(res[:n]), list(res[n:])


def _block_diag(w):
    out = jnp.zeros((LRU_W, LRU_W), w.dtype)
    for h in range(4):
        out = lax.dynamic_update_slice(out, w[h], (h * 64, h * 64))
    return out


def _unblock_diag(w):
    return jnp.concatenate([w[h * 64:(h + 1) * 64, h * 64:(h + 1) * 64] for h in range(4)], axis=0)


def _layer_params(p, l):
    row = lambda a: a[l].reshape(1, -1)
    sink_rows = jnp.repeat(p["attn_sinks"][l].reshape(4, 2), 2 * BLK, axis=1)
    sink_rows = jnp.concatenate([sink_rows, jnp.zeros((4, 4 * BLK), F32)], axis=0)
    cw = jnp.concatenate([p["conv_dw_w"][l], jnp.zeros((1, CONV_W), F32)], axis=0)
    pv = jnp.concatenate([
        row(p["conv_dw_b"]), row(p["conv_ln_g"]), row(p["conv_ln_b"]), row(p["lru_conv_b"]), row(p["lru_ba"]),
        row(p["lru_bx"]), row(p["lru_lambda"]), jnp.zeros((1, LRU_W), F32), p["lru_conv_w"][l],
        jnp.zeros((4, LRU_W), F32)], axis=0)
    return dict(
        g1=row(p["norm1"]), sink=sink_rows, cw=cw, pv=pv,
        wa=_block_diag(p["lru_wa"][l]).astype(MX), wx=_block_diag(p["lru_wx"][l]).astype(MX),
        gmix=row(p["mix_norm"]), g2=row(p["norm2"]))


_SMALL = ["norm1", "attn_sinks", "conv_dw_w", "conv_dw_b", "conv_ln_g", "conv_ln_b", "lru_conv_w", "lru_conv_b",
          "lru_wa", "lru_ba", "lru_wx", "lru_bx", "lru_lambda", "mix_norm", "norm2"]
_BIG = ["w_in", "w_out", "w_up", "w_down"]
_WEIGHTS = ["norm1", "w_in", "attn_sinks", "conv_dw_w", "conv_dw_b", "conv_ln_g", "conv_ln_b", "lru_conv_w",
            "lru_conv_b", "lru_wa", "lru_ba", "lru_wx", "lru_bx", "lru_lambda", "mix_norm", "w_out", "norm2", "w_up",
            "w_down", "final_norm"]


def kernel(x, norm1, w_in, attn_sinks, conv_dw_w, conv_dw_b, conv_ln_g, conv_ln_b, lru_conv_w, lru_conv_b, lru_wa, lru_ba, lru_wx, lru_bx, lru_lambda, mix_norm, w_out, norm2, w_up, w_down, final_norm, loss_target, m_norm1, m_w_in, m_attn_sinks, m_conv_dw_w, m_conv_dw_b, m_conv_ln_g, m_conv_ln_b, m_lru_conv_w, m_lru_conv_b, m_lru_wa, m_lru_ba, m_lru_wx, m_lru_bx, m_lru_lambda, m_mix_norm, m_w_out, m_norm2, m_w_up, m_w_down, m_final_norm, v_norm1, v_w_in, v_attn_sinks, v_conv_dw_w, v_conv_dw_b, v_conv_ln_g, v_conv_ln_b, v_lru_conv_w, v_lru_conv_b, v_lru_wa, v_lru_ba, v_lru_wx, v_lru_bx, v_lru_lambda, v_mix_norm, v_w_out, v_norm2, v_w_up, v_w_down, v_final_norm):
    w = dict(norm1=norm1, w_in=w_in, attn_sinks=attn_sinks, conv_dw_w=conv_dw_w, conv_dw_b=conv_dw_b,
             conv_ln_g=conv_ln_g, conv_ln_b=conv_ln_b, lru_conv_w=lru_conv_w, lru_conv_b=lru_conv_b, lru_wa=lru_wa,
             lru_ba=lru_ba, lru_wx=lru_wx, lru_bx=lru_bx, lru_lambda=lru_lambda, mix_norm=mix_norm, w_out=w_out,
             norm2=norm2, w_up=w_up, w_down=w_down, final_norm=final_norm)
    m = dict(norm1=m_norm1, w_in=m_w_in, attn_sinks=m_attn_sinks, conv_dw_w=m_conv_dw_w, conv_dw_b=m_conv_dw_b,
             conv_ln_g=m_conv_ln_g, conv_ln_b=m_conv_ln_b, lru_conv_w=m_lru_conv_w, lru_conv_b=m_lru_conv_b,
             lru_wa=m_lru_wa, lru_ba=m_lru_ba, lru_wx=m_lru_wx, lru_bx=m_lru_bx, lru_lambda=m_lru_lambda,
             mix_norm=m_mix_norm, w_out=m_w_out, norm2=m_norm2, w_up=m_w_up, w_down=m_w_down, final_norm=m_final_norm)
    v = dict(norm1=v_norm1, w_in=v_w_in, attn_sinks=v_attn_sinks, conv_dw_w=v_conv_dw_w, conv_dw_b=v_conv_dw_b,
             conv_ln_g=v_conv_ln_g, conv_ln_b=v_conv_ln_b, lru_conv_w=v_lru_conv_w, lru_conv_b=v_lru_conv_b,
             lru_wa=v_lru_wa, lru_ba=v_lru_ba, lru_wx=v_lru_wx, lru_bx=v_lru_bx, lru_lambda=v_lru_lambda,
             mix_norm=v_mix_norm, w_out=v_w_out, norm2=v_norm2, w_up=v_w_up, w_down=v_w_down, final_norm=v_final_norm)
    depth = w_in.shape[0]
    xi, yi, ci = _me()
    dev = (4 * xi + 2 * yi + ci).astype(jnp.int32)
    dev1 = dev.reshape(1)
    wb = {n: w[n].astype(MX) for n in _BIG}
    layer_shards = lambda l: [wb["w_out"][l], wb["w_up"][l], wb["w_down"][l]]

    _, ((g_in0, g_cw, g_lcw),) = _call(None, "gather_first", None, [], [], [], [], [],
                                        [_gather_rider([wb["w_in"][0], conv_dw_w, lru_conv_w])])
    cols = lambda g: jnp.moveaxis(g, 0, -2).reshape(g.shape[1:-1] + (N_DEV * g.shape[-1],))
    p = dict(w)
    p["conv_dw_w"] = cols(g_cw)
    p["lru_conv_w"] = cols(g_lcw)
    lp = [_layer_params(p, l) for l in range(depth)]

    gathered = [dict(w_in=cols(g_in0)), dict()]
    saved = []
    h = x[0]
    for l in range(depth):
        q, gw = lp[l], gathered[l]
        z, hn1 = _ln_in(h, q["g1"], gw["w_in"], f"ln_in{l}")
        riders = [_gather_rider(layer_shards(0))] if l == 0 else []
        (ycat, hl, uc, probs, psinks), got = _mixer_fwd(z, q["sink"], q["cw"], q["pv"], q["wa"], q["wx"],
                                                        f"mixer_fwd{l}", riders)
        if l == 0:
            gw["w_out"], gw["w_up"], gw["w_down"] = got[0]
            gw["w_out"] = gw["w_out"].reshape(D_MODEL, D_MODEL)
        riders = [_gather_rider([wb["w_in"][1]] + layer_shards(1))] if l == 0 else []
        (h1, act, h2, ym, hn2), got = _post_fwd(ycat, h, q["gmix"], gw["w_out"], q["g2"], gw["w_up"],
                                                gw["w_down"].reshape(D_FF, D_MODEL), f"post_fwd{l}", riders)
        if l == 0:
            nxt = gathered[1]
            nxt["w_in"], nxt["w_out"], nxt["w_up"], nxt["w_down"] = got[0]
            nxt["w_in"] = cols(nxt["w_in"])
            nxt["w_out"] = nxt["w_out"].reshape(D_MODEL, D_MODEL)
        saved.append(dict(h0=h, z=z, hn1=hn1, ycat=ycat, hl=hl, uc=uc, probs=probs, psinks=psinks, h1=h1, act=act,
                          ym=ym, hn2=hn2))
        h = h2
    dh, loss, dgf = _loss_head(h, final_norm.reshape(1, -1), loss_target[0], "loss_head")

    grads = [None] * depth
    big = {n: [None] * depth for n in _BIG}
    pending = []

    def send_pending():
        riders = [_scatter_rider([item[3] for item in pending])] if pending else []
        return riders, list(pending)

    def record(sent, got):
        for item, recv in zip(sent, got[0] if sent else []):
            big[item[0]][item[1]] = (item[2], recv)
        del pending[:len(sent)]

    for l in reversed(range(depth)):
        q, s, gw = lp[l], saved[l], gathered[l]
        riders, sent = send_pending()
        w_up_t = jnp.swapaxes(gw["w_up"], 1, 2).reshape(D_FF, D_MODEL)
        (dh1, dh1b, dhb, du, dg2), got = _ffn_bwd(dh, s["act"], s["h1"], q["g2"], w_up_t, gw["w_down"],
                                                  f"ffn_bwd{l}", riders)
        record(sent, got)
        dycat, dgm, d_wout = _mix_bwd(dh1b, s["ycat"], s["ym"], q["gmix"], gw["w_out"], f"mix_bwd{l}")
        pending.append(("w_down", l) + tuple(_dw(s["act"], dhb, f"dw_down{l}", "rows", 2048, D_MODEL)))
        pending.append(("w_up", l) + tuple(_dw(s["hn2"], du, f"dw_up{l}", "cols", D_MODEL, 2048)))
        pending.append(("w_out", l) + tuple(d_wout))
        riders, sent = send_pending()
        (dz, dsink, dcw, dpv, dwa, dwx), got = _mixer_bwd(
            dycat, s["z"], s["ycat"], s["hl"], s["uc"], s["probs"], s["psinks"], q["sink"], q["cw"], q["pv"], q["wa"],
            q["wx"], f"mixer_bwd{l}", riders)
        record(sent, got)
        if l > 0:
            dh, dg1, d_win = _in_bwd_dw(dz, s["h0"], dh1, s["hn1"], q["g1"], gw["w_in"], f"in_bwd{l}")
            pending.append(("w_in", l) + tuple(d_win))
        else:
            d_win = _dw(s["hn1"], dz, f"dw_in{l}", "cols", D_MODEL, IN_W)
            win_sends = _send_start([d_win[1]], True, "scatter_w_in0_start")
            dh, dg1 = _in_bwd(dz, s["h0"], dh1, q["g1"], gw["w_in"], win_sends.token, f"in_bwd{l}")
        grads[l] = dict(
            norm1=dg1[0], attn_sinks=jnp.stack([dsink[0:4, 0], dsink[0:4, 2 * BLK]], axis=1).reshape(8),
            conv_dw_w=dcw[0:CONV_K], conv_dw_b=dpv[R_CONV_B], conv_ln_g=dpv[R_LN_G], conv_ln_b=dpv[R_LN_B],
            lru_conv_w=dpv[R_LCW:R_LCW + LRU_K], lru_conv_b=dpv[R_LCONV_B], lru_wa=_unblock_diag(dwa),
            lru_ba=dpv[R_BA].reshape(4, 64), lru_wx=_unblock_diag(dwx), lru_bx=dpv[R_BX].reshape(4, 64),
            lru_lambda=dpv[R_LAM], mix_norm=dgm[0], norm2=dg2[0])

    small = [jnp.stack([grads[l][n] for l in range(depth)]) for n in _SMALL] + [dgf, loss[:, 0:1]]
    sizes = [a.size for a in small]
    total = -(-sum(sizes) // 1024) * 1024
    packed = jnp.concatenate([a.reshape(-1) for a in small] + [jnp.zeros((total - sum(sizes),), F32)])
    packed = packed.reshape(total // 128, 128)
    small_sends = _send_start([packed], False, "bcast_small_start")

    out = {}
    shard_update = lambda n, after: list(_adamw_shard(
        [big[n][l][0] for l in range(depth)], [big[n][l][1] for l in range(depth)], dev1, w[n], m[n], v[n], after,
        f"adamw_{n}"))
    for n in ("w_out", "w_up", "w_down"):
        out[n] = shard_update(n, small_sends.token)
    _, (win_recv,) = _send_wait(win_sends, out["w_down"][1], "scatter_w_in0_wait")
    big["w_in"][0] = (d_win[0], win_recv)
    (packed,), (small_recv,) = _send_wait(small_sends, win_recv, "bcast_small_wait")
    out["w_in"] = shard_update("w_in", jnp.zeros((8, 128), F32))
    parts = jnp.concatenate([packed[None], small_recv], axis=0)
    summed = _sum_parts(parts, dev1, "sum_small_grads").reshape(-1)
    small_sums, pos = [], 0
    for a, size in zip(small, sizes):
        small_sums.append(summed[pos:pos + size].reshape(a.shape))
        pos += size
    shard = lambda a: lax.dynamic_slice_in_dim(a, dev * (a.shape[-1] // N_DEV), a.shape[-1] // N_DEV, axis=a.ndim - 1)
    flat = {"lru_wa": (depth, LRU_W, 64), "lru_wx": (depth, LRU_W, 64), "final_norm": (1, D_MODEL)}
    gs, ws, ms, vs = [], [], [], []
    for n, g in zip(_SMALL + ["final_norm"], small_sums[:-1]):
        shp = flat.get(n, w[n].shape)
        gs.append((shard(g) if n in ("conv_dw_w", "lru_conv_w") else g).reshape(shp))
        ws.append(w[n].reshape(shp))
        ms.append(m[n].reshape(shp))
        vs.append(v[n].reshape(shp))
    sd, sm, sv = _adamw_small(gs, ws, ms, vs, "adamw_small")
    for j, n in enumerate(_SMALL + ["final_norm"]):
        out[n] = [a.reshape(w[n].shape) for a in (gs[j], sd[j], sm[j], sv[j])]
    loss_total = small_sums[-1][0, 0]

    result = [loss_total, dh[None]]
    for j in range(4):
        result += [out[n][j] for n in _WEIGHTS]
    return tuple(result)
```

```python
import types

import jax
import jax.numpy as jnp
from jax import lax
from jax.experimental import pallas as pl
from jax.experimental.pallas import tpu as pltpu

F32 = jnp.float32
MX = jnp.bfloat16
WIRE = jnp.bfloat16

D_MODEL = 1024
HEAD_DIM = 64
ATTN_W = 512
KV_W = 128
BLK = 128
CONV_W = 256
CONV_K = 31
LRU_W = 256
LRU_K = 4
LRU_C = 8.0
IN_W = 1792
D_FF = 4096
FF_BLK = 512
N_DEV = 8
IN_SHARD = IN_W // N_DEV
RMS_EPS = 1e-6
LN_EPS = 1e-5
MASK_VALUE = -1e30
SCALE = HEAD_DIM ** -0.5
CONV_HALO = 32
LRU_HALO = 8
CONV_CHUNK = 64
POST_TILE = 512
DW_TILE = 1024
Q0, K0, V0, CV0, CG0, RX0, RG0 = 0, 512, 640, 768, 1024, 1280, 1536
R_CONV_B, R_LN_G, R_LN_B, R_LCONV_B, R_BA, R_BX, R_LAM, R_LCW = 0, 1, 2, 3, 4, 5, 6, 8

ADAM_LR, ADAM_B1, ADAM_B2, ADAM_EPS, ADAM_WD, ADAM_STEP = 0.001, 0.9, 0.999, 1e-08, 0.01, 10

VMEM_LIMIT = 56 * 1024 * 1024
MESH = pl.DeviceIdType.MESH
ANY = pl.BlockSpec(memory_space=pl.ANY)


def _tile(t, cap=512):
    return min(cap, t)


def _dot(a, b):
    return jnp.dot(a.astype(MX), b.astype(MX), preferred_element_type=F32)


def _dot_nt(a, b):
    return lax.dot_general(a.astype(MX), b.astype(MX), (((1,), (1,)), ((), ())), preferred_element_type=F32)


def _dot_tn(a, b):
    return lax.dot_general(a.astype(MX), b.astype(MX), (((0,), (0,)), ((), ())), preferred_element_type=F32)


def _const_spec(shape):
    nd = len(shape)
    return pl.BlockSpec(shape, lambda *_: (0,) * nd, pipeline_mode=pl.Buffered(1))


def _acc_spec(shape):
    nd = len(shape)
    return pl.BlockSpec(shape, lambda *_: (0,) * nd)


def _sds(shape, dtype):
    return jax.ShapeDtypeStruct(shape, dtype)


def _sigmoid(x):
    return jax.nn.sigmoid(x)


def _rms_fwd(x, g):
    r = lax.rsqrt(jnp.mean(x * x, axis=-1, keepdims=True) + RMS_EPS)
    xh = x * r
    return xh * g, xh, r


def _rms_bwd(dy, xh, r, g):
    t = dy * g
    dx = r * (t - xh * jnp.mean(t * xh, axis=-1, keepdims=True))
    return dx, jnp.sum(dy * xh, axis=0, keepdims=True)


_GROUPS = ((0, 512), (512, 768), (768, 1024))


def _group_rms_fwd(y, g):
    parts = [_rms_fwd(y[:, a:b], g[:, a:b]) for a, b in _GROUPS]
    return (jnp.concatenate([p[0] for p in parts], axis=1),
            jnp.concatenate([p[1] for p in parts], axis=1),
            [p[2] for p in parts])


def _gelu(x):
    c = 0.7978845608028654
    u = c * (x + 0.044715 * x * x * x)
    th = jnp.tanh(u)
    val = 0.5 * x * (1.0 + th)
    grad = 0.5 * (1.0 + th) + 0.5 * x * (1.0 - th * th) * c * (1.0 + 3.0 * 0.044715 * x * x)
    return val, grad


def _neg_expm1(x):
    series = -x * (1.0 + x * (0.5 + x * (1.0 / 6.0 + x * (1.0 / 24.0))))
    return jnp.where(x > -0.02, series, 1.0 - jnp.exp(x))


def _me():
    return lax.axis_index("x"), lax.axis_index("y"), lax.axis_index("c")


def _gather_rider(arrays):
    arrays = list(arrays)
    n = len(arrays)

    def plan(ins, outs, sems):
        ssem, rsem, lsem = sems
        x, y, c = _me()
        chips = [(1 - x, y), (x, 1 - y), (1 - x, 1 - y)]

        def copy(a, k, block, to, own=False):
            dst = outs[a].at[4 * block[0] + 2 * block[1] + block[2]]
            return pltpu.make_async_remote_copy(
                src_ref=ins[a] if own else dst, dst_ref=dst, send_sem=ssem.at[7 * a + k],
                recv_sem=rsem.at[7 * a + k], device_id=to, device_id_type=MESH)

        return x, y, c, chips, copy, lsem

    def start(ins, outs, sems):
        x, y, c, chips, copy, lsem = plan(ins, outs, sems)
        for a in range(n):
            pltpu.make_async_copy(ins[a], outs[a].at[4 * x + 2 * y + c], lsem.at[a]).start()
            copy(a, 0, (x, y, c), (x, y, 1 - c), own=True).start()
            for j, chip in enumerate(chips):
                copy(a, 1 + j, (x, y, c), (*chip, c), own=True).start()

    def mid(ins, outs, sems):
        x, y, c, chips, copy, _ = plan(ins, outs, sems)
        for a in range(n):
            for j, chip in enumerate(chips):
                copy(a, 1 + j, (*chip, c), (x, y, c)).wait_recv()
                copy(a, 4 + j, (*chip, c), (x, y, 1 - c)).start()

    def finish(ins, outs, sems):
        x, y, c, chips, copy, lsem = plan(ins, outs, sems)
        for a in range(n):
            copy(a, 0, (x, y, 1 - c), (x, y, c)).wait_recv()
            for j, chip in enumerate(chips):
                copy(a, 4 + j, (*chip, 1 - c), (x, y, c)).wait_recv()
        for a in range(n):
            copy(a, 0, (x, y, c), (x, y, 1 - c), own=True).wait_send()
            for j, chip in enumerate(chips):
                copy(a, 1 + j, (x, y, c), (*chip, c), own=True).wait_send()
                copy(a, 4 + j, (*chip, c), (x, y, 1 - c)).wait_send()
            pltpu.make_async_copy(ins[a], outs[a].at[4 * x + 2 * y + c], lsem.at[a]).wait()

    return types.SimpleNamespace(
        arrays=arrays, out_shape=[_sds((N_DEV,) + a.shape, a.dtype) for a in arrays],
        scratch=[pltpu.SemaphoreType.DMA((7 * n,)), pltpu.SemaphoreType.DMA((7 * n,)), pltpu.SemaphoreType.DMA((n,))],
        start=start, mid=mid, finish=finish)


def _scatter_rider(arrays):
    arrays = list(arrays)
    n = len(arrays)

    def copies(ins, outs, sems):
        ssem, rsem = sems
        x, y, c = _me()
        out = []
        for a in range(n):
            for f in range(1, N_DEV):
                px = 1 - x if f & 4 else x
                py = 1 - y if f & 2 else y
                pc = 1 - c if f & 1 else c
                out.append(pltpu.make_async_remote_copy(
                    src_ref=ins[a].at[4 * px + 2 * py + pc], dst_ref=outs[a].at[f - 1], send_sem=ssem.at[7 * a + f - 1],
                    recv_sem=rsem.at[7 * a + f - 1], device_id=(px, py, pc), device_id_type=MESH))
        return out

    def start(ins, outs, sems):
        for cp in copies(ins, outs, sems):
            cp.start()

    def finish(ins, outs, sems):
        for cp in copies(ins, outs, sems):
            cp.wait()

    return types.SimpleNamespace(
        arrays=arrays, out_shape=[_sds((N_DEV - 1,) + a.shape[1:], a.dtype) for a in arrays],
        scratch=[pltpu.SemaphoreType.DMA((7 * n,)), pltpu.SemaphoreType.DMA((7 * n,))],
        start=start, mid=None, finish=finish)


def _call(body, name, grid, in_specs, out_specs, out_shape, scratch, operands, riders=()):
    n_in, n_out, n_scr = len(operands), len(out_shape), len(scratch)
    nsteps = grid[0] if grid else 1
    sizes = [(len(r.arrays), len(r.out_shape), len(r.scratch)) for r in riders]

    def wrapped(*refs):
        pos = n_in
        r_ins = []
        for ri, _, _ in sizes:
            r_ins.append(refs[pos:pos + ri])
            pos += ri
        outs = refs[pos:pos + n_out]
        pos += n_out
        r_outs = []
        for _, ro, _ in sizes:
            r_outs.append(refs[pos:pos + ro])
            pos += ro
        scr = refs[pos:pos + n_scr]
        pos += n_scr
        r_sems = []
        for _, _, rs in sizes:
            r_sems.append(refs[pos:pos + rs])
            pos += rs
        step = pl.program_id(0) if grid else 0

        def at(s, fn):
            if grid:
                pl.when(step == s)(fn)
            else:
                fn()

        for r, a, b, c in zip(riders, r_ins, r_outs, r_sems):
            at(0, lambda r=r, a=a, b=b, c=c: r.start(a, b, c))
        for r, a, b, c in zip(riders, r_ins, r_outs, r_sems):
            if r.mid is not None:
                at((3 * nsteps) // 4, lambda r=r, a=a, b=b, c=c: r.mid(a, b, c))
        if body is not None:
            body(*refs[:n_in], *outs, *scr)
        for r, a, b, c in zip(riders, r_ins, r_outs, r_sems):
            at(nsteps - 1, lambda r=r, a=a, b=b, c=c: r.finish(a, b, c))

    r_arrays = [a for r in riders for a in r.arrays]
    r_shapes = [s for r in riders for s in r.out_shape]
    kwargs = {}
    if grid:
        kwargs = dict(grid=grid, compiler_params=pltpu.CompilerParams(
            dimension_semantics=("arbitrary",) * len(grid), vmem_limit_bytes=VMEM_LIMIT))
    res = pl.pallas_call(
        wrapped, name=name,
        in_specs=list(in_specs) + [ANY] * len(r_arrays),
        out_specs=list(out_specs) + [ANY] * len(r_shapes),
        out_shape=list(out_shape) + r_shapes,
        scratch_shapes=list(scratch) + [s for r in riders for s in r.scratch],
        **kwargs,
    )(*operands, *r_arrays)
    host, rest = res[:n_out], res[n_out:]
    r_res = []
    for _, ro, _ in sizes:
        r_res.append(rest[:ro])
        rest = rest[ro:]
    return host, r_res


def _ln_in(h, g1, w_in, keep_hn, name):
    t = h.shape[0]
    tm = _tile(t)

    def body(h_ref, g_ref, w_ref, z_ref, *hn_ref):
        y, _, _ = _rms_fwd(h_ref[...], g_ref[...])
        hn = y.astype(MX)
        if keep_hn:
            hn_ref[0][...] = hn
        z_ref[...] = jnp.dot(hn, w_ref[...], preferred_element_type=F32)

    tile = lambda w: pl.BlockSpec((tm, w), lambda i: (i, 0))
    outs, _ = _call(
        body, name, (t // tm,),
        [tile(D_MODEL), _const_spec((1, D_MODEL)), _const_spec((D_MODEL, IN_W))],
        [tile(IN_W)] + [tile(D_MODEL)] * keep_hn,
        [_sds((t, IN_W), F32)] + [_sds((t, D_MODEL), MX)] * keep_hn, [], [h, g1, w_in])
    return outs[0], (outs[1] if keep_hn else None)


def _band2(kb, g):
    lo = lax.broadcasted_iota(jnp.int32, kb.shape, 1) < HEAD_DIM
    kr = pltpu.roll(kb, HEAD_DIM, 1)
    if g == 0:
        top, bot = jnp.where(lo, kb, 0.0), jnp.where(lo, 0.0, kr)
    else:
        top, bot = jnp.where(lo, kr, 0.0), jnp.where(lo, 0.0, kb)
    return jnp.concatenate([top, bot], axis=0)


def _attn_operands(z_ref, zh_ref, b):
    rows = slice(b * BLK, (b + 1) * BLK)
    prev = zh_ref if b == 0 else z_ref
    prow = slice(0, BLK) if b == 0 else slice((b - 1) * BLK, b * BLK)
    kb = jnp.concatenate([prev[prow, K0:K0 + KV_W], z_ref[rows, K0:K0 + KV_W]], axis=0)
    vb = jnp.concatenate([prev[prow, V0:V0 + KV_W], z_ref[rows, V0:V0 + KV_W]], axis=0)
    k2 = [_band2(kb, g) for g in range(2)]
    v2 = [_band2(vb, g) for g in range(2)]
    q2 = [jnp.concatenate([z_ref[rows, (2 * g) * BLK:(2 * g + 1) * BLK], z_ref[rows, (2 * g + 1) * BLK:(2 * g + 2) * BLK]],
                          axis=0) for g in range(2)]
    return q2, k2, v2


def _attn_block(z_ref, zh_ref, sink_ref, b, first):
    q2, k2, v2 = _attn_operands(z_ref, zh_ref, b)
    rr = lax.broadcasted_iota(jnp.int32, (4 * BLK, 2 * BLK), 0) & (BLK - 1)
    cc = lax.broadcasted_iota(jnp.int32, (4 * BLK, 2 * BLK), 1)
    first_block = jnp.logical_and(first, b == 0).astype(jnp.int32)
    mask = jnp.logical_and(jnp.logical_and(cc > rr, cc <= rr + BLK), cc >= BLK * first_block)
    s = jnp.concatenate([_dot_nt(q2[g], k2[g]) for g in range(2)], axis=0) * SCALE
    w = 2 * BLK
    out, psink = [], []
    for hh in range(2):
        sh = jnp.where(mask, s[:, hh * w:(hh + 1) * w], MASK_VALUE)
        sk = jnp.concatenate([jnp.broadcast_to(sink_ref[p:p + 1, hh * w:hh * w + 1], (BLK, 1)) for p in range(4)], axis=0)
        m = jnp.maximum(jnp.max(sh, axis=1, keepdims=True), sk)
        p = jnp.exp(sh - m)
        es = jnp.exp(sk - m)
        inv = 1.0 / (jnp.sum(p, axis=1, keepdims=True) + es)
        out.append(p * inv)
        psink.append(es * inv)
    return v2, jnp.concatenate(out, axis=1), psink


def _scan_steps(a, b, n, span, reverse):
    pos = lax.broadcasted_iota(jnp.int32, a.shape, 0) & (span - 1)
    d = 1
    while d < span:
        keep = pos < span - d if reverse else pos >= d
        shift = n - d if reverse else d
        a_sh = jnp.where(keep, pltpu.roll(a, shift, 0), 1.0)
        b_sh = jnp.where(keep, pltpu.roll(b, shift, 0), 0.0)
        b = a * b_sh + b
        a = a * a_sh
        d *= 2
    return a, b


def _scan(a, b, tm, reverse):
    return _scan_steps(a, b, tm, tm, reverse)


def _shifted_copies(ext, shifts, tm):
    rows = tm + CONV_HALO - 8
    for r in range(1, 8):
        shifts[r - 1, 0:rows, :] = ext[pl.ds(r, rows), :]


def _tap(ext, shifts, off, r0, n):
    a, r = divmod(off, 8)
    lo = 8 * a + r0
    if r == 0:
        return ext[lo:lo + n, :]
    return shifts[r - 1, lo:lo + n, :]


def _glu_fill(z_ref, zh_ref, uext, ush, first, tm, sg_out=None):
    cv = z_ref[:, CV0:CV0 + CONV_W]
    sg = _sigmoid(z_ref[:, CG0:CG0 + CONV_W])
    if sg_out is not None:
        sg_out[...] = sg
    hrow = BLK - CONV_HALO
    uh = zh_ref[hrow:BLK, CV0:CV0 + CONV_W] * _sigmoid(zh_ref[hrow:BLK, CG0:CG0 + CONV_W])
    uext[0:CONV_HALO, :] = jnp.where(first, 0.0, uh)
    uext[CONV_HALO:CONV_HALO + tm, :] = cv * sg
    _shifted_copies(uext, ush, tm)


def _conv_taps(cw_ref, pv_ref, uext, ush, out_ref, tm):
    for r0 in range(0, tm, CONV_CHUNK):
        acc = jnp.broadcast_to(pv_ref[R_CONV_B:R_CONV_B + 1, :], (CONV_CHUNK, CONV_W))
        for k in range(CONV_K):
            acc = acc + cw_ref[k:k + 1, :] * _tap(uext, ush, CONV_HALO - (CONV_K - 1) + k, r0, CONV_CHUNK)
        out_ref[r0:r0 + CONV_CHUNK, :] = acc


def _ln_silu(uc, pv_ref):
    mu = jnp.mean(uc, axis=-1, keepdims=True)
    xc = uc - mu
    rs = lax.rsqrt(jnp.mean(xc * xc, axis=-1, keepdims=True) + LN_EPS)
    xh = xc * rs
    ln = xh * pv_ref[R_LN_G:R_LN_G + 1, :] + pv_ref[R_LN_B:R_LN_B + 1, :]
    sg = _sigmoid(ln)
    return xh, rs, ln, sg


def _lru_gates(z_ref, zh_ref, pv_ref, wa_ref, wx_ref, rxext, first, tm):
    rxext[0:LRU_HALO, :] = jnp.where(first, 0.0, zh_ref[BLK - LRU_HALO:BLK, RX0:RX0 + LRU_W])
    rxext[LRU_HALO:LRU_HALO + tm, :] = z_ref[:, RX0:RX0 + LRU_W]
    xc = jnp.broadcast_to(pv_ref[R_LCONV_B:R_LCONV_B + 1, :], (tm, LRU_W))
    for k in range(LRU_K):
        xc = xc + pv_ref[R_LCW + k:R_LCW + k + 1, :] * rxext[pl.ds(LRU_HALO - (LRU_K - 1) + k, tm), :]
    r = _sigmoid(_dot(xc, wa_ref[...]) + pv_ref[R_BA:R_BA + 1, :])
    ig = _sigmoid(_dot(xc, wx_ref[...]) + pv_ref[R_BX:R_BX + 1, :])
    lam = pv_ref[R_LAM:R_LAM + 1, :]
    sp = jnp.log1p(jnp.exp(-lam))
    la = (-LRU_C * r) * sp
    a = jnp.exp(la)
    mult = jnp.sqrt(_neg_expm1(2.0 * la))
    return xc, r, ig, sp, la, a, mult


def _mixer_in_specs(tm, tile_of):
    hb = tm // BLK
    return [
        pl.BlockSpec((tm, IN_W), lambda i: (tile_of(i), 0)),
        pl.BlockSpec((BLK, IN_W), lambda i: (jnp.maximum(tile_of(i) * hb - 1, 0), 0)),
        _const_spec((8, 4 * BLK)),
        _const_spec((32, CONV_W)),
        _const_spec((16, CONV_W)),
        _const_spec((LRU_W, LRU_W)),
        _const_spec((LRU_W, LRU_W)),
    ]


def _mixer_fwd(z, sink, cw, pv, wa, wx, name, riders=()):
    t = z.shape[0]
    tm = _tile(t)
    nb = tm // BLK

    def body(z_ref, zh_ref, sink_ref, cw_ref, pv_ref, wa_ref, wx_ref, y_ref, hl_ref, uc_ref, p_ref, ps_ref,
             uext, ush, rxext, hcar):
        i = pl.program_id(0)
        first = i == 0

        @pl.when(first)
        def _():
            hcar[...] = jnp.zeros_like(hcar)

        lo = lax.broadcasted_iota(jnp.int32, (4 * BLK, BLK), 1) < HEAD_DIM
        for b in range(nb):
            rows = slice(b * BLK, (b + 1) * BLK)
            v2, prob, psink = _attn_block(z_ref, zh_ref, sink_ref, b, first)
            prob = prob.astype(MX)
            p_ref[b] = prob
            ps_ref[b] = jnp.where(lo, psink[0], psink[1])
            for g in range(2):
                o = _dot(prob[2 * g * BLK:(2 * g + 2) * BLK], v2[g])
                y_ref[rows, (2 * g) * BLK:(2 * g + 1) * BLK] = o[0:BLK]
                y_ref[rows, (2 * g + 1) * BLK:(2 * g + 2) * BLK] = o[BLK:2 * BLK]
        _glu_fill(z_ref, zh_ref, uext, ush, first, tm)
        _conv_taps(cw_ref, pv_ref, uext, ush, uc_ref, tm)
        _, _, ln, sg = _ln_silu(uc_ref[...], pv_ref)
        y_ref[:, ATTN_W:ATTN_W + CONV_W] = ln * sg
        xc, _, ig, _, _, a, mult = _lru_gates(z_ref, zh_ref, pv_ref, wa_ref, wx_ref, rxext, first, tm)
        acum, h = _scan(a, mult * (ig * xc), tm, reverse=False)
        h = h + acum * hcar[0:1, :]
        hl_ref[...] = h
        hcar[0:1, :] = h[tm - 1:tm, :]
        gl, _ = _gelu(z_ref[:, RG0:RG0 + LRU_W])
        y_ref[:, ATTN_W + CONV_W:ATTN_W + CONV_W + LRU_W] = h * gl

    tile = lambda w: pl.BlockSpec((tm, w), lambda i: (i, 0))
    return _call(
        body, name, (t // tm,), _mixer_in_specs(tm, lambda i: i),
        [tile(D_MODEL), tile(LRU_W), tile(CONV_W), pl.BlockSpec((nb, 4 * BLK, 4 * BLK), lambda i: (i, 0, 0)),
         pl.BlockSpec((nb, 4 * BLK, BLK), lambda i: (i, 0, 0))],
        [_sds((t, D_MODEL), F32), _sds((t, LRU_W), F32), _sds((t, CONV_W), F32),
         _sds((t // BLK, 4 * BLK, 4 * BLK), MX), _sds((t // BLK, 4 * BLK, BLK), F32)],
        [pltpu.VMEM((tm + CONV_HALO, CONV_W), F32), pltpu.VMEM((7, tm + CONV_HALO - 8, CONV_W), F32),
         pltpu.VMEM((tm + LRU_HALO, LRU_W), F32), pltpu.VMEM((8, LRU_W), F32)],
        [z, z, sink, cw, pv, wa, wx], riders)


def _mixer_bwd(dy, z, ycat, hl, uc, probs, psinks, sink, cw, pv, wa, wx, name, riders=()):
    t = z.shape[0]
    tm = _tile(t)
    nt = t // tm
    nb = tm // BLK
    rev = lambda i: nt - 1 - i

    def body(dy_ref, z_ref, zh_ref, sink_ref, cw_ref, pv_ref, wa_ref, wx_ref, y_ref, hl_ref, hlh_ref, uc_ref,
             p_ref, ps_ref, dz_ref, dsink_ref, dcw_ref, dpv_ref, dwa_ref, dwx_ref,
             uext, ush, sgs, rxext, dkext, dvext, ducext, dsh, dcw8, dxcext, kcar, vcar, uccar, xccar, gcar):
        i = pl.program_id(0)
        first = i == nt - 1

        @pl.when(i == 0)
        def _():
            for car in (kcar, vcar, uccar, xccar, gcar, dcw8):
                car[...] = jnp.zeros_like(car)
            for acc in (dsink_ref, dpv_ref, dwa_ref, dwx_ref):
                acc[...] = jnp.zeros_like(acc)

        def addrow(r, val):
            dpv_ref[r:r + 1, :] += jnp.sum(val, axis=0, keepdims=True)

        dkext[:, 0:tm] = jnp.zeros((KV_W, tm), F32)
        dvext[:, 0:tm] = jnp.zeros((KV_W, tm), F32)
        dkext[:, tm:tm + BLK] = kcar[...]
        dvext[:, tm:tm + BLK] = vcar[...]
        lane512 = lax.broadcasted_iota(jnp.int32, (1, 4 * BLK), 1) < 2 * BLK
        lo = lax.broadcasted_iota(jnp.int32, (4 * BLK, BLK), 1) < HEAD_DIM
        hd, w2 = HEAD_DIM, 2 * BLK
        for b in range(nb):
            rows = slice(b * BLK, (b + 1) * BLK)
            band = slice(b * BLK, (b + 2) * BLK)
            q2, k2, v2 = _attn_operands(z_ref, zh_ref, b)
            prob = p_ref[b]
            psink = [ps_ref[b, :, 0:1], ps_ref[b, :, HEAD_DIM:HEAD_DIM + 1]]
            stack = lambda ref: jnp.concatenate([ref[rows, p * BLK:(p + 1) * BLK] for p in range(4)], axis=0)
            do4 = stack(dy_ref)
            dlt = do4 * stack(y_ref)
            d0 = jnp.sum(jnp.where(lo, dlt, 0.0), axis=1, keepdims=True)
            d1 = jnp.sum(jnp.where(lo, 0.0, dlt), axis=1, keepdims=True)
            dp = jnp.concatenate([_dot_nt(do4[g * w2:(g + 1) * w2], v2[g]) for g in range(2)], axis=0)
            dl = jnp.concatenate([jnp.broadcast_to(d0, (4 * BLK, w2)), jnp.broadcast_to(d1, (4 * BLK, w2))], axis=1)
            draw = (prob * (dp - dl)) * SCALE
            e0, e1 = psink[0] * d0, psink[1] * d1
            for p in range(4):
                prs = slice(p * BLK, (p + 1) * BLK)
                s0 = jnp.sum(e0[prs], axis=0, keepdims=True)
                s1 = jnp.sum(e1[prs], axis=0, keepdims=True)
                dsink_ref[p:p + 1, :] += -jnp.where(lane512, s0, s1)
            for g in range(2):
                grs = slice(g * w2, (g + 1) * w2)
                dq = _dot(draw[grs], k2[g])
                dz_ref[rows, (2 * g) * BLK:(2 * g + 1) * BLK] = dq[0:BLK].astype(dz_ref.dtype)
                dz_ref[rows, (2 * g + 1) * BLK:(2 * g + 2) * BLK] = dq[BLK:2 * BLK].astype(dz_ref.dtype)
                tk = _dot_tn(q2[g], draw[grs])
                tv = _dot_tn(do4[grs], prob[grs])
                dkext[g * hd:(g + 1) * hd, band] += tk[0:hd, 0:w2] + tk[hd:2 * hd, w2:2 * w2]
                dvext[g * hd:(g + 1) * hd, band] += tv[0:hd, 0:w2] + tv[hd:2 * hd, w2:2 * w2]
        dz_ref[:, K0:K0 + KV_W] = jnp.transpose(dkext[:, BLK:BLK + tm]).astype(dz_ref.dtype)
        dz_ref[:, V0:V0 + KV_W] = jnp.transpose(dvext[:, BLK:BLK + tm]).astype(dz_ref.dtype)
        kcar[...] = dkext[:, 0:BLK]
        vcar[...] = dvext[:, 0:BLK]

        _glu_fill(z_ref, zh_ref, uext, ush, first, tm, sg_out=sgs)
        xh, rs, ln, sg = _ln_silu(uc_ref[...], pv_ref)
        dln = dy_ref[:, ATTN_W:ATTN_W + CONV_W] * (sg * (1.0 + ln * (1.0 - sg)))
        addrow(R_LN_G, dln * xh)
        addrow(R_LN_B, dln)
        dxh = dln * pv_ref[R_LN_G:R_LN_G + 1, :]
        duc = rs * (dxh - jnp.mean(dxh, axis=-1, keepdims=True) - xh * jnp.mean(dxh * xh, axis=-1, keepdims=True))
        addrow(R_CONV_B, duc)
        ducext[0:tm, :] = duc
        ducext[tm:tm + CONV_HALO, :] = uccar[...]
        uccar[...] = duc[0:CONV_HALO, :]
        _shifted_copies(ducext, dsh, tm)
        for r0 in range(0, tm, CONV_CHUNK):
            crow = slice(r0, r0 + CONV_CHUNK)
            duc_c = ducext[crow, :]
            du = jnp.zeros((CONV_CHUNK, CONV_W), F32)
            for k in range(CONV_K):
                prod = duc_c * _tap(uext, ush, CONV_HALO - (CONV_K - 1) + k, r0, CONV_CHUNK)
                part = prod[0:8]
                for s in range(8, CONV_CHUNK, 8):
                    part = part + prod[s:s + 8]
                dcw8[k] += part
                du = du + cw_ref[k:k + 1, :] * _tap(ducext, dsh, CONV_K - 1 - k, r0, CONV_CHUNK)
            sgc = sgs[crow, :]
            dz_ref[crow, CV0:CV0 + CONV_W] = (du * sgc).astype(dz_ref.dtype)
            u_c = uext[CONV_HALO + r0:CONV_HALO + r0 + CONV_CHUNK, :]
            dz_ref[crow, CG0:CG0 + CONV_W] = (du * u_c * (1.0 - sgc)).astype(dz_ref.dtype)

        @pl.when(i == nt - 1)
        def _():
            dcw_ref[...] = jnp.sum(dcw8[...], axis=1)

        xc, r, ig, sp, la, a, mult = _lru_gates(z_ref, zh_ref, pv_ref, wa_ref, wx_ref, rxext, first, tm)
        h = hl_ref[...]
        rowi = lax.broadcasted_iota(jnp.int32, (tm, LRU_W), 0)
        hlast = jnp.where(first, 0.0, hlh_ref[7:8, :])
        hprev = jnp.where(rowi == 0, hlast, pltpu.roll(h, 1, 0))
        dyl = dy_ref[:, ATTN_W + CONV_W:ATTN_W + CONV_W + LRU_W]
        gl, dgl = _gelu(z_ref[:, RG0:RG0 + LRU_W])
        dz_ref[:, RG0:RG0 + LRU_W] = (dyl * h * dgl).astype(dz_ref.dtype)
        dh = dyl * gl + jnp.where(rowi == tm - 1, gcar[0:1, :], 0.0)
        c = jnp.where(rowi == tm - 1, 0.0, pltpu.roll(a, tm - 1, 0))
        _, gg = _scan(c, dh, tm, reverse=True)
        gcar[0:1, :] = a[0:1, :] * gg[0:1, :]
        dmult = gg * (ig * xc)
        dig = gg * mult * xc
        dxc = gg * mult * ig
        dla = gg * hprev * a - dmult * a * a / mult
        dr = dla * (-LRU_C * sp)
        lam = pv_ref[R_LAM:R_LAM + 1, :]
        dpv_ref[R_LAM:R_LAM + 1, :] += jnp.sum(dla * (-LRU_C * r), axis=0, keepdims=True) * (-_sigmoid(-lam))
        dpa = dr * r * (1.0 - r)
        dpx = dig * ig * (1.0 - ig)
        addrow(R_BA, dpa)
        addrow(R_BX, dpx)
        dxc = dxc + _dot_nt(dpa, wa_ref[...]) + _dot_nt(dpx, wx_ref[...])
        dwa_ref[...] += _dot_tn(xc, dpa)
        dwx_ref[...] += _dot_tn(xc, dpx)
        addrow(R_LCONV_B, dxc)
        dxcext[0:tm, :] = dxc
        dxcext[tm:tm + LRU_HALO, :] = xccar[...]
        xccar[...] = dxc[0:LRU_HALO, :]
        drx = jnp.zeros((tm, LRU_W), F32)
        for k in range(LRU_K):
            addrow(R_LCW + k, dxc * rxext[pl.ds(LRU_HALO - (LRU_K - 1) + k, tm), :])
            drx = drx + pv_ref[R_LCW + k:R_LCW + k + 1, :] * dxcext[pl.ds(LRU_K - 1 - k, tm), :]
        dz_ref[:, RX0:RX0 + LRU_W] = drx.astype(dz_ref.dtype)

    tile = lambda w: pl.BlockSpec((tm, w), lambda i: (rev(i), 0))
    in_specs = [tile(D_MODEL)] + _mixer_in_specs(tm, rev) + [
        tile(D_MODEL), tile(LRU_W),
        pl.BlockSpec((8, LRU_W), lambda i: (jnp.maximum(rev(i) * (tm // 8) - 1, 0), 0)),
        tile(CONV_W), pl.BlockSpec((nb, 4 * BLK, 4 * BLK), lambda i: (rev(i), 0, 0)),
        pl.BlockSpec((nb, 4 * BLK, BLK), lambda i: (rev(i), 0, 0))]
    return _call(
        body, name, (nt,), in_specs,
        [tile(IN_W), _acc_spec((8, 4 * BLK)), _acc_spec((32, CONV_W)), _acc_spec((16, CONV_W)),
         _acc_spec((LRU_W, LRU_W)), _acc_spec((LRU_W, LRU_W))],
        [_sds((t, IN_W), MX), _sds((8, 4 * BLK), F32), _sds((32, CONV_W), F32), _sds((16, CONV_W), F32),
         _sds((LRU_W, LRU_W), F32), _sds((LRU_W, LRU_W), F32)],
        [pltpu.VMEM((tm + CONV_HALO, CONV_W), F32), pltpu.VMEM((7, tm + CONV_HALO - 8, CONV_W), F32),
         pltpu.VMEM((tm, CONV_W), F32), pltpu.VMEM((tm + LRU_HALO, LRU_W), F32),
         pltpu.VMEM((KV_W, tm + BLK), F32), pltpu.VMEM((KV_W, tm + BLK), F32),
         pltpu.VMEM((tm + CONV_HALO, CONV_W), F32), pltpu.VMEM((7, tm + CONV_HALO - 8, CONV_W), F32),
         pltpu.VMEM((32, 8, CONV_W), F32), pltpu.VMEM((tm + LRU_HALO, LRU_W), F32),
         pltpu.VMEM((KV_W, BLK), F32), pltpu.VMEM((KV_W, BLK), F32),
         pltpu.VMEM((CONV_HALO, CONV_W), F32), pltpu.VMEM((LRU_HALO, LRU_W), F32), pltpu.VMEM((8, LRU_W), F32)],
        [dy, z, z, sink, cw, pv, wa, wx, ycat, hl, hl, uc, probs, psinks], riders)


def _post_fwd(ycat, h0, gmix, w_out, g2, w_up, w_down, name, riders=()):
    t = h0.shape[0]
    tm = _tile(t, POST_TILE)
    nj = D_FF // FF_BLK

    def body(y_ref, h_ref, gm_ref, wo_ref, g2_ref, wu_ref, wd_ref, h1_ref, a_ref, h2_ref, ym_ref, hn_ref):
        ym, _, _ = _group_rms_fwd(y_ref[...], gm_ref[...])
        ym = ym.astype(MX)
        ym_ref[...] = ym
        h1 = h_ref[...] + jnp.dot(ym, wo_ref[...], preferred_element_type=F32)
        h1_ref[...] = h1
        hn, _, _ = _rms_fwd(h1, g2_ref[...])
        hn = hn.astype(MX)
        hn_ref[...] = hn
        for j in range(nj):
            u = jnp.dot(hn, wu_ref[j], preferred_element_type=F32)
            a_ref[:, j * FF_BLK:(j + 1) * FF_BLK] = jnp.square(jnp.maximum(u, 0.0)).astype(MX)
        h2_ref[...] = h1 + jnp.dot(a_ref[...], wd_ref[...], preferred_element_type=F32)

    tile = lambda w: pl.BlockSpec((tm, w), lambda i: (i, 0))
    return _call(
        body, name, (t // tm,),
        [tile(D_MODEL), tile(D_MODEL), _const_spec((1, D_MODEL)), _const_spec((D_MODEL, D_MODEL)),
         _const_spec((1, D_MODEL)), _const_spec((nj, D_MODEL, FF_BLK)), _const_spec((D_FF, D_MODEL))],
        [tile(D_MODEL), tile(D_FF), tile(D_MODEL), tile(D_MODEL), tile(D_MODEL)],
        [_sds((t, D_MODEL), F32), _sds((t, D_FF), MX), _sds((t, D_MODEL), F32), _sds((t, D_MODEL), MX),
         _sds((t, D_MODEL), MX)],
        [], [ycat, h0, gmix, w_out, g2, w_up, w_down], riders)


def _ffn_bwd(dh2, act, h1, g2, w_up_t, w_down, name, riders=()):
    t = h1.shape[0]
    tm = _tile(t, POST_TILE)
    nj = D_FF // FF_BLK

    def body(dh2_ref, a_ref, h1_ref, g2_ref, wut_ref, wd_ref, dh1_ref, dh1b_ref, dh2b_ref, du_ref, dg2_ref):
        @pl.when(pl.program_id(0) == 0)
        def _():
            dg2_ref[...] = jnp.zeros_like(dg2_ref)

        dh2 = dh2_ref[...]
        dh2b = dh2.astype(MX)
        dh2b_ref[...] = dh2b
        for j in range(nj):
            cols = slice(j * FF_BLK, (j + 1) * FF_BLK)
            da = _dot_nt(dh2b, wd_ref[j])
            du_ref[:, cols] = (da * (2.0 * jnp.sqrt(a_ref[:, cols].astype(F32)))).astype(MX)
        dhn = jnp.dot(du_ref[...], wut_ref[...], preferred_element_type=F32)
        _, xh, r = _rms_fwd(h1_ref[...], g2_ref[...])
        dx, dg = _rms_bwd(dhn, xh, r, g2_ref[...])
        dg2_ref[...] += dg
        dh1 = dh2 + dx
        dh1_ref[...] = dh1
        dh1b_ref[...] = dh1.astype(MX)

    tile = lambda w: pl.BlockSpec((tm, w), lambda i: (i, 0))
    return _call(
        body, name, (t // tm,),
        [tile(D_MODEL), tile(D_FF), tile(D_MODEL), _const_spec((1, D_MODEL)),
         _const_spec((D_FF, D_MODEL)), _const_spec((nj, FF_BLK, D_MODEL))],
        [tile(D_MODEL), tile(D_MODEL), tile(D_MODEL), tile(D_FF), _acc_spec((1, D_MODEL))],
        [_sds((t, D_MODEL), F32), _sds((t, D_MODEL), MX), _sds((t, D_MODEL), MX), _sds((t, D_FF), MX),
         _sds((1, D_MODEL), F32)],
        [], [dh2, act, h1, g2, w_up_t, w_down], riders)


def _mix_bwd(dh1, ycat, ym, gmix, w_out, name):
    t = dh1.shape[0]
    tm = _tile(t)
    nk = t // tm
    r = D_MODEL // N_DEV

    def body(dh1_ref, y_ref, ym_ref, gm_ref, wo_ref, dy_ref, dgm_ref, o_ref, o16_ref, acc):
        k = pl.program_id(0)

        @pl.when(k == 0)
        def _():
            dgm_ref[...] = jnp.zeros_like(dgm_ref)
            acc[...] = jnp.zeros_like(acc)

        dh = dh1_ref[...]
        acc[...] += _dot_tn(ym_ref[...], dh)
        dym = _dot_nt(dh, wo_ref[...])
        gm = gm_ref[...]
        _, yh, rr = _group_rms_fwd(y_ref[...], gm)
        outs, dgs = [], []
        for (a, b), rg in zip(_GROUPS, rr):
            dxg, dgg = _rms_bwd(dym[:, a:b], yh[:, a:b], rg, gm[:, a:b])
            outs.append(dxg)
            dgs.append(dgg)
        dy_ref[...] = jnp.concatenate(outs, axis=1)
        dgm_ref[...] += jnp.concatenate(dgs, axis=1)

        @pl.when(k == nk - 1)
        def _():
            for d in range(N_DEV):
                v = acc[d * r:(d + 1) * r, :]
                o_ref[d] = v
                o16_ref[d] = v.astype(o16_ref.dtype)

    tile = pl.BlockSpec((tm, D_MODEL), lambda i: (i, 0))
    slabs = _const_spec((N_DEV, r, D_MODEL))
    (dy, dgm, dw, dw16), _ = _call(
        body, name, (nk,), [tile, tile, tile, _const_spec((1, D_MODEL)), _const_spec((D_MODEL, D_MODEL))],
        [tile, _acc_spec((1, D_MODEL)), slabs, slabs],
        [_sds((t, D_MODEL), F32), _sds((1, D_MODEL), F32), _sds((N_DEV, r, D_MODEL), F32),
         _sds((N_DEV, r, D_MODEL), WIRE)],
        [pltpu.VMEM((D_MODEL, D_MODEL), F32)], [dh1, ycat, ym, gmix, w_out])
    return dy, dgm, (dw, dw16)


def _in_bwd(dz, h0, dh1, g1, w_in, after, name):
    t = h0.shape[0]
    tm = _tile(t)

    def body(dz_ref, h_ref, dh1_ref, g_ref, w_ref, after_ref, dh0_ref, dg_ref):
        @pl.when(pl.program_id(0) == 0)
        def _():
            dg_ref[...] = jnp.zeros_like(dg_ref)

        dhn = _dot_nt(dz_ref[...], w_ref[...])
        _, xh, r = _rms_fwd(h_ref[...], g_ref[...])
        dx, dg = _rms_bwd(dhn, xh, r, g_ref[...])
        dg_ref[...] += dg
        dh0_ref[...] = dh1_ref[...] + dx

    tile = lambda w: pl.BlockSpec((tm, w), lambda i: (i, 0))
    (dh0, dg), _ = _call(
        body, name, (t // tm,),
        [tile(IN_W), tile(D_MODEL), tile(D_MODEL), _const_spec((1, D_MODEL)), _const_spec((D_MODEL, IN_W)),
         _const_spec((8, 128))],
        [tile(D_MODEL), _acc_spec((1, D_MODEL))], [_sds((t, D_MODEL), F32), _sds((1, D_MODEL), F32)],
        [], [dz, h0, dh1, g1, w_in, after])
    return dh0, dg


def _in_bwd_dw(dz, h0, dh1, g1, w_in, name):
    t = h0.shape[0]
    tm = _tile(t)
    nk = t // tm

    def body(dz_ref, h_ref, dh1_ref, g_ref, w_ref, dh0_ref, dg_ref, o_ref, o16_ref, acc):
        k = pl.program_id(0)

        @pl.when(k == 0)
        def _():
            dg_ref[...] = jnp.zeros_like(dg_ref)
            acc[...] = jnp.zeros_like(acc)

        dz_t = dz_ref[...]
        hn, xh, r = _rms_fwd(h_ref[...], g_ref[...])
        acc[...] += _dot_tn(hn, dz_t)
        dhn = _dot_nt(dz_t, w_ref[...])
        dx, dg = _rms_bwd(dhn, xh, r, g_ref[...])
        dg_ref[...] += dg
        dh0_ref[...] = dh1_ref[...] + dx

        @pl.when(k == nk - 1)
        def _():
            for d in range(N_DEV):
                v = acc[:, d * IN_SHARD:(d + 1) * IN_SHARD]
                o_ref[d] = v
                o16_ref[d] = v.astype(o16_ref.dtype)

    tile = lambda w: pl.BlockSpec((tm, w), lambda i: (i, 0))
    slabs = _const_spec((N_DEV, D_MODEL, IN_SHARD))
    (dh0, dg, dw, dw16), _ = _call(
        body, name, (nk,),
        [tile(IN_W), tile(D_MODEL), tile(D_MODEL), _const_spec((1, D_MODEL)), _const_spec((D_MODEL, IN_W))],
        [tile(D_MODEL), _acc_spec((1, D_MODEL)), slabs, slabs],
        [_sds((t, D_MODEL), F32), _sds((1, D_MODEL), F32), _sds((N_DEV, D_MODEL, IN_SHARD), F32),
         _sds((N_DEV, D_MODEL, IN_SHARD), WIRE)],
        [pltpu.VMEM((D_MODEL, IN_W), F32)], [dz, h0, dh1, g1, w_in])
    return dh0, dg, (dw, dw16)


def _loss_head(h, gf, target, name):
    t = h.shape[0]
    tm = _tile(t)

    def body(h_ref, g_ref, t_ref, dh_ref, loss_ref, dg_ref):
        @pl.when(pl.program_id(0) == 0)
        def _():
            loss_ref[...] = jnp.zeros_like(loss_ref)
            dg_ref[...] = jnp.zeros_like(dg_ref)

        g = g_ref[...]
        y, xh, r = _rms_fwd(h_ref[...], g)
        err = y - t_ref[...]
        part = 0.5 * jnp.sum(jnp.mean(err * err, axis=-1, keepdims=True), axis=0, keepdims=True)
        loss_ref[...] += jnp.broadcast_to(part, loss_ref.shape)
        dx, dg = _rms_bwd(err * (1.0 / D_MODEL), xh, r, g)
        dg_ref[...] += dg
        dh_ref[...] = dx

    tile = pl.BlockSpec((tm, D_MODEL), lambda i: (i, 0))
    (dh, loss, dg), _ = _call(
        body, name, (t // tm,), [tile, _const_spec((1, D_MODEL)), tile],
        [tile, _acc_spec((1, 128)), _acc_spec((1, D_MODEL))],
        [_sds((t, D_MODEL), F32), _sds((1, 128), F32), _sds((1, D_MODEL), F32)], [], [h, gf, target])
    return dh, loss, dg


def _dw(x, y, name, split, bm, bn):
    t, m = x.shape
    n = y.shape[1]
    tk = _tile(t, DW_TILE)
    nk = t // tk
    if split == "rows":
        assert bn == n
        r, c = m // N_DEV, n
        per = bm // r
        out_block = pl.BlockSpec((per, r, c), lambda a, b, k: (a, 0, 0))
    else:
        assert bm == m
        r, c = m, n // N_DEV
        per = bn // c
        out_block = pl.BlockSpec((per, r, c), lambda a, b, k: (b, 0, 0))

    def body(x_ref, y_ref, o_ref, o16_ref, acc):
        k = pl.program_id(2)

        @pl.when(k == 0)
        def _():
            acc[...] = jnp.zeros_like(acc)

        acc[...] += _dot_tn(x_ref[...], y_ref[...])

        @pl.when(k == nk - 1)
        def _():
            for d in range(per):
                v = acc[d * r:(d + 1) * r, :] if split == "rows" else acc[:, d * c:(d + 1) * c]
                o_ref[d] = v
                o16_ref[d] = v.astype(o16_ref.dtype)

    return pl.pallas_call(
        body, name=name, grid=(m // bm, n // bn, nk),
        in_specs=[pl.BlockSpec((tk, bm), lambda a, b, k: (k, a)), pl.BlockSpec((tk, bn), lambda a, b, k: (k, b))],
        out_specs=[out_block, out_block],
        out_shape=[_sds((N_DEV, r, c), F32), _sds((N_DEV, r, c), WIRE)],
        scratch_shapes=[pltpu.VMEM((bm, bn), F32)],
        compiler_params=pltpu.CompilerParams(dimension_semantics=("arbitrary",) * 3, vmem_limit_bytes=VMEM_LIMIT),
    )(x, y)


def _adamw_math(w, g, m, v):
    m = ADAM_B1 * m + (1.0 - ADAM_B1) * g
    v = ADAM_B2 * v + (1.0 - ADAM_B2) * jnp.square(g)
    m_hat = m / (1.0 - ADAM_B1 ** ADAM_STEP)
    v_hat = v / (1.0 - ADAM_B2 ** ADAM_STEP)
    delta = -ADAM_LR * (m_hat / (jnp.sqrt(v_hat) + ADAM_EPS) + ADAM_WD * w)
    return delta, m, v


def _adamw_shard(g_own, g_recv, dev, w, m, v, after, name):
    _, r, c = w.shape
    br = r
    for cand in (256, 128, 112, 64, 56, 32, 16, 8):
        if r % cand == 0:
            br = cand
            break
    nr = r // br
    own = lambda l: pl.BlockSpec((1, br, c), lambda ll, i, d: (d[0], jnp.where(ll == l, i, (nr - 1) * (1 - l)), 0))
    recv = lambda l: pl.BlockSpec((N_DEV - 1, br, c), lambda ll, i, d: (0, jnp.where(ll == l, i, (nr - 1) * (1 - l)), 0))

    def body(dev_ref, go0, gr0, go1, gr1, w_ref, m_ref, v_ref, after_ref, g_out, d_out, m_out, v_out):
        def update(go_ref, gr_ref):
            g = go_ref[0]
            for j in range(N_DEV - 1):
                g = g + gr_ref[j].astype(F32)
            delta, mn, vn = _adamw_math(w_ref[0], g, m_ref[0], v_ref[0])
            g_out[0] = g
            d_out[0] = delta
            m_out[0] = mn
            v_out[0] = vn

        layer = pl.program_id(0)
        pl.when(layer == 0)(lambda: update(go0, gr0))
        pl.when(layer == 1)(lambda: update(go1, gr1))

    tile = pl.BlockSpec((1, br, c), lambda ll, i, d: (ll, i, 0))
    return pl.pallas_call(
        body, name=name,
        grid_spec=pltpu.PrefetchScalarGridSpec(
            num_scalar_prefetch=1, grid=(2, nr),
            in_specs=[own(0), recv(0), own(1), recv(1), tile, tile, tile,
                      pl.BlockSpec((8, 128), lambda ll, i, d: (0, 0))],
            out_specs=[tile, tile, tile, tile]),
        out_shape=[_sds((2, r, c), F32)] * 4,
        compiler_params=pltpu.CompilerParams(dimension_semantics=("arbitrary",) * 2, vmem_limit_bytes=VMEM_LIMIT),
    )(dev, g_own[0], g_recv[0], g_own[1], g_recv[1], w, m, v, after)


def _adamw_small(gs, ws, ms, vs, name):
    n = len(gs)

    def body(*refs):
        g_refs, w_refs, m_refs, v_refs = (refs[k * n:(k + 1) * n] for k in range(4))
        outs = refs[4 * n:]
        for k in range(n):
            delta, mn, vn = _adamw_math(w_refs[k][...], g_refs[k][...], m_refs[k][...], v_refs[k][...])
            outs[k][...] = delta
            outs[n + k][...] = mn
            outs[2 * n + k][...] = vn

    shapes = [_sds(w.shape, F32) for w in ws]
    res = pl.pallas_call(body, name=name, out_shape=shapes * 3,
                         compiler_params=pltpu.CompilerParams(vmem_limit_bytes=VMEM_LIMIT))(*gs, *ws, *ms, *vs)
    return res[:n], res[n:2 * n], res[2 * n:]


def _sum_parts(part, dev, name):
    def body(dev_ref, p_ref, o_ref):
        me = dev_ref[0]
        g = p_ref[me]
        for d in range(1, N_DEV):
            g = g + p_ref[jnp.bitwise_xor(me, d)]
        o_ref[...] = g

    full = pl.BlockSpec(part.shape, lambda i, d: (0, 0, 0))
    return pl.pallas_call(
        body, name=name,
        grid_spec=pltpu.PrefetchScalarGridSpec(
            num_scalar_prefetch=1, grid=(1,), in_specs=[full],
            out_specs=pl.BlockSpec(part.shape[1:], lambda i, d: (0, 0))),
        out_shape=_sds(part.shape[1:], F32))(dev, part)


HBM = pl.BlockSpec(memory_space=pltpu.HBM)
SEM = pl.BlockSpec(memory_space=pltpu.SEMAPHORE)
EFFECT = pltpu.SideEffectType.DATAFLOW_SIDE_EFFECTING


def _direct_copies(srcs, lands, ssem, rsem, scatter):
    x, y, c = _me()
    out = []
    for a in range(len(srcs)):
        for f in range(1, N_DEV):
            px = 1 - x if f & 4 else x
            py = 1 - y if f & 2 else y
            pc = 1 - c if f & 1 else c
            out.append(pltpu.make_async_remote_copy(
                src_ref=srcs[a].at[4 * px + 2 * py + pc] if scatter else srcs[a], dst_ref=lands[a].at[f - 1],
                send_sem=ssem.at[7 * a + f - 1], recv_sem=rsem.at[7 * a + f - 1],
                device_id=(px, py, pc), device_id_type=MESH))
    return out


def _send_start(arrays, scatter, name):
    arrays = list(arrays)
    n = len(arrays)
    lands = [lax.empty((N_DEV - 1,) + (a.shape[1:] if scatter else a.shape), a.dtype) for a in arrays]

    def body(*refs):
        srcs, lnds, ssem, rsem, token = refs[:n], refs[n:2 * n], refs[2 * n], refs[2 * n + 1], refs[-1]
        for cp in _direct_copies(srcs, lnds, ssem, rsem, scatter):
            cp.start()
        token[...] = jnp.zeros_like(token)

    hbm = lambda a: pltpu.HBM(a.shape, a.dtype)
    res = pl.pallas_call(
        body, name=name,
        out_shape=(pltpu.SemaphoreType.DMA((7 * n,)), pltpu.SemaphoreType.DMA((7 * n,)),
                   *[hbm(a) for a in arrays + lands], _sds((8, 128), F32)),
        in_specs=[HBM] * (2 * n),
        out_specs=(SEM, SEM, *[HBM] * (2 * n), pl.BlockSpec(memory_space=pltpu.VMEM)),
        input_output_aliases={i: 2 + i for i in range(2 * n)},
        compiler_params=pltpu.CompilerParams(has_side_effects=EFFECT),
    )(*[pltpu.with_memory_space_constraint(a, pltpu.HBM) for a in arrays + lands])
    return types.SimpleNamespace(ssem=res[0], rsem=res[1], srcs=list(res[2:2 + n]), lands=list(res[2 + n:2 + 2 * n]),
                                 token=res[-1], scatter=scatter)


def _send_wait(h, after, name):
    n = len(h.srcs)

    def body(*refs):
        srcs, lnds, ssem, rsem = refs[:n], refs[n:2 * n], refs[2 * n], refs[2 * n + 1]
        for cp in _direct_copies(srcs, lnds, ssem, rsem, h.scatter):
            cp.wait_send()
            cp.wait_recv()

    hbm = lambda a: pltpu.HBM(a.shape, a.dtype)
    res = pl.pallas_call(
        body, name=name,
        out_shape=tuple(hbm(a) for a in h.srcs + h.lands),
        in_specs=[HBM] * (2 * n) + [SEM, SEM, ANY], out_specs=[HBM] * (2 * n),
        input_output_aliases={i: i for i in range(2 * n)},
        compiler_params=pltpu.CompilerParams(has_side_effects=EFFECT),
    )(*h.srcs, *h.lands, h.ssem, h.rsem, after)
    return list(res[:n]), list(res[n:])


def _block_diag(w):
    out = jnp.zeros((LRU_W, LRU_W), w.dtype)
    for h in range(4):
        out = lax.dynamic_update_slice(out, w[h], (h * 64, h * 64))
    return out


def _unblock_diag(w):
    return jnp.concatenate([w[h * 64:(h + 1) * 64, h * 64:(h + 1) * 64] for h in range(4)], axis=0)


def _layer_params(p, l):
    row = lambda a: a[l].reshape(1, -1)
    sink_rows = jnp.repeat(p["attn_sinks"][l].reshape(4, 2), 2 * BLK, axis=1)
    sink_rows = jnp.concatenate([sink_rows, jnp.zeros((4, 4 * BLK), F32)], axis=0)
    cw = jnp.concatenate([p["conv_dw_w"][l], jnp.zeros((1, CONV_W), F32)], axis=0)
    pv = jnp.concatenate([
        row(p["conv_dw_b"]), row(p["conv_ln_g"]), row(p["conv_ln_b"]), row(p["lru_conv_b"]), row(p["lru_ba"]),
        row(p["lru_bx"]), row(p["lru_lambda"]), jnp.zeros((1, LRU_W), F32), p["lru_conv_w"][l],
        jnp.zeros((4, LRU_W), F32)], axis=0)
    return dict(
        g1=row(p["norm1"]), sink=sink_rows, cw=cw, pv=pv,
        wa=_block_diag(p["lru_wa"][l]).astype(MX), wx=_block_diag(p["lru_wx"][l]).astype(MX),
        gmix=row(p["mix_norm"]), g2=row(p["norm2"]))


_SMALL = ["norm1", "attn_sinks", "conv_dw_w", "conv_dw_b", "conv_ln_g", "conv_ln_b", "lru_conv_w", "lru_conv_b",
          "lru_wa", "lru_ba", "lru_wx", "lru_bx", "lru_lambda", "mix_norm", "norm2"]
_BIG = ["w_in", "w_out", "w_up", "w_down"]
_WEIGHTS = ["norm1", "w_in", "attn_sinks", "conv_dw_w", "conv_dw_b", "conv_ln_g", "conv_ln_b", "lru_conv_w",
            "lru_conv_b", "lru_wa", "lru_ba", "lru_wx", "lru_bx", "lru_lambda", "mix_norm", "w_out", "norm2", "w_up",
            "w_down", "final_norm"]


def kernel(x, norm1, w_in, attn_sinks, conv_dw_w, conv_dw_b, conv_ln_g, conv_ln_b, lru_conv_w, lru_conv_b, lru_wa, lru_ba, lru_wx, lru_bx, lru_lambda, mix_norm, w_out, norm2, w_up, w_down, final_norm, loss_target, m_norm1, m_w_in, m_attn_sinks, m_conv_dw_w, m_conv_dw_b, m_conv_ln_g, m_conv_ln_b, m_lru_conv_w, m_lru_conv_b, m_lru_wa, m_lru_ba, m_lru_wx, m_lru_bx, m_lru_lambda, m_mix_norm, m_w_out, m_norm2, m_w_up, m_w_down, m_final_norm, v_norm1, v_w_in, v_attn_sinks, v_conv_dw_w, v_conv_dw_b, v_conv_ln_g, v_conv_ln_b, v_lru_conv_w, v_lru_conv_b, v_lru_wa, v_lru_ba, v_lru_wx, v_lru_bx, v_lru_lambda, v_mix_norm, v_w_out, v_norm2, v_w_up, v_w_down, v_final_norm):
    w = dict(norm1=norm1, w_in=w_in, attn_sinks=attn_sinks, conv_dw_w=conv_dw_w, conv_dw_b=conv_dw_b,
             conv_ln_g=conv_ln_g, conv_ln_b=conv_ln_b, lru_conv_w=lru_conv_w, lru_conv_b=lru_conv_b, lru_wa=lru_wa,
             lru_ba=lru_ba, lru_wx=lru_wx, lru_bx=lru_bx, lru_lambda=lru_lambda, mix_norm=mix_norm, w_out=w_out,
             norm2=norm2, w_up=w_up, w_down=w_down, final_norm=final_norm)
    m = dict(norm1=m_norm1, w_in=m_w_in, attn_sinks=m_attn_sinks, conv_dw_w=m_conv_dw_w, conv_dw_b=m_conv_dw_b,
             conv_ln_g=m_conv_ln_g, conv_ln_b=m_conv_ln_b, lru_conv_w=m_lru_conv_w, lru_conv_b=m_lru_conv_b,
             lru_wa=m_lru_wa, lru_ba=m_lru_ba, lru_wx=m_lru_wx, lru_bx=m_lru_bx, lru_lambda=m_lru_lambda,
             mix_norm=m_mix_norm, w_out=m_w_out, norm2=m_norm2, w_up=m_w_up, w_down=m_w_down, final_norm=m_final_norm)
    v = dict(norm1=v_norm1, w_in=v_w_in, attn_sinks=v_attn_sinks, conv_dw_w=v_conv_dw_w, conv_dw_b=v_conv_dw_b,
             conv_ln_g=v_conv_ln_g, conv_ln_b=v_conv_ln_b, lru_conv_w=v_lru_conv_w, lru_conv_b=v_lru_conv_b,
             lru_wa=v_lru_wa, lru_ba=v_lru_ba, lru_wx=v_lru_wx, lru_bx=v_lru_bx, lru_lambda=v_lru_lambda,
             mix_norm=v_mix_norm, w_out=v_w_out, norm2=v_norm2, w_up=v_w_up, w_down=v_w_down, final_norm=v_final_norm)
    depth = w_in.shape[0]
    xi, yi, ci = _me()
    dev = (4 * xi + 2 * yi + ci).astype(jnp.int32)
    dev1 = dev.reshape(1)
    wb = {n: w[n].astype(MX) for n in _BIG}
    layer_shards = lambda l: [wb["w_out"][l], wb["w_up"][l], wb["w_down"][l]]

    _, ((g_in0, g_cw, g_lcw),) = _call(None, "gather_first", None, [], [], [], [], [],
                                        [_gather_rider([wb["w_in"][0], conv_dw_w, lru_conv_w])])
    cols = lambda g: jnp.moveaxis(g, 0, -2).reshape(g.shape[1:-1] + (N_DEV * g.shape[-1],))
    p = dict(w)
    p["conv_dw_w"] = cols(g_cw)
    p["lru_conv_w"] = cols(g_lcw)
    lp = [_layer_params(p, l) for l in range(depth)]

    gathered = [dict(w_in=cols(g_in0)), dict()]
    saved = []
    h = x[0]
    for l in range(depth):
        q, gw = lp[l], gathered[l]
        z, hn1 = _ln_in(h, q["g1"], gw["w_in"], l == 0, f"ln_in{l}")
        riders = [_gather_rider(layer_shards(0))] if l == 0 else []
        (ycat, hl, uc, probs, psinks), got = _mixer_fwd(z, q["sink"], q["cw"], q["pv"], q["wa"], q["wx"],
                                                        f"mixer_fwd{l}", riders)
        if l == 0:
            gw["w_out"], gw["w_up"], gw["w_down"] = got[0]
            gw["w_out"] = gw["w_out"].reshape(D_MODEL, D_MODEL)
        riders = [_gather_rider([wb["w_in"][1]] + layer_shards(1))] if l == 0 else []
        (h1, act, h2, ym, hn2), got = _post_fwd(ycat, h, q["gmix"], gw["w_out"], q["g2"], gw["w_up"],
                                                gw["w_down"].reshape(D_FF, D_MODEL), f"post_fwd{l}", riders)
        if l == 0:
            nxt = gathered[1]
            nxt["w_in"], nxt["w_out"], nxt["w_up"], nxt["w_down"] = got[0]
            nxt["w_in"] = cols(nxt["w_in"])
            nxt["w_out"] = nxt["w_out"].reshape(D_MODEL, D_MODEL)
        saved.append(dict(h0=h, z=z, hn1=hn1, ycat=ycat, hl=hl, uc=uc, probs=probs, psinks=psinks, h1=h1, act=act,
                          ym=ym, hn2=hn2))
        h = h2
    dh, loss, dgf = _loss_head(h, final_norm.reshape(1, -1), loss_target[0], "loss_head")

    grads = [None] * depth
    big = {n: [None] * depth for n in _BIG}
    pending = []

    def send_pending():
        riders = [_scatter_rider([item[3] for item in pending])] if pending else []
        return riders, list(pending)

    def record(sent, got):
        for item, recv in zip(sent, got[0] if sent else []):
            big[item[0]][item[1]] = (item[2], recv)
        del pending[:len(sent)]

    for l in reversed(range(depth)):
        q, s, gw = lp[l], saved[l], gathered[l]
        riders, sent = send_pending()
        w_up_t = jnp.swapaxes(gw["w_up"], 1, 2).reshape(D_FF, D_MODEL)
        (dh1, dh1b, dhb, du, dg2), got = _ffn_bwd(dh, s["act"], s["h1"], q["g2"], w_up_t, gw["w_down"],
                                                  f"ffn_bwd{l}", riders)
        record(sent, got)
        dycat, dgm, d_wout = _mix_bwd(dh1b, s["ycat"], s["ym"], q["gmix"], gw["w_out"], f"mix_bwd{l}")
        pending.append(("w_down", l) + tuple(_dw(s["act"], dhb, f"dw_down{l}", "rows", 2048, D_MODEL)))
        pending.append(("w_up", l) + tuple(_dw(s["hn2"], du, f"dw_up{l}", "cols", D_MODEL, 2048)))
        pending.append(("w_out", l) + tuple(d_wout))
        riders, sent = send_pending()
        (dz, dsink, dcw, dpv, dwa, dwx), got = _mixer_bwd(
            dycat, s["z"], s["ycat"], s["hl"], s["uc"], s["probs"], s["psinks"], q["sink"], q["cw"], q["pv"], q["wa"],
            q["wx"], f"mixer_bwd{l}", riders)
        record(sent, got)
        if l > 0:
            dh, dg1, d_win = _in_bwd_dw(dz, s["h0"], dh1, q["g1"], gw["w_in"], f"in_bwd{l}")
            pending.append(("w_in", l) + tuple(d_win))
        else:
            d_win = _dw(s["hn1"], dz, f"dw_in{l}", "cols", D_MODEL, IN_W)
            win_sends = _send_start([d_win[1]], True, "scatter_w_in0_start")
            dh, dg1 = _in_bwd(dz, s["h0"], dh1, q["g1"], gw["w_in"], win_sends.token, f"in_bwd{l}")
        grads[l] = dict(
            norm1=dg1[0], attn_sinks=jnp.stack([dsink[0:4, 0], dsink[0:4, 2 * BLK]], axis=1).reshape(8),
            conv_dw_w=dcw[0:CONV_K], conv_dw_b=dpv[R_CONV_B], conv_ln_g=dpv[R_LN_G], conv_ln_b=dpv[R_LN_B],
            lru_conv_w=dpv[R_LCW:R_LCW + LRU_K], lru_conv_b=dpv[R_LCONV_B], lru_wa=_unblock_diag(dwa),
            lru_ba=dpv[R_BA].reshape(4, 64), lru_wx=_unblock_diag(dwx), lru_bx=dpv[R_BX].reshape(4, 64),
            lru_lambda=dpv[R_LAM], mix_norm=dgm[0], norm2=dg2[0])

    small = [jnp.stack([grads[l][n] for l in range(depth)]) for n in _SMALL] + [dgf, loss[:, 0:1]]
    sizes = [a.size for a in small]
    total = -(-sum(sizes) // 1024) * 1024
    packed = jnp.concatenate([a.reshape(-1) for a in small] + [jnp.zeros((total - sum(sizes),), F32)])
    packed = packed.reshape(total // 128, 128)
    small_sends = _send_start([packed], False, "bcast_small_start")

    out = {}
    shard_update = lambda n, after: list(_adamw_shard(
        [big[n][l][0] for l in range(depth)], [big[n][l][1] for l in range(depth)], dev1, w[n], m[n], v[n], after,
        f"adamw_{n}"))
    for n in ("w_out", "w_up", "w_down"):
        out[n] = shard_update(n, small_sends.token)
    _, (win_recv,) = _send_wait(win_sends, out["w_down"][1], "scatter_w_in0_wait")
    big["w_in"][0] = (d_win[0], win_recv)
    (packed,), (small_recv,) = _send_wait(small_sends, win_recv, "bcast_small_wait")
    out["w_in"] = shard_update("w_in", jnp.zeros((8, 128), F32))
    parts = jnp.concatenate([packed[None], small_recv], axis=0)
    summed = _sum_parts(parts, dev1, "sum_small_grads").reshape(-1)
    small_sums, pos = [], 0
    for a, size in zip(small, sizes):
        small_sums.append(summed[pos:pos + size].reshape(a.shape))
        pos += size
    shard = lambda a: lax.dynamic_slice_in_dim(a, dev * (a.shape[-1] // N_DEV), a.shape[-1] // N_DEV, axis=a.ndim - 1)
    flat = {"lru_wa": (depth, LRU_W, 64), "lru_wx": (depth, LRU_W, 64), "final_norm": (1, D_MODEL)}
    gs, ws, ms, vs = [], [], [], []
    for n, g in zip(_SMALL + ["final_norm"], small_sums[:-1]):
        shp = flat.get(n, w[n].shape)
        gs.append((shard(g) if n in ("conv_dw_w", "lru_conv_w") else g).reshape(shp))
        ws.append(w[n].reshape(shp))
        ms.append(m[n].reshape(shp))
        vs.append(v[n].reshape(shp))
    sd, sm, sv = _adamw_small(gs, ws, ms, vs, "adamw_small")
    for j, n in enumerate(_SMALL + ["final_norm"]):
        out[n] = [a.reshape(w[n].shape) for a in (gs[j], sd[j], sm[j], sv[j])]
    loss_total = small_sums[-1][0, 0]

    result = [loss_total, dh[None]]
    for j in range(4):
        result += [out[n][j] for n in _WEIGHTS]
    return tuple(result)
```

```python
import types

import jax
import jax.numpy as jnp
from jax import lax
from jax.experimental import pallas as pl
from jax.experimental.pallas import tpu as pltpu

F32 = jnp.float32
MX = jnp.bfloat16
WIRE = jnp.bfloat16

D_MODEL = 1024
HEAD_DIM = 64
ATTN_W = 512
KV_W = 128
BLK = 128
CONV_W = 256
CONV_K = 31
LRU_W = 256
LRU_K = 4
LRU_C = 8.0
IN_W = 1792
D_FF = 4096
FF_BLK = 512
N_DEV = 8
IN_SHARD = IN_W // N_DEV
RMS_EPS = 1e-6
LN_EPS = 1e-5
MASK_VALUE = -1e30
SCALE = HEAD_DIM ** -0.5
CONV_HALO = 32
LRU_HALO = 8
CONV_CHUNK = 64
POST_TILE = 512
DW_TILE = 1024
Q0, K0, V0, CV0, CG0, RX0, RG0 = 0, 512, 640, 768, 1024, 1280, 1536
R_CONV_B, R_LN_G, R_LN_B, R_LCONV_B, R_BA, R_BX, R_LAM, R_LCW = 0, 1, 2, 3, 4, 5, 6, 8

ADAM_LR, ADAM_B1, ADAM_B2, ADAM_EPS, ADAM_WD, ADAM_STEP = 0.001, 0.9, 0.999, 1e-08, 0.01, 10

VMEM_LIMIT = 56 * 1024 * 1024
MESH = pl.DeviceIdType.MESH
ANY = pl.BlockSpec(memory_space=pl.ANY)


def _tile(t, cap=512):
    return min(cap, t)


def _dot(a, b):
    return jnp.dot(a.astype(MX), b.astype(MX), preferred_element_type=F32)


def _dot_nt(a, b):
    return lax.dot_general(a.astype(MX), b.astype(MX), (((1,), (1,)), ((), ())), preferred_element_type=F32)


def _dot_tn(a, b):
    return lax.dot_general(a.astype(MX), b.astype(MX), (((0,), (0,)), ((), ())), preferred_element_type=F32)


def _const_spec(shape):
    nd = len(shape)
    return pl.BlockSpec(shape, lambda *_: (0,) * nd, pipeline_mode=pl.Buffered(1))


def _acc_spec(shape):
    nd = len(shape)
    return pl.BlockSpec(shape, lambda *_: (0,) * nd)


def _sds(shape, dtype):
    return jax.ShapeDtypeStruct(shape, dtype)


def _sigmoid(x):
    return jax.nn.sigmoid(x)


def _rms_fwd(x, g):
    r = lax.rsqrt(jnp.mean(x * x, axis=-1, keepdims=True) + RMS_EPS)
    xh = x * r
    return xh * g, xh, r


def _rms_bwd(dy, xh, r, g):
    t = dy * g
    dx = r * (t - xh * jnp.mean(t * xh, axis=-1, keepdims=True))
    return dx, jnp.sum(dy * xh, axis=0, keepdims=True)


_GROUPS = ((0, 512), (512, 768), (768, 1024))


def _group_rms_fwd(y, g):
    parts = [_rms_fwd(y[:, a:b], g[:, a:b]) for a, b in _GROUPS]
    return (jnp.concatenate([p[0] for p in parts], axis=1),
            jnp.concatenate([p[1] for p in parts], axis=1),
            [p[2] for p in parts])


def _gelu(x):
    c = 0.7978845608028654
    u = c * (x + 0.044715 * x * x * x)
    th = jnp.tanh(u)
    val = 0.5 * x * (1.0 + th)
    grad = 0.5 * (1.0 + th) + 0.5 * x * (1.0 - th * th) * c * (1.0 + 3.0 * 0.044715 * x * x)
    return val, grad


def _neg_expm1(x):
    series = -x * (1.0 + x * (0.5 + x * (1.0 / 6.0 + x * (1.0 / 24.0))))
    return jnp.where(x > -0.02, series, 1.0 - jnp.exp(x))


def _me():
    return lax.axis_index("x"), lax.axis_index("y"), lax.axis_index("c")


def _gather_rider(arrays):
    arrays = list(arrays)
    n = len(arrays)

    def plan(ins, outs, sems):
        ssem, rsem, lsem = sems
        x, y, c = _me()
        chips = [(1 - x, y), (x, 1 - y), (1 - x, 1 - y)]

        def copy(a, k, block, to, own=False):
            dst = outs[a].at[4 * block[0] + 2 * block[1] + block[2]]
            return pltpu.make_async_remote_copy(
                src_ref=ins[a] if own else dst, dst_ref=dst, send_sem=ssem.at[7 * a + k],
                recv_sem=rsem.at[7 * a + k], device_id=to, device_id_type=MESH)

        return x, y, c, chips, copy, lsem

    def start(ins, outs, sems):
        x, y, c, chips, copy, lsem = plan(ins, outs, sems)
        for a in range(n):
            pltpu.make_async_copy(ins[a], outs[a].at[4 * x + 2 * y + c], lsem.at[a]).start()
            copy(a, 0, (x, y, c), (x, y, 1 - c), own=True).start()
            for j, chip in enumerate(chips):
                copy(a, 1 + j, (x, y, c), (*chip, c), own=True).start()

    def mid(ins, outs, sems):
        x, y, c, chips, copy, _ = plan(ins, outs, sems)
        for a in range(n):
            for j, chip in enumerate(chips):
                copy(a, 1 + j, (*chip, c), (x, y, c)).wait_recv()
                copy(a, 4 + j, (*chip, c), (x, y, 1 - c)).start()

    def finish(ins, outs, sems):
        x, y, c, chips, copy, lsem = plan(ins, outs, sems)
        for a in range(n):
            copy(a, 0, (x, y, 1 - c), (x, y, c)).wait_recv()
            for j, chip in enumerate(chips):
                copy(a, 4 + j, (*chip, 1 - c), (x, y, c)).wait_recv()
        for a in range(n):
            copy(a, 0, (x, y, c), (x, y, 1 - c), own=True).wait_send()
            for j, chip in enumerate(chips):
                copy(a, 1 + j, (x, y, c), (*chip, c), own=True).wait_send()
                copy(a, 4 + j, (*chip, c), (x, y, 1 - c)).wait_send()
            pltpu.make_async_copy(ins[a], outs[a].at[4 * x + 2 * y + c], lsem.at[a]).wait()

    return types.SimpleNamespace(
        arrays=arrays, out_shape=[_sds((N_DEV,) + a.shape, a.dtype) for a in arrays],
        scratch=[pltpu.SemaphoreType.DMA((7 * n,)), pltpu.SemaphoreType.DMA((7 * n,)), pltpu.SemaphoreType.DMA((n,))],
        start=start, mid=mid, finish=finish)


def _scatter_rider(arrays):
    arrays = list(arrays)
    n = len(arrays)

    def copies(ins, outs, sems):
        ssem, rsem = sems
        x, y, c = _me()
        out = []
        for a in range(n):
            for f in range(1, N_DEV):
                px = 1 - x if f & 4 else x
                py = 1 - y if f & 2 else y
                pc = 1 - c if f & 1 else c
                out.append(pltpu.make_async_remote_copy(
                    src_ref=ins[a].at[4 * px + 2 * py + pc], dst_ref=outs[a].at[f - 1], send_sem=ssem.at[7 * a + f - 1],
                    recv_sem=rsem.at[7 * a + f - 1], device_id=(px, py, pc), device_id_type=MESH))
        return out

    def start(ins, outs, sems):
        for cp in copies(ins, outs, sems):
            cp.start()

    def finish(ins, outs, sems):
        for cp in copies(ins, outs, sems):
            cp.wait()

    return types.SimpleNamespace(
        arrays=arrays, out_shape=[_sds((N_DEV - 1,) + a.shape[1:], a.dtype) for a in arrays],
        scratch=[pltpu.SemaphoreType.DMA((7 * n,)), pltpu.SemaphoreType.DMA((7 * n,))],
        start=start, mid=None, finish=finish)


def _call(body, name, grid, in_specs, out_specs, out_shape, scratch, operands, riders=()):
    n_in, n_out, n_scr = len(operands), len(out_shape), len(scratch)
    nsteps = grid[0] if grid else 1
    sizes = [(len(r.arrays), len(r.out_shape), len(r.scratch)) for r in riders]

    def wrapped(*refs):
        pos = n_in
        r_ins = []
        for ri, _, _ in sizes:
            r_ins.append(refs[pos:pos + ri])
            pos += ri
        outs = refs[pos:pos + n_out]
        pos += n_out
        r_outs = []
        for _, ro, _ in sizes:
            r_outs.append(refs[pos:pos + ro])
            pos += ro
        scr = refs[pos:pos + n_scr]
        pos += n_scr
        r_sems = []
        for _, _, rs in sizes:
            r_sems.append(refs[pos:pos + rs])
            pos += rs
        step = pl.program_id(0) if grid else 0

        def at(s, fn):
            if grid:
                pl.when(step == s)(fn)
            else:
                fn()

        for r, a, b, c in zip(riders, r_ins, r_outs, r_sems):
            at(0, lambda r=r, a=a, b=b, c=c: r.start(a, b, c))
        for r, a, b, c in zip(riders, r_ins, r_outs, r_sems):
            if r.mid is not None:
                at((3 * nsteps) // 4, lambda r=r, a=a, b=b, c=c: r.mid(a, b, c))
        if body is not None:
            body(*refs[:n_in], *outs, *scr)
        for r, a, b, c in zip(riders, r_ins, r_outs, r_sems):
            at(nsteps - 1, lambda r=r, a=a, b=b, c=c: r.finish(a, b, c))

    r_arrays = [a for r in riders for a in r.arrays]
    r_shapes = [s for r in riders for s in r.out_shape]
    kwargs = {}
    if grid:
        kwargs = dict(grid=grid, compiler_params=pltpu.CompilerParams(
            dimension_semantics=("arbitrary",) * len(grid), vmem_limit_bytes=VMEM_LIMIT))
    res = pl.pallas_call(
        wrapped, name=name,
        in_specs=list(in_specs) + [ANY] * len(r_arrays),
        out_specs=list(out_specs) + [ANY] * len(r_shapes),
        out_shape=list(out_shape) + r_shapes,
        scratch_shapes=list(scratch) + [s for r in riders for s in r.scratch],
        **kwargs,
    )(*operands, *r_arrays)
    host, rest = res[:n_out], res[n_out:]
    r_res = []
    for _, ro, _ in sizes:
        r_res.append(rest[:ro])
        rest = rest[ro:]
    return host, r_res


def _ln_in(h, g1, w_in_t, keep_hn, name):
    t = h.shape[0]
    tm = _tile(t)

    def body(h_ref, g_ref, w_ref, z_ref, *hn_ref):
        y, _, _ = _rms_fwd(h_ref[...], g_ref[...])
        hn = y.astype(MX)
        if keep_hn:
            hn_ref[0][...] = hn
        z_ref[...] = _dot_nt(hn, w_ref[...])

    tile = lambda w: pl.BlockSpec((tm, w), lambda i: (i, 0))
    outs, _ = _call(
        body, name, (t // tm,),
        [tile(D_MODEL), _const_spec((1, D_MODEL)), _const_spec((IN_W, D_MODEL))],
        [tile(IN_W)] + [tile(D_MODEL)] * keep_hn,
        [_sds((t, IN_W), F32)] + [_sds((t, D_MODEL), MX)] * keep_hn, [], [h, g1, w_in_t])
    return outs[0], (outs[1] if keep_hn else None)


def _band2(kb, g):
    lo = lax.broadcasted_iota(jnp.int32, kb.shape, 1) < HEAD_DIM
    kr = pltpu.roll(kb, HEAD_DIM, 1)
    if g == 0:
        top, bot = jnp.where(lo, kb, 0.0), jnp.where(lo, 0.0, kr)
    else:
        top, bot = jnp.where(lo, kr, 0.0), jnp.where(lo, 0.0, kb)
    return jnp.concatenate([top, bot], axis=0)


def _attn_operands(z_ref, zh_ref, b):
    rows = slice(b * BLK, (b + 1) * BLK)
    prev = zh_ref if b == 0 else z_ref
    prow = slice(0, BLK) if b == 0 else slice((b - 1) * BLK, b * BLK)
    kb = jnp.concatenate([prev[prow, K0:K0 + KV_W], z_ref[rows, K0:K0 + KV_W]], axis=0)
    vb = jnp.concatenate([prev[prow, V0:V0 + KV_W], z_ref[rows, V0:V0 + KV_W]], axis=0)
    k2 = [_band2(kb, g) for g in range(2)]
    v2 = [_band2(vb, g) for g in range(2)]
    q2 = [jnp.concatenate([z_ref[rows, (2 * g) * BLK:(2 * g + 1) * BLK], z_ref[rows, (2 * g + 1) * BLK:(2 * g + 2) * BLK]],
                          axis=0) for g in range(2)]
    return q2, k2, v2


def _attn_block(z_ref, zh_ref, sink_ref, b, first):
    q2, k2, v2 = _attn_operands(z_ref, zh_ref, b)
    rr = lax.broadcasted_iota(jnp.int32, (4 * BLK, 2 * BLK), 0) & (BLK - 1)
    cc = lax.broadcasted_iota(jnp.int32, (4 * BLK, 2 * BLK), 1)
    first_block = jnp.logical_and(first, b == 0).astype(jnp.int32)
    mask = jnp.logical_and(jnp.logical_and(cc > rr, cc <= rr + BLK), cc >= BLK * first_block)
    s = jnp.concatenate([_dot_nt(q2[g], k2[g]) for g in range(2)], axis=0) * SCALE
    w = 2 * BLK
    out, psink = [], []
    for hh in range(2):
        sh = jnp.where(mask, s[:, hh * w:(hh + 1) * w], MASK_VALUE)
        sk = jnp.concatenate([jnp.broadcast_to(sink_ref[p:p + 1, hh * w:hh * w + 1], (BLK, 1)) for p in range(4)], axis=0)
        m = jnp.maximum(jnp.max(sh, axis=1, keepdims=True), sk)
        p = jnp.exp(sh - m)
        es = jnp.exp(sk - m)
        inv = 1.0 / (jnp.sum(p, axis=1, keepdims=True) + es)
        out.append(p * inv)
        psink.append(es * inv)
    return v2, jnp.concatenate(out, axis=1), psink


def _scan_steps(a, b, n, span, reverse):
    pos = lax.broadcasted_iota(jnp.int32, a.shape, 0) & (span - 1)
    d = 1
    while d < span:
        keep = pos < span - d if reverse else pos >= d
        shift = n - d if reverse else d
        a_sh = jnp.where(keep, pltpu.roll(a, shift, 0), 1.0)
        b_sh = jnp.where(keep, pltpu.roll(b, shift, 0), 0.0)
        b = a * b_sh + b
        a = a * a_sh
        d *= 2
    return a, b


def _scan(a, b, tm, reverse):
    return _scan_steps(a, b, tm, tm, reverse)


def _shifted_copies(ext, shifts, tm):
    rows = tm + CONV_HALO - 8
    for r in range(1, 8):
        shifts[r - 1, 0:rows, :] = ext[pl.ds(r, rows), :]


def _tap(ext, shifts, off, r0, n):
    a, r = divmod(off, 8)
    lo = 8 * a + r0
    if r == 0:
        return ext[lo:lo + n, :]
    return shifts[r - 1, lo:lo + n, :]


def _glu_fill(z_ref, zh_ref, uext, ush, first, tm, sg_out=None):
    cv = z_ref[:, CV0:CV0 + CONV_W]
    sg = _sigmoid(z_ref[:, CG0:CG0 + CONV_W])
    if sg_out is not None:
        sg_out[...] = sg
    hrow = BLK - CONV_HALO
    uh = zh_ref[hrow:BLK, CV0:CV0 + CONV_W] * _sigmoid(zh_ref[hrow:BLK, CG0:CG0 + CONV_W])
    uext[0:CONV_HALO, :] = jnp.where(first, 0.0, uh)
    uext[CONV_HALO:CONV_HALO + tm, :] = cv * sg
    _shifted_copies(uext, ush, tm)


def _conv_taps(cw_ref, pv_ref, uext, ush, out_ref, tm):
    for r0 in range(0, tm, CONV_CHUNK):
        acc = jnp.broadcast_to(pv_ref[R_CONV_B:R_CONV_B + 1, :], (CONV_CHUNK, CONV_W))
        for k in range(CONV_K):
            acc = acc + cw_ref[k:k + 1, :] * _tap(uext, ush, CONV_HALO - (CONV_K - 1) + k, r0, CONV_CHUNK)
        out_ref[r0:r0 + CONV_CHUNK, :] = acc


def _ln_silu(uc, pv_ref):
    mu = jnp.mean(uc, axis=-1, keepdims=True)
    xc = uc - mu
    rs = lax.rsqrt(jnp.mean(xc * xc, axis=-1, keepdims=True) + LN_EPS)
    xh = xc * rs
    ln = xh * pv_ref[R_LN_G:R_LN_G + 1, :] + pv_ref[R_LN_B:R_LN_B + 1, :]
    sg = _sigmoid(ln)
    return xh, rs, ln, sg


def _lru_gates(z_ref, zh_ref, pv_ref, wa_ref, wx_ref, rxext, first, tm):
    rxext[0:LRU_HALO, :] = jnp.where(first, 0.0, zh_ref[BLK - LRU_HALO:BLK, RX0:RX0 + LRU_W])
    rxext[LRU_HALO:LRU_HALO + tm, :] = z_ref[:, RX0:RX0 + LRU_W]
    xc = jnp.broadcast_to(pv_ref[R_LCONV_B:R_LCONV_B + 1, :], (tm, LRU_W))
    for k in range(LRU_K):
        xc = xc + pv_ref[R_LCW + k:R_LCW + k + 1, :] * rxext[pl.ds(LRU_HALO - (LRU_K - 1) + k, tm), :]
    r = _sigmoid(_dot(xc, wa_ref[...]) + pv_ref[R_BA:R_BA + 1, :])
    ig = _sigmoid(_dot(xc, wx_ref[...]) + pv_ref[R_BX:R_BX + 1, :])
    lam = pv_ref[R_LAM:R_LAM + 1, :]
    sp = jnp.log1p(jnp.exp(-lam))
    la = (-LRU_C * r) * sp
    a = jnp.exp(la)
    mult = jnp.sqrt(_neg_expm1(2.0 * la))
    return xc, r, ig, sp, la, a, mult


def _mixer_in_specs(tm, tile_of):
    hb = tm // BLK
    return [
        pl.BlockSpec((tm, IN_W), lambda i: (tile_of(i), 0)),
        pl.BlockSpec((BLK, IN_W), lambda i: (jnp.maximum(tile_of(i) * hb - 1, 0), 0)),
        _const_spec((8, 4 * BLK)),
        _const_spec((32, CONV_W)),
        _const_spec((16, CONV_W)),
        _const_spec((LRU_W, LRU_W)),
        _const_spec((LRU_W, LRU_W)),
    ]


def _mixer_fwd(z, sink, cw, pv, wa, wx, name, riders=()):
    t = z.shape[0]
    tm = _tile(t)
    nb = tm // BLK

    def body(z_ref, zh_ref, sink_ref, cw_ref, pv_ref, wa_ref, wx_ref, y_ref, hl_ref, uc_ref, p_ref, ps_ref,
             uext, ush, rxext, hcar):
        i = pl.program_id(0)
        first = i == 0

        @pl.when(first)
        def _():
            hcar[...] = jnp.zeros_like(hcar)

        lo = lax.broadcasted_iota(jnp.int32, (4 * BLK, BLK), 1) < HEAD_DIM
        for b in range(nb):
            rows = slice(b * BLK, (b + 1) * BLK)
            v2, prob, psink = _attn_block(z_ref, zh_ref, sink_ref, b, first)
            prob = prob.astype(MX)
            p_ref[b] = prob
            ps_ref[b] = jnp.where(lo, psink[0], psink[1])
            for g in range(2):
                o = _dot(prob[2 * g * BLK:(2 * g + 2) * BLK], v2[g])
                y_ref[rows, (2 * g) * BLK:(2 * g + 1) * BLK] = o[0:BLK]
                y_ref[rows, (2 * g + 1) * BLK:(2 * g + 2) * BLK] = o[BLK:2 * BLK]
        _glu_fill(z_ref, zh_ref, uext, ush, first, tm)
        _conv_taps(cw_ref, pv_ref, uext, ush, uc_ref, tm)
        _, _, ln, sg = _ln_silu(uc_ref[...], pv_ref)
        y_ref[:, ATTN_W:ATTN_W + CONV_W] = ln * sg
        xc, _, ig, _, _, a, mult = _lru_gates(z_ref, zh_ref, pv_ref, wa_ref, wx_ref, rxext, first, tm)
        acum, h = _scan(a, mult * (ig * xc), tm, reverse=False)
        h = h + acum * hcar[0:1, :]
        hl_ref[...] = h
        hcar[0:1, :] = h[tm - 1:tm, :]
        gl, _ = _gelu(z_ref[:, RG0:RG0 + LRU_W])
        y_ref[:, ATTN_W + CONV_W:ATTN_W + CONV_W + LRU_W] = h * gl

    tile = lambda w: pl.BlockSpec((tm, w), lambda i: (i, 0))
    return _call(
        body, name, (t // tm,), _mixer_in_specs(tm, lambda i: i),
        [tile(D_MODEL), tile(LRU_W), tile(CONV_W), pl.BlockSpec((nb, 4 * BLK, 4 * BLK), lambda i: (i, 0, 0)),
         pl.BlockSpec((nb, 4 * BLK, BLK), lambda i: (i, 0, 0))],
        [_sds((t, D_MODEL), F32), _sds((t, LRU_W), F32), _sds((t, CONV_W), F32),
         _sds((t // BLK, 4 * BLK, 4 * BLK), MX), _sds((t // BLK, 4 * BLK, BLK), F32)],
        [pltpu.VMEM((tm + CONV_HALO, CONV_W), F32), pltpu.VMEM((7, tm + CONV_HALO - 8, CONV_W), F32),
         pltpu.VMEM((tm + LRU_HALO, LRU_W), F32), pltpu.VMEM((8, LRU_W), F32)],
        [z, z, sink, cw, pv, wa, wx], riders)


def _mixer_bwd(dy, z, ycat, hl, uc, probs, psinks, sink, cw, pv, wa, wx, name, riders=()):
    t = z.shape[0]
    tm = _tile(t)
    nt = t // tm
    nb = tm // BLK
    rev = lambda i: nt - 1 - i

    def body(dy_ref, z_ref, zh_ref, sink_ref, cw_ref, pv_ref, wa_ref, wx_ref, y_ref, hl_ref, hlh_ref, uc_ref,
             p_ref, ps_ref, dz_ref, dsink_ref, dcw_ref, dpv_ref, dwa_ref, dwx_ref,
             uext, ush, sgs, rxext, dkext, dvext, ducext, dsh, dcw8, dxcext, kcar, vcar, uccar, xccar, gcar):
        i = pl.program_id(0)
        first = i == nt - 1

        @pl.when(i == 0)
        def _():
            for car in (kcar, vcar, uccar, xccar, gcar, dcw8):
                car[...] = jnp.zeros_like(car)
            for acc in (dsink_ref, dpv_ref, dwa_ref, dwx_ref):
                acc[...] = jnp.zeros_like(acc)

        def addrow(r, val):
            dpv_ref[r:r + 1, :] += jnp.sum(val, axis=0, keepdims=True)

        dkext[:, 0:tm] = jnp.zeros((KV_W, tm), F32)
        dvext[:, 0:tm] = jnp.zeros((KV_W, tm), F32)
        dkext[:, tm:tm + BLK] = kcar[...]
        dvext[:, tm:tm + BLK] = vcar[...]
        lane512 = lax.broadcasted_iota(jnp.int32, (1, 4 * BLK), 1) < 2 * BLK
        lo = lax.broadcasted_iota(jnp.int32, (4 * BLK, BLK), 1) < HEAD_DIM
        hd, w2 = HEAD_DIM, 2 * BLK
        for b in range(nb):
            rows = slice(b * BLK, (b + 1) * BLK)
            band = slice(b * BLK, (b + 2) * BLK)
            q2, k2, v2 = _attn_operands(z_ref, zh_ref, b)
            prob = p_ref[b]
            psink = [ps_ref[b, :, 0:1], ps_ref[b, :, HEAD_DIM:HEAD_DIM + 1]]
            stack = lambda ref: jnp.concatenate([ref[rows, p * BLK:(p + 1) * BLK] for p in range(4)], axis=0)
            do4 = stack(dy_ref)
            dlt = do4 * stack(y_ref)
            d0 = jnp.sum(jnp.where(lo, dlt, 0.0), axis=1, keepdims=True)
            d1 = jnp.sum(jnp.where(lo, 0.0, dlt), axis=1, keepdims=True)
            dp = jnp.concatenate([_dot_nt(do4[g * w2:(g + 1) * w2], v2[g]) for g in range(2)], axis=0)
            dl = jnp.concatenate([jnp.broadcast_to(d0, (4 * BLK, w2)), jnp.broadcast_to(d1, (4 * BLK, w2))], axis=1)
            draw = (prob * (dp - dl)) * SCALE
            e0, e1 = psink[0] * d0, psink[1] * d1
            for p in range(4):
                prs = slice(p * BLK, (p + 1) * BLK)
                s0 = jnp.sum(e0[prs], axis=0, keepdims=True)
                s1 = jnp.sum(e1[prs], axis=0, keepdims=True)
                dsink_ref[p:p + 1, :] += -jnp.where(lane512, s0, s1)
            for g in range(2):
                grs = slice(g * w2, (g + 1) * w2)
                dq = _dot(draw[grs], k2[g])
                dz_ref[rows, (2 * g) * BLK:(2 * g + 1) * BLK] = dq[0:BLK].astype(dz_ref.dtype)
                dz_ref[rows, (2 * g + 1) * BLK:(2 * g + 2) * BLK] = dq[BLK:2 * BLK].astype(dz_ref.dtype)
                tk = _dot_tn(q2[g], draw[grs])
                tv = _dot_tn(do4[grs], prob[grs])
                dkext[g * hd:(g + 1) * hd, band] += tk[0:hd, 0:w2] + tk[hd:2 * hd, w2:2 * w2]
                dvext[g * hd:(g + 1) * hd, band] += tv[0:hd, 0:w2] + tv[hd:2 * hd, w2:2 * w2]
        dz_ref[:, K0:K0 + KV_W] = jnp.transpose(dkext[:, BLK:BLK + tm]).astype(dz_ref.dtype)
        dz_ref[:, V0:V0 + KV_W] = jnp.transpose(dvext[:, BLK:BLK + tm]).astype(dz_ref.dtype)
        kcar[...] = dkext[:, 0:BLK]
        vcar[...] = dvext[:, 0:BLK]

        _glu_fill(z_ref, zh_ref, uext, ush, first, tm, sg_out=sgs)
        xh, rs, ln, sg = _ln_silu(uc_ref[...], pv_ref)
        dln = dy_ref[:, ATTN_W:ATTN_W + CONV_W] * (sg * (1.0 + ln * (1.0 - sg)))
        addrow(R_LN_G, dln * xh)
        addrow(R_LN_B, dln)
        dxh = dln * pv_ref[R_LN_G:R_LN_G + 1, :]
        duc = rs * (dxh - jnp.mean(dxh, axis=-1, keepdims=True) - xh * jnp.mean(dxh * xh, axis=-1, keepdims=True))
        addrow(R_CONV_B, duc)
        ducext[0:tm, :] = duc
        ducext[tm:tm + CONV_HALO, :] = uccar[...]
        uccar[...] = duc[0:CONV_HALO, :]
        _shifted_copies(ducext, dsh, tm)
        for r0 in range(0, tm, CONV_CHUNK):
            crow = slice(r0, r0 + CONV_CHUNK)
            duc_c = ducext[crow, :]
            du = jnp.zeros((CONV_CHUNK, CONV_W), F32)
            for k in range(CONV_K):
                prod = duc_c * _tap(uext, ush, CONV_HALO - (CONV_K - 1) + k, r0, CONV_CHUNK)
                part = prod[0:8]
                for s in range(8, CONV_CHUNK, 8):
                    part = part + prod[s:s + 8]
                dcw8[k] += part
                du = du + cw_ref[k:k + 1, :] * _tap(ducext, dsh, CONV_K - 1 - k, r0, CONV_CHUNK)
            sgc = sgs[crow, :]
            dz_ref[crow, CV0:CV0 + CONV_W] = (du * sgc).astype(dz_ref.dtype)
            u_c = uext[CONV_HALO + r0:CONV_HALO + r0 + CONV_CHUNK, :]
            dz_ref[crow, CG0:CG0 + CONV_W] = (du * u_c * (1.0 - sgc)).astype(dz_ref.dtype)

        @pl.when(i == nt - 1)
        def _():
            dcw_ref[...] = jnp.sum(dcw8[...], axis=1)

        xc, r, ig, sp, la, a, mult = _lru_gates(z_ref, zh_ref, pv_ref, wa_ref, wx_ref, rxext, first, tm)
        h = hl_ref[...]
        rowi = lax.broadcasted_iota(jnp.int32, (tm, LRU_W), 0)
        hlast = jnp.where(first, 0.0, hlh_ref[7:8, :])
        hprev = jnp.where(rowi == 0, hlast, pltpu.roll(h, 1, 0))
        dyl = dy_ref[:, ATTN_W + CONV_W:ATTN_W + CONV_W + LRU_W]
        gl, dgl = _gelu(z_ref[:, RG0:RG0 + LRU_W])
        dz_ref[:, RG0:RG0 + LRU_W] = (dyl * h * dgl).astype(dz_ref.dtype)
        dh = dyl * gl + jnp.where(rowi == tm - 1, gcar[0:1, :], 0.0)
        c = jnp.where(rowi == tm - 1, 0.0, pltpu.roll(a, tm - 1, 0))
        _, gg = _scan(c, dh, tm, reverse=True)
        gcar[0:1, :] = a[0:1, :] * gg[0:1, :]
        dmult = gg * (ig * xc)
        dig = gg * mult * xc
        dxc = gg * mult * ig
        dla = gg * hprev * a - dmult * a * a / mult
        dr = dla * (-LRU_C * sp)
        lam = pv_ref[R_LAM:R_LAM + 1, :]
        dpv_ref[R_LAM:R_LAM + 1, :] += jnp.sum(dla * (-LRU_C * r), axis=0, keepdims=True) * (-_sigmoid(-lam))
        dpa = dr * r * (1.0 - r)
        dpx = dig * ig * (1.0 - ig)
        addrow(R_BA, dpa)
        addrow(R_BX, dpx)
        dxc = dxc + _dot_nt(dpa, wa_ref[...]) + _dot_nt(dpx, wx_ref[...])
        dwa_ref[...] += _dot_tn(xc, dpa)
        dwx_ref[...] += _dot_tn(xc, dpx)
        addrow(R_LCONV_B, dxc)
        dxcext[0:tm, :] = dxc
        dxcext[tm:tm + LRU_HALO, :] = xccar[...]
        xccar[...] = dxc[0:LRU_HALO, :]
        drx = jnp.zeros((tm, LRU_W), F32)
        for k in range(LRU_K):
            addrow(R_LCW + k, dxc * rxext[pl.ds(LRU_HALO - (LRU_K - 1) + k, tm), :])
            drx = drx + pv_ref[R_LCW + k:R_LCW + k + 1, :] * dxcext[pl.ds(LRU_K - 1 - k, tm), :]
        dz_ref[:, RX0:RX0 + LRU_W] = drx.astype(dz_ref.dtype)

    tile = lambda w: pl.BlockSpec((tm, w), lambda i: (rev(i), 0))
    in_specs = [tile(D_MODEL)] + _mixer_in_specs(tm, rev) + [
        tile(D_MODEL), tile(LRU_W),
        pl.BlockSpec((8, LRU_W), lambda i: (jnp.maximum(rev(i) * (tm // 8) - 1, 0), 0)),
        tile(CONV_W), pl.BlockSpec((nb, 4 * BLK, 4 * BLK), lambda i: (rev(i), 0, 0)),
        pl.BlockSpec((nb, 4 * BLK, BLK), lambda i: (rev(i), 0, 0))]
    return _call(
        body, name, (nt,), in_specs,
        [tile(IN_W), _acc_spec((8, 4 * BLK)), _acc_spec((32, CONV_W)), _acc_spec((16, CONV_W)),
         _acc_spec((LRU_W, LRU_W)), _acc_spec((LRU_W, LRU_W))],
        [_sds((t, IN_W), MX), _sds((8, 4 * BLK), F32), _sds((32, CONV_W), F32), _sds((16, CONV_W), F32),
         _sds((LRU_W, LRU_W), F32), _sds((LRU_W, LRU_W), F32)],
        [pltpu.VMEM((tm + CONV_HALO, CONV_W), F32), pltpu.VMEM((7, tm + CONV_HALO - 8, CONV_W), F32),
         pltpu.VMEM((tm, CONV_W), F32), pltpu.VMEM((tm + LRU_HALO, LRU_W), F32),
         pltpu.VMEM((KV_W, tm + BLK), F32), pltpu.VMEM((KV_W, tm + BLK), F32),
         pltpu.VMEM((tm + CONV_HALO, CONV_W), F32), pltpu.VMEM((7, tm + CONV_HALO - 8, CONV_W), F32),
         pltpu.VMEM((32, 8, CONV_W), F32), pltpu.VMEM((tm + LRU_HALO, LRU_W), F32),
         pltpu.VMEM((KV_W, BLK), F32), pltpu.VMEM((KV_W, BLK), F32),
         pltpu.VMEM((CONV_HALO, CONV_W), F32), pltpu.VMEM((LRU_HALO, LRU_W), F32), pltpu.VMEM((8, LRU_W), F32)],
        [dy, z, z, sink, cw, pv, wa, wx, ycat, hl, hl, uc, probs, psinks], riders)


def _post_fwd(ycat, h0, gmix, w_out, g2, w_up, w_down, name, riders=()):
    t = h0.shape[0]
    tm = _tile(t, POST_TILE)
    nj = D_FF // FF_BLK

    def body(y_ref, h_ref, gm_ref, wo_ref, g2_ref, wu_ref, wd_ref, h1_ref, a_ref, h2_ref, ym_ref, hn_ref):
        ym, _, _ = _group_rms_fwd(y_ref[...], gm_ref[...])
        ym = ym.astype(MX)
        ym_ref[...] = ym
        h1 = h_ref[...] + jnp.dot(ym, wo_ref[...], preferred_element_type=F32)
        h1_ref[...] = h1
        hn, _, _ = _rms_fwd(h1, g2_ref[...])
        hn = hn.astype(MX)
        hn_ref[...] = hn
        for j in range(nj):
            u = jnp.dot(hn, wu_ref[j], preferred_element_type=F32)
            a_ref[:, j * FF_BLK:(j + 1) * FF_BLK] = jnp.square(jnp.maximum(u, 0.0)).astype(MX)
        h2_ref[...] = h1 + jnp.dot(a_ref[...], wd_ref[...], preferred_element_type=F32)

    tile = lambda w: pl.BlockSpec((tm, w), lambda i: (i, 0))
    return _call(
        body, name, (t // tm,),
        [tile(D_MODEL), tile(D_MODEL), _const_spec((1, D_MODEL)), _const_spec((D_MODEL, D_MODEL)),
         _const_spec((1, D_MODEL)), _const_spec((nj, D_MODEL, FF_BLK)), _const_spec((D_FF, D_MODEL))],
        [tile(D_MODEL), tile(D_FF), tile(D_MODEL), tile(D_MODEL), tile(D_MODEL)],
        [_sds((t, D_MODEL), F32), _sds((t, D_FF), MX), _sds((t, D_MODEL), F32), _sds((t, D_MODEL), MX),
         _sds((t, D_MODEL), MX)],
        [], [ycat, h0, gmix, w_out, g2, w_up, w_down], riders)


def _ffn_bwd(dh2, act, h1, g2, w_up_t, w_down, name, riders=()):
    t = h1.shape[0]
    tm = _tile(t, POST_TILE)
    nj = D_FF // FF_BLK

    def body(dh2_ref, a_ref, h1_ref, g2_ref, wut_ref, wd_ref, dh1_ref, dh1b_ref, dh2b_ref, du_ref, dg2_ref):
        @pl.when(pl.program_id(0) == 0)
        def _():
            dg2_ref[...] = jnp.zeros_like(dg2_ref)

        dh2 = dh2_ref[...]
        dh2b = dh2.astype(MX)
        dh2b_ref[...] = dh2b
        for j in range(nj):
            cols = slice(j * FF_BLK, (j + 1) * FF_BLK)
            da = _dot_nt(dh2b, wd_ref[j])
            du_ref[:, cols] = (da * (2.0 * jnp.sqrt(a_ref[:, cols].astype(F32)))).astype(MX)
        dhn = jnp.dot(du_ref[...], wut_ref[...], preferred_element_type=F32)
        _, xh, r = _rms_fwd(h1_ref[...], g2_ref[...])
        dx, dg = _rms_bwd(dhn, xh, r, g2_ref[...])
        dg2_ref[...] += dg
        dh1 = dh2 + dx
        dh1_ref[...] = dh1
        dh1b_ref[...] = dh1.astype(MX)

    tile = lambda w: pl.BlockSpec((tm, w), lambda i: (i, 0))
    return _call(
        body, name, (t // tm,),
        [tile(D_MODEL), tile(D_FF), tile(D_MODEL), _const_spec((1, D_MODEL)),
         _const_spec((D_FF, D_MODEL)), _const_spec((nj, FF_BLK, D_MODEL))],
        [tile(D_MODEL), tile(D_MODEL), tile(D_MODEL), tile(D_FF), _acc_spec((1, D_MODEL))],
        [_sds((t, D_MODEL), F32), _sds((t, D_MODEL), MX), _sds((t, D_MODEL), MX), _sds((t, D_FF), MX),
         _sds((1, D_MODEL), F32)],
        [], [dh2, act, h1, g2, w_up_t, w_down], riders)


def _mix_bwd(dh1, ycat, ym, gmix, w_out, name):
    t = dh1.shape[0]
    tm = _tile(t)
    nk = t // tm
    r = D_MODEL // N_DEV

    def body(dh1_ref, y_ref, ym_ref, gm_ref, wo_ref, dy_ref, dgm_ref, o_ref, o16_ref, acc):
        k = pl.program_id(0)

        @pl.when(k == 0)
        def _():
            dgm_ref[...] = jnp.zeros_like(dgm_ref)
            acc[...] = jnp.zeros_like(acc)

        dh = dh1_ref[...]
        acc[...] += _dot_tn(ym_ref[...], dh)
        dym = _dot_nt(dh, wo_ref[...])
        gm = gm_ref[...]
        _, yh, rr = _group_rms_fwd(y_ref[...], gm)
        outs, dgs = [], []
        for (a, b), rg in zip(_GROUPS, rr):
            dxg, dgg = _rms_bwd(dym[:, a:b], yh[:, a:b], rg, gm[:, a:b])
            outs.append(dxg)
            dgs.append(dgg)
        dy_ref[...] = jnp.concatenate(outs, axis=1)
        dgm_ref[...] += jnp.concatenate(dgs, axis=1)

        @pl.when(k == nk - 1)
        def _():
            for d in range(N_DEV):
                v = acc[d * r:(d + 1) * r, :]
                o_ref[d] = v
                o16_ref[d] = v.astype(o16_ref.dtype)

    tile = pl.BlockSpec((tm, D_MODEL), lambda i: (i, 0))
    slabs = _const_spec((N_DEV, r, D_MODEL))
    (dy, dgm, dw, dw16), _ = _call(
        body, name, (nk,), [tile, tile, tile, _const_spec((1, D_MODEL)), _const_spec((D_MODEL, D_MODEL))],
        [tile, _acc_spec((1, D_MODEL)), slabs, slabs],
        [_sds((t, D_MODEL), F32), _sds((1, D_MODEL), F32), _sds((N_DEV, r, D_MODEL), F32),
         _sds((N_DEV, r, D_MODEL), WIRE)],
        [pltpu.VMEM((D_MODEL, D_MODEL), F32)], [dh1, ycat, ym, gmix, w_out])
    return dy, dgm, (dw, dw16)


def _in_bwd(dz, h0, dh1, g1, w_in_t, after, name):
    t = h0.shape[0]
    tm = _tile(t)

    def body(dz_ref, h_ref, dh1_ref, g_ref, w_ref, after_ref, dh0_ref, dg_ref):
        @pl.when(pl.program_id(0) == 0)
        def _():
            dg_ref[...] = jnp.zeros_like(dg_ref)

        dhn = _dot(dz_ref[...], w_ref[...])
        _, xh, r = _rms_fwd(h_ref[...], g_ref[...])
        dx, dg = _rms_bwd(dhn, xh, r, g_ref[...])
        dg_ref[...] += dg
        dh0_ref[...] = dh1_ref[...] + dx

    tile = lambda w: pl.BlockSpec((tm, w), lambda i: (i, 0))
    (dh0, dg), _ = _call(
        body, name, (t // tm,),
        [tile(IN_W), tile(D_MODEL), tile(D_MODEL), _const_spec((1, D_MODEL)), _const_spec((IN_W, D_MODEL)),
         _const_spec((8, 128))],
        [tile(D_MODEL), _acc_spec((1, D_MODEL))], [_sds((t, D_MODEL), F32), _sds((1, D_MODEL), F32)],
        [], [dz, h0, dh1, g1, w_in_t, after])
    return dh0, dg


def _in_bwd_dw(dz, h0, dh1, g1, w_in_t, name):
    t = h0.shape[0]
    tm = _tile(t)
    nk = t // tm

    def body(dz_ref, h_ref, dh1_ref, g_ref, w_ref, dh0_ref, dg_ref, o_ref, o16_ref, acc):
        k = pl.program_id(0)

        @pl.when(k == 0)
        def _():
            dg_ref[...] = jnp.zeros_like(dg_ref)
            acc[...] = jnp.zeros_like(acc)

        dz_t = dz_ref[...]
        hn, xh, r = _rms_fwd(h_ref[...], g_ref[...])
        acc[...] += _dot_tn(dz_t, hn)
        dhn = _dot(dz_t, w_ref[...])
        dx, dg = _rms_bwd(dhn, xh, r, g_ref[...])
        dg_ref[...] += dg
        dh0_ref[...] = dh1_ref[...] + dx

        @pl.when(k == nk - 1)
        def _():
            for d in range(N_DEV):
                v = acc[d * IN_SHARD:(d + 1) * IN_SHARD, :]
                o_ref[d] = v
                o16_ref[d] = v.astype(o16_ref.dtype)

    tile = lambda w: pl.BlockSpec((tm, w), lambda i: (i, 0))
    slabs = _const_spec((N_DEV, IN_SHARD, D_MODEL))
    (dh0, dg, dw, dw16), _ = _call(
        body, name, (nk,),
        [tile(IN_W), tile(D_MODEL), tile(D_MODEL), _const_spec((1, D_MODEL)), _const_spec((IN_W, D_MODEL))],
        [tile(D_MODEL), _acc_spec((1, D_MODEL)), slabs, slabs],
        [_sds((t, D_MODEL), F32), _sds((1, D_MODEL), F32), _sds((N_DEV, IN_SHARD, D_MODEL), F32),
         _sds((N_DEV, IN_SHARD, D_MODEL), WIRE)],
        [pltpu.VMEM((IN_W, D_MODEL), F32)], [dz, h0, dh1, g1, w_in_t])
    return dh0, dg, (dw, dw16)


def _loss_head(h, gf, target, name):
    t = h.shape[0]
    tm = _tile(t)

    def body(h_ref, g_ref, t_ref, dh_ref, loss_ref, dg_ref):
        @pl.when(pl.program_id(0) == 0)
        def _():
            loss_ref[...] = jnp.zeros_like(loss_ref)
            dg_ref[...] = jnp.zeros_like(dg_ref)

        g = g_ref[...]
        y, xh, r = _rms_fwd(h_ref[...], g)
        err = y - t_ref[...]
        part = 0.5 * jnp.sum(jnp.mean(err * err, axis=-1, keepdims=True), axis=0, keepdims=True)
        loss_ref[...] += jnp.broadcast_to(part, loss_ref.shape)
        dx, dg = _rms_bwd(err * (1.0 / D_MODEL), xh, r, g)
        dg_ref[...] += dg
        dh_ref[...] = dx

    tile = pl.BlockSpec((tm, D_MODEL), lambda i: (i, 0))
    (dh, loss, dg), _ = _call(
        body, name, (t // tm,), [tile, _const_spec((1, D_MODEL)), tile],
        [tile, _acc_spec((1, 128)), _acc_spec((1, D_MODEL))],
        [_sds((t, D_MODEL), F32), _sds((1, 128), F32), _sds((1, D_MODEL), F32)], [], [h, gf, target])
    return dh, loss, dg


def _dw(x, y, name, split, bm, bn):
    t, m = x.shape
    n = y.shape[1]
    tk = _tile(t, DW_TILE)
    nk = t // tk
    if split == "rows":
        assert bn == n
        r, c = m // N_DEV, n
        per = bm // r
        out_block = pl.BlockSpec((per, r, c), lambda a, b, k: (a, 0, 0))
    else:
        assert bm == m
        r, c = m, n // N_DEV
        per = bn // c
        out_block = pl.BlockSpec((per, r, c), lambda a, b, k: (b, 0, 0))

    def body(x_ref, y_ref, o_ref, o16_ref, acc):
        k = pl.program_id(2)

        @pl.when(k == 0)
        def _():
            acc[...] = jnp.zeros_like(acc)

        acc[...] += _dot_tn(x_ref[...], y_ref[...])

        @pl.when(k == nk - 1)
        def _():
            for d in range(per):
                v = acc[d * r:(d + 1) * r, :] if split == "rows" else acc[:, d * c:(d + 1) * c]
                o_ref[d] = v
                o16_ref[d] = v.astype(o16_ref.dtype)

    return pl.pallas_call(
        body, name=name, grid=(m // bm, n // bn, nk),
        in_specs=[pl.BlockSpec((tk, bm), lambda a, b, k: (k, a)), pl.BlockSpec((tk, bn), lambda a, b, k: (k, b))],
        out_specs=[out_block, out_block],
        out_shape=[_sds((N_DEV, r, c), F32), _sds((N_DEV, r, c), WIRE)],
        scratch_shapes=[pltpu.VMEM((bm, bn), F32)],
        compiler_params=pltpu.CompilerParams(dimension_semantics=("arbitrary",) * 3, vmem_limit_bytes=VMEM_LIMIT),
    )(x, y)


def _adamw_math(w, g, m, v):
    m = ADAM_B1 * m + (1.0 - ADAM_B1) * g
    v = ADAM_B2 * v + (1.0 - ADAM_B2) * jnp.square(g)
    m_hat = m / (1.0 - ADAM_B1 ** ADAM_STEP)
    v_hat = v / (1.0 - ADAM_B2 ** ADAM_STEP)
    delta = -ADAM_LR * (m_hat / (jnp.sqrt(v_hat) + ADAM_EPS) + ADAM_WD * w)
    return delta, m, v


def _adamw_shard(g_own, g_recv, dev, w, m, v, after, name):
    _, r, c = w.shape
    br = r
    for cand in (256, 128, 112, 64, 56, 32, 16, 8):
        if r % cand == 0:
            br = cand
            break
    nr = r // br
    own = lambda l: pl.BlockSpec((1, br, c), lambda ll, i, d: (d[0], jnp.where(ll == l, i, (nr - 1) * (1 - l)), 0))
    recv = lambda l: pl.BlockSpec((N_DEV - 1, br, c), lambda ll, i, d: (0, jnp.where(ll == l, i, (nr - 1) * (1 - l)), 0))

    def body(dev_ref, go0, gr0, go1, gr1, w_ref, m_ref, v_ref, after_ref, g_out, d_out, m_out, v_out):
        def update(go_ref, gr_ref):
            g = go_ref[0]
            for j in range(N_DEV - 1):
                g = g + gr_ref[j].astype(F32)
            delta, mn, vn = _adamw_math(w_ref[0], g, m_ref[0], v_ref[0])
            g_out[0] = g
            d_out[0] = delta
            m_out[0] = mn
            v_out[0] = vn

        layer = pl.program_id(0)
        pl.when(layer == 0)(lambda: update(go0, gr0))
        pl.when(layer == 1)(lambda: update(go1, gr1))

    tile = pl.BlockSpec((1, br, c), lambda ll, i, d: (ll, i, 0))
    return pl.pallas_call(
        body, name=name,
        grid_spec=pltpu.PrefetchScalarGridSpec(
            num_scalar_prefetch=1, grid=(2, nr),
            in_specs=[own(0), recv(0), own(1), recv(1), tile, tile, tile,
                      pl.BlockSpec((8, 128), lambda ll, i, d: (0, 0))],
            out_specs=[tile, tile, tile, tile]),
        out_shape=[_sds((2, r, c), F32)] * 4,
        compiler_params=pltpu.CompilerParams(dimension_semantics=("arbitrary",) * 2, vmem_limit_bytes=VMEM_LIMIT),
    )(dev, g_own[0], g_recv[0], g_own[1], g_recv[1], w, m, v, after)


def _adamw_small(gs, ws, ms, vs, name):
    n = len(gs)

    def body(*refs):
        g_refs, w_refs, m_refs, v_refs = (refs[k * n:(k + 1) * n] for k in range(4))
        outs = refs[4 * n:]
        for k in range(n):
            delta, mn, vn = _adamw_math(w_refs[k][...], g_refs[k][...], m_refs[k][...], v_refs[k][...])
            outs[k][...] = delta
            outs[n + k][...] = mn
            outs[2 * n + k][...] = vn

    shapes = [_sds(w.shape, F32) for w in ws]
    res = pl.pallas_call(body, name=name, out_shape=shapes * 3,
                         compiler_params=pltpu.CompilerParams(vmem_limit_bytes=VMEM_LIMIT))(*gs, *ws, *ms, *vs)
    return res[:n], res[n:2 * n], res[2 * n:]


def _sum_parts(part, dev, name):
    def body(dev_ref, p_ref, o_ref):
        me = dev_ref[0]
        g = p_ref[me]
        for d in range(1, N_DEV):
            g = g + p_ref[jnp.bitwise_xor(me, d)]
        o_ref[...] = g

    full = pl.BlockSpec(part.shape, lambda i, d: (0, 0, 0))
    return pl.pallas_call(
        body, name=name,
        grid_spec=pltpu.PrefetchScalarGridSpec(
            num_scalar_prefetch=1, grid=(1,), in_specs=[full],
            out_specs=pl.BlockSpec(part.shape[1:], lambda i, d: (0, 0))),
        out_shape=_sds(part.shape[1:], F32))(dev, part)


HBM = pl.BlockSpec(memory_space=pltpu.HBM)
SEM = pl.BlockSpec(memory_space=pltpu.SEMAPHORE)
EFFECT = pltpu.SideEffectType.DATAFLOW_SIDE_EFFECTING


def _direct_copies(srcs, lands, ssem, rsem, scatter):
    x, y, c = _me()
    out = []
    for a in range(len(srcs)):
        for f in range(1, N_DEV):
            px = 1 - x if f & 4 else x
            py = 1 - y if f & 2 else y
            pc = 1 - c if f & 1 else c
            out.append(pltpu.make_async_remote_copy(
                src_ref=srcs[a].at[4 * px + 2 * py + pc] if scatter else srcs[a], dst_ref=lands[a].at[f - 1],
                send_sem=ssem.at[7 * a + f - 1], recv_sem=rsem.at[7 * a + f - 1],
                device_id=(px, py, pc), device_id_type=MESH))
    return out


def _send_start(arrays, scatter, name):
    arrays = list(arrays)
    n = len(arrays)
    lands = [lax.empty((N_DEV - 1,) + (a.shape[1:] if scatter else a.shape), a.dtype) for a in arrays]

    def body(*refs):
        srcs, lnds, ssem, rsem, token = refs[:n], refs[n:2 * n], refs[2 * n], refs[2 * n + 1], refs[-1]
        for cp in _direct_copies(srcs, lnds, ssem, rsem, scatter):
            cp.start()
        token[...] = jnp.zeros_like(token)

    hbm = lambda a: pltpu.HBM(a.shape, a.dtype)
    res = pl.pallas_call(
        body, name=name,
        out_shape=(pltpu.SemaphoreType.DMA((7 * n,)), pltpu.SemaphoreType.DMA((7 * n,)),
                   *[hbm(a) for a in arrays + lands], _sds((8, 128), F32)),
        in_specs=[HBM] * (2 * n),
        out_specs=(SEM, SEM, *[HBM] * (2 * n), pl.BlockSpec(memory_space=pltpu.VMEM)),
        input_output_aliases={i: 2 + i for i in range(2 * n)},
        compiler_params=pltpu.CompilerParams(has_side_effects=EFFECT),
    )(*[pltpu.with_memory_space_constraint(a, pltpu.HBM) for a in arrays + lands])
    return types.SimpleNamespace(ssem=res[0], rsem=res[1], srcs=list(res[2:2 + n]), lands=list(res[2 + n:2 + 2 * n]),
                                 token=res[-1], scatter=scatter)


def _send_wait(h, after, name):
    n = len(h.srcs)

    def body(*refs):
        srcs, lnds, ssem, rsem = refs[:n], refs[n:2 * n], refs[2 * n], refs[2 * n + 1]
        for cp in _direct_copies(srcs, lnds, ssem, rsem, h.scatter):
            cp.wait_send()
            cp.wait_recv()

    hbm = lambda a: pltpu.HBM(a.shape, a.dtype)
    res = pl.pallas_call(
        body, name=name,
        out_shape=tuple(hbm(a) for a in h.srcs + h.lands),
        in_specs=[HBM] * (2 * n) + [SEM, SEM, ANY], out_specs=[HBM] * (2 * n),
        input_output_aliases={i: i for i in range(2 * n)},
        compiler_params=pltpu.CompilerParams(has_side_effects=EFFECT),
    )(*h.srcs, *h.lands, h.ssem, h.rsem, after)
    return list(res[:n]), list(res[n:])


def _block_diag(w):
    out = jnp.zeros((LRU_W, LRU_W), w.dtype)
    for h in range(4):
        out = lax.dynamic_update_slice(out, w[h], (h * 64, h * 64))
    return out


def _unblock_diag(w):
    return jnp.concatenate([w[h * 64:(h + 1) * 64, h * 64:(h + 1) * 64] for h in range(4)], axis=0)


def _layer_params(p, l):
    row = lambda a: a[l].reshape(1, -1)
    sink_rows = jnp.repeat(p["attn_sinks"][l].reshape(4, 2), 2 * BLK, axis=1)
    sink_rows = jnp.concatenate([sink_rows, jnp.zeros((4, 4 * BLK), F32)], axis=0)
    cw = jnp.concatenate([p["conv_dw_w"][l], jnp.zeros((1, CONV_W), F32)], axis=0)
    pv = jnp.concatenate([
        row(p["conv_dw_b"]), row(p["conv_ln_g"]), row(p["conv_ln_b"]), row(p["lru_conv_b"]), row(p["lru_ba"]),
        row(p["lru_bx"]), row(p["lru_lambda"]), jnp.zeros((1, LRU_W), F32), p["lru_conv_w"][l],
        jnp.zeros((4, LRU_W), F32)], axis=0)
    return dict(
        g1=row(p["norm1"]), sink=sink_rows, cw=cw, pv=pv,
        wa=_block_diag(p["lru_wa"][l]).astype(MX), wx=_block_diag(p["lru_wx"][l]).astype(MX),
        gmix=row(p["mix_norm"]), g2=row(p["norm2"]))


_SMALL = ["norm1", "attn_sinks", "conv_dw_w", "conv_dw_b", "conv_ln_g", "conv_ln_b", "lru_conv_w", "lru_conv_b",
          "lru_wa", "lru_ba", "lru_wx", "lru_bx", "lru_lambda", "mix_norm", "norm2"]
_BIG = ["w_in", "w_out", "w_up", "w_down"]
_WEIGHTS = ["norm1", "w_in", "attn_sinks", "conv_dw_w", "conv_dw_b", "conv_ln_g", "conv_ln_b", "lru_conv_w",
            "lru_conv_b", "lru_wa", "lru_ba", "lru_wx", "lru_bx", "lru_lambda", "mix_norm", "w_out", "norm2", "w_up",
            "w_down", "final_norm"]


def kernel(x, norm1, w_in, attn_sinks, conv_dw_w, conv_dw_b, conv_ln_g, conv_ln_b, lru_conv_w, lru_conv_b, lru_wa, lru_ba, lru_wx, lru_bx, lru_lambda, mix_norm, w_out, norm2, w_up, w_down, final_norm, loss_target, m_norm1, m_w_in, m_attn_sinks, m_conv_dw_w, m_conv_dw_b, m_conv_ln_g, m_conv_ln_b, m_lru_conv_w, m_lru_conv_b, m_lru_wa, m_lru_ba, m_lru_wx, m_lru_bx, m_lru_lambda, m_mix_norm, m_w_out, m_norm2, m_w_up, m_w_down, m_final_norm, v_norm1, v_w_in, v_attn_sinks, v_conv_dw_w, v_conv_dw_b, v_conv_ln_g, v_conv_ln_b, v_lru_conv_w, v_lru_conv_b, v_lru_wa, v_lru_ba, v_lru_wx, v_lru_bx, v_lru_lambda, v_mix_norm, v_w_out, v_norm2, v_w_up, v_w_down, v_final_norm):
    w = dict(norm1=norm1, w_in=w_in, attn_sinks=attn_sinks, conv_dw_w=conv_dw_w, conv_dw_b=conv_dw_b,
             conv_ln_g=conv_ln_g, conv_ln_b=conv_ln_b, lru_conv_w=lru_conv_w, lru_conv_b=lru_conv_b, lru_wa=lru_wa,
             lru_ba=lru_ba, lru_wx=lru_wx, lru_bx=lru_bx, lru_lambda=lru_lambda, mix_norm=mix_norm, w_out=w_out,
             norm2=norm2, w_up=w_up, w_down=w_down, final_norm=final_norm)
    m = dict(norm1=m_norm1, w_in=m_w_in, attn_sinks=m_attn_sinks, conv_dw_w=m_conv_dw_w, conv_dw_b=m_conv_dw_b,
             conv_ln_g=m_conv_ln_g, conv_ln_b=m_conv_ln_b, lru_conv_w=m_lru_conv_w, lru_conv_b=m_lru_conv_b,
             lru_wa=m_lru_wa, lru_ba=m_lru_ba, lru_wx=m_lru_wx, lru_bx=m_lru_bx, lru_lambda=m_lru_lambda,
             mix_norm=m_mix_norm, w_out=m_w_out, norm2=m_norm2, w_up=m_w_up, w_down=m_w_down, final_norm=m_final_norm)
    v = dict(norm1=v_norm1, w_in=v_w_in, attn_sinks=v_attn_sinks, conv_dw_w=v_conv_dw_w, conv_dw_b=v_conv_dw_b,
             conv_ln_g=v_conv_ln_g, conv_ln_b=v_conv_ln_b, lru_conv_w=v_lru_conv_w, lru_conv_b=v_lru_conv_b,
             lru_wa=v_lru_wa, lru_ba=v_lru_ba, lru_wx=v_lru_wx, lru_bx=v_lru_bx, lru_lambda=v_lru_lambda,
             mix_norm=v_mix_norm, w_out=v_w_out, norm2=v_norm2, w_up=v_w_up, w_down=v_w_down, final_norm=v_final_norm)
    depth = w_in.shape[0]
    xi, yi, ci = _me()
    dev = (4 * xi + 2 * yi + ci).astype(jnp.int32)
    dev1 = dev.reshape(1)
    tr = lambda a: jnp.swapaxes(a, 1, 2)
    w_t, m_t, v_t = tr(w_in), tr(m_w_in), tr(v_w_in)
    wb = {n: w[n].astype(MX) for n in _BIG if n != "w_in"}
    wb["w_in"] = w_t.astype(MX)
    layer_shards = lambda l: [wb["w_out"][l], wb["w_up"][l], wb["w_down"][l]]

    _, ((g_in0, g_cw, g_lcw),) = _call(None, "gather_first", None, [], [], [], [], [],
                                        [_gather_rider([wb["w_in"][0], conv_dw_w, lru_conv_w])])
    cols = lambda g: jnp.moveaxis(g, 0, -2).reshape(g.shape[1:-1] + (N_DEV * g.shape[-1],))
    p = dict(w)
    p["conv_dw_w"] = cols(g_cw)
    p["lru_conv_w"] = cols(g_lcw)
    lp = [_layer_params(p, l) for l in range(depth)]

    gathered = [dict(w_in=g_in0.reshape(IN_W, D_MODEL)), dict()]
    saved = []
    h = x[0]
    for l in range(depth):
        q, gw = lp[l], gathered[l]
        z, hn1 = _ln_in(h, q["g1"], gw["w_in"], l == 0, f"ln_in{l}")
        riders = [_gather_rider(layer_shards(0))] if l == 0 else []
        (ycat, hl, uc, probs, psinks), got = _mixer_fwd(z, q["sink"], q["cw"], q["pv"], q["wa"], q["wx"],
                                                        f"mixer_fwd{l}", riders)
        if l == 0:
            gw["w_out"], gw["w_up"], gw["w_down"] = got[0]
            gw["w_out"] = gw["w_out"].reshape(D_MODEL, D_MODEL)
        riders = [_gather_rider([wb["w_in"][1]] + layer_shards(1))] if l == 0 else []
        (h1, act, h2, ym, hn2), got = _post_fwd(ycat, h, q["gmix"], gw["w_out"], q["g2"], gw["w_up"],
                                                gw["w_down"].reshape(D_FF, D_MODEL), f"post_fwd{l}", riders)
        if l == 0:
            nxt = gathered[1]
            nxt["w_in"], nxt["w_out"], nxt["w_up"], nxt["w_down"] = got[0]
            nxt["w_in"] = nxt["w_in"].reshape(IN_W, D_MODEL)
            nxt["w_out"] = nxt["w_out"].reshape(D_MODEL, D_MODEL)
        saved.append(dict(h0=h, z=z, hn1=hn1, ycat=ycat, hl=hl, uc=uc, probs=probs, psinks=psinks, h1=h1, act=act,
                          ym=ym, hn2=hn2))
        h = h2
    dh, loss, dgf = _loss_head(h, final_norm.reshape(1, -1), loss_target[0], "loss_head")

    grads = [None] * depth
    big = {n: [None] * depth for n in _BIG}
    pending = []

    def send_pending():
        riders = [_scatter_rider([item[3] for item in pending])] if pending else []
        return riders, list(pending)

    def record(sent, got):
        for item, recv in zip(sent, got[0] if sent else []):
            big[item[0]][item[1]] = (item[2], recv)
        del pending[:len(sent)]

    for l in reversed(range(depth)):
        q, s, gw = lp[l], saved[l], gathered[l]
        riders, sent = send_pending()
        w_up_t = jnp.swapaxes(gw["w_up"], 1, 2).reshape(D_FF, D_MODEL)
        (dh1, dh1b, dhb, du, dg2), got = _ffn_bwd(dh, s["act"], s["h1"], q["g2"], w_up_t, gw["w_down"],
                                                  f"ffn_bwd{l}", riders)
        record(sent, got)
        dycat, dgm, d_wout = _mix_bwd(dh1b, s["ycat"], s["ym"], q["gmix"], gw["w_out"], f"mix_bwd{l}")
        pending.append(("w_down", l) + tuple(_dw(s["act"], dhb, f"dw_down{l}", "rows", 2048, D_MODEL)))
        pending.append(("w_up", l) + tuple(_dw(s["hn2"], du, f"dw_up{l}", "cols", D_MODEL, 2048)))
        pending.append(("w_out", l) + tuple(d_wout))
        riders, sent = send_pending()
        (dz, dsink, dcw, dpv, dwa, dwx), got = _mixer_bwd(
            dycat, s["z"], s["ycat"], s["hl"], s["uc"], s["probs"], s["psinks"], q["sink"], q["cw"], q["pv"], q["wa"],
            q["wx"], f"mixer_bwd{l}", riders)
        record(sent, got)
        if l > 0:
            dh, dg1, d_win = _in_bwd_dw(dz, s["h0"], dh1, q["g1"], gw["w_in"], f"in_bwd{l}")
            pending.append(("w_in", l) + tuple(d_win))
        else:
            d_win = _dw(dz, s["hn1"], f"dw_in{l}", "rows", IN_W, D_MODEL)
            win_sends = _send_start([d_win[1]], True, "scatter_w_in0_start")
            dh, dg1 = _in_bwd(dz, s["h0"], dh1, q["g1"], gw["w_in"], win_sends.token, f"in_bwd{l}")
        grads[l] = dict(
            norm1=dg1[0], attn_sinks=jnp.stack([dsink[0:4, 0], dsink[0:4, 2 * BLK]], axis=1).reshape(8),
            conv_dw_w=dcw[0:CONV_K], conv_dw_b=dpv[R_CONV_B], conv_ln_g=dpv[R_LN_G], conv_ln_b=dpv[R_LN_B],
            lru_conv_w=dpv[R_LCW:R_LCW + LRU_K], lru_conv_b=dpv[R_LCONV_B], lru_wa=_unblock_diag(dwa),
            lru_ba=dpv[R_BA].reshape(4, 64), lru_wx=_unblock_diag(dwx), lru_bx=dpv[R_BX].reshape(4, 64),
            lru_lambda=dpv[R_LAM], mix_norm=dgm[0], norm2=dg2[0])

    small = [jnp.stack([grads[l][n] for l in range(depth)]) for n in _SMALL] + [dgf, loss[:, 0:1]]
    sizes = [a.size for a in small]
    total = -(-sum(sizes) // 1024) * 1024
    packed = jnp.concatenate([a.reshape(-1) for a in small] + [jnp.zeros((total - sum(sizes),), F32)])
    packed = packed.reshape(total // 128, 128)
    small_sends = _send_start([packed], False, "bcast_small_start")

    out = {}
    shard_update = lambda n, wmv, after: list(_adamw_shard(
        [big[n][l][0] for l in range(depth)], [big[n][l][1] for l in range(depth)], dev1, *wmv, after, f"adamw_{n}"))
    for n in ("w_out", "w_up", "w_down"):
        out[n] = shard_update(n, (w[n], m[n], v[n]), small_sends.token)
    _, (win_recv,) = _send_wait(win_sends, out["w_down"][1], "scatter_w_in0_wait")
    big["w_in"][0] = (d_win[0], win_recv)
    (packed,), (small_recv,) = _send_wait(small_sends, win_recv, "bcast_small_wait")
    out["w_in"] = [tr(a) for a in shard_update("w_in", (w_t, m_t, v_t), jnp.zeros((8, 128), F32))]
    parts = jnp.concatenate([packed[None], small_recv], axis=0)
    summed = _sum_parts(parts, dev1, "sum_small_grads").reshape(-1)
    small_sums, pos = [], 0
    for a, size in zip(small, sizes):
        small_sums.append(summed[pos:pos + size].reshape(a.shape))
        pos += size
    shard = lambda a: lax.dynamic_slice_in_dim(a, dev * (a.shape[-1] // N_DEV), a.shape[-1] // N_DEV, axis=a.ndim - 1)
    flat = {"lru_wa": (depth, LRU_W, 64), "lru_wx": (depth, LRU_W, 64), "final_norm": (1, D_MODEL)}
    gs, ws, ms, vs = [], [], [], []
    for n, g in zip(_SMALL + ["final_norm"], small_sums[:-1]):
        shp = flat.get(n, w[n].shape)
        gs.append((shard(g) if n in ("conv_dw_w", "lru_conv_w") else g).reshape(shp))
        ws.append(w[n].reshape(shp))
        ms.append(m[n].reshape(shp))
        vs.append(v[n].reshape(shp))
    sd, sm, sv = _adamw_small(gs, ws, ms, vs, "adamw_small")
    for j, n in enumerate(_SMALL + ["final_norm"]):
        out[n] = [a.reshape(w[n].shape) for a in (gs[j], sd[j], sm[j], sv[j])]
    loss_total = small_sums[-1][0, 0]

    result = [loss_total, dh[None]]
    for j in range(4):
        result += [out[n][j] for n in _WEIGHTS]
    return tuple(result)
```

```python
import types

import jax
import jax.numpy as jnp
from jax import lax
from jax.experimental import pallas as pl
from jax.experimental.pallas import tpu as pltpu

F32 = jnp.float32
MX = jnp.bfloat16
WIRE = jnp.bfloat16

D_MODEL = 1024
HEAD_DIM = 64
ATTN_W = 512
KV_W = 128
BLK = 128
CONV_W = 256
CONV_K = 31
LRU_W = 256
LRU_K = 4
LRU_C = 8.0
IN_W = 1792
D_FF = 4096
FF_BLK = 512
N_DEV = 8
IN_SHARD = IN_W // N_DEV
RMS_EPS = 1e-6
LN_EPS = 1e-5
MASK_VALUE = -1e30
SCALE = HEAD_DIM ** -0.5
CONV_HALO = 32
LRU_HALO = 8
CONV_CHUNK = 64
POST_TILE = 512
STREAM_TILE = 1024
DW_TILE = 1024
Q0, K0, V0, CV0, CG0, RX0, RG0 = 0, 512, 640, 768, 1024, 1280, 1536
R_CONV_B, R_LN_G, R_LN_B, R_LCONV_B, R_BA, R_BX, R_LAM, R_LCW = 0, 1, 2, 3, 4, 5, 6, 8

ADAM_LR, ADAM_B1, ADAM_B2, ADAM_EPS, ADAM_WD, ADAM_STEP = 0.001, 0.9, 0.999, 1e-08, 0.01, 10

VMEM_LIMIT = 56 * 1024 * 1024
MESH = pl.DeviceIdType.MESH
ANY = pl.BlockSpec(memory_space=pl.ANY)


def _tile(t, cap=512):
    return min(cap, t)


def _dot(a, b):
    return jnp.dot(a.astype(MX), b.astype(MX), preferred_element_type=F32)


def _dot_nt(a, b):
    return lax.dot_general(a.astype(MX), b.astype(MX), (((1,), (1,)), ((), ())), preferred_element_type=F32)


def _dot_tn(a, b):
    return lax.dot_general(a.astype(MX), b.astype(MX), (((0,), (0,)), ((), ())), preferred_element_type=F32)


def _const_spec(shape):
    nd = len(shape)
    return pl.BlockSpec(shape, lambda *_: (0,) * nd, pipeline_mode=pl.Buffered(1))


def _acc_spec(shape):
    nd = len(shape)
    return pl.BlockSpec(shape, lambda *_: (0,) * nd)


def _sds(shape, dtype):
    return jax.ShapeDtypeStruct(shape, dtype)


def _sigmoid(x):
    return jax.nn.sigmoid(x)


def _rms_fwd(x, g):
    r = lax.rsqrt(jnp.mean(x * x, axis=-1, keepdims=True) + RMS_EPS)
    xh = x * r
    return xh * g, xh, r


def _rms_bwd(dy, xh, r, g):
    t = dy * g
    dx = r * (t - xh * jnp.mean(t * xh, axis=-1, keepdims=True))
    return dx, jnp.sum(dy * xh, axis=0, keepdims=True)


_GROUPS = ((0, 512), (512, 768), (768, 1024))


def _group_rms_fwd(y, g):
    parts = [_rms_fwd(y[:, a:b], g[:, a:b]) for a, b in _GROUPS]
    return (jnp.concatenate([p[0] for p in parts], axis=1),
            jnp.concatenate([p[1] for p in parts], axis=1),
            [p[2] for p in parts])


def _gelu(x):
    c = 0.7978845608028654
    u = c * (x + 0.044715 * x * x * x)
    th = jnp.tanh(u)
    val = 0.5 * x * (1.0 + th)
    grad = 0.5 * (1.0 + th) + 0.5 * x * (1.0 - th * th) * c * (1.0 + 3.0 * 0.044715 * x * x)
    return val, grad


def _neg_expm1(x):
    series = -x * (1.0 + x * (0.5 + x * (1.0 / 6.0 + x * (1.0 / 24.0))))
    return jnp.where(x > -0.02, series, 1.0 - jnp.exp(x))


def _me():
    return lax.axis_index("x"), lax.axis_index("y"), lax.axis_index("c")


def _gather_rider(arrays):
    arrays = list(arrays)
    n = len(arrays)

    def plan(ins, outs, sems):
        ssem, rsem, lsem = sems
        x, y, c = _me()
        chips = [(1 - x, y), (x, 1 - y), (1 - x, 1 - y)]

        def copy(a, k, block, to, own=False):
            dst = outs[a].at[4 * block[0] + 2 * block[1] + block[2]]
            return pltpu.make_async_remote_copy(
                src_ref=ins[a] if own else dst, dst_ref=dst, send_sem=ssem.at[7 * a + k],
                recv_sem=rsem.at[7 * a + k], device_id=to, device_id_type=MESH)

        return x, y, c, chips, copy, lsem

    def start(ins, outs, sems):
        x, y, c, chips, copy, lsem = plan(ins, outs, sems)
        for a in range(n):
            pltpu.make_async_copy(ins[a], outs[a].at[4 * x + 2 * y + c], lsem.at[a]).start()
            copy(a, 0, (x, y, c), (x, y, 1 - c), own=True).start()
            for j, chip in enumerate(chips):
                copy(a, 1 + j, (x, y, c), (*chip, c), own=True).start()

    def mid(ins, outs, sems):
        x, y, c, chips, copy, _ = plan(ins, outs, sems)
        for a in range(n):
            for j, chip in enumerate(chips):
                copy(a, 1 + j, (*chip, c), (x, y, c)).wait_recv()
                copy(a, 4 + j, (*chip, c), (x, y, 1 - c)).start()

    def finish(ins, outs, sems):
        x, y, c, chips, copy, lsem = plan(ins, outs, sems)
        for a in range(n):
            copy(a, 0, (x, y, 1 - c), (x, y, c)).wait_recv()
            for j, chip in enumerate(chips):
                copy(a, 4 + j, (*chip, 1 - c), (x, y, c)).wait_recv()
        for a in range(n):
            copy(a, 0, (x, y, c), (x, y, 1 - c), own=True).wait_send()
            for j, chip in enumerate(chips):
                copy(a, 1 + j, (x, y, c), (*chip, c), own=True).wait_send()
                copy(a, 4 + j, (*chip, c), (x, y, 1 - c)).wait_send()
            pltpu.make_async_copy(ins[a], outs[a].at[4 * x + 2 * y + c], lsem.at[a]).wait()

    return types.SimpleNamespace(
        arrays=arrays, out_shape=[_sds((N_DEV,) + a.shape, a.dtype) for a in arrays],
        scratch=[pltpu.SemaphoreType.DMA((7 * n,)), pltpu.SemaphoreType.DMA((7 * n,)), pltpu.SemaphoreType.DMA((n,))],
        start=start, mid=mid, finish=finish)


def _scatter_rider(arrays):
    arrays = list(arrays)
    n = len(arrays)

    def copies(ins, outs, sems):
        ssem, rsem = sems
        x, y, c = _me()
        out = []
        for a in range(n):
            for f in range(1, N_DEV):
                px = 1 - x if f & 4 else x
                py = 1 - y if f & 2 else y
                pc = 1 - c if f & 1 else c
                out.append(pltpu.make_async_remote_copy(
                    src_ref=ins[a].at[4 * px + 2 * py + pc], dst_ref=outs[a].at[f - 1], send_sem=ssem.at[7 * a + f - 1],
                    recv_sem=rsem.at[7 * a + f - 1], device_id=(px, py, pc), device_id_type=MESH))
        return out

    def start(ins, outs, sems):
        for cp in copies(ins, outs, sems):
            cp.start()

    def finish(ins, outs, sems):
        for cp in copies(ins, outs, sems):
            cp.wait()

    return types.SimpleNamespace(
        arrays=arrays, out_shape=[_sds((N_DEV - 1,) + a.shape[1:], a.dtype) for a in arrays],
        scratch=[pltpu.SemaphoreType.DMA((7 * n,)), pltpu.SemaphoreType.DMA((7 * n,))],
        start=start, mid=None, finish=finish)


def _call(body, name, grid, in_specs, out_specs, out_shape, scratch, operands, riders=()):
    n_in, n_out, n_scr = len(operands), len(out_shape), len(scratch)
    nsteps = grid[0] if grid else 1
    sizes = [(len(r.arrays), len(r.out_shape), len(r.scratch)) for r in riders]

    def wrapped(*refs):
        pos = n_in
        r_ins = []
        for ri, _, _ in sizes:
            r_ins.append(refs[pos:pos + ri])
            pos += ri
        outs = refs[pos:pos + n_out]
        pos += n_out
        r_outs = []
        for _, ro, _ in sizes:
            r_outs.append(refs[pos:pos + ro])
            pos += ro
        scr = refs[pos:pos + n_scr]
        pos += n_scr
        r_sems = []
        for _, _, rs in sizes:
            r_sems.append(refs[pos:pos + rs])
            pos += rs
        step = pl.program_id(0) if grid else 0

        def at(s, fn):
            if grid:
                pl.when(step == s)(fn)
            else:
                fn()

        for r, a, b, c in zip(riders, r_ins, r_outs, r_sems):
            at(0, lambda r=r, a=a, b=b, c=c: r.start(a, b, c))
        for r, a, b, c in zip(riders, r_ins, r_outs, r_sems):
            if r.mid is not None:
                at((3 * nsteps) // 4, lambda r=r, a=a, b=b, c=c: r.mid(a, b, c))
        if body is not None:
            body(*refs[:n_in], *outs, *scr)
        for r, a, b, c in zip(riders, r_ins, r_outs, r_sems):
            at(nsteps - 1, lambda r=r, a=a, b=b, c=c: r.finish(a, b, c))

    r_arrays = [a for r in riders for a in r.arrays]
    r_shapes = [s for r in riders for s in r.out_shape]
    kwargs = {}
    if grid:
        kwargs = dict(grid=grid, compiler_params=pltpu.CompilerParams(
            dimension_semantics=("arbitrary",) * len(grid), vmem_limit_bytes=VMEM_LIMIT))
    res = pl.pallas_call(
        wrapped, name=name,
        in_specs=list(in_specs) + [ANY] * len(r_arrays),
        out_specs=list(out_specs) + [ANY] * len(r_shapes),
        out_shape=list(out_shape) + r_shapes,
        scratch_shapes=list(scratch) + [s for r in riders for s in r.scratch],
        **kwargs,
    )(*operands, *r_arrays)
    host, rest = res[:n_out], res[n_out:]
    r_res = []
    for _, ro, _ in sizes:
        r_res.append(rest[:ro])
        rest = rest[ro:]
    return host, r_res


def _ln_in(h, g1, w_in_t, keep_hn, name):
    t = h.shape[0]
    tm = _tile(t, STREAM_TILE)

    def body(h_ref, g_ref, w_ref, z_ref, *hn_ref):
        y, _, _ = _rms_fwd(h_ref[...], g_ref[...])
        hn = y.astype(MX)
        if keep_hn:
            hn_ref[0][...] = hn
        z_ref[...] = _dot_nt(hn, w_ref[...])

    tile = lambda w: pl.BlockSpec((tm, w), lambda i: (i, 0))
    outs, _ = _call(
        body, name, (t // tm,),
        [tile(D_MODEL), _const_spec((1, D_MODEL)), _const_spec((IN_W, D_MODEL))],
        [tile(IN_W)] + [tile(D_MODEL)] * keep_hn,
        [_sds((t, IN_W), F32)] + [_sds((t, D_MODEL), MX)] * keep_hn, [], [h, g1, w_in_t])
    return outs[0], (outs[1] if keep_hn else None)


def _band2(kb, g):
    lo = lax.broadcasted_iota(jnp.int32, kb.shape, 1) < HEAD_DIM
    kr = pltpu.roll(kb, HEAD_DIM, 1)
    if g == 0:
        top, bot = jnp.where(lo, kb, 0.0), jnp.where(lo, 0.0, kr)
    else:
        top, bot = jnp.where(lo, kr, 0.0), jnp.where(lo, 0.0, kb)
    return jnp.concatenate([top, bot], axis=0)


def _attn_operands(z_ref, zh_ref, b):
    rows = slice(b * BLK, (b + 1) * BLK)
    prev = zh_ref if b == 0 else z_ref
    prow = slice(0, BLK) if b == 0 else slice((b - 1) * BLK, b * BLK)
    kb = jnp.concatenate([prev[prow, K0:K0 + KV_W], z_ref[rows, K0:K0 + KV_W]], axis=0)
    vb = jnp.concatenate([prev[prow, V0:V0 + KV_W], z_ref[rows, V0:V0 + KV_W]], axis=0)
    k2 = [_band2(kb, g) for g in range(2)]
    v2 = [_band2(vb, g) for g in range(2)]
    q2 = [jnp.concatenate([z_ref[rows, (2 * g) * BLK:(2 * g + 1) * BLK], z_ref[rows, (2 * g + 1) * BLK:(2 * g + 2) * BLK]],
                          axis=0) for g in range(2)]
    return q2, k2, v2


def _attn_block(z_ref, zh_ref, sink_ref, b, first):
    q2, k2, v2 = _attn_operands(z_ref, zh_ref, b)
    rr = lax.broadcasted_iota(jnp.int32, (4 * BLK, 2 * BLK), 0) & (BLK - 1)
    cc = lax.broadcasted_iota(jnp.int32, (4 * BLK, 2 * BLK), 1)
    first_block = jnp.logical_and(first, b == 0).astype(jnp.int32)
    mask = jnp.logical_and(jnp.logical_and(cc > rr, cc <= rr + BLK), cc >= BLK * first_block)
    s = jnp.concatenate([_dot_nt(q2[g], k2[g]) for g in range(2)], axis=0) * SCALE
    w = 2 * BLK
    out, psink = [], []
    for hh in range(2):
        sh = jnp.where(mask, s[:, hh * w:(hh + 1) * w], MASK_VALUE)
        sk = jnp.concatenate([jnp.broadcast_to(sink_ref[p:p + 1, hh * w:hh * w + 1], (BLK, 1)) for p in range(4)], axis=0)
        m = jnp.maximum(jnp.max(sh, axis=1, keepdims=True), sk)
        p = jnp.exp(sh - m)
        es = jnp.exp(sk - m)
        inv = 1.0 / (jnp.sum(p, axis=1, keepdims=True) + es)
        out.append(p * inv)
        psink.append(es * inv)
    return v2, jnp.concatenate(out, axis=1), psink


def _scan_steps(a, b, n, span, reverse):
    pos = lax.broadcasted_iota(jnp.int32, a.shape, 0) & (span - 1)
    d = 1
    while d < span:
        keep = pos < span - d if reverse else pos >= d
        shift = n - d if reverse else d
        a_sh = jnp.where(keep, pltpu.roll(a, shift, 0), 1.0)
        b_sh = jnp.where(keep, pltpu.roll(b, shift, 0), 0.0)
        b = a * b_sh + b
        a = a * a_sh
        d *= 2
    return a, b


def _scan(a, b, tm, reverse):
    return _scan_steps(a, b, tm, tm, reverse)


def _shifted_copies(ext, shifts, tm):
    rows = tm + CONV_HALO - 8
    for r in range(1, 8):
        shifts[r - 1, 0:rows, :] = ext[pl.ds(r, rows), :]


def _tap(ext, shifts, off, r0, n):
    a, r = divmod(off, 8)
    lo = 8 * a + r0
    if r == 0:
        return ext[lo:lo + n, :]
    return shifts[r - 1, lo:lo + n, :]


def _glu_fill(z_ref, zh_ref, uext, ush, first, tm, sg_out=None):
    cv = z_ref[:, CV0:CV0 + CONV_W]
    sg = _sigmoid(z_ref[:, CG0:CG0 + CONV_W])
    if sg_out is not None:
        sg_out[...] = sg
    hrow = BLK - CONV_HALO
    uh = zh_ref[hrow:BLK, CV0:CV0 + CONV_W] * _sigmoid(zh_ref[hrow:BLK, CG0:CG0 + CONV_W])
    uext[0:CONV_HALO, :] = jnp.where(first, 0.0, uh)
    uext[CONV_HALO:CONV_HALO + tm, :] = cv * sg
    _shifted_copies(uext, ush, tm)


def _conv_taps(cw_ref, pv_ref, uext, ush, out_ref, tm):
    for r0 in range(0, tm, CONV_CHUNK):
        acc = jnp.broadcast_to(pv_ref[R_CONV_B:R_CONV_B + 1, :], (CONV_CHUNK, CONV_W))
        for k in range(CONV_K):
            acc = acc + cw_ref[k:k + 1, :] * _tap(uext, ush, CONV_HALO - (CONV_K - 1) + k, r0, CONV_CHUNK)
        out_ref[r0:r0 + CONV_CHUNK, :] = acc


def _ln_silu(uc, pv_ref):
    mu = jnp.mean(uc, axis=-1, keepdims=True)
    xc = uc - mu
    rs = lax.rsqrt(jnp.mean(xc * xc, axis=-1, keepdims=True) + LN_EPS)
    xh = xc * rs
    ln = xh * pv_ref[R_LN_G:R_LN_G + 1, :] + pv_ref[R_LN_B:R_LN_B + 1, :]
    sg = _sigmoid(ln)
    return xh, rs, ln, sg


def _lru_gates(z_ref, zh_ref, pv_ref, wa_ref, wx_ref, rxext, first, tm):
    rxext[0:LRU_HALO, :] = jnp.where(first, 0.0, zh_ref[BLK - LRU_HALO:BLK, RX0:RX0 + LRU_W])
    rxext[LRU_HALO:LRU_HALO + tm, :] = z_ref[:, RX0:RX0 + LRU_W]
    xc = jnp.broadcast_to(pv_ref[R_LCONV_B:R_LCONV_B + 1, :], (tm, LRU_W))
    for k in range(LRU_K):
        xc = xc + pv_ref[R_LCW + k:R_LCW + k + 1, :] * rxext[pl.ds(LRU_HALO - (LRU_K - 1) + k, tm), :]
    r = _sigmoid(_dot(xc, wa_ref[...]) + pv_ref[R_BA:R_BA + 1, :])
    ig = _sigmoid(_dot(xc, wx_ref[...]) + pv_ref[R_BX:R_BX + 1, :])
    lam = pv_ref[R_LAM:R_LAM + 1, :]
    sp = jnp.log1p(jnp.exp(-lam))
    la = (-LRU_C * r) * sp
    a = jnp.exp(la)
    mult = jnp.sqrt(_neg_expm1(2.0 * la))
    return xc, r, ig, sp, la, a, mult


def _mixer_in_specs(tm, tile_of):
    hb = tm // BLK
    return [
        pl.BlockSpec((tm, IN_W), lambda i: (tile_of(i), 0)),
        pl.BlockSpec((BLK, IN_W), lambda i: (jnp.maximum(tile_of(i) * hb - 1, 0), 0)),
        _const_spec((8, 4 * BLK)),
        _const_spec((32, CONV_W)),
        _const_spec((16, CONV_W)),
        _const_spec((LRU_W, LRU_W)),
        _const_spec((LRU_W, LRU_W)),
    ]


def _mixer_fwd(z, sink, cw, pv, wa, wx, name, riders=()):
    t = z.shape[0]
    tm = _tile(t)
    nb = tm // BLK

    def body(z_ref, zh_ref, sink_ref, cw_ref, pv_ref, wa_ref, wx_ref, y_ref, hl_ref, uc_ref, p_ref, ps_ref,
             uext, ush, rxext, hcar):
        i = pl.program_id(0)
        first = i == 0

        @pl.when(first)
        def _():
            hcar[...] = jnp.zeros_like(hcar)

        lo = lax.broadcasted_iota(jnp.int32, (4 * BLK, BLK), 1) < HEAD_DIM
        for b in range(nb):
            rows = slice(b * BLK, (b + 1) * BLK)
            v2, prob, psink = _attn_block(z_ref, zh_ref, sink_ref, b, first)
            prob = prob.astype(MX)
            p_ref[b] = prob
            ps_ref[b] = jnp.where(lo, psink[0], psink[1])
            for g in range(2):
                o = _dot(prob[2 * g * BLK:(2 * g + 2) * BLK], v2[g])
                y_ref[rows, (2 * g) * BLK:(2 * g + 1) * BLK] = o[0:BLK]
                y_ref[rows, (2 * g + 1) * BLK:(2 * g + 2) * BLK] = o[BLK:2 * BLK]
        _glu_fill(z_ref, zh_ref, uext, ush, first, tm)
        _conv_taps(cw_ref, pv_ref, uext, ush, uc_ref, tm)
        _, _, ln, sg = _ln_silu(uc_ref[...], pv_ref)
        y_ref[:, ATTN_W:ATTN_W + CONV_W] = ln * sg
        xc, _, ig, _, _, a, mult = _lru_gates(z_ref, zh_ref, pv_ref, wa_ref, wx_ref, rxext, first, tm)
        acum, h = _scan(a, mult * (ig * xc), tm, reverse=False)
        h = h + acum * hcar[0:1, :]
        hl_ref[...] = h
        hcar[0:1, :] = h[tm - 1:tm, :]
        gl, _ = _gelu(z_ref[:, RG0:RG0 + LRU_W])
        y_ref[:, ATTN_W + CONV_W:ATTN_W + CONV_W + LRU_W] = h * gl

    tile = lambda w: pl.BlockSpec((tm, w), lambda i: (i, 0))
    return _call(
        body, name, (t // tm,), _mixer_in_specs(tm, lambda i: i),
        [tile(D_MODEL), tile(LRU_W), tile(CONV_W), pl.BlockSpec((nb, 4 * BLK, 4 * BLK), lambda i: (i, 0, 0)),
         pl.BlockSpec((nb, 4 * BLK, BLK), lambda i: (i, 0, 0))],
        [_sds((t, D_MODEL), F32), _sds((t, LRU_W), F32), _sds((t, CONV_W), F32),
         _sds((t // BLK, 4 * BLK, 4 * BLK), MX), _sds((t // BLK, 4 * BLK, BLK), F32)],
        [pltpu.VMEM((tm + CONV_HALO, CONV_W), F32), pltpu.VMEM((7, tm + CONV_HALO - 8, CONV_W), F32),
         pltpu.VMEM((tm + LRU_HALO, LRU_W), F32), pltpu.VMEM((8, LRU_W), F32)],
        [z, z, sink, cw, pv, wa, wx], riders)


def _mixer_bwd(dy, z, ycat, hl, uc, probs, psinks, sink, cw, pv, wa, wx, name, riders=()):
    t = z.shape[0]
    tm = _tile(t)
    nt = t // tm
    nb = tm // BLK
    rev = lambda i: nt - 1 - i

    def body(dy_ref, z_ref, zh_ref, sink_ref, cw_ref, pv_ref, wa_ref, wx_ref, y_ref, hl_ref, hlh_ref, uc_ref,
             p_ref, ps_ref, dz_ref, dsink_ref, dcw_ref, dpv_ref, dwa_ref, dwx_ref,
             uext, ush, sgs, rxext, dkext, dvext, ducext, dsh, dcw8, dxcext, kcar, vcar, uccar, xccar, gcar):
        i = pl.program_id(0)
        first = i == nt - 1

        @pl.when(i == 0)
        def _():
            for car in (kcar, vcar, uccar, xccar, gcar, dcw8):
                car[...] = jnp.zeros_like(car)
            for acc in (dsink_ref, dpv_ref, dwa_ref, dwx_ref):
                acc[...] = jnp.zeros_like(acc)

        def addrow(r, val):
            dpv_ref[r:r + 1, :] += jnp.sum(val, axis=0, keepdims=True)

        dkext[:, 0:tm] = jnp.zeros((KV_W, tm), F32)
        dvext[:, 0:tm] = jnp.zeros((KV_W, tm), F32)
        dkext[:, tm:tm + BLK] = kcar[...]
        dvext[:, tm:tm + BLK] = vcar[...]
        lane512 = lax.broadcasted_iota(jnp.int32, (1, 4 * BLK), 1) < 2 * BLK
        lo = lax.broadcasted_iota(jnp.int32, (4 * BLK, BLK), 1) < HEAD_DIM
        hd, w2 = HEAD_DIM, 2 * BLK
        for b in range(nb):
            rows = slice(b * BLK, (b + 1) * BLK)
            band = slice(b * BLK, (b + 2) * BLK)
            q2, k2, v2 = _attn_operands(z_ref, zh_ref, b)
            prob = p_ref[b]
            psink = [ps_ref[b, :, 0:1], ps_ref[b, :, HEAD_DIM:HEAD_DIM + 1]]
            stack = lambda ref: jnp.concatenate([ref[rows, p * BLK:(p + 1) * BLK] for p in range(4)], axis=0)
            do4 = stack(dy_ref)
            dlt = do4 * stack(y_ref)
            d0 = jnp.sum(jnp.where(lo, dlt, 0.0), axis=1, keepdims=True)
            d1 = jnp.sum(jnp.where(lo, 0.0, dlt), axis=1, keepdims=True)
            dp = jnp.concatenate([_dot_nt(do4[g * w2:(g + 1) * w2], v2[g]) for g in range(2)], axis=0)
            dl = jnp.concatenate([jnp.broadcast_to(d0, (4 * BLK, w2)), jnp.broadcast_to(d1, (4 * BLK, w2))], axis=1)
            draw = (prob * (dp - dl)) * SCALE
            e0, e1 = psink[0] * d0, psink[1] * d1
            for p in range(4):
                prs = slice(p * BLK, (p + 1) * BLK)
                s0 = jnp.sum(e0[prs], axis=0, keepdims=True)
                s1 = jnp.sum(e1[prs], axis=0, keepdims=True)
                dsink_ref[p:p + 1, :] += -jnp.where(lane512, s0, s1)
            for g in range(2):
                grs = slice(g * w2, (g + 1) * w2)
                dq = _dot(draw[grs], k2[g])
                dz_ref[rows, (2 * g) * BLK:(2 * g + 1) * BLK] = dq[0:BLK].astype(dz_ref.dtype)
                dz_ref[rows, (2 * g + 1) * BLK:(2 * g + 2) * BLK] = dq[BLK:2 * BLK].astype(dz_ref.dtype)
                tk = _dot_tn(q2[g], draw[grs])
                tv = _dot_tn(do4[grs], prob[grs])
                dkext[g * hd:(g + 1) * hd, band] += tk[0:hd, 0:w2] + tk[hd:2 * hd, w2:2 * w2]
                dvext[g * hd:(g + 1) * hd, band] += tv[0:hd, 0:w2] + tv[hd:2 * hd, w2:2 * w2]
        dz_ref[:, K0:K0 + KV_W] = jnp.transpose(dkext[:, BLK:BLK + tm]).astype(dz_ref.dtype)
        dz_ref[:, V0:V0 + KV_W] = jnp.transpose(dvext[:, BLK:BLK + tm]).astype(dz_ref.dtype)
        kcar[...] = dkext[:, 0:BLK]
        vcar[...] = dvext[:, 0:BLK]

        _glu_fill(z_ref, zh_ref, uext, ush, first, tm, sg_out=sgs)
        xh, rs, ln, sg = _ln_silu(uc_ref[...], pv_ref)
        dln = dy_ref[:, ATTN_W:ATTN_W + CONV_W] * (sg * (1.0 + ln * (1.0 - sg)))
        addrow(R_LN_G, dln * xh)
        addrow(R_LN_B, dln)
        dxh = dln * pv_ref[R_LN_G:R_LN_G + 1, :]
        duc = rs * (dxh - jnp.mean(dxh, axis=-1, keepdims=True) - xh * jnp.mean(dxh * xh, axis=-1, keepdims=True))
        addrow(R_CONV_B, duc)
        ducext[0:tm, :] = duc
        ducext[tm:tm + CONV_HALO, :] = uccar[...]
        uccar[...] = duc[0:CONV_HALO, :]
        _shifted_copies(ducext, dsh, tm)
        for r0 in range(0, tm, CONV_CHUNK):
            crow = slice(r0, r0 + CONV_CHUNK)
            duc_c = ducext[crow, :]
            du = jnp.zeros((CONV_CHUNK, CONV_W), F32)
            for k in range(CONV_K):
                prod = duc_c * _tap(uext, ush, CONV_HALO - (CONV_K - 1) + k, r0, CONV_CHUNK)
                part = prod[0:8]
                for s in range(8, CONV_CHUNK, 8):
                    part = part + prod[s:s + 8]
                dcw8[k] += part
                du = du + cw_ref[k:k + 1, :] * _tap(ducext, dsh, CONV_K - 1 - k, r0, CONV_CHUNK)
            sgc = sgs[crow, :]
            dz_ref[crow, CV0:CV0 + CONV_W] = (du * sgc).astype(dz_ref.dtype)
            u_c = uext[CONV_HALO + r0:CONV_HALO + r0 + CONV_CHUNK, :]
            dz_ref[crow, CG0:CG0 + CONV_W] = (du * u_c * (1.0 - sgc)).astype(dz_ref.dtype)

        @pl.when(i == nt - 1)
        def _():
            dcw_ref[...] = jnp.sum(dcw8[...], axis=1)

        xc, r, ig, sp, la, a, mult = _lru_gates(z_ref, zh_ref, pv_ref, wa_ref, wx_ref, rxext, first, tm)
        h = hl_ref[...]
        rowi = lax.broadcasted_iota(jnp.int32, (tm, LRU_W), 0)
        hlast = jnp.where(first, 0.0, hlh_ref[7:8, :])
        hprev = jnp.where(rowi == 0, hlast, pltpu.roll(h, 1, 0))
        dyl = dy_ref[:, ATTN_W + CONV_W:ATTN_W + CONV_W + LRU_W]
        gl, dgl = _gelu(z_ref[:, RG0:RG0 + LRU_W])
        dz_ref[:, RG0:RG0 + LRU_W] = (dyl * h * dgl).astype(dz_ref.dtype)
        dh = dyl * gl + jnp.where(rowi == tm - 1, gcar[0:1, :], 0.0)
        c = jnp.where(rowi == tm - 1, 0.0, pltpu.roll(a, tm - 1, 0))
        _, gg = _scan(c, dh, tm, reverse=True)
        gcar[0:1, :] = a[0:1, :] * gg[0:1, :]
        dmult = gg * (ig * xc)
        dig = gg * mult * xc
        dxc = gg * mult * ig
        dla = gg * hprev * a - dmult * a * a / mult
        dr = dla * (-LRU_C * sp)
        lam = pv_ref[R_LAM:R_LAM + 1, :]
        dpv_ref[R_LAM:R_LAM + 1, :] += jnp.sum(dla * (-LRU_C * r), axis=0, keepdims=True) * (-_sigmoid(-lam))
        dpa = dr * r * (1.0 - r)
        dpx = dig * ig * (1.0 - ig)
        addrow(R_BA, dpa)
        addrow(R_BX, dpx)
        dxc = dxc + _dot_nt(dpa, wa_ref[...]) + _dot_nt(dpx, wx_ref[...])
        dwa_ref[...] += _dot_tn(xc, dpa)
        dwx_ref[...] += _dot_tn(xc, dpx)
        addrow(R_LCONV_B, dxc)
        dxcext[0:tm, :] = dxc
        dxcext[tm:tm + LRU_HALO, :] = xccar[...]
        xccar[...] = dxc[0:LRU_HALO, :]
        drx = jnp.zeros((tm, LRU_W), F32)
        for k in range(LRU_K):
            addrow(R_LCW + k, dxc * rxext[pl.ds(LRU_HALO - (LRU_K - 1) + k, tm), :])
            drx = drx + pv_ref[R_LCW + k:R_LCW + k + 1, :] * dxcext[pl.ds(LRU_K - 1 - k, tm), :]
        dz_ref[:, RX0:RX0 + LRU_W] = drx.astype(dz_ref.dtype)

    tile = lambda w: pl.BlockSpec((tm, w), lambda i: (rev(i), 0))
    in_specs = [tile(D_MODEL)] + _mixer_in_specs(tm, rev) + [
        tile(D_MODEL), tile(LRU_W),
        pl.BlockSpec((8, LRU_W), lambda i: (jnp.maximum(rev(i) * (tm // 8) - 1, 0), 0)),
        tile(CONV_W), pl.BlockSpec((nb, 4 * BLK, 4 * BLK), lambda i: (rev(i), 0, 0)),
        pl.BlockSpec((nb, 4 * BLK, BLK), lambda i: (rev(i), 0, 0))]
    return _call(
        body, name, (nt,), in_specs,
        [tile(IN_W), _acc_spec((8, 4 * BLK)), _acc_spec((32, CONV_W)), _acc_spec((16, CONV_W)),
         _acc_spec((LRU_W, LRU_W)), _acc_spec((LRU_W, LRU_W))],
        [_sds((t, IN_W), MX), _sds((8, 4 * BLK), F32), _sds((32, CONV_W), F32), _sds((16, CONV_W), F32),
         _sds((LRU_W, LRU_W), F32), _sds((LRU_W, LRU_W), F32)],
        [pltpu.VMEM((tm + CONV_HALO, CONV_W), F32), pltpu.VMEM((7, tm + CONV_HALO - 8, CONV_W), F32),
         pltpu.VMEM((tm, CONV_W), F32), pltpu.VMEM((tm + LRU_HALO, LRU_W), F32),
         pltpu.VMEM((KV_W, tm + BLK), F32), pltpu.VMEM((KV_W, tm + BLK), F32),
         pltpu.VMEM((tm + CONV_HALO, CONV_W), F32), pltpu.VMEM((7, tm + CONV_HALO - 8, CONV_W), F32),
         pltpu.VMEM((32, 8, CONV_W), F32), pltpu.VMEM((tm + LRU_HALO, LRU_W), F32),
         pltpu.VMEM((KV_W, BLK), F32), pltpu.VMEM((KV_W, BLK), F32),
         pltpu.VMEM((CONV_HALO, CONV_W), F32), pltpu.VMEM((LRU_HALO, LRU_W), F32), pltpu.VMEM((8, LRU_W), F32)],
        [dy, z, z, sink, cw, pv, wa, wx, ycat, hl, hl, uc, probs, psinks], riders)


def _post_fwd(ycat, h0, gmix, w_out, g2, w_up, w_down, name, riders=()):
    t = h0.shape[0]
    tm = _tile(t, POST_TILE)
    nj = D_FF // FF_BLK

    def body(y_ref, h_ref, gm_ref, wo_ref, g2_ref, wu_ref, wd_ref, h1_ref, a_ref, h2_ref, ym_ref, hn_ref):
        ym, _, _ = _group_rms_fwd(y_ref[...], gm_ref[...])
        ym = ym.astype(MX)
        ym_ref[...] = ym
        h1 = h_ref[...] + jnp.dot(ym, wo_ref[...], preferred_element_type=F32)
        h1_ref[...] = h1
        hn, _, _ = _rms_fwd(h1, g2_ref[...])
        hn = hn.astype(MX)
        hn_ref[...] = hn
        for j in range(nj):
            u = jnp.dot(hn, wu_ref[j], preferred_element_type=F32)
            a_ref[:, j * FF_BLK:(j + 1) * FF_BLK] = jnp.square(jnp.maximum(u, 0.0)).astype(MX)
        h2_ref[...] = h1 + jnp.dot(a_ref[...], wd_ref[...], preferred_element_type=F32)

    tile = lambda w: pl.BlockSpec((tm, w), lambda i: (i, 0))
    return _call(
        body, name, (t // tm,),
        [tile(D_MODEL), tile(D_MODEL), _const_spec((1, D_MODEL)), _const_spec((D_MODEL, D_MODEL)),
         _const_spec((1, D_MODEL)), _const_spec((nj, D_MODEL, FF_BLK)), _const_spec((D_FF, D_MODEL))],
        [tile(D_MODEL), tile(D_FF), tile(D_MODEL), tile(D_MODEL), tile(D_MODEL)],
        [_sds((t, D_MODEL), F32), _sds((t, D_FF), MX), _sds((t, D_MODEL), F32), _sds((t, D_MODEL), MX),
         _sds((t, D_MODEL), MX)],
        [], [ycat, h0, gmix, w_out, g2, w_up, w_down], riders)


def _ffn_bwd(dh2, act, h1, g2, w_up_t, w_down, name, riders=()):
    t = h1.shape[0]
    tm = _tile(t, POST_TILE)
    nj = D_FF // FF_BLK

    def body(dh2_ref, a_ref, h1_ref, g2_ref, wut_ref, wd_ref, dh1_ref, dh1b_ref, dh2b_ref, du_ref, dg2_ref):
        @pl.when(pl.program_id(0) == 0)
        def _():
            dg2_ref[...] = jnp.zeros_like(dg2_ref)

        dh2 = dh2_ref[...]
        dh2b = dh2.astype(MX)
        dh2b_ref[...] = dh2b
        for j in range(nj):
            cols = slice(j * FF_BLK, (j + 1) * FF_BLK)
            da = _dot_nt(dh2b, wd_ref[j])
            du_ref[:, cols] = (da * (2.0 * jnp.sqrt(a_ref[:, cols].astype(F32)))).astype(MX)
        dhn = jnp.dot(du_ref[...], wut_ref[...], preferred_element_type=F32)
        _, xh, r = _rms_fwd(h1_ref[...], g2_ref[...])
        dx, dg = _rms_bwd(dhn, xh, r, g2_ref[...])
        dg2_ref[...] += dg
        dh1 = dh2 + dx
        dh1_ref[...] = dh1
        dh1b_ref[...] = dh1.astype(MX)

    tile = lambda w: pl.BlockSpec((tm, w), lambda i: (i, 0))
    return _call(
        body, name, (t // tm,),
        [tile(D_MODEL), tile(D_FF), tile(D_MODEL), _const_spec((1, D_MODEL)),
         _const_spec((D_FF, D_MODEL)), _const_spec((nj, FF_BLK, D_MODEL))],
        [tile(D_MODEL), tile(D_MODEL), tile(D_MODEL), tile(D_FF), _acc_spec((1, D_MODEL))],
        [_sds((t, D_MODEL), F32), _sds((t, D_MODEL), MX), _sds((t, D_MODEL), MX), _sds((t, D_FF), MX),
         _sds((1, D_MODEL), F32)],
        [], [dh2, act, h1, g2, w_up_t, w_down], riders)


def _mix_bwd(dh1, ycat, ym, gmix, w_out, name):
    t = dh1.shape[0]
    tm = _tile(t)
    nk = t // tm
    r = D_MODEL // N_DEV

    def body(dh1_ref, y_ref, ym_ref, gm_ref, wo_ref, dy_ref, dgm_ref, o_ref, o16_ref, acc):
        k = pl.program_id(0)

        @pl.when(k == 0)
        def _():
            dgm_ref[...] = jnp.zeros_like(dgm_ref)
            acc[...] = jnp.zeros_like(acc)

        dh = dh1_ref[...]
        acc[...] += _dot_tn(ym_ref[...], dh)
        dym = _dot_nt(dh, wo_ref[...])
        gm = gm_ref[...]
        _, yh, rr = _group_rms_fwd(y_ref[...], gm)
        outs, dgs = [], []
        for (a, b), rg in zip(_GROUPS, rr):
            dxg, dgg = _rms_bwd(dym[:, a:b], yh[:, a:b], rg, gm[:, a:b])
            outs.append(dxg)
            dgs.append(dgg)
        dy_ref[...] = jnp.concatenate(outs, axis=1)
        dgm_ref[...] += jnp.concatenate(dgs, axis=1)

        @pl.when(k == nk - 1)
        def _():
            for d in range(N_DEV):
                v = acc[d * r:(d + 1) * r, :]
                o_ref[d] = v
                o16_ref[d] = v.astype(o16_ref.dtype)

    tile = pl.BlockSpec((tm, D_MODEL), lambda i: (i, 0))
    slabs = _const_spec((N_DEV, r, D_MODEL))
    (dy, dgm, dw, dw16), _ = _call(
        body, name, (nk,), [tile, tile, tile, _const_spec((1, D_MODEL)), _const_spec((D_MODEL, D_MODEL))],
        [tile, _acc_spec((1, D_MODEL)), slabs, slabs],
        [_sds((t, D_MODEL), F32), _sds((1, D_MODEL), F32), _sds((N_DEV, r, D_MODEL), F32),
         _sds((N_DEV, r, D_MODEL), WIRE)],
        [pltpu.VMEM((D_MODEL, D_MODEL), F32)], [dh1, ycat, ym, gmix, w_out])
    return dy, dgm, (dw, dw16)


def _in_bwd(dz, h0, dh1, g1, w_in_t, after, name):
    t = h0.shape[0]
    tm = _tile(t, STREAM_TILE)

    def body(dz_ref, h_ref, dh1_ref, g_ref, w_ref, after_ref, dh0_ref, dg_ref):
        @pl.when(pl.program_id(0) == 0)
        def _():
            dg_ref[...] = jnp.zeros_like(dg_ref)

        dhn = _dot(dz_ref[...], w_ref[...])
        _, xh, r = _rms_fwd(h_ref[...], g_ref[...])
        dx, dg = _rms_bwd(dhn, xh, r, g_ref[...])
        dg_ref[...] += dg
        dh0_ref[...] = dh1_ref[...] + dx

    tile = lambda w: pl.BlockSpec((tm, w), lambda i: (i, 0))
    (dh0, dg), _ = _call(
        body, name, (t // tm,),
        [tile(IN_W), tile(D_MODEL), tile(D_MODEL), _const_spec((1, D_MODEL)), _const_spec((IN_W, D_MODEL)),
         _const_spec((8, 128))],
        [tile(D_MODEL), _acc_spec((1, D_MODEL))], [_sds((t, D_MODEL), F32), _sds((1, D_MODEL), F32)],
        [], [dz, h0, dh1, g1, w_in_t, after])
    return dh0, dg


def _in_bwd_dw(dz, h0, dh1, g1, w_in_t, name):
    t = h0.shape[0]
    tm = _tile(t)
    nk = t // tm

    def body(dz_ref, h_ref, dh1_ref, g_ref, w_ref, dh0_ref, dg_ref, o_ref, o16_ref, acc):
        k = pl.program_id(0)

        @pl.when(k == 0)
        def _():
            dg_ref[...] = jnp.zeros_like(dg_ref)
            acc[...] = jnp.zeros_like(acc)

        dz_t = dz_ref[...]
        hn, xh, r = _rms_fwd(h_ref[...], g_ref[...])
        acc[...] += _dot_tn(dz_t, hn)
        dhn = _dot(dz_t, w_ref[...])
        dx, dg = _rms_bwd(dhn, xh, r, g_ref[...])
        dg_ref[...] += dg
        dh0_ref[...] = dh1_ref[...] + dx

        @pl.when(k == nk - 1)
        def _():
            for d in range(N_DEV):
                v = acc[d * IN_SHARD:(d + 1) * IN_SHARD, :]
                o_ref[d] = v
                o16_ref[d] = v.astype(o16_ref.dtype)

    tile = lambda w: pl.BlockSpec((tm, w), lambda i: (i, 0))
    slabs = _const_spec((N_DEV, IN_SHARD, D_MODEL))
    (dh0, dg, dw, dw16), _ = _call(
        body, name, (nk,),
        [tile(IN_W), tile(D_MODEL), tile(D_MODEL), _const_spec((1, D_MODEL)), _const_spec((IN_W, D_MODEL))],
        [tile(D_MODEL), _acc_spec((1, D_MODEL)), slabs, slabs],
        [_sds((t, D_MODEL), F32), _sds((1, D_MODEL), F32), _sds((N_DEV, IN_SHARD, D_MODEL), F32),
         _sds((N_DEV, IN_SHARD, D_MODEL), WIRE)],
        [pltpu.VMEM((IN_W, D_MODEL), F32)], [dz, h0, dh1, g1, w_in_t])
    return dh0, dg, (dw, dw16)


def _loss_head(h, gf, target, name):
    t = h.shape[0]
    tm = _tile(t, STREAM_TILE)

    def body(h_ref, g_ref, t_ref, dh_ref, loss_ref, dg_ref):
        @pl.when(pl.program_id(0) == 0)
        def _():
            loss_ref[...] = jnp.zeros_like(loss_ref)
            dg_ref[...] = jnp.zeros_like(dg_ref)

        g = g_ref[...]
        y, xh, r = _rms_fwd(h_ref[...], g)
        err = y - t_ref[...]
        part = 0.5 * jnp.sum(jnp.mean(err * err, axis=-1, keepdims=True), axis=0, keepdims=True)
        loss_ref[...] += jnp.broadcast_to(part, loss_ref.shape)
        dx, dg = _rms_bwd(err * (1.0 / D_MODEL), xh, r, g)
        dg_ref[...] += dg
        dh_ref[...] = dx

    tile = pl.BlockSpec((tm, D_MODEL), lambda i: (i, 0))
    (dh, loss, dg), _ = _call(
        body, name, (t // tm,), [tile, _const_spec((1, D_MODEL)), tile],
        [tile, _acc_spec((1, 128)), _acc_spec((1, D_MODEL))],
        [_sds((t, D_MODEL), F32), _sds((1, 128), F32), _sds((1, D_MODEL), F32)], [], [h, gf, target])
    return dh, loss, dg


def _dw(x, y, name, split, bm, bn):
    t, m = x.shape
    n = y.shape[1]
    tk = _tile(t, DW_TILE)
    nk = t // tk
    if split == "rows":
        assert bn == n
        r, c = m // N_DEV, n
        per = bm // r
        out_block = pl.BlockSpec((per, r, c), lambda a, b, k: (a, 0, 0))
    else:
        assert bm == m
        r, c = m, n // N_DEV
        per = bn // c
        out_block = pl.BlockSpec((per, r, c), lambda a, b, k: (b, 0, 0))

    def body(x_ref, y_ref, o_ref, o16_ref, acc):
        k = pl.program_id(2)

        @pl.when(k == 0)
        def _():
            acc[...] = jnp.zeros_like(acc)

        acc[...] += _dot_tn(x_ref[...], y_ref[...])

        @pl.when(k == nk - 1)
        def _():
            for d in range(per):
                v = acc[d * r:(d + 1) * r, :] if split == "rows" else acc[:, d * c:(d + 1) * c]
                o_ref[d] = v
                o16_ref[d] = v.astype(o16_ref.dtype)

    return pl.pallas_call(
        body, name=name, grid=(m // bm, n // bn, nk),
        in_specs=[pl.BlockSpec((tk, bm), lambda a, b, k: (k, a)), pl.BlockSpec((tk, bn), lambda a, b, k: (k, b))],
        out_specs=[out_block, out_block],
        out_shape=[_sds((N_DEV, r, c), F32), _sds((N_DEV, r, c), WIRE)],
        scratch_shapes=[pltpu.VMEM((bm, bn), F32)],
        compiler_params=pltpu.CompilerParams(dimension_semantics=("arbitrary",) * 3, vmem_limit_bytes=VMEM_LIMIT),
    )(x, y)


def _adamw_math(w, g, m, v):
    m = ADAM_B1 * m + (1.0 - ADAM_B1) * g
    v = ADAM_B2 * v + (1.0 - ADAM_B2) * jnp.square(g)
    m_hat = m / (1.0 - ADAM_B1 ** ADAM_STEP)
    v_hat = v / (1.0 - ADAM_B2 ** ADAM_STEP)
    delta = -ADAM_LR * (m_hat / (jnp.sqrt(v_hat) + ADAM_EPS) + ADAM_WD * w)
    return delta, m, v


def _adamw_shard(g_own, g_recv, dev, w, m, v, after, name):
    _, r, c = w.shape
    br = r
    for cand in (256, 128, 112, 64, 56, 32, 16, 8):
        if r % cand == 0:
            br = cand
            break
    nr = r // br
    own = lambda l: pl.BlockSpec((1, br, c), lambda ll, i, d: (d[0], jnp.where(ll == l, i, (nr - 1) * (1 - l)), 0))
    recv = lambda l: pl.BlockSpec((N_DEV - 1, br, c), lambda ll, i, d: (0, jnp.where(ll == l, i, (nr - 1) * (1 - l)), 0))

    def body(dev_ref, go0, gr0, go1, gr1, w_ref, m_ref, v_ref, after_ref, g_out, d_out, m_out, v_out):
        def update(go_ref, gr_ref):
            g = go_ref[0]
            for j in range(N_DEV - 1):
                g = g + gr_ref[j].astype(F32)
            delta, mn, vn = _adamw_math(w_ref[0], g, m_ref[0], v_ref[0])
            g_out[0] = g
            d_out[0] = delta
            m_out[0] = mn
            v_out[0] = vn

        layer = pl.program_id(0)
        pl.when(layer == 0)(lambda: update(go0, gr0))
        pl.when(layer == 1)(lambda: update(go1, gr1))

    tile = pl.BlockSpec((1, br, c), lambda ll, i, d: (ll, i, 0))
    return pl.pallas_call(
        body, name=name,
        grid_spec=pltpu.PrefetchScalarGridSpec(
            num_scalar_prefetch=1, grid=(2, nr),
            in_specs=[own(0), recv(0), own(1), recv(1), tile, tile, tile,
                      pl.BlockSpec((8, 128), lambda ll, i, d: (0, 0))],
            out_specs=[tile, tile, tile, tile]),
        out_shape=[_sds((2, r, c), F32)] * 4,
        compiler_params=pltpu.CompilerParams(dimension_semantics=("arbitrary",) * 2, vmem_limit_bytes=VMEM_LIMIT),
    )(dev, g_own[0], g_recv[0], g_own[1], g_recv[1], w, m, v, after)


def _adamw_small(gs, ws, ms, vs, name):
    n = len(gs)

    def body(*refs):
        g_refs, w_refs, m_refs, v_refs = (refs[k * n:(k + 1) * n] for k in range(4))
        outs = refs[4 * n:]
        for k in range(n):
            delta, mn, vn = _adamw_math(w_refs[k][...], g_refs[k][...], m_refs[k][...], v_refs[k][...])
            outs[k][...] = delta
            outs[n + k][...] = mn
            outs[2 * n + k][...] = vn

    shapes = [_sds(w.shape, F32) for w in ws]
    res = pl.pallas_call(body, name=name, out_shape=shapes * 3,
                         compiler_params=pltpu.CompilerParams(vmem_limit_bytes=VMEM_LIMIT))(*gs, *ws, *ms, *vs)
    return res[:n], res[n:2 * n], res[2 * n:]


def _sum_parts(part, dev, name):
    def body(dev_ref, p_ref, o_ref):
        me = dev_ref[0]
        g = p_ref[me]
        for d in range(1, N_DEV):
            g = g + p_ref[jnp.bitwise_xor(me, d)]
        o_ref[...] = g

    full = pl.BlockSpec(part.shape, lambda i, d: (0, 0, 0))
    return pl.pallas_call(
        body, name=name,
        grid_spec=pltpu.PrefetchScalarGridSpec(
            num_scalar_prefetch=1, grid=(1,), in_specs=[full],
            out_specs=pl.BlockSpec(part.shape[1:], lambda i, d: (0, 0))),
        out_shape=_sds(part.shape[1:], F32))(dev, part)


HBM = pl.BlockSpec(memory_space=pltpu.HBM)
SEM = pl.BlockSpec(memory_space=pltpu.SEMAPHORE)
EFFECT = pltpu.SideEffectType.DATAFLOW_SIDE_EFFECTING


def _direct_copies(srcs, lands, ssem, rsem, scatter):
    x, y, c = _me()
    out = []
    for a in range(len(srcs)):
        for f in range(1, N_DEV):
            px = 1 - x if f & 4 else x
            py = 1 - y if f & 2 else y
            pc = 1 - c if f & 1 else c
            out.append(pltpu.make_async_remote_copy(
                src_ref=srcs[a].at[4 * px + 2 * py + pc] if scatter else srcs[a], dst_ref=lands[a].at[f - 1],
                send_sem=ssem.at[7 * a + f - 1], recv_sem=rsem.at[7 * a + f - 1],
                device_id=(px, py, pc), device_id_type=MESH))
    return out


def _send_start(arrays, scatter, name):
    arrays = list(arrays)
    n = len(arrays)
    lands = [lax.empty((N_DEV - 1,) + (a.shape[1:] if scatter else a.shape), a.dtype) for a in arrays]

    def body(*refs):
        srcs, lnds, ssem, rsem, token = refs[:n], refs[n:2 * n], refs[2 * n], refs[2 * n + 1], refs[-1]
        for cp in _direct_copies(srcs, lnds, ssem, rsem, scatter):
            cp.start()
        token[...] = jnp.zeros_like(token)

    hbm = lambda a: pltpu.HBM(a.shape, a.dtype)
    res = pl.pallas_call(
        body, name=name,
        out_shape=(pltpu.SemaphoreType.DMA((7 * n,)), pltpu.SemaphoreType.DMA((7 * n,)),
                   *[hbm(a) for a in arrays + lands], _sds((8, 128), F32)),
        in_specs=[HBM] * (2 * n),
        out_specs=(SEM, SEM, *[HBM] * (2 * n), pl.BlockSpec(memory_space=pltpu.VMEM)),
        input_output_aliases={i: 2 + i for i in range(2 * n)},
        compiler_params=pltpu.CompilerParams(has_side_effects=EFFECT),
    )(*[pltpu.with_memory_space_constraint(a, pltpu.HBM) for a in arrays + lands])
    return types.SimpleNamespace(ssem=res[0], rsem=res[1], srcs=list(res[2:2 + n]), lands=list(res[2 + n:2 + 2 * n]),
                                 token=res[-1], scatter=scatter)


def _send_wait(h, after, name):
    n = len(h.srcs)

    def body(*refs):
        srcs, lnds, ssem, rsem = refs[:n], refs[n:2 * n], refs[2 * n], refs[2 * n + 1]
        for cp in _direct_copies(srcs, lnds, ssem, rsem, h.scatter):
            cp.wait_send()
            cp.wait_recv()

    hbm = lambda a: pltpu.HBM(a.shape, a.dtype)
    res = pl.pallas_call(
        body, name=name,
        out_shape=tuple(hbm(a) for a in h.srcs + h.lands),
        in_specs=[HBM] * (2 * n) + [SEM, SEM, ANY], out_specs=[HBM] * (2 * n),
        input_output_aliases={i: i for i in range(2 * n)},
        compiler_params=pltpu.CompilerParams(has_side_effects=EFFECT),
    )(*h.srcs, *h.lands, h.ssem, h.rsem, after)
    return list(res[:n]), list(res[n:])


def _block_diag(w):
    out = jnp.zeros((LRU_W, LRU_W), w.dtype)
    for h in range(4):
        out = lax.dynamic_update_slice(out, w[h], (h * 64, h * 64))
    return out


def _unblock_diag(w):
    return jnp.concatenate([w[h * 64:(h + 1) * 64, h * 64:(h + 1) * 64] for h in range(4)], axis=0)


def _layer_params(p, l):
    row = lambda a: a[l].reshape(1, -1)
    sink_rows = jnp.repeat(p["attn_sinks"][l].reshape(4, 2), 2 * BLK, axis=1)
    sink_rows = jnp.concatenate([sink_rows, jnp.zeros((4, 4 * BLK), F32)], axis=0)
    cw = jnp.concatenate([p["conv_dw_w"][l], jnp.zeros((1, CONV_W), F32)], axis=0)
    pv = jnp.concatenate([
        row(p["conv_dw_b"]), row(p["conv_ln_g"]), row(p["conv_ln_b"]), row(p["lru_conv_b"]), row(p["lru_ba"]),
        row(p["lru_bx"]), row(p["lru_lambda"]), jnp.zeros((1, LRU_W), F32), p["lru_conv_w"][l],
        jnp.zeros((4, LRU_W), F32)], axis=0)
    return dict(
        g1=row(p["norm1"]), sink=sink_rows, cw=cw, pv=pv,
        wa=_block_diag(p["lru_wa"][l]).astype(MX), wx=_block_diag(p["lru_wx"][l]).astype(MX),
        gmix=row(p["mix_norm"]), g2=row(p["norm2"]))


_SMALL = ["norm1", "attn_sinks", "conv_dw_w", "conv_dw_b", "conv_ln_g", "conv_ln_b", "lru_conv_w", "lru_conv_b",
          "lru_wa", "lru_ba", "lru_wx", "lru_bx", "lru_lambda", "mix_norm", "norm2"]
_BIG = ["w_in", "w_out", "w_up", "w_down"]
_WEIGHTS = ["norm1", "w_in", "attn_sinks", "conv_dw_w", "conv_dw_b", "conv_ln_g", "conv_ln_b", "lru_conv_w",
            "lru_conv_b", "lru_wa", "lru_ba", "lru_wx", "lru_bx", "lru_lambda", "mix_norm", "w_out", "norm2", "w_up",
            "w_down", "final_norm"]


def kernel(x, norm1, w_in, attn_sinks, conv_dw_w, conv_dw_b, conv_ln_g, conv_ln_b, lru_conv_w, lru_conv_b, lru_wa, lru_ba, lru_wx, lru_bx, lru_lambda, mix_norm, w_out, norm2, w_up, w_down, final_norm, loss_target, m_norm1, m_w_in, m_attn_sinks, m_conv_dw_w, m_conv_dw_b, m_conv_ln_g, m_conv_ln_b, m_lru_conv_w, m_lru_conv_b, m_lru_wa, m_lru_ba, m_lru_wx, m_lru_bx, m_lru_lambda, m_mix_norm, m_w_out, m_norm2, m_w_up, m_w_down, m_final_norm, v_norm1, v_w_in, v_attn_sinks, v_conv_dw_w, v_conv_dw_b, v_conv_ln_g, v_conv_ln_b, v_lru_conv_w, v_lru_conv_b, v_lru_wa, v_lru_ba, v_lru_wx, v_lru_bx, v_lru_lambda, v_mix_norm, v_w_out, v_norm2, v_w_up, v_w_down, v_final_norm):
    w = dict(norm1=norm1, w_in=w_in, attn_sinks=attn_sinks, conv_dw_w=conv_dw_w, conv_dw_b=conv_dw_b,
             conv_ln_g=conv_ln_g, conv_ln_b=conv_ln_b, lru_conv_w=lru_conv_w, lru_conv_b=lru_conv_b, lru_wa=lru_wa,
             lru_ba=lru_ba, lru_wx=lru_wx, lru_bx=lru_bx, lru_lambda=lru_lambda, mix_norm=mix_norm, w_out=w_out,
             norm2=norm2, w_up=w_up, w_down=w_down, final_norm=final_norm)
    m = dict(norm1=m_norm1, w_in=m_w_in, attn_sinks=m_attn_sinks, conv_dw_w=m_conv_dw_w, conv_dw_b=m_conv_dw_b,
             conv_ln_g=m_conv_ln_g, conv_ln_b=m_conv_ln_b, lru_conv_w=m_lru_conv_w, lru_conv_b=m_lru_conv_b,
             lru_wa=m_lru_wa, lru_ba=m_lru_ba, lru_wx=m_lru_wx, lru_bx=m_lru_bx, lru_lambda=m_lru_lambda,
             mix_norm=m_mix_norm, w_out=m_w_out, norm2=m_norm2, w_up=m_w_up, w_down=m_w_down, final_norm=m_final_norm)
    v = dict(norm1=v_norm1, w_in=v_w_in, attn_sinks=v_attn_sinks, conv_dw_w=v_conv_dw_w, conv_dw_b=v_conv_dw_b,
             conv_ln_g=v_conv_ln_g, conv_ln_b=v_conv_ln_b, lru_conv_w=v_lru_conv_w, lru_conv_b=v_lru_conv_b,
             lru_wa=v_lru_wa, lru_ba=v_lru_ba, lru_wx=v_lru_wx, lru_bx=v_lru_bx, lru_lambda=v_lru_lambda,
             mix_norm=v_mix_norm, w_out=v_w_out, norm2=v_norm2, w_up=v_w_up, w_down=v_w_down, final_norm=v_final_norm)
    depth = w_in.shape[0]
    xi, yi, ci = _me()
    dev = (4 * xi + 2 * yi + ci).astype(jnp.int32)
    dev1 = dev.reshape(1)
    tr = lambda a: jnp.swapaxes(a, 1, 2)
    w_t, m_t, v_t = tr(w_in), tr(m_w_in), tr(v_w_in)
    wb = {n: w[n].astype(MX) for n in _BIG if n != "w_in"}
    wb["w_in"] = w_t.astype(MX)
    layer_shards = lambda l: [wb["w_out"][l], wb["w_up"][l], wb["w_down"][l]]

    _, ((g_in0, g_cw, g_lcw),) = _call(None, "gather_first", None, [], [], [], [], [],
                                        [_gather_rider([wb["w_in"][0], conv_dw_w, lru_conv_w])])
    cols = lambda g: jnp.moveaxis(g, 0, -2).reshape(g.shape[1:-1] + (N_DEV * g.shape[-1],))
    p = dict(w)
    p["conv_dw_w"] = cols(g_cw)
    p["lru_conv_w"] = cols(g_lcw)
    lp = [_layer_params(p, l) for l in range(depth)]

    gathered = [dict(w_in=g_in0.reshape(IN_W, D_MODEL)), dict()]
    saved = []
    h = x[0]
    for l in range(depth):
        q, gw = lp[l], gathered[l]
        z, hn1 = _ln_in(h, q["g1"], gw["w_in"], l == 0, f"ln_in{l}")
        riders = [_gather_rider(layer_shards(0))] if l == 0 else []
        (ycat, hl, uc, probs, psinks), got = _mixer_fwd(z, q["sink"], q["cw"], q["pv"], q["wa"], q["wx"],
                                                        f"mixer_fwd{l}", riders)
        if l == 0:
            gw["w_out"], gw["w_up"], gw["w_down"] = got[0]
            gw["w_out"] = gw["w_out"].reshape(D_MODEL, D_MODEL)
        riders = [_gather_rider([wb["w_in"][1]] + layer_shards(1))] if l == 0 else []
        (h1, act, h2, ym, hn2), got = _post_fwd(ycat, h, q["gmix"], gw["w_out"], q["g2"], gw["w_up"],
                                                gw["w_down"].reshape(D_FF, D_MODEL), f"post_fwd{l}", riders)
        if l == 0:
            nxt = gathered[1]
            nxt["w_in"], nxt["w_out"], nxt["w_up"], nxt["w_down"] = got[0]
            nxt["w_in"] = nxt["w_in"].reshape(IN_W, D_MODEL)
            nxt["w_out"] = nxt["w_out"].reshape(D_MODEL, D_MODEL)
        saved.append(dict(h0=h, z=z, hn1=hn1, ycat=ycat, hl=hl, uc=uc, probs=probs, psinks=psinks, h1=h1, act=act,
                          ym=ym, hn2=hn2))
        h = h2
    dh, loss, dgf = _loss_head(h, final_norm.reshape(1, -1), loss_target[0], "loss_head")

    grads = [None] * depth
    big = {n: [None] * depth for n in _BIG}
    pending = []

    def send_pending():
        riders = [_scatter_rider([item[3] for item in pending])] if pending else []
        return riders, list(pending)

    def record(sent, got):
        for item, recv in zip(sent, got[0] if sent else []):
            big[item[0]][item[1]] = (item[2], recv)
        del pending[:len(sent)]

    for l in reversed(range(depth)):
        q, s, gw = lp[l], saved[l], gathered[l]
        riders, sent = send_pending()
        w_up_t = jnp.swapaxes(gw["w_up"], 1, 2).reshape(D_FF, D_MODEL)
        (dh1, dh1b, dhb, du, dg2), got = _ffn_bwd(dh, s["act"], s["h1"], q["g2"], w_up_t, gw["w_down"],
                                                  f"ffn_bwd{l}", riders)
        record(sent, got)
        dycat, dgm, d_wout = _mix_bwd(dh1b, s["ycat"], s["ym"], q["gmix"], gw["w_out"], f"mix_bwd{l}")
        pending.append(("w_down", l) + tuple(_dw(s["act"], dhb, f"dw_down{l}", "rows", 2048, D_MODEL)))
        pending.append(("w_up", l) + tuple(_dw(s["hn2"], du, f"dw_up{l}", "cols", D_MODEL, 2048)))
        pending.append(("w_out", l) + tuple(d_wout))
        riders, sent = send_pending()
        (dz, dsink, dcw, dpv, dwa, dwx), got = _mixer_bwd(
            dycat, s["z"], s["ycat"], s["hl"], s["uc"], s["probs"], s["psinks"], q["sink"], q["cw"], q["pv"], q["wa"],
            q["wx"], f"mixer_bwd{l}", riders)
        record(sent, got)
        if l > 0:
            dh, dg1, d_win = _in_bwd_dw(dz, s["h0"], dh1, q["g1"], gw["w_in"], f"in_bwd{l}")
            pending.append(("w_in", l) + tuple(d_win))
        else:
            d_win = _dw(dz, s["hn1"], f"dw_in{l}", "rows", IN_W, D_MODEL)
            win_sends = _send_start([d_win[1]], True, "scatter_w_in0_start")
            dh, dg1 = _in_bwd(dz, s["h0"], dh1, q["g1"], gw["w_in"], win_sends.token, f"in_bwd{l}")
        grads[l] = dict(
            norm1=dg1[0], attn_sinks=jnp.stack([dsink[0:4, 0], dsink[0:4, 2 * BLK]], axis=1).reshape(8),
            conv_dw_w=dcw[0:CONV_K], conv_dw_b=dpv[R_CONV_B], conv_ln_g=dpv[R_LN_G], conv_ln_b=dpv[R_LN_B],
            lru_conv_w=dpv[R_LCW:R_LCW + LRU_K], lru_conv_b=dpv[R_LCONV_B], lru_wa=_unblock_diag(dwa),
            lru_ba=dpv[R_BA].reshape(4, 64), lru_wx=_unblock_diag(dwx), lru_bx=dpv[R_BX].reshape(4, 64),
            lru_lambda=dpv[R_LAM], mix_norm=dgm[0], norm2=dg2[0])

    small = [jnp.stack([grads[l][n] for l in range(depth)]) for n in _SMALL] + [dgf, loss[:, 0:1]]
    sizes = [a.size for a in small]
    total = -(-sum(sizes) // 1024) * 1024
    packed = jnp.concatenate([a.reshape(-1) for a in small] + [jnp.zeros((total - sum(sizes),), F32)])
    packed = packed.reshape(total // 128, 128)
    small_sends = _send_start([packed], False, "bcast_small_start")

    out = {}
    shard_update = lambda n, wmv, after: list(_adamw_shard(
        [big[n][l][0] for l in range(depth)], [big[n][l][1] for l in range(depth)], dev1, *wmv, after, f"adamw_{n}"))
    for n in ("w_out", "w_up", "w_down"):
        out[n] = shard_update(n, (w[n], m[n], v[n]), small_sends.token)
    _, (win_recv,) = _send_wait(win_sends, out["w_down"][1], "scatter_w_in0_wait")
    big["w_in"][0] = (d_win[0], win_recv)
    (packed,), (small_recv,) = _send_wait(small_sends, win_recv, "bcast_small_wait")
    out["w_in"] = [tr(a) for a in shard_update("w_in", (w_t, m_t, v_t), jnp.zeros((8, 128), F32))]
    parts = jnp.concatenate([packed[None], small_recv], axis=0)
    summed = _sum_parts(parts, dev1, "sum_small_grads").reshape(-1)
    small_sums, pos = [], 0
    for a, size in zip(small, sizes):
        small_sums.append(summed[pos:pos + size].reshape(a.shape))
        pos += size
    shard = lambda a: lax.dynamic_slice_in_dim(a, dev * (a.shape[-1] // N_DEV), a.shape[-1] // N_DEV, axis=a.ndim - 1)
    flat = {"lru_wa": (depth, LRU_W, 64), "lru_wx": (depth, LRU_W, 64), "final_norm": (1, D_MODEL)}
    gs, ws, ms, vs = [], [], [], []
    for n, g in zip(_SMALL + ["final_norm"], small_sums[:-1]):
        shp = flat.get(n, w[n].shape)
        gs.append((shard(g) if n in ("conv_dw_w", "lru_conv_w") else g).reshape(shp))
        ws.append(w[n].reshape(shp))
        ms.append(m[n].reshape(shp))
        vs.append(v[n].reshape(shp))
    sd, sm, sv = _adamw_small(gs, ws, ms, vs, "adamw_small")
    for j, n in enumerate(_SMALL + ["final_norm"]):
        out[n] = [a.reshape(w[n].shape) for a in (gs[j], sd[j], sm[j], sv[j])]
    loss_total = small_sums[-1][0, 0]

    result = [loss_total, dh[None]]
    for j in range(4):
        result += [out[n][j] for n in _WEIGHTS]
    return tuple(result)
```

```python
import types

import jax
import jax.numpy as jnp
from jax import lax
from jax.experimental import pallas as pl
from jax.experimental.pallas import tpu as pltpu

F32 = jnp.float32
MX = jnp.bfloat16
WIRE = jnp.bfloat16

D_MODEL = 1024
HEAD_DIM = 64
ATTN_W = 512
KV_W = 128
BLK = 128
CONV_W = 256
CONV_K = 31
LRU_W = 256
LRU_K = 4
LRU_C = 8.0
IN_W = 1792
D_FF = 4096
FF_BLK = 512
N_DEV = 8
IN_SHARD = IN_W // N_DEV
RMS_EPS = 1e-6
LN_EPS = 1e-5
MASK_VALUE = -1e30
SCALE = HEAD_DIM ** -0.5
CONV_HALO = 32
LRU_HALO = 8
CONV_CHUNK = 64
POST_TILE = 512
STREAM_TILE = 1024
DW_TILE = 1024
Q0, K0, V0, CV0, CG0, RX0, RG0 = 0, 512, 640, 768, 1024, 1280, 1536
R_CONV_B, R_LN_G, R_LN_B, R_LCONV_B, R_BA, R_BX, R_LAM, R_LCW = 0, 1, 2, 3, 4, 5, 6, 8

ADAM_LR, ADAM_B1, ADAM_B2, ADAM_EPS, ADAM_WD, ADAM_STEP = 0.001, 0.9, 0.999, 1e-08, 0.01, 10

VMEM_LIMIT = 56 * 1024 * 1024
MESH = pl.DeviceIdType.MESH
ANY = pl.BlockSpec(memory_space=pl.ANY)


def _tile(t, cap=512):
    return min(cap, t)


def _dot(a, b):
    return jnp.dot(a.astype(MX), b.astype(MX), preferred_element_type=F32)


def _dot_nt(a, b):
    return lax.dot_general(a.astype(MX), b.astype(MX), (((1,), (1,)), ((), ())), preferred_element_type=F32)


def _dot_tn(a, b):
    return lax.dot_general(a.astype(MX), b.astype(MX), (((0,), (0,)), ((), ())), preferred_element_type=F32)


def _const_spec(shape):
    nd = len(shape)
    return pl.BlockSpec(shape, lambda *_: (0,) * nd, pipeline_mode=pl.Buffered(1))


def _acc_spec(shape):
    nd = len(shape)
    return pl.BlockSpec(shape, lambda *_: (0,) * nd)


def _sds(shape, dtype):
    return jax.ShapeDtypeStruct(shape, dtype)


def _sigmoid(x):
    return jax.nn.sigmoid(x)


def _rms_fwd(x, g):
    r = lax.rsqrt(jnp.mean(x * x, axis=-1, keepdims=True) + RMS_EPS)
    xh = x * r
    return xh * g, xh, r


def _rms_bwd(dy, xh, r, g):
    t = dy * g
    dx = r * (t - xh * jnp.mean(t * xh, axis=-1, keepdims=True))
    return dx, jnp.sum(dy * xh, axis=0, keepdims=True)


_GROUPS = ((0, 512), (512, 768), (768, 1024))


def _group_rms_fwd(y, g):
    parts = [_rms_fwd(y[:, a:b], g[:, a:b]) for a, b in _GROUPS]
    return (jnp.concatenate([p[0] for p in parts], axis=1),
            jnp.concatenate([p[1] for p in parts], axis=1),
            [p[2] for p in parts])


def _gelu(x):
    c = 0.7978845608028654
    u = c * (x + 0.044715 * x * x * x)
    th = jnp.tanh(u)
    val = 0.5 * x * (1.0 + th)
    grad = 0.5 * (1.0 + th) + 0.5 * x * (1.0 - th * th) * c * (1.0 + 3.0 * 0.044715 * x * x)
    return val, grad


def _neg_expm1(x):
    series = -x * (1.0 + x * (0.5 + x * (1.0 / 6.0 + x * (1.0 / 24.0))))
    return jnp.where(x > -0.02, series, 1.0 - jnp.exp(x))


def _me():
    return lax.axis_index("x"), lax.axis_index("y"), lax.axis_index("c")


def _gather_rider(arrays):
    arrays = list(arrays)
    n = len(arrays)

    def plan(ins, outs, sems):
        ssem, rsem, lsem = sems
        x, y, c = _me()
        chips = [(1 - x, y), (x, 1 - y), (1 - x, 1 - y)]

        def copy(a, k, block, to, own=False):
            dst = outs[a].at[4 * block[0] + 2 * block[1] + block[2]]
            return pltpu.make_async_remote_copy(
                src_ref=ins[a] if own else dst, dst_ref=dst, send_sem=ssem.at[7 * a + k],
                recv_sem=rsem.at[7 * a + k], device_id=to, device_id_type=MESH)

        return x, y, c, chips, copy, lsem

    def start(ins, outs, sems):
        x, y, c, chips, copy, lsem = plan(ins, outs, sems)
        for a in range(n):
            pltpu.make_async_copy(ins[a], outs[a].at[4 * x + 2 * y + c], lsem.at[a]).start()
            copy(a, 0, (x, y, c), (x, y, 1 - c), own=True).start()
            for j, chip in enumerate(chips):
                copy(a, 1 + j, (x, y, c), (*chip, c), own=True).start()

    def mid(ins, outs, sems):
        x, y, c, chips, copy, _ = plan(ins, outs, sems)
        for a in range(n):
            for j, chip in enumerate(chips):
                copy(a, 1 + j, (*chip, c), (x, y, c)).wait_recv()
                copy(a, 4 + j, (*chip, c), (x, y, 1 - c)).start()

    def finish(ins, outs, sems):
        x, y, c, chips, copy, lsem = plan(ins, outs, sems)
        for a in range(n):
            copy(a, 0, (x, y, 1 - c), (x, y, c)).wait_recv()
            for j, chip in enumerate(chips):
                copy(a, 4 + j, (*chip, 1 - c), (x, y, c)).wait_recv()
        for a in range(n):
            copy(a, 0, (x, y, c), (x, y, 1 - c), own=True).wait_send()
            for j, chip in enumerate(chips):
                copy(a, 1 + j, (x, y, c), (*chip, c), own=True).wait_send()
                copy(a, 4 + j, (*chip, c), (x, y, 1 - c)).wait_send()
            pltpu.make_async_copy(ins[a], outs[a].at[4 * x + 2 * y + c], lsem.at[a]).wait()

    return types.SimpleNamespace(
        arrays=arrays, out_shape=[_sds((N_DEV,) + a.shape, a.dtype) for a in arrays],
        scratch=[pltpu.SemaphoreType.DMA((7 * n,)), pltpu.SemaphoreType.DMA((7 * n,)), pltpu.SemaphoreType.DMA((n,))],
        start=start, mid=mid, finish=finish)


def _scatter_rider(arrays):
    arrays = list(arrays)
    n = len(arrays)

    def copies(ins, outs, sems):
        ssem, rsem = sems
        x, y, c = _me()
        out = []
        for a in range(n):
            for f in range(1, N_DEV):
                px = 1 - x if f & 4 else x
                py = 1 - y if f & 2 else y
                pc = 1 - c if f & 1 else c
                out.append(pltpu.make_async_remote_copy(
                    src_ref=ins[a].at[4 * px + 2 * py + pc], dst_ref=outs[a].at[f - 1], send_sem=ssem.at[7 * a + f - 1],
                    recv_sem=rsem.at[7 * a + f - 1], device_id=(px, py, pc), device_id_type=MESH))
        return out

    def start(ins, outs, sems):
        for cp in copies(ins, outs, sems):
            cp.start()

    def finish(ins, outs, sems):
        for cp in copies(ins, outs, sems):
            cp.wait()

    return types.SimpleNamespace(
        arrays=arrays, out_shape=[_sds((N_DEV - 1,) + a.shape[1:], a.dtype) for a in arrays],
        scratch=[pltpu.SemaphoreType.DMA((7 * n,)), pltpu.SemaphoreType.DMA((7 * n,))],
        start=start, mid=None, finish=finish)


def _call(body, name, grid, in_specs, out_specs, out_shape, scratch, operands, riders=()):
    n_in, n_out, n_scr = len(operands), len(out_shape), len(scratch)
    nsteps = grid[0] if grid else 1
    sizes = [(len(r.arrays), len(r.out_shape), len(r.scratch)) for r in riders]

    def wrapped(*refs):
        pos = n_in
        r_ins = []
        for ri, _, _ in sizes:
            r_ins.append(refs[pos:pos + ri])
            pos += ri
        outs = refs[pos:pos + n_out]
        pos += n_out
        r_outs = []
        for _, ro, _ in sizes:
            r_outs.append(refs[pos:pos + ro])
            pos += ro
        scr = refs[pos:pos + n_scr]
        pos += n_scr
        r_sems = []
        for _, _, rs in sizes:
            r_sems.append(refs[pos:pos + rs])
            pos += rs
        step = pl.program_id(0) if grid else 0

        def at(s, fn):
            if grid:
                pl.when(step == s)(fn)
            else:
                fn()

        for r, a, b, c in zip(riders, r_ins, r_outs, r_sems):
            at(0, lambda r=r, a=a, b=b, c=c: r.start(a, b, c))
        for r, a, b, c in zip(riders, r_ins, r_outs, r_sems):
            if r.mid is not None:
                at((3 * nsteps) // 4, lambda r=r, a=a, b=b, c=c: r.mid(a, b, c))
        if body is not None:
            body(*refs[:n_in], *outs, *scr)
        for r, a, b, c in zip(riders, r_ins, r_outs, r_sems):
            at(nsteps - 1, lambda r=r, a=a, b=b, c=c: r.finish(a, b, c))

    r_arrays = [a for r in riders for a in r.arrays]
    r_shapes = [s for r in riders for s in r.out_shape]
    kwargs = {}
    if grid:
        kwargs = dict(grid=grid, compiler_params=pltpu.CompilerParams(
            dimension_semantics=("arbitrary",) * len(grid), vmem_limit_bytes=VMEM_LIMIT))
    res = pl.pallas_call(
        wrapped, name=name,
        in_specs=list(in_specs) + [ANY] * len(r_arrays),
        out_specs=list(out_specs) + [ANY] * len(r_shapes),
        out_shape=list(out_shape) + r_shapes,
        scratch_shapes=list(scratch) + [s for r in riders for s in r.scratch],
        **kwargs,
    )(*operands, *r_arrays)
    host, rest = res[:n_out], res[n_out:]
    r_res = []
    for _, ro, _ in sizes:
        r_res.append(rest[:ro])
        rest = rest[ro:]
    return host, r_res


def _ln_in(h, g1, w_in_t, keep_hn, name):
    t = h.shape[0]
    tm = _tile(t, STREAM_TILE)

    def body(h_ref, g_ref, w_ref, z_ref, *hn_ref):
        y, _, _ = _rms_fwd(h_ref[...], g_ref[...])
        hn = y.astype(MX)
        if keep_hn:
            hn_ref[0][...] = hn
        z_ref[...] = _dot_nt(hn, w_ref[...])

    tile = lambda w: pl.BlockSpec((tm, w), lambda i: (i, 0))
    outs, _ = _call(
        body, name, (t // tm,),
        [tile(D_MODEL), _const_spec((1, D_MODEL)), _const_spec((IN_W, D_MODEL))],
        [tile(IN_W)] + [tile(D_MODEL)] * keep_hn,
        [_sds((t, IN_W), F32)] + [_sds((t, D_MODEL), MX)] * keep_hn, [], [h, g1, w_in_t])
    return outs[0], (outs[1] if keep_hn else None)


def _band2(kb, g):
    lo = lax.broadcasted_iota(jnp.int32, kb.shape, 1) < HEAD_DIM
    kr = pltpu.roll(kb, HEAD_DIM, 1)
    if g == 0:
        top, bot = jnp.where(lo, kb, 0.0), jnp.where(lo, 0.0, kr)
    else:
        top, bot = jnp.where(lo, kr, 0.0), jnp.where(lo, 0.0, kb)
    return jnp.concatenate([top, bot], axis=0)


def _attn_operands(z_ref, zh_ref, b):
    rows = slice(b * BLK, (b + 1) * BLK)
    prev = zh_ref if b == 0 else z_ref
    prow = slice(0, BLK) if b == 0 else slice((b - 1) * BLK, b * BLK)
    kb = jnp.concatenate([prev[prow, K0:K0 + KV_W], z_ref[rows, K0:K0 + KV_W]], axis=0)
    vb = jnp.concatenate([prev[prow, V0:V0 + KV_W], z_ref[rows, V0:V0 + KV_W]], axis=0)
    k2 = [_band2(kb, g) for g in range(2)]
    v2 = [_band2(vb, g) for g in range(2)]
    q2 = [jnp.concatenate([z_ref[rows, (2 * g) * BLK:(2 * g + 1) * BLK], z_ref[rows, (2 * g + 1) * BLK:(2 * g + 2) * BLK]],
                          axis=0) for g in range(2)]
    return q2, k2, v2


def _attn_block(z_ref, zh_ref, sink_ref, b, first):
    q2, k2, v2 = _attn_operands(z_ref, zh_ref, b)
    rr = lax.broadcasted_iota(jnp.int32, (4 * BLK, 2 * BLK), 0) & (BLK - 1)
    cc = lax.broadcasted_iota(jnp.int32, (4 * BLK, 2 * BLK), 1)
    first_block = jnp.logical_and(first, b == 0).astype(jnp.int32)
    mask = jnp.logical_and(jnp.logical_and(cc > rr, cc <= rr + BLK), cc >= BLK * first_block)
    s = jnp.concatenate([_dot_nt(q2[g], k2[g]) for g in range(2)], axis=0) * SCALE
    w = 2 * BLK
    out, psink = [], []
    for hh in range(2):
        sh = jnp.where(mask, s[:, hh * w:(hh + 1) * w], MASK_VALUE)
        sk = jnp.concatenate([jnp.broadcast_to(sink_ref[p:p + 1, hh * w:hh * w + 1], (BLK, 1)) for p in range(4)], axis=0)
        m = jnp.maximum(jnp.max(sh, axis=1, keepdims=True), sk)
        p = jnp.exp(sh - m)
        es = jnp.exp(sk - m)
        inv = 1.0 / (jnp.sum(p, axis=1, keepdims=True) + es)
        out.append(p * inv)
        psink.append(es * inv)
    return v2, jnp.concatenate(out, axis=1), psink


def _scan_steps(a, b, n, span, reverse):
    pos = lax.broadcasted_iota(jnp.int32, a.shape, 0) & (span - 1)
    d = 1
    while d < span:
        keep = pos < span - d if reverse else pos >= d
        shift = n - d if reverse else d
        a_sh = jnp.where(keep, pltpu.roll(a, shift, 0), 1.0)
        b_sh = jnp.where(keep, pltpu.roll(b, shift, 0), 0.0)
        b = a * b_sh + b
        a = a * a_sh
        d *= 2
    return a, b


def _scan(a, b, tm, reverse):
    return _scan_steps(a, b, tm, tm, reverse)


def _shifted_copies(ext, shifts, tm):
    rows = tm + CONV_HALO - 8
    for r in range(1, 8):
        shifts[r - 1, 0:rows, :] = ext[pl.ds(r, rows), :]


def _tap(ext, shifts, off, r0, n):
    a, r = divmod(off, 8)
    lo = 8 * a + r0
    if r == 0:
        return ext[lo:lo + n, :]
    return shifts[r - 1, lo:lo + n, :]


def _glu_fill(z_ref, zh_ref, uext, ush, first, tm, sg_out=None):
    cv = z_ref[:, CV0:CV0 + CONV_W]
    sg = _sigmoid(z_ref[:, CG0:CG0 + CONV_W])
    if sg_out is not None:
        sg_out[...] = sg
    hrow = BLK - CONV_HALO
    uh = zh_ref[hrow:BLK, CV0:CV0 + CONV_W] * _sigmoid(zh_ref[hrow:BLK, CG0:CG0 + CONV_W])
    uext[0:CONV_HALO, :] = jnp.where(first, 0.0, uh)
    uext[CONV_HALO:CONV_HALO + tm, :] = cv * sg
    _shifted_copies(uext, ush, tm)


def _conv_taps(cw_ref, pv_ref, uext, ush, out_ref, tm):
    for r0 in range(0, tm, CONV_CHUNK):
        acc = jnp.broadcast_to(pv_ref[R_CONV_B:R_CONV_B + 1, :], (CONV_CHUNK, CONV_W))
        for k in range(CONV_K):
            acc = acc + cw_ref[k:k + 1, :] * _tap(uext, ush, CONV_HALO - (CONV_K - 1) + k, r0, CONV_CHUNK)
        out_ref[r0:r0 + CONV_CHUNK, :] = acc


def _ln_silu(uc, pv_ref):
    mu = jnp.mean(uc, axis=-1, keepdims=True)
    xc = uc - mu
    rs = lax.rsqrt(jnp.mean(xc * xc, axis=-1, keepdims=True) + LN_EPS)
    xh = xc * rs
    ln = xh * pv_ref[R_LN_G:R_LN_G + 1, :] + pv_ref[R_LN_B:R_LN_B + 1, :]
    sg = _sigmoid(ln)
    return xh, rs, ln, sg


def _lru_gates(z_ref, zh_ref, pv_ref, wa_ref, wx_ref, rxext, first, tm):
    rxext[0:LRU_HALO, :] = jnp.where(first, 0.0, zh_ref[BLK - LRU_HALO:BLK, RX0:RX0 + LRU_W])
    rxext[LRU_HALO:LRU_HALO + tm, :] = z_ref[:, RX0:RX0 + LRU_W]
    xc = jnp.broadcast_to(pv_ref[R_LCONV_B:R_LCONV_B + 1, :], (tm, LRU_W))
    for k in range(LRU_K):
        xc = xc + pv_ref[R_LCW + k:R_LCW + k + 1, :] * rxext[pl.ds(LRU_HALO - (LRU_K - 1) + k, tm), :]
    r = _sigmoid(_dot(xc, wa_ref[...]) + pv_ref[R_BA:R_BA + 1, :])
    ig = _sigmoid(_dot(xc, wx_ref[...]) + pv_ref[R_BX:R_BX + 1, :])
    lam = pv_ref[R_LAM:R_LAM + 1, :]
    sp = jnp.log1p(jnp.exp(-lam))
    la = (-LRU_C * r) * sp
    a = jnp.exp(la)
    mult = jnp.sqrt(_neg_expm1(2.0 * la))
    return xc, r, ig, sp, la, a, mult


def _mixer_in_specs(tm, tile_of):
    hb = tm // BLK
    return [
        pl.BlockSpec((tm, IN_W), lambda i: (tile_of(i), 0)),
        pl.BlockSpec((BLK, IN_W), lambda i: (jnp.maximum(tile_of(i) * hb - 1, 0), 0)),
        _const_spec((8, 4 * BLK)),
        _const_spec((32, CONV_W)),
        _const_spec((16, CONV_W)),
        _const_spec((LRU_W, LRU_W)),
        _const_spec((LRU_W, LRU_W)),
    ]


def _mixer_fwd(z, sink, cw, pv, wa, wx, name, riders=()):
    t = z.shape[0]
    tm = _tile(t)
    nb = tm // BLK

    def body(z_ref, zh_ref, sink_ref, cw_ref, pv_ref, wa_ref, wx_ref, y_ref, hl_ref, uc_ref, p_ref, ps_ref,
             uext, ush, rxext, hcar):
        i = pl.program_id(0)
        first = i == 0

        @pl.when(first)
        def _():
            hcar[...] = jnp.zeros_like(hcar)

        lo = lax.broadcasted_iota(jnp.int32, (4 * BLK, BLK), 1) < HEAD_DIM
        for b in range(nb):
            rows = slice(b * BLK, (b + 1) * BLK)
            v2, prob, psink = _attn_block(z_ref, zh_ref, sink_ref, b, first)
            prob = prob.astype(MX)
            p_ref[b] = prob
            ps_ref[b] = jnp.where(lo, psink[0], psink[1])
            for g in range(2):
                o = _dot(prob[2 * g * BLK:(2 * g + 2) * BLK], v2[g])
                y_ref[rows, (2 * g) * BLK:(2 * g + 1) * BLK] = o[0:BLK]
                y_ref[rows, (2 * g + 1) * BLK:(2 * g + 2) * BLK] = o[BLK:2 * BLK]
        _glu_fill(z_ref, zh_ref, uext, ush, first, tm)
        _conv_taps(cw_ref, pv_ref, uext, ush, uc_ref, tm)
        _, _, ln, sg = _ln_silu(uc_ref[...], pv_ref)
        y_ref[:, ATTN_W:ATTN_W + CONV_W] = ln * sg
        xc, _, ig, _, _, a, mult = _lru_gates(z_ref, zh_ref, pv_ref, wa_ref, wx_ref, rxext, first, tm)
        acum, h = _scan(a, mult * (ig * xc), tm, reverse=False)
        h = h + acum * hcar[0:1, :]
        hl_ref[...] = h
        hcar[0:1, :] = h[tm - 1:tm, :]
        gl, _ = _gelu(z_ref[:, RG0:RG0 + LRU_W])
        y_ref[:, ATTN_W + CONV_W:ATTN_W + CONV_W + LRU_W] = h * gl

    tile = lambda w: pl.BlockSpec((tm, w), lambda i: (i, 0))
    return _call(
        body, name, (t // tm,), _mixer_in_specs(tm, lambda i: i),
        [tile(D_MODEL), tile(LRU_W), tile(CONV_W), pl.BlockSpec((nb, 4 * BLK, 4 * BLK), lambda i: (i, 0, 0)),
         pl.BlockSpec((nb, 4 * BLK, BLK), lambda i: (i, 0, 0))],
        [_sds((t, D_MODEL), F32), _sds((t, LRU_W), F32), _sds((t, CONV_W), F32),
         _sds((t // BLK, 4 * BLK, 4 * BLK), MX), _sds((t // BLK, 4 * BLK, BLK), F32)],
        [pltpu.VMEM((tm + CONV_HALO, CONV_W), F32), pltpu.VMEM((7, tm + CONV_HALO - 8, CONV_W), F32),
         pltpu.VMEM((tm + LRU_HALO, LRU_W), F32), pltpu.VMEM((8, LRU_W), F32)],
        [z, z, sink, cw, pv, wa, wx], riders)


def _mixer_bwd(dy, z, ycat, hl, uc, probs, psinks, sink, cw, pv, wa, wx, name, riders=()):
    t = z.shape[0]
    tm = _tile(t)
    nt = t // tm
    nb = tm // BLK
    rev = lambda i: nt - 1 - i

    def body(dy_ref, z_ref, zh_ref, sink_ref, cw_ref, pv_ref, wa_ref, wx_ref, y_ref, hl_ref, hlh_ref, uc_ref,
             p_ref, ps_ref, dz_ref, dsink_ref, dcw_ref, dpv_ref, dwa_ref, dwx_ref,
             uext, ush, sgs, rxext, dkext, dvext, ducext, dsh, dcw8, dxcext, kcar, vcar, uccar, xccar, gcar):
        i = pl.program_id(0)
        first = i == nt - 1

        @pl.when(i == 0)
        def _():
            for car in (kcar, vcar, uccar, xccar, gcar, dcw8):
                car[...] = jnp.zeros_like(car)
            for acc in (dsink_ref, dpv_ref, dwa_ref, dwx_ref):
                acc[...] = jnp.zeros_like(acc)

        def addrow(r, val):
            dpv_ref[r:r + 1, :] += jnp.sum(val, axis=0, keepdims=True)

        dkext[:, 0:tm] = jnp.zeros((KV_W, tm), F32)
        dvext[:, 0:tm] = jnp.zeros((KV_W, tm), F32)
        dkext[:, tm:tm + BLK] = kcar[...]
        dvext[:, tm:tm + BLK] = vcar[...]
        lane512 = lax.broadcasted_iota(jnp.int32, (1, 4 * BLK), 1) < 2 * BLK
        lo = lax.broadcasted_iota(jnp.int32, (4 * BLK, BLK), 1) < HEAD_DIM
        hd, w2 = HEAD_DIM, 2 * BLK
        for b in range(nb):
            rows = slice(b * BLK, (b + 1) * BLK)
            band = slice(b * BLK, (b + 2) * BLK)
            q2, k2, v2 = _attn_operands(z_ref, zh_ref, b)
            prob = p_ref[b]
            psink = [ps_ref[b, :, 0:1], ps_ref[b, :, HEAD_DIM:HEAD_DIM + 1]]
            stack = lambda ref: jnp.concatenate([ref[rows, p * BLK:(p + 1) * BLK] for p in range(4)], axis=0)
            do4 = stack(dy_ref)
            dlt = do4 * stack(y_ref)
            d0 = jnp.sum(jnp.where(lo, dlt, 0.0), axis=1, keepdims=True)
            d1 = jnp.sum(jnp.where(lo, 0.0, dlt), axis=1, keepdims=True)
            dp = jnp.concatenate([_dot_nt(do4[g * w2:(g + 1) * w2], v2[g]) for g in range(2)], axis=0)
            dl = jnp.concatenate([jnp.broadcast_to(d0, (4 * BLK, w2)), jnp.broadcast_to(d1, (4 * BLK, w2))], axis=1)
            draw = (prob * (dp - dl)) * SCALE
            e0, e1 = psink[0] * d0, psink[1] * d1
            for p in range(4):
                prs = slice(p * BLK, (p + 1) * BLK)
                s0 = jnp.sum(e0[prs], axis=0, keepdims=True)
                s1 = jnp.sum(e1[prs], axis=0, keepdims=True)
                dsink_ref[p:p + 1, :] += -jnp.where(lane512, s0, s1)
            for g in range(2):
                grs = slice(g * w2, (g + 1) * w2)
                dq = _dot(draw[grs], k2[g])
                dz_ref[rows, (2 * g) * BLK:(2 * g + 1) * BLK] = dq[0:BLK].astype(dz_ref.dtype)
                dz_ref[rows, (2 * g + 1) * BLK:(2 * g + 2) * BLK] = dq[BLK:2 * BLK].astype(dz_ref.dtype)
                tk = _dot_tn(q2[g], draw[grs])
                tv = _dot_tn(do4[grs], prob[grs])
                dkext[g * hd:(g + 1) * hd, band] += tk[0:hd, 0:w2] + tk[hd:2 * hd, w2:2 * w2]
                dvext[g * hd:(g + 1) * hd, band] += tv[0:hd, 0:w2] + tv[hd:2 * hd, w2:2 * w2]
        dz_ref[:, K0:K0 + KV_W] = jnp.transpose(dkext[:, BLK:BLK + tm]).astype(dz_ref.dtype)
        dz_ref[:, V0:V0 + KV_W] = jnp.transpose(dvext[:, BLK:BLK + tm]).astype(dz_ref.dtype)
        kcar[...] = dkext[:, 0:BLK]
        vcar[...] = dvext[:, 0:BLK]

        _glu_fill(z_ref, zh_ref, uext, ush, first, tm, sg_out=sgs)
        xh, rs, ln, sg = _ln_silu(uc_ref[...], pv_ref)
        dln = dy_ref[:, ATTN_W:ATTN_W + CONV_W] * (sg * (1.0 + ln * (1.0 - sg)))
        addrow(R_LN_G, dln * xh)
        addrow(R_LN_B, dln)
        dxh = dln * pv_ref[R_LN_G:R_LN_G + 1, :]
        duc = rs * (dxh - jnp.mean(dxh, axis=-1, keepdims=True) - xh * jnp.mean(dxh * xh, axis=-1, keepdims=True))
        addrow(R_CONV_B, duc)
        ducext[0:tm, :] = duc
        ducext[tm:tm + CONV_HALO, :] = uccar[...]
        uccar[...] = duc[0:CONV_HALO, :]
        _shifted_copies(ducext, dsh, tm)
        for r0 in range(0, tm, CONV_CHUNK):
            crow = slice(r0, r0 + CONV_CHUNK)
            duc_c = ducext[crow, :]
            du = jnp.zeros((CONV_CHUNK, CONV_W), F32)
            for k in range(CONV_K):
                prod = duc_c * _tap(uext, ush, CONV_HALO - (CONV_K - 1) + k, r0, CONV_CHUNK)
                part = prod[0:8]
                for s in range(8, CONV_CHUNK, 8):
                    part = part + prod[s:s + 8]
                dcw8[k] += part
                du = du + cw_ref[k:k + 1, :] * _tap(ducext, dsh, CONV_K - 1 - k, r0, CONV_CHUNK)
            sgc = sgs[crow, :]
            dz_ref[crow, CV0:CV0 + CONV_W] = (du * sgc).astype(dz_ref.dtype)
            u_c = uext[CONV_HALO + r0:CONV_HALO + r0 + CONV_CHUNK, :]
            dz_ref[crow, CG0:CG0 + CONV_W] = (du * u_c * (1.0 - sgc)).astype(dz_ref.dtype)

        @pl.when(i == nt - 1)
        def _():
            dcw_ref[...] = jnp.sum(dcw8[...], axis=1)

        xc, r, ig, sp, la, a, mult = _lru_gates(z_ref, zh_ref, pv_ref, wa_ref, wx_ref, rxext, first, tm)
        h = hl_ref[...]
        rowi = lax.broadcasted_iota(jnp.int32, (tm, LRU_W), 0)
        hlast = jnp.where(first, 0.0, hlh_ref[7:8, :])
        hprev = jnp.where(rowi == 0, hlast, pltpu.roll(h, 1, 0))
        dyl = dy_ref[:, ATTN_W + CONV_W:ATTN_W + CONV_W + LRU_W]
        gl, dgl = _gelu(z_ref[:, RG0:RG0 + LRU_W])
        dz_ref[:, RG0:RG0 + LRU_W] = (dyl * h * dgl).astype(dz_ref.dtype)
        dh = dyl * gl + jnp.where(rowi == tm - 1, gcar[0:1, :], 0.0)
        c = jnp.where(rowi == tm - 1, 0.0, pltpu.roll(a, tm - 1, 0))
        _, gg = _scan(c, dh, tm, reverse=True)
        gcar[0:1, :] = a[0:1, :] * gg[0:1, :]
        dmult = gg * (ig * xc)
        dig = gg * mult * xc
        dxc = gg * mult * ig
        dla = gg * hprev * a - dmult * a * a / mult
        dr = dla * (-LRU_C * sp)
        lam = pv_ref[R_LAM:R_LAM + 1, :]
        dpv_ref[R_LAM:R_LAM + 1, :] += jnp.sum(dla * (-LRU_C * r), axis=0, keepdims=True) * (-_sigmoid(-lam))
        dpa = dr * r * (1.0 - r)
        dpx = dig * ig * (1.0 - ig)
        addrow(R_BA, dpa)
        addrow(R_BX, dpx)
        dxc = dxc + _dot_nt(dpa, wa_ref[...]) + _dot_nt(dpx, wx_ref[...])
        dwa_ref[...] += _dot_tn(xc, dpa)
        dwx_ref[...] += _dot_tn(xc, dpx)
        addrow(R_LCONV_B, dxc)
        dxcext[0:tm, :] = dxc
        dxcext[tm:tm + LRU_HALO, :] = xccar[...]
        xccar[...] = dxc[0:LRU_HALO, :]
        drx = jnp.zeros((tm, LRU_W), F32)
        for k in range(LRU_K):
            addrow(R_LCW + k, dxc * rxext[pl.ds(LRU_HALO - (LRU_K - 1) + k, tm), :])
            drx = drx + pv_ref[R_LCW + k:R_LCW + k + 1, :] * dxcext[pl.ds(LRU_K - 1 - k, tm), :]
        dz_ref[:, RX0:RX0 + LRU_W] = drx.astype(dz_ref.dtype)

    tile = lambda w: pl.BlockSpec((tm, w), lambda i: (rev(i), 0))
    in_specs = [tile(D_MODEL)] + _mixer_in_specs(tm, rev) + [
        tile(D_MODEL), tile(LRU_W),
        pl.BlockSpec((8, LRU_W), lambda i: (jnp.maximum(rev(i) * (tm // 8) - 1, 0), 0)),
        tile(CONV_W), pl.BlockSpec((nb, 4 * BLK, 4 * BLK), lambda i: (rev(i), 0, 0)),
        pl.BlockSpec((nb, 4 * BLK, BLK), lambda i: (rev(i), 0, 0))]
    return _call(
        body, name, (nt,), in_specs,
        [tile(IN_W), _acc_spec((8, 4 * BLK)), _acc_spec((32, CONV_W)), _acc_spec((16, CONV_W)),
         _acc_spec((LRU_W, LRU_W)), _acc_spec((LRU_W, LRU_W))],
        [_sds((t, IN_W), MX), _sds((8, 4 * BLK), F32), _sds((32, CONV_W), F32), _sds((16, CONV_W), F32),
         _sds((LRU_W, LRU_W), F32), _sds((LRU_W, LRU_W), F32)],
        [pltpu.VMEM((tm + CONV_HALO, CONV_W), F32), pltpu.VMEM((7, tm + CONV_HALO - 8, CONV_W), F32),
         pltpu.VMEM((tm, CONV_W), F32), pltpu.VMEM((tm + LRU_HALO, LRU_W), F32),
         pltpu.VMEM((KV_W, tm + BLK), F32), pltpu.VMEM((KV_W, tm + BLK), F32),
         pltpu.VMEM((tm + CONV_HALO, CONV_W), F32), pltpu.VMEM((7, tm + CONV_HALO - 8, CONV_W), F32),
         pltpu.VMEM((32, 8, CONV_W), F32), pltpu.VMEM((tm + LRU_HALO, LRU_W), F32),
         pltpu.VMEM((KV_W, BLK), F32), pltpu.VMEM((KV_W, BLK), F32),
         pltpu.VMEM((CONV_HALO, CONV_W), F32), pltpu.VMEM((LRU_HALO, LRU_W), F32), pltpu.VMEM((8, LRU_W), F32)],
        [dy, z, z, sink, cw, pv, wa, wx, ycat, hl, hl, uc, probs, psinks], riders)


def _post_fwd(ycat, h0, gmix, w_out, g2, w_up, w_down, name, riders=()):
    t = h0.shape[0]
    tm = _tile(t, POST_TILE)
    nj = D_FF // FF_BLK

    def body(y_ref, h_ref, gm_ref, wo_ref, g2_ref, wu_ref, wd_ref, h1_ref, a_ref, h2_ref, ym_ref, hn_ref):
        ym, _, _ = _group_rms_fwd(y_ref[...], gm_ref[...])
        ym = ym.astype(MX)
        ym_ref[...] = ym
        h1 = h_ref[...] + jnp.dot(ym, wo_ref[...], preferred_element_type=F32)
        h1_ref[...] = h1
        hn, _, _ = _rms_fwd(h1, g2_ref[...])
        hn = hn.astype(MX)
        hn_ref[...] = hn
        for j in range(nj):
            u = jnp.dot(hn, wu_ref[j], preferred_element_type=F32)
            a_ref[:, j * FF_BLK:(j + 1) * FF_BLK] = jnp.square(jnp.maximum(u, 0.0)).astype(MX)
        h2_ref[...] = h1 + jnp.dot(a_ref[...], wd_ref[...], preferred_element_type=F32)

    tile = lambda w: pl.BlockSpec((tm, w), lambda i: (i, 0))
    return _call(
        body, name, (t // tm,),
        [tile(D_MODEL), tile(D_MODEL), _const_spec((1, D_MODEL)), _const_spec((D_MODEL, D_MODEL)),
         _const_spec((1, D_MODEL)), _const_spec((nj, D_MODEL, FF_BLK)), _const_spec((D_FF, D_MODEL))],
        [tile(D_MODEL), tile(D_FF), tile(D_MODEL), tile(D_MODEL), tile(D_MODEL)],
        [_sds((t, D_MODEL), F32), _sds((t, D_FF), MX), _sds((t, D_MODEL), F32), _sds((t, D_MODEL), MX),
         _sds((t, D_MODEL), MX)],
        [], [ycat, h0, gmix, w_out, g2, w_up, w_down], riders)


def _ffn_bwd(dh2, act, h1, g2, w_up_t, w_down, name, riders=()):
    t = h1.shape[0]
    tm = _tile(t, POST_TILE)
    nj = D_FF // FF_BLK

    def body(dh2_ref, a_ref, h1_ref, g2_ref, wut_ref, wd_ref, dh1_ref, dh1b_ref, dh2b_ref, du_ref, dg2_ref):
        @pl.when(pl.program_id(0) == 0)
        def _():
            dg2_ref[...] = jnp.zeros_like(dg2_ref)

        dh2 = dh2_ref[...]
        dh2b = dh2.astype(MX)
        dh2b_ref[...] = dh2b
        for j in range(nj):
            cols = slice(j * FF_BLK, (j + 1) * FF_BLK)
            da = _dot_nt(dh2b, wd_ref[j])
            du_ref[:, cols] = (da * (2.0 * jnp.sqrt(a_ref[:, cols].astype(F32)))).astype(MX)
        dhn = jnp.dot(du_ref[...], wut_ref[...], preferred_element_type=F32)
        _, xh, r = _rms_fwd(h1_ref[...], g2_ref[...])
        dx, dg = _rms_bwd(dhn, xh, r, g2_ref[...])
        dg2_ref[...] += dg
        dh1 = dh2 + dx
        dh1_ref[...] = dh1
        dh1b_ref[...] = dh1.astype(MX)

    tile = lambda w: pl.BlockSpec((tm, w), lambda i: (i, 0))
    return _call(
        body, name, (t // tm,),
        [tile(D_MODEL), tile(D_FF), tile(D_MODEL), _const_spec((1, D_MODEL)),
         _const_spec((D_FF, D_MODEL)), _const_spec((nj, FF_BLK, D_MODEL))],
        [tile(D_MODEL), tile(D_MODEL), tile(D_MODEL), tile(D_FF), _acc_spec((1, D_MODEL))],
        [_sds((t, D_MODEL), F32), _sds((t, D_MODEL), MX), _sds((t, D_MODEL), MX), _sds((t, D_FF), MX),
         _sds((1, D_MODEL), F32)],
        [], [dh2, act, h1, g2, w_up_t, w_down], riders)


def _mix_bwd(dh1, ycat, ym, gmix, w_out, name):
    t = dh1.shape[0]
    tm = _tile(t)
    nk = t // tm
    r = D_MODEL // N_DEV

    def body(dh1_ref, y_ref, ym_ref, gm_ref, wo_ref, dy_ref, dgm_ref, o_ref, o16_ref, acc):
        k = pl.program_id(0)

        @pl.when(k == 0)
        def _():
            dgm_ref[...] = jnp.zeros_like(dgm_ref)
            acc[...] = jnp.zeros_like(acc)

        dh = dh1_ref[...]
        acc[...] += _dot_tn(ym_ref[...], dh)
        dym = _dot_nt(dh, wo_ref[...])
        gm = gm_ref[...]
        _, yh, rr = _group_rms_fwd(y_ref[...], gm)
        outs, dgs = [], []
        for (a, b), rg in zip(_GROUPS, rr):
            dxg, dgg = _rms_bwd(dym[:, a:b], yh[:, a:b], rg, gm[:, a:b])
            outs.append(dxg)
            dgs.append(dgg)
        dy_ref[...] = jnp.concatenate(outs, axis=1)
        dgm_ref[...] += jnp.concatenate(dgs, axis=1)

        @pl.when(k == nk - 1)
        def _():
            for d in range(N_DEV):
                v = acc[d * r:(d + 1) * r, :]
                o_ref[d] = v
                o16_ref[d] = v.astype(o16_ref.dtype)

    tile = pl.BlockSpec((tm, D_MODEL), lambda i: (i, 0))
    slabs = _const_spec((N_DEV, r, D_MODEL))
    (dy, dgm, dw, dw16), _ = _call(
        body, name, (nk,), [tile, tile, tile, _const_spec((1, D_MODEL)), _const_spec((D_MODEL, D_MODEL))],
        [tile, _acc_spec((1, D_MODEL)), slabs, slabs],
        [_sds((t, D_MODEL), F32), _sds((1, D_MODEL), F32), _sds((N_DEV, r, D_MODEL), F32),
         _sds((N_DEV, r, D_MODEL), WIRE)],
        [pltpu.VMEM((D_MODEL, D_MODEL), F32)], [dh1, ycat, ym, gmix, w_out])
    return dy, dgm, (dw, dw16)


def _in_bwd(dz, h0, dh1, g1, w_in_t, after, name):
    t = h0.shape[0]
    tm = _tile(t, STREAM_TILE)

    def body(dz_ref, h_ref, dh1_ref, g_ref, w_ref, after_ref, dh0_ref, dg_ref):
        @pl.when(pl.program_id(0) == 0)
        def _():
            dg_ref[...] = jnp.zeros_like(dg_ref)

        dhn = _dot(dz_ref[...], w_ref[...])
        _, xh, r = _rms_fwd(h_ref[...], g_ref[...])
        dx, dg = _rms_bwd(dhn, xh, r, g_ref[...])
        dg_ref[...] += dg
        dh0_ref[...] = dh1_ref[...] + dx

    tile = lambda w: pl.BlockSpec((tm, w), lambda i: (i, 0))
    (dh0, dg), _ = _call(
        body, name, (t // tm,),
        [tile(IN_W), tile(D_MODEL), tile(D_MODEL), _const_spec((1, D_MODEL)), _const_spec((IN_W, D_MODEL)),
         _const_spec((8, 128))],
        [tile(D_MODEL), _acc_spec((1, D_MODEL))], [_sds((t, D_MODEL), F32), _sds((1, D_MODEL), F32)],
        [], [dz, h0, dh1, g1, w_in_t, after])
    return dh0, dg


def _in_bwd_dw(dz, h0, dh1, g1, w_in_t, name):
    t = h0.shape[0]
    tm = _tile(t)
    nk = t // tm

    def body(dz_ref, h_ref, dh1_ref, g_ref, w_ref, dh0_ref, dg_ref, o_ref, o16_ref, acc):
        k = pl.program_id(0)

        @pl.when(k == 0)
        def _():
            dg_ref[...] = jnp.zeros_like(dg_ref)
            acc[...] = jnp.zeros_like(acc)

        dz_t = dz_ref[...]
        hn, xh, r = _rms_fwd(h_ref[...], g_ref[...])
        acc[...] += _dot_tn(dz_t, hn)
        dhn = _dot(dz_t, w_ref[...])
        dx, dg = _rms_bwd(dhn, xh, r, g_ref[...])
        dg_ref[...] += dg
        dh0_ref[...] = dh1_ref[...] + dx

        @pl.when(k == nk - 1)
        def _():
            for d in range(N_DEV):
                v = acc[d * IN_SHARD:(d + 1) * IN_SHARD, :]
                o_ref[d] = v
                o16_ref[d] = v.astype(o16_ref.dtype)

    tile = lambda w: pl.BlockSpec((tm, w), lambda i: (i, 0))
    slabs = _const_spec((N_DEV, IN_SHARD, D_MODEL))
    (dh0, dg, dw, dw16), _ = _call(
        body, name, (nk,),
        [tile(IN_W), tile(D_MODEL), tile(D_MODEL), _const_spec((1, D_MODEL)), _const_spec((IN_W, D_MODEL))],
        [tile(D_MODEL), _acc_spec((1, D_MODEL)), slabs, slabs],
        [_sds((t, D_MODEL), F32), _sds((1, D_MODEL), F32), _sds((N_DEV, IN_SHARD, D_MODEL), F32),
         _sds((N_DEV, IN_SHARD, D_MODEL), WIRE)],
        [pltpu.VMEM((IN_W, D_MODEL), F32)], [dz, h0, dh1, g1, w_in_t])
    return dh0, dg, (dw, dw16)


def _loss_head(h, gf, target, name):
    t = h.shape[0]
    tm = _tile(t, STREAM_TILE)

    def body(h_ref, g_ref, t_ref, dh_ref, loss_ref, dg_ref):
        @pl.when(pl.program_id(0) == 0)
        def _():
            loss_ref[...] = jnp.zeros_like(loss_ref)
            dg_ref[...] = jnp.zeros_like(dg_ref)

        g = g_ref[...]
        y, xh, r = _rms_fwd(h_ref[...], g)
        err = y - t_ref[...]
        part = 0.5 * jnp.sum(jnp.mean(err * err, axis=-1, keepdims=True), axis=0, keepdims=True)
        loss_ref[...] += jnp.broadcast_to(part, loss_ref.shape)
        dx, dg = _rms_bwd(err * (1.0 / D_MODEL), xh, r, g)
        dg_ref[...] += dg
        dh_ref[...] = dx

    tile = pl.BlockSpec((tm, D_MODEL), lambda i: (i, 0))
    (dh, loss, dg), _ = _call(
        body, name, (t // tm,), [tile, _const_spec((1, D_MODEL)), tile],
        [tile, _acc_spec((1, 128)), _acc_spec((1, D_MODEL))],
        [_sds((t, D_MODEL), F32), _sds((1, 128), F32), _sds((1, D_MODEL), F32)], [], [h, gf, target])
    return dh, loss, dg


def _dw(x, y, name, split, bm, bn):
    t, m = x.shape
    n = y.shape[1]
    tk = _tile(t, DW_TILE)
    nk = t // tk
    if split == "rows":
        assert bn == n
        r, c = m // N_DEV, n
        per = bm // r
        out_block = pl.BlockSpec((per, r, c), lambda a, b, k: (a, 0, 0))
    else:
        assert bm == m
        r, c = m, n // N_DEV
        per = bn // c
        out_block = pl.BlockSpec((per, r, c), lambda a, b, k: (b, 0, 0))

    def body(x_ref, y_ref, o_ref, o16_ref, acc):
        k = pl.program_id(2)

        @pl.when(k == 0)
        def _():
            acc[...] = jnp.zeros_like(acc)

        acc[...] += _dot_tn(x_ref[...], y_ref[...])

        @pl.when(k == nk - 1)
        def _():
            for d in range(per):
                v = acc[d * r:(d + 1) * r, :] if split == "rows" else acc[:, d * c:(d + 1) * c]
                o_ref[d] = v
                o16_ref[d] = v.astype(o16_ref.dtype)

    return pl.pallas_call(
        body, name=name, grid=(m // bm, n // bn, nk),
        in_specs=[pl.BlockSpec((tk, bm), lambda a, b, k: (k, a)), pl.BlockSpec((tk, bn), lambda a, b, k: (k, b))],
        out_specs=[out_block, out_block],
        out_shape=[_sds((N_DEV, r, c), F32), _sds((N_DEV, r, c), WIRE)],
        scratch_shapes=[pltpu.VMEM((bm, bn), F32)],
        compiler_params=pltpu.CompilerParams(dimension_semantics=("arbitrary",) * 3, vmem_limit_bytes=VMEM_LIMIT),
    )(x, y)


def _adamw_math(w, g, m, v):
    m = ADAM_B1 * m + (1.0 - ADAM_B1) * g
    v = ADAM_B2 * v + (1.0 - ADAM_B2) * jnp.square(g)
    m_hat = m / (1.0 - ADAM_B1 ** ADAM_STEP)
    v_hat = v / (1.0 - ADAM_B2 ** ADAM_STEP)
    delta = -ADAM_LR * (m_hat / (jnp.sqrt(v_hat) + ADAM_EPS) + ADAM_WD * w)
    return delta, m, v


def _adamw_shard(g_own, g_recv, dev, w, m, v, after, name):
    _, r, c = w.shape
    br = r
    for cand in (256, 128, 112, 64, 56, 32, 16, 8):
        if r % cand == 0:
            br = cand
            break
    nr = r // br
    own = lambda l: pl.BlockSpec((1, br, c), lambda ll, i, d: (d[0], jnp.where(ll == l, i, (nr - 1) * (1 - l)), 0))
    recv = lambda l: pl.BlockSpec((N_DEV - 1, br, c), lambda ll, i, d: (0, jnp.where(ll == l, i, (nr - 1) * (1 - l)), 0))

    def body(dev_ref, go0, gr0, go1, gr1, w_ref, m_ref, v_ref, after_ref, g_out, d_out, m_out, v_out):
        def update(go_ref, gr_ref):
            g = go_ref[0]
            for j in range(N_DEV - 1):
                g = g + gr_ref[j].astype(F32)
            delta, mn, vn = _adamw_math(w_ref[0], g, m_ref[0], v_ref[0])
            g_out[0] = g
            d_out[0] = delta
            m_out[0] = mn
            v_out[0] = vn

        layer = pl.program_id(0)
        pl.when(layer == 0)(lambda: update(go0, gr0))
        pl.when(layer == 1)(lambda: update(go1, gr1))

    tile = pl.BlockSpec((1, br, c), lambda ll, i, d: (ll, i, 0))
    return pl.pallas_call(
        body, name=name,
        grid_spec=pltpu.PrefetchScalarGridSpec(
            num_scalar_prefetch=1, grid=(2, nr),
            in_specs=[own(0), recv(0), own(1), recv(1), tile, tile, tile,
                      pl.BlockSpec((8, 128), lambda ll, i, d: (0, 0))],
            out_specs=[tile, tile, tile, tile]),
        out_shape=[_sds((2, r, c), F32)] * 4,
        compiler_params=pltpu.CompilerParams(dimension_semantics=("arbitrary",) * 2, vmem_limit_bytes=VMEM_LIMIT),
    )(dev, g_own[0], g_recv[0], g_own[1], g_recv[1], w, m, v, after)


def _adamw_small(gs, ws, ms, vs, name):
    n = len(gs)

    def body(*refs):
        g_refs, w_refs, m_refs, v_refs = (refs[k * n:(k + 1) * n] for k in range(4))
        outs = refs[4 * n:]
        for k in range(n):
            delta, mn, vn = _adamw_math(w_refs[k][...], g_refs[k][...], m_refs[k][...], v_refs[k][...])
            outs[k][...] = delta
            outs[n + k][...] = mn
            outs[2 * n + k][...] = vn

    shapes = [_sds(w.shape, F32) for w in ws]
    res = pl.pallas_call(body, name=name, out_shape=shapes * 3,
                         compiler_params=pltpu.CompilerParams(vmem_limit_bytes=VMEM_LIMIT))(*gs, *ws, *ms, *vs)
    return res[:n], res[n:2 * n], res[2 * n:]


def _sum_parts(own, recv, dev, name):
    def body(dev_ref, own_ref, recv_ref, o_ref):
        me = dev_ref[0]

        def block(d):
            f = jnp.bitwise_xor(me, d)
            return jnp.where(f == 0, own_ref[...], recv_ref[jnp.maximum(f - 1, 0)])

        g = block(0)
        for d in range(1, N_DEV):
            g = g + block(d)
        o_ref[...] = g

    return pl.pallas_call(
        body, name=name,
        grid_spec=pltpu.PrefetchScalarGridSpec(
            num_scalar_prefetch=1, grid=(1,),
            in_specs=[pl.BlockSpec(own.shape, lambda i, d: (0, 0)), pl.BlockSpec(recv.shape, lambda i, d: (0, 0, 0))],
            out_specs=pl.BlockSpec(own.shape, lambda i, d: (0, 0))),
        out_shape=_sds(own.shape, F32))(dev, own, recv)


HBM = pl.BlockSpec(memory_space=pltpu.HBM)
SEM = pl.BlockSpec(memory_space=pltpu.SEMAPHORE)
EFFECT = pltpu.SideEffectType.DATAFLOW_SIDE_EFFECTING


def _direct_copies(srcs, lands, ssem, rsem, scatter):
    x, y, c = _me()
    out = []
    for a in range(len(srcs)):
        for f in range(1, N_DEV):
            px = 1 - x if f & 4 else x
            py = 1 - y if f & 2 else y
            pc = 1 - c if f & 1 else c
            out.append(pltpu.make_async_remote_copy(
                src_ref=srcs[a].at[4 * px + 2 * py + pc] if scatter else srcs[a], dst_ref=lands[a].at[f - 1],
                send_sem=ssem.at[7 * a + f - 1], recv_sem=rsem.at[7 * a + f - 1],
                device_id=(px, py, pc), device_id_type=MESH))
    return out


def _send_start(arrays, scatter, name):
    arrays = list(arrays)
    n = len(arrays)
    lands = [lax.empty((N_DEV - 1,) + (a.shape[1:] if scatter else a.shape), a.dtype) for a in arrays]

    def body(*refs):
        srcs, lnds, ssem, rsem, token = refs[:n], refs[n:2 * n], refs[2 * n], refs[2 * n + 1], refs[-1]
        for cp in _direct_copies(srcs, lnds, ssem, rsem, scatter):
            cp.start()
        token[...] = jnp.zeros_like(token)

    hbm = lambda a: pltpu.HBM(a.shape, a.dtype)
    res = pl.pallas_call(
        body, name=name,
        out_shape=(pltpu.SemaphoreType.DMA((7 * n,)), pltpu.SemaphoreType.DMA((7 * n,)),
                   *[hbm(a) for a in arrays + lands], _sds((8, 128), F32)),
        in_specs=[HBM] * (2 * n),
        out_specs=(SEM, SEM, *[HBM] * (2 * n), pl.BlockSpec(memory_space=pltpu.VMEM)),
        input_output_aliases={i: 2 + i for i in range(2 * n)},
        compiler_params=pltpu.CompilerParams(has_side_effects=EFFECT),
    )(*[pltpu.with_memory_space_constraint(a, pltpu.HBM) for a in arrays + lands])
    return types.SimpleNamespace(ssem=res[0], rsem=res[1], srcs=list(res[2:2 + n]), lands=list(res[2 + n:2 + 2 * n]),
                                 token=res[-1], scatter=scatter)


def _send_wait(h, after, name):
    n = len(h.srcs)

    def body(*refs):
        srcs, lnds, ssem, rsem = refs[:n], refs[n:2 * n], refs[2 * n], refs[2 * n + 1]
        for cp in _direct_copies(srcs, lnds, ssem, rsem, h.scatter):
            cp.wait_send()
            cp.wait_recv()

    hbm = lambda a: pltpu.HBM(a.shape, a.dtype)
    res = pl.pallas_call(
        body, name=name,
        out_shape=tuple(hbm(a) for a in h.srcs + h.lands),
        in_specs=[HBM] * (2 * n) + [SEM, SEM, ANY], out_specs=[HBM] * (2 * n),
        input_output_aliases={i: i for i in range(2 * n)},
        compiler_params=pltpu.CompilerParams(has_side_effects=EFFECT),
    )(*h.srcs, *h.lands, h.ssem, h.rsem, after)
    return list(res[:n]), list(res[n:])


def _block_diag(w):
    out = jnp.zeros((LRU_W, LRU_W), w.dtype)
    for h in range(4):
        out = lax.dynamic_update_slice(out, w[h], (h * 64, h * 64))
    return out


def _unblock_diag(w):
    return jnp.concatenate([w[h * 64:(h + 1) * 64, h * 64:(h + 1) * 64] for h in range(4)], axis=0)


def _layer_params(p, l):
    row = lambda a: a[l].reshape(1, -1)
    sink_rows = jnp.repeat(p["attn_sinks"][l].reshape(4, 2), 2 * BLK, axis=1)
    sink_rows = jnp.concatenate([sink_rows, jnp.zeros((4, 4 * BLK), F32)], axis=0)
    cw = jnp.concatenate([p["conv_dw_w"][l], jnp.zeros((1, CONV_W), F32)], axis=0)
    pv = jnp.concatenate([
        row(p["conv_dw_b"]), row(p["conv_ln_g"]), row(p["conv_ln_b"]), row(p["lru_conv_b"]), row(p["lru_ba"]),
        row(p["lru_bx"]), row(p["lru_lambda"]), jnp.zeros((1, LRU_W), F32), p["lru_conv_w"][l],
        jnp.zeros((4, LRU_W), F32)], axis=0)
    return dict(
        g1=row(p["norm1"]), sink=sink_rows, cw=cw, pv=pv,
        wa=_block_diag(p["lru_wa"][l]).astype(MX), wx=_block_diag(p["lru_wx"][l]).astype(MX),
        gmix=row(p["mix_norm"]), g2=row(p["norm2"]))


_SMALL = ["norm1", "attn_sinks", "conv_dw_w", "conv_dw_b", "conv_ln_g", "conv_ln_b", "lru_conv_w", "lru_conv_b",
          "lru_wa", "lru_ba", "lru_wx", "lru_bx", "lru_lambda", "mix_norm", "norm2"]
_BIG = ["w_in", "w_out", "w_up", "w_down"]
_WEIGHTS = ["norm1", "w_in", "attn_sinks", "conv_dw_w", "conv_dw_b", "conv_ln_g", "conv_ln_b", "lru_conv_w",
            "lru_conv_b", "lru_wa", "lru_ba", "lru_wx", "lru_bx", "lru_lambda", "mix_norm", "w_out", "norm2", "w_up",
            "w_down", "final_norm"]


def kernel(x, norm1, w_in, attn_sinks, conv_dw_w, conv_dw_b, conv_ln_g, conv_ln_b, lru_conv_w, lru_conv_b, lru_wa, lru_ba, lru_wx, lru_bx, lru_lambda, mix_norm, w_out, norm2, w_up, w_down, final_norm, loss_target, m_norm1, m_w_in, m_attn_sinks, m_conv_dw_w, m_conv_dw_b, m_conv_ln_g, m_conv_ln_b, m_lru_conv_w, m_lru_conv_b, m_lru_wa, m_lru_ba, m_lru_wx, m_lru_bx, m_lru_lambda, m_mix_norm, m_w_out, m_norm2, m_w_up, m_w_down, m_final_norm, v_norm1, v_w_in, v_attn_sinks, v_conv_dw_w, v_conv_dw_b, v_conv_ln_g, v_conv_ln_b, v_lru_conv_w, v_lru_conv_b, v_lru_wa, v_lru_ba, v_lru_wx, v_lru_bx, v_lru_lambda, v_mix_norm, v_w_out, v_norm2, v_w_up, v_w_down, v_final_norm):
    w = dict(norm1=norm1, w_in=w_in, attn_sinks=attn_sinks, conv_dw_w=conv_dw_w, conv_dw_b=conv_dw_b,
             conv_ln_g=conv_ln_g, conv_ln_b=conv_ln_b, lru_conv_w=lru_conv_w, lru_conv_b=lru_conv_b, lru_wa=lru_wa,
             lru_ba=lru_ba, lru_wx=lru_wx, lru_bx=lru_bx, lru_lambda=lru_lambda, mix_norm=mix_norm, w_out=w_out,
             norm2=norm2, w_up=w_up, w_down=w_down, final_norm=final_norm)
    m = dict(norm1=m_norm1, w_in=m_w_in, attn_sinks=m_attn_sinks, conv_dw_w=m_conv_dw_w, conv_dw_b=m_conv_dw_b,
             conv_ln_g=m_conv_ln_g, conv_ln_b=m_conv_ln_b, lru_conv_w=m_lru_conv_w, lru_conv_b=m_lru_conv_b,
             lru_wa=m_lru_wa, lru_ba=m_lru_ba, lru_wx=m_lru_wx, lru_bx=m_lru_bx, lru_lambda=m_lru_lambda,
             mix_norm=m_mix_norm, w_out=m_w_out, norm2=m_norm2, w_up=m_w_up, w_down=m_w_down, final_norm=m_final_norm)
    v = dict(norm1=v_norm1, w_in=v_w_in, attn_sinks=v_attn_sinks, conv_dw_w=v_conv_dw_w, conv_dw_b=v_conv_dw_b,
             conv_ln_g=v_conv_ln_g, conv_ln_b=v_conv_ln_b, lru_conv_w=v_lru_conv_w, lru_conv_b=v_lru_conv_b,
             lru_wa=v_lru_wa, lru_ba=v_lru_ba, lru_wx=v_lru_wx, lru_bx=v_lru_bx, lru_lambda=v_lru_lambda,
             mix_norm=v_mix_norm, w_out=v_w_out, norm2=v_norm2, w_up=v_w_up, w_down=v_w_down, final_norm=v_final_norm)
    depth = w_in.shape[0]
    xi, yi, ci = _me()
    dev = (4 * xi + 2 * yi + ci).astype(jnp.int32)
    dev1 = dev.reshape(1)
    tr = lambda a: jnp.swapaxes(a, 1, 2)
    w_t, m_t, v_t = tr(w_in), tr(m_w_in), tr(v_w_in)
    wb = {n: w[n].astype(MX) for n in _BIG if n != "w_in"}
    wb["w_in"] = w_t.astype(MX)
    layer_shards = lambda l: [wb["w_out"][l], wb["w_up"][l], wb["w_down"][l]]

    _, ((g_in0, g_cw, g_lcw),) = _call(None, "gather_first", None, [], [], [], [], [],
                                        [_gather_rider([wb["w_in"][0], conv_dw_w, lru_conv_w])])
    cols = lambda g: jnp.moveaxis(g, 0, -2).reshape(g.shape[1:-1] + (N_DEV * g.shape[-1],))
    p = dict(w)
    p["conv_dw_w"] = cols(g_cw)
    p["lru_conv_w"] = cols(g_lcw)
    lp = [_layer_params(p, l) for l in range(depth)]

    gathered = [dict(w_in=g_in0.reshape(IN_W, D_MODEL)), dict()]
    saved = []
    h = x[0]
    for l in range(depth):
        q, gw = lp[l], gathered[l]
        z, hn1 = _ln_in(h, q["g1"], gw["w_in"], l == 0, f"ln_in{l}")
        riders = [_gather_rider(layer_shards(0))] if l == 0 else []
        (ycat, hl, uc, probs, psinks), got = _mixer_fwd(z, q["sink"], q["cw"], q["pv"], q["wa"], q["wx"],
                                                        f"mixer_fwd{l}", riders)
        if l == 0:
            gw["w_out"], gw["w_up"], gw["w_down"] = got[0]
            gw["w_out"] = gw["w_out"].reshape(D_MODEL, D_MODEL)
        riders = [_gather_rider([wb["w_in"][1]] + layer_shards(1))] if l == 0 else []
        (h1, act, h2, ym, hn2), got = _post_fwd(ycat, h, q["gmix"], gw["w_out"], q["g2"], gw["w_up"],
                                                gw["w_down"].reshape(D_FF, D_MODEL), f"post_fwd{l}", riders)
        if l == 0:
            nxt = gathered[1]
            nxt["w_in"], nxt["w_out"], nxt["w_up"], nxt["w_down"] = got[0]
            nxt["w_in"] = nxt["w_in"].reshape(IN_W, D_MODEL)
            nxt["w_out"] = nxt["w_out"].reshape(D_MODEL, D_MODEL)
        saved.append(dict(h0=h, z=z, hn1=hn1, ycat=ycat, hl=hl, uc=uc, probs=probs, psinks=psinks, h1=h1, act=act,
                          ym=ym, hn2=hn2))
        h = h2
    dh, loss, dgf = _loss_head(h, final_norm.reshape(1, -1), loss_target[0], "loss_head")

    grads = [None] * depth
    big = {n: [None] * depth for n in _BIG}
    pending = []

    def send_pending():
        riders = [_scatter_rider([item[3] for item in pending])] if pending else []
        return riders, list(pending)

    def record(sent, got):
        for item, recv in zip(sent, got[0] if sent else []):
            big[item[0]][item[1]] = (item[2], recv)
        del pending[:len(sent)]

    for l in reversed(range(depth)):
        q, s, gw = lp[l], saved[l], gathered[l]
        riders, sent = send_pending()
        w_up_t = jnp.swapaxes(gw["w_up"], 1, 2).reshape(D_FF, D_MODEL)
        (dh1, dh1b, dhb, du, dg2), got = _ffn_bwd(dh, s["act"], s["h1"], q["g2"], w_up_t, gw["w_down"],
                                                  f"ffn_bwd{l}", riders)
        record(sent, got)
        dycat, dgm, d_wout = _mix_bwd(dh1b, s["ycat"], s["ym"], q["gmix"], gw["w_out"], f"mix_bwd{l}")
        pending.append(("w_down", l) + tuple(_dw(s["act"], dhb, f"dw_down{l}", "rows", 2048, D_MODEL)))
        pending.append(("w_up", l) + tuple(_dw(s["hn2"], du, f"dw_up{l}", "cols", D_MODEL, 2048)))
        pending.append(("w_out", l) + tuple(d_wout))
        riders, sent = send_pending()
        (dz, dsink, dcw, dpv, dwa, dwx), got = _mixer_bwd(
            dycat, s["z"], s["ycat"], s["hl"], s["uc"], s["probs"], s["psinks"], q["sink"], q["cw"], q["pv"], q["wa"],
            q["wx"], f"mixer_bwd{l}", riders)
        record(sent, got)
        if l > 0:
            dh, dg1, d_win = _in_bwd_dw(dz, s["h0"], dh1, q["g1"], gw["w_in"], f"in_bwd{l}")
            pending.append(("w_in", l) + tuple(d_win))
        else:
            d_win = _dw(dz, s["hn1"], f"dw_in{l}", "rows", IN_W, D_MODEL)
            win_sends = _send_start([d_win[1]], True, "scatter_w_in0_start")
            dh, dg1 = _in_bwd(dz, s["h0"], dh1, q["g1"], gw["w_in"], win_sends.token, f"in_bwd{l}")
        grads[l] = dict(
            norm1=dg1[0], attn_sinks=jnp.stack([dsink[0:4, 0], dsink[0:4, 2 * BLK]], axis=1).reshape(8),
            conv_dw_w=dcw[0:CONV_K], conv_dw_b=dpv[R_CONV_B], conv_ln_g=dpv[R_LN_G], conv_ln_b=dpv[R_LN_B],
            lru_conv_w=dpv[R_LCW:R_LCW + LRU_K], lru_conv_b=dpv[R_LCONV_B], lru_wa=_unblock_diag(dwa),
            lru_ba=dpv[R_BA].reshape(4, 64), lru_wx=_unblock_diag(dwx), lru_bx=dpv[R_BX].reshape(4, 64),
            lru_lambda=dpv[R_LAM], mix_norm=dgm[0], norm2=dg2[0])

    small = [jnp.stack([grads[l][n] for l in range(depth)]) for n in _SMALL] + [dgf, loss[:, 0:1]]

    def as_rows(a):
        flat = a.reshape(-1)
        pad = (-flat.size) % 1024
        if pad:
            flat = jnp.concatenate([flat, jnp.zeros((pad,), F32)])
        return flat.reshape(-1, 128)

    pieces = [as_rows(a) for a in small]
    packed = jnp.concatenate(pieces, axis=0)
    small_sends = _send_start([packed], False, "bcast_small_start")

    out = {}
    shard_update = lambda n, wmv, after: list(_adamw_shard(
        [big[n][l][0] for l in range(depth)], [big[n][l][1] for l in range(depth)], dev1, *wmv, after, f"adamw_{n}"))
    for n in ("w_out", "w_up", "w_down"):
        out[n] = shard_update(n, (w[n], m[n], v[n]), small_sends.token)
    _, (win_recv,) = _send_wait(win_sends, out["w_down"][1], "scatter_w_in0_wait")
    big["w_in"][0] = (d_win[0], win_recv)
    (packed,), (small_recv,) = _send_wait(small_sends, win_recv, "bcast_small_wait")
    out["w_in"] = [tr(a) for a in shard_update("w_in", (w_t, m_t, v_t), jnp.zeros((8, 128), F32))]
    summed = _sum_parts(packed, small_recv, dev1, "sum_small_grads")
    small_sums, row = [], 0
    for a, piece in zip(small, pieces):
        got = summed[row:row + piece.shape[0]]
        small_sums.append(got.reshape(a.shape) if a.size == piece.size else got.reshape(-1)[:a.size].reshape(a.shape))
        row += piece.shape[0]
    shard = lambda a: lax.dynamic_slice_in_dim(a, dev * (a.shape[-1] // N_DEV), a.shape[-1] // N_DEV, axis=a.ndim - 1)
    flat = {"lru_wa": (depth, LRU_W, 64), "lru_wx": (depth, LRU_W, 64), "final_norm": (1, D_MODEL)}
    gs, ws, ms, vs = [], [], [], []
    for n, g in zip(_SMALL + ["final_norm"], small_sums[:-1]):
        shp = flat.get(n, w[n].shape)
        gs.append((shard(g) if n in ("conv_dw_w", "lru_conv_w") else g).reshape(shp))
        ws.append(w[n].reshape(shp))
        ms.append(m[n].reshape(shp))
        vs.append(v[n].reshape(shp))
    sd, sm, sv = _adamw_small(gs, ws, ms, vs, "adamw_small")
    for j, n in enumerate(_SMALL + ["final_norm"]):
        out[n] = [a.reshape(w[n].shape) for a in (gs[j], sd[j], sm[j], sv[j])]
    loss_total = small_sums[-1][0, 0]

    result = [loss_total, dh[None]]
    for j in range(4):
        result += [out[n][j] for n in _WEIGHTS]
    return tuple(result)
```

```python
import types

import jax
import jax.numpy as jnp
from jax import lax
from jax.experimental import pallas as pl
from jax.experimental.pallas import tpu as pltpu

F32 = jnp.float32
MX = jnp.bfloat16
WIRE = jnp.bfloat16

D_MODEL = 1024
HEAD_DIM = 64
ATTN_W = 512
KV_W = 128
BLK = 128
CONV_W = 256
CONV_K = 31
LRU_W = 256
LRU_K = 4
LRU_C = 8.0
IN_W = 1792
D_FF = 4096
FF_BLK = 512
N_DEV = 8
IN_SHARD = IN_W // N_DEV
RMS_EPS = 1e-6
LN_EPS = 1e-5
MASK_VALUE = -1e30
SCALE = HEAD_DIM ** -0.5
CONV_HALO = 32
LRU_HALO = 8
CONV_CHUNK = 64
POST_TILE = 512
STREAM_TILE = 1024
DW_TILE = 1024
Q0, K0, V0, CV0, CG0, RX0, RG0 = 0, 512, 640, 768, 1024, 1280, 1536
R_CONV_B, R_LN_G, R_LN_B, R_LCONV_B, R_BA, R_BX, R_LAM, R_LCW = 0, 1, 2, 3, 4, 5, 6, 8

ADAM_LR, ADAM_B1, ADAM_B2, ADAM_EPS, ADAM_WD, ADAM_STEP = 0.001, 0.9, 0.999, 1e-08, 0.01, 10

VMEM_LIMIT = 56 * 1024 * 1024
MESH = pl.DeviceIdType.MESH
ANY = pl.BlockSpec(memory_space=pl.ANY)


def _tile(t, cap=512):
    return min(cap, t)


def _dot(a, b):
    return jnp.dot(a.astype(MX), b.astype(MX), preferred_element_type=F32)


def _dot_nt(a, b):
    return lax.dot_general(a.astype(MX), b.astype(MX), (((1,), (1,)), ((), ())), preferred_element_type=F32)


def _dot_tn(a, b):
    return lax.dot_general(a.astype(MX), b.astype(MX), (((0,), (0,)), ((), ())), preferred_element_type=F32)


def _const_spec(shape):
    nd = len(shape)
    return pl.BlockSpec(shape, lambda *_: (0,) * nd, pipeline_mode=pl.Buffered(1))


def _acc_spec(shape):
    nd = len(shape)
    return pl.BlockSpec(shape, lambda *_: (0,) * nd)


def _sds(shape, dtype):
    return jax.ShapeDtypeStruct(shape, dtype)


def _sigmoid(x):
    return jax.nn.sigmoid(x)


def _rms_fwd(x, g):
    r = lax.rsqrt(jnp.mean(x * x, axis=-1, keepdims=True) + RMS_EPS)
    xh = x * r
    return xh * g, xh, r


def _rms_bwd(dy, xh, r, g):
    t = dy * g
    dx = r * (t - xh * jnp.mean(t * xh, axis=-1, keepdims=True))
    return dx, jnp.sum(dy * xh, axis=0, keepdims=True)


_GROUPS = ((0, 512), (512, 768), (768, 1024))


def _group_rms_fwd(y, g):
    parts = [_rms_fwd(y[:, a:b], g[:, a:b]) for a, b in _GROUPS]
    return (jnp.concatenate([p[0] for p in parts], axis=1),
            jnp.concatenate([p[1] for p in parts], axis=1),
            [p[2] for p in parts])


def _gelu(x):
    c = 0.7978845608028654
    u = c * (x + 0.044715 * x * x * x)
    th = jnp.tanh(u)
    val = 0.5 * x * (1.0 + th)
    grad = 0.5 * (1.0 + th) + 0.5 * x * (1.0 - th * th) * c * (1.0 + 3.0 * 0.044715 * x * x)
    return val, grad


def _neg_expm1(x):
    series = -x * (1.0 + x * (0.5 + x * (1.0 / 6.0 + x * (1.0 / 24.0))))
    return jnp.where(x > -0.02, series, 1.0 - jnp.exp(x))


def _me():
    return lax.axis_index("x"), lax.axis_index("y"), lax.axis_index("c")


def _gather_rider(arrays):
    arrays = list(arrays)
    n = len(arrays)

    def plan(ins, outs, sems):
        ssem, rsem, lsem = sems
        x, y, c = _me()
        chips = [(1 - x, y), (x, 1 - y), (1 - x, 1 - y)]

        def copy(a, k, block, to, own=False):
            dst = outs[a].at[4 * block[0] + 2 * block[1] + block[2]]
            return pltpu.make_async_remote_copy(
                src_ref=ins[a] if own else dst, dst_ref=dst, send_sem=ssem.at[7 * a + k],
                recv_sem=rsem.at[7 * a + k], device_id=to, device_id_type=MESH)

        return x, y, c, chips, copy, lsem

    def start(ins, outs, sems):
        x, y, c, chips, copy, lsem = plan(ins, outs, sems)
        for a in range(n):
            pltpu.make_async_copy(ins[a], outs[a].at[4 * x + 2 * y + c], lsem.at[a]).start()
            copy(a, 0, (x, y, c), (x, y, 1 - c), own=True).start()
            for j, chip in enumerate(chips):
                copy(a, 1 + j, (x, y, c), (*chip, c), own=True).start()

    def mid(ins, outs, sems):
        x, y, c, chips, copy, _ = plan(ins, outs, sems)
        for a in range(n):
            for j, chip in enumerate(chips):
                copy(a, 1 + j, (*chip, c), (x, y, c)).wait_recv()
                copy(a, 4 + j, (*chip, c), (x, y, 1 - c)).start()

    def finish(ins, outs, sems):
        x, y, c, chips, copy, lsem = plan(ins, outs, sems)
        for a in range(n):
            copy(a, 0, (x, y, 1 - c), (x, y, c)).wait_recv()
            for j, chip in enumerate(chips):
                copy(a, 4 + j, (*chip, 1 - c), (x, y, c)).wait_recv()
        for a in range(n):
            copy(a, 0, (x, y, c), (x, y, 1 - c), own=True).wait_send()
            for j, chip in enumerate(chips):
                copy(a, 1 + j, (x, y, c), (*chip, c), own=True).wait_send()
                copy(a, 4 + j, (*chip, c), (x, y, 1 - c)).wait_send()
            pltpu.make_async_copy(ins[a], outs[a].at[4 * x + 2 * y + c], lsem.at[a]).wait()

    return types.SimpleNamespace(
        arrays=arrays, out_shape=[_sds((N_DEV,) + a.shape, a.dtype) for a in arrays],
        scratch=[pltpu.SemaphoreType.DMA((7 * n,)), pltpu.SemaphoreType.DMA((7 * n,)), pltpu.SemaphoreType.DMA((n,))],
        start=start, mid=mid, finish=finish)


def _scatter_rider(arrays):
    arrays = list(arrays)
    n = len(arrays)

    def copies(ins, outs, sems):
        ssem, rsem = sems
        x, y, c = _me()
        out = []
        for a in range(n):
            for f in range(1, N_DEV):
                px = 1 - x if f & 4 else x
                py = 1 - y if f & 2 else y
                pc = 1 - c if f & 1 else c
                out.append(pltpu.make_async_remote_copy(
                    src_ref=ins[a].at[4 * px + 2 * py + pc], dst_ref=outs[a].at[f - 1], send_sem=ssem.at[7 * a + f - 1],
                    recv_sem=rsem.at[7 * a + f - 1], device_id=(px, py, pc), device_id_type=MESH))
        return out

    def start(ins, outs, sems):
        for cp in copies(ins, outs, sems):
            cp.start()

    def finish(ins, outs, sems):
        for cp in copies(ins, outs, sems):
            cp.wait()

    return types.SimpleNamespace(
        arrays=arrays, out_shape=[_sds((N_DEV - 1,) + a.shape[1:], a.dtype) for a in arrays],
        scratch=[pltpu.SemaphoreType.DMA((7 * n,)), pltpu.SemaphoreType.DMA((7 * n,))],
        start=start, mid=None, finish=finish)


def _call(body, name, grid, in_specs, out_specs, out_shape, scratch, operands, riders=()):
    n_in, n_out, n_scr = len(operands), len(out_shape), len(scratch)
    nsteps = grid[0] if grid else 1
    sizes = [(len(r.arrays), len(r.out_shape), len(r.scratch)) for r in riders]

    def wrapped(*refs):
        pos = n_in
        r_ins = []
        for ri, _, _ in sizes:
            r_ins.append(refs[pos:pos + ri])
            pos += ri
        outs = refs[pos:pos + n_out]
        pos += n_out
        r_outs = []
        for _, ro, _ in sizes:
            r_outs.append(refs[pos:pos + ro])
            pos += ro
        scr = refs[pos:pos + n_scr]
        pos += n_scr
        r_sems = []
        for _, _, rs in sizes:
            r_sems.append(refs[pos:pos + rs])
            pos += rs
        step = pl.program_id(0) if grid else 0

        def at(s, fn):
            if grid:
                pl.when(step == s)(fn)
            else:
                fn()

        for r, a, b, c in zip(riders, r_ins, r_outs, r_sems):
            at(0, lambda r=r, a=a, b=b, c=c: r.start(a, b, c))
        for r, a, b, c in zip(riders, r_ins, r_outs, r_sems):
            if r.mid is not None:
                at((3 * nsteps) // 4, lambda r=r, a=a, b=b, c=c: r.mid(a, b, c))
        if body is not None:
            body(*refs[:n_in], *outs, *scr)
        for r, a, b, c in zip(riders, r_ins, r_outs, r_sems):
            at(nsteps - 1, lambda r=r, a=a, b=b, c=c: r.finish(a, b, c))

    r_arrays = [a for r in riders for a in r.arrays]
    r_shapes = [s for r in riders for s in r.out_shape]
    kwargs = {}
    if grid:
        kwargs = dict(grid=grid, compiler_params=pltpu.CompilerParams(
            dimension_semantics=("arbitrary",) * len(grid), vmem_limit_bytes=VMEM_LIMIT))
    res = pl.pallas_call(
        wrapped, name=name,
        in_specs=list(in_specs) + [ANY] * len(r_arrays),
        out_specs=list(out_specs) + [ANY] * len(r_shapes),
        out_shape=list(out_shape) + r_shapes,
        scratch_shapes=list(scratch) + [s for r in riders for s in r.scratch],
        **kwargs,
    )(*operands, *r_arrays)
    host, rest = res[:n_out], res[n_out:]
    r_res = []
    for _, ro, _ in sizes:
        r_res.append(rest[:ro])
        rest = rest[ro:]
    return host, r_res


def _ln_in(h, g1, w_in_t, keep_hn, name):
    t = h.shape[0]
    tm = _tile(t, STREAM_TILE)

    def body(h_ref, g_ref, w_ref, z_ref, *hn_ref):
        y, _, _ = _rms_fwd(h_ref[...], g_ref[...])
        hn = y.astype(MX)
        if keep_hn:
            hn_ref[0][...] = hn
        z_ref[...] = _dot_nt(hn, w_ref[...])

    tile = lambda w: pl.BlockSpec((tm, w), lambda i: (i, 0))
    outs, _ = _call(
        body, name, (t // tm,),
        [tile(D_MODEL), _const_spec((1, D_MODEL)), _const_spec((IN_W, D_MODEL))],
        [tile(IN_W)] + [tile(D_MODEL)] * keep_hn,
        [_sds((t, IN_W), F32)] + [_sds((t, D_MODEL), MX)] * keep_hn, [], [h, g1, w_in_t])
    return outs[0], (outs[1] if keep_hn else None)


def _band2(kb, g):
    lo = lax.broadcasted_iota(jnp.int32, kb.shape, 1) < HEAD_DIM
    kr = pltpu.roll(kb, HEAD_DIM, 1)
    if g == 0:
        top, bot = jnp.where(lo, kb, 0.0), jnp.where(lo, 0.0, kr)
    else:
        top, bot = jnp.where(lo, kr, 0.0), jnp.where(lo, 0.0, kb)
    return jnp.concatenate([top, bot], axis=0)


def _attn_operands(z_ref, zh_ref, b):
    rows = slice(b * BLK, (b + 1) * BLK)
    prev = zh_ref if b == 0 else z_ref
    prow = slice(0, BLK) if b == 0 else slice((b - 1) * BLK, b * BLK)
    kb = jnp.concatenate([prev[prow, K0:K0 + KV_W], z_ref[rows, K0:K0 + KV_W]], axis=0)
    vb = jnp.concatenate([prev[prow, V0:V0 + KV_W], z_ref[rows, V0:V0 + KV_W]], axis=0)
    k2 = [_band2(kb, g) for g in range(2)]
    v2 = [_band2(vb, g) for g in range(2)]
    q2 = [jnp.concatenate([z_ref[rows, (2 * g) * BLK:(2 * g + 1) * BLK], z_ref[rows, (2 * g + 1) * BLK:(2 * g + 2) * BLK]],
                          axis=0) for g in range(2)]
    return q2, k2, v2


def _attn_block(z_ref, zh_ref, sink_ref, b, first):
    q2, k2, v2 = _attn_operands(z_ref, zh_ref, b)
    rr = lax.broadcasted_iota(jnp.int32, (4 * BLK, 2 * BLK), 0) & (BLK - 1)
    cc = lax.broadcasted_iota(jnp.int32, (4 * BLK, 2 * BLK), 1)
    first_block = jnp.logical_and(first, b == 0).astype(jnp.int32)
    mask = jnp.logical_and(jnp.logical_and(cc > rr, cc <= rr + BLK), cc >= BLK * first_block)
    s = jnp.concatenate([_dot_nt(q2[g], k2[g]) for g in range(2)], axis=0) * SCALE
    w = 2 * BLK
    out, psink = [], []
    for hh in range(2):
        sh = jnp.where(mask, s[:, hh * w:(hh + 1) * w], MASK_VALUE)
        sk = jnp.concatenate([jnp.broadcast_to(sink_ref[p:p + 1, hh * w:hh * w + 1], (BLK, 1)) for p in range(4)], axis=0)
        m = jnp.maximum(jnp.max(sh, axis=1, keepdims=True), sk)
        p = jnp.exp(sh - m)
        es = jnp.exp(sk - m)
        inv = 1.0 / (jnp.sum(p, axis=1, keepdims=True) + es)
        out.append(p * inv)
        psink.append(es * inv)
    return v2, jnp.concatenate(out, axis=1), psink


def _scan_steps(a, b, n, span, reverse):
    pos = lax.broadcasted_iota(jnp.int32, a.shape, 0) & (span - 1)
    d = 1
    while d < span:
        keep = pos < span - d if reverse else pos >= d
        shift = n - d if reverse else d
        a_sh = jnp.where(keep, pltpu.roll(a, shift, 0), 1.0)
        b_sh = jnp.where(keep, pltpu.roll(b, shift, 0), 0.0)
        b = a * b_sh + b
        a = a * a_sh
        d *= 2
    return a, b


def _scan(a, b, tm, reverse):
    return _scan_steps(a, b, tm, tm, reverse)


def _shifted_copies(ext, shifts, tm):
    rows = tm + CONV_HALO - 8
    for r in range(1, 8):
        shifts[r - 1, 0:rows, :] = ext[pl.ds(r, rows), :]


def _tap(ext, shifts, off, r0, n):
    a, r = divmod(off, 8)
    lo = 8 * a + r0
    if r == 0:
        return ext[lo:lo + n, :]
    return shifts[r - 1, lo:lo + n, :]


def _glu_fill(z_ref, zh_ref, uext, ush, first, tm, sg_out=None):
    cv = z_ref[:, CV0:CV0 + CONV_W]
    sg = _sigmoid(z_ref[:, CG0:CG0 + CONV_W])
    if sg_out is not None:
        sg_out[...] = sg
    hrow = BLK - CONV_HALO
    uh = zh_ref[hrow:BLK, CV0:CV0 + CONV_W] * _sigmoid(zh_ref[hrow:BLK, CG0:CG0 + CONV_W])
    uext[0:CONV_HALO, :] = jnp.where(first, 0.0, uh)
    uext[CONV_HALO:CONV_HALO + tm, :] = cv * sg
    _shifted_copies(uext, ush, tm)


def _conv_taps(cw_ref, pv_ref, uext, ush, out_ref, tm):
    for r0 in range(0, tm, CONV_CHUNK):
        acc = jnp.broadcast_to(pv_ref[R_CONV_B:R_CONV_B + 1, :], (CONV_CHUNK, CONV_W))
        for k in range(CONV_K):
            acc = acc + cw_ref[k:k + 1, :] * _tap(uext, ush, CONV_HALO - (CONV_K - 1) + k, r0, CONV_CHUNK)
        out_ref[r0:r0 + CONV_CHUNK, :] = acc


def _ln_silu(uc, pv_ref):
    mu = jnp.mean(uc, axis=-1, keepdims=True)
    xc = uc - mu
    rs = lax.rsqrt(jnp.mean(xc * xc, axis=-1, keepdims=True) + LN_EPS)
    xh = xc * rs
    ln = xh * pv_ref[R_LN_G:R_LN_G + 1, :] + pv_ref[R_LN_B:R_LN_B + 1, :]
    sg = _sigmoid(ln)
    return xh, rs, ln, sg


def _lru_gates(z_ref, zh_ref, pv_ref, wa_ref, wx_ref, rxext, first, tm):
    rxext[0:LRU_HALO, :] = jnp.where(first, 0.0, zh_ref[BLK - LRU_HALO:BLK, RX0:RX0 + LRU_W])
    rxext[LRU_HALO:LRU_HALO + tm, :] = z_ref[:, RX0:RX0 + LRU_W]
    xc = jnp.broadcast_to(pv_ref[R_LCONV_B:R_LCONV_B + 1, :], (tm, LRU_W))
    for k in range(LRU_K):
        xc = xc + pv_ref[R_LCW + k:R_LCW + k + 1, :] * rxext[pl.ds(LRU_HALO - (LRU_K - 1) + k, tm), :]
    r = _sigmoid(_dot(xc, wa_ref[...]) + pv_ref[R_BA:R_BA + 1, :])
    ig = _sigmoid(_dot(xc, wx_ref[...]) + pv_ref[R_BX:R_BX + 1, :])
    lam = pv_ref[R_LAM:R_LAM + 1, :]
    sp = jnp.log1p(jnp.exp(-lam))
    la = (-LRU_C * r) * sp
    a = jnp.exp(la)
    mult = jnp.sqrt(_neg_expm1(2.0 * la))
    return xc, r, ig, sp, la, a, mult


def _mixer_in_specs(tm, tile_of):
    hb = tm // BLK
    return [
        pl.BlockSpec((tm, IN_W), lambda i: (tile_of(i), 0)),
        pl.BlockSpec((BLK, IN_W), lambda i: (jnp.maximum(tile_of(i) * hb - 1, 0), 0)),
        _const_spec((8, 4 * BLK)),
        _const_spec((32, CONV_W)),
        _const_spec((16, CONV_W)),
        _const_spec((LRU_W, LRU_W)),
        _const_spec((LRU_W, LRU_W)),
    ]


def _mixer_fwd(z, sink, cw, pv, wa, wx, name, riders=()):
    t = z.shape[0]
    tm = _tile(t)
    nb = tm // BLK

    def body(z_ref, zh_ref, sink_ref, cw_ref, pv_ref, wa_ref, wx_ref, y_ref, hl_ref, uc_ref, p_ref, ps_ref,
             uext, ush, rxext, hcar):
        i = pl.program_id(0)
        first = i == 0

        @pl.when(first)
        def _():
            hcar[...] = jnp.zeros_like(hcar)

        lo = lax.broadcasted_iota(jnp.int32, (4 * BLK, BLK), 1) < HEAD_DIM
        for b in range(nb):
            rows = slice(b * BLK, (b + 1) * BLK)
            v2, prob, psink = _attn_block(z_ref, zh_ref, sink_ref, b, first)
            prob = prob.astype(MX)
            p_ref[b] = prob
            ps_ref[b] = jnp.where(lo, psink[0], psink[1])
            for g in range(2):
                o = _dot(prob[2 * g * BLK:(2 * g + 2) * BLK], v2[g])
                y_ref[rows, (2 * g) * BLK:(2 * g + 1) * BLK] = o[0:BLK]
                y_ref[rows, (2 * g + 1) * BLK:(2 * g + 2) * BLK] = o[BLK:2 * BLK]
        _glu_fill(z_ref, zh_ref, uext, ush, first, tm)
        _conv_taps(cw_ref, pv_ref, uext, ush, uc_ref, tm)
        _, _, ln, sg = _ln_silu(uc_ref[...], pv_ref)
        y_ref[:, ATTN_W:ATTN_W + CONV_W] = ln * sg
        xc, _, ig, _, _, a, mult = _lru_gates(z_ref, zh_ref, pv_ref, wa_ref, wx_ref, rxext, first, tm)
        acum, h = _scan(a, mult * (ig * xc), tm, reverse=False)
        h = h + acum * hcar[0:1, :]
        hl_ref[...] = h
        hcar[0:1, :] = h[tm - 1:tm, :]
        gl, _ = _gelu(z_ref[:, RG0:RG0 + LRU_W])
        y_ref[:, ATTN_W + CONV_W:ATTN_W + CONV_W + LRU_W] = h * gl

    tile = lambda w: pl.BlockSpec((tm, w), lambda i: (i, 0))
    return _call(
        body, name, (t // tm,), _mixer_in_specs(tm, lambda i: i),
        [tile(D_MODEL), tile(LRU_W), tile(CONV_W), pl.BlockSpec((nb, 4 * BLK, 4 * BLK), lambda i: (i, 0, 0)),
         pl.BlockSpec((nb, 4 * BLK, BLK), lambda i: (i, 0, 0))],
        [_sds((t, D_MODEL), F32), _sds((t, LRU_W), F32), _sds((t, CONV_W), F32),
         _sds((t // BLK, 4 * BLK, 4 * BLK), MX), _sds((t // BLK, 4 * BLK, BLK), F32)],
        [pltpu.VMEM((tm + CONV_HALO, CONV_W), F32), pltpu.VMEM((7, tm + CONV_HALO - 8, CONV_W), F32),
         pltpu.VMEM((tm + LRU_HALO, LRU_W), F32), pltpu.VMEM((8, LRU_W), F32)],
        [z, z, sink, cw, pv, wa, wx], riders)


def _mixer_bwd(dy, z, ycat, hl, uc, probs, psinks, sink, cw, pv, wa, wx, name, riders=()):
    t = z.shape[0]
    tm = _tile(t)
    nt = t // tm
    nb = tm // BLK
    rev = lambda i: nt - 1 - i

    def body(dy_ref, z_ref, zh_ref, sink_ref, cw_ref, pv_ref, wa_ref, wx_ref, y_ref, hl_ref, hlh_ref, uc_ref,
             p_ref, ps_ref, dz_ref, dsink_ref, dcw_ref, dpv_ref, dwa_ref, dwx_ref,
             uext, ush, sgs, rxext, dkext, dvext, ducext, dsh, dcw8, dxcext, kcar, vcar, uccar, xccar, gcar):
        i = pl.program_id(0)
        first = i == nt - 1

        @pl.when(i == 0)
        def _():
            for car in (kcar, vcar, uccar, xccar, gcar, dcw8):
                car[...] = jnp.zeros_like(car)
            for acc in (dsink_ref, dpv_ref, dwa_ref, dwx_ref):
                acc[...] = jnp.zeros_like(acc)

        def addrow(r, val):
            dpv_ref[r:r + 1, :] += jnp.sum(val, axis=0, keepdims=True)

        dkext[:, 0:tm] = jnp.zeros((KV_W, tm), F32)
        dvext[:, 0:tm] = jnp.zeros((KV_W, tm), F32)
        dkext[:, tm:tm + BLK] = kcar[...]
        dvext[:, tm:tm + BLK] = vcar[...]
        lane512 = lax.broadcasted_iota(jnp.int32, (1, 4 * BLK), 1) < 2 * BLK
        lo = lax.broadcasted_iota(jnp.int32, (4 * BLK, BLK), 1) < HEAD_DIM
        hd, w2 = HEAD_DIM, 2 * BLK
        for b in range(nb):
            rows = slice(b * BLK, (b + 1) * BLK)
            band = slice(b * BLK, (b + 2) * BLK)
            q2, k2, v2 = _attn_operands(z_ref, zh_ref, b)
            prob = p_ref[b]
            psink = [ps_ref[b, :, 0:1], ps_ref[b, :, HEAD_DIM:HEAD_DIM + 1]]
            stack = lambda ref: jnp.concatenate([ref[rows, p * BLK:(p + 1) * BLK] for p in range(4)], axis=0)
            do4 = stack(dy_ref)
            dlt = do4 * stack(y_ref)
            d0 = jnp.sum(jnp.where(lo, dlt, 0.0), axis=1, keepdims=True)
            d1 = jnp.sum(jnp.where(lo, 0.0, dlt), axis=1, keepdims=True)
            dp = jnp.concatenate([_dot_nt(do4[g * w2:(g + 1) * w2], v2[g]) for g in range(2)], axis=0)
            dl = jnp.concatenate([jnp.broadcast_to(d0, (4 * BLK, w2)), jnp.broadcast_to(d1, (4 * BLK, w2))], axis=1)
            draw = (prob * (dp - dl)) * SCALE
            e0, e1 = psink[0] * d0, psink[1] * d1
            for p in range(4):
                prs = slice(p * BLK, (p + 1) * BLK)
                s0 = jnp.sum(e0[prs], axis=0, keepdims=True)
                s1 = jnp.sum(e1[prs], axis=0, keepdims=True)
                dsink_ref[p:p + 1, :] += -jnp.where(lane512, s0, s1)
            for g in range(2):
                grs = slice(g * w2, (g + 1) * w2)
                dq = _dot(draw[grs], k2[g])
                dz_ref[rows, (2 * g) * BLK:(2 * g + 1) * BLK] = dq[0:BLK].astype(dz_ref.dtype)
                dz_ref[rows, (2 * g + 1) * BLK:(2 * g + 2) * BLK] = dq[BLK:2 * BLK].astype(dz_ref.dtype)
                tk = _dot_tn(q2[g], draw[grs])
                tv = _dot_tn(do4[grs], prob[grs])
                dkext[g * hd:(g + 1) * hd, band] += tk[0:hd, 0:w2] + tk[hd:2 * hd, w2:2 * w2]
                dvext[g * hd:(g + 1) * hd, band] += tv[0:hd, 0:w2] + tv[hd:2 * hd, w2:2 * w2]
        dz_ref[:, K0:K0 + KV_W] = jnp.transpose(dkext[:, BLK:BLK + tm]).astype(dz_ref.dtype)
        dz_ref[:, V0:V0 + KV_W] = jnp.transpose(dvext[:, BLK:BLK + tm]).astype(dz_ref.dtype)
        kcar[...] = dkext[:, 0:BLK]
        vcar[...] = dvext[:, 0:BLK]

        _glu_fill(z_ref, zh_ref, uext, ush, first, tm, sg_out=sgs)
        xh, rs, ln, sg = _ln_silu(uc_ref[...], pv_ref)
        dln = dy_ref[:, ATTN_W:ATTN_W + CONV_W] * (sg * (1.0 + ln * (1.0 - sg)))
        addrow(R_LN_G, dln * xh)
        addrow(R_LN_B, dln)
        dxh = dln * pv_ref[R_LN_G:R_LN_G + 1, :]
        duc = rs * (dxh - jnp.mean(dxh, axis=-1, keepdims=True) - xh * jnp.mean(dxh * xh, axis=-1, keepdims=True))
        addrow(R_CONV_B, duc)
        ducext[0:tm, :] = duc
        ducext[tm:tm + CONV_HALO, :] = uccar[...]
        uccar[...] = duc[0:CONV_HALO, :]
        _shifted_copies(ducext, dsh, tm)
        for r0 in range(0, tm, CONV_CHUNK):
            crow = slice(r0, r0 + CONV_CHUNK)
            duc_c = ducext[crow, :]
            du = jnp.zeros((CONV_CHUNK, CONV_W), F32)
            for k in range(CONV_K):
                prod = duc_c * _tap(uext, ush, CONV_HALO - (CONV_K - 1) + k, r0, CONV_CHUNK)
                part = prod[0:8]
                for s in range(8, CONV_CHUNK, 8):
                    part = part + prod[s:s + 8]
                dcw8[k] += part
                du = du + cw_ref[k:k + 1, :] * _tap(ducext, dsh, CONV_K - 1 - k, r0, CONV_CHUNK)
            sgc = sgs[crow, :]
            dz_ref[crow, CV0:CV0 + CONV_W] = (du * sgc).astype(dz_ref.dtype)
            u_c = uext[CONV_HALO + r0:CONV_HALO + r0 + CONV_CHUNK, :]
            dz_ref[crow, CG0:CG0 + CONV_W] = (du * u_c * (1.0 - sgc)).astype(dz_ref.dtype)

        @pl.when(i == nt - 1)
        def _():
            dcw_ref[...] = jnp.sum(dcw8[...], axis=1)

        xc, r, ig, sp, la, a, mult = _lru_gates(z_ref, zh_ref, pv_ref, wa_ref, wx_ref, rxext, first, tm)
        h = hl_ref[...]
        rowi = lax.broadcasted_iota(jnp.int32, (tm, LRU_W), 0)
        hlast = jnp.where(first, 0.0, hlh_ref[7:8, :])
        hprev = jnp.where(rowi == 0, hlast, pltpu.roll(h, 1, 0))
        dyl = dy_ref[:, ATTN_W + CONV_W:ATTN_W + CONV_W + LRU_W]
        gl, dgl = _gelu(z_ref[:, RG0:RG0 + LRU_W])
        dz_ref[:, RG0:RG0 + LRU_W] = (dyl * h * dgl).astype(dz_ref.dtype)
        dh = dyl * gl + jnp.where(rowi == tm - 1, gcar[0:1, :], 0.0)
        c = jnp.where(rowi == tm - 1, 0.0, pltpu.roll(a, tm - 1, 0))
        _, gg = _scan(c, dh, tm, reverse=True)
        gcar[0:1, :] = a[0:1, :] * gg[0:1, :]
        dmult = gg * (ig * xc)
        dig = gg * mult * xc
        dxc = gg * mult * ig
        dla = gg * hprev * a - dmult * a * a / mult
        dr = dla * (-LRU_C * sp)
        lam = pv_ref[R_LAM:R_LAM + 1, :]
        dpv_ref[R_LAM:R_LAM + 1, :] += jnp.sum(dla * (-LRU_C * r), axis=0, keepdims=True) * (-_sigmoid(-lam))
        dpa = dr * r * (1.0 - r)
        dpx = dig * ig * (1.0 - ig)
        addrow(R_BA, dpa)
        addrow(R_BX, dpx)
        dxc = dxc + _dot_nt(dpa, wa_ref[...]) + _dot_nt(dpx, wx_ref[...])
        dwa_ref[...] += _dot_tn(xc, dpa)
        dwx_ref[...] += _dot_tn(xc, dpx)
        addrow(R_LCONV_B, dxc)
        dxcext[0:tm, :] = dxc
        dxcext[tm:tm + LRU_HALO, :] = xccar[...]
        xccar[...] = dxc[0:LRU_HALO, :]
        drx = jnp.zeros((tm, LRU_W), F32)
        for k in range(LRU_K):
            addrow(R_LCW + k, dxc * rxext[pl.ds(LRU_HALO - (LRU_K - 1) + k, tm), :])
            drx = drx + pv_ref[R_LCW + k:R_LCW + k + 1, :] * dxcext[pl.ds(LRU_K - 1 - k, tm), :]
        dz_ref[:, RX0:RX0 + LRU_W] = drx.astype(dz_ref.dtype)

    tile = lambda w: pl.BlockSpec((tm, w), lambda i: (rev(i), 0))
    in_specs = [tile(D_MODEL)] + _mixer_in_specs(tm, rev) + [
        tile(D_MODEL), tile(LRU_W),
        pl.BlockSpec((8, LRU_W), lambda i: (jnp.maximum(rev(i) * (tm // 8) - 1, 0), 0)),
        tile(CONV_W), pl.BlockSpec((nb, 4 * BLK, 4 * BLK), lambda i: (rev(i), 0, 0)),
        pl.BlockSpec((nb, 4 * BLK, BLK), lambda i: (rev(i), 0, 0))]
    return _call(
        body, name, (nt,), in_specs,
        [tile(IN_W), _acc_spec((8, 4 * BLK)), _acc_spec((32, CONV_W)), _acc_spec((16, CONV_W)),
         _acc_spec((LRU_W, LRU_W)), _acc_spec((LRU_W, LRU_W))],
        [_sds((t, IN_W), MX), _sds((8, 4 * BLK), F32), _sds((32, CONV_W), F32), _sds((16, CONV_W), F32),
         _sds((LRU_W, LRU_W), F32), _sds((LRU_W, LRU_W), F32)],
        [pltpu.VMEM((tm + CONV_HALO, CONV_W), F32), pltpu.VMEM((7, tm + CONV_HALO - 8, CONV_W), F32),
         pltpu.VMEM((tm, CONV_W), F32), pltpu.VMEM((tm + LRU_HALO, LRU_W), F32),
         pltpu.VMEM((KV_W, tm + BLK), F32), pltpu.VMEM((KV_W, tm + BLK), F32),
         pltpu.VMEM((tm + CONV_HALO, CONV_W), F32), pltpu.VMEM((7, tm + CONV_HALO - 8, CONV_W), F32),
         pltpu.VMEM((32, 8, CONV_W), F32), pltpu.VMEM((tm + LRU_HALO, LRU_W), F32),
         pltpu.VMEM((KV_W, BLK), F32), pltpu.VMEM((KV_W, BLK), F32),
         pltpu.VMEM((CONV_HALO, CONV_W), F32), pltpu.VMEM((LRU_HALO, LRU_W), F32), pltpu.VMEM((8, LRU_W), F32)],
        [dy, z, z, sink, cw, pv, wa, wx, ycat, hl, hl, uc, probs, psinks], riders)


def _post_fwd(ycat, h0, gmix, w_out, g2, w_up, w_down, name, riders=()):
    t = h0.shape[0]
    tm = _tile(t, POST_TILE)
    nj = D_FF // FF_BLK

    def body(y_ref, h_ref, gm_ref, wo_ref, g2_ref, wu_ref, wd_ref, h1_ref, a_ref, h2_ref, ym_ref, hn_ref):
        ym, _, _ = _group_rms_fwd(y_ref[...], gm_ref[...])
        ym = ym.astype(MX)
        ym_ref[...] = ym
        h1 = h_ref[...] + jnp.dot(ym, wo_ref[...], preferred_element_type=F32)
        h1_ref[...] = h1
        hn, _, _ = _rms_fwd(h1, g2_ref[...])
        hn = hn.astype(MX)
        hn_ref[...] = hn
        for j in range(nj):
            u = jnp.dot(hn, wu_ref[j], preferred_element_type=F32)
            a_ref[:, j * FF_BLK:(j + 1) * FF_BLK] = jnp.square(jnp.maximum(u, 0.0)).astype(MX)
        h2_ref[...] = h1 + jnp.dot(a_ref[...], wd_ref[...], preferred_element_type=F32)

    tile = lambda w: pl.BlockSpec((tm, w), lambda i: (i, 0))
    return _call(
        body, name, (t // tm,),
        [tile(D_MODEL), tile(D_MODEL), _const_spec((1, D_MODEL)), _const_spec((D_MODEL, D_MODEL)),
         _const_spec((1, D_MODEL)), _const_spec((nj, D_MODEL, FF_BLK)), _const_spec((D_FF, D_MODEL))],
        [tile(D_MODEL), tile(D_FF), tile(D_MODEL), tile(D_MODEL), tile(D_MODEL)],
        [_sds((t, D_MODEL), F32), _sds((t, D_FF), MX), _sds((t, D_MODEL), F32), _sds((t, D_MODEL), MX),
         _sds((t, D_MODEL), MX)],
        [], [ycat, h0, gmix, w_out, g2, w_up, w_down], riders)


def _ffn_bwd(dh2, act, h1, g2, w_up_t, w_down, name, riders=()):
    t = h1.shape[0]
    tm = _tile(t, POST_TILE)
    nj = D_FF // FF_BLK

    def body(dh2_ref, a_ref, h1_ref, g2_ref, wut_ref, wd_ref, dh1_ref, dh1b_ref, dh2b_ref, du_ref, dg2_ref):
        @pl.when(pl.program_id(0) == 0)
        def _():
            dg2_ref[...] = jnp.zeros_like(dg2_ref)

        dh2 = dh2_ref[...]
        dh2b = dh2.astype(MX)
        dh2b_ref[...] = dh2b
        for j in range(nj):
            cols = slice(j * FF_BLK, (j + 1) * FF_BLK)
            da = _dot_nt(dh2b, wd_ref[j])
            du_ref[:, cols] = (da * (2.0 * jnp.sqrt(a_ref[:, cols].astype(F32)))).astype(MX)
        dhn = jnp.dot(du_ref[...], wut_ref[...], preferred_element_type=F32)
        _, xh, r = _rms_fwd(h1_ref[...], g2_ref[...])
        dx, dg = _rms_bwd(dhn, xh, r, g2_ref[...])
        dg2_ref[...] += dg
        dh1 = dh2 + dx
        dh1_ref[...] = dh1
        dh1b_ref[...] = dh1.astype(MX)

    tile = lambda w: pl.BlockSpec((tm, w), lambda i: (i, 0))
    return _call(
        body, name, (t // tm,),
        [tile(D_MODEL), tile(D_FF), tile(D_MODEL), _const_spec((1, D_MODEL)),
         _const_spec((D_FF, D_MODEL)), _const_spec((nj, FF_BLK, D_MODEL))],
        [tile(D_MODEL), tile(D_MODEL), tile(D_MODEL), tile(D_FF), _acc_spec((1, D_MODEL))],
        [_sds((t, D_MODEL), F32), _sds((t, D_MODEL), MX), _sds((t, D_MODEL), MX), _sds((t, D_FF), MX),
         _sds((1, D_MODEL), F32)],
        [], [dh2, act, h1, g2, w_up_t, w_down], riders)


def _mix_bwd(dh1, ycat, ym, gmix, w_out, name):
    t = dh1.shape[0]
    tm = _tile(t)
    nk = t // tm
    r = D_MODEL // N_DEV

    def body(dh1_ref, y_ref, ym_ref, gm_ref, wo_ref, dy_ref, dgm_ref, o_ref, o16_ref, acc):
        k = pl.program_id(0)

        @pl.when(k == 0)
        def _():
            dgm_ref[...] = jnp.zeros_like(dgm_ref)
            acc[...] = jnp.zeros_like(acc)

        dh = dh1_ref[...]
        acc[...] += _dot_tn(ym_ref[...], dh)
        dym = _dot_nt(dh, wo_ref[...])
        gm = gm_ref[...]
        _, yh, rr = _group_rms_fwd(y_ref[...], gm)
        outs, dgs = [], []
        for (a, b), rg in zip(_GROUPS, rr):
            dxg, dgg = _rms_bwd(dym[:, a:b], yh[:, a:b], rg, gm[:, a:b])
            outs.append(dxg)
            dgs.append(dgg)
        dy_ref[...] = jnp.concatenate(outs, axis=1)
        dgm_ref[...] += jnp.concatenate(dgs, axis=1)

        @pl.when(k == nk - 1)
        def _():
            for d in range(N_DEV):
                v = acc[d * r:(d + 1) * r, :]
                o_ref[d] = v
                o16_ref[d] = v.astype(o16_ref.dtype)

    tile = pl.BlockSpec((tm, D_MODEL), lambda i: (i, 0))
    slabs = _const_spec((N_DEV, r, D_MODEL))
    (dy, dgm, dw, dw16), _ = _call(
        body, name, (nk,), [tile, tile, tile, _const_spec((1, D_MODEL)), _const_spec((D_MODEL, D_MODEL))],
        [tile, _acc_spec((1, D_MODEL)), slabs, slabs],
        [_sds((t, D_MODEL), F32), _sds((1, D_MODEL), F32), _sds((N_DEV, r, D_MODEL), F32),
         _sds((N_DEV, r, D_MODEL), WIRE)],
        [pltpu.VMEM((D_MODEL, D_MODEL), F32)], [dh1, ycat, ym, gmix, w_out])
    return dy, dgm, (dw, dw16)


def _in_bwd(dz, h0, dh1, g1, w_in_t, after, name):
    t = h0.shape[0]
    tm = _tile(t, STREAM_TILE)

    def body(dz_ref, h_ref, dh1_ref, g_ref, w_ref, after_ref, dh0_ref, dg_ref):
        @pl.when(pl.program_id(0) == 0)
        def _():
            dg_ref[...] = jnp.zeros_like(dg_ref)

        dhn = _dot(dz_ref[...], w_ref[...])
        _, xh, r = _rms_fwd(h_ref[...], g_ref[...])
        dx, dg = _rms_bwd(dhn, xh, r, g_ref[...])
        dg_ref[...] += dg
        dh0_ref[...] = dh1_ref[...] + dx

    tile = lambda w: pl.BlockSpec((tm, w), lambda i: (i, 0))
    (dh0, dg), _ = _call(
        body, name, (t // tm,),
        [tile(IN_W), tile(D_MODEL), tile(D_MODEL), _const_spec((1, D_MODEL)), _const_spec((IN_W, D_MODEL)),
         _const_spec((8, 128))],
        [tile(D_MODEL), _acc_spec((1, D_MODEL))], [_sds((t, D_MODEL), F32), _sds((1, D_MODEL), F32)],
        [], [dz, h0, dh1, g1, w_in_t, after])
    return dh0, dg


def _in_bwd_dw(dz, h0, dh1, g1, w_in_t, name):
    t = h0.shape[0]
    tm = _tile(t)
    nk = t // tm

    def body(dz_ref, h_ref, dh1_ref, g_ref, w_ref, dh0_ref, dg_ref, o_ref, o16_ref, acc):
        k = pl.program_id(0)

        @pl.when(k == 0)
        def _():
            dg_ref[...] = jnp.zeros_like(dg_ref)
            acc[...] = jnp.zeros_like(acc)

        dz_t = dz_ref[...]
        hn, xh, r = _rms_fwd(h_ref[...], g_ref[...])
        acc[...] += _dot_tn(dz_t, hn)
        dhn = _dot(dz_t, w_ref[...])
        dx, dg = _rms_bwd(dhn, xh, r, g_ref[...])
        dg_ref[...] += dg
        dh0_ref[...] = dh1_ref[...] + dx

        @pl.when(k == nk - 1)
        def _():
            for d in range(N_DEV):
                v = acc[d * IN_SHARD:(d + 1) * IN_SHARD, :]
                o_ref[d] = v
                o16_ref[d] = v.astype(o16_ref.dtype)

    tile = lambda w: pl.BlockSpec((tm, w), lambda i: (i, 0))
    slabs = _const_spec((N_DEV, IN_SHARD, D_MODEL))
    (dh0, dg, dw, dw16), _ = _call(
        body, name, (nk,),
        [tile(IN_W), tile(D_MODEL), tile(D_MODEL), _const_spec((1, D_MODEL)), _const_spec((IN_W, D_MODEL))],
        [tile(D_MODEL), _acc_spec((1, D_MODEL)), slabs, slabs],
        [_sds((t, D_MODEL), F32), _sds((1, D_MODEL), F32), _sds((N_DEV, IN_SHARD, D_MODEL), F32),
         _sds((N_DEV, IN_SHARD, D_MODEL), WIRE)],
        [pltpu.VMEM((IN_W, D_MODEL), F32)], [dz, h0, dh1, g1, w_in_t])
    return dh0, dg, (dw, dw16)


def _loss_head(h, gf, target, name):
    t = h.shape[0]
    tm = _tile(t, STREAM_TILE)

    def body(h_ref, g_ref, t_ref, dh_ref, loss_ref, dg_ref):
        @pl.when(pl.program_id(0) == 0)
        def _():
            loss_ref[...] = jnp.zeros_like(loss_ref)
            dg_ref[...] = jnp.zeros_like(dg_ref)

        g = g_ref[...]
        y, xh, r = _rms_fwd(h_ref[...], g)
        err = y - t_ref[...]
        part = 0.5 * jnp.sum(jnp.mean(err * err, axis=-1, keepdims=True), axis=0, keepdims=True)
        loss_ref[...] += jnp.broadcast_to(part, loss_ref.shape)
        dx, dg = _rms_bwd(err * (1.0 / D_MODEL), xh, r, g)
        dg_ref[...] += dg
        dh_ref[...] = dx

    tile = pl.BlockSpec((tm, D_MODEL), lambda i: (i, 0))
    (dh, loss, dg), _ = _call(
        body, name, (t // tm,), [tile, _const_spec((1, D_MODEL)), tile],
        [tile, _acc_spec((1, 128)), _acc_spec((1, D_MODEL))],
        [_sds((t, D_MODEL), F32), _sds((1, 128), F32), _sds((1, D_MODEL), F32)], [], [h, gf, target])
    return dh, loss, dg


def _dw(x, y, name, split, bm, bn):
    t, m = x.shape
    n = y.shape[1]
    tk = _tile(t, DW_TILE)
    nk = t // tk
    if split == "rows":
        assert bn == n
        r, c = m // N_DEV, n
        per = bm // r
        out_block = pl.BlockSpec((per, r, c), lambda a, b, k: (a, 0, 0))
    else:
        assert bm == m
        r, c = m, n // N_DEV
        per = bn // c
        out_block = pl.BlockSpec((per, r, c), lambda a, b, k: (b, 0, 0))

    def body(x_ref, y_ref, o_ref, o16_ref, acc):
        k = pl.program_id(2)

        @pl.when(k == 0)
        def _():
            acc[...] = jnp.zeros_like(acc)

        acc[...] += _dot_tn(x_ref[...], y_ref[...])

        @pl.when(k == nk - 1)
        def _():
            for d in range(per):
                v = acc[d * r:(d + 1) * r, :] if split == "rows" else acc[:, d * c:(d + 1) * c]
                o_ref[d] = v
                o16_ref[d] = v.astype(o16_ref.dtype)

    return pl.pallas_call(
        body, name=name, grid=(m // bm, n // bn, nk),
        in_specs=[pl.BlockSpec((tk, bm), lambda a, b, k: (k, a)), pl.BlockSpec((tk, bn), lambda a, b, k: (k, b))],
        out_specs=[out_block, out_block],
        out_shape=[_sds((N_DEV, r, c), F32), _sds((N_DEV, r, c), WIRE)],
        scratch_shapes=[pltpu.VMEM((bm, bn), F32)],
        compiler_params=pltpu.CompilerParams(dimension_semantics=("arbitrary",) * 3, vmem_limit_bytes=VMEM_LIMIT),
    )(x, y)


def _adamw_math(w, g, m, v):
    m = ADAM_B1 * m + (1.0 - ADAM_B1) * g
    v = ADAM_B2 * v + (1.0 - ADAM_B2) * jnp.square(g)
    m_hat = m / (1.0 - ADAM_B1 ** ADAM_STEP)
    v_hat = v / (1.0 - ADAM_B2 ** ADAM_STEP)
    delta = -ADAM_LR * (m_hat / (jnp.sqrt(v_hat) + ADAM_EPS) + ADAM_WD * w)
    return delta, m, v


def _adamw_shard(g_own, g_recv, dev, w, m, v, after, name):
    _, r, c = w.shape
    br = r
    for cand in (256, 128, 112, 64, 56, 32, 16, 8):
        if r % cand == 0:
            br = cand
            break
    nr = r // br
    own = lambda l: pl.BlockSpec((1, br, c), lambda ll, i, d: (d[0], jnp.where(ll == l, i, (nr - 1) * (1 - l)), 0))
    recv = lambda l: pl.BlockSpec((N_DEV - 1, br, c), lambda ll, i, d: (0, jnp.where(ll == l, i, (nr - 1) * (1 - l)), 0))

    def body(dev_ref, go0, gr0, go1, gr1, w_ref, m_ref, v_ref, after_ref, g_out, d_out, m_out, v_out):
        def update(go_ref, gr_ref):
            g = go_ref[0]
            for j in range(N_DEV - 1):
                g = g + gr_ref[j].astype(F32)
            delta, mn, vn = _adamw_math(w_ref[0], g, m_ref[0], v_ref[0])
            g_out[0] = g
            d_out[0] = delta
            m_out[0] = mn
            v_out[0] = vn

        layer = pl.program_id(0)
        pl.when(layer == 0)(lambda: update(go0, gr0))
        pl.when(layer == 1)(lambda: update(go1, gr1))

    tile = pl.BlockSpec((1, br, c), lambda ll, i, d: (ll, i, 0))
    return pl.pallas_call(
        body, name=name,
        grid_spec=pltpu.PrefetchScalarGridSpec(
            num_scalar_prefetch=1, grid=(2, nr),
            in_specs=[own(0), recv(0), own(1), recv(1), tile, tile, tile,
                      pl.BlockSpec((8, 128), lambda ll, i, d: (0, 0))],
            out_specs=[tile, tile, tile, tile]),
        out_shape=[_sds((2, r, c), F32)] * 4,
        compiler_params=pltpu.CompilerParams(dimension_semantics=("arbitrary",) * 2, vmem_limit_bytes=VMEM_LIMIT),
    )(dev, g_own[0], g_recv[0], g_own[1], g_recv[1], w, m, v, after)


def _adamw_small(gs, ws, ms, vs, name):
    n = len(gs)

    def body(*refs):
        g_refs, w_refs, m_refs, v_refs = (refs[k * n:(k + 1) * n] for k in range(4))
        outs = refs[4 * n:]
        for k in range(n):
            delta, mn, vn = _adamw_math(w_refs[k][...], g_refs[k][...], m_refs[k][...], v_refs[k][...])
            outs[k][...] = delta
            outs[n + k][...] = mn
            outs[2 * n + k][...] = vn

    shapes = [_sds(w.shape, F32) for w in ws]
    res = pl.pallas_call(body, name=name, out_shape=shapes * 3,
                         compiler_params=pltpu.CompilerParams(vmem_limit_bytes=VMEM_LIMIT))(*gs, *ws, *ms, *vs)
    return res[:n], res[n:2 * n], res[2 * n:]


def _sum_parts(own, recv, dev, name):
    def body(dev_ref, own_ref, recv_ref, o_ref):
        me = dev_ref[0]

        def block(d):
            f = jnp.bitwise_xor(me, d)
            return jnp.where(f == 0, own_ref[...], recv_ref[jnp.maximum(f - 1, 0)])

        g = block(0)
        for d in range(1, N_DEV):
            g = g + block(d)
        o_ref[...] = g

    return pl.pallas_call(
        body, name=name,
        grid_spec=pltpu.PrefetchScalarGridSpec(
            num_scalar_prefetch=1, grid=(1,),
            in_specs=[pl.BlockSpec(own.shape, lambda i, d: (0, 0)), pl.BlockSpec(recv.shape, lambda i, d: (0, 0, 0))],
            out_specs=pl.BlockSpec(own.shape, lambda i, d: (0, 0))),
        out_shape=_sds(own.shape, F32))(dev, own, recv)


HBM = pl.BlockSpec(memory_space=pltpu.HBM)
SEM = pl.BlockSpec(memory_space=pltpu.SEMAPHORE)
EFFECT = pltpu.SideEffectType.DATAFLOW_SIDE_EFFECTING


def _direct_copies(srcs, lands, ssem, rsem, scatter):
    x, y, c = _me()
    out = []
    for a in range(len(srcs)):
        for f in range(1, N_DEV):
            px = 1 - x if f & 4 else x
            py = 1 - y if f & 2 else y
            pc = 1 - c if f & 1 else c
            out.append(pltpu.make_async_remote_copy(
                src_ref=srcs[a].at[4 * px + 2 * py + pc] if scatter else srcs[a], dst_ref=lands[a].at[f - 1],
                send_sem=ssem.at[7 * a + f - 1], recv_sem=rsem.at[7 * a + f - 1],
                device_id=(px, py, pc), device_id_type=MESH))
    return out


def _send_start(arrays, scatter, name):
    arrays = list(arrays)
    n = len(arrays)
    lands = [lax.empty((N_DEV - 1,) + (a.shape[1:] if scatter else a.shape), a.dtype) for a in arrays]

    def body(*refs):
        srcs, lnds, ssem, rsem, token = refs[:n], refs[n:2 * n], refs[2 * n], refs[2 * n + 1], refs[-1]
        for cp in _direct_copies(srcs, lnds, ssem, rsem, scatter):
            cp.start()
        token[...] = jnp.zeros_like(token)

    hbm = lambda a: pltpu.HBM(a.shape, a.dtype)
    res = pl.pallas_call(
        body, name=name,
        out_shape=(pltpu.SemaphoreType.DMA((7 * n,)), pltpu.SemaphoreType.DMA((7 * n,)),
                   *[hbm(a) for a in arrays + lands], _sds((8, 128), F32)),
        in_specs=[HBM] * (2 * n),
        out_specs=(SEM, SEM, *[HBM] * (2 * n), pl.BlockSpec(memory_space=pltpu.VMEM)),
        input_output_aliases={i: 2 + i for i in range(2 * n)},
        compiler_params=pltpu.CompilerParams(has_side_effects=EFFECT),
    )(*[pltpu.with_memory_space_constraint(a, pltpu.HBM) for a in arrays + lands])
    return types.SimpleNamespace(ssem=res[0], rsem=res[1], srcs=list(res[2:2 + n]), lands=list(res[2 + n:2 + 2 * n]),
                                 token=res[-1], scatter=scatter)


def _send_wait(h, after, name):
    n = len(h.srcs)

    def body(*refs):
        srcs, lnds, ssem, rsem = refs[:n], refs[n:2 * n], refs[2 * n], refs[2 * n + 1]
        for cp in _direct_copies(srcs, lnds, ssem, rsem, h.scatter):
            cp.wait_send()
            cp.wait_recv()

    hbm = lambda a: pltpu.HBM(a.shape, a.dtype)
    res = pl.pallas_call(
        body, name=name,
        out_shape=tuple(hbm(a) for a in h.srcs + h.lands),
        in_specs=[HBM] * (2 * n) + [SEM, SEM, ANY], out_specs=[HBM] * (2 * n),
        input_output_aliases={i: i for i in range(2 * n)},
        compiler_params=pltpu.CompilerParams(has_side_effects=EFFECT),
    )(*h.srcs, *h.lands, h.ssem, h.rsem, after)
    return list(res[:n]), list(res[n:])


def _block_diag(w):
    out = jnp.zeros((LRU_W, LRU_W), w.dtype)
    for h in range(4):
        out = lax.dynamic_update_slice(out, w[h], (h * 64, h * 64))
    return out


def _unblock_diag(w):
    return jnp.concatenate([w[h * 64:(h + 1) * 64, h * 64:(h + 1) * 64] for h in range(4)], axis=0)


def _layer_params(p, l):
    row = lambda a: a[l].reshape(1, -1)
    sink_rows = jnp.repeat(p["attn_sinks"][l].reshape(4, 2), 2 * BLK, axis=1)
    sink_rows = jnp.concatenate([sink_rows, jnp.zeros((4, 4 * BLK), F32)], axis=0)
    cw = jnp.concatenate([p["conv_dw_w"][l], jnp.zeros((1, CONV_W), F32)], axis=0)
    pv = jnp.concatenate([
        row(p["conv_dw_b"]), row(p["conv_ln_g"]), row(p["conv_ln_b"]), row(p["lru_conv_b"]), row(p["lru_ba"]),
        row(p["lru_bx"]), row(p["lru_lambda"]), jnp.zeros((1, LRU_W), F32), p["lru_conv_w"][l],
        jnp.zeros((4, LRU_W), F32)], axis=0)
    return dict(
        g1=row(p["norm1"]), sink=sink_rows, cw=cw, pv=pv,
        wa=_block_diag(p["lru_wa"][l]).astype(MX), wx=_block_diag(p["lru_wx"][l]).astype(MX),
        gmix=row(p["mix_norm"]), g2=row(p["norm2"]))


_SMALL = ["norm1", "attn_sinks", "conv_dw_w", "conv_dw_b", "conv_ln_g", "conv_ln_b", "lru_conv_w", "lru_conv_b",
          "lru_wa", "lru_ba", "lru_wx", "lru_bx", "lru_lambda", "mix_norm", "norm2"]
_BIG = ["w_in", "w_out", "w_up", "w_down"]
_WEIGHTS = ["norm1", "w_in", "attn_sinks", "conv_dw_w", "conv_dw_b", "conv_ln_g", "conv_ln_b", "lru_conv_w",
            "lru_conv_b", "lru_wa", "lru_ba", "lru_wx", "lru_bx", "lru_lambda", "mix_norm", "w_out", "norm2", "w_up",
            "w_down", "final_norm"]


def kernel(x, norm1, w_in, attn_sinks, conv_dw_w, conv_dw_b, conv_ln_g, conv_ln_b, lru_conv_w, lru_conv_b, lru_wa, lru_ba, lru_wx, lru_bx, lru_lambda, mix_norm, w_out, norm2, w_up, w_down, final_norm, loss_target, m_norm1, m_w_in, m_attn_sinks, m_conv_dw_w, m_conv_dw_b, m_conv_ln_g, m_conv_ln_b, m_lru_conv_w, m_lru_conv_b, m_lru_wa, m_lru_ba, m_lru_wx, m_lru_bx, m_lru_lambda, m_mix_norm, m_w_out, m_norm2, m_w_up, m_w_down, m_final_norm, v_norm1, v_w_in, v_attn_sinks, v_conv_dw_w, v_conv_dw_b, v_conv_ln_g, v_conv_ln_b, v_lru_conv_w, v_lru_conv_b, v_lru_wa, v_lru_ba, v_lru_wx, v_lru_bx, v_lru_lambda, v_mix_norm, v_w_out, v_norm2, v_w_up, v_w_down, v_final_norm):
    w = dict(norm1=norm1, w_in=w_in, attn_sinks=attn_sinks, conv_dw_w=conv_dw_w, conv_dw_b=conv_dw_b,
             conv_ln_g=conv_ln_g, conv_ln_b=conv_ln_b, lru_conv_w=lru_conv_w, lru_conv_b=lru_conv_b, lru_wa=lru_wa,
             lru_ba=lru_ba, lru_wx=lru_wx, lru_bx=lru_bx, lru_lambda=lru_lambda, mix_norm=mix_norm, w_out=w_out,
             norm2=norm2, w_up=w_up, w_down=w_down, final_norm=final_norm)
    m = dict(norm1=m_norm1, w_in=m_w_in, attn_sinks=m_attn_sinks, conv_dw_w=m_conv_dw_w, conv_dw_b=m_conv_dw_b,
             conv_ln_g=m_conv_ln_g, conv_ln_b=m_conv_ln_b, lru_conv_w=m_lru_conv_w, lru_conv_b=m_lru_conv_b,
             lru_wa=m_lru_wa, lru_ba=m_lru_ba, lru_wx=m_lru_wx, lru_bx=m_lru_bx, lru_lambda=m_lru_lambda,
             mix_norm=m_mix_norm, w_out=m_w_out, norm2=m_norm2, w_up=m_w_up, w_down=m_w_down, final_norm=m_final_norm)
    v = dict(norm1=v_norm1, w_in=v_w_in, attn_sinks=v_attn_sinks, conv_dw_w=v_conv_dw_w, conv_dw_b=v_conv_dw_b,
             conv_ln_g=v_conv_ln_g, conv_ln_b=v_conv_ln_b, lru_conv_w=v_lru_conv_w, lru_conv_b=v_lru_conv_b,
             lru_wa=v_lru_wa, lru_ba=v_lru_ba, lru_wx=v_lru_wx, lru_bx=v_lru_bx, lru_lambda=v_lru_lambda,
             mix_norm=v_mix_norm, w_out=v_w_out, norm2=v_norm2, w_up=v_w_up, w_down=v_w_down, final_norm=v_final_norm)
    depth = w_in.shape[0]
    xi, yi, ci = _me()
    dev = (4 * xi + 2 * yi + ci).astype(jnp.int32)
    dev1 = dev.reshape(1)
    tr = lambda a: jnp.swapaxes(a, 1, 2)
    w_t, m_t, v_t = tr(w_in), tr(m_w_in), tr(v_w_in)
    wb = {n: w[n].astype(MX) for n in _BIG if n != "w_in"}
    wb["w_in"] = w_t.astype(MX)
    layer_shards = lambda l: [wb["w_out"][l], wb["w_up"][l], wb["w_down"][l]]

    _, ((g_in0, g_cw, g_lcw),) = _call(None, "gather_first", None, [], [], [], [], [],
                                        [_gather_rider([wb["w_in"][0], conv_dw_w, lru_conv_w])])
    cols = lambda g: jnp.moveaxis(g, 0, -2).reshape(g.shape[1:-1] + (N_DEV * g.shape[-1],))
    p = dict(w)
    p["conv_dw_w"] = cols(g_cw)
    p["lru_conv_w"] = cols(g_lcw)
    lp = [_layer_params(p, l) for l in range(depth)]

    gathered = [dict(w_in=g_in0.reshape(IN_W, D_MODEL)), dict()]
    saved = []
    h = x[0]
    for l in range(depth):
        q, gw = lp[l], gathered[l]
        z, hn1 = _ln_in(h, q["g1"], gw["w_in"], l == 0, f"ln_in{l}")
        riders = [_gather_rider(layer_shards(0))] if l == 0 else []
        (ycat, hl, uc, probs, psinks), got = _mixer_fwd(z, q["sink"], q["cw"], q["pv"], q["wa"], q["wx"],
                                                        f"mixer_fwd{l}", riders)
        if l == 0:
            gw["w_out"], gw["w_up"], gw["w_down"] = got[0]
            gw["w_out"] = gw["w_out"].reshape(D_MODEL, D_MODEL)
        riders = [_gather_rider([wb["w_in"][1]] + layer_shards(1))] if l == 0 else []
        (h1, act, h2, ym, hn2), got = _post_fwd(ycat, h, q["gmix"], gw["w_out"], q["g2"], gw["w_up"],
                                                gw["w_down"].reshape(D_FF, D_MODEL), f"post_fwd{l}", riders)
        if l == 0:
            nxt = gathered[1]
            nxt["w_in"], nxt["w_out"], nxt["w_up"], nxt["w_down"] = got[0]
            nxt["w_in"] = nxt["w_in"].reshape(IN_W, D_MODEL)
            nxt["w_out"] = nxt["w_out"].reshape(D_MODEL, D_MODEL)
        saved.append(dict(h0=h, z=z, hn1=hn1, ycat=ycat, hl=hl, uc=uc, probs=probs, psinks=psinks, h1=h1, act=act,
                          ym=ym, hn2=hn2))
        h = h2
    dh, loss, dgf = _loss_head(h, final_norm.reshape(1, -1), loss_target[0], "loss_head")

    grads = [None] * depth
    big = {n: [None] * depth for n in _BIG}
    pending = []

    def send_pending():
        riders = [_scatter_rider([item[3] for item in pending])] if pending else []
        return riders, list(pending)

    def record(sent, got):
        for item, recv in zip(sent, got[0] if sent else []):
            big[item[0]][item[1]] = (item[2], recv)
        del pending[:len(sent)]

    for l in reversed(range(depth)):
        q, s, gw = lp[l], saved[l], gathered[l]
        riders, sent = send_pending()
        w_up_t = jnp.swapaxes(gw["w_up"], 1, 2).reshape(D_FF, D_MODEL)
        (dh1, dh1b, dhb, du, dg2), got = _ffn_bwd(dh, s["act"], s["h1"], q["g2"], w_up_t, gw["w_down"],
                                                  f"ffn_bwd{l}", riders)
        record(sent, got)
        dycat, dgm, d_wout = _mix_bwd(dh1b, s["ycat"], s["ym"], q["gmix"], gw["w_out"], f"mix_bwd{l}")
        pending.append(("w_down", l) + tuple(_dw(s["act"], dhb, f"dw_down{l}", "rows", 2048, D_MODEL)))
        pending.append(("w_up", l) + tuple(_dw(s["hn2"], du, f"dw_up{l}", "cols", D_MODEL, 2048)))
        pending.append(("w_out", l) + tuple(d_wout))
        riders, sent = send_pending()
        (dz, dsink, dcw, dpv, dwa, dwx), got = _mixer_bwd(
            dycat, s["z"], s["ycat"], s["hl"], s["uc"], s["probs"], s["psinks"], q["sink"], q["cw"], q["pv"], q["wa"],
            q["wx"], f"mixer_bwd{l}", riders)
        record(sent, got)
        if l > 0:
            dh, dg1, d_win = _in_bwd_dw(dz, s["h0"], dh1, q["g1"], gw["w_in"], f"in_bwd{l}")
            pending.append(("w_in", l) + tuple(d_win))
        else:
            d_win = _dw(dz, s["hn1"], f"dw_in{l}", "rows", IN_W, D_MODEL)
            win_sends = _send_start([d_win[1]], True, "scatter_w_in0_start")
            dh, dg1 = _in_bwd(dz, s["h0"], dh1, q["g1"], gw["w_in"], win_sends.token, f"in_bwd{l}")
        grads[l] = dict(
            norm1=dg1[0], attn_sinks=jnp.stack([dsink[0:4, 0], dsink[0:4, 2 * BLK]], axis=1).reshape(8),
            conv_dw_w=dcw[0:CONV_K], conv_dw_b=dpv[R_CONV_B], conv_ln_g=dpv[R_LN_G], conv_ln_b=dpv[R_LN_B],
            lru_conv_w=dpv[R_LCW:R_LCW + LRU_K], lru_conv_b=dpv[R_LCONV_B], lru_wa=_unblock_diag(dwa),
            lru_ba=dpv[R_BA].reshape(4, 64), lru_wx=_unblock_diag(dwx), lru_bx=dpv[R_BX].reshape(4, 64),
            lru_lambda=dpv[R_LAM], mix_norm=dgm[0], norm2=dg2[0])

    small = [jnp.stack([grads[l][n] for l in range(depth)]) for n in _SMALL] + [dgf, loss[:, 0:1]]

    def as_rows(a):
        flat = a.reshape(-1)
        pad = (-flat.size) % 1024
        if pad:
            flat = jnp.concatenate([flat, jnp.zeros((pad,), F32)])
        return flat.reshape(-1, 128)

    pieces = [as_rows(a) for a in small]
    packed = jnp.concatenate(pieces, axis=0)
    small_sends = _send_start([packed], False, "bcast_small_start")

    out = {}
    shard_update = lambda n, wmv, after: list(_adamw_shard(
        [big[n][l][0] for l in range(depth)], [big[n][l][1] for l in range(depth)], dev1, *wmv, after, f"adamw_{n}"))
    for n in ("w_out", "w_up", "w_down"):
        out[n] = shard_update(n, (w[n], m[n], v[n]), small_sends.token)
    _, (win_recv,) = _send_wait(win_sends, out["w_down"][1], "scatter_w_in0_wait")
    big["w_in"][0] = (d_win[0], win_recv)
    out["w_in"] = [tr(a) for a in shard_update("w_in", (w_t, m_t, v_t), jnp.zeros((8, 128), F32))]
    (packed,), (small_recv,) = _send_wait(small_sends, out["w_in"][1], "bcast_small_wait")
    summed = _sum_parts(packed, small_recv, dev1, "sum_small_grads")
    small_sums, row = [], 0
    for a, piece in zip(small, pieces):
        got = summed[row:row + piece.shape[0]]
        small_sums.append(got.reshape(a.shape) if a.size == piece.size else got.reshape(-1)[:a.size].reshape(a.shape))
        row += piece.shape[0]
    shard = lambda a: lax.dynamic_slice_in_dim(a, dev * (a.shape[-1] // N_DEV), a.shape[-1] // N_DEV, axis=a.ndim - 1)
    flat = {"lru_wa": (depth, LRU_W, 64), "lru_wx": (depth, LRU_W, 64), "final_norm": (1, D_MODEL)}
    gs, ws, ms, vs = [], [], [], []
    for n, g in zip(_SMALL + ["final_norm"], small_sums[:-1]):
        shp = flat.get(n, w[n].shape)
        gs.append((shard(g) if n in ("conv_dw_w", "lru_conv_w") else g).reshape(shp))
        ws.append(w[n].reshape(shp))
        ms.append(m[n].reshape(shp))
        vs.append(v[n].reshape(shp))
    sd, sm, sv = _adamw_small(gs, ws, ms, vs, "adamw_small")
    for j, n in enumerate(_SMALL + ["final_norm"]):
        out[n] = [a.reshape(w[n].shape) for a in (gs[j], sd[j], sm[j], sv[j])]
    loss_total = small_sums[-1][0, 0]

    result = [loss_total, dh[None]]
    for j in range(4):
        result += [out[n][j] for n in _WEIGHTS]
    return tuple(result)
```

```python
import types

import jax
import jax.numpy as jnp
from jax import lax
from jax.experimental import pallas as pl
from jax.experimental.pallas import tpu as pltpu

F32 = jnp.float32
MX = jnp.bfloat16
WIRE = jnp.bfloat16

D_MODEL = 1024
HEAD_DIM = 64
ATTN_W = 512
KV_W = 128
BLK = 128
CONV_W = 256
CONV_K = 31
LRU_W = 256
LRU_K = 4
LRU_C = 8.0
IN_W = 1792
D_FF = 4096
FF_BLK = 512
N_DEV = 8
IN_SHARD = IN_W // N_DEV
RMS_EPS = 1e-6
LN_EPS = 1e-5
MASK_VALUE = -1e30
SCALE = HEAD_DIM ** -0.5
CONV_HALO = 32
LRU_HALO = 8
CONV_CHUNK = 64
MIXER_TILE = 1024
POST_TILE = 512
STREAM_TILE = 1024
DW_TILE = 1024
Q0, K0, V0, CV0, CG0, RX0, RG0 = 0, 512, 640, 768, 1024, 1280, 1536
R_CONV_B, R_LN_G, R_LN_B, R_LCONV_B, R_BA, R_BX, R_LAM, R_LCW = 0, 1, 2, 3, 4, 5, 6, 8

ADAM_LR, ADAM_B1, ADAM_B2, ADAM_EPS, ADAM_WD, ADAM_STEP = 0.001, 0.9, 0.999, 1e-08, 0.01, 10

VMEM_LIMIT = 56 * 1024 * 1024
MESH = pl.DeviceIdType.MESH
ANY = pl.BlockSpec(memory_space=pl.ANY)


def _tile(t, cap=512):
    return min(cap, t)


def _dot(a, b):
    return jnp.dot(a.astype(MX), b.astype(MX), preferred_element_type=F32)


def _dot_nt(a, b):
    return lax.dot_general(a.astype(MX), b.astype(MX), (((1,), (1,)), ((), ())), preferred_element_type=F32)


def _dot_tn(a, b):
    return lax.dot_general(a.astype(MX), b.astype(MX), (((0,), (0,)), ((), ())), preferred_element_type=F32)


def _const_spec(shape):
    nd = len(shape)
    return pl.BlockSpec(shape, lambda *_: (0,) * nd, pipeline_mode=pl.Buffered(1))


def _acc_spec(shape):
    nd = len(shape)
    return pl.BlockSpec(shape, lambda *_: (0,) * nd)


def _sds(shape, dtype):
    return jax.ShapeDtypeStruct(shape, dtype)


def _sigmoid(x):
    return jax.nn.sigmoid(x)


def _rms_fwd(x, g):
    r = lax.rsqrt(jnp.mean(x * x, axis=-1, keepdims=True) + RMS_EPS)
    xh = x * r
    return xh * g, xh, r


def _rms_bwd(dy, xh, r, g):
    t = dy * g
    dx = r * (t - xh * jnp.mean(t * xh, axis=-1, keepdims=True))
    return dx, jnp.sum(dy * xh, axis=0, keepdims=True)


_GROUPS = ((0, 512), (512, 768), (768, 1024))


def _group_rms_fwd(y, g):
    parts = [_rms_fwd(y[:, a:b], g[:, a:b]) for a, b in _GROUPS]
    return (jnp.concatenate([p[0] for p in parts], axis=1),
            jnp.concatenate([p[1] for p in parts], axis=1),
            [p[2] for p in parts])


def _gelu(x):
    c = 0.7978845608028654
    u = c * (x + 0.044715 * x * x * x)
    th = jnp.tanh(u)
    val = 0.5 * x * (1.0 + th)
    grad = 0.5 * (1.0 + th) + 0.5 * x * (1.0 - th * th) * c * (1.0 + 3.0 * 0.044715 * x * x)
    return val, grad


def _neg_expm1(x):
    series = -x * (1.0 + x * (0.5 + x * (1.0 / 6.0 + x * (1.0 / 24.0))))
    return jnp.where(x > -0.02, series, 1.0 - jnp.exp(x))


def _me():
    return lax.axis_index("x"), lax.axis_index("y"), lax.axis_index("c")


def _gather_rider(arrays):
    arrays = list(arrays)
    n = len(arrays)

    def plan(ins, outs, sems):
        ssem, rsem, lsem = sems
        x, y, c = _me()
        chips = [(1 - x, y), (x, 1 - y), (1 - x, 1 - y)]

        def copy(a, k, block, to, own=False):
            dst = outs[a].at[4 * block[0] + 2 * block[1] + block[2]]
            return pltpu.make_async_remote_copy(
                src_ref=ins[a] if own else dst, dst_ref=dst, send_sem=ssem.at[7 * a + k],
                recv_sem=rsem.at[7 * a + k], device_id=to, device_id_type=MESH)

        return x, y, c, chips, copy, lsem

    def start(ins, outs, sems):
        x, y, c, chips, copy, lsem = plan(ins, outs, sems)
        for a in range(n):
            pltpu.make_async_copy(ins[a], outs[a].at[4 * x + 2 * y + c], lsem.at[a]).start()
            copy(a, 0, (x, y, c), (x, y, 1 - c), own=True).start()
            for j, chip in enumerate(chips):
                copy(a, 1 + j, (x, y, c), (*chip, c), own=True).start()

    def mid(ins, outs, sems):
        x, y, c, chips, copy, _ = plan(ins, outs, sems)
        for a in range(n):
            for j, chip in enumerate(chips):
                copy(a, 1 + j, (*chip, c), (x, y, c)).wait_recv()
                copy(a, 4 + j, (*chip, c), (x, y, 1 - c)).start()

    def finish(ins, outs, sems):
        x, y, c, chips, copy, lsem = plan(ins, outs, sems)
        for a in range(n):
            copy(a, 0, (x, y, 1 - c), (x, y, c)).wait_recv()
            for j, chip in enumerate(chips):
                copy(a, 4 + j, (*chip, 1 - c), (x, y, c)).wait_recv()
        for a in range(n):
            copy(a, 0, (x, y, c), (x, y, 1 - c), own=True).wait_send()
            for j, chip in enumerate(chips):
                copy(a, 1 + j, (x, y, c), (*chip, c), own=True).wait_send()
                copy(a, 4 + j, (*chip, c), (x, y, 1 - c)).wait_send()
            pltpu.make_async_copy(ins[a], outs[a].at[4 * x + 2 * y + c], lsem.at[a]).wait()

    return types.SimpleNamespace(
        arrays=arrays, out_shape=[_sds((N_DEV,) + a.shape, a.dtype) for a in arrays],
        scratch=[pltpu.SemaphoreType.DMA((7 * n,)), pltpu.SemaphoreType.DMA((7 * n,)), pltpu.SemaphoreType.DMA((n,))],
        start=start, mid=mid, finish=finish)


def _scatter_rider(arrays):
    arrays = list(arrays)
    n = len(arrays)

    def copies(ins, outs, sems):
        ssem, rsem = sems
        x, y, c = _me()
        out = []
        for a in range(n):
            for f in range(1, N_DEV):
                px = 1 - x if f & 4 else x
                py = 1 - y if f & 2 else y
                pc = 1 - c if f & 1 else c
                out.append(pltpu.make_async_remote_copy(
                    src_ref=ins[a].at[4 * px + 2 * py + pc], dst_ref=outs[a].at[f - 1], send_sem=ssem.at[7 * a + f - 1],
                    recv_sem=rsem.at[7 * a + f - 1], device_id=(px, py, pc), device_id_type=MESH))
        return out

    def start(ins, outs, sems):
        for cp in copies(ins, outs, sems):
            cp.start()

    def finish(ins, outs, sems):
        for cp in copies(ins, outs, sems):
            cp.wait()

    return types.SimpleNamespace(
        arrays=arrays, out_shape=[_sds((N_DEV - 1,) + a.shape[1:], a.dtype) for a in arrays],
        scratch=[pltpu.SemaphoreType.DMA((7 * n,)), pltpu.SemaphoreType.DMA((7 * n,))],
        start=start, mid=None, finish=finish)


def _call(body, name, grid, in_specs, out_specs, out_shape, scratch, operands, riders=()):
    n_in, n_out, n_scr = len(operands), len(out_shape), len(scratch)
    nsteps = grid[0] if grid else 1
    sizes = [(len(r.arrays), len(r.out_shape), len(r.scratch)) for r in riders]

    def wrapped(*refs):
        pos = n_in
        r_ins = []
        for ri, _, _ in sizes:
            r_ins.append(refs[pos:pos + ri])
            pos += ri
        outs = refs[pos:pos + n_out]
        pos += n_out
        r_outs = []
        for _, ro, _ in sizes:
            r_outs.append(refs[pos:pos + ro])
            pos += ro
        scr = refs[pos:pos + n_scr]
        pos += n_scr
        r_sems = []
        for _, _, rs in sizes:
            r_sems.append(refs[pos:pos + rs])
            pos += rs
        step = pl.program_id(0) if grid else 0

        def at(s, fn):
            if grid:
                pl.when(step == s)(fn)
            else:
                fn()

        for r, a, b, c in zip(riders, r_ins, r_outs, r_sems):
            at(0, lambda r=r, a=a, b=b, c=c: r.start(a, b, c))
        for r, a, b, c in zip(riders, r_ins, r_outs, r_sems):
            if r.mid is not None:
                at((3 * nsteps) // 4, lambda r=r, a=a, b=b, c=c: r.mid(a, b, c))
        if body is not None:
            body(*refs[:n_in], *outs, *scr)
        for r, a, b, c in zip(riders, r_ins, r_outs, r_sems):
            at(nsteps - 1, lambda r=r, a=a, b=b, c=c: r.finish(a, b, c))

    r_arrays = [a for r in riders for a in r.arrays]
    r_shapes = [s for r in riders for s in r.out_shape]
    kwargs = {}
    if grid:
        kwargs = dict(grid=grid, compiler_params=pltpu.CompilerParams(
            dimension_semantics=("arbitrary",) * len(grid), vmem_limit_bytes=VMEM_LIMIT))
    res = pl.pallas_call(
        wrapped, name=name,
        in_specs=list(in_specs) + [ANY] * len(r_arrays),
        out_specs=list(out_specs) + [ANY] * len(r_shapes),
        out_shape=list(out_shape) + r_shapes,
        scratch_shapes=list(scratch) + [s for r in riders for s in r.scratch],
        **kwargs,
    )(*operands, *r_arrays)
    host, rest = res[:n_out], res[n_out:]
    r_res = []
    for _, ro, _ in sizes:
        r_res.append(rest[:ro])
        rest = rest[ro:]
    return host, r_res


def _ln_in(h, g1, w_in_t, keep_hn, name):
    t = h.shape[0]
    tm = _tile(t, STREAM_TILE)

    def body(h_ref, g_ref, w_ref, z_ref, *hn_ref):
        y, _, _ = _rms_fwd(h_ref[...], g_ref[...])
        hn = y.astype(MX)
        if keep_hn:
            hn_ref[0][...] = hn
        z_ref[...] = _dot_nt(hn, w_ref[...])

    tile = lambda w: pl.BlockSpec((tm, w), lambda i: (i, 0))
    outs, _ = _call(
        body, name, (t // tm,),
        [tile(D_MODEL), _const_spec((1, D_MODEL)), _const_spec((IN_W, D_MODEL))],
        [tile(IN_W)] + [tile(D_MODEL)] * keep_hn,
        [_sds((t, IN_W), F32)] + [_sds((t, D_MODEL), MX)] * keep_hn, [], [h, g1, w_in_t])
    return outs[0], (outs[1] if keep_hn else None)


def _band2(kb, g):
    lo = lax.broadcasted_iota(jnp.int32, kb.shape, 1) < HEAD_DIM
    kr = pltpu.roll(kb, HEAD_DIM, 1)
    if g == 0:
        top, bot = jnp.where(lo, kb, 0.0), jnp.where(lo, 0.0, kr)
    else:
        top, bot = jnp.where(lo, kr, 0.0), jnp.where(lo, 0.0, kb)
    return jnp.concatenate([top, bot], axis=0)


def _attn_operands(z_ref, zh_ref, b):
    rows = slice(b * BLK, (b + 1) * BLK)
    prev = zh_ref if b == 0 else z_ref
    prow = slice(0, BLK) if b == 0 else slice((b - 1) * BLK, b * BLK)
    kb = jnp.concatenate([prev[prow, K0:K0 + KV_W], z_ref[rows, K0:K0 + KV_W]], axis=0)
    vb = jnp.concatenate([prev[prow, V0:V0 + KV_W], z_ref[rows, V0:V0 + KV_W]], axis=0)
    k2 = [_band2(kb, g) for g in range(2)]
    v2 = [_band2(vb, g) for g in range(2)]
    q2 = [jnp.concatenate([z_ref[rows, (2 * g) * BLK:(2 * g + 1) * BLK], z_ref[rows, (2 * g + 1) * BLK:(2 * g + 2) * BLK]],
                          axis=0) for g in range(2)]
    return q2, k2, v2


def _attn_block(z_ref, zh_ref, sink_ref, b, first):
    q2, k2, v2 = _attn_operands(z_ref, zh_ref, b)
    rr = lax.broadcasted_iota(jnp.int32, (4 * BLK, 2 * BLK), 0) & (BLK - 1)
    cc = lax.broadcasted_iota(jnp.int32, (4 * BLK, 2 * BLK), 1)
    first_block = jnp.logical_and(first, b == 0).astype(jnp.int32)
    mask = jnp.logical_and(jnp.logical_and(cc > rr, cc <= rr + BLK), cc >= BLK * first_block)
    s = jnp.concatenate([_dot_nt(q2[g], k2[g]) for g in range(2)], axis=0) * SCALE
    w = 2 * BLK
    out, psink = [], []
    for hh in range(2):
        sh = jnp.where(mask, s[:, hh * w:(hh + 1) * w], MASK_VALUE)
        sk = jnp.concatenate([jnp.broadcast_to(sink_ref[p:p + 1, hh * w:hh * w + 1], (BLK, 1)) for p in range(4)], axis=0)
        m = jnp.maximum(jnp.max(sh, axis=1, keepdims=True), sk)
        p = jnp.exp(sh - m)
        es = jnp.exp(sk - m)
        inv = 1.0 / (jnp.sum(p, axis=1, keepdims=True) + es)
        out.append(p * inv)
        psink.append(es * inv)
    return v2, jnp.concatenate(out, axis=1), psink


def _scan_steps(a, b, n, span, reverse):
    pos = lax.broadcasted_iota(jnp.int32, a.shape, 0) & (span - 1)
    d = 1
    while d < span:
        keep = pos < span - d if reverse else pos >= d
        shift = n - d if reverse else d
        a_sh = jnp.where(keep, pltpu.roll(a, shift, 0), 1.0)
        b_sh = jnp.where(keep, pltpu.roll(b, shift, 0), 0.0)
        b = a * b_sh + b
        a = a * a_sh
        d *= 2
    return a, b


def _scan(a, b, tm, reverse):
    return _scan_steps(a, b, tm, tm, reverse)


def _shifted_copies(ext, shifts, tm):
    rows = tm + CONV_HALO - 8
    for r in range(1, 8):
        shifts[r - 1, 0:rows, :] = ext[pl.ds(r, rows), :]


def _tap(ext, shifts, off, r0, n):
    a, r = divmod(off, 8)
    lo = 8 * a + r0
    if r == 0:
        return ext[lo:lo + n, :]
    return shifts[r - 1, lo:lo + n, :]


def _glu_fill(z_ref, zh_ref, uext, ush, first, tm, sg_out=None):
    cv = z_ref[:, CV0:CV0 + CONV_W]
    sg = _sigmoid(z_ref[:, CG0:CG0 + CONV_W])
    if sg_out is not None:
        sg_out[...] = sg
    hrow = BLK - CONV_HALO
    uh = zh_ref[hrow:BLK, CV0:CV0 + CONV_W] * _sigmoid(zh_ref[hrow:BLK, CG0:CG0 + CONV_W])
    uext[0:CONV_HALO, :] = jnp.where(first, 0.0, uh)
    uext[CONV_HALO:CONV_HALO + tm, :] = cv * sg
    _shifted_copies(uext, ush, tm)


def _conv_taps(cw_ref, pv_ref, uext, ush, out_ref, tm):
    for r0 in range(0, tm, CONV_CHUNK):
        acc = jnp.broadcast_to(pv_ref[R_CONV_B:R_CONV_B + 1, :], (CONV_CHUNK, CONV_W))
        for k in range(CONV_K):
            acc = acc + cw_ref[k:k + 1, :] * _tap(uext, ush, CONV_HALO - (CONV_K - 1) + k, r0, CONV_CHUNK)
        out_ref[r0:r0 + CONV_CHUNK, :] = acc


def _ln_silu(uc, pv_ref):
    mu = jnp.mean(uc, axis=-1, keepdims=True)
    xc = uc - mu
    rs = lax.rsqrt(jnp.mean(xc * xc, axis=-1, keepdims=True) + LN_EPS)
    xh = xc * rs
    ln = xh * pv_ref[R_LN_G:R_LN_G + 1, :] + pv_ref[R_LN_B:R_LN_B + 1, :]
    sg = _sigmoid(ln)
    return xh, rs, ln, sg


def _lru_gates(z_ref, zh_ref, pv_ref, wa_ref, wx_ref, rxext, first, tm):
    rxext[0:LRU_HALO, :] = jnp.where(first, 0.0, zh_ref[BLK - LRU_HALO:BLK, RX0:RX0 + LRU_W])
    rxext[LRU_HALO:LRU_HALO + tm, :] = z_ref[:, RX0:RX0 + LRU_W]
    xc = jnp.broadcast_to(pv_ref[R_LCONV_B:R_LCONV_B + 1, :], (tm, LRU_W))
    for k in range(LRU_K):
        xc = xc + pv_ref[R_LCW + k:R_LCW + k + 1, :] * rxext[pl.ds(LRU_HALO - (LRU_K - 1) + k, tm), :]
    r = _sigmoid(_dot(xc, wa_ref[...]) + pv_ref[R_BA:R_BA + 1, :])
    ig = _sigmoid(_dot(xc, wx_ref[...]) + pv_ref[R_BX:R_BX + 1, :])
    lam = pv_ref[R_LAM:R_LAM + 1, :]
    sp = jnp.log1p(jnp.exp(-lam))
    la = (-LRU_C * r) * sp
    a = jnp.exp(la)
    mult = jnp.sqrt(_neg_expm1(2.0 * la))
    return xc, r, ig, sp, la, a, mult


def _mixer_in_specs(tm, tile_of):
    hb = tm // BLK
    return [
        pl.BlockSpec((tm, IN_W), lambda i: (tile_of(i), 0)),
        pl.BlockSpec((BLK, IN_W), lambda i: (jnp.maximum(tile_of(i) * hb - 1, 0), 0)),
        _const_spec((8, 4 * BLK)),
        _const_spec((32, CONV_W)),
        _const_spec((16, CONV_W)),
        _const_spec((LRU_W, LRU_W)),
        _const_spec((LRU_W, LRU_W)),
    ]


def _mixer_fwd(z, sink, cw, pv, wa, wx, name, riders=()):
    t = z.shape[0]
    tm = _tile(t, MIXER_TILE)
    nb = tm // BLK

    def body(z_ref, zh_ref, sink_ref, cw_ref, pv_ref, wa_ref, wx_ref, y_ref, hl_ref, uc_ref, p_ref, ps_ref,
             uext, ush, rxext, hcar):
        i = pl.program_id(0)
        first = i == 0

        @pl.when(first)
        def _():
            hcar[...] = jnp.zeros_like(hcar)

        lo = lax.broadcasted_iota(jnp.int32, (4 * BLK, BLK), 1) < HEAD_DIM
        for b in range(nb):
            rows = slice(b * BLK, (b + 1) * BLK)
            v2, prob, psink = _attn_block(z_ref, zh_ref, sink_ref, b, first)
            prob = prob.astype(MX)
            p_ref[b] = prob
            ps_ref[b] = jnp.where(lo, psink[0], psink[1])
            for g in range(2):
                o = _dot(prob[2 * g * BLK:(2 * g + 2) * BLK], v2[g])
                y_ref[rows, (2 * g) * BLK:(2 * g + 1) * BLK] = o[0:BLK]
                y_ref[rows, (2 * g + 1) * BLK:(2 * g + 2) * BLK] = o[BLK:2 * BLK]
        _glu_fill(z_ref, zh_ref, uext, ush, first, tm)
        _conv_taps(cw_ref, pv_ref, uext, ush, uc_ref, tm)
        _, _, ln, sg = _ln_silu(uc_ref[...], pv_ref)
        y_ref[:, ATTN_W:ATTN_W + CONV_W] = ln * sg
        xc, _, ig, _, _, a, mult = _lru_gates(z_ref, zh_ref, pv_ref, wa_ref, wx_ref, rxext, first, tm)
        acum, h = _scan(a, mult * (ig * xc), tm, reverse=False)
        h = h + acum * hcar[0:1, :]
        hl_ref[...] = h
        hcar[0:1, :] = h[tm - 1:tm, :]
        gl, _ = _gelu(z_ref[:, RG0:RG0 + LRU_W])
        y_ref[:, ATTN_W + CONV_W:ATTN_W + CONV_W + LRU_W] = h * gl

    tile = lambda w: pl.BlockSpec((tm, w), lambda i: (i, 0))
    return _call(
        body, name, (t // tm,), _mixer_in_specs(tm, lambda i: i),
        [tile(D_MODEL), tile(LRU_W), tile(CONV_W), pl.BlockSpec((nb, 4 * BLK, 4 * BLK), lambda i: (i, 0, 0)),
         pl.BlockSpec((nb, 4 * BLK, BLK), lambda i: (i, 0, 0))],
        [_sds((t, D_MODEL), F32), _sds((t, LRU_W), F32), _sds((t, CONV_W), F32),
         _sds((t // BLK, 4 * BLK, 4 * BLK), MX), _sds((t // BLK, 4 * BLK, BLK), F32)],
        [pltpu.VMEM((tm + CONV_HALO, CONV_W), F32), pltpu.VMEM((7, tm + CONV_HALO - 8, CONV_W), F32),
         pltpu.VMEM((tm + LRU_HALO, LRU_W), F32), pltpu.VMEM((8, LRU_W), F32)],
        [z, z, sink, cw, pv, wa, wx], riders)


def _mixer_bwd(dy, z, ycat, hl, uc, probs, psinks, sink, cw, pv, wa, wx, name, riders=()):
    t = z.shape[0]
    tm = _tile(t)
    nt = t // tm
    nb = tm // BLK
    rev = lambda i: nt - 1 - i

    def body(dy_ref, z_ref, zh_ref, sink_ref, cw_ref, pv_ref, wa_ref, wx_ref, y_ref, hl_ref, hlh_ref, uc_ref,
             p_ref, ps_ref, dz_ref, dsink_ref, dcw_ref, dpv_ref, dwa_ref, dwx_ref,
             uext, ush, sgs, rxext, dkext, dvext, ducext, dsh, dcw8, dxcext, kcar, vcar, uccar, xccar, gcar):
        i = pl.program_id(0)
        first = i == nt - 1

        @pl.when(i == 0)
        def _():
            for car in (kcar, vcar, uccar, xccar, gcar, dcw8):
                car[...] = jnp.zeros_like(car)
            for acc in (dsink_ref, dpv_ref, dwa_ref, dwx_ref):
                acc[...] = jnp.zeros_like(acc)

        def addrow(r, val):
            dpv_ref[r:r + 1, :] += jnp.sum(val, axis=0, keepdims=True)

        dkext[:, 0:tm] = jnp.zeros((KV_W, tm), F32)
        dvext[:, 0:tm] = jnp.zeros((KV_W, tm), F32)
        dkext[:, tm:tm + BLK] = kcar[...]
        dvext[:, tm:tm + BLK] = vcar[...]
        lane512 = lax.broadcasted_iota(jnp.int32, (1, 4 * BLK), 1) < 2 * BLK
        lo = lax.broadcasted_iota(jnp.int32, (4 * BLK, BLK), 1) < HEAD_DIM
        hd, w2 = HEAD_DIM, 2 * BLK
        for b in range(nb):
            rows = slice(b * BLK, (b + 1) * BLK)
            band = slice(b * BLK, (b + 2) * BLK)
            q2, k2, v2 = _attn_operands(z_ref, zh_ref, b)
            prob = p_ref[b]
            psink = [ps_ref[b, :, 0:1], ps_ref[b, :, HEAD_DIM:HEAD_DIM + 1]]
            stack = lambda ref: jnp.concatenate([ref[rows, p * BLK:(p + 1) * BLK] for p in range(4)], axis=0)
            do4 = stack(dy_ref)
            dlt = do4 * stack(y_ref)
            d0 = jnp.sum(jnp.where(lo, dlt, 0.0), axis=1, keepdims=True)
            d1 = jnp.sum(jnp.where(lo, 0.0, dlt), axis=1, keepdims=True)
            dp = jnp.concatenate([_dot_nt(do4[g * w2:(g + 1) * w2], v2[g]) for g in range(2)], axis=0)
            dl = jnp.concatenate([jnp.broadcast_to(d0, (4 * BLK, w2)), jnp.broadcast_to(d1, (4 * BLK, w2))], axis=1)
            draw = (prob * (dp - dl)) * SCALE
            e0, e1 = psink[0] * d0, psink[1] * d1
            for p in range(4):
                prs = slice(p * BLK, (p + 1) * BLK)
                s0 = jnp.sum(e0[prs], axis=0, keepdims=True)
                s1 = jnp.sum(e1[prs], axis=0, keepdims=True)
                dsink_ref[p:p + 1, :] += -jnp.where(lane512, s0, s1)
            for g in range(2):
                grs = slice(g * w2, (g + 1) * w2)
                dq = _dot(draw[grs], k2[g])
                dz_ref[rows, (2 * g) * BLK:(2 * g + 1) * BLK] = dq[0:BLK].astype(dz_ref.dtype)
                dz_ref[rows, (2 * g + 1) * BLK:(2 * g + 2) * BLK] = dq[BLK:2 * BLK].astype(dz_ref.dtype)
                tk = _dot_tn(q2[g], draw[grs])
                tv = _dot_tn(do4[grs], prob[grs])
                dkext[g * hd:(g + 1) * hd, band] += tk[0:hd, 0:w2] + tk[hd:2 * hd, w2:2 * w2]
                dvext[g * hd:(g + 1) * hd, band] += tv[0:hd, 0:w2] + tv[hd:2 * hd, w2:2 * w2]
        dz_ref[:, K0:K0 + KV_W] = jnp.transpose(dkext[:, BLK:BLK + tm]).astype(dz_ref.dtype)
        dz_ref[:, V0:V0 + KV_W] = jnp.transpose(dvext[:, BLK:BLK + tm]).astype(dz_ref.dtype)
        kcar[...] = dkext[:, 0:BLK]
        vcar[...] = dvext[:, 0:BLK]

        _glu_fill(z_ref, zh_ref, uext, ush, first, tm, sg_out=sgs)
        xh, rs, ln, sg = _ln_silu(uc_ref[...], pv_ref)
        dln = dy_ref[:, ATTN_W:ATTN_W + CONV_W] * (sg * (1.0 + ln * (1.0 - sg)))
        addrow(R_LN_G, dln * xh)
        addrow(R_LN_B, dln)
        dxh = dln * pv_ref[R_LN_G:R_LN_G + 1, :]
        duc = rs * (dxh - jnp.mean(dxh, axis=-1, keepdims=True) - xh * jnp.mean(dxh * xh, axis=-1, keepdims=True))
        addrow(R_CONV_B, duc)
        ducext[0:tm, :] = duc
        ducext[tm:tm + CONV_HALO, :] = uccar[...]
        uccar[...] = duc[0:CONV_HALO, :]
        _shifted_copies(ducext, dsh, tm)
        for r0 in range(0, tm, CONV_CHUNK):
            crow = slice(r0, r0 + CONV_CHUNK)
            duc_c = ducext[crow, :]
            du = jnp.zeros((CONV_CHUNK, CONV_W), F32)
            for k in range(CONV_K):
                prod = duc_c * _tap(uext, ush, CONV_HALO - (CONV_K - 1) + k, r0, CONV_CHUNK)
                part = prod[0:8]
                for s in range(8, CONV_CHUNK, 8):
                    part = part + prod[s:s + 8]
                dcw8[k] += part
                du = du + cw_ref[k:k + 1, :] * _tap(ducext, dsh, CONV_K - 1 - k, r0, CONV_CHUNK)
            sgc = sgs[crow, :]
            dz_ref[crow, CV0:CV0 + CONV_W] = (du * sgc).astype(dz_ref.dtype)
            u_c = uext[CONV_HALO + r0:CONV_HALO + r0 + CONV_CHUNK, :]
            dz_ref[crow, CG0:CG0 + CONV_W] = (du * u_c * (1.0 - sgc)).astype(dz_ref.dtype)

        @pl.when(i == nt - 1)
        def _():
            dcw_ref[...] = jnp.sum(dcw8[...], axis=1)

        xc, r, ig, sp, la, a, mult = _lru_gates(z_ref, zh_ref, pv_ref, wa_ref, wx_ref, rxext, first, tm)
        h = hl_ref[...]
        rowi = lax.broadcasted_iota(jnp.int32, (tm, LRU_W), 0)
        hlast = jnp.where(first, 0.0, hlh_ref[7:8, :])
        hprev = jnp.where(rowi == 0, hlast, pltpu.roll(h, 1, 0))
        dyl = dy_ref[:, ATTN_W + CONV_W:ATTN_W + CONV_W + LRU_W]
        gl, dgl = _gelu(z_ref[:, RG0:RG0 + LRU_W])
        dz_ref[:, RG0:RG0 + LRU_W] = (dyl * h * dgl).astype(dz_ref.dtype)
        dh = dyl * gl + jnp.where(rowi == tm - 1, gcar[0:1, :], 0.0)
        c = jnp.where(rowi == tm - 1, 0.0, pltpu.roll(a, tm - 1, 0))
        _, gg = _scan(c, dh, tm, reverse=True)
        gcar[0:1, :] = a[0:1, :] * gg[0:1, :]
        dmult = gg * (ig * xc)
        dig = gg * mult * xc
        dxc = gg * mult * ig
        dla = gg * hprev * a - dmult * a * a / mult
        dr = dla * (-LRU_C * sp)
        lam = pv_ref[R_LAM:R_LAM + 1, :]
        dpv_ref[R_LAM:R_LAM + 1, :] += jnp.sum(dla * (-LRU_C * r), axis=0, keepdims=True) * (-_sigmoid(-lam))
        dpa = dr * r * (1.0 - r)
        dpx = dig * ig * (1.0 - ig)
        addrow(R_BA, dpa)
        addrow(R_BX, dpx)
        dxc = dxc + _dot_nt(dpa, wa_ref[...]) + _dot_nt(dpx, wx_ref[...])
        dwa_ref[...] += _dot_tn(xc, dpa)
        dwx_ref[...] += _dot_tn(xc, dpx)
        addrow(R_LCONV_B, dxc)
        dxcext[0:tm, :] = dxc
        dxcext[tm:tm + LRU_HALO, :] = xccar[...]
        xccar[...] = dxc[0:LRU_HALO, :]
        drx = jnp.zeros((tm, LRU_W), F32)
        for k in range(LRU_K):
            addrow(R_LCW + k, dxc * rxext[pl.ds(LRU_HALO - (LRU_K - 1) + k, tm), :])
            drx = drx + pv_ref[R_LCW + k:R_LCW + k + 1, :] * dxcext[pl.ds(LRU_K - 1 - k, tm), :]
        dz_ref[:, RX0:RX0 + LRU_W] = drx.astype(dz_ref.dtype)

    tile = lambda w: pl.BlockSpec((tm, w), lambda i: (rev(i), 0))
    in_specs = [tile(D_MODEL)] + _mixer_in_specs(tm, rev) + [
        tile(D_MODEL), tile(LRU_W),
        pl.BlockSpec((8, LRU_W), lambda i: (jnp.maximum(rev(i) * (tm // 8) - 1, 0), 0)),
        tile(CONV_W), pl.BlockSpec((nb, 4 * BLK, 4 * BLK), lambda i: (rev(i), 0, 0)),
        pl.BlockSpec((nb, 4 * BLK, BLK), lambda i: (rev(i), 0, 0))]
    return _call(
        body, name, (nt,), in_specs,
        [tile(IN_W), _acc_spec((8, 4 * BLK)), _acc_spec((32, CONV_W)), _acc_spec((16, CONV_W)),
         _acc_spec((LRU_W, LRU_W)), _acc_spec((LRU_W, LRU_W))],
        [_sds((t, IN_W), MX), _sds((8, 4 * BLK), F32), _sds((32, CONV_W), F32), _sds((16, CONV_W), F32),
         _sds((LRU_W, LRU_W), F32), _sds((LRU_W, LRU_W), F32)],
        [pltpu.VMEM((tm + CONV_HALO, CONV_W), F32), pltpu.VMEM((7, tm + CONV_HALO - 8, CONV_W), F32),
         pltpu.VMEM((tm, CONV_W), F32), pltpu.VMEM((tm + LRU_HALO, LRU_W), F32),
         pltpu.VMEM((KV_W, tm + BLK), F32), pltpu.VMEM((KV_W, tm + BLK), F32),
         pltpu.VMEM((tm + CONV_HALO, CONV_W), F32), pltpu.VMEM((7, tm + CONV_HALO - 8, CONV_W), F32),
         pltpu.VMEM((32, 8, CONV_W), F32), pltpu.VMEM((tm + LRU_HALO, LRU_W), F32),
         pltpu.VMEM((KV_W, BLK), F32), pltpu.VMEM((KV_W, BLK), F32),
         pltpu.VMEM((CONV_HALO, CONV_W), F32), pltpu.VMEM((LRU_HALO, LRU_W), F32), pltpu.VMEM((8, LRU_W), F32)],
        [dy, z, z, sink, cw, pv, wa, wx, ycat, hl, hl, uc, probs, psinks], riders)


def _post_fwd(ycat, h0, gmix, w_out, g2, w_up, w_down, name, riders=()):
    t = h0.shape[0]
    tm = _tile(t, POST_TILE)
    nj = D_FF // FF_BLK

    def body(y_ref, h_ref, gm_ref, wo_ref, g2_ref, wu_ref, wd_ref, h1_ref, a_ref, h2_ref, ym_ref, hn_ref):
        ym, _, _ = _group_rms_fwd(y_ref[...], gm_ref[...])
        ym = ym.astype(MX)
        ym_ref[...] = ym
        h1 = h_ref[...] + jnp.dot(ym, wo_ref[...], preferred_element_type=F32)
        h1_ref[...] = h1
        hn, _, _ = _rms_fwd(h1, g2_ref[...])
        hn = hn.astype(MX)
        hn_ref[...] = hn
        for j in range(nj):
            u = jnp.dot(hn, wu_ref[j], preferred_element_type=F32)
            a_ref[:, j * FF_BLK:(j + 1) * FF_BLK] = jnp.square(jnp.maximum(u, 0.0)).astype(MX)
        h2_ref[...] = h1 + jnp.dot(a_ref[...], wd_ref[...], preferred_element_type=F32)

    tile = lambda w: pl.BlockSpec((tm, w), lambda i: (i, 0))
    return _call(
        body, name, (t // tm,),
        [tile(D_MODEL), tile(D_MODEL), _const_spec((1, D_MODEL)), _const_spec((D_MODEL, D_MODEL)),
         _const_spec((1, D_MODEL)), _const_spec((nj, D_MODEL, FF_BLK)), _const_spec((D_FF, D_MODEL))],
        [tile(D_MODEL), tile(D_FF), tile(D_MODEL), tile(D_MODEL), tile(D_MODEL)],
        [_sds((t, D_MODEL), F32), _sds((t, D_FF), MX), _sds((t, D_MODEL), F32), _sds((t, D_MODEL), MX),
         _sds((t, D_MODEL), MX)],
        [], [ycat, h0, gmix, w_out, g2, w_up, w_down], riders)


def _ffn_bwd(dh2, act, h1, g2, w_up_t, w_down, name, riders=()):
    t = h1.shape[0]
    tm = _tile(t, POST_TILE)
    nj = D_FF // FF_BLK

    def body(dh2_ref, a_ref, h1_ref, g2_ref, wut_ref, wd_ref, dh1_ref, dh1b_ref, dh2b_ref, du_ref, dg2_ref):
        @pl.when(pl.program_id(0) == 0)
        def _():
            dg2_ref[...] = jnp.zeros_like(dg2_ref)

        dh2 = dh2_ref[...]
        dh2b = dh2.astype(MX)
        dh2b_ref[...] = dh2b
        for j in range(nj):
            cols = slice(j * FF_BLK, (j + 1) * FF_BLK)
            da = _dot_nt(dh2b, wd_ref[j])
            du_ref[:, cols] = (da * (2.0 * jnp.sqrt(a_ref[:, cols].astype(F32)))).astype(MX)
        dhn = jnp.dot(du_ref[...], wut_ref[...], preferred_element_type=F32)
        _, xh, r = _rms_fwd(h1_ref[...], g2_ref[...])
        dx, dg = _rms_bwd(dhn, xh, r, g2_ref[...])
        dg2_ref[...] += dg
        dh1 = dh2 + dx
        dh1_ref[...] = dh1
        dh1b_ref[...] = dh1.astype(MX)

    tile = lambda w: pl.BlockSpec((tm, w), lambda i: (i, 0))
    return _call(
        body, name, (t // tm,),
        [tile(D_MODEL), tile(D_FF), tile(D_MODEL), _const_spec((1, D_MODEL)),
         _const_spec((D_FF, D_MODEL)), _const_spec((nj, FF_BLK, D_MODEL))],
        [tile(D_MODEL), tile(D_MODEL), tile(D_MODEL), tile(D_FF), _acc_spec((1, D_MODEL))],
        [_sds((t, D_MODEL), F32), _sds((t, D_MODEL), MX), _sds((t, D_MODEL), MX), _sds((t, D_FF), MX),
         _sds((1, D_MODEL), F32)],
        [], [dh2, act, h1, g2, w_up_t, w_down], riders)


def _mix_bwd(dh1, ycat, ym, gmix, w_out, name):
    t = dh1.shape[0]
    tm = _tile(t)
    nk = t // tm
    r = D_MODEL // N_DEV

    def body(dh1_ref, y_ref, ym_ref, gm_ref, wo_ref, dy_ref, dgm_ref, o_ref, o16_ref, acc):
        k = pl.program_id(0)

        @pl.when(k == 0)
        def _():
            dgm_ref[...] = jnp.zeros_like(dgm_ref)
            acc[...] = jnp.zeros_like(acc)

        dh = dh1_ref[...]
        acc[...] += _dot_tn(ym_ref[...], dh)
        dym = _dot_nt(dh, wo_ref[...])
        gm = gm_ref[...]
        _, yh, rr = _group_rms_fwd(y_ref[...], gm)
        outs, dgs = [], []
        for (a, b), rg in zip(_GROUPS, rr):
            dxg, dgg = _rms_bwd(dym[:, a:b], yh[:, a:b], rg, gm[:, a:b])
            outs.append(dxg)
            dgs.append(dgg)
        dy_ref[...] = jnp.concatenate(outs, axis=1)
        dgm_ref[...] += jnp.concatenate(dgs, axis=1)

        @pl.when(k == nk - 1)
        def _():
            for d in range(N_DEV):
                v = acc[d * r:(d + 1) * r, :]
                o_ref[d] = v
                o16_ref[d] = v.astype(o16_ref.dtype)

    tile = pl.BlockSpec((tm, D_MODEL), lambda i: (i, 0))
    slabs = _const_spec((N_DEV, r, D_MODEL))
    (dy, dgm, dw, dw16), _ = _call(
        body, name, (nk,), [tile, tile, tile, _const_spec((1, D_MODEL)), _const_spec((D_MODEL, D_MODEL))],
        [tile, _acc_spec((1, D_MODEL)), slabs, slabs],
        [_sds((t, D_MODEL), F32), _sds((1, D_MODEL), F32), _sds((N_DEV, r, D_MODEL), F32),
         _sds((N_DEV, r, D_MODEL), WIRE)],
        [pltpu.VMEM((D_MODEL, D_MODEL), F32)], [dh1, ycat, ym, gmix, w_out])
    return dy, dgm, (dw, dw16)


def _in_bwd(dz, h0, dh1, g1, w_in_t, after, name):
    t = h0.shape[0]
    tm = _tile(t, STREAM_TILE)

    def body(dz_ref, h_ref, dh1_ref, g_ref, w_ref, after_ref, dh0_ref, dg_ref):
        @pl.when(pl.program_id(0) == 0)
        def _():
            dg_ref[...] = jnp.zeros_like(dg_ref)

        dhn = _dot(dz_ref[...], w_ref[...])
        _, xh, r = _rms_fwd(h_ref[...], g_ref[...])
        dx, dg = _rms_bwd(dhn, xh, r, g_ref[...])
        dg_ref[...] += dg
        dh0_ref[...] = dh1_ref[...] + dx

    tile = lambda w: pl.BlockSpec((tm, w), lambda i: (i, 0))
    (dh0, dg), _ = _call(
        body, name, (t // tm,),
        [tile(IN_W), tile(D_MODEL), tile(D_MODEL), _const_spec((1, D_MODEL)), _const_spec((IN_W, D_MODEL)),
         _const_spec((8, 128))],
        [tile(D_MODEL), _acc_spec((1, D_MODEL))], [_sds((t, D_MODEL), F32), _sds((1, D_MODEL), F32)],
        [], [dz, h0, dh1, g1, w_in_t, after])
    return dh0, dg


def _in_bwd_dw(dz, h0, dh1, g1, w_in_t, name):
    t = h0.shape[0]
    tm = _tile(t)
    nk = t // tm

    def body(dz_ref, h_ref, dh1_ref, g_ref, w_ref, dh0_ref, dg_ref, o_ref, o16_ref, acc):
        k = pl.program_id(0)

        @pl.when(k == 0)
        def _():
            dg_ref[...] = jnp.zeros_like(dg_ref)
            acc[...] = jnp.zeros_like(acc)

        dz_t = dz_ref[...]
        hn, xh, r = _rms_fwd(h_ref[...], g_ref[...])
        acc[...] += _dot_tn(dz_t, hn)
        dhn = _dot(dz_t, w_ref[...])
        dx, dg = _rms_bwd(dhn, xh, r, g_ref[...])
        dg_ref[...] += dg
        dh0_ref[...] = dh1_ref[...] + dx

        @pl.when(k == nk - 1)
        def _():
            for d in range(N_DEV):
                v = acc[d * IN_SHARD:(d + 1) * IN_SHARD, :]
                o_ref[d] = v
                o16_ref[d] = v.astype(o16_ref.dtype)

    tile = lambda w: pl.BlockSpec((tm, w), lambda i: (i, 0))
    slabs = _const_spec((N_DEV, IN_SHARD, D_MODEL))
    (dh0, dg, dw, dw16), _ = _call(
        body, name, (nk,),
        [tile(IN_W), tile(D_MODEL), tile(D_MODEL), _const_spec((1, D_MODEL)), _const_spec((IN_W, D_MODEL))],
        [tile(D_MODEL), _acc_spec((1, D_MODEL)), slabs, slabs],
        [_sds((t, D_MODEL), F32), _sds((1, D_MODEL), F32), _sds((N_DEV, IN_SHARD, D_MODEL), F32),
         _sds((N_DEV, IN_SHARD, D_MODEL), WIRE)],
        [pltpu.VMEM((IN_W, D_MODEL), F32)], [dz, h0, dh1, g1, w_in_t])
    return dh0, dg, (dw, dw16)


def _loss_head(h, gf, target, name):
    t = h.shape[0]
    tm = _tile(t, STREAM_TILE)

    def body(h_ref, g_ref, t_ref, dh_ref, loss_ref, dg_ref):
        @pl.when(pl.program_id(0) == 0)
        def _():
            loss_ref[...] = jnp.zeros_like(loss_ref)
            dg_ref[...] = jnp.zeros_like(dg_ref)

        g = g_ref[...]
        y, xh, r = _rms_fwd(h_ref[...], g)
        err = y - t_ref[...]
        part = 0.5 * jnp.sum(jnp.mean(err * err, axis=-1, keepdims=True), axis=0, keepdims=True)
        loss_ref[...] += jnp.broadcast_to(part, loss_ref.shape)
        dx, dg = _rms_bwd(err * (1.0 / D_MODEL), xh, r, g)
        dg_ref[...] += dg
        dh_ref[...] = dx

    tile = pl.BlockSpec((tm, D_MODEL), lambda i: (i, 0))
    (dh, loss, dg), _ = _call(
        body, name, (t // tm,), [tile, _const_spec((1, D_MODEL)), tile],
        [tile, _acc_spec((1, 128)), _acc_spec((1, D_MODEL))],
        [_sds((t, D_MODEL), F32), _sds((1, 128), F32), _sds((1, D_MODEL), F32)], [], [h, gf, target])
    return dh, loss, dg


def _dw(x, y, name, split, bm, bn):
    t, m = x.shape
    n = y.shape[1]
    tk = _tile(t, DW_TILE)
    nk = t // tk
    if split == "rows":
        assert bn == n
        r, c = m // N_DEV, n
        per = bm // r
        out_block = pl.BlockSpec((per, r, c), lambda a, b, k: (a, 0, 0))
    else:
        assert bm == m
        r, c = m, n // N_DEV
        per = bn // c
        out_block = pl.BlockSpec((per, r, c), lambda a, b, k: (b, 0, 0))

    def body(x_ref, y_ref, o_ref, o16_ref, acc):
        k = pl.program_id(2)

        @pl.when(k == 0)
        def _():
            acc[...] = jnp.zeros_like(acc)

        acc[...] += _dot_tn(x_ref[...], y_ref[...])

        @pl.when(k == nk - 1)
        def _():
            for d in range(per):
                v = acc[d * r:(d + 1) * r, :] if split == "rows" else acc[:, d * c:(d + 1) * c]
                o_ref[d] = v
                o16_ref[d] = v.astype(o16_ref.dtype)

    return pl.pallas_call(
        body, name=name, grid=(m // bm, n // bn, nk),
        in_specs=[pl.BlockSpec((tk, bm), lambda a, b, k: (k, a)), pl.BlockSpec((tk, bn), lambda a, b, k: (k, b))],
        out_specs=[out_block, out_block],
        out_shape=[_sds((N_DEV, r, c), F32), _sds((N_DEV, r, c), WIRE)],
        scratch_shapes=[pltpu.VMEM((bm, bn), F32)],
        compiler_params=pltpu.CompilerParams(dimension_semantics=("arbitrary",) * 3, vmem_limit_bytes=VMEM_LIMIT),
    )(x, y)


def _adamw_math(w, g, m, v):
    m = ADAM_B1 * m + (1.0 - ADAM_B1) * g
    v = ADAM_B2 * v + (1.0 - ADAM_B2) * jnp.square(g)
    m_hat = m / (1.0 - ADAM_B1 ** ADAM_STEP)
    v_hat = v / (1.0 - ADAM_B2 ** ADAM_STEP)
    delta = -ADAM_LR * (m_hat / (jnp.sqrt(v_hat) + ADAM_EPS) + ADAM_WD * w)
    return delta, m, v


def _adamw_shard(g_own, g_recv, dev, w, m, v, after, name):
    _, r, c = w.shape
    br = r
    for cand in (256, 128, 112, 64, 56, 32, 16, 8):
        if r % cand == 0:
            br = cand
            break
    nr = r // br
    own = lambda l: pl.BlockSpec((1, br, c), lambda ll, i, d: (d[0], jnp.where(ll == l, i, (nr - 1) * (1 - l)), 0))
    recv = lambda l: pl.BlockSpec((N_DEV - 1, br, c), lambda ll, i, d: (0, jnp.where(ll == l, i, (nr - 1) * (1 - l)), 0))

    def body(dev_ref, go0, gr0, go1, gr1, w_ref, m_ref, v_ref, after_ref, g_out, d_out, m_out, v_out):
        def update(go_ref, gr_ref):
            g = go_ref[0]
            for j in range(N_DEV - 1):
                g = g + gr_ref[j].astype(F32)
            delta, mn, vn = _adamw_math(w_ref[0], g, m_ref[0], v_ref[0])
            g_out[0] = g
            d_out[0] = delta
            m_out[0] = mn
            v_out[0] = vn

        layer = pl.program_id(0)
        pl.when(layer == 0)(lambda: update(go0, gr0))
        pl.when(layer == 1)(lambda: update(go1, gr1))

    tile = pl.BlockSpec((1, br, c), lambda ll, i, d: (ll, i, 0))
    return pl.pallas_call(
        body, name=name,
        grid_spec=pltpu.PrefetchScalarGridSpec(
            num_scalar_prefetch=1, grid=(2, nr),
            in_specs=[own(0), recv(0), own(1), recv(1), tile, tile, tile,
                      pl.BlockSpec((8, 128), lambda ll, i, d: (0, 0))],
            out_specs=[tile, tile, tile, tile]),
        out_shape=[_sds((2, r, c), F32)] * 4,
        compiler_params=pltpu.CompilerParams(dimension_semantics=("arbitrary",) * 2, vmem_limit_bytes=VMEM_LIMIT),
    )(dev, g_own[0], g_recv[0], g_own[1], g_recv[1], w, m, v, after)


def _adamw_small(gs, ws, ms, vs, name):
    n = len(gs)

    def body(*refs):
        g_refs, w_refs, m_refs, v_refs = (refs[k * n:(k + 1) * n] for k in range(4))
        outs = refs[4 * n:]
        for k in range(n):
            delta, mn, vn = _adamw_math(w_refs[k][...], g_refs[k][...], m_refs[k][...], v_refs[k][...])
            outs[k][...] = delta
            outs[n + k][...] = mn
            outs[2 * n + k][...] = vn

    shapes = [_sds(w.shape, F32) for w in ws]
    res = pl.pallas_call(body, name=name, out_shape=shapes * 3,
                         compiler_params=pltpu.CompilerParams(vmem_limit_bytes=VMEM_LIMIT))(*gs, *ws, *ms, *vs)
    return res[:n], res[n:2 * n], res[2 * n:]


def _sum_parts(own, recv, dev, name):
    def body(dev_ref, own_ref, recv_ref, o_ref):
        me = dev_ref[0]

        def block(d):
            f = jnp.bitwise_xor(me, d)
            return jnp.where(f == 0, own_ref[...], recv_ref[jnp.maximum(f - 1, 0)])

        g = block(0)
        for d in range(1, N_DEV):
            g = g + block(d)
        o_ref[...] = g

    return pl.pallas_call(
        body, name=name,
        grid_spec=pltpu.PrefetchScalarGridSpec(
            num_scalar_prefetch=1, grid=(1,),
            in_specs=[pl.BlockSpec(own.shape, lambda i, d: (0, 0)), pl.BlockSpec(recv.shape, lambda i, d: (0, 0, 0))],
            out_specs=pl.BlockSpec(own.shape, lambda i, d: (0, 0))),
        out_shape=_sds(own.shape, F32))(dev, own, recv)


HBM = pl.BlockSpec(memory_space=pltpu.HBM)
SEM = pl.BlockSpec(memory_space=pltpu.SEMAPHORE)
EFFECT = pltpu.SideEffectType.DATAFLOW_SIDE_EFFECTING


def _direct_copies(srcs, lands, ssem, rsem, scatter):
    x, y, c = _me()
    out = []
    for a in range(len(srcs)):
        for f in range(1, N_DEV):
            px = 1 - x if f & 4 else x
            py = 1 - y if f & 2 else y
            pc = 1 - c if f & 1 else c
            out.append(pltpu.make_async_remote_copy(
                src_ref=srcs[a].at[4 * px + 2 * py + pc] if scatter else srcs[a], dst_ref=lands[a].at[f - 1],
                send_sem=ssem.at[7 * a + f - 1], recv_sem=rsem.at[7 * a + f - 1],
                device_id=(px, py, pc), device_id_type=MESH))
    return out


def _send_start(arrays, scatter, name):
    arrays = list(arrays)
    n = len(arrays)
    lands = [lax.empty((N_DEV - 1,) + (a.shape[1:] if scatter else a.shape), a.dtype) for a in arrays]

    def body(*refs):
        srcs, lnds, ssem, rsem, token = refs[:n], refs[n:2 * n], refs[2 * n], refs[2 * n + 1], refs[-1]
        for cp in _direct_copies(srcs, lnds, ssem, rsem, scatter):
            cp.start()
        token[...] = jnp.zeros_like(token)

    hbm = lambda a: pltpu.HBM(a.shape, a.dtype)
    res = pl.pallas_call(
        body, name=name,
        out_shape=(pltpu.SemaphoreType.DMA((7 * n,)), pltpu.SemaphoreType.DMA((7 * n,)),
                   *[hbm(a) for a in arrays + lands], _sds((8, 128), F32)),
        in_specs=[HBM] * (2 * n),
        out_specs=(SEM, SEM, *[HBM] * (2 * n), pl.BlockSpec(memory_space=pltpu.VMEM)),
        input_output_aliases={i: 2 + i for i in range(2 * n)},
        compiler_params=pltpu.CompilerParams(has_side_effects=EFFECT),
    )(*[pltpu.with_memory_space_constraint(a, pltpu.HBM) for a in arrays + lands])
    return types.SimpleNamespace(ssem=res[0], rsem=res[1], srcs=list(res[2:2 + n]), lands=list(res[2 + n:2 + 2 * n]),
                                 token=res[-1], scatter=scatter)


def _send_wait(h, after, name):
    n = len(h.srcs)

    def body(*refs):
        srcs, lnds, ssem, rsem = refs[:n], refs[n:2 * n], refs[2 * n], refs[2 * n + 1]
        for cp in _direct_copies(srcs, lnds, ssem, rsem, h.scatter):
            cp.wait_send()
            cp.wait_recv()

    hbm = lambda a: pltpu.HBM(a.shape, a.dtype)
    res = pl.pallas_call(
        body, name=name,
        out_shape=tuple(hbm(a) for a in h.srcs + h.lands),
        in_specs=[HBM] * (2 * n) + [SEM, SEM, ANY], out_specs=[HBM] * (2 * n),
        input_output_aliases={i: i for i in range(2 * n)},
        compiler_params=pltpu.CompilerParams(has_side_effects=EFFECT),
    )(*h.srcs, *h.lands, h.ssem, h.rsem, after)
    return list(res[:n]), list(res[n:])


def _block_diag(w):
    out = jnp.zeros((LRU_W, LRU_W), w.dtype)
    for h in range(4):
        out = lax.dynamic_update_slice(out, w[h], (h * 64, h * 64))
    return out


def _unblock_diag(w):
    return jnp.concatenate([w[h * 64:(h + 1) * 64, h * 64:(h + 1) * 64] for h in range(4)], axis=0)


def _layer_params(p, l):
    row = lambda a: a[l].reshape(1, -1)
    sink_rows = jnp.repeat(p["attn_sinks"][l].reshape(4, 2), 2 * BLK, axis=1)
    sink_rows = jnp.concatenate([sink_rows, jnp.zeros((4, 4 * BLK), F32)], axis=0)
    cw = jnp.concatenate([p["conv_dw_w"][l], jnp.zeros((1, CONV_W), F32)], axis=0)
    pv = jnp.concatenate([
        row(p["conv_dw_b"]), row(p["conv_ln_g"]), row(p["conv_ln_b"]), row(p["lru_conv_b"]), row(p["lru_ba"]),
        row(p["lru_bx"]), row(p["lru_lambda"]), jnp.zeros((1, LRU_W), F32), p["lru_conv_w"][l],
        jnp.zeros((4, LRU_W), F32)], axis=0)
    return dict(
        g1=row(p["norm1"]), sink=sink_rows, cw=cw, pv=pv,
        wa=_block_diag(p["lru_wa"][l]).astype(MX), wx=_block_diag(p["lru_wx"][l]).astype(MX),
        gmix=row(p["mix_norm"]), g2=row(p["norm2"]))


_SMALL = ["norm1", "attn_sinks", "conv_dw_w", "conv_dw_b", "conv_ln_g", "conv_ln_b", "lru_conv_w", "lru_conv_b",
          "lru_wa", "lru_ba", "lru_wx", "lru_bx", "lru_lambda", "mix_norm", "norm2"]
_BIG = ["w_in", "w_out", "w_up", "w_down"]
_WEIGHTS = ["norm1", "w_in", "attn_sinks", "conv_dw_w", "conv_dw_b", "conv_ln_g", "conv_ln_b", "lru_conv_w",
            "lru_conv_b", "lru_wa", "lru_ba", "lru_wx", "lru_bx", "lru_lambda", "mix_norm", "w_out", "norm2", "w_up",
            "w_down", "final_norm"]


def kernel(x, norm1, w_in, attn_sinks, conv_dw_w, conv_dw_b, conv_ln_g, conv_ln_b, lru_conv_w, lru_conv_b, lru_wa, lru_ba, lru_wx, lru_bx, lru_lambda, mix_norm, w_out, norm2, w_up, w_down, final_norm, loss_target, m_norm1, m_w_in, m_attn_sinks, m_conv_dw_w, m_conv_dw_b, m_conv_ln_g, m_conv_ln_b, m_lru_conv_w, m_lru_conv_b, m_lru_wa, m_lru_ba, m_lru_wx, m_lru_bx, m_lru_lambda, m_mix_norm, m_w_out, m_norm2, m_w_up, m_w_down, m_final_norm, v_norm1, v_w_in, v_attn_sinks, v_conv_dw_w, v_conv_dw_b, v_conv_ln_g, v_conv_ln_b, v_lru_conv_w, v_lru_conv_b, v_lru_wa, v_lru_ba, v_lru_wx, v_lru_bx, v_lru_lambda, v_mix_norm, v_w_out, v_norm2, v_w_up, v_w_down, v_final_norm):
    w = dict(norm1=norm1, w_in=w_in, attn_sinks=attn_sinks, conv_dw_w=conv_dw_w, conv_dw_b=conv_dw_b,
             conv_ln_g=conv_ln_g, conv_ln_b=conv_ln_b, lru_conv_w=lru_conv_w, lru_conv_b=lru_conv_b, lru_wa=lru_wa,
             lru_ba=lru_ba, lru_wx=lru_wx, lru_bx=lru_bx, lru_lambda=lru_lambda, mix_norm=mix_norm, w_out=w_out,
             norm2=norm2, w_up=w_up, w_down=w_down, final_norm=final_norm)
    m = dict(norm1=m_norm1, w_in=m_w_in, attn_sinks=m_attn_sinks, conv_dw_w=m_conv_dw_w, conv_dw_b=m_conv_dw_b,
             conv_ln_g=m_conv_ln_g, conv_ln_b=m_conv_ln_b, lru_conv_w=m_lru_conv_w, lru_conv_b=m_lru_conv_b,
             lru_wa=m_lru_wa, lru_ba=m_lru_ba, lru_wx=m_lru_wx, lru_bx=m_lru_bx, lru_lambda=m_lru_lambda,
             mix_norm=m_mix_norm, w_out=m_w_out, norm2=m_norm2, w_up=m_w_up, w_down=m_w_down, final_norm=m_final_norm)
    v = dict(norm1=v_norm1, w_in=v_w_in, attn_sinks=v_attn_sinks, conv_dw_w=v_conv_dw_w, conv_dw_b=v_conv_dw_b,
             conv_ln_g=v_conv_ln_g, conv_ln_b=v_conv_ln_b, lru_conv_w=v_lru_conv_w, lru_conv_b=v_lru_conv_b,
             lru_wa=v_lru_wa, lru_ba=v_lru_ba, lru_wx=v_lru_wx, lru_bx=v_lru_bx, lru_lambda=v_lru_lambda,
             mix_norm=v_mix_norm, w_out=v_w_out, norm2=v_norm2, w_up=v_w_up, w_down=v_w_down, final_norm=v_final_norm)
    depth = w_in.shape[0]
    xi, yi, ci = _me()
    dev = (4 * xi + 2 * yi + ci).astype(jnp.int32)
    dev1 = dev.reshape(1)
    tr = lambda a: jnp.swapaxes(a, 1, 2)
    w_t, m_t, v_t = tr(w_in), tr(m_w_in), tr(v_w_in)
    wb = {n: w[n].astype(MX) for n in _BIG if n != "w_in"}
    wb["w_in"] = w_t.astype(MX)
    layer_shards = lambda l: [wb["w_out"][l], wb["w_up"][l], wb["w_down"][l]]

    _, ((g_in0, g_cw, g_lcw),) = _call(None, "gather_first", None, [], [], [], [], [],
                                        [_gather_rider([wb["w_in"][0], conv_dw_w, lru_conv_w])])
    cols = lambda g: jnp.moveaxis(g, 0, -2).reshape(g.shape[1:-1] + (N_DEV * g.shape[-1],))
    p = dict(w)
    p["conv_dw_w"] = cols(g_cw)
    p["lru_conv_w"] = cols(g_lcw)
    lp = [_layer_params(p, l) for l in range(depth)]

    gathered = [dict(w_in=g_in0.reshape(IN_W, D_MODEL)), dict()]
    saved = []
    h = x[0]
    for l in range(depth):
        q, gw = lp[l], gathered[l]
        z, hn1 = _ln_in(h, q["g1"], gw["w_in"], l == 0, f"ln_in{l}")
        riders = [_gather_rider(layer_shards(0))] if l == 0 else []
        (ycat, hl, uc, probs, psinks), got = _mixer_fwd(z, q["sink"], q["cw"], q["pv"], q["wa"], q["wx"],
                                                        f"mixer_fwd{l}", riders)
        if l == 0:
            gw["w_out"], gw["w_up"], gw["w_down"] = got[0]
            gw["w_out"] = gw["w_out"].reshape(D_MODEL, D_MODEL)
        riders = [_gather_rider([wb["w_in"][1]] + layer_shards(1))] if l == 0 else []
        (h1, act, h2, ym, hn2), got = _post_fwd(ycat, h, q["gmix"], gw["w_out"], q["g2"], gw["w_up"],
                                                gw["w_down"].reshape(D_FF, D_MODEL), f"post_fwd{l}", riders)
        if l == 0:
            nxt = gathered[1]
            nxt["w_in"], nxt["w_out"], nxt["w_up"], nxt["w_down"] = got[0]
            nxt["w_in"] = nxt["w_in"].reshape(IN_W, D_MODEL)
            nxt["w_out"] = nxt["w_out"].reshape(D_MODEL, D_MODEL)
        saved.append(dict(h0=h, z=z, hn1=hn1, ycat=ycat, hl=hl, uc=uc, probs=probs, psinks=psinks, h1=h1, act=act,
                          ym=ym, hn2=hn2))
        h = h2
    dh, loss, dgf = _loss_head(h, final_norm.reshape(1, -1), loss_target[0], "loss_head")

    grads = [None] * depth
    big = {n: [None] * depth for n in _BIG}
    pending = []

    def send_pending():
        riders = [_scatter_rider([item[3] for item in pending])] if pending else []
        return riders, list(pending)

    def record(sent, got):
        for item, recv in zip(sent, got[0] if sent else []):
            big[item[0]][item[1]] = (item[2], recv)
        del pending[:len(sent)]

    for l in reversed(range(depth)):
        q, s, gw = lp[l], saved[l], gathered[l]
        riders, sent = send_pending()
        w_up_t = jnp.swapaxes(gw["w_up"], 1, 2).reshape(D_FF, D_MODEL)
        (dh1, dh1b, dhb, du, dg2), got = _ffn_bwd(dh, s["act"], s["h1"], q["g2"], w_up_t, gw["w_down"],
                                                  f"ffn_bwd{l}", riders)
        record(sent, got)
        dycat, dgm, d_wout = _mix_bwd(dh1b, s["ycat"], s["ym"], q["gmix"], gw["w_out"], f"mix_bwd{l}")
        pending.append(("w_down", l) + tuple(_dw(s["act"], dhb, f"dw_down{l}", "rows", 2048, D_MODEL)))
        pending.append(("w_up", l) + tuple(_dw(s["hn2"], du, f"dw_up{l}", "cols", D_MODEL, 2048)))
        pending.append(("w_out", l) + tuple(d_wout))
        riders, sent = send_pending()
        (dz, dsink, dcw, dpv, dwa, dwx), got = _mixer_bwd(
            dycat, s["z"], s["ycat"], s["hl"], s["uc"], s["probs"], s["psinks"], q["sink"], q["cw"], q["pv"], q["wa"],
            q["wx"], f"mixer_bwd{l}", riders)
        record(sent, got)
        if l > 0:
            dh, dg1, d_win = _in_bwd_dw(dz, s["h0"], dh1, q["g1"], gw["w_in"], f"in_bwd{l}")
            pending.append(("w_in", l) + tuple(d_win))
        else:
            d_win = _dw(dz, s["hn1"], f"dw_in{l}", "rows", IN_W, D_MODEL)
            win_sends = _send_start([d_win[1]], True, "scatter_w_in0_start")
            dh, dg1 = _in_bwd(dz, s["h0"], dh1, q["g1"], gw["w_in"], win_sends.token, f"in_bwd{l}")
        grads[l] = dict(
            norm1=dg1[0], attn_sinks=jnp.stack([dsink[0:4, 0], dsink[0:4, 2 * BLK]], axis=1).reshape(8),
            conv_dw_w=dcw[0:CONV_K], conv_dw_b=dpv[R_CONV_B], conv_ln_g=dpv[R_LN_G], conv_ln_b=dpv[R_LN_B],
            lru_conv_w=dpv[R_LCW:R_LCW + LRU_K], lru_conv_b=dpv[R_LCONV_B], lru_wa=_unblock_diag(dwa),
            lru_ba=dpv[R_BA].reshape(4, 64), lru_wx=_unblock_diag(dwx), lru_bx=dpv[R_BX].reshape(4, 64),
            lru_lambda=dpv[R_LAM], mix_norm=dgm[0], norm2=dg2[0])

    small = [jnp.stack([grads[l][n] for l in range(depth)]) for n in _SMALL] + [dgf, loss[:, 0:1]]

    def as_rows(a):
        flat = a.reshape(-1)
        pad = (-flat.size) % 1024
        if pad:
            flat = jnp.concatenate([flat, jnp.zeros((pad,), F32)])
        return flat.reshape(-1, 128)

    pieces = [as_rows(a) for a in small]
    packed = jnp.concatenate(pieces, axis=0)
    small_sends = _send_start([packed], False, "bcast_small_start")

    out = {}
    shard_update = lambda n, wmv, after: list(_adamw_shard(
        [big[n][l][0] for l in range(depth)], [big[n][l][1] for l in range(depth)], dev1, *wmv, after, f"adamw_{n}"))
    for n in ("w_out", "w_up", "w_down"):
        out[n] = shard_update(n, (w[n], m[n], v[n]), small_sends.token)
    _, (win_recv,) = _send_wait(win_sends, out["w_down"][1], "scatter_w_in0_wait")
    big["w_in"][0] = (d_win[0], win_recv)
    out["w_in"] = [tr(a) for a in shard_update("w_in", (w_t, m_t, v_t), jnp.zeros((8, 128), F32))]
    (packed,), (small_recv,) = _send_wait(small_sends, out["w_in"][1], "bcast_small_wait")
    summed = _sum_parts(packed, small_recv, dev1, "sum_small_grads")
    small_sums, row = [], 0
    for a, piece in zip(small, pieces):
        got = summed[row:row + piece.shape[0]]
        small_sums.append(got.reshape(a.shape) if a.size == piece.size else got.reshape(-1)[:a.size].reshape(a.shape))
        row += piece.shape[0]
    shard = lambda a: lax.dynamic_slice_in_dim(a, dev * (a.shape[-1] // N_DEV), a.shape[-1] // N_DEV, axis=a.ndim - 1)
    flat = {"lru_wa": (depth, LRU_W, 64), "lru_wx": (depth, LRU_W, 64), "final_norm": (1, D_MODEL)}
    gs, ws, ms, vs = [], [], [], []
    for n, g in zip(_SMALL + ["final_norm"], small_sums[:-1]):
        shp = flat.get(n, w[n].shape)
        gs.append((shard(g) if n in ("conv_dw_w", "lru_conv_w") else g).reshape(shp))
        ws.append(w[n].reshape(shp))
        ms.append(m[n].reshape(shp))
        vs.append(v[n].reshape(shp))
    sd, sm, sv = _adamw_small(gs, ws, ms, vs, "adamw_small")
    for j, n in enumerate(_SMALL + ["final_norm"]):
        out[n] = [a.reshape(w[n].shape) for a in (gs[j], sd[j], sm[j], sv[j])]
    loss_total = small_sums[-1][0, 0]

    result = [loss_total, dh[None]]
    for j in range(4):
        result += [out[n][j] for n in _WEIGHTS]
    return tuple(result)
```

```python
import types

import jax
import jax.numpy as jnp
from jax import lax
from jax.experimental import pallas as pl
from jax.experimental.pallas import tpu as pltpu

F32 = jnp.float32
MX = jnp.bfloat16
WIRE = jnp.bfloat16

D_MODEL = 1024
HEAD_DIM = 64
ATTN_W = 512
KV_W = 128
BLK = 128
CONV_W = 256
CONV_K = 31
LRU_W = 256
LRU_K = 4
LRU_C = 8.0
IN_W = 1792
D_FF = 4096
FF_BLK = 512
N_DEV = 8
IN_SHARD = IN_W // N_DEV
RMS_EPS = 1e-6
LN_EPS = 1e-5
MASK_VALUE = -1e30
SCALE = HEAD_DIM ** -0.5
CONV_HALO = 32
LRU_HALO = 8
CONV_CHUNK = 64
POST_TILE = 512
STREAM_TILE = 1024
DW_TILE = 1024
Q0, K0, V0, CV0, CG0, RX0, RG0 = 0, 512, 640, 768, 1024, 1280, 1536
R_CONV_B, R_LN_G, R_LN_B, R_LCONV_B, R_BA, R_BX, R_LAM, R_LCW = 0, 1, 2, 3, 4, 5, 6, 8

ADAM_LR, ADAM_B1, ADAM_B2, ADAM_EPS, ADAM_WD, ADAM_STEP = 0.001, 0.9, 0.999, 1e-08, 0.01, 10

VMEM_LIMIT = 56 * 1024 * 1024
MESH = pl.DeviceIdType.MESH
ANY = pl.BlockSpec(memory_space=pl.ANY)


def _tile(t, cap=512):
    return min(cap, t)


def _dot(a, b):
    return jnp.dot(a.astype(MX), b.astype(MX), preferred_element_type=F32)


def _dot_nt(a, b):
    return lax.dot_general(a.astype(MX), b.astype(MX), (((1,), (1,)), ((), ())), preferred_element_type=F32)


def _dot_tn(a, b):
    return lax.dot_general(a.astype(MX), b.astype(MX), (((0,), (0,)), ((), ())), preferred_element_type=F32)


def _const_spec(shape):
    nd = len(shape)
    return pl.BlockSpec(shape, lambda *_: (0,) * nd, pipeline_mode=pl.Buffered(1))


def _acc_spec(shape):
    nd = len(shape)
    return pl.BlockSpec(shape, lambda *_: (0,) * nd)


def _sds(shape, dtype):
    return jax.ShapeDtypeStruct(shape, dtype)


def _sigmoid(x):
    return jax.nn.sigmoid(x)


def _rms_fwd(x, g):
    r = lax.rsqrt(jnp.mean(x * x, axis=-1, keepdims=True) + RMS_EPS)
    xh = x * r
    return xh * g, xh, r


def _rms_bwd(dy, xh, r, g):
    t = dy * g
    dx = r * (t - xh * jnp.mean(t * xh, axis=-1, keepdims=True))
    return dx, jnp.sum(dy * xh, axis=0, keepdims=True)


_GROUPS = ((0, 512), (512, 768), (768, 1024))


def _group_rms_fwd(y, g):
    parts = [_rms_fwd(y[:, a:b], g[:, a:b]) for a, b in _GROUPS]
    return (jnp.concatenate([p[0] for p in parts], axis=1),
            jnp.concatenate([p[1] for p in parts], axis=1),
            [p[2] for p in parts])


def _gelu(x):
    c = 0.7978845608028654
    u = c * (x + 0.044715 * x * x * x)
    th = jnp.tanh(u)
    val = 0.5 * x * (1.0 + th)
    grad = 0.5 * (1.0 + th) + 0.5 * x * (1.0 - th * th) * c * (1.0 + 3.0 * 0.044715 * x * x)
    return val, grad


def _neg_expm1(x):
    series = -x * (1.0 + x * (0.5 + x * (1.0 / 6.0 + x * (1.0 / 24.0))))
    return jnp.where(x > -0.02, series, 1.0 - jnp.exp(x))


def _me():
    return lax.axis_index("x"), lax.axis_index("y"), lax.axis_index("c")


def _gather_rider(arrays):
    arrays = list(arrays)
    n = len(arrays)

    def plan(ins, outs, sems):
        ssem, rsem, lsem = sems
        x, y, c = _me()
        chips = [(1 - x, y), (x, 1 - y), (1 - x, 1 - y)]

        def copy(a, k, block, to, own=False):
            dst = outs[a].at[4 * block[0] + 2 * block[1] + block[2]]
            return pltpu.make_async_remote_copy(
                src_ref=ins[a] if own else dst, dst_ref=dst, send_sem=ssem.at[7 * a + k],
                recv_sem=rsem.at[7 * a + k], device_id=to, device_id_type=MESH)

        return x, y, c, chips, copy, lsem

    def start(ins, outs, sems):
        x, y, c, chips, copy, lsem = plan(ins, outs, sems)
        for a in range(n):
            pltpu.make_async_copy(ins[a], outs[a].at[4 * x + 2 * y + c], lsem.at[a]).start()
            copy(a, 0, (x, y, c), (x, y, 1 - c), own=True).start()
            for j, chip in enumerate(chips):
                copy(a, 1 + j, (x, y, c), (*chip, c), own=True).start()

    def mid(ins, outs, sems):
        x, y, c, chips, copy, _ = plan(ins, outs, sems)
        for a in range(n):
            for j, chip in enumerate(chips):
                copy(a, 1 + j, (*chip, c), (x, y, c)).wait_recv()
                copy(a, 4 + j, (*chip, c), (x, y, 1 - c)).start()

    def finish(ins, outs, sems):
        x, y, c, chips, copy, lsem = plan(ins, outs, sems)
        for a in range(n):
            copy(a, 0, (x, y, 1 - c), (x, y, c)).wait_recv()
            for j, chip in enumerate(chips):
                copy(a, 4 + j, (*chip, 1 - c), (x, y, c)).wait_recv()
        for a in range(n):
            copy(a, 0, (x, y, c), (x, y, 1 - c), own=True).wait_send()
            for j, chip in enumerate(chips):
                copy(a, 1 + j, (x, y, c), (*chip, c), own=True).wait_send()
                copy(a, 4 + j, (*chip, c), (x, y, 1 - c)).wait_send()
            pltpu.make_async_copy(ins[a], outs[a].at[4 * x + 2 * y + c], lsem.at[a]).wait()

    return types.SimpleNamespace(
        arrays=arrays, out_shape=[_sds((N_DEV,) + a.shape, a.dtype) for a in arrays],
        scratch=[pltpu.SemaphoreType.DMA((7 * n,)), pltpu.SemaphoreType.DMA((7 * n,)), pltpu.SemaphoreType.DMA((n,))],
        start=start, mid=mid, finish=finish)


def _scatter_rider(arrays):
    arrays = list(arrays)
    n = len(arrays)

    def copies(ins, outs, sems):
        ssem, rsem = sems
        x, y, c = _me()
        out = []
        for a in range(n):
            for f in range(1, N_DEV):
                px = 1 - x if f & 4 else x
                py = 1 - y if f & 2 else y
                pc = 1 - c if f & 1 else c
                out.append(pltpu.make_async_remote_copy(
                    src_ref=ins[a].at[4 * px + 2 * py + pc], dst_ref=outs[a].at[f - 1], send_sem=ssem.at[7 * a + f - 1],
                    recv_sem=rsem.at[7 * a + f - 1], device_id=(px, py, pc), device_id_type=MESH))
        return out

    def start(ins, outs, sems):
        for cp in copies(ins, outs, sems):
            cp.start()

    def finish(ins, outs, sems):
        for cp in copies(ins, outs, sems):
            cp.wait()

    return types.SimpleNamespace(
        arrays=arrays, out_shape=[_sds((N_DEV - 1,) + a.shape[1:], a.dtype) for a in arrays],
        scratch=[pltpu.SemaphoreType.DMA((7 * n,)), pltpu.SemaphoreType.DMA((7 * n,))],
        start=start, mid=None, finish=finish)


def _call(body, name, grid, in_specs, out_specs, out_shape, scratch, operands, riders=()):
    n_in, n_out, n_scr = len(operands), len(out_shape), len(scratch)
    nsteps = grid[0] if grid else 1
    sizes = [(len(r.arrays), len(r.out_shape), len(r.scratch)) for r in riders]

    def wrapped(*refs):
        pos = n_in
        r_ins = []
        for ri, _, _ in sizes:
            r_ins.append(refs[pos:pos + ri])
            pos += ri
        outs = refs[pos:pos + n_out]
        pos += n_out
        r_outs = []
        for _, ro, _ in sizes:
            r_outs.append(refs[pos:pos + ro])
            pos += ro
        scr = refs[pos:pos + n_scr]
        pos += n_scr
        r_sems = []
        for _, _, rs in sizes:
            r_sems.append(refs[pos:pos + rs])
            pos += rs
        step = pl.program_id(0) if grid else 0

        def at(s, fn):
            if grid:
                pl.when(step == s)(fn)
            else:
                fn()

        for r, a, b, c in zip(riders, r_ins, r_outs, r_sems):
            at(0, lambda r=r, a=a, b=b, c=c: r.start(a, b, c))
        for r, a, b, c in zip(riders, r_ins, r_outs, r_sems):
            if r.mid is not None:
                at((3 * nsteps) // 4, lambda r=r, a=a, b=b, c=c: r.mid(a, b, c))
        if body is not None:
            body(*refs[:n_in], *outs, *scr)
        for r, a, b, c in zip(riders, r_ins, r_outs, r_sems):
            at(nsteps - 1, lambda r=r, a=a, b=b, c=c: r.finish(a, b, c))

    r_arrays = [a for r in riders for a in r.arrays]
    r_shapes = [s for r in riders for s in r.out_shape]
    kwargs = {}
    if grid:
        kwargs = dict(grid=grid, compiler_params=pltpu.CompilerParams(
            dimension_semantics=("arbitrary",) * len(grid), vmem_limit_bytes=VMEM_LIMIT))
    res = pl.pallas_call(
        wrapped, name=name,
        in_specs=list(in_specs) + [ANY] * len(r_arrays),
        out_specs=list(out_specs) + [ANY] * len(r_shapes),
        out_shape=list(out_shape) + r_shapes,
        scratch_shapes=list(scratch) + [s for r in riders for s in r.scratch],
        **kwargs,
    )(*operands, *r_arrays)
    host, rest = res[:n_out], res[n_out:]
    r_res = []
    for _, ro, _ in sizes:
        r_res.append(rest[:ro])
        rest = rest[ro:]
    return host, r_res


def _ln_in(h, g1, w_in_t, keep_hn, name):
    t = h.shape[0]
    tm = _tile(t, STREAM_TILE)

    def body(h_ref, g_ref, w_ref, z_ref, *hn_ref):
        y, _, _ = _rms_fwd(h_ref[...], g_ref[...])
        hn = y.astype(MX)
        if keep_hn:
            hn_ref[0][...] = hn
        z_ref[...] = _dot_nt(hn, w_ref[...])

    tile = lambda w: pl.BlockSpec((tm, w), lambda i: (i, 0))
    outs, _ = _call(
        body, name, (t // tm,),
        [tile(D_MODEL), _const_spec((1, D_MODEL)), _const_spec((IN_W, D_MODEL))],
        [tile(IN_W)] + [tile(D_MODEL)] * keep_hn,
        [_sds((t, IN_W), F32)] + [_sds((t, D_MODEL), MX)] * keep_hn, [], [h, g1, w_in_t])
    return outs[0], (outs[1] if keep_hn else None)


def _band2(kb, g):
    lo = lax.broadcasted_iota(jnp.int32, kb.shape, 1) < HEAD_DIM
    kr = pltpu.roll(kb, HEAD_DIM, 1)
    if g == 0:
        top, bot = jnp.where(lo, kb, 0.0), jnp.where(lo, 0.0, kr)
    else:
        top, bot = jnp.where(lo, kr, 0.0), jnp.where(lo, 0.0, kb)
    return jnp.concatenate([top, bot], axis=0)


def _attn_operands(z_ref, zh_ref, b):
    rows = slice(b * BLK, (b + 1) * BLK)
    prev = zh_ref if b == 0 else z_ref
    prow = slice(0, BLK) if b == 0 else slice((b - 1) * BLK, b * BLK)
    kb = jnp.concatenate([prev[prow, K0:K0 + KV_W], z_ref[rows, K0:K0 + KV_W]], axis=0)
    vb = jnp.concatenate([prev[prow, V0:V0 + KV_W], z_ref[rows, V0:V0 + KV_W]], axis=0)
    k2 = [_band2(kb, g) for g in range(2)]
    v2 = [_band2(vb, g) for g in range(2)]
    q2 = [jnp.concatenate([z_ref[rows, (2 * g) * BLK:(2 * g + 1) * BLK], z_ref[rows, (2 * g + 1) * BLK:(2 * g + 2) * BLK]],
                          axis=0) for g in range(2)]
    return q2, k2, v2


def _attn_block(z_ref, zh_ref, sink_ref, b, first):
    q2, k2, v2 = _attn_operands(z_ref, zh_ref, b)
    rr = lax.broadcasted_iota(jnp.int32, (4 * BLK, 2 * BLK), 0) & (BLK - 1)
    cc = lax.broadcasted_iota(jnp.int32, (4 * BLK, 2 * BLK), 1)
    first_block = jnp.logical_and(first, b == 0).astype(jnp.int32)
    mask = jnp.logical_and(jnp.logical_and(cc > rr, cc <= rr + BLK), cc >= BLK * first_block)
    s = jnp.concatenate([_dot_nt(q2[g], k2[g]) for g in range(2)], axis=0) * SCALE
    w = 2 * BLK
    out, psink = [], []
    for hh in range(2):
        sh = jnp.where(mask, s[:, hh * w:(hh + 1) * w], MASK_VALUE)
        sk = jnp.concatenate([jnp.broadcast_to(sink_ref[p:p + 1, hh * w:hh * w + 1], (BLK, 1)) for p in range(4)], axis=0)
        m = jnp.maximum(jnp.max(sh, axis=1, keepdims=True), sk)
        p = jnp.exp(sh - m)
        es = jnp.exp(sk - m)
        inv = 1.0 / (jnp.sum(p, axis=1, keepdims=True) + es)
        out.append(p * inv)
        psink.append(es * inv)
    return v2, jnp.concatenate(out, axis=1), psink


def _scan_steps(a, b, n, span, reverse):
    pos = lax.broadcasted_iota(jnp.int32, a.shape, 0) & (span - 1)
    d = 1
    while d < span:
        keep = pos < span - d if reverse else pos >= d
        shift = n - d if reverse else d
        a_sh = jnp.where(keep, pltpu.roll(a, shift, 0), 1.0)
        b_sh = jnp.where(keep, pltpu.roll(b, shift, 0), 0.0)
        b = a * b_sh + b
        a = a * a_sh
        d *= 2
    return a, b


def _scan(a, b, tm, reverse):
    return _scan_steps(a, b, tm, tm, reverse)


def _shifted_copies(ext, shifts, tm):
    rows = tm + CONV_HALO - 8
    for r in range(1, 8):
        shifts[r - 1, 0:rows, :] = ext[pl.ds(r, rows), :]


def _tap(ext, shifts, off, r0, n):
    a, r = divmod(off, 8)
    lo = 8 * a + r0
    if r == 0:
        return ext[lo:lo + n, :]
    return shifts[r - 1, lo:lo + n, :]


def _glu_fill(z_ref, zh_ref, uext, ush, first, tm, sg_out=None):
    cv = z_ref[:, CV0:CV0 + CONV_W]
    sg = _sigmoid(z_ref[:, CG0:CG0 + CONV_W])
    if sg_out is not None:
        sg_out[...] = sg
    hrow = BLK - CONV_HALO
    uh = zh_ref[hrow:BLK, CV0:CV0 + CONV_W] * _sigmoid(zh_ref[hrow:BLK, CG0:CG0 + CONV_W])
    uext[0:CONV_HALO, :] = jnp.where(first, 0.0, uh)
    uext[CONV_HALO:CONV_HALO + tm, :] = cv * sg
    _shifted_copies(uext, ush, tm)


def _conv_taps(cw_ref, pv_ref, uext, ush, out_ref, tm):
    for r0 in range(0, tm, CONV_CHUNK):
        acc = jnp.broadcast_to(pv_ref[R_CONV_B:R_CONV_B + 1, :], (CONV_CHUNK, CONV_W))
        for k in range(CONV_K):
            acc = acc + cw_ref[k:k + 1, :] * _tap(uext, ush, CONV_HALO - (CONV_K - 1) + k, r0, CONV_CHUNK)
        out_ref[r0:r0 + CONV_CHUNK, :] = acc


def _ln_silu(uc, pv_ref):
    mu = jnp.mean(uc, axis=-1, keepdims=True)
    xc = uc - mu
    rs = lax.rsqrt(jnp.mean(xc * xc, axis=-1, keepdims=True) + LN_EPS)
    xh = xc * rs
    ln = xh * pv_ref[R_LN_G:R_LN_G + 1, :] + pv_ref[R_LN_B:R_LN_B + 1, :]
    sg = _sigmoid(ln)
    return xh, rs, ln, sg


def _lru_gates(z_ref, zh_ref, pv_ref, wa_ref, wx_ref, rxext, first, tm):
    rxext[0:LRU_HALO, :] = jnp.where(first, 0.0, zh_ref[BLK - LRU_HALO:BLK, RX0:RX0 + LRU_W])
    rxext[LRU_HALO:LRU_HALO + tm, :] = z_ref[:, RX0:RX0 + LRU_W]
    xc = jnp.broadcast_to(pv_ref[R_LCONV_B:R_LCONV_B + 1, :], (tm, LRU_W))
    for k in range(LRU_K):
        xc = xc + pv_ref[R_LCW + k:R_LCW + k + 1, :] * rxext[pl.ds(LRU_HALO - (LRU_K - 1) + k, tm), :]
    r = _sigmoid(_dot(xc, wa_ref[...]) + pv_ref[R_BA:R_BA + 1, :])
    ig = _sigmoid(_dot(xc, wx_ref[...]) + pv_ref[R_BX:R_BX + 1, :])
    lam = pv_ref[R_LAM:R_LAM + 1, :]
    sp = jnp.log1p(jnp.exp(-lam))
    la = (-LRU_C * r) * sp
    a = jnp.exp(la)
    mult = jnp.sqrt(_neg_expm1(2.0 * la))
    return xc, r, ig, sp, la, a, mult


def _mixer_in_specs(tm, tile_of):
    hb = tm // BLK
    return [
        pl.BlockSpec((tm, IN_W), lambda i: (tile_of(i), 0)),
        pl.BlockSpec((BLK, IN_W), lambda i: (jnp.maximum(tile_of(i) * hb - 1, 0), 0)),
        _const_spec((8, 4 * BLK)),
        _const_spec((32, CONV_W)),
        _const_spec((16, CONV_W)),
        _const_spec((LRU_W, LRU_W)),
        _const_spec((LRU_W, LRU_W)),
    ]


def _mixer_fwd(z, sink, cw, pv, wa, wx, name, riders=()):
    t = z.shape[0]
    tm = _tile(t)
    nb = tm // BLK

    def body(z_ref, zh_ref, sink_ref, cw_ref, pv_ref, wa_ref, wx_ref, y_ref, hl_ref, uc_ref, p_ref, ps_ref,
             uext, ush, rxext, hcar):
        i = pl.program_id(0)
        first = i == 0

        @pl.when(first)
        def _():
            hcar[...] = jnp.zeros_like(hcar)

        lo = lax.broadcasted_iota(jnp.int32, (4 * BLK, BLK), 1) < HEAD_DIM
        for b in range(nb):
            rows = slice(b * BLK, (b + 1) * BLK)
            v2, prob, psink = _attn_block(z_ref, zh_ref, sink_ref, b, first)
            prob = prob.astype(MX)
            p_ref[b] = prob
            ps_ref[b] = jnp.where(lo, psink[0], psink[1])
            for g in range(2):
                o = _dot(prob[2 * g * BLK:(2 * g + 2) * BLK], v2[g])
                y_ref[rows, (2 * g) * BLK:(2 * g + 1) * BLK] = o[0:BLK]
                y_ref[rows, (2 * g + 1) * BLK:(2 * g + 2) * BLK] = o[BLK:2 * BLK]
        _glu_fill(z_ref, zh_ref, uext, ush, first, tm)
        _conv_taps(cw_ref, pv_ref, uext, ush, uc_ref, tm)
        _, _, ln, sg = _ln_silu(uc_ref[...], pv_ref)
        y_ref[:, ATTN_W:ATTN_W + CONV_W] = ln * sg
        xc, _, ig, _, _, a, mult = _lru_gates(z_ref, zh_ref, pv_ref, wa_ref, wx_ref, rxext, first, tm)
        acum, h = _scan(a, mult * (ig * xc), tm, reverse=False)
        h = h + acum * hcar[0:1, :]
        hl_ref[...] = h
        hcar[0:1, :] = h[tm - 1:tm, :]
        gl, _ = _gelu(z_ref[:, RG0:RG0 + LRU_W])
        y_ref[:, ATTN_W + CONV_W:ATTN_W + CONV_W + LRU_W] = h * gl

    tile = lambda w: pl.BlockSpec((tm, w), lambda i: (i, 0))
    return _call(
        body, name, (t // tm,), _mixer_in_specs(tm, lambda i: i),
        [tile(D_MODEL), tile(LRU_W), tile(CONV_W), pl.BlockSpec((nb, 4 * BLK, 4 * BLK), lambda i: (i, 0, 0)),
         pl.BlockSpec((nb, 4 * BLK, BLK), lambda i: (i, 0, 0))],
        [_sds((t, D_MODEL), F32), _sds((t, LRU_W), F32), _sds((t, CONV_W), F32),
         _sds((t // BLK, 4 * BLK, 4 * BLK), MX), _sds((t // BLK, 4 * BLK, BLK), F32)],
        [pltpu.VMEM((tm + CONV_HALO, CONV_W), F32), pltpu.VMEM((7, tm + CONV_HALO - 8, CONV_W), F32),
         pltpu.VMEM((tm + LRU_HALO, LRU_W), F32), pltpu.VMEM((8, LRU_W), F32)],
        [z, z, sink, cw, pv, wa, wx], riders)


def _mixer_bwd(dy, z, ycat, hl, uc, probs, psinks, sink, cw, pv, wa, wx, name, riders=()):
    t = z.shape[0]
    tm = _tile(t)
    nt = t // tm
    nb = tm // BLK
    rev = lambda i: nt - 1 - i

    def body(dy_ref, z_ref, zh_ref, sink_ref, cw_ref, pv_ref, wa_ref, wx_ref, y_ref, hl_ref, hlh_ref, uc_ref,
             p_ref, ps_ref, dz_ref, dsink_ref, dcw_ref, dpv_ref, dwa_ref, dwx_ref,
             uext, ush, sgs, rxext, dkext, dvext, ducext, dsh, dcw8, dxcext, kcar, vcar, uccar, xccar, gcar):
        i = pl.program_id(0)
        first = i == nt - 1

        @pl.when(i == 0)
        def _():
            for car in (kcar, vcar, uccar, xccar, gcar, dcw8):
                car[...] = jnp.zeros_like(car)
            for acc in (dsink_ref, dpv_ref, dwa_ref, dwx_ref):
                acc[...] = jnp.zeros_like(acc)

        def addrow(r, val):
            dpv_ref[r:r + 1, :] += jnp.sum(val, axis=0, keepdims=True)

        dkext[:, 0:tm] = jnp.zeros((KV_W, tm), F32)
        dvext[:, 0:tm] = jnp.zeros((KV_W, tm), F32)
        dkext[:, tm:tm + BLK] = kcar[...]
        dvext[:, tm:tm + BLK] = vcar[...]
        lane512 = lax.broadcasted_iota(jnp.int32, (1, 4 * BLK), 1) < 2 * BLK
        lo = lax.broadcasted_iota(jnp.int32, (4 * BLK, BLK), 1) < HEAD_DIM
        hd, w2 = HEAD_DIM, 2 * BLK
        for b in range(nb):
            rows = slice(b * BLK, (b + 1) * BLK)
            band = slice(b * BLK, (b + 2) * BLK)
            q2, k2, v2 = _attn_operands(z_ref, zh_ref, b)
            prob = p_ref[b]
            psink = [ps_ref[b, :, 0:1], ps_ref[b, :, HEAD_DIM:HEAD_DIM + 1]]
            stack = lambda ref: jnp.concatenate([ref[rows, p * BLK:(p + 1) * BLK] for p in range(4)], axis=0)
            do4 = stack(dy_ref)
            dlt = do4 * stack(y_ref)
            d0 = jnp.sum(jnp.where(lo, dlt, 0.0), axis=1, keepdims=True)
            d1 = jnp.sum(jnp.where(lo, 0.0, dlt), axis=1, keepdims=True)
            dp = jnp.concatenate([_dot_nt(do4[g * w2:(g + 1) * w2], v2[g]) for g in range(2)], axis=0)
            dl = jnp.concatenate([jnp.broadcast_to(d0, (4 * BLK, w2)), jnp.broadcast_to(d1, (4 * BLK, w2))], axis=1)
            draw = (prob * (dp - dl)) * SCALE
            e0, e1 = psink[0] * d0, psink[1] * d1
            for p in range(4):
                prs = slice(p * BLK, (p + 1) * BLK)
                s0 = jnp.sum(e0[prs], axis=0, keepdims=True)
                s1 = jnp.sum(e1[prs], axis=0, keepdims=True)
                dsink_ref[p:p + 1, :] += -jnp.where(lane512, s0, s1)
            for g in range(2):
                grs = slice(g * w2, (g + 1) * w2)
                dq = _dot(draw[grs], k2[g])
                dz_ref[rows, (2 * g) * BLK:(2 * g + 1) * BLK] = dq[0:BLK].astype(dz_ref.dtype)
                dz_ref[rows, (2 * g + 1) * BLK:(2 * g + 2) * BLK] = dq[BLK:2 * BLK].astype(dz_ref.dtype)
                tk = _dot_tn(q2[g], draw[grs])
                tv = _dot_tn(do4[grs], prob[grs])
                dkext[g * hd:(g + 1) * hd, band] += tk[0:hd, 0:w2] + tk[hd:2 * hd, w2:2 * w2]
                dvext[g * hd:(g + 1) * hd, band] += tv[0:hd, 0:w2] + tv[hd:2 * hd, w2:2 * w2]
        dz_ref[:, K0:K0 + KV_W] = jnp.transpose(dkext[:, BLK:BLK + tm]).astype(dz_ref.dtype)
        dz_ref[:, V0:V0 + KV_W] = jnp.transpose(dvext[:, BLK:BLK + tm]).astype(dz_ref.dtype)
        kcar[...] = dkext[:, 0:BLK]
        vcar[...] = dvext[:, 0:BLK]

        _glu_fill(z_ref, zh_ref, uext, ush, first, tm, sg_out=sgs)
        xh, rs, ln, sg = _ln_silu(uc_ref[...], pv_ref)
        dln = dy_ref[:, ATTN_W:ATTN_W + CONV_W] * (sg * (1.0 + ln * (1.0 - sg)))
        addrow(R_LN_G, dln * xh)
        addrow(R_LN_B, dln)
        dxh = dln * pv_ref[R_LN_G:R_LN_G + 1, :]
        duc = rs * (dxh - jnp.mean(dxh, axis=-1, keepdims=True) - xh * jnp.mean(dxh * xh, axis=-1, keepdims=True))
        addrow(R_CONV_B, duc)
        ducext[0:tm, :] = duc
        ducext[tm:tm + CONV_HALO, :] = uccar[...]
        uccar[...] = duc[0:CONV_HALO, :]
        _shifted_copies(ducext, dsh, tm)
        for r0 in range(0, tm, CONV_CHUNK):
            crow = slice(r0, r0 + CONV_CHUNK)
            duc_c = ducext[crow, :]
            du = jnp.zeros((CONV_CHUNK, CONV_W), F32)
            for k in range(CONV_K):
                prod = duc_c * _tap(uext, ush, CONV_HALO - (CONV_K - 1) + k, r0, CONV_CHUNK)
                part = prod[0:8]
                for s in range(8, CONV_CHUNK, 8):
                    part = part + prod[s:s + 8]
                dcw8[k] += part
                du = du + cw_ref[k:k + 1, :] * _tap(ducext, dsh, CONV_K - 1 - k, r0, CONV_CHUNK)
            sgc = sgs[crow, :]
            dz_ref[crow, CV0:CV0 + CONV_W] = (du * sgc).astype(dz_ref.dtype)
            u_c = uext[CONV_HALO + r0:CONV_HALO + r0 + CONV_CHUNK, :]
            dz_ref[crow, CG0:CG0 + CONV_W] = (du * u_c * (1.0 - sgc)).astype(dz_ref.dtype)

        @pl.when(i == nt - 1)
        def _():
            dcw_ref[...] = jnp.sum(dcw8[...], axis=1)

        xc, r, ig, sp, la, a, mult = _lru_gates(z_ref, zh_ref, pv_ref, wa_ref, wx_ref, rxext, first, tm)
        h = hl_ref[...]
        rowi = lax.broadcasted_iota(jnp.int32, (tm, LRU_W), 0)
        hlast = jnp.where(first, 0.0, hlh_ref[7:8, :])
        hprev = jnp.where(rowi == 0, hlast, pltpu.roll(h, 1, 0))
        dyl = dy_ref[:, ATTN_W + CONV_W:ATTN_W + CONV_W + LRU_W]
        gl, dgl = _gelu(z_ref[:, RG0:RG0 + LRU_W])
        dz_ref[:, RG0:RG0 + LRU_W] = (dyl * h * dgl).astype(dz_ref.dtype)
        dh = dyl * gl + jnp.where(rowi == tm - 1, gcar[0:1, :], 0.0)
        c = jnp.where(rowi == tm - 1, 0.0, pltpu.roll(a, tm - 1, 0))
        _, gg = _scan(c, dh, tm, reverse=True)
        gcar[0:1, :] = a[0:1, :] * gg[0:1, :]
        dmult = gg * (ig * xc)
        dig = gg * mult * xc
        dxc = gg * mult * ig
        dla = gg * hprev * a - dmult * a * a / mult
        dr = dla * (-LRU_C * sp)
        lam = pv_ref[R_LAM:R_LAM + 1, :]
        dpv_ref[R_LAM:R_LAM + 1, :] += jnp.sum(dla * (-LRU_C * r), axis=0, keepdims=True) * (-_sigmoid(-lam))
        dpa = dr * r * (1.0 - r)
        dpx = dig * ig * (1.0 - ig)
        addrow(R_BA, dpa)
        addrow(R_BX, dpx)
        dxc = dxc + _dot_nt(dpa, wa_ref[...]) + _dot_nt(dpx, wx_ref[...])
        dwa_ref[...] += _dot_tn(xc, dpa)
        dwx_ref[...] += _dot_tn(xc, dpx)
        addrow(R_LCONV_B, dxc)
        dxcext[0:tm, :] = dxc
        dxcext[tm:tm + LRU_HALO, :] = xccar[...]
        xccar[...] = dxc[0:LRU_HALO, :]
        drx = jnp.zeros((tm, LRU_W), F32)
        for k in range(LRU_K):
            addrow(R_LCW + k, dxc * rxext[pl.ds(LRU_HALO - (LRU_K - 1) + k, tm), :])
            drx = drx + pv_ref[R_LCW + k:R_LCW + k + 1, :] * dxcext[pl.ds(LRU_K - 1 - k, tm), :]
        dz_ref[:, RX0:RX0 + LRU_W] = drx.astype(dz_ref.dtype)

    tile = lambda w: pl.BlockSpec((tm, w), lambda i: (rev(i), 0))
    in_specs = [tile(D_MODEL)] + _mixer_in_specs(tm, rev) + [
        tile(D_MODEL), tile(LRU_W),
        pl.BlockSpec((8, LRU_W), lambda i: (jnp.maximum(rev(i) * (tm // 8) - 1, 0), 0)),
        tile(CONV_W), pl.BlockSpec((nb, 4 * BLK, 4 * BLK), lambda i: (rev(i), 0, 0)),
        pl.BlockSpec((nb, 4 * BLK, BLK), lambda i: (rev(i), 0, 0))]
    return _call(
        body, name, (nt,), in_specs,
        [tile(IN_W), _acc_spec((8, 4 * BLK)), _acc_spec((32, CONV_W)), _acc_spec((16, CONV_W)),
         _acc_spec((LRU_W, LRU_W)), _acc_spec((LRU_W, LRU_W))],
        [_sds((t, IN_W), MX), _sds((8, 4 * BLK), F32), _sds((32, CONV_W), F32), _sds((16, CONV_W), F32),
         _sds((LRU_W, LRU_W), F32), _sds((LRU_W, LRU_W), F32)],
        [pltpu.VMEM((tm + CONV_HALO, CONV_W), F32), pltpu.VMEM((7, tm + CONV_HALO - 8, CONV_W), F32),
         pltpu.VMEM((tm, CONV_W), F32), pltpu.VMEM((tm + LRU_HALO, LRU_W), F32),
         pltpu.VMEM((KV_W, tm + BLK), F32), pltpu.VMEM((KV_W, tm + BLK), F32),
         pltpu.VMEM((tm + CONV_HALO, CONV_W), F32), pltpu.VMEM((7, tm + CONV_HALO - 8, CONV_W), F32),
         pltpu.VMEM((32, 8, CONV_W), F32), pltpu.VMEM((tm + LRU_HALO, LRU_W), F32),
         pltpu.VMEM((KV_W, BLK), F32), pltpu.VMEM((KV_W, BLK), F32),
         pltpu.VMEM((CONV_HALO, CONV_W), F32), pltpu.VMEM((LRU_HALO, LRU_W), F32), pltpu.VMEM((8, LRU_W), F32)],
        [dy, z, z, sink, cw, pv, wa, wx, ycat, hl, hl, uc, probs, psinks], riders)


def _post_fwd(ycat, h0, gmix, w_out, g2, w_up, w_down, name, riders=()):
    t = h0.shape[0]
    tm = _tile(t, POST_TILE)
    nj = D_FF // FF_BLK

    def body(y_ref, h_ref, gm_ref, wo_ref, g2_ref, wu_ref, wd_ref, h1_ref, a_ref, h2_ref, ym_ref, hn_ref):
        ym, _, _ = _group_rms_fwd(y_ref[...], gm_ref[...])
        ym = ym.astype(MX)
        ym_ref[...] = ym
        h1 = h_ref[...] + jnp.dot(ym, wo_ref[...], preferred_element_type=F32)
        h1_ref[...] = h1
        hn, _, _ = _rms_fwd(h1, g2_ref[...])
        hn = hn.astype(MX)
        hn_ref[...] = hn
        for j in range(nj):
            u = jnp.dot(hn, wu_ref[j], preferred_element_type=F32)
            a_ref[:, j * FF_BLK:(j + 1) * FF_BLK] = jnp.square(jnp.maximum(u, 0.0)).astype(MX)
        h2_ref[...] = h1 + jnp.dot(a_ref[...], wd_ref[...], preferred_element_type=F32)

    tile = lambda w: pl.BlockSpec((tm, w), lambda i: (i, 0))
    return _call(
        body, name, (t // tm,),
        [tile(D_MODEL), tile(D_MODEL), _const_spec((1, D_MODEL)), _const_spec((D_MODEL, D_MODEL)),
         _const_spec((1, D_MODEL)), _const_spec((nj, D_MODEL, FF_BLK)), _const_spec((D_FF, D_MODEL))],
        [tile(D_MODEL), tile(D_FF), tile(D_MODEL), tile(D_MODEL), tile(D_MODEL)],
        [_sds((t, D_MODEL), F32), _sds((t, D_FF), MX), _sds((t, D_MODEL), F32), _sds((t, D_MODEL), MX),
         _sds((t, D_MODEL), MX)],
        [], [ycat, h0, gmix, w_out, g2, w_up, w_down], riders)


def _ffn_bwd(dh2, act, h1, g2, w_up_t, w_down, name, riders=()):
    t = h1.shape[0]
    tm = _tile(t, POST_TILE)
    nj = D_FF // FF_BLK

    def body(dh2_ref, a_ref, h1_ref, g2_ref, wut_ref, wd_ref, dh1_ref, dh1b_ref, dh2b_ref, du_ref, dg2_ref):
        @pl.when(pl.program_id(0) == 0)
        def _():
            dg2_ref[...] = jnp.zeros_like(dg2_ref)

        dh2 = dh2_ref[...]
        dh2b = dh2.astype(MX)
        dh2b_ref[...] = dh2b
        for j in range(nj):
            cols = slice(j * FF_BLK, (j + 1) * FF_BLK)
            da = _dot_nt(dh2b, wd_ref[j])
            du_ref[:, cols] = (da * (2.0 * jnp.sqrt(a_ref[:, cols].astype(F32)))).astype(MX)
        dhn = jnp.dot(du_ref[...], wut_ref[...], preferred_element_type=F32)
        _, xh, r = _rms_fwd(h1_ref[...], g2_ref[...])
        dx, dg = _rms_bwd(dhn, xh, r, g2_ref[...])
        dg2_ref[...] += dg
        dh1 = dh2 + dx
        dh1_ref[...] = dh1
        dh1b_ref[...] = dh1.astype(MX)

    tile = lambda w: pl.BlockSpec((tm, w), lambda i: (i, 0))
    return _call(
        body, name, (t // tm,),
        [tile(D_MODEL), tile(D_FF), tile(D_MODEL), _const_spec((1, D_MODEL)),
         _const_spec((D_FF, D_MODEL)), _const_spec((nj, FF_BLK, D_MODEL))],
        [tile(D_MODEL), tile(D_MODEL), tile(D_MODEL), tile(D_FF), _acc_spec((1, D_MODEL))],
        [_sds((t, D_MODEL), F32), _sds((t, D_MODEL), MX), _sds((t, D_MODEL), MX), _sds((t, D_FF), MX),
         _sds((1, D_MODEL), F32)],
        [], [dh2, act, h1, g2, w_up_t, w_down], riders)


def _mix_bwd(dh1, ycat, ym, gmix, w_out, name):
    t = dh1.shape[0]
    tm = _tile(t, STREAM_TILE)
    nk = t // tm
    r = D_MODEL // N_DEV

    def body(dh1_ref, y_ref, ym_ref, gm_ref, wo_ref, dy_ref, dgm_ref, o_ref, o16_ref, acc):
        k = pl.program_id(0)

        @pl.when(k == 0)
        def _():
            dgm_ref[...] = jnp.zeros_like(dgm_ref)
            acc[...] = jnp.zeros_like(acc)

        dh = dh1_ref[...]
        acc[...] += _dot_tn(ym_ref[...], dh)
        dym = _dot_nt(dh, wo_ref[...])
        gm = gm_ref[...]
        _, yh, rr = _group_rms_fwd(y_ref[...], gm)
        outs, dgs = [], []
        for (a, b), rg in zip(_GROUPS, rr):
            dxg, dgg = _rms_bwd(dym[:, a:b], yh[:, a:b], rg, gm[:, a:b])
            outs.append(dxg)
            dgs.append(dgg)
        dy_ref[...] = jnp.concatenate(outs, axis=1)
        dgm_ref[...] += jnp.concatenate(dgs, axis=1)

        @pl.when(k == nk - 1)
        def _():
            for d in range(N_DEV):
                v = acc[d * r:(d + 1) * r, :]
                o_ref[d] = v
                o16_ref[d] = v.astype(o16_ref.dtype)

    tile = pl.BlockSpec((tm, D_MODEL), lambda i: (i, 0))
    slabs = _const_spec((N_DEV, r, D_MODEL))
    (dy, dgm, dw, dw16), _ = _call(
        body, name, (nk,), [tile, tile, tile, _const_spec((1, D_MODEL)), _const_spec((D_MODEL, D_MODEL))],
        [tile, _acc_spec((1, D_MODEL)), slabs, slabs],
        [_sds((t, D_MODEL), F32), _sds((1, D_MODEL), F32), _sds((N_DEV, r, D_MODEL), F32),
         _sds((N_DEV, r, D_MODEL), WIRE)],
        [pltpu.VMEM((D_MODEL, D_MODEL), F32)], [dh1, ycat, ym, gmix, w_out])
    return dy, dgm, (dw, dw16)


def _in_bwd(dz, h0, dh1, g1, w_in_t, after, name):
    t = h0.shape[0]
    tm = _tile(t, STREAM_TILE)

    def body(dz_ref, h_ref, dh1_ref, g_ref, w_ref, after_ref, dh0_ref, dg_ref):
        @pl.when(pl.program_id(0) == 0)
        def _():
            dg_ref[...] = jnp.zeros_like(dg_ref)

        dhn = _dot(dz_ref[...], w_ref[...])
        _, xh, r = _rms_fwd(h_ref[...], g_ref[...])
        dx, dg = _rms_bwd(dhn, xh, r, g_ref[...])
        dg_ref[...] += dg
        dh0_ref[...] = dh1_ref[...] + dx

    tile = lambda w: pl.BlockSpec((tm, w), lambda i: (i, 0))
    (dh0, dg), _ = _call(
        body, name, (t // tm,),
        [tile(IN_W), tile(D_MODEL), tile(D_MODEL), _const_spec((1, D_MODEL)), _const_spec((IN_W, D_MODEL)),
         _const_spec((8, 128))],
        [tile(D_MODEL), _acc_spec((1, D_MODEL))], [_sds((t, D_MODEL), F32), _sds((1, D_MODEL), F32)],
        [], [dz, h0, dh1, g1, w_in_t, after])
    return dh0, dg


def _in_bwd_dw(dz, h0, dh1, g1, w_in_t, name):
    t = h0.shape[0]
    tm = _tile(t)
    nk = t // tm

    def body(dz_ref, h_ref, dh1_ref, g_ref, w_ref, dh0_ref, dg_ref, o_ref, o16_ref, acc):
        k = pl.program_id(0)

        @pl.when(k == 0)
        def _():
            dg_ref[...] = jnp.zeros_like(dg_ref)
            acc[...] = jnp.zeros_like(acc)

        dz_t = dz_ref[...]
        hn, xh, r = _rms_fwd(h_ref[...], g_ref[...])
        acc[...] += _dot_tn(dz_t, hn)
        dhn = _dot(dz_t, w_ref[...])
        dx, dg = _rms_bwd(dhn, xh, r, g_ref[...])
        dg_ref[...] += dg
        dh0_ref[...] = dh1_ref[...] + dx

        @pl.when(k == nk - 1)
        def _():
            for d in range(N_DEV):
                v = acc[d * IN_SHARD:(d + 1) * IN_SHARD, :]
                o_ref[d] = v
                o16_ref[d] = v.astype(o16_ref.dtype)

    tile = lambda w: pl.BlockSpec((tm, w), lambda i: (i, 0))
    slabs = _const_spec((N_DEV, IN_SHARD, D_MODEL))
    (dh0, dg, dw, dw16), _ = _call(
        body, name, (nk,),
        [tile(IN_W), tile(D_MODEL), tile(D_MODEL), _const_spec((1, D_MODEL)), _const_spec((IN_W, D_MODEL))],
        [tile(D_MODEL), _acc_spec((1, D_MODEL)), slabs, slabs],
        [_sds((t, D_MODEL), F32), _sds((1, D_MODEL), F32), _sds((N_DEV, IN_SHARD, D_MODEL), F32),
         _sds((N_DEV, IN_SHARD, D_MODEL), WIRE)],
        [pltpu.VMEM((IN_W, D_MODEL), F32)], [dz, h0, dh1, g1, w_in_t])
    return dh0, dg, (dw, dw16)


def _loss_head(h, gf, target, name):
    t = h.shape[0]
    tm = _tile(t, STREAM_TILE)

    def body(h_ref, g_ref, t_ref, dh_ref, loss_ref, dg_ref):
        @pl.when(pl.program_id(0) == 0)
        def _():
            loss_ref[...] = jnp.zeros_like(loss_ref)
            dg_ref[...] = jnp.zeros_like(dg_ref)

        g = g_ref[...]
        y, xh, r = _rms_fwd(h_ref[...], g)
        err = y - t_ref[...]
        part = 0.5 * jnp.sum(jnp.mean(err * err, axis=-1, keepdims=True), axis=0, keepdims=True)
        loss_ref[...] += jnp.broadcast_to(part, loss_ref.shape)
        dx, dg = _rms_bwd(err * (1.0 / D_MODEL), xh, r, g)
        dg_ref[...] += dg
        dh_ref[...] = dx

    tile = pl.BlockSpec((tm, D_MODEL), lambda i: (i, 0))
    (dh, loss, dg), _ = _call(
        body, name, (t // tm,), [tile, _const_spec((1, D_MODEL)), tile],
        [tile, _acc_spec((1, 128)), _acc_spec((1, D_MODEL))],
        [_sds((t, D_MODEL), F32), _sds((1, 128), F32), _sds((1, D_MODEL), F32)], [], [h, gf, target])
    return dh, loss, dg


def _dw(x, y, name, split, bm, bn):
    t, m = x.shape
    n = y.shape[1]
    tk = _tile(t, DW_TILE)
    nk = t // tk
    if split == "rows":
        assert bn == n
        r, c = m // N_DEV, n
        per = bm // r
        out_block = pl.BlockSpec((per, r, c), lambda a, b, k: (a, 0, 0))
    else:
        assert bm == m
        r, c = m, n // N_DEV
        per = bn // c
        out_block = pl.BlockSpec((per, r, c), lambda a, b, k: (b, 0, 0))

    def body(x_ref, y_ref, o_ref, o16_ref, acc):
        k = pl.program_id(2)

        @pl.when(k == 0)
        def _():
            acc[...] = jnp.zeros_like(acc)

        acc[...] += _dot_tn(x_ref[...], y_ref[...])

        @pl.when(k == nk - 1)
        def _():
            for d in range(per):
                v = acc[d * r:(d + 1) * r, :] if split == "rows" else acc[:, d * c:(d + 1) * c]
                o_ref[d] = v
                o16_ref[d] = v.astype(o16_ref.dtype)

    return pl.pallas_call(
        body, name=name, grid=(m // bm, n // bn, nk),
        in_specs=[pl.BlockSpec((tk, bm), lambda a, b, k: (k, a)), pl.BlockSpec((tk, bn), lambda a, b, k: (k, b))],
        out_specs=[out_block, out_block],
        out_shape=[_sds((N_DEV, r, c), F32), _sds((N_DEV, r, c), WIRE)],
        scratch_shapes=[pltpu.VMEM((bm, bn), F32)],
        compiler_params=pltpu.CompilerParams(dimension_semantics=("arbitrary",) * 3, vmem_limit_bytes=VMEM_LIMIT),
    )(x, y)


def _adamw_math(w, g, m, v):
    m = ADAM_B1 * m + (1.0 - ADAM_B1) * g
    v = ADAM_B2 * v + (1.0 - ADAM_B2) * jnp.square(g)
    m_hat = m / (1.0 - ADAM_B1 ** ADAM_STEP)
    v_hat = v / (1.0 - ADAM_B2 ** ADAM_STEP)
    delta = -ADAM_LR * (m_hat / (jnp.sqrt(v_hat) + ADAM_EPS) + ADAM_WD * w)
    return delta, m, v


def _adamw_shard(g_own, g_recv, dev, w, m, v, after, name):
    _, r, c = w.shape
    br = r
    for cand in (256, 128, 112, 64, 56, 32, 16, 8):
        if r % cand == 0:
            br = cand
            break
    nr = r // br
    own = lambda l: pl.BlockSpec((1, br, c), lambda ll, i, d: (d[0], jnp.where(ll == l, i, (nr - 1) * (1 - l)), 0))
    recv = lambda l: pl.BlockSpec((N_DEV - 1, br, c), lambda ll, i, d: (0, jnp.where(ll == l, i, (nr - 1) * (1 - l)), 0))

    def body(dev_ref, go0, gr0, go1, gr1, w_ref, m_ref, v_ref, after_ref, g_out, d_out, m_out, v_out):
        def update(go_ref, gr_ref):
            g = go_ref[0]
            for j in range(N_DEV - 1):
                g = g + gr_ref[j].astype(F32)
            delta, mn, vn = _adamw_math(w_ref[0], g, m_ref[0], v_ref[0])
            g_out[0] = g
            d_out[0] = delta
            m_out[0] = mn
            v_out[0] = vn

        layer = pl.program_id(0)
        pl.when(layer == 0)(lambda: update(go0, gr0))
        pl.when(layer == 1)(lambda: update(go1, gr1))

    tile = pl.BlockSpec((1, br, c), lambda ll, i, d: (ll, i, 0))
    return pl.pallas_call(
        body, name=name,
        grid_spec=pltpu.PrefetchScalarGridSpec(
            num_scalar_prefetch=1, grid=(2, nr),
            in_specs=[own(0), recv(0), own(1), recv(1), tile, tile, tile,
                      pl.BlockSpec((8, 128), lambda ll, i, d: (0, 0))],
            out_specs=[tile, tile, tile, tile]),
        out_shape=[_sds((2, r, c), F32)] * 4,
        compiler_params=pltpu.CompilerParams(dimension_semantics=("arbitrary",) * 2, vmem_limit_bytes=VMEM_LIMIT),
    )(dev, g_own[0], g_recv[0], g_own[1], g_recv[1], w, m, v, after)


def _adamw_small(gs, ws, ms, vs, name):
    n = len(gs)

    def body(*refs):
        g_refs, w_refs, m_refs, v_refs = (refs[k * n:(k + 1) * n] for k in range(4))
        outs = refs[4 * n:]
        for k in range(n):
            delta, mn, vn = _adamw_math(w_refs[k][...], g_refs[k][...], m_refs[k][...], v_refs[k][...])
            outs[k][...] = delta
            outs[n + k][...] = mn
            outs[2 * n + k][...] = vn

    shapes = [_sds(w.shape, F32) for w in ws]
    res = pl.pallas_call(body, name=name, out_shape=shapes * 3,
                         compiler_params=pltpu.CompilerParams(vmem_limit_bytes=VMEM_LIMIT))(*gs, *ws, *ms, *vs)
    return res[:n], res[n:2 * n], res[2 * n:]


def _sum_parts(own, recv, dev, name):
    def body(dev_ref, own_ref, recv_ref, o_ref):
        me = dev_ref[0]

        def block(d):
            f = jnp.bitwise_xor(me, d)
            return jnp.where(f == 0, own_ref[...], recv_ref[jnp.maximum(f - 1, 0)])

        g = block(0)
        for d in range(1, N_DEV):
            g = g + block(d)
        o_ref[...] = g

    return pl.pallas_call(
        body, name=name,
        grid_spec=pltpu.PrefetchScalarGridSpec(
            num_scalar_prefetch=1, grid=(1,),
            in_specs=[pl.BlockSpec(own.shape, lambda i, d: (0, 0)), pl.BlockSpec(recv.shape, lambda i, d: (0, 0, 0))],
            out_specs=pl.BlockSpec(own.shape, lambda i, d: (0, 0))),
        out_shape=_sds(own.shape, F32))(dev, own, recv)


HBM = pl.BlockSpec(memory_space=pltpu.HBM)
SEM = pl.BlockSpec(memory_space=pltpu.SEMAPHORE)
EFFECT = pltpu.SideEffectType.DATAFLOW_SIDE_EFFECTING


def _direct_copies(srcs, lands, ssem, rsem, scatter):
    x, y, c = _me()
    out = []
    for a in range(len(srcs)):
        for f in range(1, N_DEV):
            px = 1 - x if f & 4 else x
            py = 1 - y if f & 2 else y
            pc = 1 - c if f & 1 else c
            out.append(pltpu.make_async_remote_copy(
                src_ref=srcs[a].at[4 * px + 2 * py + pc] if scatter else srcs[a], dst_ref=lands[a].at[f - 1],
                send_sem=ssem.at[7 * a + f - 1], recv_sem=rsem.at[7 * a + f - 1],
                device_id=(px, py, pc), device_id_type=MESH))
    return out


def _send_start(arrays, scatter, name):
    arrays = list(arrays)
    n = len(arrays)
    lands = [lax.empty((N_DEV - 1,) + (a.shape[1:] if scatter else a.shape), a.dtype) for a in arrays]

    def body(*refs):
        srcs, lnds, ssem, rsem, token = refs[:n], refs[n:2 * n], refs[2 * n], refs[2 * n + 1], refs[-1]
        for cp in _direct_copies(srcs, lnds, ssem, rsem, scatter):
            cp.start()
        token[...] = jnp.zeros_like(token)

    hbm = lambda a: pltpu.HBM(a.shape, a.dtype)
    res = pl.pallas_call(
        body, name=name,
        out_shape=(pltpu.SemaphoreType.DMA((7 * n,)), pltpu.SemaphoreType.DMA((7 * n,)),
                   *[hbm(a) for a in arrays + lands], _sds((8, 128), F32)),
        in_specs=[HBM] * (2 * n),
        out_specs=(SEM, SEM, *[HBM] * (2 * n), pl.BlockSpec(memory_space=pltpu.VMEM)),
        input_output_aliases={i: 2 + i for i in range(2 * n)},
        compiler_params=pltpu.CompilerParams(has_side_effects=EFFECT),
    )(*[pltpu.with_memory_space_constraint(a, pltpu.HBM) for a in arrays + lands])
    return types.SimpleNamespace(ssem=res[0], rsem=res[1], srcs=list(res[2:2 + n]), lands=list(res[2 + n:2 + 2 * n]),
                                 token=res[-1], scatter=scatter)


def _send_wait(h, after, name):
    n = len(h.srcs)

    def body(*refs):
        srcs, lnds, ssem, rsem = refs[:n], refs[n:2 * n], refs[2 * n], refs[2 * n + 1]
        for cp in _direct_copies(srcs, lnds, ssem, rsem, h.scatter):
            cp.wait_send()
            cp.wait_recv()

    hbm = lambda a: pltpu.HBM(a.shape, a.dtype)
    res = pl.pallas_call(
        body, name=name,
        out_shape=tuple(hbm(a) for a in h.srcs + h.lands),
        in_specs=[HBM] * (2 * n) + [SEM, SEM, ANY], out_specs=[HBM] * (2 * n),
        input_output_aliases={i: i for i in range(2 * n)},
        compiler_params=pltpu.CompilerParams(has_side_effects=EFFECT),
    )(*h.srcs, *h.lands, h.ssem, h.rsem, after)
    return list(res[:n]), list(res[n:])


def _block_diag(w):
    out = jnp.zeros((LRU_W, LRU_W), w.dtype)
    for h in range(4):
        out = lax.dynamic_update_slice(out, w[h], (h * 64, h * 64))
    return out


def _unblock_diag(w):
    return jnp.concatenate([w[h * 64:(h + 1) * 64, h * 64:(h + 1) * 64] for h in range(4)], axis=0)


def _layer_params(p, l):
    row = lambda a: a[l].reshape(1, -1)
    sink_rows = jnp.repeat(p["attn_sinks"][l].reshape(4, 2), 2 * BLK, axis=1)
    sink_rows = jnp.concatenate([sink_rows, jnp.zeros((4, 4 * BLK), F32)], axis=0)
    cw = jnp.concatenate([p["conv_dw_w"][l], jnp.zeros((1, CONV_W), F32)], axis=0)
    pv = jnp.concatenate([
        row(p["conv_dw_b"]), row(p["conv_ln_g"]), row(p["conv_ln_b"]), row(p["lru_conv_b"]), row(p["lru_ba"]),
        row(p["lru_bx"]), row(p["lru_lambda"]), jnp.zeros((1, LRU_W), F32), p["lru_conv_w"][l],
        jnp.zeros((4, LRU_W), F32)], axis=0)
    return dict(
        g1=row(p["norm1"]), sink=sink_rows, cw=cw, pv=pv,
        wa=_block_diag(p["lru_wa"][l]).astype(MX), wx=_block_diag(p["lru_wx"][l]).astype(MX),
        gmix=row(p["mix_norm"]), g2=row(p["norm2"]))


_SMALL = ["norm1", "attn_sinks", "conv_dw_w", "conv_dw_b", "conv_ln_g", "conv_ln_b", "lru_conv_w", "lru_conv_b",
          "lru_wa", "lru_ba", "lru_wx", "lru_bx", "lru_lambda", "mix_norm", "norm2"]
_BIG = ["w_in", "w_out", "w_up", "w_down"]
_WEIGHTS = ["norm1", "w_in", "attn_sinks", "conv_dw_w", "conv_dw_b", "conv_ln_g", "conv_ln_b", "lru_conv_w",
            "lru_conv_b", "lru_wa", "lru_ba", "lru_wx", "lru_bx", "lru_lambda", "mix_norm", "w_out", "norm2", "w_up",
            "w_down", "final_norm"]


def kernel(x, norm1, w_in, attn_sinks, conv_dw_w, conv_dw_b, conv_ln_g, conv_ln_b, lru_conv_w, lru_conv_b, lru_wa, lru_ba, lru_wx, lru_bx, lru_lambda, mix_norm, w_out, norm2, w_up, w_down, final_norm, loss_target, m_norm1, m_w_in, m_attn_sinks, m_conv_dw_w, m_conv_dw_b, m_conv_ln_g, m_conv_ln_b, m_lru_conv_w, m_lru_conv_b, m_lru_wa, m_lru_ba, m_lru_wx, m_lru_bx, m_lru_lambda, m_mix_norm, m_w_out, m_norm2, m_w_up, m_w_down, m_final_norm, v_norm1, v_w_in, v_attn_sinks, v_conv_dw_w, v_conv_dw_b, v_conv_ln_g, v_conv_ln_b, v_lru_conv_w, v_lru_conv_b, v_lru_wa, v_lru_ba, v_lru_wx, v_lru_bx, v_lru_lambda, v_mix_norm, v_w_out, v_norm2, v_w_up, v_w_down, v_final_norm):
    w = dict(norm1=norm1, w_in=w_in, attn_sinks=attn_sinks, conv_dw_w=conv_dw_w, conv_dw_b=conv_dw_b,
             conv_ln_g=conv_ln_g, conv_ln_b=conv_ln_b, lru_conv_w=lru_conv_w, lru_conv_b=lru_conv_b, lru_wa=lru_wa,
             lru_ba=lru_ba, lru_wx=lru_wx, lru_bx=lru_bx, lru_lambda=lru_lambda, mix_norm=mix_norm, w_out=w_out,
             norm2=norm2, w_up=w_up, w_down=w_down, final_norm=final_norm)
    m = dict(norm1=m_norm1, w_in=m_w_in, attn_sinks=m_attn_sinks, conv_dw_w=m_conv_dw_w, conv_dw_b=m_conv_dw_b,
             conv_ln_g=m_conv_ln_g, conv_ln_b=m_conv_ln_b, lru_conv_w=m_lru_conv_w, lru_conv_b=m_lru_conv_b,
             lru_wa=m_lru_wa, lru_ba=m_lru_ba, lru_wx=m_lru_wx, lru_bx=m_lru_bx, lru_lambda=m_lru_lambda,
             mix_norm=m_mix_norm, w_out=m_w_out, norm2=m_norm2, w_up=m_w_up, w_down=m_w_down, final_norm=m_final_norm)
    v = dict(norm1=v_norm1, w_in=v_w_in, attn_sinks=v_attn_sinks, conv_dw_w=v_conv_dw_w, conv_dw_b=v_conv_dw_b,
             conv_ln_g=v_conv_ln_g, conv_ln_b=v_conv_ln_b, lru_conv_w=v_lru_conv_w, lru_conv_b=v_lru_conv_b,
             lru_wa=v_lru_wa, lru_ba=v_lru_ba, lru_wx=v_lru_wx, lru_bx=v_lru_bx, lru_lambda=v_lru_lambda,
             mix_norm=v_mix_norm, w_out=v_w_out, norm2=v_norm2, w_up=v_w_up, w_down=v_w_down, final_norm=v_final_norm)
    depth = w_in.shape[0]
    xi, yi, ci = _me()
    dev = (4 * xi + 2 * yi + ci).astype(jnp.int32)
    dev1 = dev.reshape(1)
    tr = lambda a: jnp.swapaxes(a, 1, 2)
    w_t, m_t, v_t = tr(w_in), tr(m_w_in), tr(v_w_in)
    wb = {n: w[n].astype(MX) for n in _BIG if n != "w_in"}
    wb["w_in"] = w_t.astype(MX)
    layer_shards = lambda l: [wb["w_out"][l], wb["w_up"][l], wb["w_down"][l]]

    _, ((g_in0, g_cw, g_lcw),) = _call(None, "gather_first", None, [], [], [], [], [],
                                        [_gather_rider([wb["w_in"][0], conv_dw_w, lru_conv_w])])
    cols = lambda g: jnp.moveaxis(g, 0, -2).reshape(g.shape[1:-1] + (N_DEV * g.shape[-1],))
    p = dict(w)
    p["conv_dw_w"] = cols(g_cw)
    p["lru_conv_w"] = cols(g_lcw)
    lp = [_layer_params(p, l) for l in range(depth)]

    gathered = [dict(w_in=g_in0.reshape(IN_W, D_MODEL)), dict()]
    saved = []
    h = x[0]
    for l in range(depth):
        q, gw = lp[l], gathered[l]
        z, hn1 = _ln_in(h, q["g1"], gw["w_in"], l == 0, f"ln_in{l}")
        riders = [_gather_rider(layer_shards(0))] if l == 0 else []
        (ycat, hl, uc, probs, psinks), got = _mixer_fwd(z, q["sink"], q["cw"], q["pv"], q["wa"], q["wx"],
                                                        f"mixer_fwd{l}", riders)
        if l == 0:
            gw["w_out"], gw["w_up"], gw["w_down"] = got[0]
            gw["w_out"] = gw["w_out"].reshape(D_MODEL, D_MODEL)
        riders = [_gather_rider([wb["w_in"][1]] + layer_shards(1))] if l == 0 else []
        (h1, act, h2, ym, hn2), got = _post_fwd(ycat, h, q["gmix"], gw["w_out"], q["g2"], gw["w_up"],
                                                gw["w_down"].reshape(D_FF, D_MODEL), f"post_fwd{l}", riders)
        if l == 0:
            nxt = gathered[1]
            nxt["w_in"], nxt["w_out"], nxt["w_up"], nxt["w_down"] = got[0]
            nxt["w_in"] = nxt["w_in"].reshape(IN_W, D_MODEL)
            nxt["w_out"] = nxt["w_out"].reshape(D_MODEL, D_MODEL)
        saved.append(dict(h0=h, z=z, hn1=hn1, ycat=ycat, hl=hl, uc=uc, probs=probs, psinks=psinks, h1=h1, act=act,
                          ym=ym, hn2=hn2))
        h = h2
    dh, loss, dgf = _loss_head(h, final_norm.reshape(1, -1), loss_target[0], "loss_head")

    grads = [None] * depth
    big = {n: [None] * depth for n in _BIG}
    pending = []

    def send_pending():
        riders = [_scatter_rider([item[3] for item in pending])] if pending else []
        return riders, list(pending)

    def record(sent, got):
        for item, recv in zip(sent, got[0] if sent else []):
            big[item[0]][item[1]] = (item[2], recv)
        del pending[:len(sent)]

    for l in reversed(range(depth)):
        q, s, gw = lp[l], saved[l], gathered[l]
        riders, sent = send_pending()
        w_up_t = jnp.swapaxes(gw["w_up"], 1, 2).reshape(D_FF, D_MODEL)
        (dh1, dh1b, dhb, du, dg2), got = _ffn_bwd(dh, s["act"], s["h1"], q["g2"], w_up_t, gw["w_down"],
                                                  f"ffn_bwd{l}", riders)
        record(sent, got)
        dycat, dgm, d_wout = _mix_bwd(dh1b, s["ycat"], s["ym"], q["gmix"], gw["w_out"], f"mix_bwd{l}")
        pending.append(("w_down", l) + tuple(_dw(s["act"], dhb, f"dw_down{l}", "rows", 2048, D_MODEL)))
        pending.append(("w_up", l) + tuple(_dw(s["hn2"], du, f"dw_up{l}", "cols", D_MODEL, 2048)))
        pending.append(("w_out", l) + tuple(d_wout))
        riders, sent = send_pending()
        (dz, dsink, dcw, dpv, dwa, dwx), got = _mixer_bwd(
            dycat, s["z"], s["ycat"], s["hl"], s["uc"], s["probs"], s["psinks"], q["sink"], q["cw"], q["pv"], q["wa"],
            q["wx"], f"mixer_bwd{l}", riders)
        record(sent, got)
        if l > 0:
            dh, dg1, d_win = _in_bwd_dw(dz, s["h0"], dh1, q["g1"], gw["w_in"], f"in_bwd{l}")
            pending.append(("w_in", l) + tuple(d_win))
        else:
            d_win = _dw(dz, s["hn1"], f"dw_in{l}", "rows", IN_W, D_MODEL)
            win_sends = _send_start([d_win[1]], True, "scatter_w_in0_start")
            dh, dg1 = _in_bwd(dz, s["h0"], dh1, q["g1"], gw["w_in"], win_sends.token, f"in_bwd{l}")
        grads[l] = dict(
            norm1=dg1[0], attn_sinks=jnp.stack([dsink[0:4, 0], dsink[0:4, 2 * BLK]], axis=1).reshape(8),
            conv_dw_w=dcw[0:CONV_K], conv_dw_b=dpv[R_CONV_B], conv_ln_g=dpv[R_LN_G], conv_ln_b=dpv[R_LN_B],
            lru_conv_w=dpv[R_LCW:R_LCW + LRU_K], lru_conv_b=dpv[R_LCONV_B], lru_wa=_unblock_diag(dwa),
            lru_ba=dpv[R_BA].reshape(4, 64), lru_wx=_unblock_diag(dwx), lru_bx=dpv[R_BX].reshape(4, 64),
            lru_lambda=dpv[R_LAM], mix_norm=dgm[0], norm2=dg2[0])

    small = [jnp.stack([grads[l][n] for l in range(depth)]) for n in _SMALL] + [dgf, loss[:, 0:1]]

    def as_rows(a):
        flat = a.reshape(-1)
        pad = (-flat.size) % 1024
        if pad:
            flat = jnp.concatenate([flat, jnp.zeros((pad,), F32)])
        return flat.reshape(-1, 128)

    pieces = [as_rows(a) for a in small]
    packed = jnp.concatenate(pieces, axis=0)
    small_sends = _send_start([packed], False, "bcast_small_start")

    out = {}
    shard_update = lambda n, wmv, after: list(_adamw_shard(
        [big[n][l][0] for l in range(depth)], [big[n][l][1] for l in range(depth)], dev1, *wmv, after, f"adamw_{n}"))
    for n in ("w_out", "w_up", "w_down"):
        out[n] = shard_update(n, (w[n], m[n], v[n]), small_sends.token)
    _, (win_recv,) = _send_wait(win_sends, out["w_down"][1], "scatter_w_in0_wait")
    big["w_in"][0] = (d_win[0], win_recv)
    out["w_in"] = [tr(a) for a in shard_update("w_in", (w_t, m_t, v_t), jnp.zeros((8, 128), F32))]
    (packed,), (small_recv,) = _send_wait(small_sends, out["w_in"][1], "bcast_small_wait")
    summed = _sum_parts(packed, small_recv, dev1, "sum_small_grads")
    small_sums, row = [], 0
    for a, piece in zip(small, pieces):
        got = summed[row:row + piece.shape[0]]
        small_sums.append(got.reshape(a.shape) if a.size == piece.size else got.reshape(-1)[:a.size].reshape(a.shape))
        row += piece.shape[0]
    shard = lambda a: lax.dynamic_slice_in_dim(a, dev * (a.shape[-1] // N_DEV), a.shape[-1] // N_DEV, axis=a.ndim - 1)
    flat = {"lru_wa": (depth, LRU_W, 64), "lru_wx": (depth, LRU_W, 64), "final_norm": (1, D_MODEL)}
    gs, ws, ms, vs = [], [], [], []
    for n, g in zip(_SMALL + ["final_norm"], small_sums[:-1]):
        shp = flat.get(n, w[n].shape)
        gs.append((shard(g) if n in ("conv_dw_w", "lru_conv_w") else g).reshape(shp))
        ws.append(w[n].reshape(shp))
        ms.append(m[n].reshape(shp))
        vs.append(v[n].reshape(shp))
    sd, sm, sv = _adamw_small(gs, ws, ms, vs, "adamw_small")
    for j, n in enumerate(_SMALL + ["final_norm"]):
        out[n] = [a.reshape(w[n].shape) for a in (gs[j], sd[j], sm[j], sv[j])]
    loss_total = small_sums[-1][0, 0]

    result = [loss_total, dh[None]]
    for j in range(4):
        result += [out[n][j] for n in _WEIGHTS]
    return tuple(result)
```

```python
import types

import jax
import jax.numpy as jnp
from jax import lax
from jax.experimental import pallas as pl
from jax.experimental.pallas import tpu as pltpu

F32 = jnp.float32
MX = jnp.bfloat16
WIRE = jnp.bfloat16

D_MODEL = 1024
HEAD_DIM = 64
ATTN_W = 512
KV_W = 128
BLK = 128
CONV_W = 256
CONV_K = 31
LRU_W = 256
LRU_K = 4
LRU_C = 8.0
IN_W = 1792
D_FF = 4096
FF_BLK = 512
N_DEV = 8
IN_SHARD = IN_W // N_DEV
RMS_EPS = 1e-6
LN_EPS = 1e-5
MASK_VALUE = -1e30
SCALE = HEAD_DIM ** -0.5
CONV_HALO = 32
LRU_HALO = 8
CONV_CHUNK = 64
POST_TILE = 512
STREAM_TILE = 1024
DW_TILE = 1024
Q0, K0, V0, CV0, CG0, RX0, RG0 = 0, 512, 640, 768, 1024, 1280, 1536
R_CONV_B, R_LN_G, R_LN_B, R_LCONV_B, R_BA, R_BX, R_LAM, R_LCW = 0, 1, 2, 3, 4, 5, 6, 8

ADAM_LR, ADAM_B1, ADAM_B2, ADAM_EPS, ADAM_WD, ADAM_STEP = 0.001, 0.9, 0.999, 1e-08, 0.01, 10

VMEM_LIMIT = 56 * 1024 * 1024
MESH = pl.DeviceIdType.MESH
ANY = pl.BlockSpec(memory_space=pl.ANY)


def _tile(t, cap=512):
    return min(cap, t)


def _dot(a, b):
    return jnp.dot(a.astype(MX), b.astype(MX), preferred_element_type=F32)


def _dot_nt(a, b):
    return lax.dot_general(a.astype(MX), b.astype(MX), (((1,), (1,)), ((), ())), preferred_element_type=F32)


def _dot_tn(a, b):
    return lax.dot_general(a.astype(MX), b.astype(MX), (((0,), (0,)), ((), ())), preferred_element_type=F32)


def _const_spec(shape):
    nd = len(shape)
    return pl.BlockSpec(shape, lambda *_: (0,) * nd, pipeline_mode=pl.Buffered(1))


def _acc_spec(shape):
    nd = len(shape)
    return pl.BlockSpec(shape, lambda *_: (0,) * nd)


def _sds(shape, dtype):
    return jax.ShapeDtypeStruct(shape, dtype)


def _sigmoid(x):
    return jax.nn.sigmoid(x)


def _rms_fwd(x, g):
    r = lax.rsqrt(jnp.mean(x * x, axis=-1, keepdims=True) + RMS_EPS)
    xh = x * r
    return xh * g, xh, r


def _rms_bwd(dy, xh, r, g):
    t = dy * g
    dx = r * (t - xh * jnp.mean(t * xh, axis=-1, keepdims=True))
    return dx, jnp.sum(dy * xh, axis=0, keepdims=True)


_GROUPS = ((0, 512), (512, 768), (768, 1024))


def _group_rms_fwd(y, g):
    parts = [_rms_fwd(y[:, a:b], g[:, a:b]) for a, b in _GROUPS]
    return (jnp.concatenate([p[0] for p in parts], axis=1),
            jnp.concatenate([p[1] for p in parts], axis=1),
            [p[2] for p in parts])


def _gelu(x):
    c = 0.7978845608028654
    u = c * (x + 0.044715 * x * x * x)
    th = jnp.tanh(u)
    val = 0.5 * x * (1.0 + th)
    grad = 0.5 * (1.0 + th) + 0.5 * x * (1.0 - th * th) * c * (1.0 + 3.0 * 0.044715 * x * x)
    return val, grad


def _neg_expm1(x):
    series = -x * (1.0 + x * (0.5 + x * (1.0 / 6.0 + x * (1.0 / 24.0))))
    return jnp.where(x > -0.02, series, 1.0 - jnp.exp(x))


def _me():
    return lax.axis_index("x"), lax.axis_index("y"), lax.axis_index("c")


def _gather_rider(arrays):
    arrays = list(arrays)
    n = len(arrays)

    def plan(ins, outs, sems):
        ssem, rsem, lsem = sems
        x, y, c = _me()
        chips = [(1 - x, y), (x, 1 - y), (1 - x, 1 - y)]

        def copy(a, k, block, to, own=False):
            dst = outs[a].at[4 * block[0] + 2 * block[1] + block[2]]
            return pltpu.make_async_remote_copy(
                src_ref=ins[a] if own else dst, dst_ref=dst, send_sem=ssem.at[7 * a + k],
                recv_sem=rsem.at[7 * a + k], device_id=to, device_id_type=MESH)

        return x, y, c, chips, copy, lsem

    def start(ins, outs, sems):
        x, y, c, chips, copy, lsem = plan(ins, outs, sems)
        for a in range(n):
            pltpu.make_async_copy(ins[a], outs[a].at[4 * x + 2 * y + c], lsem.at[a]).start()
            copy(a, 0, (x, y, c), (x, y, 1 - c), own=True).start()
            for j, chip in enumerate(chips):
                copy(a, 1 + j, (x, y, c), (*chip, c), own=True).start()

    def mid(ins, outs, sems):
        x, y, c, chips, copy, _ = plan(ins, outs, sems)
        for a in range(n):
            for j, chip in enumerate(chips):
                copy(a, 1 + j, (*chip, c), (x, y, c)).wait_recv()
                copy(a, 4 + j, (*chip, c), (x, y, 1 - c)).start()

    def finish(ins, outs, sems):
        x, y, c, chips, copy, lsem = plan(ins, outs, sems)
        for a in range(n):
            copy(a, 0, (x, y, 1 - c), (x, y, c)).wait_recv()
            for j, chip in enumerate(chips):
                copy(a, 4 + j, (*chip, 1 - c), (x, y, c)).wait_recv()
        for a in range(n):
            copy(a, 0, (x, y, c), (x, y, 1 - c), own=True).wait_send()
            for j, chip in enumerate(chips):
                copy(a, 1 + j, (x, y, c), (*chip, c), own=True).wait_send()
                copy(a, 4 + j, (*chip, c), (x, y, 1 - c)).wait_send()
            pltpu.make_async_copy(ins[a], outs[a].at[4 * x + 2 * y + c], lsem.at[a]).wait()

    return types.SimpleNamespace(
        arrays=arrays, out_shape=[_sds((N_DEV,) + a.shape, a.dtype) for a in arrays],
        scratch=[pltpu.SemaphoreType.DMA((7 * n,)), pltpu.SemaphoreType.DMA((7 * n,)), pltpu.SemaphoreType.DMA((n,))],
        start=start, mid=mid, finish=finish)


def _scatter_rider(arrays):
    arrays = list(arrays)
    n = len(arrays)

    def copies(ins, outs, sems):
        ssem, rsem = sems
        x, y, c = _me()
        out = []
        for a in range(n):
            for f in range(1, N_DEV):
                px = 1 - x if f & 4 else x
                py = 1 - y if f & 2 else y
                pc = 1 - c if f & 1 else c
                out.append(pltpu.make_async_remote_copy(
                    src_ref=ins[a].at[4 * px + 2 * py + pc], dst_ref=outs[a].at[f - 1], send_sem=ssem.at[7 * a + f - 1],
                    recv_sem=rsem.at[7 * a + f - 1], device_id=(px, py, pc), device_id_type=MESH))
        return out

    def start(ins, outs, sems):
        for cp in copies(ins, outs, sems):
            cp.start()

    def finish(ins, outs, sems):
        for cp in copies(ins, outs, sems):
            cp.wait()

    return types.SimpleNamespace(
        arrays=arrays, out_shape=[_sds((N_DEV - 1,) + a.shape[1:], a.dtype) for a in arrays],
        scratch=[pltpu.SemaphoreType.DMA((7 * n,)), pltpu.SemaphoreType.DMA((7 * n,))],
        start=start, mid=None, finish=finish)


def _call(body, name, grid, in_specs, out_specs, out_shape, scratch, operands, riders=()):
    n_in, n_out, n_scr = len(operands), len(out_shape), len(scratch)
    nsteps = grid[0] if grid else 1
    sizes = [(len(r.arrays), len(r.out_shape), len(r.scratch)) for r in riders]

    def wrapped(*refs):
        pos = n_in
        r_ins = []
        for ri, _, _ in sizes:
            r_ins.append(refs[pos:pos + ri])
            pos += ri
        outs = refs[pos:pos + n_out]
        pos += n_out
        r_outs = []
        for _, ro, _ in sizes:
            r_outs.append(refs[pos:pos + ro])
            pos += ro
        scr = refs[pos:pos + n_scr]
        pos += n_scr
        r_sems = []
        for _, _, rs in sizes:
            r_sems.append(refs[pos:pos + rs])
            pos += rs
        step = pl.program_id(0) if grid else 0

        def at(s, fn):
            if grid:
                pl.when(step == s)(fn)
            else:
                fn()

        for r, a, b, c in zip(riders, r_ins, r_outs, r_sems):
            at(0, lambda r=r, a=a, b=b, c=c: r.start(a, b, c))
        for r, a, b, c in zip(riders, r_ins, r_outs, r_sems):
            if r.mid is not None:
                at((3 * nsteps) // 4, lambda r=r, a=a, b=b, c=c: r.mid(a, b, c))
        if body is not None:
            body(*refs[:n_in], *outs, *scr)
        for r, a, b, c in zip(riders, r_ins, r_outs, r_sems):
            at(nsteps - 1, lambda r=r, a=a, b=b, c=c: r.finish(a, b, c))

    r_arrays = [a for r in riders for a in r.arrays]
    r_shapes = [s for r in riders for s in r.out_shape]
    kwargs = {}
    if grid:
        kwargs = dict(grid=grid, compiler_params=pltpu.CompilerParams(
            dimension_semantics=("arbitrary",) * len(grid), vmem_limit_bytes=VMEM_LIMIT))
    res = pl.pallas_call(
        wrapped, name=name,
        in_specs=list(in_specs) + [ANY] * len(r_arrays),
        out_specs=list(out_specs) + [ANY] * len(r_shapes),
        out_shape=list(out_shape) + r_shapes,
        scratch_shapes=list(scratch) + [s for r in riders for s in r.scratch],
        **kwargs,
    )(*operands, *r_arrays)
    host, rest = res[:n_out], res[n_out:]
    r_res = []
    for _, ro, _ in sizes:
        r_res.append(rest[:ro])
        rest = rest[ro:]
    return host, r_res


def _ln_in(h, g1, w_in_t, keep_hn, name):
    t = h.shape[0]
    tm = _tile(t, STREAM_TILE)

    def body(h_ref, g_ref, w_ref, z_ref, *hn_ref):
        y, _, _ = _rms_fwd(h_ref[...], g_ref[...])
        hn = y.astype(MX)
        if keep_hn:
            hn_ref[0][...] = hn
        z_ref[...] = _dot_nt(hn, w_ref[...])

    tile = lambda w: pl.BlockSpec((tm, w), lambda i: (i, 0))
    outs, _ = _call(
        body, name, (t // tm,),
        [tile(D_MODEL), _const_spec((1, D_MODEL)), _const_spec((IN_W, D_MODEL))],
        [tile(IN_W)] + [tile(D_MODEL)] * keep_hn,
        [_sds((t, IN_W), F32)] + [_sds((t, D_MODEL), MX)] * keep_hn, [], [h, g1, w_in_t])
    return outs[0], (outs[1] if keep_hn else None)


def _band2(kb, g):
    lo = lax.broadcasted_iota(jnp.int32, kb.shape, 1) < HEAD_DIM
    kr = pltpu.roll(kb, HEAD_DIM, 1)
    if g == 0:
        top, bot = jnp.where(lo, kb, 0.0), jnp.where(lo, 0.0, kr)
    else:
        top, bot = jnp.where(lo, kr, 0.0), jnp.where(lo, 0.0, kb)
    return jnp.concatenate([top, bot], axis=0)


def _attn_operands(z_ref, zh_ref, b):
    rows = slice(b * BLK, (b + 1) * BLK)
    prev = zh_ref if b == 0 else z_ref
    prow = slice(0, BLK) if b == 0 else slice((b - 1) * BLK, b * BLK)
    kb = jnp.concatenate([prev[prow, K0:K0 + KV_W], z_ref[rows, K0:K0 + KV_W]], axis=0)
    vb = jnp.concatenate([prev[prow, V0:V0 + KV_W], z_ref[rows, V0:V0 + KV_W]], axis=0)
    k2 = [_band2(kb, g) for g in range(2)]
    v2 = [_band2(vb, g) for g in range(2)]
    q2 = [jnp.concatenate([z_ref[rows, (2 * g) * BLK:(2 * g + 1) * BLK], z_ref[rows, (2 * g + 1) * BLK:(2 * g + 2) * BLK]],
                          axis=0) for g in range(2)]
    return q2, k2, v2


def _attn_block(z_ref, zh_ref, sink_ref, b, first):
    q2, k2, v2 = _attn_operands(z_ref, zh_ref, b)
    rr = lax.broadcasted_iota(jnp.int32, (4 * BLK, 2 * BLK), 0) & (BLK - 1)
    cc = lax.broadcasted_iota(jnp.int32, (4 * BLK, 2 * BLK), 1)
    first_block = jnp.logical_and(first, b == 0).astype(jnp.int32)
    mask = jnp.logical_and(jnp.logical_and(cc > rr, cc <= rr + BLK), cc >= BLK * first_block)
    s = jnp.concatenate([_dot_nt(q2[g], k2[g]) for g in range(2)], axis=0) * SCALE
    w = 2 * BLK
    out, psink = [], []
    for hh in range(2):
        sh = jnp.where(mask, s[:, hh * w:(hh + 1) * w], MASK_VALUE)
        sk = jnp.concatenate([jnp.broadcast_to(sink_ref[p:p + 1, hh * w:hh * w + 1], (BLK, 1)) for p in range(4)], axis=0)
        m = jnp.maximum(jnp.max(sh, axis=1, keepdims=True), sk)
        p = jnp.exp(sh - m)
        es = jnp.exp(sk - m)
        inv = 1.0 / (jnp.sum(p, axis=1, keepdims=True) + es)
        out.append(p * inv)
        psink.append(es * inv)
    return v2, jnp.concatenate(out, axis=1), psink


def _scan_steps(a, b, n, span, reverse):
    pos = lax.broadcasted_iota(jnp.int32, a.shape, 0) & (span - 1)
    d = 1
    while d < span:
        keep = pos < span - d if reverse else pos >= d
        shift = n - d if reverse else d
        a_sh = jnp.where(keep, pltpu.roll(a, shift, 0), 1.0)
        b_sh = jnp.where(keep, pltpu.roll(b, shift, 0), 0.0)
        b = a * b_sh + b
        a = a * a_sh
        d *= 2
    return a, b


def _scan(a, b, tm, reverse):
    return _scan_steps(a, b, tm, tm, reverse)


def _shifted_copies(ext, shifts, tm):
    rows = tm + CONV_HALO - 8
    for r in range(1, 8):
        shifts[r - 1, 0:rows, :] = ext[pl.ds(r, rows), :]


def _tap(ext, shifts, off, r0, n):
    a, r = divmod(off, 8)
    lo = 8 * a + r0
    if r == 0:
        return ext[lo:lo + n, :]
    return shifts[r - 1, lo:lo + n, :]


def _glu_fill(z_ref, zh_ref, uext, ush, first, tm, sg_out=None):
    cv = z_ref[:, CV0:CV0 + CONV_W]
    sg = _sigmoid(z_ref[:, CG0:CG0 + CONV_W])
    if sg_out is not None:
        sg_out[...] = sg
    hrow = BLK - CONV_HALO
    uh = zh_ref[hrow:BLK, CV0:CV0 + CONV_W] * _sigmoid(zh_ref[hrow:BLK, CG0:CG0 + CONV_W])
    uext[0:CONV_HALO, :] = jnp.where(first, 0.0, uh)
    uext[CONV_HALO:CONV_HALO + tm, :] = cv * sg
    _shifted_copies(uext, ush, tm)


def _conv_taps(cw_ref, pv_ref, uext, ush, out_ref, tm):
    for r0 in range(0, tm, CONV_CHUNK):
        acc = jnp.broadcast_to(pv_ref[R_CONV_B:R_CONV_B + 1, :], (CONV_CHUNK, CONV_W))
        for k in range(CONV_K):
            acc = acc + cw_ref[k:k + 1, :] * _tap(uext, ush, CONV_HALO - (CONV_K - 1) + k, r0, CONV_CHUNK)
        out_ref[r0:r0 + CONV_CHUNK, :] = acc


def _ln_silu(uc, pv_ref):
    mu = jnp.mean(uc, axis=-1, keepdims=True)
    xc = uc - mu
    rs = lax.rsqrt(jnp.mean(xc * xc, axis=-1, keepdims=True) + LN_EPS)
    xh = xc * rs
    ln = xh * pv_ref[R_LN_G:R_LN_G + 1, :] + pv_ref[R_LN_B:R_LN_B + 1, :]
    sg = _sigmoid(ln)
    return xh, rs, ln, sg


def _lru_gates(z_ref, zh_ref, pv_ref, wa_ref, wx_ref, rxext, first, tm):
    rxext[0:LRU_HALO, :] = jnp.where(first, 0.0, zh_ref[BLK - LRU_HALO:BLK, RX0:RX0 + LRU_W])
    rxext[LRU_HALO:LRU_HALO + tm, :] = z_ref[:, RX0:RX0 + LRU_W]
    xc = jnp.broadcast_to(pv_ref[R_LCONV_B:R_LCONV_B + 1, :], (tm, LRU_W))
    for k in range(LRU_K):
        xc = xc + pv_ref[R_LCW + k:R_LCW + k + 1, :] * rxext[pl.ds(LRU_HALO - (LRU_K - 1) + k, tm), :]
    r = _sigmoid(_dot(xc, wa_ref[...]) + pv_ref[R_BA:R_BA + 1, :])
    ig = _sigmoid(_dot(xc, wx_ref[...]) + pv_ref[R_BX:R_BX + 1, :])
    lam = pv_ref[R_LAM:R_LAM + 1, :]
    sp = jnp.log1p(jnp.exp(-lam))
    la = (-LRU_C * r) * sp
    a = jnp.exp(la)
    mult = jnp.sqrt(_neg_expm1(2.0 * la))
    return xc, r, ig, sp, la, a, mult


def _mixer_in_specs(tm, tile_of):
    hb = tm // BLK
    return [
        pl.BlockSpec((tm, IN_W), lambda i: (tile_of(i), 0)),
        pl.BlockSpec((BLK, IN_W), lambda i: (jnp.maximum(tile_of(i) * hb - 1, 0), 0)),
        _const_spec((8, 4 * BLK)),
        _const_spec((32, CONV_W)),
        _const_spec((16, CONV_W)),
        _const_spec((LRU_W, LRU_W)),
        _const_spec((LRU_W, LRU_W)),
    ]


def _mixer_fwd(z, sink, cw, pv, wa, wx, name, riders=()):
    t = z.shape[0]
    tm = _tile(t)
    nb = tm // BLK

    def body(z_ref, zh_ref, sink_ref, cw_ref, pv_ref, wa_ref, wx_ref, y_ref, hl_ref, uc_ref, p_ref, ps_ref,
             uext, ush, rxext, hcar):
        i = pl.program_id(0)
        first = i == 0

        @pl.when(first)
        def _():
            hcar[...] = jnp.zeros_like(hcar)

        lo = lax.broadcasted_iota(jnp.int32, (4 * BLK, BLK), 1) < HEAD_DIM
        for b in range(nb):
            rows = slice(b * BLK, (b + 1) * BLK)
            v2, prob, psink = _attn_block(z_ref, zh_ref, sink_ref, b, first)
            prob = prob.astype(MX)
            p_ref[b] = prob
            ps_ref[b] = jnp.where(lo, psink[0], psink[1])
            for g in range(2):
                o = _dot(prob[2 * g * BLK:(2 * g + 2) * BLK], v2[g])
                y_ref[rows, (2 * g) * BLK:(2 * g + 1) * BLK] = o[0:BLK]
                y_ref[rows, (2 * g + 1) * BLK:(2 * g + 2) * BLK] = o[BLK:2 * BLK]
        _glu_fill(z_ref, zh_ref, uext, ush, first, tm)
        _conv_taps(cw_ref, pv_ref, uext, ush, uc_ref, tm)
        _, _, ln, sg = _ln_silu(uc_ref[...], pv_ref)
        y_ref[:, ATTN_W:ATTN_W + CONV_W] = ln * sg
        xc, _, ig, _, _, a, mult = _lru_gates(z_ref, zh_ref, pv_ref, wa_ref, wx_ref, rxext, first, tm)
        acum, h = _scan(a, mult * (ig * xc), tm, reverse=False)
        h = h + acum * hcar[0:1, :]
        hl_ref[...] = h
        hcar[0:1, :] = h[tm - 1:tm, :]
        gl, _ = _gelu(z_ref[:, RG0:RG0 + LRU_W])
        y_ref[:, ATTN_W + CONV_W:ATTN_W + CONV_W + LRU_W] = h * gl

    tile = lambda w: pl.BlockSpec((tm, w), lambda i: (i, 0))
    return _call(
        body, name, (t // tm,), _mixer_in_specs(tm, lambda i: i),
        [tile(D_MODEL), tile(LRU_W), tile(CONV_W), pl.BlockSpec((nb, 4 * BLK, 4 * BLK), lambda i: (i, 0, 0)),
         pl.BlockSpec((nb, 4 * BLK, BLK), lambda i: (i, 0, 0))],
        [_sds((t, D_MODEL), F32), _sds((t, LRU_W), F32), _sds((t, CONV_W), F32),
         _sds((t // BLK, 4 * BLK, 4 * BLK), MX), _sds((t // BLK, 4 * BLK, BLK), F32)],
        [pltpu.VMEM((tm + CONV_HALO, CONV_W), F32), pltpu.VMEM((7, tm + CONV_HALO - 8, CONV_W), F32),
         pltpu.VMEM((tm + LRU_HALO, LRU_W), F32), pltpu.VMEM((8, LRU_W), F32)],
        [z, z, sink, cw, pv, wa, wx], riders)


def _mixer_bwd(dy, z, ycat, hl, uc, probs, psinks, sink, cw, pv, wa, wx, name, riders=()):
    t = z.shape[0]
    tm = _tile(t)
    nt = t // tm
    nb = tm // BLK
    rev = lambda i: nt - 1 - i

    def body(dy_ref, z_ref, zh_ref, sink_ref, cw_ref, pv_ref, wa_ref, wx_ref, y_ref, hl_ref, hlh_ref, uc_ref,
             p_ref, ps_ref, dz_ref, dsink_ref, dcw_ref, dpv_ref, dwa_ref, dwx_ref,
             uext, ush, sgs, rxext, dkext, dvext, ducext, dsh, dcw8, dxcext, kcar, vcar, uccar, xccar, gcar):
        i = pl.program_id(0)
        first = i == nt - 1

        @pl.when(i == 0)
        def _():
            for car in (kcar, vcar, uccar, xccar, gcar, dcw8):
                car[...] = jnp.zeros_like(car)
            for acc in (dsink_ref, dpv_ref, dwa_ref, dwx_ref):
                acc[...] = jnp.zeros_like(acc)

        def addrow(r, val):
            dpv_ref[r:r + 1, :] += jnp.sum(val, axis=0, keepdims=True)

        dkext[:, 0:tm] = jnp.zeros((KV_W, tm), F32)
        dvext[:, 0:tm] = jnp.zeros((KV_W, tm), F32)
        dkext[:, tm:tm + BLK] = kcar[...]
        dvext[:, tm:tm + BLK] = vcar[...]
        lane512 = lax.broadcasted_iota(jnp.int32, (1, 4 * BLK), 1) < 2 * BLK
        lo = lax.broadcasted_iota(jnp.int32, (4 * BLK, BLK), 1) < HEAD_DIM
        hd, w2 = HEAD_DIM, 2 * BLK
        for b in range(nb):
            rows = slice(b * BLK, (b + 1) * BLK)
            band = slice(b * BLK, (b + 2) * BLK)
            q2, k2, v2 = _attn_operands(z_ref, zh_ref, b)
            prob = p_ref[b]
            psink = [ps_ref[b, :, 0:1], ps_ref[b, :, HEAD_DIM:HEAD_DIM + 1]]
            stack = lambda ref: jnp.concatenate([ref[rows, p * BLK:(p + 1) * BLK] for p in range(4)], axis=0)
            do4 = stack(dy_ref)
            dlt = do4 * stack(y_ref)
            d0 = jnp.sum(jnp.where(lo, dlt, 0.0), axis=1, keepdims=True)
            d1 = jnp.sum(jnp.where(lo, 0.0, dlt), axis=1, keepdims=True)
            dp = jnp.concatenate([_dot_nt(do4[g * w2:(g + 1) * w2], v2[g]) for g in range(2)], axis=0)
            dl = jnp.concatenate([jnp.broadcast_to(d0, (4 * BLK, w2)), jnp.broadcast_to(d1, (4 * BLK, w2))], axis=1)
            draw = (prob * (dp - dl)) * SCALE
            e0, e1 = psink[0] * d0, psink[1] * d1
            for p in range(4):
                prs = slice(p * BLK, (p + 1) * BLK)
                s0 = jnp.sum(e0[prs], axis=0, keepdims=True)
                s1 = jnp.sum(e1[prs], axis=0, keepdims=True)
                dsink_ref[p:p + 1, :] += -jnp.where(lane512, s0, s1)
            for g in range(2):
                grs = slice(g * w2, (g + 1) * w2)
                dq = _dot(draw[grs], k2[g])
                dz_ref[rows, (2 * g) * BLK:(2 * g + 1) * BLK] = dq[0:BLK].astype(dz_ref.dtype)
                dz_ref[rows, (2 * g + 1) * BLK:(2 * g + 2) * BLK] = dq[BLK:2 * BLK].astype(dz_ref.dtype)
                tk = _dot_tn(q2[g], draw[grs])
                tv = _dot_tn(do4[grs], prob[grs])
                dkext[g * hd:(g + 1) * hd, band] += tk[0:hd, 0:w2] + tk[hd:2 * hd, w2:2 * w2]
                dvext[g * hd:(g + 1) * hd, band] += tv[0:hd, 0:w2] + tv[hd:2 * hd, w2:2 * w2]
        dz_ref[:, K0:K0 + KV_W] = jnp.transpose(dkext[:, BLK:BLK + tm]).astype(dz_ref.dtype)
        dz_ref[:, V0:V0 + KV_W] = jnp.transpose(dvext[:, BLK:BLK + tm]).astype(dz_ref.dtype)
        kcar[...] = dkext[:, 0:BLK]
        vcar[...] = dvext[:, 0:BLK]

        _glu_fill(z_ref, zh_ref, uext, ush, first, tm, sg_out=sgs)
        xh, rs, ln, sg = _ln_silu(uc_ref[...], pv_ref)
        dln = dy_ref[:, ATTN_W:ATTN_W + CONV_W] * (sg * (1.0 + ln * (1.0 - sg)))
        addrow(R_LN_G, dln * xh)
        addrow(R_LN_B, dln)
        dxh = dln * pv_ref[R_LN_G:R_LN_G + 1, :]
        duc = rs * (dxh - jnp.mean(dxh, axis=-1, keepdims=True) - xh * jnp.mean(dxh * xh, axis=-1, keepdims=True))
        addrow(R_CONV_B, duc)
        ducext[0:tm, :] = duc
        ducext[tm:tm + CONV_HALO, :] = uccar[...]
        uccar[...] = duc[0:CONV_HALO, :]
        _shifted_copies(ducext, dsh, tm)
        for r0 in range(0, tm, CONV_CHUNK):
            crow = slice(r0, r0 + CONV_CHUNK)
            duc_c = ducext[crow, :]
            du = jnp.zeros((CONV_CHUNK, CONV_W), F32)
            for k in range(CONV_K):
                prod = duc_c * _tap(uext, ush, CONV_HALO - (CONV_K - 1) + k, r0, CONV_CHUNK)
                part = prod[0:8]
                for s in range(8, CONV_CHUNK, 8):
                    part = part + prod[s:s + 8]
                dcw8[k] += part
                du = du + cw_ref[k:k + 1, :] * _tap(ducext, dsh, CONV_K - 1 - k, r0, CONV_CHUNK)
            sgc = sgs[crow, :]
            dz_ref[crow, CV0:CV0 + CONV_W] = (du * sgc).astype(dz_ref.dtype)
            u_c = uext[CONV_HALO + r0:CONV_HALO + r0 + CONV_CHUNK, :]
            dz_ref[crow, CG0:CG0 + CONV_W] = (du * u_c * (1.0 - sgc)).astype(dz_ref.dtype)

        @pl.when(i == nt - 1)
        def _():
            dcw_ref[...] = jnp.sum(dcw8[...], axis=1)

        xc, r, ig, sp, la, a, mult = _lru_gates(z_ref, zh_ref, pv_ref, wa_ref, wx_ref, rxext, first, tm)
        h = hl_ref[...]
        rowi = lax.broadcasted_iota(jnp.int32, (tm, LRU_W), 0)
        hlast = jnp.where(first, 0.0, hlh_ref[7:8, :])
        hprev = jnp.where(rowi == 0, hlast, pltpu.roll(h, 1, 0))
        dyl = dy_ref[:, ATTN_W + CONV_W:ATTN_W + CONV_W + LRU_W]
        gl, dgl = _gelu(z_ref[:, RG0:RG0 + LRU_W])
        dz_ref[:, RG0:RG0 + LRU_W] = (dyl * h * dgl).astype(dz_ref.dtype)
        dh = dyl * gl + jnp.where(rowi == tm - 1, gcar[0:1, :], 0.0)
        c = jnp.where(rowi == tm - 1, 0.0, pltpu.roll(a, tm - 1, 0))
        _, gg = _scan(c, dh, tm, reverse=True)
        gcar[0:1, :] = a[0:1, :] * gg[0:1, :]
        dmult = gg * (ig * xc)
        dig = gg * mult * xc
        dxc = gg * mult * ig
        dla = gg * hprev * a - dmult * a * a / mult
        dr = dla * (-LRU_C * sp)
        lam = pv_ref[R_LAM:R_LAM + 1, :]
        dpv_ref[R_LAM:R_LAM + 1, :] += jnp.sum(dla * (-LRU_C * r), axis=0, keepdims=True) * (-_sigmoid(-lam))
        dpa = dr * r * (1.0 - r)
        dpx = dig * ig * (1.0 - ig)
        addrow(R_BA, dpa)
        addrow(R_BX, dpx)
        dxc = dxc + _dot_nt(dpa, wa_ref[...]) + _dot_nt(dpx, wx_ref[...])
        dwa_ref[...] += _dot_tn(xc, dpa)
        dwx_ref[...] += _dot_tn(xc, dpx)
        addrow(R_LCONV_B, dxc)
        dxcext[0:tm, :] = dxc
        dxcext[tm:tm + LRU_HALO, :] = xccar[...]
        xccar[...] = dxc[0:LRU_HALO, :]
        drx = jnp.zeros((tm, LRU_W), F32)
        for k in range(LRU_K):
            addrow(R_LCW + k, dxc * rxext[pl.ds(LRU_HALO - (LRU_K - 1) + k, tm), :])
            drx = drx + pv_ref[R_LCW + k:R_LCW + k + 1, :] * dxcext[pl.ds(LRU_K - 1 - k, tm), :]
        dz_ref[:, RX0:RX0 + LRU_W] = drx.astype(dz_ref.dtype)

    tile = lambda w: pl.BlockSpec((tm, w), lambda i: (rev(i), 0))
    in_specs = [tile(D_MODEL)] + _mixer_in_specs(tm, rev) + [
        tile(D_MODEL), tile(LRU_W),
        pl.BlockSpec((8, LRU_W), lambda i: (jnp.maximum(rev(i) * (tm // 8) - 1, 0), 0)),
        tile(CONV_W), pl.BlockSpec((nb, 4 * BLK, 4 * BLK), lambda i: (rev(i), 0, 0)),
        pl.BlockSpec((nb, 4 * BLK, BLK), lambda i: (rev(i), 0, 0))]
    return _call(
        body, name, (nt,), in_specs,
        [tile(IN_W), _acc_spec((8, 4 * BLK)), _acc_spec((32, CONV_W)), _acc_spec((16, CONV_W)),
         _acc_spec((LRU_W, LRU_W)), _acc_spec((LRU_W, LRU_W))],
        [_sds((t, IN_W), MX), _sds((8, 4 * BLK), F32), _sds((32, CONV_W), F32), _sds((16, CONV_W), F32),
         _sds((LRU_W, LRU_W), F32), _sds((LRU_W, LRU_W), F32)],
        [pltpu.VMEM((tm + CONV_HALO, CONV_W), F32), pltpu.VMEM((7, tm + CONV_HALO - 8, CONV_W), F32),
         pltpu.VMEM((tm, CONV_W), F32), pltpu.VMEM((tm + LRU_HALO, LRU_W), F32),
         pltpu.VMEM((KV_W, tm + BLK), F32), pltpu.VMEM((KV_W, tm + BLK), F32),
         pltpu.VMEM((tm + CONV_HALO, CONV_W), F32), pltpu.VMEM((7, tm + CONV_HALO - 8, CONV_W), F32),
         pltpu.VMEM((32, 8, CONV_W), F32), pltpu.VMEM((tm + LRU_HALO, LRU_W), F32),
         pltpu.VMEM((KV_W, BLK), F32), pltpu.VMEM((KV_W, BLK), F32),
         pltpu.VMEM((CONV_HALO, CONV_W), F32), pltpu.VMEM((LRU_HALO, LRU_W), F32), pltpu.VMEM((8, LRU_W), F32)],
        [dy, z, z, sink, cw, pv, wa, wx, ycat, hl, hl, uc, probs, psinks], riders)


def _post_fwd(ycat, h0, gmix, w_out, g2, w_up, w_down, name, riders=()):
    t = h0.shape[0]
    tm = _tile(t, POST_TILE)
    nj = D_FF // FF_BLK

    def body(y_ref, h_ref, gm_ref, wo_ref, g2_ref, wu_ref, wd_ref, h1_ref, a_ref, h2_ref, ym_ref, hn_ref):
        ym, _, _ = _group_rms_fwd(y_ref[...], gm_ref[...])
        ym = ym.astype(MX)
        ym_ref[...] = ym
        h1 = h_ref[...] + jnp.dot(ym, wo_ref[...], preferred_element_type=F32)
        h1_ref[...] = h1
        hn, _, _ = _rms_fwd(h1, g2_ref[...])
        hn = hn.astype(MX)
        hn_ref[...] = hn
        for j in range(nj):
            u = jnp.dot(hn, wu_ref[j], preferred_element_type=F32)
            a_ref[:, j * FF_BLK:(j + 1) * FF_BLK] = jnp.square(jnp.maximum(u, 0.0)).astype(MX)
        h2_ref[...] = h1 + jnp.dot(a_ref[...], wd_ref[...], preferred_element_type=F32)

    tile = lambda w: pl.BlockSpec((tm, w), lambda i: (i, 0))
    return _call(
        body, name, (t // tm,),
        [tile(D_MODEL), tile(D_MODEL), _const_spec((1, D_MODEL)), _const_spec((D_MODEL, D_MODEL)),
         _const_spec((1, D_MODEL)), _const_spec((nj, D_MODEL, FF_BLK)), _const_spec((D_FF, D_MODEL))],
        [tile(D_MODEL), tile(D_FF), tile(D_MODEL), tile(D_MODEL), tile(D_MODEL)],
        [_sds((t, D_MODEL), F32), _sds((t, D_FF), MX), _sds((t, D_MODEL), F32), _sds((t, D_MODEL), MX),
         _sds((t, D_MODEL), MX)],
        [], [ycat, h0, gmix, w_out, g2, w_up, w_down], riders)


def _ffn_bwd(dh2, act, h1, g2, w_up_t, w_down, name, riders=()):
    t = h1.shape[0]
    tm = _tile(t, POST_TILE)
    nj = D_FF // FF_BLK

    def body(dh2_ref, a_ref, h1_ref, g2_ref, wut_ref, wd_ref, dh1_ref, dh1b_ref, dh2b_ref, du_ref, dg2_ref):
        @pl.when(pl.program_id(0) == 0)
        def _():
            dg2_ref[...] = jnp.zeros_like(dg2_ref)

        dh2 = dh2_ref[...]
        dh2b = dh2.astype(MX)
        dh2b_ref[...] = dh2b
        for j in range(nj):
            cols = slice(j * FF_BLK, (j + 1) * FF_BLK)
            da = _dot_nt(dh2b, wd_ref[j])
            du_ref[:, cols] = (da * (2.0 * jnp.sqrt(a_ref[:, cols].astype(F32)))).astype(MX)
        dhn = jnp.dot(du_ref[...], wut_ref[...], preferred_element_type=F32)
        _, xh, r = _rms_fwd(h1_ref[...], g2_ref[...])
        dx, dg = _rms_bwd(dhn, xh, r, g2_ref[...])
        dg2_ref[...] += dg
        dh1 = dh2 + dx
        dh1_ref[...] = dh1
        dh1b_ref[...] = dh1.astype(MX)

    tile = lambda w: pl.BlockSpec((tm, w), lambda i: (i, 0))
    return _call(
        body, name, (t // tm,),
        [tile(D_MODEL), tile(D_FF), tile(D_MODEL), _const_spec((1, D_MODEL)),
         _const_spec((D_FF, D_MODEL)), _const_spec((nj, FF_BLK, D_MODEL))],
        [tile(D_MODEL), tile(D_MODEL), tile(D_MODEL), tile(D_FF), _acc_spec((1, D_MODEL))],
        [_sds((t, D_MODEL), F32), _sds((t, D_MODEL), MX), _sds((t, D_MODEL), MX), _sds((t, D_FF), MX),
         _sds((1, D_MODEL), F32)],
        [], [dh2, act, h1, g2, w_up_t, w_down], riders)


def _mix_bwd(dh1, ycat, ym, gmix, w_out, name):
    t = dh1.shape[0]
    tm = _tile(t)
    nk = t // tm
    r = D_MODEL // N_DEV

    def body(dh1_ref, y_ref, ym_ref, gm_ref, wo_ref, dy_ref, dgm_ref, o_ref, o16_ref, acc):
        k = pl.program_id(0)

        @pl.when(k == 0)
        def _():
            dgm_ref[...] = jnp.zeros_like(dgm_ref)
            acc[...] = jnp.zeros_like(acc)

        dh = dh1_ref[...]
        acc[...] += _dot_tn(ym_ref[...], dh)
        dym = _dot_nt(dh, wo_ref[...])
        gm = gm_ref[...]
        _, yh, rr = _group_rms_fwd(y_ref[...], gm)
        outs, dgs = [], []
        for (a, b), rg in zip(_GROUPS, rr):
            dxg, dgg = _rms_bwd(dym[:, a:b], yh[:, a:b], rg, gm[:, a:b])
            outs.append(dxg)
            dgs.append(dgg)
        dy_ref[...] = jnp.concatenate(outs, axis=1)
        dgm_ref[...] += jnp.concatenate(dgs, axis=1)

        @pl.when(k == nk - 1)
        def _():
            for d in range(N_DEV):
                v = acc[d * r:(d + 1) * r, :]
                o_ref[d] = v
                o16_ref[d] = v.astype(o16_ref.dtype)

    tile = pl.BlockSpec((tm, D_MODEL), lambda i: (i, 0))
    slabs = _const_spec((N_DEV, r, D_MODEL))
    (dy, dgm, dw, dw16), _ = _call(
        body, name, (nk,), [tile, tile, tile, _const_spec((1, D_MODEL)), _const_spec((D_MODEL, D_MODEL))],
        [tile, _acc_spec((1, D_MODEL)), slabs, slabs],
        [_sds((t, D_MODEL), F32), _sds((1, D_MODEL), F32), _sds((N_DEV, r, D_MODEL), F32),
         _sds((N_DEV, r, D_MODEL), WIRE)],
        [pltpu.VMEM((D_MODEL, D_MODEL), F32)], [dh1, ycat, ym, gmix, w_out])
    return dy, dgm, (dw, dw16)


def _in_bwd(dz, h0, dh1, g1, w_in_t, after, name):
    t = h0.shape[0]
    tm = _tile(t, STREAM_TILE)

    def body(dz_ref, h_ref, dh1_ref, g_ref, w_ref, after_ref, dh0_ref, dg_ref):
        @pl.when(pl.program_id(0) == 0)
        def _():
            dg_ref[...] = jnp.zeros_like(dg_ref)

        dhn = _dot(dz_ref[...], w_ref[...])
        _, xh, r = _rms_fwd(h_ref[...], g_ref[...])
        dx, dg = _rms_bwd(dhn, xh, r, g_ref[...])
        dg_ref[...] += dg
        dh0_ref[...] = dh1_ref[...] + dx

    tile = lambda w: pl.BlockSpec((tm, w), lambda i: (i, 0))
    (dh0, dg), _ = _call(
        body, name, (t // tm,),
        [tile(IN_W), tile(D_MODEL), tile(D_MODEL), _const_spec((1, D_MODEL)), _const_spec((IN_W, D_MODEL)),
         _const_spec((8, 128))],
        [tile(D_MODEL), _acc_spec((1, D_MODEL))], [_sds((t, D_MODEL), F32), _sds((1, D_MODEL), F32)],
        [], [dz, h0, dh1, g1, w_in_t, after])
    return dh0, dg


def _in_bwd_dw(dz, h0, dh1, g1, w_in_t, name):
    t = h0.shape[0]
    tm = _tile(t)
    nk = t // tm

    def body(dz_ref, h_ref, dh1_ref, g_ref, w_ref, dh0_ref, dg_ref, o_ref, o16_ref, acc):
        k = pl.program_id(0)

        @pl.when(k == 0)
        def _():
            dg_ref[...] = jnp.zeros_like(dg_ref)
            acc[...] = jnp.zeros_like(acc)

        dz_t = dz_ref[...]
        hn, xh, r = _rms_fwd(h_ref[...], g_ref[...])
        acc[...] += _dot_tn(dz_t, hn)
        dhn = _dot(dz_t, w_ref[...])
        dx, dg = _rms_bwd(dhn, xh, r, g_ref[...])
        dg_ref[...] += dg
        dh0_ref[...] = dh1_ref[...] + dx

        @pl.when(k == nk - 1)
        def _():
            for d in range(N_DEV):
                v = acc[d * IN_SHARD:(d + 1) * IN_SHARD, :]
                o_ref[d] = v
                o16_ref[d] = v.astype(o16_ref.dtype)

    tile = lambda w: pl.BlockSpec((tm, w), lambda i: (i, 0))
    slabs = _const_spec((N_DEV, IN_SHARD, D_MODEL))
    (dh0, dg, dw, dw16), _ = _call(
        body, name, (nk,),
        [tile(IN_W), tile(D_MODEL), tile(D_MODEL), _const_spec((1, D_MODEL)), _const_spec((IN_W, D_MODEL))],
        [tile(D_MODEL), _acc_spec((1, D_MODEL)), slabs, slabs],
        [_sds((t, D_MODEL), F32), _sds((1, D_MODEL), F32), _sds((N_DEV, IN_SHARD, D_MODEL), F32),
         _sds((N_DEV, IN_SHARD, D_MODEL), WIRE)],
        [pltpu.VMEM((IN_W, D_MODEL), F32)], [dz, h0, dh1, g1, w_in_t])
    return dh0, dg, (dw, dw16)


def _loss_head(h, gf, target, name):
    t = h.shape[0]
    tm = _tile(t, STREAM_TILE)

    def body(h_ref, g_ref, t_ref, dh_ref, loss_ref, dg_ref):
        @pl.when(pl.program_id(0) == 0)
        def _():
            loss_ref[...] = jnp.zeros_like(loss_ref)
            dg_ref[...] = jnp.zeros_like(dg_ref)

        g = g_ref[...]
        y, xh, r = _rms_fwd(h_ref[...], g)
        err = y - t_ref[...]
        part = 0.5 * jnp.sum(jnp.mean(err * err, axis=-1, keepdims=True), axis=0, keepdims=True)
        loss_ref[...] += jnp.broadcast_to(part, loss_ref.shape)
        dx, dg = _rms_bwd(err * (1.0 / D_MODEL), xh, r, g)
        dg_ref[...] += dg
        dh_ref[...] = dx

    tile = pl.BlockSpec((tm, D_MODEL), lambda i: (i, 0))
    (dh, loss, dg), _ = _call(
        body, name, (t // tm,), [tile, _const_spec((1, D_MODEL)), tile],
        [tile, _acc_spec((1, 128)), _acc_spec((1, D_MODEL))],
        [_sds((t, D_MODEL), F32), _sds((1, 128), F32), _sds((1, D_MODEL), F32)], [], [h, gf, target])
    return dh, loss, dg


def _dw(x, y, name, split, bm, bn):
    t, m = x.shape
    n = y.shape[1]
    tk = _tile(t, DW_TILE)
    nk = t // tk
    if split == "rows":
        assert bn == n
        r, c = m // N_DEV, n
        per = bm // r
        out_block = pl.BlockSpec((per, r, c), lambda a, b, k: (a, 0, 0))
    else:
        assert bm == m
        r, c = m, n // N_DEV
        per = bn // c
        out_block = pl.BlockSpec((per, r, c), lambda a, b, k: (b, 0, 0))

    def body(x_ref, y_ref, o_ref, o16_ref, acc):
        k = pl.program_id(2)

        @pl.when(k == 0)
        def _():
            acc[...] = jnp.zeros_like(acc)

        acc[...] += _dot_tn(x_ref[...], y_ref[...])

        @pl.when(k == nk - 1)
        def _():
            for d in range(per):
                v = acc[d * r:(d + 1) * r, :] if split == "rows" else acc[:, d * c:(d + 1) * c]
                o_ref[d] = v
                o16_ref[d] = v.astype(o16_ref.dtype)

    return pl.pallas_call(
        body, name=name, grid=(m // bm, n // bn, nk),
        in_specs=[pl.BlockSpec((tk, bm), lambda a, b, k: (k, a)), pl.BlockSpec((tk, bn), lambda a, b, k: (k, b))],
        out_specs=[out_block, out_block],
        out_shape=[_sds((N_DEV, r, c), F32), _sds((N_DEV, r, c), WIRE)],
        scratch_shapes=[pltpu.VMEM((bm, bn), F32)],
        compiler_params=pltpu.CompilerParams(dimension_semantics=("arbitrary",) * 3, vmem_limit_bytes=VMEM_LIMIT),
    )(x, y)


def _adamw_math(w, g, m, v):
    m = ADAM_B1 * m + (1.0 - ADAM_B1) * g
    v = ADAM_B2 * v + (1.0 - ADAM_B2) * jnp.square(g)
    m_hat = m / (1.0 - ADAM_B1 ** ADAM_STEP)
    v_hat = v / (1.0 - ADAM_B2 ** ADAM_STEP)
    delta = -ADAM_LR * (m_hat / (jnp.sqrt(v_hat) + ADAM_EPS) + ADAM_WD * w)
    return delta, m, v


def _adamw_shard(g_own, g_recv, dev, w, m, v, after, name):
    _, r, c = w.shape
    br = r
    for cand in (256, 128, 112, 64, 56, 32, 16, 8):
        if r % cand == 0:
            br = cand
            break
    nr = r // br
    own = lambda l: pl.BlockSpec((1, br, c), lambda ll, i, d: (d[0], jnp.where(ll == l, i, (nr - 1) * (1 - l)), 0))
    recv = lambda l: pl.BlockSpec((N_DEV - 1, br, c), lambda ll, i, d: (0, jnp.where(ll == l, i, (nr - 1) * (1 - l)), 0))

    def body(dev_ref, go0, gr0, go1, gr1, w_ref, m_ref, v_ref, after_ref, g_out, d_out, m_out, v_out):
        def update(go_ref, gr_ref):
            g = go_ref[0]
            for j in range(N_DEV - 1):
                g = g + gr_ref[j].astype(F32)
            delta, mn, vn = _adamw_math(w_ref[0], g, m_ref[0], v_ref[0])
            g_out[0] = g
            d_out[0] = delta
            m_out[0] = mn
            v_out[0] = vn

        layer = pl.program_id(0)
        pl.when(layer == 0)(lambda: update(go0, gr0))
        pl.when(layer == 1)(lambda: update(go1, gr1))

    tile = pl.BlockSpec((1, br, c), lambda ll, i, d: (ll, i, 0))
    return pl.pallas_call(
        body, name=name,
        grid_spec=pltpu.PrefetchScalarGridSpec(
            num_scalar_prefetch=1, grid=(2, nr),
            in_specs=[own(0), recv(0), own(1), recv(1), tile, tile, tile,
                      pl.BlockSpec((8, 128), lambda ll, i, d: (0, 0))],
            out_specs=[tile, tile, tile, tile]),
        out_shape=[_sds((2, r, c), F32)] * 4,
        compiler_params=pltpu.CompilerParams(dimension_semantics=("arbitrary",) * 2, vmem_limit_bytes=VMEM_LIMIT),
    )(dev, g_own[0], g_recv[0], g_own[1], g_recv[1], w, m, v, after)


def _adamw_small(gs, ws, ms, vs, name):
    n = len(gs)

    def body(*refs):
        g_refs, w_refs, m_refs, v_refs = (refs[k * n:(k + 1) * n] for k in range(4))
        outs = refs[4 * n:]
        for k in range(n):
            delta, mn, vn = _adamw_math(w_refs[k][...], g_refs[k][...], m_refs[k][...], v_refs[k][...])
            outs[k][...] = delta
            outs[n + k][...] = mn
            outs[2 * n + k][...] = vn

    shapes = [_sds(w.shape, F32) for w in ws]
    res = pl.pallas_call(body, name=name, out_shape=shapes * 3,
                         compiler_params=pltpu.CompilerParams(vmem_limit_bytes=VMEM_LIMIT))(*gs, *ws, *ms, *vs)
    return res[:n], res[n:2 * n], res[2 * n:]


def _sum_parts(own, recv, dev, name):
    def body(dev_ref, own_ref, recv_ref, o_ref):
        me = dev_ref[0]

        def block(d):
            f = jnp.bitwise_xor(me, d)
            return jnp.where(f == 0, own_ref[...], recv_ref[jnp.maximum(f - 1, 0)])

        g = block(0)
        for d in range(1, N_DEV):
            g = g + block(d)
        o_ref[...] = g

    return pl.pallas_call(
        body, name=name,
        grid_spec=pltpu.PrefetchScalarGridSpec(
            num_scalar_prefetch=1, grid=(1,),
            in_specs=[pl.BlockSpec(own.shape, lambda i, d: (0, 0)), pl.BlockSpec(recv.shape, lambda i, d: (0, 0, 0))],
            out_specs=pl.BlockSpec(own.shape, lambda i, d: (0, 0))),
        out_shape=_sds(own.shape, F32))(dev, own, recv)


HBM = pl.BlockSpec(memory_space=pltpu.HBM)
SEM = pl.BlockSpec(memory_space=pltpu.SEMAPHORE)
EFFECT = pltpu.SideEffectType.DATAFLOW_SIDE_EFFECTING


def _direct_copies(srcs, lands, ssem, rsem, scatter):
    x, y, c = _me()
    out = []
    for a in range(len(srcs)):
        for f in range(1, N_DEV):
            px = 1 - x if f & 4 else x
            py = 1 - y if f & 2 else y
            pc = 1 - c if f & 1 else c
            out.append(pltpu.make_async_remote_copy(
                src_ref=srcs[a].at[4 * px + 2 * py + pc] if scatter else srcs[a], dst_ref=lands[a].at[f - 1],
                send_sem=ssem.at[7 * a + f - 1], recv_sem=rsem.at[7 * a + f - 1],
                device_id=(px, py, pc), device_id_type=MESH))
    return out


def _send_start(arrays, scatter, name):
    arrays = list(arrays)
    n = len(arrays)
    lands = [lax.empty((N_DEV - 1,) + (a.shape[1:] if scatter else a.shape), a.dtype) for a in arrays]

    def body(*refs):
        srcs, lnds, ssem, rsem, token = refs[:n], refs[n:2 * n], refs[2 * n], refs[2 * n + 1], refs[-1]
        for cp in _direct_copies(srcs, lnds, ssem, rsem, scatter):
            cp.start()
        token[...] = jnp.zeros_like(token)

    hbm = lambda a: pltpu.HBM(a.shape, a.dtype)
    res = pl.pallas_call(
        body, name=name,
        out_shape=(pltpu.SemaphoreType.DMA((7 * n,)), pltpu.SemaphoreType.DMA((7 * n,)),
                   *[hbm(a) for a in arrays + lands], _sds((8, 128), F32)),
        in_specs=[HBM] * (2 * n),
        out_specs=(SEM, SEM, *[HBM] * (2 * n), pl.BlockSpec(memory_space=pltpu.VMEM)),
        input_output_aliases={i: 2 + i for i in range(2 * n)},
        compiler_params=pltpu.CompilerParams(has_side_effects=EFFECT),
    )(*[pltpu.with_memory_space_constraint(a, pltpu.HBM) for a in arrays + lands])
    return types.SimpleNamespace(ssem=res[0], rsem=res[1], srcs=list(res[2:2 + n]), lands=list(res[2 + n:2 + 2 * n]),
                                 token=res[-1], scatter=scatter)


def _send_wait(h, after, name):
    n = len(h.srcs)

    def body(*refs):
        srcs, lnds, ssem, rsem = refs[:n], refs[n:2 * n], refs[2 * n], refs[2 * n + 1]
        for cp in _direct_copies(srcs, lnds, ssem, rsem, h.scatter):
            cp.wait_send()
            cp.wait_recv()

    hbm = lambda a: pltpu.HBM(a.shape, a.dtype)
    res = pl.pallas_call(
        body, name=name,
        out_shape=tuple(hbm(a) for a in h.srcs + h.lands),
        in_specs=[HBM] * (2 * n) + [SEM, SEM, ANY], out_specs=[HBM] * (2 * n),
        input_output_aliases={i: i for i in range(2 * n)},
        compiler_params=pltpu.CompilerParams(has_side_effects=EFFECT),
    )(*h.srcs, *h.lands, h.ssem, h.rsem, after)
    return list(res[:n]), list(res[n:])


def _block_diag(w):
    out = jnp.zeros((LRU_W, LRU_W), w.dtype)
    for h in range(4):
        out = lax.dynamic_update_slice(out, w[h], (h * 64, h * 64))
    return out


def _layer_params(p, l):
    row = lambda a: a[l].reshape(1, -1)
    sink_rows = jnp.repeat(p["attn_sinks"][l].reshape(4, 2), 2 * BLK, axis=1)
    sink_rows = jnp.concatenate([sink_rows, jnp.zeros((4, 4 * BLK), F32)], axis=0)
    cw = jnp.concatenate([p["conv_dw_w"][l], jnp.zeros((1, CONV_W), F32)], axis=0)
    pv = jnp.concatenate([
        row(p["conv_dw_b"]), row(p["conv_ln_g"]), row(p["conv_ln_b"]), row(p["lru_conv_b"]), row(p["lru_ba"]),
        row(p["lru_bx"]), row(p["lru_lambda"]), jnp.zeros((1, LRU_W), F32), p["lru_conv_w"][l],
        jnp.zeros((4, LRU_W), F32)], axis=0)
    return dict(
        g1=row(p["norm1"]), sink=sink_rows, cw=cw, pv=pv,
        wa=_block_diag(p["lru_wa"][l]).astype(MX), wx=_block_diag(p["lru_wx"][l]).astype(MX),
        gmix=row(p["mix_norm"]), g2=row(p["norm2"]))


_SMALL = ["norm1", "attn_sinks", "conv_dw_w", "conv_dw_b", "conv_ln_g", "conv_ln_b", "lru_conv_w", "lru_conv_b",
          "lru_wa", "lru_ba", "lru_wx", "lru_bx", "lru_lambda", "mix_norm", "norm2"]
_BIG = ["w_in", "w_out", "w_up", "w_down"]
_WEIGHTS = ["norm1", "w_in", "attn_sinks", "conv_dw_w", "conv_dw_b", "conv_ln_g", "conv_ln_b", "lru_conv_w",
            "lru_conv_b", "lru_wa", "lru_ba", "lru_wx", "lru_bx", "lru_lambda", "mix_norm", "w_out", "norm2", "w_up",
            "w_down", "final_norm"]


def kernel(x, norm1, w_in, attn_sinks, conv_dw_w, conv_dw_b, conv_ln_g, conv_ln_b, lru_conv_w, lru_conv_b, lru_wa, lru_ba, lru_wx, lru_bx, lru_lambda, mix_norm, w_out, norm2, w_up, w_down, final_norm, loss_target, m_norm1, m_w_in, m_attn_sinks, m_conv_dw_w, m_conv_dw_b, m_conv_ln_g, m_conv_ln_b, m_lru_conv_w, m_lru_conv_b, m_lru_wa, m_lru_ba, m_lru_wx, m_lru_bx, m_lru_lambda, m_mix_norm, m_w_out, m_norm2, m_w_up, m_w_down, m_final_norm, v_norm1, v_w_in, v_attn_sinks, v_conv_dw_w, v_conv_dw_b, v_conv_ln_g, v_conv_ln_b, v_lru_conv_w, v_lru_conv_b, v_lru_wa, v_lru_ba, v_lru_wx, v_lru_bx, v_lru_lambda, v_mix_norm, v_w_out, v_norm2, v_w_up, v_w_down, v_final_norm):
    w = dict(norm1=norm1, w_in=w_in, attn_sinks=attn_sinks, conv_dw_w=conv_dw_w, conv_dw_b=conv_dw_b,
             conv_ln_g=conv_ln_g, conv_ln_b=conv_ln_b, lru_conv_w=lru_conv_w, lru_conv_b=lru_conv_b, lru_wa=lru_wa,
             lru_ba=lru_ba, lru_wx=lru_wx, lru_bx=lru_bx, lru_lambda=lru_lambda, mix_norm=mix_norm, w_out=w_out,
             norm2=norm2, w_up=w_up, w_down=w_down, final_norm=final_norm)
    m = dict(norm1=m_norm1, w_in=m_w_in, attn_sinks=m_attn_sinks, conv_dw_w=m_conv_dw_w, conv_dw_b=m_conv_dw_b,
             conv_ln_g=m_conv_ln_g, conv_ln_b=m_conv_ln_b, lru_conv_w=m_lru_conv_w, lru_conv_b=m_lru_conv_b,
             lru_wa=m_lru_wa, lru_ba=m_lru_ba, lru_wx=m_lru_wx, lru_bx=m_lru_bx, lru_lambda=m_lru_lambda,
             mix_norm=m_mix_norm, w_out=m_w_out, norm2=m_norm2, w_up=m_w_up, w_down=m_w_down, final_norm=m_final_norm)
    v = dict(norm1=v_norm1, w_in=v_w_in, attn_sinks=v_attn_sinks, conv_dw_w=v_conv_dw_w, conv_dw_b=v_conv_dw_b,
             conv_ln_g=v_conv_ln_g, conv_ln_b=v_conv_ln_b, lru_conv_w=v_lru_conv_w, lru_conv_b=v_lru_conv_b,
             lru_wa=v_lru_wa, lru_ba=v_lru_ba, lru_wx=v_lru_wx, lru_bx=v_lru_bx, lru_lambda=v_lru_lambda,
             mix_norm=v_mix_norm, w_out=v_w_out, norm2=v_norm2, w_up=v_w_up, w_down=v_w_down, final_norm=v_final_norm)
    depth = w_in.shape[0]
    xi, yi, ci = _me()
    dev = (4 * xi + 2 * yi + ci).astype(jnp.int32)
    dev1 = dev.reshape(1)
    tr = lambda a: jnp.swapaxes(a, 1, 2)
    w_t, m_t, v_t = tr(w_in), tr(m_w_in), tr(v_w_in)
    wb = {n: w[n].astype(MX) for n in _BIG if n != "w_in"}
    wb["w_in"] = w_t.astype(MX)
    layer_shards = lambda l: [wb["w_out"][l], wb["w_up"][l], wb["w_down"][l]]

    _, ((g_in0, g_cw, g_lcw),) = _call(None, "gather_first", None, [], [], [], [], [],
                                        [_gather_rider([wb["w_in"][0], conv_dw_w, lru_conv_w])])
    cols = lambda g: jnp.moveaxis(g, 0, -2).reshape(g.shape[1:-1] + (N_DEV * g.shape[-1],))
    p = dict(w)
    p["conv_dw_w"] = cols(g_cw)
    p["lru_conv_w"] = cols(g_lcw)
    lp = [_layer_params(p, l) for l in range(depth)]

    gathered = [dict(w_in=g_in0.reshape(IN_W, D_MODEL)), dict()]
    saved = []
    h = x[0]
    for l in range(depth):
        q, gw = lp[l], gathered[l]
        z, hn1 = _ln_in(h, q["g1"], gw["w_in"], l == 0, f"ln_in{l}")
        riders = [_gather_rider(layer_shards(0))] if l == 0 else []
        (ycat, hl, uc, probs, psinks), got = _mixer_fwd(z, q["sink"], q["cw"], q["pv"], q["wa"], q["wx"],
                                                        f"mixer_fwd{l}", riders)
        if l == 0:
            gw["w_out"], gw["w_up"], gw["w_down"] = got[0]
            gw["w_out"] = gw["w_out"].reshape(D_MODEL, D_MODEL)
        riders = [_gather_rider([wb["w_in"][1]] + layer_shards(1))] if l == 0 else []
        (h1, act, h2, ym, hn2), got = _post_fwd(ycat, h, q["gmix"], gw["w_out"], q["g2"], gw["w_up"],
                                                gw["w_down"].reshape(D_FF, D_MODEL), f"post_fwd{l}", riders)
        if l == 0:
            nxt = gathered[1]
            nxt["w_in"], nxt["w_out"], nxt["w_up"], nxt["w_down"] = got[0]
            nxt["w_in"] = nxt["w_in"].reshape(IN_W, D_MODEL)
            nxt["w_out"] = nxt["w_out"].reshape(D_MODEL, D_MODEL)
        saved.append(dict(h0=h, z=z, hn1=hn1, ycat=ycat, hl=hl, uc=uc, probs=probs, psinks=psinks, h1=h1, act=act,
                          ym=ym, hn2=hn2))
        h = h2
    dh, loss, dgf = _loss_head(h, final_norm.reshape(1, -1), loss_target[0], "loss_head")

    grads = [None] * depth
    big = {n: [None] * depth for n in _BIG}
    pending = []

    def send_pending():
        riders = [_scatter_rider([item[3] for item in pending])] if pending else []
        return riders, list(pending)

    def record(sent, got):
        for item, recv in zip(sent, got[0] if sent else []):
            big[item[0]][item[1]] = (item[2], recv)
        del pending[:len(sent)]

    for l in reversed(range(depth)):
        q, s, gw = lp[l], saved[l], gathered[l]
        riders, sent = send_pending()
        w_up_t = jnp.swapaxes(gw["w_up"], 1, 2).reshape(D_FF, D_MODEL)
        (dh1, dh1b, dhb, du, dg2), got = _ffn_bwd(dh, s["act"], s["h1"], q["g2"], w_up_t, gw["w_down"],
                                                  f"ffn_bwd{l}", riders)
        record(sent, got)
        dycat, dgm, d_wout = _mix_bwd(dh1b, s["ycat"], s["ym"], q["gmix"], gw["w_out"], f"mix_bwd{l}")
        pending.append(("w_down", l) + tuple(_dw(s["act"], dhb, f"dw_down{l}", "rows", 2048, D_MODEL)))
        pending.append(("w_up", l) + tuple(_dw(s["hn2"], du, f"dw_up{l}", "cols", D_MODEL, 2048)))
        pending.append(("w_out", l) + tuple(d_wout))
        riders, sent = send_pending()
        (dz, dsink, dcw, dpv, dwa, dwx), got = _mixer_bwd(
            dycat, s["z"], s["ycat"], s["hl"], s["uc"], s["probs"], s["psinks"], q["sink"], q["cw"], q["pv"], q["wa"],
            q["wx"], f"mixer_bwd{l}", riders)
        record(sent, got)
        if l > 0:
            dh, dg1, d_win = _in_bwd_dw(dz, s["h0"], dh1, q["g1"], gw["w_in"], f"in_bwd{l}")
            pending.append(("w_in", l) + tuple(d_win))
        else:
            d_win = _dw(dz, s["hn1"], f"dw_in{l}", "rows", IN_W, D_MODEL)
            win_sends = _send_start([d_win[1]], True, "scatter_w_in0_start")
            dh, dg1 = _in_bwd(dz, s["h0"], dh1, q["g1"], gw["w_in"], win_sends.token, f"in_bwd{l}")
        grads[l] = dict(dg1=dg1, dsink=dsink, dcw=dcw, dpv=dpv, dwa=dwa, dwx=dwx, dgm=dgm, dg2=dg2)

    acc = {k: jnp.stack([grads[l][k] for l in range(depth)]) for k in grads[0]}
    dpv = acc["dpv"]
    unblock = lambda a: jnp.concatenate([a[:, h * 64:(h + 1) * 64, h * 64:(h + 1) * 64] for h in range(4)], axis=1)
    by_name = dict(
        norm1=acc["dg1"][:, 0], attn_sinks=jnp.stack([acc["dsink"][:, 0:4, 0], acc["dsink"][:, 0:4, 2 * BLK]],
                                                     axis=2).reshape(depth, 8),
        conv_dw_w=acc["dcw"][:, 0:CONV_K], conv_dw_b=dpv[:, R_CONV_B], conv_ln_g=dpv[:, R_LN_G],
        conv_ln_b=dpv[:, R_LN_B], lru_conv_w=dpv[:, R_LCW:R_LCW + LRU_K], lru_conv_b=dpv[:, R_LCONV_B],
        lru_wa=unblock(acc["dwa"]), lru_ba=dpv[:, R_BA].reshape(depth, 4, 64), lru_wx=unblock(acc["dwx"]),
        lru_bx=dpv[:, R_BX].reshape(depth, 4, 64), lru_lambda=dpv[:, R_LAM], mix_norm=acc["dgm"][:, 0],
        norm2=acc["dg2"][:, 0])
    small = [by_name[n] for n in _SMALL] + [dgf, loss[:, 0:1]]

    def as_rows(a):
        flat = a.reshape(-1)
        pad = (-flat.size) % 1024
        if pad:
            flat = jnp.concatenate([flat, jnp.zeros((pad,), F32)])
        return flat.reshape(-1, 128)

    pieces = [as_rows(a) for a in small]
    packed = jnp.concatenate(pieces, axis=0)
    small_sends = _send_start([packed], False, "bcast_small_start")

    out = {}
    shard_update = lambda n, wmv, after: list(_adamw_shard(
        [big[n][l][0] for l in range(depth)], [big[n][l][1] for l in range(depth)], dev1, *wmv, after, f"adamw_{n}"))
    for n in ("w_out", "w_up", "w_down"):
        out[n] = shard_update(n, (w[n], m[n], v[n]), small_sends.token)
    _, (win_recv,) = _send_wait(win_sends, out["w_down"][1], "scatter_w_in0_wait")
    big["w_in"][0] = (d_win[0], win_recv)
    out["w_in"] = [tr(a) for a in shard_update("w_in", (w_t, m_t, v_t), jnp.zeros((8, 128), F32))]
    (packed,), (small_recv,) = _send_wait(small_sends, out["w_in"][1], "bcast_small_wait")
    summed = _sum_parts(packed, small_recv, dev1, "sum_small_grads")
    small_sums, row = [], 0
    for a, piece in zip(small, pieces):
        got = summed[row:row + piece.shape[0]]
        small_sums.append(got.reshape(a.shape) if a.size == piece.size else got.reshape(-1)[:a.size].reshape(a.shape))
        row += piece.shape[0]
    shard = lambda a: lax.dynamic_slice_in_dim(a, dev * (a.shape[-1] // N_DEV), a.shape[-1] // N_DEV, axis=a.ndim - 1)
    flat = {"lru_wa": (depth, LRU_W, 64), "lru_wx": (depth, LRU_W, 64), "final_norm": (1, D_MODEL)}
    gs, ws, ms, vs = [], [], [], []
    for n, g in zip(_SMALL + ["final_norm"], small_sums[:-1]):
        shp = flat.get(n, w[n].shape)
        gs.append((shard(g) if n in ("conv_dw_w", "lru_conv_w") else g).reshape(shp))
        ws.append(w[n].reshape(shp))
        ms.append(m[n].reshape(shp))
        vs.append(v[n].reshape(shp))
    sd, sm, sv = _adamw_small(gs, ws, ms, vs, "adamw_small")
    for j, n in enumerate(_SMALL + ["final_norm"]):
        out[n] = [a.reshape(w[n].shape) for a in (gs[j], sd[j], sm[j], sv[j])]
    loss_total = small_sums[-1][0, 0]

    result = [loss_total, dh[None]]
    for j in range(4):
        result += [out[n][j] for n in _WEIGHTS]
    return tuple(result)
```

```python
import types

import jax
import jax.numpy as jnp
from jax import lax
from jax.experimental import pallas as pl
from jax.experimental.pallas import tpu as pltpu

F32 = jnp.float32
MX = jnp.bfloat16
WIRE = jnp.bfloat16

D_MODEL = 1024
HEAD_DIM = 64
ATTN_W = 512
KV_W = 128
BLK = 128
CONV_W = 256
CONV_K = 31
LRU_W = 256
LRU_K = 4
LRU_C = 8.0
IN_W = 1792
D_FF = 4096
FF_BLK = 512
N_DEV = 8
IN_SHARD = IN_W // N_DEV
RMS_EPS = 1e-6
LN_EPS = 1e-5
MASK_VALUE = -1e30
SCALE = HEAD_DIM ** -0.5
CONV_HALO = 32
LRU_HALO = 8
CONV_CHUNK = 64
POST_TILE = 512
STREAM_TILE = 1024
DW_TILE = 1024
Q0, K0, V0, CV0, CG0, RX0, RG0 = 0, 512, 640, 768, 1024, 1280, 1536
R_CONV_B, R_LN_G, R_LN_B, R_LCONV_B, R_BA, R_BX, R_LAM, R_LCW = 0, 1, 2, 3, 4, 5, 6, 8

ADAM_LR, ADAM_B1, ADAM_B2, ADAM_EPS, ADAM_WD, ADAM_STEP = 0.001, 0.9, 0.999, 1e-08, 0.01, 10

VMEM_LIMIT = 56 * 1024 * 1024
MESH = pl.DeviceIdType.MESH
ANY = pl.BlockSpec(memory_space=pl.ANY)


def _tile(t, cap=512):
    return min(cap, t)


def _dot(a, b):
    return jnp.dot(a.astype(MX), b.astype(MX), preferred_element_type=F32)


def _dot_nt(a, b):
    return lax.dot_general(a.astype(MX), b.astype(MX), (((1,), (1,)), ((), ())), preferred_element_type=F32)


def _dot_tn(a, b):
    return lax.dot_general(a.astype(MX), b.astype(MX), (((0,), (0,)), ((), ())), preferred_element_type=F32)


def _const_spec(shape):
    nd = len(shape)
    return pl.BlockSpec(shape, lambda *_: (0,) * nd, pipeline_mode=pl.Buffered(1))


def _acc_spec(shape):
    nd = len(shape)
    return pl.BlockSpec(shape, lambda *_: (0,) * nd)


def _sds(shape, dtype):
    return jax.ShapeDtypeStruct(shape, dtype)


def _sigmoid(x):
    return jax.nn.sigmoid(x)


def _rms_fwd(x, g):
    r = lax.rsqrt(jnp.mean(x * x, axis=-1, keepdims=True) + RMS_EPS)
    xh = x * r
    return xh * g, xh, r


def _rms_bwd(dy, xh, r, g):
    t = dy * g
    dx = r * (t - xh * jnp.mean(t * xh, axis=-1, keepdims=True))
    return dx, jnp.sum(dy * xh, axis=0, keepdims=True)


_GROUPS = ((0, 512), (512, 768), (768, 1024))


def _group_rms_fwd(y, g):
    parts = [_rms_fwd(y[:, a:b], g[:, a:b]) for a, b in _GROUPS]
    return (jnp.concatenate([p[0] for p in parts], axis=1),
            jnp.concatenate([p[1] for p in parts], axis=1),
            [p[2] for p in parts])


def _gelu(x):
    c = 0.7978845608028654
    u = c * (x + 0.044715 * x * x * x)
    th = jnp.tanh(u)
    val = 0.5 * x * (1.0 + th)
    grad = 0.5 * (1.0 + th) + 0.5 * x * (1.0 - th * th) * c * (1.0 + 3.0 * 0.044715 * x * x)
    return val, grad


def _neg_expm1(x):
    series = -x * (1.0 + x * (0.5 + x * (1.0 / 6.0 + x * (1.0 / 24.0))))
    return jnp.where(x > -0.02, series, 1.0 - jnp.exp(x))


def _me():
    return lax.axis_index("x"), lax.axis_index("y"), lax.axis_index("c")


def _gather_rider(arrays):
    arrays = list(arrays)
    n = len(arrays)

    def plan(ins, outs, sems):
        ssem, rsem, lsem = sems
        x, y, c = _me()
        chips = [(1 - x, y), (x, 1 - y), (1 - x, 1 - y)]

        def copy(a, k, block, to, own=False):
            dst = outs[a].at[4 * block[0] + 2 * block[1] + block[2]]
            return pltpu.make_async_remote_copy(
                src_ref=ins[a] if own else dst, dst_ref=dst, send_sem=ssem.at[7 * a + k],
                recv_sem=rsem.at[7 * a + k], device_id=to, device_id_type=MESH)

        return x, y, c, chips, copy, lsem

    def start(ins, outs, sems):
        x, y, c, chips, copy, lsem = plan(ins, outs, sems)
        for a in range(n):
            pltpu.make_async_copy(ins[a], outs[a].at[4 * x + 2 * y + c], lsem.at[a]).start()
            copy(a, 0, (x, y, c), (x, y, 1 - c), own=True).start()
            for j, chip in enumerate(chips):
                copy(a, 1 + j, (x, y, c), (*chip, c), own=True).start()

    def mid(ins, outs, sems):
        x, y, c, chips, copy, _ = plan(ins, outs, sems)
        for a in range(n):
            for j, chip in enumerate(chips):
                copy(a, 1 + j, (*chip, c), (x, y, c)).wait_recv()
                copy(a, 4 + j, (*chip, c), (x, y, 1 - c)).start()

    def finish(ins, outs, sems):
        x, y, c, chips, copy, lsem = plan(ins, outs, sems)
        for a in range(n):
            copy(a, 0, (x, y, 1 - c), (x, y, c)).wait_recv()
            for j, chip in enumerate(chips):
                copy(a, 4 + j, (*chip, 1 - c), (x, y, c)).wait_recv()
        for a in range(n):
            copy(a, 0, (x, y, c), (x, y, 1 - c), own=True).wait_send()
            for j, chip in enumerate(chips):
                copy(a, 1 + j, (x, y, c), (*chip, c), own=True).wait_send()
                copy(a, 4 + j, (*chip, c), (x, y, 1 - c)).wait_send()
            pltpu.make_async_copy(ins[a], outs[a].at[4 * x + 2 * y + c], lsem.at[a]).wait()

    return types.SimpleNamespace(
        arrays=arrays, out_shape=[_sds((N_DEV,) + a.shape, a.dtype) for a in arrays],
        scratch=[pltpu.SemaphoreType.DMA((7 * n,)), pltpu.SemaphoreType.DMA((7 * n,)), pltpu.SemaphoreType.DMA((n,))],
        start=start, mid=mid, finish=finish)


def _scatter_rider(arrays):
    arrays = list(arrays)
    n = len(arrays)

    def copies(ins, outs, sems):
        ssem, rsem = sems
        x, y, c = _me()
        out = []
        for a in range(n):
            for f in range(1, N_DEV):
                px = 1 - x if f & 4 else x
                py = 1 - y if f & 2 else y
                pc = 1 - c if f & 1 else c
                out.append(pltpu.make_async_remote_copy(
                    src_ref=ins[a].at[4 * px + 2 * py + pc], dst_ref=outs[a].at[f - 1], send_sem=ssem.at[7 * a + f - 1],
                    recv_sem=rsem.at[7 * a + f - 1], device_id=(px, py, pc), device_id_type=MESH))
        return out

    def start(ins, outs, sems):
        for cp in copies(ins, outs, sems):
            cp.start()

    def finish(ins, outs, sems):
        for cp in copies(ins, outs, sems):
            cp.wait()

    return types.SimpleNamespace(
        arrays=arrays, out_shape=[_sds((N_DEV - 1,) + a.shape[1:], a.dtype) for a in arrays],
        scratch=[pltpu.SemaphoreType.DMA((7 * n,)), pltpu.SemaphoreType.DMA((7 * n,))],
        start=start, mid=None, finish=finish)


def _call(body, name, grid, in_specs, out_specs, out_shape, scratch, operands, riders=()):
    n_in, n_out, n_scr = len(operands), len(out_shape), len(scratch)
    nsteps = grid[0] if grid else 1
    sizes = [(len(r.arrays), len(r.out_shape), len(r.scratch)) for r in riders]

    def wrapped(*refs):
        pos = n_in
        r_ins = []
        for ri, _, _ in sizes:
            r_ins.append(refs[pos:pos + ri])
            pos += ri
        outs = refs[pos:pos + n_out]
        pos += n_out
        r_outs = []
        for _, ro, _ in sizes:
            r_outs.append(refs[pos:pos + ro])
            pos += ro
        scr = refs[pos:pos + n_scr]
        pos += n_scr
        r_sems = []
        for _, _, rs in sizes:
            r_sems.append(refs[pos:pos + rs])
            pos += rs
        step = pl.program_id(0) if grid else 0

        def at(s, fn):
            if grid:
                pl.when(step == s)(fn)
            else:
                fn()

        for r, a, b, c in zip(riders, r_ins, r_outs, r_sems):
            at(0, lambda r=r, a=a, b=b, c=c: r.start(a, b, c))
        for r, a, b, c in zip(riders, r_ins, r_outs, r_sems):
            if r.mid is not None:
                at((3 * nsteps) // 4, lambda r=r, a=a, b=b, c=c: r.mid(a, b, c))
        if body is not None:
            body(*refs[:n_in], *outs, *scr)
        for r, a, b, c in zip(riders, r_ins, r_outs, r_sems):
            at(nsteps - 1, lambda r=r, a=a, b=b, c=c: r.finish(a, b, c))

    r_arrays = [a for r in riders for a in r.arrays]
    r_shapes = [s for r in riders for s in r.out_shape]
    kwargs = {}
    if grid:
        kwargs = dict(grid=grid, compiler_params=pltpu.CompilerParams(
            dimension_semantics=("arbitrary",) * len(grid), vmem_limit_bytes=VMEM_LIMIT))
    res = pl.pallas_call(
        wrapped, name=name,
        in_specs=list(in_specs) + [ANY] * len(r_arrays),
        out_specs=list(out_specs) + [ANY] * len(r_shapes),
        out_shape=list(out_shape) + r_shapes,
        scratch_shapes=list(scratch) + [s for r in riders for s in r.scratch],
        **kwargs,
    )(*operands, *r_arrays)
    host, rest = res[:n_out], res[n_out:]
    r_res = []
    for _, ro, _ in sizes:
        r_res.append(rest[:ro])
        rest = rest[ro:]
    return host, r_res


def _ln_in(h, g1, w_in_t, keep_hn, name):
    t = h.shape[0]
    tm = _tile(t, STREAM_TILE)

    def body(h_ref, g_ref, w_ref, z_ref, *hn_ref):
        y, _, _ = _rms_fwd(h_ref[...], g_ref[...])
        hn = y.astype(MX)
        if keep_hn:
            hn_ref[0][...] = hn
        z_ref[...] = _dot_nt(hn, w_ref[...])

    tile = lambda w: pl.BlockSpec((tm, w), lambda i: (i, 0))
    outs, _ = _call(
        body, name, (t // tm,),
        [tile(D_MODEL), _const_spec((1, D_MODEL)), _const_spec((IN_W, D_MODEL))],
        [tile(IN_W)] + [tile(D_MODEL)] * keep_hn,
        [_sds((t, IN_W), F32)] + [_sds((t, D_MODEL), MX)] * keep_hn, [], [h, g1, w_in_t])
    return outs[0], (outs[1] if keep_hn else None)


def _band2(kb, g):
    lo = lax.broadcasted_iota(jnp.int32, kb.shape, 1) < HEAD_DIM
    kr = pltpu.roll(kb, HEAD_DIM, 1)
    if g == 0:
        top, bot = jnp.where(lo, kb, 0.0), jnp.where(lo, 0.0, kr)
    else:
        top, bot = jnp.where(lo, kr, 0.0), jnp.where(lo, 0.0, kb)
    return jnp.concatenate([top, bot], axis=0)


def _attn_operands(z_ref, zh_ref, b):
    rows = slice(b * BLK, (b + 1) * BLK)
    prev = zh_ref if b == 0 else z_ref
    prow = slice(0, BLK) if b == 0 else slice((b - 1) * BLK, b * BLK)
    kb = jnp.concatenate([prev[prow, K0:K0 + KV_W], z_ref[rows, K0:K0 + KV_W]], axis=0)
    vb = jnp.concatenate([prev[prow, V0:V0 + KV_W], z_ref[rows, V0:V0 + KV_W]], axis=0)
    k2 = [_band2(kb, g) for g in range(2)]
    v2 = [_band2(vb, g) for g in range(2)]
    q2 = [jnp.concatenate([z_ref[rows, (2 * g) * BLK:(2 * g + 1) * BLK], z_ref[rows, (2 * g + 1) * BLK:(2 * g + 2) * BLK]],
                          axis=0) for g in range(2)]
    return q2, k2, v2


def _attn_block(z_ref, zh_ref, sink_ref, b, first):
    q2, k2, v2 = _attn_operands(z_ref, zh_ref, b)
    rr = lax.broadcasted_iota(jnp.int32, (4 * BLK, 2 * BLK), 0) & (BLK - 1)
    cc = lax.broadcasted_iota(jnp.int32, (4 * BLK, 2 * BLK), 1)
    first_block = jnp.logical_and(first, b == 0).astype(jnp.int32)
    mask = jnp.logical_and(jnp.logical_and(cc > rr, cc <= rr + BLK), cc >= BLK * first_block)
    s = jnp.concatenate([_dot_nt(q2[g], k2[g]) for g in range(2)], axis=0) * SCALE
    w = 2 * BLK
    out, psink = [], []
    for hh in range(2):
        sh = jnp.where(mask, s[:, hh * w:(hh + 1) * w], MASK_VALUE)
        sk = jnp.concatenate([jnp.broadcast_to(sink_ref[p:p + 1, hh * w:hh * w + 1], (BLK, 1)) for p in range(4)], axis=0)
        m = jnp.maximum(jnp.max(sh, axis=1, keepdims=True), sk)
        p = jnp.exp(sh - m)
        es = jnp.exp(sk - m)
        inv = 1.0 / (jnp.sum(p, axis=1, keepdims=True) + es)
        out.append(p * inv)
        psink.append(es * inv)
    return v2, jnp.concatenate(out, axis=1), psink


def _scan_steps(a, b, n, span, reverse):
    pos = lax.broadcasted_iota(jnp.int32, a.shape, 0) & (span - 1)
    d = 1
    while d < span:
        keep = pos < span - d if reverse else pos >= d
        shift = n - d if reverse else d
        a_sh = jnp.where(keep, pltpu.roll(a, shift, 0), 1.0)
        b_sh = jnp.where(keep, pltpu.roll(b, shift, 0), 0.0)
        b = a * b_sh + b
        a = a * a_sh
        d *= 2
    return a, b


def _scan(a, b, tm, reverse):
    return _scan_steps(a, b, tm, tm, reverse)


def _shifted_copies(ext, shifts, tm):
    rows = tm + CONV_HALO - 8
    for r in range(1, 8):
        shifts[r - 1, 0:rows, :] = ext[pl.ds(r, rows), :]


def _tap(ext, shifts, off, r0, n):
    a, r = divmod(off, 8)
    lo = 8 * a + r0
    if r == 0:
        return ext[lo:lo + n, :]
    return shifts[r - 1, lo:lo + n, :]


def _glu_fill(z_ref, zh_ref, uext, ush, first, tm, sg_out=None):
    cv = z_ref[:, CV0:CV0 + CONV_W]
    sg = _sigmoid(z_ref[:, CG0:CG0 + CONV_W])
    if sg_out is not None:
        sg_out[...] = sg
    hrow = BLK - CONV_HALO
    uh = zh_ref[hrow:BLK, CV0:CV0 + CONV_W] * _sigmoid(zh_ref[hrow:BLK, CG0:CG0 + CONV_W])
    uext[0:CONV_HALO, :] = jnp.where(first, 0.0, uh)
    uext[CONV_HALO:CONV_HALO + tm, :] = cv * sg
    _shifted_copies(uext, ush, tm)


def _conv_taps(cw_ref, pv_ref, uext, ush, out_ref, tm):
    for r0 in range(0, tm, CONV_CHUNK):
        acc = jnp.broadcast_to(pv_ref[R_CONV_B:R_CONV_B + 1, :], (CONV_CHUNK, CONV_W))
        for k in range(CONV_K):
            acc = acc + cw_ref[k:k + 1, :] * _tap(uext, ush, CONV_HALO - (CONV_K - 1) + k, r0, CONV_CHUNK)
        out_ref[r0:r0 + CONV_CHUNK, :] = acc


def _ln_silu(uc, pv_ref):
    mu = jnp.mean(uc, axis=-1, keepdims=True)
    xc = uc - mu
    rs = lax.rsqrt(jnp.mean(xc * xc, axis=-1, keepdims=True) + LN_EPS)
    xh = xc * rs
    ln = xh * pv_ref[R_LN_G:R_LN_G + 1, :] + pv_ref[R_LN_B:R_LN_B + 1, :]
    sg = _sigmoid(ln)
    return xh, rs, ln, sg


def _lru_gates(z_ref, zh_ref, pv_ref, wa_ref, wx_ref, rxext, first, tm):
    rxext[0:LRU_HALO, :] = jnp.where(first, 0.0, zh_ref[BLK - LRU_HALO:BLK, RX0:RX0 + LRU_W])
    rxext[LRU_HALO:LRU_HALO + tm, :] = z_ref[:, RX0:RX0 + LRU_W]
    xc = jnp.broadcast_to(pv_ref[R_LCONV_B:R_LCONV_B + 1, :], (tm, LRU_W))
    for k in range(LRU_K):
        xc = xc + pv_ref[R_LCW + k:R_LCW + k + 1, :] * rxext[pl.ds(LRU_HALO - (LRU_K - 1) + k, tm), :]
    r = _sigmoid(_dot(xc, wa_ref[...]) + pv_ref[R_BA:R_BA + 1, :])
    ig = _sigmoid(_dot(xc, wx_ref[...]) + pv_ref[R_BX:R_BX + 1, :])
    lam = pv_ref[R_LAM:R_LAM + 1, :]
    sp = jnp.log1p(jnp.exp(-lam))
    la = (-LRU_C * r) * sp
    a = jnp.exp(la)
    mult = jnp.sqrt(_neg_expm1(2.0 * la))
    return xc, r, ig, sp, la, a, mult


def _mixer_in_specs(tm, tile_of):
    hb = tm // BLK
    return [
        pl.BlockSpec((tm, IN_W), lambda i: (tile_of(i), 0)),
        pl.BlockSpec((BLK, IN_W), lambda i: (jnp.maximum(tile_of(i) * hb - 1, 0), 0)),
        _const_spec((8, 4 * BLK)),
        _const_spec((32, CONV_W)),
        _const_spec((16, CONV_W)),
        _const_spec((LRU_W, LRU_W)),
        _const_spec((LRU_W, LRU_W)),
    ]


def _mixer_fwd(z, sink, cw, pv, wa, wx, name, riders=()):
    t = z.shape[0]
    tm = _tile(t)
    nb = tm // BLK

    def body(z_ref, zh_ref, sink_ref, cw_ref, pv_ref, wa_ref, wx_ref, y_ref, hl_ref, uc_ref, p_ref, ps_ref,
             uext, ush, rxext, hcar):
        i = pl.program_id(0)
        first = i == 0

        @pl.when(first)
        def _():
            hcar[...] = jnp.zeros_like(hcar)

        lo = lax.broadcasted_iota(jnp.int32, (4 * BLK, BLK), 1) < HEAD_DIM
        for b in range(nb):
            rows = slice(b * BLK, (b + 1) * BLK)
            v2, prob, psink = _attn_block(z_ref, zh_ref, sink_ref, b, first)
            prob = prob.astype(MX)
            p_ref[b] = prob
            ps_ref[b] = jnp.where(lo, psink[0], psink[1])
            for g in range(2):
                o = _dot(prob[2 * g * BLK:(2 * g + 2) * BLK], v2[g])
                y_ref[rows, (2 * g) * BLK:(2 * g + 1) * BLK] = o[0:BLK]
                y_ref[rows, (2 * g + 1) * BLK:(2 * g + 2) * BLK] = o[BLK:2 * BLK]
        _glu_fill(z_ref, zh_ref, uext, ush, first, tm)
        _conv_taps(cw_ref, pv_ref, uext, ush, uc_ref, tm)
        _, _, ln, sg = _ln_silu(uc_ref[...], pv_ref)
        y_ref[:, ATTN_W:ATTN_W + CONV_W] = ln * sg
        xc, _, ig, _, _, a, mult = _lru_gates(z_ref, zh_ref, pv_ref, wa_ref, wx_ref, rxext, first, tm)
        acum, h = _scan(a, mult * (ig * xc), tm, reverse=False)
        h = h + acum * hcar[0:1, :]
        hl_ref[...] = h
        hcar[0:1, :] = h[tm - 1:tm, :]
        gl, _ = _gelu(z_ref[:, RG0:RG0 + LRU_W])
        y_ref[:, ATTN_W + CONV_W:ATTN_W + CONV_W + LRU_W] = h * gl

    tile = lambda w: pl.BlockSpec((tm, w), lambda i: (i, 0))
    return _call(
        body, name, (t // tm,), _mixer_in_specs(tm, lambda i: i),
        [tile(D_MODEL), tile(LRU_W), tile(CONV_W), pl.BlockSpec((nb, 4 * BLK, 4 * BLK), lambda i: (i, 0, 0)),
         pl.BlockSpec((nb, 4 * BLK, BLK), lambda i: (i, 0, 0))],
        [_sds((t, D_MODEL), F32), _sds((t, LRU_W), F32), _sds((t, CONV_W), F32),
         _sds((t // BLK, 4 * BLK, 4 * BLK), MX), _sds((t // BLK, 4 * BLK, BLK), F32)],
        [pltpu.VMEM((tm + CONV_HALO, CONV_W), F32), pltpu.VMEM((7, tm + CONV_HALO - 8, CONV_W), F32),
         pltpu.VMEM((tm + LRU_HALO, LRU_W), F32), pltpu.VMEM((8, LRU_W), F32)],
        [z, z, sink, cw, pv, wa, wx], riders)


def _mixer_bwd(dy, z, ycat, hl, uc, probs, psinks, sink, cw, pv, wa, wx, name, riders=()):
    t = z.shape[0]
    tm = _tile(t)
    nt = t // tm
    nb = tm // BLK
    rev = lambda i: nt - 1 - i

    def body(dy_ref, z_ref, zh_ref, sink_ref, cw_ref, pv_ref, wa_ref, wx_ref, y_ref, hl_ref, hlh_ref, uc_ref,
             p_ref, ps_ref, dz_ref, dsink_ref, dcw_ref, dpv_ref, dwa_ref, dwx_ref,
             uext, ush, sgs, rxext, dkext, dvext, ducext, dsh, dcw8, dxcext, kcar, vcar, uccar, xccar, gcar):
        i = pl.program_id(0)
        first = i == nt - 1

        @pl.when(i == 0)
        def _():
            for car in (kcar, vcar, uccar, xccar, gcar, dcw8):
                car[...] = jnp.zeros_like(car)
            for acc in (dsink_ref, dpv_ref, dwa_ref, dwx_ref):
                acc[...] = jnp.zeros_like(acc)

        def addrow(r, val):
            dpv_ref[r:r + 1, :] += jnp.sum(val, axis=0, keepdims=True)

        dkext[:, 0:tm] = jnp.zeros((KV_W, tm), F32)
        dvext[:, 0:tm] = jnp.zeros((KV_W, tm), F32)
        dkext[:, tm:tm + BLK] = kcar[...]
        dvext[:, tm:tm + BLK] = vcar[...]
        lane512 = lax.broadcasted_iota(jnp.int32, (1, 4 * BLK), 1) < 2 * BLK
        lo = lax.broadcasted_iota(jnp.int32, (4 * BLK, BLK), 1) < HEAD_DIM
        hd, w2 = HEAD_DIM, 2 * BLK
        for b in range(nb):
            rows = slice(b * BLK, (b + 1) * BLK)
            band = slice(b * BLK, (b + 2) * BLK)
            q2, k2, v2 = _attn_operands(z_ref, zh_ref, b)
            prob = p_ref[b]
            psink = [ps_ref[b, :, 0:1], ps_ref[b, :, HEAD_DIM:HEAD_DIM + 1]]
            stack = lambda ref: jnp.concatenate([ref[rows, p * BLK:(p + 1) * BLK] for p in range(4)], axis=0)
            do4 = stack(dy_ref)
            dlt = do4 * stack(y_ref)
            d0 = jnp.sum(jnp.where(lo, dlt, 0.0), axis=1, keepdims=True)
            d1 = jnp.sum(jnp.where(lo, 0.0, dlt), axis=1, keepdims=True)
            dp = jnp.concatenate([_dot_nt(do4[g * w2:(g + 1) * w2], v2[g]) for g in range(2)], axis=0)
            dl = jnp.concatenate([jnp.broadcast_to(d0, (4 * BLK, w2)), jnp.broadcast_to(d1, (4 * BLK, w2))], axis=1)
            draw = (prob * (dp - dl)) * SCALE
            e0, e1 = psink[0] * d0, psink[1] * d1
            for p in range(4):
                prs = slice(p * BLK, (p + 1) * BLK)
                s0 = jnp.sum(e0[prs], axis=0, keepdims=True)
                s1 = jnp.sum(e1[prs], axis=0, keepdims=True)
                dsink_ref[p:p + 1, :] += -jnp.where(lane512, s0, s1)
            for g in range(2):
                grs = slice(g * w2, (g + 1) * w2)
                dq = _dot(draw[grs], k2[g])
                dz_ref[rows, (2 * g) * BLK:(2 * g + 1) * BLK] = dq[0:BLK].astype(dz_ref.dtype)
                dz_ref[rows, (2 * g + 1) * BLK:(2 * g + 2) * BLK] = dq[BLK:2 * BLK].astype(dz_ref.dtype)
                tk = _dot_tn(q2[g], draw[grs])
                tv = _dot_tn(do4[grs], prob[grs])
                dkext[g * hd:(g + 1) * hd, band] += tk[0:hd, 0:w2] + tk[hd:2 * hd, w2:2 * w2]
                dvext[g * hd:(g + 1) * hd, band] += tv[0:hd, 0:w2] + tv[hd:2 * hd, w2:2 * w2]
        dz_ref[:, K0:K0 + KV_W] = jnp.transpose(dkext[:, BLK:BLK + tm]).astype(dz_ref.dtype)
        dz_ref[:, V0:V0 + KV_W] = jnp.transpose(dvext[:, BLK:BLK + tm]).astype(dz_ref.dtype)
        kcar[...] = dkext[:, 0:BLK]
        vcar[...] = dvext[:, 0:BLK]

        _glu_fill(z_ref, zh_ref, uext, ush, first, tm, sg_out=sgs)
        xh, rs, ln, sg = _ln_silu(uc_ref[...], pv_ref)
        dln = dy_ref[:, ATTN_W:ATTN_W + CONV_W] * (sg * (1.0 + ln * (1.0 - sg)))
        addrow(R_LN_G, dln * xh)
        addrow(R_LN_B, dln)
        dxh = dln * pv_ref[R_LN_G:R_LN_G + 1, :]
        duc = rs * (dxh - jnp.mean(dxh, axis=-1, keepdims=True) - xh * jnp.mean(dxh * xh, axis=-1, keepdims=True))
        addrow(R_CONV_B, duc)
        ducext[0:tm, :] = duc
        ducext[tm:tm + CONV_HALO, :] = uccar[...]
        uccar[...] = duc[0:CONV_HALO, :]
        _shifted_copies(ducext, dsh, tm)
        for r0 in range(0, tm, CONV_CHUNK):
            crow = slice(r0, r0 + CONV_CHUNK)
            duc_c = ducext[crow, :]
            du = jnp.zeros((CONV_CHUNK, CONV_W), F32)
            for k in range(CONV_K):
                prod = duc_c * _tap(uext, ush, CONV_HALO - (CONV_K - 1) + k, r0, CONV_CHUNK)
                part = prod[0:8]
                for s in range(8, CONV_CHUNK, 8):
                    part = part + prod[s:s + 8]
                dcw8[k] += part
                du = du + cw_ref[k:k + 1, :] * _tap(ducext, dsh, CONV_K - 1 - k, r0, CONV_CHUNK)
            sgc = sgs[crow, :]
            dz_ref[crow, CV0:CV0 + CONV_W] = (du * sgc).astype(dz_ref.dtype)
            u_c = uext[CONV_HALO + r0:CONV_HALO + r0 + CONV_CHUNK, :]
            dz_ref[crow, CG0:CG0 + CONV_W] = (du * u_c * (1.0 - sgc)).astype(dz_ref.dtype)

        @pl.when(i == nt - 1)
        def _():
            dcw_ref[...] = jnp.sum(dcw8[...], axis=1)

        xc, r, ig, sp, la, a, mult = _lru_gates(z_ref, zh_ref, pv_ref, wa_ref, wx_ref, rxext, first, tm)
        h = hl_ref[...]
        rowi = lax.broadcasted_iota(jnp.int32, (tm, LRU_W), 0)
        hlast = jnp.where(first, 0.0, hlh_ref[7:8, :])
        hprev = jnp.where(rowi == 0, hlast, pltpu.roll(h, 1, 0))
        dyl = dy_ref[:, ATTN_W + CONV_W:ATTN_W + CONV_W + LRU_W]
        gl, dgl = _gelu(z_ref[:, RG0:RG0 + LRU_W])
        dz_ref[:, RG0:RG0 + LRU_W] = (dyl * h * dgl).astype(dz_ref.dtype)
        dh = dyl * gl + jnp.where(rowi == tm - 1, gcar[0:1, :], 0.0)
        c = jnp.where(rowi == tm - 1, 0.0, pltpu.roll(a, tm - 1, 0))
        _, gg = _scan(c, dh, tm, reverse=True)
        gcar[0:1, :] = a[0:1, :] * gg[0:1, :]
        dmult = gg * (ig * xc)
        dig = gg * mult * xc
        dxc = gg * mult * ig
        dla = gg * hprev * a - dmult * a * a / mult
        dr = dla * (-LRU_C * sp)
        lam = pv_ref[R_LAM:R_LAM + 1, :]
        dpv_ref[R_LAM:R_LAM + 1, :] += jnp.sum(dla * (-LRU_C * r), axis=0, keepdims=True) * (-_sigmoid(-lam))
        dpa = dr * r * (1.0 - r)
        dpx = dig * ig * (1.0 - ig)
        addrow(R_BA, dpa)
        addrow(R_BX, dpx)
        dxc = dxc + _dot_nt(dpa, wa_ref[...]) + _dot_nt(dpx, wx_ref[...])
        dwa_ref[...] += _dot_tn(xc, dpa)
        dwx_ref[...] += _dot_tn(xc, dpx)
        addrow(R_LCONV_B, dxc)
        dxcext[0:tm, :] = dxc
        dxcext[tm:tm + LRU_HALO, :] = xccar[...]
        xccar[...] = dxc[0:LRU_HALO, :]
        drx = jnp.zeros((tm, LRU_W), F32)
        for k in range(LRU_K):
            addrow(R_LCW + k, dxc * rxext[pl.ds(LRU_HALO - (LRU_K - 1) + k, tm), :])
            drx = drx + pv_ref[R_LCW + k:R_LCW + k + 1, :] * dxcext[pl.ds(LRU_K - 1 - k, tm), :]
        dz_ref[:, RX0:RX0 + LRU_W] = drx.astype(dz_ref.dtype)

    tile = lambda w: pl.BlockSpec((tm, w), lambda i: (rev(i), 0))
    in_specs = [tile(D_MODEL)] + _mixer_in_specs(tm, rev) + [
        tile(D_MODEL), tile(LRU_W),
        pl.BlockSpec((8, LRU_W), lambda i: (jnp.maximum(rev(i) * (tm // 8) - 1, 0), 0)),
        tile(CONV_W), pl.BlockSpec((nb, 4 * BLK, 4 * BLK), lambda i: (rev(i), 0, 0)),
        pl.BlockSpec((nb, 4 * BLK, BLK), lambda i: (rev(i), 0, 0))]
    return _call(
        body, name, (nt,), in_specs,
        [tile(IN_W), _acc_spec((8, 4 * BLK)), _acc_spec((32, CONV_W)), _acc_spec((16, CONV_W)),
         _acc_spec((LRU_W, LRU_W)), _acc_spec((LRU_W, LRU_W))],
        [_sds((t, IN_W), MX), _sds((8, 4 * BLK), F32), _sds((32, CONV_W), F32), _sds((16, CONV_W), F32),
         _sds((LRU_W, LRU_W), F32), _sds((LRU_W, LRU_W), F32)],
        [pltpu.VMEM((tm + CONV_HALO, CONV_W), F32), pltpu.VMEM((7, tm + CONV_HALO - 8, CONV_W), F32),
         pltpu.VMEM((tm, CONV_W), F32), pltpu.VMEM((tm + LRU_HALO, LRU_W), F32),
         pltpu.VMEM((KV_W, tm + BLK), F32), pltpu.VMEM((KV_W, tm + BLK), F32),
         pltpu.VMEM((tm + CONV_HALO, CONV_W), F32), pltpu.VMEM((7, tm + CONV_HALO - 8, CONV_W), F32),
         pltpu.VMEM((32, 8, CONV_W), F32), pltpu.VMEM((tm + LRU_HALO, LRU_W), F32),
         pltpu.VMEM((KV_W, BLK), F32), pltpu.VMEM((KV_W, BLK), F32),
         pltpu.VMEM((CONV_HALO, CONV_W), F32), pltpu.VMEM((LRU_HALO, LRU_W), F32), pltpu.VMEM((8, LRU_W), F32)],
        [dy, z, z, sink, cw, pv, wa, wx, ycat, hl, hl, uc, probs, psinks], riders)


def _post_fwd(ycat, h0, gmix, w_out, g2, w_up, w_down, name, riders=()):
    t = h0.shape[0]
    tm = _tile(t, POST_TILE)
    nj = D_FF // FF_BLK

    def body(y_ref, h_ref, gm_ref, wo_ref, g2_ref, wu_ref, wd_ref, h1_ref, a_ref, h2_ref, ym_ref, hn_ref):
        ym, _, _ = _group_rms_fwd(y_ref[...], gm_ref[...])
        ym = ym.astype(MX)
        ym_ref[...] = ym
        h1 = h_ref[...] + jnp.dot(ym, wo_ref[...], preferred_element_type=F32)
        h1_ref[...] = h1
        hn, _, _ = _rms_fwd(h1, g2_ref[...])
        hn = hn.astype(MX)
        hn_ref[...] = hn
        for j in range(nj):
            u = jnp.dot(hn, wu_ref[j], preferred_element_type=F32)
            a_ref[:, j * FF_BLK:(j + 1) * FF_BLK] = jnp.square(jnp.maximum(u, 0.0)).astype(MX)
        h2_ref[...] = h1 + jnp.dot(a_ref[...], wd_ref[...], preferred_element_type=F32)

    tile = lambda w: pl.BlockSpec((tm, w), lambda i: (i, 0))
    return _call(
        body, name, (t // tm,),
        [tile(D_MODEL), tile(D_MODEL), _const_spec((1, D_MODEL)), _const_spec((D_MODEL, D_MODEL)),
         _const_spec((1, D_MODEL)), _const_spec((nj, D_MODEL, FF_BLK)), _const_spec((D_FF, D_MODEL))],
        [tile(D_MODEL), tile(D_FF), tile(D_MODEL), tile(D_MODEL), tile(D_MODEL)],
        [_sds((t, D_MODEL), F32), _sds((t, D_FF), MX), _sds((t, D_MODEL), F32), _sds((t, D_MODEL), MX),
         _sds((t, D_MODEL), MX)],
        [], [ycat, h0, gmix, w_out, g2, w_up, w_down], riders)


def _ffn_bwd(dh2, act, h1, g2, w_up_t, w_down, name, riders=()):
    t = h1.shape[0]
    tm = _tile(t, POST_TILE)
    nj = D_FF // FF_BLK

    def body(dh2_ref, a_ref, h1_ref, g2_ref, wut_ref, wd_ref, dh1_ref, dh1b_ref, dh2b_ref, du_ref, dg2_ref):
        @pl.when(pl.program_id(0) == 0)
        def _():
            dg2_ref[...] = jnp.zeros_like(dg2_ref)

        dh2 = dh2_ref[...]
        dh2b = dh2.astype(MX)
        dh2b_ref[...] = dh2b
        for j in range(nj):
            cols = slice(j * FF_BLK, (j + 1) * FF_BLK)
            da = _dot_nt(dh2b, wd_ref[j])
            du_ref[:, cols] = (da * (2.0 * jnp.sqrt(a_ref[:, cols].astype(F32)))).astype(MX)
        dhn = jnp.dot(du_ref[...], wut_ref[...], preferred_element_type=F32)
        _, xh, r = _rms_fwd(h1_ref[...], g2_ref[...])
        dx, dg = _rms_bwd(dhn, xh, r, g2_ref[...])
        dg2_ref[...] += dg
        dh1 = dh2 + dx
        dh1_ref[...] = dh1
        dh1b_ref[...] = dh1.astype(MX)

    tile = lambda w: pl.BlockSpec((tm, w), lambda i: (i, 0))
    return _call(
        body, name, (t // tm,),
        [tile(D_MODEL), tile(D_FF), tile(D_MODEL), _const_spec((1, D_MODEL)),
         _const_spec((D_FF, D_MODEL)), _const_spec((nj, FF_BLK, D_MODEL))],
        [tile(D_MODEL), tile(D_MODEL), tile(D_MODEL), tile(D_FF), _acc_spec((1, D_MODEL))],
        [_sds((t, D_MODEL), F32), _sds((t, D_MODEL), MX), _sds((t, D_MODEL), MX), _sds((t, D_FF), MX),
         _sds((1, D_MODEL), F32)],
        [], [dh2, act, h1, g2, w_up_t, w_down], riders)


def _mix_bwd(dh1, ycat, ym, gmix, w_out, name):
    t = dh1.shape[0]
    tm = _tile(t)
    nk = t // tm
    r = D_MODEL // N_DEV

    def body(dh1_ref, y_ref, ym_ref, gm_ref, wo_ref, dy_ref, dgm_ref, o_ref, o16_ref, acc):
        k = pl.program_id(0)

        @pl.when(k == 0)
        def _():
            dgm_ref[...] = jnp.zeros_like(dgm_ref)
            acc[...] = jnp.zeros_like(acc)

        dh = dh1_ref[...]
        acc[...] += _dot_tn(ym_ref[...], dh)
        dym = _dot_nt(dh, wo_ref[...])
        gm = gm_ref[...]
        _, yh, rr = _group_rms_fwd(y_ref[...], gm)
        outs, dgs = [], []
        for (a, b), rg in zip(_GROUPS, rr):
            dxg, dgg = _rms_bwd(dym[:, a:b], yh[:, a:b], rg, gm[:, a:b])
            outs.append(dxg)
            dgs.append(dgg)
        dy_ref[...] = jnp.concatenate(outs, axis=1)
        dgm_ref[...] += jnp.concatenate(dgs, axis=1)

        @pl.when(k == nk - 1)
        def _():
            for d in range(N_DEV):
                v = acc[d * r:(d + 1) * r, :]
                o_ref[d] = v
                o16_ref[d] = v.astype(o16_ref.dtype)

    tile = pl.BlockSpec((tm, D_MODEL), lambda i: (i, 0))
    slabs = _const_spec((N_DEV, r, D_MODEL))
    (dy, dgm, dw, dw16), _ = _call(
        body, name, (nk,), [tile, tile, tile, _const_spec((1, D_MODEL)), _const_spec((D_MODEL, D_MODEL))],
        [tile, _acc_spec((1, D_MODEL)), slabs, slabs],
        [_sds((t, D_MODEL), F32), _sds((1, D_MODEL), F32), _sds((N_DEV, r, D_MODEL), F32),
         _sds((N_DEV, r, D_MODEL), WIRE)],
        [pltpu.VMEM((D_MODEL, D_MODEL), F32)], [dh1, ycat, ym, gmix, w_out])
    return dy, dgm, (dw, dw16)


def _in_bwd(dz, h0, dh1, g1, w_in_t, after, name):
    t = h0.shape[0]
    tm = _tile(t, STREAM_TILE)

    def body(dz_ref, h_ref, dh1_ref, g_ref, w_ref, after_ref, dh0_ref, dg_ref):
        @pl.when(pl.program_id(0) == 0)
        def _():
            dg_ref[...] = jnp.zeros_like(dg_ref)

        dhn = _dot(dz_ref[...], w_ref[...])
        _, xh, r = _rms_fwd(h_ref[...], g_ref[...])
        dx, dg = _rms_bwd(dhn, xh, r, g_ref[...])
        dg_ref[...] += dg
        dh0_ref[...] = dh1_ref[...] + dx

    tile = lambda w: pl.BlockSpec((tm, w), lambda i: (i, 0))
    (dh0, dg), _ = _call(
        body, name, (t // tm,),
        [tile(IN_W), tile(D_MODEL), tile(D_MODEL), _const_spec((1, D_MODEL)), _const_spec((IN_W, D_MODEL)),
         _const_spec((8, 128))],
        [tile(D_MODEL), _acc_spec((1, D_MODEL))], [_sds((t, D_MODEL), F32), _sds((1, D_MODEL), F32)],
        [], [dz, h0, dh1, g1, w_in_t, after])
    return dh0, dg


def _in_bwd_dw(dz, h0, dh1, g1, w_in_t, name):
    t = h0.shape[0]
    tm = _tile(t)
    nk = t // tm

    def body(dz_ref, h_ref, dh1_ref, g_ref, w_ref, dh0_ref, dg_ref, o_ref, o16_ref, acc):
        k = pl.program_id(0)

        @pl.when(k == 0)
        def _():
            dg_ref[...] = jnp.zeros_like(dg_ref)
            acc[...] = jnp.zeros_like(acc)

        dz_t = dz_ref[...]
        hn, xh, r = _rms_fwd(h_ref[...], g_ref[...])
        acc[...] += _dot_tn(dz_t, hn)
        dhn = _dot(dz_t, w_ref[...])
        dx, dg = _rms_bwd(dhn, xh, r, g_ref[...])
        dg_ref[...] += dg
        dh0_ref[...] = dh1_ref[...] + dx

        @pl.when(k == nk - 1)
        def _():
            for d in range(N_DEV):
                v = acc[d * IN_SHARD:(d + 1) * IN_SHARD, :]
                o_ref[d] = v
                o16_ref[d] = v.astype(o16_ref.dtype)

    tile = lambda w: pl.BlockSpec((tm, w), lambda i: (i, 0))
    slabs = _const_spec((N_DEV, IN_SHARD, D_MODEL))
    (dh0, dg, dw, dw16), _ = _call(
        body, name, (nk,),
        [tile(IN_W), tile(D_MODEL), tile(D_MODEL), _const_spec((1, D_MODEL)), _const_spec((IN_W, D_MODEL))],
        [tile(D_MODEL), _acc_spec((1, D_MODEL)), slabs, slabs],
        [_sds((t, D_MODEL), F32), _sds((1, D_MODEL), F32), _sds((N_DEV, IN_SHARD, D_MODEL), F32),
         _sds((N_DEV, IN_SHARD, D_MODEL), WIRE)],
        [pltpu.VMEM((IN_W, D_MODEL), F32)], [dz, h0, dh1, g1, w_in_t])
    return dh0, dg, (dw, dw16)


def _loss_head(h, gf, target, name):
    t = h.shape[0]
    tm = _tile(t, STREAM_TILE)

    def body(h_ref, g_ref, t_ref, dh_ref, loss_ref, dg_ref):
        @pl.when(pl.program_id(0) == 0)
        def _():
            loss_ref[...] = jnp.zeros_like(loss_ref)
            dg_ref[...] = jnp.zeros_like(dg_ref)

        g = g_ref[...]
        y, xh, r = _rms_fwd(h_ref[...], g)
        err = y - t_ref[...]
        part = 0.5 * jnp.sum(jnp.mean(err * err, axis=-1, keepdims=True), axis=0, keepdims=True)
        loss_ref[...] += jnp.broadcast_to(part, loss_ref.shape)
        dx, dg = _rms_bwd(err * (1.0 / D_MODEL), xh, r, g)
        dg_ref[...] += dg
        dh_ref[...] = dx

    tile = pl.BlockSpec((tm, D_MODEL), lambda i: (i, 0))
    (dh, loss, dg), _ = _call(
        body, name, (t // tm,), [tile, _const_spec((1, D_MODEL)), tile],
        [tile, _acc_spec((1, 128)), _acc_spec((1, D_MODEL))],
        [_sds((t, D_MODEL), F32), _sds((1, 128), F32), _sds((1, D_MODEL), F32)], [], [h, gf, target])
    return dh, loss, dg


def _dw(x, y, name, split, bm, bn):
    t, m = x.shape
    n = y.shape[1]
    tk = _tile(t, DW_TILE)
    nk = t // tk
    if split == "rows":
        assert bn == n
        r, c = m // N_DEV, n
        per = bm // r
        out_block = pl.BlockSpec((per, r, c), lambda a, b, k: (a, 0, 0))
    else:
        assert bm == m
        r, c = m, n // N_DEV
        per = bn // c
        out_block = pl.BlockSpec((per, r, c), lambda a, b, k: (b, 0, 0))

    def body(x_ref, y_ref, o_ref, o16_ref, acc):
        k = pl.program_id(2)

        @pl.when(k == 0)
        def _():
            acc[...] = jnp.zeros_like(acc)

        acc[...] += _dot_tn(x_ref[...], y_ref[...])

        @pl.when(k == nk - 1)
        def _():
            for d in range(per):
                v = acc[d * r:(d + 1) * r, :] if split == "rows" else acc[:, d * c:(d + 1) * c]
                o_ref[d] = v
                o16_ref[d] = v.astype(o16_ref.dtype)

    return pl.pallas_call(
        body, name=name, grid=(m // bm, n // bn, nk),
        in_specs=[pl.BlockSpec((tk, bm), lambda a, b, k: (k, a)), pl.BlockSpec((tk, bn), lambda a, b, k: (k, b))],
        out_specs=[out_block, out_block],
        out_shape=[_sds((N_DEV, r, c), F32), _sds((N_DEV, r, c), WIRE)],
        scratch_shapes=[pltpu.VMEM((bm, bn), F32)],
        compiler_params=pltpu.CompilerParams(dimension_semantics=("arbitrary",) * 3, vmem_limit_bytes=VMEM_LIMIT),
    )(x, y)


def _adamw_math(w, g, m, v):
    m = ADAM_B1 * m + (1.0 - ADAM_B1) * g
    v = ADAM_B2 * v + (1.0 - ADAM_B2) * jnp.square(g)
    m_hat = m / (1.0 - ADAM_B1 ** ADAM_STEP)
    v_hat = v / (1.0 - ADAM_B2 ** ADAM_STEP)
    delta = -ADAM_LR * (m_hat / (jnp.sqrt(v_hat) + ADAM_EPS) + ADAM_WD * w)
    return delta, m, v


def _adamw_shard(g_own, g_recv, dev, w, m, v, after, name):
    _, r, c = w.shape
    br = r
    for cand in (256, 128, 112, 64, 56, 32, 16, 8):
        if r % cand == 0:
            br = cand
            break
    nr = r // br
    own = lambda l: pl.BlockSpec((1, br, c), lambda ll, i, d: (d[0], jnp.where(ll == l, i, (nr - 1) * (1 - l)), 0))
    recv = lambda l: pl.BlockSpec((N_DEV - 1, br, c), lambda ll, i, d: (0, jnp.where(ll == l, i, (nr - 1) * (1 - l)), 0))

    def body(dev_ref, go0, gr0, go1, gr1, w_ref, m_ref, v_ref, after_ref, g_out, d_out, m_out, v_out):
        def update(go_ref, gr_ref):
            g = go_ref[0]
            for j in range(N_DEV - 1):
                g = g + gr_ref[j].astype(F32)
            delta, mn, vn = _adamw_math(w_ref[0], g, m_ref[0], v_ref[0])
            g_out[0] = g
            d_out[0] = delta
            m_out[0] = mn
            v_out[0] = vn

        layer = pl.program_id(0)
        pl.when(layer == 0)(lambda: update(go0, gr0))
        pl.when(layer == 1)(lambda: update(go1, gr1))

    tile = pl.BlockSpec((1, br, c), lambda ll, i, d: (ll, i, 0))
    return pl.pallas_call(
        body, name=name,
        grid_spec=pltpu.PrefetchScalarGridSpec(
            num_scalar_prefetch=1, grid=(2, nr),
            in_specs=[own(0), recv(0), own(1), recv(1), tile, tile, tile,
                      pl.BlockSpec((8, 128), lambda ll, i, d: (0, 0))],
            out_specs=[tile, tile, tile, tile]),
        out_shape=[_sds((2, r, c), F32)] * 4,
        compiler_params=pltpu.CompilerParams(dimension_semantics=("arbitrary",) * 2, vmem_limit_bytes=VMEM_LIMIT),
    )(dev, g_own[0], g_recv[0], g_own[1], g_recv[1], w, m, v, after)


def _adamw_small(gs, ws, ms, vs, name):
    n = len(gs)

    def body(*refs):
        g_refs, w_refs, m_refs, v_refs = (refs[k * n:(k + 1) * n] for k in range(4))
        outs = refs[4 * n:]
        for k in range(n):
            delta, mn, vn = _adamw_math(w_refs[k][...], g_refs[k][...], m_refs[k][...], v_refs[k][...])
            outs[k][...] = delta
            outs[n + k][...] = mn
            outs[2 * n + k][...] = vn

    shapes = [_sds(w.shape, F32) for w in ws]
    res = pl.pallas_call(body, name=name, out_shape=shapes * 3,
                         compiler_params=pltpu.CompilerParams(vmem_limit_bytes=VMEM_LIMIT))(*gs, *ws, *ms, *vs)
    return res[:n], res[n:2 * n], res[2 * n:]


def _sum_parts(own, recv, dev, name):
    def body(dev_ref, own_ref, recv_ref, o_ref):
        me = dev_ref[0]

        def block(d):
            f = jnp.bitwise_xor(me, d)
            return jnp.where(f == 0, own_ref[...], recv_ref[jnp.maximum(f - 1, 0)])

        g = block(0)
        for d in range(1, N_DEV):
            g = g + block(d)
        o_ref[...] = g

    return pl.pallas_call(
        body, name=name,
        grid_spec=pltpu.PrefetchScalarGridSpec(
            num_scalar_prefetch=1, grid=(1,),
            in_specs=[pl.BlockSpec(own.shape, lambda i, d: (0, 0)), pl.BlockSpec(recv.shape, lambda i, d: (0, 0, 0))],
            out_specs=pl.BlockSpec(own.shape, lambda i, d: (0, 0))),
        out_shape=_sds(own.shape, F32))(dev, own, recv)


HBM = pl.BlockSpec(memory_space=pltpu.HBM)
SEM = pl.BlockSpec(memory_space=pltpu.SEMAPHORE)
EFFECT = pltpu.SideEffectType.DATAFLOW_SIDE_EFFECTING


def _direct_copies(srcs, lands, ssem, rsem, scatter):
    x, y, c = _me()
    out = []
    for a in range(len(srcs)):
        for f in range(1, N_DEV):
            px = 1 - x if f & 4 else x
            py = 1 - y if f & 2 else y
            pc = 1 - c if f & 1 else c
            out.append(pltpu.make_async_remote_copy(
                src_ref=srcs[a].at[4 * px + 2 * py + pc] if scatter else srcs[a], dst_ref=lands[a].at[f - 1],
                send_sem=ssem.at[7 * a + f - 1], recv_sem=rsem.at[7 * a + f - 1],
                device_id=(px, py, pc), device_id_type=MESH))
    return out


def _send_start(arrays, scatter, name):
    arrays = list(arrays)
    n = len(arrays)
    lands = [lax.empty((N_DEV - 1,) + (a.shape[1:] if scatter else a.shape), a.dtype) for a in arrays]

    def body(*refs):
        srcs, lnds, ssem, rsem, token = refs[:n], refs[n:2 * n], refs[2 * n], refs[2 * n + 1], refs[-1]
        for cp in _direct_copies(srcs, lnds, ssem, rsem, scatter):
            cp.start()
        token[...] = jnp.zeros_like(token)

    hbm = lambda a: pltpu.HBM(a.shape, a.dtype)
    res = pl.pallas_call(
        body, name=name,
        out_shape=(pltpu.SemaphoreType.DMA((7 * n,)), pltpu.SemaphoreType.DMA((7 * n,)),
                   *[hbm(a) for a in arrays + lands], _sds((8, 128), F32)),
        in_specs=[HBM] * (2 * n),
        out_specs=(SEM, SEM, *[HBM] * (2 * n), pl.BlockSpec(memory_space=pltpu.VMEM)),
        input_output_aliases={i: 2 + i for i in range(2 * n)},
        compiler_params=pltpu.CompilerParams(has_side_effects=EFFECT),
    )(*[pltpu.with_memory_space_constraint(a, pltpu.HBM) for a in arrays + lands])
    return types.SimpleNamespace(ssem=res[0], rsem=res[1], srcs=list(res[2:2 + n]), lands=list(res[2 + n:2 + 2 * n]),
                                 token=res[-1], scatter=scatter)


def _send_wait(h, after, name):
    n = len(h.srcs)

    def body(*refs):
        srcs, lnds, ssem, rsem = refs[:n], refs[n:2 * n], refs[2 * n], refs[2 * n + 1]
        for cp in _direct_copies(srcs, lnds, ssem, rsem, h.scatter):
            cp.wait_send()
            cp.wait_recv()

    hbm = lambda a: pltpu.HBM(a.shape, a.dtype)
    res = pl.pallas_call(
        body, name=name,
        out_shape=tuple(hbm(a) for a in h.srcs + h.lands),
        in_specs=[HBM] * (2 * n) + [SEM, SEM, ANY], out_specs=[HBM] * (2 * n),
        input_output_aliases={i: i for i in range(2 * n)},
        compiler_params=pltpu.CompilerParams(has_side_effects=EFFECT),
    )(*h.srcs, *h.lands, h.ssem, h.rsem, after)
    return list(res[:n]), list(res[n:])


def _block_diag(w):
    out = jnp.zeros((LRU_W, LRU_W), w.dtype)
    for h in range(4):
        out = lax.dynamic_update_slice(out, w[h], (h * 64, h * 64))
    return out


def _layer_params(p, l):
    row = lambda a: a[l].reshape(1, -1)
    sink_rows = jnp.repeat(p["attn_sinks"][l].reshape(4, 2), 2 * BLK, axis=1)
    sink_rows = jnp.concatenate([sink_rows, jnp.zeros((4, 4 * BLK), F32)], axis=0)
    cw = jnp.concatenate([p["conv_dw_w"][l], jnp.zeros((1, CONV_W), F32)], axis=0)
    pv = jnp.concatenate([
        row(p["conv_dw_b"]), row(p["conv_ln_g"]), row(p["conv_ln_b"]), row(p["lru_conv_b"]), row(p["lru_ba"]),
        row(p["lru_bx"]), row(p["lru_lambda"]), jnp.zeros((1, LRU_W), F32), p["lru_conv_w"][l],
        jnp.zeros((4, LRU_W), F32)], axis=0)
    return dict(
        g1=row(p["norm1"]), sink=sink_rows, cw=cw, pv=pv,
        wa=_block_diag(p["lru_wa"][l]).astype(MX), wx=_block_diag(p["lru_wx"][l]).astype(MX),
        gmix=row(p["mix_norm"]), g2=row(p["norm2"]))


_SMALL = ["norm1", "attn_sinks", "conv_dw_w", "conv_dw_b", "conv_ln_g", "conv_ln_b", "lru_conv_w", "lru_conv_b",
          "lru_wa", "lru_ba", "lru_wx", "lru_bx", "lru_lambda", "mix_norm", "norm2"]
_BIG = ["w_in", "w_out", "w_up", "w_down"]
_WEIGHTS = ["norm1", "w_in", "attn_sinks", "conv_dw_w", "conv_dw_b", "conv_ln_g", "conv_ln_b", "lru_conv_w",
            "lru_conv_b", "lru_wa", "lru_ba", "lru_wx", "lru_bx", "lru_lambda", "mix_norm", "w_out", "norm2", "w_up",
            "w_down", "final_norm"]


def kernel(x, norm1, w_in, attn_sinks, conv_dw_w, conv_dw_b, conv_ln_g, conv_ln_b, lru_conv_w, lru_conv_b, lru_wa, lru_ba, lru_wx, lru_bx, lru_lambda, mix_norm, w_out, norm2, w_up, w_down, final_norm, loss_target, m_norm1, m_w_in, m_attn_sinks, m_conv_dw_w, m_conv_dw_b, m_conv_ln_g, m_conv_ln_b, m_lru_conv_w, m_lru_conv_b, m_lru_wa, m_lru_ba, m_lru_wx, m_lru_bx, m_lru_lambda, m_mix_norm, m_w_out, m_norm2, m_w_up, m_w_down, m_final_norm, v_norm1, v_w_in, v_attn_sinks, v_conv_dw_w, v_conv_dw_b, v_conv_ln_g, v_conv_ln_b, v_lru_conv_w, v_lru_conv_b, v_lru_wa, v_lru_ba, v_lru_wx, v_lru_bx, v_lru_lambda, v_mix_norm, v_w_out, v_norm2, v_w_up, v_w_down, v_final_norm):
    w = dict(norm1=norm1, w_in=w_in, attn_sinks=attn_sinks, conv_dw_w=conv_dw_w, conv_dw_b=conv_dw_b,
             conv_ln_g=conv_ln_g, conv_ln_b=conv_ln_b, lru_conv_w=lru_conv_w, lru_conv_b=lru_conv_b, lru_wa=lru_wa,
             lru_ba=lru_ba, lru_wx=lru_wx, lru_bx=lru_bx, lru_lambda=lru_lambda, mix_norm=mix_norm, w_out=w_out,
             norm2=norm2, w_up=w_up, w_down=w_down, final_norm=final_norm)
    m = dict(norm1=m_norm1, w_in=m_w_in, attn_sinks=m_attn_sinks, conv_dw_w=m_conv_dw_w, conv_dw_b=m_conv_dw_b,
             conv_ln_g=m_conv_ln_g, conv_ln_b=m_conv_ln_b, lru_conv_w=m_lru_conv_w, lru_conv_b=m_lru_conv_b,
             lru_wa=m_lru_wa, lru_ba=m_lru_ba, lru_wx=m_lru_wx, lru_bx=m_lru_bx, lru_lambda=m_lru_lambda,
             mix_norm=m_mix_norm, w_out=m_w_out, norm2=m_norm2, w_up=m_w_up, w_down=m_w_down, final_norm=m_final_norm)
    v = dict(norm1=v_norm1, w_in=v_w_in, attn_sinks=v_attn_sinks, conv_dw_w=v_conv_dw_w, conv_dw_b=v_conv_dw_b,
             conv_ln_g=v_conv_ln_g, conv_ln_b=v_conv_ln_b, lru_conv_w=v_lru_conv_w, lru_conv_b=v_lru_conv_b,
             lru_wa=v_lru_wa, lru_ba=v_lru_ba, lru_wx=v_lru_wx, lru_bx=v_lru_bx, lru_lambda=v_lru_lambda,
             mix_norm=v_mix_norm, w_out=v_w_out, norm2=v_norm2, w_up=v_w_up, w_down=v_w_down, final_norm=v_final_norm)
    depth = w_in.shape[0]
    xi, yi, ci = _me()
    dev = (4 * xi + 2 * yi + ci).astype(jnp.int32)
    dev1 = dev.reshape(1)
    tr = lambda a: jnp.swapaxes(a, 1, 2)
    w_t, m_t, v_t = tr(w_in), tr(m_w_in), tr(v_w_in)
    wb = {n: w[n].astype(MX) for n in _BIG if n != "w_in"}
    wb["w_in"] = w_t.astype(MX)
    layer_shards = lambda l: [wb["w_out"][l], wb["w_up"][l], wb["w_down"][l]]

    _, ((g_in0, g_cw, g_lcw),) = _call(None, "gather_first", None, [], [], [], [], [],
                                        [_gather_rider([wb["w_in"][0], conv_dw_w, lru_conv_w])])
    cols = lambda g: jnp.moveaxis(g, 0, -2).reshape(g.shape[1:-1] + (N_DEV * g.shape[-1],))
    p = dict(w)
    p["conv_dw_w"] = cols(g_cw)
    p["lru_conv_w"] = cols(g_lcw)
    lp = [_layer_params(p, l) for l in range(depth)]

    gathered = [dict(w_in=g_in0.reshape(IN_W, D_MODEL)), dict()]
    saved = []
    h = x[0]
    for l in range(depth):
        q, gw = lp[l], gathered[l]
        z, hn1 = _ln_in(h, q["g1"], gw["w_in"], l == 0, f"ln_in{l}")
        (ycat, hl, uc, probs, psinks), got = _mixer_fwd(z, q["sink"], q["cw"], q["pv"], q["wa"], q["wx"],
                                                        f"mixer_fwd{l}", [_gather_rider(layer_shards(l))])
        gw["w_out"], gw["w_up"], gw["w_down"] = got[0]
        gw["w_out"] = gw["w_out"].reshape(D_MODEL, D_MODEL)
        riders = [_gather_rider([wb["w_in"][l + 1]])] if l + 1 < depth else []
        (h1, act, h2, ym, hn2), got = _post_fwd(ycat, h, q["gmix"], gw["w_out"], q["g2"], gw["w_up"],
                                                gw["w_down"].reshape(D_FF, D_MODEL), f"post_fwd{l}", riders)
        if l + 1 < depth:
            gathered[l + 1]["w_in"] = got[0][0].reshape(IN_W, D_MODEL)
        saved.append(dict(h0=h, z=z, hn1=hn1, ycat=ycat, hl=hl, uc=uc, probs=probs, psinks=psinks, h1=h1, act=act,
                          ym=ym, hn2=hn2))
        h = h2
    dh, loss, dgf = _loss_head(h, final_norm.reshape(1, -1), loss_target[0], "loss_head")

    grads = [None] * depth
    big = {n: [None] * depth for n in _BIG}
    pending = []

    def send_pending():
        riders = [_scatter_rider([item[3] for item in pending])] if pending else []
        return riders, list(pending)

    def record(sent, got):
        for item, recv in zip(sent, got[0] if sent else []):
            big[item[0]][item[1]] = (item[2], recv)
        del pending[:len(sent)]

    for l in reversed(range(depth)):
        q, s, gw = lp[l], saved[l], gathered[l]
        riders, sent = send_pending()
        w_up_t = jnp.swapaxes(gw["w_up"], 1, 2).reshape(D_FF, D_MODEL)
        (dh1, dh1b, dhb, du, dg2), got = _ffn_bwd(dh, s["act"], s["h1"], q["g2"], w_up_t, gw["w_down"],
                                                  f"ffn_bwd{l}", riders)
        record(sent, got)
        dycat, dgm, d_wout = _mix_bwd(dh1b, s["ycat"], s["ym"], q["gmix"], gw["w_out"], f"mix_bwd{l}")
        pending.append(("w_down", l) + tuple(_dw(s["act"], dhb, f"dw_down{l}", "rows", 2048, D_MODEL)))
        pending.append(("w_up", l) + tuple(_dw(s["hn2"], du, f"dw_up{l}", "cols", D_MODEL, 2048)))
        pending.append(("w_out", l) + tuple(d_wout))
        riders, sent = send_pending()
        (dz, dsink, dcw, dpv, dwa, dwx), got = _mixer_bwd(
            dycat, s["z"], s["ycat"], s["hl"], s["uc"], s["probs"], s["psinks"], q["sink"], q["cw"], q["pv"], q["wa"],
            q["wx"], f"mixer_bwd{l}", riders)
        record(sent, got)
        if l > 0:
            dh, dg1, d_win = _in_bwd_dw(dz, s["h0"], dh1, q["g1"], gw["w_in"], f"in_bwd{l}")
            pending.append(("w_in", l) + tuple(d_win))
        else:
            d_win = _dw(dz, s["hn1"], f"dw_in{l}", "rows", IN_W, D_MODEL)
            win_sends = _send_start([d_win[1]], True, "scatter_w_in0_start")
            dh, dg1 = _in_bwd(dz, s["h0"], dh1, q["g1"], gw["w_in"], win_sends.token, f"in_bwd{l}")
        grads[l] = dict(dg1=dg1, dsink=dsink, dcw=dcw, dpv=dpv, dwa=dwa, dwx=dwx, dgm=dgm, dg2=dg2)

    acc = {k: jnp.stack([grads[l][k] for l in range(depth)]) for k in grads[0]}
    dpv = acc["dpv"]
    unblock = lambda a: jnp.concatenate([a[:, h * 64:(h + 1) * 64, h * 64:(h + 1) * 64] for h in range(4)], axis=1)
    by_name = dict(
        norm1=acc["dg1"][:, 0], attn_sinks=jnp.stack([acc["dsink"][:, 0:4, 0], acc["dsink"][:, 0:4, 2 * BLK]],
                                                     axis=2).reshape(depth, 8),
        conv_dw_w=acc["dcw"][:, 0:CONV_K], conv_dw_b=dpv[:, R_CONV_B], conv_ln_g=dpv[:, R_LN_G],
        conv_ln_b=dpv[:, R_LN_B], lru_conv_w=dpv[:, R_LCW:R_LCW + LRU_K], lru_conv_b=dpv[:, R_LCONV_B],
        lru_wa=unblock(acc["dwa"]), lru_ba=dpv[:, R_BA].reshape(depth, 4, 64), lru_wx=unblock(acc["dwx"]),
        lru_bx=dpv[:, R_BX].reshape(depth, 4, 64), lru_lambda=dpv[:, R_LAM], mix_norm=acc["dgm"][:, 0],
        norm2=acc["dg2"][:, 0])
    small = [by_name[n] for n in _SMALL] + [dgf, loss[:, 0:1]]

    def as_rows(a):
        flat = a.reshape(-1)
        pad = (-flat.size) % 1024
        if pad:
            flat = jnp.concatenate([flat, jnp.zeros((pad,), F32)])
        return flat.reshape(-1, 128)

    pieces = [as_rows(a) for a in small]
    packed = jnp.concatenate(pieces, axis=0)
    small_sends = _send_start([packed], False, "bcast_small_start")

    out = {}
    shard_update = lambda n, wmv, after: list(_adamw_shard(
        [big[n][l][0] for l in range(depth)], [big[n][l][1] for l in range(depth)], dev1, *wmv, after, f"adamw_{n}"))
    for n in ("w_out", "w_up", "w_down"):
        out[n] = shard_update(n, (w[n], m[n], v[n]), small_sends.token)
    _, (win_recv,) = _send_wait(win_sends, out["w_down"][1], "scatter_w_in0_wait")
    big["w_in"][0] = (d_win[0], win_recv)
    out["w_in"] = [tr(a) for a in shard_update("w_in", (w_t, m_t, v_t), jnp.zeros((8, 128), F32))]
    (packed,), (small_recv,) = _send_wait(small_sends, out["w_in"][1], "bcast_small_wait")
    summed = _sum_parts(packed, small_recv, dev1, "sum_small_grads")
    small_sums, row = [], 0
    for a, piece in zip(small, pieces):
        got = summed[row:row + piece.shape[0]]
        small_sums.append(got.reshape(a.shape) if a.size == piece.size else got.reshape(-1)[:a.size].reshape(a.shape))
        row += piece.shape[0]
    shard = lambda a: lax.dynamic_slice_in_dim(a, dev * (a.shape[-1] // N_DEV), a.shape[-1] // N_DEV, axis=a.ndim - 1)
    flat = {"lru_wa": (depth, LRU_W, 64), "lru_wx": (depth, LRU_W, 64), "final_norm": (1, D_MODEL)}
    gs, ws, ms, vs = [], [], [], []
    for n, g in zip(_SMALL + ["final_norm"], small_sums[:-1]):
        shp = flat.get(n, w[n].shape)
        gs.append((shard(g) if n in ("conv_dw_w", "lru_conv_w") else g).reshape(shp))
        ws.append(w[n].reshape(shp))
        ms.append(m[n].reshape(shp))
        vs.append(v[n].reshape(shp))
    sd, sm, sv = _adamw_small(gs, ws, ms, vs, "adamw_small")
    for j, n in enumerate(_SMALL + ["final_norm"]):
        out[n] = [a.reshape(w[n].shape) for a in (gs[j], sd[j], sm[j], sv[j])]
    loss_total = small_sums[-1][0, 0]

    result = [loss_total, dh[None]]
    for j in range(4):
        result += [out[n][j] for n in _WEIGHTS]
    return tuple(result)
```

```python
import types

import jax
import jax.numpy as jnp
from jax import lax
from jax.experimental import pallas as pl
from jax.experimental.pallas import tpu as pltpu

F32 = jnp.float32
MX = jnp.bfloat16
WIRE = jnp.bfloat16

D_MODEL = 1024
HEAD_DIM = 64
ATTN_W = 512
KV_W = 128
BLK = 128
CONV_W = 256
CONV_K = 31
LRU_W = 256
LRU_K = 4
LRU_C = 8.0
IN_W = 1792
D_FF = 4096
FF_BLK = 512
N_DEV = 8
IN_SHARD = IN_W // N_DEV
RMS_EPS = 1e-6
LN_EPS = 1e-5
MASK_VALUE = -1e30
SCALE = HEAD_DIM ** -0.5
CONV_HALO = 32
LRU_HALO = 8
CONV_CHUNK = 64
POST_TILE = 512
STREAM_TILE = 1024
DW_TILE = 1024
Q0, K0, V0, CV0, CG0, RX0, RG0 = 0, 512, 640, 768, 1024, 1280, 1536
R_CONV_B, R_LN_G, R_LN_B, R_LCONV_B, R_BA, R_BX, R_LAM, R_LCW = 0, 1, 2, 3, 4, 5, 6, 8

ADAM_LR, ADAM_B1, ADAM_B2, ADAM_EPS, ADAM_WD, ADAM_STEP = 0.001, 0.9, 0.999, 1e-08, 0.01, 10

VMEM_LIMIT = 56 * 1024 * 1024
MESH = pl.DeviceIdType.MESH
ANY = pl.BlockSpec(memory_space=pl.ANY)


def _tile(t, cap=512):
    return min(cap, t)


def _dot(a, b):
    return jnp.dot(a.astype(MX), b.astype(MX), preferred_element_type=F32)


def _dot_nt(a, b):
    return lax.dot_general(a.astype(MX), b.astype(MX), (((1,), (1,)), ((), ())), preferred_element_type=F32)


def _dot_tn(a, b):
    return lax.dot_general(a.astype(MX), b.astype(MX), (((0,), (0,)), ((), ())), preferred_element_type=F32)


def _const_spec(shape):
    nd = len(shape)
    return pl.BlockSpec(shape, lambda *_: (0,) * nd, pipeline_mode=pl.Buffered(1))


def _acc_spec(shape):
    nd = len(shape)
    return pl.BlockSpec(shape, lambda *_: (0,) * nd)


def _sds(shape, dtype):
    return jax.ShapeDtypeStruct(shape, dtype)


def _sigmoid(x):
    return jax.nn.sigmoid(x)


def _rms_fwd(x, g):
    r = lax.rsqrt(jnp.mean(x * x, axis=-1, keepdims=True) + RMS_EPS)
    xh = x * r
    return xh * g, xh, r


def _rms_bwd(dy, xh, r, g):
    t = dy * g
    dx = r * (t - xh * jnp.mean(t * xh, axis=-1, keepdims=True))
    return dx, jnp.sum(dy * xh, axis=0, keepdims=True)


_GROUPS = ((0, 512), (512, 768), (768, 1024))


def _group_rms_fwd(y, g):
    parts = [_rms_fwd(y[:, a:b], g[:, a:b]) for a, b in _GROUPS]
    return (jnp.concatenate([p[0] for p in parts], axis=1),
            jnp.concatenate([p[1] for p in parts], axis=1),
            [p[2] for p in parts])


def _gelu(x):
    c = 0.7978845608028654
    u = c * (x + 0.044715 * x * x * x)
    th = jnp.tanh(u)
    val = 0.5 * x * (1.0 + th)
    grad = 0.5 * (1.0 + th) + 0.5 * x * (1.0 - th * th) * c * (1.0 + 3.0 * 0.044715 * x * x)
    return val, grad


def _neg_expm1(x):
    series = -x * (1.0 + x * (0.5 + x * (1.0 / 6.0 + x * (1.0 / 24.0))))
    return jnp.where(x > -0.02, series, 1.0 - jnp.exp(x))


def _me():
    return lax.axis_index("x"), lax.axis_index("y"), lax.axis_index("c")


def _gather_rider(arrays):
    arrays = list(arrays)
    n = len(arrays)

    def plan(ins, outs, sems):
        ssem, rsem, lsem = sems
        x, y, c = _me()
        chips = [(1 - x, y), (x, 1 - y), (1 - x, 1 - y)]

        def copy(a, k, block, to, own=False):
            dst = outs[a].at[4 * block[0] + 2 * block[1] + block[2]]
            return pltpu.make_async_remote_copy(
                src_ref=ins[a] if own else dst, dst_ref=dst, send_sem=ssem.at[7 * a + k],
                recv_sem=rsem.at[7 * a + k], device_id=to, device_id_type=MESH)

        return x, y, c, chips, copy, lsem

    def start(ins, outs, sems):
        x, y, c, chips, copy, lsem = plan(ins, outs, sems)
        for a in range(n):
            pltpu.make_async_copy(ins[a], outs[a].at[4 * x + 2 * y + c], lsem.at[a]).start()
            copy(a, 0, (x, y, c), (x, y, 1 - c), own=True).start()
            for j, chip in enumerate(chips):
                copy(a, 1 + j, (x, y, c), (*chip, c), own=True).start()

    def mid(ins, outs, sems):
        x, y, c, chips, copy, _ = plan(ins, outs, sems)
        for a in range(n):
            for j, chip in enumerate(chips):
                copy(a, 1 + j, (*chip, c), (x, y, c)).wait_recv()
                copy(a, 4 + j, (*chip, c), (x, y, 1 - c)).start()

    def finish(ins, outs, sems):
        x, y, c, chips, copy, lsem = plan(ins, outs, sems)
        for a in range(n):
            copy(a, 0, (x, y, 1 - c), (x, y, c)).wait_recv()
            for j, chip in enumerate(chips):
                copy(a, 4 + j, (*chip, 1 - c), (x, y, c)).wait_recv()
        for a in range(n):
            copy(a, 0, (x, y, c), (x, y, 1 - c), own=True).wait_send()
            for j, chip in enumerate(chips):
                copy(a, 1 + j, (x, y, c), (*chip, c), own=True).wait_send()
                copy(a, 4 + j, (*chip, c), (x, y, 1 - c)).wait_send()
            pltpu.make_async_copy(ins[a], outs[a].at[4 * x + 2 * y + c], lsem.at[a]).wait()

    return types.SimpleNamespace(
        arrays=arrays, out_shape=[_sds((N_DEV,) + a.shape, a.dtype) for a in arrays],
        scratch=[pltpu.SemaphoreType.DMA((7 * n,)), pltpu.SemaphoreType.DMA((7 * n,)), pltpu.SemaphoreType.DMA((n,))],
        start=start, mid=mid, finish=finish)


def _scatter_rider(arrays):
    arrays = list(arrays)
    n = len(arrays)

    def copies(ins, outs, sems):
        ssem, rsem = sems
        x, y, c = _me()
        out = []
        for a in range(n):
            for f in range(1, N_DEV):
                px = 1 - x if f & 4 else x
                py = 1 - y if f & 2 else y
                pc = 1 - c if f & 1 else c
                out.append(pltpu.make_async_remote_copy(
                    src_ref=ins[a].at[4 * px + 2 * py + pc], dst_ref=outs[a].at[f - 1], send_sem=ssem.at[7 * a + f - 1],
                    recv_sem=rsem.at[7 * a + f - 1], device_id=(px, py, pc), device_id_type=MESH))
        return out

    def start(ins, outs, sems):
        for cp in copies(ins, outs, sems):
            cp.start()

    def finish(ins, outs, sems):
        for cp in copies(ins, outs, sems):
            cp.wait()

    return types.SimpleNamespace(
        arrays=arrays, out_shape=[_sds((N_DEV - 1,) + a.shape[1:], a.dtype) for a in arrays],
        scratch=[pltpu.SemaphoreType.DMA((7 * n,)), pltpu.SemaphoreType.DMA((7 * n,))],
        start=start, mid=None, finish=finish)


def _call(body, name, grid, in_specs, out_specs, out_shape, scratch, operands, riders=()):
    n_in, n_out, n_scr = len(operands), len(out_shape), len(scratch)
    nsteps = grid[0] if grid else 1
    sizes = [(len(r.arrays), len(r.out_shape), len(r.scratch)) for r in riders]

    def wrapped(*refs):
        pos = n_in
        r_ins = []
        for ri, _, _ in sizes:
            r_ins.append(refs[pos:pos + ri])
            pos += ri
        outs = refs[pos:pos + n_out]
        pos += n_out
        r_outs = []
        for _, ro, _ in sizes:
            r_outs.append(refs[pos:pos + ro])
            pos += ro
        scr = refs[pos:pos + n_scr]
        pos += n_scr
        r_sems = []
        for _, _, rs in sizes:
            r_sems.append(refs[pos:pos + rs])
            pos += rs
        step = pl.program_id(0) if grid else 0

        def at(s, fn):
            if grid:
                pl.when(step == s)(fn)
            else:
                fn()

        for r, a, b, c in zip(riders, r_ins, r_outs, r_sems):
            at(0, lambda r=r, a=a, b=b, c=c: r.start(a, b, c))
        for r, a, b, c in zip(riders, r_ins, r_outs, r_sems):
            if r.mid is not None:
                at((7 * nsteps) // 8, lambda r=r, a=a, b=b, c=c: r.mid(a, b, c))
        if body is not None:
            body(*refs[:n_in], *outs, *scr)
        for r, a, b, c in zip(riders, r_ins, r_outs, r_sems):
            at(nsteps - 1, lambda r=r, a=a, b=b, c=c: r.finish(a, b, c))

    r_arrays = [a for r in riders for a in r.arrays]
    r_shapes = [s for r in riders for s in r.out_shape]
    kwargs = {}
    if grid:
        kwargs = dict(grid=grid, compiler_params=pltpu.CompilerParams(
            dimension_semantics=("arbitrary",) * len(grid), vmem_limit_bytes=VMEM_LIMIT))
    res = pl.pallas_call(
        wrapped, name=name,
        in_specs=list(in_specs) + [ANY] * len(r_arrays),
        out_specs=list(out_specs) + [ANY] * len(r_shapes),
        out_shape=list(out_shape) + r_shapes,
        scratch_shapes=list(scratch) + [s for r in riders for s in r.scratch],
        **kwargs,
    )(*operands, *r_arrays)
    host, rest = res[:n_out], res[n_out:]
    r_res = []
    for _, ro, _ in sizes:
        r_res.append(rest[:ro])
        rest = rest[ro:]
    return host, r_res


def _ln_in(h, g1, w_in_t, keep_hn, name):
    t = h.shape[0]
    tm = _tile(t, STREAM_TILE)

    def body(h_ref, g_ref, w_ref, z_ref, *hn_ref):
        y, _, _ = _rms_fwd(h_ref[...], g_ref[...])
        hn = y.astype(MX)
        if keep_hn:
            hn_ref[0][...] = hn
        z_ref[...] = _dot_nt(hn, w_ref[...])

    tile = lambda w: pl.BlockSpec((tm, w), lambda i: (i, 0))
    outs, _ = _call(
        body, name, (t // tm,),
        [tile(D_MODEL), _const_spec((1, D_MODEL)), _const_spec((IN_W, D_MODEL))],
        [tile(IN_W)] + [tile(D_MODEL)] * keep_hn,
        [_sds((t, IN_W), F32)] + [_sds((t, D_MODEL), MX)] * keep_hn, [], [h, g1, w_in_t])
    return outs[0], (outs[1] if keep_hn else None)


def _band2(kb, g):
    lo = lax.broadcasted_iota(jnp.int32, kb.shape, 1) < HEAD_DIM
    kr = pltpu.roll(kb, HEAD_DIM, 1)
    if g == 0:
        top, bot = jnp.where(lo, kb, 0.0), jnp.where(lo, 0.0, kr)
    else:
        top, bot = jnp.where(lo, kr, 0.0), jnp.where(lo, 0.0, kb)
    return jnp.concatenate([top, bot], axis=0)


def _attn_operands(z_ref, zh_ref, b):
    rows = slice(b * BLK, (b + 1) * BLK)
    prev = zh_ref if b == 0 else z_ref
    prow = slice(0, BLK) if b == 0 else slice((b - 1) * BLK, b * BLK)
    kb = jnp.concatenate([prev[prow, K0:K0 + KV_W], z_ref[rows, K0:K0 + KV_W]], axis=0)
    vb = jnp.concatenate([prev[prow, V0:V0 + KV_W], z_ref[rows, V0:V0 + KV_W]], axis=0)
    k2 = [_band2(kb, g) for g in range(2)]
    v2 = [_band2(vb, g) for g in range(2)]
    q2 = [jnp.concatenate([z_ref[rows, (2 * g) * BLK:(2 * g + 1) * BLK], z_ref[rows, (2 * g + 1) * BLK:(2 * g + 2) * BLK]],
                          axis=0) for g in range(2)]
    return q2, k2, v2


def _attn_block(z_ref, zh_ref, sink_ref, b, first):
    q2, k2, v2 = _attn_operands(z_ref, zh_ref, b)
    rr = lax.broadcasted_iota(jnp.int32, (4 * BLK, 2 * BLK), 0) & (BLK - 1)
    cc = lax.broadcasted_iota(jnp.int32, (4 * BLK, 2 * BLK), 1)
    first_block = jnp.logical_and(first, b == 0).astype(jnp.int32)
    mask = jnp.logical_and(jnp.logical_and(cc > rr, cc <= rr + BLK), cc >= BLK * first_block)
    s = jnp.concatenate([_dot_nt(q2[g], k2[g]) for g in range(2)], axis=0) * SCALE
    w = 2 * BLK
    out, psink = [], []
    for hh in range(2):
        sh = jnp.where(mask, s[:, hh * w:(hh + 1) * w], MASK_VALUE)
        sk = jnp.concatenate([jnp.broadcast_to(sink_ref[p:p + 1, hh * w:hh * w + 1], (BLK, 1)) for p in range(4)], axis=0)
        m = jnp.maximum(jnp.max(sh, axis=1, keepdims=True), sk)
        p = jnp.exp(sh - m)
        es = jnp.exp(sk - m)
        inv = 1.0 / (jnp.sum(p, axis=1, keepdims=True) + es)
        out.append(p * inv)
        psink.append(es * inv)
    return v2, jnp.concatenate(out, axis=1), psink


def _scan_steps(a, b, n, span, reverse):
    pos = lax.broadcasted_iota(jnp.int32, a.shape, 0) & (span - 1)
    d = 1
    while d < span:
        keep = pos < span - d if reverse else pos >= d
        shift = n - d if reverse else d
        a_sh = jnp.where(keep, pltpu.roll(a, shift, 0), 1.0)
        b_sh = jnp.where(keep, pltpu.roll(b, shift, 0), 0.0)
        b = a * b_sh + b
        a = a * a_sh
        d *= 2
    return a, b


def _scan(a, b, tm, reverse):
    return _scan_steps(a, b, tm, tm, reverse)


def _shifted_copies(ext, shifts, tm):
    rows = tm + CONV_HALO - 8
    for r in range(1, 8):
        shifts[r - 1, 0:rows, :] = ext[pl.ds(r, rows), :]


def _tap(ext, shifts, off, r0, n):
    a, r = divmod(off, 8)
    lo = 8 * a + r0
    if r == 0:
        return ext[lo:lo + n, :]
    return shifts[r - 1, lo:lo + n, :]


def _glu_fill(z_ref, zh_ref, uext, ush, first, tm, sg_out=None):
    cv = z_ref[:, CV0:CV0 + CONV_W]
    sg = _sigmoid(z_ref[:, CG0:CG0 + CONV_W])
    if sg_out is not None:
        sg_out[...] = sg
    hrow = BLK - CONV_HALO
    uh = zh_ref[hrow:BLK, CV0:CV0 + CONV_W] * _sigmoid(zh_ref[hrow:BLK, CG0:CG0 + CONV_W])
    uext[0:CONV_HALO, :] = jnp.where(first, 0.0, uh)
    uext[CONV_HALO:CONV_HALO + tm, :] = cv * sg
    _shifted_copies(uext, ush, tm)


def _conv_taps(cw_ref, pv_ref, uext, ush, out_ref, tm):
    for r0 in range(0, tm, CONV_CHUNK):
        acc = jnp.broadcast_to(pv_ref[R_CONV_B:R_CONV_B + 1, :], (CONV_CHUNK, CONV_W))
        for k in range(CONV_K):
            acc = acc + cw_ref[k:k + 1, :] * _tap(uext, ush, CONV_HALO - (CONV_K - 1) + k, r0, CONV_CHUNK)
        out_ref[r0:r0 + CONV_CHUNK, :] = acc


def _ln_silu(uc, pv_ref):
    mu = jnp.mean(uc, axis=-1, keepdims=True)
    xc = uc - mu
    rs = lax.rsqrt(jnp.mean(xc * xc, axis=-1, keepdims=True) + LN_EPS)
    xh = xc * rs
    ln = xh * pv_ref[R_LN_G:R_LN_G + 1, :] + pv_ref[R_LN_B:R_LN_B + 1, :]
    sg = _sigmoid(ln)
    return xh, rs, ln, sg


def _lru_gates(z_ref, zh_ref, pv_ref, wa_ref, wx_ref, rxext, first, tm):
    rxext[0:LRU_HALO, :] = jnp.where(first, 0.0, zh_ref[BLK - LRU_HALO:BLK, RX0:RX0 + LRU_W])
    rxext[LRU_HALO:LRU_HALO + tm, :] = z_ref[:, RX0:RX0 + LRU_W]
    xc = jnp.broadcast_to(pv_ref[R_LCONV_B:R_LCONV_B + 1, :], (tm, LRU_W))
    for k in range(LRU_K):
        xc = xc + pv_ref[R_LCW + k:R_LCW + k + 1, :] * rxext[pl.ds(LRU_HALO - (LRU_K - 1) + k, tm), :]
    r = _sigmoid(_dot(xc, wa_ref[...]) + pv_ref[R_BA:R_BA + 1, :])
    ig = _sigmoid(_dot(xc, wx_ref[...]) + pv_ref[R_BX:R_BX + 1, :])
    lam = pv_ref[R_LAM:R_LAM + 1, :]
    sp = jnp.log1p(jnp.exp(-lam))
    la = (-LRU_C * r) * sp
    a = jnp.exp(la)
    mult = jnp.sqrt(_neg_expm1(2.0 * la))
    return xc, r, ig, sp, la, a, mult


def _mixer_in_specs(tm, tile_of):
    hb = tm // BLK
    return [
        pl.BlockSpec((tm, IN_W), lambda i: (tile_of(i), 0)),
        pl.BlockSpec((BLK, IN_W), lambda i: (jnp.maximum(tile_of(i) * hb - 1, 0), 0)),
        _const_spec((8, 4 * BLK)),
        _const_spec((32, CONV_W)),
        _const_spec((16, CONV_W)),
        _const_spec((LRU_W, LRU_W)),
        _const_spec((LRU_W, LRU_W)),
    ]


def _mixer_fwd(z, sink, cw, pv, wa, wx, name, riders=()):
    t = z.shape[0]
    tm = _tile(t)
    nb = tm // BLK

    def body(z_ref, zh_ref, sink_ref, cw_ref, pv_ref, wa_ref, wx_ref, y_ref, hl_ref, uc_ref, p_ref, ps_ref,
             uext, ush, rxext, hcar):
        i = pl.program_id(0)
        first = i == 0

        @pl.when(first)
        def _():
            hcar[...] = jnp.zeros_like(hcar)

        lo = lax.broadcasted_iota(jnp.int32, (4 * BLK, BLK), 1) < HEAD_DIM
        for b in range(nb):
            rows = slice(b * BLK, (b + 1) * BLK)
            v2, prob, psink = _attn_block(z_ref, zh_ref, sink_ref, b, first)
            prob = prob.astype(MX)
            p_ref[b] = prob
            ps_ref[b] = jnp.where(lo, psink[0], psink[1])
            for g in range(2):
                o = _dot(prob[2 * g * BLK:(2 * g + 2) * BLK], v2[g])
                y_ref[rows, (2 * g) * BLK:(2 * g + 1) * BLK] = o[0:BLK]
                y_ref[rows, (2 * g + 1) * BLK:(2 * g + 2) * BLK] = o[BLK:2 * BLK]
        _glu_fill(z_ref, zh_ref, uext, ush, first, tm)
        _conv_taps(cw_ref, pv_ref, uext, ush, uc_ref, tm)
        _, _, ln, sg = _ln_silu(uc_ref[...], pv_ref)
        y_ref[:, ATTN_W:ATTN_W + CONV_W] = ln * sg
        xc, _, ig, _, _, a, mult = _lru_gates(z_ref, zh_ref, pv_ref, wa_ref, wx_ref, rxext, first, tm)
        acum, h = _scan(a, mult * (ig * xc), tm, reverse=False)
        h = h + acum * hcar[0:1, :]
        hl_ref[...] = h
        hcar[0:1, :] = h[tm - 1:tm, :]
        gl, _ = _gelu(z_ref[:, RG0:RG0 + LRU_W])
        y_ref[:, ATTN_W + CONV_W:ATTN_W + CONV_W + LRU_W] = h * gl

    tile = lambda w: pl.BlockSpec((tm, w), lambda i: (i, 0))
    return _call(
        body, name, (t // tm,), _mixer_in_specs(tm, lambda i: i),
        [tile(D_MODEL), tile(LRU_W), tile(CONV_W), pl.BlockSpec((nb, 4 * BLK, 4 * BLK), lambda i: (i, 0, 0)),
         pl.BlockSpec((nb, 4 * BLK, BLK), lambda i: (i, 0, 0))],
        [_sds((t, D_MODEL), F32), _sds((t, LRU_W), F32), _sds((t, CONV_W), F32),
         _sds((t // BLK, 4 * BLK, 4 * BLK), MX), _sds((t // BLK, 4 * BLK, BLK), F32)],
        [pltpu.VMEM((tm + CONV_HALO, CONV_W), F32), pltpu.VMEM((7, tm + CONV_HALO - 8, CONV_W), F32),
         pltpu.VMEM((tm + LRU_HALO, LRU_W), F32), pltpu.VMEM((8, LRU_W), F32)],
        [z, z, sink, cw, pv, wa, wx], riders)


def _mixer_bwd(dy, z, ycat, hl, uc, probs, psinks, sink, cw, pv, wa, wx, name, riders=()):
    t = z.shape[0]
    tm = _tile(t)
    nt = t // tm
    nb = tm // BLK
    rev = lambda i: nt - 1 - i

    def body(dy_ref, z_ref, zh_ref, sink_ref, cw_ref, pv_ref, wa_ref, wx_ref, y_ref, hl_ref, hlh_ref, uc_ref,
             p_ref, ps_ref, dz_ref, dsink_ref, dcw_ref, dpv_ref, dwa_ref, dwx_ref,
             uext, ush, sgs, rxext, dkext, dvext, ducext, dsh, dcw8, dxcext, kcar, vcar, uccar, xccar, gcar):
        i = pl.program_id(0)
        first = i == nt - 1

        @pl.when(i == 0)
        def _():
            for car in (kcar, vcar, uccar, xccar, gcar, dcw8):
                car[...] = jnp.zeros_like(car)
            for acc in (dsink_ref, dpv_ref, dwa_ref, dwx_ref):
                acc[...] = jnp.zeros_like(acc)

        def addrow(r, val):
            dpv_ref[r:r + 1, :] += jnp.sum(val, axis=0, keepdims=True)

        dkext[:, 0:tm] = jnp.zeros((KV_W, tm), F32)
        dvext[:, 0:tm] = jnp.zeros((KV_W, tm), F32)
        dkext[:, tm:tm + BLK] = kcar[...]
        dvext[:, tm:tm + BLK] = vcar[...]
        lane512 = lax.broadcasted_iota(jnp.int32, (1, 4 * BLK), 1) < 2 * BLK
        lo = lax.broadcasted_iota(jnp.int32, (4 * BLK, BLK), 1) < HEAD_DIM
        hd, w2 = HEAD_DIM, 2 * BLK
        for b in range(nb):
            rows = slice(b * BLK, (b + 1) * BLK)
            band = slice(b * BLK, (b + 2) * BLK)
            q2, k2, v2 = _attn_operands(z_ref, zh_ref, b)
            prob = p_ref[b]
            psink = [ps_ref[b, :, 0:1], ps_ref[b, :, HEAD_DIM:HEAD_DIM + 1]]
            stack = lambda ref: jnp.concatenate([ref[rows, p * BLK:(p + 1) * BLK] for p in range(4)], axis=0)
            do4 = stack(dy_ref)
            dlt = do4 * stack(y_ref)
            d0 = jnp.sum(jnp.where(lo, dlt, 0.0), axis=1, keepdims=True)
            d1 = jnp.sum(jnp.where(lo, 0.0, dlt), axis=1, keepdims=True)
            dp = jnp.concatenate([_dot_nt(do4[g * w2:(g + 1) * w2], v2[g]) for g in range(2)], axis=0)
            dl = jnp.concatenate([jnp.broadcast_to(d0, (4 * BLK, w2)), jnp.broadcast_to(d1, (4 * BLK, w2))], axis=1)
            draw = (prob * (dp - dl)) * SCALE
            e0, e1 = psink[0] * d0, psink[1] * d1
            for p in range(4):
                prs = slice(p * BLK, (p + 1) * BLK)
                s0 = jnp.sum(e0[prs], axis=0, keepdims=True)
                s1 = jnp.sum(e1[prs], axis=0, keepdims=True)
                dsink_ref[p:p + 1, :] += -jnp.where(lane512, s0, s1)
            for g in range(2):
                grs = slice(g * w2, (g + 1) * w2)
                dq = _dot(draw[grs], k2[g])
                dz_ref[rows, (2 * g) * BLK:(2 * g + 1) * BLK] = dq[0:BLK].astype(dz_ref.dtype)
                dz_ref[rows, (2 * g + 1) * BLK:(2 * g + 2) * BLK] = dq[BLK:2 * BLK].astype(dz_ref.dtype)
                tk = _dot_tn(q2[g], draw[grs])
                tv = _dot_tn(do4[grs], prob[grs])
                dkext[g * hd:(g + 1) * hd, band] += tk[0:hd, 0:w2] + tk[hd:2 * hd, w2:2 * w2]
                dvext[g * hd:(g + 1) * hd, band] += tv[0:hd, 0:w2] + tv[hd:2 * hd, w2:2 * w2]
        dz_ref[:, K0:K0 + KV_W] = jnp.transpose(dkext[:, BLK:BLK + tm]).astype(dz_ref.dtype)
        dz_ref[:, V0:V0 + KV_W] = jnp.transpose(dvext[:, BLK:BLK + tm]).astype(dz_ref.dtype)
        kcar[...] = dkext[:, 0:BLK]
        vcar[...] = dvext[:, 0:BLK]

        _glu_fill(z_ref, zh_ref, uext, ush, first, tm, sg_out=sgs)
        xh, rs, ln, sg = _ln_silu(uc_ref[...], pv_ref)
        dln = dy_ref[:, ATTN_W:ATTN_W + CONV_W] * (sg * (1.0 + ln * (1.0 - sg)))
        addrow(R_LN_G, dln * xh)
        addrow(R_LN_B, dln)
        dxh = dln * pv_ref[R_LN_G:R_LN_G + 1, :]
        duc = rs * (dxh - jnp.mean(dxh, axis=-1, keepdims=True) - xh * jnp.mean(dxh * xh, axis=-1, keepdims=True))
        addrow(R_CONV_B, duc)
        ducext[0:tm, :] = duc
        ducext[tm:tm + CONV_HALO, :] = uccar[...]
        uccar[...] = duc[0:CONV_HALO, :]
        _shifted_copies(ducext, dsh, tm)
        for r0 in range(0, tm, CONV_CHUNK):
            crow = slice(r0, r0 + CONV_CHUNK)
            duc_c = ducext[crow, :]
            du = jnp.zeros((CONV_CHUNK, CONV_W), F32)
            for k in range(CONV_K):
                prod = duc_c * _tap(uext, ush, CONV_HALO - (CONV_K - 1) + k, r0, CONV_CHUNK)
                part = prod[0:8]
                for s in range(8, CONV_CHUNK, 8):
                    part = part + prod[s:s + 8]
                dcw8[k] += part
                du = du + cw_ref[k:k + 1, :] * _tap(ducext, dsh, CONV_K - 1 - k, r0, CONV_CHUNK)
            sgc = sgs[crow, :]
            dz_ref[crow, CV0:CV0 + CONV_W] = (du * sgc).astype(dz_ref.dtype)
            u_c = uext[CONV_HALO + r0:CONV_HALO + r0 + CONV_CHUNK, :]
            dz_ref[crow, CG0:CG0 + CONV_W] = (du * u_c * (1.0 - sgc)).astype(dz_ref.dtype)

        @pl.when(i == nt - 1)
        def _():
            dcw_ref[...] = jnp.sum(dcw8[...], axis=1)

        xc, r, ig, sp, la, a, mult = _lru_gates(z_ref, zh_ref, pv_ref, wa_ref, wx_ref, rxext, first, tm)
        h = hl_ref[...]
        rowi = lax.broadcasted_iota(jnp.int32, (tm, LRU_W), 0)
        hlast = jnp.where(first, 0.0, hlh_ref[7:8, :])
        hprev = jnp.where(rowi == 0, hlast, pltpu.roll(h, 1, 0))
        dyl = dy_ref[:, ATTN_W + CONV_W:ATTN_W + CONV_W + LRU_W]
        gl, dgl = _gelu(z_ref[:, RG0:RG0 + LRU_W])
        dz_ref[:, RG0:RG0 + LRU_W] = (dyl * h * dgl).astype(dz_ref.dtype)
        dh = dyl * gl + jnp.where(rowi == tm - 1, gcar[0:1, :], 0.0)
        c = jnp.where(rowi == tm - 1, 0.0, pltpu.roll(a, tm - 1, 0))
        _, gg = _scan(c, dh, tm, reverse=True)
        gcar[0:1, :] = a[0:1, :] * gg[0:1, :]
        dmult = gg * (ig * xc)
        dig = gg * mult * xc
        dxc = gg * mult * ig
        dla = gg * hprev * a - dmult * a * a / mult
        dr = dla * (-LRU_C * sp)
        lam = pv_ref[R_LAM:R_LAM + 1, :]
        dpv_ref[R_LAM:R_LAM + 1, :] += jnp.sum(dla * (-LRU_C * r), axis=0, keepdims=True) * (-_sigmoid(-lam))
        dpa = dr * r * (1.0 - r)
        dpx = dig * ig * (1.0 - ig)
        addrow(R_BA, dpa)
        addrow(R_BX, dpx)
        dxc = dxc + _dot_nt(dpa, wa_ref[...]) + _dot_nt(dpx, wx_ref[...])
        dwa_ref[...] += _dot_tn(xc, dpa)
        dwx_ref[...] += _dot_tn(xc, dpx)
        addrow(R_LCONV_B, dxc)
        dxcext[0:tm, :] = dxc
        dxcext[tm:tm + LRU_HALO, :] = xccar[...]
        xccar[...] = dxc[0:LRU_HALO, :]
        drx = jnp.zeros((tm, LRU_W), F32)
        for k in range(LRU_K):
            addrow(R_LCW + k, dxc * rxext[pl.ds(LRU_HALO - (LRU_K - 1) + k, tm), :])
            drx = drx + pv_ref[R_LCW + k:R_LCW + k + 1, :] * dxcext[pl.ds(LRU_K - 1 - k, tm), :]
        dz_ref[:, RX0:RX0 + LRU_W] = drx.astype(dz_ref.dtype)

    tile = lambda w: pl.BlockSpec((tm, w), lambda i: (rev(i), 0))
    in_specs = [tile(D_MODEL)] + _mixer_in_specs(tm, rev) + [
        tile(D_MODEL), tile(LRU_W),
        pl.BlockSpec((8, LRU_W), lambda i: (jnp.maximum(rev(i) * (tm // 8) - 1, 0), 0)),
        tile(CONV_W), pl.BlockSpec((nb, 4 * BLK, 4 * BLK), lambda i: (rev(i), 0, 0)),
        pl.BlockSpec((nb, 4 * BLK, BLK), lambda i: (rev(i), 0, 0))]
    return _call(
        body, name, (nt,), in_specs,
        [tile(IN_W), _acc_spec((8, 4 * BLK)), _acc_spec((32, CONV_W)), _acc_spec((16, CONV_W)),
         _acc_spec((LRU_W, LRU_W)), _acc_spec((LRU_W, LRU_W))],
        [_sds((t, IN_W), MX), _sds((8, 4 * BLK), F32), _sds((32, CONV_W), F32), _sds((16, CONV_W), F32),
         _sds((LRU_W, LRU_W), F32), _sds((LRU_W, LRU_W), F32)],
        [pltpu.VMEM((tm + CONV_HALO, CONV_W), F32), pltpu.VMEM((7, tm + CONV_HALO - 8, CONV_W), F32),
         pltpu.VMEM((tm, CONV_W), F32), pltpu.VMEM((tm + LRU_HALO, LRU_W), F32),
         pltpu.VMEM((KV_W, tm + BLK), F32), pltpu.VMEM((KV_W, tm + BLK), F32),
         pltpu.VMEM((tm + CONV_HALO, CONV_W), F32), pltpu.VMEM((7, tm + CONV_HALO - 8, CONV_W), F32),
         pltpu.VMEM((32, 8, CONV_W), F32), pltpu.VMEM((tm + LRU_HALO, LRU_W), F32),
         pltpu.VMEM((KV_W, BLK), F32), pltpu.VMEM((KV_W, BLK), F32),
         pltpu.VMEM((CONV_HALO, CONV_W), F32), pltpu.VMEM((LRU_HALO, LRU_W), F32), pltpu.VMEM((8, LRU_W), F32)],
        [dy, z, z, sink, cw, pv, wa, wx, ycat, hl, hl, uc, probs, psinks], riders)


def _post_fwd(ycat, h0, gmix, w_out, g2, w_up, w_down, name, riders=()):
    t = h0.shape[0]
    tm = _tile(t, POST_TILE)
    nj = D_FF // FF_BLK

    def body(y_ref, h_ref, gm_ref, wo_ref, g2_ref, wu_ref, wd_ref, h1_ref, a_ref, h2_ref, ym_ref, hn_ref):
        ym, _, _ = _group_rms_fwd(y_ref[...], gm_ref[...])
        ym = ym.astype(MX)
        ym_ref[...] = ym
        h1 = h_ref[...] + jnp.dot(ym, wo_ref[...], preferred_element_type=F32)
        h1_ref[...] = h1
        hn, _, _ = _rms_fwd(h1, g2_ref[...])
        hn = hn.astype(MX)
        hn_ref[...] = hn
        for j in range(nj):
            u = jnp.dot(hn, wu_ref[j], preferred_element_type=F32)
            a_ref[:, j * FF_BLK:(j + 1) * FF_BLK] = jnp.square(jnp.maximum(u, 0.0)).astype(MX)
        h2_ref[...] = h1 + jnp.dot(a_ref[...], wd_ref[...], preferred_element_type=F32)

    tile = lambda w: pl.BlockSpec((tm, w), lambda i: (i, 0))
    return _call(
        body, name, (t // tm,),
        [tile(D_MODEL), tile(D_MODEL), _const_spec((1, D_MODEL)), _const_spec((D_MODEL, D_MODEL)),
         _const_spec((1, D_MODEL)), _const_spec((nj, D_MODEL, FF_BLK)), _const_spec((D_FF, D_MODEL))],
        [tile(D_MODEL), tile(D_FF), tile(D_MODEL), tile(D_MODEL), tile(D_MODEL)],
        [_sds((t, D_MODEL), F32), _sds((t, D_FF), MX), _sds((t, D_MODEL), F32), _sds((t, D_MODEL), MX),
         _sds((t, D_MODEL), MX)],
        [], [ycat, h0, gmix, w_out, g2, w_up, w_down], riders)


def _ffn_bwd(dh2, act, h1, g2, w_up_t, w_down, name, riders=()):
    t = h1.shape[0]
    tm = _tile(t, POST_TILE)
    nj = D_FF // FF_BLK

    def body(dh2_ref, a_ref, h1_ref, g2_ref, wut_ref, wd_ref, dh1_ref, dh1b_ref, dh2b_ref, du_ref, dg2_ref):
        @pl.when(pl.program_id(0) == 0)
        def _():
            dg2_ref[...] = jnp.zeros_like(dg2_ref)

        dh2 = dh2_ref[...]
        dh2b = dh2.astype(MX)
        dh2b_ref[...] = dh2b
        for j in range(nj):
            cols = slice(j * FF_BLK, (j + 1) * FF_BLK)
            da = _dot_nt(dh2b, wd_ref[j])
            du_ref[:, cols] = (da * (2.0 * jnp.sqrt(a_ref[:, cols].astype(F32)))).astype(MX)
        dhn = jnp.dot(du_ref[...], wut_ref[...], preferred_element_type=F32)
        _, xh, r = _rms_fwd(h1_ref[...], g2_ref[...])
        dx, dg = _rms_bwd(dhn, xh, r, g2_ref[...])
        dg2_ref[...] += dg
        dh1 = dh2 + dx
        dh1_ref[...] = dh1
        dh1b_ref[...] = dh1.astype(MX)

    tile = lambda w: pl.BlockSpec((tm, w), lambda i: (i, 0))
    return _call(
        body, name, (t // tm,),
        [tile(D_MODEL), tile(D_FF), tile(D_MODEL), _const_spec((1, D_MODEL)),
         _const_spec((D_FF, D_MODEL)), _const_spec((nj, FF_BLK, D_MODEL))],
        [tile(D_MODEL), tile(D_MODEL), tile(D_MODEL), tile(D_FF), _acc_spec((1, D_MODEL))],
        [_sds((t, D_MODEL), F32), _sds((t, D_MODEL), MX), _sds((t, D_MODEL), MX), _sds((t, D_FF), MX),
         _sds((1, D_MODEL), F32)],
        [], [dh2, act, h1, g2, w_up_t, w_down], riders)


def _mix_bwd(dh1, ycat, ym, gmix, w_out, name):
    t = dh1.shape[0]
    tm = _tile(t)
    nk = t // tm
    r = D_MODEL // N_DEV

    def body(dh1_ref, y_ref, ym_ref, gm_ref, wo_ref, dy_ref, dgm_ref, o_ref, o16_ref, acc):
        k = pl.program_id(0)

        @pl.when(k == 0)
        def _():
            dgm_ref[...] = jnp.zeros_like(dgm_ref)
            acc[...] = jnp.zeros_like(acc)

        dh = dh1_ref[...]
        acc[...] += _dot_tn(ym_ref[...], dh)
        dym = _dot_nt(dh, wo_ref[...])
        gm = gm_ref[...]
        _, yh, rr = _group_rms_fwd(y_ref[...], gm)
        outs, dgs = [], []
        for (a, b), rg in zip(_GROUPS, rr):
            dxg, dgg = _rms_bwd(dym[:, a:b], yh[:, a:b], rg, gm[:, a:b])
            outs.append(dxg)
            dgs.append(dgg)
        dy_ref[...] = jnp.concatenate(outs, axis=1)
        dgm_ref[...] += jnp.concatenate(dgs, axis=1)

        @pl.when(k == nk - 1)
        def _():
            for d in range(N_DEV):
                v = acc[d * r:(d + 1) * r, :]
                o_ref[d] = v
                o16_ref[d] = v.astype(o16_ref.dtype)

    tile = pl.BlockSpec((tm, D_MODEL), lambda i: (i, 0))
    slabs = _const_spec((N_DEV, r, D_MODEL))
    (dy, dgm, dw, dw16), _ = _call(
        body, name, (nk,), [tile, tile, tile, _const_spec((1, D_MODEL)), _const_spec((D_MODEL, D_MODEL))],
        [tile, _acc_spec((1, D_MODEL)), slabs, slabs],
        [_sds((t, D_MODEL), F32), _sds((1, D_MODEL), F32), _sds((N_DEV, r, D_MODEL), F32),
         _sds((N_DEV, r, D_MODEL), WIRE)],
        [pltpu.VMEM((D_MODEL, D_MODEL), F32)], [dh1, ycat, ym, gmix, w_out])
    return dy, dgm, (dw, dw16)


def _in_bwd(dz, h0, dh1, g1, w_in_t, after, name):
    t = h0.shape[0]
    tm = _tile(t, STREAM_TILE)

    def body(dz_ref, h_ref, dh1_ref, g_ref, w_ref, after_ref, dh0_ref, dg_ref):
        @pl.when(pl.program_id(0) == 0)
        def _():
            dg_ref[...] = jnp.zeros_like(dg_ref)

        dhn = _dot(dz_ref[...], w_ref[...])
        _, xh, r = _rms_fwd(h_ref[...], g_ref[...])
        dx, dg = _rms_bwd(dhn, xh, r, g_ref[...])
        dg_ref[...] += dg
        dh0_ref[...] = dh1_ref[...] + dx

    tile = lambda w: pl.BlockSpec((tm, w), lambda i: (i, 0))
    (dh0, dg), _ = _call(
        body, name, (t // tm,),
        [tile(IN_W), tile(D_MODEL), tile(D_MODEL), _const_spec((1, D_MODEL)), _const_spec((IN_W, D_MODEL)),
         _const_spec((8, 128))],
        [tile(D_MODEL), _acc_spec((1, D_MODEL))], [_sds((t, D_MODEL), F32), _sds((1, D_MODEL), F32)],
        [], [dz, h0, dh1, g1, w_in_t, after])
    return dh0, dg


def _in_bwd_dw(dz, h0, dh1, g1, w_in_t, name):
    t = h0.shape[0]
    tm = _tile(t)
    nk = t // tm

    def body(dz_ref, h_ref, dh1_ref, g_ref, w_ref, dh0_ref, dg_ref, o_ref, o16_ref, acc):
        k = pl.program_id(0)

        @pl.when(k == 0)
        def _():
            dg_ref[...] = jnp.zeros_like(dg_ref)
            acc[...] = jnp.zeros_like(acc)

        dz_t = dz_ref[...]
        hn, xh, r = _rms_fwd(h_ref[...], g_ref[...])
        acc[...] += _dot_tn(dz_t, hn)
        dhn = _dot(dz_t, w_ref[...])
        dx, dg = _rms_bwd(dhn, xh, r, g_ref[...])
        dg_ref[...] += dg
        dh0_ref[...] = dh1_ref[...] + dx

        @pl.when(k == nk - 1)
        def _():
            for d in range(N_DEV):
                v = acc[d * IN_SHARD:(d + 1) * IN_SHARD, :]
                o_ref[d] = v
                o16_ref[d] = v.astype(o16_ref.dtype)

    tile = lambda w: pl.BlockSpec((tm, w), lambda i: (i, 0))
    slabs = _const_spec((N_DEV, IN_SHARD, D_MODEL))
    (dh0, dg, dw, dw16), _ = _call(
        body, name, (nk,),
        [tile(IN_W), tile(D_MODEL), tile(D_MODEL), _const_spec((1, D_MODEL)), _const_spec((IN_W, D_MODEL))],
        [tile(D_MODEL), _acc_spec((1, D_MODEL)), slabs, slabs],
        [_sds((t, D_MODEL), F32), _sds((1, D_MODEL), F32), _sds((N_DEV, IN_SHARD, D_MODEL), F32),
         _sds((N_DEV, IN_SHARD, D_MODEL), WIRE)],
        [pltpu.VMEM((IN_W, D_MODEL), F32)], [dz, h0, dh1, g1, w_in_t])
    return dh0, dg, (dw, dw16)


def _loss_head(h, gf, target, name):
    t = h.shape[0]
    tm = _tile(t, STREAM_TILE)

    def body(h_ref, g_ref, t_ref, dh_ref, loss_ref, dg_ref):
        @pl.when(pl.program_id(0) == 0)
        def _():
            loss_ref[...] = jnp.zeros_like(loss_ref)
            dg_ref[...] = jnp.zeros_like(dg_ref)

        g = g_ref[...]
        y, xh, r = _rms_fwd(h_ref[...], g)
        err = y - t_ref[...]
        part = 0.5 * jnp.sum(jnp.mean(err * err, axis=-1, keepdims=True), axis=0, keepdims=True)
        loss_ref[...] += jnp.broadcast_to(part, loss_ref.shape)
        dx, dg = _rms_bwd(err * (1.0 / D_MODEL), xh, r, g)
        dg_ref[...] += dg
        dh_ref[...] = dx

    tile = pl.BlockSpec((tm, D_MODEL), lambda i: (i, 0))
    (dh, loss, dg), _ = _call(
        body, name, (t // tm,), [tile, _const_spec((1, D_MODEL)), tile],
        [tile, _acc_spec((1, 128)), _acc_spec((1, D_MODEL))],
        [_sds((t, D_MODEL), F32), _sds((1, 128), F32), _sds((1, D_MODEL), F32)], [], [h, gf, target])
    return dh, loss, dg


def _dw(x, y, name, split, bm, bn):
    t, m = x.shape
    n = y.shape[1]
    tk = _tile(t, DW_TILE)
    nk = t // tk
    if split == "rows":
        assert bn == n
        r, c = m // N_DEV, n
        per = bm // r
        out_block = pl.BlockSpec((per, r, c), lambda a, b, k: (a, 0, 0))
    else:
        assert bm == m
        r, c = m, n // N_DEV
        per = bn // c
        out_block = pl.BlockSpec((per, r, c), lambda a, b, k: (b, 0, 0))

    def body(x_ref, y_ref, o_ref, o16_ref, acc):
        k = pl.program_id(2)

        @pl.when(k == 0)
        def _():
            acc[...] = jnp.zeros_like(acc)

        acc[...] += _dot_tn(x_ref[...], y_ref[...])

        @pl.when(k == nk - 1)
        def _():
            for d in range(per):
                v = acc[d * r:(d + 1) * r, :] if split == "rows" else acc[:, d * c:(d + 1) * c]
                o_ref[d] = v
                o16_ref[d] = v.astype(o16_ref.dtype)

    return pl.pallas_call(
        body, name=name, grid=(m // bm, n // bn, nk),
        in_specs=[pl.BlockSpec((tk, bm), lambda a, b, k: (k, a)), pl.BlockSpec((tk, bn), lambda a, b, k: (k, b))],
        out_specs=[out_block, out_block],
        out_shape=[_sds((N_DEV, r, c), F32), _sds((N_DEV, r, c), WIRE)],
        scratch_shapes=[pltpu.VMEM((bm, bn), F32)],
        compiler_params=pltpu.CompilerParams(dimension_semantics=("arbitrary",) * 3, vmem_limit_bytes=VMEM_LIMIT),
    )(x, y)


def _adamw_math(w, g, m, v):
    m = ADAM_B1 * m + (1.0 - ADAM_B1) * g
    v = ADAM_B2 * v + (1.0 - ADAM_B2) * jnp.square(g)
    m_hat = m / (1.0 - ADAM_B1 ** ADAM_STEP)
    v_hat = v / (1.0 - ADAM_B2 ** ADAM_STEP)
    delta = -ADAM_LR * (m_hat / (jnp.sqrt(v_hat) + ADAM_EPS) + ADAM_WD * w)
    return delta, m, v


def _adamw_shard(g_own, g_recv, dev, w, m, v, after, name):
    _, r, c = w.shape
    br = r
    for cand in (256, 128, 112, 64, 56, 32, 16, 8):
        if r % cand == 0:
            br = cand
            break
    nr = r // br
    own = lambda l: pl.BlockSpec((1, br, c), lambda ll, i, d: (d[0], jnp.where(ll == l, i, (nr - 1) * (1 - l)), 0))
    recv = lambda l: pl.BlockSpec((N_DEV - 1, br, c), lambda ll, i, d: (0, jnp.where(ll == l, i, (nr - 1) * (1 - l)), 0))

    def body(dev_ref, go0, gr0, go1, gr1, w_ref, m_ref, v_ref, after_ref, g_out, d_out, m_out, v_out):
        def update(go_ref, gr_ref):
            g = go_ref[0]
            for j in range(N_DEV - 1):
                g = g + gr_ref[j].astype(F32)
            delta, mn, vn = _adamw_math(w_ref[0], g, m_ref[0], v_ref[0])
            g_out[0] = g
            d_out[0] = delta
            m_out[0] = mn
            v_out[0] = vn

        layer = pl.program_id(0)
        pl.when(layer == 0)(lambda: update(go0, gr0))
        pl.when(layer == 1)(lambda: update(go1, gr1))

    tile = pl.BlockSpec((1, br, c), lambda ll, i, d: (ll, i, 0))
    return pl.pallas_call(
        body, name=name,
        grid_spec=pltpu.PrefetchScalarGridSpec(
            num_scalar_prefetch=1, grid=(2, nr),
            in_specs=[own(0), recv(0), own(1), recv(1), tile, tile, tile,
                      pl.BlockSpec((8, 128), lambda ll, i, d: (0, 0))],
            out_specs=[tile, tile, tile, tile]),
        out_shape=[_sds((2, r, c), F32)] * 4,
        compiler_params=pltpu.CompilerParams(dimension_semantics=("arbitrary",) * 2, vmem_limit_bytes=VMEM_LIMIT),
    )(dev, g_own[0], g_recv[0], g_own[1], g_recv[1], w, m, v, after)


def _adamw_small(gs, ws, ms, vs, name):
    n = len(gs)

    def body(*refs):
        g_refs, w_refs, m_refs, v_refs = (refs[k * n:(k + 1) * n] for k in range(4))
        outs = refs[4 * n:]
        for k in range(n):
            delta, mn, vn = _adamw_math(w_refs[k][...], g_refs[k][...], m_refs[k][...], v_refs[k][...])
            outs[k][...] = delta
            outs[n + k][...] = mn
            outs[2 * n + k][...] = vn

    shapes = [_sds(w.shape, F32) for w in ws]
    res = pl.pallas_call(body, name=name, out_shape=shapes * 3,
                         compiler_params=pltpu.CompilerParams(vmem_limit_bytes=VMEM_LIMIT))(*gs, *ws, *ms, *vs)
    return res[:n], res[n:2 * n], res[2 * n:]


def _sum_parts(own, recv, dev, name):
    def body(dev_ref, own_ref, recv_ref, o_ref):
        me = dev_ref[0]

        def block(d):
            f = jnp.bitwise_xor(me, d)
            return jnp.where(f == 0, own_ref[...], recv_ref[jnp.maximum(f - 1, 0)])

        g = block(0)
        for d in range(1, N_DEV):
            g = g + block(d)
        o_ref[...] = g

    return pl.pallas_call(
        body, name=name,
        grid_spec=pltpu.PrefetchScalarGridSpec(
            num_scalar_prefetch=1, grid=(1,),
            in_specs=[pl.BlockSpec(own.shape, lambda i, d: (0, 0)), pl.BlockSpec(recv.shape, lambda i, d: (0, 0, 0))],
            out_specs=pl.BlockSpec(own.shape, lambda i, d: (0, 0))),
        out_shape=_sds(own.shape, F32))(dev, own, recv)


HBM = pl.BlockSpec(memory_space=pltpu.HBM)
SEM = pl.BlockSpec(memory_space=pltpu.SEMAPHORE)
EFFECT = pltpu.SideEffectType.DATAFLOW_SIDE_EFFECTING


def _direct_copies(srcs, lands, ssem, rsem, scatter):
    x, y, c = _me()
    out = []
    for a in range(len(srcs)):
        for f in range(1, N_DEV):
            px = 1 - x if f & 4 else x
            py = 1 - y if f & 2 else y
            pc = 1 - c if f & 1 else c
            out.append(pltpu.make_async_remote_copy(
                src_ref=srcs[a].at[4 * px + 2 * py + pc] if scatter else srcs[a], dst_ref=lands[a].at[f - 1],
                send_sem=ssem.at[7 * a + f - 1], recv_sem=rsem.at[7 * a + f - 1],
                device_id=(px, py, pc), device_id_type=MESH))
    return out


def _send_start(arrays, scatter, name):
    arrays = list(arrays)
    n = len(arrays)
    lands = [lax.empty((N_DEV - 1,) + (a.shape[1:] if scatter else a.shape), a.dtype) for a in arrays]

    def body(*refs):
        srcs, lnds, ssem, rsem, token = refs[:n], refs[n:2 * n], refs[2 * n], refs[2 * n + 1], refs[-1]
        for cp in _direct_copies(srcs, lnds, ssem, rsem, scatter):
            cp.start()
        token[...] = jnp.zeros_like(token)

    hbm = lambda a: pltpu.HBM(a.shape, a.dtype)
    res = pl.pallas_call(
        body, name=name,
        out_shape=(pltpu.SemaphoreType.DMA((7 * n,)), pltpu.SemaphoreType.DMA((7 * n,)),
                   *[hbm(a) for a in arrays + lands], _sds((8, 128), F32)),
        in_specs=[HBM] * (2 * n),
        out_specs=(SEM, SEM, *[HBM] * (2 * n), pl.BlockSpec(memory_space=pltpu.VMEM)),
        input_output_aliases={i: 2 + i for i in range(2 * n)},
        compiler_params=pltpu.CompilerParams(has_side_effects=EFFECT),
    )(*[pltpu.with_memory_space_constraint(a, pltpu.HBM) for a in arrays + lands])
    return types.SimpleNamespace(ssem=res[0], rsem=res[1], srcs=list(res[2:2 + n]), lands=list(res[2 + n:2 + 2 * n]),
                                 token=res[-1], scatter=scatter)


def _send_wait(h, after, name):
    n = len(h.srcs)

    def body(*refs):
        srcs, lnds, ssem, rsem = refs[:n], refs[n:2 * n], refs[2 * n], refs[2 * n + 1]
        for cp in _direct_copies(srcs, lnds, ssem, rsem, h.scatter):
            cp.wait_send()
            cp.wait_recv()

    hbm = lambda a: pltpu.HBM(a.shape, a.dtype)
    res = pl.pallas_call(
        body, name=name,
        out_shape=tuple(hbm(a) for a in h.srcs + h.lands),
        in_specs=[HBM] * (2 * n) + [SEM, SEM, ANY], out_specs=[HBM] * (2 * n),
        input_output_aliases={i: i for i in range(2 * n)},
        compiler_params=pltpu.CompilerParams(has_side_effects=EFFECT),
    )(*h.srcs, *h.lands, h.ssem, h.rsem, after)
    return list(res[:n]), list(res[n:])


def _block_diag(w):
    out = jnp.zeros((LRU_W, LRU_W), w.dtype)
    for h in range(4):
        out = lax.dynamic_update_slice(out, w[h], (h * 64, h * 64))
    return out


def _layer_params(p, l):
    row = lambda a: a[l].reshape(1, -1)
    sink_rows = jnp.repeat(p["attn_sinks"][l].reshape(4, 2), 2 * BLK, axis=1)
    sink_rows = jnp.concatenate([sink_rows, jnp.zeros((4, 4 * BLK), F32)], axis=0)
    cw = jnp.concatenate([p["conv_dw_w"][l], jnp.zeros((1, CONV_W), F32)], axis=0)
    pv = jnp.concatenate([
        row(p["conv_dw_b"]), row(p["conv_ln_g"]), row(p["conv_ln_b"]), row(p["lru_conv_b"]), row(p["lru_ba"]),
        row(p["lru_bx"]), row(p["lru_lambda"]), jnp.zeros((1, LRU_W), F32), p["lru_conv_w"][l],
        jnp.zeros((4, LRU_W), F32)], axis=0)
    return dict(
        g1=row(p["norm1"]), sink=sink_rows, cw=cw, pv=pv,
        wa=_block_diag(p["lru_wa"][l]).astype(MX), wx=_block_diag(p["lru_wx"][l]).astype(MX),
        gmix=row(p["mix_norm"]), g2=row(p["norm2"]))


_SMALL = ["norm1", "attn_sinks", "conv_dw_w", "conv_dw_b", "conv_ln_g", "conv_ln_b", "lru_conv_w", "lru_conv_b",
          "lru_wa", "lru_ba", "lru_wx", "lru_bx", "lru_lambda", "mix_norm", "norm2"]
_BIG = ["w_in", "w_out", "w_up", "w_down"]
_WEIGHTS = ["norm1", "w_in", "attn_sinks", "conv_dw_w", "conv_dw_b", "conv_ln_g", "conv_ln_b", "lru_conv_w",
            "lru_conv_b", "lru_wa", "lru_ba", "lru_wx", "lru_bx", "lru_lambda", "mix_norm", "w_out", "norm2", "w_up",
            "w_down", "final_norm"]


def kernel(x, norm1, w_in, attn_sinks, conv_dw_w, conv_dw_b, conv_ln_g, conv_ln_b, lru_conv_w, lru_conv_b, lru_wa, lru_ba, lru_wx, lru_bx, lru_lambda, mix_norm, w_out, norm2, w_up, w_down, final_norm, loss_target, m_norm1, m_w_in, m_attn_sinks, m_conv_dw_w, m_conv_dw_b, m_conv_ln_g, m_conv_ln_b, m_lru_conv_w, m_lru_conv_b, m_lru_wa, m_lru_ba, m_lru_wx, m_lru_bx, m_lru_lambda, m_mix_norm, m_w_out, m_norm2, m_w_up, m_w_down, m_final_norm, v_norm1, v_w_in, v_attn_sinks, v_conv_dw_w, v_conv_dw_b, v_conv_ln_g, v_conv_ln_b, v_lru_conv_w, v_lru_conv_b, v_lru_wa, v_lru_ba, v_lru_wx, v_lru_bx, v_lru_lambda, v_mix_norm, v_w_out, v_norm2, v_w_up, v_w_down, v_final_norm):
    w = dict(norm1=norm1, w_in=w_in, attn_sinks=attn_sinks, conv_dw_w=conv_dw_w, conv_dw_b=conv_dw_b,
             conv_ln_g=conv_ln_g, conv_ln_b=conv_ln_b, lru_conv_w=lru_conv_w, lru_conv_b=lru_conv_b, lru_wa=lru_wa,
             lru_ba=lru_ba, lru_wx=lru_wx, lru_bx=lru_bx, lru_lambda=lru_lambda, mix_norm=mix_norm, w_out=w_out,
             norm2=norm2, w_up=w_up, w_down=w_down, final_norm=final_norm)
    m = dict(norm1=m_norm1, w_in=m_w_in, attn_sinks=m_attn_sinks, conv_dw_w=m_conv_dw_w, conv_dw_b=m_conv_dw_b,
             conv_ln_g=m_conv_ln_g, conv_ln_b=m_conv_ln_b, lru_conv_w=m_lru_conv_w, lru_conv_b=m_lru_conv_b,
             lru_wa=m_lru_wa, lru_ba=m_lru_ba, lru_wx=m_lru_wx, lru_bx=m_lru_bx, lru_lambda=m_lru_lambda,
             mix_norm=m_mix_norm, w_out=m_w_out, norm2=m_norm2, w_up=m_w_up, w_down=m_w_down, final_norm=m_final_norm)
    v = dict(norm1=v_norm1, w_in=v_w_in, attn_sinks=v_attn_sinks, conv_dw_w=v_conv_dw_w, conv_dw_b=v_conv_dw_b,
             conv_ln_g=v_conv_ln_g, conv_ln_b=v_conv_ln_b, lru_conv_w=v_lru_conv_w, lru_conv_b=v_lru_conv_b,
             lru_wa=v_lru_wa, lru_ba=v_lru_ba, lru_wx=v_lru_wx, lru_bx=v_lru_bx, lru_lambda=v_lru_lambda,
             mix_norm=v_mix_norm, w_out=v_w_out, norm2=v_norm2, w_up=v_w_up, w_down=v_w_down, final_norm=v_final_norm)
    depth = w_in.shape[0]
    xi, yi, ci = _me()
    dev = (4 * xi + 2 * yi + ci).astype(jnp.int32)
    dev1 = dev.reshape(1)
    tr = lambda a: jnp.swapaxes(a, 1, 2)
    w_t, m_t, v_t = tr(w_in), tr(m_w_in), tr(v_w_in)
    wb = {n: w[n].astype(MX) for n in _BIG if n != "w_in"}
    wb["w_in"] = w_t.astype(MX)
    layer_shards = lambda l: [wb["w_out"][l], wb["w_up"][l], wb["w_down"][l]]

    _, ((g_in0, g_cw, g_lcw),) = _call(None, "gather_first", None, [], [], [], [], [],
                                        [_gather_rider([wb["w_in"][0], conv_dw_w, lru_conv_w])])
    cols = lambda g: jnp.moveaxis(g, 0, -2).reshape(g.shape[1:-1] + (N_DEV * g.shape[-1],))
    p = dict(w)
    p["conv_dw_w"] = cols(g_cw)
    p["lru_conv_w"] = cols(g_lcw)
    lp = [_layer_params(p, l) for l in range(depth)]

    gathered = [dict(w_in=g_in0.reshape(IN_W, D_MODEL)), dict()]
    saved = []
    h = x[0]
    for l in range(depth):
        q, gw = lp[l], gathered[l]
        z, hn1 = _ln_in(h, q["g1"], gw["w_in"], l == 0, f"ln_in{l}")
        riders = [_gather_rider(layer_shards(0))] if l == 0 else []
        (ycat, hl, uc, probs, psinks), got = _mixer_fwd(z, q["sink"], q["cw"], q["pv"], q["wa"], q["wx"],
                                                        f"mixer_fwd{l}", riders)
        if l == 0:
            gw["w_out"], gw["w_up"], gw["w_down"] = got[0]
            gw["w_out"] = gw["w_out"].reshape(D_MODEL, D_MODEL)
        riders = [_gather_rider([wb["w_in"][1]] + layer_shards(1))] if l == 0 else []
        (h1, act, h2, ym, hn2), got = _post_fwd(ycat, h, q["gmix"], gw["w_out"], q["g2"], gw["w_up"],
                                                gw["w_down"].reshape(D_FF, D_MODEL), f"post_fwd{l}", riders)
        if l == 0:
            nxt = gathered[1]
            nxt["w_in"], nxt["w_out"], nxt["w_up"], nxt["w_down"] = got[0]
            nxt["w_in"] = nxt["w_in"].reshape(IN_W, D_MODEL)
            nxt["w_out"] = nxt["w_out"].reshape(D_MODEL, D_MODEL)
        saved.append(dict(h0=h, z=z, hn1=hn1, ycat=ycat, hl=hl, uc=uc, probs=probs, psinks=psinks, h1=h1, act=act,
                          ym=ym, hn2=hn2))
        h = h2
    dh, loss, dgf = _loss_head(h, final_norm.reshape(1, -1), loss_target[0], "loss_head")

    grads = [None] * depth
    big = {n: [None] * depth for n in _BIG}
    pending = []

    def send_pending():
        riders = [_scatter_rider([item[3] for item in pending])] if pending else []
        return riders, list(pending)

    def record(sent, got):
        for item, recv in zip(sent, got[0] if sent else []):
            big[item[0]][item[1]] = (item[2], recv)
        del pending[:len(sent)]

    for l in reversed(range(depth)):
        q, s, gw = lp[l], saved[l], gathered[l]
        riders, sent = send_pending()
        w_up_t = jnp.swapaxes(gw["w_up"], 1, 2).reshape(D_FF, D_MODEL)
        (dh1, dh1b, dhb, du, dg2), got = _ffn_bwd(dh, s["act"], s["h1"], q["g2"], w_up_t, gw["w_down"],
                                                  f"ffn_bwd{l}", riders)
        record(sent, got)
        dycat, dgm, d_wout = _mix_bwd(dh1b, s["ycat"], s["ym"], q["gmix"], gw["w_out"], f"mix_bwd{l}")
        pending.append(("w_down", l) + tuple(_dw(s["act"], dhb, f"dw_down{l}", "rows", 2048, D_MODEL)))
        pending.append(("w_up", l) + tuple(_dw(s["hn2"], du, f"dw_up{l}", "cols", D_MODEL, 2048)))
        pending.append(("w_out", l) + tuple(d_wout))
        riders, sent = send_pending()
        (dz, dsink, dcw, dpv, dwa, dwx), got = _mixer_bwd(
            dycat, s["z"], s["ycat"], s["hl"], s["uc"], s["probs"], s["psinks"], q["sink"], q["cw"], q["pv"], q["wa"],
            q["wx"], f"mixer_bwd{l}", riders)
        record(sent, got)
        if l > 0:
            dh, dg1, d_win = _in_bwd_dw(dz, s["h0"], dh1, q["g1"], gw["w_in"], f"in_bwd{l}")
            pending.append(("w_in", l) + tuple(d_win))
        else:
            d_win = _dw(dz, s["hn1"], f"dw_in{l}", "rows", IN_W, D_MODEL)
            win_sends = _send_start([d_win[1]], True, "scatter_w_in0_start")
            dh, dg1 = _in_bwd(dz, s["h0"], dh1, q["g1"], gw["w_in"], win_sends.token, f"in_bwd{l}")
        grads[l] = dict(dg1=dg1, dsink=dsink, dcw=dcw, dpv=dpv, dwa=dwa, dwx=dwx, dgm=dgm, dg2=dg2)

    acc = {k: jnp.stack([grads[l][k] for l in range(depth)]) for k in grads[0]}
    dpv = acc["dpv"]
    unblock = lambda a: jnp.concatenate([a[:, h * 64:(h + 1) * 64, h * 64:(h + 1) * 64] for h in range(4)], axis=1)
    by_name = dict(
        norm1=acc["dg1"][:, 0], attn_sinks=jnp.stack([acc["dsink"][:, 0:4, 0], acc["dsink"][:, 0:4, 2 * BLK]],
                                                     axis=2).reshape(depth, 8),
        conv_dw_w=acc["dcw"][:, 0:CONV_K], conv_dw_b=dpv[:, R_CONV_B], conv_ln_g=dpv[:, R_LN_G],
        conv_ln_b=dpv[:, R_LN_B], lru_conv_w=dpv[:, R_LCW:R_LCW + LRU_K], lru_conv_b=dpv[:, R_LCONV_B],
        lru_wa=unblock(acc["dwa"]), lru_ba=dpv[:, R_BA].reshape(depth, 4, 64), lru_wx=unblock(acc["dwx"]),
        lru_bx=dpv[:, R_BX].reshape(depth, 4, 64), lru_lambda=dpv[:, R_LAM], mix_norm=acc["dgm"][:, 0],
        norm2=acc["dg2"][:, 0])
    small = [by_name[n] for n in _SMALL] + [dgf, loss[:, 0:1]]

    def as_rows(a):
        flat = a.reshape(-1)
        pad = (-flat.size) % 1024
        if pad:
            flat = jnp.concatenate([flat, jnp.zeros((pad,), F32)])
        return flat.reshape(-1, 128)

    pieces = [as_rows(a) for a in small]
    packed = jnp.concatenate(pieces, axis=0)
    small_sends = _send_start([packed], False, "bcast_small_start")

    out = {}
    shard_update = lambda n, wmv, after: list(_adamw_shard(
        [big[n][l][0] for l in range(depth)], [big[n][l][1] for l in range(depth)], dev1, *wmv, after, f"adamw_{n}"))
    for n in ("w_out", "w_up", "w_down"):
        out[n] = shard_update(n, (w[n], m[n], v[n]), small_sends.token)
    _, (win_recv,) = _send_wait(win_sends, out["w_down"][1], "scatter_w_in0_wait")
    big["w_in"][0] = (d_win[0], win_recv)
    out["w_in"] = [tr(a) for a in shard_update("w_in", (w_t, m_t, v_t), jnp.zeros((8, 128), F32))]
    (packed,), (small_recv,) = _send_wait(small_sends, out["w_in"][1], "bcast_small_wait")
    summed = _sum_parts(packed, small_recv, dev1, "sum_small_grads")
    small_sums, row = [], 0
    for a, piece in zip(small, pieces):
        got = summed[row:row + piece.shape[0]]
        small_sums.append(got.reshape(a.shape) if a.size == piece.size else got.reshape(-1)[:a.size].reshape(a.shape))
        row += piece.shape[0]
    shard = lambda a: lax.dynamic_slice_in_dim(a, dev * (a.shape[-1] // N_DEV), a.shape[-1] // N_DEV, axis=a.ndim - 1)
    flat = {"lru_wa": (depth, LRU_W, 64), "lru_wx": (depth, LRU_W, 64), "final_norm": (1, D_MODEL)}
    gs, ws, ms, vs = [], [], [], []
    for n, g in zip(_SMALL + ["final_norm"], small_sums[:-1]):
        shp = flat.get(n, w[n].shape)
        gs.append((shard(g) if n in ("conv_dw_w", "lru_conv_w") else g).reshape(shp))
        ws.append(w[n].reshape(shp))
        ms.append(m[n].reshape(shp))
        vs.append(v[n].reshape(shp))
    sd, sm, sv = _adamw_small(gs, ws, ms, vs, "adamw_small")
    for j, n in enumerate(_SMALL + ["final_norm"]):
        out[n] = [a.reshape(w[n].shape) for a in (gs[j], sd[j], sm[j], sv[j])]
    loss_total = small_sums[-1][0, 0]

    result = [loss_total, dh[None]]
    for j in range(4):
        result += [out[n][j] for n in _WEIGHTS]
    return tuple(result)
```

```python
import types

import jax
import jax.numpy as jnp
from jax import lax
from jax.experimental import pallas as pl
from jax.experimental.pallas import tpu as pltpu

F32 = jnp.float32
MX = jnp.bfloat16
WIRE = jnp.bfloat16

D_MODEL = 1024
HEAD_DIM = 64
ATTN_W = 512
KV_W = 128
BLK = 128
CONV_W = 256
CONV_K = 31
LRU_W = 256
LRU_K = 4
LRU_C = 8.0
IN_W = 1792
D_FF = 4096
FF_BLK = 512
N_DEV = 8
IN_SHARD = IN_W // N_DEV
RMS_EPS = 1e-6
LN_EPS = 1e-5
MASK_VALUE = -1e30
SCALE = HEAD_DIM ** -0.5
CONV_HALO = 32
LRU_HALO = 8
CONV_CHUNK = 64
POST_TILE = 512
STREAM_TILE = 1024
DW_TILE = 1024
Q0, K0, V0, CV0, CG0, RX0, RG0 = 0, 512, 640, 768, 1024, 1280, 1536
R_CONV_B, R_LN_G, R_LN_B, R_LCONV_B, R_BA, R_BX, R_LAM, R_LCW = 0, 1, 2, 3, 4, 5, 6, 8

ADAM_LR, ADAM_B1, ADAM_B2, ADAM_EPS, ADAM_WD, ADAM_STEP = 0.001, 0.9, 0.999, 1e-08, 0.01, 10

VMEM_LIMIT = 56 * 1024 * 1024
MESH = pl.DeviceIdType.MESH
ANY = pl.BlockSpec(memory_space=pl.ANY)


def _tile(t, cap=512):
    return min(cap, t)


def _dot(a, b):
    return jnp.dot(a.astype(MX), b.astype(MX), preferred_element_type=F32)


def _dot_nt(a, b):
    return lax.dot_general(a.astype(MX), b.astype(MX), (((1,), (1,)), ((), ())), preferred_element_type=F32)


def _dot_tn(a, b):
    return lax.dot_general(a.astype(MX), b.astype(MX), (((0,), (0,)), ((), ())), preferred_element_type=F32)


def _const_spec(shape):
    nd = len(shape)
    return pl.BlockSpec(shape, lambda *_: (0,) * nd, pipeline_mode=pl.Buffered(1))


def _acc_spec(shape):
    nd = len(shape)
    return pl.BlockSpec(shape, lambda *_: (0,) * nd)


def _sds(shape, dtype):
    return jax.ShapeDtypeStruct(shape, dtype)


def _sigmoid(x):
    return jax.nn.sigmoid(x)


def _rms_fwd(x, g):
    r = lax.rsqrt(jnp.mean(x * x, axis=-1, keepdims=True) + RMS_EPS)
    xh = x * r
    return xh * g, xh, r


def _rms_bwd(dy, xh, r, g):
    t = dy * g
    dx = r * (t - xh * jnp.mean(t * xh, axis=-1, keepdims=True))
    return dx, jnp.sum(dy * xh, axis=0, keepdims=True)


_GROUPS = ((0, 512), (512, 768), (768, 1024))


def _group_rms_fwd(y, g):
    parts = [_rms_fwd(y[:, a:b], g[:, a:b]) for a, b in _GROUPS]
    return (jnp.concatenate([p[0] for p in parts], axis=1),
            jnp.concatenate([p[1] for p in parts], axis=1),
            [p[2] for p in parts])


def _gelu(x):
    c = 0.7978845608028654
    u = c * (x + 0.044715 * x * x * x)
    th = jnp.tanh(u)
    val = 0.5 * x * (1.0 + th)
    grad = 0.5 * (1.0 + th) + 0.5 * x * (1.0 - th * th) * c * (1.0 + 3.0 * 0.044715 * x * x)
    return val, grad


def _neg_expm1(x):
    series = -x * (1.0 + x * (0.5 + x * (1.0 / 6.0 + x * (1.0 / 24.0))))
    return jnp.where(x > -0.02, series, 1.0 - jnp.exp(x))


def _me():
    return lax.axis_index("x"), lax.axis_index("y"), lax.axis_index("c")


def _gather_rider(arrays):
    arrays = list(arrays)
    n = len(arrays)

    def plan(ins, outs, sems):
        ssem, rsem, lsem = sems
        x, y, c = _me()
        chips = [(1 - x, y), (x, 1 - y), (1 - x, 1 - y)]

        def copy(a, k, block, to, own=False):
            dst = outs[a].at[4 * block[0] + 2 * block[1] + block[2]]
            return pltpu.make_async_remote_copy(
                src_ref=ins[a] if own else dst, dst_ref=dst, send_sem=ssem.at[7 * a + k],
                recv_sem=rsem.at[7 * a + k], device_id=to, device_id_type=MESH)

        return x, y, c, chips, copy, lsem

    def start(ins, outs, sems):
        x, y, c, chips, copy, lsem = plan(ins, outs, sems)
        for a in range(n):
            pltpu.make_async_copy(ins[a], outs[a].at[4 * x + 2 * y + c], lsem.at[a]).start()
            copy(a, 0, (x, y, c), (x, y, 1 - c), own=True).start()
            for j, chip in enumerate(chips):
                copy(a, 1 + j, (x, y, c), (*chip, c), own=True).start()

    def mid(ins, outs, sems):
        x, y, c, chips, copy, _ = plan(ins, outs, sems)
        for a in range(n):
            for j, chip in enumerate(chips):
                copy(a, 1 + j, (*chip, c), (x, y, c)).wait_recv()
                copy(a, 4 + j, (*chip, c), (x, y, 1 - c)).start()

    def finish(ins, outs, sems):
        x, y, c, chips, copy, lsem = plan(ins, outs, sems)
        for a in range(n):
            copy(a, 0, (x, y, 1 - c), (x, y, c)).wait_recv()
            for j, chip in enumerate(chips):
                copy(a, 4 + j, (*chip, 1 - c), (x, y, c)).wait_recv()
        for a in range(n):
            copy(a, 0, (x, y, c), (x, y, 1 - c), own=True).wait_send()
            for j, chip in enumerate(chips):
                copy(a, 1 + j, (x, y, c), (*chip, c), own=True).wait_send()
                copy(a, 4 + j, (*chip, c), (x, y, 1 - c)).wait_send()
            pltpu.make_async_copy(ins[a], outs[a].at[4 * x + 2 * y + c], lsem.at[a]).wait()

    return types.SimpleNamespace(
        arrays=arrays, out_shape=[_sds((N_DEV,) + a.shape, a.dtype) for a in arrays],
        scratch=[pltpu.SemaphoreType.DMA((7 * n,)), pltpu.SemaphoreType.DMA((7 * n,)), pltpu.SemaphoreType.DMA((n,))],
        start=start, mid=mid, finish=finish)


def _scatter_rider(arrays):
    arrays = list(arrays)
    n = len(arrays)

    def copies(ins, outs, sems):
        ssem, rsem = sems
        x, y, c = _me()
        out = []
        for a in range(n):
            for f in range(1, N_DEV):
                px = 1 - x if f & 4 else x
                py = 1 - y if f & 2 else y
                pc = 1 - c if f & 1 else c
                out.append(pltpu.make_async_remote_copy(
                    src_ref=ins[a].at[4 * px + 2 * py + pc], dst_ref=outs[a].at[f - 1], send_sem=ssem.at[7 * a + f - 1],
                    recv_sem=rsem.at[7 * a + f - 1], device_id=(px, py, pc), device_id_type=MESH))
        return out

    def start(ins, outs, sems):
        for cp in copies(ins, outs, sems):
            cp.start()

    def finish(ins, outs, sems):
        for cp in copies(ins, outs, sems):
            cp.wait()

    return types.SimpleNamespace(
        arrays=arrays, out_shape=[_sds((N_DEV - 1,) + a.shape[1:], a.dtype) for a in arrays],
        scratch=[pltpu.SemaphoreType.DMA((7 * n,)), pltpu.SemaphoreType.DMA((7 * n,))],
        start=start, mid=None, finish=finish)


def _call(body, name, grid, in_specs, out_specs, out_shape, scratch, operands, riders=()):
    n_in, n_out, n_scr = len(operands), len(out_shape), len(scratch)
    nsteps = grid[0] if grid else 1
    sizes = [(len(r.arrays), len(r.out_shape), len(r.scratch)) for r in riders]

    def wrapped(*refs):
        pos = n_in
        r_ins = []
        for ri, _, _ in sizes:
            r_ins.append(refs[pos:pos + ri])
            pos += ri
        outs = refs[pos:pos + n_out]
        pos += n_out
        r_outs = []
        for _, ro, _ in sizes:
            r_outs.append(refs[pos:pos + ro])
            pos += ro
        scr = refs[pos:pos + n_scr]
        pos += n_scr
        r_sems = []
        for _, _, rs in sizes:
            r_sems.append(refs[pos:pos + rs])
            pos += rs
        step = pl.program_id(0) if grid else 0

        def at(s, fn):
            if grid:
                pl.when(step == s)(fn)
            else:
                fn()

        for r, a, b, c in zip(riders, r_ins, r_outs, r_sems):
            at(0, lambda r=r, a=a, b=b, c=c: r.start(a, b, c))
        for r, a, b, c in zip(riders, r_ins, r_outs, r_sems):
            if r.mid is not None:
                at((7 * nsteps) // 8, lambda r=r, a=a, b=b, c=c: r.mid(a, b, c))
        if body is not None:
            body(*refs[:n_in], *outs, *scr)
        for r, a, b, c in zip(riders, r_ins, r_outs, r_sems):
            at(nsteps - 1, lambda r=r, a=a, b=b, c=c: r.finish(a, b, c))

    r_arrays = [a for r in riders for a in r.arrays]
    r_shapes = [s for r in riders for s in r.out_shape]
    kwargs = {}
    if grid:
        kwargs = dict(grid=grid, compiler_params=pltpu.CompilerParams(
            dimension_semantics=("arbitrary",) * len(grid), vmem_limit_bytes=VMEM_LIMIT))
    res = pl.pallas_call(
        wrapped, name=name,
        in_specs=list(in_specs) + [ANY] * len(r_arrays),
        out_specs=list(out_specs) + [ANY] * len(r_shapes),
        out_shape=list(out_shape) + r_shapes,
        scratch_shapes=list(scratch) + [s for r in riders for s in r.scratch],
        **kwargs,
    )(*operands, *r_arrays)
    host, rest = res[:n_out], res[n_out:]
    r_res = []
    for _, ro, _ in sizes:
        r_res.append(rest[:ro])
        rest = rest[ro:]
    return host, r_res


def _ln_in(h, g1, w_in_t, keep_hn, name):
    t = h.shape[0]
    tm = _tile(t, STREAM_TILE)

    def body(h_ref, g_ref, w_ref, z_ref, *hn_ref):
        y, _, _ = _rms_fwd(h_ref[...], g_ref[...])
        hn = y.astype(MX)
        if keep_hn:
            hn_ref[0][...] = hn
        z_ref[...] = _dot_nt(hn, w_ref[...])

    tile = lambda w: pl.BlockSpec((tm, w), lambda i: (i, 0))
    outs, _ = _call(
        body, name, (t // tm,),
        [tile(D_MODEL), _const_spec((1, D_MODEL)), _const_spec((IN_W, D_MODEL))],
        [tile(IN_W)] + [tile(D_MODEL)] * keep_hn,
        [_sds((t, IN_W), F32)] + [_sds((t, D_MODEL), MX)] * keep_hn, [], [h, g1, w_in_t])
    return outs[0], (outs[1] if keep_hn else None)


def _band2(kb, g):
    lo = lax.broadcasted_iota(jnp.int32, kb.shape, 1) < HEAD_DIM
    kr = pltpu.roll(kb, HEAD_DIM, 1)
    if g == 0:
        top, bot = jnp.where(lo, kb, 0.0), jnp.where(lo, 0.0, kr)
    else:
        top, bot = jnp.where(lo, kr, 0.0), jnp.where(lo, 0.0, kb)
    return jnp.concatenate([top, bot], axis=0)


def _attn_operands(z_ref, zh_ref, b):
    rows = slice(b * BLK, (b + 1) * BLK)
    prev = zh_ref if b == 0 else z_ref
    prow = slice(0, BLK) if b == 0 else slice((b - 1) * BLK, b * BLK)
    kb = jnp.concatenate([prev[prow, K0:K0 + KV_W], z_ref[rows, K0:K0 + KV_W]], axis=0)
    vb = jnp.concatenate([prev[prow, V0:V0 + KV_W], z_ref[rows, V0:V0 + KV_W]], axis=0)
    k2 = [_band2(kb, g) for g in range(2)]
    v2 = [_band2(vb, g) for g in range(2)]
    q2 = [jnp.concatenate([z_ref[rows, (2 * g) * BLK:(2 * g + 1) * BLK], z_ref[rows, (2 * g + 1) * BLK:(2 * g + 2) * BLK]],
                          axis=0) for g in range(2)]
    return q2, k2, v2


def _attn_block(z_ref, zh_ref, sink_ref, b, first):
    q2, k2, v2 = _attn_operands(z_ref, zh_ref, b)
    rr = lax.broadcasted_iota(jnp.int32, (4 * BLK, 2 * BLK), 0) & (BLK - 1)
    cc = lax.broadcasted_iota(jnp.int32, (4 * BLK, 2 * BLK), 1)
    first_block = jnp.logical_and(first, b == 0).astype(jnp.int32)
    mask = jnp.logical_and(jnp.logical_and(cc > rr, cc <= rr + BLK), cc >= BLK * first_block)
    s = jnp.concatenate([_dot_nt(q2[g], k2[g]) for g in range(2)], axis=0) * SCALE
    w = 2 * BLK
    out, psink = [], []
    for hh in range(2):
        sh = jnp.where(mask, s[:, hh * w:(hh + 1) * w], MASK_VALUE)
        sk = jnp.concatenate([jnp.broadcast_to(sink_ref[p:p + 1, hh * w:hh * w + 1], (BLK, 1)) for p in range(4)], axis=0)
        m = jnp.maximum(jnp.max(sh, axis=1, keepdims=True), sk)
        p = jnp.exp(sh - m)
        es = jnp.exp(sk - m)
        inv = 1.0 / (jnp.sum(p, axis=1, keepdims=True) + es)
        out.append(p * inv)
        psink.append(es * inv)
    return v2, jnp.concatenate(out, axis=1), psink


def _scan_steps(a, b, n, span, reverse):
    pos = lax.broadcasted_iota(jnp.int32, a.shape, 0) & (span - 1)
    d = 1
    while d < span:
        keep = pos < span - d if reverse else pos >= d
        shift = n - d if reverse else d
        a_sh = jnp.where(keep, pltpu.roll(a, shift, 0), 1.0)
        b_sh = jnp.where(keep, pltpu.roll(b, shift, 0), 0.0)
        b = a * b_sh + b
        a = a * a_sh
        d *= 2
    return a, b


def _scan(a, b, tm, reverse):
    return _scan_steps(a, b, tm, tm, reverse)


def _shifted_copies(ext, shifts, tm):
    rows = tm + CONV_HALO - 8
    for r in range(1, 8):
        shifts[r - 1, 0:rows, :] = ext[pl.ds(r, rows), :]


def _tap(ext, shifts, off, r0, n):
    a, r = divmod(off, 8)
    lo = 8 * a + r0
    if r == 0:
        return ext[lo:lo + n, :]
    return shifts[r - 1, lo:lo + n, :]


def _glu_fill(z_ref, zh_ref, uext, ush, first, tm, sg_out=None):
    cv = z_ref[:, CV0:CV0 + CONV_W]
    sg = _sigmoid(z_ref[:, CG0:CG0 + CONV_W])
    if sg_out is not None:
        sg_out[...] = sg
    hrow = BLK - CONV_HALO
    uh = zh_ref[hrow:BLK, CV0:CV0 + CONV_W] * _sigmoid(zh_ref[hrow:BLK, CG0:CG0 + CONV_W])
    uext[0:CONV_HALO, :] = jnp.where(first, 0.0, uh)
    uext[CONV_HALO:CONV_HALO + tm, :] = cv * sg
    _shifted_copies(uext, ush, tm)


def _conv_taps(cw_ref, pv_ref, uext, ush, out_ref, tm):
    for r0 in range(0, tm, CONV_CHUNK):
        acc = jnp.broadcast_to(pv_ref[R_CONV_B:R_CONV_B + 1, :], (CONV_CHUNK, CONV_W))
        for k in range(CONV_K):
            acc = acc + cw_ref[k:k + 1, :] * _tap(uext, ush, CONV_HALO - (CONV_K - 1) + k, r0, CONV_CHUNK)
        out_ref[r0:r0 + CONV_CHUNK, :] = acc


def _ln_silu(uc, pv_ref):
    mu = jnp.mean(uc, axis=-1, keepdims=True)
    xc = uc - mu
    rs = lax.rsqrt(jnp.mean(xc * xc, axis=-1, keepdims=True) + LN_EPS)
    xh = xc * rs
    ln = xh * pv_ref[R_LN_G:R_LN_G + 1, :] + pv_ref[R_LN_B:R_LN_B + 1, :]
    sg = _sigmoid(ln)
    return xh, rs, ln, sg


def _lru_gates(z_ref, zh_ref, pv_ref, wa_ref, wx_ref, rxext, first, tm):
    rxext[0:LRU_HALO, :] = jnp.where(first, 0.0, zh_ref[BLK - LRU_HALO:BLK, RX0:RX0 + LRU_W])
    rxext[LRU_HALO:LRU_HALO + tm, :] = z_ref[:, RX0:RX0 + LRU_W]
    xc = jnp.broadcast_to(pv_ref[R_LCONV_B:R_LCONV_B + 1, :], (tm, LRU_W))
    for k in range(LRU_K):
        xc = xc + pv_ref[R_LCW + k:R_LCW + k + 1, :] * rxext[pl.ds(LRU_HALO - (LRU_K - 1) + k, tm), :]
    r = _sigmoid(_dot(xc, wa_ref[...]) + pv_ref[R_BA:R_BA + 1, :])
    ig = _sigmoid(_dot(xc, wx_ref[...]) + pv_ref[R_BX:R_BX + 1, :])
    lam = pv_ref[R_LAM:R_LAM + 1, :]
    sp = jnp.log1p(jnp.exp(-lam))
    la = (-LRU_C * r) * sp
    a = jnp.exp(la)
    mult = jnp.sqrt(_neg_expm1(2.0 * la))
    return xc, r, ig, sp, la, a, mult


def _mixer_in_specs(tm, tile_of):
    hb = tm // BLK
    return [
        pl.BlockSpec((tm, IN_W), lambda i: (tile_of(i), 0)),
        pl.BlockSpec((BLK, IN_W), lambda i: (jnp.maximum(tile_of(i) * hb - 1, 0), 0)),
        _const_spec((8, 4 * BLK)),
        _const_spec((32, CONV_W)),
        _const_spec((16, CONV_W)),
        _const_spec((LRU_W, LRU_W)),
        _const_spec((LRU_W, LRU_W)),
    ]


def _mixer_fwd(z, sink, cw, pv, wa, wx, name, riders=()):
    t = z.shape[0]
    tm = _tile(t)
    nb = tm // BLK

    def body(z_ref, zh_ref, sink_ref, cw_ref, pv_ref, wa_ref, wx_ref, y_ref, hl_ref, uc_ref, p_ref, ps_ref,
             uext, ush, rxext, hcar):
        i = pl.program_id(0)
        first = i == 0

        @pl.when(first)
        def _():
            hcar[...] = jnp.zeros_like(hcar)

        lo = lax.broadcasted_iota(jnp.int32, (4 * BLK, BLK), 1) < HEAD_DIM
        for b in range(nb):
            rows = slice(b * BLK, (b + 1) * BLK)
            v2, prob, psink = _attn_block(z_ref, zh_ref, sink_ref, b, first)
            prob = prob.astype(MX)
            p_ref[b] = prob
            ps_ref[b] = jnp.where(lo, psink[0], psink[1])
            for g in range(2):
                o = _dot(prob[2 * g * BLK:(2 * g + 2) * BLK], v2[g])
                y_ref[rows, (2 * g) * BLK:(2 * g + 1) * BLK] = o[0:BLK]
                y_ref[rows, (2 * g + 1) * BLK:(2 * g + 2) * BLK] = o[BLK:2 * BLK]
        _glu_fill(z_ref, zh_ref, uext, ush, first, tm)
        _conv_taps(cw_ref, pv_ref, uext, ush, uc_ref, tm)
        _, _, ln, sg = _ln_silu(uc_ref[...], pv_ref)
        y_ref[:, ATTN_W:ATTN_W + CONV_W] = ln * sg
        xc, _, ig, _, _, a, mult = _lru_gates(z_ref, zh_ref, pv_ref, wa_ref, wx_ref, rxext, first, tm)
        acum, h = _scan(a, mult * (ig * xc), tm, reverse=False)
        h = h + acum * hcar[0:1, :]
        hl_ref[...] = h
        hcar[0:1, :] = h[tm - 1:tm, :]
        gl, _ = _gelu(z_ref[:, RG0:RG0 + LRU_W])
        y_ref[:, ATTN_W + CONV_W:ATTN_W + CONV_W + LRU_W] = h * gl

    tile = lambda w: pl.BlockSpec((tm, w), lambda i: (i, 0))
    return _call(
        body, name, (t // tm,), _mixer_in_specs(tm, lambda i: i),
        [tile(D_MODEL), tile(LRU_W), tile(CONV_W), pl.BlockSpec((nb, 4 * BLK, 4 * BLK), lambda i: (i, 0, 0)),
         pl.BlockSpec((nb, 4 * BLK, BLK), lambda i: (i, 0, 0))],
        [_sds((t, D_MODEL), F32), _sds((t, LRU_W), F32), _sds((t, CONV_W), F32),
         _sds((t // BLK, 4 * BLK, 4 * BLK), MX), _sds((t // BLK, 4 * BLK, BLK), F32)],
        [pltpu.VMEM((tm + CONV_HALO, CONV_W), F32), pltpu.VMEM((7, tm + CONV_HALO - 8, CONV_W), F32),
         pltpu.VMEM((tm + LRU_HALO, LRU_W), F32), pltpu.VMEM((8, LRU_W), F32)],
        [z, z, sink, cw, pv, wa, wx], riders)


def _mixer_bwd(dy, z, ycat, hl, uc, probs, psinks, sink, cw, pv, wa, wx, name, riders=()):
    t = z.shape[0]
    tm = _tile(t)
    nt = t // tm
    nb = tm // BLK
    rev = lambda i: nt - 1 - i

    def body(dy_ref, z_ref, zh_ref, sink_ref, cw_ref, pv_ref, wa_ref, wx_ref, y_ref, hl_ref, hlh_ref, uc_ref,
             p_ref, ps_ref, dz_ref, dsink_ref, dcw_ref, dpv_ref, dwa_ref, dwx_ref,
             uext, ush, sgs, rxext, dkext, dvext, ducext, dsh, dcw8, dxcext, kcar, vcar, uccar, xccar, gcar):
        i = pl.program_id(0)
        first = i == nt - 1

        @pl.when(i == 0)
        def _():
            for car in (kcar, vcar, uccar, xccar, gcar, dcw8):
                car[...] = jnp.zeros_like(car)
            for acc in (dsink_ref, dpv_ref, dwa_ref, dwx_ref):
                acc[...] = jnp.zeros_like(acc)

        def addrow(r, val):
            dpv_ref[r:r + 1, :] += jnp.sum(val, axis=0, keepdims=True)

        dkext[:, 0:tm] = jnp.zeros((KV_W, tm), F32)
        dvext[:, 0:tm] = jnp.zeros((KV_W, tm), F32)
        dkext[:, tm:tm + BLK] = kcar[...]
        dvext[:, tm:tm + BLK] = vcar[...]
        lane512 = lax.broadcasted_iota(jnp.int32, (1, 4 * BLK), 1) < 2 * BLK
        lo = lax.broadcasted_iota(jnp.int32, (4 * BLK, BLK), 1) < HEAD_DIM
        hd, w2 = HEAD_DIM, 2 * BLK
        for b in range(nb):
            rows = slice(b * BLK, (b + 1) * BLK)
            band = slice(b * BLK, (b + 2) * BLK)
            q2, k2, v2 = _attn_operands(z_ref, zh_ref, b)
            prob = p_ref[b]
            psink = [ps_ref[b, :, 0:1], ps_ref[b, :, HEAD_DIM:HEAD_DIM + 1]]
            stack = lambda ref: jnp.concatenate([ref[rows, p * BLK:(p + 1) * BLK] for p in range(4)], axis=0)
            do4 = stack(dy_ref)
            dlt = do4 * stack(y_ref)
            d0 = jnp.sum(jnp.where(lo, dlt, 0.0), axis=1, keepdims=True)
            d1 = jnp.sum(jnp.where(lo, 0.0, dlt), axis=1, keepdims=True)
            dp = jnp.concatenate([_dot_nt(do4[g * w2:(g + 1) * w2], v2[g]) for g in range(2)], axis=0)
            dl = jnp.concatenate([jnp.broadcast_to(d0, (4 * BLK, w2)), jnp.broadcast_to(d1, (4 * BLK, w2))], axis=1)
            draw = (prob * (dp - dl)) * SCALE
            e0, e1 = psink[0] * d0, psink[1] * d1
            for p in range(4):
                prs = slice(p * BLK, (p + 1) * BLK)
                s0 = jnp.sum(e0[prs], axis=0, keepdims=True)
                s1 = jnp.sum(e1[prs], axis=0, keepdims=True)
                dsink_ref[p:p + 1, :] += -jnp.where(lane512, s0, s1)
            for g in range(2):
                grs = slice(g * w2, (g + 1) * w2)
                dq = _dot(draw[grs], k2[g])
                dz_ref[rows, (2 * g) * BLK:(2 * g + 1) * BLK] = dq[0:BLK].astype(dz_ref.dtype)
                dz_ref[rows, (2 * g + 1) * BLK:(2 * g + 2) * BLK] = dq[BLK:2 * BLK].astype(dz_ref.dtype)
                tk = _dot_tn(q2[g], draw[grs])
                tv = _dot_tn(do4[grs], prob[grs])
                dkext[g * hd:(g + 1) * hd, band] += tk[0:hd, 0:w2] + tk[hd:2 * hd, w2:2 * w2]
                dvext[g * hd:(g + 1) * hd, band] += tv[0:hd, 0:w2] + tv[hd:2 * hd, w2:2 * w2]
        dz_ref[:, K0:K0 + KV_W] = jnp.transpose(dkext[:, BLK:BLK + tm]).astype(dz_ref.dtype)
        dz_ref[:, V0:V0 + KV_W] = jnp.transpose(dvext[:, BLK:BLK + tm]).astype(dz_ref.dtype)
        kcar[...] = dkext[:, 0:BLK]
        vcar[...] = dvext[:, 0:BLK]

        _glu_fill(z_ref, zh_ref, uext, ush, first, tm, sg_out=sgs)
        xh, rs, ln, sg = _ln_silu(uc_ref[...], pv_ref)
        dln = dy_ref[:, ATTN_W:ATTN_W + CONV_W] * (sg * (1.0 + ln * (1.0 - sg)))
        addrow(R_LN_G, dln * xh)
        addrow(R_LN_B, dln)
        dxh = dln * pv_ref[R_LN_G:R_LN_G + 1, :]
        duc = rs * (dxh - jnp.mean(dxh, axis=-1, keepdims=True) - xh * jnp.mean(dxh * xh, axis=-1, keepdims=True))
        addrow(R_CONV_B, duc)
        ducext[0:tm, :] = duc
        ducext[tm:tm + CONV_HALO, :] = uccar[...]
        uccar[...] = duc[0:CONV_HALO, :]
        _shifted_copies(ducext, dsh, tm)
        for r0 in range(0, tm, CONV_CHUNK):
            crow = slice(r0, r0 + CONV_CHUNK)
            duc_c = ducext[crow, :]
            du = jnp.zeros((CONV_CHUNK, CONV_W), F32)
            for k in range(CONV_K):
                prod = duc_c * _tap(uext, ush, CONV_HALO - (CONV_K - 1) + k, r0, CONV_CHUNK)
                part = prod[0:8]
                for s in range(8, CONV_CHUNK, 8):
                    part = part + prod[s:s + 8]
                dcw8[k] += part
                du = du + cw_ref[k:k + 1, :] * _tap(ducext, dsh, CONV_K - 1 - k, r0, CONV_CHUNK)
            sgc = sgs[crow, :]
            dz_ref[crow, CV0:CV0 + CONV_W] = (du * sgc).astype(dz_ref.dtype)
            u_c = uext[CONV_HALO + r0:CONV_HALO + r0 + CONV_CHUNK, :]
            dz_ref[crow, CG0:CG0 + CONV_W] = (du * u_c * (1.0 - sgc)).astype(dz_ref.dtype)

        @pl.when(i == nt - 1)
        def _():
            dcw_ref[...] = jnp.sum(dcw8[...], axis=1)

        xc, r, ig, sp, la, a, mult = _lru_gates(z_ref, zh_ref, pv_ref, wa_ref, wx_ref, rxext, first, tm)
        h = hl_ref[...]
        rowi = lax.broadcasted_iota(jnp.int32, (tm, LRU_W), 0)
        hlast = jnp.where(first, 0.0, hlh_ref[7:8, :])
        hprev = jnp.where(rowi == 0, hlast, pltpu.roll(h, 1, 0))
        dyl = dy_ref[:, ATTN_W + CONV_W:ATTN_W + CONV_W + LRU_W]
        gl, dgl = _gelu(z_ref[:, RG0:RG0 + LRU_W])
        dz_ref[:, RG0:RG0 + LRU_W] = (dyl * h * dgl).astype(dz_ref.dtype)
        dh = dyl * gl + jnp.where(rowi == tm - 1, gcar[0:1, :], 0.0)
        c = jnp.where(rowi == tm - 1, 0.0, pltpu.roll(a, tm - 1, 0))
        _, gg = _scan(c, dh, tm, reverse=True)
        gcar[0:1, :] = a[0:1, :] * gg[0:1, :]
        dmult = gg * (ig * xc)
        dig = gg * mult * xc
        dxc = gg * mult * ig
        dla = gg * hprev * a - dmult * a * a / mult
        dr = dla * (-LRU_C * sp)
        lam = pv_ref[R_LAM:R_LAM + 1, :]
        dpv_ref[R_LAM:R_LAM + 1, :] += jnp.sum(dla * (-LRU_C * r), axis=0, keepdims=True) * (-_sigmoid(-lam))
        dpa = dr * r * (1.0 - r)
        dpx = dig * ig * (1.0 - ig)
        addrow(R_BA, dpa)
        addrow(R_BX, dpx)
        dxc = dxc + _dot_nt(dpa, wa_ref[...]) + _dot_nt(dpx, wx_ref[...])
        dwa_ref[...] += _dot_tn(xc, dpa)
        dwx_ref[...] += _dot_tn(xc, dpx)
        addrow(R_LCONV_B, dxc)
        dxcext[0:tm, :] = dxc
        dxcext[tm:tm + LRU_HALO, :] = xccar[...]
        xccar[...] = dxc[0:LRU_HALO, :]
        drx = jnp.zeros((tm, LRU_W), F32)
        for k in range(LRU_K):
            addrow(R_LCW + k, dxc * rxext[pl.ds(LRU_HALO - (LRU_K - 1) + k, tm), :])
            drx = drx + pv_ref[R_LCW + k:R_LCW + k + 1, :] * dxcext[pl.ds(LRU_K - 1 - k, tm), :]
        dz_ref[:, RX0:RX0 + LRU_W] = drx.astype(dz_ref.dtype)

    tile = lambda w: pl.BlockSpec((tm, w), lambda i: (rev(i), 0))
    in_specs = [tile(D_MODEL)] + _mixer_in_specs(tm, rev) + [
        tile(D_MODEL), tile(LRU_W),
        pl.BlockSpec((8, LRU_W), lambda i: (jnp.maximum(rev(i) * (tm // 8) - 1, 0), 0)),
        tile(CONV_W), pl.BlockSpec((nb, 4 * BLK, 4 * BLK), lambda i: (rev(i), 0, 0)),
        pl.BlockSpec((nb, 4 * BLK, BLK), lambda i: (rev(i), 0, 0))]
    return _call(
        body, name, (nt,), in_specs,
        [tile(IN_W), _acc_spec((8, 4 * BLK)), _acc_spec((32, CONV_W)), _acc_spec((16, CONV_W)),
         _acc_spec((LRU_W, LRU_W)), _acc_spec((LRU_W, LRU_W))],
        [_sds((t, IN_W), MX), _sds((8, 4 * BLK), F32), _sds((32, CONV_W), F32), _sds((16, CONV_W), F32),
         _sds((LRU_W, LRU_W), F32), _sds((LRU_W, LRU_W), F32)],
        [pltpu.VMEM((tm + CONV_HALO, CONV_W), F32), pltpu.VMEM((7, tm + CONV_HALO - 8, CONV_W), F32),
         pltpu.VMEM((tm, CONV_W), F32), pltpu.VMEM((tm + LRU_HALO, LRU_W), F32),
         pltpu.VMEM((KV_W, tm + BLK), F32), pltpu.VMEM((KV_W, tm + BLK), F32),
         pltpu.VMEM((tm + CONV_HALO, CONV_W), F32), pltpu.VMEM((7, tm + CONV_HALO - 8, CONV_W), F32),
         pltpu.VMEM((32, 8, CONV_W), F32), pltpu.VMEM((tm + LRU_HALO, LRU_W), F32),
         pltpu.VMEM((KV_W, BLK), F32), pltpu.VMEM((KV_W, BLK), F32),
         pltpu.VMEM((CONV_HALO, CONV_W), F32), pltpu.VMEM((LRU_HALO, LRU_W), F32), pltpu.VMEM((8, LRU_W), F32)],
        [dy, z, z, sink, cw, pv, wa, wx, ycat, hl, hl, uc, probs, psinks], riders)


def _post_fwd(ycat, h0, gmix, w_out, g2, w_up, w_down, name, riders=()):
    t = h0.shape[0]
    tm = _tile(t, POST_TILE)
    nj = D_FF // FF_BLK

    def body(y_ref, h_ref, gm_ref, wo_ref, g2_ref, wu_ref, wd_ref, h1_ref, a_ref, h2_ref, ym_ref, hn_ref):
        ym, _, _ = _group_rms_fwd(y_ref[...], gm_ref[...])
        ym = ym.astype(MX)
        ym_ref[...] = ym
        h1 = h_ref[...] + jnp.dot(ym, wo_ref[...], preferred_element_type=F32)
        h1_ref[...] = h1
        hn, _, _ = _rms_fwd(h1, g2_ref[...])
        hn = hn.astype(MX)
        hn_ref[...] = hn
        for j in range(nj):
            u = jnp.dot(hn, wu_ref[j], preferred_element_type=F32)
            a_ref[:, j * FF_BLK:(j + 1) * FF_BLK] = jnp.square(jnp.maximum(u, 0.0)).astype(MX)
        h2_ref[...] = h1 + jnp.dot(a_ref[...], wd_ref[...], preferred_element_type=F32)

    tile = lambda w: pl.BlockSpec((tm, w), lambda i: (i, 0))
    return _call(
        body, name, (t // tm,),
        [tile(D_MODEL), tile(D_MODEL), _const_spec((1, D_MODEL)), _const_spec((D_MODEL, D_MODEL)),
         _const_spec((1, D_MODEL)), _const_spec((nj, D_MODEL, FF_BLK)), _const_spec((D_FF, D_MODEL))],
        [tile(D_MODEL), tile(D_FF), tile(D_MODEL), tile(D_MODEL), tile(D_MODEL)],
        [_sds((t, D_MODEL), F32), _sds((t, D_FF), MX), _sds((t, D_MODEL), F32), _sds((t, D_MODEL), MX),
         _sds((t, D_MODEL), MX)],
        [], [ycat, h0, gmix, w_out, g2, w_up, w_down], riders)


def _ffn_bwd(dh2, act, h1, g2, w_up_t, w_down, name, riders=()):
    t = h1.shape[0]
    tm = _tile(t, POST_TILE)
    nj = D_FF // FF_BLK

    def body(dh2_ref, a_ref, h1_ref, g2_ref, wut_ref, wd_ref, dh1_ref, dh1b_ref, dh2b_ref, du_ref, dg2_ref):
        @pl.when(pl.program_id(0) == 0)
        def _():
            dg2_ref[...] = jnp.zeros_like(dg2_ref)

        dh2 = dh2_ref[...]
        dh2b = dh2.astype(MX)
        dh2b_ref[...] = dh2b
        for j in range(nj):
            cols = slice(j * FF_BLK, (j + 1) * FF_BLK)
            da = _dot_nt(dh2b, wd_ref[j])
            du_ref[:, cols] = (da * (2.0 * jnp.sqrt(a_ref[:, cols].astype(F32)))).astype(MX)
        dhn = jnp.dot(du_ref[...], wut_ref[...], preferred_element_type=F32)
        _, xh, r = _rms_fwd(h1_ref[...], g2_ref[...])
        dx, dg = _rms_bwd(dhn, xh, r, g2_ref[...])
        dg2_ref[...] += dg
        dh1 = dh2 + dx
        dh1_ref[...] = dh1
        dh1b_ref[...] = dh1.astype(MX)

    tile = lambda w: pl.BlockSpec((tm, w), lambda i: (i, 0))
    return _call(
        body, name, (t // tm,),
        [tile(D_MODEL), tile(D_FF), tile(D_MODEL), _const_spec((1, D_MODEL)),
         _const_spec((D_FF, D_MODEL)), _const_spec((nj, FF_BLK, D_MODEL))],
        [tile(D_MODEL), tile(D_MODEL), tile(D_MODEL), tile(D_FF), _acc_spec((1, D_MODEL))],
        [_sds((t, D_MODEL), F32), _sds((t, D_MODEL), MX), _sds((t, D_MODEL), MX), _sds((t, D_FF), MX),
         _sds((1, D_MODEL), F32)],
        [], [dh2, act, h1, g2, w_up_t, w_down], riders)


def _mix_bwd(dh1, ycat, ym, gmix, w_out, name):
    t = dh1.shape[0]
    tm = _tile(t)
    nk = t // tm
    r = D_MODEL // N_DEV

    def body(dh1_ref, y_ref, ym_ref, gm_ref, wo_ref, dy_ref, dgm_ref, o_ref, o16_ref, acc):
        k = pl.program_id(0)

        @pl.when(k == 0)
        def _():
            dgm_ref[...] = jnp.zeros_like(dgm_ref)
            acc[...] = jnp.zeros_like(acc)

        dh = dh1_ref[...]
        acc[...] += _dot_tn(ym_ref[...], dh)
        dym = _dot_nt(dh, wo_ref[...])
        gm = gm_ref[...]
        _, yh, rr = _group_rms_fwd(y_ref[...], gm)
        outs, dgs = [], []
        for (a, b), rg in zip(_GROUPS, rr):
            dxg, dgg = _rms_bwd(dym[:, a:b], yh[:, a:b], rg, gm[:, a:b])
            outs.append(dxg)
            dgs.append(dgg)
        dy_ref[...] = jnp.concatenate(outs, axis=1)
        dgm_ref[...] += jnp.concatenate(dgs, axis=1)

        @pl.when(k == nk - 1)
        def _():
            for d in range(N_DEV):
                v = acc[d * r:(d + 1) * r, :]
                o_ref[d] = v
                o16_ref[d] = v.astype(o16_ref.dtype)

    tile = pl.BlockSpec((tm, D_MODEL), lambda i: (i, 0))
    slabs = _const_spec((N_DEV, r, D_MODEL))
    (dy, dgm, dw, dw16), _ = _call(
        body, name, (nk,), [tile, tile, tile, _const_spec((1, D_MODEL)), _const_spec((D_MODEL, D_MODEL))],
        [tile, _acc_spec((1, D_MODEL)), slabs, slabs],
        [_sds((t, D_MODEL), F32), _sds((1, D_MODEL), F32), _sds((N_DEV, r, D_MODEL), F32),
         _sds((N_DEV, r, D_MODEL), WIRE)],
        [pltpu.VMEM((D_MODEL, D_MODEL), F32)], [dh1, ycat, ym, gmix, w_out])
    return dy, dgm, (dw, dw16)


def _in_bwd(dz, h0, dh1, g1, w_in_t, after, name):
    t = h0.shape[0]
    tm = _tile(t, STREAM_TILE)

    def body(dz_ref, h_ref, dh1_ref, g_ref, w_ref, after_ref, dh0_ref, dg_ref):
        @pl.when(pl.program_id(0) == 0)
        def _():
            dg_ref[...] = jnp.zeros_like(dg_ref)

        dhn = _dot(dz_ref[...], w_ref[...])
        _, xh, r = _rms_fwd(h_ref[...], g_ref[...])
        dx, dg = _rms_bwd(dhn, xh, r, g_ref[...])
        dg_ref[...] += dg
        dh0_ref[...] = dh1_ref[...] + dx

    tile = lambda w: pl.BlockSpec((tm, w), lambda i: (i, 0))
    (dh0, dg), _ = _call(
        body, name, (t // tm,),
        [tile(IN_W), tile(D_MODEL), tile(D_MODEL), _const_spec((1, D_MODEL)), _const_spec((IN_W, D_MODEL)),
         _const_spec((8, 128))],
        [tile(D_MODEL), _acc_spec((1, D_MODEL))], [_sds((t, D_MODEL), F32), _sds((1, D_MODEL), F32)],
        [], [dz, h0, dh1, g1, w_in_t, after])
    return dh0, dg


def _in_bwd_dw(dz, h0, dh1, g1, w_in_t, name):
    t = h0.shape[0]
    tm = _tile(t)
    nk = t // tm

    def body(dz_ref, h_ref, dh1_ref, g_ref, w_ref, dh0_ref, dg_ref, o_ref, o16_ref, acc):
        k = pl.program_id(0)

        @pl.when(k == 0)
        def _():
            dg_ref[...] = jnp.zeros_like(dg_ref)
            acc[...] = jnp.zeros_like(acc)

        dz_t = dz_ref[...]
        hn, xh, r = _rms_fwd(h_ref[...], g_ref[...])
        acc[...] += _dot_tn(dz_t, hn)
        dhn = _dot(dz_t, w_ref[...])
        dx, dg = _rms_bwd(dhn, xh, r, g_ref[...])
        dg_ref[...] += dg
        dh0_ref[...] = dh1_ref[...] + dx

        @pl.when(k == nk - 1)
        def _():
            for d in range(N_DEV):
                v = acc[d * IN_SHARD:(d + 1) * IN_SHARD, :]
                o_ref[d] = v
                o16_ref[d] = v.astype(o16_ref.dtype)

    tile = lambda w: pl.BlockSpec((tm, w), lambda i: (i, 0))
    slabs = _const_spec((N_DEV, IN_SHARD, D_MODEL))
    (dh0, dg, dw, dw16), _ = _call(
        body, name, (nk,),
        [tile(IN_W), tile(D_MODEL), tile(D_MODEL), _const_spec((1, D_MODEL)), _const_spec((IN_W, D_MODEL))],
        [tile(D_MODEL), _acc_spec((1, D_MODEL)), slabs, slabs],
        [_sds((t, D_MODEL), F32), _sds((1, D_MODEL), F32), _sds((N_DEV, IN_SHARD, D_MODEL), F32),
         _sds((N_DEV, IN_SHARD, D_MODEL), WIRE)],
        [pltpu.VMEM((IN_W, D_MODEL), F32)], [dz, h0, dh1, g1, w_in_t])
    return dh0, dg, (dw, dw16)


def _loss_head(h, gf, target, name):
    t = h.shape[0]
    tm = _tile(t, STREAM_TILE)

    def body(h_ref, g_ref, t_ref, dh_ref, loss_ref, dg_ref):
        @pl.when(pl.program_id(0) == 0)
        def _():
            loss_ref[...] = jnp.zeros_like(loss_ref)
            dg_ref[...] = jnp.zeros_like(dg_ref)

        g = g_ref[...]
        y, xh, r = _rms_fwd(h_ref[...], g)
        err = y - t_ref[...]
        part = 0.5 * jnp.sum(jnp.mean(err * err, axis=-1, keepdims=True), axis=0, keepdims=True)
        loss_ref[...] += jnp.broadcast_to(part, loss_ref.shape)
        dx, dg = _rms_bwd(err * (1.0 / D_MODEL), xh, r, g)
        dg_ref[...] += dg
        dh_ref[...] = dx

    tile = pl.BlockSpec((tm, D_MODEL), lambda i: (i, 0))
    (dh, loss, dg), _ = _call(
        body, name, (t // tm,), [tile, _const_spec((1, D_MODEL)), tile],
        [tile, _acc_spec((1, 128)), _acc_spec((1, D_MODEL))],
        [_sds((t, D_MODEL), F32), _sds((1, 128), F32), _sds((1, D_MODEL), F32)], [], [h, gf, target])
    return dh, loss, dg


def _dw(x, y, name, split, bm, bn):
    t, m = x.shape
    n = y.shape[1]
    tk = _tile(t, DW_TILE)
    nk = t // tk
    if split == "rows":
        assert bn == n
        r, c = m // N_DEV, n
        per = bm // r
        out_block = pl.BlockSpec((per, r, c), lambda a, b, k: (a, 0, 0))
    else:
        assert bm == m
        r, c = m, n // N_DEV
        per = bn // c
        out_block = pl.BlockSpec((per, r, c), lambda a, b, k: (b, 0, 0))

    def body(x_ref, y_ref, o_ref, o16_ref, acc):
        k = pl.program_id(2)

        @pl.when(k == 0)
        def _():
            acc[...] = jnp.zeros_like(acc)

        acc[...] += _dot_tn(x_ref[...], y_ref[...])

        @pl.when(k == nk - 1)
        def _():
            for d in range(per):
                v = acc[d * r:(d + 1) * r, :] if split == "rows" else acc[:, d * c:(d + 1) * c]
                o_ref[d] = v
                o16_ref[d] = v.astype(o16_ref.dtype)

    return pl.pallas_call(
        body, name=name, grid=(m // bm, n // bn, nk),
        in_specs=[pl.BlockSpec((tk, bm), lambda a, b, k: (k, a)), pl.BlockSpec((tk, bn), lambda a, b, k: (k, b))],
        out_specs=[out_block, out_block],
        out_shape=[_sds((N_DEV, r, c), F32), _sds((N_DEV, r, c), WIRE)],
        scratch_shapes=[pltpu.VMEM((bm, bn), F32)],
        compiler_params=pltpu.CompilerParams(dimension_semantics=("arbitrary",) * 3, vmem_limit_bytes=VMEM_LIMIT),
    )(x, y)


def _adamw_math(w, g, m, v):
    m = ADAM_B1 * m + (1.0 - ADAM_B1) * g
    v = ADAM_B2 * v + (1.0 - ADAM_B2) * jnp.square(g)
    m_hat = m / (1.0 - ADAM_B1 ** ADAM_STEP)
    v_hat = v / (1.0 - ADAM_B2 ** ADAM_STEP)
    delta = -ADAM_LR * (m_hat / (jnp.sqrt(v_hat) + ADAM_EPS) + ADAM_WD * w)
    return delta, m, v


def _adamw_shard(g_own, g_recv, dev, w, m, v, after, name):
    _, r, c = w.shape
    br = r
    for cand in (256, 128, 112, 64, 56, 32, 16, 8):
        if r % cand == 0:
            br = cand
            break
    nr = r // br
    own = lambda l: pl.BlockSpec((1, br, c), lambda ll, i, d: (d[0], jnp.where(ll == l, i, (nr - 1) * (1 - l)), 0))
    recv = lambda l: pl.BlockSpec((N_DEV - 1, br, c), lambda ll, i, d: (0, jnp.where(ll == l, i, (nr - 1) * (1 - l)), 0))

    def body(dev_ref, go0, gr0, go1, gr1, w_ref, m_ref, v_ref, after_ref, g_out, d_out, m_out, v_out):
        def update(go_ref, gr_ref):
            g = go_ref[0]
            for j in range(N_DEV - 1):
                g = g + gr_ref[j].astype(F32)
            delta, mn, vn = _adamw_math(w_ref[0], g, m_ref[0], v_ref[0])
            g_out[0] = g
            d_out[0] = delta
            m_out[0] = mn
            v_out[0] = vn

        layer = pl.program_id(0)
        pl.when(layer == 0)(lambda: update(go0, gr0))
        pl.when(layer == 1)(lambda: update(go1, gr1))

    tile = pl.BlockSpec((1, br, c), lambda ll, i, d: (ll, i, 0))
    return pl.pallas_call(
        body, name=name,
        grid_spec=pltpu.PrefetchScalarGridSpec(
            num_scalar_prefetch=1, grid=(2, nr),
            in_specs=[own(0), recv(0), own(1), recv(1), tile, tile, tile,
                      pl.BlockSpec((8, 128), lambda ll, i, d: (0, 0))],
            out_specs=[tile, tile, tile, tile]),
        out_shape=[_sds((2, r, c), F32)] * 4,
        compiler_params=pltpu.CompilerParams(dimension_semantics=("arbitrary",) * 2, vmem_limit_bytes=VMEM_LIMIT),
    )(dev, g_own[0], g_recv[0], g_own[1], g_recv[1], w, m, v, after)


def _adamw_small(gs, ws, ms, vs, name):
    n = len(gs)

    def body(*refs):
        g_refs, w_refs, m_refs, v_refs = (refs[k * n:(k + 1) * n] for k in range(4))
        outs = refs[4 * n:]
        for k in range(n):
            delta, mn, vn = _adamw_math(w_refs[k][...], g_refs[k][...], m_refs[k][...], v_refs[k][...])
            outs[k][...] = delta
            outs[n + k][...] = mn
            outs[2 * n + k][...] = vn

    shapes = [_sds(w.shape, F32) for w in ws]
    res = pl.pallas_call(body, name=name, out_shape=shapes * 3,
                         compiler_params=pltpu.CompilerParams(vmem_limit_bytes=VMEM_LIMIT))(*gs, *ws, *ms, *vs)
    return res[:n], res[n:2 * n], res[2 * n:]


def _sum_parts(own, recv, dev, name):
    def body(dev_ref, own_ref, recv_ref, o_ref):
        me = dev_ref[0]

        def block(d):
            f = jnp.bitwise_xor(me, d)
            return jnp.where(f == 0, own_ref[...], recv_ref[jnp.maximum(f - 1, 0)])

        g = block(0)
        for d in range(1, N_DEV):
            g = g + block(d)
        o_ref[...] = g

    return pl.pallas_call(
        body, name=name,
        grid_spec=pltpu.PrefetchScalarGridSpec(
            num_scalar_prefetch=1, grid=(1,),
            in_specs=[pl.BlockSpec(own.shape, lambda i, d: (0, 0)), pl.BlockSpec(recv.shape, lambda i, d: (0, 0, 0))],
            out_specs=pl.BlockSpec(own.shape, lambda i, d: (0, 0))),
        out_shape=_sds(own.shape, F32))(dev, own, recv)


HBM = pl.BlockSpec(memory_space=pltpu.HBM)
SEM = pl.BlockSpec(memory_space=pltpu.SEMAPHORE)
EFFECT = pltpu.SideEffectType.DATAFLOW_SIDE_EFFECTING


def _direct_copies(srcs, lands, ssem, rsem, scatter):
    x, y, c = _me()
    out = []
    for a in range(len(srcs)):
        for f in range(1, N_DEV):
            px = 1 - x if f & 4 else x
            py = 1 - y if f & 2 else y
            pc = 1 - c if f & 1 else c
            out.append(pltpu.make_async_remote_copy(
                src_ref=srcs[a].at[4 * px + 2 * py + pc] if scatter else srcs[a], dst_ref=lands[a].at[f - 1],
                send_sem=ssem.at[7 * a + f - 1], recv_sem=rsem.at[7 * a + f - 1],
                device_id=(px, py, pc), device_id_type=MESH))
    return out


def _send_start(arrays, scatter, name):
    arrays = list(arrays)
    n = len(arrays)
    lands = [lax.empty((N_DEV - 1,) + (a.shape[1:] if scatter else a.shape), a.dtype) for a in arrays]

    def body(*refs):
        srcs, lnds, ssem, rsem, token = refs[:n], refs[n:2 * n], refs[2 * n], refs[2 * n + 1], refs[-1]
        for cp in _direct_copies(srcs, lnds, ssem, rsem, scatter):
            cp.start()
        token[...] = jnp.zeros_like(token)

    hbm = lambda a: pltpu.HBM(a.shape, a.dtype)
    res = pl.pallas_call(
        body, name=name,
        out_shape=(pltpu.SemaphoreType.DMA((7 * n,)), pltpu.SemaphoreType.DMA((7 * n,)),
                   *[hbm(a) for a in arrays + lands], _sds((8, 128), F32)),
        in_specs=[HBM] * (2 * n),
        out_specs=(SEM, SEM, *[HBM] * (2 * n), pl.BlockSpec(memory_space=pltpu.VMEM)),
        input_output_aliases={i: 2 + i for i in range(2 * n)},
        compiler_params=pltpu.CompilerParams(has_side_effects=EFFECT),
    )(*[pltpu.with_memory_space_constraint(a, pltpu.HBM) for a in arrays + lands])
    return types.SimpleNamespace(ssem=res[0], rsem=res[1], srcs=list(res[2:2 + n]), lands=list(res[2 + n:2 + 2 * n]),
                                 token=res[-1], scatter=scatter)


def _send_wait(h, after, name):
    n = len(h.srcs)

    def body(*refs):
        srcs, lnds, ssem, rsem = refs[:n], refs[n:2 * n], refs[2 * n], refs[2 * n + 1]
        for cp in _direct_copies(srcs, lnds, ssem, rsem, h.scatter):
            cp.wait_send()
            cp.wait_recv()

    hbm = lambda a: pltpu.HBM(a.shape, a.dtype)
    res = pl.pallas_call(
        body, name=name,
        out_shape=tuple(hbm(a) for a in h.srcs + h.lands),
        in_specs=[HBM] * (2 * n) + [SEM, SEM, ANY], out_specs=[HBM] * (2 * n),
        input_output_aliases={i: i for i in range(2 * n)},
        compiler_params=pltpu.CompilerParams(has_side_effects=EFFECT),
    )(*h.srcs, *h.lands, h.ssem, h.rsem, after)
    return list(res[:n]), list(res[n:])


def _block_diag(w):
    out = jnp.zeros((LRU_W, LRU_W), w.dtype)
    for h in range(4):
        out = lax.dynamic_update_slice(out, w[h], (h * 64, h * 64))
    return out


def _layer_params(p, l):
    row = lambda a: a[l].reshape(1, -1)
    sink_rows = jnp.repeat(p["attn_sinks"][l].reshape(4, 2), 2 * BLK, axis=1)
    sink_rows = jnp.concatenate([sink_rows, jnp.zeros((4, 4 * BLK), F32)], axis=0)
    cw = jnp.concatenate([p["conv_dw_w"][l], jnp.zeros((1, CONV_W), F32)], axis=0)
    pv = jnp.concatenate([
        row(p["conv_dw_b"]), row(p["conv_ln_g"]), row(p["conv_ln_b"]), row(p["lru_conv_b"]), row(p["lru_ba"]),
        row(p["lru_bx"]), row(p["lru_lambda"]), jnp.zeros((1, LRU_W), F32), p["lru_conv_w"][l],
        jnp.zeros((4, LRU_W), F32)], axis=0)
    return dict(
        g1=row(p["norm1"]), sink=sink_rows, cw=cw, pv=pv,
        wa=_block_diag(p["lru_wa"][l]).astype(MX), wx=_block_diag(p["lru_wx"][l]).astype(MX),
        gmix=row(p["mix_norm"]), g2=row(p["norm2"]))


_SMALL = ["norm1", "attn_sinks", "conv_dw_w", "conv_dw_b", "conv_ln_g", "conv_ln_b", "lru_conv_w", "lru_conv_b",
          "lru_wa", "lru_ba", "lru_wx", "lru_bx", "lru_lambda", "mix_norm", "norm2"]
_BIG = ["w_in", "w_out", "w_up", "w_down"]
_WEIGHTS = ["norm1", "w_in", "attn_sinks", "conv_dw_w", "conv_dw_b", "conv_ln_g", "conv_ln_b", "lru_conv_w",
            "lru_conv_b", "lru_wa", "lru_ba", "lru_wx", "lru_bx", "lru_lambda", "mix_norm", "w_out", "norm2", "w_up",
            "w_down", "final_norm"]


def kernel(x, norm1, w_in, attn_sinks, conv_dw_w, conv_dw_b, conv_ln_g, conv_ln_b, lru_conv_w, lru_conv_b, lru_wa, lru_ba, lru_wx, lru_bx, lru_lambda, mix_norm, w_out, norm2, w_up, w_down, final_norm, loss_target, m_norm1, m_w_in, m_attn_sinks, m_conv_dw_w, m_conv_dw_b, m_conv_ln_g, m_conv_ln_b, m_lru_conv_w, m_lru_conv_b, m_lru_wa, m_lru_ba, m_lru_wx, m_lru_bx, m_lru_lambda, m_mix_norm, m_w_out, m_norm2, m_w_up, m_w_down, m_final_norm, v_norm1, v_w_in, v_attn_sinks, v_conv_dw_w, v_conv_dw_b, v_conv_ln_g, v_conv_ln_b, v_lru_conv_w, v_lru_conv_b, v_lru_wa, v_lru_ba, v_lru_wx, v_lru_bx, v_lru_lambda, v_mix_norm, v_w_out, v_norm2, v_w_up, v_w_down, v_final_norm):
    w = dict(norm1=norm1, w_in=w_in, attn_sinks=attn_sinks, conv_dw_w=conv_dw_w, conv_dw_b=conv_dw_b,
             conv_ln_g=conv_ln_g, conv_ln_b=conv_ln_b, lru_conv_w=lru_conv_w, lru_conv_b=lru_conv_b, lru_wa=lru_wa,
             lru_ba=lru_ba, lru_wx=lru_wx, lru_bx=lru_bx, lru_lambda=lru_lambda, mix_norm=mix_norm, w_out=w_out,
             norm2=norm2, w_up=w_up, w_down=w_down, final_norm=final_norm)
    m = dict(norm1=m_norm1, w_in=m_w_in, attn_sinks=m_attn_sinks, conv_dw_w=m_conv_dw_w, conv_dw_b=m_conv_dw_b,
             conv_ln_g=m_conv_ln_g, conv_ln_b=m_conv_ln_b, lru_conv_w=m_lru_conv_w, lru_conv_b=m_lru_conv_b,
             lru_wa=m_lru_wa, lru_ba=m_lru_ba, lru_wx=m_lru_wx, lru_bx=m_lru_bx, lru_lambda=m_lru_lambda,
             mix_norm=m_mix_norm, w_out=m_w_out, norm2=m_norm2, w_up=m_w_up, w_down=m_w_down, final_norm=m_final_norm)
    v = dict(norm1=v_norm1, w_in=v_w_in, attn_sinks=v_attn_sinks, conv_dw_w=v_conv_dw_w, conv_dw_b=v_conv_dw_b,
             conv_ln_g=v_conv_ln_g, conv_ln_b=v_conv_ln_b, lru_conv_w=v_lru_conv_w, lru_conv_b=v_lru_conv_b,
             lru_wa=v_lru_wa, lru_ba=v_lru_ba, lru_wx=v_lru_wx, lru_bx=v_lru_bx, lru_lambda=v_lru_lambda,
             mix_norm=v_mix_norm, w_out=v_w_out, norm2=v_norm2, w_up=v_w_up, w_down=v_w_down, final_norm=v_final_norm)
    depth = w_in.shape[0]
    xi, yi, ci = _me()
    dev = (4 * xi + 2 * yi + ci).astype(jnp.int32)
    dev1 = dev.reshape(1)
    tr = lambda a: jnp.swapaxes(a, 1, 2)
    w_t, m_t, v_t = tr(w_in), tr(m_w_in), tr(v_w_in)
    wb = {n: w[n].astype(MX) for n in _BIG if n != "w_in"}
    wb["w_in"] = w_t.astype(MX)
    layer_shards = lambda l: [wb["w_out"][l], wb["w_up"][l], wb["w_down"][l]]

    _, ((g_in0, g_cw, g_lcw),) = _call(None, "gather_first", None, [], [], [], [], [],
                                        [_gather_rider([wb["w_in"][0], conv_dw_w, lru_conv_w])])
    cols = lambda g: jnp.moveaxis(g, 0, -2).reshape(g.shape[1:-1] + (N_DEV * g.shape[-1],))
    p = dict(w)
    p["conv_dw_w"] = cols(g_cw)
    p["lru_conv_w"] = cols(g_lcw)
    lp = [_layer_params(p, l) for l in range(depth)]

    gathered = [dict(w_in=g_in0.reshape(IN_W, D_MODEL)), dict()]
    saved = []
    h = x[0]
    for l in range(depth):
        q, gw = lp[l], gathered[l]
        z, hn1 = _ln_in(h, q["g1"], gw["w_in"], False, f"ln_in{l}")
        riders = [_gather_rider(layer_shards(0))] if l == 0 else []
        (ycat, hl, uc, probs, psinks), got = _mixer_fwd(z, q["sink"], q["cw"], q["pv"], q["wa"], q["wx"],
                                                        f"mixer_fwd{l}", riders)
        if l == 0:
            gw["w_out"], gw["w_up"], gw["w_down"] = got[0]
            gw["w_out"] = gw["w_out"].reshape(D_MODEL, D_MODEL)
        riders = [_gather_rider([wb["w_in"][1]] + layer_shards(1))] if l == 0 else []
        (h1, act, h2, ym, hn2), got = _post_fwd(ycat, h, q["gmix"], gw["w_out"], q["g2"], gw["w_up"],
                                                gw["w_down"].reshape(D_FF, D_MODEL), f"post_fwd{l}", riders)
        if l == 0:
            nxt = gathered[1]
            nxt["w_in"], nxt["w_out"], nxt["w_up"], nxt["w_down"] = got[0]
            nxt["w_in"] = nxt["w_in"].reshape(IN_W, D_MODEL)
            nxt["w_out"] = nxt["w_out"].reshape(D_MODEL, D_MODEL)
        saved.append(dict(h0=h, z=z, hn1=hn1, ycat=ycat, hl=hl, uc=uc, probs=probs, psinks=psinks, h1=h1, act=act,
                          ym=ym, hn2=hn2))
        h = h2
    dh, loss, dgf = _loss_head(h, final_norm.reshape(1, -1), loss_target[0], "loss_head")

    grads = [None] * depth
    big = {n: [None] * depth for n in _BIG}
    pending = []

    def send_pending():
        riders = [_scatter_rider([item[3] for item in pending])] if pending else []
        return riders, list(pending)

    def record(sent, got):
        for item, recv in zip(sent, got[0] if sent else []):
            big[item[0]][item[1]] = (item[2], recv)
        del pending[:len(sent)]

    for l in reversed(range(depth)):
        q, s, gw = lp[l], saved[l], gathered[l]
        riders, sent = send_pending()
        w_up_t = jnp.swapaxes(gw["w_up"], 1, 2).reshape(D_FF, D_MODEL)
        (dh1, dh1b, dhb, du, dg2), got = _ffn_bwd(dh, s["act"], s["h1"], q["g2"], w_up_t, gw["w_down"],
                                                  f"ffn_bwd{l}", riders)
        record(sent, got)
        dycat, dgm, d_wout = _mix_bwd(dh1b, s["ycat"], s["ym"], q["gmix"], gw["w_out"], f"mix_bwd{l}")
        pending.append(("w_down", l) + tuple(_dw(s["act"], dhb, f"dw_down{l}", "rows", 2048, D_MODEL)))
        pending.append(("w_up", l) + tuple(_dw(s["hn2"], du, f"dw_up{l}", "cols", D_MODEL, 2048)))
        pending.append(("w_out", l) + tuple(d_wout))
        riders, sent = send_pending()
        (dz, dsink, dcw, dpv, dwa, dwx), got = _mixer_bwd(
            dycat, s["z"], s["ycat"], s["hl"], s["uc"], s["probs"], s["psinks"], q["sink"], q["cw"], q["pv"], q["wa"],
            q["wx"], f"mixer_bwd{l}", riders)
        record(sent, got)
        dh, dg1, d_win = _in_bwd_dw(dz, s["h0"], dh1, q["g1"], gw["w_in"], f"in_bwd{l}")
        if l > 0:
            pending.append(("w_in", l) + tuple(d_win))
        else:
            win_sends = _send_start([d_win[1]], True, "scatter_w_in0_start")
        grads[l] = dict(dg1=dg1, dsink=dsink, dcw=dcw, dpv=dpv, dwa=dwa, dwx=dwx, dgm=dgm, dg2=dg2)

    acc = {k: jnp.stack([grads[l][k] for l in range(depth)]) for k in grads[0]}
    dpv = acc["dpv"]
    unblock = lambda a: jnp.concatenate([a[:, h * 64:(h + 1) * 64, h * 64:(h + 1) * 64] for h in range(4)], axis=1)
    by_name = dict(
        norm1=acc["dg1"][:, 0], attn_sinks=jnp.stack([acc["dsink"][:, 0:4, 0], acc["dsink"][:, 0:4, 2 * BLK]],
                                                     axis=2).reshape(depth, 8),
        conv_dw_w=acc["dcw"][:, 0:CONV_K], conv_dw_b=dpv[:, R_CONV_B], conv_ln_g=dpv[:, R_LN_G],
        conv_ln_b=dpv[:, R_LN_B], lru_conv_w=dpv[:, R_LCW:R_LCW + LRU_K], lru_conv_b=dpv[:, R_LCONV_B],
        lru_wa=unblock(acc["dwa"]), lru_ba=dpv[:, R_BA].reshape(depth, 4, 64), lru_wx=unblock(acc["dwx"]),
        lru_bx=dpv[:, R_BX].reshape(depth, 4, 64), lru_lambda=dpv[:, R_LAM], mix_norm=acc["dgm"][:, 0],
        norm2=acc["dg2"][:, 0])
    small = [by_name[n] for n in _SMALL] + [dgf, loss[:, 0:1]]

    def as_rows(a):
        flat = a.reshape(-1)
        pad = (-flat.size) % 1024
        if pad:
            flat = jnp.concatenate([flat, jnp.zeros((pad,), F32)])
        return flat.reshape(-1, 128)

    pieces = [as_rows(a) for a in small]
    packed = jnp.concatenate(pieces, axis=0)
    small_sends = _send_start([packed], False, "bcast_small_start")

    out = {}
    shard_update = lambda n, wmv, after: list(_adamw_shard(
        [big[n][l][0] for l in range(depth)], [big[n][l][1] for l in range(depth)], dev1, *wmv, after, f"adamw_{n}"))
    for n in ("w_out", "w_up", "w_down"):
        out[n] = shard_update(n, (w[n], m[n], v[n]), small_sends.token)
    _, (win_recv,) = _send_wait(win_sends, out["w_down"][1], "scatter_w_in0_wait")
    big["w_in"][0] = (d_win[0], win_recv)
    out["w_in"] = [tr(a) for a in shard_update("w_in", (w_t, m_t, v_t), jnp.zeros((8, 128), F32))]
    (packed,), (small_recv,) = _send_wait(small_sends, out["w_in"][1], "bcast_small_wait")
    summed = _sum_parts(packed, small_recv, dev1, "sum_small_grads")
    small_sums, row = [], 0
    for a, piece in zip(small, pieces):
        got = summed[row:row + piece.shape[0]]
        small_sums.append(got.reshape(a.shape) if a.size == piece.size else got.reshape(-1)[:a.size].reshape(a.shape))
        row += piece.shape[0]
    shard = lambda a: lax.dynamic_slice_in_dim(a, dev * (a.shape[-1] // N_DEV), a.shape[-1] // N_DEV, axis=a.ndim - 1)
    flat = {"lru_wa": (depth, LRU_W, 64), "lru_wx": (depth, LRU_W, 64), "final_norm": (1, D_MODEL)}
    gs, ws, ms, vs = [], [], [], []
    for n, g in zip(_SMALL + ["final_norm"], small_sums[:-1]):
        shp = flat.get(n, w[n].shape)
        gs.append((shard(g) if n in ("conv_dw_w", "lru_conv_w") else g).reshape(shp))
        ws.append(w[n].reshape(shp))
        ms.append(m[n].reshape(shp))
        vs.append(v[n].reshape(shp))
    sd, sm, sv = _adamw_small(gs, ws, ms, vs, "adamw_small")
    for j, n in enumerate(_SMALL + ["final_norm"]):
        out[n] = [a.reshape(w[n].shape) for a in (gs[j], sd[j], sm[j], sv[j])]
    loss_total = small_sums[-1][0, 0]

    result = [loss_total, dh[None]]
    for j in range(4):
        result += [out[n][j] for n in _WEIGHTS]
    return tuple(result)
```

```python
import types

import jax
import jax.numpy as jnp
from jax import lax
from jax.experimental import pallas as pl
from jax.experimental.pallas import tpu as pltpu

F32 = jnp.float32
MX = jnp.bfloat16
WIRE = jnp.bfloat16

D_MODEL = 1024
HEAD_DIM = 64
ATTN_W = 512
KV_W = 128
BLK = 128
CONV_W = 256
CONV_K = 31
LRU_W = 256
LRU_K = 4
LRU_C = 8.0
IN_W = 1792
D_FF = 4096
FF_BLK = 512
N_DEV = 8
IN_SHARD = IN_W // N_DEV
RMS_EPS = 1e-6
LN_EPS = 1e-5
MASK_VALUE = -1e30
SCALE = HEAD_DIM ** -0.5
CONV_HALO = 32
LRU_HALO = 8
CONV_CHUNK = 64
POST_TILE = 512
STREAM_TILE = 1024
DW_TILE = 1024
Q0, K0, V0, CV0, CG0, RX0, RG0 = 0, 512, 640, 768, 1024, 1280, 1536
R_CONV_B, R_LN_G, R_LN_B, R_LCONV_B, R_BA, R_BX, R_LAM, R_LCW = 0, 1, 2, 3, 4, 5, 6, 8

ADAM_LR, ADAM_B1, ADAM_B2, ADAM_EPS, ADAM_WD, ADAM_STEP = 0.001, 0.9, 0.999, 1e-08, 0.01, 10

VMEM_LIMIT = 56 * 1024 * 1024
MESH = pl.DeviceIdType.MESH
ANY = pl.BlockSpec(memory_space=pl.ANY)


def _tile(t, cap=512):
    return min(cap, t)


def _dot(a, b):
    return jnp.dot(a.astype(MX), b.astype(MX), preferred_element_type=F32)


def _dot_nt(a, b):
    return lax.dot_general(a.astype(MX), b.astype(MX), (((1,), (1,)), ((), ())), preferred_element_type=F32)


def _dot_tn(a, b):
    return lax.dot_general(a.astype(MX), b.astype(MX), (((0,), (0,)), ((), ())), preferred_element_type=F32)


def _const_spec(shape):
    nd = len(shape)
    return pl.BlockSpec(shape, lambda *_: (0,) * nd, pipeline_mode=pl.Buffered(1))


def _acc_spec(shape):
    nd = len(shape)
    return pl.BlockSpec(shape, lambda *_: (0,) * nd)


def _sds(shape, dtype):
    return jax.ShapeDtypeStruct(shape, dtype)


def _sigmoid(x):
    return jax.nn.sigmoid(x)


def _rms_fwd(x, g):
    r = lax.rsqrt(jnp.mean(x * x, axis=-1, keepdims=True) + RMS_EPS)
    xh = x * r
    return xh * g, xh, r


def _rms_bwd(dy, xh, r, g):
    t = dy * g
    dx = r * (t - xh * jnp.mean(t * xh, axis=-1, keepdims=True))
    return dx, jnp.sum(dy * xh, axis=0, keepdims=True)


_GROUPS = ((0, 512), (512, 768), (768, 1024))


def _group_rms_fwd(y, g):
    parts = [_rms_fwd(y[:, a:b], g[:, a:b]) for a, b in _GROUPS]
    return (jnp.concatenate([p[0] for p in parts], axis=1),
            jnp.concatenate([p[1] for p in parts], axis=1),
            [p[2] for p in parts])


def _gelu(x):
    c = 0.7978845608028654
    u = c * (x + 0.044715 * x * x * x)
    th = jnp.tanh(u)
    val = 0.5 * x * (1.0 + th)
    grad = 0.5 * (1.0 + th) + 0.5 * x * (1.0 - th * th) * c * (1.0 + 3.0 * 0.044715 * x * x)
    return val, grad


def _neg_expm1(x):
    series = -x * (1.0 + x * (0.5 + x * (1.0 / 6.0 + x * (1.0 / 24.0))))
    return jnp.where(x > -0.02, series, 1.0 - jnp.exp(x))


def _me():
    return lax.axis_index("x"), lax.axis_index("y"), lax.axis_index("c")


def _gather_rider(arrays):
    arrays = list(arrays)
    n = len(arrays)

    def plan(ins, outs, sems):
        ssem, rsem, lsem = sems
        x, y, c = _me()
        chips = [(1 - x, y), (x, 1 - y), (1 - x, 1 - y)]

        def copy(a, k, block, to, own=False):
            dst = outs[a].at[4 * block[0] + 2 * block[1] + block[2]]
            return pltpu.make_async_remote_copy(
                src_ref=ins[a] if own else dst, dst_ref=dst, send_sem=ssem.at[7 * a + k],
                recv_sem=rsem.at[7 * a + k], device_id=to, device_id_type=MESH)

        return x, y, c, chips, copy, lsem

    def start(ins, outs, sems):
        x, y, c, chips, copy, lsem = plan(ins, outs, sems)
        for a in range(n):
            pltpu.make_async_copy(ins[a], outs[a].at[4 * x + 2 * y + c], lsem.at[a]).start()
            copy(a, 0, (x, y, c), (x, y, 1 - c), own=True).start()
            for j, chip in enumerate(chips):
                copy(a, 1 + j, (x, y, c), (*chip, c), own=True).start()

    def mid(ins, outs, sems):
        x, y, c, chips, copy, _ = plan(ins, outs, sems)
        for a in range(n):
            for j, chip in enumerate(chips):
                copy(a, 1 + j, (*chip, c), (x, y, c)).wait_recv()
                copy(a, 4 + j, (*chip, c), (x, y, 1 - c)).start()

    def finish(ins, outs, sems):
        x, y, c, chips, copy, lsem = plan(ins, outs, sems)
        for a in range(n):
            copy(a, 0, (x, y, 1 - c), (x, y, c)).wait_recv()
            for j, chip in enumerate(chips):
                copy(a, 4 + j, (*chip, 1 - c), (x, y, c)).wait_recv()
        for a in range(n):
            copy(a, 0, (x, y, c), (x, y, 1 - c), own=True).wait_send()
            for j, chip in enumerate(chips):
                copy(a, 1 + j, (x, y, c), (*chip, c), own=True).wait_send()
                copy(a, 4 + j, (*chip, c), (x, y, 1 - c)).wait_send()
            pltpu.make_async_copy(ins[a], outs[a].at[4 * x + 2 * y + c], lsem.at[a]).wait()

    return types.SimpleNamespace(
        arrays=arrays, out_shape=[_sds((N_DEV,) + a.shape, a.dtype) for a in arrays],
        scratch=[pltpu.SemaphoreType.DMA((7 * n,)), pltpu.SemaphoreType.DMA((7 * n,)), pltpu.SemaphoreType.DMA((n,))],
        start=start, mid=mid, finish=finish)


def _scatter_rider(arrays):
    arrays = list(arrays)
    n = len(arrays)

    def copies(ins, outs, sems):
        ssem, rsem = sems
        x, y, c = _me()
        out = []
        for a in range(n):
            for f in range(1, N_DEV):
                px = 1 - x if f & 4 else x
                py = 1 - y if f & 2 else y
                pc = 1 - c if f & 1 else c
                out.append(pltpu.make_async_remote_copy(
                    src_ref=ins[a].at[4 * px + 2 * py + pc], dst_ref=outs[a].at[f - 1], send_sem=ssem.at[7 * a + f - 1],
                    recv_sem=rsem.at[7 * a + f - 1], device_id=(px, py, pc), device_id_type=MESH))
        return out

    def start(ins, outs, sems):
        for cp in copies(ins, outs, sems):
            cp.start()

    def finish(ins, outs, sems):
        for cp in copies(ins, outs, sems):
            cp.wait()

    return types.SimpleNamespace(
        arrays=arrays, out_shape=[_sds((N_DEV - 1,) + a.shape[1:], a.dtype) for a in arrays],
        scratch=[pltpu.SemaphoreType.DMA((7 * n,)), pltpu.SemaphoreType.DMA((7 * n,))],
        start=start, mid=None, finish=finish)


def _call(body, name, grid, in_specs, out_specs, out_shape, scratch, operands, riders=()):
    n_in, n_out, n_scr = len(operands), len(out_shape), len(scratch)
    nsteps = grid[0] if grid else 1
    sizes = [(len(r.arrays), len(r.out_shape), len(r.scratch)) for r in riders]

    def wrapped(*refs):
        pos = n_in
        r_ins = []
        for ri, _, _ in sizes:
            r_ins.append(refs[pos:pos + ri])
            pos += ri
        outs = refs[pos:pos + n_out]
        pos += n_out
        r_outs = []
        for _, ro, _ in sizes:
            r_outs.append(refs[pos:pos + ro])
            pos += ro
        scr = refs[pos:pos + n_scr]
        pos += n_scr
        r_sems = []
        for _, _, rs in sizes:
            r_sems.append(refs[pos:pos + rs])
            pos += rs
        step = pl.program_id(0) if grid else 0

        def at(s, fn):
            if grid:
                pl.when(step == s)(fn)
            else:
                fn()

        for r, a, b, c in zip(riders, r_ins, r_outs, r_sems):
            at(0, lambda r=r, a=a, b=b, c=c: r.start(a, b, c))
        for r, a, b, c in zip(riders, r_ins, r_outs, r_sems):
            if r.mid is not None:
                at((7 * nsteps) // 8, lambda r=r, a=a, b=b, c=c: r.mid(a, b, c))
        if body is not None:
            body(*refs[:n_in], *outs, *scr)
        for r, a, b, c in zip(riders, r_ins, r_outs, r_sems):
            at(nsteps - 1, lambda r=r, a=a, b=b, c=c: r.finish(a, b, c))

    r_arrays = [a for r in riders for a in r.arrays]
    r_shapes = [s for r in riders for s in r.out_shape]
    kwargs = {}
    if grid:
        kwargs = dict(grid=grid, compiler_params=pltpu.CompilerParams(
            dimension_semantics=("arbitrary",) * len(grid), vmem_limit_bytes=VMEM_LIMIT))
    res = pl.pallas_call(
        wrapped, name=name,
        in_specs=list(in_specs) + [ANY] * len(r_arrays),
        out_specs=list(out_specs) + [ANY] * len(r_shapes),
        out_shape=list(out_shape) + r_shapes,
        scratch_shapes=list(scratch) + [s for r in riders for s in r.scratch],
        **kwargs,
    )(*operands, *r_arrays)
    host, rest = res[:n_out], res[n_out:]
    r_res = []
    for _, ro, _ in sizes:
        r_res.append(rest[:ro])
        rest = rest[ro:]
    return host, r_res


def _ln_in(h, g1, w_in_t, keep_hn, name):
    t = h.shape[0]
    tm = _tile(t, STREAM_TILE)

    def body(h_ref, g_ref, w_ref, z_ref, *hn_ref):
        y, _, _ = _rms_fwd(h_ref[...], g_ref[...])
        hn = y.astype(MX)
        if keep_hn:
            hn_ref[0][...] = hn
        z_ref[...] = _dot_nt(hn, w_ref[...])

    tile = lambda w: pl.BlockSpec((tm, w), lambda i: (i, 0))
    outs, _ = _call(
        body, name, (t // tm,),
        [tile(D_MODEL), _const_spec((1, D_MODEL)), _const_spec((IN_W, D_MODEL))],
        [tile(IN_W)] + [tile(D_MODEL)] * keep_hn,
        [_sds((t, IN_W), F32)] + [_sds((t, D_MODEL), MX)] * keep_hn, [], [h, g1, w_in_t])
    return outs[0], (outs[1] if keep_hn else None)


def _band2(kb, g):
    lo = lax.broadcasted_iota(jnp.int32, kb.shape, 1) < HEAD_DIM
    kr = pltpu.roll(kb, HEAD_DIM, 1)
    if g == 0:
        top, bot = jnp.where(lo, kb, 0.0), jnp.where(lo, 0.0, kr)
    else:
        top, bot = jnp.where(lo, kr, 0.0), jnp.where(lo, 0.0, kb)
    return jnp.concatenate([top, bot], axis=0)


def _attn_operands(z_ref, zh_ref, b):
    rows = slice(b * BLK, (b + 1) * BLK)
    prev = zh_ref if b == 0 else z_ref
    prow = slice(0, BLK) if b == 0 else slice((b - 1) * BLK, b * BLK)
    kb = jnp.concatenate([prev[prow, K0:K0 + KV_W], z_ref[rows, K0:K0 + KV_W]], axis=0)
    vb = jnp.concatenate([prev[prow, V0:V0 + KV_W], z_ref[rows, V0:V0 + KV_W]], axis=0)
    k2 = [_band2(kb, g) for g in range(2)]
    v2 = [_band2(vb, g) for g in range(2)]
    q2 = [jnp.concatenate([z_ref[rows, (2 * g) * BLK:(2 * g + 1) * BLK], z_ref[rows, (2 * g + 1) * BLK:(2 * g + 2) * BLK]],
                          axis=0) for g in range(2)]
    return q2, k2, v2


def _attn_block(z_ref, zh_ref, sink_ref, b, first):
    q2, k2, v2 = _attn_operands(z_ref, zh_ref, b)
    rr = lax.broadcasted_iota(jnp.int32, (4 * BLK, 2 * BLK), 0) & (BLK - 1)
    cc = lax.broadcasted_iota(jnp.int32, (4 * BLK, 2 * BLK), 1)
    first_block = jnp.logical_and(first, b == 0).astype(jnp.int32)
    mask = jnp.logical_and(jnp.logical_and(cc > rr, cc <= rr + BLK), cc >= BLK * first_block)
    s = jnp.concatenate([_dot_nt(q2[g], k2[g]) for g in range(2)], axis=0) * SCALE
    w = 2 * BLK
    out, psink = [], []
    for hh in range(2):
        sh = jnp.where(mask, s[:, hh * w:(hh + 1) * w], MASK_VALUE)
        sk = jnp.concatenate([jnp.broadcast_to(sink_ref[p:p + 1, hh * w:hh * w + 1], (BLK, 1)) for p in range(4)], axis=0)
        m = jnp.maximum(jnp.max(sh, axis=1, keepdims=True), sk)
        p = jnp.exp(sh - m)
        es = jnp.exp(sk - m)
        inv = 1.0 / (jnp.sum(p, axis=1, keepdims=True) + es)
        out.append(p * inv)
        psink.append(es * inv)
    return v2, jnp.concatenate(out, axis=1), psink


def _scan_steps(a, b, n, span, reverse):
    pos = lax.broadcasted_iota(jnp.int32, a.shape, 0) & (span - 1)
    d = 1
    while d < span:
        keep = pos < span - d if reverse else pos >= d
        shift = n - d if reverse else d
        a_sh = jnp.where(keep, pltpu.roll(a, shift, 0), 1.0)
        b_sh = jnp.where(keep, pltpu.roll(b, shift, 0), 0.0)
        b = a * b_sh + b
        a = a * a_sh
        d *= 2
    return a, b


def _scan(a, b, tm, reverse):
    return _scan_steps(a, b, tm, tm, reverse)


def _shifted_copies(ext, shifts, tm):
    rows = tm + CONV_HALO - 8
    for r in range(1, 8):
        shifts[r - 1, 0:rows, :] = ext[pl.ds(r, rows), :]


def _tap(ext, shifts, off, r0, n):
    a, r = divmod(off, 8)
    lo = 8 * a + r0
    if r == 0:
        return ext[lo:lo + n, :]
    return shifts[r - 1, lo:lo + n, :]


def _glu_fill(z_ref, zh_ref, uext, ush, first, tm, sg_out=None):
    cv = z_ref[:, CV0:CV0 + CONV_W]
    sg = _sigmoid(z_ref[:, CG0:CG0 + CONV_W])
    if sg_out is not None:
        sg_out[...] = sg
    hrow = BLK - CONV_HALO
    uh = zh_ref[hrow:BLK, CV0:CV0 + CONV_W] * _sigmoid(zh_ref[hrow:BLK, CG0:CG0 + CONV_W])
    uext[0:CONV_HALO, :] = jnp.where(first, 0.0, uh)
    uext[CONV_HALO:CONV_HALO + tm, :] = cv * sg
    _shifted_copies(uext, ush, tm)


def _conv_taps(cw_ref, pv_ref, uext, ush, out_ref, tm):
    for r0 in range(0, tm, CONV_CHUNK):
        acc = jnp.broadcast_to(pv_ref[R_CONV_B:R_CONV_B + 1, :], (CONV_CHUNK, CONV_W))
        for k in range(CONV_K):
            acc = acc + cw_ref[k:k + 1, :] * _tap(uext, ush, CONV_HALO - (CONV_K - 1) + k, r0, CONV_CHUNK)
        out_ref[r0:r0 + CONV_CHUNK, :] = acc


def _ln_silu(uc, pv_ref):
    mu = jnp.mean(uc, axis=-1, keepdims=True)
    xc = uc - mu
    rs = lax.rsqrt(jnp.mean(xc * xc, axis=-1, keepdims=True) + LN_EPS)
    xh = xc * rs
    ln = xh * pv_ref[R_LN_G:R_LN_G + 1, :] + pv_ref[R_LN_B:R_LN_B + 1, :]
    sg = _sigmoid(ln)
    return xh, rs, ln, sg


def _lru_gates(z_ref, zh_ref, pv_ref, wa_ref, wx_ref, rxext, first, tm):
    rxext[0:LRU_HALO, :] = jnp.where(first, 0.0, zh_ref[BLK - LRU_HALO:BLK, RX0:RX0 + LRU_W])
    rxext[LRU_HALO:LRU_HALO + tm, :] = z_ref[:, RX0:RX0 + LRU_W]
    xc = jnp.broadcast_to(pv_ref[R_LCONV_B:R_LCONV_B + 1, :], (tm, LRU_W))
    for k in range(LRU_K):
        xc = xc + pv_ref[R_LCW + k:R_LCW + k + 1, :] * rxext[pl.ds(LRU_HALO - (LRU_K - 1) + k, tm), :]
    r = _sigmoid(_dot(xc, wa_ref[...]) + pv_ref[R_BA:R_BA + 1, :])
    ig = _sigmoid(_dot(xc, wx_ref[...]) + pv_ref[R_BX:R_BX + 1, :])
    lam = pv_ref[R_LAM:R_LAM + 1, :]
    sp = jnp.log1p(jnp.exp(-lam))
    la = (-LRU_C * r) * sp
    a = jnp.exp(la)
    mult = jnp.sqrt(_neg_expm1(2.0 * la))
    return xc, r, ig, sp, la, a, mult


def _mixer_in_specs(tm, tile_of):
    hb = tm // BLK
    return [
        pl.BlockSpec((tm, IN_W), lambda i: (tile_of(i), 0)),
        pl.BlockSpec((BLK, IN_W), lambda i: (jnp.maximum(tile_of(i) * hb - 1, 0), 0)),
        _const_spec((8, 4 * BLK)),
        _const_spec((32, CONV_W)),
        _const_spec((16, CONV_W)),
        _const_spec((LRU_W, LRU_W)),
        _const_spec((LRU_W, LRU_W)),
    ]


def _mixer_fwd(z, sink, cw, pv, wa, wx, name, riders=()):
    t = z.shape[0]
    tm = _tile(t)
    nb = tm // BLK

    def body(z_ref, zh_ref, sink_ref, cw_ref, pv_ref, wa_ref, wx_ref, y_ref, hl_ref, uc_ref, p_ref, ps_ref,
             uext, ush, rxext, hcar):
        i = pl.program_id(0)
        first = i == 0

        @pl.when(first)
        def _():
            hcar[...] = jnp.zeros_like(hcar)

        lo = lax.broadcasted_iota(jnp.int32, (4 * BLK, BLK), 1) < HEAD_DIM
        for b in range(nb):
            rows = slice(b * BLK, (b + 1) * BLK)
            v2, prob, psink = _attn_block(z_ref, zh_ref, sink_ref, b, first)
            prob = prob.astype(MX)
            p_ref[b] = prob
            ps_ref[b] = jnp.where(lo, psink[0], psink[1])
            for g in range(2):
                o = _dot(prob[2 * g * BLK:(2 * g + 2) * BLK], v2[g])
                y_ref[rows, (2 * g) * BLK:(2 * g + 1) * BLK] = o[0:BLK]
                y_ref[rows, (2 * g + 1) * BLK:(2 * g + 2) * BLK] = o[BLK:2 * BLK]
        _glu_fill(z_ref, zh_ref, uext, ush, first, tm)
        _conv_taps(cw_ref, pv_ref, uext, ush, uc_ref, tm)
        _, _, ln, sg = _ln_silu(uc_ref[...], pv_ref)
        y_ref[:, ATTN_W:ATTN_W + CONV_W] = ln * sg
        xc, _, ig, _, _, a, mult = _lru_gates(z_ref, zh_ref, pv_ref, wa_ref, wx_ref, rxext, first, tm)
        acum, h = _scan(a, mult * (ig * xc), tm, reverse=False)
        h = h + acum * hcar[0:1, :]
        hl_ref[...] = h
        hcar[0:1, :] = h[tm - 1:tm, :]
        gl, _ = _gelu(z_ref[:, RG0:RG0 + LRU_W])
        y_ref[:, ATTN_W + CONV_W:ATTN_W + CONV_W + LRU_W] = h * gl

    tile = lambda w: pl.BlockSpec((tm, w), lambda i: (i, 0))
    return _call(
        body, name, (t // tm,), _mixer_in_specs(tm, lambda i: i),
        [tile(D_MODEL), tile(LRU_W), tile(CONV_W), pl.BlockSpec((nb, 4 * BLK, 4 * BLK), lambda i: (i, 0, 0)),
         pl.BlockSpec((nb, 4 * BLK, BLK), lambda i: (i, 0, 0))],
        [_sds((t, D_MODEL), F32), _sds((t, LRU_W), F32), _sds((t, CONV_W), F32),
         _sds((t // BLK, 4 * BLK, 4 * BLK), MX), _sds((t // BLK, 4 * BLK, BLK), F32)],
        [pltpu.VMEM((tm + CONV_HALO, CONV_W), F32), pltpu.VMEM((7, tm + CONV_HALO - 8, CONV_W), F32),
         pltpu.VMEM((tm + LRU_HALO, LRU_W), F32), pltpu.VMEM((8, LRU_W), F32)],
        [z, z, sink, cw, pv, wa, wx], riders)


def _mixer_bwd(dy, z, ycat, hl, uc, probs, psinks, sink, cw, pv, wa, wx, name, riders=()):
    t = z.shape[0]
    tm = _tile(t)
    nt = t // tm
    nb = tm // BLK
    rev = lambda i: nt - 1 - i

    def body(dy_ref, z_ref, zh_ref, sink_ref, cw_ref, pv_ref, wa_ref, wx_ref, y_ref, hl_ref, hlh_ref, uc_ref,
             p_ref, ps_ref, dz_ref, dsink_ref, dcw_ref, dpv_ref, dwa_ref, dwx_ref,
             uext, ush, sgs, rxext, dkext, dvext, ducext, dsh, dcw8, dxcext, kcar, vcar, uccar, xccar, gcar):
        i = pl.program_id(0)
        first = i == nt - 1

        @pl.when(i == 0)
        def _():
            for car in (kcar, vcar, uccar, xccar, gcar, dcw8):
                car[...] = jnp.zeros_like(car)
            for acc in (dsink_ref, dpv_ref, dwa_ref, dwx_ref):
                acc[...] = jnp.zeros_like(acc)

        def addrow(r, val):
            dpv_ref[r:r + 1, :] += jnp.sum(val, axis=0, keepdims=True)

        dkext[:, 0:tm] = jnp.zeros((KV_W, tm), F32)
        dvext[:, 0:tm] = jnp.zeros((KV_W, tm), F32)
        dkext[:, tm:tm + BLK] = kcar[...]
        dvext[:, tm:tm + BLK] = vcar[...]
        lane512 = lax.broadcasted_iota(jnp.int32, (1, 4 * BLK), 1) < 2 * BLK
        lo = lax.broadcasted_iota(jnp.int32, (4 * BLK, BLK), 1) < HEAD_DIM
        hd, w2 = HEAD_DIM, 2 * BLK
        for b in range(nb):
            rows = slice(b * BLK, (b + 1) * BLK)
            band = slice(b * BLK, (b + 2) * BLK)
            q2, k2, v2 = _attn_operands(z_ref, zh_ref, b)
            prob = p_ref[b]
            psink = [ps_ref[b, :, 0:1], ps_ref[b, :, HEAD_DIM:HEAD_DIM + 1]]
            stack = lambda ref: jnp.concatenate([ref[rows, p * BLK:(p + 1) * BLK] for p in range(4)], axis=0)
            do4 = stack(dy_ref)
            dlt = do4 * stack(y_ref)
            d0 = jnp.sum(jnp.where(lo, dlt, 0.0), axis=1, keepdims=True)
            d1 = jnp.sum(jnp.where(lo, 0.0, dlt), axis=1, keepdims=True)
            dp = jnp.concatenate([_dot_nt(do4[g * w2:(g + 1) * w2], v2[g]) for g in range(2)], axis=0)
            dl = jnp.concatenate([jnp.broadcast_to(d0, (4 * BLK, w2)), jnp.broadcast_to(d1, (4 * BLK, w2))], axis=1)
            draw = (prob * (dp - dl)) * SCALE
            e0, e1 = psink[0] * d0, psink[1] * d1
            for p in range(4):
                prs = slice(p * BLK, (p + 1) * BLK)
                s0 = jnp.sum(e0[prs], axis=0, keepdims=True)
                s1 = jnp.sum(e1[prs], axis=0, keepdims=True)
                dsink_ref[p:p + 1, :] += -jnp.where(lane512, s0, s1)
            for g in range(2):
                grs = slice(g * w2, (g + 1) * w2)
                dq = _dot(draw[grs], k2[g])
                dz_ref[rows, (2 * g) * BLK:(2 * g + 1) * BLK] = dq[0:BLK].astype(dz_ref.dtype)
                dz_ref[rows, (2 * g + 1) * BLK:(2 * g + 2) * BLK] = dq[BLK:2 * BLK].astype(dz_ref.dtype)
                tk = _dot_tn(q2[g], draw[grs])
                tv = _dot_tn(do4[grs], prob[grs])
                dkext[g * hd:(g + 1) * hd, band] += tk[0:hd, 0:w2] + tk[hd:2 * hd, w2:2 * w2]
                dvext[g * hd:(g + 1) * hd, band] += tv[0:hd, 0:w2] + tv[hd:2 * hd, w2:2 * w2]
        dz_ref[:, K0:K0 + KV_W] = jnp.transpose(dkext[:, BLK:BLK + tm]).astype(dz_ref.dtype)
        dz_ref[:, V0:V0 + KV_W] = jnp.transpose(dvext[:, BLK:BLK + tm]).astype(dz_ref.dtype)
        kcar[...] = dkext[:, 0:BLK]
        vcar[...] = dvext[:, 0:BLK]

        _glu_fill(z_ref, zh_ref, uext, ush, first, tm, sg_out=sgs)
        xh, rs, ln, sg = _ln_silu(uc_ref[...], pv_ref)
        dln = dy_ref[:, ATTN_W:ATTN_W + CONV_W] * (sg * (1.0 + ln * (1.0 - sg)))
        addrow(R_LN_G, dln * xh)
        addrow(R_LN_B, dln)
        dxh = dln * pv_ref[R_LN_G:R_LN_G + 1, :]
        duc = rs * (dxh - jnp.mean(dxh, axis=-1, keepdims=True) - xh * jnp.mean(dxh * xh, axis=-1, keepdims=True))
        addrow(R_CONV_B, duc)
        ducext[0:tm, :] = duc
        ducext[tm:tm + CONV_HALO, :] = uccar[...]
        uccar[...] = duc[0:CONV_HALO, :]
        _shifted_copies(ducext, dsh, tm)
        for r0 in range(0, tm, CONV_CHUNK):
            crow = slice(r0, r0 + CONV_CHUNK)
            duc_c = ducext[crow, :]
            du = jnp.zeros((CONV_CHUNK, CONV_W), F32)
            for k in range(CONV_K):
                prod = duc_c * _tap(uext, ush, CONV_HALO - (CONV_K - 1) + k, r0, CONV_CHUNK)
                part = prod[0:8]
                for s in range(8, CONV_CHUNK, 8):
                    part = part + prod[s:s + 8]
                dcw8[k] += part
                du = du + cw_ref[k:k + 1, :] * _tap(ducext, dsh, CONV_K - 1 - k, r0, CONV_CHUNK)
            sgc = sgs[crow, :]
            dz_ref[crow, CV0:CV0 + CONV_W] = (du * sgc).astype(dz_ref.dtype)
            u_c = uext[CONV_HALO + r0:CONV_HALO + r0 + CONV_CHUNK, :]
            dz_ref[crow, CG0:CG0 + CONV_W] = (du * u_c * (1.0 - sgc)).astype(dz_ref.dtype)

        @pl.when(i == nt - 1)
        def _():
            dcw_ref[...] = jnp.sum(dcw8[...], axis=1)

        xc, r, ig, sp, la, a, mult = _lru_gates(z_ref, zh_ref, pv_ref, wa_ref, wx_ref, rxext, first, tm)
        h = hl_ref[...]
        rowi = lax.broadcasted_iota(jnp.int32, (tm, LRU_W), 0)
        hlast = jnp.where(first, 0.0, hlh_ref[7:8, :])
        hprev = jnp.where(rowi == 0, hlast, pltpu.roll(h, 1, 0))
        dyl = dy_ref[:, ATTN_W + CONV_W:ATTN_W + CONV_W + LRU_W]
        gl, dgl = _gelu(z_ref[:, RG0:RG0 + LRU_W])
        dz_ref[:, RG0:RG0 + LRU_W] = (dyl * h * dgl).astype(dz_ref.dtype)
        dh = dyl * gl + jnp.where(rowi == tm - 1, gcar[0:1, :], 0.0)
        c = jnp.where(rowi == tm - 1, 0.0, pltpu.roll(a, tm - 1, 0))
        _, gg = _scan(c, dh, tm, reverse=True)
        gcar[0:1, :] = a[0:1, :] * gg[0:1, :]
        dmult = gg * (ig * xc)
        dig = gg * mult * xc
        dxc = gg * mult * ig
        dla = gg * hprev * a - dmult * a * a / mult
        dr = dla * (-LRU_C * sp)
        lam = pv_ref[R_LAM:R_LAM + 1, :]
        dpv_ref[R_LAM:R_LAM + 1, :] += jnp.sum(dla * (-LRU_C * r), axis=0, keepdims=True) * (-_sigmoid(-lam))
        dpa = dr * r * (1.0 - r)
        dpx = dig * ig * (1.0 - ig)
        addrow(R_BA, dpa)
        addrow(R_BX, dpx)
        dxc = dxc + _dot_nt(dpa, wa_ref[...]) + _dot_nt(dpx, wx_ref[...])
        dwa_ref[...] += _dot_tn(xc, dpa)
        dwx_ref[...] += _dot_tn(xc, dpx)
        addrow(R_LCONV_B, dxc)
        dxcext[0:tm, :] = dxc
        dxcext[tm:tm + LRU_HALO, :] = xccar[...]
        xccar[...] = dxc[0:LRU_HALO, :]
        drx = jnp.zeros((tm, LRU_W), F32)
        for k in range(LRU_K):
            addrow(R_LCW + k, dxc * rxext[pl.ds(LRU_HALO - (LRU_K - 1) + k, tm), :])
            drx = drx + pv_ref[R_LCW + k:R_LCW + k + 1, :] * dxcext[pl.ds(LRU_K - 1 - k, tm), :]
        dz_ref[:, RX0:RX0 + LRU_W] = drx.astype(dz_ref.dtype)

    tile = lambda w: pl.BlockSpec((tm, w), lambda i: (rev(i), 0))
    in_specs = [tile(D_MODEL)] + _mixer_in_specs(tm, rev) + [
        tile(D_MODEL), tile(LRU_W),
        pl.BlockSpec((8, LRU_W), lambda i: (jnp.maximum(rev(i) * (tm // 8) - 1, 0), 0)),
        tile(CONV_W), pl.BlockSpec((nb, 4 * BLK, 4 * BLK), lambda i: (rev(i), 0, 0)),
        pl.BlockSpec((nb, 4 * BLK, BLK), lambda i: (rev(i), 0, 0))]
    return _call(
        body, name, (nt,), in_specs,
        [tile(IN_W), _acc_spec((8, 4 * BLK)), _acc_spec((32, CONV_W)), _acc_spec((16, CONV_W)),
         _acc_spec((LRU_W, LRU_W)), _acc_spec((LRU_W, LRU_W))],
        [_sds((t, IN_W), MX), _sds((8, 4 * BLK), F32), _sds((32, CONV_W), F32), _sds((16, CONV_W), F32),
         _sds((LRU_W, LRU_W), F32), _sds((LRU_W, LRU_W), F32)],
        [pltpu.VMEM((tm + CONV_HALO, CONV_W), F32), pltpu.VMEM((7, tm + CONV_HALO - 8, CONV_W), F32),
         pltpu.VMEM((tm, CONV_W), F32), pltpu.VMEM((tm + LRU_HALO, LRU_W), F32),
         pltpu.VMEM((KV_W, tm + BLK), F32), pltpu.VMEM((KV_W, tm + BLK), F32),
         pltpu.VMEM((tm + CONV_HALO, CONV_W), F32), pltpu.VMEM((7, tm + CONV_HALO - 8, CONV_W), F32),
         pltpu.VMEM((32, 8, CONV_W), F32), pltpu.VMEM((tm + LRU_HALO, LRU_W), F32),
         pltpu.VMEM((KV_W, BLK), F32), pltpu.VMEM((KV_W, BLK), F32),
         pltpu.VMEM((CONV_HALO, CONV_W), F32), pltpu.VMEM((LRU_HALO, LRU_W), F32), pltpu.VMEM((8, LRU_W), F32)],
        [dy, z, z, sink, cw, pv, wa, wx, ycat, hl, hl, uc, probs, psinks], riders)


def _post_fwd(ycat, h0, gmix, w_out, g2, w_up, w_down, name, riders=()):
    t = h0.shape[0]
    tm = _tile(t, POST_TILE)
    nj = D_FF // FF_BLK

    def body(y_ref, h_ref, gm_ref, wo_ref, g2_ref, wu_ref, wd_ref, h1_ref, a_ref, h2_ref, ym_ref, hn_ref):
        ym, _, _ = _group_rms_fwd(y_ref[...], gm_ref[...])
        ym = ym.astype(MX)
        ym_ref[...] = ym
        h1 = h_ref[...] + jnp.dot(ym, wo_ref[...], preferred_element_type=F32)
        h1_ref[...] = h1
        hn, _, _ = _rms_fwd(h1, g2_ref[...])
        hn = hn.astype(MX)
        hn_ref[...] = hn
        for j in range(nj):
            u = jnp.dot(hn, wu_ref[j], preferred_element_type=F32)
            a_ref[:, j * FF_BLK:(j + 1) * FF_BLK] = jnp.square(jnp.maximum(u, 0.0)).astype(MX)
        h2_ref[...] = h1 + jnp.dot(a_ref[...], wd_ref[...], preferred_element_type=F32)

    tile = lambda w: pl.BlockSpec((tm, w), lambda i: (i, 0))
    return _call(
        body, name, (t // tm,),
        [tile(D_MODEL), tile(D_MODEL), _const_spec((1, D_MODEL)), _const_spec((D_MODEL, D_MODEL)),
         _const_spec((1, D_MODEL)), _const_spec((nj, D_MODEL, FF_BLK)), _const_spec((D_FF, D_MODEL))],
        [tile(D_MODEL), tile(D_FF), tile(D_MODEL), tile(D_MODEL), tile(D_MODEL)],
        [_sds((t, D_MODEL), F32), _sds((t, D_FF), MX), _sds((t, D_MODEL), F32), _sds((t, D_MODEL), MX),
         _sds((t, D_MODEL), MX)],
        [], [ycat, h0, gmix, w_out, g2, w_up, w_down], riders)


def _ffn_bwd(dh2, act, h1, g2, w_up_t, w_down, name, riders=()):
    t = h1.shape[0]
    tm = _tile(t, POST_TILE)
    nj = D_FF // FF_BLK

    def body(dh2_ref, a_ref, h1_ref, g2_ref, wut_ref, wd_ref, dh1_ref, dh1b_ref, dh2b_ref, du_ref, dg2_ref):
        @pl.when(pl.program_id(0) == 0)
        def _():
            dg2_ref[...] = jnp.zeros_like(dg2_ref)

        dh2 = dh2_ref[...]
        dh2b = dh2.astype(MX)
        dh2b_ref[...] = dh2b
        for j in range(nj):
            cols = slice(j * FF_BLK, (j + 1) * FF_BLK)
            da = _dot_nt(dh2b, wd_ref[j])
            du_ref[:, cols] = (da * (2.0 * jnp.sqrt(a_ref[:, cols].astype(F32)))).astype(MX)
        dhn = jnp.dot(du_ref[...], wut_ref[...], preferred_element_type=F32)
        _, xh, r = _rms_fwd(h1_ref[...], g2_ref[...])
        dx, dg = _rms_bwd(dhn, xh, r, g2_ref[...])
        dg2_ref[...] += dg
        dh1 = dh2 + dx
        dh1_ref[...] = dh1
        dh1b_ref[...] = dh1.astype(MX)

    tile = lambda w: pl.BlockSpec((tm, w), lambda i: (i, 0))
    return _call(
        body, name, (t // tm,),
        [tile(D_MODEL), tile(D_FF), tile(D_MODEL), _const_spec((1, D_MODEL)),
         _const_spec((D_FF, D_MODEL)), _const_spec((nj, FF_BLK, D_MODEL))],
        [tile(D_MODEL), tile(D_MODEL), tile(D_MODEL), tile(D_FF), _acc_spec((1, D_MODEL))],
        [_sds((t, D_MODEL), F32), _sds((t, D_MODEL), MX), _sds((t, D_MODEL), MX), _sds((t, D_FF), MX),
         _sds((1, D_MODEL), F32)],
        [], [dh2, act, h1, g2, w_up_t, w_down], riders)


def _mix_bwd(dh1, ycat, ym, gmix, w_out, name):
    t = dh1.shape[0]
    tm = _tile(t)
    nk = t // tm
    r = D_MODEL // N_DEV

    def body(dh1_ref, y_ref, ym_ref, gm_ref, wo_ref, dy_ref, dgm_ref, o_ref, o16_ref, acc):
        k = pl.program_id(0)

        @pl.when(k == 0)
        def _():
            dgm_ref[...] = jnp.zeros_like(dgm_ref)
            acc[...] = jnp.zeros_like(acc)

        dh = dh1_ref[...]
        acc[...] += _dot_tn(ym_ref[...], dh)
        dym = _dot_nt(dh, wo_ref[...])
        gm = gm_ref[...]
        _, yh, rr = _group_rms_fwd(y_ref[...], gm)
        outs, dgs = [], []
        for (a, b), rg in zip(_GROUPS, rr):
            dxg, dgg = _rms_bwd(dym[:, a:b], yh[:, a:b], rg, gm[:, a:b])
            outs.append(dxg)
            dgs.append(dgg)
        dy_ref[...] = jnp.concatenate(outs, axis=1)
        dgm_ref[...] += jnp.concatenate(dgs, axis=1)

        @pl.when(k == nk - 1)
        def _():
            for d in range(N_DEV):
                v = acc[d * r:(d + 1) * r, :]
                o_ref[d] = v
                o16_ref[d] = v.astype(o16_ref.dtype)

    tile = pl.BlockSpec((tm, D_MODEL), lambda i: (i, 0))
    slabs = _const_spec((N_DEV, r, D_MODEL))
    (dy, dgm, dw, dw16), _ = _call(
        body, name, (nk,), [tile, tile, tile, _const_spec((1, D_MODEL)), _const_spec((D_MODEL, D_MODEL))],
        [tile, _acc_spec((1, D_MODEL)), slabs, slabs],
        [_sds((t, D_MODEL), F32), _sds((1, D_MODEL), F32), _sds((N_DEV, r, D_MODEL), F32),
         _sds((N_DEV, r, D_MODEL), WIRE)],
        [pltpu.VMEM((D_MODEL, D_MODEL), F32)], [dh1, ycat, ym, gmix, w_out])
    return dy, dgm, (dw, dw16)


def _in_bwd(dz, h0, dh1, g1, w_in_t, after, name):
    t = h0.shape[0]
    tm = _tile(t, STREAM_TILE)

    def body(dz_ref, h_ref, dh1_ref, g_ref, w_ref, after_ref, dh0_ref, dg_ref):
        @pl.when(pl.program_id(0) == 0)
        def _():
            dg_ref[...] = jnp.zeros_like(dg_ref)

        dhn = _dot(dz_ref[...], w_ref[...])
        _, xh, r = _rms_fwd(h_ref[...], g_ref[...])
        dx, dg = _rms_bwd(dhn, xh, r, g_ref[...])
        dg_ref[...] += dg
        dh0_ref[...] = dh1_ref[...] + dx

    tile = lambda w: pl.BlockSpec((tm, w), lambda i: (i, 0))
    (dh0, dg), _ = _call(
        body, name, (t // tm,),
        [tile(IN_W), tile(D_MODEL), tile(D_MODEL), _const_spec((1, D_MODEL)), _const_spec((IN_W, D_MODEL)),
         _const_spec((8, 128))],
        [tile(D_MODEL), _acc_spec((1, D_MODEL))], [_sds((t, D_MODEL), F32), _sds((1, D_MODEL), F32)],
        [], [dz, h0, dh1, g1, w_in_t, after])
    return dh0, dg


def _in_bwd_dw(dz, h0, dh1, g1, w_in_t, name):
    t = h0.shape[0]
    tm = _tile(t)
    nk = t // tm

    def body(dz_ref, h_ref, dh1_ref, g_ref, w_ref, dh0_ref, dg_ref, o_ref, o16_ref, acc):
        k = pl.program_id(0)

        @pl.when(k == 0)
        def _():
            dg_ref[...] = jnp.zeros_like(dg_ref)
            acc[...] = jnp.zeros_like(acc)

        dz_t = dz_ref[...]
        hn, xh, r = _rms_fwd(h_ref[...], g_ref[...])
        acc[...] += _dot_tn(dz_t, hn)
        dhn = _dot(dz_t, w_ref[...])
        dx, dg = _rms_bwd(dhn, xh, r, g_ref[...])
        dg_ref[...] += dg
        dh0_ref[...] = dh1_ref[...] + dx

        @pl.when(k == nk - 1)
        def _():
            for d in range(N_DEV):
                v = acc[d * IN_SHARD:(d + 1) * IN_SHARD, :]
                o_ref[d] = v
                o16_ref[d] = v.astype(o16_ref.dtype)

    tile = lambda w: pl.BlockSpec((tm, w), lambda i: (i, 0))
    slabs = _const_spec((N_DEV, IN_SHARD, D_MODEL))
    (dh0, dg, dw, dw16), _ = _call(
        body, name, (nk,),
        [tile(IN_W), tile(D_MODEL), tile(D_MODEL), _const_spec((1, D_MODEL)), _const_spec((IN_W, D_MODEL))],
        [tile(D_MODEL), _acc_spec((1, D_MODEL)), slabs, slabs],
        [_sds((t, D_MODEL), F32), _sds((1, D_MODEL), F32), _sds((N_DEV, IN_SHARD, D_MODEL), F32),
         _sds((N_DEV, IN_SHARD, D_MODEL), WIRE)],
        [pltpu.VMEM((IN_W, D_MODEL), F32)], [dz, h0, dh1, g1, w_in_t])
    return dh0, dg, (dw, dw16)


def _loss_head(h, gf, target, name):
    t = h.shape[0]
    tm = _tile(t, STREAM_TILE)

    def body(h_ref, g_ref, t_ref, dh_ref, loss_ref, dg_ref):
        @pl.when(pl.program_id(0) == 0)
        def _():
            loss_ref[...] = jnp.zeros_like(loss_ref)
            dg_ref[...] = jnp.zeros_like(dg_ref)

        g = g_ref[...]
        y, xh, r = _rms_fwd(h_ref[...], g)
        err = y - t_ref[...]
        part = 0.5 * jnp.sum(jnp.mean(err * err, axis=-1, keepdims=True), axis=0, keepdims=True)
        loss_ref[...] += jnp.broadcast_to(part, loss_ref.shape)
        dx, dg = _rms_bwd(err * (1.0 / D_MODEL), xh, r, g)
        dg_ref[...] += dg
        dh_ref[...] = dx

    tile = pl.BlockSpec((tm, D_MODEL), lambda i: (i, 0))
    (dh, loss, dg), _ = _call(
        body, name, (t // tm,), [tile, _const_spec((1, D_MODEL)), tile],
        [tile, _acc_spec((1, 128)), _acc_spec((1, D_MODEL))],
        [_sds((t, D_MODEL), F32), _sds((1, 128), F32), _sds((1, D_MODEL), F32)], [], [h, gf, target])
    return dh, loss, dg


def _dw(x, y, name, split, bm, bn):
    t, m = x.shape
    n = y.shape[1]
    tk = _tile(t, DW_TILE)
    nk = t // tk
    if split == "rows":
        assert bn == n
        r, c = m // N_DEV, n
        per = bm // r
        out_block = pl.BlockSpec((per, r, c), lambda a, b, k: (a, 0, 0))
    else:
        assert bm == m
        r, c = m, n // N_DEV
        per = bn // c
        out_block = pl.BlockSpec((per, r, c), lambda a, b, k: (b, 0, 0))

    def body(x_ref, y_ref, o_ref, o16_ref, acc):
        k = pl.program_id(2)

        @pl.when(k == 0)
        def _():
            acc[...] = jnp.zeros_like(acc)

        acc[...] += _dot_tn(x_ref[...], y_ref[...])

        @pl.when(k == nk - 1)
        def _():
            for d in range(per):
                v = acc[d * r:(d + 1) * r, :] if split == "rows" else acc[:, d * c:(d + 1) * c]
                o_ref[d] = v
                o16_ref[d] = v.astype(o16_ref.dtype)

    return pl.pallas_call(
        body, name=name, grid=(m // bm, n // bn, nk),
        in_specs=[pl.BlockSpec((tk, bm), lambda a, b, k: (k, a)), pl.BlockSpec((tk, bn), lambda a, b, k: (k, b))],
        out_specs=[out_block, out_block],
        out_shape=[_sds((N_DEV, r, c), F32), _sds((N_DEV, r, c), WIRE)],
        scratch_shapes=[pltpu.VMEM((bm, bn), F32)],
        compiler_params=pltpu.CompilerParams(dimension_semantics=("arbitrary",) * 3, vmem_limit_bytes=VMEM_LIMIT),
    )(x, y)


def _adamw_math(w, g, m, v):
    m = ADAM_B1 * m + (1.0 - ADAM_B1) * g
    v = ADAM_B2 * v + (1.0 - ADAM_B2) * jnp.square(g)
    m_hat = m / (1.0 - ADAM_B1 ** ADAM_STEP)
    v_hat = v / (1.0 - ADAM_B2 ** ADAM_STEP)
    delta = -ADAM_LR * (m_hat / (jnp.sqrt(v_hat) + ADAM_EPS) + ADAM_WD * w)
    return delta, m, v


def _adamw_shard(g_own, g_recv, dev, w, m, v, after, name):
    _, r, c = w.shape
    br = r
    for cand in (256, 128, 112, 64, 56, 32, 16, 8):
        if r % cand == 0:
            br = cand
            break
    nr = r // br
    own = lambda l: pl.BlockSpec((1, br, c), lambda ll, i, d: (d[0], jnp.where(ll == l, i, (nr - 1) * (1 - l)), 0))
    recv = lambda l: pl.BlockSpec((N_DEV - 1, br, c), lambda ll, i, d: (0, jnp.where(ll == l, i, (nr - 1) * (1 - l)), 0))

    def body(dev_ref, go0, gr0, go1, gr1, w_ref, m_ref, v_ref, after_ref, g_out, d_out, m_out, v_out):
        def update(go_ref, gr_ref):
            g = go_ref[0]
            for j in range(N_DEV - 1):
                g = g + gr_ref[j].astype(F32)
            delta, mn, vn = _adamw_math(w_ref[0], g, m_ref[0], v_ref[0])
            g_out[0] = g
            d_out[0] = delta
            m_out[0] = mn
            v_out[0] = vn

        layer = pl.program_id(0)
        pl.when(layer == 0)(lambda: update(go0, gr0))
        pl.when(layer == 1)(lambda: update(go1, gr1))

    tile = pl.BlockSpec((1, br, c), lambda ll, i, d: (ll, i, 0))
    return pl.pallas_call(
        body, name=name,
        grid_spec=pltpu.PrefetchScalarGridSpec(
            num_scalar_prefetch=1, grid=(2, nr),
            in_specs=[own(0), recv(0), own(1), recv(1), tile, tile, tile,
                      pl.BlockSpec((8, 128), lambda ll, i, d: (0, 0))],
            out_specs=[tile, tile, tile, tile]),
        out_shape=[_sds((2, r, c), F32)] * 4,
        compiler_params=pltpu.CompilerParams(dimension_semantics=("arbitrary",) * 2, vmem_limit_bytes=VMEM_LIMIT),
    )(dev, g_own[0], g_recv[0], g_own[1], g_recv[1], w, m, v, after)


def _adamw_small(gs, ws, ms, vs, name):
    n = len(gs)

    def body(*refs):
        g_refs, w_refs, m_refs, v_refs = (refs[k * n:(k + 1) * n] for k in range(4))
        outs = refs[4 * n:]
        for k in range(n):
            delta, mn, vn = _adamw_math(w_refs[k][...], g_refs[k][...], m_refs[k][...], v_refs[k][...])
            outs[k][...] = delta
            outs[n + k][...] = mn
            outs[2 * n + k][...] = vn

    shapes = [_sds(w.shape, F32) for w in ws]
    res = pl.pallas_call(body, name=name, out_shape=shapes * 3,
                         compiler_params=pltpu.CompilerParams(vmem_limit_bytes=VMEM_LIMIT))(*gs, *ws, *ms, *vs)
    return res[:n], res[n:2 * n], res[2 * n:]


def _sum_parts(own, recv, dev, name):
    def body(dev_ref, own_ref, recv_ref, o_ref):
        me = dev_ref[0]

        def block(d):
            f = jnp.bitwise_xor(me, d)
            return jnp.where(f == 0, own_ref[...], recv_ref[jnp.maximum(f - 1, 0)])

        g = block(0)
        for d in range(1, N_DEV):
            g = g + block(d)
        o_ref[...] = g

    return pl.pallas_call(
        body, name=name,
        grid_spec=pltpu.PrefetchScalarGridSpec(
            num_scalar_prefetch=1, grid=(1,),
            in_specs=[pl.BlockSpec(own.shape, lambda i, d: (0, 0)), pl.BlockSpec(recv.shape, lambda i, d: (0, 0, 0))],
            out_specs=pl.BlockSpec(own.shape, lambda i, d: (0, 0))),
        out_shape=_sds(own.shape, F32))(dev, own, recv)


HBM = pl.BlockSpec(memory_space=pltpu.HBM)
SEM = pl.BlockSpec(memory_space=pltpu.SEMAPHORE)
EFFECT = pltpu.SideEffectType.DATAFLOW_SIDE_EFFECTING


def _direct_copies(srcs, lands, ssem, rsem, scatter):
    x, y, c = _me()
    out = []
    for a in range(len(srcs)):
        for f in range(1, N_DEV):
            px = 1 - x if f & 4 else x
            py = 1 - y if f & 2 else y
            pc = 1 - c if f & 1 else c
            out.append(pltpu.make_async_remote_copy(
                src_ref=srcs[a].at[4 * px + 2 * py + pc] if scatter else srcs[a], dst_ref=lands[a].at[f - 1],
                send_sem=ssem.at[7 * a + f - 1], recv_sem=rsem.at[7 * a + f - 1],
                device_id=(px, py, pc), device_id_type=MESH))
    return out


def _send_start(arrays, scatter, name):
    arrays = list(arrays)
    n = len(arrays)
    lands = [lax.empty((N_DEV - 1,) + (a.shape[1:] if scatter else a.shape), a.dtype) for a in arrays]

    def body(*refs):
        srcs, lnds, ssem, rsem, token = refs[:n], refs[n:2 * n], refs[2 * n], refs[2 * n + 1], refs[-1]
        for cp in _direct_copies(srcs, lnds, ssem, rsem, scatter):
            cp.start()
        token[...] = jnp.zeros_like(token)

    hbm = lambda a: pltpu.HBM(a.shape, a.dtype)
    res = pl.pallas_call(
        body, name=name,
        out_shape=(pltpu.SemaphoreType.DMA((7 * n,)), pltpu.SemaphoreType.DMA((7 * n,)),
                   *[hbm(a) for a in arrays + lands], _sds((8, 128), F32)),
        in_specs=[HBM] * (2 * n),
        out_specs=(SEM, SEM, *[HBM] * (2 * n), pl.BlockSpec(memory_space=pltpu.VMEM)),
        input_output_aliases={i: 2 + i for i in range(2 * n)},
        compiler_params=pltpu.CompilerParams(has_side_effects=EFFECT),
    )(*[pltpu.with_memory_space_constraint(a, pltpu.HBM) for a in arrays + lands])
    return types.SimpleNamespace(ssem=res[0], rsem=res[1], srcs=list(res[2:2 + n]), lands=list(res[2 + n:2 + 2 * n]),
                                 token=res[-1], scatter=scatter)


def _send_wait(h, after, name):
    n = len(h.srcs)

    def body(*refs):
        srcs, lnds, ssem, rsem = refs[:n], refs[n:2 * n], refs[2 * n], refs[2 * n + 1]
        for cp in _direct_copies(srcs, lnds, ssem, rsem, h.scatter):
            cp.wait_send()
            cp.wait_recv()

    hbm = lambda a: pltpu.HBM(a.shape, a.dtype)
    res = pl.pallas_call(
        body, name=name,
        out_shape=tuple(hbm(a) for a in h.srcs + h.lands),
        in_specs=[HBM] * (2 * n) + [SEM, SEM, ANY], out_specs=[HBM] * (2 * n),
        input_output_aliases={i: i for i in range(2 * n)},
        compiler_params=pltpu.CompilerParams(has_side_effects=EFFECT),
    )(*h.srcs, *h.lands, h.ssem, h.rsem, after)
    return list(res[:n]), list(res[n:])


def _block_diag(w):
    out = jnp.zeros((LRU_W, LRU_W), w.dtype)
    for h in range(4):
        out = lax.dynamic_update_slice(out, w[h], (h * 64, h * 64))
    return out


def _layer_params(p, l):
    row = lambda a: a[l].reshape(1, -1)
    sink_rows = jnp.repeat(p["attn_sinks"][l].reshape(4, 2), 2 * BLK, axis=1)
    sink_rows = jnp.concatenate([sink_rows, jnp.zeros((4, 4 * BLK), F32)], axis=0)
    cw = jnp.concatenate([p["conv_dw_w"][l], jnp.zeros((1, CONV_W), F32)], axis=0)
    pv = jnp.concatenate([
        row(p["conv_dw_b"]), row(p["conv_ln_g"]), row(p["conv_ln_b"]), row(p["lru_conv_b"]), row(p["lru_ba"]),
        row(p["lru_bx"]), row(p["lru_lambda"]), jnp.zeros((1, LRU_W), F32), p["lru_conv_w"][l],
        jnp.zeros((4, LRU_W), F32)], axis=0)
    return dict(
        g1=row(p["norm1"]), sink=sink_rows, cw=cw, pv=pv,
        wa=_block_diag(p["lru_wa"][l]).astype(MX), wx=_block_diag(p["lru_wx"][l]).astype(MX),
        gmix=row(p["mix_norm"]), g2=row(p["norm2"]))


_SMALL = ["norm1", "attn_sinks", "conv_dw_w", "conv_dw_b", "conv_ln_g", "conv_ln_b", "lru_conv_w", "lru_conv_b",
          "lru_wa", "lru_ba", "lru_wx", "lru_bx", "lru_lambda", "mix_norm", "norm2"]
_BIG = ["w_in", "w_out", "w_up", "w_down"]
_WEIGHTS = ["norm1", "w_in", "attn_sinks", "conv_dw_w", "conv_dw_b", "conv_ln_g", "conv_ln_b", "lru_conv_w",
            "lru_conv_b", "lru_wa", "lru_ba", "lru_wx", "lru_bx", "lru_lambda", "mix_norm", "w_out", "norm2", "w_up",
            "w_down", "final_norm"]


def kernel(x, norm1, w_in, attn_sinks, conv_dw_w, conv_dw_b, conv_ln_g, conv_ln_b, lru_conv_w, lru_conv_b, lru_wa, lru_ba, lru_wx, lru_bx, lru_lambda, mix_norm, w_out, norm2, w_up, w_down, final_norm, loss_target, m_norm1, m_w_in, m_attn_sinks, m_conv_dw_w, m_conv_dw_b, m_conv_ln_g, m_conv_ln_b, m_lru_conv_w, m_lru_conv_b, m_lru_wa, m_lru_ba, m_lru_wx, m_lru_bx, m_lru_lambda, m_mix_norm, m_w_out, m_norm2, m_w_up, m_w_down, m_final_norm, v_norm1, v_w_in, v_attn_sinks, v_conv_dw_w, v_conv_dw_b, v_conv_ln_g, v_conv_ln_b, v_lru_conv_w, v_lru_conv_b, v_lru_wa, v_lru_ba, v_lru_wx, v_lru_bx, v_lru_lambda, v_mix_norm, v_w_out, v_norm2, v_w_up, v_w_down, v_final_norm):
    w = dict(norm1=norm1, w_in=w_in, attn_sinks=attn_sinks, conv_dw_w=conv_dw_w, conv_dw_b=conv_dw_b,
             conv_ln_g=conv_ln_g, conv_ln_b=conv_ln_b, lru_conv_w=lru_conv_w, lru_conv_b=lru_conv_b, lru_wa=lru_wa,
             lru_ba=lru_ba, lru_wx=lru_wx, lru_bx=lru_bx, lru_lambda=lru_lambda, mix_norm=mix_norm, w_out=w_out,
             norm2=norm2, w_up=w_up, w_down=w_down, final_norm=final_norm)
    m = dict(norm1=m_norm1, w_in=m_w_in, attn_sinks=m_attn_sinks, conv_dw_w=m_conv_dw_w, conv_dw_b=m_conv_dw_b,
             conv_ln_g=m_conv_ln_g, conv_ln_b=m_conv_ln_b, lru_conv_w=m_lru_conv_w, lru_conv_b=m_lru_conv_b,
             lru_wa=m_lru_wa, lru_ba=m_lru_ba, lru_wx=m_lru_wx, lru_bx=m_lru_bx, lru_lambda=m_lru_lambda,
             mix_norm=m_mix_norm, w_out=m_w_out, norm2=m_norm2, w_up=m_w_up, w_down=m_w_down, final_norm=m_final_norm)
    v = dict(norm1=v_norm1, w_in=v_w_in, attn_sinks=v_attn_sinks, conv_dw_w=v_conv_dw_w, conv_dw_b=v_conv_dw_b,
             conv_ln_g=v_conv_ln_g, conv_ln_b=v_conv_ln_b, lru_conv_w=v_lru_conv_w, lru_conv_b=v_lru_conv_b,
             lru_wa=v_lru_wa, lru_ba=v_lru_ba, lru_wx=v_lru_wx, lru_bx=v_lru_bx, lru_lambda=v_lru_lambda,
             mix_norm=v_mix_norm, w_out=v_w_out, norm2=v_norm2, w_up=v_w_up, w_down=v_w_down, final_norm=v_final_norm)
    depth = w_in.shape[0]
    xi, yi, ci = _me()
    dev = (4 * xi + 2 * yi + ci).astype(jnp.int32)
    dev1 = dev.reshape(1)
    tr = lambda a: jnp.swapaxes(a, 1, 2)
    w_t, m_t, v_t = tr(w_in), tr(m_w_in), tr(v_w_in)
    wb = {n: w[n].astype(MX) for n in _BIG if n != "w_in"}
    wb["w_in"] = w_t.astype(MX)
    layer_shards = lambda l: [wb["w_out"][l], wb["w_up"][l], wb["w_down"][l]]

    _, ((g_in0, g_cw, g_lcw),) = _call(None, "gather_first", None, [], [], [], [], [],
                                        [_gather_rider([wb["w_in"][0], conv_dw_w, lru_conv_w])])
    cols = lambda g: jnp.moveaxis(g, 0, -2).reshape(g.shape[1:-1] + (N_DEV * g.shape[-1],))
    p = dict(w)
    p["conv_dw_w"] = cols(g_cw)
    p["lru_conv_w"] = cols(g_lcw)
    lp = [_layer_params(p, l) for l in range(depth)]

    gathered = [dict(w_in=g_in0.reshape(IN_W, D_MODEL)), dict()]
    saved = []
    h = x[0]
    for l in range(depth):
        q, gw = lp[l], gathered[l]
        z, hn1 = _ln_in(h, q["g1"], gw["w_in"], l == 0, f"ln_in{l}")
        riders = [_gather_rider(layer_shards(0))] if l == 0 else []
        (ycat, hl, uc, probs, psinks), got = _mixer_fwd(z, q["sink"], q["cw"], q["pv"], q["wa"], q["wx"],
                                                        f"mixer_fwd{l}", riders)
        if l == 0:
            gw["w_out"], gw["w_up"], gw["w_down"] = got[0]
            gw["w_out"] = gw["w_out"].reshape(D_MODEL, D_MODEL)
        riders = [_gather_rider([wb["w_in"][1]] + layer_shards(1))] if l == 0 else []
        (h1, act, h2, ym, hn2), got = _post_fwd(ycat, h, q["gmix"], gw["w_out"], q["g2"], gw["w_up"],
                                                gw["w_down"].reshape(D_FF, D_MODEL), f"post_fwd{l}", riders)
        if l == 0:
            nxt = gathered[1]
            nxt["w_in"], nxt["w_out"], nxt["w_up"], nxt["w_down"] = got[0]
            nxt["w_in"] = nxt["w_in"].reshape(IN_W, D_MODEL)
            nxt["w_out"] = nxt["w_out"].reshape(D_MODEL, D_MODEL)
        saved.append(dict(h0=h, z=z, hn1=hn1, ycat=ycat, hl=hl, uc=uc, probs=probs, psinks=psinks, h1=h1, act=act,
                          ym=ym, hn2=hn2))
        h = h2
    dh, loss, dgf = _loss_head(h, final_norm.reshape(1, -1), loss_target[0], "loss_head")

    grads = [None] * depth
    big = {n: [None] * depth for n in _BIG}
    pending = []

    def send_pending():
        riders = [_scatter_rider([item[3] for item in pending])] if pending else []
        return riders, list(pending)

    def record(sent, got):
        for item, recv in zip(sent, got[0] if sent else []):
            big[item[0]][item[1]] = (item[2], recv)
        del pending[:len(sent)]

    for l in reversed(range(depth)):
        q, s, gw = lp[l], saved[l], gathered[l]
        riders, sent = send_pending()
        w_up_t = jnp.swapaxes(gw["w_up"], 1, 2).reshape(D_FF, D_MODEL)
        (dh1, dh1b, dhb, du, dg2), got = _ffn_bwd(dh, s["act"], s["h1"], q["g2"], w_up_t, gw["w_down"],
                                                  f"ffn_bwd{l}", riders)
        record(sent, got)
        dycat, dgm, d_wout = _mix_bwd(dh1b, s["ycat"], s["ym"], q["gmix"], gw["w_out"], f"mix_bwd{l}")
        pending.append(("w_down", l) + tuple(_dw(s["act"], dhb, f"dw_down{l}", "rows", 2048, D_MODEL)))
        pending.append(("w_up", l) + tuple(_dw(s["hn2"], du, f"dw_up{l}", "cols", D_MODEL, 2048)))
        pending.append(("w_out", l) + tuple(d_wout))
        riders, sent = send_pending()
        (dz, dsink, dcw, dpv, dwa, dwx), got = _mixer_bwd(
            dycat, s["z"], s["ycat"], s["hl"], s["uc"], s["probs"], s["psinks"], q["sink"], q["cw"], q["pv"], q["wa"],
            q["wx"], f"mixer_bwd{l}", riders)
        record(sent, got)
        if l > 0:
            dh, dg1, d_win1 = _in_bwd_dw(dz, s["h0"], dh1, q["g1"], gw["w_in"], f"in_bwd{l}")
            win1_sends = _send_start([d_win1[1]], True, "scatter_w_in1_start")
        else:
            d_win = _dw(dz, s["hn1"], f"dw_in{l}", "rows", IN_W, D_MODEL)
            win_sends = _send_start([d_win[1]], True, "scatter_w_in0_start")
            dh, dg1 = _in_bwd(dz, s["h0"], dh1, q["g1"], gw["w_in"], win_sends.token, f"in_bwd{l}")
        grads[l] = dict(dg1=dg1, dsink=dsink, dcw=dcw, dpv=dpv, dwa=dwa, dwx=dwx, dgm=dgm, dg2=dg2)

    acc = {k: jnp.stack([grads[l][k] for l in range(depth)]) for k in grads[0]}
    dpv = acc["dpv"]
    unblock = lambda a: jnp.concatenate([a[:, h * 64:(h + 1) * 64, h * 64:(h + 1) * 64] for h in range(4)], axis=1)
    by_name = dict(
        norm1=acc["dg1"][:, 0], attn_sinks=jnp.stack([acc["dsink"][:, 0:4, 0], acc["dsink"][:, 0:4, 2 * BLK]],
                                                     axis=2).reshape(depth, 8),
        conv_dw_w=acc["dcw"][:, 0:CONV_K], conv_dw_b=dpv[:, R_CONV_B], conv_ln_g=dpv[:, R_LN_G],
        conv_ln_b=dpv[:, R_LN_B], lru_conv_w=dpv[:, R_LCW:R_LCW + LRU_K], lru_conv_b=dpv[:, R_LCONV_B],
        lru_wa=unblock(acc["dwa"]), lru_ba=dpv[:, R_BA].reshape(depth, 4, 64), lru_wx=unblock(acc["dwx"]),
        lru_bx=dpv[:, R_BX].reshape(depth, 4, 64), lru_lambda=dpv[:, R_LAM], mix_norm=acc["dgm"][:, 0],
        norm2=acc["dg2"][:, 0])
    small = [by_name[n] for n in _SMALL] + [dgf, loss[:, 0:1]]

    def as_rows(a):
        flat = a.reshape(-1)
        pad = (-flat.size) % 1024
        if pad:
            flat = jnp.concatenate([flat, jnp.zeros((pad,), F32)])
        return flat.reshape(-1, 128)

    pieces = [as_rows(a) for a in small]
    packed = jnp.concatenate(pieces, axis=0)
    small_sends = _send_start([packed], False, "bcast_small_start")

    out = {}
    shard_update = lambda n, wmv, after: list(_adamw_shard(
        [big[n][l][0] for l in range(depth)], [big[n][l][1] for l in range(depth)], dev1, *wmv, after, f"adamw_{n}"))
    for n in ("w_out", "w_up", "w_down"):
        out[n] = shard_update(n, (w[n], m[n], v[n]), small_sends.token)
    _, (win_recv,) = _send_wait(win_sends, out["w_down"][1], "scatter_w_in0_wait")
    big["w_in"][0] = (d_win[0], win_recv)
    _, (win1_recv,) = _send_wait(win1_sends, win_recv, "scatter_w_in1_wait")
    big["w_in"][1] = (d_win1[0], win1_recv)
    out["w_in"] = [tr(a) for a in shard_update("w_in", (w_t, m_t, v_t), jnp.zeros((8, 128), F32))]
    (packed,), (small_recv,) = _send_wait(small_sends, out["w_in"][1], "bcast_small_wait")
    summed = _sum_parts(packed, small_recv, dev1, "sum_small_grads")
    small_sums, row = [], 0
    for a, piece in zip(small, pieces):
        got = summed[row:row + piece.shape[0]]
        small_sums.append(got.reshape(a.shape) if a.size == piece.size else got.reshape(-1)[:a.size].reshape(a.shape))
        row += piece.shape[0]
    shard = lambda a: lax.dynamic_slice_in_dim(a, dev * (a.shape[-1] // N_DEV), a.shape[-1] // N_DEV, axis=a.ndim - 1)
    flat = {"lru_wa": (depth, LRU_W, 64), "lru_wx": (depth, LRU_W, 64), "final_norm": (1, D_MODEL)}
    gs, ws, ms, vs = [], [], [], []
    for n, g in zip(_SMALL + ["final_norm"], small_sums[:-1]):
        shp = flat.get(n, w[n].shape)
        gs.append((shard(g) if n in ("conv_dw_w", "lru_conv_w") else g).reshape(shp))
        ws.append(w[n].reshape(shp))
        ms.append(m[n].reshape(shp))
        vs.append(v[n].reshape(shp))
    sd, sm, sv = _adamw_small(gs, ws, ms, vs, "adamw_small")
    for j, n in enumerate(_SMALL + ["final_norm"]):
        out[n] = [a.reshape(w[n].shape) for a in (gs[j], sd[j], sm[j], sv[j])]
    loss_total = small_sums[-1][0, 0]

    result = [loss_total, dh[None]]
    for j in range(4):
        result += [out[n][j] for n in _WEIGHTS]
    return tuple(result)
```
